```python
import jax, jax.numpy as jnp
from jax import lax
import numpy as np

D_MODEL = 1024
BATCH = 8
SEQ = 2048
DEPTH = 2

HEAD_DIM = 64
N_HEADS_A = D_MODEL // HEAD_DIM
WIDTH_A = N_HEADS_A * HEAD_DIM
N_Q_HEADS_B = D_MODEL // HEAD_DIM
N_KV_HEADS_B = 4
GROUP_B = N_Q_HEADS_B // N_KV_HEADS_B
WIDTH_B = N_Q_HEADS_B * HEAD_DIM
KV_WIDTH_B = N_KV_HEADS_B * HEAD_DIM
WINDOW = 128
Q_BLOCK = 128
ROT_DIM = HEAD_DIM // 4
ROPE_THETA = 500000.0
EPS = 1e-6
N_A_LAYERS = DEPTH // 2
N_B_LAYERS = DEPTH - N_A_LAYERS

kernel_name = "yoco_fox_swa_sink_hybrid"


def rmsnorm(x, g):
    xf = x.astype(jnp.float32)
    y = xf * lax.rsqrt(jnp.mean(xf * xf, axis=-1, keepdims=True) + EPS)
    return (y * g.astype(jnp.float32)).astype(x.dtype)


def partial_rope(x, positions):
    x_rot, x_pass = x[..., :ROT_DIM], x[..., ROT_DIM:]
    half = ROT_DIM // 2
    inv_freq = jnp.power(jnp.float32(ROPE_THETA), -jnp.arange(0, ROT_DIM, 2, dtype=jnp.float32) / ROT_DIM)
    ang = positions.astype(jnp.float32)[:, None] * inv_freq[None, :]
    cos = jnp.cos(ang)[None, :, None, :]
    sin = jnp.sin(ang)[None, :, None, :]
    xr = x_rot.astype(jnp.float32)
    x1, x2 = xr[..., :half], xr[..., half:]
    rot = jnp.concatenate([x1 * cos - x2 * sin, x1 * sin + x2 * cos], axis=-1)
    return jnp.concatenate([rot.astype(x.dtype), x_pass], axis=-1)


def fox_attention(q, k, v, log_f):
    b, s, h, d = q.shape
    nb = s // Q_BLOCK
    scale = HEAD_DIM ** -0.5
    c = jnp.cumsum(log_f, axis=1)
    c_k = jnp.transpose(c, (0, 2, 1))
    q_blocks = jnp.moveaxis(q.reshape(b, nb, Q_BLOCK, h, d), 1, 0)
    c_blocks = jnp.moveaxis(c_k.reshape(b, h, nb, Q_BLOCK), 2, 0)
    k_pos = jnp.arange(s)

    def block(args):
        idx, qi, ci = args
        logits = jnp.einsum('bqhd,bkhd->bhqk', qi, k, preferred_element_type=jnp.float32) * scale
        logits = logits + ci[..., :, None] - c_k[..., None, :]
        q_pos = idx * Q_BLOCK + jnp.arange(Q_BLOCK)
        causal = k_pos[None, :] <= q_pos[:, None]
        logits = jnp.where(causal, logits, -jnp.inf)
        p = jax.nn.softmax(logits, axis=-1)
        return jnp.einsum('bhqk,bkhd->bqhd', p.astype(v.dtype), v)

    out = lax.map(block, (jnp.arange(nb), q_blocks, c_blocks))
    return jnp.moveaxis(out, 0, 1).reshape(b, s, h, d)


def swa_sink_attention(q, k, v, sinks):
    b, s, hq, d = q.shape
    hkv = k.shape[2]
    g = hq // hkv
    nb = s // WINDOW
    scale = HEAD_DIM ** -0.5
    qb = q.reshape(b, nb, WINDOW, hkv, g, d)
    pad = ((0, 0), (WINDOW, 0), (0, 0), (0, 0))
    kb = jnp.pad(k, pad).reshape(b, nb + 1, WINDOW, hkv, d)
    vb = jnp.pad(v, pad).reshape(b, nb + 1, WINDOW, hkv, d)
    k_band = jnp.concatenate([kb[:, :-1], kb[:, 1:]], axis=2)
    v_band = jnp.concatenate([vb[:, :-1], vb[:, 1:]], axis=2)
    logits = jnp.einsum('bnqhgd,bnkhd->bnhgqk', qb, k_band, preferred_element_type=jnp.float32) * scale
    diff = (jnp.arange(WINDOW)[:, None] + WINDOW) - jnp.arange(2 * WINDOW)[None, :]
    in_window = (diff >= 0) & (diff < WINDOW)
    k_abs = jnp.arange(nb)[:, None] * WINDOW + jnp.arange(2 * WINDOW)[None, :] - WINDOW
    valid = in_window[None] & (k_abs >= 0)[:, None, :]
    logits = jnp.where(valid[None, :, None, None], logits, -jnp.inf)
    sink = jnp.broadcast_to(sinks.astype(jnp.float32).reshape(1, 1, hkv, g, 1, 1), logits.shape[:-1] + (1,))
    probs = jax.nn.softmax(jnp.concatenate([logits, sink], axis=-1), axis=-1)[..., :-1]
    out = jnp.einsum('bnhgqk,bnkhd->bnqhgd', probs.astype(v.dtype), v_band)
    return out.reshape(b, s, hq, d)


def _fwd_setup_inputs(seed: int = 0) -> dict:
    key = jax.random.key(seed)
    ks = jax.random.split(key, 20)
    f32 = jnp.float32
    in_a = 3 * WIDTH_A + N_HEADS_A + WIDTH_A
    in_b = WIDTH_B + WIDTH_B
    return {
        "x": jax.random.normal(ks[0], (BATCH, SEQ, D_MODEL), f32),
        "positions": jnp.arange(SEQ, dtype=jnp.int32),
        "norm_a_g": 1.0 + 0.02 * jax.random.normal(ks[1], (N_A_LAYERS, D_MODEL), f32),
        "w_in_a": jax.random.normal(ks[2], (N_A_LAYERS, D_MODEL, in_a), f32) * D_MODEL ** -0.5,
        "b_forget": 3.0 + 0.1 * jax.random.normal(ks[3], (N_A_LAYERS, N_HEADS_A), f32),
        "qnorm_a_g": 1.0 + 0.02 * jax.random.normal(ks[4], (N_A_LAYERS, HEAD_DIM), f32),
        "knorm_a_g": 1.0 + 0.02 * jax.random.normal(ks[5], (N_A_LAYERS, HEAD_DIM), f32),
        "w_out_a": jax.random.normal(ks[6], (N_A_LAYERS, WIDTH_A, D_MODEL), f32) * WIDTH_A ** -0.5,
        "kv_norm_g": 1.0 + 0.02 * jax.random.normal(ks[7], (D_MODEL,), f32),
        "w_kv": jax.random.normal(ks[8], (D_MODEL, 2 * KV_WIDTH_B), f32) * D_MODEL ** -0.5,
        "knorm_b_g": 1.0 + 0.02 * jax.random.normal(ks[9], (HEAD_DIM,), f32),
        "norm_b_g": 1.0 + 0.02 * jax.random.normal(ks[10], (N_B_LAYERS, D_MODEL), f32),
        "w_in_b": jax.random.normal(ks[11], (N_B_LAYERS, D_MODEL, in_b), f32) * D_MODEL ** -0.5,
        "qnorm_b_g": 1.0 + 0.02 * jax.random.normal(ks[12], (N_B_LAYERS, HEAD_DIM), f32),
        "sinks": 0.5 * jax.random.normal(ks[13], (N_B_LAYERS, N_Q_HEADS_B), f32),
        "w_out_b": jax.random.normal(ks[14], (N_B_LAYERS, WIDTH_B, D_MODEL), f32) * WIDTH_B ** -0.5,
    }


def _fwd_reference(x, positions, norm_a_g, w_in_a, b_forget, qnorm_a_g, knorm_a_g, w_out_a,
              kv_norm_g, w_kv, knorm_b_g, norm_b_g, w_in_b, qnorm_b_g, sinks, w_out_b):
    b, s, _ = x.shape
    h = x
    k_shared = None
    v_shared = None
    for layer in range(DEPTH):
        if layer < N_A_LAYERS:
            i = layer
            u = rmsnorm(h, norm_a_g[i])
            proj = u @ w_in_a[i]
            q, k, v, f_logit, gate = jnp.split(
                proj, [WIDTH_A, 2 * WIDTH_A, 3 * WIDTH_A, 3 * WIDTH_A + N_HEADS_A], axis=-1)
            q = rmsnorm(q.reshape(b, s, N_HEADS_A, HEAD_DIM), qnorm_a_g[i])
            k = rmsnorm(k.reshape(b, s, N_HEADS_A, HEAD_DIM), knorm_a_g[i])
            v = v.reshape(b, s, N_HEADS_A, HEAD_DIM)
            log_f = jax.nn.log_sigmoid((f_logit + b_forget[i]).astype(jnp.float32))
            o = fox_attention(q, k, v, log_f).reshape(b, s, WIDTH_A)
            h = h + (o * jax.nn.silu(gate)) @ w_out_a[i]
        else:
            if layer == N_A_LAYERS:
                u_kv = rmsnorm(h, kv_norm_g)
                k_s, v_s = jnp.split(u_kv @ w_kv, [KV_WIDTH_B], axis=-1)
                k_shared = partial_rope(rmsnorm(k_s.reshape(b, s, N_KV_HEADS_B, HEAD_DIM), knorm_b_g), positions)
                v_shared = v_s.reshape(b, s, N_KV_HEADS_B, HEAD_DIM)
            j = layer - N_A_LAYERS
            u = rmsnorm(h, norm_b_g[j])
            q, gate = jnp.split(u @ w_in_b[j], [WIDTH_B], axis=-1)
            q = partial_rope(rmsnorm(q.reshape(b, s, N_Q_HEADS_B, HEAD_DIM), qnorm_b_g[j]), positions)
            o = swa_sink_attention(q, k_shared, v_shared, sinks[j]).reshape(b, s, WIDTH_B)
            h = h + (o * jax.nn.silu(gate)) @ w_out_b[j]
    return h


import jax as _jax
import jax.numpy as _jnp

TWIN_FORMAT = 'train_step'
FWD_PARAMS = ['x', 'positions', 'norm_a_g', 'w_in_a', 'b_forget', 'qnorm_a_g', 'knorm_a_g', 'w_out_a', 'kv_norm_g', 'w_kv', 'knorm_b_g', 'norm_b_g', 'w_in_b', 'qnorm_b_g', 'sinks', 'w_out_b']
TWIN_WEIGHTS = ['norm_a_g', 'w_in_a', 'b_forget', 'qnorm_a_g', 'knorm_a_g', 'w_out_a', 'kv_norm_g', 'w_kv', 'knorm_b_g', 'norm_b_g', 'w_in_b', 'qnorm_b_g', 'sinks', 'w_out_b']
TWIN_DIFF_INPUT = 'x'
TWIN_INPUTS = ['x', 'positions', 'norm_a_g', 'w_in_a', 'b_forget', 'qnorm_a_g', 'knorm_a_g', 'w_out_a', 'kv_norm_g', 'w_kv', 'knorm_b_g', 'norm_b_g', 'w_in_b', 'qnorm_b_g', 'sinks', 'w_out_b', 'loss_target', 'm_norm_a_g', 'm_w_in_a', 'm_b_forget', 'm_qnorm_a_g', 'm_knorm_a_g', 'm_w_out_a', 'm_kv_norm_g', 'm_w_kv', 'm_knorm_b_g', 'm_norm_b_g', 'm_w_in_b', 'm_qnorm_b_g', 'm_sinks', 'm_w_out_b', 'v_norm_a_g', 'v_w_in_a', 'v_b_forget', 'v_qnorm_a_g', 'v_knorm_a_g', 'v_w_out_a', 'v_kv_norm_g', 'v_w_kv', 'v_knorm_b_g', 'v_norm_b_g', 'v_w_in_b', 'v_qnorm_b_g', 'v_sinks', 'v_w_out_b']
TWIN_OUTPUTS = ['loss', 'grad_x', 'grad_norm_a_g', 'grad_w_in_a', 'grad_b_forget', 'grad_qnorm_a_g', 'grad_knorm_a_g', 'grad_w_out_a', 'grad_kv_norm_g', 'grad_w_kv', 'grad_knorm_b_g', 'grad_norm_b_g', 'grad_w_in_b', 'grad_qnorm_b_g', 'grad_sinks', 'grad_w_out_b', 'delta_norm_a_g', 'delta_w_in_a', 'delta_b_forget', 'delta_qnorm_a_g', 'delta_knorm_a_g', 'delta_w_out_a', 'delta_kv_norm_g', 'delta_w_kv', 'delta_knorm_b_g', 'delta_norm_b_g', 'delta_w_in_b', 'delta_qnorm_b_g', 'delta_sinks', 'delta_w_out_b', 'new_m_norm_a_g', 'new_m_w_in_a', 'new_m_b_forget', 'new_m_qnorm_a_g', 'new_m_knorm_a_g', 'new_m_w_out_a', 'new_m_kv_norm_g', 'new_m_w_kv', 'new_m_knorm_b_g', 'new_m_norm_b_g', 'new_m_w_in_b', 'new_m_qnorm_b_g', 'new_m_sinks', 'new_m_w_out_b', 'new_v_norm_a_g', 'new_v_w_in_a', 'new_v_b_forget', 'new_v_qnorm_a_g', 'new_v_knorm_a_g', 'new_v_w_out_a', 'new_v_kv_norm_g', 'new_v_w_kv', 'new_v_knorm_b_g', 'new_v_norm_b_g', 'new_v_w_in_b', 'new_v_qnorm_b_g', 'new_v_sinks', 'new_v_w_out_b']
TWIN_LEAF_KINDS = {'loss': 'loss', 'grad_x': 'grad_x', 'grad_norm_a_g': 'grad_w', 'grad_w_in_a': 'grad_w', 'grad_b_forget': 'grad_w', 'grad_qnorm_a_g': 'grad_w', 'grad_knorm_a_g': 'grad_w', 'grad_w_out_a': 'grad_w', 'grad_kv_norm_g': 'grad_w', 'grad_w_kv': 'grad_w', 'grad_knorm_b_g': 'grad_w', 'grad_norm_b_g': 'grad_w', 'grad_w_in_b': 'grad_w', 'grad_qnorm_b_g': 'grad_w', 'grad_sinks': 'grad_w', 'grad_w_out_b': 'grad_w', 'delta_norm_a_g': 'delta_w', 'delta_w_in_a': 'delta_w', 'delta_b_forget': 'delta_w', 'delta_qnorm_a_g': 'delta_w', 'delta_knorm_a_g': 'delta_w', 'delta_w_out_a': 'delta_w', 'delta_kv_norm_g': 'delta_w', 'delta_w_kv': 'delta_w', 'delta_knorm_b_g': 'delta_w', 'delta_norm_b_g': 'delta_w', 'delta_w_in_b': 'delta_w', 'delta_qnorm_b_g': 'delta_w', 'delta_sinks': 'delta_w', 'delta_w_out_b': 'delta_w', 'new_m_norm_a_g': 'new_m', 'new_m_w_in_a': 'new_m', 'new_m_b_forget': 'new_m', 'new_m_qnorm_a_g': 'new_m', 'new_m_knorm_a_g': 'new_m', 'new_m_w_out_a': 'new_m', 'new_m_kv_norm_g': 'new_m', 'new_m_w_kv': 'new_m', 'new_m_knorm_b_g': 'new_m', 'new_m_norm_b_g': 'new_m', 'new_m_w_in_b': 'new_m', 'new_m_qnorm_b_g': 'new_m', 'new_m_sinks': 'new_m', 'new_m_w_out_b': 'new_m', 'new_v_norm_a_g': 'new_v', 'new_v_w_in_a': 'new_v', 'new_v_b_forget': 'new_v', 'new_v_qnorm_a_g': 'new_v', 'new_v_knorm_a_g': 'new_v', 'new_v_w_out_a': 'new_v', 'new_v_kv_norm_g': 'new_v', 'new_v_w_kv': 'new_v', 'new_v_knorm_b_g': 'new_v', 'new_v_norm_b_g': 'new_v', 'new_v_w_in_b': 'new_v', 'new_v_qnorm_b_g': 'new_v', 'new_v_sinks': 'new_v', 'new_v_w_out_b': 'new_v'}


def _forward(args):
    return _fwd_reference(*[args[k] for k in FWD_PARAMS])


def _output_shape():
    out = _jax.eval_shape(lambda: _forward(_fwd_setup_inputs(0)))
    return out.shape, out.dtype

N_MICROBATCH = 1
ADAM_LR = 0.001
ADAM_B1 = 0.9
ADAM_B2 = 0.999
ADAM_EPS = 1e-08
ADAM_WD = 0.01
ADAM_STEP = 10
PER_EXAMPLE_BATCH_AXIS = {'x': 0, 'loss_target': 0}
SHARED_INPUTS = ['positions']
_WEIGHT_DTYPES = {'norm_a_g': _jnp.float32, 'w_in_a': _jnp.float32, 'b_forget': _jnp.float32, 'qnorm_a_g': _jnp.float32, 'knorm_a_g': _jnp.float32, 'w_out_a': _jnp.float32, 'kv_norm_g': _jnp.float32, 'w_kv': _jnp.float32, 'knorm_b_g': _jnp.float32, 'norm_b_g': _jnp.float32, 'w_in_b': _jnp.float32, 'qnorm_b_g': _jnp.float32, 'sinks': _jnp.float32, 'w_out_b': _jnp.float32}
MOMENT_SCALE = {'norm_a_g': 1.070797e+00, 'w_in_a': 3.892487e-02, 'b_forget': 9.747994e+00, 'qnorm_a_g': 4.036772e+00, 'knorm_a_g': 4.006484e+00, 'w_out_a': 3.673448e-02, 'kv_norm_g': 8.897323e-02, 'w_kv': 5.038521e-02, 'knorm_b_g': 1.249334e+00, 'norm_b_g': 1.038099e-01, 'w_in_b': 2.310879e-02, 'qnorm_b_g': 1.250624e+00, 'sinks': 2.043102e-01, 'w_out_b': 2.219589e-02}


def _to_microbatches(a, axis):
    t = _jnp.moveaxis(a, axis, 0)
    t = t.reshape((N_MICROBATCH, t.shape[0] // N_MICROBATCH) + t.shape[1:])
    return _jnp.moveaxis(t, 1, axis + 1)


def setup_inputs(seed: int = 0) -> dict:
    inp = _fwd_setup_inputs(seed)
    key = _jax.random.fold_in(_jax.random.key(seed), 7919)
    shape, _ = _output_shape()
    out = dict(inp)
    out["loss_target"] = _jax.random.normal(_jax.random.fold_in(key, 0), shape, _jnp.float32)
    for i, name in enumerate(TWIN_WEIGHTS):
        w = inp[name].astype(_jnp.float32)
        if MOMENT_SCALE is None:
            s = _jnp.sqrt(_jnp.mean(_jnp.square(w)) + 1e-30)
        else:
            s = MOMENT_SCALE[name]
        km, kv = _jax.random.split(_jax.random.fold_in(key, i + 1))
        out[name] = w
        out["m_" + name] = s * _jax.random.normal(km, w.shape, _jnp.float32)
        out["v_" + name] = (s * s) * _jax.random.uniform(kv, w.shape, _jnp.float32, 0.5, 1.5)
    if N_MICROBATCH > 1:
        for name, axis in PER_EXAMPLE_BATCH_AXIS.items():
            out[name] = _to_microbatches(out[name], axis)
    return {'x': out['x'], 'positions': out['positions'], 'norm_a_g': out['norm_a_g'], 'w_in_a': out['w_in_a'], 'b_forget': out['b_forget'], 'qnorm_a_g': out['qnorm_a_g'], 'knorm_a_g': out['knorm_a_g'], 'w_out_a': out['w_out_a'], 'kv_norm_g': out['kv_norm_g'], 'w_kv': out['w_kv'], 'knorm_b_g': out['knorm_b_g'], 'norm_b_g': out['norm_b_g'], 'w_in_b': out['w_in_b'], 'qnorm_b_g': out['qnorm_b_g'], 'sinks': out['sinks'], 'w_out_b': out['w_out_b'], 'loss_target': out['loss_target'], 'm_norm_a_g': out['m_norm_a_g'], 'm_w_in_a': out['m_w_in_a'], 'm_b_forget': out['m_b_forget'], 'm_qnorm_a_g': out['m_qnorm_a_g'], 'm_knorm_a_g': out['m_knorm_a_g'], 'm_w_out_a': out['m_w_out_a'], 'm_kv_norm_g': out['m_kv_norm_g'], 'm_w_kv': out['m_w_kv'], 'm_knorm_b_g': out['m_knorm_b_g'], 'm_norm_b_g': out['m_norm_b_g'], 'm_w_in_b': out['m_w_in_b'], 'm_qnorm_b_g': out['m_qnorm_b_g'], 'm_sinks': out['m_sinks'], 'm_w_out_b': out['m_w_out_b'], 'v_norm_a_g': out['v_norm_a_g'], 'v_w_in_a': out['v_w_in_a'], 'v_b_forget': out['v_b_forget'], 'v_qnorm_a_g': out['v_qnorm_a_g'], 'v_knorm_a_g': out['v_knorm_a_g'], 'v_w_out_a': out['v_w_out_a'], 'v_kv_norm_g': out['v_kv_norm_g'], 'v_w_kv': out['v_w_kv'], 'v_knorm_b_g': out['v_knorm_b_g'], 'v_norm_b_g': out['v_norm_b_g'], 'v_w_in_b': out['v_w_in_b'], 'v_qnorm_b_g': out['v_qnorm_b_g'], 'v_sinks': out['v_sinks'], 'v_w_out_b': out['v_w_out_b']}


def _loss(weights, diff, rest, loss_target):
    with _jax.named_scope("forward"):
        args = {**rest, TWIN_DIFF_INPUT: diff, **{k: w.astype(_WEIGHT_DTYPES[k]) for k, w in weights.items()}}
        y = _forward(args)
    with _jax.named_scope("loss_head"):
        err = _jnp.square(y.astype(_jnp.float32) - loss_target)
        return 0.5 * _jnp.sum(_jnp.mean(err, axis=-1)) if err.ndim else 0.5 * err


def _adamw(w, g, m, v):
    m = ADAM_B1 * m + (1.0 - ADAM_B1) * g
    v = ADAM_B2 * v + (1.0 - ADAM_B2) * _jnp.square(g)
    m_hat = m / (1.0 - ADAM_B1 ** ADAM_STEP)
    v_hat = v / (1.0 - ADAM_B2 ** ADAM_STEP)
    delta = -ADAM_LR * (m_hat / (_jnp.sqrt(v_hat) + ADAM_EPS) + ADAM_WD * w)
    return delta, m, v


def reference(x, positions, norm_a_g, w_in_a, b_forget, qnorm_a_g, knorm_a_g, w_out_a, kv_norm_g, w_kv, knorm_b_g, norm_b_g, w_in_b, qnorm_b_g, sinks, w_out_b, loss_target, m_norm_a_g, m_w_in_a, m_b_forget, m_qnorm_a_g, m_knorm_a_g, m_w_out_a, m_kv_norm_g, m_w_kv, m_knorm_b_g, m_norm_b_g, m_w_in_b, m_qnorm_b_g, m_sinks, m_w_out_b, v_norm_a_g, v_w_in_a, v_b_forget, v_qnorm_a_g, v_knorm_a_g, v_w_out_a, v_kv_norm_g, v_w_kv, v_knorm_b_g, v_norm_b_g, v_w_in_b, v_qnorm_b_g, v_sinks, v_w_out_b):
    given = dict(x=x, positions=positions, norm_a_g=norm_a_g, w_in_a=w_in_a, b_forget=b_forget, qnorm_a_g=qnorm_a_g, knorm_a_g=knorm_a_g, w_out_a=w_out_a, kv_norm_g=kv_norm_g, w_kv=w_kv, knorm_b_g=knorm_b_g, norm_b_g=norm_b_g, w_in_b=w_in_b, qnorm_b_g=qnorm_b_g, sinks=sinks, w_out_b=w_out_b, loss_target=loss_target, m_norm_a_g=m_norm_a_g, m_w_in_a=m_w_in_a, m_b_forget=m_b_forget, m_qnorm_a_g=m_qnorm_a_g, m_knorm_a_g=m_knorm_a_g, m_w_out_a=m_w_out_a, m_kv_norm_g=m_kv_norm_g, m_w_kv=m_w_kv, m_knorm_b_g=m_knorm_b_g, m_norm_b_g=m_norm_b_g, m_w_in_b=m_w_in_b, m_qnorm_b_g=m_qnorm_b_g, m_sinks=m_sinks, m_w_out_b=m_w_out_b, v_norm_a_g=v_norm_a_g, v_w_in_a=v_w_in_a, v_b_forget=v_b_forget, v_qnorm_a_g=v_qnorm_a_g, v_knorm_a_g=v_knorm_a_g, v_w_out_a=v_w_out_a, v_kv_norm_g=v_kv_norm_g, v_w_kv=v_w_kv, v_knorm_b_g=v_knorm_b_g, v_norm_b_g=v_norm_b_g, v_w_in_b=v_w_in_b, v_qnorm_b_g=v_qnorm_b_g, v_sinks=v_sinks, v_w_out_b=v_w_out_b)
    weights = {n: given[n] for n in TWIN_WEIGHTS}
    shared = {n: given[n] for n in SHARED_INPUTS}
    per_example = {n: given[n] for n in ['x']}
    grad_fn = _jax.value_and_grad(_loss, argnums=(0, 1))

    def one_microbatch(ex, loss_target):
        ex = dict(ex)
        diff = ex.pop(TWIN_DIFF_INPUT)
        return grad_fn(weights, diff, {**shared, **ex}, loss_target)

    if N_MICROBATCH == 1:
        loss, (grad_w, grad_x) = one_microbatch(per_example, given["loss_target"])
    else:
        def body(carry, xs):
            loss_sum, grad_sum = carry
            l_k, (gw_k, gx_k) = one_microbatch(xs[0], xs[1])
            with _jax.named_scope("update"):
                return (loss_sum + l_k, _jax.tree.map(_jnp.add, grad_sum, gw_k)), gx_k

        init = (_jnp.zeros((), _jnp.float32), _jax.tree.map(_jnp.zeros_like, weights))
        (loss, grad_w), grad_x = _jax.lax.scan(body, init, (per_example, given["loss_target"]))
    with _jax.named_scope("update"):
        delta_w, new_m, new_v = {}, {}, {}
        for n in TWIN_WEIGHTS:
            delta_w[n], new_m[n], new_v[n] = _adamw(weights[n], grad_w[n], given["m_" + n], given["v_" + n])
    return (loss, grad_x, *[grad_w[n] for n in TWIN_WEIGHTS], *[delta_w[n] for n in TWIN_WEIGHTS],
            *[new_m[n] for n in TWIN_WEIGHTS], *[new_v[n] for n in TWIN_WEIGHTS])
```

```python
import functools

import jax
import jax.numpy as jnp
from jax import lax
from jax.experimental import pallas as pl
from jax.experimental.pallas import tpu as pltpu

F32, BF16 = jnp.float32, jnp.bfloat16

S = 2048
D = 1024
HD = 64
N_DEV = 8
NA = 4352
FOFF = 3072
GOFF = 3328
NA_RAW = 4112
EPS = 1e-6
QSCALE = 0.125
ROPE_THETA = 500000.0
ROT = 16
WIN = 128
TQ = 256
TM = 256
LANES = 128

ADAM_LR, ADAM_B1, ADAM_B2, ADAM_EPS, ADAM_WD, ADAM_STEP = 0.001, 0.9, 0.999, 1e-08, 0.01, 10

VMEM_LIMIT = 48 * 1024 * 1024


def _params():
    return pltpu.CompilerParams(vmem_limit_bytes=VMEM_LIMIT)


def _sds(shape, dtype):
    return jax.ShapeDtypeStruct(shape, dtype)


def _dot_nt(a, b):
    return lax.dot_general(a, b, (((1,), (1,)), ((), ())), preferred_element_type=F32)


def _dot_tn(a, b):
    return lax.dot_general(a, b, (((0,), (0,)), ((), ())), preferred_element_type=F32)


def _dot_nn(a, b):
    return lax.dot_general(a, b, (((1,), (0,)), ((), ())), preferred_element_type=F32)


def _sigmoid(g):
    return 1.0 / (1.0 + jnp.exp(-g))


def _lane_iota(shape):
    return lax.broadcasted_iota(jnp.int32, shape, len(shape) - 1)


def _mm(a, b, mode, tm, tn, tk, out_dtype=F32, add=None, name="mm"):
    if mode == "nn":
        (m, k), n = a.shape, b.shape[1]
        a_spec = pl.BlockSpec((tm, tk), lambda i, j, kk: (i, kk))
        b_spec = pl.BlockSpec((tk, tn), lambda i, j, kk: (kk, j))
        dot = _dot_nn
    elif mode == "nt":
        (m, k), n = a.shape, b.shape[0]
        a_spec = pl.BlockSpec((tm, tk), lambda i, j, kk: (i, kk))
        b_spec = pl.BlockSpec((tn, tk), lambda i, j, kk: (j, kk))
        dot = _dot_nt
    else:
        (k, m), n = a.shape, b.shape[1]
        a_spec = pl.BlockSpec((tk, tm), lambda i, j, kk: (kk, i))
        b_spec = pl.BlockSpec((tk, tn), lambda i, j, kk: (kk, j))
        dot = _dot_tn
    assert m % tm == 0 and n % tn == 0 and k % tk == 0, (m, n, k, tm, tn, tk)
    nk = k // tk
    has_add = add is not None

    def body(*refs):
        if has_add:
            a_ref, b_ref, add_ref, o_ref, acc = refs
        else:
            a_ref, b_ref, o_ref, acc = refs
        p = dot(a_ref[...].astype(BF16), b_ref[...].astype(BF16))

        def finish(total):
            if has_add:
                total = add_ref[...] + total
            o_ref[...] = total.astype(out_dtype)

        if nk == 1:
            finish(p)
        else:
            kk = pl.program_id(2)

            @pl.when(kk == 0)
            def _():
                acc[...] = p

            @pl.when(kk > 0)
            def _():
                acc[...] += p

            @pl.when(kk == nk - 1)
            def _():
                finish(acc[...])

    in_specs = [a_spec, b_spec]
    args = [a, b]
    if has_add:
        in_specs.append(pl.BlockSpec((tm, tn), lambda i, j, kk: (i, j)))
        args.append(add)
    acc_shape = (tm, tn) if nk > 1 else (8, LANES)
    return pl.pallas_call(
        body, name=name, grid=(m // tm, n // tn, nk), in_specs=in_specs,
        out_specs=pl.BlockSpec((tm, tn), lambda i, j, kk: (i, j)),
        out_shape=_sds((m, n), out_dtype), scratch_shapes=[pltpu.VMEM(acc_shape, F32)],
        compiler_params=_params())(*args)


def _rms_rinv(x):
    return lax.rsqrt(jnp.mean(x * x, axis=-1, keepdims=True) + EPS)


def _rms_bwd_core(du, x, g):
    r = _rms_rinv(x)
    dug = du * g
    dx = r * (dug - x * ((r * r) * jnp.mean(dug * x, axis=-1, keepdims=True)))
    dg = jnp.sum(du * (x * r), axis=0, keepdims=True)
    return dx, dg


def _half_sum(v, lo_half):
    s0 = jnp.sum(jnp.where(lo_half, v, 0.0), axis=-1, keepdims=True)
    s1 = jnp.sum(jnp.where(lo_half, 0.0, v), axis=-1, keepdims=True)
    return jnp.where(lo_half, s0, s1)


def _head_rinv(x, lo_half):
    return lax.rsqrt(_half_sum(x * x, lo_half) * (1.0 / HD) + EPS)


def _head_norm_bwd(dn, x, g, lo_half):
    r = _head_rinv(x, lo_half)
    dng = dn * g
    dx = r * (dng - x * ((r * r) * (_half_sum(dng * x, lo_half) * (1.0 / HD))))
    dg = jnp.sum(dn * (x * r), axis=0, keepdims=True)
    return dx, dg


def _rope_swap(x, lane):
    l64 = lane & (HD - 1)
    return jnp.where(l64 < ROT // 2, pltpu.roll(x, LANES - ROT // 2, 1), pltpu.roll(x, ROT // 2, 1))


def _rope_fwd(x, cos, sin, lane):
    return x * cos + _rope_swap(x, lane) * sin


def _rope_bwd(dy, cos, sin, lane):
    return dy * cos + jnp.where((lane & (HD - 1)) < ROT, _rope_swap(dy * sin, lane), 0.0)


def _g2(g_ref):
    g = g_ref[...]
    return jnp.concatenate([g, g], axis=-1)


def _rms_fwd(x, g, name):
    def body(x_ref, g_ref, u_ref):
        xv = x_ref[...]
        u_ref[...] = ((xv * _rms_rinv(xv)) * g_ref[...]).astype(BF16)

    return pl.pallas_call(
        body, name=name, grid=(S // TM,),
        in_specs=[pl.BlockSpec((TM, D), lambda i: (i, 0)), pl.BlockSpec((1, D), lambda i: (0, 0))],
        out_specs=pl.BlockSpec((TM, D), lambda i: (i, 0)), out_shape=_sds((S, D), BF16),
        compiler_params=_params())(x, g)


def _rms2_fwd(h, g1, g2):
    def body(x_ref, g1_ref, g2_ref, u1_ref, u2_ref):
        xv = x_ref[...]
        xn = xv * _rms_rinv(xv)
        u1_ref[...] = (xn * g1_ref[...]).astype(BF16)
        u2_ref[...] = (xn * g2_ref[...]).astype(BF16)

    row = pl.BlockSpec((TM, D), lambda i: (i, 0))
    vec = pl.BlockSpec((1, D), lambda i: (0, 0))
    return pl.pallas_call(
        body, name="rms2_fwd", grid=(S // TM,), in_specs=[row, vec, vec], out_specs=[row, row],
        out_shape=[_sds((S, D), BF16)] * 2, compiler_params=_params())(h, g1, g2)


def _prep_a(proj, gq, gk):
    def body(q_ref, k_ref, v_ref, gq_ref, gk_ref, qo_ref, ko_ref, vo_ref):
        lo_half = _lane_iota((TM, LANES)) < HD
        q = q_ref[...]
        k = k_ref[...]
        qo_ref[...] = (((q * _head_rinv(q, lo_half)) * _g2(gq_ref)) * QSCALE).astype(BF16)
        ko_ref[...] = ((k * _head_rinv(k, lo_half)) * _g2(gk_ref)).astype(BF16)
        vo_ref[...] = v_ref[...].astype(BF16)

    blk = lambda off: pl.BlockSpec((TM, LANES), lambda i, j: (i, j + off))
    gspec = pl.BlockSpec((1, HD), lambda i, j: (0, 0))
    return pl.pallas_call(
        body, name="prep_a", grid=(S // TM, 8),
        in_specs=[blk(0), blk(8), blk(16), gspec, gspec], out_specs=[blk(0)] * 3,
        out_shape=[_sds((S, D), BF16)] * 3, compiler_params=_params())(proj, proj, proj, gq, gk)


def _fgate_fwd(proj, b_pad):
    def body(f_ref, b_ref, c_ref, carry):
        @pl.when(pl.program_id(0) == 0)
        def _():
            carry[...] = jnp.zeros_like(carry)

        z = f_ref[...] + b_ref[...]
        logf = jnp.minimum(z, 0.0) - jnp.log1p(jnp.exp(-jnp.abs(z)))
        r = lax.broadcasted_iota(jnp.int32, (TM, TM), 0)
        c = lax.broadcasted_iota(jnp.int32, (TM, TM), 1)
        tri = (r >= c).astype(F32)
        loc = jnp.dot(tri, logf, precision=lax.Precision.HIGHEST, preferred_element_type=F32) + carry[0:1, :]
        c_ref[...] = loc
        carry[0:1, :] = loc[TM - 1:TM, :]

    return pl.pallas_call(
        body, name="fgate_fwd", grid=(S // TM,),
        in_specs=[pl.BlockSpec((TM, LANES), lambda i: (i, FOFF // LANES)), pl.BlockSpec((1, LANES), lambda i: (0, 0))],
        out_specs=pl.BlockSpec((TM, LANES), lambda i: (i, 0)), out_shape=_sds((S, LANES), F32),
        scratch_shapes=[pltpu.VMEM((8, LANES), F32)], compiler_params=_params())(proj, b_pad)


def _fgate_bwd(dproj, dc, proj, b_pad):
    nt = S // TM

    def body(dp_ref, dc_ref, f_ref, b_ref, o_ref, db_ref, carry):
        @pl.when(pl.program_id(0) == 0)
        def _():
            carry[...] = jnp.zeros_like(carry)
            db_ref[...] = jnp.zeros_like(db_ref)

        r = lax.broadcasted_iota(jnp.int32, (TM, TM), 0)
        c = lax.broadcasted_iota(jnp.int32, (TM, TM), 1)
        tri = (c >= r).astype(F32)
        dlogf = jnp.dot(tri, dc_ref[...], precision=lax.Precision.HIGHEST, preferred_element_type=F32) + carry[0:1, :]
        carry[0:1, :] = dlogf[0:1, :]
        z = f_ref[...] + b_ref[...]
        df = dlogf * (1.0 / (1.0 + jnp.exp(z)))
        db_ref[...] += jnp.sum(df, axis=0, keepdims=True)
        o_ref[...] = jnp.concatenate([df, jnp.zeros_like(df)], axis=-1).astype(BF16)

    return pl.pallas_call(
        body, name="fgate_bwd", grid=(nt,),
        in_specs=[pl.BlockSpec(memory_space=pl.ANY),
                  pl.BlockSpec((TM, LANES), lambda i: (nt - 1 - i, 0)),
                  pl.BlockSpec((TM, LANES), lambda i: (nt - 1 - i, FOFF // LANES)),
                  pl.BlockSpec((1, LANES), lambda i: (0, 0))],
        out_specs=[pl.BlockSpec((TM, 2 * LANES), lambda i: (nt - 1 - i, FOFF // (2 * LANES))),
                   pl.BlockSpec((1, LANES), lambda i: (0, 0))],
        out_shape=[_sds((S, NA), BF16), _sds((1, LANES), F32)],
        scratch_shapes=[pltpu.VMEM((8, LANES), F32)], input_output_aliases={0: 0},
        compiler_params=_params())(dproj, dc, proj, b_pad)


def _causal_mask():
    r = lax.broadcasted_iota(jnp.int32, (TQ, TQ), 0)
    c = lax.broadcasted_iota(jnp.int32, (TQ, TQ), 1)
    return c <= r


def _pick_lane(block, lane, idx):
    return jnp.sum(jnp.where(lane == idx, block, 0.0), axis=-1, keepdims=True)


def _fox_fwd(qn, kn, vb, proj, ccol, crow):
    nq = S // TQ

    def body(q_ref, k_ref, v_ref, g_ref, cc_ref, cr_ref, o_ref, z_ref, lse_ref):
        hp, i = pl.program_id(0), pl.program_id(1)
        lane = _lane_iota((TQ, LANES))
        causal = _causal_mask()
        ccb = cc_ref[...]
        outs, lses = [], []
        for hh in range(2):
            lo = HD * hh
            q = q_ref[:, lo:lo + HD]
            ccol_h = _pick_lane(ccb, lane, 2 * hp + hh)

            def blk(j, carry, masked, q=q, ccol_h=ccol_h, lo=lo, hh=hh):
                m, l, acc = carry
                off = pl.multiple_of(j * TQ, TQ)
                kj = k_ref[pl.ds(off, TQ), lo:lo + HD]
                vj = v_ref[pl.ds(off, TQ), lo:lo + HD]
                s = (_dot_nt(q, kj) + ccol_h) - cr_ref[hh, j]
                if masked:
                    s = jnp.where(causal, s, -jnp.inf)
                m_new = jnp.maximum(m, jnp.max(s, axis=-1, keepdims=True))
                p = jnp.exp(s - m_new)
                alpha = jnp.exp(m - m_new)
                l = alpha * l + jnp.sum(p, axis=-1, keepdims=True)
                acc = alpha * acc + _dot_nn(p.astype(BF16), vj)
                return m_new, l, acc

            init = (jnp.full((TQ, 1), -jnp.inf, F32), jnp.zeros((TQ, 1), F32), jnp.zeros((TQ, HD), F32))
            carry = lax.fori_loop(0, i, lambda j, cr: blk(j, cr, False), init)
            m, l, acc = blk(i, carry, True)
            outs.append(acc / l)
            lses.append(m + jnp.log(l))
        o = jnp.concatenate(outs, axis=-1)
        o_ref[...] = o
        g = g_ref[...]
        z_ref[...] = (o * (g * _sigmoid(g))).astype(BF16)
        lse_ref[...] = jnp.where(lane < HD, lses[0], lses[1])

    qblk = pl.BlockSpec((TQ, LANES), lambda hp, i: (i, hp))
    full = pl.BlockSpec((S, LANES), lambda hp, i: (0, hp))
    return pl.pallas_call(
        body, name="fox_fwd", grid=(8, nq),
        in_specs=[qblk, full, full,
                  pl.BlockSpec((TQ, LANES), lambda hp, i: (i, GOFF // LANES + hp)),
                  pl.BlockSpec((TQ, LANES), lambda hp, i: (i, 0)),
                  pl.BlockSpec((2, nq, 1, TQ), lambda hp, i: (hp, 0, 0, 0))],
        out_specs=[qblk, qblk, qblk],
        out_shape=[_sds((S, D), F32), _sds((S, D), BF16), _sds((S, D), F32)],
        compiler_params=_params())(qn, kn, vb, proj, ccol, crow)


def _fox_bwd_dq(qn, kn, vb, dz, proj, o, lse, ccol, crow):
    nq = S // TQ

    def body(q_ref, k_ref, v_ref, dz_ref, g_ref, o_ref, lse_ref, cc_ref, cr_ref,
             dq_ref, delta_ref, do_ref, dg_ref, dr_ref):
        hp, i = pl.program_id(0), pl.program_id(1)
        lane = _lane_iota((TQ, LANES))
        lo_half = lane < HD
        drs = []
        causal = _causal_mask()
        g = g_ref[...]
        sg = _sigmoid(g)
        dzv = dz_ref[...]
        ov = o_ref[...]
        do = dzv * (g * sg)
        dg_ref[...] = (dzv * ov * (sg * (1.0 + g * (1.0 - sg)))).astype(BF16)
        do_ref[...] = do.astype(BF16)
        delta = _half_sum(do * ov, lo_half)
        delta_ref[...] = delta
        ccb = cc_ref[...]
        dqs = []
        for hh in range(2):
            lo = HD * hh
            q = q_ref[:, lo:lo + HD]
            dob = do[:, lo:lo + HD].astype(BF16)
            lse_h = lse_ref[:, lo:lo + 1]
            del_h = delta[:, lo:lo + 1]
            ccol_h = _pick_lane(ccb, lane, 2 * hp + hh)

            def blk(j, carry, masked, q=q, dob=dob, lse_h=lse_h, del_h=del_h, ccol_h=ccol_h, lo=lo, hh=hh):
                dq, dr = carry
                off = pl.multiple_of(j * TQ, TQ)
                kj = k_ref[pl.ds(off, TQ), lo:lo + HD]
                vj = v_ref[pl.ds(off, TQ), lo:lo + HD]
                s = (_dot_nt(q, kj) + ccol_h) - cr_ref[hh, j]
                if masked:
                    s = jnp.where(causal, s, -jnp.inf)
                p = jnp.exp(s - lse_h)
                ds = p * (_dot_nt(dob, vj) - del_h)
                return dq + _dot_nn(ds.astype(BF16), kj), dr + jnp.sum(ds, axis=-1, keepdims=True)

            init = (jnp.zeros((TQ, HD), F32), jnp.zeros((TQ, 1), F32))
            dq, dr = blk(i, lax.fori_loop(0, i, lambda j, a: blk(j, a, False), init), True)
            dqs.append(dq)
            drs.append(dr)
        dq_ref[...] = jnp.concatenate(dqs, axis=-1)
        dr_ref[...] = jnp.where(lo_half, drs[0], drs[1])

    qblk = pl.BlockSpec((TQ, LANES), lambda hp, i: (i, hp))
    full = pl.BlockSpec((S, LANES), lambda hp, i: (0, hp))
    return pl.pallas_call(
        body, name="fox_bwd_dq", grid=(8, nq),
        in_specs=[qblk, full, full, qblk,
                  pl.BlockSpec((TQ, LANES), lambda hp, i: (i, GOFF // LANES + hp)),
                  qblk, qblk,
                  pl.BlockSpec((TQ, LANES), lambda hp, i: (i, 0)),
                  pl.BlockSpec((2, nq, 1, TQ), lambda hp, i: (hp, 0, 0, 0))],
        out_specs=[qblk, qblk, qblk, qblk, qblk],
        out_shape=[_sds((S, D), F32), _sds((S, D), F32), _sds((S, D), BF16), _sds((S, D), BF16), _sds((S, D), F32)],
        compiler_params=_params())(qn, kn, vb, dz, proj, o, lse, ccol, crow)


def _fox_bwd_dkv(qn, kn, vb, dob, lse, delta, ccol, crow):
    nq = S // TQ

    def body(q_ref, k_ref, v_ref, do_ref, lse_ref, del_ref, cc_ref, cr_ref, dk_ref, dv_ref, dc_ref):
        hp, j = pl.program_id(0), pl.program_id(1)
        lane = _lane_iota((TQ, LANES))
        causal = _causal_mask()
        dks, dvs = [], []
        for hh in range(2):
            lo = HD * hh
            kj = k_ref[:, lo:lo + HD]
            vj = v_ref[:, lo:lo + HD]
            crow_h = cr_ref[hh, 0]

            def blk(i, carry, masked, kj=kj, vj=vj, crow_h=crow_h, lo=lo, hh=hh):
                dk, dv, dc = carry
                off = pl.multiple_of(i * TQ, TQ)
                qi = q_ref[pl.ds(off, TQ), lo:lo + HD]
                doi = do_ref[pl.ds(off, TQ), lo:lo + HD]
                lse_i = lse_ref[pl.ds(off, TQ), lo:lo + 1]
                del_i = del_ref[pl.ds(off, TQ), lo:lo + 1]
                ccol_i = _pick_lane(cc_ref[pl.ds(off, TQ), :], lane, 2 * hp + hh)
                s = (_dot_nt(qi, kj) + ccol_i) - crow_h
                if masked:
                    s = jnp.where(causal, s, -jnp.inf)
                p = jnp.exp(s - lse_i)
                dv = dv + _dot_tn(p.astype(BF16), doi)
                ds = p * (_dot_nt(doi, vj) - del_i)
                dk = dk + _dot_tn(ds.astype(BF16), qi)
                dc = dc - jnp.sum(ds, axis=0, keepdims=True)
                return dk, dv, dc

            init = (jnp.zeros((TQ, HD), F32), jnp.zeros((TQ, HD), F32), jnp.zeros((1, TQ), F32))
            carry = blk(j, init, True)
            dk, dv, dc = lax.fori_loop(j + 1, nq, lambda i, cr: blk(i, cr, False), carry)
            dks.append(dk)
            dvs.append(dv)
            dc_ref[hh, 0] = dc
        dk_ref[...] = jnp.concatenate(dks, axis=-1)
        dv_ref[...] = jnp.concatenate(dvs, axis=-1)

    kblk = pl.BlockSpec((TQ, LANES), lambda hp, j: (j, hp))
    full = pl.BlockSpec((S, LANES), lambda hp, j: (0, hp))
    cblk = pl.BlockSpec((2, 1, 1, TQ), lambda hp, j: (hp, j, 0, 0))
    return pl.pallas_call(
        body, name="fox_bwd_dkv", grid=(8, nq),
        in_specs=[full, kblk, kblk, full, full, full, pl.BlockSpec((S, LANES), lambda hp, j: (0, 0)), cblk],
        out_specs=[kblk, kblk, cblk],
        out_shape=[_sds((S, D), F32), _sds((S, D), F32), _sds((16, nq, 1, TQ), F32)],
        compiler_params=_params())(qn, kn, vb, dob, lse, delta, ccol, crow)


def _prep_a_bwd(dq, dk, dv, dgate, proj, gq, gk):
    def body(dq_ref, dk_ref, dv_ref, dgt_ref, x_ref, gq_ref, gk_ref, o_ref, dgq_ref, dgk_ref):
        i, j = pl.program_id(0), pl.program_id(1)

        @pl.when((i == 0) & (j == 0))
        def _():
            dgq_ref[...] = jnp.zeros_like(dgq_ref)
            dgk_ref[...] = jnp.zeros_like(dgk_ref)

        @pl.when(j < 16)
        def _():
            lo_half = _lane_iota((TM, LANES)) < HD
            is_q = j < 8
            dn = jnp.where(is_q, dq_ref[...] * QSCALE, dk_ref[...])
            g = jnp.where(is_q, _g2(gq_ref), _g2(gk_ref))
            dx, dg = _head_norm_bwd(dn, x_ref[...], g, lo_half)
            o_ref[...] = dx.astype(BF16)
            dgq_ref[...] += jnp.where(is_q, dg, 0.0)
            dgk_ref[...] += jnp.where(is_q, 0.0, dg)

        @pl.when((j >= 16) & (j < 24))
        def _():
            o_ref[...] = dv_ref[...].astype(BF16)

        @pl.when(j >= 24)
        def _():
            o_ref[...] = dgt_ref[...]

    clamp = lambda off: (lambda i, j: (i, jnp.clip(j - off, 0, 7)))
    gspec = pl.BlockSpec((1, HD), lambda i, j: (0, 0))
    acc = pl.BlockSpec((1, LANES), lambda i, j: (0, 0))
    return pl.pallas_call(
        body, name="prep_a_bwd", grid=(S // TM, 32),
        in_specs=[pl.BlockSpec((TM, LANES), clamp(0)), pl.BlockSpec((TM, LANES), clamp(8)),
                  pl.BlockSpec((TM, LANES), clamp(16)), pl.BlockSpec((TM, LANES), clamp(24)),
                  pl.BlockSpec((TM, LANES), lambda i, j: (i, jnp.minimum(j, 15))), gspec, gspec],
        out_specs=[pl.BlockSpec((TM, LANES), lambda i, j: (i, jnp.where(j < 24, j, j + 2))), acc, acc],
        out_shape=[_sds((S, NA), BF16), _sds((1, LANES), F32), _sds((1, LANES), F32)],
        compiler_params=_params())(dq, dk, dv, dgate, proj, gq, gk)


def _prep_b(pb, gq, cos2, sin2):
    def body(x_ref, g_ref, c_ref, s_ref, o_ref):
        lane = _lane_iota((TM, LANES))
        x = x_ref[...]
        xn = (x * _head_rinv(x, lane < HD)) * _g2(g_ref)
        o_ref[...] = (_rope_fwd(xn, c_ref[...], s_ref[...], lane) * QSCALE).astype(BF16)

    blk = pl.BlockSpec((TM, LANES), lambda i, j: (i, j))
    tab = pl.BlockSpec((TM, LANES), lambda i, j: (i, 0))
    return pl.pallas_call(
        body, name="prep_b", grid=(S // TM, 8),
        in_specs=[blk, pl.BlockSpec((1, HD), lambda i, j: (0, 0)), tab, tab], out_specs=blk,
        out_shape=_sds((S, D), BF16), compiler_params=_params())(pb, gq, cos2, sin2)


def _prep_kv(kv, gk, cos2, sin2):
    def body(k_ref, v_ref, g_ref, c_ref, s_ref, ko_ref, vo_ref):
        lane = _lane_iota((TM, LANES))
        x = k_ref[...]
        xn = (x * _head_rinv(x, lane < HD)) * _g2(g_ref)
        ko_ref[...] = _rope_fwd(xn, c_ref[...], s_ref[...], lane).astype(BF16)
        vo_ref[...] = v_ref[...].astype(BF16)

    blk = lambda off: pl.BlockSpec((TM, LANES), lambda i, j: (i, j + off))
    tab = pl.BlockSpec((TM, LANES), lambda i, j: (i, 0))
    return pl.pallas_call(
        body, name="prep_kv", grid=(S // TM, 2),
        in_specs=[blk(0), blk(2), pl.BlockSpec((1, HD), lambda i, j: (0, 0)), tab, tab],
        out_specs=[blk(0), blk(0)], out_shape=[_sds((S, 256), BF16)] * 2,
        compiler_params=_params())(kv, kv, gk, cos2, sin2)


def _swa_mask(n):
    r = lax.broadcasted_iota(jnp.int32, (4 * WIN, 2 * WIN), 0) & (WIN - 1)
    c = lax.broadcasted_iota(jnp.int32, (4 * WIN, 2 * WIN), 1)
    return (c > r) & (c <= r + WIN) & ((c >= WIN) | (n > 0))


def _stack4(ref_or_val, base):
    return jnp.concatenate([ref_or_val[:, base + HD * g: base + HD * (g + 1)] for g in range(4)], axis=0)


def _sink_col(s_ref, first):
    r = lax.broadcasted_iota(jnp.int32, (4 * WIN, 1), 0)
    col = jnp.full((4 * WIN, 1), s_ref[first + 3], F32)
    for g in range(2, -1, -1):
        col = jnp.where(r < WIN * (g + 1), s_ref[first + g], col)
    return col


def _swa_fwd(qb, ksh, vsh, pb, sinks):
    nb = S // WIN

    def body(q_ref, kp_ref, kc_ref, vp_ref, vc_ref, g_ref, s_ref, o_ref, z_ref, lse_ref):
        kp, n = pl.program_id(0), pl.program_id(1)
        valid = _swa_mask(n)
        outs, lses = [], []
        for kh in range(2):
            lo = HD * kh
            kb = jnp.concatenate([kp_ref[:, lo:lo + HD], kc_ref[:, lo:lo + HD]], axis=0)
            vb = jnp.concatenate([vp_ref[:, lo:lo + HD], vc_ref[:, lo:lo + HD]], axis=0)
            qs = _stack4(q_ref, 4 * HD * kh)
            s = jnp.where(valid, _dot_nt(qs, kb), -jnp.inf)
            sink = _sink_col(s_ref, (2 * kp + kh) * 4)
            m = jnp.maximum(jnp.max(s, axis=-1, keepdims=True), sink)
            p = jnp.exp(s - m)
            l = jnp.sum(p, axis=-1, keepdims=True) + jnp.exp(sink - m)
            os_ = _dot_nn(p.astype(BF16), vb) / l
            lse = m + jnp.log(l)
            for g in range(4):
                outs.append(os_[WIN * g:WIN * (g + 1), :])
                lses.append(jnp.broadcast_to(lse[WIN * g:WIN * (g + 1), :], (WIN, HD)))
        o = jnp.concatenate(outs, axis=-1)
        o_ref[...] = o
        g = g_ref[...]
        z_ref[...] = (o * (g * _sigmoid(g))).astype(BF16)
        lse_ref[...] = jnp.concatenate(lses, axis=-1)

    qblk = pl.BlockSpec((WIN, 512), lambda kp, n: (n, kp))
    prev = pl.BlockSpec((WIN, LANES), lambda kp, n: (jnp.maximum(n - 1, 0), kp))
    cur = pl.BlockSpec((WIN, LANES), lambda kp, n: (n, kp))
    return pl.pallas_call(
        body, name="swa_fwd", grid=(2, nb),
        in_specs=[qblk, prev, cur, prev, cur, pl.BlockSpec((WIN, 512), lambda kp, n: (n, 2 + kp)),
                  pl.BlockSpec(memory_space=pltpu.SMEM)],
        out_specs=[qblk, qblk, qblk],
        out_shape=[_sds((S, D), F32), _sds((S, D), BF16), _sds((S, D), F32)],
        compiler_params=_params())(qb, ksh, ksh, vsh, vsh, pb, sinks)


def _swa_bwd(qb, ksh, vsh, dz, o, lse, pb, sinks):
    nb = S // WIN

    def body(q_ref, kp_ref, kc_ref, vp_ref, vc_ref, dz_ref, o_ref, lse_ref, g_ref, s_ref,
             dq_ref, dg_ref, dka_ref, dkb_ref, dva_ref, dvb_ref, dsink_ref):
        kp, n = pl.program_id(0), pl.program_id(1)

        @pl.when((kp == 0) & (n == 0))
        def _():
            dsink_ref[...] = jnp.zeros_like(dsink_ref)

        valid = _swa_mask(n)
        g = g_ref[...]
        sg = _sigmoid(g)
        dzv = dz_ref[...]
        ov = o_ref[...]
        do = dzv * (g * sg)
        dg_ref[...] = (dzv * ov * (sg * (1.0 + g * (1.0 - sg)))).astype(BF16)
        prod = do * ov
        lane1 = _lane_iota((1, LANES))
        dqs, dkas, dkbs, dvas, dvbs = [], [], [], [], []
        dsink = jnp.zeros((1, LANES), F32)
        for kh in range(2):
            lo = HD * kh
            kb = jnp.concatenate([kp_ref[:, lo:lo + HD], kc_ref[:, lo:lo + HD]], axis=0)
            vb = jnp.concatenate([vp_ref[:, lo:lo + HD], vc_ref[:, lo:lo + HD]], axis=0)
            base = 4 * HD * kh
            qs = _stack4(q_ref, base)
            dos = _stack4(do, base).astype(BF16)
            delta = jnp.sum(_stack4(prod, base), axis=-1, keepdims=True)
            lse_s = jnp.concatenate([lse_ref[:, base + HD * gg: base + HD * gg + 1] for gg in range(4)], axis=0)
            s = jnp.where(valid, _dot_nt(qs, kb), -jnp.inf)
            p = jnp.exp(s - lse_s)
            ds = p * (_dot_nt(dos, vb) - delta)
            dsb = ds.astype(BF16)
            dqst = _dot_nn(dsb, kb)
            dkband = _dot_tn(dsb, qs)
            dvband = _dot_tn(p.astype(BF16), dos)
            for gg in range(4):
                dqs.append(dqst[WIN * gg:WIN * (gg + 1), :])
            dkbs.append(dkband[0:WIN, :])
            dkas.append(dkband[WIN:2 * WIN, :])
            dvbs.append(dvband[0:WIN, :])
            dvas.append(dvband[WIN:2 * WIN, :])
            first = (2 * kp + kh) * 4
            ps_delta = jnp.exp(_sink_col(s_ref, first) - lse_s) * delta
            for gg in range(4):
                val = jnp.sum(ps_delta[WIN * gg:WIN * (gg + 1), :], axis=0, keepdims=True)
                dsink = dsink - jnp.where(lane1 == first + gg, val, 0.0)
        dq_ref[...] = jnp.concatenate(dqs, axis=-1)
        dka_ref[...] = jnp.concatenate(dkas, axis=-1)
        dkb_ref[...] = jnp.concatenate(dkbs, axis=-1)
        dva_ref[...] = jnp.concatenate(dvas, axis=-1)
        dvb_ref[...] = jnp.concatenate(dvbs, axis=-1)
        dsink_ref[...] += dsink

    qblk = pl.BlockSpec((WIN, 512), lambda kp, n: (n, kp))
    prev = pl.BlockSpec((WIN, LANES), lambda kp, n: (jnp.maximum(n - 1, 0), kp))
    cur = pl.BlockSpec((WIN, LANES), lambda kp, n: (n, kp))
    return pl.pallas_call(
        body, name="swa_bwd", grid=(2, nb),
        in_specs=[qblk, prev, cur, prev, cur, qblk, qblk, qblk,
                  pl.BlockSpec((WIN, 512), lambda kp, n: (n, 2 + kp)), pl.BlockSpec(memory_space=pltpu.SMEM)],
        out_specs=[qblk, qblk, cur, cur, cur, cur, pl.BlockSpec((1, LANES), lambda kp, n: (0, 0))],
        out_shape=[_sds((S, D), F32), _sds((S, D), BF16)] + [_sds((S, 256), F32)] * 4 + [_sds((1, LANES), F32)],
        compiler_params=_params())(qb, ksh, ksh, vsh, vsh, dz, o, lse, pb, sinks)


def _prep_b_bwd(dq, dgate, pb, gq, cos2, sin2):
    def body(dq_ref, dgt_ref, x_ref, g_ref, c_ref, s_ref, o_ref, dgq_ref):
        i, j = pl.program_id(0), pl.program_id(1)

        @pl.when((i == 0) & (j == 0))
        def _():
            dgq_ref[...] = jnp.zeros_like(dgq_ref)

        @pl.when(j < 8)
        def _():
            lane = _lane_iota((TM, LANES))
            dy = dq_ref[...] * QSCALE
            dn = _rope_bwd(dy, c_ref[...], s_ref[...], lane)
            dx, dg = _head_norm_bwd(dn, x_ref[...], _g2(g_ref), lane < HD)
            o_ref[...] = dx.astype(BF16)
            dgq_ref[...] += dg

        @pl.when(j >= 8)
        def _():
            o_ref[...] = dgt_ref[...]

    tab = pl.BlockSpec((TM, LANES), lambda i, j: (i, 0))
    return pl.pallas_call(
        body, name="prep_b_bwd", grid=(S // TM, 16),
        in_specs=[pl.BlockSpec((TM, LANES), lambda i, j: (i, jnp.minimum(j, 7))),
                  pl.BlockSpec((TM, LANES), lambda i, j: (i, jnp.maximum(j - 8, 0))),
                  pl.BlockSpec((TM, LANES), lambda i, j: (i, jnp.minimum(j, 7))),
                  pl.BlockSpec((1, HD), lambda i, j: (0, 0)), tab, tab],
        out_specs=[pl.BlockSpec((TM, LANES), lambda i, j: (i, j)), pl.BlockSpec((1, LANES), lambda i, j: (0, 0))],
        out_shape=[_sds((S, 2048), BF16), _sds((1, LANES), F32)],
        compiler_params=_params())(dq, dgate, pb, gq, cos2, sin2)


def _prep_kv_bwd(dka, dkb, dva, dvb, kv, gk, cos2, sin2):
    nb = S // WIN

    def body(dka_ref, dkb_ref, dva_ref, dvb_ref, x_ref, g_ref, c_ref, s_ref, o_ref, dgk_ref):
        i, j = pl.program_id(0), pl.program_id(1)

        @pl.when((i == 0) & (j == 0))
        def _():
            dgk_ref[...] = jnp.zeros_like(dgk_ref)

        has_next = i < nb - 1

        @pl.when(j < 2)
        def _():
            lane = _lane_iota((WIN, LANES))
            dy = dka_ref[...] + jnp.where(has_next, dkb_ref[...], 0.0)
            dn = _rope_bwd(dy, c_ref[...], s_ref[...], lane)
            dx, dg = _head_norm_bwd(dn, x_ref[...], _g2(g_ref), lane < HD)
            o_ref[...] = dx.astype(BF16)
            dgk_ref[...] += dg

        @pl.when(j >= 2)
        def _():
            o_ref[...] = (dva_ref[...] + jnp.where(has_next, dvb_ref[...], 0.0)).astype(BF16)

    a_k = pl.BlockSpec((WIN, LANES), lambda i, j: (i, jnp.minimum(j, 1)))
    b_k = pl.BlockSpec((WIN, LANES), lambda i, j: (jnp.minimum(i + 1, nb - 1), jnp.minimum(j, 1)))
    a_v = pl.BlockSpec((WIN, LANES), lambda i, j: (i, jnp.maximum(j - 2, 0)))
    b_v = pl.BlockSpec((WIN, LANES), lambda i, j: (jnp.minimum(i + 1, nb - 1), jnp.maximum(j - 2, 0)))
    tab = pl.BlockSpec((WIN, LANES), lambda i, j: (i, 0))
    return pl.pallas_call(
        body, name="prep_kv_bwd", grid=(nb, 4),
        in_specs=[a_k, b_k, a_v, b_v, a_k, pl.BlockSpec((1, HD), lambda i, j: (0, 0)), tab, tab],
        out_specs=[pl.BlockSpec((WIN, LANES), lambda i, j: (i, j)), pl.BlockSpec((1, LANES), lambda i, j: (0, 0))],
        out_shape=[_sds((S, 512), BF16), _sds((1, LANES), F32)],
        compiler_params=_params())(dka, dkb, dva, dvb, kv, gk, cos2, sin2)


def _loss_dy(y, tgt):
    def body(y_ref, t_ref, dy_ref, l_ref):
        @pl.when(pl.program_id(0) == 0)
        def _():
            l_ref[...] = jnp.zeros_like(l_ref)

        e = y_ref[...] - t_ref[...]
        dy_ref[...] = e * (1.0 / D)
        l_ref[...] += jnp.sum(jnp.sum(e * e, axis=-1, keepdims=True), axis=0, keepdims=True)

    row = pl.BlockSpec((TM, D), lambda i: (i, 0))
    return pl.pallas_call(
        body, name="loss_dy", grid=(S // TM,), in_specs=[row, row],
        out_specs=[row, pl.BlockSpec((1, LANES), lambda i: (0, 0))],
        out_shape=[_sds((S, D), F32), _sds((1, LANES), F32)], compiler_params=_params())(y, tgt)


def _rms2_bwd(du_b, du_kv, h1, g_b, g_kv, dy):
    def body(dub_ref, dukv_ref, x_ref, gb_ref, gkv_ref, dy_ref, dh_ref, dgb_ref, dgkv_ref):
        @pl.when(pl.program_id(0) == 0)
        def _():
            dgb_ref[...] = jnp.zeros_like(dgb_ref)
            dgkv_ref[...] = jnp.zeros_like(dgkv_ref)

        x = x_ref[...]
        dx1, dg1 = _rms_bwd_core(dub_ref[...], x, gb_ref[...])
        dx2, dg2 = _rms_bwd_core(dukv_ref[...], x, gkv_ref[...])
        dh_ref[...] = dy_ref[...] + dx1 + dx2
        dgb_ref[...] += dg1
        dgkv_ref[...] += dg2

    row = pl.BlockSpec((TM, D), lambda i: (i, 0))
    vec = pl.BlockSpec((1, D), lambda i: (0, 0))
    return pl.pallas_call(
        body, name="rms2_bwd", grid=(S // TM,), in_specs=[row, row, row, vec, vec, row],
        out_specs=[row, vec, vec], out_shape=[_sds((S, D), F32), _sds((1, D), F32), _sds((1, D), F32)],
        compiler_params=_params())(du_b, du_kv, h1, g_b, g_kv, dy)


def _rms_bwd(du, x, g, dres):
    def body(du_ref, x_ref, g_ref, dr_ref, dx_ref, dg_ref):
        @pl.when(pl.program_id(0) == 0)
        def _():
            dg_ref[...] = jnp.zeros_like(dg_ref)

        dx, dg = _rms_bwd_core(du_ref[...], x_ref[...], g_ref[...])
        dx_ref[...] = dr_ref[...] + dx
        dg_ref[...] += dg

    row = pl.BlockSpec((TM, D), lambda i: (i, 0))
    vec = pl.BlockSpec((1, D), lambda i: (0, 0))
    return pl.pallas_call(
        body, name="rms_bwd", grid=(S // TM,), in_specs=[row, row, vec, row], out_specs=[row, vec],
        out_shape=[_sds((S, D), F32), _sds((1, D), F32)], compiler_params=_params())(du, x, g, dres)


def _exchange(gather_parts, a2a_parts, name):
    ng, n = len(gather_parts), len(gather_parts) + len(a2a_parts)
    parts = list(gather_parts) + list(a2a_parts)
    out_shape = ([_sds((N_DEV,) + p.shape, p.dtype) for p in gather_parts]
                 + [_sds(p.shape, p.dtype) for p in a2a_parts])

    def body(*refs):
        ins, outs = refs[:n], refs[n:2 * n]
        send_sems, recv_sems, local_sems = refs[2 * n:]
        x, y, c = lax.axis_index("x"), lax.axis_index("y"), lax.axis_index("c")
        me = 4 * x + 2 * y + c

        def src(a, dev):
            return ins[a] if a < ng else ins[a].at[dev]

        local = []
        for a in range(n):
            cp = pltpu.make_async_copy(src(a, me), outs[a].at[me], local_sems.at[a])
            cp.start()
            local.append(cp)
        remote = []
        for k in range(1, N_DEV):
            peer = (x ^ ((k >> 2) & 1), y ^ ((k >> 1) & 1), c ^ (k & 1))
            pidx = me ^ k
            for a in range(n):
                sem = a * (N_DEV - 1) + k - 1
                cp = pltpu.make_async_remote_copy(
                    src_ref=src(a, pidx), dst_ref=outs[a].at[me], send_sem=send_sems.at[sem],
                    recv_sem=recv_sems.at[sem], device_id=peer, device_id_type=pl.DeviceIdType.MESH)
                cp.start()
                arrival = pltpu.make_async_remote_copy(
                    src_ref=src(a, pidx), dst_ref=outs[a].at[pidx], send_sem=send_sems.at[sem],
                    recv_sem=recv_sems.at[sem], device_id=peer, device_id_type=pl.DeviceIdType.MESH)
                remote.append((cp, arrival))
        for cp, arrival in remote:
            arrival.wait_recv()
            cp.wait_send()
        for cp in local:
            cp.wait()

    nsem = n * (N_DEV - 1)
    anyspec = pl.BlockSpec(memory_space=pl.ANY)
    return pl.pallas_call(
        body, name=name, in_specs=[anyspec] * n, out_specs=[anyspec] * n, out_shape=out_shape,
        scratch_shapes=[pltpu.SemaphoreType.DMA((nsem,)), pltpu.SemaphoreType.DMA((nsem,)),
                        pltpu.SemaphoreType.DMA((n,))],
        compiler_params=pltpu.CompilerParams(has_side_effects=True))(*parts)


def _adamw(w, g, m, v):
    m = ADAM_B1 * m + (1.0 - ADAM_B1) * g
    v = ADAM_B2 * v + (1.0 - ADAM_B2) * (g * g)
    m_hat = m / (1.0 - ADAM_B1 ** ADAM_STEP)
    v_hat = v / (1.0 - ADAM_B2 ** ADAM_STEP)
    delta = -ADAM_LR * (m_hat / (jnp.sqrt(v_hat) + ADAM_EPS) + ADAM_WD * w)
    return delta, m, v


def _sum_adamw(recv, w, m, v, name):
    rows, cols = w.shape
    tr = 128

    def body(r_ref, w_ref, m_ref, v_ref, g_ref, d_ref, nm_ref, nv_ref):
        g = r_ref[0].astype(F32)
        for dev in range(1, N_DEV):
            g = g + r_ref[dev].astype(F32)
        g_ref[...] = g
        d_ref[...], nm_ref[...], nv_ref[...] = _adamw(w_ref[...], g, m_ref[...], v_ref[...])

    blk = pl.BlockSpec((tr, cols), lambda i: (i, 0))
    return pl.pallas_call(
        body, name=name, grid=(rows // tr,),
        in_specs=[pl.BlockSpec((N_DEV, tr, cols), lambda i: (0, i, 0)), blk, blk, blk],
        out_specs=[blk] * 4, out_shape=[_sds((rows, cols), F32)] * 4,
        compiler_params=_params())(recv, w, m, v)


SLAB_ROWS = 16
SLOT = {"kv_norm_g": (8, 0, D), "norm_b_g": (9, 0, D), "b_forget": (10, 0, 16), "qnorm_a_g": (10, 128, HD),
        "knorm_a_g": (10, 256, HD), "knorm_b_g": (10, 384, HD), "qnorm_b_g": (10, 512, HD), "sinks": (10, 640, 16)}
SMALL = ["norm_a_g", "b_forget", "qnorm_a_g", "knorm_a_g", "kv_norm_g", "knorm_b_g", "norm_b_g", "qnorm_b_g", "sinks"]


def _pack_small(dg_a, dg_kv, dg_b, db_f, dgq_a, dgk_a, dgk_b, dgq_b, dsinks):
    def fold(ref):
        return ref[:, 0:HD] + ref[:, HD:2 * HD]

    def body(dga_ref, dgkv_ref, dgb_ref, dbf_ref, dgqa_ref, dgka_ref, dgkb_ref, dgqb_ref, dsk_ref, slab_ref):
        slab_ref[...] = jnp.zeros_like(slab_ref)
        for r in range(N_DEV):
            slab_ref[r:r + 1, 0:LANES] = dga_ref[:, LANES * r:LANES * (r + 1)]
        slab_ref[8:9, :] = dgkv_ref[...]
        slab_ref[9:10, :] = dgb_ref[...]
        slab_ref[10:11, 0:LANES] = dbf_ref[...]
        slab_ref[10:11, 128:128 + HD] = fold(dgqa_ref)
        slab_ref[10:11, 256:256 + HD] = fold(dgka_ref)
        slab_ref[10:11, 384:384 + HD] = fold(dgkb_ref)
        slab_ref[10:11, 512:512 + HD] = fold(dgqb_ref)
        slab_ref[10:11, 640:640 + LANES] = dsk_ref[...]

    return pl.pallas_call(body, name="pack_small", out_shape=_sds((SLAB_ROWS, D), F32),
                          compiler_params=_params())(dg_a, dg_kv, dg_b, db_f, dgq_a, dgk_a, dgk_b, dgq_b, dsinks)


def _small_adamw(recv, ws, ms, vs):
    k = len(SMALL)

    def body(*refs):
        r_ref = refs[0]
        w_refs, m_refs, v_refs = refs[1:1 + k], refs[1 + k:1 + 2 * k], refs[1 + 2 * k:1 + 3 * k]
        outs = refs[1 + 3 * k:1 + 7 * k]
        tot = refs[1 + 7 * k]
        g = r_ref[0]
        for dev in range(1, N_DEV):
            g = g + r_ref[dev]
        tot[...] = g
        me = 4 * lax.axis_index("x") + 2 * lax.axis_index("y") + lax.axis_index("c")
        for p, name in enumerate(SMALL):
            if name == "norm_a_g":
                mine = lax.broadcasted_iota(jnp.int32, (N_DEV, LANES), 0) == me
                gp = jnp.sum(jnp.where(mine, tot[0:N_DEV, 0:LANES], 0.0), axis=0, keepdims=True)
            else:
                row, lo, width = SLOT[name]
                gp = tot[row:row + 1, lo:lo + width]
            d, nm, nv = _adamw(w_refs[p][...], gp, m_refs[p][...], v_refs[p][...])
            outs[p][...] = gp
            outs[k + p][...] = d
            outs[2 * k + p][...] = nm
            outs[3 * k + p][...] = nv

    shapes = [_sds(w.shape, F32) for w in ws]
    return pl.pallas_call(body, name="small_adamw", out_shape=shapes * 4,
                          scratch_shapes=[pltpu.VMEM((SLAB_ROWS, D), F32)],
                          compiler_params=_params())(recv, *ws, *ms, *vs)


def _rope_tables(positions):
    inv_freq = jnp.power(jnp.float32(ROPE_THETA), -jnp.arange(0, ROT, 2, dtype=F32) / ROT)
    ang = positions.astype(F32)[:, None] * inv_freq[None, :]
    cos, sin = jnp.cos(ang), jnp.sin(ang)
    c64 = jnp.concatenate([cos, cos, jnp.ones((S, HD - ROT), F32)], axis=-1)
    s64 = jnp.concatenate([-sin, sin, jnp.zeros((S, HD - ROT), F32)], axis=-1)
    return jnp.tile(c64, (1, 2)), jnp.tile(s64, (1, 2))


def _local_step(x, tgt, positions, g_a, wa, b_forget, gq_a, gk_a, w_out_a, g_kv, w_kv, gk_b, g_b, w_in_b, gq_b,
                sinks, w_out_b):
    nq = S // TQ
    cos2, sin2 = _rope_tables(positions)
    b_pad = jnp.pad(b_forget, ((0, 0), (0, LANES - 16)))

    u_a = _rms_fwd(x, g_a, "rms_a_fwd")
    proj = _mm(u_a, wa, "nn", S, 256, D, name="mm_in_a")
    qn, kn, vb = _prep_a(proj, gq_a, gk_a)
    ccol = _fgate_fwd(proj, b_pad)
    crow = ccol[:, :16].T.reshape(16, nq, 1, TQ)
    o_a, z_a, lse_a = _fox_fwd(qn, kn, vb, proj, ccol, crow)
    h1 = _mm(z_a, w_out_a, "nn", 1024, 512, D, add=x, name="mm_out_a")
    u_kv, u_b = _rms2_fwd(h1, g_kv, g_b)
    kv = _mm(u_kv, w_kv, "nn", 1024, 512, D, name="mm_kv")
    pb = _mm(u_b, w_in_b, "nn", 1024, 512, D, name="mm_in_b")
    qb = _prep_b(pb, gq_b, cos2, sin2)
    ksh, vsh = _prep_kv(kv, gk_b, cos2, sin2)
    sinks1 = sinks.reshape(16)
    o_b, z_b, lse_b = _swa_fwd(qb, ksh, vsh, pb, sinks1)
    y = _mm(z_b, w_out_b, "nn", 1024, 512, D, add=h1, name="mm_out_b")
    dy, lsum = _loss_dy(y, tgt)
    dw_out_b = _mm(z_b, dy, "tn", 512, 512, S, name="mm_dw_out_b")
    dz_b = _mm(dy, w_out_b, "nt", 1024, 512, D, name="mm_dz_b")
    dq_b, dgate_b, dka, dkb, dva, dvb, dsinks = _swa_bwd(qb, ksh, vsh, dz_b, o_b, lse_b, pb, sinks1)
    dpb, dgq_b = _prep_b_bwd(dq_b, dgate_b, pb, gq_b, cos2, sin2)
    dkv, dgk_b = _prep_kv_bwd(dka, dkb, dva, dvb, kv, gk_b, cos2, sin2)
    dw_in_b = _mm(u_b, dpb, "tn", 512, 512, S, name="mm_dw_in_b")
    du_b = _mm(dpb, w_in_b, "nt", 1024, 512, 2048, name="mm_du_b")
    dw_kv = _mm(u_kv, dkv, "tn", 512, 512, S, name="mm_dw_kv")
    du_kv = _mm(dkv, w_kv, "nt", 1024, 512, 512, name="mm_du_kv")
    dh1, dg_b, dg_kv = _rms2_bwd(du_b, du_kv, h1, g_b, g_kv, dy)
    dw_out_a = _mm(z_a, dh1, "tn", 512, 512, S, name="mm_dw_out_a")
    dz_a = _mm(dh1, w_out_a, "nt", 1024, 512, D, name="mm_dz_a")
    dq_a, delta_a, do_a, dgate_a, drow = _fox_bwd_dq(qn, kn, vb, dz_a, proj, o_a, lse_a, ccol, crow)
    dk_a, dv_a, dc = _fox_bwd_dkv(qn, kn, vb, do_a, lse_a, delta_a, ccol, crow)
    dproj, dgq_a, dgk_a = _prep_a_bwd(dq_a, dk_a, dv_a, dgate_a, proj, gq_a, gk_a)
    dc_col = jnp.pad(drow.reshape(S, 16, HD)[:, :, 0] + dc.reshape(16, S).T, ((0, 0), (0, LANES - 16)))
    dproj, db_f = _fgate_bwd(dproj, dc_col, proj, b_pad)
    dwa = _mm(u_a, dproj, "tn", 1024, 256, S, name="mm_dw_in_a")
    du_a = _mm(dproj, wa, "nt", 1024, 512, NA // 2, name="mm_du_a")
    dx, dg_a = _rms_bwd(du_a, x, g_a, dh1)
    slab = _pack_small(dg_a, dg_kv, dg_b, db_f, dgq_a, dgk_a, dgk_b, dgq_b, dsinks)
    return lsum, dx, dwa, dw_out_a, dw_kv, dw_in_b, dw_out_b, slab


def kernel(x, positions, norm_a_g, w_in_a, b_forget, qnorm_a_g, knorm_a_g, w_out_a, kv_norm_g, w_kv, knorm_b_g, norm_b_g, w_in_b, qnorm_b_g, sinks, w_out_b, loss_target, m_norm_a_g, m_w_in_a, m_b_forget, m_qnorm_a_g, m_knorm_a_g, m_w_out_a, m_kv_norm_g, m_w_kv, m_knorm_b_g, m_norm_b_g, m_w_in_b, m_qnorm_b_g, m_sinks, m_w_out_b, v_norm_a_g, v_w_in_a, v_b_forget, v_qnorm_a_g, v_knorm_a_g, v_w_out_a, v_kv_norm_g, v_w_kv, v_knorm_b_g, v_norm_b_g, v_w_in_b, v_qnorm_b_g, v_sinks, v_w_out_b):
    shards = [w_in_a[0].astype(BF16), w_out_a[0].astype(BF16), w_kv.astype(BF16), w_in_b[0].astype(BF16),
              w_out_b[0].astype(BF16), norm_a_g]
    wa_g, woa_g, wkv_g, wib_g, wob_g, ga_g = _exchange(shards, [], "gather_weights")
    wa_raw = wa_g.transpose(1, 0, 2).reshape(D, NA_RAW)
    wa = jnp.concatenate([wa_raw[:, :FOFF + 16], jnp.zeros((D, GOFF - FOFF - 16), BF16), wa_raw[:, FOFF + 16:]], axis=1)
    w_out_a_f = woa_g.reshape(D, D)
    w_kv_f = wkv_g.reshape(D, 512)
    w_in_b_f = wib_g.transpose(1, 0, 2).reshape(D, 2048)
    w_out_b_f = wob_g.reshape(D, D)
    g_a = ga_g.reshape(1, D)

    lsum, dx, dwa, dw_out_a, dw_kv, dw_in_b, dw_out_b, slab = _local_step(
        x[0], loss_target[0], positions, g_a, wa, b_forget, qnorm_a_g, knorm_a_g, w_out_a_f,
        kv_norm_g.reshape(1, D), w_kv_f, knorm_b_g.reshape(1, HD), norm_b_g, w_in_b_f, qnorm_b_g, sinks, w_out_b_f)

    loss = lax.psum(lsum[0, 0] * (0.5 / D), ("x", "y", "c"))

    dwa_raw = jnp.concatenate([dwa[:, :FOFF + 16], dwa[:, GOFF:]], axis=1)
    parts = [dwa_raw.reshape(D, N_DEV, NA_RAW // N_DEV).transpose(1, 0, 2).astype(BF16),
             dw_out_a.reshape(N_DEV, D // N_DEV, D).astype(BF16),
             dw_kv.reshape(N_DEV, D // N_DEV, 512).astype(BF16),
             dw_in_b.reshape(D, N_DEV, 2048 // N_DEV).transpose(1, 0, 2).astype(BF16),
             dw_out_b.reshape(N_DEV, D // N_DEV, D).astype(BF16)]
    slab_g, r_wa, r_woa, r_wkv, r_wib, r_wob = _exchange([slab], parts, "exchange_grads")

    big = {}
    for name, recv, w, m, v in (
            ("w_in_a", r_wa, w_in_a[0], m_w_in_a[0], v_w_in_a[0]),
            ("w_out_a", r_woa, w_out_a[0], m_w_out_a[0], v_w_out_a[0]),
            ("w_kv", r_wkv, w_kv, m_w_kv, v_w_kv),
            ("w_in_b", r_wib, w_in_b[0], m_w_in_b[0], v_w_in_b[0]),
            ("w_out_b", r_wob, w_out_b[0], m_w_out_b[0], v_w_out_b[0])):
        big[name] = _sum_adamw(recv, w, m, v, "adamw_" + name)

    r2 = lambda a: a.reshape(1, -1)
    small_w = dict(norm_a_g=norm_a_g, b_forget=b_forget, qnorm_a_g=qnorm_a_g, knorm_a_g=knorm_a_g,
                   kv_norm_g=kv_norm_g, knorm_b_g=knorm_b_g, norm_b_g=norm_b_g, qnorm_b_g=qnorm_b_g, sinks=sinks)
    small_m = dict(norm_a_g=m_norm_a_g, b_forget=m_b_forget, qnorm_a_g=m_qnorm_a_g, knorm_a_g=m_knorm_a_g,
                   kv_norm_g=m_kv_norm_g, knorm_b_g=m_knorm_b_g, norm_b_g=m_norm_b_g, qnorm_b_g=m_qnorm_b_g,
                   sinks=m_sinks)
    small_v = dict(norm_a_g=v_norm_a_g, b_forget=v_b_forget, qnorm_a_g=v_qnorm_a_g, knorm_a_g=v_knorm_a_g,
                   kv_norm_g=v_kv_norm_g, knorm_b_g=v_knorm_b_g, norm_b_g=v_norm_b_g, qnorm_b_g=v_qnorm_b_g,
                   sinks=v_sinks)
    res = _small_adamw(slab_g, [r2(small_w[n]) for n in SMALL], [r2(small_m[n]) for n in SMALL],
                       [r2(small_v[n]) for n in SMALL])
    k = len(SMALL)
    small = {n: [res[q * k + p].reshape(small_w[n].shape) for q in range(4)] for p, n in enumerate(SMALL)}

    order = ["norm_a_g", "w_in_a", "b_forget", "qnorm_a_g", "knorm_a_g", "w_out_a", "kv_norm_g", "w_kv",
             "knorm_b_g", "norm_b_g", "w_in_b", "qnorm_b_g", "sinks", "w_out_b"]
    shapes = dict(w_in_a=w_in_a.shape, w_out_a=w_out_a.shape, w_kv=w_kv.shape, w_in_b=w_in_b.shape,
                  w_out_b=w_out_b.shape)

    def leaf(n, q):
        return big[n][q].reshape(shapes[n]) if n in big else small[n][q]

    outs = [loss, dx[None]]
    for q in range(4):
        outs.extend(leaf(n, q) for n in order)
    return tuple(outs)
```

```python
import jax
import jax.numpy as jnp
from jax import lax
from jax.experimental import pallas as pl
from jax.experimental.pallas import tpu as pltpu

F32, BF16 = jnp.float32, jnp.bfloat16

S = 2048
D = 1024
HD = 64
N_HEADS = 16
N_DEV = 8
NA = 4352
FOFF = 3072
GOFF = 3328
NA_RAW = 4112
EPS = 1e-6
QSCALE = 0.125
ROPE_THETA = 500000.0
ROT = 16
WIN = 128
TQ = 256
TK = 256
KS = TQ // 2
TM = 256
RB = 512
CB = 256
LANES = 128

ADAM_LR, ADAM_B1, ADAM_B2, ADAM_EPS, ADAM_WD, ADAM_STEP = 0.001, 0.9, 0.999, 1e-08, 0.01, 10

VMEM_LIMIT = 48 * 1024 * 1024


def _params():
    return pltpu.CompilerParams(vmem_limit_bytes=VMEM_LIMIT)


def _sds(shape, dtype):
    return jax.ShapeDtypeStruct(shape, dtype)


def _dot_nt(a, b):
    return lax.dot_general(a, b, (((1,), (1,)), ((), ())), preferred_element_type=F32)


def _dot_tn(a, b):
    return lax.dot_general(a, b, (((0,), (0,)), ((), ())), preferred_element_type=F32)


def _dot_nn(a, b):
    return lax.dot_general(a, b, (((1,), (0,)), ((), ())), preferred_element_type=F32)


def _sigmoid(g):
    return 1.0 / (1.0 + jnp.exp(-g))


def _lane_iota(shape):
    return lax.broadcasted_iota(jnp.int32, shape, len(shape) - 1)


def _mm(a, b, mode, tm, tn, tk, out_dtype=F32, add=None, name="mm"):
    if mode == "nn":
        (m, k), n = a.shape, b.shape[1]
        a_spec = pl.BlockSpec((tm, tk), lambda i, j, kk: (i, kk))
        b_spec = pl.BlockSpec((tk, tn), lambda i, j, kk: (kk, j))
        dot = _dot_nn
    elif mode == "nt":
        (m, k), n = a.shape, b.shape[0]
        a_spec = pl.BlockSpec((tm, tk), lambda i, j, kk: (i, kk))
        b_spec = pl.BlockSpec((tn, tk), lambda i, j, kk: (j, kk))
        dot = _dot_nt
    else:
        (k, m), n = a.shape, b.shape[1]
        a_spec = pl.BlockSpec((tk, tm), lambda i, j, kk: (kk, i))
        b_spec = pl.BlockSpec((tk, tn), lambda i, j, kk: (kk, j))
        dot = _dot_tn
    assert m % tm == 0 and n % tn == 0 and k % tk == 0, (m, n, k, tm, tn, tk)
    nk = k // tk
    has_add = add is not None

    def body(*refs):
        if has_add:
            a_ref, b_ref, add_ref, o_ref, acc = refs
        else:
            a_ref, b_ref, o_ref, acc = refs
        p = dot(a_ref[...].astype(BF16), b_ref[...].astype(BF16))

        def finish(total):
            if has_add:
                total = add_ref[...] + total
            o_ref[...] = total.astype(out_dtype)

        if nk == 1:
            finish(p)
        else:
            kk = pl.program_id(2)

            @pl.when(kk == 0)
            def _():
                acc[...] = p

            @pl.when(kk > 0)
            def _():
                acc[...] += p

            @pl.when(kk == nk - 1)
            def _():
                finish(acc[...])

    in_specs = [a_spec, b_spec]
    args = [a, b]
    if has_add:
        in_specs.append(pl.BlockSpec((tm, tn), lambda i, j, kk: (i, j)))
        args.append(add)
    acc_shape = (tm, tn) if nk > 1 else (8, LANES)
    return pl.pallas_call(
        body, name=name, grid=(m // tm, n // tn, nk), in_specs=in_specs,
        out_specs=pl.BlockSpec((tm, tn), lambda i, j, kk: (i, j)),
        out_shape=_sds((m, n), out_dtype), scratch_shapes=[pltpu.VMEM(acc_shape, F32)],
        compiler_params=_params())(*args)


def _rms_rinv(x):
    return lax.rsqrt(jnp.mean(x * x, axis=-1, keepdims=True) + EPS)


def _rms_bwd_core(du, x, g):
    r = _rms_rinv(x)
    dug = du * g
    dx = r * (dug - x * ((r * r) * jnp.mean(dug * x, axis=-1, keepdims=True)))
    dg = jnp.sum(du * (x * r), axis=0, keepdims=True)
    return dx, dg


def _half_sum(v, lo_half):
    s0 = jnp.sum(jnp.where(lo_half, v, 0.0), axis=-1, keepdims=True)
    s1 = jnp.sum(jnp.where(lo_half, 0.0, v), axis=-1, keepdims=True)
    return jnp.where(lo_half, s0, s1)


def _head_rinv(x, lo_half):
    return lax.rsqrt(_half_sum(x * x, lo_half) * (1.0 / HD) + EPS)


def _head_norm_bwd(dn, x, g, lo_half):
    r = _head_rinv(x, lo_half)
    dng = dn * g
    dx = r * (dng - x * ((r * r) * (_half_sum(dng * x, lo_half) * (1.0 / HD))))
    dg = jnp.sum(dn * (x * r), axis=0, keepdims=True)
    return dx, dg


def _rope_swap(x, lane):
    l64 = lane & (HD - 1)
    return jnp.where(l64 < ROT // 2, pltpu.roll(x, LANES - ROT // 2, 1), pltpu.roll(x, ROT // 2, 1))


def _rope_fwd(x, cos, sin, lane):
    return x * cos + _rope_swap(x, lane) * sin


def _rope_bwd(dy, cos, sin, lane):
    return dy * cos + jnp.where((lane & (HD - 1)) < ROT, _rope_swap(dy * sin, lane), 0.0)


def _g2(g_ref):
    g = g_ref[...]
    return jnp.concatenate([g, g], axis=-1)


def _pairs(width):
    return [slice(LANES * c, LANES * (c + 1)) for c in range(width // LANES)]


def _rms_fwd(x, g, name):
    def body(x_ref, g_ref, u_ref):
        xv = x_ref[...]
        u_ref[...] = ((xv * _rms_rinv(xv)) * g_ref[...]).astype(BF16)

    return pl.pallas_call(
        body, name=name, grid=(S // TM,),
        in_specs=[pl.BlockSpec((TM, D), lambda i: (i, 0)), pl.BlockSpec((1, D), lambda i: (0, 0))],
        out_specs=pl.BlockSpec((TM, D), lambda i: (i, 0)), out_shape=_sds((S, D), BF16),
        compiler_params=_params())(x, g)


def _rms2_fwd(h, g1, g2):
    def body(x_ref, g1_ref, g2_ref, u1_ref, u2_ref):
        xv = x_ref[...]
        xn = xv * _rms_rinv(xv)
        u1_ref[...] = (xn * g1_ref[...]).astype(BF16)
        u2_ref[...] = (xn * g2_ref[...]).astype(BF16)

    row = pl.BlockSpec((TM, D), lambda i: (i, 0))
    vec = pl.BlockSpec((1, D), lambda i: (0, 0))
    return pl.pallas_call(
        body, name="rms2_fwd", grid=(S // TM,), in_specs=[row, vec, vec], out_specs=[row, row],
        out_shape=[_sds((S, D), BF16)] * 2, compiler_params=_params())(h, g1, g2)


def _prep_a(proj, gq, gk):
    def body(q_ref, k_ref, v_ref, gq_ref, gk_ref, qo_ref, ko_ref, vo_ref):
        lo_half = _lane_iota((RB, LANES)) < HD
        gq2, gk2 = _g2(gq_ref), _g2(gk_ref)
        for c in _pairs(CB):
            q = q_ref[:, c]
            k = k_ref[:, c]
            qo_ref[:, c] = (((q * _head_rinv(q, lo_half)) * gq2) * QSCALE).astype(BF16)
            ko_ref[:, c] = ((k * _head_rinv(k, lo_half)) * gk2).astype(BF16)
        vo_ref[...] = v_ref[...].astype(BF16)

    nc = D // CB
    blk = lambda off: pl.BlockSpec((RB, CB), lambda i, j: (i, j + off))
    gspec = pl.BlockSpec((1, HD), lambda i, j: (0, 0))
    return pl.pallas_call(
        body, name="prep_a", grid=(S // RB, nc),
        in_specs=[blk(0), blk(nc), blk(2 * nc), gspec, gspec], out_specs=[blk(0)] * 3,
        out_shape=[_sds((S, D), BF16)] * 3, compiler_params=_params())(proj, proj, proj, gq, gk)


def _pick_lane(block, lane, idx):
    return jnp.sum(jnp.where(lane == idx, block, 0.0), axis=-1, keepdims=True)


def _fgate_fwd(proj, b_pad):
    def body(f_ref, b_ref, c_ref, cbc_ref, carry):
        @pl.when(pl.program_id(0) == 0)
        def _():
            carry[...] = jnp.zeros_like(carry)

        z = f_ref[...] + b_ref[...]
        logf = jnp.minimum(z, 0.0) - jnp.log1p(jnp.exp(-jnp.abs(z)))
        r = lax.broadcasted_iota(jnp.int32, (TM, TM), 0)
        c = lax.broadcasted_iota(jnp.int32, (TM, TM), 1)
        tri = (r >= c).astype(F32)
        loc = jnp.dot(tri, logf, precision=lax.Precision.HIGHEST, preferred_element_type=F32) + carry[0:1, :]
        c_ref[...] = loc
        carry[0:1, :] = loc[TM - 1:TM, :]
        lane = _lane_iota((TM, LANES))
        for h in range(N_HEADS):
            cbc_ref[:, LANES * h:LANES * (h + 1)] = jnp.broadcast_to(_pick_lane(loc, lane, h), (TM, LANES))

    return pl.pallas_call(
        body, name="fgate_fwd", grid=(S // TM,),
        in_specs=[pl.BlockSpec((TM, LANES), lambda i: (i, FOFF // LANES)), pl.BlockSpec((1, LANES), lambda i: (0, 0))],
        out_specs=[pl.BlockSpec((TM, LANES), lambda i: (i, 0)), pl.BlockSpec((TM, N_HEADS * LANES), lambda i: (i, 0))],
        out_shape=[_sds((S, LANES), F32), _sds((S, N_HEADS * LANES), F32)],
        scratch_shapes=[pltpu.VMEM((8, LANES), F32)], compiler_params=_params())(proj, b_pad)


def _fgate_bwd(dproj, dc, proj, b_pad):
    nt = S // TM

    def body(dp_ref, dc_ref, f_ref, b_ref, o_ref, db_ref, carry):
        @pl.when(pl.program_id(0) == 0)
        def _():
            carry[...] = jnp.zeros_like(carry)
            db_ref[...] = jnp.zeros_like(db_ref)

        r = lax.broadcasted_iota(jnp.int32, (TM, TM), 0)
        c = lax.broadcasted_iota(jnp.int32, (TM, TM), 1)
        tri = (c >= r).astype(F32)
        dlogf = jnp.dot(tri, dc_ref[...], precision=lax.Precision.HIGHEST, preferred_element_type=F32) + carry[0:1, :]
        carry[0:1, :] = dlogf[0:1, :]
        z = f_ref[...] + b_ref[...]
        df = dlogf * (1.0 / (1.0 + jnp.exp(z)))
        db_ref[...] += jnp.sum(df, axis=0, keepdims=True)
        o_ref[...] = jnp.concatenate([df, jnp.zeros_like(df)], axis=-1).astype(BF16)

    return pl.pallas_call(
        body, name="fgate_bwd", grid=(nt,),
        in_specs=[pl.BlockSpec(memory_space=pl.ANY),
                  pl.BlockSpec((TM, LANES), lambda i: (nt - 1 - i, 0)),
                  pl.BlockSpec((TM, LANES), lambda i: (nt - 1 - i, FOFF // LANES)),
                  pl.BlockSpec((1, LANES), lambda i: (0, 0))],
        out_specs=[pl.BlockSpec((TM, 2 * LANES), lambda i: (nt - 1 - i, FOFF // (2 * LANES))),
                   pl.BlockSpec((1, LANES), lambda i: (0, 0))],
        out_shape=[_sds((S, NA), BF16), _sds((1, LANES), F32)],
        scratch_shapes=[pltpu.VMEM((8, LANES), F32)], input_output_aliases={0: 0},
        compiler_params=_params())(dproj, dc, proj, b_pad)


def _key_le_query(offset, keys=TK):
    r = lax.broadcasted_iota(jnp.int32, (keys, TQ), 0)
    c = lax.broadcasted_iota(jnp.int32, (keys, TQ), 1)
    return (r + offset) <= c


def _widen(tile):
    return jnp.concatenate([tile] * (TQ // LANES), axis=1)


def _fox_fwd(qn, kn, vb, proj, crow, cbc):
    nq, per = S // TQ, TQ // TK

    def body(q_ref, k_ref, v_ref, g_ref, cq_ref, cbc_ref, o_ref, z_ref, lse_ref, st_s, pt_s):
        i = pl.program_id(1)
        qs = [q_ref[:, HD * hh:HD * (hh + 1)] for hh in range(2)]
        cqs = [cq_ref[hh, 0] for hh in range(2)]

        def scores(s, hh):
            off = pl.multiple_of(s * KS, KS)
            kj = k_ref[pl.ds(off, KS), HD * hh:HD * (hh + 1)]
            return (_dot_nt(kj, qs[hh]) + cqs[hh]) - _widen(cbc_ref[pl.ds(off, KS), LANES * hh:LANES * (hh + 1)])

        def values(s, hh, pt):
            off = pl.multiple_of(s * KS, KS)
            return _dot_tn(v_ref[pl.ds(off, KS), HD * hh:HD * (hh + 1)], pt)

        def step(s, slot, carries, mask=None, last=False):
            if not last:
                for hh in range(2):
                    st_s[1 - slot, hh] = scores(s + 1, hh)
            pvs = [values(jnp.maximum(s - 1, 0), hh, pt_s[1 - slot, hh]) for hh in range(2)]
            out = []
            for hh in range(2):
                m, l, acc = carries[hh]
                st = st_s[slot, hh]
                if mask is not None:
                    st = jnp.where(mask, st, -jnp.inf)
                m_new = jnp.maximum(m, jnp.max(st, axis=0, keepdims=True))
                pt = jnp.exp(st - m_new)
                alpha = jnp.exp(m - m_new)
                pt_s[slot, hh] = pt.astype(BF16)
                out.append((m_new, alpha * l + jnp.sum(pt, axis=0, keepdims=True), alpha * (acc + pvs[hh])))
            return tuple(out)

        for hh in range(2):
            st_s[0, hh] = scores(0, hh)
            pt_s[1, hh] = jnp.zeros((KS, TQ), BF16)
        one = (jnp.full((1, TQ), -jnp.inf, F32), jnp.zeros((1, TQ), F32), jnp.zeros((HD, TQ), F32))
        carries = lax.fori_loop(0, i, lambda t, cr: step(2 * t + 1, 1, step(2 * t, 0, cr)), (one, one))
        carries = step(2 * i, 0, carries, mask=_key_le_query(0, KS))
        carries = step(2 * i + 1, 1, carries, mask=_key_le_query(KS, KS), last=True)
        accs = []
        for hh in range(2):
            m, l, acc = carries[hh]
            acc = acc + values(2 * i + 1, hh, pt_s[1, hh])
            accs.append(acc / l)
            lse_ref[hh, 0] = m + jnp.log(l)
        o = jnp.concatenate(accs, axis=0).T
        o_ref[...] = o
        g = g_ref[...]
        z_ref[...] = (o * (g * _sigmoid(g))).astype(BF16)

    qblk = pl.BlockSpec((TQ, LANES), lambda hp, i: (i, hp))
    full = pl.BlockSpec((S, LANES), lambda hp, i: (0, hp))
    rows = pl.BlockSpec((2, 1, 1, TQ), lambda hp, i: (hp, i, 0, 0))
    return pl.pallas_call(
        body, name="fox_fwd", grid=(N_HEADS // 2, nq),
        in_specs=[qblk, full, full,
                  pl.BlockSpec((TQ, LANES), lambda hp, i: (i, GOFF // LANES + hp)),
                  rows, pl.BlockSpec((S, 2 * LANES), lambda hp, i: (0, hp))],
        out_specs=[qblk, qblk, rows],
        out_shape=[_sds((S, D), F32), _sds((S, D), BF16), _sds((N_HEADS, nq, 1, TQ), F32)],
        scratch_shapes=[pltpu.VMEM((2, 2, KS, TQ), F32), pltpu.VMEM((2, 2, KS, TQ), BF16)],
        compiler_params=_params())(qn, kn, vb, proj, crow, cbc)


def _fox_bwd_dq(qn, kn, vb, dz, proj, o, lse, crow, cbc):
    nq, per = S // TQ, TQ // TK

    def body(q_ref, k_ref, v_ref, dz_ref, g_ref, o_ref, lse_ref, cq_ref, cbc_ref,
             dq_ref, delta_ref, do_ref, dg_ref, dr_ref, st_s, dp_s, ds_s):
        i = pl.program_id(1)
        g = g_ref[...]
        sg = _sigmoid(g)
        dzv = dz_ref[...]
        ov = o_ref[...]
        do = dzv * (g * sg)
        dg_ref[...] = (dzv * ov * (sg * (1.0 + g * (1.0 - sg)))).astype(BF16)
        do_ref[...] = do.astype(BF16)
        prod_t = (do * ov).T
        qs = [q_ref[:, HD * hh:HD * (hh + 1)] for hh in range(2)]
        dobs = [do[:, HD * hh:HD * (hh + 1)].astype(BF16) for hh in range(2)]
        lses = [lse_ref[hh, 0] for hh in range(2)]
        cqs = [cq_ref[hh, 0] for hh in range(2)]
        deltas = [jnp.sum(prod_t[HD * hh:HD * (hh + 1), :], axis=0, keepdims=True) for hh in range(2)]
        for hh in range(2):
            delta_ref[hh, 0] = deltas[hh]

        def products(s, hh):
            off = pl.multiple_of(s * KS, KS)
            kj = k_ref[pl.ds(off, KS), HD * hh:HD * (hh + 1)]
            vj = v_ref[pl.ds(off, KS), HD * hh:HD * (hh + 1)]
            st = (_dot_nt(kj, qs[hh]) + cqs[hh]) - _widen(cbc_ref[pl.ds(off, KS), LANES * hh:LANES * (hh + 1)])
            return st, _dot_nt(vj, dobs[hh])

        def dq_of(s, hh, dst):
            off = pl.multiple_of(s * KS, KS)
            return _dot_tn(k_ref[pl.ds(off, KS), HD * hh:HD * (hh + 1)], dst)

        def step(s, slot, carries, mask=None, last=False):
            if not last:
                for hh in range(2):
                    st_s[1 - slot, hh], dp_s[1 - slot, hh] = products(s + 1, hh)
            dqs = [dq_of(jnp.maximum(s - 1, 0), hh, ds_s[1 - slot, hh]) for hh in range(2)]
            out = []
            for hh in range(2):
                dqt, dr = carries[hh]
                st = st_s[slot, hh]
                if mask is not None:
                    st = jnp.where(mask, st, -jnp.inf)
                dst = jnp.exp(st - lses[hh]) * (dp_s[slot, hh] - deltas[hh])
                ds_s[slot, hh] = dst.astype(BF16)
                out.append((dqt + dqs[hh], dr + jnp.sum(dst, axis=0, keepdims=True)))
            return tuple(out)

        for hh in range(2):
            st_s[0, hh], dp_s[0, hh] = products(0, hh)
            ds_s[1, hh] = jnp.zeros((KS, TQ), BF16)
        one = (jnp.zeros((HD, TQ), F32), jnp.zeros((1, TQ), F32))
        carries = lax.fori_loop(0, i, lambda t, cr: step(2 * t + 1, 1, step(2 * t, 0, cr)), (one, one))
        carries = step(2 * i, 0, carries, mask=_key_le_query(0, KS))
        carries = step(2 * i + 1, 1, carries, mask=_key_le_query(KS, KS), last=True)
        for hh in range(2):
            dr_ref[hh, 0] = carries[hh][1]
        dq_ref[...] = jnp.concatenate(
            [carries[hh][0] + dq_of(2 * i + 1, hh, ds_s[1, hh]) for hh in range(2)], axis=0).T

    qblk = pl.BlockSpec((TQ, LANES), lambda hp, i: (i, hp))
    full = pl.BlockSpec((S, LANES), lambda hp, i: (0, hp))
    rows = pl.BlockSpec((2, 1, 1, TQ), lambda hp, i: (hp, i, 0, 0))
    rows_shape = _sds((N_HEADS, nq, 1, TQ), F32)
    return pl.pallas_call(
        body, name="fox_bwd_dq", grid=(N_HEADS // 2, nq),
        in_specs=[qblk, full, full, qblk,
                  pl.BlockSpec((TQ, LANES), lambda hp, i: (i, GOFF // LANES + hp)),
                  qblk, rows, rows, pl.BlockSpec((S, 2 * LANES), lambda hp, i: (0, hp))],
        out_specs=[qblk, rows, qblk, qblk, rows],
        out_shape=[_sds((S, D), F32), rows_shape, _sds((S, D), BF16), _sds((S, D), BF16), rows_shape],
        scratch_shapes=[pltpu.VMEM((2, 2, KS, TQ), F32), pltpu.VMEM((2, 2, KS, TQ), F32),
                        pltpu.VMEM((2, 2, KS, TQ), BF16)],
        compiler_params=_params())(qn, kn, vb, dz, proj, o, lse, crow, cbc)


def _fox_bwd_dkv(qn, kn, vb, dob, lse, delta, crow, cbc):
    nq, per = S // TQ, TQ // TK

    def body(q_ref, k_ref, v_ref, do_ref, lse_ref, del_ref, cq_ref, cbc_ref, dk_ref, dv_ref, dcs_ref,
             st_s, dp_s, pt_s, ds_s):
        j = pl.program_id(1)
        kjs = [k_ref[:, HD * hh:HD * (hh + 1)] for hh in range(2)]
        vjs = [v_ref[:, HD * hh:HD * (hh + 1)] for hh in range(2)]

        def rows_of(ref, u, hh):
            off = pl.multiple_of(u * TQ, TQ)
            return ref[pl.ds(off, TQ), HD * hh:HD * (hh + 1)]

        def products(u, hh):
            st = (_dot_nt(kjs[hh], rows_of(q_ref, u, hh)) + cq_ref[hh, u]) - _widen(
                cbc_ref[:, LANES * hh:LANES * (hh + 1)])
            return st, _dot_nt(vjs[hh], rows_of(do_ref, u, hh))

        def step(u, slot, carries, masked=False):
            nxt = jnp.minimum(u + 1, nq - 1)
            for hh in range(2):
                st_s[1 - slot, hh], dp_s[1 - slot, hh] = products(nxt, hh)
            prev = jnp.maximum(u - 1, 0)
            dvs = [_dot_nn(pt_s[1 - slot, hh], rows_of(do_ref, prev, hh)) for hh in range(2)]
            dks = [_dot_nn(ds_s[1 - slot, hh], rows_of(q_ref, prev, hh)) for hh in range(2)]
            out = []
            for hh in range(2):
                dk, dv, dcs = carries[hh]
                st = st_s[slot, hh]
                if masked:
                    st = jnp.where(_key_le_query((j - u) * TQ), st, -jnp.inf)
                pt = jnp.exp(st - lse_ref[hh, u])
                dst = pt * (dp_s[slot, hh] - del_ref[hh, u])
                pt_s[slot, hh] = pt.astype(BF16)
                ds_s[slot, hh] = dst.astype(BF16)
                out.append((dk + dks[hh], dv + dvs[hh], dcs + (dst[:, :LANES] + dst[:, LANES:])))
            return tuple(out)

        t0 = j // 2
        for hh in range(2):
            st_s[0, hh], dp_s[0, hh] = products(2 * t0, hh)
            pt_s[1, hh] = jnp.zeros((TK, TQ), BF16)
            ds_s[1, hh] = jnp.zeros((TK, TQ), BF16)
        one = (jnp.zeros((TK, HD), F32), jnp.zeros((TK, HD), F32), jnp.zeros((TK, LANES), F32))
        carries = step(2 * t0 + 1, 1, step(2 * t0, 0, (one, one), masked=True), masked=True)
        carries = lax.fori_loop(t0 + 1, nq // 2, lambda t, cr: step(2 * t + 1, 1, step(2 * t, 0, cr)), carries)
        dks, dvs = [], []
        for hh in range(2):
            dk, dv, dcs = carries[hh]
            dks.append(dk + _dot_nn(ds_s[1, hh], rows_of(q_ref, nq - 1, hh)))
            dvs.append(dv + _dot_nn(pt_s[1, hh], rows_of(do_ref, nq - 1, hh)))
            dcs_ref[:, LANES * hh:LANES * (hh + 1)] = jnp.broadcast_to(
                -jnp.sum(dcs, axis=-1, keepdims=True), (TK, LANES))
        dk_ref[...] = jnp.concatenate(dks, axis=-1)
        dv_ref[...] = jnp.concatenate(dvs, axis=-1)

    kblk = pl.BlockSpec((TK, LANES), lambda hp, j: (j, hp))
    full = pl.BlockSpec((S, LANES), lambda hp, j: (0, hp))
    rows = pl.BlockSpec((2, nq, 1, TQ), lambda hp, j: (hp, 0, 0, 0))
    cblk = pl.BlockSpec((TK, 2 * LANES), lambda hp, j: (j, hp))
    return pl.pallas_call(
        body, name="fox_bwd_dkv", grid=(N_HEADS // 2, S // TK),
        in_specs=[full, kblk, kblk, full, rows, rows, rows, cblk],
        out_specs=[kblk, kblk, cblk],
        out_shape=[_sds((S, D), F32), _sds((S, D), F32), _sds((S, N_HEADS * LANES), F32)],
        scratch_shapes=[pltpu.VMEM((2, 2, TK, TQ), F32), pltpu.VMEM((2, 2, TK, TQ), F32),
                        pltpu.VMEM((2, 2, TK, TQ), BF16), pltpu.VMEM((2, 2, TK, TQ), BF16)],
        compiler_params=_params())(qn, kn, vb, dob, lse, delta, crow, cbc)


def _prep_a_bwd(dq, dk, dv, dgate, proj, gq, gk):
    nc = D // CB

    def body(dq_ref, dk_ref, dv_ref, dgt_ref, x_ref, gq_ref, gk_ref, o_ref, dgq_ref, dgk_ref):
        i, j = pl.program_id(0), pl.program_id(1)

        @pl.when((i == 0) & (j == 0))
        def _():
            dgq_ref[...] = jnp.zeros_like(dgq_ref)
            dgk_ref[...] = jnp.zeros_like(dgk_ref)

        @pl.when(j < 2 * nc)
        def _():
            lo_half = _lane_iota((RB, LANES)) < HD
            is_q = j < nc
            g = jnp.where(is_q, _g2(gq_ref), _g2(gk_ref))
            dg_tot = jnp.zeros((1, LANES), F32)
            for c in _pairs(CB):
                dn = jnp.where(is_q, dq_ref[:, c] * QSCALE, dk_ref[:, c])
                dx, dg = _head_norm_bwd(dn, x_ref[:, c], g, lo_half)
                o_ref[:, c] = dx.astype(BF16)
                dg_tot = dg_tot + dg
            dgq_ref[...] += jnp.where(is_q, dg_tot, 0.0)
            dgk_ref[...] += jnp.where(is_q, 0.0, dg_tot)

        @pl.when((j >= 2 * nc) & (j < 3 * nc))
        def _():
            o_ref[...] = dv_ref[...].astype(BF16)

        @pl.when(j >= 3 * nc)
        def _():
            o_ref[...] = dgt_ref[...]

    clamp = lambda off: (lambda i, j: (i, jnp.clip(j - off, 0, nc - 1)))
    gspec = pl.BlockSpec((1, HD), lambda i, j: (0, 0))
    acc = pl.BlockSpec((1, LANES), lambda i, j: (0, 0))
    gate_shift = (GOFF - 3 * D) // CB
    return pl.pallas_call(
        body, name="prep_a_bwd", grid=(S // RB, 4 * nc),
        in_specs=[pl.BlockSpec((RB, CB), clamp(0)), pl.BlockSpec((RB, CB), clamp(nc)),
                  pl.BlockSpec((RB, CB), clamp(2 * nc)), pl.BlockSpec((RB, CB), clamp(3 * nc)),
                  pl.BlockSpec((RB, CB), lambda i, j: (i, jnp.minimum(j, 2 * nc - 1))), gspec, gspec],
        out_specs=[pl.BlockSpec((RB, CB), lambda i, j: (i, jnp.where(j < 3 * nc, j, j + gate_shift))), acc, acc],
        out_shape=[_sds((S, NA), BF16), _sds((1, LANES), F32), _sds((1, LANES), F32)],
        compiler_params=_params())(dq, dk, dv, dgate, proj, gq, gk)


def _prep_b(pb, gq, cos2, sin2):
    def body(x_ref, g_ref, c_ref, s_ref, o_ref):
        lane = _lane_iota((RB, LANES))
        g2, cos, sin = _g2(g_ref), c_ref[...], s_ref[...]
        for c in _pairs(CB):
            x = x_ref[:, c]
            xn = (x * _head_rinv(x, lane < HD)) * g2
            o_ref[:, c] = (_rope_fwd(xn, cos, sin, lane) * QSCALE).astype(BF16)

    blk = pl.BlockSpec((RB, CB), lambda i, j: (i, j))
    tab = pl.BlockSpec((RB, LANES), lambda i, j: (i, 0))
    return pl.pallas_call(
        body, name="prep_b", grid=(S // RB, D // CB),
        in_specs=[blk, pl.BlockSpec((1, HD), lambda i, j: (0, 0)), tab, tab], out_specs=blk,
        out_shape=_sds((S, D), BF16), compiler_params=_params())(pb, gq, cos2, sin2)


def _prep_kv(kv, gk, cos2, sin2):
    def body(k_ref, v_ref, g_ref, c_ref, s_ref, ko_ref, vo_ref):
        lane = _lane_iota((RB, LANES))
        g2, cos, sin = _g2(g_ref), c_ref[...], s_ref[...]
        for c in _pairs(CB):
            x = k_ref[:, c]
            xn = (x * _head_rinv(x, lane < HD)) * g2
            ko_ref[:, c] = _rope_fwd(xn, cos, sin, lane).astype(BF16)
        vo_ref[...] = v_ref[...].astype(BF16)

    blk = lambda off: pl.BlockSpec((RB, CB), lambda i: (i, off))
    tab = pl.BlockSpec((RB, LANES), lambda i: (i, 0))
    return pl.pallas_call(
        body, name="prep_kv", grid=(S // RB,),
        in_specs=[blk(0), blk(1), pl.BlockSpec((1, HD), lambda i: (0, 0)), tab, tab],
        out_specs=[blk(0), blk(0)], out_shape=[_sds((S, CB), BF16)] * 2,
        compiler_params=_params())(kv, kv, gk, cos2, sin2)


def _swa_mask(n):
    r = lax.broadcasted_iota(jnp.int32, (4 * WIN, 2 * WIN), 0) & (WIN - 1)
    c = lax.broadcasted_iota(jnp.int32, (4 * WIN, 2 * WIN), 1)
    return (c > r) & (c <= r + WIN) & ((c >= WIN) | (n > 0))


def _stack4(ref_or_val, base):
    return jnp.concatenate([ref_or_val[:, base + HD * g: base + HD * (g + 1)] for g in range(4)], axis=0)


def _sink_col(s_ref, first):
    r = lax.broadcasted_iota(jnp.int32, (4 * WIN, 1), 0)
    col = jnp.full((4 * WIN, 1), s_ref[first + 3], F32)
    for g in range(2, -1, -1):
        col = jnp.where(r < WIN * (g + 1), s_ref[first + g], col)
    return col


def _swa_fwd(qb, ksh, vsh, pb, sinks):
    nb = S // WIN

    def body(q_ref, kp_ref, kc_ref, vp_ref, vc_ref, g_ref, s_ref, o_ref, z_ref, lse_ref):
        kp, n = pl.program_id(0), pl.program_id(1)
        valid = _swa_mask(n)
        outs, lses = [], []
        for kh in range(2):
            lo = HD * kh
            kb = jnp.concatenate([kp_ref[:, lo:lo + HD], kc_ref[:, lo:lo + HD]], axis=0)
            vb = jnp.concatenate([vp_ref[:, lo:lo + HD], vc_ref[:, lo:lo + HD]], axis=0)
            qs = _stack4(q_ref, 4 * HD * kh)
            s = jnp.where(valid, _dot_nt(qs, kb), -jnp.inf)
            sink = _sink_col(s_ref, (2 * kp + kh) * 4)
            m = jnp.maximum(jnp.max(s, axis=-1, keepdims=True), sink)
            p = jnp.exp(s - m)
            l = jnp.sum(p, axis=-1, keepdims=True) + jnp.exp(sink - m)
            os_ = _dot_nn(p.astype(BF16), vb) / l
            lse = m + jnp.log(l)
            for g in range(4):
                outs.append(os_[WIN * g:WIN * (g + 1), :])
                lses.append(jnp.broadcast_to(lse[WIN * g:WIN * (g + 1), :], (WIN, HD)))
        o = jnp.concatenate(outs, axis=-1)
        o_ref[...] = o
        g = g_ref[...]
        z_ref[...] = (o * (g * _sigmoid(g))).astype(BF16)
        lse_ref[...] = jnp.concatenate(lses, axis=-1)

    qblk = pl.BlockSpec((WIN, 512), lambda kp, n: (n, kp))
    prev = pl.BlockSpec((WIN, LANES), lambda kp, n: (jnp.maximum(n - 1, 0), kp))
    cur = pl.BlockSpec((WIN, LANES), lambda kp, n: (n, kp))
    return pl.pallas_call(
        body, name="swa_fwd", grid=(2, nb),
        in_specs=[qblk, prev, cur, prev, cur, pl.BlockSpec((WIN, 512), lambda kp, n: (n, 2 + kp)),
                  pl.BlockSpec(memory_space=pltpu.SMEM)],
        out_specs=[qblk, qblk, qblk],
        out_shape=[_sds((S, D), F32), _sds((S, D), BF16), _sds((S, D), F32)],
        compiler_params=_params())(qb, ksh, ksh, vsh, vsh, pb, sinks)


def _swa_bwd(qb, ksh, vsh, dz, o, lse, pb, sinks):
    nb = S // WIN

    def body(q_ref, kp_ref, kc_ref, vp_ref, vc_ref, dz_ref, o_ref, lse_ref, g_ref, s_ref,
             dq_ref, dg_ref, dka_ref, dkb_ref, dva_ref, dvb_ref, dsink_ref):
        kp, n = pl.program_id(0), pl.program_id(1)

        @pl.when((kp == 0) & (n == 0))
        def _():
            dsink_ref[...] = jnp.zeros_like(dsink_ref)

        valid = _swa_mask(n)
        g = g_ref[...]
        sg = _sigmoid(g)
        dzv = dz_ref[...]
        ov = o_ref[...]
        do = dzv * (g * sg)
        dg_ref[...] = (dzv * ov * (sg * (1.0 + g * (1.0 - sg)))).astype(BF16)
        prod = do * ov
        lane1 = _lane_iota((1, LANES))
        dqs, dkas, dkbs, dvas, dvbs = [], [], [], [], []
        dsink = jnp.zeros((1, LANES), F32)
        for kh in range(2):
            lo = HD * kh
            kb = jnp.concatenate([kp_ref[:, lo:lo + HD], kc_ref[:, lo:lo + HD]], axis=0)
            vb = jnp.concatenate([vp_ref[:, lo:lo + HD], vc_ref[:, lo:lo + HD]], axis=0)
            base = 4 * HD * kh
            qs = _stack4(q_ref, base)
            dos = _stack4(do, base).astype(BF16)
            delta = jnp.sum(_stack4(prod, base), axis=-1, keepdims=True)
            lse_s = jnp.concatenate([lse_ref[:, base + HD * gg: base + HD * gg + 1] for gg in range(4)], axis=0)
            s = jnp.where(valid, _dot_nt(qs, kb), -jnp.inf)
            p = jnp.exp(s - lse_s)
            ds = p * (_dot_nt(dos, vb) - delta)
            dsb = ds.astype(BF16)
            dqst = _dot_nn(dsb, kb)
            dkband = _dot_tn(dsb, qs)
            dvband = _dot_tn(p.astype(BF16), dos)
            for gg in range(4):
                dqs.append(dqst[WIN * gg:WIN * (gg + 1), :])
            dkbs.append(dkband[0:WIN, :])
            dkas.append(dkband[WIN:2 * WIN, :])
            dvbs.append(dvband[0:WIN, :])
            dvas.append(dvband[WIN:2 * WIN, :])
            first = (2 * kp + kh) * 4
            ps_delta = jnp.exp(_sink_col(s_ref, first) - lse_s) * delta
            for gg in range(4):
                val = jnp.sum(ps_delta[WIN * gg:WIN * (gg + 1), :], axis=0, keepdims=True)
                dsink = dsink - jnp.where(lane1 == first + gg, val, 0.0)
        dq_ref[...] = jnp.concatenate(dqs, axis=-1)
        dka_ref[...] = jnp.concatenate(dkas, axis=-1)
        dkb_ref[...] = jnp.concatenate(dkbs, axis=-1)
        dva_ref[...] = jnp.concatenate(dvas, axis=-1)
        dvb_ref[...] = jnp.concatenate(dvbs, axis=-1)
        dsink_ref[...] += dsink

    qblk = pl.BlockSpec((WIN, 512), lambda kp, n: (n, kp))
    prev = pl.BlockSpec((WIN, LANES), lambda kp, n: (jnp.maximum(n - 1, 0), kp))
    cur = pl.BlockSpec((WIN, LANES), lambda kp, n: (n, kp))
    return pl.pallas_call(
        body, name="swa_bwd", grid=(2, nb),
        in_specs=[qblk, prev, cur, prev, cur, qblk, qblk, qblk,
                  pl.BlockSpec((WIN, 512), lambda kp, n: (n, 2 + kp)), pl.BlockSpec(memory_space=pltpu.SMEM)],
        out_specs=[qblk, qblk, cur, cur, cur, cur, pl.BlockSpec((1, LANES), lambda kp, n: (0, 0))],
        out_shape=[_sds((S, D), F32), _sds((S, D), BF16)] + [_sds((S, 256), F32)] * 4 + [_sds((1, LANES), F32)],
        compiler_params=_params())(qb, ksh, ksh, vsh, vsh, dz, o, lse, pb, sinks)


def _prep_b_bwd(dq, dgate, pb, gq, cos2, sin2):
    nc = D // CB

    def body(dq_ref, dgt_ref, x_ref, g_ref, c_ref, s_ref, o_ref, dgq_ref):
        i, j = pl.program_id(0), pl.program_id(1)

        @pl.when((i == 0) & (j == 0))
        def _():
            dgq_ref[...] = jnp.zeros_like(dgq_ref)

        @pl.when(j < nc)
        def _():
            lane = _lane_iota((RB, LANES))
            g2, cos, sin = _g2(g_ref), c_ref[...], s_ref[...]
            dg_tot = jnp.zeros((1, LANES), F32)
            for c in _pairs(CB):
                dn = _rope_bwd(dq_ref[:, c] * QSCALE, cos, sin, lane)
                dx, dg = _head_norm_bwd(dn, x_ref[:, c], g2, lane < HD)
                o_ref[:, c] = dx.astype(BF16)
                dg_tot = dg_tot + dg
            dgq_ref[...] += dg_tot

        @pl.when(j >= nc)
        def _():
            o_ref[...] = dgt_ref[...]

    tab = pl.BlockSpec((RB, LANES), lambda i, j: (i, 0))
    return pl.pallas_call(
        body, name="prep_b_bwd", grid=(S // RB, 2 * nc),
        in_specs=[pl.BlockSpec((RB, CB), lambda i, j: (i, jnp.minimum(j, nc - 1))),
                  pl.BlockSpec((RB, CB), lambda i, j: (i, jnp.maximum(j - nc, 0))),
                  pl.BlockSpec((RB, CB), lambda i, j: (i, jnp.minimum(j, nc - 1))),
                  pl.BlockSpec((1, HD), lambda i, j: (0, 0)), tab, tab],
        out_specs=[pl.BlockSpec((RB, CB), lambda i, j: (i, j)), pl.BlockSpec((1, LANES), lambda i, j: (0, 0))],
        out_shape=[_sds((S, 2 * D), BF16), _sds((1, LANES), F32)],
        compiler_params=_params())(dq, dgate, pb, gq, cos2, sin2)


def _prep_kv_bwd(dka, dkb, dva, dvb, kv, gk, cos2, sin2):
    nt = S // RB
    per = RB // WIN

    def shifted(cur_ref, nxt_ref, has_next):
        return jnp.concatenate([cur_ref[WIN:RB, :], jnp.where(has_next, nxt_ref[...], 0.0)], axis=0)

    def body(dka_ref, dkb_ref, dkn_ref, dva_ref, dvb_ref, dvn_ref, x_ref, g_ref, c_ref, s_ref, o_ref, dgk_ref):
        i, j = pl.program_id(0), pl.program_id(1)

        @pl.when((i == 0) & (j == 0))
        def _():
            dgk_ref[...] = jnp.zeros_like(dgk_ref)

        has_next = i < nt - 1

        @pl.when(j == 0)
        def _():
            lane = _lane_iota((RB, LANES))
            g2, cos, sin = _g2(g_ref), c_ref[...], s_ref[...]
            dy_all = dka_ref[...] + shifted(dkb_ref, dkn_ref, has_next)
            dg_tot = jnp.zeros((1, LANES), F32)
            for c in _pairs(CB):
                dn = _rope_bwd(dy_all[:, c], cos, sin, lane)
                dx, dg = _head_norm_bwd(dn, x_ref[:, c], g2, lane < HD)
                o_ref[:, c] = dx.astype(BF16)
                dg_tot = dg_tot + dg
            dgk_ref[...] += dg_tot

        @pl.when(j == 1)
        def _():
            o_ref[...] = (dva_ref[...] + shifted(dvb_ref, dvn_ref, has_next)).astype(BF16)

    cur = pl.BlockSpec((RB, CB), lambda i, j: (i, 0))
    nxt = pl.BlockSpec((WIN, CB), lambda i, j: (jnp.minimum(per * (i + 1), S // WIN - 1), 0))
    tab = pl.BlockSpec((RB, LANES), lambda i, j: (i, 0))
    return pl.pallas_call(
        body, name="prep_kv_bwd", grid=(nt, 2),
        in_specs=[cur, cur, nxt, cur, cur, nxt, cur, pl.BlockSpec((1, HD), lambda i, j: (0, 0)), tab, tab],
        out_specs=[pl.BlockSpec((RB, CB), lambda i, j: (i, j)), pl.BlockSpec((1, LANES), lambda i, j: (0, 0))],
        out_shape=[_sds((S, 2 * CB), BF16), _sds((1, LANES), F32)],
        compiler_params=_params())(dka, dkb, dkb, dva, dvb, dvb, kv, gk, cos2, sin2)


def _loss_dy(y, tgt):
    def body(y_ref, t_ref, dy_ref, l_ref):
        @pl.when(pl.program_id(0) == 0)
        def _():
            l_ref[...] = jnp.zeros_like(l_ref)

        e = y_ref[...] - t_ref[...]
        dy_ref[...] = e * (1.0 / D)
        l_ref[...] += jnp.sum(jnp.sum(e * e, axis=-1, keepdims=True), axis=0, keepdims=True)

    row = pl.BlockSpec((TM, D), lambda i: (i, 0))
    return pl.pallas_call(
        body, name="loss_dy", grid=(S // TM,), in_specs=[row, row],
        out_specs=[row, pl.BlockSpec((1, LANES), lambda i: (0, 0))],
        out_shape=[_sds((S, D), F32), _sds((1, LANES), F32)], compiler_params=_params())(y, tgt)


def _rms2_bwd(du_b, du_kv, h1, g_b, g_kv, dy):
    def body(dub_ref, dukv_ref, x_ref, gb_ref, gkv_ref, dy_ref, dh_ref, dgb_ref, dgkv_ref):
        @pl.when(pl.program_id(0) == 0)
        def _():
            dgb_ref[...] = jnp.zeros_like(dgb_ref)
            dgkv_ref[...] = jnp.zeros_like(dgkv_ref)

        x = x_ref[...]
        dx1, dg1 = _rms_bwd_core(dub_ref[...], x, gb_ref[...])
        dx2, dg2 = _rms_bwd_core(dukv_ref[...], x, gkv_ref[...])
        dh_ref[...] = dy_ref[...] + dx1 + dx2
        dgb_ref[...] += dg1
        dgkv_ref[...] += dg2

    row = pl.BlockSpec((TM, D), lambda i: (i, 0))
    vec = pl.BlockSpec((1, D), lambda i: (0, 0))
    return pl.pallas_call(
        body, name="rms2_bwd", grid=(S // TM,), in_specs=[row, row, row, vec, vec, row],
        out_specs=[row, vec, vec], out_shape=[_sds((S, D), F32), _sds((1, D), F32), _sds((1, D), F32)],
        compiler_params=_params())(du_b, du_kv, h1, g_b, g_kv, dy)


def _rms_bwd(du, x, g, dres):
    def body(du_ref, x_ref, g_ref, dr_ref, dx_ref, dg_ref):
        @pl.when(pl.program_id(0) == 0)
        def _():
            dg_ref[...] = jnp.zeros_like(dg_ref)

        dx, dg = _rms_bwd_core(du_ref[...], x_ref[...], g_ref[...])
        dx_ref[...] = dr_ref[...] + dx
        dg_ref[...] += dg

    row = pl.BlockSpec((TM, D), lambda i: (i, 0))
    vec = pl.BlockSpec((1, D), lambda i: (0, 0))
    return pl.pallas_call(
        body, name="rms_bwd", grid=(S // TM,), in_specs=[row, row, vec, row], out_specs=[row, vec],
        out_shape=[_sds((S, D), F32), _sds((1, D), F32)], compiler_params=_params())(du, x, g, dres)


def _exchange(gather_parts, a2a_parts, name):
    ng, n = len(gather_parts), len(gather_parts) + len(a2a_parts)
    parts = list(gather_parts) + list(a2a_parts)
    out_shape = ([_sds((N_DEV,) + p.shape, p.dtype) for p in gather_parts]
                 + [_sds(p.shape, p.dtype) for p in a2a_parts])

    def body(*refs):
        ins, outs = refs[:n], refs[n:2 * n]
        send_sems, recv_sems, local_sems = refs[2 * n:]
        x, y, c = lax.axis_index("x"), lax.axis_index("y"), lax.axis_index("c")
        me = 4 * x + 2 * y + c

        def src(a, dev):
            return ins[a] if a < ng else ins[a].at[dev]

        local = []
        for a in range(n):
            cp = pltpu.make_async_copy(src(a, me), outs[a].at[me], local_sems.at[a])
            cp.start()
            local.append(cp)
        remote = []
        for k in range(1, N_DEV):
            peer = (x ^ ((k >> 2) & 1), y ^ ((k >> 1) & 1), c ^ (k & 1))
            pidx = me ^ k
            for a in range(n):
                sem = a * (N_DEV - 1) + k - 1
                cp = pltpu.make_async_remote_copy(
                    src_ref=src(a, pidx), dst_ref=outs[a].at[me], send_sem=send_sems.at[sem],
                    recv_sem=recv_sems.at[sem], device_id=peer, device_id_type=pl.DeviceIdType.MESH)
                cp.start()
                arrival = pltpu.make_async_remote_copy(
                    src_ref=src(a, pidx), dst_ref=outs[a].at[pidx], send_sem=send_sems.at[sem],
                    recv_sem=recv_sems.at[sem], device_id=peer, device_id_type=pl.DeviceIdType.MESH)
                remote.append((cp, arrival))
        for cp, arrival in remote:
            arrival.wait_recv()
            cp.wait_send()
        for cp in local:
            cp.wait()

    nsem = n * (N_DEV - 1)
    anyspec = pl.BlockSpec(memory_space=pl.ANY)
    return pl.pallas_call(
        body, name=name, in_specs=[anyspec] * n, out_specs=[anyspec] * n, out_shape=out_shape,
        scratch_shapes=[pltpu.SemaphoreType.DMA((nsem,)), pltpu.SemaphoreType.DMA((nsem,)),
                        pltpu.SemaphoreType.DMA((n,))],
        compiler_params=pltpu.CompilerParams(has_side_effects=True))(*parts)


def _adamw(w, g, m, v):
    m = ADAM_B1 * m + (1.0 - ADAM_B1) * g
    v = ADAM_B2 * v + (1.0 - ADAM_B2) * (g * g)
    m_hat = m / (1.0 - ADAM_B1 ** ADAM_STEP)
    v_hat = v / (1.0 - ADAM_B2 ** ADAM_STEP)
    delta = -ADAM_LR * (m_hat / (jnp.sqrt(v_hat) + ADAM_EPS) + ADAM_WD * w)
    return delta, m, v


def _sum_adamw(recv, w, m, v, name):
    rows, cols = w.shape
    tr = 128

    def body(r_ref, w_ref, m_ref, v_ref, g_ref, d_ref, nm_ref, nv_ref):
        g = r_ref[0].astype(F32)
        for dev in range(1, N_DEV):
            g = g + r_ref[dev].astype(F32)
        g_ref[...] = g
        d_ref[...], nm_ref[...], nv_ref[...] = _adamw(w_ref[...], g, m_ref[...], v_ref[...])

    blk = pl.BlockSpec((tr, cols), lambda i: (i, 0))
    return pl.pallas_call(
        body, name=name, grid=(rows // tr,),
        in_specs=[pl.BlockSpec((N_DEV, tr, cols), lambda i: (0, i, 0)), blk, blk, blk],
        out_specs=[blk] * 4, out_shape=[_sds((rows, cols), F32)] * 4,
        compiler_params=_params())(recv, w, m, v)


SLAB_ROWS = 16
SLOT = {"kv_norm_g": (8, 0, D), "norm_b_g": (9, 0, D), "b_forget": (10, 0, 16), "qnorm_a_g": (10, 128, HD),
        "knorm_a_g": (10, 256, HD), "knorm_b_g": (10, 384, HD), "qnorm_b_g": (10, 512, HD), "sinks": (10, 640, 16)}
SMALL = ["norm_a_g", "b_forget", "qnorm_a_g", "knorm_a_g", "kv_norm_g", "knorm_b_g", "norm_b_g", "qnorm_b_g", "sinks"]


def _pack_small(dg_a, dg_kv, dg_b, db_f, dgq_a, dgk_a, dgk_b, dgq_b, dsinks):
    def fold(ref):
        return ref[:, 0:HD] + ref[:, HD:2 * HD]

    def body(dga_ref, dgkv_ref, dgb_ref, dbf_ref, dgqa_ref, dgka_ref, dgkb_ref, dgqb_ref, dsk_ref, slab_ref):
        slab_ref[...] = jnp.zeros_like(slab_ref)
        for r in range(N_DEV):
            slab_ref[r:r + 1, 0:LANES] = dga_ref[:, LANES * r:LANES * (r + 1)]
        slab_ref[8:9, :] = dgkv_ref[...]
        slab_ref[9:10, :] = dgb_ref[...]
        slab_ref[10:11, 0:LANES] = dbf_ref[...]
        slab_ref[10:11, 128:128 + HD] = fold(dgqa_ref)
        slab_ref[10:11, 256:256 + HD] = fold(dgka_ref)
        slab_ref[10:11, 384:384 + HD] = fold(dgkb_ref)
        slab_ref[10:11, 512:512 + HD] = fold(dgqb_ref)
        slab_ref[10:11, 640:640 + LANES] = dsk_ref[...]

    return pl.pallas_call(body, name="pack_small", out_shape=_sds((SLAB_ROWS, D), F32),
                          compiler_params=_params())(dg_a, dg_kv, dg_b, db_f, dgq_a, dgk_a, dgk_b, dgq_b, dsinks)


def _small_adamw(recv, ws, ms, vs):
    k = len(SMALL)

    def body(*refs):
        r_ref = refs[0]
        w_refs, m_refs, v_refs = refs[1:1 + k], refs[1 + k:1 + 2 * k], refs[1 + 2 * k:1 + 3 * k]
        outs = refs[1 + 3 * k:1 + 7 * k]
        tot = refs[1 + 7 * k]
        g = r_ref[0]
        for dev in range(1, N_DEV):
            g = g + r_ref[dev]
        tot[...] = g
        me = 4 * lax.axis_index("x") + 2 * lax.axis_index("y") + lax.axis_index("c")
        for p, name in enumerate(SMALL):
            if name == "norm_a_g":
                mine = lax.broadcasted_iota(jnp.int32, (N_DEV, LANES), 0) == me
                gp = jnp.sum(jnp.where(mine, tot[0:N_DEV, 0:LANES], 0.0), axis=0, keepdims=True)
            else:
                row, lo, width = SLOT[name]
                gp = tot[row:row + 1, lo:lo + width]
            d, nm, nv = _adamw(w_refs[p][...], gp, m_refs[p][...], v_refs[p][...])
            outs[p][...] = gp
            outs[k + p][...] = d
            outs[2 * k + p][...] = nm
            outs[3 * k + p][...] = nv

    shapes = [_sds(w.shape, F32) for w in ws]
    return pl.pallas_call(body, name="small_adamw", out_shape=shapes * 4,
                          scratch_shapes=[pltpu.VMEM((SLAB_ROWS, D), F32)],
                          compiler_params=_params())(recv, *ws, *ms, *vs)


def _rope_tables(positions):
    inv_freq = jnp.power(jnp.float32(ROPE_THETA), -jnp.arange(0, ROT, 2, dtype=F32) / ROT)
    ang = positions.astype(F32)[:, None] * inv_freq[None, :]
    cos, sin = jnp.cos(ang), jnp.sin(ang)
    c64 = jnp.concatenate([cos, cos, jnp.ones((S, HD - ROT), F32)], axis=-1)
    s64 = jnp.concatenate([-sin, sin, jnp.zeros((S, HD - ROT), F32)], axis=-1)
    return jnp.tile(c64, (1, 2)), jnp.tile(s64, (1, 2))


def _local_step(x, tgt, positions, g_a, wa, b_forget, gq_a, gk_a, w_out_a, g_kv, w_kv, gk_b, g_b, w_in_b, gq_b,
                sinks, w_out_b):
    nq = S // TQ
    cos2, sin2 = _rope_tables(positions)
    b_pad = jnp.pad(b_forget, ((0, 0), (0, LANES - N_HEADS)))

    u_a = _rms_fwd(x, g_a, "rms_a_fwd")
    proj = _mm(u_a, wa, "nn", S, 256, D, name="mm_in_a")
    qn, kn, vb = _prep_a(proj, gq_a, gk_a)
    ccol, cbc = _fgate_fwd(proj, b_pad)
    crow = ccol[:, :N_HEADS].T.reshape(N_HEADS, nq, 1, TQ)
    o_a, z_a, lse_a = _fox_fwd(qn, kn, vb, proj, crow, cbc)
    h1 = _mm(z_a, w_out_a, "nn", 1024, 512, D, add=x, name="mm_out_a")
    u_kv, u_b = _rms2_fwd(h1, g_kv, g_b)
    kv = _mm(u_kv, w_kv, "nn", 1024, 512, D, name="mm_kv")
    pb = _mm(u_b, w_in_b, "nn", 1024, 512, D, name="mm_in_b")
    qb = _prep_b(pb, gq_b, cos2, sin2)
    ksh, vsh = _prep_kv(kv, gk_b, cos2, sin2)
    sinks1 = sinks.reshape(N_HEADS)
    o_b, z_b, lse_b = _swa_fwd(qb, ksh, vsh, pb, sinks1)
    y = _mm(z_b, w_out_b, "nn", 1024, 512, D, add=h1, name="mm_out_b")
    dy, lsum = _loss_dy(y, tgt)
    dw_out_b = _mm(z_b, dy, "tn", 512, 512, S, name="mm_dw_out_b")
    dz_b = _mm(dy, w_out_b, "nt", 1024, 512, D, name="mm_dz_b")
    dq_b, dgate_b, dka, dkb, dva, dvb, dsinks = _swa_bwd(qb, ksh, vsh, dz_b, o_b, lse_b, pb, sinks1)
    dpb, dgq_b = _prep_b_bwd(dq_b, dgate_b, pb, gq_b, cos2, sin2)
    dkv, dgk_b = _prep_kv_bwd(dka, dkb, dva, dvb, kv, gk_b, cos2, sin2)
    dw_in_b = _mm(u_b, dpb, "tn", 512, 512, S, name="mm_dw_in_b")
    du_b = _mm(dpb, w_in_b, "nt", 1024, 512, 2048, name="mm_du_b")
    dw_kv = _mm(u_kv, dkv, "tn", 512, 512, S, name="mm_dw_kv")
    du_kv = _mm(dkv, w_kv, "nt", 1024, 512, 512, name="mm_du_kv")
    dh1, dg_b, dg_kv = _rms2_bwd(du_b, du_kv, h1, g_b, g_kv, dy)
    dw_out_a = _mm(z_a, dh1, "tn", 512, 512, S, name="mm_dw_out_a")
    dz_a = _mm(dh1, w_out_a, "nt", 1024, 512, D, name="mm_dz_a")
    dq_a, delta_a, do_a, dgate_a, drow = _fox_bwd_dq(qn, kn, vb, dz_a, proj, o_a, lse_a, crow, cbc)
    dk_a, dv_a, dcs = _fox_bwd_dkv(qn, kn, vb, do_a, lse_a, delta_a, crow, cbc)
    dproj, dgq_a, dgk_a = _prep_a_bwd(dq_a, dk_a, dv_a, dgate_a, proj, gq_a, gk_a)
    dc_col = jnp.pad(drow.reshape(N_HEADS, S).T + dcs[:, ::LANES], ((0, 0), (0, LANES - N_HEADS)))
    dproj, db_f = _fgate_bwd(dproj, dc_col, proj, b_pad)
    dwa = _mm(u_a, dproj, "tn", 1024, 256, S, name="mm_dw_in_a")
    du_a = _mm(dproj, wa, "nt", 1024, 512, NA // 2, name="mm_du_a")
    dx, dg_a = _rms_bwd(du_a, x, g_a, dh1)
    slab = _pack_small(dg_a, dg_kv, dg_b, db_f, dgq_a, dgk_a, dgk_b, dgq_b, dsinks)
    return lsum, dx, dwa, dw_out_a, dw_kv, dw_in_b, dw_out_b, slab


def kernel(x, positions, norm_a_g, w_in_a, b_forget, qnorm_a_g, knorm_a_g, w_out_a, kv_norm_g, w_kv, knorm_b_g, norm_b_g, w_in_b, qnorm_b_g, sinks, w_out_b, loss_target, m_norm_a_g, m_w_in_a, m_b_forget, m_qnorm_a_g, m_knorm_a_g, m_w_out_a, m_kv_norm_g, m_w_kv, m_knorm_b_g, m_norm_b_g, m_w_in_b, m_qnorm_b_g, m_sinks, m_w_out_b, v_norm_a_g, v_w_in_a, v_b_forget, v_qnorm_a_g, v_knorm_a_g, v_w_out_a, v_kv_norm_g, v_w_kv, v_knorm_b_g, v_norm_b_g, v_w_in_b, v_qnorm_b_g, v_sinks, v_w_out_b):
    shards = [w_in_a[0].astype(BF16), w_out_a[0].astype(BF16), w_kv.astype(BF16), w_in_b[0].astype(BF16),
              w_out_b[0].astype(BF16), norm_a_g]
    wa_g, woa_g, wkv_g, wib_g, wob_g, ga_g = _exchange(shards, [], "gather_weights")
    wa_raw = wa_g.transpose(1, 0, 2).reshape(D, NA_RAW)
    wa = jnp.concatenate([wa_raw[:, :FOFF + 16], jnp.zeros((D, GOFF - FOFF - 16), BF16), wa_raw[:, FOFF + 16:]], axis=1)
    w_out_a_f = woa_g.reshape(D, D)
    w_kv_f = wkv_g.reshape(D, 512)
    w_in_b_f = wib_g.transpose(1, 0, 2).reshape(D, 2048)
    w_out_b_f = wob_g.reshape(D, D)
    g_a = ga_g.reshape(1, D)

    lsum, dx, dwa, dw_out_a, dw_kv, dw_in_b, dw_out_b, slab = _local_step(
        x[0], loss_target[0], positions, g_a, wa, b_forget, qnorm_a_g, knorm_a_g, w_out_a_f,
        kv_norm_g.reshape(1, D), w_kv_f, knorm_b_g.reshape(1, HD), norm_b_g, w_in_b_f, qnorm_b_g, sinks, w_out_b_f)

    loss = lax.psum(lsum[0, 0] * (0.5 / D), ("x", "y", "c"))

    dwa_raw = jnp.concatenate([dwa[:, :FOFF + 16], dwa[:, GOFF:]], axis=1)
    parts = [dwa_raw.reshape(D, N_DEV, NA_RAW // N_DEV).transpose(1, 0, 2).astype(BF16),
             dw_out_a.reshape(N_DEV, D // N_DEV, D).astype(BF16),
             dw_kv.reshape(N_DEV, D // N_DEV, 512).astype(BF16),
             dw_in_b.reshape(D, N_DEV, 2048 // N_DEV).transpose(1, 0, 2).astype(BF16),
             dw_out_b.reshape(N_DEV, D // N_DEV, D).astype(BF16)]
    slab_g, r_wa, r_woa, r_wkv, r_wib, r_wob = _exchange([slab], parts, "exchange_grads")

    big = {}
    for name, recv, w, m, v in (
            ("w_in_a", r_wa, w_in_a[0], m_w_in_a[0], v_w_in_a[0]),
            ("w_out_a", r_woa, w_out_a[0], m_w_out_a[0], v_w_out_a[0]),
            ("w_kv", r_wkv, w_kv, m_w_kv, v_w_kv),
            ("w_in_b", r_wib, w_in_b[0], m_w_in_b[0], v_w_in_b[0]),
            ("w_out_b", r_wob, w_out_b[0], m_w_out_b[0], v_w_out_b[0])):
        big[name] = _sum_adamw(recv, w, m, v, "adamw_" + name)

    r2 = lambda a: a.reshape(1, -1)
    small_w = dict(norm_a_g=norm_a_g, b_forget=b_forget, qnorm_a_g=qnorm_a_g, knorm_a_g=knorm_a_g,
                   kv_norm_g=kv_norm_g, knorm_b_g=knorm_b_g, norm_b_g=norm_b_g, qnorm_b_g=qnorm_b_g, sinks=sinks)
    small_m = dict(norm_a_g=m_norm_a_g, b_forget=m_b_forget, qnorm_a_g=m_qnorm_a_g, knorm_a_g=m_knorm_a_g,
                   kv_norm_g=m_kv_norm_g, knorm_b_g=m_knorm_b_g, norm_b_g=m_norm_b_g, qnorm_b_g=m_qnorm_b_g,
                   sinks=m_sinks)
    small_v = dict(norm_a_g=v_norm_a_g, b_forget=v_b_forget, qnorm_a_g=v_qnorm_a_g, knorm_a_g=v_knorm_a_g,
                   kv_norm_g=v_kv_norm_g, knorm_b_g=v_knorm_b_g, norm_b_g=v_norm_b_g, qnorm_b_g=v_qnorm_b_g,
                   sinks=v_sinks)
    res = _small_adamw(slab_g, [r2(small_w[n]) for n in SMALL], [r2(small_m[n]) for n in SMALL],
                       [r2(small_v[n]) for n in SMALL])
    k = len(SMALL)
    small = {n: [res[q * k + p].reshape(small_w[n].shape) for q in range(4)] for p, n in enumerate(SMALL)}

    order = ["norm_a_g", "w_in_a", "b_forget", "qnorm_a_g", "knorm_a_g", "w_out_a", "kv_norm_g", "w_kv",
             "knorm_b_g", "norm_b_g", "w_in_b", "qnorm_b_g", "sinks", "w_out_b"]
    shapes = dict(w_in_a=w_in_a.shape, w_out_a=w_out_a.shape, w_kv=w_kv.shape, w_in_b=w_in_b.shape,
                  w_out_b=w_out_b.shape)

    def leaf(n, q):
        return big[n][q].reshape(shapes[n]) if n in big else small[n][q]

    outs = [loss, dx[None]]
    for q in range(4):
        outs.extend(leaf(n, q) for n in order)
    return tuple(outs)
```

```python
import jax
import jax.numpy as jnp
from jax import lax
from jax.experimental import pallas as pl
from jax.experimental.pallas import tpu as pltpu

F32, BF16 = jnp.float32, jnp.bfloat16

S = 2048
D = 1024
HD = 64
N_HEADS = 16
N_DEV = 8
NA = 4352
FOFF = 3072
GOFF = 3328
NA_RAW = 4112
EPS = 1e-6
QSCALE = 0.125
ROPE_THETA = 500000.0
ROT = 16
WIN = 128
TQ = 256
TK = 256
KS = TQ // 2
TM = 256
RB = 512
CB = 256
LANES = 128

ADAM_LR, ADAM_B1, ADAM_B2, ADAM_EPS, ADAM_WD, ADAM_STEP = 0.001, 0.9, 0.999, 1e-08, 0.01, 10

VMEM_LIMIT = 48 * 1024 * 1024


def _params():
    return pltpu.CompilerParams(vmem_limit_bytes=VMEM_LIMIT)


def _sds(shape, dtype):
    return jax.ShapeDtypeStruct(shape, dtype)


def _dot_nt(a, b):
    return lax.dot_general(a, b, (((1,), (1,)), ((), ())), preferred_element_type=F32)


def _dot_tn(a, b):
    return lax.dot_general(a, b, (((0,), (0,)), ((), ())), preferred_element_type=F32)


def _dot_nn(a, b):
    return lax.dot_general(a, b, (((1,), (0,)), ((), ())), preferred_element_type=F32)


def _sigmoid(g):
    return 1.0 / (1.0 + jnp.exp(-g))


def _lane_iota(shape):
    return lax.broadcasted_iota(jnp.int32, shape, len(shape) - 1)


def _send_view(kind, ref, dev):
    if kind in ("gather_rows", "gather_cols"):
        return ref
    if kind == "a2a_slots":
        return ref.at[dev]
    if kind == "a2a_rows":
        rows = ref.shape[0] // N_DEV
        return ref.at[pl.ds(pl.multiple_of(dev * rows, rows), rows)]
    cols = ref.shape[1] // N_DEV
    return ref.at[:, pl.ds(pl.multiple_of(dev * cols, cols), cols)]


def _land_view(kind, ref, dev):
    if kind == "gather_cols":
        cols = ref.shape[1] // N_DEV
        return ref.at[:, pl.ds(pl.multiple_of(dev * cols, cols), cols)]
    return ref.at[dev]


def _landing_sds(kind, arr):
    if kind == "gather_rows":
        return _sds((N_DEV,) + arr.shape, arr.dtype)
    if kind == "gather_cols":
        return _sds((arr.shape[0], N_DEV * arr.shape[1]), arr.dtype)
    if kind == "a2a_rows":
        return _sds((N_DEV, arr.shape[0] // N_DEV, arr.shape[1]), arr.dtype)
    if kind == "a2a_cols":
        return _sds((N_DEV, arr.shape[0], arr.shape[1] // N_DEV), arr.dtype)
    return _sds(arr.shape, arr.dtype)


def _exchange_sems(n_parts):
    n = n_parts * (N_DEV - 1)
    return [pltpu.SemaphoreType.DMA((n,)), pltpu.SemaphoreType.DMA((n,)), pltpu.SemaphoreType.DMA((n_parts,))]


def _exchange_ops(kinds, srcs, dsts, sems, start, wait):
    send_sems, recv_sems, local_sems = sems
    x, y, c = lax.axis_index("x"), lax.axis_index("y"), lax.axis_index("c")
    me = 4 * x + 2 * y + c

    def local(a):
        return pltpu.make_async_copy(_send_view(kinds[a], srcs[a], me), _land_view(kinds[a], dsts[a], me),
                                     local_sems.at[a])

    def remote(a, k, landing_dev):
        peer = (x ^ ((k >> 2) & 1), y ^ ((k >> 1) & 1), c ^ (k & 1))
        sem = a * (N_DEV - 1) + k - 1
        return pltpu.make_async_remote_copy(
            src_ref=_send_view(kinds[a], srcs[a], me ^ k), dst_ref=_land_view(kinds[a], dsts[a], landing_dev),
            send_sem=send_sems.at[sem], recv_sem=recv_sems.at[sem], device_id=peer,
            device_id_type=pl.DeviceIdType.MESH)

    pairs = [(a, k) for k in range(1, N_DEV) for a in range(len(kinds))]
    if start:
        for a in range(len(kinds)):
            local(a).start()
        for a, k in pairs:
            remote(a, k, me).start()
    if wait:
        for a, k in pairs:
            remote(a, k, me ^ k).wait_recv()
            remote(a, k, me).wait_send()
        for a in range(len(kinds)):
            local(a).wait()


def _call(body, *, name, args, in_specs, out_specs, out_shape, grid=(), scratch_shapes=(), aliases=None, rider=()):
    n_in, n_out, n_scr, n_r = len(in_specs), len(out_specs), len(scratch_shapes), len(rider)
    kinds = [kind for kind, _ in rider]

    def kernel_body(*refs):
        c_in, r_in = refs[:n_in], refs[n_in:n_in + n_r]
        c_out = refs[n_in + n_r:n_in + n_r + n_out]
        r_out = refs[n_in + n_r + n_out:n_in + 2 * n_r + n_out]
        rest = refs[n_in + 2 * n_r + n_out:]
        c_scr, sems = rest[:n_scr], rest[n_scr:]
        if n_r:
            assert grid, "a rider needs a gridded call"
            ids = [pl.program_id(ax) for ax in range(len(grid))]
            first, last = ids[0] == 0, ids[0] == grid[0] - 1
            for pid, size in zip(ids[1:], grid[1:]):
                first = first & (pid == 0)
                last = last & (pid == size - 1)
            pl.when(first)(lambda: _exchange_ops(kinds, r_in, r_out, sems, True, False))
        body(*c_in, *c_out, *c_scr)
        if n_r:
            pl.when(last)(lambda: _exchange_ops(kinds, r_in, r_out, sems, False, True))

    anyspec = pl.BlockSpec(memory_space=pl.ANY)
    params = pltpu.CompilerParams(vmem_limit_bytes=VMEM_LIMIT, has_side_effects=bool(n_r))
    outs = pl.pallas_call(
        kernel_body, name=name, grid=grid, in_specs=list(in_specs) + [anyspec] * n_r,
        out_specs=list(out_specs) + [anyspec] * n_r,
        out_shape=list(out_shape) + [_landing_sds(kind, arr) for kind, arr in rider],
        scratch_shapes=list(scratch_shapes) + (_exchange_sems(n_r) if n_r else []),
        input_output_aliases=aliases or {}, compiler_params=params)(*args, *[arr for _, arr in rider])
    return list(outs)


def _mm(a, b, mode, tm, tn, tk, out_dtype=F32, add=None, name="mm", rider=()):
    if mode == "nn":
        (m, k), n = a.shape, b.shape[1]
        a_spec = pl.BlockSpec((tm, tk), lambda i, j, kk: (i, kk))
        b_spec = pl.BlockSpec((tk, tn), lambda i, j, kk: (kk, j))
        dot = _dot_nn
    elif mode == "nt":
        (m, k), n = a.shape, b.shape[0]
        a_spec = pl.BlockSpec((tm, tk), lambda i, j, kk: (i, kk))
        b_spec = pl.BlockSpec((tn, tk), lambda i, j, kk: (j, kk))
        dot = _dot_nt
    else:
        (k, m), n = a.shape, b.shape[1]
        a_spec = pl.BlockSpec((tk, tm), lambda i, j, kk: (kk, i))
        b_spec = pl.BlockSpec((tk, tn), lambda i, j, kk: (kk, j))
        dot = _dot_tn
    assert m % tm == 0 and n % tn == 0 and k % tk == 0, (m, n, k, tm, tn, tk)
    nk = k // tk
    has_add = add is not None

    def body(*refs):
        if has_add:
            a_ref, b_ref, add_ref, o_ref, acc = refs
        else:
            a_ref, b_ref, o_ref, acc = refs
        p = dot(a_ref[...].astype(BF16), b_ref[...].astype(BF16))

        def finish(total):
            if has_add:
                total = add_ref[...] + total
            o_ref[...] = total.astype(out_dtype)

        if nk == 1:
            finish(p)
        else:
            kk = pl.program_id(2)

            @pl.when(kk == 0)
            def _():
                acc[...] = p

            @pl.when(kk > 0)
            def _():
                acc[...] += p

            @pl.when(kk == nk - 1)
            def _():
                finish(acc[...])

    in_specs = [a_spec, b_spec]
    args = [a, b]
    if has_add:
        in_specs.append(pl.BlockSpec((tm, tn), lambda i, j, kk: (i, j)))
        args.append(add)
    acc_shape = (tm, tn) if nk > 1 else (8, LANES)
    outs = _call(body, name=name, args=args, grid=(m // tm, n // tn, nk), in_specs=in_specs,
                 out_specs=[pl.BlockSpec((tm, tn), lambda i, j, kk: (i, j))], out_shape=[_sds((m, n), out_dtype)],
                 scratch_shapes=[pltpu.VMEM(acc_shape, F32)], rider=rider)
    return outs if rider else outs[0]


def _rms_rinv(x):
    return lax.rsqrt(jnp.mean(x * x, axis=-1, keepdims=True) + EPS)


def _rms_bwd_core(du, x, g):
    r = _rms_rinv(x)
    dug = du * g
    dx = r * (dug - x * ((r * r) * jnp.mean(dug * x, axis=-1, keepdims=True)))
    dg = jnp.sum(du * (x * r), axis=0, keepdims=True)
    return dx, dg


def _half_sum(v, lo_half):
    s0 = jnp.sum(jnp.where(lo_half, v, 0.0), axis=-1, keepdims=True)
    s1 = jnp.sum(jnp.where(lo_half, 0.0, v), axis=-1, keepdims=True)
    return jnp.where(lo_half, s0, s1)


def _head_rinv(x, lo_half):
    return lax.rsqrt(_half_sum(x * x, lo_half) * (1.0 / HD) + EPS)


def _head_norm_bwd(dn, x, g, lo_half):
    r = _head_rinv(x, lo_half)
    dng = dn * g
    dx = r * (dng - x * ((r * r) * (_half_sum(dng * x, lo_half) * (1.0 / HD))))
    dg = jnp.sum(dn * (x * r), axis=0, keepdims=True)
    return dx, dg


def _rope_swap(x, lane):
    l64 = lane & (HD - 1)
    return jnp.where(l64 < ROT // 2, pltpu.roll(x, LANES - ROT // 2, 1), pltpu.roll(x, ROT // 2, 1))


def _rope_fwd(x, cos, sin, lane):
    return x * cos + _rope_swap(x, lane) * sin


def _rope_bwd(dy, cos, sin, lane):
    return dy * cos + jnp.where((lane & (HD - 1)) < ROT, _rope_swap(dy * sin, lane), 0.0)


def _g2(g_ref):
    g = g_ref[...]
    return jnp.concatenate([g, g], axis=-1)


def _pairs(width):
    return [slice(LANES * c, LANES * (c + 1)) for c in range(width // LANES)]


def _rms_fwd(x, g, name):
    def body(x_ref, g_ref, u_ref):
        xv = x_ref[...]
        u_ref[...] = ((xv * _rms_rinv(xv)) * g_ref[...]).astype(BF16)

    return pl.pallas_call(
        body, name=name, grid=(S // TM,),
        in_specs=[pl.BlockSpec((TM, D), lambda i: (i, 0)), pl.BlockSpec((1, D), lambda i: (0, 0))],
        out_specs=pl.BlockSpec((TM, D), lambda i: (i, 0)), out_shape=_sds((S, D), BF16),
        compiler_params=_params())(x, g)


def _rms2_fwd(h, g1, g2):
    def body(x_ref, g1_ref, g2_ref, u1_ref, u2_ref):
        xv = x_ref[...]
        xn = xv * _rms_rinv(xv)
        u1_ref[...] = (xn * g1_ref[...]).astype(BF16)
        u2_ref[...] = (xn * g2_ref[...]).astype(BF16)

    row = pl.BlockSpec((TM, D), lambda i: (i, 0))
    vec = pl.BlockSpec((1, D), lambda i: (0, 0))
    return pl.pallas_call(
        body, name="rms2_fwd", grid=(S // TM,), in_specs=[row, vec, vec], out_specs=[row, row],
        out_shape=[_sds((S, D), BF16)] * 2, compiler_params=_params())(h, g1, g2)


def _prep_a(proj, gq, gk):
    def body(q_ref, k_ref, v_ref, gq_ref, gk_ref, qo_ref, ko_ref, vo_ref):
        lo_half = _lane_iota((RB, LANES)) < HD
        gq2, gk2 = _g2(gq_ref), _g2(gk_ref)
        for c in _pairs(CB):
            q = q_ref[:, c]
            k = k_ref[:, c]
            qo_ref[:, c] = (((q * _head_rinv(q, lo_half)) * gq2) * QSCALE).astype(BF16)
            ko_ref[:, c] = ((k * _head_rinv(k, lo_half)) * gk2).astype(BF16)
        vo_ref[...] = v_ref[...].astype(BF16)

    nc = D // CB
    blk = lambda off: pl.BlockSpec((RB, CB), lambda i, j: (i, j + off))
    gspec = pl.BlockSpec((1, HD), lambda i, j: (0, 0))
    return pl.pallas_call(
        body, name="prep_a", grid=(S // RB, nc),
        in_specs=[blk(0), blk(nc), blk(2 * nc), gspec, gspec], out_specs=[blk(0)] * 3,
        out_shape=[_sds((S, D), BF16)] * 3, compiler_params=_params())(proj, proj, proj, gq, gk)


def _pick_lane(block, lane, idx):
    return jnp.sum(jnp.where(lane == idx, block, 0.0), axis=-1, keepdims=True)


def _fgate_fwd(proj, b_pad):
    def body(f_ref, b_ref, c_ref, cbc_ref, carry):
        @pl.when(pl.program_id(0) == 0)
        def _():
            carry[...] = jnp.zeros_like(carry)

        z = f_ref[...] + b_ref[...]
        logf = jnp.minimum(z, 0.0) - jnp.log1p(jnp.exp(-jnp.abs(z)))
        r = lax.broadcasted_iota(jnp.int32, (TM, TM), 0)
        c = lax.broadcasted_iota(jnp.int32, (TM, TM), 1)
        tri = (r >= c).astype(F32)
        loc = jnp.dot(tri, logf, precision=lax.Precision.HIGHEST, preferred_element_type=F32) + carry[0:1, :]
        c_ref[...] = loc
        carry[0:1, :] = loc[TM - 1:TM, :]
        lane = _lane_iota((TM, LANES))
        for h in range(N_HEADS):
            cbc_ref[:, LANES * h:LANES * (h + 1)] = jnp.broadcast_to(_pick_lane(loc, lane, h), (TM, LANES))

    return pl.pallas_call(
        body, name="fgate_fwd", grid=(S // TM,),
        in_specs=[pl.BlockSpec((TM, LANES), lambda i: (i, FOFF // LANES)), pl.BlockSpec((1, LANES), lambda i: (0, 0))],
        out_specs=[pl.BlockSpec((TM, LANES), lambda i: (i, 0)), pl.BlockSpec((TM, N_HEADS * LANES), lambda i: (i, 0))],
        out_shape=[_sds((S, LANES), F32), _sds((S, N_HEADS * LANES), F32)],
        scratch_shapes=[pltpu.VMEM((8, LANES), F32)], compiler_params=_params())(proj, b_pad)


def _fgate_bwd(dproj, drow, dcs, proj, b_pad):
    nt = S // TM

    def body(dp_ref, dr_ref, dcs_ref, f_ref, b_ref, o_ref, db_ref, carry):
        @pl.when(pl.program_id(0) == 0)
        def _():
            carry[...] = jnp.zeros_like(carry)
            db_ref[...] = jnp.zeros_like(db_ref)

        lane = _lane_iota((TM, LANES))
        dc = dr_ref[...]
        for h in range(N_HEADS):
            dc = dc + jnp.where(lane == h, dcs_ref[:, LANES * h:LANES * h + 1], 0.0)
        r = lax.broadcasted_iota(jnp.int32, (TM, TM), 0)
        c = lax.broadcasted_iota(jnp.int32, (TM, TM), 1)
        tri = (c >= r).astype(F32)
        dlogf = jnp.dot(tri, dc, precision=lax.Precision.HIGHEST, preferred_element_type=F32) + carry[0:1, :]
        carry[0:1, :] = dlogf[0:1, :]
        z = f_ref[...] + b_ref[...]
        df = dlogf * (1.0 / (1.0 + jnp.exp(z)))
        db_ref[...] += jnp.sum(df, axis=0, keepdims=True)
        o_ref[...] = jnp.concatenate([df, jnp.zeros_like(df)], axis=-1).astype(BF16)

    return pl.pallas_call(
        body, name="fgate_bwd", grid=(nt,),
        in_specs=[pl.BlockSpec(memory_space=pl.ANY),
                  pl.BlockSpec((TM, LANES), lambda i: (nt - 1 - i, 0)),
                  pl.BlockSpec((TM, N_HEADS * LANES), lambda i: (nt - 1 - i, 0)),
                  pl.BlockSpec((TM, LANES), lambda i: (nt - 1 - i, FOFF // LANES)),
                  pl.BlockSpec((1, LANES), lambda i: (0, 0))],
        out_specs=[pl.BlockSpec((TM, 2 * LANES), lambda i: (nt - 1 - i, FOFF // (2 * LANES))),
                   pl.BlockSpec((1, LANES), lambda i: (0, 0))],
        out_shape=[_sds((S, NA), BF16), _sds((1, LANES), F32)],
        scratch_shapes=[pltpu.VMEM((8, LANES), F32)], input_output_aliases={0: 0},
        compiler_params=_params())(dproj, drow, dcs, proj, b_pad)


def _key_le_query(offset, keys=TK):
    r = lax.broadcasted_iota(jnp.int32, (keys, TQ), 0)
    c = lax.broadcasted_iota(jnp.int32, (keys, TQ), 1)
    return (r + offset) <= c


def _widen(tile):
    return jnp.concatenate([tile] * (TQ // LANES), axis=1)


def _fox_fwd(qn, kn, vb, proj, crow, cbc, rider=()):
    nq, per = S // TQ, TQ // TK

    def body(q_ref, k_ref, v_ref, g_ref, cq_ref, cbc_ref, o_ref, z_ref, lse_ref, st_s, pt_s):
        i = pl.program_id(1)
        qs = [q_ref[:, HD * hh:HD * (hh + 1)] for hh in range(2)]
        cqs = [cq_ref[hh, 0] for hh in range(2)]

        def scores(s, hh):
            off = pl.multiple_of(s * KS, KS)
            kj = k_ref[pl.ds(off, KS), HD * hh:HD * (hh + 1)]
            return (_dot_nt(kj, qs[hh]) + cqs[hh]) - _widen(cbc_ref[pl.ds(off, KS), LANES * hh:LANES * (hh + 1)])

        def values(s, hh, pt):
            off = pl.multiple_of(s * KS, KS)
            return _dot_tn(v_ref[pl.ds(off, KS), HD * hh:HD * (hh + 1)], pt)

        def step(s, slot, carries, mask=None, last=False):
            if not last:
                for hh in range(2):
                    st_s[1 - slot, hh] = scores(s + 1, hh)
            pvs = [values(jnp.maximum(s - 1, 0), hh, pt_s[1 - slot, hh]) for hh in range(2)]
            out = []
            for hh in range(2):
                m, l, acc = carries[hh]
                st = st_s[slot, hh]
                if mask is not None:
                    st = jnp.where(mask, st, -jnp.inf)
                m_new = jnp.maximum(m, jnp.max(st, axis=0, keepdims=True))
                pt = jnp.exp(st - m_new)
                alpha = jnp.exp(m - m_new)
                pt_s[slot, hh] = pt.astype(BF16)
                out.append((m_new, alpha * l + jnp.sum(pt, axis=0, keepdims=True), alpha * (acc + pvs[hh])))
            return tuple(out)

        for hh in range(2):
            st_s[0, hh] = scores(0, hh)
            pt_s[1, hh] = jnp.zeros((KS, TQ), BF16)
        one = (jnp.full((1, TQ), -jnp.inf, F32), jnp.zeros((1, TQ), F32), jnp.zeros((HD, TQ), F32))
        carries = lax.fori_loop(0, i, lambda t, cr: step(2 * t + 1, 1, step(2 * t, 0, cr)), (one, one))
        carries = step(2 * i, 0, carries, mask=_key_le_query(0, KS))
        carries = step(2 * i + 1, 1, carries, mask=_key_le_query(KS, KS), last=True)
        accs = []
        for hh in range(2):
            m, l, acc = carries[hh]
            acc = acc + values(2 * i + 1, hh, pt_s[1, hh])
            accs.append(acc / l)
            lse_ref[hh, 0] = m + jnp.log(l)
        o = jnp.concatenate(accs, axis=0).T
        o_ref[...] = o
        g = g_ref[...]
        z_ref[...] = (o * (g * _sigmoid(g))).astype(BF16)

    qblk = pl.BlockSpec((TQ, LANES), lambda hp, i: (i, hp))
    full = pl.BlockSpec((S, LANES), lambda hp, i: (0, hp))
    rows = pl.BlockSpec((2, 1, 1, TQ), lambda hp, i: (hp, i, 0, 0))
    return _call(
        body, name="fox_fwd", args=(qn, kn, vb, proj, crow, cbc), grid=(N_HEADS // 2, nq),
        in_specs=[qblk, full, full,
                  pl.BlockSpec((TQ, LANES), lambda hp, i: (i, GOFF // LANES + hp)),
                  rows, pl.BlockSpec((S, 2 * LANES), lambda hp, i: (0, hp))],
        out_specs=[qblk, qblk, rows],
        out_shape=[_sds((S, D), F32), _sds((S, D), BF16), _sds((N_HEADS, nq, 1, TQ), F32)],
        scratch_shapes=[pltpu.VMEM((2, 2, KS, TQ), F32), pltpu.VMEM((2, 2, KS, TQ), BF16)], rider=rider)


def _fox_bwd_dq(qn, kn, vb, dz, proj, o, lse, crow, cbc, rider=()):
    nq, per = S // TQ, TQ // TK

    def body(q_ref, k_ref, v_ref, dz_ref, g_ref, o_ref, lse_ref, cq_ref, cbc_ref,
             dq_ref, delta_ref, do_ref, dg_ref, dr_ref, st_s, dp_s, ds_s):
        i = pl.program_id(1)
        g = g_ref[...]
        sg = _sigmoid(g)
        dzv = dz_ref[...]
        ov = o_ref[...]
        do = dzv * (g * sg)
        dg_ref[...] = (dzv * ov * (sg * (1.0 + g * (1.0 - sg)))).astype(BF16)
        do_ref[...] = do.astype(BF16)
        prod_t = (do * ov).T
        qs = [q_ref[:, HD * hh:HD * (hh + 1)] for hh in range(2)]
        dobs = [do[:, HD * hh:HD * (hh + 1)].astype(BF16) for hh in range(2)]
        lses = [lse_ref[hh, 0] for hh in range(2)]
        cqs = [cq_ref[hh, 0] for hh in range(2)]
        deltas = [jnp.sum(prod_t[HD * hh:HD * (hh + 1), :], axis=0, keepdims=True) for hh in range(2)]
        for hh in range(2):
            delta_ref[hh, 0] = deltas[hh]

        def products(s, hh):
            off = pl.multiple_of(s * KS, KS)
            kj = k_ref[pl.ds(off, KS), HD * hh:HD * (hh + 1)]
            vj = v_ref[pl.ds(off, KS), HD * hh:HD * (hh + 1)]
            st = (_dot_nt(kj, qs[hh]) + cqs[hh]) - _widen(cbc_ref[pl.ds(off, KS), LANES * hh:LANES * (hh + 1)])
            return st, _dot_nt(vj, dobs[hh])

        def dq_of(s, hh, dst):
            off = pl.multiple_of(s * KS, KS)
            return _dot_tn(k_ref[pl.ds(off, KS), HD * hh:HD * (hh + 1)], dst)

        def step(s, slot, carries, mask=None, last=False):
            if not last:
                for hh in range(2):
                    st_s[1 - slot, hh], dp_s[1 - slot, hh] = products(s + 1, hh)
            dqs = [dq_of(jnp.maximum(s - 1, 0), hh, ds_s[1 - slot, hh]) for hh in range(2)]
            out = []
            for hh in range(2):
                dqt, dr = carries[hh]
                st = st_s[slot, hh]
                if mask is not None:
                    st = jnp.where(mask, st, -jnp.inf)
                dst = jnp.exp(st - lses[hh]) * (dp_s[slot, hh] - deltas[hh])
                ds_s[slot, hh] = dst.astype(BF16)
                out.append((dqt + dqs[hh], dr + jnp.sum(dst, axis=0, keepdims=True)))
            return tuple(out)

        for hh in range(2):
            st_s[0, hh], dp_s[0, hh] = products(0, hh)
            ds_s[1, hh] = jnp.zeros((KS, TQ), BF16)
        one = (jnp.zeros((HD, TQ), F32), jnp.zeros((1, TQ), F32))
        carries = lax.fori_loop(0, i, lambda t, cr: step(2 * t + 1, 1, step(2 * t, 0, cr)), (one, one))
        carries = step(2 * i, 0, carries, mask=_key_le_query(0, KS))
        carries = step(2 * i + 1, 1, carries, mask=_key_le_query(KS, KS), last=True)
        for hh in range(2):
            dr_ref[hh, 0] = carries[hh][1]
        dq_ref[...] = jnp.concatenate(
            [carries[hh][0] + dq_of(2 * i + 1, hh, ds_s[1, hh]) for hh in range(2)], axis=0).T

    qblk = pl.BlockSpec((TQ, LANES), lambda hp, i: (i, hp))
    full = pl.BlockSpec((S, LANES), lambda hp, i: (0, hp))
    rows = pl.BlockSpec((2, 1, 1, TQ), lambda hp, i: (hp, i, 0, 0))
    rows_shape = _sds((N_HEADS, nq, 1, TQ), F32)
    return _call(
        body, name="fox_bwd_dq", args=(qn, kn, vb, dz, proj, o, lse, crow, cbc), grid=(N_HEADS // 2, nq),
        in_specs=[qblk, full, full, qblk,
                  pl.BlockSpec((TQ, LANES), lambda hp, i: (i, GOFF // LANES + hp)),
                  qblk, rows, rows, pl.BlockSpec((S, 2 * LANES), lambda hp, i: (0, hp))],
        out_specs=[qblk, rows, qblk, qblk, rows],
        out_shape=[_sds((S, D), F32), rows_shape, _sds((S, D), BF16), _sds((S, D), BF16), rows_shape],
        scratch_shapes=[pltpu.VMEM((2, 2, KS, TQ), F32), pltpu.VMEM((2, 2, KS, TQ), F32),
                        pltpu.VMEM((2, 2, KS, TQ), BF16)], rider=rider)


def _fox_bwd_dkv(qn, kn, vb, dob, lse, delta, crow, cbc):
    nq, per = S // TQ, TQ // TK

    def body(q_ref, k_ref, v_ref, do_ref, lse_ref, del_ref, cq_ref, cbc_ref, dk_ref, dv_ref, dcs_ref,
             st_s, dp_s, pt_s, ds_s):
        j = pl.program_id(1)
        kjs = [k_ref[:, HD * hh:HD * (hh + 1)] for hh in range(2)]
        vjs = [v_ref[:, HD * hh:HD * (hh + 1)] for hh in range(2)]

        def rows_of(ref, u, hh):
            off = pl.multiple_of(u * TQ, TQ)
            return ref[pl.ds(off, TQ), HD * hh:HD * (hh + 1)]

        def products(u, hh):
            st = (_dot_nt(kjs[hh], rows_of(q_ref, u, hh)) + cq_ref[hh, u]) - _widen(
                cbc_ref[:, LANES * hh:LANES * (hh + 1)])
            return st, _dot_nt(vjs[hh], rows_of(do_ref, u, hh))

        def step(u, slot, carries, masked=False):
            nxt = jnp.minimum(u + 1, nq - 1)
            for hh in range(2):
                st_s[1 - slot, hh], dp_s[1 - slot, hh] = products(nxt, hh)
            prev = jnp.maximum(u - 1, 0)
            dvs = [_dot_nn(pt_s[1 - slot, hh], rows_of(do_ref, prev, hh)) for hh in range(2)]
            dks = [_dot_nn(ds_s[1 - slot, hh], rows_of(q_ref, prev, hh)) for hh in range(2)]
            out = []
            for hh in range(2):
                dk, dv, dcs = carries[hh]
                st = st_s[slot, hh]
                if masked:
                    st = jnp.where(_key_le_query((j - u) * TQ), st, -jnp.inf)
                pt = jnp.exp(st - lse_ref[hh, u])
                dst = pt * (dp_s[slot, hh] - del_ref[hh, u])
                pt_s[slot, hh] = pt.astype(BF16)
                ds_s[slot, hh] = dst.astype(BF16)
                out.append((dk + dks[hh], dv + dvs[hh], dcs + (dst[:, :LANES] + dst[:, LANES:])))
            return tuple(out)

        t0 = j // 2
        for hh in range(2):
            st_s[0, hh], dp_s[0, hh] = products(2 * t0, hh)
            pt_s[1, hh] = jnp.zeros((TK, TQ), BF16)
            ds_s[1, hh] = jnp.zeros((TK, TQ), BF16)
        one = (jnp.zeros((TK, HD), F32), jnp.zeros((TK, HD), F32), jnp.zeros((TK, LANES), F32))
        carries = step(2 * t0 + 1, 1, step(2 * t0, 0, (one, one), masked=True), masked=True)
        carries = lax.fori_loop(t0 + 1, nq // 2, lambda t, cr: step(2 * t + 1, 1, step(2 * t, 0, cr)), carries)
        dks, dvs = [], []
        for hh in range(2):
            dk, dv, dcs = carries[hh]
            dks.append(dk + _dot_nn(ds_s[1, hh], rows_of(q_ref, nq - 1, hh)))
            dvs.append(dv + _dot_nn(pt_s[1, hh], rows_of(do_ref, nq - 1, hh)))
            dcs_ref[:, LANES * hh:LANES * (hh + 1)] = jnp.broadcast_to(
                -jnp.sum(dcs, axis=-1, keepdims=True), (TK, LANES))
        dk_ref[...] = jnp.concatenate(dks, axis=-1)
        dv_ref[...] = jnp.concatenate(dvs, axis=-1)

    kblk = pl.BlockSpec((TK, LANES), lambda hp, j: (j, hp))
    full = pl.BlockSpec((S, LANES), lambda hp, j: (0, hp))
    rows = pl.BlockSpec((2, nq, 1, TQ), lambda hp, j: (hp, 0, 0, 0))
    cblk = pl.BlockSpec((TK, 2 * LANES), lambda hp, j: (j, hp))
    return pl.pallas_call(
        body, name="fox_bwd_dkv", grid=(N_HEADS // 2, S // TK),
        in_specs=[full, kblk, kblk, full, rows, rows, rows, cblk],
        out_specs=[kblk, kblk, cblk],
        out_shape=[_sds((S, D), F32), _sds((S, D), F32), _sds((S, N_HEADS * LANES), F32)],
        scratch_shapes=[pltpu.VMEM((2, 2, TK, TQ), F32), pltpu.VMEM((2, 2, TK, TQ), F32),
                        pltpu.VMEM((2, 2, TK, TQ), BF16), pltpu.VMEM((2, 2, TK, TQ), BF16)],
        compiler_params=_params())(qn, kn, vb, dob, lse, delta, crow, cbc)


def _prep_a_bwd(dq, dk, dv, dgate, proj, gq, gk):
    nc = D // CB

    def body(dq_ref, dk_ref, dv_ref, dgt_ref, x_ref, gq_ref, gk_ref, o_ref, dgq_ref, dgk_ref):
        i, j = pl.program_id(0), pl.program_id(1)

        @pl.when((i == 0) & (j == 0))
        def _():
            dgq_ref[...] = jnp.zeros_like(dgq_ref)
            dgk_ref[...] = jnp.zeros_like(dgk_ref)

        @pl.when(j < 2 * nc)
        def _():
            lo_half = _lane_iota((RB, LANES)) < HD
            is_q = j < nc
            g = jnp.where(is_q, _g2(gq_ref), _g2(gk_ref))
            dg_tot = jnp.zeros((1, LANES), F32)
            for c in _pairs(CB):
                dn = jnp.where(is_q, dq_ref[:, c] * QSCALE, dk_ref[:, c])
                dx, dg = _head_norm_bwd(dn, x_ref[:, c], g, lo_half)
                o_ref[:, c] = dx.astype(BF16)
                dg_tot = dg_tot + dg
            dgq_ref[...] += jnp.where(is_q, dg_tot, 0.0)
            dgk_ref[...] += jnp.where(is_q, 0.0, dg_tot)

        @pl.when((j >= 2 * nc) & (j < 3 * nc))
        def _():
            o_ref[...] = dv_ref[...].astype(BF16)

        @pl.when(j >= 3 * nc)
        def _():
            o_ref[...] = dgt_ref[...]

    clamp = lambda off: (lambda i, j: (i, jnp.clip(j - off, 0, nc - 1)))
    gspec = pl.BlockSpec((1, HD), lambda i, j: (0, 0))
    acc = pl.BlockSpec((1, LANES), lambda i, j: (0, 0))
    gate_shift = (GOFF - 3 * D) // CB
    return pl.pallas_call(
        body, name="prep_a_bwd", grid=(S // RB, 4 * nc),
        in_specs=[pl.BlockSpec((RB, CB), clamp(0)), pl.BlockSpec((RB, CB), clamp(nc)),
                  pl.BlockSpec((RB, CB), clamp(2 * nc)), pl.BlockSpec((RB, CB), clamp(3 * nc)),
                  pl.BlockSpec((RB, CB), lambda i, j: (i, jnp.minimum(j, 2 * nc - 1))), gspec, gspec],
        out_specs=[pl.BlockSpec((RB, CB), lambda i, j: (i, jnp.where(j < 3 * nc, j, j + gate_shift))), acc, acc],
        out_shape=[_sds((S, NA), BF16), _sds((1, LANES), F32), _sds((1, LANES), F32)],
        compiler_params=_params())(dq, dk, dv, dgate, proj, gq, gk)


def _prep_b(pb, gq, cos2, sin2):
    def body(x_ref, g_ref, c_ref, s_ref, o_ref):
        lane = _lane_iota((RB, LANES))
        g2, cos, sin = _g2(g_ref), c_ref[...], s_ref[...]
        for c in _pairs(CB):
            x = x_ref[:, c]
            xn = (x * _head_rinv(x, lane < HD)) * g2
            o_ref[:, c] = (_rope_fwd(xn, cos, sin, lane) * QSCALE).astype(BF16)

    blk = pl.BlockSpec((RB, CB), lambda i, j: (i, j))
    tab = pl.BlockSpec((RB, LANES), lambda i, j: (i, 0))
    return pl.pallas_call(
        body, name="prep_b", grid=(S // RB, D // CB),
        in_specs=[blk, pl.BlockSpec((1, HD), lambda i, j: (0, 0)), tab, tab], out_specs=blk,
        out_shape=_sds((S, D), BF16), compiler_params=_params())(pb, gq, cos2, sin2)


def _prep_kv(kv, gk, cos2, sin2):
    def body(k_ref, v_ref, g_ref, c_ref, s_ref, ko_ref, vo_ref):
        lane = _lane_iota((RB, LANES))
        g2, cos, sin = _g2(g_ref), c_ref[...], s_ref[...]
        for c in _pairs(CB):
            x = k_ref[:, c]
            xn = (x * _head_rinv(x, lane < HD)) * g2
            ko_ref[:, c] = _rope_fwd(xn, cos, sin, lane).astype(BF16)
        vo_ref[...] = v_ref[...].astype(BF16)

    blk = lambda off: pl.BlockSpec((RB, CB), lambda i: (i, off))
    tab = pl.BlockSpec((RB, LANES), lambda i: (i, 0))
    return pl.pallas_call(
        body, name="prep_kv", grid=(S // RB,),
        in_specs=[blk(0), blk(1), pl.BlockSpec((1, HD), lambda i: (0, 0)), tab, tab],
        out_specs=[blk(0), blk(0)], out_shape=[_sds((S, CB), BF16)] * 2,
        compiler_params=_params())(kv, kv, gk, cos2, sin2)


def _swa_mask(n):
    r = lax.broadcasted_iota(jnp.int32, (4 * WIN, 2 * WIN), 0) & (WIN - 1)
    c = lax.broadcasted_iota(jnp.int32, (4 * WIN, 2 * WIN), 1)
    return (c > r) & (c <= r + WIN) & ((c >= WIN) | (n > 0))


def _stack4(ref_or_val, base):
    return jnp.concatenate([ref_or_val[:, base + HD * g: base + HD * (g + 1)] for g in range(4)], axis=0)


def _sink_col(s_ref, first):
    r = lax.broadcasted_iota(jnp.int32, (4 * WIN, 1), 0)
    col = jnp.full((4 * WIN, 1), s_ref[first + 3], F32)
    for g in range(2, -1, -1):
        col = jnp.where(r < WIN * (g + 1), s_ref[first + g], col)
    return col


def _swa_fwd(qb, ksh, vsh, pb, sinks):
    nb = S // WIN

    def body(q_ref, kp_ref, kc_ref, vp_ref, vc_ref, g_ref, s_ref, o_ref, z_ref, lse_ref):
        kp, n = pl.program_id(0), pl.program_id(1)
        valid = _swa_mask(n)
        outs, lses = [], []
        for kh in range(2):
            lo = HD * kh
            kb = jnp.concatenate([kp_ref[:, lo:lo + HD], kc_ref[:, lo:lo + HD]], axis=0)
            vb = jnp.concatenate([vp_ref[:, lo:lo + HD], vc_ref[:, lo:lo + HD]], axis=0)
            qs = _stack4(q_ref, 4 * HD * kh)
            s = jnp.where(valid, _dot_nt(qs, kb), -jnp.inf)
            sink = _sink_col(s_ref, (2 * kp + kh) * 4)
            m = jnp.maximum(jnp.max(s, axis=-1, keepdims=True), sink)
            p = jnp.exp(s - m)
            l = jnp.sum(p, axis=-1, keepdims=True) + jnp.exp(sink - m)
            os_ = _dot_nn(p.astype(BF16), vb) / l
            lse = m + jnp.log(l)
            for g in range(4):
                outs.append(os_[WIN * g:WIN * (g + 1), :])
                lses.append(jnp.broadcast_to(lse[WIN * g:WIN * (g + 1), :], (WIN, HD)))
        o = jnp.concatenate(outs, axis=-1)
        o_ref[...] = o
        g = g_ref[...]
        z_ref[...] = (o * (g * _sigmoid(g))).astype(BF16)
        lse_ref[...] = jnp.concatenate(lses, axis=-1)

    qblk = pl.BlockSpec((WIN, 512), lambda kp, n: (n, kp))
    prev = pl.BlockSpec((WIN, LANES), lambda kp, n: (jnp.maximum(n - 1, 0), kp))
    cur = pl.BlockSpec((WIN, LANES), lambda kp, n: (n, kp))
    return pl.pallas_call(
        body, name="swa_fwd", grid=(2, nb),
        in_specs=[qblk, prev, cur, prev, cur, pl.BlockSpec((WIN, 512), lambda kp, n: (n, 2 + kp)),
                  pl.BlockSpec(memory_space=pltpu.SMEM)],
        out_specs=[qblk, qblk, qblk],
        out_shape=[_sds((S, D), F32), _sds((S, D), BF16), _sds((S, D), F32)],
        compiler_params=_params())(qb, ksh, ksh, vsh, vsh, pb, sinks)


def _swa_bwd(qb, ksh, vsh, dz, o, lse, pb, sinks):
    nb = S // WIN

    def body(q_ref, kp_ref, kc_ref, vp_ref, vc_ref, dz_ref, o_ref, lse_ref, g_ref, s_ref,
             dq_ref, dg_ref, dka_ref, dkb_ref, dva_ref, dvb_ref, dsink_ref):
        kp, n = pl.program_id(0), pl.program_id(1)

        @pl.when((kp == 0) & (n == 0))
        def _():
            dsink_ref[...] = jnp.zeros_like(dsink_ref)

        valid = _swa_mask(n)
        g = g_ref[...]
        sg = _sigmoid(g)
        dzv = dz_ref[...]
        ov = o_ref[...]
        do = dzv * (g * sg)
        dg_ref[...] = (dzv * ov * (sg * (1.0 + g * (1.0 - sg)))).astype(BF16)
        prod = do * ov
        lane1 = _lane_iota((1, LANES))
        dqs, dkas, dkbs, dvas, dvbs = [], [], [], [], []
        dsink = jnp.zeros((1, LANES), F32)
        for kh in range(2):
            lo = HD * kh
            kb = jnp.concatenate([kp_ref[:, lo:lo + HD], kc_ref[:, lo:lo + HD]], axis=0)
            vb = jnp.concatenate([vp_ref[:, lo:lo + HD], vc_ref[:, lo:lo + HD]], axis=0)
            base = 4 * HD * kh
            qs = _stack4(q_ref, base)
            dos = _stack4(do, base).astype(BF16)
            delta = jnp.sum(_stack4(prod, base), axis=-1, keepdims=True)
            lse_s = jnp.concatenate([lse_ref[:, base + HD * gg: base + HD * gg + 1] for gg in range(4)], axis=0)
            s = jnp.where(valid, _dot_nt(qs, kb), -jnp.inf)
            p = jnp.exp(s - lse_s)
            ds = p * (_dot_nt(dos, vb) - delta)
            dsb = ds.astype(BF16)
            dqst = _dot_nn(dsb, kb)
            dkband = _dot_tn(dsb, qs)
            dvband = _dot_tn(p.astype(BF16), dos)
            for gg in range(4):
                dqs.append(dqst[WIN * gg:WIN * (gg + 1), :])
            dkbs.append(dkband[0:WIN, :])
            dkas.append(dkband[WIN:2 * WIN, :])
            dvbs.append(dvband[0:WIN, :])
            dvas.append(dvband[WIN:2 * WIN, :])
            first = (2 * kp + kh) * 4
            ps_delta = jnp.exp(_sink_col(s_ref, first) - lse_s) * delta
            for gg in range(4):
                val = jnp.sum(ps_delta[WIN * gg:WIN * (gg + 1), :], axis=0, keepdims=True)
                dsink = dsink - jnp.where(lane1 == first + gg, val, 0.0)
        dq_ref[...] = jnp.concatenate(dqs, axis=-1)
        dka_ref[...] = jnp.concatenate(dkas, axis=-1)
        dkb_ref[...] = jnp.concatenate(dkbs, axis=-1)
        dva_ref[...] = jnp.concatenate(dvas, axis=-1)
        dvb_ref[...] = jnp.concatenate(dvbs, axis=-1)
        dsink_ref[...] += dsink

    qblk = pl.BlockSpec((WIN, 512), lambda kp, n: (n, kp))
    prev = pl.BlockSpec((WIN, LANES), lambda kp, n: (jnp.maximum(n - 1, 0), kp))
    cur = pl.BlockSpec((WIN, LANES), lambda kp, n: (n, kp))
    return pl.pallas_call(
        body, name="swa_bwd", grid=(2, nb),
        in_specs=[qblk, prev, cur, prev, cur, qblk, qblk, qblk,
                  pl.BlockSpec((WIN, 512), lambda kp, n: (n, 2 + kp)), pl.BlockSpec(memory_space=pltpu.SMEM)],
        out_specs=[qblk, qblk, cur, cur, cur, cur, pl.BlockSpec((1, LANES), lambda kp, n: (0, 0))],
        out_shape=[_sds((S, D), F32), _sds((S, D), BF16)] + [_sds((S, 256), F32)] * 4 + [_sds((1, LANES), F32)],
        compiler_params=_params())(qb, ksh, ksh, vsh, vsh, dz, o, lse, pb, sinks)


def _prep_b_bwd(dq, dgate, pb, gq, cos2, sin2):
    nc = D // CB

    def body(dq_ref, dgt_ref, x_ref, g_ref, c_ref, s_ref, o_ref, dgq_ref):
        i, j = pl.program_id(0), pl.program_id(1)

        @pl.when((i == 0) & (j == 0))
        def _():
            dgq_ref[...] = jnp.zeros_like(dgq_ref)

        @pl.when(j < nc)
        def _():
            lane = _lane_iota((RB, LANES))
            g2, cos, sin = _g2(g_ref), c_ref[...], s_ref[...]
            dg_tot = jnp.zeros((1, LANES), F32)
            for c in _pairs(CB):
                dn = _rope_bwd(dq_ref[:, c] * QSCALE, cos, sin, lane)
                dx, dg = _head_norm_bwd(dn, x_ref[:, c], g2, lane < HD)
                o_ref[:, c] = dx.astype(BF16)
                dg_tot = dg_tot + dg
            dgq_ref[...] += dg_tot

        @pl.when(j >= nc)
        def _():
            o_ref[...] = dgt_ref[...]

    tab = pl.BlockSpec((RB, LANES), lambda i, j: (i, 0))
    return pl.pallas_call(
        body, name="prep_b_bwd", grid=(S // RB, 2 * nc),
        in_specs=[pl.BlockSpec((RB, CB), lambda i, j: (i, jnp.minimum(j, nc - 1))),
                  pl.BlockSpec((RB, CB), lambda i, j: (i, jnp.maximum(j - nc, 0))),
                  pl.BlockSpec((RB, CB), lambda i, j: (i, jnp.minimum(j, nc - 1))),
                  pl.BlockSpec((1, HD), lambda i, j: (0, 0)), tab, tab],
        out_specs=[pl.BlockSpec((RB, CB), lambda i, j: (i, j)), pl.BlockSpec((1, LANES), lambda i, j: (0, 0))],
        out_shape=[_sds((S, 2 * D), BF16), _sds((1, LANES), F32)],
        compiler_params=_params())(dq, dgate, pb, gq, cos2, sin2)


def _prep_kv_bwd(dka, dkb, dva, dvb, kv, gk, cos2, sin2):
    nt = S // RB
    per = RB // WIN

    def shifted(cur_ref, nxt_ref, has_next):
        return jnp.concatenate([cur_ref[WIN:RB, :], jnp.where(has_next, nxt_ref[...], 0.0)], axis=0)

    def body(dka_ref, dkb_ref, dkn_ref, dva_ref, dvb_ref, dvn_ref, x_ref, g_ref, c_ref, s_ref, o_ref, dgk_ref):
        i, j = pl.program_id(0), pl.program_id(1)

        @pl.when((i == 0) & (j == 0))
        def _():
            dgk_ref[...] = jnp.zeros_like(dgk_ref)

        has_next = i < nt - 1

        @pl.when(j == 0)
        def _():
            lane = _lane_iota((RB, LANES))
            g2, cos, sin = _g2(g_ref), c_ref[...], s_ref[...]
            dy_all = dka_ref[...] + shifted(dkb_ref, dkn_ref, has_next)
            dg_tot = jnp.zeros((1, LANES), F32)
            for c in _pairs(CB):
                dn = _rope_bwd(dy_all[:, c], cos, sin, lane)
                dx, dg = _head_norm_bwd(dn, x_ref[:, c], g2, lane < HD)
                o_ref[:, c] = dx.astype(BF16)
                dg_tot = dg_tot + dg
            dgk_ref[...] += dg_tot

        @pl.when(j == 1)
        def _():
            o_ref[...] = (dva_ref[...] + shifted(dvb_ref, dvn_ref, has_next)).astype(BF16)

    cur = pl.BlockSpec((RB, CB), lambda i, j: (i, 0))
    nxt = pl.BlockSpec((WIN, CB), lambda i, j: (jnp.minimum(per * (i + 1), S // WIN - 1), 0))
    tab = pl.BlockSpec((RB, LANES), lambda i, j: (i, 0))
    return pl.pallas_call(
        body, name="prep_kv_bwd", grid=(nt, 2),
        in_specs=[cur, cur, nxt, cur, cur, nxt, cur, pl.BlockSpec((1, HD), lambda i, j: (0, 0)), tab, tab],
        out_specs=[pl.BlockSpec((RB, CB), lambda i, j: (i, j)), pl.BlockSpec((1, LANES), lambda i, j: (0, 0))],
        out_shape=[_sds((S, 2 * CB), BF16), _sds((1, LANES), F32)],
        compiler_params=_params())(dka, dkb, dkb, dva, dvb, dvb, kv, gk, cos2, sin2)


def _loss_dy(y, tgt):
    def body(y_ref, t_ref, dy_ref, l_ref):
        @pl.when(pl.program_id(0) == 0)
        def _():
            l_ref[...] = jnp.zeros_like(l_ref)

        e = y_ref[...] - t_ref[...]
        dy_ref[...] = e * (1.0 / D)
        l_ref[...] += jnp.sum(jnp.sum(e * e, axis=-1, keepdims=True), axis=0, keepdims=True)

    row = pl.BlockSpec((TM, D), lambda i: (i, 0))
    return pl.pallas_call(
        body, name="loss_dy", grid=(S // TM,), in_specs=[row, row],
        out_specs=[row, pl.BlockSpec((1, LANES), lambda i: (0, 0))],
        out_shape=[_sds((S, D), F32), _sds((1, LANES), F32)], compiler_params=_params())(y, tgt)


def _rms2_bwd(du_b, du_kv, h1, g_b, g_kv, dy):
    def body(dub_ref, dukv_ref, x_ref, gb_ref, gkv_ref, dy_ref, dh_ref, dgb_ref, dgkv_ref):
        @pl.when(pl.program_id(0) == 0)
        def _():
            dgb_ref[...] = jnp.zeros_like(dgb_ref)
            dgkv_ref[...] = jnp.zeros_like(dgkv_ref)

        x = x_ref[...]
        dx1, dg1 = _rms_bwd_core(dub_ref[...], x, gb_ref[...])
        dx2, dg2 = _rms_bwd_core(dukv_ref[...], x, gkv_ref[...])
        dh_ref[...] = dy_ref[...] + dx1 + dx2
        dgb_ref[...] += dg1
        dgkv_ref[...] += dg2

    row = pl.BlockSpec((TM, D), lambda i: (i, 0))
    vec = pl.BlockSpec((1, D), lambda i: (0, 0))
    return pl.pallas_call(
        body, name="rms2_bwd", grid=(S // TM,), in_specs=[row, row, row, vec, vec, row],
        out_specs=[row, vec, vec], out_shape=[_sds((S, D), F32), _sds((1, D), F32), _sds((1, D), F32)],
        compiler_params=_params())(du_b, du_kv, h1, g_b, g_kv, dy)


def _rms_bwd(du, x, g, dres):
    def body(du_ref, x_ref, g_ref, dr_ref, dx_ref, dg_ref):
        @pl.when(pl.program_id(0) == 0)
        def _():
            dg_ref[...] = jnp.zeros_like(dg_ref)

        dx, dg = _rms_bwd_core(du_ref[...], x_ref[...], g_ref[...])
        dx_ref[...] = dr_ref[...] + dx
        dg_ref[...] += dg

    row = pl.BlockSpec((TM, D), lambda i: (i, 0))
    vec = pl.BlockSpec((1, D), lambda i: (0, 0))
    return pl.pallas_call(
        body, name="rms_bwd", grid=(S // TM,), in_specs=[row, row, vec, row], out_specs=[row, vec],
        out_shape=[_sds((S, D), F32), _sds((1, D), F32)], compiler_params=_params())(du, x, g, dres)


def _gather_first(w_in_a, w_out_a, w_kv, w_in_b, w_out_b, norm_a_g):
    def body(wia_ref, woa_ref, wkv_ref, wib_ref, wob_ref, ga_ref,
             wa_g, ga_g, woa_s, wkv_s, wib_s, wob_s, wa_s, *sems):
        wa_s[...] = wia_ref[0].astype(BF16)
        woa_s[...] = woa_ref[0].astype(BF16)
        wkv_s[...] = wkv_ref[...].astype(BF16)
        wib_s[...] = wib_ref[0].astype(BF16)
        wob_s[...] = wob_ref[0].astype(BF16)
        _exchange_ops(["gather_rows", "gather_rows"], [wa_s, ga_ref], [wa_g, ga_g], sems, True, True)

    vmem = pl.BlockSpec(memory_space=pltpu.VMEM)
    anyspec = pl.BlockSpec(memory_space=pl.ANY)
    shard = lambda w: _sds(w.shape[-2:], BF16)
    return pl.pallas_call(
        body, name="gather_first", in_specs=[vmem] * 6, out_specs=[anyspec, anyspec, vmem, vmem, vmem, vmem],
        out_shape=[_sds((N_DEV,) + w_in_a.shape[-2:], BF16), _sds((N_DEV,) + norm_a_g.shape, F32),
                   shard(w_out_a), shard(w_kv), shard(w_in_b), shard(w_out_b)],
        scratch_shapes=[pltpu.VMEM(w_in_a.shape[-2:], BF16)] + _exchange_sems(2),
        compiler_params=pltpu.CompilerParams(vmem_limit_bytes=VMEM_LIMIT, has_side_effects=True))(
            w_in_a, w_out_a, w_kv, w_in_b, w_out_b, norm_a_g)


def _gather_slab(slab):
    def body(s_ref, o_ref, *sems):
        _exchange_ops(["gather_rows"], [s_ref], [o_ref], sems, True, True)

    anyspec = pl.BlockSpec(memory_space=pl.ANY)
    return pl.pallas_call(
        body, name="gather_slab", in_specs=[anyspec], out_specs=anyspec,
        out_shape=_sds((N_DEV,) + slab.shape, slab.dtype), scratch_shapes=_exchange_sems(1),
        compiler_params=pltpu.CompilerParams(has_side_effects=True))(slab)


def _adamw(w, g, m, v):
    m = ADAM_B1 * m + (1.0 - ADAM_B1) * g
    v = ADAM_B2 * v + (1.0 - ADAM_B2) * (g * g)
    m_hat = m / (1.0 - ADAM_B1 ** ADAM_STEP)
    v_hat = v / (1.0 - ADAM_B2 ** ADAM_STEP)
    delta = -ADAM_LR * (m_hat / (jnp.sqrt(v_hat) + ADAM_EPS) + ADAM_WD * w)
    return delta, m, v


def _sum_adamw(recv, w, m, v, name):
    lead = w.ndim - 2
    rows, cols = w.shape[-2:]
    tr = 128

    def body(r_ref, w_ref, m_ref, v_ref, g_ref, d_ref, nm_ref, nv_ref):
        g = r_ref[0].astype(F32)
        for dev in range(1, N_DEV):
            g = g + r_ref[dev].astype(F32)
        g_ref[...] = g
        d_ref[...], nm_ref[...], nv_ref[...] = _adamw(w_ref[...], g, m_ref[...], v_ref[...])

    blk = pl.BlockSpec((None,) * lead + (tr, cols), lambda i: (0,) * lead + (i, 0))
    return pl.pallas_call(
        body, name=name, grid=(rows // tr,),
        in_specs=[pl.BlockSpec((N_DEV, tr, cols), lambda i: (0, i, 0)), blk, blk, blk],
        out_specs=[blk] * 4, out_shape=[_sds(w.shape, F32)] * 4,
        compiler_params=_params())(recv, w, m, v)


SLAB_ROWS = 16
SLOT = {"kv_norm_g": (8, 0, D), "norm_b_g": (9, 0, D), "b_forget": (10, 0, 16), "qnorm_a_g": (10, 128, HD),
        "knorm_a_g": (10, 256, HD), "knorm_b_g": (10, 384, HD), "qnorm_b_g": (10, 512, HD), "sinks": (10, 640, 16)}
SMALL = ["norm_a_g", "b_forget", "qnorm_a_g", "knorm_a_g", "kv_norm_g", "knorm_b_g", "norm_b_g", "qnorm_b_g", "sinks"]


LOSS_ROW = 11


def _pack_small(dg_a, dg_kv, dg_b, db_f, dgq_a, dgk_a, dgk_b, dgq_b, dsinks, lsum):
    def fold(ref):
        return ref[:, 0:HD] + ref[:, HD:2 * HD]

    def body(dga_ref, dgkv_ref, dgb_ref, dbf_ref, dgqa_ref, dgka_ref, dgkb_ref, dgqb_ref, dsk_ref, ls_ref, slab_ref):
        slab_ref[...] = jnp.zeros_like(slab_ref)
        for r in range(N_DEV):
            slab_ref[r:r + 1, 0:LANES] = dga_ref[:, LANES * r:LANES * (r + 1)]
        slab_ref[8:9, :] = dgkv_ref[...]
        slab_ref[9:10, :] = dgb_ref[...]
        slab_ref[10:11, 0:LANES] = dbf_ref[...]
        slab_ref[10:11, 128:128 + HD] = fold(dgqa_ref)
        slab_ref[10:11, 256:256 + HD] = fold(dgka_ref)
        slab_ref[10:11, 384:384 + HD] = fold(dgkb_ref)
        slab_ref[10:11, 512:512 + HD] = fold(dgqb_ref)
        slab_ref[10:11, 640:640 + LANES] = dsk_ref[...]
        slab_ref[LOSS_ROW:LOSS_ROW + 1, 0:LANES] = ls_ref[...]

    return pl.pallas_call(body, name="pack_small", out_shape=_sds((SLAB_ROWS, D), F32), compiler_params=_params())(
        dg_a, dg_kv, dg_b, db_f, dgq_a, dgk_a, dgk_b, dgq_b, dsinks, lsum)


def _small_adamw(recv, ws, ms, vs):
    k = len(SMALL)

    def body(*refs):
        r_ref = refs[0]
        w_refs, m_refs, v_refs = refs[1:1 + k], refs[1 + k:1 + 2 * k], refs[1 + 2 * k:1 + 3 * k]
        outs = refs[1 + 3 * k:1 + 7 * k]
        loss_ref, tot = refs[1 + 7 * k], refs[2 + 7 * k]
        g = r_ref[0]
        for dev in range(1, N_DEV):
            g = g + r_ref[dev]
        tot[...] = g
        loss_ref[...] = tot[LOSS_ROW:LOSS_ROW + 1, 0:LANES] * (0.5 / D)
        me = 4 * lax.axis_index("x") + 2 * lax.axis_index("y") + lax.axis_index("c")
        for p, name in enumerate(SMALL):
            if name == "norm_a_g":
                mine = lax.broadcasted_iota(jnp.int32, (N_DEV, LANES), 0) == me
                gp = jnp.sum(jnp.where(mine, tot[0:N_DEV, 0:LANES], 0.0), axis=0, keepdims=True)
            else:
                row, lo, width = SLOT[name]
                gp = tot[row:row + 1, lo:lo + width]
            d, nm, nv = _adamw(w_refs[p][...], gp, m_refs[p][...], v_refs[p][...])
            outs[p][...] = gp
            outs[k + p][...] = d
            outs[2 * k + p][...] = nm
            outs[3 * k + p][...] = nv

    shapes = [_sds(w.shape, F32) for w in ws]
    return pl.pallas_call(body, name="small_adamw", out_shape=shapes * 4 + [_sds((1, LANES), F32)],
                          scratch_shapes=[pltpu.VMEM((SLAB_ROWS, D), F32)],
                          compiler_params=_params())(recv, *ws, *ms, *vs)


def _rope_tables(positions):
    inv_freq = jnp.power(jnp.float32(ROPE_THETA), -jnp.arange(0, ROT, 2, dtype=F32) / ROT)
    ang = positions.astype(F32)[:, None] * inv_freq[None, :]
    cos, sin = jnp.cos(ang), jnp.sin(ang)
    c64 = jnp.concatenate([cos, cos, jnp.ones((S, HD - ROT), F32)], axis=-1)
    s64 = jnp.concatenate([-sin, sin, jnp.zeros((S, HD - ROT), F32)], axis=-1)
    return jnp.tile(c64, (1, 2)), jnp.tile(s64, (1, 2))


def _local_step(x, tgt, positions, g_a, wa, b_forget, gq_a, gk_a, g_kv, gk_b, g_b, gq_b, sinks,
                woa_s, wkv_s, wib_s, wob_s):
    nq = S // TQ
    cos2, sin2 = _rope_tables(positions)
    b_pad = jnp.pad(b_forget, ((0, 0), (0, LANES - N_HEADS)))

    u_a = _rms_fwd(x, g_a, "rms_a_fwd")
    proj = _mm(u_a, wa, "nn", S, 256, D, name="mm_in_a")
    qn, kn, vb = _prep_a(proj, gq_a, gk_a)
    ccol, cbc = _fgate_fwd(proj, b_pad)
    crow = ccol[:, :N_HEADS].T.reshape(N_HEADS, nq, 1, TQ)
    o_a, z_a, lse_a, woa_g, wkv_g, w_in_b, wob_g = _fox_fwd(
        qn, kn, vb, proj, crow, cbc,
        rider=[("gather_rows", woa_s), ("gather_rows", wkv_s), ("gather_cols", wib_s), ("gather_rows", wob_s)])
    w_out_a, w_kv, w_out_b = woa_g.reshape(D, D), wkv_g.reshape(D, 512), wob_g.reshape(D, D)
    h1 = _mm(z_a, w_out_a, "nn", 1024, 512, D, add=x, name="mm_out_a")
    u_kv, u_b = _rms2_fwd(h1, g_kv, g_b)
    kv = _mm(u_kv, w_kv, "nn", 1024, 512, D, name="mm_kv")
    pb = _mm(u_b, w_in_b, "nn", 1024, 512, D, name="mm_in_b")
    qb = _prep_b(pb, gq_b, cos2, sin2)
    ksh, vsh = _prep_kv(kv, gk_b, cos2, sin2)
    sinks1 = sinks.reshape(N_HEADS)
    o_b, z_b, lse_b = _swa_fwd(qb, ksh, vsh, pb, sinks1)
    y = _mm(z_b, w_out_b, "nn", 1024, 512, D, add=h1, name="mm_out_b")
    dy, lsum = _loss_dy(y, tgt)
    dw_out_b = _mm(z_b, dy, "tn", 512, 512, S, out_dtype=BF16, name="mm_dw_out_b")
    dz_b = _mm(dy, w_out_b, "nt", 1024, 512, D, name="mm_dz_b")
    dq_b, dgate_b, dka, dkb, dva, dvb, dsinks = _swa_bwd(qb, ksh, vsh, dz_b, o_b, lse_b, pb, sinks1)
    dpb, dgq_b = _prep_b_bwd(dq_b, dgate_b, pb, gq_b, cos2, sin2)
    dkv, dgk_b = _prep_kv_bwd(dka, dkb, dva, dvb, kv, gk_b, cos2, sin2)
    dw_in_b = _mm(u_b, dpb, "tn", 512, 512, S, out_dtype=BF16, name="mm_dw_in_b")
    du_b = _mm(dpb, w_in_b, "nt", 1024, 512, 2048, name="mm_du_b")
    dw_kv = _mm(u_kv, dkv, "tn", 512, 512, S, out_dtype=BF16, name="mm_dw_kv")
    du_kv = _mm(dkv, w_kv, "nt", 1024, 512, 512, name="mm_du_kv")
    dh1, dg_b, dg_kv = _rms2_bwd(du_b, du_kv, h1, g_b, g_kv, dy)
    dw_out_a = _mm(z_a, dh1, "tn", 512, 512, S, out_dtype=BF16, name="mm_dw_out_a")
    dz_a = _mm(dh1, w_out_a, "nt", 1024, 512, D, name="mm_dz_a")
    dq_a, delta_a, do_a, dgate_a, drow, r_wob, r_wib, r_wkv, r_woa = _fox_bwd_dq(
        qn, kn, vb, dz_a, proj, o_a, lse_a, crow, cbc,
        rider=[("a2a_rows", dw_out_b), ("a2a_cols", dw_in_b), ("a2a_rows", dw_kv), ("a2a_rows", dw_out_a)])
    dk_a, dv_a, dcs = _fox_bwd_dkv(qn, kn, vb, do_a, lse_a, delta_a, crow, cbc)
    dproj, dgq_a, dgk_a = _prep_a_bwd(dq_a, dk_a, dv_a, dgate_a, proj, gq_a, gk_a)
    drow_col = jnp.pad(drow.reshape(N_HEADS, S).T, ((0, 0), (0, LANES - N_HEADS)))
    dproj, db_f = _fgate_bwd(dproj, drow_col, dcs, proj, b_pad)
    dwa = _mm(u_a, dproj, "tn", 1024, 256, S, out_dtype=BF16, name="mm_dw_in_a")
    dwa_raw = jnp.concatenate([dwa[:, :FOFF + N_HEADS], dwa[:, GOFF:]], axis=1)
    dwa_slots = dwa_raw.reshape(D, N_DEV, NA_RAW // N_DEV).transpose(1, 0, 2)
    du_a, r_wa = _mm(dproj, wa, "nt", 1024, 512, NA // 2, name="mm_du_a", rider=[("a2a_slots", dwa_slots)])
    dx, dg_a = _rms_bwd(du_a, x, g_a, dh1)
    slab = _pack_small(dg_a, dg_kv, dg_b, db_f, dgq_a, dgk_a, dgk_b, dgq_b, dsinks, lsum)
    return dx, r_wa, r_woa, r_wkv, r_wib, r_wob, _gather_slab(slab)


def kernel(x, positions, norm_a_g, w_in_a, b_forget, qnorm_a_g, knorm_a_g, w_out_a, kv_norm_g, w_kv, knorm_b_g, norm_b_g, w_in_b, qnorm_b_g, sinks, w_out_b, loss_target, m_norm_a_g, m_w_in_a, m_b_forget, m_qnorm_a_g, m_knorm_a_g, m_w_out_a, m_kv_norm_g, m_w_kv, m_knorm_b_g, m_norm_b_g, m_w_in_b, m_qnorm_b_g, m_sinks, m_w_out_b, v_norm_a_g, v_w_in_a, v_b_forget, v_qnorm_a_g, v_knorm_a_g, v_w_out_a, v_kv_norm_g, v_w_kv, v_knorm_b_g, v_norm_b_g, v_w_in_b, v_qnorm_b_g, v_sinks, v_w_out_b):
    wa_g, ga_g, woa_s, wkv_s, wib_s, wob_s = _gather_first(w_in_a, w_out_a, w_kv, w_in_b, w_out_b, norm_a_g)
    wa_raw = wa_g.transpose(1, 0, 2).reshape(D, NA_RAW)
    wa = jnp.concatenate([wa_raw[:, :FOFF + N_HEADS], jnp.zeros((D, GOFF - FOFF - N_HEADS), BF16),
                          wa_raw[:, FOFF + N_HEADS:]], axis=1)

    dx, r_wa, r_woa, r_wkv, r_wib, r_wob, slab_g = _local_step(
        x[0], loss_target[0], positions, ga_g.reshape(1, D), wa, b_forget, qnorm_a_g, knorm_a_g,
        kv_norm_g.reshape(1, D), knorm_b_g.reshape(1, HD), norm_b_g, qnorm_b_g, sinks, woa_s, wkv_s, wib_s, wob_s)

    big = {}
    for name, recv, w, m, v in (
            ("w_in_a", r_wa, w_in_a, m_w_in_a, v_w_in_a), ("w_out_a", r_woa, w_out_a, m_w_out_a, v_w_out_a),
            ("w_kv", r_wkv, w_kv, m_w_kv, v_w_kv), ("w_in_b", r_wib, w_in_b, m_w_in_b, v_w_in_b),
            ("w_out_b", r_wob, w_out_b, m_w_out_b, v_w_out_b)):
        big[name] = _sum_adamw(recv, w, m, v, "adamw_" + name)

    r2 = lambda a: a.reshape(1, -1)
    small_w = dict(norm_a_g=norm_a_g, b_forget=b_forget, qnorm_a_g=qnorm_a_g, knorm_a_g=knorm_a_g,
                   kv_norm_g=kv_norm_g, knorm_b_g=knorm_b_g, norm_b_g=norm_b_g, qnorm_b_g=qnorm_b_g, sinks=sinks)
    small_m = dict(norm_a_g=m_norm_a_g, b_forget=m_b_forget, qnorm_a_g=m_qnorm_a_g, knorm_a_g=m_knorm_a_g,
                   kv_norm_g=m_kv_norm_g, knorm_b_g=m_knorm_b_g, norm_b_g=m_norm_b_g, qnorm_b_g=m_qnorm_b_g,
                   sinks=m_sinks)
    small_v = dict(norm_a_g=v_norm_a_g, b_forget=v_b_forget, qnorm_a_g=v_qnorm_a_g, knorm_a_g=v_knorm_a_g,
                   kv_norm_g=v_kv_norm_g, knorm_b_g=v_knorm_b_g, norm_b_g=v_norm_b_g, qnorm_b_g=v_qnorm_b_g,
                   sinks=v_sinks)
    res = _small_adamw(slab_g, [r2(small_w[n]) for n in SMALL], [r2(small_m[n]) for n in SMALL],
                       [r2(small_v[n]) for n in SMALL])
    k = len(SMALL)
    small = {n: [res[q * k + p].reshape(small_w[n].shape) for q in range(4)] for p, n in enumerate(SMALL)}
    loss = res[4 * k][0, 0]

    order = ["norm_a_g", "w_in_a", "b_forget", "qnorm_a_g", "knorm_a_g", "w_out_a", "kv_norm_g", "w_kv",
             "knorm_b_g", "norm_b_g", "w_in_b", "qnorm_b_g", "sinks", "w_out_b"]

    def leaf(n, q):
        return big[n][q] if n in big else small[n][q]

    outs = [loss, dx[None]]
    for q in range(4):
        outs.extend(leaf(n, q) for n in order)
    return tuple(outs)
```

```python
import jax
import jax.numpy as jnp
from jax import lax
from jax.experimental import pallas as pl
from jax.experimental.pallas import tpu as pltpu

F32, BF16 = jnp.float32, jnp.bfloat16

S = 2048
D = 1024
HD = 64
N_HEADS = 16
N_DEV = 8
NA = 4352
FOFF = 3072
GOFF = 3328
NA_RAW = 4112
EPS = 1e-6
QSCALE = 0.125
ROPE_THETA = 500000.0
ROT = 16
WIN = 128
TQ = 256
TK = 256
KS = TQ // 2
TM = 256
RB = 512
CB = 256
LANES = 128

ADAM_LR, ADAM_B1, ADAM_B2, ADAM_EPS, ADAM_WD, ADAM_STEP = 0.001, 0.9, 0.999, 1e-08, 0.01, 10

VMEM_LIMIT = 48 * 1024 * 1024


def _params():
    return pltpu.CompilerParams(vmem_limit_bytes=VMEM_LIMIT)


def _sds(shape, dtype):
    return jax.ShapeDtypeStruct(shape, dtype)


def _dot_nt(a, b):
    return lax.dot_general(a, b, (((1,), (1,)), ((), ())), preferred_element_type=F32)


def _dot_tn(a, b):
    return lax.dot_general(a, b, (((0,), (0,)), ((), ())), preferred_element_type=F32)


def _dot_nn(a, b):
    return lax.dot_general(a, b, (((1,), (0,)), ((), ())), preferred_element_type=F32)


def _sigmoid(g):
    return 1.0 / (1.0 + jnp.exp(-g))


def _lane_iota(shape):
    return lax.broadcasted_iota(jnp.int32, shape, len(shape) - 1)


def _flips(kind):
    return (2, 4, 6) if kind == "a2a_chips" else tuple(range(1, N_DEV))


def _send_view(kind, ref, dev):
    if kind in ("gather_rows", "gather_cols"):
        return ref
    if kind == "a2a_slots":
        return ref.at[dev]
    if kind == "a2a_chips":
        return ref.at[dev >> 1]
    if kind == "a2a_rows":
        rows = ref.shape[0] // N_DEV
        return ref.at[pl.ds(pl.multiple_of(dev * rows, rows), rows)]
    cols = ref.shape[1] // N_DEV
    return ref.at[:, pl.ds(pl.multiple_of(dev * cols, cols), cols)]


def _land_view(kind, ref, dev):
    if kind == "gather_cols":
        cols = ref.shape[1] // N_DEV
        return ref.at[:, pl.ds(pl.multiple_of(dev * cols, cols), cols)]
    if kind == "a2a_chips":
        return ref.at[dev >> 1]
    return ref.at[dev]


def _landing_sds(kind, arr):
    if kind == "gather_rows":
        return _sds((N_DEV,) + arr.shape, arr.dtype)
    if kind == "gather_cols":
        return _sds((arr.shape[0], N_DEV * arr.shape[1]), arr.dtype)
    if kind == "a2a_rows":
        return _sds((N_DEV, arr.shape[0] // N_DEV, arr.shape[1]), arr.dtype)
    if kind == "a2a_cols":
        return _sds((N_DEV, arr.shape[0], arr.shape[1] // N_DEV), arr.dtype)
    return _sds(arr.shape, arr.dtype)


def _exchange_sems(n_parts):
    n = n_parts * (N_DEV - 1)
    return [pltpu.SemaphoreType.DMA((n,)), pltpu.SemaphoreType.DMA((n,)), pltpu.SemaphoreType.DMA((n_parts,))]


def _exchange_ops(kinds, srcs, dsts, sems, start, wait):
    send_sems, recv_sems, local_sems = sems
    x, y, c = lax.axis_index("x"), lax.axis_index("y"), lax.axis_index("c")
    me = 4 * x + 2 * y + c

    def local(a):
        return pltpu.make_async_copy(_send_view(kinds[a], srcs[a], me), _land_view(kinds[a], dsts[a], me),
                                     local_sems.at[a])

    def remote(a, k, landing_dev):
        peer = (x ^ ((k >> 2) & 1), y ^ ((k >> 1) & 1), c ^ (k & 1))
        sem = a * (N_DEV - 1) + k - 1
        return pltpu.make_async_remote_copy(
            src_ref=_send_view(kinds[a], srcs[a], me ^ k), dst_ref=_land_view(kinds[a], dsts[a], landing_dev),
            send_sem=send_sems.at[sem], recv_sem=recv_sems.at[sem], device_id=peer,
            device_id_type=pl.DeviceIdType.MESH)

    pairs = [(a, k) for k in range(1, N_DEV) for a in range(len(kinds)) if k in _flips(kinds[a])]
    if start:
        for a in range(len(kinds)):
            local(a).start()
        for a, k in pairs:
            remote(a, k, me).start()
    if wait:
        for a, k in pairs:
            remote(a, k, me ^ k).wait_recv()
            remote(a, k, me).wait_send()
        for a in range(len(kinds)):
            local(a).wait()


def _gather_two_level(srcs, dsts, sems):
    send_sems, recv_sems, local_sems = sems
    x, y, c = lax.axis_index("x"), lax.axis_index("y"), lax.axis_index("c")
    me, sibling = (x, y, c), (x, y, 1 - c)
    chips = [(1 - x, y), (x, 1 - y), (1 - x, 1 - y)]

    def slot(ref, dev):
        return ref.at[4 * dev[0] + 2 * dev[1] + dev[2]]

    def copy(a, k, block, to, src=None):
        return pltpu.make_async_remote_copy(
            src_ref=slot(dsts[a], block) if src is None else src, dst_ref=slot(dsts[a], block),
            send_sem=send_sems.at[a * (N_DEV - 1) + k], recv_sem=recv_sems.at[a * (N_DEV - 1) + k],
            device_id=to, device_id_type=pl.DeviceIdType.MESH)

    parts = range(len(srcs))
    mine = [pltpu.make_async_copy(srcs[a], slot(dsts[a], me), local_sems.at[a]) for a in parts]
    first = [copy(a, 0, me, sibling, src=srcs[a]) for a in parts]
    first += [copy(a, 1 + j, me, (*chip, c), src=srcs[a]) for j, chip in enumerate(chips) for a in parts]
    for cp in mine + first:
        cp.start()
    passed = []
    for j, chip in enumerate(chips):
        for a in parts:
            copy(a, 1 + j, (*chip, c), me).wait_recv()
            fwd = copy(a, 4 + j, (*chip, c), sibling)
            fwd.start()
            passed.append(fwd)
    for a in parts:
        copy(a, 0, sibling, me).wait_recv()
        for j, chip in enumerate(chips):
            copy(a, 4 + j, (*chip, 1 - c), me).wait_recv()
    for cp in first + passed:
        cp.wait_send()
    for cp in mine:
        cp.wait()


def _call(body, *, name, args, in_specs, out_specs, out_shape, grid=(), scratch_shapes=(), aliases=None, rider=()):
    n_in, n_out, n_scr, n_r = len(in_specs), len(out_specs), len(scratch_shapes), len(rider)
    kinds = [kind for kind, _ in rider]

    def kernel_body(*refs):
        c_in, r_in = refs[:n_in], refs[n_in:n_in + n_r]
        c_out = refs[n_in + n_r:n_in + n_r + n_out]
        r_out = refs[n_in + n_r + n_out:n_in + 2 * n_r + n_out]
        rest = refs[n_in + 2 * n_r + n_out:]
        c_scr, sems = rest[:n_scr], rest[n_scr:]
        if n_r:
            assert grid, "a rider needs a gridded call"
            ids = [pl.program_id(ax) for ax in range(len(grid))]
            first, last = ids[0] == 0, ids[0] == grid[0] - 1
            for pid, size in zip(ids[1:], grid[1:]):
                first = first & (pid == 0)
                last = last & (pid == size - 1)
            pl.when(first)(lambda: _exchange_ops(kinds, r_in, r_out, sems, True, False))
        body(*c_in, *c_out, *c_scr)
        if n_r:
            pl.when(last)(lambda: _exchange_ops(kinds, r_in, r_out, sems, False, True))

    anyspec = pl.BlockSpec(memory_space=pl.ANY)
    params = pltpu.CompilerParams(vmem_limit_bytes=VMEM_LIMIT, has_side_effects=bool(n_r))
    outs = pl.pallas_call(
        kernel_body, name=name, grid=grid, in_specs=list(in_specs) + [anyspec] * n_r,
        out_specs=list(out_specs) + [anyspec] * n_r,
        out_shape=list(out_shape) + [_landing_sds(kind, arr) for kind, arr in rider],
        scratch_shapes=list(scratch_shapes) + (_exchange_sems(n_r) if n_r else []),
        input_output_aliases=aliases or {}, compiler_params=params)(*args, *[arr for _, arr in rider])
    return list(outs)


def _mm(a, b, mode, tm, tn, tk, out_dtype=F32, add=None, name="mm", rider=()):
    if mode == "nn":
        (m, k), n = a.shape, b.shape[1]
        a_spec = pl.BlockSpec((tm, tk), lambda i, j, kk: (i, kk))
        b_spec = pl.BlockSpec((tk, tn), lambda i, j, kk: (kk, j))
        dot = _dot_nn
    elif mode == "nt":
        (m, k), n = a.shape, b.shape[0]
        a_spec = pl.BlockSpec((tm, tk), lambda i, j, kk: (i, kk))
        b_spec = pl.BlockSpec((tn, tk), lambda i, j, kk: (j, kk))
        dot = _dot_nt
    else:
        (k, m), n = a.shape, b.shape[1]
        a_spec = pl.BlockSpec((tk, tm), lambda i, j, kk: (kk, i))
        b_spec = pl.BlockSpec((tk, tn), lambda i, j, kk: (kk, j))
        dot = _dot_tn
    assert m % tm == 0 and n % tn == 0 and k % tk == 0, (m, n, k, tm, tn, tk)
    nk = k // tk
    has_add = add is not None

    def body(*refs):
        if has_add:
            a_ref, b_ref, add_ref, o_ref, acc = refs
        else:
            a_ref, b_ref, o_ref, acc = refs
        p = dot(a_ref[...].astype(BF16), b_ref[...].astype(BF16))

        def finish(total):
            if has_add:
                total = add_ref[...] + total
            o_ref[...] = total.astype(out_dtype)

        if nk == 1:
            finish(p)
        else:
            kk = pl.program_id(2)

            @pl.when(kk == 0)
            def _():
                acc[...] = p

            @pl.when(kk > 0)
            def _():
                acc[...] += p

            @pl.when(kk == nk - 1)
            def _():
                finish(acc[...])

    in_specs = [a_spec, b_spec]
    args = [a, b]
    if has_add:
        in_specs.append(pl.BlockSpec((tm, tn), lambda i, j, kk: (i, j)))
        args.append(add)
    acc_shape = (tm, tn) if nk > 1 else (8, LANES)
    outs = _call(body, name=name, args=args, grid=(m // tm, n // tn, nk), in_specs=in_specs,
                 out_specs=[pl.BlockSpec((tm, tn), lambda i, j, kk: (i, j))], out_shape=[_sds((m, n), out_dtype)],
                 scratch_shapes=[pltpu.VMEM(acc_shape, F32)], rider=rider)
    return outs if rider else outs[0]


def _rms_rinv(x):
    return lax.rsqrt(jnp.mean(x * x, axis=-1, keepdims=True) + EPS)


def _rms_bwd_core(du, x, g):
    r = _rms_rinv(x)
    dug = du * g
    dx = r * (dug - x * ((r * r) * jnp.mean(dug * x, axis=-1, keepdims=True)))
    dg = jnp.sum(du * (x * r), axis=0, keepdims=True)
    return dx, dg


def _half_sum(v, lo_half):
    s0 = jnp.sum(jnp.where(lo_half, v, 0.0), axis=-1, keepdims=True)
    s1 = jnp.sum(jnp.where(lo_half, 0.0, v), axis=-1, keepdims=True)
    return jnp.where(lo_half, s0, s1)


def _head_rinv(x, lo_half):
    return lax.rsqrt(_half_sum(x * x, lo_half) * (1.0 / HD) + EPS)


def _head_norm_bwd(dn, x, g, lo_half):
    r = _head_rinv(x, lo_half)
    dng = dn * g
    dx = r * (dng - x * ((r * r) * (_half_sum(dng * x, lo_half) * (1.0 / HD))))
    dg = jnp.sum(dn * (x * r), axis=0, keepdims=True)
    return dx, dg


def _rope_swap(x, lane):
    l64 = lane & (HD - 1)
    return jnp.where(l64 < ROT // 2, pltpu.roll(x, LANES - ROT // 2, 1), pltpu.roll(x, ROT // 2, 1))


def _rope_fwd(x, cos, sin, lane):
    return x * cos + _rope_swap(x, lane) * sin


def _rope_bwd(dy, cos, sin, lane):
    return dy * cos + jnp.where((lane & (HD - 1)) < ROT, _rope_swap(dy * sin, lane), 0.0)


def _g2(g_ref):
    g = g_ref[...]
    return jnp.concatenate([g, g], axis=-1)


def _pairs(width):
    return [slice(LANES * c, LANES * (c + 1)) for c in range(width // LANES)]


def _rms_fwd(x, g, name):
    def body(x_ref, g_ref, u_ref):
        xv = x_ref[...]
        u_ref[...] = ((xv * _rms_rinv(xv)) * g_ref[...]).astype(BF16)

    return pl.pallas_call(
        body, name=name, grid=(S // TM,),
        in_specs=[pl.BlockSpec((TM, D), lambda i: (i, 0)), pl.BlockSpec((1, D), lambda i: (0, 0))],
        out_specs=pl.BlockSpec((TM, D), lambda i: (i, 0)), out_shape=_sds((S, D), BF16),
        compiler_params=_params())(x, g)


def _rms2_fwd(h, g1, g2):
    def body(x_ref, g1_ref, g2_ref, u1_ref, u2_ref):
        xv = x_ref[...]
        xn = xv * _rms_rinv(xv)
        u1_ref[...] = (xn * g1_ref[...]).astype(BF16)
        u2_ref[...] = (xn * g2_ref[...]).astype(BF16)

    row = pl.BlockSpec((TM, D), lambda i: (i, 0))
    vec = pl.BlockSpec((1, D), lambda i: (0, 0))
    return pl.pallas_call(
        body, name="rms2_fwd", grid=(S // TM,), in_specs=[row, vec, vec], out_specs=[row, row],
        out_shape=[_sds((S, D), BF16)] * 2, compiler_params=_params())(h, g1, g2)


def _prep_a(proj, gq, gk):
    def body(q_ref, k_ref, v_ref, gq_ref, gk_ref, qo_ref, ko_ref, vo_ref):
        lo_half = _lane_iota((RB, LANES)) < HD
        gq2, gk2 = _g2(gq_ref), _g2(gk_ref)
        for c in _pairs(CB):
            q = q_ref[:, c]
            k = k_ref[:, c]
            qo_ref[:, c] = (((q * _head_rinv(q, lo_half)) * gq2) * QSCALE).astype(BF16)
            ko_ref[:, c] = ((k * _head_rinv(k, lo_half)) * gk2).astype(BF16)
        vo_ref[...] = v_ref[...].astype(BF16)

    nc = D // CB
    blk = lambda off: pl.BlockSpec((RB, CB), lambda i, j: (i, j + off))
    gspec = pl.BlockSpec((1, HD), lambda i, j: (0, 0))
    return pl.pallas_call(
        body, name="prep_a", grid=(S // RB, nc),
        in_specs=[blk(0), blk(nc), blk(2 * nc), gspec, gspec], out_specs=[blk(0)] * 3,
        out_shape=[_sds((S, D), BF16)] * 3, compiler_params=_params())(proj, proj, proj, gq, gk)


def _pick_lane(block, lane, idx):
    return jnp.sum(jnp.where(lane == idx, block, 0.0), axis=-1, keepdims=True)


def _fgate_fwd(proj, b_pad):
    def body(f_ref, b_ref, c_ref, cbc_ref, carry):
        @pl.when(pl.program_id(0) == 0)
        def _():
            carry[...] = jnp.zeros_like(carry)

        z = f_ref[...] + b_ref[...]
        logf = jnp.minimum(z, 0.0) - jnp.log1p(jnp.exp(-jnp.abs(z)))
        r = lax.broadcasted_iota(jnp.int32, (TM, TM), 0)
        c = lax.broadcasted_iota(jnp.int32, (TM, TM), 1)
        tri = (r >= c).astype(F32)
        loc = jnp.dot(tri, logf, precision=lax.Precision.HIGHEST, preferred_element_type=F32) + carry[0:1, :]
        c_ref[...] = loc
        carry[0:1, :] = loc[TM - 1:TM, :]
        lane = _lane_iota((TM, LANES))
        for h in range(N_HEADS):
            cbc_ref[:, LANES * h:LANES * (h + 1)] = jnp.broadcast_to(_pick_lane(loc, lane, h), (TM, LANES))

    return pl.pallas_call(
        body, name="fgate_fwd", grid=(S // TM,),
        in_specs=[pl.BlockSpec((TM, LANES), lambda i: (i, FOFF // LANES)), pl.BlockSpec((1, LANES), lambda i: (0, 0))],
        out_specs=[pl.BlockSpec((TM, LANES), lambda i: (i, 0)), pl.BlockSpec((TM, N_HEADS * LANES), lambda i: (i, 0))],
        out_shape=[_sds((S, LANES), F32), _sds((S, N_HEADS * LANES), F32)],
        scratch_shapes=[pltpu.VMEM((8, LANES), F32)], compiler_params=_params())(proj, b_pad)


def _fgate_bwd(dproj, drow, dcs, proj, b_pad):
    nt = S // TM

    def body(dp_ref, dr_ref, dcs_ref, f_ref, b_ref, o_ref, db_ref, carry):
        @pl.when(pl.program_id(0) == 0)
        def _():
            carry[...] = jnp.zeros_like(carry)
            db_ref[...] = jnp.zeros_like(db_ref)

        lane = _lane_iota((TM, LANES))
        dc = dr_ref[...]
        for h in range(N_HEADS):
            dc = dc + jnp.where(lane == h, dcs_ref[:, LANES * h:LANES * h + 1], 0.0)
        r = lax.broadcasted_iota(jnp.int32, (TM, TM), 0)
        c = lax.broadcasted_iota(jnp.int32, (TM, TM), 1)
        tri = (c >= r).astype(F32)
        dlogf = jnp.dot(tri, dc, precision=lax.Precision.HIGHEST, preferred_element_type=F32) + carry[0:1, :]
        carry[0:1, :] = dlogf[0:1, :]
        z = f_ref[...] + b_ref[...]
        df = dlogf * (1.0 / (1.0 + jnp.exp(z)))
        db_ref[...] += jnp.sum(df, axis=0, keepdims=True)
        o_ref[...] = jnp.concatenate([df, jnp.zeros_like(df)], axis=-1).astype(BF16)

    return pl.pallas_call(
        body, name="fgate_bwd", grid=(nt,),
        in_specs=[pl.BlockSpec(memory_space=pl.ANY),
                  pl.BlockSpec((TM, LANES), lambda i: (nt - 1 - i, 0)),
                  pl.BlockSpec((TM, N_HEADS * LANES), lambda i: (nt - 1 - i, 0)),
                  pl.BlockSpec((TM, LANES), lambda i: (nt - 1 - i, FOFF // LANES)),
                  pl.BlockSpec((1, LANES), lambda i: (0, 0))],
        out_specs=[pl.BlockSpec((TM, 2 * LANES), lambda i: (nt - 1 - i, FOFF // (2 * LANES))),
                   pl.BlockSpec((1, LANES), lambda i: (0, 0))],
        out_shape=[_sds((S, NA), BF16), _sds((1, LANES), F32)],
        scratch_shapes=[pltpu.VMEM((8, LANES), F32)], input_output_aliases={0: 0},
        compiler_params=_params())(dproj, drow, dcs, proj, b_pad)


def _key_le_query(offset, keys=TK):
    r = lax.broadcasted_iota(jnp.int32, (keys, TQ), 0)
    c = lax.broadcasted_iota(jnp.int32, (keys, TQ), 1)
    return (r + offset) <= c


def _widen(tile):
    return jnp.concatenate([tile] * (TQ // LANES), axis=1)


def _fox_fwd(qn, kn, vb, proj, crow, cbc, rider=()):
    nq, per = S // TQ, TQ // TK

    def body(q_ref, k_ref, v_ref, g_ref, cq_ref, cbc_ref, o_ref, z_ref, lse_ref, st_s, pt_s):
        i = pl.program_id(1)
        qs = [q_ref[:, HD * hh:HD * (hh + 1)] for hh in range(2)]
        cqs = [cq_ref[hh, 0] for hh in range(2)]

        def scores(s, hh):
            off = pl.multiple_of(s * KS, KS)
            kj = k_ref[pl.ds(off, KS), HD * hh:HD * (hh + 1)]
            return (_dot_nt(kj, qs[hh]) + cqs[hh]) - _widen(cbc_ref[pl.ds(off, KS), LANES * hh:LANES * (hh + 1)])

        def values(s, hh, pt):
            off = pl.multiple_of(s * KS, KS)
            return _dot_tn(v_ref[pl.ds(off, KS), HD * hh:HD * (hh + 1)], pt)

        def step(s, slot, carries, mask=None, last=False):
            if not last:
                for hh in range(2):
                    st_s[1 - slot, hh] = scores(s + 1, hh)
            pvs = [values(jnp.maximum(s - 1, 0), hh, pt_s[1 - slot, hh]) for hh in range(2)]
            out = []
            for hh in range(2):
                m, l, acc = carries[hh]
                st = st_s[slot, hh]
                if mask is not None:
                    st = jnp.where(mask, st, -jnp.inf)
                m_new = jnp.maximum(m, jnp.max(st, axis=0, keepdims=True))
                pt = jnp.exp(st - m_new)
                alpha = jnp.exp(m - m_new)
                pt_s[slot, hh] = pt.astype(BF16)
                out.append((m_new, alpha * l + jnp.sum(pt, axis=0, keepdims=True), alpha * (acc + pvs[hh])))
            return tuple(out)

        for hh in range(2):
            st_s[0, hh] = scores(0, hh)
            pt_s[1, hh] = jnp.zeros((KS, TQ), BF16)
        one = (jnp.full((1, TQ), -jnp.inf, F32), jnp.zeros((1, TQ), F32), jnp.zeros((HD, TQ), F32))
        carries = lax.fori_loop(0, i, lambda t, cr: step(2 * t + 1, 1, step(2 * t, 0, cr)), (one, one))
        carries = step(2 * i, 0, carries, mask=_key_le_query(0, KS))
        carries = step(2 * i + 1, 1, carries, mask=_key_le_query(KS, KS), last=True)
        accs = []
        for hh in range(2):
            m, l, acc = carries[hh]
            acc = acc + values(2 * i + 1, hh, pt_s[1, hh])
            accs.append(acc / l)
            lse_ref[hh, 0] = m + jnp.log(l)
        o = jnp.concatenate(accs, axis=0).T
        o_ref[...] = o
        g = g_ref[...]
        z_ref[...] = (o * (g * _sigmoid(g))).astype(BF16)

    qblk = pl.BlockSpec((TQ, LANES), lambda hp, i: (i, hp))
    full = pl.BlockSpec((S, LANES), lambda hp, i: (0, hp))
    rows = pl.BlockSpec((2, 1, 1, TQ), lambda hp, i: (hp, i, 0, 0))
    return _call(
        body, name="fox_fwd", args=(qn, kn, vb, proj, crow, cbc), grid=(N_HEADS // 2, nq),
        in_specs=[qblk, full, full,
                  pl.BlockSpec((TQ, LANES), lambda hp, i: (i, GOFF // LANES + hp)),
                  rows, pl.BlockSpec((S, 2 * LANES), lambda hp, i: (0, hp))],
        out_specs=[qblk, qblk, rows],
        out_shape=[_sds((S, D), F32), _sds((S, D), BF16), _sds((N_HEADS, nq, 1, TQ), F32)],
        scratch_shapes=[pltpu.VMEM((2, 2, KS, TQ), F32), pltpu.VMEM((2, 2, KS, TQ), BF16)], rider=rider)


def _fox_bwd_dq(qn, kn, vb, dz, proj, o, lse, crow, cbc, rider=()):
    nq, per = S // TQ, TQ // TK

    def body(q_ref, k_ref, v_ref, dz_ref, g_ref, o_ref, lse_ref, cq_ref, cbc_ref,
             dq_ref, delta_ref, do_ref, dg_ref, dr_ref, st_s, dp_s, ds_s):
        i = pl.program_id(1)
        g = g_ref[...]
        sg = _sigmoid(g)
        dzv = dz_ref[...]
        ov = o_ref[...]
        do = dzv * (g * sg)
        dg_ref[...] = (dzv * ov * (sg * (1.0 + g * (1.0 - sg)))).astype(BF16)
        do_ref[...] = do.astype(BF16)
        prod_t = (do * ov).T
        qs = [q_ref[:, HD * hh:HD * (hh + 1)] for hh in range(2)]
        dobs = [do[:, HD * hh:HD * (hh + 1)].astype(BF16) for hh in range(2)]
        lses = [lse_ref[hh, 0] for hh in range(2)]
        cqs = [cq_ref[hh, 0] for hh in range(2)]
        deltas = [jnp.sum(prod_t[HD * hh:HD * (hh + 1), :], axis=0, keepdims=True) for hh in range(2)]
        for hh in range(2):
            delta_ref[hh, 0] = deltas[hh]

        def products(s, hh):
            off = pl.multiple_of(s * KS, KS)
            kj = k_ref[pl.ds(off, KS), HD * hh:HD * (hh + 1)]
            vj = v_ref[pl.ds(off, KS), HD * hh:HD * (hh + 1)]
            st = (_dot_nt(kj, qs[hh]) + cqs[hh]) - _widen(cbc_ref[pl.ds(off, KS), LANES * hh:LANES * (hh + 1)])
            return st, _dot_nt(vj, dobs[hh])

        def dq_of(s, hh, dst):
            off = pl.multiple_of(s * KS, KS)
            return _dot_tn(k_ref[pl.ds(off, KS), HD * hh:HD * (hh + 1)], dst)

        def step(s, slot, carries, mask=None, last=False):
            if not last:
                for hh in range(2):
                    st_s[1 - slot, hh], dp_s[1 - slot, hh] = products(s + 1, hh)
            dqs = [dq_of(jnp.maximum(s - 1, 0), hh, ds_s[1 - slot, hh]) for hh in range(2)]
            out = []
            for hh in range(2):
                dqt, dr = carries[hh]
                st = st_s[slot, hh]
                if mask is not None:
                    st = jnp.where(mask, st, -jnp.inf)
                dst = jnp.exp(st - lses[hh]) * (dp_s[slot, hh] - deltas[hh])
                ds_s[slot, hh] = dst.astype(BF16)
                out.append((dqt + dqs[hh], dr + jnp.sum(dst, axis=0, keepdims=True)))
            return tuple(out)

        for hh in range(2):
            st_s[0, hh], dp_s[0, hh] = products(0, hh)
            ds_s[1, hh] = jnp.zeros((KS, TQ), BF16)
        one = (jnp.zeros((HD, TQ), F32), jnp.zeros((1, TQ), F32))
        carries = lax.fori_loop(0, i, lambda t, cr: step(2 * t + 1, 1, step(2 * t, 0, cr)), (one, one))
        carries = step(2 * i, 0, carries, mask=_key_le_query(0, KS))
        carries = step(2 * i + 1, 1, carries, mask=_key_le_query(KS, KS), last=True)
        for hh in range(2):
            dr_ref[hh, 0] = carries[hh][1]
        dq_ref[...] = jnp.concatenate(
            [carries[hh][0] + dq_of(2 * i + 1, hh, ds_s[1, hh]) for hh in range(2)], axis=0).T

    qblk = pl.BlockSpec((TQ, LANES), lambda hp, i: (i, hp))
    full = pl.BlockSpec((S, LANES), lambda hp, i: (0, hp))
    rows = pl.BlockSpec((2, 1, 1, TQ), lambda hp, i: (hp, i, 0, 0))
    rows_shape = _sds((N_HEADS, nq, 1, TQ), F32)
    return _call(
        body, name="fox_bwd_dq", args=(qn, kn, vb, dz, proj, o, lse, crow, cbc), grid=(N_HEADS // 2, nq),
        in_specs=[qblk, full, full, qblk,
                  pl.BlockSpec((TQ, LANES), lambda hp, i: (i, GOFF // LANES + hp)),
                  qblk, rows, rows, pl.BlockSpec((S, 2 * LANES), lambda hp, i: (0, hp))],
        out_specs=[qblk, rows, qblk, qblk, rows],
        out_shape=[_sds((S, D), F32), rows_shape, _sds((S, D), BF16), _sds((S, D), BF16), rows_shape],
        scratch_shapes=[pltpu.VMEM((2, 2, KS, TQ), F32), pltpu.VMEM((2, 2, KS, TQ), F32),
                        pltpu.VMEM((2, 2, KS, TQ), BF16)], rider=rider)


def _fox_bwd_dkv(qn, kn, vb, dob, lse, delta, crow, cbc):
    nq, per = S // TQ, TQ // TK

    def body(q_ref, k_ref, v_ref, do_ref, lse_ref, del_ref, cq_ref, cbc_ref, dk_ref, dv_ref, dcs_ref,
             st_s, dp_s, pt_s, ds_s):
        j = pl.program_id(1)
        kjs = [k_ref[:, HD * hh:HD * (hh + 1)] for hh in range(2)]
        vjs = [v_ref[:, HD * hh:HD * (hh + 1)] for hh in range(2)]

        def rows_of(ref, u, hh):
            off = pl.multiple_of(u * TQ, TQ)
            return ref[pl.ds(off, TQ), HD * hh:HD * (hh + 1)]

        def products(u, hh):
            st = (_dot_nt(kjs[hh], rows_of(q_ref, u, hh)) + cq_ref[hh, u]) - _widen(
                cbc_ref[:, LANES * hh:LANES * (hh + 1)])
            return st, _dot_nt(vjs[hh], rows_of(do_ref, u, hh))

        def step(u, slot, carries, masked=False):
            nxt = jnp.minimum(u + 1, nq - 1)
            for hh in range(2):
                st_s[1 - slot, hh], dp_s[1 - slot, hh] = products(nxt, hh)
            prev = jnp.maximum(u - 1, 0)
            dvs = [_dot_nn(pt_s[1 - slot, hh], rows_of(do_ref, prev, hh)) for hh in range(2)]
            dks = [_dot_nn(ds_s[1 - slot, hh], rows_of(q_ref, prev, hh)) for hh in range(2)]
            out = []
            for hh in range(2):
                dk, dv, dcs = carries[hh]
                st = st_s[slot, hh]
                if masked:
                    st = jnp.where(_key_le_query((j - u) * TQ), st, -jnp.inf)
                pt = jnp.exp(st - lse_ref[hh, u])
                dst = pt * (dp_s[slot, hh] - del_ref[hh, u])
                pt_s[slot, hh] = pt.astype(BF16)
                ds_s[slot, hh] = dst.astype(BF16)
                out.append((dk + dks[hh], dv + dvs[hh], dcs + (dst[:, :LANES] + dst[:, LANES:])))
            return tuple(out)

        t0 = j // 2
        for hh in range(2):
            st_s[0, hh], dp_s[0, hh] = products(2 * t0, hh)
            pt_s[1, hh] = jnp.zeros((TK, TQ), BF16)
            ds_s[1, hh] = jnp.zeros((TK, TQ), BF16)
        one = (jnp.zeros((TK, HD), F32), jnp.zeros((TK, HD), F32), jnp.zeros((TK, LANES), F32))
        carries = step(2 * t0 + 1, 1, step(2 * t0, 0, (one, one), masked=True), masked=True)
        carries = lax.fori_loop(t0 + 1, nq // 2, lambda t, cr: step(2 * t + 1, 1, step(2 * t, 0, cr)), carries)
        dks, dvs = [], []
        for hh in range(2):
            dk, dv, dcs = carries[hh]
            dks.append(dk + _dot_nn(ds_s[1, hh], rows_of(q_ref, nq - 1, hh)))
            dvs.append(dv + _dot_nn(pt_s[1, hh], rows_of(do_ref, nq - 1, hh)))
            dcs_ref[:, LANES * hh:LANES * (hh + 1)] = jnp.broadcast_to(
                -jnp.sum(dcs, axis=-1, keepdims=True), (TK, LANES))
        dk_ref[...] = jnp.concatenate(dks, axis=-1)
        dv_ref[...] = jnp.concatenate(dvs, axis=-1)

    kblk = pl.BlockSpec((TK, LANES), lambda hp, j: (j, hp))
    full = pl.BlockSpec((S, LANES), lambda hp, j: (0, hp))
    rows = pl.BlockSpec((2, nq, 1, TQ), lambda hp, j: (hp, 0, 0, 0))
    cblk = pl.BlockSpec((TK, 2 * LANES), lambda hp, j: (j, hp))
    return pl.pallas_call(
        body, name="fox_bwd_dkv", grid=(N_HEADS // 2, S // TK),
        in_specs=[full, kblk, kblk, full, rows, rows, rows, cblk],
        out_specs=[kblk, kblk, cblk],
        out_shape=[_sds((S, D), F32), _sds((S, D), F32), _sds((S, N_HEADS * LANES), F32)],
        scratch_shapes=[pltpu.VMEM((2, 2, TK, TQ), F32), pltpu.VMEM((2, 2, TK, TQ), F32),
                        pltpu.VMEM((2, 2, TK, TQ), BF16), pltpu.VMEM((2, 2, TK, TQ), BF16)],
        compiler_params=_params())(qn, kn, vb, dob, lse, delta, crow, cbc)


def _prep_a_bwd(dq, dk, dv, dgate, proj, gq, gk):
    nc = D // CB

    def body(dq_ref, dk_ref, dv_ref, dgt_ref, x_ref, gq_ref, gk_ref, o_ref, dgq_ref, dgk_ref):
        i, j = pl.program_id(0), pl.program_id(1)

        @pl.when((i == 0) & (j == 0))
        def _():
            dgq_ref[...] = jnp.zeros_like(dgq_ref)
            dgk_ref[...] = jnp.zeros_like(dgk_ref)

        @pl.when(j < 2 * nc)
        def _():
            lo_half = _lane_iota((RB, LANES)) < HD
            is_q = j < nc
            g = jnp.where(is_q, _g2(gq_ref), _g2(gk_ref))
            dg_tot = jnp.zeros((1, LANES), F32)
            for c in _pairs(CB):
                dn = jnp.where(is_q, dq_ref[:, c] * QSCALE, dk_ref[:, c])
                dx, dg = _head_norm_bwd(dn, x_ref[:, c], g, lo_half)
                o_ref[:, c] = dx.astype(BF16)
                dg_tot = dg_tot + dg
            dgq_ref[...] += jnp.where(is_q, dg_tot, 0.0)
            dgk_ref[...] += jnp.where(is_q, 0.0, dg_tot)

        @pl.when((j >= 2 * nc) & (j < 3 * nc))
        def _():
            o_ref[...] = dv_ref[...].astype(BF16)

        @pl.when(j >= 3 * nc)
        def _():
            o_ref[...] = dgt_ref[...]

    clamp = lambda off: (lambda i, j: (i, jnp.clip(j - off, 0, nc - 1)))
    gspec = pl.BlockSpec((1, HD), lambda i, j: (0, 0))
    acc = pl.BlockSpec((1, LANES), lambda i, j: (0, 0))
    gate_shift = (GOFF - 3 * D) // CB
    return pl.pallas_call(
        body, name="prep_a_bwd", grid=(S // RB, 4 * nc),
        in_specs=[pl.BlockSpec((RB, CB), clamp(0)), pl.BlockSpec((RB, CB), clamp(nc)),
                  pl.BlockSpec((RB, CB), clamp(2 * nc)), pl.BlockSpec((RB, CB), clamp(3 * nc)),
                  pl.BlockSpec((RB, CB), lambda i, j: (i, jnp.minimum(j, 2 * nc - 1))), gspec, gspec],
        out_specs=[pl.BlockSpec((RB, CB), lambda i, j: (i, jnp.where(j < 3 * nc, j, j + gate_shift))), acc, acc],
        out_shape=[_sds((S, NA), BF16), _sds((1, LANES), F32), _sds((1, LANES), F32)],
        compiler_params=_params())(dq, dk, dv, dgate, proj, gq, gk)


def _prep_b(pb, gq, cos2, sin2):
    def body(x_ref, g_ref, c_ref, s_ref, o_ref):
        lane = _lane_iota((RB, LANES))
        g2, cos, sin = _g2(g_ref), c_ref[...], s_ref[...]
        for c in _pairs(CB):
            x = x_ref[:, c]
            xn = (x * _head_rinv(x, lane < HD)) * g2
            o_ref[:, c] = (_rope_fwd(xn, cos, sin, lane) * QSCALE).astype(BF16)

    blk = pl.BlockSpec((RB, CB), lambda i, j: (i, j))
    tab = pl.BlockSpec((RB, LANES), lambda i, j: (i, 0))
    return pl.pallas_call(
        body, name="prep_b", grid=(S // RB, D // CB),
        in_specs=[blk, pl.BlockSpec((1, HD), lambda i, j: (0, 0)), tab, tab], out_specs=blk,
        out_shape=_sds((S, D), BF16), compiler_params=_params())(pb, gq, cos2, sin2)


def _prep_kv(kv, gk, cos2, sin2):
    def body(k_ref, v_ref, g_ref, c_ref, s_ref, ko_ref, vo_ref):
        lane = _lane_iota((RB, LANES))
        g2, cos, sin = _g2(g_ref), c_ref[...], s_ref[...]
        for c in _pairs(CB):
            x = k_ref[:, c]
            xn = (x * _head_rinv(x, lane < HD)) * g2
            ko_ref[:, c] = _rope_fwd(xn, cos, sin, lane).astype(BF16)
        vo_ref[...] = v_ref[...].astype(BF16)

    blk = lambda off: pl.BlockSpec((RB, CB), lambda i: (i, off))
    tab = pl.BlockSpec((RB, LANES), lambda i: (i, 0))
    return pl.pallas_call(
        body, name="prep_kv", grid=(S // RB,),
        in_specs=[blk(0), blk(1), pl.BlockSpec((1, HD), lambda i: (0, 0)), tab, tab],
        out_specs=[blk(0), blk(0)], out_shape=[_sds((S, CB), BF16)] * 2,
        compiler_params=_params())(kv, kv, gk, cos2, sin2)


def _swa_mask(n):
    r = lax.broadcasted_iota(jnp.int32, (4 * WIN, 2 * WIN), 0) & (WIN - 1)
    c = lax.broadcasted_iota(jnp.int32, (4 * WIN, 2 * WIN), 1)
    return (c > r) & (c <= r + WIN) & ((c >= WIN) | (n > 0))


def _stack4(ref_or_val, base):
    return jnp.concatenate([ref_or_val[:, base + HD * g: base + HD * (g + 1)] for g in range(4)], axis=0)


def _sink_col(s_ref, first):
    r = lax.broadcasted_iota(jnp.int32, (4 * WIN, 1), 0)
    col = jnp.full((4 * WIN, 1), s_ref[first + 3], F32)
    for g in range(2, -1, -1):
        col = jnp.where(r < WIN * (g + 1), s_ref[first + g], col)
    return col


def _swa_fwd(qb, ksh, vsh, pb, sinks):
    nb = S // WIN

    def body(q_ref, kp_ref, kc_ref, vp_ref, vc_ref, g_ref, s_ref, o_ref, z_ref, lse_ref):
        kp, n = pl.program_id(0), pl.program_id(1)
        valid = _swa_mask(n)
        outs, lses = [], []
        for kh in range(2):
            lo = HD * kh
            kb = jnp.concatenate([kp_ref[:, lo:lo + HD], kc_ref[:, lo:lo + HD]], axis=0)
            vb = jnp.concatenate([vp_ref[:, lo:lo + HD], vc_ref[:, lo:lo + HD]], axis=0)
            qs = _stack4(q_ref, 4 * HD * kh)
            s = jnp.where(valid, _dot_nt(qs, kb), -jnp.inf)
            sink = _sink_col(s_ref, (2 * kp + kh) * 4)
            m = jnp.maximum(jnp.max(s, axis=-1, keepdims=True), sink)
            p = jnp.exp(s - m)
            l = jnp.sum(p, axis=-1, keepdims=True) + jnp.exp(sink - m)
            os_ = _dot_nn(p.astype(BF16), vb) / l
            lse = m + jnp.log(l)
            for g in range(4):
                outs.append(os_[WIN * g:WIN * (g + 1), :])
                lses.append(jnp.broadcast_to(lse[WIN * g:WIN * (g + 1), :], (WIN, HD)))
        o = jnp.concatenate(outs, axis=-1)
        o_ref[...] = o
        g = g_ref[...]
        z_ref[...] = (o * (g * _sigmoid(g))).astype(BF16)
        lse_ref[...] = jnp.concatenate(lses, axis=-1)

    qblk = pl.BlockSpec((WIN, 512), lambda kp, n: (n, kp))
    prev = pl.BlockSpec((WIN, LANES), lambda kp, n: (jnp.maximum(n - 1, 0), kp))
    cur = pl.BlockSpec((WIN, LANES), lambda kp, n: (n, kp))
    return pl.pallas_call(
        body, name="swa_fwd", grid=(2, nb),
        in_specs=[qblk, prev, cur, prev, cur, pl.BlockSpec((WIN, 512), lambda kp, n: (n, 2 + kp)),
                  pl.BlockSpec(memory_space=pltpu.SMEM)],
        out_specs=[qblk, qblk, qblk],
        out_shape=[_sds((S, D), F32), _sds((S, D), BF16), _sds((S, D), F32)],
        compiler_params=_params())(qb, ksh, ksh, vsh, vsh, pb, sinks)


def _swa_bwd(qb, ksh, vsh, dz, o, lse, pb, sinks):
    nb = S // WIN

    def body(q_ref, kp_ref, kc_ref, vp_ref, vc_ref, dz_ref, o_ref, lse_ref, g_ref, s_ref,
             dq_ref, dg_ref, dka_ref, dkb_ref, dva_ref, dvb_ref, dsink_ref):
        kp, n = pl.program_id(0), pl.program_id(1)

        @pl.when((kp == 0) & (n == 0))
        def _():
            dsink_ref[...] = jnp.zeros_like(dsink_ref)

        valid = _swa_mask(n)
        g = g_ref[...]
        sg = _sigmoid(g)
        dzv = dz_ref[...]
        ov = o_ref[...]
        do = dzv * (g * sg)
        dg_ref[...] = (dzv * ov * (sg * (1.0 + g * (1.0 - sg)))).astype(BF16)
        prod = do * ov
        lane1 = _lane_iota((1, LANES))
        dqs, dkas, dkbs, dvas, dvbs = [], [], [], [], []
        dsink = jnp.zeros((1, LANES), F32)
        for kh in range(2):
            lo = HD * kh
            kb = jnp.concatenate([kp_ref[:, lo:lo + HD], kc_ref[:, lo:lo + HD]], axis=0)
            vb = jnp.concatenate([vp_ref[:, lo:lo + HD], vc_ref[:, lo:lo + HD]], axis=0)
            base = 4 * HD * kh
            qs = _stack4(q_ref, base)
            dos = _stack4(do, base).astype(BF16)
            delta = jnp.sum(_stack4(prod, base), axis=-1, keepdims=True)
            lse_s = jnp.concatenate([lse_ref[:, base + HD * gg: base + HD * gg + 1] for gg in range(4)], axis=0)
            s = jnp.where(valid, _dot_nt(qs, kb), -jnp.inf)
            p = jnp.exp(s - lse_s)
            ds = p * (_dot_nt(dos, vb) - delta)
            dsb = ds.astype(BF16)
            dqst = _dot_nn(dsb, kb)
            dkband = _dot_tn(dsb, qs)
            dvband = _dot_tn(p.astype(BF16), dos)
            for gg in range(4):
                dqs.append(dqst[WIN * gg:WIN * (gg + 1), :])
            dkbs.append(dkband[0:WIN, :])
            dkas.append(dkband[WIN:2 * WIN, :])
            dvbs.append(dvband[0:WIN, :])
            dvas.append(dvband[WIN:2 * WIN, :])
            first = (2 * kp + kh) * 4
            ps_delta = jnp.exp(_sink_col(s_ref, first) - lse_s) * delta
            for gg in range(4):
                val = jnp.sum(ps_delta[WIN * gg:WIN * (gg + 1), :], axis=0, keepdims=True)
                dsink = dsink - jnp.where(lane1 == first + gg, val, 0.0)
        dq_ref[...] = jnp.concatenate(dqs, axis=-1)
        dka_ref[...] = jnp.concatenate(dkas, axis=-1)
        dkb_ref[...] = jnp.concatenate(dkbs, axis=-1)
        dva_ref[...] = jnp.concatenate(dvas, axis=-1)
        dvb_ref[...] = jnp.concatenate(dvbs, axis=-1)
        dsink_ref[...] += dsink

    qblk = pl.BlockSpec((WIN, 512), lambda kp, n: (n, kp))
    prev = pl.BlockSpec((WIN, LANES), lambda kp, n: (jnp.maximum(n - 1, 0), kp))
    cur = pl.BlockSpec((WIN, LANES), lambda kp, n: (n, kp))
    return pl.pallas_call(
        body, name="swa_bwd", grid=(2, nb),
        in_specs=[qblk, prev, cur, prev, cur, qblk, qblk, qblk,
                  pl.BlockSpec((WIN, 512), lambda kp, n: (n, 2 + kp)), pl.BlockSpec(memory_space=pltpu.SMEM)],
        out_specs=[qblk, qblk, cur, cur, cur, cur, pl.BlockSpec((1, LANES), lambda kp, n: (0, 0))],
        out_shape=[_sds((S, D), F32), _sds((S, D), BF16)] + [_sds((S, 256), F32)] * 4 + [_sds((1, LANES), F32)],
        compiler_params=_params())(qb, ksh, ksh, vsh, vsh, dz, o, lse, pb, sinks)


def _prep_b_bwd(dq, dgate, pb, gq, cos2, sin2):
    nc = D // CB

    def body(dq_ref, dgt_ref, x_ref, g_ref, c_ref, s_ref, o_ref, dgq_ref):
        i, j = pl.program_id(0), pl.program_id(1)

        @pl.when((i == 0) & (j == 0))
        def _():
            dgq_ref[...] = jnp.zeros_like(dgq_ref)

        @pl.when(j < nc)
        def _():
            lane = _lane_iota((RB, LANES))
            g2, cos, sin = _g2(g_ref), c_ref[...], s_ref[...]
            dg_tot = jnp.zeros((1, LANES), F32)
            for c in _pairs(CB):
                dn = _rope_bwd(dq_ref[:, c] * QSCALE, cos, sin, lane)
                dx, dg = _head_norm_bwd(dn, x_ref[:, c], g2, lane < HD)
                o_ref[:, c] = dx.astype(BF16)
                dg_tot = dg_tot + dg
            dgq_ref[...] += dg_tot

        @pl.when(j >= nc)
        def _():
            o_ref[...] = dgt_ref[...]

    tab = pl.BlockSpec((RB, LANES), lambda i, j: (i, 0))
    return pl.pallas_call(
        body, name="prep_b_bwd", grid=(S // RB, 2 * nc),
        in_specs=[pl.BlockSpec((RB, CB), lambda i, j: (i, jnp.minimum(j, nc - 1))),
                  pl.BlockSpec((RB, CB), lambda i, j: (i, jnp.maximum(j - nc, 0))),
                  pl.BlockSpec((RB, CB), lambda i, j: (i, jnp.minimum(j, nc - 1))),
                  pl.BlockSpec((1, HD), lambda i, j: (0, 0)), tab, tab],
        out_specs=[pl.BlockSpec((RB, CB), lambda i, j: (i, j)), pl.BlockSpec((1, LANES), lambda i, j: (0, 0))],
        out_shape=[_sds((S, 2 * D), BF16), _sds((1, LANES), F32)],
        compiler_params=_params())(dq, dgate, pb, gq, cos2, sin2)


def _prep_kv_bwd(dka, dkb, dva, dvb, kv, gk, cos2, sin2):
    nt = S // RB
    per = RB // WIN

    def shifted(cur_ref, nxt_ref, has_next):
        return jnp.concatenate([cur_ref[WIN:RB, :], jnp.where(has_next, nxt_ref[...], 0.0)], axis=0)

    def body(dka_ref, dkb_ref, dkn_ref, dva_ref, dvb_ref, dvn_ref, x_ref, g_ref, c_ref, s_ref, o_ref, dgk_ref):
        i, j = pl.program_id(0), pl.program_id(1)

        @pl.when((i == 0) & (j == 0))
        def _():
            dgk_ref[...] = jnp.zeros_like(dgk_ref)

        has_next = i < nt - 1

        @pl.when(j == 0)
        def _():
            lane = _lane_iota((RB, LANES))
            g2, cos, sin = _g2(g_ref), c_ref[...], s_ref[...]
            dy_all = dka_ref[...] + shifted(dkb_ref, dkn_ref, has_next)
            dg_tot = jnp.zeros((1, LANES), F32)
            for c in _pairs(CB):
                dn = _rope_bwd(dy_all[:, c], cos, sin, lane)
                dx, dg = _head_norm_bwd(dn, x_ref[:, c], g2, lane < HD)
                o_ref[:, c] = dx.astype(BF16)
                dg_tot = dg_tot + dg
            dgk_ref[...] += dg_tot

        @pl.when(j == 1)
        def _():
            o_ref[...] = (dva_ref[...] + shifted(dvb_ref, dvn_ref, has_next)).astype(BF16)

    cur = pl.BlockSpec((RB, CB), lambda i, j: (i, 0))
    nxt = pl.BlockSpec((WIN, CB), lambda i, j: (jnp.minimum(per * (i + 1), S // WIN - 1), 0))
    tab = pl.BlockSpec((RB, LANES), lambda i, j: (i, 0))
    return pl.pallas_call(
        body, name="prep_kv_bwd", grid=(nt, 2),
        in_specs=[cur, cur, nxt, cur, cur, nxt, cur, pl.BlockSpec((1, HD), lambda i, j: (0, 0)), tab, tab],
        out_specs=[pl.BlockSpec((RB, CB), lambda i, j: (i, j)), pl.BlockSpec((1, LANES), lambda i, j: (0, 0))],
        out_shape=[_sds((S, 2 * CB), BF16), _sds((1, LANES), F32)],
        compiler_params=_params())(dka, dkb, dkb, dva, dvb, dvb, kv, gk, cos2, sin2)


def _loss_dy(y, tgt):
    def body(y_ref, t_ref, dy_ref, l_ref):
        @pl.when(pl.program_id(0) == 0)
        def _():
            l_ref[...] = jnp.zeros_like(l_ref)

        e = y_ref[...] - t_ref[...]
        dy_ref[...] = e * (1.0 / D)
        l_ref[...] += jnp.sum(jnp.sum(e * e, axis=-1, keepdims=True), axis=0, keepdims=True)

    row = pl.BlockSpec((TM, D), lambda i: (i, 0))
    return pl.pallas_call(
        body, name="loss_dy", grid=(S // TM,), in_specs=[row, row],
        out_specs=[row, pl.BlockSpec((1, LANES), lambda i: (0, 0))],
        out_shape=[_sds((S, D), F32), _sds((1, LANES), F32)], compiler_params=_params())(y, tgt)


def _rms2_bwd(du_b, du_kv, h1, g_b, g_kv, dy):
    def body(dub_ref, dukv_ref, x_ref, gb_ref, gkv_ref, dy_ref, dh_ref, dgb_ref, dgkv_ref):
        @pl.when(pl.program_id(0) == 0)
        def _():
            dgb_ref[...] = jnp.zeros_like(dgb_ref)
            dgkv_ref[...] = jnp.zeros_like(dgkv_ref)

        x = x_ref[...]
        dx1, dg1 = _rms_bwd_core(dub_ref[...], x, gb_ref[...])
        dx2, dg2 = _rms_bwd_core(dukv_ref[...], x, gkv_ref[...])
        dh_ref[...] = dy_ref[...] + dx1 + dx2
        dgb_ref[...] += dg1
        dgkv_ref[...] += dg2

    row = pl.BlockSpec((TM, D), lambda i: (i, 0))
    vec = pl.BlockSpec((1, D), lambda i: (0, 0))
    return pl.pallas_call(
        body, name="rms2_bwd", grid=(S // TM,), in_specs=[row, row, row, vec, vec, row],
        out_specs=[row, vec, vec], out_shape=[_sds((S, D), F32), _sds((1, D), F32), _sds((1, D), F32)],
        compiler_params=_params())(du_b, du_kv, h1, g_b, g_kv, dy)


def _rms_bwd(du, x, g, dres):
    def body(du_ref, x_ref, g_ref, dr_ref, dx_ref, dg_ref):
        @pl.when(pl.program_id(0) == 0)
        def _():
            dg_ref[...] = jnp.zeros_like(dg_ref)

        dx, dg = _rms_bwd_core(du_ref[...], x_ref[...], g_ref[...])
        dx_ref[...] = dr_ref[...] + dx
        dg_ref[...] += dg

    row = pl.BlockSpec((TM, D), lambda i: (i, 0))
    vec = pl.BlockSpec((1, D), lambda i: (0, 0))
    return pl.pallas_call(
        body, name="rms_bwd", grid=(S // TM,), in_specs=[row, row, vec, row], out_specs=[row, vec],
        out_shape=[_sds((S, D), F32), _sds((1, D), F32)], compiler_params=_params())(du, x, g, dres)


def _gather_first(w_in_a, w_out_a, w_kv, w_in_b, w_out_b, norm_a_g):
    def body(wia_ref, woa_ref, wkv_ref, wib_ref, wob_ref, ga_ref,
             wa_g, ga_g, woa_s, wkv_s, wib_s, wob_s, wa_s, *sems):
        wa_s[...] = wia_ref[0].astype(BF16)
        woa_s[...] = woa_ref[0].astype(BF16)
        wkv_s[...] = wkv_ref[...].astype(BF16)
        wib_s[...] = wib_ref[0].astype(BF16)
        wob_s[...] = wob_ref[0].astype(BF16)
        _gather_two_level([wa_s, ga_ref], [wa_g, ga_g], sems)

    vmem = pl.BlockSpec(memory_space=pltpu.VMEM)
    anyspec = pl.BlockSpec(memory_space=pl.ANY)
    shard = lambda w: _sds(w.shape[-2:], BF16)
    return pl.pallas_call(
        body, name="gather_first", in_specs=[vmem] * 6, out_specs=[anyspec, anyspec, vmem, vmem, vmem, vmem],
        out_shape=[_sds((N_DEV,) + w_in_a.shape[-2:], BF16), _sds((N_DEV,) + norm_a_g.shape, F32),
                   shard(w_out_a), shard(w_kv), shard(w_in_b), shard(w_out_b)],
        scratch_shapes=[pltpu.VMEM(w_in_a.shape[-2:], BF16)] + _exchange_sems(2),
        compiler_params=pltpu.CompilerParams(vmem_limit_bytes=VMEM_LIMIT, has_side_effects=True))(
            w_in_a, w_out_a, w_kv, w_in_b, w_out_b, norm_a_g)


def _pair_reduce(slots):
    n_chip = N_DEV // 2
    _, rows, cols = slots.shape

    def body(s_ref, o_ref, own_v, sib_v, send_sems, recv_sems, local_sems):
        x, y, c = lax.axis_index("x"), lax.axis_index("y"), lax.axis_index("c")
        copies = []
        for j in range(n_chip):
            own = pltpu.make_async_copy(s_ref.at[2 * j + c], own_v.at[j], local_sems.at[j])
            give = pltpu.make_async_remote_copy(
                src_ref=s_ref.at[2 * j + 1 - c], dst_ref=sib_v.at[j], send_sem=send_sems.at[j],
                recv_sem=recv_sems.at[j], device_id=(x, y, 1 - c), device_id_type=pl.DeviceIdType.MESH)
            own.start()
            give.start()
            copies.append((own, give))
        for j, (own, give) in enumerate(copies):
            own.wait()
            give.wait()
            o_ref[j] = (own_v[j].astype(F32) + sib_v[j].astype(F32)).astype(BF16)

    half = _sds((n_chip, rows, cols), slots.dtype)
    return pl.pallas_call(
        body, name="pair_reduce", in_specs=[pl.BlockSpec(memory_space=pl.ANY)],
        out_specs=pl.BlockSpec(memory_space=pltpu.VMEM), out_shape=half,
        scratch_shapes=[pltpu.VMEM(half.shape, half.dtype), pltpu.VMEM(half.shape, half.dtype),
                        pltpu.SemaphoreType.DMA((n_chip,)), pltpu.SemaphoreType.DMA((n_chip,)),
                        pltpu.SemaphoreType.DMA((n_chip,))],
        compiler_params=pltpu.CompilerParams(vmem_limit_bytes=VMEM_LIMIT, has_side_effects=True))(slots)


def _gather_slab(slab):
    def body(s_ref, o_ref, *sems):
        _exchange_ops(["gather_rows"], [s_ref], [o_ref], sems, True, True)

    anyspec = pl.BlockSpec(memory_space=pl.ANY)
    return pl.pallas_call(
        body, name="gather_slab", in_specs=[anyspec], out_specs=anyspec,
        out_shape=_sds((N_DEV,) + slab.shape, slab.dtype), scratch_shapes=_exchange_sems(1),
        compiler_params=pltpu.CompilerParams(has_side_effects=True))(slab)


def _adamw(w, g, m, v):
    m = ADAM_B1 * m + (1.0 - ADAM_B1) * g
    v = ADAM_B2 * v + (1.0 - ADAM_B2) * (g * g)
    m_hat = m / (1.0 - ADAM_B1 ** ADAM_STEP)
    v_hat = v / (1.0 - ADAM_B2 ** ADAM_STEP)
    delta = -ADAM_LR * (m_hat / (jnp.sqrt(v_hat) + ADAM_EPS) + ADAM_WD * w)
    return delta, m, v


def _sum_adamw(recv, w, m, v, name):
    lead = w.ndim - 2
    rows, cols = w.shape[-2:]
    tr = 128

    n_slots = recv.shape[0]

    def body(r_ref, w_ref, m_ref, v_ref, g_ref, d_ref, nm_ref, nv_ref):
        g = r_ref[0].astype(F32)
        for slot in range(1, n_slots):
            g = g + r_ref[slot].astype(F32)
        g_ref[...] = g
        d_ref[...], nm_ref[...], nv_ref[...] = _adamw(w_ref[...], g, m_ref[...], v_ref[...])

    blk = pl.BlockSpec((None,) * lead + (tr, cols), lambda i: (0,) * lead + (i, 0))
    return pl.pallas_call(
        body, name=name, grid=(rows // tr,),
        in_specs=[pl.BlockSpec((n_slots, tr, cols), lambda i: (0, i, 0)), blk, blk, blk],
        out_specs=[blk] * 4, out_shape=[_sds(w.shape, F32)] * 4,
        compiler_params=_params())(recv, w, m, v)


SLAB_ROWS = 16
SLOT = {"kv_norm_g": (8, 0, D), "norm_b_g": (9, 0, D), "b_forget": (10, 0, 16), "qnorm_a_g": (10, 128, HD),
        "knorm_a_g": (10, 256, HD), "knorm_b_g": (10, 384, HD), "qnorm_b_g": (10, 512, HD), "sinks": (10, 640, 16)}
SMALL = ["norm_a_g", "b_forget", "qnorm_a_g", "knorm_a_g", "kv_norm_g", "knorm_b_g", "norm_b_g", "qnorm_b_g", "sinks"]


LOSS_ROW = 11


def _pack_small(dg_a, dg_kv, dg_b, db_f, dgq_a, dgk_a, dgk_b, dgq_b, dsinks, lsum):
    def fold(ref):
        return ref[:, 0:HD] + ref[:, HD:2 * HD]

    def body(dga_ref, dgkv_ref, dgb_ref, dbf_ref, dgqa_ref, dgka_ref, dgkb_ref, dgqb_ref, dsk_ref, ls_ref, slab_ref):
        slab_ref[...] = jnp.zeros_like(slab_ref)
        for r in range(N_DEV):
            slab_ref[r:r + 1, 0:LANES] = dga_ref[:, LANES * r:LANES * (r + 1)]
        slab_ref[8:9, :] = dgkv_ref[...]
        slab_ref[9:10, :] = dgb_ref[...]
        slab_ref[10:11, 0:LANES] = dbf_ref[...]
        slab_ref[10:11, 128:128 + HD] = fold(dgqa_ref)
        slab_ref[10:11, 256:256 + HD] = fold(dgka_ref)
        slab_ref[10:11, 384:384 + HD] = fold(dgkb_ref)
        slab_ref[10:11, 512:512 + HD] = fold(dgqb_ref)
        slab_ref[10:11, 640:640 + LANES] = dsk_ref[...]
        slab_ref[LOSS_ROW:LOSS_ROW + 1, 0:LANES] = ls_ref[...]

    return pl.pallas_call(body, name="pack_small", out_shape=_sds((SLAB_ROWS, D), F32), compiler_params=_params())(
        dg_a, dg_kv, dg_b, db_f, dgq_a, dgk_a, dgk_b, dgq_b, dsinks, lsum)


def _small_adamw(recv, ws, ms, vs):
    k = len(SMALL)

    def body(*refs):
        r_ref = refs[0]
        w_refs, m_refs, v_refs = refs[1:1 + k], refs[1 + k:1 + 2 * k], refs[1 + 2 * k:1 + 3 * k]
        outs = refs[1 + 3 * k:1 + 7 * k]
        loss_ref, tot = refs[1 + 7 * k], refs[2 + 7 * k]
        g = r_ref[0]
        for dev in range(1, N_DEV):
            g = g + r_ref[dev]
        tot[...] = g
        loss_ref[...] = tot[LOSS_ROW:LOSS_ROW + 1, 0:LANES] * (0.5 / D)
        me = 4 * lax.axis_index("x") + 2 * lax.axis_index("y") + lax.axis_index("c")
        for p, name in enumerate(SMALL):
            if name == "norm_a_g":
                mine = lax.broadcasted_iota(jnp.int32, (N_DEV, LANES), 0) == me
                gp = jnp.sum(jnp.where(mine, tot[0:N_DEV, 0:LANES], 0.0), axis=0, keepdims=True)
            else:
                row, lo, width = SLOT[name]
                gp = tot[row:row + 1, lo:lo + width]
            d, nm, nv = _adamw(w_refs[p][...], gp, m_refs[p][...], v_refs[p][...])
            outs[p][...] = gp
            outs[k + p][...] = d
            outs[2 * k + p][...] = nm
            outs[3 * k + p][...] = nv

    shapes = [_sds(w.shape, F32) for w in ws]
    return pl.pallas_call(body, name="small_adamw", out_shape=shapes * 4 + [_sds((1, LANES), F32)],
                          scratch_shapes=[pltpu.VMEM((SLAB_ROWS, D), F32)],
                          compiler_params=_params())(recv, *ws, *ms, *vs)


def _rope_tables(positions):
    inv_freq = jnp.power(jnp.float32(ROPE_THETA), -jnp.arange(0, ROT, 2, dtype=F32) / ROT)
    ang = positions.astype(F32)[:, None] * inv_freq[None, :]
    cos, sin = jnp.cos(ang), jnp.sin(ang)
    c64 = jnp.concatenate([cos, cos, jnp.ones((S, HD - ROT), F32)], axis=-1)
    s64 = jnp.concatenate([-sin, sin, jnp.zeros((S, HD - ROT), F32)], axis=-1)
    return jnp.tile(c64, (1, 2)), jnp.tile(s64, (1, 2))


def _local_step(x, tgt, positions, g_a, wa, b_forget, gq_a, gk_a, g_kv, gk_b, g_b, gq_b, sinks,
                woa_s, wkv_s, wib_s, wob_s):
    nq = S // TQ
    cos2, sin2 = _rope_tables(positions)
    b_pad = jnp.pad(b_forget, ((0, 0), (0, LANES - N_HEADS)))

    u_a = _rms_fwd(x, g_a, "rms_a_fwd")
    proj = _mm(u_a, wa, "nn", S, 256, D, name="mm_in_a")
    qn, kn, vb = _prep_a(proj, gq_a, gk_a)
    ccol, cbc = _fgate_fwd(proj, b_pad)
    crow = ccol[:, :N_HEADS].T.reshape(N_HEADS, nq, 1, TQ)
    o_a, z_a, lse_a, woa_g, wkv_g, w_in_b, wob_g = _fox_fwd(
        qn, kn, vb, proj, crow, cbc,
        rider=[("gather_rows", woa_s), ("gather_rows", wkv_s), ("gather_cols", wib_s), ("gather_rows", wob_s)])
    w_out_a, w_kv, w_out_b = woa_g.reshape(D, D), wkv_g.reshape(D, 512), wob_g.reshape(D, D)
    h1 = _mm(z_a, w_out_a, "nn", 1024, 512, D, add=x, name="mm_out_a")
    u_kv, u_b = _rms2_fwd(h1, g_kv, g_b)
    kv = _mm(u_kv, w_kv, "nn", 1024, 512, D, name="mm_kv")
    pb = _mm(u_b, w_in_b, "nn", 1024, 512, D, name="mm_in_b")
    qb = _prep_b(pb, gq_b, cos2, sin2)
    ksh, vsh = _prep_kv(kv, gk_b, cos2, sin2)
    sinks1 = sinks.reshape(N_HEADS)
    o_b, z_b, lse_b = _swa_fwd(qb, ksh, vsh, pb, sinks1)
    y = _mm(z_b, w_out_b, "nn", 1024, 512, D, add=h1, name="mm_out_b")
    dy, lsum = _loss_dy(y, tgt)
    dw_out_b = _mm(z_b, dy, "tn", 512, 512, S, out_dtype=BF16, name="mm_dw_out_b")
    dz_b = _mm(dy, w_out_b, "nt", 1024, 512, D, name="mm_dz_b")
    dq_b, dgate_b, dka, dkb, dva, dvb, dsinks = _swa_bwd(qb, ksh, vsh, dz_b, o_b, lse_b, pb, sinks1)
    dpb, dgq_b = _prep_b_bwd(dq_b, dgate_b, pb, gq_b, cos2, sin2)
    dkv, dgk_b = _prep_kv_bwd(dka, dkb, dva, dvb, kv, gk_b, cos2, sin2)
    dw_in_b = _mm(u_b, dpb, "tn", 512, 512, S, out_dtype=BF16, name="mm_dw_in_b")
    du_b = _mm(dpb, w_in_b, "nt", 1024, 512, 2048, name="mm_du_b")
    dw_kv = _mm(u_kv, dkv, "tn", 512, 512, S, out_dtype=BF16, name="mm_dw_kv")
    du_kv = _mm(dkv, w_kv, "nt", 1024, 512, 512, name="mm_du_kv")
    dh1, dg_b, dg_kv = _rms2_bwd(du_b, du_kv, h1, g_b, g_kv, dy)
    dw_out_a = _mm(z_a, dh1, "tn", 512, 512, S, out_dtype=BF16, name="mm_dw_out_a")
    dz_a = _mm(dh1, w_out_a, "nt", 1024, 512, D, name="mm_dz_a")
    dq_a, delta_a, do_a, dgate_a, drow, r_wob, r_wib, r_wkv, r_woa = _fox_bwd_dq(
        qn, kn, vb, dz_a, proj, o_a, lse_a, crow, cbc,
        rider=[("a2a_rows", dw_out_b), ("a2a_cols", dw_in_b), ("a2a_rows", dw_kv), ("a2a_rows", dw_out_a)])
    dk_a, dv_a, dcs = _fox_bwd_dkv(qn, kn, vb, do_a, lse_a, delta_a, crow, cbc)
    dproj, dgq_a, dgk_a = _prep_a_bwd(dq_a, dk_a, dv_a, dgate_a, proj, gq_a, gk_a)
    drow_col = jnp.pad(drow.reshape(N_HEADS, S).T, ((0, 0), (0, LANES - N_HEADS)))
    dproj, db_f = _fgate_bwd(dproj, drow_col, dcs, proj, b_pad)
    dwa = _mm(u_a, dproj, "tn", 1024, 256, S, out_dtype=BF16, name="mm_dw_in_a")
    dwa_raw = jnp.concatenate([dwa[:, :FOFF + N_HEADS], dwa[:, GOFF:]], axis=1)
    dwa_slots = dwa_raw.reshape(D, N_DEV, NA_RAW // N_DEV).transpose(1, 0, 2)
    du_a, r_wa = _mm(dproj, wa, "nt", 1024, 512, NA // 2, name="mm_du_a",
                     rider=[("a2a_chips", _pair_reduce(dwa_slots))])
    dx, dg_a = _rms_bwd(du_a, x, g_a, dh1)
    slab = _pack_small(dg_a, dg_kv, dg_b, db_f, dgq_a, dgk_a, dgk_b, dgq_b, dsinks, lsum)
    return dx, r_wa, r_woa, r_wkv, r_wib, r_wob, _gather_slab(slab)


def kernel(x, positions, norm_a_g, w_in_a, b_forget, qnorm_a_g, knorm_a_g, w_out_a, kv_norm_g, w_kv, knorm_b_g, norm_b_g, w_in_b, qnorm_b_g, sinks, w_out_b, loss_target, m_norm_a_g, m_w_in_a, m_b_forget, m_qnorm_a_g, m_knorm_a_g, m_w_out_a, m_kv_norm_g, m_w_kv, m_knorm_b_g, m_norm_b_g, m_w_in_b, m_qnorm_b_g, m_sinks, m_w_out_b, v_norm_a_g, v_w_in_a, v_b_forget, v_qnorm_a_g, v_knorm_a_g, v_w_out_a, v_kv_norm_g, v_w_kv, v_knorm_b_g, v_norm_b_g, v_w_in_b, v_qnorm_b_g, v_sinks, v_w_out_b):
    wa_g, ga_g, woa_s, wkv_s, wib_s, wob_s = _gather_first(w_in_a, w_out_a, w_kv, w_in_b, w_out_b, norm_a_g)
    wa_raw = wa_g.transpose(1, 0, 2).reshape(D, NA_RAW)
    wa = jnp.concatenate([wa_raw[:, :FOFF + N_HEADS], jnp.zeros((D, GOFF - FOFF - N_HEADS), BF16),
                          wa_raw[:, FOFF + N_HEADS:]], axis=1)

    dx, r_wa, r_woa, r_wkv, r_wib, r_wob, slab_g = _local_step(
        x[0], loss_target[0], positions, ga_g.reshape(1, D), wa, b_forget, qnorm_a_g, knorm_a_g,
        kv_norm_g.reshape(1, D), knorm_b_g.reshape(1, HD), norm_b_g, qnorm_b_g, sinks, woa_s, wkv_s, wib_s, wob_s)

    big = {}
    for name, recv, w, m, v in (
            ("w_in_a", r_wa, w_in_a, m_w_in_a, v_w_in_a), ("w_out_a", r_woa, w_out_a, m_w_out_a, v_w_out_a),
            ("w_kv", r_wkv, w_kv, m_w_kv, v_w_kv), ("w_in_b", r_wib, w_in_b, m_w_in_b, v_w_in_b),
            ("w_out_b", r_wob, w_out_b, m_w_out_b, v_w_out_b)):
        big[name] = _sum_adamw(recv, w, m, v, "adamw_" + name)

    r2 = lambda a: a.reshape(1, -1)
    small_w = dict(norm_a_g=norm_a_g, b_forget=b_forget, qnorm_a_g=qnorm_a_g, knorm_a_g=knorm_a_g,
                   kv_norm_g=kv_norm_g, knorm_b_g=knorm_b_g, norm_b_g=norm_b_g, qnorm_b_g=qnorm_b_g, sinks=sinks)
    small_m = dict(norm_a_g=m_norm_a_g, b_forget=m_b_forget, qnorm_a_g=m_qnorm_a_g, knorm_a_g=m_knorm_a_g,
                   kv_norm_g=m_kv_norm_g, knorm_b_g=m_knorm_b_g, norm_b_g=m_norm_b_g, qnorm_b_g=m_qnorm_b_g,
                   sinks=m_sinks)
    small_v = dict(norm_a_g=v_norm_a_g, b_forget=v_b_forget, qnorm_a_g=v_qnorm_a_g, knorm_a_g=v_knorm_a_g,
                   kv_norm_g=v_kv_norm_g, knorm_b_g=v_knorm_b_g, norm_b_g=v_norm_b_g, qnorm_b_g=v_qnorm_b_g,
                   sinks=v_sinks)
    res = _small_adamw(slab_g, [r2(small_w[n]) for n in SMALL], [r2(small_m[n]) for n in SMALL],
                       [r2(small_v[n]) for n in SMALL])
    k = len(SMALL)
    small = {n: [res[q * k + p].reshape(small_w[n].shape) for q in range(4)] for p, n in enumerate(SMALL)}
    loss = res[4 * k][0, 0]

    order = ["norm_a_g", "w_in_a", "b_forget", "qnorm_a_g", "knorm_a_g", "w_out_a", "kv_norm_g", "w_kv",
             "knorm_b_g", "norm_b_g", "w_in_b", "qnorm_b_g", "sinks", "w_out_b"]

    def leaf(n, q):
        return big[n][q] if n in big else small[n][q]

    outs = [loss, dx[None]]
    for q in range(4):
        outs.extend(leaf(n, q) for n in order)
    return tuple(outs)
```

```python
import jax
import jax.numpy as jnp
from jax import lax
from jax.experimental import pallas as pl
from jax.experimental.pallas import tpu as pltpu

F32, BF16 = jnp.float32, jnp.bfloat16

S = 2048
D = 1024
HD = 64
N_HEADS = 16
N_DEV = 8
NA = 4352
FOFF = 3072
GOFF = 3328
NA_RAW = 4112
EPS = 1e-6
QSCALE = 0.125
ROPE_THETA = 500000.0
ROT = 16
WIN = 128
TQ = 256
TK = 256
KS = TQ // 2
HPS = 4
HW = HPS * HD
TM = 256
RB = 512
CB = 256
LANES = 128

ADAM_LR, ADAM_B1, ADAM_B2, ADAM_EPS, ADAM_WD, ADAM_STEP = 0.001, 0.9, 0.999, 1e-08, 0.01, 10

VMEM_LIMIT = 48 * 1024 * 1024


def _params():
    return pltpu.CompilerParams(vmem_limit_bytes=VMEM_LIMIT)


def _sds(shape, dtype):
    return jax.ShapeDtypeStruct(shape, dtype)


def _dot_nt(a, b):
    return lax.dot_general(a, b, (((1,), (1,)), ((), ())), preferred_element_type=F32)


def _dot_tn(a, b):
    return lax.dot_general(a, b, (((0,), (0,)), ((), ())), preferred_element_type=F32)


def _dot_nn(a, b):
    return lax.dot_general(a, b, (((1,), (0,)), ((), ())), preferred_element_type=F32)


def _sigmoid(g):
    return 1.0 / (1.0 + jnp.exp(-g))


def _lane_iota(shape):
    return lax.broadcasted_iota(jnp.int32, shape, len(shape) - 1)


def _flips(kind):
    return (2, 4, 6) if kind == "a2a_chips" else tuple(range(1, N_DEV))


def _send_view(kind, ref, dev):
    if kind in ("gather_rows", "gather_cols"):
        return ref
    if kind == "a2a_slots":
        return ref.at[dev]
    if kind == "a2a_chips":
        return ref.at[dev >> 1]
    if kind == "a2a_rows":
        rows = ref.shape[0] // N_DEV
        return ref.at[pl.ds(pl.multiple_of(dev * rows, rows), rows)]
    cols = ref.shape[1] // N_DEV
    return ref.at[:, pl.ds(pl.multiple_of(dev * cols, cols), cols)]


def _land_view(kind, ref, dev):
    if kind == "gather_cols":
        cols = ref.shape[1] // N_DEV
        return ref.at[:, pl.ds(pl.multiple_of(dev * cols, cols), cols)]
    if kind == "a2a_chips":
        return ref.at[dev >> 1]
    return ref.at[dev]


def _landing_sds(kind, arr):
    if kind == "gather_rows":
        return _sds((N_DEV,) + arr.shape, arr.dtype)
    if kind == "gather_cols":
        return _sds((arr.shape[0], N_DEV * arr.shape[1]), arr.dtype)
    if kind == "a2a_rows":
        return _sds((N_DEV, arr.shape[0] // N_DEV, arr.shape[1]), arr.dtype)
    if kind == "a2a_cols":
        return _sds((N_DEV, arr.shape[0], arr.shape[1] // N_DEV), arr.dtype)
    return _sds(arr.shape, arr.dtype)


def _exchange_sems(n_parts):
    n = n_parts * (N_DEV - 1)
    return [pltpu.SemaphoreType.DMA((n,)), pltpu.SemaphoreType.DMA((n,)), pltpu.SemaphoreType.DMA((n_parts,))]


def _exchange_ops(kinds, srcs, dsts, sems, start, wait):
    send_sems, recv_sems, local_sems = sems
    x, y, c = lax.axis_index("x"), lax.axis_index("y"), lax.axis_index("c")
    me = 4 * x + 2 * y + c

    def local(a):
        return pltpu.make_async_copy(_send_view(kinds[a], srcs[a], me), _land_view(kinds[a], dsts[a], me),
                                     local_sems.at[a])

    def remote(a, k, landing_dev):
        peer = (x ^ ((k >> 2) & 1), y ^ ((k >> 1) & 1), c ^ (k & 1))
        sem = a * (N_DEV - 1) + k - 1
        return pltpu.make_async_remote_copy(
            src_ref=_send_view(kinds[a], srcs[a], me ^ k), dst_ref=_land_view(kinds[a], dsts[a], landing_dev),
            send_sem=send_sems.at[sem], recv_sem=recv_sems.at[sem], device_id=peer,
            device_id_type=pl.DeviceIdType.MESH)

    pairs = [(a, k) for k in range(1, N_DEV) for a in range(len(kinds)) if k in _flips(kinds[a])]
    if start:
        for a in range(len(kinds)):
            local(a).start()
        for a, k in pairs:
            remote(a, k, me).start()
    if wait:
        for a, k in pairs:
            remote(a, k, me ^ k).wait_recv()
            remote(a, k, me).wait_send()
        for a in range(len(kinds)):
            local(a).wait()


def _gather_two_level(srcs, dsts, sems):
    send_sems, recv_sems, local_sems = sems
    x, y, c = lax.axis_index("x"), lax.axis_index("y"), lax.axis_index("c")
    me, sibling = (x, y, c), (x, y, 1 - c)
    chips = [(1 - x, y), (x, 1 - y), (1 - x, 1 - y)]

    def slot(ref, dev):
        return ref.at[4 * dev[0] + 2 * dev[1] + dev[2]]

    def copy(a, k, block, to, src=None):
        return pltpu.make_async_remote_copy(
            src_ref=slot(dsts[a], block) if src is None else src, dst_ref=slot(dsts[a], block),
            send_sem=send_sems.at[a * (N_DEV - 1) + k], recv_sem=recv_sems.at[a * (N_DEV - 1) + k],
            device_id=to, device_id_type=pl.DeviceIdType.MESH)

    parts = range(len(srcs))
    mine = [pltpu.make_async_copy(srcs[a], slot(dsts[a], me), local_sems.at[a]) for a in parts]
    first = [copy(a, 0, me, sibling, src=srcs[a]) for a in parts]
    first += [copy(a, 1 + j, me, (*chip, c), src=srcs[a]) for j, chip in enumerate(chips) for a in parts]
    for cp in mine + first:
        cp.start()
    passed = []
    for j, chip in enumerate(chips):
        for a in parts:
            copy(a, 1 + j, (*chip, c), me).wait_recv()
            fwd = copy(a, 4 + j, (*chip, c), sibling)
            fwd.start()
            passed.append(fwd)
    for a in parts:
        copy(a, 0, sibling, me).wait_recv()
        for j, chip in enumerate(chips):
            copy(a, 4 + j, (*chip, 1 - c), me).wait_recv()
    for cp in first + passed:
        cp.wait_send()
    for cp in mine:
        cp.wait()


def _call(body, *, name, args, in_specs, out_specs, out_shape, grid=(), scratch_shapes=(), aliases=None, rider=()):
    n_in, n_out, n_scr, n_r = len(in_specs), len(out_specs), len(scratch_shapes), len(rider)
    kinds = [kind for kind, _ in rider]

    def kernel_body(*refs):
        c_in, r_in = refs[:n_in], refs[n_in:n_in + n_r]
        c_out = refs[n_in + n_r:n_in + n_r + n_out]
        r_out = refs[n_in + n_r + n_out:n_in + 2 * n_r + n_out]
        rest = refs[n_in + 2 * n_r + n_out:]
        c_scr, sems = rest[:n_scr], rest[n_scr:]
        if n_r:
            assert grid, "a rider needs a gridded call"
            ids = [pl.program_id(ax) for ax in range(len(grid))]
            first, last = ids[0] == 0, ids[0] == grid[0] - 1
            for pid, size in zip(ids[1:], grid[1:]):
                first = first & (pid == 0)
                last = last & (pid == size - 1)
            pl.when(first)(lambda: _exchange_ops(kinds, r_in, r_out, sems, True, False))
        body(*c_in, *c_out, *c_scr)
        if n_r:
            pl.when(last)(lambda: _exchange_ops(kinds, r_in, r_out, sems, False, True))

    anyspec = pl.BlockSpec(memory_space=pl.ANY)
    params = pltpu.CompilerParams(vmem_limit_bytes=VMEM_LIMIT, has_side_effects=bool(n_r))
    outs = pl.pallas_call(
        kernel_body, name=name, grid=grid, in_specs=list(in_specs) + [anyspec] * n_r,
        out_specs=list(out_specs) + [anyspec] * n_r,
        out_shape=list(out_shape) + [_landing_sds(kind, arr) for kind, arr in rider],
        scratch_shapes=list(scratch_shapes) + (_exchange_sems(n_r) if n_r else []),
        input_output_aliases=aliases or {}, compiler_params=params)(*args, *[arr for _, arr in rider])
    return list(outs)


def _mm(a, b, mode, tm, tn, tk, out_dtype=F32, add=None, name="mm", rider=()):
    if mode == "nn":
        (m, k), n = a.shape, b.shape[1]
        a_spec = pl.BlockSpec((tm, tk), lambda i, j, kk: (i, kk))
        b_spec = pl.BlockSpec((tk, tn), lambda i, j, kk: (kk, j))
        dot = _dot_nn
    elif mode == "nt":
        (m, k), n = a.shape, b.shape[0]
        a_spec = pl.BlockSpec((tm, tk), lambda i, j, kk: (i, kk))
        b_spec = pl.BlockSpec((tn, tk), lambda i, j, kk: (j, kk))
        dot = _dot_nt
    else:
        (k, m), n = a.shape, b.shape[1]
        a_spec = pl.BlockSpec((tk, tm), lambda i, j, kk: (kk, i))
        b_spec = pl.BlockSpec((tk, tn), lambda i, j, kk: (kk, j))
        dot = _dot_tn
    assert m % tm == 0 and n % tn == 0 and k % tk == 0, (m, n, k, tm, tn, tk)
    nk = k // tk
    has_add = add is not None

    def body(*refs):
        if has_add:
            a_ref, b_ref, add_ref, o_ref, acc = refs
        else:
            a_ref, b_ref, o_ref, acc = refs
        p = dot(a_ref[...].astype(BF16), b_ref[...].astype(BF16))

        def finish(total):
            if has_add:
                total = add_ref[...] + total
            o_ref[...] = total.astype(out_dtype)

        if nk == 1:
            finish(p)
        else:
            kk = pl.program_id(2)

            @pl.when(kk == 0)
            def _():
                acc[...] = p

            @pl.when(kk > 0)
            def _():
                acc[...] += p

            @pl.when(kk == nk - 1)
            def _():
                finish(acc[...])

    in_specs = [a_spec, b_spec]
    args = [a, b]
    if has_add:
        in_specs.append(pl.BlockSpec((tm, tn), lambda i, j, kk: (i, j)))
        args.append(add)
    acc_shape = (tm, tn) if nk > 1 else (8, LANES)
    outs = _call(body, name=name, args=args, grid=(m // tm, n // tn, nk), in_specs=in_specs,
                 out_specs=[pl.BlockSpec((tm, tn), lambda i, j, kk: (i, j))], out_shape=[_sds((m, n), out_dtype)],
                 scratch_shapes=[pltpu.VMEM(acc_shape, F32)], rider=rider)
    return outs if rider else outs[0]


def _rms_rinv(x):
    return lax.rsqrt(jnp.mean(x * x, axis=-1, keepdims=True) + EPS)


def _rms_bwd_core(du, x, g):
    r = _rms_rinv(x)
    dug = du * g
    dx = r * (dug - x * ((r * r) * jnp.mean(dug * x, axis=-1, keepdims=True)))
    dg = jnp.sum(du * (x * r), axis=0, keepdims=True)
    return dx, dg


def _half_sum(v, lo_half):
    s0 = jnp.sum(jnp.where(lo_half, v, 0.0), axis=-1, keepdims=True)
    s1 = jnp.sum(jnp.where(lo_half, 0.0, v), axis=-1, keepdims=True)
    return jnp.where(lo_half, s0, s1)


def _head_rinv(x, lo_half):
    return lax.rsqrt(_half_sum(x * x, lo_half) * (1.0 / HD) + EPS)


def _head_norm_bwd(dn, x, g, lo_half):
    r = _head_rinv(x, lo_half)
    dng = dn * g
    dx = r * (dng - x * ((r * r) * (_half_sum(dng * x, lo_half) * (1.0 / HD))))
    dg = jnp.sum(dn * (x * r), axis=0, keepdims=True)
    return dx, dg


def _rope_swap(x, lane):
    l64 = lane & (HD - 1)
    return jnp.where(l64 < ROT // 2, pltpu.roll(x, LANES - ROT // 2, 1), pltpu.roll(x, ROT // 2, 1))


def _rope_fwd(x, cos, sin, lane):
    return x * cos + _rope_swap(x, lane) * sin


def _rope_bwd(dy, cos, sin, lane):
    return dy * cos + jnp.where((lane & (HD - 1)) < ROT, _rope_swap(dy * sin, lane), 0.0)


def _g2(g_ref):
    g = g_ref[...]
    return jnp.concatenate([g, g], axis=-1)


def _pairs(width):
    return [slice(LANES * c, LANES * (c + 1)) for c in range(width // LANES)]


def _rms_fwd(x, g, name):
    def body(x_ref, g_ref, u_ref):
        xv = x_ref[...]
        u_ref[...] = ((xv * _rms_rinv(xv)) * g_ref[...]).astype(BF16)

    return pl.pallas_call(
        body, name=name, grid=(S // TM,),
        in_specs=[pl.BlockSpec((TM, D), lambda i: (i, 0)), pl.BlockSpec((1, D), lambda i: (0, 0))],
        out_specs=pl.BlockSpec((TM, D), lambda i: (i, 0)), out_shape=_sds((S, D), BF16),
        compiler_params=_params())(x, g)


def _rms2_fwd(h, g1, g2):
    def body(x_ref, g1_ref, g2_ref, u1_ref, u2_ref):
        xv = x_ref[...]
        xn = xv * _rms_rinv(xv)
        u1_ref[...] = (xn * g1_ref[...]).astype(BF16)
        u2_ref[...] = (xn * g2_ref[...]).astype(BF16)

    row = pl.BlockSpec((TM, D), lambda i: (i, 0))
    vec = pl.BlockSpec((1, D), lambda i: (0, 0))
    return pl.pallas_call(
        body, name="rms2_fwd", grid=(S // TM,), in_specs=[row, vec, vec], out_specs=[row, row],
        out_shape=[_sds((S, D), BF16)] * 2, compiler_params=_params())(h, g1, g2)


def _prep_a(proj, gq, gk):
    def body(q_ref, k_ref, v_ref, gq_ref, gk_ref, qo_ref, ko_ref, vo_ref):
        lo_half = _lane_iota((RB, LANES)) < HD
        gq2, gk2 = _g2(gq_ref), _g2(gk_ref)
        for c in _pairs(CB):
            q = q_ref[:, c]
            k = k_ref[:, c]
            qo_ref[:, c] = (((q * _head_rinv(q, lo_half)) * gq2) * QSCALE).astype(BF16)
            ko_ref[:, c] = ((k * _head_rinv(k, lo_half)) * gk2).astype(BF16)
        vo_ref[...] = v_ref[...].astype(BF16)

    nc = D // CB
    blk = lambda off: pl.BlockSpec((RB, CB), lambda i, j: (i, j + off))
    gspec = pl.BlockSpec((1, HD), lambda i, j: (0, 0))
    return pl.pallas_call(
        body, name="prep_a", grid=(S // RB, nc),
        in_specs=[blk(0), blk(nc), blk(2 * nc), gspec, gspec], out_specs=[blk(0)] * 3,
        out_shape=[_sds((S, D), BF16)] * 3, compiler_params=_params())(proj, proj, proj, gq, gk)


def _pick_lane(block, lane, idx):
    return jnp.sum(jnp.where(lane == idx, block, 0.0), axis=-1, keepdims=True)


def _fgate_fwd(proj, b_pad):
    def body(f_ref, b_ref, c_ref, cbc_ref, carry):
        @pl.when(pl.program_id(0) == 0)
        def _():
            carry[...] = jnp.zeros_like(carry)

        z = f_ref[...] + b_ref[...]
        logf = jnp.minimum(z, 0.0) - jnp.log1p(jnp.exp(-jnp.abs(z)))
        r = lax.broadcasted_iota(jnp.int32, (TM, TM), 0)
        c = lax.broadcasted_iota(jnp.int32, (TM, TM), 1)
        tri = (r >= c).astype(F32)
        loc = jnp.dot(tri, logf, precision=lax.Precision.HIGHEST, preferred_element_type=F32) + carry[0:1, :]
        c_ref[...] = loc
        carry[0:1, :] = loc[TM - 1:TM, :]
        lane = _lane_iota((TM, LANES))
        for h in range(N_HEADS):
            cbc_ref[:, LANES * h:LANES * (h + 1)] = jnp.broadcast_to(_pick_lane(loc, lane, h), (TM, LANES))

    return pl.pallas_call(
        body, name="fgate_fwd", grid=(S // TM,),
        in_specs=[pl.BlockSpec((TM, LANES), lambda i: (i, FOFF // LANES)), pl.BlockSpec((1, LANES), lambda i: (0, 0))],
        out_specs=[pl.BlockSpec((TM, LANES), lambda i: (i, 0)), pl.BlockSpec((TM, N_HEADS * LANES), lambda i: (i, 0))],
        out_shape=[_sds((S, LANES), F32), _sds((S, N_HEADS * LANES), F32)],
        scratch_shapes=[pltpu.VMEM((8, LANES), F32)], compiler_params=_params())(proj, b_pad)


def _fgate_bwd(dproj, drow, dcs, proj, b_pad):
    nt = S // TM

    def body(dp_ref, dr_ref, dcs_ref, f_ref, b_ref, o_ref, db_ref, carry):
        @pl.when(pl.program_id(0) == 0)
        def _():
            carry[...] = jnp.zeros_like(carry)
            db_ref[...] = jnp.zeros_like(db_ref)

        lane = _lane_iota((TM, LANES))
        dc = dr_ref[...]
        for h in range(N_HEADS):
            dc = dc + jnp.where(lane == h, dcs_ref[:, LANES * h:LANES * h + 1], 0.0)
        r = lax.broadcasted_iota(jnp.int32, (TM, TM), 0)
        c = lax.broadcasted_iota(jnp.int32, (TM, TM), 1)
        tri = (c >= r).astype(F32)
        dlogf = jnp.dot(tri, dc, precision=lax.Precision.HIGHEST, preferred_element_type=F32) + carry[0:1, :]
        carry[0:1, :] = dlogf[0:1, :]
        z = f_ref[...] + b_ref[...]
        df = dlogf * (1.0 / (1.0 + jnp.exp(z)))
        db_ref[...] += jnp.sum(df, axis=0, keepdims=True)
        o_ref[...] = jnp.concatenate([df, jnp.zeros_like(df)], axis=-1).astype(BF16)

    return pl.pallas_call(
        body, name="fgate_bwd", grid=(nt,),
        in_specs=[pl.BlockSpec(memory_space=pl.ANY),
                  pl.BlockSpec((TM, LANES), lambda i: (nt - 1 - i, 0)),
                  pl.BlockSpec((TM, N_HEADS * LANES), lambda i: (nt - 1 - i, 0)),
                  pl.BlockSpec((TM, LANES), lambda i: (nt - 1 - i, FOFF // LANES)),
                  pl.BlockSpec((1, LANES), lambda i: (0, 0))],
        out_specs=[pl.BlockSpec((TM, 2 * LANES), lambda i: (nt - 1 - i, FOFF // (2 * LANES))),
                   pl.BlockSpec((1, LANES), lambda i: (0, 0))],
        out_shape=[_sds((S, NA), BF16), _sds((1, LANES), F32)],
        scratch_shapes=[pltpu.VMEM((8, LANES), F32)], input_output_aliases={0: 0},
        compiler_params=_params())(dproj, drow, dcs, proj, b_pad)


def _key_le_query(offset, keys=TK):
    r = lax.broadcasted_iota(jnp.int32, (keys, TQ), 0)
    c = lax.broadcasted_iota(jnp.int32, (keys, TQ), 1)
    return (r + offset) <= c


def _widen(tile):
    return jnp.concatenate([tile] * (TQ // LANES), axis=1)


def _fox_fwd(qn, kn, vb, proj, crow, cbc, rider=()):
    nq, per = S // TQ, TQ // TK

    def body(q_ref, k_ref, v_ref, g_ref, cq_ref, cbc_ref, o_ref, z_ref, lse_ref, st_s, pt_s):
        i = pl.program_id(1)
        qs = [q_ref[:, HD * hh:HD * (hh + 1)] for hh in range(HPS)]
        cqs = [cq_ref[hh, 0] for hh in range(HPS)]

        def scores(s, hh):
            off = pl.multiple_of(s * KS, KS)
            kj = k_ref[pl.ds(off, KS), HD * hh:HD * (hh + 1)]
            return (_dot_nt(kj, qs[hh]) + cqs[hh]) - _widen(cbc_ref[pl.ds(off, KS), LANES * hh:LANES * (hh + 1)])

        def values(s, hh, pt):
            off = pl.multiple_of(s * KS, KS)
            return _dot_tn(v_ref[pl.ds(off, KS), HD * hh:HD * (hh + 1)], pt)

        def step(s, slot, carries, mask=None, last=False):
            if not last:
                for hh in range(HPS):
                    st_s[1 - slot, hh] = scores(s + 1, hh)
            pvs = [values(jnp.maximum(s - 1, 0), hh, pt_s[1 - slot, hh]) for hh in range(HPS)]
            out = []
            for hh in range(HPS):
                m, l, acc = carries[hh]
                st = st_s[slot, hh]
                if mask is not None:
                    st = jnp.where(mask, st, -jnp.inf)
                m_new = jnp.maximum(m, jnp.max(st, axis=0, keepdims=True))
                pt = jnp.exp(st - m_new)
                alpha = jnp.exp(m - m_new)
                pt_s[slot, hh] = pt.astype(BF16)
                out.append((m_new, alpha * l + jnp.sum(pt, axis=0, keepdims=True), alpha * (acc + pvs[hh])))
            return tuple(out)

        for hh in range(HPS):
            st_s[0, hh] = scores(0, hh)
            pt_s[1, hh] = jnp.zeros((KS, TQ), BF16)
        one = (jnp.full((1, TQ), -jnp.inf, F32), jnp.zeros((1, TQ), F32), jnp.zeros((HD, TQ), F32))
        carries = lax.fori_loop(0, i, lambda t, cr: step(2 * t + 1, 1, step(2 * t, 0, cr)), (one,) * HPS)
        carries = step(2 * i, 0, carries, mask=_key_le_query(0, KS))
        carries = step(2 * i + 1, 1, carries, mask=_key_le_query(KS, KS), last=True)
        accs = []
        for hh in range(HPS):
            m, l, acc = carries[hh]
            acc = acc + values(2 * i + 1, hh, pt_s[1, hh])
            accs.append(acc / l)
            lse_ref[hh, 0] = m + jnp.log(l)
        o = jnp.concatenate(accs, axis=0).T
        o_ref[...] = o
        g = g_ref[...]
        z_ref[...] = (o * (g * _sigmoid(g))).astype(BF16)

    qblk = pl.BlockSpec((TQ, HW), lambda hp, i: (i, hp))
    full = pl.BlockSpec((S, HW), lambda hp, i: (0, hp))
    rows = pl.BlockSpec((HPS, 1, 1, TQ), lambda hp, i: (hp, i, 0, 0))
    return _call(
        body, name="fox_fwd", args=(qn, kn, vb, proj, crow, cbc), grid=(N_HEADS // HPS, nq),
        in_specs=[qblk, full, full,
                  pl.BlockSpec((TQ, HW), lambda hp, i: (i, GOFF // HW + hp)),
                  rows, pl.BlockSpec((S, HPS * LANES), lambda hp, i: (0, hp))],
        out_specs=[qblk, qblk, rows],
        out_shape=[_sds((S, D), F32), _sds((S, D), BF16), _sds((N_HEADS, nq, 1, TQ), F32)],
        scratch_shapes=[pltpu.VMEM((2, HPS, KS, TQ), F32), pltpu.VMEM((2, HPS, KS, TQ), BF16)], rider=rider)


def _fox_bwd_dq(qn, kn, vb, dz, proj, o, lse, crow, cbc, rider=()):
    nq, per = S // TQ, TQ // TK

    def body(q_ref, k_ref, v_ref, dz_ref, g_ref, o_ref, lse_ref, cq_ref, cbc_ref,
             dq_ref, delta_ref, do_ref, dg_ref, dr_ref, st_s, dp_s, ds_s):
        i = pl.program_id(1)
        g = g_ref[...]
        sg = _sigmoid(g)
        dzv = dz_ref[...]
        ov = o_ref[...]
        do = dzv * (g * sg)
        dg_ref[...] = (dzv * ov * (sg * (1.0 + g * (1.0 - sg)))).astype(BF16)
        do_ref[...] = do.astype(BF16)
        prod_t = (do * ov).T
        qs = [q_ref[:, HD * hh:HD * (hh + 1)] for hh in range(HPS)]
        dobs = [do[:, HD * hh:HD * (hh + 1)].astype(BF16) for hh in range(HPS)]
        lses = [lse_ref[hh, 0] for hh in range(HPS)]
        cqs = [cq_ref[hh, 0] for hh in range(HPS)]
        deltas = [jnp.sum(prod_t[HD * hh:HD * (hh + 1), :], axis=0, keepdims=True) for hh in range(HPS)]
        for hh in range(HPS):
            delta_ref[hh, 0] = deltas[hh]

        def products(s, hh):
            off = pl.multiple_of(s * KS, KS)
            kj = k_ref[pl.ds(off, KS), HD * hh:HD * (hh + 1)]
            vj = v_ref[pl.ds(off, KS), HD * hh:HD * (hh + 1)]
            st = (_dot_nt(kj, qs[hh]) + cqs[hh]) - _widen(cbc_ref[pl.ds(off, KS), LANES * hh:LANES * (hh + 1)])
            return st, _dot_nt(vj, dobs[hh])

        def dq_of(s, hh, dst):
            off = pl.multiple_of(s * KS, KS)
            return _dot_tn(k_ref[pl.ds(off, KS), HD * hh:HD * (hh + 1)], dst)

        def step(s, slot, carries, mask=None, last=False):
            if not last:
                for hh in range(HPS):
                    st_s[1 - slot, hh], dp_s[1 - slot, hh] = products(s + 1, hh)
            dqs = [dq_of(jnp.maximum(s - 1, 0), hh, ds_s[1 - slot, hh]) for hh in range(HPS)]
            out = []
            for hh in range(HPS):
                dqt, dr = carries[hh]
                st = st_s[slot, hh]
                if mask is not None:
                    st = jnp.where(mask, st, -jnp.inf)
                dst = jnp.exp(st - lses[hh]) * (dp_s[slot, hh] - deltas[hh])
                ds_s[slot, hh] = dst.astype(BF16)
                out.append((dqt + dqs[hh], dr + jnp.sum(dst, axis=0, keepdims=True)))
            return tuple(out)

        for hh in range(HPS):
            st_s[0, hh], dp_s[0, hh] = products(0, hh)
            ds_s[1, hh] = jnp.zeros((KS, TQ), BF16)
        one = (jnp.zeros((HD, TQ), F32), jnp.zeros((1, TQ), F32))
        carries = lax.fori_loop(0, i, lambda t, cr: step(2 * t + 1, 1, step(2 * t, 0, cr)), (one,) * HPS)
        carries = step(2 * i, 0, carries, mask=_key_le_query(0, KS))
        carries = step(2 * i + 1, 1, carries, mask=_key_le_query(KS, KS), last=True)
        for hh in range(HPS):
            dr_ref[hh, 0] = carries[hh][1]
        dq_ref[...] = jnp.concatenate(
            [carries[hh][0] + dq_of(2 * i + 1, hh, ds_s[1, hh]) for hh in range(HPS)], axis=0).T

    qblk = pl.BlockSpec((TQ, HW), lambda hp, i: (i, hp))
    full = pl.BlockSpec((S, HW), lambda hp, i: (0, hp))
    rows = pl.BlockSpec((HPS, 1, 1, TQ), lambda hp, i: (hp, i, 0, 0))
    rows_shape = _sds((N_HEADS, nq, 1, TQ), F32)
    return _call(
        body, name="fox_bwd_dq", args=(qn, kn, vb, dz, proj, o, lse, crow, cbc), grid=(N_HEADS // HPS, nq),
        in_specs=[qblk, full, full, qblk,
                  pl.BlockSpec((TQ, HW), lambda hp, i: (i, GOFF // HW + hp)),
                  qblk, rows, rows, pl.BlockSpec((S, HPS * LANES), lambda hp, i: (0, hp))],
        out_specs=[qblk, rows, qblk, qblk, rows],
        out_shape=[_sds((S, D), F32), rows_shape, _sds((S, D), BF16), _sds((S, D), BF16), rows_shape],
        scratch_shapes=[pltpu.VMEM((2, HPS, KS, TQ), F32), pltpu.VMEM((2, HPS, KS, TQ), F32),
                        pltpu.VMEM((2, HPS, KS, TQ), BF16)], rider=rider)


def _fox_bwd_dkv(qn, kn, vb, dob, lse, delta, crow, cbc):
    nq, per = S // TQ, TQ // TK

    def body(q_ref, k_ref, v_ref, do_ref, lse_ref, del_ref, cq_ref, cbc_ref, dk_ref, dv_ref, dcs_ref,
             st_s, dp_s, pt_s, ds_s):
        j = pl.program_id(1)
        kjs = [k_ref[:, HD * hh:HD * (hh + 1)] for hh in range(HPS)]
        vjs = [v_ref[:, HD * hh:HD * (hh + 1)] for hh in range(HPS)]

        def rows_of(ref, u, hh):
            off = pl.multiple_of(u * TQ, TQ)
            return ref[pl.ds(off, TQ), HD * hh:HD * (hh + 1)]

        def products(u, hh):
            st = (_dot_nt(kjs[hh], rows_of(q_ref, u, hh)) + cq_ref[hh, u]) - _widen(
                cbc_ref[:, LANES * hh:LANES * (hh + 1)])
            return st, _dot_nt(vjs[hh], rows_of(do_ref, u, hh))

        def step(u, slot, carries, masked=False):
            nxt = jnp.minimum(u + 1, nq - 1)
            for hh in range(HPS):
                st_s[1 - slot, hh], dp_s[1 - slot, hh] = products(nxt, hh)
            prev = jnp.maximum(u - 1, 0)
            dvs = [_dot_nn(pt_s[1 - slot, hh], rows_of(do_ref, prev, hh)) for hh in range(HPS)]
            dks = [_dot_nn(ds_s[1 - slot, hh], rows_of(q_ref, prev, hh)) for hh in range(HPS)]
            out = []
            for hh in range(HPS):
                dk, dv, dcs = carries[hh]
                st = st_s[slot, hh]
                if masked:
                    st = jnp.where(_key_le_query((j - u) * TQ), st, -jnp.inf)
                pt = jnp.exp(st - lse_ref[hh, u])
                dst = pt * (dp_s[slot, hh] - del_ref[hh, u])
                pt_s[slot, hh] = pt.astype(BF16)
                ds_s[slot, hh] = dst.astype(BF16)
                out.append((dk + dks[hh], dv + dvs[hh], dcs + (dst[:, :LANES] + dst[:, LANES:])))
            return tuple(out)

        t0 = j // 2
        for hh in range(HPS):
            st_s[0, hh], dp_s[0, hh] = products(2 * t0, hh)
            pt_s[1, hh] = jnp.zeros((TK, TQ), BF16)
            ds_s[1, hh] = jnp.zeros((TK, TQ), BF16)
        one = (jnp.zeros((TK, HD), F32), jnp.zeros((TK, HD), F32), jnp.zeros((TK, LANES), F32))
        carries = step(2 * t0 + 1, 1, step(2 * t0, 0, (one,) * HPS, masked=True), masked=True)
        carries = lax.fori_loop(t0 + 1, nq // 2, lambda t, cr: step(2 * t + 1, 1, step(2 * t, 0, cr)), carries)
        dks, dvs = [], []
        for hh in range(HPS):
            dk, dv, dcs = carries[hh]
            dks.append(dk + _dot_nn(ds_s[1, hh], rows_of(q_ref, nq - 1, hh)))
            dvs.append(dv + _dot_nn(pt_s[1, hh], rows_of(do_ref, nq - 1, hh)))
            dcs_ref[:, LANES * hh:LANES * (hh + 1)] = jnp.broadcast_to(
                -jnp.sum(dcs, axis=-1, keepdims=True), (TK, LANES))
        dk_ref[...] = jnp.concatenate(dks, axis=-1)
        dv_ref[...] = jnp.concatenate(dvs, axis=-1)

    kblk = pl.BlockSpec((TK, HW), lambda hp, j: (j, hp))
    full = pl.BlockSpec((S, HW), lambda hp, j: (0, hp))
    rows = pl.BlockSpec((HPS, nq, 1, TQ), lambda hp, j: (hp, 0, 0, 0))
    cblk = pl.BlockSpec((TK, HPS * LANES), lambda hp, j: (j, hp))
    return pl.pallas_call(
        body, name="fox_bwd_dkv", grid=(N_HEADS // HPS, S // TK),
        in_specs=[full, kblk, kblk, full, rows, rows, rows, cblk],
        out_specs=[kblk, kblk, cblk],
        out_shape=[_sds((S, D), F32), _sds((S, D), F32), _sds((S, N_HEADS * LANES), F32)],
        scratch_shapes=[pltpu.VMEM((2, HPS, TK, TQ), F32), pltpu.VMEM((2, HPS, TK, TQ), F32),
                        pltpu.VMEM((2, HPS, TK, TQ), BF16), pltpu.VMEM((2, HPS, TK, TQ), BF16)],
        compiler_params=_params())(qn, kn, vb, dob, lse, delta, crow, cbc)


def _prep_a_bwd(dq, dk, dv, dgate, proj, gq, gk):
    nc = D // CB

    def body(dq_ref, dk_ref, dv_ref, dgt_ref, x_ref, gq_ref, gk_ref, o_ref, dgq_ref, dgk_ref):
        i, j = pl.program_id(0), pl.program_id(1)

        @pl.when((i == 0) & (j == 0))
        def _():
            dgq_ref[...] = jnp.zeros_like(dgq_ref)
            dgk_ref[...] = jnp.zeros_like(dgk_ref)

        @pl.when(j < 2 * nc)
        def _():
            lo_half = _lane_iota((RB, LANES)) < HD
            is_q = j < nc
            g = jnp.where(is_q, _g2(gq_ref), _g2(gk_ref))
            dg_tot = jnp.zeros((1, LANES), F32)
            for c in _pairs(CB):
                dn = jnp.where(is_q, dq_ref[:, c] * QSCALE, dk_ref[:, c])
                dx, dg = _head_norm_bwd(dn, x_ref[:, c], g, lo_half)
                o_ref[:, c] = dx.astype(BF16)
                dg_tot = dg_tot + dg
            dgq_ref[...] += jnp.where(is_q, dg_tot, 0.0)
            dgk_ref[...] += jnp.where(is_q, 0.0, dg_tot)

        @pl.when((j >= 2 * nc) & (j < 3 * nc))
        def _():
            o_ref[...] = dv_ref[...].astype(BF16)

        @pl.when(j >= 3 * nc)
        def _():
            o_ref[...] = dgt_ref[...]

    clamp = lambda off: (lambda i, j: (i, jnp.clip(j - off, 0, nc - 1)))
    gspec = pl.BlockSpec((1, HD), lambda i, j: (0, 0))
    acc = pl.BlockSpec((1, LANES), lambda i, j: (0, 0))
    gate_shift = (GOFF - 3 * D) // CB
    return pl.pallas_call(
        body, name="prep_a_bwd", grid=(S // RB, 4 * nc),
        in_specs=[pl.BlockSpec((RB, CB), clamp(0)), pl.BlockSpec((RB, CB), clamp(nc)),
                  pl.BlockSpec((RB, CB), clamp(2 * nc)), pl.BlockSpec((RB, CB), clamp(3 * nc)),
                  pl.BlockSpec((RB, CB), lambda i, j: (i, jnp.minimum(j, 2 * nc - 1))), gspec, gspec],
        out_specs=[pl.BlockSpec((RB, CB), lambda i, j: (i, jnp.where(j < 3 * nc, j, j + gate_shift))), acc, acc],
        out_shape=[_sds((S, NA), BF16), _sds((1, LANES), F32), _sds((1, LANES), F32)],
        compiler_params=_params())(dq, dk, dv, dgate, proj, gq, gk)


def _prep_b(pb, gq, cos2, sin2):
    def body(x_ref, g_ref, c_ref, s_ref, o_ref):
        lane = _lane_iota((RB, LANES))
        g2, cos, sin = _g2(g_ref), c_ref[...], s_ref[...]
        for c in _pairs(CB):
            x = x_ref[:, c]
            xn = (x * _head_rinv(x, lane < HD)) * g2
            o_ref[:, c] = (_rope_fwd(xn, cos, sin, lane) * QSCALE).astype(BF16)

    blk = pl.BlockSpec((RB, CB), lambda i, j: (i, j))
    tab = pl.BlockSpec((RB, LANES), lambda i, j: (i, 0))
    return pl.pallas_call(
        body, name="prep_b", grid=(S // RB, D // CB),
        in_specs=[blk, pl.BlockSpec((1, HD), lambda i, j: (0, 0)), tab, tab], out_specs=blk,
        out_shape=_sds((S, D), BF16), compiler_params=_params())(pb, gq, cos2, sin2)


def _prep_kv(kv, gk, cos2, sin2):
    def body(k_ref, v_ref, g_ref, c_ref, s_ref, ko_ref, vo_ref):
        lane = _lane_iota((RB, LANES))
        g2, cos, sin = _g2(g_ref), c_ref[...], s_ref[...]
        for c in _pairs(CB):
            x = k_ref[:, c]
            xn = (x * _head_rinv(x, lane < HD)) * g2
            ko_ref[:, c] = _rope_fwd(xn, cos, sin, lane).astype(BF16)
        vo_ref[...] = v_ref[...].astype(BF16)

    blk = lambda off: pl.BlockSpec((RB, CB), lambda i: (i, off))
    tab = pl.BlockSpec((RB, LANES), lambda i: (i, 0))
    return pl.pallas_call(
        body, name="prep_kv", grid=(S // RB,),
        in_specs=[blk(0), blk(1), pl.BlockSpec((1, HD), lambda i: (0, 0)), tab, tab],
        out_specs=[blk(0), blk(0)], out_shape=[_sds((S, CB), BF16)] * 2,
        compiler_params=_params())(kv, kv, gk, cos2, sin2)


def _swa_mask(n):
    r = lax.broadcasted_iota(jnp.int32, (4 * WIN, 2 * WIN), 0) & (WIN - 1)
    c = lax.broadcasted_iota(jnp.int32, (4 * WIN, 2 * WIN), 1)
    return (c > r) & (c <= r + WIN) & ((c >= WIN) | (n > 0))


def _stack4(ref_or_val, base):
    return jnp.concatenate([ref_or_val[:, base + HD * g: base + HD * (g + 1)] for g in range(4)], axis=0)


def _sink_col(s_ref, first):
    r = lax.broadcasted_iota(jnp.int32, (4 * WIN, 1), 0)
    col = jnp.full((4 * WIN, 1), s_ref[first + 3], F32)
    for g in range(2, -1, -1):
        col = jnp.where(r < WIN * (g + 1), s_ref[first + g], col)
    return col


def _swa_fwd(qb, ksh, vsh, pb, sinks):
    nb = S // WIN

    def body(q_ref, kp_ref, kc_ref, vp_ref, vc_ref, g_ref, s_ref, o_ref, z_ref, lse_ref):
        kp, n = pl.program_id(0), pl.program_id(1)
        valid = _swa_mask(n)
        outs, lses = [], []
        for kh in range(2):
            lo = HD * kh
            kb = jnp.concatenate([kp_ref[:, lo:lo + HD], kc_ref[:, lo:lo + HD]], axis=0)
            vb = jnp.concatenate([vp_ref[:, lo:lo + HD], vc_ref[:, lo:lo + HD]], axis=0)
            qs = _stack4(q_ref, 4 * HD * kh)
            s = jnp.where(valid, _dot_nt(qs, kb), -jnp.inf)
            sink = _sink_col(s_ref, (2 * kp + kh) * 4)
            m = jnp.maximum(jnp.max(s, axis=-1, keepdims=True), sink)
            p = jnp.exp(s - m)
            l = jnp.sum(p, axis=-1, keepdims=True) + jnp.exp(sink - m)
            os_ = _dot_nn(p.astype(BF16), vb) / l
            lse = m + jnp.log(l)
            for g in range(4):
                outs.append(os_[WIN * g:WIN * (g + 1), :])
                lses.append(jnp.broadcast_to(lse[WIN * g:WIN * (g + 1), :], (WIN, HD)))
        o = jnp.concatenate(outs, axis=-1)
        o_ref[...] = o
        g = g_ref[...]
        z_ref[...] = (o * (g * _sigmoid(g))).astype(BF16)
        lse_ref[...] = jnp.concatenate(lses, axis=-1)

    qblk = pl.BlockSpec((WIN, 512), lambda kp, n: (n, kp))
    prev = pl.BlockSpec((WIN, LANES), lambda kp, n: (jnp.maximum(n - 1, 0), kp))
    cur = pl.BlockSpec((WIN, LANES), lambda kp, n: (n, kp))
    return pl.pallas_call(
        body, name="swa_fwd", grid=(2, nb),
        in_specs=[qblk, prev, cur, prev, cur, pl.BlockSpec((WIN, 512), lambda kp, n: (n, 2 + kp)),
                  pl.BlockSpec(memory_space=pltpu.SMEM)],
        out_specs=[qblk, qblk, qblk],
        out_shape=[_sds((S, D), F32), _sds((S, D), BF16), _sds((S, D), F32)],
        compiler_params=_params())(qb, ksh, ksh, vsh, vsh, pb, sinks)


def _swa_bwd(qb, ksh, vsh, dz, o, lse, pb, sinks):
    nb = S // WIN

    def body(q_ref, kp_ref, kc_ref, vp_ref, vc_ref, dz_ref, o_ref, lse_ref, g_ref, s_ref,
             dq_ref, dg_ref, dka_ref, dkb_ref, dva_ref, dvb_ref, dsink_ref):
        kp, n = pl.program_id(0), pl.program_id(1)

        @pl.when((kp == 0) & (n == 0))
        def _():
            dsink_ref[...] = jnp.zeros_like(dsink_ref)

        valid = _swa_mask(n)
        g = g_ref[...]
        sg = _sigmoid(g)
        dzv = dz_ref[...]
        ov = o_ref[...]
        do = dzv * (g * sg)
        dg_ref[...] = (dzv * ov * (sg * (1.0 + g * (1.0 - sg)))).astype(BF16)
        prod = do * ov
        lane1 = _lane_iota((1, LANES))
        dqs, dkas, dkbs, dvas, dvbs = [], [], [], [], []
        dsink = jnp.zeros((1, LANES), F32)
        for kh in range(2):
            lo = HD * kh
            kb = jnp.concatenate([kp_ref[:, lo:lo + HD], kc_ref[:, lo:lo + HD]], axis=0)
            vb = jnp.concatenate([vp_ref[:, lo:lo + HD], vc_ref[:, lo:lo + HD]], axis=0)
            base = 4 * HD * kh
            qs = _stack4(q_ref, base)
            dos = _stack4(do, base).astype(BF16)
            delta = jnp.sum(_stack4(prod, base), axis=-1, keepdims=True)
            lse_s = jnp.concatenate([lse_ref[:, base + HD * gg: base + HD * gg + 1] for gg in range(4)], axis=0)
            s = jnp.where(valid, _dot_nt(qs, kb), -jnp.inf)
            p = jnp.exp(s - lse_s)
            ds = p * (_dot_nt(dos, vb) - delta)
            dsb = ds.astype(BF16)
            dqst = _dot_nn(dsb, kb)
            dkband = _dot_tn(dsb, qs)
            dvband = _dot_tn(p.astype(BF16), dos)
            for gg in range(4):
                dqs.append(dqst[WIN * gg:WIN * (gg + 1), :])
            dkbs.append(dkband[0:WIN, :])
            dkas.append(dkband[WIN:2 * WIN, :])
            dvbs.append(dvband[0:WIN, :])
            dvas.append(dvband[WIN:2 * WIN, :])
            first = (2 * kp + kh) * 4
            ps_delta = jnp.exp(_sink_col(s_ref, first) - lse_s) * delta
            for gg in range(4):
                val = jnp.sum(ps_delta[WIN * gg:WIN * (gg + 1), :], axis=0, keepdims=True)
                dsink = dsink - jnp.where(lane1 == first + gg, val, 0.0)
        dq_ref[...] = jnp.concatenate(dqs, axis=-1)
        dka_ref[...] = jnp.concatenate(dkas, axis=-1)
        dkb_ref[...] = jnp.concatenate(dkbs, axis=-1)
        dva_ref[...] = jnp.concatenate(dvas, axis=-1)
        dvb_ref[...] = jnp.concatenate(dvbs, axis=-1)
        dsink_ref[...] += dsink

    qblk = pl.BlockSpec((WIN, 512), lambda kp, n: (n, kp))
    prev = pl.BlockSpec((WIN, LANES), lambda kp, n: (jnp.maximum(n - 1, 0), kp))
    cur = pl.BlockSpec((WIN, LANES), lambda kp, n: (n, kp))
    return pl.pallas_call(
        body, name="swa_bwd", grid=(2, nb),
        in_specs=[qblk, prev, cur, prev, cur, qblk, qblk, qblk,
                  pl.BlockSpec((WIN, 512), lambda kp, n: (n, 2 + kp)), pl.BlockSpec(memory_space=pltpu.SMEM)],
        out_specs=[qblk, qblk, cur, cur, cur, cur, pl.BlockSpec((1, LANES), lambda kp, n: (0, 0))],
        out_shape=[_sds((S, D), F32), _sds((S, D), BF16)] + [_sds((S, 256), F32)] * 4 + [_sds((1, LANES), F32)],
        compiler_params=_params())(qb, ksh, ksh, vsh, vsh, dz, o, lse, pb, sinks)


def _prep_b_bwd(dq, dgate, pb, gq, cos2, sin2):
    nc = D // CB

    def body(dq_ref, dgt_ref, x_ref, g_ref, c_ref, s_ref, o_ref, dgq_ref):
        i, j = pl.program_id(0), pl.program_id(1)

        @pl.when((i == 0) & (j == 0))
        def _():
            dgq_ref[...] = jnp.zeros_like(dgq_ref)

        @pl.when(j < nc)
        def _():
            lane = _lane_iota((RB, LANES))
            g2, cos, sin = _g2(g_ref), c_ref[...], s_ref[...]
            dg_tot = jnp.zeros((1, LANES), F32)
            for c in _pairs(CB):
                dn = _rope_bwd(dq_ref[:, c] * QSCALE, cos, sin, lane)
                dx, dg = _head_norm_bwd(dn, x_ref[:, c], g2, lane < HD)
                o_ref[:, c] = dx.astype(BF16)
                dg_tot = dg_tot + dg
            dgq_ref[...] += dg_tot

        @pl.when(j >= nc)
        def _():
            o_ref[...] = dgt_ref[...]

    tab = pl.BlockSpec((RB, LANES), lambda i, j: (i, 0))
    return pl.pallas_call(
        body, name="prep_b_bwd", grid=(S // RB, 2 * nc),
        in_specs=[pl.BlockSpec((RB, CB), lambda i, j: (i, jnp.minimum(j, nc - 1))),
                  pl.BlockSpec((RB, CB), lambda i, j: (i, jnp.maximum(j - nc, 0))),
                  pl.BlockSpec((RB, CB), lambda i, j: (i, jnp.minimum(j, nc - 1))),
                  pl.BlockSpec((1, HD), lambda i, j: (0, 0)), tab, tab],
        out_specs=[pl.BlockSpec((RB, CB), lambda i, j: (i, j)), pl.BlockSpec((1, LANES), lambda i, j: (0, 0))],
        out_shape=[_sds((S, 2 * D), BF16), _sds((1, LANES), F32)],
        compiler_params=_params())(dq, dgate, pb, gq, cos2, sin2)


def _prep_kv_bwd(dka, dkb, dva, dvb, kv, gk, cos2, sin2):
    nt = S // RB
    per = RB // WIN

    def shifted(cur_ref, nxt_ref, has_next):
        return jnp.concatenate([cur_ref[WIN:RB, :], jnp.where(has_next, nxt_ref[...], 0.0)], axis=0)

    def body(dka_ref, dkb_ref, dkn_ref, dva_ref, dvb_ref, dvn_ref, x_ref, g_ref, c_ref, s_ref, o_ref, dgk_ref):
        i, j = pl.program_id(0), pl.program_id(1)

        @pl.when((i == 0) & (j == 0))
        def _():
            dgk_ref[...] = jnp.zeros_like(dgk_ref)

        has_next = i < nt - 1

        @pl.when(j == 0)
        def _():
            lane = _lane_iota((RB, LANES))
            g2, cos, sin = _g2(g_ref), c_ref[...], s_ref[...]
            dy_all = dka_ref[...] + shifted(dkb_ref, dkn_ref, has_next)
            dg_tot = jnp.zeros((1, LANES), F32)
            for c in _pairs(CB):
                dn = _rope_bwd(dy_all[:, c], cos, sin, lane)
                dx, dg = _head_norm_bwd(dn, x_ref[:, c], g2, lane < HD)
                o_ref[:, c] = dx.astype(BF16)
                dg_tot = dg_tot + dg
            dgk_ref[...] += dg_tot

        @pl.when(j == 1)
        def _():
            o_ref[...] = (dva_ref[...] + shifted(dvb_ref, dvn_ref, has_next)).astype(BF16)

    cur = pl.BlockSpec((RB, CB), lambda i, j: (i, 0))
    nxt = pl.BlockSpec((WIN, CB), lambda i, j: (jnp.minimum(per * (i + 1), S // WIN - 1), 0))
    tab = pl.BlockSpec((RB, LANES), lambda i, j: (i, 0))
    return pl.pallas_call(
        body, name="prep_kv_bwd", grid=(nt, 2),
        in_specs=[cur, cur, nxt, cur, cur, nxt, cur, pl.BlockSpec((1, HD), lambda i, j: (0, 0)), tab, tab],
        out_specs=[pl.BlockSpec((RB, CB), lambda i, j: (i, j)), pl.BlockSpec((1, LANES), lambda i, j: (0, 0))],
        out_shape=[_sds((S, 2 * CB), BF16), _sds((1, LANES), F32)],
        compiler_params=_params())(dka, dkb, dkb, dva, dvb, dvb, kv, gk, cos2, sin2)


def _loss_dy(y, tgt):
    def body(y_ref, t_ref, dy_ref, l_ref):
        @pl.when(pl.program_id(0) == 0)
        def _():
            l_ref[...] = jnp.zeros_like(l_ref)

        e = y_ref[...] - t_ref[...]
        dy_ref[...] = e * (1.0 / D)
        l_ref[...] += jnp.sum(jnp.sum(e * e, axis=-1, keepdims=True), axis=0, keepdims=True)

    row = pl.BlockSpec((TM, D), lambda i: (i, 0))
    return pl.pallas_call(
        body, name="loss_dy", grid=(S // TM,), in_specs=[row, row],
        out_specs=[row, pl.BlockSpec((1, LANES), lambda i: (0, 0))],
        out_shape=[_sds((S, D), F32), _sds((1, LANES), F32)], compiler_params=_params())(y, tgt)


def _rms2_bwd(du_b, du_kv, h1, g_b, g_kv, dy):
    def body(dub_ref, dukv_ref, x_ref, gb_ref, gkv_ref, dy_ref, dh_ref, dgb_ref, dgkv_ref):
        @pl.when(pl.program_id(0) == 0)
        def _():
            dgb_ref[...] = jnp.zeros_like(dgb_ref)
            dgkv_ref[...] = jnp.zeros_like(dgkv_ref)

        x = x_ref[...]
        dx1, dg1 = _rms_bwd_core(dub_ref[...], x, gb_ref[...])
        dx2, dg2 = _rms_bwd_core(dukv_ref[...], x, gkv_ref[...])
        dh_ref[...] = dy_ref[...] + dx1 + dx2
        dgb_ref[...] += dg1
        dgkv_ref[...] += dg2

    row = pl.BlockSpec((TM, D), lambda i: (i, 0))
    vec = pl.BlockSpec((1, D), lambda i: (0, 0))
    return pl.pallas_call(
        body, name="rms2_bwd", grid=(S // TM,), in_specs=[row, row, row, vec, vec, row],
        out_specs=[row, vec, vec], out_shape=[_sds((S, D), F32), _sds((1, D), F32), _sds((1, D), F32)],
        compiler_params=_params())(du_b, du_kv, h1, g_b, g_kv, dy)


def _rms_bwd(du, x, g, dres):
    def body(du_ref, x_ref, g_ref, dr_ref, dx_ref, dg_ref):
        @pl.when(pl.program_id(0) == 0)
        def _():
            dg_ref[...] = jnp.zeros_like(dg_ref)

        dx, dg = _rms_bwd_core(du_ref[...], x_ref[...], g_ref[...])
        dx_ref[...] = dr_ref[...] + dx
        dg_ref[...] += dg

    row = pl.BlockSpec((TM, D), lambda i: (i, 0))
    vec = pl.BlockSpec((1, D), lambda i: (0, 0))
    return pl.pallas_call(
        body, name="rms_bwd", grid=(S // TM,), in_specs=[row, row, vec, row], out_specs=[row, vec],
        out_shape=[_sds((S, D), F32), _sds((1, D), F32)], compiler_params=_params())(du, x, g, dres)


def _gather_first(w_in_a, w_out_a, w_kv, w_in_b, w_out_b, norm_a_g):
    def body(wia_ref, woa_ref, wkv_ref, wib_ref, wob_ref, ga_ref,
             wa_g, ga_g, woa_s, wkv_s, wib_s, wob_s, wa_s, *sems):
        wa_s[...] = wia_ref[0].astype(BF16)
        woa_s[...] = woa_ref[0].astype(BF16)
        wkv_s[...] = wkv_ref[...].astype(BF16)
        wib_s[...] = wib_ref[0].astype(BF16)
        wob_s[...] = wob_ref[0].astype(BF16)
        _gather_two_level([wa_s, ga_ref], [wa_g, ga_g], sems)

    vmem = pl.BlockSpec(memory_space=pltpu.VMEM)
    anyspec = pl.BlockSpec(memory_space=pl.ANY)
    shard = lambda w: _sds(w.shape[-2:], BF16)
    return pl.pallas_call(
        body, name="gather_first", in_specs=[vmem] * 6, out_specs=[anyspec, anyspec, vmem, vmem, vmem, vmem],
        out_shape=[_sds((N_DEV,) + w_in_a.shape[-2:], BF16), _sds((N_DEV,) + norm_a_g.shape, F32),
                   shard(w_out_a), shard(w_kv), shard(w_in_b), shard(w_out_b)],
        scratch_shapes=[pltpu.VMEM(w_in_a.shape[-2:], BF16)] + _exchange_sems(2),
        compiler_params=pltpu.CompilerParams(vmem_limit_bytes=VMEM_LIMIT, has_side_effects=True))(
            w_in_a, w_out_a, w_kv, w_in_b, w_out_b, norm_a_g)


def _pair_reduce(slots):
    n_chip = N_DEV // 2
    _, rows, cols = slots.shape

    def body(s_ref, o_ref, own_v, sib_v, send_sems, recv_sems, local_sems):
        x, y, c = lax.axis_index("x"), lax.axis_index("y"), lax.axis_index("c")
        copies = []
        for j in range(n_chip):
            own = pltpu.make_async_copy(s_ref.at[2 * j + c], own_v.at[j], local_sems.at[j])
            give = pltpu.make_async_remote_copy(
                src_ref=s_ref.at[2 * j + 1 - c], dst_ref=sib_v.at[j], send_sem=send_sems.at[j],
                recv_sem=recv_sems.at[j], device_id=(x, y, 1 - c), device_id_type=pl.DeviceIdType.MESH)
            own.start()
            give.start()
            copies.append((own, give))
        for j, (own, give) in enumerate(copies):
            own.wait()
            give.wait()
            o_ref[j] = (own_v[j].astype(F32) + sib_v[j].astype(F32)).astype(BF16)

    half = _sds((n_chip, rows, cols), slots.dtype)
    return pl.pallas_call(
        body, name="pair_reduce", in_specs=[pl.BlockSpec(memory_space=pl.ANY)],
        out_specs=pl.BlockSpec(memory_space=pltpu.VMEM), out_shape=half,
        scratch_shapes=[pltpu.VMEM(half.shape, half.dtype), pltpu.VMEM(half.shape, half.dtype),
                        pltpu.SemaphoreType.DMA((n_chip,)), pltpu.SemaphoreType.DMA((n_chip,)),
                        pltpu.SemaphoreType.DMA((n_chip,))],
        compiler_params=pltpu.CompilerParams(vmem_limit_bytes=VMEM_LIMIT, has_side_effects=True))(slots)


def _padded_col(c):
    return c if c < FOFF + N_HEADS else c + (GOFF - FOFF - N_HEADS)


def _shard_pieces():
    width = NA_RAW // N_DEV
    split = FOFF + N_HEADS
    pieces = []
    for d in range(N_DEV):
        lo, hi = width * d, width * (d + 1)
        if lo < split < hi:
            pieces += [(d, 0, lo, split - lo), (d, split - lo, GOFF, hi - split)]
        else:
            pieces.append((d, 0, _padded_col(lo), width))
    return pieces


def _unshard_wa(wa_g):
    def body(w_ref, o_ref):
        o_ref[:, FOFF + N_HEADS:GOFF] = jnp.zeros((TM, GOFF - FOFF - N_HEADS), BF16)
        for d, src, dst, width in _shard_pieces():
            o_ref[:, dst:dst + width] = w_ref[d, :, src:src + width]

    return pl.pallas_call(
        body, name="unshard_wa", grid=(D // TM,),
        in_specs=[pl.BlockSpec((N_DEV, TM, NA_RAW // N_DEV), lambda i: (0, i, 0))],
        out_specs=pl.BlockSpec((TM, NA), lambda i: (i, 0)), out_shape=_sds((D, NA), BF16),
        compiler_params=_params())(wa_g)


def _reshard_dwa(dwa):
    def body(g_ref, o_ref):
        for d, src, dst, width in _shard_pieces():
            o_ref[d, :, src:src + width] = g_ref[:, dst:dst + width]

    return pl.pallas_call(
        body, name="reshard_dwa", grid=(D // TM,), in_specs=[pl.BlockSpec((TM, NA), lambda i: (i, 0))],
        out_specs=pl.BlockSpec((N_DEV, TM, NA_RAW // N_DEV), lambda i: (0, i, 0)),
        out_shape=_sds((N_DEV, D, NA_RAW // N_DEV), dwa.dtype), compiler_params=_params())(dwa)


def _gather_slab(slab):
    def body(s_ref, o_ref, *sems):
        _exchange_ops(["gather_rows"], [s_ref], [o_ref], sems, True, True)

    anyspec = pl.BlockSpec(memory_space=pl.ANY)
    return pl.pallas_call(
        body, name="gather_slab", in_specs=[anyspec], out_specs=anyspec,
        out_shape=_sds((N_DEV,) + slab.shape, slab.dtype), scratch_shapes=_exchange_sems(1),
        compiler_params=pltpu.CompilerParams(has_side_effects=True))(slab)


def _adamw(w, g, m, v):
    m = ADAM_B1 * m + (1.0 - ADAM_B1) * g
    v = ADAM_B2 * v + (1.0 - ADAM_B2) * (g * g)
    m_hat = m / (1.0 - ADAM_B1 ** ADAM_STEP)
    v_hat = v / (1.0 - ADAM_B2 ** ADAM_STEP)
    delta = -ADAM_LR * (m_hat / (jnp.sqrt(v_hat) + ADAM_EPS) + ADAM_WD * w)
    return delta, m, v


def _sum_adamw(recv, w, m, v, name):
    lead = w.ndim - 2
    rows, cols = w.shape[-2:]
    tr = 128

    n_slots = recv.shape[0]

    def body(r_ref, w_ref, m_ref, v_ref, g_ref, d_ref, nm_ref, nv_ref):
        g = r_ref[0].astype(F32)
        for slot in range(1, n_slots):
            g = g + r_ref[slot].astype(F32)
        g_ref[...] = g
        d_ref[...], nm_ref[...], nv_ref[...] = _adamw(w_ref[...], g, m_ref[...], v_ref[...])

    blk = pl.BlockSpec((None,) * lead + (tr, cols), lambda i: (0,) * lead + (i, 0))
    return pl.pallas_call(
        body, name=name, grid=(rows // tr,),
        in_specs=[pl.BlockSpec((n_slots, tr, cols), lambda i: (0, i, 0)), blk, blk, blk],
        out_specs=[blk] * 4, out_shape=[_sds(w.shape, F32)] * 4,
        compiler_params=_params())(recv, w, m, v)


SLAB_ROWS = 16
SLOT = {"kv_norm_g": (8, 0, D), "norm_b_g": (9, 0, D), "b_forget": (10, 0, 16), "qnorm_a_g": (10, 128, HD),
        "knorm_a_g": (10, 256, HD), "knorm_b_g": (10, 384, HD), "qnorm_b_g": (10, 512, HD), "sinks": (10, 640, 16)}
SMALL = ["norm_a_g", "b_forget", "qnorm_a_g", "knorm_a_g", "kv_norm_g", "knorm_b_g", "norm_b_g", "qnorm_b_g", "sinks"]


LOSS_ROW = 11


def _pack_small(dg_a, dg_kv, dg_b, db_f, dgq_a, dgk_a, dgk_b, dgq_b, dsinks, lsum):
    def fold(ref):
        return ref[:, 0:HD] + ref[:, HD:2 * HD]

    def body(dga_ref, dgkv_ref, dgb_ref, dbf_ref, dgqa_ref, dgka_ref, dgkb_ref, dgqb_ref, dsk_ref, ls_ref, slab_ref):
        slab_ref[...] = jnp.zeros_like(slab_ref)
        for r in range(N_DEV):
            slab_ref[r:r + 1, 0:LANES] = dga_ref[:, LANES * r:LANES * (r + 1)]
        slab_ref[8:9, :] = dgkv_ref[...]
        slab_ref[9:10, :] = dgb_ref[...]
        slab_ref[10:11, 0:LANES] = dbf_ref[...]
        slab_ref[10:11, 128:128 + HD] = fold(dgqa_ref)
        slab_ref[10:11, 256:256 + HD] = fold(dgka_ref)
        slab_ref[10:11, 384:384 + HD] = fold(dgkb_ref)
        slab_ref[10:11, 512:512 + HD] = fold(dgqb_ref)
        slab_ref[10:11, 640:640 + LANES] = dsk_ref[...]
        slab_ref[LOSS_ROW:LOSS_ROW + 1, 0:LANES] = ls_ref[...]

    return pl.pallas_call(body, name="pack_small", out_shape=_sds((SLAB_ROWS, D), F32), compiler_params=_params())(
        dg_a, dg_kv, dg_b, db_f, dgq_a, dgk_a, dgk_b, dgq_b, dsinks, lsum)


def _small_adamw(recv, ws, ms, vs):
    k = len(SMALL)

    def body(*refs):
        r_ref = refs[0]
        w_refs, m_refs, v_refs = refs[1:1 + k], refs[1 + k:1 + 2 * k], refs[1 + 2 * k:1 + 3 * k]
        outs = refs[1 + 3 * k:1 + 7 * k]
        loss_ref, tot = refs[1 + 7 * k], refs[2 + 7 * k]
        g = r_ref[0]
        for dev in range(1, N_DEV):
            g = g + r_ref[dev]
        tot[...] = g
        loss_ref[...] = tot[LOSS_ROW:LOSS_ROW + 1, 0:LANES] * (0.5 / D)
        me = 4 * lax.axis_index("x") + 2 * lax.axis_index("y") + lax.axis_index("c")
        for p, name in enumerate(SMALL):
            if name == "norm_a_g":
                mine = lax.broadcasted_iota(jnp.int32, (N_DEV, LANES), 0) == me
                gp = jnp.sum(jnp.where(mine, tot[0:N_DEV, 0:LANES], 0.0), axis=0, keepdims=True)
            else:
                row, lo, width = SLOT[name]
                gp = tot[row:row + 1, lo:lo + width]
            d, nm, nv = _adamw(w_refs[p][...], gp, m_refs[p][...], v_refs[p][...])
            outs[p][...] = gp
            outs[k + p][...] = d
            outs[2 * k + p][...] = nm
            outs[3 * k + p][...] = nv

    shapes = [_sds(w.shape, F32) for w in ws]
    return pl.pallas_call(body, name="small_adamw", out_shape=shapes * 4 + [_sds((1, LANES), F32)],
                          scratch_shapes=[pltpu.VMEM((SLAB_ROWS, D), F32)],
                          compiler_params=_params())(recv, *ws, *ms, *vs)


def _rope_tables(positions):
    inv_freq = jnp.power(jnp.float32(ROPE_THETA), -jnp.arange(0, ROT, 2, dtype=F32) / ROT)
    ang = positions.astype(F32)[:, None] * inv_freq[None, :]
    cos, sin = jnp.cos(ang), jnp.sin(ang)
    c64 = jnp.concatenate([cos, cos, jnp.ones((S, HD - ROT), F32)], axis=-1)
    s64 = jnp.concatenate([-sin, sin, jnp.zeros((S, HD - ROT), F32)], axis=-1)
    return jnp.tile(c64, (1, 2)), jnp.tile(s64, (1, 2))


def _local_step(x, tgt, positions, g_a, wa, b_forget, gq_a, gk_a, g_kv, gk_b, g_b, gq_b, sinks,
                woa_s, wkv_s, wib_s, wob_s):
    nq = S // TQ
    cos2, sin2 = _rope_tables(positions)
    b_pad = jnp.pad(b_forget, ((0, 0), (0, LANES - N_HEADS)))

    u_a = _rms_fwd(x, g_a, "rms_a_fwd")
    proj = _mm(u_a, wa, "nn", S, 256, D, name="mm_in_a")
    qn, kn, vb = _prep_a(proj, gq_a, gk_a)
    ccol, cbc = _fgate_fwd(proj, b_pad)
    crow = ccol[:, :N_HEADS].T.reshape(N_HEADS, nq, 1, TQ)
    o_a, z_a, lse_a, woa_g, wkv_g, w_in_b, wob_g = _fox_fwd(
        qn, kn, vb, proj, crow, cbc,
        rider=[("gather_rows", woa_s), ("gather_rows", wkv_s), ("gather_cols", wib_s), ("gather_rows", wob_s)])
    w_out_a, w_kv, w_out_b = woa_g.reshape(D, D), wkv_g.reshape(D, 512), wob_g.reshape(D, D)
    h1 = _mm(z_a, w_out_a, "nn", 1024, 512, D, add=x, name="mm_out_a")
    u_kv, u_b = _rms2_fwd(h1, g_kv, g_b)
    kv = _mm(u_kv, w_kv, "nn", 1024, 512, D, name="mm_kv")
    pb = _mm(u_b, w_in_b, "nn", 1024, 512, D, name="mm_in_b")
    qb = _prep_b(pb, gq_b, cos2, sin2)
    ksh, vsh = _prep_kv(kv, gk_b, cos2, sin2)
    sinks1 = sinks.reshape(N_HEADS)
    o_b, z_b, lse_b = _swa_fwd(qb, ksh, vsh, pb, sinks1)
    y = _mm(z_b, w_out_b, "nn", 1024, 512, D, add=h1, name="mm_out_b")
    dy, lsum = _loss_dy(y, tgt)
    dw_out_b = _mm(z_b, dy, "tn", 512, 512, S, out_dtype=BF16, name="mm_dw_out_b")
    dz_b = _mm(dy, w_out_b, "nt", 1024, 512, D, name="mm_dz_b")
    dq_b, dgate_b, dka, dkb, dva, dvb, dsinks = _swa_bwd(qb, ksh, vsh, dz_b, o_b, lse_b, pb, sinks1)
    dpb, dgq_b = _prep_b_bwd(dq_b, dgate_b, pb, gq_b, cos2, sin2)
    dkv, dgk_b = _prep_kv_bwd(dka, dkb, dva, dvb, kv, gk_b, cos2, sin2)
    dw_in_b = _mm(u_b, dpb, "tn", 512, 512, S, out_dtype=BF16, name="mm_dw_in_b")
    du_b = _mm(dpb, w_in_b, "nt", 1024, 512, 2048, name="mm_du_b")
    dw_kv = _mm(u_kv, dkv, "tn", 512, 512, S, out_dtype=BF16, name="mm_dw_kv")
    du_kv = _mm(dkv, w_kv, "nt", 1024, 512, 512, name="mm_du_kv")
    dh1, dg_b, dg_kv = _rms2_bwd(du_b, du_kv, h1, g_b, g_kv, dy)
    dw_out_a = _mm(z_a, dh1, "tn", 512, 512, S, out_dtype=BF16, name="mm_dw_out_a")
    dz_a = _mm(dh1, w_out_a, "nt", 1024, 512, D, name="mm_dz_a")
    dq_a, delta_a, do_a, dgate_a, drow, r_wob, r_wib, r_wkv, r_woa = _fox_bwd_dq(
        qn, kn, vb, dz_a, proj, o_a, lse_a, crow, cbc,
        rider=[("a2a_rows", dw_out_b), ("a2a_cols", dw_in_b), ("a2a_rows", dw_kv), ("a2a_rows", dw_out_a)])
    dk_a, dv_a, dcs = _fox_bwd_dkv(qn, kn, vb, do_a, lse_a, delta_a, crow, cbc)
    dproj, dgq_a, dgk_a = _prep_a_bwd(dq_a, dk_a, dv_a, dgate_a, proj, gq_a, gk_a)
    drow_col = jnp.pad(drow.reshape(N_HEADS, S).T, ((0, 0), (0, LANES - N_HEADS)))
    dproj, db_f = _fgate_bwd(dproj, drow_col, dcs, proj, b_pad)
    dwa = _mm(u_a, dproj, "tn", 1024, 256, S, out_dtype=BF16, name="mm_dw_in_a")
    du_a, r_wa = _mm(dproj, wa, "nt", 1024, 512, NA // 2, name="mm_du_a",
                     rider=[("a2a_chips", _pair_reduce(_reshard_dwa(dwa)))])
    dx, dg_a = _rms_bwd(du_a, x, g_a, dh1)
    slab = _pack_small(dg_a, dg_kv, dg_b, db_f, dgq_a, dgk_a, dgk_b, dgq_b, dsinks, lsum)
    return dx, r_wa, r_woa, r_wkv, r_wib, r_wob, _gather_slab(slab)


def kernel(x, positions, norm_a_g, w_in_a, b_forget, qnorm_a_g, knorm_a_g, w_out_a, kv_norm_g, w_kv, knorm_b_g, norm_b_g, w_in_b, qnorm_b_g, sinks, w_out_b, loss_target, m_norm_a_g, m_w_in_a, m_b_forget, m_qnorm_a_g, m_knorm_a_g, m_w_out_a, m_kv_norm_g, m_w_kv, m_knorm_b_g, m_norm_b_g, m_w_in_b, m_qnorm_b_g, m_sinks, m_w_out_b, v_norm_a_g, v_w_in_a, v_b_forget, v_qnorm_a_g, v_knorm_a_g, v_w_out_a, v_kv_norm_g, v_w_kv, v_knorm_b_g, v_norm_b_g, v_w_in_b, v_qnorm_b_g, v_sinks, v_w_out_b):
    wa_g, ga_g, woa_s, wkv_s, wib_s, wob_s = _gather_first(w_in_a, w_out_a, w_kv, w_in_b, w_out_b, norm_a_g)
    dx, r_wa, r_woa, r_wkv, r_wib, r_wob, slab_g = _local_step(
        x[0], loss_target[0], positions, ga_g.reshape(1, D), _unshard_wa(wa_g), b_forget, qnorm_a_g, knorm_a_g,
        kv_norm_g.reshape(1, D), knorm_b_g.reshape(1, HD), norm_b_g, qnorm_b_g, sinks, woa_s, wkv_s, wib_s, wob_s)

    big = {}
    for name, recv, w, m, v in (
            ("w_in_a", r_wa, w_in_a, m_w_in_a, v_w_in_a), ("w_out_a", r_woa, w_out_a, m_w_out_a, v_w_out_a),
            ("w_kv", r_wkv, w_kv, m_w_kv, v_w_kv), ("w_in_b", r_wib, w_in_b, m_w_in_b, v_w_in_b),
            ("w_out_b", r_wob, w_out_b, m_w_out_b, v_w_out_b)):
        big[name] = _sum_adamw(recv, w, m, v, "adamw_" + name)

    r2 = lambda a: a.reshape(1, -1)
    small_w = dict(norm_a_g=norm_a_g, b_forget=b_forget, qnorm_a_g=qnorm_a_g, knorm_a_g=knorm_a_g,
                   kv_norm_g=kv_norm_g, knorm_b_g=knorm_b_g, norm_b_g=norm_b_g, qnorm_b_g=qnorm_b_g, sinks=sinks)
    small_m = dict(norm_a_g=m_norm_a_g, b_forget=m_b_forget, qnorm_a_g=m_qnorm_a_g, knorm_a_g=m_knorm_a_g,
                   kv_norm_g=m_kv_norm_g, knorm_b_g=m_knorm_b_g, norm_b_g=m_norm_b_g, qnorm_b_g=m_qnorm_b_g,
                   sinks=m_sinks)
    small_v = dict(norm_a_g=v_norm_a_g, b_forget=v_b_forget, qnorm_a_g=v_qnorm_a_g, knorm_a_g=v_knorm_a_g,
                   kv_norm_g=v_kv_norm_g, knorm_b_g=v_knorm_b_g, norm_b_g=v_norm_b_g, qnorm_b_g=v_qnorm_b_g,
                   sinks=v_sinks)
    res = _small_adamw(slab_g, [r2(small_w[n]) for n in SMALL], [r2(small_m[n]) for n in SMALL],
                       [r2(small_v[n]) for n in SMALL])
    k = len(SMALL)
    small = {n: [res[q * k + p].reshape(small_w[n].shape) for q in range(4)] for p, n in enumerate(SMALL)}
    loss = res[4 * k][0, 0]

    order = ["norm_a_g", "w_in_a", "b_forget", "qnorm_a_g", "knorm_a_g", "w_out_a", "kv_norm_g", "w_kv",
             "knorm_b_g", "norm_b_g", "w_in_b", "qnorm_b_g", "sinks", "w_out_b"]

    def leaf(n, q):
        return big[n][q] if n in big else small[n][q]

    outs = [loss, dx[None]]
    for q in range(4):
        outs.extend(leaf(n, q) for n in order)
    return tuple(outs)
```

```python
import jax
import jax.numpy as jnp
from jax import lax
from jax.experimental import pallas as pl
from jax.experimental.pallas import tpu as pltpu

F32, BF16 = jnp.float32, jnp.bfloat16

S = 2048
D = 1024
HD = 64
N_HEADS = 16
N_DEV = 8
NA = 4352
GOFF = 3072
FOFF = 4096
RAW_F = 3072
RAW_G = RAW_F + N_HEADS
NA_RAW = 4112
EPS = 1e-6
QSCALE = 0.125
ROPE_THETA = 500000.0
ROT = 16
WIN = 128
TQ = 256
TK = 256
KS = TQ // 2
HPS = 8
HW = HPS * HD
TM = 256
RB = 512
CB = 256
LANES = 128

ADAM_LR, ADAM_B1, ADAM_B2, ADAM_EPS, ADAM_WD, ADAM_STEP = 0.001, 0.9, 0.999, 1e-08, 0.01, 10

VMEM_LIMIT = 48 * 1024 * 1024


def _params():
    return pltpu.CompilerParams(vmem_limit_bytes=VMEM_LIMIT)


def _sds(shape, dtype):
    return jax.ShapeDtypeStruct(shape, dtype)


def _dot_nt(a, b):
    return lax.dot_general(a, b, (((1,), (1,)), ((), ())), preferred_element_type=F32)


def _dot_tn(a, b):
    return lax.dot_general(a, b, (((0,), (0,)), ((), ())), preferred_element_type=F32)


def _dot_nn(a, b):
    return lax.dot_general(a, b, (((1,), (0,)), ((), ())), preferred_element_type=F32)


def _sigmoid(g):
    return 1.0 / (1.0 + jnp.exp(-g))


def _lane_iota(shape):
    return lax.broadcasted_iota(jnp.int32, shape, len(shape) - 1)


def _flips(kind):
    return (2, 4, 6) if kind == "a2a_chips" else tuple(range(1, N_DEV))


def _send_view(kind, ref, dev):
    if kind in ("gather_rows", "gather_cols"):
        return ref
    if kind == "a2a_slots":
        return ref.at[dev]
    if kind == "a2a_chips":
        return ref.at[dev >> 1]
    if kind == "a2a_rows":
        rows = ref.shape[0] // N_DEV
        return ref.at[pl.ds(pl.multiple_of(dev * rows, rows), rows)]
    cols = ref.shape[1] // N_DEV
    return ref.at[:, pl.ds(pl.multiple_of(dev * cols, cols), cols)]


def _land_view(kind, ref, dev):
    if kind == "gather_cols":
        cols = ref.shape[1] // N_DEV
        return ref.at[:, pl.ds(pl.multiple_of(dev * cols, cols), cols)]
    if kind == "a2a_chips":
        return ref.at[dev >> 1]
    return ref.at[dev]


def _landing_sds(kind, arr):
    if kind == "gather_rows":
        return _sds((N_DEV,) + arr.shape, arr.dtype)
    if kind == "gather_cols":
        return _sds((arr.shape[0], N_DEV * arr.shape[1]), arr.dtype)
    if kind == "a2a_rows":
        return _sds((N_DEV, arr.shape[0] // N_DEV, arr.shape[1]), arr.dtype)
    if kind == "a2a_cols":
        return _sds((N_DEV, arr.shape[0], arr.shape[1] // N_DEV), arr.dtype)
    return _sds(arr.shape, arr.dtype)


def _exchange_sems(n_parts):
    n = n_parts * (N_DEV - 1)
    return [pltpu.SemaphoreType.DMA((n,)), pltpu.SemaphoreType.DMA((n,)), pltpu.SemaphoreType.DMA((n_parts,))]


def _exchange_ops(kinds, srcs, dsts, sems, start, wait):
    send_sems, recv_sems, local_sems = sems
    x, y, c = lax.axis_index("x"), lax.axis_index("y"), lax.axis_index("c")
    me = 4 * x + 2 * y + c

    def local(a):
        return pltpu.make_async_copy(_send_view(kinds[a], srcs[a], me), _land_view(kinds[a], dsts[a], me),
                                     local_sems.at[a])

    def remote(a, k, landing_dev):
        peer = (x ^ ((k >> 2) & 1), y ^ ((k >> 1) & 1), c ^ (k & 1))
        sem = a * (N_DEV - 1) + k - 1
        return pltpu.make_async_remote_copy(
            src_ref=_send_view(kinds[a], srcs[a], me ^ k), dst_ref=_land_view(kinds[a], dsts[a], landing_dev),
            send_sem=send_sems.at[sem], recv_sem=recv_sems.at[sem], device_id=peer,
            device_id_type=pl.DeviceIdType.MESH)

    pairs = [(a, k) for k in range(1, N_DEV) for a in range(len(kinds)) if k in _flips(kinds[a])]
    if start:
        for a in range(len(kinds)):
            local(a).start()
        for a, k in pairs:
            remote(a, k, me).start()
    if wait:
        for a, k in pairs:
            remote(a, k, me ^ k).wait_recv()
            remote(a, k, me).wait_send()
        for a in range(len(kinds)):
            local(a).wait()


def _gather_two_level(srcs, dsts, sems):
    send_sems, recv_sems, local_sems = sems
    x, y, c = lax.axis_index("x"), lax.axis_index("y"), lax.axis_index("c")
    me, sibling = (x, y, c), (x, y, 1 - c)
    chips = [(1 - x, y), (x, 1 - y), (1 - x, 1 - y)]

    def slot(ref, dev):
        return ref.at[4 * dev[0] + 2 * dev[1] + dev[2]]

    def copy(a, k, block, to, src=None):
        return pltpu.make_async_remote_copy(
            src_ref=slot(dsts[a], block) if src is None else src, dst_ref=slot(dsts[a], block),
            send_sem=send_sems.at[a * (N_DEV - 1) + k], recv_sem=recv_sems.at[a * (N_DEV - 1) + k],
            device_id=to, device_id_type=pl.DeviceIdType.MESH)

    parts = range(len(srcs))
    mine = [pltpu.make_async_copy(srcs[a], slot(dsts[a], me), local_sems.at[a]) for a in parts]
    first = [copy(a, 0, me, sibling, src=srcs[a]) for a in parts]
    first += [copy(a, 1 + j, me, (*chip, c), src=srcs[a]) for j, chip in enumerate(chips) for a in parts]
    for cp in mine + first:
        cp.start()
    passed = []
    for j, chip in enumerate(chips):
        for a in parts:
            copy(a, 1 + j, (*chip, c), me).wait_recv()
            fwd = copy(a, 4 + j, (*chip, c), sibling)
            fwd.start()
            passed.append(fwd)
    for a in parts:
        copy(a, 0, sibling, me).wait_recv()
        for j, chip in enumerate(chips):
            copy(a, 4 + j, (*chip, 1 - c), me).wait_recv()
    for cp in first + passed:
        cp.wait_send()
    for cp in mine:
        cp.wait()


def _call(body, *, name, args, in_specs, out_specs, out_shape, grid=(), scratch_shapes=(), aliases=None, rider=()):
    n_in, n_out, n_scr, n_r = len(in_specs), len(out_specs), len(scratch_shapes), len(rider)
    kinds = [kind for kind, _ in rider]

    def kernel_body(*refs):
        c_in, r_in = refs[:n_in], refs[n_in:n_in + n_r]
        c_out = refs[n_in + n_r:n_in + n_r + n_out]
        r_out = refs[n_in + n_r + n_out:n_in + 2 * n_r + n_out]
        rest = refs[n_in + 2 * n_r + n_out:]
        c_scr, sems = rest[:n_scr], rest[n_scr:]
        if n_r:
            assert grid, "a rider needs a gridded call"
            ids = [pl.program_id(ax) for ax in range(len(grid))]
            first, last = ids[0] == 0, ids[0] == grid[0] - 1
            for pid, size in zip(ids[1:], grid[1:]):
                first = first & (pid == 0)
                last = last & (pid == size - 1)
            pl.when(first)(lambda: _exchange_ops(kinds, r_in, r_out, sems, True, False))
        body(*c_in, *c_out, *c_scr)
        if n_r:
            pl.when(last)(lambda: _exchange_ops(kinds, r_in, r_out, sems, False, True))

    anyspec = pl.BlockSpec(memory_space=pl.ANY)
    params = pltpu.CompilerParams(vmem_limit_bytes=VMEM_LIMIT, has_side_effects=bool(n_r))
    outs = pl.pallas_call(
        kernel_body, name=name, grid=grid, in_specs=list(in_specs) + [anyspec] * n_r,
        out_specs=list(out_specs) + [anyspec] * n_r,
        out_shape=list(out_shape) + [_landing_sds(kind, arr) for kind, arr in rider],
        scratch_shapes=list(scratch_shapes) + (_exchange_sems(n_r) if n_r else []),
        input_output_aliases=aliases or {}, compiler_params=params)(*args, *[arr for _, arr in rider])
    return list(outs)


def _mm(a, b, mode, tm, tn, tk, out_dtype=F32, add=None, name="mm", rider=()):
    if mode == "nn":
        (m, k), n = a.shape, b.shape[1]
        a_spec = pl.BlockSpec((tm, tk), lambda i, j, kk: (i, kk))
        b_spec = pl.BlockSpec((tk, tn), lambda i, j, kk: (kk, j))
        dot = _dot_nn
    elif mode == "nt":
        (m, k), n = a.shape, b.shape[0]
        a_spec = pl.BlockSpec((tm, tk), lambda i, j, kk: (i, kk))
        b_spec = pl.BlockSpec((tn, tk), lambda i, j, kk: (j, kk))
        dot = _dot_nt
    else:
        (k, m), n = a.shape, b.shape[1]
        a_spec = pl.BlockSpec((tk, tm), lambda i, j, kk: (kk, i))
        b_spec = pl.BlockSpec((tk, tn), lambda i, j, kk: (kk, j))
        dot = _dot_tn
    assert m % tm == 0 and n % tn == 0 and k % tk == 0, (m, n, k, tm, tn, tk)
    nk = k // tk
    has_add = add is not None

    def body(*refs):
        if has_add:
            a_ref, b_ref, add_ref, o_ref, acc = refs
        else:
            a_ref, b_ref, o_ref, acc = refs
        p = dot(a_ref[...].astype(BF16), b_ref[...].astype(BF16))

        def finish(total):
            if has_add:
                total = add_ref[...] + total
            o_ref[...] = total.astype(out_dtype)

        if nk == 1:
            finish(p)
        else:
            kk = pl.program_id(2)

            @pl.when(kk == 0)
            def _():
                acc[...] = p

            @pl.when(kk > 0)
            def _():
                acc[...] += p

            @pl.when(kk == nk - 1)
            def _():
                finish(acc[...])

    in_specs = [a_spec, b_spec]
    args = [a, b]
    if has_add:
        in_specs.append(pl.BlockSpec((tm, tn), lambda i, j, kk: (i, j)))
        args.append(add)
    acc_shape = (tm, tn) if nk > 1 else (8, LANES)
    outs = _call(body, name=name, args=args, grid=(m // tm, n // tn, nk), in_specs=in_specs,
                 out_specs=[pl.BlockSpec((tm, tn), lambda i, j, kk: (i, j))], out_shape=[_sds((m, n), out_dtype)],
                 scratch_shapes=[pltpu.VMEM(acc_shape, F32)], rider=rider)
    return outs if rider else outs[0]


def _rms_rinv(x):
    return lax.rsqrt(jnp.mean(x * x, axis=-1, keepdims=True) + EPS)


def _rms_bwd_core(du, x, g):
    r = _rms_rinv(x)
    dug = du * g
    dx = r * (dug - x * ((r * r) * jnp.mean(dug * x, axis=-1, keepdims=True)))
    dg = jnp.sum(du * (x * r), axis=0, keepdims=True)
    return dx, dg


def _half_sum(v, lo_half):
    s0 = jnp.sum(jnp.where(lo_half, v, 0.0), axis=-1, keepdims=True)
    s1 = jnp.sum(jnp.where(lo_half, 0.0, v), axis=-1, keepdims=True)
    return jnp.where(lo_half, s0, s1)


def _head_rinv(x, lo_half):
    return lax.rsqrt(_half_sum(x * x, lo_half) * (1.0 / HD) + EPS)


def _head_norm_bwd(dn, x, g, lo_half):
    r = _head_rinv(x, lo_half)
    dng = dn * g
    dx = r * (dng - x * ((r * r) * (_half_sum(dng * x, lo_half) * (1.0 / HD))))
    dg = jnp.sum(dn * (x * r), axis=0, keepdims=True)
    return dx, dg


def _rope_swap(x, lane):
    l64 = lane & (HD - 1)
    return jnp.where(l64 < ROT // 2, pltpu.roll(x, LANES - ROT // 2, 1), pltpu.roll(x, ROT // 2, 1))


def _rope_fwd(x, cos, sin, lane):
    return x * cos + _rope_swap(x, lane) * sin


def _rope_bwd(dy, cos, sin, lane):
    return dy * cos + jnp.where((lane & (HD - 1)) < ROT, _rope_swap(dy * sin, lane), 0.0)


def _g2(g_ref):
    g = g_ref[...]
    return jnp.concatenate([g, g], axis=-1)


def _pairs(width):
    return [slice(LANES * c, LANES * (c + 1)) for c in range(width // LANES)]


def _rms_fwd(x, g, name):
    def body(x_ref, g_ref, u_ref):
        xv = x_ref[...]
        u_ref[...] = ((xv * _rms_rinv(xv)) * g_ref[...]).astype(BF16)

    return pl.pallas_call(
        body, name=name, grid=(S // TM,),
        in_specs=[pl.BlockSpec((TM, D), lambda i: (i, 0)), pl.BlockSpec((1, D), lambda i: (0, 0))],
        out_specs=pl.BlockSpec((TM, D), lambda i: (i, 0)), out_shape=_sds((S, D), BF16),
        compiler_params=_params())(x, g)


def _rms2_fwd(h, g1, g2):
    def body(x_ref, g1_ref, g2_ref, u1_ref, u2_ref):
        xv = x_ref[...]
        xn = xv * _rms_rinv(xv)
        u1_ref[...] = (xn * g1_ref[...]).astype(BF16)
        u2_ref[...] = (xn * g2_ref[...]).astype(BF16)

    row = pl.BlockSpec((TM, D), lambda i: (i, 0))
    vec = pl.BlockSpec((1, D), lambda i: (0, 0))
    return pl.pallas_call(
        body, name="rms2_fwd", grid=(S // TM,), in_specs=[row, vec, vec], out_specs=[row, row],
        out_shape=[_sds((S, D), BF16)] * 2, compiler_params=_params())(h, g1, g2)


def _prep_a(proj, gq, gk):
    def body(q_ref, k_ref, v_ref, gq_ref, gk_ref, qo_ref, ko_ref, vo_ref):
        lo_half = _lane_iota((TM, LANES)) < HD
        gq2, gk2 = _g2(gq_ref), _g2(gk_ref)
        for c in _pairs(D):
            q = q_ref[:, c]
            k = k_ref[:, c]
            qo_ref[:, c] = (((q * _head_rinv(q, lo_half)) * gq2) * QSCALE).astype(BF16)
            ko_ref[:, c] = ((k * _head_rinv(k, lo_half)) * gk2).astype(BF16)
        vo_ref[...] = v_ref[...].astype(BF16)

    blk = lambda off: pl.BlockSpec((TM, D), lambda i: (i, off))
    gspec = pl.BlockSpec((1, HD), lambda i: (0, 0))
    return pl.pallas_call(
        body, name="prep_a", grid=(S // TM,),
        in_specs=[blk(0), blk(1), blk(2), gspec, gspec], out_specs=[blk(0)] * 3,
        out_shape=[_sds((S, D), BF16)] * 3, compiler_params=_params())(proj, proj, proj, gq, gk)


def _pick_lane(block, lane, idx):
    return jnp.sum(jnp.where(lane == idx, block, 0.0), axis=-1, keepdims=True)


def _fgate_fwd(proj, b_pad):
    def body(f_ref, b_ref, c_ref, cbc_ref, carry):
        @pl.when(pl.program_id(0) == 0)
        def _():
            carry[...] = jnp.zeros_like(carry)

        z = f_ref[...] + b_ref[...]
        logf = jnp.minimum(z, 0.0) - jnp.log1p(jnp.exp(-jnp.abs(z)))
        r = lax.broadcasted_iota(jnp.int32, (TM, TM), 0)
        c = lax.broadcasted_iota(jnp.int32, (TM, TM), 1)
        tri = (r >= c).astype(F32)
        loc = jnp.dot(tri, logf, precision=lax.Precision.HIGHEST, preferred_element_type=F32) + carry[0:1, :]
        c_ref[...] = loc
        carry[0:1, :] = loc[TM - 1:TM, :]
        lane = _lane_iota((TM, LANES))
        for h in range(N_HEADS):
            cbc_ref[:, LANES * h:LANES * (h + 1)] = jnp.broadcast_to(_pick_lane(loc, lane, h), (TM, LANES))

    return pl.pallas_call(
        body, name="fgate_fwd", grid=(S // TM,),
        in_specs=[pl.BlockSpec((TM, LANES), lambda i: (i, FOFF // LANES)), pl.BlockSpec((1, LANES), lambda i: (0, 0))],
        out_specs=[pl.BlockSpec((TM, LANES), lambda i: (i, 0)), pl.BlockSpec((TM, N_HEADS * LANES), lambda i: (i, 0))],
        out_shape=[_sds((S, LANES), F32), _sds((S, N_HEADS * LANES), F32)],
        scratch_shapes=[pltpu.VMEM((8, LANES), F32)], compiler_params=_params())(proj, b_pad)


def _key_le_query(offset, keys=TK):
    r = lax.broadcasted_iota(jnp.int32, (keys, TQ), 0)
    c = lax.broadcasted_iota(jnp.int32, (keys, TQ), 1)
    return (r + offset) <= c


def _widen(tile):
    return jnp.concatenate([tile] * (TQ // LANES), axis=1)


def _fox_fwd(qn, kn, vb, proj, crow, cbc, rider=()):
    nq, per = S // TQ, TQ // TK

    def body(q_ref, k_ref, v_ref, g_ref, cq_ref, cbc_ref, o_ref, z_ref, lse_ref, st_s, pt_s):
        i = pl.program_id(1)
        qs = [q_ref[:, HD * hh:HD * (hh + 1)] for hh in range(HPS)]
        cqs = [cq_ref[hh, 0] for hh in range(HPS)]

        def scores(s, hh):
            off = pl.multiple_of(s * KS, KS)
            kj = k_ref[pl.ds(off, KS), HD * hh:HD * (hh + 1)]
            return (_dot_nt(kj, qs[hh]) + cqs[hh]) - _widen(cbc_ref[pl.ds(off, KS), LANES * hh:LANES * (hh + 1)])

        def values(s, hh, pt):
            off = pl.multiple_of(s * KS, KS)
            return _dot_tn(v_ref[pl.ds(off, KS), HD * hh:HD * (hh + 1)], pt)

        def step(s, slot, carries, mask=None, last=False):
            if not last:
                for hh in range(HPS):
                    st_s[1 - slot, hh] = scores(s + 1, hh)
            pvs = [values(jnp.maximum(s - 1, 0), hh, pt_s[1 - slot, hh]) for hh in range(HPS)]
            out = []
            for hh in range(HPS):
                m, l, acc = carries[hh]
                st = st_s[slot, hh]
                if mask is not None:
                    st = jnp.where(mask, st, -jnp.inf)
                m_new = jnp.maximum(m, jnp.max(st, axis=0, keepdims=True))
                pt = jnp.exp(st - m_new)
                alpha = jnp.exp(m - m_new)
                pt_s[slot, hh] = pt.astype(BF16)
                out.append((m_new, alpha * l + jnp.sum(pt, axis=0, keepdims=True), alpha * (acc + pvs[hh])))
            return tuple(out)

        for hh in range(HPS):
            st_s[0, hh] = scores(0, hh)
            pt_s[1, hh] = jnp.zeros((KS, TQ), BF16)
        one = (jnp.full((1, TQ), -jnp.inf, F32), jnp.zeros((1, TQ), F32), jnp.zeros((HD, TQ), F32))
        carries = lax.fori_loop(0, i, lambda t, cr: step(2 * t + 1, 1, step(2 * t, 0, cr)), (one,) * HPS)
        carries = step(2 * i, 0, carries, mask=_key_le_query(0, KS))
        carries = step(2 * i + 1, 1, carries, mask=_key_le_query(KS, KS), last=True)
        accs = []
        for hh in range(HPS):
            m, l, acc = carries[hh]
            acc = acc + values(2 * i + 1, hh, pt_s[1, hh])
            accs.append(acc / l)
            lse_ref[hh, 0] = m + jnp.log(l)
        o = jnp.concatenate(accs, axis=0).T
        o_ref[...] = o
        g = g_ref[...]
        z_ref[...] = (o * (g * _sigmoid(g))).astype(BF16)

    qblk = pl.BlockSpec((TQ, HW), lambda hp, i: (i, hp))
    full = pl.BlockSpec((S, HW), lambda hp, i: (0, hp))
    rows = pl.BlockSpec((HPS, 1, 1, TQ), lambda hp, i: (hp, i, 0, 0))
    return _call(
        body, name="fox_fwd", args=(qn, kn, vb, proj, crow, cbc), grid=(N_HEADS // HPS, nq),
        in_specs=[qblk, full, full,
                  pl.BlockSpec((TQ, HW), lambda hp, i: (i, GOFF // HW + hp)),
                  rows, pl.BlockSpec((S, HPS * LANES), lambda hp, i: (0, hp))],
        out_specs=[qblk, qblk, rows],
        out_shape=[_sds((S, D), F32), _sds((S, D), BF16), _sds((N_HEADS, nq, 1, TQ), F32)],
        scratch_shapes=[pltpu.VMEM((2, HPS, KS, TQ), F32), pltpu.VMEM((2, HPS, KS, TQ), BF16)], rider=rider)


def _fox_bwd_dq(qn, kn, vb, dz, proj, o, lse, crow, cbc, rider=()):
    nq, per = S // TQ, TQ // TK

    def body(q_ref, k_ref, v_ref, dz_ref, g_ref, o_ref, lse_ref, cq_ref, cbc_ref,
             dq_ref, delta_ref, do_ref, dg_ref, dr_ref, st_s, dp_s, ds_s):
        i = pl.program_id(1)
        g = g_ref[...]
        sg = _sigmoid(g)
        dzv = dz_ref[...]
        ov = o_ref[...]
        do = dzv * (g * sg)
        dg_ref[...] = (dzv * ov * (sg * (1.0 + g * (1.0 - sg)))).astype(BF16)
        do_ref[...] = do.astype(BF16)
        prod_t = (do * ov).T
        qs = [q_ref[:, HD * hh:HD * (hh + 1)] for hh in range(HPS)]
        dobs = [do[:, HD * hh:HD * (hh + 1)].astype(BF16) for hh in range(HPS)]
        lses = [lse_ref[hh, 0] for hh in range(HPS)]
        cqs = [cq_ref[hh, 0] for hh in range(HPS)]
        deltas = [jnp.sum(prod_t[HD * hh:HD * (hh + 1), :], axis=0, keepdims=True) for hh in range(HPS)]
        for hh in range(HPS):
            delta_ref[hh, 0] = deltas[hh]

        def products(s, hh):
            off = pl.multiple_of(s * KS, KS)
            kj = k_ref[pl.ds(off, KS), HD * hh:HD * (hh + 1)]
            vj = v_ref[pl.ds(off, KS), HD * hh:HD * (hh + 1)]
            st = (_dot_nt(kj, qs[hh]) + cqs[hh]) - _widen(cbc_ref[pl.ds(off, KS), LANES * hh:LANES * (hh + 1)])
            return st, _dot_nt(vj, dobs[hh])

        def dq_of(s, hh, dst):
            off = pl.multiple_of(s * KS, KS)
            return _dot_tn(k_ref[pl.ds(off, KS), HD * hh:HD * (hh + 1)], dst)

        def step(s, slot, carries, mask=None, last=False):
            if not last:
                for hh in range(HPS):
                    st_s[1 - slot, hh], dp_s[1 - slot, hh] = products(s + 1, hh)
            dqs = [dq_of(jnp.maximum(s - 1, 0), hh, ds_s[1 - slot, hh]) for hh in range(HPS)]
            out = []
            for hh in range(HPS):
                dqt, dr = carries[hh]
                st = st_s[slot, hh]
                if mask is not None:
                    st = jnp.where(mask, st, -jnp.inf)
                dst = jnp.exp(st - lses[hh]) * (dp_s[slot, hh] - deltas[hh])
                ds_s[slot, hh] = dst.astype(BF16)
                out.append((dqt + dqs[hh], dr + jnp.sum(dst, axis=0, keepdims=True)))
            return tuple(out)

        for hh in range(HPS):
            st_s[0, hh], dp_s[0, hh] = products(0, hh)
            ds_s[1, hh] = jnp.zeros((KS, TQ), BF16)
        one = (jnp.zeros((HD, TQ), F32), jnp.zeros((1, TQ), F32))
        carries = lax.fori_loop(0, i, lambda t, cr: step(2 * t + 1, 1, step(2 * t, 0, cr)), (one,) * HPS)
        carries = step(2 * i, 0, carries, mask=_key_le_query(0, KS))
        carries = step(2 * i + 1, 1, carries, mask=_key_le_query(KS, KS), last=True)
        for hh in range(HPS):
            dr_ref[hh, 0] = carries[hh][1]
        dq_ref[...] = jnp.concatenate(
            [carries[hh][0] + dq_of(2 * i + 1, hh, ds_s[1, hh]) for hh in range(HPS)], axis=0).T

    qblk = pl.BlockSpec((TQ, HW), lambda hp, i: (i, hp))
    full = pl.BlockSpec((S, HW), lambda hp, i: (0, hp))
    rows = pl.BlockSpec((HPS, 1, 1, TQ), lambda hp, i: (hp, i, 0, 0))
    rows_shape = _sds((N_HEADS, nq, 1, TQ), F32)
    return _call(
        body, name="fox_bwd_dq", args=(qn, kn, vb, dz, proj, o, lse, crow, cbc), grid=(N_HEADS // HPS, nq),
        in_specs=[qblk, full, full, qblk,
                  pl.BlockSpec((TQ, HW), lambda hp, i: (i, GOFF // HW + hp)),
                  qblk, rows, rows, pl.BlockSpec((S, HPS * LANES), lambda hp, i: (0, hp))],
        out_specs=[qblk, rows, qblk, qblk, rows],
        out_shape=[_sds((S, D), F32), rows_shape, _sds((S, D), BF16), _sds((S, D), BF16), rows_shape],
        scratch_shapes=[pltpu.VMEM((2, HPS, KS, TQ), F32), pltpu.VMEM((2, HPS, KS, TQ), F32),
                        pltpu.VMEM((2, HPS, KS, TQ), BF16)], rider=rider)


def _fox_bwd_dkv(qn, kn, vb, dob, lse, delta, crow, cbc):
    nq, per = S // TQ, TQ // TK

    def body(q_ref, k_ref, v_ref, do_ref, lse_ref, del_ref, cq_ref, cbc_ref, dk_ref, dv_ref, dcs_ref,
             st_s, dp_s, pt_s, ds_s):
        j = pl.program_id(1)
        kjs = [k_ref[:, HD * hh:HD * (hh + 1)] for hh in range(HPS)]
        vjs = [v_ref[:, HD * hh:HD * (hh + 1)] for hh in range(HPS)]

        def rows_of(ref, u, hh):
            off = pl.multiple_of(u * TQ, TQ)
            return ref[pl.ds(off, TQ), HD * hh:HD * (hh + 1)]

        def products(u, hh):
            st = (_dot_nt(kjs[hh], rows_of(q_ref, u, hh)) + cq_ref[hh, u]) - _widen(
                cbc_ref[:, LANES * hh:LANES * (hh + 1)])
            return st, _dot_nt(vjs[hh], rows_of(do_ref, u, hh))

        def step(u, slot, carries, masked=False):
            nxt = jnp.minimum(u + 1, nq - 1)
            for hh in range(HPS):
                st_s[1 - slot, hh], dp_s[1 - slot, hh] = products(nxt, hh)
            prev = jnp.maximum(u - 1, 0)
            dvs = [_dot_nn(pt_s[1 - slot, hh], rows_of(do_ref, prev, hh)) for hh in range(HPS)]
            dks = [_dot_nn(ds_s[1 - slot, hh], rows_of(q_ref, prev, hh)) for hh in range(HPS)]
            out = []
            for hh in range(HPS):
                dk, dv, dcs = carries[hh]
                st = st_s[slot, hh]
                if masked:
                    st = jnp.where(_key_le_query((j - u) * TQ), st, -jnp.inf)
                pt = jnp.exp(st - lse_ref[hh, u])
                dst = pt * (dp_s[slot, hh] - del_ref[hh, u])
                pt_s[slot, hh] = pt.astype(BF16)
                ds_s[slot, hh] = dst.astype(BF16)
                out.append((dk + dks[hh], dv + dvs[hh], dcs + (dst[:, :LANES] + dst[:, LANES:])))
            return tuple(out)

        t0 = j // 2
        for hh in range(HPS):
            st_s[0, hh], dp_s[0, hh] = products(2 * t0, hh)
            pt_s[1, hh] = jnp.zeros((TK, TQ), BF16)
            ds_s[1, hh] = jnp.zeros((TK, TQ), BF16)
        one = (jnp.zeros((TK, HD), F32), jnp.zeros((TK, HD), F32), jnp.zeros((TK, LANES), F32))
        carries = step(2 * t0 + 1, 1, step(2 * t0, 0, (one,) * HPS, masked=True), masked=True)
        carries = lax.fori_loop(t0 + 1, nq // 2, lambda t, cr: step(2 * t + 1, 1, step(2 * t, 0, cr)), carries)
        dks, dvs = [], []
        for hh in range(HPS):
            dk, dv, dcs = carries[hh]
            dks.append(dk + _dot_nn(ds_s[1, hh], rows_of(q_ref, nq - 1, hh)))
            dvs.append(dv + _dot_nn(pt_s[1, hh], rows_of(do_ref, nq - 1, hh)))
            dcs_ref[:, LANES * hh:LANES * (hh + 1)] = jnp.broadcast_to(
                -jnp.sum(dcs, axis=-1, keepdims=True), (TK, LANES))
        dk_ref[...] = jnp.concatenate(dks, axis=-1)
        dv_ref[...] = jnp.concatenate(dvs, axis=-1)

    kblk = pl.BlockSpec((TK, HW), lambda hp, j: (j, hp))
    full = pl.BlockSpec((S, HW), lambda hp, j: (0, hp))
    rows = pl.BlockSpec((HPS, nq, 1, TQ), lambda hp, j: (hp, 0, 0, 0))
    cblk = pl.BlockSpec((TK, HPS * LANES), lambda hp, j: (j, hp))
    return pl.pallas_call(
        body, name="fox_bwd_dkv", grid=(N_HEADS // HPS, S // TK),
        in_specs=[full, kblk, kblk, full, rows, rows, rows, cblk],
        out_specs=[kblk, kblk, cblk],
        out_shape=[_sds((S, D), F32), _sds((S, D), F32), _sds((S, N_HEADS * LANES), F32)],
        scratch_shapes=[pltpu.VMEM((2, HPS, TK, TQ), F32), pltpu.VMEM((2, HPS, TK, TQ), F32),
                        pltpu.VMEM((2, HPS, TK, TQ), BF16), pltpu.VMEM((2, HPS, TK, TQ), BF16)],
        compiler_params=_params())(qn, kn, vb, dob, lse, delta, crow, cbc)


def _prep_a_bwd(dq, dk, dv, dgate, drow, dcs, proj, b_pad, gq, gk):
    nt = S // TM

    def body(dq_ref, dk_ref, dv_ref, dgt_ref, dr_ref, dcs_ref, xq_ref, xk_ref, f_ref, b_ref, gq_ref, gk_ref,
             o_ref, dgq_ref, dgk_ref, db_ref, carry):
        @pl.when(pl.program_id(0) == 0)
        def _():
            carry[...] = jnp.zeros_like(carry)
            dgq_ref[...] = jnp.zeros_like(dgq_ref)
            dgk_ref[...] = jnp.zeros_like(dgk_ref)
            db_ref[...] = jnp.zeros_like(db_ref)

        lane = _lane_iota((TM, LANES))
        lo_half = lane < HD
        gq2, gk2 = _g2(gq_ref), _g2(gk_ref)
        dgq, dgk = jnp.zeros((1, LANES), F32), jnp.zeros((1, LANES), F32)
        for c in _pairs(D):
            dxq, dg = _head_norm_bwd(dq_ref[:, c] * QSCALE, xq_ref[:, c], gq2, lo_half)
            o_ref[:, c] = dxq.astype(BF16)
            dgq = dgq + dg
            dxk, dg = _head_norm_bwd(dk_ref[:, c], xk_ref[:, c], gk2, lo_half)
            o_ref[:, D + c.start:D + c.stop] = dxk.astype(BF16)
            dgk = dgk + dg
        dgq_ref[...] += dgq
        dgk_ref[...] += dgk
        o_ref[:, 2 * D:3 * D] = dv_ref[...].astype(BF16)
        o_ref[:, GOFF:GOFF + D] = dgt_ref[...]

        dc = dr_ref[...]
        for h in range(N_HEADS):
            dc = dc + jnp.where(lane == h, dcs_ref[:, LANES * h:LANES * h + 1], 0.0)
        r = lax.broadcasted_iota(jnp.int32, (TM, TM), 0)
        c = lax.broadcasted_iota(jnp.int32, (TM, TM), 1)
        tri = (c >= r).astype(F32)
        dlogf = jnp.dot(tri, dc, precision=lax.Precision.HIGHEST, preferred_element_type=F32) + carry[0:1, :]
        carry[0:1, :] = dlogf[0:1, :]
        df = dlogf * (1.0 / (1.0 + jnp.exp(f_ref[...] + b_ref[...])))
        db_ref[...] += jnp.sum(df, axis=0, keepdims=True)
        o_ref[:, FOFF:FOFF + LANES] = df.astype(BF16)
        o_ref[:, FOFF + LANES:NA] = jnp.zeros((TM, NA - FOFF - LANES), BF16)

    rev = lambda width, col: pl.BlockSpec((TM, width), lambda i: (nt - 1 - i, col))
    gspec = pl.BlockSpec((1, HD), lambda i: (0, 0))
    acc = pl.BlockSpec((1, LANES), lambda i: (0, 0))
    return pl.pallas_call(
        body, name="prep_a_bwd", grid=(nt,),
        in_specs=[rev(D, 0), rev(D, 0), rev(D, 0), rev(D, 0), rev(LANES, 0), rev(N_HEADS * LANES, 0),
                  rev(D, 0), rev(D, 1), rev(LANES, FOFF // LANES), acc, gspec, gspec],
        out_specs=[rev(NA, 0), acc, acc, acc],
        out_shape=[_sds((S, NA), BF16)] + [_sds((1, LANES), F32)] * 3,
        scratch_shapes=[pltpu.VMEM((8, LANES), F32)],
        compiler_params=_params())(dq, dk, dv, dgate, drow, dcs, proj, proj, proj, b_pad, gq, gk)


def _prep_b(pb, gq, cos2, sin2):
    def body(x_ref, g_ref, c_ref, s_ref, o_ref):
        lane = _lane_iota((TM, LANES))
        g2, cos, sin = _g2(g_ref), c_ref[...], s_ref[...]
        for c in _pairs(D):
            x = x_ref[:, c]
            xn = (x * _head_rinv(x, lane < HD)) * g2
            o_ref[:, c] = (_rope_fwd(xn, cos, sin, lane) * QSCALE).astype(BF16)

    blk = pl.BlockSpec((TM, D), lambda i: (i, 0))
    tab = pl.BlockSpec((TM, LANES), lambda i: (i, 0))
    return pl.pallas_call(
        body, name="prep_b", grid=(S // TM,),
        in_specs=[blk, pl.BlockSpec((1, HD), lambda i: (0, 0)), tab, tab], out_specs=blk,
        out_shape=_sds((S, D), BF16), compiler_params=_params())(pb, gq, cos2, sin2)


def _prep_kv(kv, gk, cos2, sin2):
    def body(k_ref, v_ref, g_ref, c_ref, s_ref, ko_ref, vo_ref):
        lane = _lane_iota((RB, LANES))
        g2, cos, sin = _g2(g_ref), c_ref[...], s_ref[...]
        for c in _pairs(CB):
            x = k_ref[:, c]
            xn = (x * _head_rinv(x, lane < HD)) * g2
            ko_ref[:, c] = _rope_fwd(xn, cos, sin, lane).astype(BF16)
        vo_ref[...] = v_ref[...].astype(BF16)

    blk = lambda off: pl.BlockSpec((RB, CB), lambda i: (i, off))
    tab = pl.BlockSpec((RB, LANES), lambda i: (i, 0))
    return pl.pallas_call(
        body, name="prep_kv", grid=(S // RB,),
        in_specs=[blk(0), blk(1), pl.BlockSpec((1, HD), lambda i: (0, 0)), tab, tab],
        out_specs=[blk(0), blk(0)], out_shape=[_sds((S, CB), BF16)] * 2,
        compiler_params=_params())(kv, kv, gk, cos2, sin2)


def _swa_mask(n):
    r = lax.broadcasted_iota(jnp.int32, (4 * WIN, 2 * WIN), 0) & (WIN - 1)
    c = lax.broadcasted_iota(jnp.int32, (4 * WIN, 2 * WIN), 1)
    return (c > r) & (c <= r + WIN) & ((c >= WIN) | (n > 0))


def _stack4(ref_or_val, base):
    return jnp.concatenate([ref_or_val[:, base + HD * g: base + HD * (g + 1)] for g in range(4)], axis=0)


def _sink_col(s_ref, first):
    r = lax.broadcasted_iota(jnp.int32, (4 * WIN, 1), 0)
    col = jnp.full((4 * WIN, 1), s_ref[first + 3], F32)
    for g in range(2, -1, -1):
        col = jnp.where(r < WIN * (g + 1), s_ref[first + g], col)
    return col


def _swa_fwd(qb, ksh, vsh, pb, sinks):
    nb = S // WIN

    def body(q_ref, kp_ref, kc_ref, vp_ref, vc_ref, g_ref, s_ref, o_ref, z_ref, lse_ref):
        kp, n = pl.program_id(0), pl.program_id(1)
        valid = _swa_mask(n)
        outs, lses = [], []
        for kh in range(2):
            lo = HD * kh
            kb = jnp.concatenate([kp_ref[:, lo:lo + HD], kc_ref[:, lo:lo + HD]], axis=0)
            vb = jnp.concatenate([vp_ref[:, lo:lo + HD], vc_ref[:, lo:lo + HD]], axis=0)
            qs = _stack4(q_ref, 4 * HD * kh)
            s = jnp.where(valid, _dot_nt(qs, kb), -jnp.inf)
            sink = _sink_col(s_ref, (2 * kp + kh) * 4)
            m = jnp.maximum(jnp.max(s, axis=-1, keepdims=True), sink)
            p = jnp.exp(s - m)
            l = jnp.sum(p, axis=-1, keepdims=True) + jnp.exp(sink - m)
            os_ = _dot_nn(p.astype(BF16), vb) / l
            lse = m + jnp.log(l)
            for g in range(4):
                outs.append(os_[WIN * g:WIN * (g + 1), :])
                lses.append(jnp.broadcast_to(lse[WIN * g:WIN * (g + 1), :], (WIN, HD)))
        o = jnp.concatenate(outs, axis=-1)
        o_ref[...] = o
        g = g_ref[...]
        z_ref[...] = (o * (g * _sigmoid(g))).astype(BF16)
        lse_ref[...] = jnp.concatenate(lses, axis=-1)

    qblk = pl.BlockSpec((WIN, 512), lambda kp, n: (n, kp))
    prev = pl.BlockSpec((WIN, LANES), lambda kp, n: (jnp.maximum(n - 1, 0), kp))
    cur = pl.BlockSpec((WIN, LANES), lambda kp, n: (n, kp))
    return pl.pallas_call(
        body, name="swa_fwd", grid=(2, nb),
        in_specs=[qblk, prev, cur, prev, cur, pl.BlockSpec((WIN, 512), lambda kp, n: (n, 2 + kp)),
                  pl.BlockSpec(memory_space=pltpu.SMEM)],
        out_specs=[qblk, qblk, qblk],
        out_shape=[_sds((S, D), F32), _sds((S, D), BF16), _sds((S, D), F32)],
        compiler_params=_params())(qb, ksh, ksh, vsh, vsh, pb, sinks)


def _swa_bwd(qb, ksh, vsh, dz, o, lse, pb, sinks):
    nb = S // WIN

    def body(q_ref, kp_ref, kc_ref, vp_ref, vc_ref, dz_ref, o_ref, lse_ref, g_ref, s_ref,
             dq_ref, dg_ref, dka_ref, dkb_ref, dva_ref, dvb_ref, dsink_ref):
        kp, n = pl.program_id(0), pl.program_id(1)

        @pl.when((kp == 0) & (n == 0))
        def _():
            dsink_ref[...] = jnp.zeros_like(dsink_ref)

        valid = _swa_mask(n)
        g = g_ref[...]
        sg = _sigmoid(g)
        dzv = dz_ref[...]
        ov = o_ref[...]
        do = dzv * (g * sg)
        dg_ref[...] = (dzv * ov * (sg * (1.0 + g * (1.0 - sg)))).astype(BF16)
        prod = do * ov
        lane1 = _lane_iota((1, LANES))
        dqs, dkas, dkbs, dvas, dvbs = [], [], [], [], []
        dsink = jnp.zeros((1, LANES), F32)
        for kh in range(2):
            lo = HD * kh
            kb = jnp.concatenate([kp_ref[:, lo:lo + HD], kc_ref[:, lo:lo + HD]], axis=0)
            vb = jnp.concatenate([vp_ref[:, lo:lo + HD], vc_ref[:, lo:lo + HD]], axis=0)
            base = 4 * HD * kh
            qs = _stack4(q_ref, base)
            dos = _stack4(do, base).astype(BF16)
            delta = jnp.sum(_stack4(prod, base), axis=-1, keepdims=True)
            lse_s = jnp.concatenate([lse_ref[:, base + HD * gg: base + HD * gg + 1] for gg in range(4)], axis=0)
            s = jnp.where(valid, _dot_nt(qs, kb), -jnp.inf)
            p = jnp.exp(s - lse_s)
            ds = p * (_dot_nt(dos, vb) - delta)
            dsb = ds.astype(BF16)
            dqst = _dot_nn(dsb, kb)
            dkband = _dot_tn(dsb, qs)
            dvband = _dot_tn(p.astype(BF16), dos)
            for gg in range(4):
                dqs.append(dqst[WIN * gg:WIN * (gg + 1), :])
            dkbs.append(dkband[0:WIN, :])
            dkas.append(dkband[WIN:2 * WIN, :])
            dvbs.append(dvband[0:WIN, :])
            dvas.append(dvband[WIN:2 * WIN, :])
            first = (2 * kp + kh) * 4
            ps_delta = jnp.exp(_sink_col(s_ref, first) - lse_s) * delta
            for gg in range(4):
                val = jnp.sum(ps_delta[WIN * gg:WIN * (gg + 1), :], axis=0, keepdims=True)
                dsink = dsink - jnp.where(lane1 == first + gg, val, 0.0)
        dq_ref[...] = jnp.concatenate(dqs, axis=-1)
        dka_ref[...] = jnp.concatenate(dkas, axis=-1)
        dkb_ref[...] = jnp.concatenate(dkbs, axis=-1)
        dva_ref[...] = jnp.concatenate(dvas, axis=-1)
        dvb_ref[...] = jnp.concatenate(dvbs, axis=-1)
        dsink_ref[...] += dsink

    qblk = pl.BlockSpec((WIN, 512), lambda kp, n: (n, kp))
    prev = pl.BlockSpec((WIN, LANES), lambda kp, n: (jnp.maximum(n - 1, 0), kp))
    cur = pl.BlockSpec((WIN, LANES), lambda kp, n: (n, kp))
    return pl.pallas_call(
        body, name="swa_bwd", grid=(2, nb),
        in_specs=[qblk, prev, cur, prev, cur, qblk, qblk, qblk,
                  pl.BlockSpec((WIN, 512), lambda kp, n: (n, 2 + kp)), pl.BlockSpec(memory_space=pltpu.SMEM)],
        out_specs=[qblk, qblk, cur, cur, cur, cur, pl.BlockSpec((1, LANES), lambda kp, n: (0, 0))],
        out_shape=[_sds((S, D), F32), _sds((S, D), BF16)] + [_sds((S, 256), F32)] * 4 + [_sds((1, LANES), F32)],
        compiler_params=_params())(qb, ksh, ksh, vsh, vsh, dz, o, lse, pb, sinks)


def _prep_b_bwd(dq, dgate, pb, gq, cos2, sin2):
    def body(dq_ref, dgt_ref, x_ref, g_ref, c_ref, s_ref, o_ref, dgq_ref):
        @pl.when(pl.program_id(0) == 0)
        def _():
            dgq_ref[...] = jnp.zeros_like(dgq_ref)

        lane = _lane_iota((TM, LANES))
        g2, cos, sin = _g2(g_ref), c_ref[...], s_ref[...]
        dg_tot = jnp.zeros((1, LANES), F32)
        for c in _pairs(D):
            dn = _rope_bwd(dq_ref[:, c] * QSCALE, cos, sin, lane)
            dx, dg = _head_norm_bwd(dn, x_ref[:, c], g2, lane < HD)
            o_ref[:, c] = dx.astype(BF16)
            dg_tot = dg_tot + dg
        dgq_ref[...] += dg_tot
        o_ref[:, D:2 * D] = dgt_ref[...]

    row = pl.BlockSpec((TM, D), lambda i: (i, 0))
    tab = pl.BlockSpec((TM, LANES), lambda i: (i, 0))
    return pl.pallas_call(
        body, name="prep_b_bwd", grid=(S // TM,),
        in_specs=[row, row, row, pl.BlockSpec((1, HD), lambda i: (0, 0)), tab, tab],
        out_specs=[pl.BlockSpec((TM, 2 * D), lambda i: (i, 0)), pl.BlockSpec((1, LANES), lambda i: (0, 0))],
        out_shape=[_sds((S, 2 * D), BF16), _sds((1, LANES), F32)],
        compiler_params=_params())(dq, dgate, pb, gq, cos2, sin2)


def _prep_kv_bwd(dka, dkb, dva, dvb, kv, gk, cos2, sin2):
    nt = S // RB
    per = RB // WIN

    def shifted(cur_ref, nxt_ref, has_next):
        return jnp.concatenate([cur_ref[WIN:RB, :], jnp.where(has_next, nxt_ref[...], 0.0)], axis=0)

    def body(dka_ref, dkb_ref, dkn_ref, dva_ref, dvb_ref, dvn_ref, x_ref, g_ref, c_ref, s_ref, o_ref, dgk_ref):
        i, j = pl.program_id(0), pl.program_id(1)

        @pl.when((i == 0) & (j == 0))
        def _():
            dgk_ref[...] = jnp.zeros_like(dgk_ref)

        has_next = i < nt - 1

        @pl.when(j == 0)
        def _():
            lane = _lane_iota((RB, LANES))
            g2, cos, sin = _g2(g_ref), c_ref[...], s_ref[...]
            dy_all = dka_ref[...] + shifted(dkb_ref, dkn_ref, has_next)
            dg_tot = jnp.zeros((1, LANES), F32)
            for c in _pairs(CB):
                dn = _rope_bwd(dy_all[:, c], cos, sin, lane)
                dx, dg = _head_norm_bwd(dn, x_ref[:, c], g2, lane < HD)
                o_ref[:, c] = dx.astype(BF16)
                dg_tot = dg_tot + dg
            dgk_ref[...] += dg_tot

        @pl.when(j == 1)
        def _():
            o_ref[...] = (dva_ref[...] + shifted(dvb_ref, dvn_ref, has_next)).astype(BF16)

    cur = pl.BlockSpec((RB, CB), lambda i, j: (i, 0))
    nxt = pl.BlockSpec((WIN, CB), lambda i, j: (jnp.minimum(per * (i + 1), S // WIN - 1), 0))
    tab = pl.BlockSpec((RB, LANES), lambda i, j: (i, 0))
    return pl.pallas_call(
        body, name="prep_kv_bwd", grid=(nt, 2),
        in_specs=[cur, cur, nxt, cur, cur, nxt, cur, pl.BlockSpec((1, HD), lambda i, j: (0, 0)), tab, tab],
        out_specs=[pl.BlockSpec((RB, CB), lambda i, j: (i, j)), pl.BlockSpec((1, LANES), lambda i, j: (0, 0))],
        out_shape=[_sds((S, 2 * CB), BF16), _sds((1, LANES), F32)],
        compiler_params=_params())(dka, dkb, dkb, dva, dvb, dvb, kv, gk, cos2, sin2)


def _loss_dy(y, tgt):
    def body(y_ref, t_ref, dy_ref, l_ref):
        @pl.when(pl.program_id(0) == 0)
        def _():
            l_ref[...] = jnp.zeros_like(l_ref)

        e = y_ref[...] - t_ref[...]
        dy_ref[...] = e * (1.0 / D)
        l_ref[...] += jnp.sum(jnp.sum(e * e, axis=-1, keepdims=True), axis=0, keepdims=True)

    row = pl.BlockSpec((TM, D), lambda i: (i, 0))
    return pl.pallas_call(
        body, name="loss_dy", grid=(S // TM,), in_specs=[row, row],
        out_specs=[row, pl.BlockSpec((1, LANES), lambda i: (0, 0))],
        out_shape=[_sds((S, D), F32), _sds((1, LANES), F32)], compiler_params=_params())(y, tgt)


def _rms2_bwd(du_b, du_kv, h1, g_b, g_kv, dy):
    def body(dub_ref, dukv_ref, x_ref, gb_ref, gkv_ref, dy_ref, dh_ref, dgb_ref, dgkv_ref):
        @pl.when(pl.program_id(0) == 0)
        def _():
            dgb_ref[...] = jnp.zeros_like(dgb_ref)
            dgkv_ref[...] = jnp.zeros_like(dgkv_ref)

        x = x_ref[...]
        dx1, dg1 = _rms_bwd_core(dub_ref[...], x, gb_ref[...])
        dx2, dg2 = _rms_bwd_core(dukv_ref[...], x, gkv_ref[...])
        dh_ref[...] = dy_ref[...] + dx1 + dx2
        dgb_ref[...] += dg1
        dgkv_ref[...] += dg2

    row = pl.BlockSpec((TM, D), lambda i: (i, 0))
    vec = pl.BlockSpec((1, D), lambda i: (0, 0))
    return pl.pallas_call(
        body, name="rms2_bwd", grid=(S // TM,), in_specs=[row, row, row, vec, vec, row],
        out_specs=[row, vec, vec], out_shape=[_sds((S, D), F32), _sds((1, D), F32), _sds((1, D), F32)],
        compiler_params=_params())(du_b, du_kv, h1, g_b, g_kv, dy)


def _rms_bwd(du, x, g, dres):
    def body(du_ref, x_ref, g_ref, dr_ref, dx_ref, dg_ref):
        @pl.when(pl.program_id(0) == 0)
        def _():
            dg_ref[...] = jnp.zeros_like(dg_ref)

        dx, dg = _rms_bwd_core(du_ref[...], x_ref[...], g_ref[...])
        dx_ref[...] = dr_ref[...] + dx
        dg_ref[...] += dg

    row = pl.BlockSpec((TM, D), lambda i: (i, 0))
    vec = pl.BlockSpec((1, D), lambda i: (0, 0))
    return pl.pallas_call(
        body, name="rms_bwd", grid=(S // TM,), in_specs=[row, row, vec, row], out_specs=[row, vec],
        out_shape=[_sds((S, D), F32), _sds((1, D), F32)], compiler_params=_params())(du, x, g, dres)


def _gather_first(w_in_a, w_out_a, w_kv, w_in_b, w_out_b, norm_a_g):
    def body(wia_ref, woa_ref, wkv_ref, wib_ref, wob_ref, ga_ref,
             wa_g, ga_g, woa_s, wkv_s, wib_s, wob_s, wa_s, *sems):
        wa_s[...] = wia_ref[0].astype(BF16)
        woa_s[...] = woa_ref[0].astype(BF16)
        wkv_s[...] = wkv_ref[...].astype(BF16)
        wib_s[...] = wib_ref[0].astype(BF16)
        wob_s[...] = wob_ref[0].astype(BF16)
        _gather_two_level([wa_s, ga_ref], [wa_g, ga_g], sems)

    vmem = pl.BlockSpec(memory_space=pltpu.VMEM)
    anyspec = pl.BlockSpec(memory_space=pl.ANY)
    shard = lambda w: _sds(w.shape[-2:], BF16)
    return pl.pallas_call(
        body, name="gather_first", in_specs=[vmem] * 6, out_specs=[anyspec, anyspec, vmem, vmem, vmem, vmem],
        out_shape=[_sds((N_DEV,) + w_in_a.shape[-2:], BF16), _sds((N_DEV,) + norm_a_g.shape, F32),
                   shard(w_out_a), shard(w_kv), shard(w_in_b), shard(w_out_b)],
        scratch_shapes=[pltpu.VMEM(w_in_a.shape[-2:], BF16)] + _exchange_sems(2),
        compiler_params=pltpu.CompilerParams(vmem_limit_bytes=VMEM_LIMIT, has_side_effects=True))(
            w_in_a, w_out_a, w_kv, w_in_b, w_out_b, norm_a_g)


def _pair_reduce(slots):
    n_chip = N_DEV // 2
    _, rows, cols = slots.shape

    def body(s_ref, o_ref, own_v, sib_v, send_sems, recv_sems, local_sems):
        x, y, c = lax.axis_index("x"), lax.axis_index("y"), lax.axis_index("c")
        copies = []
        for j in range(n_chip):
            own = pltpu.make_async_copy(s_ref.at[2 * j + c], own_v.at[j], local_sems.at[j])
            give = pltpu.make_async_remote_copy(
                src_ref=s_ref.at[2 * j + 1 - c], dst_ref=sib_v.at[j], send_sem=send_sems.at[j],
                recv_sem=recv_sems.at[j], device_id=(x, y, 1 - c), device_id_type=pl.DeviceIdType.MESH)
            own.start()
            give.start()
            copies.append((own, give))
        for j, (own, give) in enumerate(copies):
            own.wait()
            give.wait()
            o_ref[j] = (own_v[j].astype(F32) + sib_v[j].astype(F32)).astype(BF16)

    half = _sds((n_chip, rows, cols), slots.dtype)
    return pl.pallas_call(
        body, name="pair_reduce", in_specs=[pl.BlockSpec(memory_space=pl.ANY)],
        out_specs=pl.BlockSpec(memory_space=pltpu.VMEM), out_shape=half,
        scratch_shapes=[pltpu.VMEM(half.shape, half.dtype), pltpu.VMEM(half.shape, half.dtype),
                        pltpu.SemaphoreType.DMA((n_chip,)), pltpu.SemaphoreType.DMA((n_chip,)),
                        pltpu.SemaphoreType.DMA((n_chip,))],
        compiler_params=pltpu.CompilerParams(vmem_limit_bytes=VMEM_LIMIT, has_side_effects=True))(slots)


def _padded_col(c):
    if c < RAW_F:
        return c
    return FOFF + (c - RAW_F) if c < RAW_G else GOFF + (c - RAW_G)


def _shard_pieces():
    width = NA_RAW // N_DEV
    pieces = []
    for d in range(N_DEV):
        cuts = [width * d] + [c for c in (RAW_F, RAW_G) if width * d < c < width * (d + 1)] + [width * (d + 1)]
        for lo, hi in zip(cuts[:-1], cuts[1:]):
            pieces.append((d, lo - width * d, _padded_col(lo), hi - lo))
    return pieces


def _unshard_wa(wa_g):
    def body(w_ref, o_ref):
        o_ref[:, FOFF + N_HEADS:NA] = jnp.zeros((TM, NA - FOFF - N_HEADS), BF16)
        for d, src, dst, width in _shard_pieces():
            o_ref[:, dst:dst + width] = w_ref[d, :, src:src + width]

    return pl.pallas_call(
        body, name="unshard_wa", grid=(D // TM,),
        in_specs=[pl.BlockSpec((N_DEV, TM, NA_RAW // N_DEV), lambda i: (0, i, 0))],
        out_specs=pl.BlockSpec((TM, NA), lambda i: (i, 0)), out_shape=_sds((D, NA), BF16),
        compiler_params=_params())(wa_g)


def _reshard_dwa(dwa):
    def body(g_ref, o_ref):
        for d, src, dst, width in _shard_pieces():
            o_ref[d, :, src:src + width] = g_ref[:, dst:dst + width]

    return pl.pallas_call(
        body, name="reshard_dwa", grid=(D // TM,), in_specs=[pl.BlockSpec((TM, NA), lambda i: (i, 0))],
        out_specs=pl.BlockSpec((N_DEV, TM, NA_RAW // N_DEV), lambda i: (0, i, 0)),
        out_shape=_sds((N_DEV, D, NA_RAW // N_DEV), dwa.dtype), compiler_params=_params())(dwa)


def _gather_slab(slab):
    def body(s_ref, o_ref, *sems):
        _exchange_ops(["gather_rows"], [s_ref], [o_ref], sems, True, True)

    anyspec = pl.BlockSpec(memory_space=pl.ANY)
    return pl.pallas_call(
        body, name="gather_slab", in_specs=[anyspec], out_specs=anyspec,
        out_shape=_sds((N_DEV,) + slab.shape, slab.dtype), scratch_shapes=_exchange_sems(1),
        compiler_params=pltpu.CompilerParams(has_side_effects=True))(slab)


def _adamw(w, g, m, v):
    m = ADAM_B1 * m + (1.0 - ADAM_B1) * g
    v = ADAM_B2 * v + (1.0 - ADAM_B2) * (g * g)
    m_hat = m / (1.0 - ADAM_B1 ** ADAM_STEP)
    v_hat = v / (1.0 - ADAM_B2 ** ADAM_STEP)
    delta = -ADAM_LR * (m_hat / (jnp.sqrt(v_hat) + ADAM_EPS) + ADAM_WD * w)
    return delta, m, v


def _sum_adamw(recv, w, m, v, name):
    lead = w.ndim - 2
    rows, cols = w.shape[-2:]
    tr = 128

    n_slots = recv.shape[0]

    def body(r_ref, w_ref, m_ref, v_ref, g_ref, d_ref, nm_ref, nv_ref):
        g = r_ref[0].astype(F32)
        for slot in range(1, n_slots):
            g = g + r_ref[slot].astype(F32)
        g_ref[...] = g
        d_ref[...], nm_ref[...], nv_ref[...] = _adamw(w_ref[...], g, m_ref[...], v_ref[...])

    blk = pl.BlockSpec((None,) * lead + (tr, cols), lambda i: (0,) * lead + (i, 0))
    return pl.pallas_call(
        body, name=name, grid=(rows // tr,),
        in_specs=[pl.BlockSpec((n_slots, tr, cols), lambda i: (0, i, 0)), blk, blk, blk],
        out_specs=[blk] * 4, out_shape=[_sds(w.shape, F32)] * 4,
        compiler_params=_params())(recv, w, m, v)


SLAB_ROWS = 16
SLOT = {"kv_norm_g": (8, 0, D), "norm_b_g": (9, 0, D), "b_forget": (10, 0, 16), "qnorm_a_g": (10, 128, HD),
        "knorm_a_g": (10, 256, HD), "knorm_b_g": (10, 384, HD), "qnorm_b_g": (10, 512, HD), "sinks": (10, 640, 16)}
SMALL = ["norm_a_g", "b_forget", "qnorm_a_g", "knorm_a_g", "kv_norm_g", "knorm_b_g", "norm_b_g", "qnorm_b_g", "sinks"]


LOSS_ROW = 11


def _pack_small(dg_a, dg_kv, dg_b, db_f, dgq_a, dgk_a, dgk_b, dgq_b, dsinks, lsum):
    def fold(ref):
        return ref[:, 0:HD] + ref[:, HD:2 * HD]

    def body(dga_ref, dgkv_ref, dgb_ref, dbf_ref, dgqa_ref, dgka_ref, dgkb_ref, dgqb_ref, dsk_ref, ls_ref, slab_ref):
        slab_ref[...] = jnp.zeros_like(slab_ref)
        for r in range(N_DEV):
            slab_ref[r:r + 1, 0:LANES] = dga_ref[:, LANES * r:LANES * (r + 1)]
        slab_ref[8:9, :] = dgkv_ref[...]
        slab_ref[9:10, :] = dgb_ref[...]
        slab_ref[10:11, 0:LANES] = dbf_ref[...]
        slab_ref[10:11, 128:128 + HD] = fold(dgqa_ref)
        slab_ref[10:11, 256:256 + HD] = fold(dgka_ref)
        slab_ref[10:11, 384:384 + HD] = fold(dgkb_ref)
        slab_ref[10:11, 512:512 + HD] = fold(dgqb_ref)
        slab_ref[10:11, 640:640 + LANES] = dsk_ref[...]
        slab_ref[LOSS_ROW:LOSS_ROW + 1, 0:LANES] = ls_ref[...]

    return pl.pallas_call(body, name="pack_small", out_shape=_sds((SLAB_ROWS, D), F32), compiler_params=_params())(
        dg_a, dg_kv, dg_b, db_f, dgq_a, dgk_a, dgk_b, dgq_b, dsinks, lsum)


def _small_adamw(recv, ws, ms, vs):
    k = len(SMALL)

    def body(*refs):
        r_ref = refs[0]
        w_refs, m_refs, v_refs = refs[1:1 + k], refs[1 + k:1 + 2 * k], refs[1 + 2 * k:1 + 3 * k]
        outs = refs[1 + 3 * k:1 + 7 * k]
        loss_ref, tot = refs[1 + 7 * k], refs[2 + 7 * k]
        g = r_ref[0]
        for dev in range(1, N_DEV):
            g = g + r_ref[dev]
        tot[...] = g
        loss_ref[...] = tot[LOSS_ROW:LOSS_ROW + 1, 0:LANES] * (0.5 / D)
        me = 4 * lax.axis_index("x") + 2 * lax.axis_index("y") + lax.axis_index("c")
        for p, name in enumerate(SMALL):
            if name == "norm_a_g":
                mine = lax.broadcasted_iota(jnp.int32, (N_DEV, LANES), 0) == me
                gp = jnp.sum(jnp.where(mine, tot[0:N_DEV, 0:LANES], 0.0), axis=0, keepdims=True)
            else:
                row, lo, width = SLOT[name]
                gp = tot[row:row + 1, lo:lo + width]
            d, nm, nv = _adamw(w_refs[p][...], gp, m_refs[p][...], v_refs[p][...])
            outs[p][...] = gp
            outs[k + p][...] = d
            outs[2 * k + p][...] = nm
            outs[3 * k + p][...] = nv

    shapes = [_sds(w.shape, F32) for w in ws]
    return pl.pallas_call(body, name="small_adamw", out_shape=shapes * 4 + [_sds((1, LANES), F32)],
                          scratch_shapes=[pltpu.VMEM((SLAB_ROWS, D), F32)],
                          compiler_params=_params())(recv, *ws, *ms, *vs)


def _rope_tables(positions):
    inv_freq = jnp.power(jnp.float32(ROPE_THETA), -jnp.arange(0, ROT, 2, dtype=F32) / ROT)
    ang = positions.astype(F32)[:, None] * inv_freq[None, :]
    cos, sin = jnp.cos(ang), jnp.sin(ang)
    c64 = jnp.concatenate([cos, cos, jnp.ones((S, HD - ROT), F32)], axis=-1)
    s64 = jnp.concatenate([-sin, sin, jnp.zeros((S, HD - ROT), F32)], axis=-1)
    return jnp.tile(c64, (1, 2)), jnp.tile(s64, (1, 2))


def _local_step(x, tgt, positions, g_a, wa, b_forget, gq_a, gk_a, g_kv, gk_b, g_b, gq_b, sinks,
                woa_s, wkv_s, wib_s, wob_s):
    nq = S // TQ
    cos2, sin2 = _rope_tables(positions)
    b_pad = jnp.pad(b_forget, ((0, 0), (0, LANES - N_HEADS)))

    u_a = _rms_fwd(x, g_a, "rms_a_fwd")
    proj = _mm(u_a, wa, "nn", S, 256, D, name="mm_in_a")
    qn, kn, vb = _prep_a(proj, gq_a, gk_a)
    ccol, cbc = _fgate_fwd(proj, b_pad)
    crow = ccol[:, :N_HEADS].T.reshape(N_HEADS, nq, 1, TQ)
    o_a, z_a, lse_a, woa_g, wkv_g, w_in_b, wob_g = _fox_fwd(
        qn, kn, vb, proj, crow, cbc,
        rider=[("gather_rows", woa_s), ("gather_rows", wkv_s), ("gather_cols", wib_s), ("gather_rows", wob_s)])
    w_out_a, w_kv, w_out_b = woa_g.reshape(D, D), wkv_g.reshape(D, 512), wob_g.reshape(D, D)
    h1 = _mm(z_a, w_out_a, "nn", 1024, 512, D, add=x, name="mm_out_a")
    u_kv, u_b = _rms2_fwd(h1, g_kv, g_b)
    kv = _mm(u_kv, w_kv, "nn", 1024, 512, D, name="mm_kv")
    pb = _mm(u_b, w_in_b, "nn", 1024, 512, D, name="mm_in_b")
    qb = _prep_b(pb, gq_b, cos2, sin2)
    ksh, vsh = _prep_kv(kv, gk_b, cos2, sin2)
    sinks1 = sinks.reshape(N_HEADS)
    o_b, z_b, lse_b = _swa_fwd(qb, ksh, vsh, pb, sinks1)
    y = _mm(z_b, w_out_b, "nn", 1024, 512, D, add=h1, name="mm_out_b")
    dy, lsum = _loss_dy(y, tgt)
    dw_out_b = _mm(z_b, dy, "tn", 512, 512, S, out_dtype=BF16, name="mm_dw_out_b")
    dz_b = _mm(dy, w_out_b, "nt", 1024, 512, D, name="mm_dz_b")
    dq_b, dgate_b, dka, dkb, dva, dvb, dsinks = _swa_bwd(qb, ksh, vsh, dz_b, o_b, lse_b, pb, sinks1)
    dpb, dgq_b = _prep_b_bwd(dq_b, dgate_b, pb, gq_b, cos2, sin2)
    dkv, dgk_b = _prep_kv_bwd(dka, dkb, dva, dvb, kv, gk_b, cos2, sin2)
    dw_in_b = _mm(u_b, dpb, "tn", 512, 512, S, out_dtype=BF16, name="mm_dw_in_b")
    du_b = _mm(dpb, w_in_b, "nt", 1024, 512, 2048, name="mm_du_b")
    dw_kv = _mm(u_kv, dkv, "tn", 512, 512, S, out_dtype=BF16, name="mm_dw_kv")
    du_kv = _mm(dkv, w_kv, "nt", 1024, 512, 512, name="mm_du_kv")
    dh1, dg_b, dg_kv = _rms2_bwd(du_b, du_kv, h1, g_b, g_kv, dy)
    dw_out_a = _mm(z_a, dh1, "tn", 512, 512, S, out_dtype=BF16, name="mm_dw_out_a")
    dz_a = _mm(dh1, w_out_a, "nt", 1024, 512, D, name="mm_dz_a")
    dq_a, delta_a, do_a, dgate_a, drow, r_wob, r_wib, r_wkv, r_woa = _fox_bwd_dq(
        qn, kn, vb, dz_a, proj, o_a, lse_a, crow, cbc,
        rider=[("a2a_rows", dw_out_b), ("a2a_cols", dw_in_b), ("a2a_rows", dw_kv), ("a2a_rows", dw_out_a)])
    dk_a, dv_a, dcs = _fox_bwd_dkv(qn, kn, vb, do_a, lse_a, delta_a, crow, cbc)
    drow_col = jnp.pad(drow.reshape(N_HEADS, S).T, ((0, 0), (0, LANES - N_HEADS)))
    dproj, dgq_a, dgk_a, db_f = _prep_a_bwd(dq_a, dk_a, dv_a, dgate_a, drow_col, dcs, proj, b_pad, gq_a, gk_a)
    dwa = _mm(u_a, dproj, "tn", 1024, 256, S, out_dtype=BF16, name="mm_dw_in_a")
    du_a, r_wa = _mm(dproj, wa, "nt", 1024, 512, NA // 2, name="mm_du_a",
                     rider=[("a2a_chips", _pair_reduce(_reshard_dwa(dwa)))])
    dx, dg_a = _rms_bwd(du_a, x, g_a, dh1)
    slab = _pack_small(dg_a, dg_kv, dg_b, db_f, dgq_a, dgk_a, dgk_b, dgq_b, dsinks, lsum)
    return dx, r_wa, r_woa, r_wkv, r_wib, r_wob, _gather_slab(slab)


def kernel(x, positions, norm_a_g, w_in_a, b_forget, qnorm_a_g, knorm_a_g, w_out_a, kv_norm_g, w_kv, knorm_b_g, norm_b_g, w_in_b, qnorm_b_g, sinks, w_out_b, loss_target, m_norm_a_g, m_w_in_a, m_b_forget, m_qnorm_a_g, m_knorm_a_g, m_w_out_a, m_kv_norm_g, m_w_kv, m_knorm_b_g, m_norm_b_g, m_w_in_b, m_qnorm_b_g, m_sinks, m_w_out_b, v_norm_a_g, v_w_in_a, v_b_forget, v_qnorm_a_g, v_knorm_a_g, v_w_out_a, v_kv_norm_g, v_w_kv, v_knorm_b_g, v_norm_b_g, v_w_in_b, v_qnorm_b_g, v_sinks, v_w_out_b):
    wa_g, ga_g, woa_s, wkv_s, wib_s, wob_s = _gather_first(w_in_a, w_out_a, w_kv, w_in_b, w_out_b, norm_a_g)
    dx, r_wa, r_woa, r_wkv, r_wib, r_wob, slab_g = _local_step(
        x[0], loss_target[0], positions, ga_g.reshape(1, D), _unshard_wa(wa_g), b_forget, qnorm_a_g, knorm_a_g,
        kv_norm_g.reshape(1, D), knorm_b_g.reshape(1, HD), norm_b_g, qnorm_b_g, sinks, woa_s, wkv_s, wib_s, wob_s)

    big = {}
    for name, recv, w, m, v in (
            ("w_in_a", r_wa, w_in_a, m_w_in_a, v_w_in_a), ("w_out_a", r_woa, w_out_a, m_w_out_a, v_w_out_a),
            ("w_kv", r_wkv, w_kv, m_w_kv, v_w_kv), ("w_in_b", r_wib, w_in_b, m_w_in_b, v_w_in_b),
            ("w_out_b", r_wob, w_out_b, m_w_out_b, v_w_out_b)):
        big[name] = _sum_adamw(recv, w, m, v, "adamw_" + name)

    r2 = lambda a: a.reshape(1, -1)
    small_w = dict(norm_a_g=norm_a_g, b_forget=b_forget, qnorm_a_g=qnorm_a_g, knorm_a_g=knorm_a_g,
                   kv_norm_g=kv_norm_g, knorm_b_g=knorm_b_g, norm_b_g=norm_b_g, qnorm_b_g=qnorm_b_g, sinks=sinks)
    small_m = dict(norm_a_g=m_norm_a_g, b_forget=m_b_forget, qnorm_a_g=m_qnorm_a_g, knorm_a_g=m_knorm_a_g,
                   kv_norm_g=m_kv_norm_g, knorm_b_g=m_knorm_b_g, norm_b_g=m_norm_b_g, qnorm_b_g=m_qnorm_b_g,
                   sinks=m_sinks)
    small_v = dict(norm_a_g=v_norm_a_g, b_forget=v_b_forget, qnorm_a_g=v_qnorm_a_g, knorm_a_g=v_knorm_a_g,
                   kv_norm_g=v_kv_norm_g, knorm_b_g=v_knorm_b_g, norm_b_g=v_norm_b_g, qnorm_b_g=v_qnorm_b_g,
                   sinks=v_sinks)
    res = _small_adamw(slab_g, [r2(small_w[n]) for n in SMALL], [r2(small_m[n]) for n in SMALL],
                       [r2(small_v[n]) for n in SMALL])
    k = len(SMALL)
    small = {n: [res[q * k + p].reshape(small_w[n].shape) for q in range(4)] for p, n in enumerate(SMALL)}
    loss = res[4 * k][0, 0]

    order = ["norm_a_g", "w_in_a", "b_forget", "qnorm_a_g", "knorm_a_g", "w_out_a", "kv_norm_g", "w_kv",
             "knorm_b_g", "norm_b_g", "w_in_b", "qnorm_b_g", "sinks", "w_out_b"]

    def leaf(n, q):
        return big[n][q] if n in big else small[n][q]

    outs = [loss, dx[None]]
    for q in range(4):
        outs.extend(leaf(n, q) for n in order)
    return tuple(outs)
```

```python
import jax
import jax.numpy as jnp
from jax import lax
from jax.experimental import pallas as pl
from jax.experimental.pallas import tpu as pltpu

F32, BF16 = jnp.float32, jnp.bfloat16

S = 2048
D = 1024
HD = 64
N_HEADS = 16
N_DEV = 8
NA = 4352
GOFF = 3072
FOFF = 4096
RAW_F = 3072
RAW_G = RAW_F + N_HEADS
NA_RAW = 4112
EPS = 1e-6
QSCALE = 0.125
ROPE_THETA = 500000.0
ROT = 16
WIN = 128
TQ = 256
TK = 256
KS = TQ // 2
HPS = 8
HW = HPS * HD
TM = 256
RB = 512
CB = 256
LANES = 128

ADAM_LR, ADAM_B1, ADAM_B2, ADAM_EPS, ADAM_WD, ADAM_STEP = 0.001, 0.9, 0.999, 1e-08, 0.01, 10

VMEM_LIMIT = 56 * 1024 * 1024


def _params():
    return pltpu.CompilerParams(vmem_limit_bytes=VMEM_LIMIT)


def _sds(shape, dtype):
    return jax.ShapeDtypeStruct(shape, dtype)


def _dot_nt(a, b):
    return lax.dot_general(a, b, (((1,), (1,)), ((), ())), preferred_element_type=F32)


def _dot_tn(a, b):
    return lax.dot_general(a, b, (((0,), (0,)), ((), ())), preferred_element_type=F32)


def _dot_nn(a, b):
    return lax.dot_general(a, b, (((1,), (0,)), ((), ())), preferred_element_type=F32)


def _sigmoid(g):
    return 1.0 / (1.0 + jnp.exp(-g))


def _lane_iota(shape):
    return lax.broadcasted_iota(jnp.int32, shape, len(shape) - 1)


def _flips(kind):
    return (2, 4, 6) if kind == "a2a_chips" else tuple(range(1, N_DEV))


def _send_view(kind, ref, dev):
    if kind in ("gather_rows", "gather_cols"):
        return ref
    if kind == "a2a_slots":
        return ref.at[dev]
    if kind == "a2a_chips":
        return ref.at[dev >> 1]
    if kind == "a2a_rows":
        rows = ref.shape[0] // N_DEV
        return ref.at[pl.ds(pl.multiple_of(dev * rows, rows), rows)]
    cols = ref.shape[1] // N_DEV
    return ref.at[:, pl.ds(pl.multiple_of(dev * cols, cols), cols)]


def _land_view(kind, ref, dev):
    if kind == "gather_cols":
        cols = ref.shape[1] // N_DEV
        return ref.at[:, pl.ds(pl.multiple_of(dev * cols, cols), cols)]
    if kind == "a2a_chips":
        return ref.at[dev >> 1]
    return ref.at[dev]


def _landing_sds(kind, arr):
    if kind == "gather_rows":
        return _sds((N_DEV,) + arr.shape, arr.dtype)
    if kind == "gather_cols":
        return _sds((arr.shape[0], N_DEV * arr.shape[1]), arr.dtype)
    if kind == "a2a_rows":
        return _sds((N_DEV, arr.shape[0] // N_DEV, arr.shape[1]), arr.dtype)
    if kind == "a2a_cols":
        return _sds((N_DEV, arr.shape[0], arr.shape[1] // N_DEV), arr.dtype)
    return _sds(arr.shape, arr.dtype)


def _exchange_sems(n_parts):
    n = n_parts * (N_DEV - 1)
    return [pltpu.SemaphoreType.DMA((n,)), pltpu.SemaphoreType.DMA((n,)), pltpu.SemaphoreType.DMA((n_parts,))]


def _exchange_ops(kinds, srcs, dsts, sems, start, wait):
    send_sems, recv_sems, local_sems = sems
    x, y, c = lax.axis_index("x"), lax.axis_index("y"), lax.axis_index("c")
    me = 4 * x + 2 * y + c

    def local(a):
        return pltpu.make_async_copy(_send_view(kinds[a], srcs[a], me), _land_view(kinds[a], dsts[a], me),
                                     local_sems.at[a])

    def remote(a, k, landing_dev):
        peer = (x ^ ((k >> 2) & 1), y ^ ((k >> 1) & 1), c ^ (k & 1))
        sem = a * (N_DEV - 1) + k - 1
        return pltpu.make_async_remote_copy(
            src_ref=_send_view(kinds[a], srcs[a], me ^ k), dst_ref=_land_view(kinds[a], dsts[a], landing_dev),
            send_sem=send_sems.at[sem], recv_sem=recv_sems.at[sem], device_id=peer,
            device_id_type=pl.DeviceIdType.MESH)

    pairs = [(a, k) for k in range(1, N_DEV) for a in range(len(kinds)) if k in _flips(kinds[a])]
    if start:
        for a in range(len(kinds)):
            local(a).start()
        for a, k in pairs:
            remote(a, k, me).start()
    if wait:
        for a, k in pairs:
            remote(a, k, me ^ k).wait_recv()
            remote(a, k, me).wait_send()
        for a in range(len(kinds)):
            local(a).wait()


def _gather_two_level(srcs, dsts, sems):
    send_sems, recv_sems, local_sems = sems
    x, y, c = lax.axis_index("x"), lax.axis_index("y"), lax.axis_index("c")
    me, sibling = (x, y, c), (x, y, 1 - c)
    chips = [(1 - x, y), (x, 1 - y), (1 - x, 1 - y)]

    def slot(ref, dev):
        return ref.at[4 * dev[0] + 2 * dev[1] + dev[2]]

    def copy(a, k, block, to, src=None):
        return pltpu.make_async_remote_copy(
            src_ref=slot(dsts[a], block) if src is None else src, dst_ref=slot(dsts[a], block),
            send_sem=send_sems.at[a * (N_DEV - 1) + k], recv_sem=recv_sems.at[a * (N_DEV - 1) + k],
            device_id=to, device_id_type=pl.DeviceIdType.MESH)

    parts = range(len(srcs))
    mine = [pltpu.make_async_copy(srcs[a], slot(dsts[a], me), local_sems.at[a]) for a in parts]
    first = [copy(a, 0, me, sibling, src=srcs[a]) for a in parts]
    first += [copy(a, 1 + j, me, (*chip, c), src=srcs[a]) for j, chip in enumerate(chips) for a in parts]
    for cp in mine + first:
        cp.start()
    passed = []
    for j, chip in enumerate(chips):
        for a in parts:
            copy(a, 1 + j, (*chip, c), me).wait_recv()
            fwd = copy(a, 4 + j, (*chip, c), sibling)
            fwd.start()
            passed.append(fwd)
    for a in parts:
        copy(a, 0, sibling, me).wait_recv()
        for j, chip in enumerate(chips):
            copy(a, 4 + j, (*chip, 1 - c), me).wait_recv()
    for cp in first + passed:
        cp.wait_send()
    for cp in mine:
        cp.wait()


def _call(body, *, name, args, in_specs, out_specs, out_shape, grid=(), scratch_shapes=(), aliases=None, rider=()):
    n_in, n_out, n_scr, n_r = len(in_specs), len(out_specs), len(scratch_shapes), len(rider)
    kinds = [kind for kind, _ in rider]

    def kernel_body(*refs):
        c_in, r_in = refs[:n_in], refs[n_in:n_in + n_r]
        c_out = refs[n_in + n_r:n_in + n_r + n_out]
        r_out = refs[n_in + n_r + n_out:n_in + 2 * n_r + n_out]
        rest = refs[n_in + 2 * n_r + n_out:]
        c_scr, sems = rest[:n_scr], rest[n_scr:]
        if n_r:
            assert grid, "a rider needs a gridded call"
            ids = [pl.program_id(ax) for ax in range(len(grid))]
            first, last = ids[0] == 0, ids[0] == grid[0] - 1
            for pid, size in zip(ids[1:], grid[1:]):
                first = first & (pid == 0)
                last = last & (pid == size - 1)
            pl.when(first)(lambda: _exchange_ops(kinds, r_in, r_out, sems, True, False))
        body(*c_in, *c_out, *c_scr)
        if n_r:
            pl.when(last)(lambda: _exchange_ops(kinds, r_in, r_out, sems, False, True))

    anyspec = pl.BlockSpec(memory_space=pl.ANY)
    params = pltpu.CompilerParams(vmem_limit_bytes=VMEM_LIMIT, has_side_effects=bool(n_r))
    outs = pl.pallas_call(
        kernel_body, name=name, grid=grid, in_specs=list(in_specs) + [anyspec] * n_r,
        out_specs=list(out_specs) + [anyspec] * n_r,
        out_shape=list(out_shape) + [_landing_sds(kind, arr) for kind, arr in rider],
        scratch_shapes=list(scratch_shapes) + (_exchange_sems(n_r) if n_r else []),
        input_output_aliases=aliases or {}, compiler_params=params)(*args, *[arr for _, arr in rider])
    return list(outs)


def _mm(a, b, mode, tm, tn, tk, out_dtype=F32, add=None, name="mm", rider=()):
    if mode == "nn":
        (m, k), n = a.shape, b.shape[1]
        a_spec = pl.BlockSpec((tm, tk), lambda i, j, kk: (i, kk))
        b_spec = pl.BlockSpec((tk, tn), lambda i, j, kk: (kk, j))
        dot = _dot_nn
    elif mode == "nt":
        (m, k), n = a.shape, b.shape[0]
        a_spec = pl.BlockSpec((tm, tk), lambda i, j, kk: (i, kk))
        b_spec = pl.BlockSpec((tn, tk), lambda i, j, kk: (j, kk))
        dot = _dot_nt
    else:
        (k, m), n = a.shape, b.shape[1]
        a_spec = pl.BlockSpec((tk, tm), lambda i, j, kk: (kk, i))
        b_spec = pl.BlockSpec((tk, tn), lambda i, j, kk: (kk, j))
        dot = _dot_tn
    assert m % tm == 0 and n % tn == 0 and k % tk == 0, (m, n, k, tm, tn, tk)
    nk = k // tk
    has_add = add is not None

    def body(*refs):
        if has_add:
            a_ref, b_ref, add_ref, o_ref, acc = refs
        else:
            a_ref, b_ref, o_ref, acc = refs
        p = dot(a_ref[...].astype(BF16), b_ref[...].astype(BF16))

        def finish(total):
            if has_add:
                total = add_ref[...] + total
            o_ref[...] = total.astype(out_dtype)

        if nk == 1:
            finish(p)
        else:
            kk = pl.program_id(2)

            @pl.when(kk == 0)
            def _():
                acc[...] = p

            @pl.when(kk > 0)
            def _():
                acc[...] += p

            @pl.when(kk == nk - 1)
            def _():
                finish(acc[...])

    in_specs = [a_spec, b_spec]
    args = [a, b]
    if has_add:
        in_specs.append(pl.BlockSpec((tm, tn), lambda i, j, kk: (i, j)))
        args.append(add)
    acc_shape = (tm, tn) if nk > 1 else (8, LANES)
    outs = _call(body, name=name, args=args, grid=(m // tm, n // tn, nk), in_specs=in_specs,
                 out_specs=[pl.BlockSpec((tm, tn), lambda i, j, kk: (i, j))], out_shape=[_sds((m, n), out_dtype)],
                 scratch_shapes=[pltpu.VMEM(acc_shape, F32)], rider=rider)
    return outs if rider else outs[0]


def _rms_rinv(x):
    return lax.rsqrt(jnp.mean(x * x, axis=-1, keepdims=True) + EPS)


def _rms_bwd_core(du, x, g):
    r = _rms_rinv(x)
    dug = du * g
    dx = r * (dug - x * ((r * r) * jnp.mean(dug * x, axis=-1, keepdims=True)))
    dg = jnp.sum(du * (x * r), axis=0, keepdims=True)
    return dx, dg


def _half_sum(v, lo_half):
    s0 = jnp.sum(jnp.where(lo_half, v, 0.0), axis=-1, keepdims=True)
    s1 = jnp.sum(jnp.where(lo_half, 0.0, v), axis=-1, keepdims=True)
    return jnp.where(lo_half, s0, s1)


def _head_rinv(x, lo_half):
    return lax.rsqrt(_half_sum(x * x, lo_half) * (1.0 / HD) + EPS)


def _head_norm_bwd(dn, x, g, lo_half):
    r = _head_rinv(x, lo_half)
    dng = dn * g
    dx = r * (dng - x * ((r * r) * (_half_sum(dng * x, lo_half) * (1.0 / HD))))
    dg = jnp.sum(dn * (x * r), axis=0, keepdims=True)
    return dx, dg


def _rope_swap(x, lane):
    l64 = lane & (HD - 1)
    return jnp.where(l64 < ROT // 2, pltpu.roll(x, LANES - ROT // 2, 1), pltpu.roll(x, ROT // 2, 1))


def _rope_fwd(x, cos, sin, lane):
    return x * cos + _rope_swap(x, lane) * sin


def _rope_bwd(dy, cos, sin, lane):
    return dy * cos + jnp.where((lane & (HD - 1)) < ROT, _rope_swap(dy * sin, lane), 0.0)


def _g2(g_ref):
    g = g_ref[...]
    return jnp.concatenate([g, g], axis=-1)


def _pairs(width):
    return [slice(LANES * c, LANES * (c + 1)) for c in range(width // LANES)]


def _rms_fwd(x, g, name):
    def body(x_ref, g_ref, u_ref):
        xv = x_ref[...]
        u_ref[...] = ((xv * _rms_rinv(xv)) * g_ref[...]).astype(BF16)

    return pl.pallas_call(
        body, name=name, grid=(S // TM,),
        in_specs=[pl.BlockSpec((TM, D), lambda i: (i, 0)), pl.BlockSpec((1, D), lambda i: (0, 0))],
        out_specs=pl.BlockSpec((TM, D), lambda i: (i, 0)), out_shape=_sds((S, D), BF16),
        compiler_params=_params())(x, g)


def _rms2_fwd(h, g1, g2):
    def body(x_ref, g1_ref, g2_ref, u1_ref, u2_ref):
        xv = x_ref[...]
        xn = xv * _rms_rinv(xv)
        u1_ref[...] = (xn * g1_ref[...]).astype(BF16)
        u2_ref[...] = (xn * g2_ref[...]).astype(BF16)

    row = pl.BlockSpec((TM, D), lambda i: (i, 0))
    vec = pl.BlockSpec((1, D), lambda i: (0, 0))
    return pl.pallas_call(
        body, name="rms2_fwd", grid=(S // TM,), in_specs=[row, vec, vec], out_specs=[row, row],
        out_shape=[_sds((S, D), BF16)] * 2, compiler_params=_params())(h, g1, g2)


def _prep_a(proj, gq, gk):
    def body(q_ref, k_ref, v_ref, gq_ref, gk_ref, qo_ref, ko_ref, vo_ref):
        lo_half = _lane_iota((TM, LANES)) < HD
        gq2, gk2 = _g2(gq_ref), _g2(gk_ref)
        for c in _pairs(D):
            q = q_ref[:, c]
            k = k_ref[:, c]
            qo_ref[:, c] = (((q * _head_rinv(q, lo_half)) * gq2) * QSCALE).astype(BF16)
            ko_ref[:, c] = ((k * _head_rinv(k, lo_half)) * gk2).astype(BF16)
        vo_ref[...] = v_ref[...].astype(BF16)

    blk = lambda off: pl.BlockSpec((TM, D), lambda i: (i, off))
    gspec = pl.BlockSpec((1, HD), lambda i: (0, 0))
    return pl.pallas_call(
        body, name="prep_a", grid=(S // TM,),
        in_specs=[blk(0), blk(1), blk(2), gspec, gspec], out_specs=[blk(0)] * 3,
        out_shape=[_sds((S, D), BF16)] * 3, compiler_params=_params())(proj, proj, proj, gq, gk)


def _pick_lane(block, lane, idx):
    return jnp.sum(jnp.where(lane == idx, block, 0.0), axis=-1, keepdims=True)


def _fgate_fwd(proj, b_pad):
    def body(f_ref, b_ref, c_ref, cbc_ref, carry):
        @pl.when(pl.program_id(0) == 0)
        def _():
            carry[...] = jnp.zeros_like(carry)

        z = f_ref[...] + b_ref[...]
        logf = jnp.minimum(z, 0.0) - jnp.log1p(jnp.exp(-jnp.abs(z)))
        r = lax.broadcasted_iota(jnp.int32, (TM, TM), 0)
        c = lax.broadcasted_iota(jnp.int32, (TM, TM), 1)
        tri = (r >= c).astype(F32)
        loc = jnp.dot(tri, logf, precision=lax.Precision.HIGHEST, preferred_element_type=F32) + carry[0:1, :]
        c_ref[...] = loc
        carry[0:1, :] = loc[TM - 1:TM, :]
        lane = _lane_iota((TM, LANES))
        for h in range(N_HEADS):
            cbc_ref[:, LANES * h:LANES * (h + 1)] = jnp.broadcast_to(_pick_lane(loc, lane, h), (TM, LANES))

    return pl.pallas_call(
        body, name="fgate_fwd", grid=(S // TM,),
        in_specs=[pl.BlockSpec((TM, LANES), lambda i: (i, FOFF // LANES)), pl.BlockSpec((1, LANES), lambda i: (0, 0))],
        out_specs=[pl.BlockSpec((TM, LANES), lambda i: (i, 0)), pl.BlockSpec((TM, N_HEADS * LANES), lambda i: (i, 0))],
        out_shape=[_sds((S, LANES), F32), _sds((S, N_HEADS * LANES), F32)],
        scratch_shapes=[pltpu.VMEM((8, LANES), F32)], compiler_params=_params())(proj, b_pad)


def _key_le_query(offset, keys=TK):
    r = lax.broadcasted_iota(jnp.int32, (keys, TQ), 0)
    c = lax.broadcasted_iota(jnp.int32, (keys, TQ), 1)
    return (r + offset) <= c


def _widen(tile):
    return jnp.concatenate([tile] * (TQ // LANES), axis=1)


def _fox_fwd(qn, kn, vb, proj, crow, cbc, rider=()):
    nq, per = S // TQ, TQ // TK

    def body(q_ref, k_ref, v_ref, g_ref, cq_ref, cbc_ref, o_ref, z_ref, lse_ref, st_s, pt_s):
        i = pl.program_id(1)
        qs = [q_ref[:, HD * hh:HD * (hh + 1)] for hh in range(HPS)]
        cqs = [cq_ref[hh, 0] for hh in range(HPS)]

        def scores(s, hh):
            off = pl.multiple_of(s * KS, KS)
            kj = k_ref[pl.ds(off, KS), HD * hh:HD * (hh + 1)]
            return (_dot_nt(kj, qs[hh]) + cqs[hh]) - _widen(cbc_ref[pl.ds(off, KS), LANES * hh:LANES * (hh + 1)])

        def values(s, hh, pt):
            off = pl.multiple_of(s * KS, KS)
            return _dot_tn(v_ref[pl.ds(off, KS), HD * hh:HD * (hh + 1)], pt)

        def step(s, slot, carries, mask=None, last=False):
            if not last:
                for hh in range(HPS):
                    st_s[1 - slot, hh] = scores(s + 1, hh)
            pvs = [values(jnp.maximum(s - 1, 0), hh, pt_s[1 - slot, hh]) for hh in range(HPS)]
            out = []
            for hh in range(HPS):
                m, l, acc = carries[hh]
                st = st_s[slot, hh]
                if mask is not None:
                    st = jnp.where(mask, st, -jnp.inf)
                m_new = jnp.maximum(m, jnp.max(st, axis=0, keepdims=True))
                pt = jnp.exp(st - m_new)
                alpha = jnp.exp(m - m_new)
                pt_s[slot, hh] = pt.astype(BF16)
                out.append((m_new, alpha * l + jnp.sum(pt, axis=0, keepdims=True), alpha * (acc + pvs[hh])))
            return tuple(out)

        for hh in range(HPS):
            st_s[0, hh] = scores(0, hh)
            pt_s[1, hh] = jnp.zeros((KS, TQ), BF16)
        one = (jnp.full((1, TQ), -jnp.inf, F32), jnp.zeros((1, TQ), F32), jnp.zeros((HD, TQ), F32))
        carries = lax.fori_loop(0, i, lambda t, cr: step(2 * t + 1, 1, step(2 * t, 0, cr)), (one,) * HPS)
        carries = step(2 * i, 0, carries, mask=_key_le_query(0, KS))
        carries = step(2 * i + 1, 1, carries, mask=_key_le_query(KS, KS), last=True)
        accs = []
        for hh in range(HPS):
            m, l, acc = carries[hh]
            acc = acc + values(2 * i + 1, hh, pt_s[1, hh])
            accs.append(acc / l)
            lse_ref[hh, 0] = m + jnp.log(l)
        o = jnp.concatenate(accs, axis=0).T
        o_ref[...] = o
        g = g_ref[...]
        z_ref[...] = (o * (g * _sigmoid(g))).astype(BF16)

    qblk = pl.BlockSpec((TQ, HW), lambda hp, i: (i, hp))
    full = pl.BlockSpec((S, HW), lambda hp, i: (0, hp))
    rows = pl.BlockSpec((HPS, 1, 1, TQ), lambda hp, i: (hp, i, 0, 0))
    return _call(
        body, name="fox_fwd", args=(qn, kn, vb, proj, crow, cbc), grid=(N_HEADS // HPS, nq),
        in_specs=[qblk, full, full,
                  pl.BlockSpec((TQ, HW), lambda hp, i: (i, GOFF // HW + hp)),
                  rows, pl.BlockSpec((S, HPS * LANES), lambda hp, i: (0, hp))],
        out_specs=[qblk, qblk, rows],
        out_shape=[_sds((S, D), F32), _sds((S, D), BF16), _sds((N_HEADS, nq, 1, TQ), F32)],
        scratch_shapes=[pltpu.VMEM((2, HPS, KS, TQ), F32), pltpu.VMEM((2, HPS, KS, TQ), BF16)], rider=rider)


def _fox_bwd_pre(dz, proj, o):
    nq = S // TQ

    def body(dz_ref, g_ref, o_ref, do_ref, dg_ref, delta_ref):
        g = g_ref[...]
        sg = _sigmoid(g)
        dzv = dz_ref[...]
        ov = o_ref[...]
        do = dzv * (g * sg)
        dg_ref[...] = (dzv * ov * (sg * (1.0 + g * (1.0 - sg)))).astype(BF16)
        do_ref[...] = do.astype(BF16)
        prod_t = (do * ov).T
        for h in range(N_HEADS):
            delta_ref[h, 0] = jnp.sum(prod_t[HD * h:HD * (h + 1), :], axis=0, keepdims=True)

    row = pl.BlockSpec((TQ, D), lambda i: (i, 0))
    return pl.pallas_call(
        body, name="fox_bwd_pre", grid=(nq,),
        in_specs=[row, pl.BlockSpec((TQ, D), lambda i: (i, GOFF // D)), row],
        out_specs=[row, row, pl.BlockSpec((N_HEADS, 1, 1, TQ), lambda i: (0, i, 0, 0))],
        out_shape=[_sds((S, D), BF16), _sds((S, D), BF16), _sds((N_HEADS, nq, 1, TQ), F32)],
        compiler_params=_params())(dz, proj, o)


def _fox_bwd(qn, kn, vb, dob, lse, delta, crow, cbc, rider=()):
    nq, nkb = S // TQ, S // TK

    def body(q_ref, k_ref, v_ref, do_ref, lse_ref, del_ref, cq_ref, cbc_ref,
             dk_ref, dv_ref, dcs_ref, dq_ref, dr_ref, st_s, dp_s, pt_s, ds_s, dq_acc, dr_acc):
        j = pl.program_id(1)

        @pl.when(j == 0)
        def _():
            dq_acc[...] = jnp.zeros_like(dq_acc)
            dr_acc[...] = jnp.zeros_like(dr_acc)

        kjs = [k_ref[:, HD * hh:HD * (hh + 1)] for hh in range(HPS)]
        vjs = [v_ref[:, HD * hh:HD * (hh + 1)] for hh in range(HPS)]

        def rows_of(ref, u, hh):
            off = pl.multiple_of(u * TQ, TQ)
            return ref[pl.ds(off, TQ), HD * hh:HD * (hh + 1)]

        def products(u, hh):
            st = (_dot_nt(kjs[hh], rows_of(q_ref, u, hh)) + cq_ref[hh, u]) - _widen(
                cbc_ref[:, LANES * hh:LANES * (hh + 1)])
            return st, _dot_nt(vjs[hh], rows_of(do_ref, u, hh))

        def step(u, slot, carries, masked=False):
            nxt = jnp.minimum(u + 1, nq - 1)
            for hh in range(HPS):
                st_s[1 - slot, hh], dp_s[1 - slot, hh] = products(nxt, hh)
            prev = jnp.maximum(u - 1, 0)
            dvs = [_dot_nn(pt_s[1 - slot, hh], rows_of(do_ref, prev, hh)) for hh in range(HPS)]
            dks = [_dot_nn(ds_s[1 - slot, hh], rows_of(q_ref, prev, hh)) for hh in range(HPS)]
            for hh in range(HPS):
                dq_acc[hh, prev] += _dot_tn(kjs[hh], ds_s[1 - slot, hh])
            out = []
            for hh in range(HPS):
                dk, dv, dcs = carries[hh]
                st = st_s[slot, hh]
                if masked:
                    st = jnp.where(_key_le_query((j - u) * TQ), st, -jnp.inf)
                pt = jnp.exp(st - lse_ref[hh, u])
                dst = pt * (dp_s[slot, hh] - del_ref[hh, u])
                pt_s[slot, hh] = pt.astype(BF16)
                ds_s[slot, hh] = dst.astype(BF16)
                dr_acc[hh, u] += jnp.sum(dst, axis=0, keepdims=True)
                out.append((dk + dks[hh], dv + dvs[hh], dcs + (dst[:, :LANES] + dst[:, LANES:])))
            return tuple(out)

        t0 = j // 2
        for hh in range(HPS):
            st_s[0, hh], dp_s[0, hh] = products(2 * t0, hh)
            pt_s[1, hh] = jnp.zeros((TK, TQ), BF16)
            ds_s[1, hh] = jnp.zeros((TK, TQ), BF16)
        one = (jnp.zeros((TK, HD), F32), jnp.zeros((TK, HD), F32), jnp.zeros((TK, LANES), F32))
        carries = step(2 * t0 + 1, 1, step(2 * t0, 0, (one,) * HPS, masked=True), masked=True)
        carries = lax.fori_loop(t0 + 1, nq // 2, lambda t, cr: step(2 * t + 1, 1, step(2 * t, 0, cr)), carries)
        dks, dvs = [], []
        for hh in range(HPS):
            dk, dv, dcs = carries[hh]
            dks.append(dk + _dot_nn(ds_s[1, hh], rows_of(q_ref, nq - 1, hh)))
            dvs.append(dv + _dot_nn(pt_s[1, hh], rows_of(do_ref, nq - 1, hh)))
            dq_acc[hh, nq - 1] += _dot_tn(kjs[hh], ds_s[1, hh])
            dcs_ref[:, LANES * hh:LANES * (hh + 1)] = jnp.broadcast_to(
                -jnp.sum(dcs, axis=-1, keepdims=True), (TK, LANES))
        dk_ref[...] = jnp.concatenate(dks, axis=-1)
        dv_ref[...] = jnp.concatenate(dvs, axis=-1)

        @pl.when(j == nkb - 1)
        def _():
            for i in range(nq):
                dq_ref[TQ * i:TQ * (i + 1), :] = jnp.concatenate([dq_acc[hh, i] for hh in range(HPS)], axis=0).T
            dr_ref[...] = dr_acc[...]

    kblk = pl.BlockSpec((TK, HW), lambda hp, j: (j, hp))
    full = pl.BlockSpec((S, HW), lambda hp, j: (0, hp))
    rows = pl.BlockSpec((HPS, nq, 1, TQ), lambda hp, j: (hp, 0, 0, 0))
    cblk = pl.BlockSpec((TK, HPS * LANES), lambda hp, j: (j, hp))
    return _call(
        body, name="fox_bwd", args=(qn, kn, vb, dob, lse, delta, crow, cbc), grid=(N_HEADS // HPS, nkb),
        in_specs=[full, kblk, kblk, full, rows, rows, rows, cblk],
        out_specs=[kblk, kblk, cblk, full, rows],
        out_shape=[_sds((S, D), F32), _sds((S, D), F32), _sds((S, N_HEADS * LANES), F32), _sds((S, D), F32),
                   _sds((N_HEADS, nq, 1, TQ), F32)],
        scratch_shapes=[pltpu.VMEM((2, HPS, TK, TQ), F32), pltpu.VMEM((2, HPS, TK, TQ), F32),
                        pltpu.VMEM((2, HPS, TK, TQ), BF16), pltpu.VMEM((2, HPS, TK, TQ), BF16),
                        pltpu.VMEM((HPS, nq, HD, TQ), F32), pltpu.VMEM((HPS, nq, 1, TQ), F32)], rider=rider)


def _prep_a_bwd(dq, dk, dv, dgate, drow, dcs, proj, b_pad, gq, gk):
    nt = S // TM

    def body(dq_ref, dk_ref, dv_ref, dgt_ref, dr_ref, dcs_ref, xq_ref, xk_ref, f_ref, b_ref, gq_ref, gk_ref,
             o_ref, dgq_ref, dgk_ref, db_ref, carry):
        @pl.when(pl.program_id(0) == 0)
        def _():
            carry[...] = jnp.zeros_like(carry)
            dgq_ref[...] = jnp.zeros_like(dgq_ref)
            dgk_ref[...] = jnp.zeros_like(dgk_ref)
            db_ref[...] = jnp.zeros_like(db_ref)

        lane = _lane_iota((TM, LANES))
        lo_half = lane < HD
        gq2, gk2 = _g2(gq_ref), _g2(gk_ref)
        dgq, dgk = jnp.zeros((1, LANES), F32), jnp.zeros((1, LANES), F32)
        for c in _pairs(D):
            dxq, dg = _head_norm_bwd(dq_ref[:, c] * QSCALE, xq_ref[:, c], gq2, lo_half)
            o_ref[:, c] = dxq.astype(BF16)
            dgq = dgq + dg
            dxk, dg = _head_norm_bwd(dk_ref[:, c], xk_ref[:, c], gk2, lo_half)
            o_ref[:, D + c.start:D + c.stop] = dxk.astype(BF16)
            dgk = dgk + dg
        dgq_ref[...] += dgq
        dgk_ref[...] += dgk
        o_ref[:, 2 * D:3 * D] = dv_ref[...].astype(BF16)
        o_ref[:, GOFF:GOFF + D] = dgt_ref[...]

        dc = dr_ref[...]
        for h in range(N_HEADS):
            dc = dc + jnp.where(lane == h, dcs_ref[:, LANES * h:LANES * h + 1], 0.0)
        r = lax.broadcasted_iota(jnp.int32, (TM, TM), 0)
        c = lax.broadcasted_iota(jnp.int32, (TM, TM), 1)
        tri = (c >= r).astype(F32)
        dlogf = jnp.dot(tri, dc, precision=lax.Precision.HIGHEST, preferred_element_type=F32) + carry[0:1, :]
        carry[0:1, :] = dlogf[0:1, :]
        df = dlogf * (1.0 / (1.0 + jnp.exp(f_ref[...] + b_ref[...])))
        db_ref[...] += jnp.sum(df, axis=0, keepdims=True)
        o_ref[:, FOFF:FOFF + LANES] = df.astype(BF16)
        o_ref[:, FOFF + LANES:NA] = jnp.zeros((TM, NA - FOFF - LANES), BF16)

    rev = lambda width, col: pl.BlockSpec((TM, width), lambda i: (nt - 1 - i, col))
    gspec = pl.BlockSpec((1, HD), lambda i: (0, 0))
    acc = pl.BlockSpec((1, LANES), lambda i: (0, 0))
    return pl.pallas_call(
        body, name="prep_a_bwd", grid=(nt,),
        in_specs=[rev(D, 0), rev(D, 0), rev(D, 0), rev(D, 0), rev(LANES, 0), rev(N_HEADS * LANES, 0),
                  rev(D, 0), rev(D, 1), rev(LANES, FOFF // LANES), acc, gspec, gspec],
        out_specs=[rev(NA, 0), acc, acc, acc],
        out_shape=[_sds((S, NA), BF16)] + [_sds((1, LANES), F32)] * 3,
        scratch_shapes=[pltpu.VMEM((8, LANES), F32)],
        compiler_params=_params())(dq, dk, dv, dgate, drow, dcs, proj, proj, proj, b_pad, gq, gk)


def _prep_b(pb, gq, cos2, sin2):
    def body(x_ref, g_ref, c_ref, s_ref, o_ref):
        lane = _lane_iota((TM, LANES))
        g2, cos, sin = _g2(g_ref), c_ref[...], s_ref[...]
        for c in _pairs(D):
            x = x_ref[:, c]
            xn = (x * _head_rinv(x, lane < HD)) * g2
            o_ref[:, c] = (_rope_fwd(xn, cos, sin, lane) * QSCALE).astype(BF16)

    blk = pl.BlockSpec((TM, D), lambda i: (i, 0))
    tab = pl.BlockSpec((TM, LANES), lambda i: (i, 0))
    return pl.pallas_call(
        body, name="prep_b", grid=(S // TM,),
        in_specs=[blk, pl.BlockSpec((1, HD), lambda i: (0, 0)), tab, tab], out_specs=blk,
        out_shape=_sds((S, D), BF16), compiler_params=_params())(pb, gq, cos2, sin2)


def _prep_kv(kv, gk, cos2, sin2):
    def body(k_ref, v_ref, g_ref, c_ref, s_ref, ko_ref, vo_ref):
        lane = _lane_iota((RB, LANES))
        g2, cos, sin = _g2(g_ref), c_ref[...], s_ref[...]
        for c in _pairs(CB):
            x = k_ref[:, c]
            xn = (x * _head_rinv(x, lane < HD)) * g2
            ko_ref[:, c] = _rope_fwd(xn, cos, sin, lane).astype(BF16)
        vo_ref[...] = v_ref[...].astype(BF16)

    blk = lambda off: pl.BlockSpec((RB, CB), lambda i: (i, off))
    tab = pl.BlockSpec((RB, LANES), lambda i: (i, 0))
    return pl.pallas_call(
        body, name="prep_kv", grid=(S // RB,),
        in_specs=[blk(0), blk(1), pl.BlockSpec((1, HD), lambda i: (0, 0)), tab, tab],
        out_specs=[blk(0), blk(0)], out_shape=[_sds((S, CB), BF16)] * 2,
        compiler_params=_params())(kv, kv, gk, cos2, sin2)


def _swa_mask(n):
    r = lax.broadcasted_iota(jnp.int32, (4 * WIN, 2 * WIN), 0) & (WIN - 1)
    c = lax.broadcasted_iota(jnp.int32, (4 * WIN, 2 * WIN), 1)
    return (c > r) & (c <= r + WIN) & ((c >= WIN) | (n > 0))


def _stack4(ref_or_val, base):
    return jnp.concatenate([ref_or_val[:, base + HD * g: base + HD * (g + 1)] for g in range(4)], axis=0)


def _sink_col(s_ref, first):
    r = lax.broadcasted_iota(jnp.int32, (4 * WIN, 1), 0)
    col = jnp.full((4 * WIN, 1), s_ref[first + 3], F32)
    for g in range(2, -1, -1):
        col = jnp.where(r < WIN * (g + 1), s_ref[first + g], col)
    return col


def _swa_fwd(qb, ksh, vsh, pb, sinks):
    nb = S // WIN

    def body(q_ref, kp_ref, kc_ref, vp_ref, vc_ref, g_ref, s_ref, o_ref, z_ref, lse_ref):
        kp, n = pl.program_id(0), pl.program_id(1)
        valid = _swa_mask(n)
        outs, lses = [], []
        for kh in range(2):
            lo = HD * kh
            kb = jnp.concatenate([kp_ref[:, lo:lo + HD], kc_ref[:, lo:lo + HD]], axis=0)
            vb = jnp.concatenate([vp_ref[:, lo:lo + HD], vc_ref[:, lo:lo + HD]], axis=0)
            qs = _stack4(q_ref, 4 * HD * kh)
            s = jnp.where(valid, _dot_nt(qs, kb), -jnp.inf)
            sink = _sink_col(s_ref, (2 * kp + kh) * 4)
            m = jnp.maximum(jnp.max(s, axis=-1, keepdims=True), sink)
            p = jnp.exp(s - m)
            l = jnp.sum(p, axis=-1, keepdims=True) + jnp.exp(sink - m)
            os_ = _dot_nn(p.astype(BF16), vb) / l
            lse = m + jnp.log(l)
            for g in range(4):
                outs.append(os_[WIN * g:WIN * (g + 1), :])
                lses.append(jnp.broadcast_to(lse[WIN * g:WIN * (g + 1), :], (WIN, HD)))
        o = jnp.concatenate(outs, axis=-1)
        o_ref[...] = o
        g = g_ref[...]
        z_ref[...] = (o * (g * _sigmoid(g))).astype(BF16)
        lse_ref[...] = jnp.concatenate(lses, axis=-1)

    qblk = pl.BlockSpec((WIN, 512), lambda kp, n: (n, kp))
    prev = pl.BlockSpec((WIN, LANES), lambda kp, n: (jnp.maximum(n - 1, 0), kp))
    cur = pl.BlockSpec((WIN, LANES), lambda kp, n: (n, kp))
    return pl.pallas_call(
        body, name="swa_fwd", grid=(2, nb),
        in_specs=[qblk, prev, cur, prev, cur, pl.BlockSpec((WIN, 512), lambda kp, n: (n, 2 + kp)),
                  pl.BlockSpec(memory_space=pltpu.SMEM)],
        out_specs=[qblk, qblk, qblk],
        out_shape=[_sds((S, D), F32), _sds((S, D), BF16), _sds((S, D), F32)],
        compiler_params=_params())(qb, ksh, ksh, vsh, vsh, pb, sinks)


def _swa_bwd(qb, ksh, vsh, dz, o, lse, pb, sinks):
    nb = S // WIN

    def body(q_ref, kp_ref, kc_ref, vp_ref, vc_ref, dz_ref, o_ref, lse_ref, g_ref, s_ref,
             dq_ref, dg_ref, dka_ref, dkb_ref, dva_ref, dvb_ref, dsink_ref):
        kp, n = pl.program_id(0), pl.program_id(1)

        @pl.when((kp == 0) & (n == 0))
        def _():
            dsink_ref[...] = jnp.zeros_like(dsink_ref)

        valid = _swa_mask(n)
        g = g_ref[...]
        sg = _sigmoid(g)
        dzv = dz_ref[...]
        ov = o_ref[...]
        do = dzv * (g * sg)
        dg_ref[...] = (dzv * ov * (sg * (1.0 + g * (1.0 - sg)))).astype(BF16)
        prod = do * ov
        lane1 = _lane_iota((1, LANES))
        dqs, dkas, dkbs, dvas, dvbs = [], [], [], [], []
        dsink = jnp.zeros((1, LANES), F32)
        for kh in range(2):
            lo = HD * kh
            kb = jnp.concatenate([kp_ref[:, lo:lo + HD], kc_ref[:, lo:lo + HD]], axis=0)
            vb = jnp.concatenate([vp_ref[:, lo:lo + HD], vc_ref[:, lo:lo + HD]], axis=0)
            base = 4 * HD * kh
            qs = _stack4(q_ref, base)
            dos = _stack4(do, base).astype(BF16)
            delta = jnp.sum(_stack4(prod, base), axis=-1, keepdims=True)
            lse_s = jnp.concatenate([lse_ref[:, base + HD * gg: base + HD * gg + 1] for gg in range(4)], axis=0)
            s = jnp.where(valid, _dot_nt(qs, kb), -jnp.inf)
            p = jnp.exp(s - lse_s)
            ds = p * (_dot_nt(dos, vb) - delta)
            dsb = ds.astype(BF16)
            dqst = _dot_nn(dsb, kb)
            dkband = _dot_tn(dsb, qs)
            dvband = _dot_tn(p.astype(BF16), dos)
            for gg in range(4):
                dqs.append(dqst[WIN * gg:WIN * (gg + 1), :])
            dkbs.append(dkband[0:WIN, :])
            dkas.append(dkband[WIN:2 * WIN, :])
            dvbs.append(dvband[0:WIN, :])
            dvas.append(dvband[WIN:2 * WIN, :])
            first = (2 * kp + kh) * 4
            ps_delta = jnp.exp(_sink_col(s_ref, first) - lse_s) * delta
            for gg in range(4):
                val = jnp.sum(ps_delta[WIN * gg:WIN * (gg + 1), :], axis=0, keepdims=True)
                dsink = dsink - jnp.where(lane1 == first + gg, val, 0.0)
        dq_ref[...] = jnp.concatenate(dqs, axis=-1)
        dka_ref[...] = jnp.concatenate(dkas, axis=-1)
        dkb_ref[...] = jnp.concatenate(dkbs, axis=-1)
        dva_ref[...] = jnp.concatenate(dvas, axis=-1)
        dvb_ref[...] = jnp.concatenate(dvbs, axis=-1)
        dsink_ref[...] += dsink

    qblk = pl.BlockSpec((WIN, 512), lambda kp, n: (n, kp))
    prev = pl.BlockSpec((WIN, LANES), lambda kp, n: (jnp.maximum(n - 1, 0), kp))
    cur = pl.BlockSpec((WIN, LANES), lambda kp, n: (n, kp))
    return pl.pallas_call(
        body, name="swa_bwd", grid=(2, nb),
        in_specs=[qblk, prev, cur, prev, cur, qblk, qblk, qblk,
                  pl.BlockSpec((WIN, 512), lambda kp, n: (n, 2 + kp)), pl.BlockSpec(memory_space=pltpu.SMEM)],
        out_specs=[qblk, qblk, cur, cur, cur, cur, pl.BlockSpec((1, LANES), lambda kp, n: (0, 0))],
        out_shape=[_sds((S, D), F32), _sds((S, D), BF16)] + [_sds((S, 256), F32)] * 4 + [_sds((1, LANES), F32)],
        compiler_params=_params())(qb, ksh, ksh, vsh, vsh, dz, o, lse, pb, sinks)


def _prep_b_bwd(dq, dgate, pb, gq, cos2, sin2):
    def body(dq_ref, dgt_ref, x_ref, g_ref, c_ref, s_ref, o_ref, dgq_ref):
        @pl.when(pl.program_id(0) == 0)
        def _():
            dgq_ref[...] = jnp.zeros_like(dgq_ref)

        lane = _lane_iota((TM, LANES))
        g2, cos, sin = _g2(g_ref), c_ref[...], s_ref[...]
        dg_tot = jnp.zeros((1, LANES), F32)
        for c in _pairs(D):
            dn = _rope_bwd(dq_ref[:, c] * QSCALE, cos, sin, lane)
            dx, dg = _head_norm_bwd(dn, x_ref[:, c], g2, lane < HD)
            o_ref[:, c] = dx.astype(BF16)
            dg_tot = dg_tot + dg
        dgq_ref[...] += dg_tot
        o_ref[:, D:2 * D] = dgt_ref[...]

    row = pl.BlockSpec((TM, D), lambda i: (i, 0))
    tab = pl.BlockSpec((TM, LANES), lambda i: (i, 0))
    return pl.pallas_call(
        body, name="prep_b_bwd", grid=(S // TM,),
        in_specs=[row, row, row, pl.BlockSpec((1, HD), lambda i: (0, 0)), tab, tab],
        out_specs=[pl.BlockSpec((TM, 2 * D), lambda i: (i, 0)), pl.BlockSpec((1, LANES), lambda i: (0, 0))],
        out_shape=[_sds((S, 2 * D), BF16), _sds((1, LANES), F32)],
        compiler_params=_params())(dq, dgate, pb, gq, cos2, sin2)


def _prep_kv_bwd(dka, dkb, dva, dvb, kv, gk, cos2, sin2):
    nt = S // RB
    per = RB // WIN

    def shifted(cur_ref, nxt_ref, has_next):
        return jnp.concatenate([cur_ref[WIN:RB, :], jnp.where(has_next, nxt_ref[...], 0.0)], axis=0)

    def body(dka_ref, dkb_ref, dkn_ref, dva_ref, dvb_ref, dvn_ref, x_ref, g_ref, c_ref, s_ref, o_ref, dgk_ref):
        i, j = pl.program_id(0), pl.program_id(1)

        @pl.when((i == 0) & (j == 0))
        def _():
            dgk_ref[...] = jnp.zeros_like(dgk_ref)

        has_next = i < nt - 1

        @pl.when(j == 0)
        def _():
            lane = _lane_iota((RB, LANES))
            g2, cos, sin = _g2(g_ref), c_ref[...], s_ref[...]
            dy_all = dka_ref[...] + shifted(dkb_ref, dkn_ref, has_next)
            dg_tot = jnp.zeros((1, LANES), F32)
            for c in _pairs(CB):
                dn = _rope_bwd(dy_all[:, c], cos, sin, lane)
                dx, dg = _head_norm_bwd(dn, x_ref[:, c], g2, lane < HD)
                o_ref[:, c] = dx.astype(BF16)
                dg_tot = dg_tot + dg
            dgk_ref[...] += dg_tot

        @pl.when(j == 1)
        def _():
            o_ref[...] = (dva_ref[...] + shifted(dvb_ref, dvn_ref, has_next)).astype(BF16)

    cur = pl.BlockSpec((RB, CB), lambda i, j: (i, 0))
    nxt = pl.BlockSpec((WIN, CB), lambda i, j: (jnp.minimum(per * (i + 1), S // WIN - 1), 0))
    tab = pl.BlockSpec((RB, LANES), lambda i, j: (i, 0))
    return pl.pallas_call(
        body, name="prep_kv_bwd", grid=(nt, 2),
        in_specs=[cur, cur, nxt, cur, cur, nxt, cur, pl.BlockSpec((1, HD), lambda i, j: (0, 0)), tab, tab],
        out_specs=[pl.BlockSpec((RB, CB), lambda i, j: (i, j)), pl.BlockSpec((1, LANES), lambda i, j: (0, 0))],
        out_shape=[_sds((S, 2 * CB), BF16), _sds((1, LANES), F32)],
        compiler_params=_params())(dka, dkb, dkb, dva, dvb, dvb, kv, gk, cos2, sin2)


def _loss_dy(y, tgt):
    def body(y_ref, t_ref, dy_ref, l_ref):
        @pl.when(pl.program_id(0) == 0)
        def _():
            l_ref[...] = jnp.zeros_like(l_ref)

        e = y_ref[...] - t_ref[...]
        dy_ref[...] = e * (1.0 / D)
        l_ref[...] += jnp.sum(jnp.sum(e * e, axis=-1, keepdims=True), axis=0, keepdims=True)

    row = pl.BlockSpec((TM, D), lambda i: (i, 0))
    return pl.pallas_call(
        body, name="loss_dy", grid=(S // TM,), in_specs=[row, row],
        out_specs=[row, pl.BlockSpec((1, LANES), lambda i: (0, 0))],
        out_shape=[_sds((S, D), F32), _sds((1, LANES), F32)], compiler_params=_params())(y, tgt)


def _rms2_bwd(du_b, du_kv, h1, g_b, g_kv, dy):
    def body(dub_ref, dukv_ref, x_ref, gb_ref, gkv_ref, dy_ref, dh_ref, dgb_ref, dgkv_ref):
        @pl.when(pl.program_id(0) == 0)
        def _():
            dgb_ref[...] = jnp.zeros_like(dgb_ref)
            dgkv_ref[...] = jnp.zeros_like(dgkv_ref)

        x = x_ref[...]
        dx1, dg1 = _rms_bwd_core(dub_ref[...], x, gb_ref[...])
        dx2, dg2 = _rms_bwd_core(dukv_ref[...], x, gkv_ref[...])
        dh_ref[...] = dy_ref[...] + dx1 + dx2
        dgb_ref[...] += dg1
        dgkv_ref[...] += dg2

    row = pl.BlockSpec((TM, D), lambda i: (i, 0))
    vec = pl.BlockSpec((1, D), lambda i: (0, 0))
    return pl.pallas_call(
        body, name="rms2_bwd", grid=(S // TM,), in_specs=[row, row, row, vec, vec, row],
        out_specs=[row, vec, vec], out_shape=[_sds((S, D), F32), _sds((1, D), F32), _sds((1, D), F32)],
        compiler_params=_params())(du_b, du_kv, h1, g_b, g_kv, dy)


def _rms_bwd(du, x, g, dres):
    def body(du_ref, x_ref, g_ref, dr_ref, dx_ref, dg_ref):
        @pl.when(pl.program_id(0) == 0)
        def _():
            dg_ref[...] = jnp.zeros_like(dg_ref)

        dx, dg = _rms_bwd_core(du_ref[...], x_ref[...], g_ref[...])
        dx_ref[...] = dr_ref[...] + dx
        dg_ref[...] += dg

    row = pl.BlockSpec((TM, D), lambda i: (i, 0))
    vec = pl.BlockSpec((1, D), lambda i: (0, 0))
    return pl.pallas_call(
        body, name="rms_bwd", grid=(S // TM,), in_specs=[row, row, vec, row], out_specs=[row, vec],
        out_shape=[_sds((S, D), F32), _sds((1, D), F32)], compiler_params=_params())(du, x, g, dres)


def _gather_first(w_in_a, w_out_a, w_kv, w_in_b, w_out_b, norm_a_g):
    def body(wia_ref, woa_ref, wkv_ref, wib_ref, wob_ref, ga_ref,
             wa_g, ga_g, woa_s, wkv_s, wib_s, wob_s, wa_s, *sems):
        wa_s[...] = wia_ref[0].astype(BF16)
        woa_s[...] = woa_ref[0].astype(BF16)
        wkv_s[...] = wkv_ref[...].astype(BF16)
        wib_s[...] = wib_ref[0].astype(BF16)
        wob_s[...] = wob_ref[0].astype(BF16)
        _gather_two_level([wa_s, ga_ref], [wa_g, ga_g], sems)

    vmem = pl.BlockSpec(memory_space=pltpu.VMEM)
    anyspec = pl.BlockSpec(memory_space=pl.ANY)
    shard = lambda w: _sds(w.shape[-2:], BF16)
    return pl.pallas_call(
        body, name="gather_first", in_specs=[vmem] * 6, out_specs=[anyspec, anyspec, vmem, vmem, vmem, vmem],
        out_shape=[_sds((N_DEV,) + w_in_a.shape[-2:], BF16), _sds((N_DEV,) + norm_a_g.shape, F32),
                   shard(w_out_a), shard(w_kv), shard(w_in_b), shard(w_out_b)],
        scratch_shapes=[pltpu.VMEM(w_in_a.shape[-2:], BF16)] + _exchange_sems(2),
        compiler_params=pltpu.CompilerParams(vmem_limit_bytes=VMEM_LIMIT, has_side_effects=True))(
            w_in_a, w_out_a, w_kv, w_in_b, w_out_b, norm_a_g)


def _pair_reduce(slots):
    n_chip = N_DEV // 2
    _, rows, cols = slots.shape

    def body(s_ref, o_ref, own_v, sib_v, send_sems, recv_sems, local_sems):
        x, y, c = lax.axis_index("x"), lax.axis_index("y"), lax.axis_index("c")
        copies = []
        for j in range(n_chip):
            own = pltpu.make_async_copy(s_ref.at[2 * j + c], own_v.at[j], local_sems.at[j])
            give = pltpu.make_async_remote_copy(
                src_ref=s_ref.at[2 * j + 1 - c], dst_ref=sib_v.at[j], send_sem=send_sems.at[j],
                recv_sem=recv_sems.at[j], device_id=(x, y, 1 - c), device_id_type=pl.DeviceIdType.MESH)
            own.start()
            give.start()
            copies.append((own, give))
        for j, (own, give) in enumerate(copies):
            own.wait()
            give.wait()
            o_ref[j] = (own_v[j].astype(F32) + sib_v[j].astype(F32)).astype(BF16)

    half = _sds((n_chip, rows, cols), slots.dtype)
    return pl.pallas_call(
        body, name="pair_reduce", in_specs=[pl.BlockSpec(memory_space=pl.ANY)],
        out_specs=pl.BlockSpec(memory_space=pltpu.VMEM), out_shape=half,
        scratch_shapes=[pltpu.VMEM(half.shape, half.dtype), pltpu.VMEM(half.shape, half.dtype),
                        pltpu.SemaphoreType.DMA((n_chip,)), pltpu.SemaphoreType.DMA((n_chip,)),
                        pltpu.SemaphoreType.DMA((n_chip,))],
        compiler_params=pltpu.CompilerParams(vmem_limit_bytes=VMEM_LIMIT, has_side_effects=True))(slots)


def _padded_col(c):
    if c < RAW_F:
        return c
    return FOFF + (c - RAW_F) if c < RAW_G else GOFF + (c - RAW_G)


def _shard_pieces():
    width = NA_RAW // N_DEV
    pieces = []
    for d in range(N_DEV):
        cuts = [width * d] + [c for c in (RAW_F, RAW_G) if width * d < c < width * (d + 1)] + [width * (d + 1)]
        for lo, hi in zip(cuts[:-1], cuts[1:]):
            pieces.append((d, lo - width * d, _padded_col(lo), hi - lo))
    return pieces


def _unshard_wa(wa_g):
    def body(w_ref, o_ref):
        o_ref[:, FOFF + N_HEADS:NA] = jnp.zeros((TM, NA - FOFF - N_HEADS), BF16)
        for d, src, dst, width in _shard_pieces():
            o_ref[:, dst:dst + width] = w_ref[d, :, src:src + width]

    return pl.pallas_call(
        body, name="unshard_wa", grid=(D // TM,),
        in_specs=[pl.BlockSpec((N_DEV, TM, NA_RAW // N_DEV), lambda i: (0, i, 0))],
        out_specs=pl.BlockSpec((TM, NA), lambda i: (i, 0)), out_shape=_sds((D, NA), BF16),
        compiler_params=_params())(wa_g)


def _reshard_dwa(dwa):
    def body(g_ref, o_ref):
        for d, src, dst, width in _shard_pieces():
            o_ref[d, :, src:src + width] = g_ref[:, dst:dst + width]

    return pl.pallas_call(
        body, name="reshard_dwa", grid=(D // TM,), in_specs=[pl.BlockSpec((TM, NA), lambda i: (i, 0))],
        out_specs=pl.BlockSpec((N_DEV, TM, NA_RAW // N_DEV), lambda i: (0, i, 0)),
        out_shape=_sds((N_DEV, D, NA_RAW // N_DEV), dwa.dtype), compiler_params=_params())(dwa)


def _gather_slab(slab):
    def body(s_ref, o_ref, *sems):
        _exchange_ops(["gather_rows"], [s_ref], [o_ref], sems, True, True)

    anyspec = pl.BlockSpec(memory_space=pl.ANY)
    return pl.pallas_call(
        body, name="gather_slab", in_specs=[anyspec], out_specs=anyspec,
        out_shape=_sds((N_DEV,) + slab.shape, slab.dtype), scratch_shapes=_exchange_sems(1),
        compiler_params=pltpu.CompilerParams(has_side_effects=True))(slab)


def _adamw(w, g, m, v):
    m = ADAM_B1 * m + (1.0 - ADAM_B1) * g
    v = ADAM_B2 * v + (1.0 - ADAM_B2) * (g * g)
    m_hat = m / (1.0 - ADAM_B1 ** ADAM_STEP)
    v_hat = v / (1.0 - ADAM_B2 ** ADAM_STEP)
    delta = -ADAM_LR * (m_hat / (jnp.sqrt(v_hat) + ADAM_EPS) + ADAM_WD * w)
    return delta, m, v


def _sum_adamw(recv, w, m, v, name):
    lead = w.ndim - 2
    rows, cols = w.shape[-2:]
    tr = 128

    n_slots = recv.shape[0]

    def body(r_ref, w_ref, m_ref, v_ref, g_ref, d_ref, nm_ref, nv_ref):
        g = r_ref[0].astype(F32)
        for slot in range(1, n_slots):
            g = g + r_ref[slot].astype(F32)
        g_ref[...] = g
        d_ref[...], nm_ref[...], nv_ref[...] = _adamw(w_ref[...], g, m_ref[...], v_ref[...])

    blk = pl.BlockSpec((None,) * lead + (tr, cols), lambda i: (0,) * lead + (i, 0))
    return pl.pallas_call(
        body, name=name, grid=(rows // tr,),
        in_specs=[pl.BlockSpec((n_slots, tr, cols), lambda i: (0, i, 0)), blk, blk, blk],
        out_specs=[blk] * 4, out_shape=[_sds(w.shape, F32)] * 4,
        compiler_params=_params())(recv, w, m, v)


SLAB_ROWS = 16
SLOT = {"kv_norm_g": (8, 0, D), "norm_b_g": (9, 0, D), "b_forget": (10, 0, 16), "qnorm_a_g": (10, 128, HD),
        "knorm_a_g": (10, 256, HD), "knorm_b_g": (10, 384, HD), "qnorm_b_g": (10, 512, HD), "sinks": (10, 640, 16)}
SMALL = ["norm_a_g", "b_forget", "qnorm_a_g", "knorm_a_g", "kv_norm_g", "knorm_b_g", "norm_b_g", "qnorm_b_g", "sinks"]


LOSS_ROW = 11


def _pack_small(dg_a, dg_kv, dg_b, db_f, dgq_a, dgk_a, dgk_b, dgq_b, dsinks, lsum):
    def fold(ref):
        return ref[:, 0:HD] + ref[:, HD:2 * HD]

    def body(dga_ref, dgkv_ref, dgb_ref, dbf_ref, dgqa_ref, dgka_ref, dgkb_ref, dgqb_ref, dsk_ref, ls_ref, slab_ref):
        slab_ref[...] = jnp.zeros_like(slab_ref)
        for r in range(N_DEV):
            slab_ref[r:r + 1, 0:LANES] = dga_ref[:, LANES * r:LANES * (r + 1)]
        slab_ref[8:9, :] = dgkv_ref[...]
        slab_ref[9:10, :] = dgb_ref[...]
        slab_ref[10:11, 0:LANES] = dbf_ref[...]
        slab_ref[10:11, 128:128 + HD] = fold(dgqa_ref)
        slab_ref[10:11, 256:256 + HD] = fold(dgka_ref)
        slab_ref[10:11, 384:384 + HD] = fold(dgkb_ref)
        slab_ref[10:11, 512:512 + HD] = fold(dgqb_ref)
        slab_ref[10:11, 640:640 + LANES] = dsk_ref[...]
        slab_ref[LOSS_ROW:LOSS_ROW + 1, 0:LANES] = ls_ref[...]

    return pl.pallas_call(body, name="pack_small", out_shape=_sds((SLAB_ROWS, D), F32), compiler_params=_params())(
        dg_a, dg_kv, dg_b, db_f, dgq_a, dgk_a, dgk_b, dgq_b, dsinks, lsum)


def _small_adamw(recv, ws, ms, vs):
    k = len(SMALL)

    def body(*refs):
        r_ref = refs[0]
        w_refs, m_refs, v_refs = refs[1:1 + k], refs[1 + k:1 + 2 * k], refs[1 + 2 * k:1 + 3 * k]
        outs = refs[1 + 3 * k:1 + 7 * k]
        loss_ref, tot = refs[1 + 7 * k], refs[2 + 7 * k]
        g = r_ref[0]
        for dev in range(1, N_DEV):
            g = g + r_ref[dev]
        tot[...] = g
        loss_ref[...] = tot[LOSS_ROW:LOSS_ROW + 1, 0:LANES] * (0.5 / D)
        me = 4 * lax.axis_index("x") + 2 * lax.axis_index("y") + lax.axis_index("c")
        for p, name in enumerate(SMALL):
            if name == "norm_a_g":
                mine = lax.broadcasted_iota(jnp.int32, (N_DEV, LANES), 0) == me
                gp = jnp.sum(jnp.where(mine, tot[0:N_DEV, 0:LANES], 0.0), axis=0, keepdims=True)
            else:
                row, lo, width = SLOT[name]
                gp = tot[row:row + 1, lo:lo + width]
            d, nm, nv = _adamw(w_refs[p][...], gp, m_refs[p][...], v_refs[p][...])
            outs[p][...] = gp
            outs[k + p][...] = d
            outs[2 * k + p][...] = nm
            outs[3 * k + p][...] = nv

    shapes = [_sds(w.shape, F32) for w in ws]
    return pl.pallas_call(body, name="small_adamw", out_shape=shapes * 4 + [_sds((1, LANES), F32)],
                          scratch_shapes=[pltpu.VMEM((SLAB_ROWS, D), F32)],
                          compiler_params=_params())(recv, *ws, *ms, *vs)


def _rope_tables(positions):
    inv_freq = jnp.power(jnp.float32(ROPE_THETA), -jnp.arange(0, ROT, 2, dtype=F32) / ROT)
    ang = positions.astype(F32)[:, None] * inv_freq[None, :]
    cos, sin = jnp.cos(ang), jnp.sin(ang)
    c64 = jnp.concatenate([cos, cos, jnp.ones((S, HD - ROT), F32)], axis=-1)
    s64 = jnp.concatenate([-sin, sin, jnp.zeros((S, HD - ROT), F32)], axis=-1)
    return jnp.tile(c64, (1, 2)), jnp.tile(s64, (1, 2))


def _local_step(x, tgt, positions, g_a, wa, b_forget, gq_a, gk_a, g_kv, gk_b, g_b, gq_b, sinks,
                woa_s, wkv_s, wib_s, wob_s):
    nq = S // TQ
    cos2, sin2 = _rope_tables(positions)
    b_pad = jnp.pad(b_forget, ((0, 0), (0, LANES - N_HEADS)))

    u_a = _rms_fwd(x, g_a, "rms_a_fwd")
    proj = _mm(u_a, wa, "nn", S, 256, D, name="mm_in_a")
    qn, kn, vb = _prep_a(proj, gq_a, gk_a)
    ccol, cbc = _fgate_fwd(proj, b_pad)
    crow = ccol[:, :N_HEADS].T.reshape(N_HEADS, nq, 1, TQ)
    o_a, z_a, lse_a, woa_g, wkv_g, w_in_b, wob_g = _fox_fwd(
        qn, kn, vb, proj, crow, cbc,
        rider=[("gather_rows", woa_s), ("gather_rows", wkv_s), ("gather_cols", wib_s), ("gather_rows", wob_s)])
    w_out_a, w_kv, w_out_b = woa_g.reshape(D, D), wkv_g.reshape(D, 512), wob_g.reshape(D, D)
    h1 = _mm(z_a, w_out_a, "nn", 1024, 512, D, add=x, name="mm_out_a")
    u_kv, u_b = _rms2_fwd(h1, g_kv, g_b)
    kv = _mm(u_kv, w_kv, "nn", 1024, 512, D, name="mm_kv")
    pb = _mm(u_b, w_in_b, "nn", 1024, 512, D, name="mm_in_b")
    qb = _prep_b(pb, gq_b, cos2, sin2)
    ksh, vsh = _prep_kv(kv, gk_b, cos2, sin2)
    sinks1 = sinks.reshape(N_HEADS)
    o_b, z_b, lse_b = _swa_fwd(qb, ksh, vsh, pb, sinks1)
    y = _mm(z_b, w_out_b, "nn", 1024, 512, D, add=h1, name="mm_out_b")
    dy, lsum = _loss_dy(y, tgt)
    dw_out_b = _mm(z_b, dy, "tn", 512, 512, S, out_dtype=BF16, name="mm_dw_out_b")
    dz_b = _mm(dy, w_out_b, "nt", 1024, 512, D, name="mm_dz_b")
    dq_b, dgate_b, dka, dkb, dva, dvb, dsinks = _swa_bwd(qb, ksh, vsh, dz_b, o_b, lse_b, pb, sinks1)
    dpb, dgq_b = _prep_b_bwd(dq_b, dgate_b, pb, gq_b, cos2, sin2)
    dkv, dgk_b = _prep_kv_bwd(dka, dkb, dva, dvb, kv, gk_b, cos2, sin2)
    dw_in_b = _mm(u_b, dpb, "tn", 512, 512, S, out_dtype=BF16, name="mm_dw_in_b")
    du_b = _mm(dpb, w_in_b, "nt", 1024, 512, 2048, name="mm_du_b")
    dw_kv = _mm(u_kv, dkv, "tn", 512, 512, S, out_dtype=BF16, name="mm_dw_kv")
    du_kv = _mm(dkv, w_kv, "nt", 1024, 512, 512, name="mm_du_kv")
    dh1, dg_b, dg_kv = _rms2_bwd(du_b, du_kv, h1, g_b, g_kv, dy)
    dw_out_a = _mm(z_a, dh1, "tn", 512, 512, S, out_dtype=BF16, name="mm_dw_out_a")
    dz_a = _mm(dh1, w_out_a, "nt", 1024, 512, D, name="mm_dz_a")
    do_a, dgate_a, delta_a = _fox_bwd_pre(dz_a, proj, o_a)
    dk_a, dv_a, dcs, dq_a, drow, r_wob, r_wib, r_wkv, r_woa = _fox_bwd(
        qn, kn, vb, do_a, lse_a, delta_a, crow, cbc,
        rider=[("a2a_rows", dw_out_b), ("a2a_cols", dw_in_b), ("a2a_rows", dw_kv), ("a2a_rows", dw_out_a)])
    drow_col = jnp.pad(drow.reshape(N_HEADS, S).T, ((0, 0), (0, LANES - N_HEADS)))
    dproj, dgq_a, dgk_a, db_f = _prep_a_bwd(dq_a, dk_a, dv_a, dgate_a, drow_col, dcs, proj, b_pad, gq_a, gk_a)
    dwa = _mm(u_a, dproj, "tn", 1024, 256, S, out_dtype=BF16, name="mm_dw_in_a")
    du_a, r_wa = _mm(dproj, wa, "nt", 1024, 512, NA // 2, name="mm_du_a",
                     rider=[("a2a_chips", _pair_reduce(_reshard_dwa(dwa)))])
    dx, dg_a = _rms_bwd(du_a, x, g_a, dh1)
    slab = _pack_small(dg_a, dg_kv, dg_b, db_f, dgq_a, dgk_a, dgk_b, dgq_b, dsinks, lsum)
    return dx, r_wa, r_woa, r_wkv, r_wib, r_wob, _gather_slab(slab)


def kernel(x, positions, norm_a_g, w_in_a, b_forget, qnorm_a_g, knorm_a_g, w_out_a, kv_norm_g, w_kv, knorm_b_g, norm_b_g, w_in_b, qnorm_b_g, sinks, w_out_b, loss_target, m_norm_a_g, m_w_in_a, m_b_forget, m_qnorm_a_g, m_knorm_a_g, m_w_out_a, m_kv_norm_g, m_w_kv, m_knorm_b_g, m_norm_b_g, m_w_in_b, m_qnorm_b_g, m_sinks, m_w_out_b, v_norm_a_g, v_w_in_a, v_b_forget, v_qnorm_a_g, v_knorm_a_g, v_w_out_a, v_kv_norm_g, v_w_kv, v_knorm_b_g, v_norm_b_g, v_w_in_b, v_qnorm_b_g, v_sinks, v_w_out_b):
    wa_g, ga_g, woa_s, wkv_s, wib_s, wob_s = _gather_first(w_in_a, w_out_a, w_kv, w_in_b, w_out_b, norm_a_g)
    dx, r_wa, r_woa, r_wkv, r_wib, r_wob, slab_g = _local_step(
        x[0], loss_target[0], positions, ga_g.reshape(1, D), _unshard_wa(wa_g), b_forget, qnorm_a_g, knorm_a_g,
        kv_norm_g.reshape(1, D), knorm_b_g.reshape(1, HD), norm_b_g, qnorm_b_g, sinks, woa_s, wkv_s, wib_s, wob_s)

    big = {}
    for name, recv, w, m, v in (
            ("w_in_a", r_wa, w_in_a, m_w_in_a, v_w_in_a), ("w_out_a", r_woa, w_out_a, m_w_out_a, v_w_out_a),
            ("w_kv", r_wkv, w_kv, m_w_kv, v_w_kv), ("w_in_b", r_wib, w_in_b, m_w_in_b, v_w_in_b),
            ("w_out_b", r_wob, w_out_b, m_w_out_b, v_w_out_b)):
        big[name] = _sum_adamw(recv, w, m, v, "adamw_" + name)

    r2 = lambda a: a.reshape(1, -1)
    small_w = dict(norm_a_g=norm_a_g, b_forget=b_forget, qnorm_a_g=qnorm_a_g, knorm_a_g=knorm_a_g,
                   kv_norm_g=kv_norm_g, knorm_b_g=knorm_b_g, norm_b_g=norm_b_g, qnorm_b_g=qnorm_b_g, sinks=sinks)
    small_m = dict(norm_a_g=m_norm_a_g, b_forget=m_b_forget, qnorm_a_g=m_qnorm_a_g, knorm_a_g=m_knorm_a_g,
                   kv_norm_g=m_kv_norm_g, knorm_b_g=m_knorm_b_g, norm_b_g=m_norm_b_g, qnorm_b_g=m_qnorm_b_g,
                   sinks=m_sinks)
    small_v = dict(norm_a_g=v_norm_a_g, b_forget=v_b_forget, qnorm_a_g=v_qnorm_a_g, knorm_a_g=v_knorm_a_g,
                   kv_norm_g=v_kv_norm_g, knorm_b_g=v_knorm_b_g, norm_b_g=v_norm_b_g, qnorm_b_g=v_qnorm_b_g,
                   sinks=v_sinks)
    res = _small_adamw(slab_g, [r2(small_w[n]) for n in SMALL], [r2(small_m[n]) for n in SMALL],
                       [r2(small_v[n]) for n in SMALL])
    k = len(SMALL)
    small = {n: [res[q * k + p].reshape(small_w[n].shape) for q in range(4)] for p, n in enumerate(SMALL)}
    loss = res[4 * k][0, 0]

    order = ["norm_a_g", "w_in_a", "b_forget", "qnorm_a_g", "knorm_a_g", "w_out_a", "kv_norm_g", "w_kv",
             "knorm_b_g", "norm_b_g", "w_in_b", "qnorm_b_g", "sinks", "w_out_b"]

    def leaf(n, q):
        return big[n][q] if n in big else small[n][q]

    outs = [loss, dx[None]]
    for q in range(4):
        outs.extend(leaf(n, q) for n in order)
    return tuple(outs)
```

```python
import jax
import jax.numpy as jnp
from jax import lax
from jax.experimental import pallas as pl
from jax.experimental.pallas import tpu as pltpu

F32, BF16 = jnp.float32, jnp.bfloat16

S = 2048
D = 1024
HD = 64
N_HEADS = 16
N_DEV = 8
NA = 4352
GOFF = 3072
FOFF = 4096
RAW_F = 3072
RAW_G = RAW_F + N_HEADS
NA_RAW = 4112
EPS = 1e-6
QSCALE = 0.125
ROPE_THETA = 500000.0
ROT = 16
WIN = 128
TQ = 256
TK = 256
KS = TQ // 2
HPS = 8
HW = HPS * HD
TM = 256
RB = 512
CB = 256
LANES = 128

ADAM_LR, ADAM_B1, ADAM_B2, ADAM_EPS, ADAM_WD, ADAM_STEP = 0.001, 0.9, 0.999, 1e-08, 0.01, 10

VMEM_LIMIT = 56 * 1024 * 1024


def _params():
    return pltpu.CompilerParams(vmem_limit_bytes=VMEM_LIMIT)


def _sds(shape, dtype):
    return jax.ShapeDtypeStruct(shape, dtype)


def _dot_nt(a, b):
    return lax.dot_general(a, b, (((1,), (1,)), ((), ())), preferred_element_type=F32)


def _dot_tn(a, b):
    return lax.dot_general(a, b, (((0,), (0,)), ((), ())), preferred_element_type=F32)


def _dot_nn(a, b):
    return lax.dot_general(a, b, (((1,), (0,)), ((), ())), preferred_element_type=F32)


def _sigmoid(g):
    return 1.0 / (1.0 + jnp.exp(-g))


def _lane_iota(shape):
    return lax.broadcasted_iota(jnp.int32, shape, len(shape) - 1)


def _flips(kind):
    return (2, 4, 6) if kind == "a2a_chips" else tuple(range(1, N_DEV))


def _send_view(kind, ref, dev):
    if kind in ("gather_rows", "gather_cols"):
        return ref
    if kind == "a2a_slots":
        return ref.at[dev]
    if kind == "a2a_chips":
        return ref.at[dev >> 1]
    if kind == "a2a_rows":
        rows = ref.shape[0] // N_DEV
        return ref.at[pl.ds(pl.multiple_of(dev * rows, rows), rows)]
    cols = ref.shape[1] // N_DEV
    return ref.at[:, pl.ds(pl.multiple_of(dev * cols, cols), cols)]


def _land_view(kind, ref, dev):
    if kind == "gather_cols":
        cols = ref.shape[1] // N_DEV
        return ref.at[:, pl.ds(pl.multiple_of(dev * cols, cols), cols)]
    if kind == "a2a_chips":
        return ref.at[dev >> 1]
    return ref.at[dev]


def _landing_sds(kind, arr):
    if kind == "gather_rows":
        return _sds((N_DEV,) + arr.shape, arr.dtype)
    if kind == "gather_cols":
        return _sds((arr.shape[0], N_DEV * arr.shape[1]), arr.dtype)
    if kind == "a2a_rows":
        return _sds((N_DEV, arr.shape[0] // N_DEV, arr.shape[1]), arr.dtype)
    if kind == "a2a_cols":
        return _sds((N_DEV, arr.shape[0], arr.shape[1] // N_DEV), arr.dtype)
    return _sds(arr.shape, arr.dtype)


def _exchange_sems(n_parts):
    n = n_parts * (N_DEV - 1)
    return [pltpu.SemaphoreType.DMA((n,)), pltpu.SemaphoreType.DMA((n,)), pltpu.SemaphoreType.DMA((n_parts,))]


def _exchange_ops(kinds, srcs, dsts, sems, start, wait):
    send_sems, recv_sems, local_sems = sems
    x, y, c = lax.axis_index("x"), lax.axis_index("y"), lax.axis_index("c")
    me = 4 * x + 2 * y + c

    def local(a):
        return pltpu.make_async_copy(_send_view(kinds[a], srcs[a], me), _land_view(kinds[a], dsts[a], me),
                                     local_sems.at[a])

    def remote(a, k, landing_dev):
        peer = (x ^ ((k >> 2) & 1), y ^ ((k >> 1) & 1), c ^ (k & 1))
        sem = a * (N_DEV - 1) + k - 1
        return pltpu.make_async_remote_copy(
            src_ref=_send_view(kinds[a], srcs[a], me ^ k), dst_ref=_land_view(kinds[a], dsts[a], landing_dev),
            send_sem=send_sems.at[sem], recv_sem=recv_sems.at[sem], device_id=peer,
            device_id_type=pl.DeviceIdType.MESH)

    pairs = [(a, k) for k in range(1, N_DEV) for a in range(len(kinds)) if k in _flips(kinds[a])]
    if start:
        for a in range(len(kinds)):
            local(a).start()
        for a, k in pairs:
            remote(a, k, me).start()
    if wait:
        for a, k in pairs:
            remote(a, k, me ^ k).wait_recv()
            remote(a, k, me).wait_send()
        for a in range(len(kinds)):
            local(a).wait()


def _gather_two_level(srcs, dsts, sems):
    send_sems, recv_sems, local_sems = sems
    x, y, c = lax.axis_index("x"), lax.axis_index("y"), lax.axis_index("c")
    me, sibling = (x, y, c), (x, y, 1 - c)
    chips = [(1 - x, y), (x, 1 - y), (1 - x, 1 - y)]

    def slot(ref, dev):
        return ref.at[4 * dev[0] + 2 * dev[1] + dev[2]]

    def copy(a, k, block, to, src=None):
        return pltpu.make_async_remote_copy(
            src_ref=slot(dsts[a], block) if src is None else src, dst_ref=slot(dsts[a], block),
            send_sem=send_sems.at[a * (N_DEV - 1) + k], recv_sem=recv_sems.at[a * (N_DEV - 1) + k],
            device_id=to, device_id_type=pl.DeviceIdType.MESH)

    parts = range(len(srcs))
    mine = [pltpu.make_async_copy(srcs[a], slot(dsts[a], me), local_sems.at[a]) for a in parts]
    first = [copy(a, 0, me, sibling, src=srcs[a]) for a in parts]
    first += [copy(a, 1 + j, me, (*chip, c), src=srcs[a]) for j, chip in enumerate(chips) for a in parts]
    for cp in mine + first:
        cp.start()
    passed = []
    for j, chip in enumerate(chips):
        for a in parts:
            copy(a, 1 + j, (*chip, c), me).wait_recv()
            fwd = copy(a, 4 + j, (*chip, c), sibling)
            fwd.start()
            passed.append(fwd)
    for a in parts:
        copy(a, 0, sibling, me).wait_recv()
        for j, chip in enumerate(chips):
            copy(a, 4 + j, (*chip, 1 - c), me).wait_recv()
    for cp in first + passed:
        cp.wait_send()
    for cp in mine:
        cp.wait()


def _call(body, *, name, args, in_specs, out_specs, out_shape, grid=(), scratch_shapes=(), aliases=None, rider=()):
    n_in, n_out, n_scr, n_r = len(in_specs), len(out_specs), len(scratch_shapes), len(rider)
    kinds = [kind for kind, _ in rider]

    def kernel_body(*refs):
        c_in, r_in = refs[:n_in], refs[n_in:n_in + n_r]
        c_out = refs[n_in + n_r:n_in + n_r + n_out]
        r_out = refs[n_in + n_r + n_out:n_in + 2 * n_r + n_out]
        rest = refs[n_in + 2 * n_r + n_out:]
        c_scr, sems = rest[:n_scr], rest[n_scr:]
        if n_r:
            assert grid, "a rider needs a gridded call"
            ids = [pl.program_id(ax) for ax in range(len(grid))]
            first, last = ids[0] == 0, ids[0] == grid[0] - 1
            for pid, size in zip(ids[1:], grid[1:]):
                first = first & (pid == 0)
                last = last & (pid == size - 1)
            pl.when(first)(lambda: _exchange_ops(kinds, r_in, r_out, sems, True, False))
        body(*c_in, *c_out, *c_scr)
        if n_r:
            pl.when(last)(lambda: _exchange_ops(kinds, r_in, r_out, sems, False, True))

    anyspec = pl.BlockSpec(memory_space=pl.ANY)
    params = pltpu.CompilerParams(vmem_limit_bytes=VMEM_LIMIT, has_side_effects=bool(n_r))
    outs = pl.pallas_call(
        kernel_body, name=name, grid=grid, in_specs=list(in_specs) + [anyspec] * n_r,
        out_specs=list(out_specs) + [anyspec] * n_r,
        out_shape=list(out_shape) + [_landing_sds(kind, arr) for kind, arr in rider],
        scratch_shapes=list(scratch_shapes) + (_exchange_sems(n_r) if n_r else []),
        input_output_aliases=aliases or {}, compiler_params=params)(*args, *[arr for _, arr in rider])
    return list(outs)


def _mm(a, b, mode, tm, tn, tk, out_dtype=F32, add=None, name="mm", rider=()):
    if mode == "nn":
        (m, k), n = a.shape, b.shape[1]
        a_spec = pl.BlockSpec((tm, tk), lambda i, j, kk: (i, kk))
        b_spec = pl.BlockSpec((tk, tn), lambda i, j, kk: (kk, j))
        dot = _dot_nn
    elif mode == "nt":
        (m, k), n = a.shape, b.shape[0]
        a_spec = pl.BlockSpec((tm, tk), lambda i, j, kk: (i, kk))
        b_spec = pl.BlockSpec((tn, tk), lambda i, j, kk: (j, kk))
        dot = _dot_nt
    else:
        (k, m), n = a.shape, b.shape[1]
        a_spec = pl.BlockSpec((tk, tm), lambda i, j, kk: (kk, i))
        b_spec = pl.BlockSpec((tk, tn), lambda i, j, kk: (kk, j))
        dot = _dot_tn
    assert m % tm == 0 and n % tn == 0 and k % tk == 0, (m, n, k, tm, tn, tk)
    nk = k // tk
    has_add = add is not None

    def body(*refs):
        if has_add:
            a_ref, b_ref, add_ref, o_ref, acc = refs
        else:
            a_ref, b_ref, o_ref, acc = refs
        p = dot(a_ref[...].astype(BF16), b_ref[...].astype(BF16))

        def finish(total):
            if has_add:
                total = add_ref[...] + total
            o_ref[...] = total.astype(out_dtype)

        if nk == 1:
            finish(p)
        else:
            kk = pl.program_id(2)

            @pl.when(kk == 0)
            def _():
                acc[...] = p

            @pl.when(kk > 0)
            def _():
                acc[...] += p

            @pl.when(kk == nk - 1)
            def _():
                finish(acc[...])

    in_specs = [a_spec, b_spec]
    args = [a, b]
    if has_add:
        in_specs.append(pl.BlockSpec((tm, tn), lambda i, j, kk: (i, j)))
        args.append(add)
    acc_shape = (tm, tn) if nk > 1 else (8, LANES)
    outs = _call(body, name=name, args=args, grid=(m // tm, n // tn, nk), in_specs=in_specs,
                 out_specs=[pl.BlockSpec((tm, tn), lambda i, j, kk: (i, j))], out_shape=[_sds((m, n), out_dtype)],
                 scratch_shapes=[pltpu.VMEM(acc_shape, F32)], rider=rider)
    return outs if rider else outs[0]


def _rms_rinv(x):
    return lax.rsqrt(jnp.mean(x * x, axis=-1, keepdims=True) + EPS)


def _rms_bwd_core(du, x, g):
    r = _rms_rinv(x)
    dug = du * g
    dx = r * (dug - x * ((r * r) * jnp.mean(dug * x, axis=-1, keepdims=True)))
    dg = jnp.sum(du * (x * r), axis=0, keepdims=True)
    return dx, dg


def _half_sum(v, lo_half):
    s0 = jnp.sum(jnp.where(lo_half, v, 0.0), axis=-1, keepdims=True)
    s1 = jnp.sum(jnp.where(lo_half, 0.0, v), axis=-1, keepdims=True)
    return jnp.where(lo_half, s0, s1)


def _head_rinv(x, lo_half):
    return lax.rsqrt(_half_sum(x * x, lo_half) * (1.0 / HD) + EPS)


def _head_norm_bwd(dn, x, g, lo_half):
    r = _head_rinv(x, lo_half)
    dng = dn * g
    dx = r * (dng - x * ((r * r) * (_half_sum(dng * x, lo_half) * (1.0 / HD))))
    dg = jnp.sum(dn * (x * r), axis=0, keepdims=True)
    return dx, dg


def _rope_swap(x, lane):
    l64 = lane & (HD - 1)
    return jnp.where(l64 < ROT // 2, pltpu.roll(x, LANES - ROT // 2, 1), pltpu.roll(x, ROT // 2, 1))


def _rope_fwd(x, cos, sin, lane):
    return x * cos + _rope_swap(x, lane) * sin


def _rope_bwd(dy, cos, sin, lane):
    return dy * cos + jnp.where((lane & (HD - 1)) < ROT, _rope_swap(dy * sin, lane), 0.0)


def _g2(g_ref):
    g = g_ref[...]
    return jnp.concatenate([g, g], axis=-1)


def _pairs(width):
    return [slice(LANES * c, LANES * (c + 1)) for c in range(width // LANES)]


def _rms_fwd(x, g, name):
    def body(x_ref, g_ref, u_ref):
        xv = x_ref[...]
        u_ref[...] = ((xv * _rms_rinv(xv)) * g_ref[...]).astype(BF16)

    return pl.pallas_call(
        body, name=name, grid=(S // TM,),
        in_specs=[pl.BlockSpec((TM, D), lambda i: (i, 0)), pl.BlockSpec((1, D), lambda i: (0, 0))],
        out_specs=pl.BlockSpec((TM, D), lambda i: (i, 0)), out_shape=_sds((S, D), BF16),
        compiler_params=_params())(x, g)


def _rms2_fwd(h, g1, g2):
    def body(x_ref, g1_ref, g2_ref, u1_ref, u2_ref):
        xv = x_ref[...]
        xn = xv * _rms_rinv(xv)
        u1_ref[...] = (xn * g1_ref[...]).astype(BF16)
        u2_ref[...] = (xn * g2_ref[...]).astype(BF16)

    row = pl.BlockSpec((TM, D), lambda i: (i, 0))
    vec = pl.BlockSpec((1, D), lambda i: (0, 0))
    return pl.pallas_call(
        body, name="rms2_fwd", grid=(S // TM,), in_specs=[row, vec, vec], out_specs=[row, row],
        out_shape=[_sds((S, D), BF16)] * 2, compiler_params=_params())(h, g1, g2)


def _prep_a(proj, gq, gk):
    def body(q_ref, k_ref, v_ref, gq_ref, gk_ref, qo_ref, ko_ref, vo_ref):
        lo_half = _lane_iota((TM, LANES)) < HD
        gq2, gk2 = _g2(gq_ref), _g2(gk_ref)
        for c in _pairs(D):
            q = q_ref[:, c]
            k = k_ref[:, c]
            qo_ref[:, c] = (((q * _head_rinv(q, lo_half)) * gq2) * QSCALE).astype(BF16)
            ko_ref[:, c] = ((k * _head_rinv(k, lo_half)) * gk2).astype(BF16)
        vo_ref[...] = v_ref[...].astype(BF16)

    blk = lambda off: pl.BlockSpec((TM, D), lambda i: (i, off))
    gspec = pl.BlockSpec((1, HD), lambda i: (0, 0))
    return pl.pallas_call(
        body, name="prep_a", grid=(S // TM,),
        in_specs=[blk(0), blk(1), blk(2), gspec, gspec], out_specs=[blk(0)] * 3,
        out_shape=[_sds((S, D), BF16)] * 3, compiler_params=_params())(proj, proj, proj, gq, gk)


def _pick_lane(block, lane, idx):
    return jnp.sum(jnp.where(lane == idx, block, 0.0), axis=-1, keepdims=True)


def _fgate_fwd(proj, b_pad):
    def body(f_ref, b_ref, c_ref, cbc_ref, carry):
        @pl.when(pl.program_id(0) == 0)
        def _():
            carry[...] = jnp.zeros_like(carry)

        z = f_ref[...] + b_ref[...]
        logf = jnp.minimum(z, 0.0) - jnp.log1p(jnp.exp(-jnp.abs(z)))
        r = lax.broadcasted_iota(jnp.int32, (TM, TM), 0)
        c = lax.broadcasted_iota(jnp.int32, (TM, TM), 1)
        tri = (r >= c).astype(F32)
        loc = jnp.dot(tri, logf, precision=lax.Precision.HIGHEST, preferred_element_type=F32) + carry[0:1, :]
        c_ref[...] = loc
        carry[0:1, :] = loc[TM - 1:TM, :]
        lane = _lane_iota((TM, LANES))
        for h in range(N_HEADS):
            cbc_ref[:, LANES * h:LANES * (h + 1)] = jnp.broadcast_to(_pick_lane(loc, lane, h), (TM, LANES))

    return pl.pallas_call(
        body, name="fgate_fwd", grid=(S // TM,),
        in_specs=[pl.BlockSpec((TM, LANES), lambda i: (i, FOFF // LANES)), pl.BlockSpec((1, LANES), lambda i: (0, 0))],
        out_specs=[pl.BlockSpec((TM, LANES), lambda i: (i, 0)), pl.BlockSpec((TM, N_HEADS * LANES), lambda i: (i, 0))],
        out_shape=[_sds((S, LANES), F32), _sds((S, N_HEADS * LANES), F32)],
        scratch_shapes=[pltpu.VMEM((8, LANES), F32)], compiler_params=_params())(proj, b_pad)


def _key_le_query(offset, keys=TK):
    r = lax.broadcasted_iota(jnp.int32, (keys, TQ), 0)
    c = lax.broadcasted_iota(jnp.int32, (keys, TQ), 1)
    return (r + offset) <= c


def _widen(tile):
    return jnp.concatenate([tile] * (TQ // LANES), axis=1)


def _fox_fwd(qn, kn, vb, proj, crow, cbc, rider=()):
    nq, per = S // TQ, TQ // TK

    def body(q_ref, k_ref, v_ref, g_ref, cq_ref, cbc_ref, o_ref, z_ref, lse_ref, st_s, pt_s):
        i = pl.program_id(1)
        qs = [q_ref[:, HD * hh:HD * (hh + 1)] for hh in range(HPS)]
        cqs = [cq_ref[hh, 0] for hh in range(HPS)]

        def scores(s, hh):
            off = pl.multiple_of(s * KS, KS)
            kj = k_ref[pl.ds(off, KS), HD * hh:HD * (hh + 1)]
            return (_dot_nt(kj, qs[hh]) + cqs[hh]) - _widen(cbc_ref[pl.ds(off, KS), LANES * hh:LANES * (hh + 1)])

        def values(s, hh, pt):
            off = pl.multiple_of(s * KS, KS)
            return _dot_tn(v_ref[pl.ds(off, KS), HD * hh:HD * (hh + 1)], pt)

        def step(s, slot, carries, mask=None, last=False):
            if not last:
                for hh in range(HPS):
                    st_s[1 - slot, hh] = scores(s + 1, hh)
            pvs = [values(jnp.maximum(s - 1, 0), hh, pt_s[1 - slot, hh]) for hh in range(HPS)]
            out = []
            for hh in range(HPS):
                m, l, acc = carries[hh]
                st = st_s[slot, hh]
                if mask is not None:
                    st = jnp.where(mask, st, -jnp.inf)
                m_new = jnp.maximum(m, jnp.max(st, axis=0, keepdims=True))
                pt = jnp.exp(st - m_new)
                alpha = jnp.exp(m - m_new)
                pt_s[slot, hh] = pt.astype(BF16)
                out.append((m_new, alpha * l + jnp.sum(pt, axis=0, keepdims=True), alpha * (acc + pvs[hh])))
            return tuple(out)

        for hh in range(HPS):
            st_s[0, hh] = scores(0, hh)
            pt_s[1, hh] = jnp.zeros((KS, TQ), BF16)
        one = (jnp.full((1, TQ), -jnp.inf, F32), jnp.zeros((1, TQ), F32), jnp.zeros((HD, TQ), F32))
        carries = lax.fori_loop(0, i, lambda t, cr: step(2 * t + 1, 1, step(2 * t, 0, cr)), (one,) * HPS)
        carries = step(2 * i, 0, carries, mask=_key_le_query(0, KS))
        carries = step(2 * i + 1, 1, carries, mask=_key_le_query(KS, KS), last=True)
        accs = []
        for hh in range(HPS):
            m, l, acc = carries[hh]
            acc = acc + values(2 * i + 1, hh, pt_s[1, hh])
            accs.append(acc / l)
            lse_ref[hh, 0] = m + jnp.log(l)
        o = jnp.concatenate(accs, axis=0).T
        o_ref[...] = o
        g = g_ref[...]
        z_ref[...] = (o * (g * _sigmoid(g))).astype(BF16)

    qblk = pl.BlockSpec((TQ, HW), lambda hp, i: (i, hp))
    full = pl.BlockSpec((S, HW), lambda hp, i: (0, hp))
    rows = pl.BlockSpec((HPS, 1, 1, TQ), lambda hp, i: (hp, i, 0, 0))
    return _call(
        body, name="fox_fwd", args=(qn, kn, vb, proj, crow, cbc), grid=(N_HEADS // HPS, nq),
        in_specs=[qblk, full, full,
                  pl.BlockSpec((TQ, HW), lambda hp, i: (i, GOFF // HW + hp)),
                  rows, pl.BlockSpec((S, HPS * LANES), lambda hp, i: (0, hp))],
        out_specs=[qblk, qblk, rows],
        out_shape=[_sds((S, D), F32), _sds((S, D), BF16), _sds((N_HEADS, nq, 1, TQ), F32)],
        scratch_shapes=[pltpu.VMEM((2, HPS, KS, TQ), F32), pltpu.VMEM((2, HPS, KS, TQ), BF16)], rider=rider)


def _fox_bwd_pre(dz, proj, o):
    nq = S // TQ

    def body(dz_ref, g_ref, o_ref, do_ref, dg_ref, delta_ref):
        g = g_ref[...]
        sg = _sigmoid(g)
        dzv = dz_ref[...]
        ov = o_ref[...]
        do = dzv * (g * sg)
        dg_ref[...] = (dzv * ov * (sg * (1.0 + g * (1.0 - sg)))).astype(BF16)
        do_ref[...] = do.astype(BF16)
        prod_t = (do * ov).T
        for h in range(N_HEADS):
            delta_ref[h, 0] = jnp.sum(prod_t[HD * h:HD * (h + 1), :], axis=0, keepdims=True)

    row = pl.BlockSpec((TQ, D), lambda i: (i, 0))
    return pl.pallas_call(
        body, name="fox_bwd_pre", grid=(nq,),
        in_specs=[row, pl.BlockSpec((TQ, D), lambda i: (i, GOFF // D)), row],
        out_specs=[row, row, pl.BlockSpec((N_HEADS, 1, 1, TQ), lambda i: (0, i, 0, 0))],
        out_shape=[_sds((S, D), BF16), _sds((S, D), BF16), _sds((N_HEADS, nq, 1, TQ), F32)],
        compiler_params=_params())(dz, proj, o)


def _fox_bwd(qn, kn, vb, dob, lse, delta, crow, cbc, rider=()):
    nq, nkb = S // TQ, S // TK

    def body(q_ref, k_ref, v_ref, do_ref, lse_ref, del_ref, cq_ref, cbc_ref,
             dk_ref, dv_ref, dcs_ref, dq_ref, dr_ref, st_s, dp_s, pt_s, ds_s, dq_acc, dr_acc):
        j = pl.program_id(1)

        @pl.when(j == 0)
        def _():
            dq_acc[...] = jnp.zeros_like(dq_acc)
            dr_acc[...] = jnp.zeros_like(dr_acc)

        kjs = [k_ref[:, HD * hh:HD * (hh + 1)] for hh in range(HPS)]
        vjs = [v_ref[:, HD * hh:HD * (hh + 1)] for hh in range(HPS)]

        def rows_of(ref, u, hh):
            off = pl.multiple_of(u * TQ, TQ)
            return ref[pl.ds(off, TQ), HD * hh:HD * (hh + 1)]

        def products(u, hh):
            st = (_dot_nt(kjs[hh], rows_of(q_ref, u, hh)) + cq_ref[hh, u]) - _widen(
                cbc_ref[:, LANES * hh:LANES * (hh + 1)])
            return st, _dot_nt(vjs[hh], rows_of(do_ref, u, hh))

        def step(u, slot, carries, masked=False):
            nxt = jnp.minimum(u + 1, nq - 1)
            for hh in range(HPS):
                st_s[1 - slot, hh], dp_s[1 - slot, hh] = products(nxt, hh)
            prev = jnp.maximum(u - 1, 0)
            dvs = [_dot_nn(pt_s[1 - slot, hh], rows_of(do_ref, prev, hh)) for hh in range(HPS)]
            dks = [_dot_nn(ds_s[1 - slot, hh], rows_of(q_ref, prev, hh)) for hh in range(HPS)]
            for hh in range(HPS):
                dq_acc[hh, prev] += _dot_tn(kjs[hh], ds_s[1 - slot, hh])
            out = []
            for hh in range(HPS):
                dk, dv, dcs = carries[hh]
                st = st_s[slot, hh]
                if masked:
                    st = jnp.where(_key_le_query((j - u) * TQ), st, -jnp.inf)
                pt = jnp.exp(st - lse_ref[hh, u])
                dst = pt * (dp_s[slot, hh] - del_ref[hh, u])
                pt_s[slot, hh] = pt.astype(BF16)
                ds_s[slot, hh] = dst.astype(BF16)
                dr_acc[hh, u] += jnp.sum(dst, axis=0, keepdims=True)
                out.append((dk + dks[hh], dv + dvs[hh], dcs + (dst[:, :LANES] + dst[:, LANES:])))
            return tuple(out)

        t0 = j // 2
        for hh in range(HPS):
            st_s[0, hh], dp_s[0, hh] = products(2 * t0, hh)
            pt_s[1, hh] = jnp.zeros((TK, TQ), BF16)
            ds_s[1, hh] = jnp.zeros((TK, TQ), BF16)
        one = (jnp.zeros((TK, HD), F32), jnp.zeros((TK, HD), F32), jnp.zeros((TK, LANES), F32))
        carries = step(2 * t0 + 1, 1, step(2 * t0, 0, (one,) * HPS, masked=True), masked=True)
        carries = lax.fori_loop(t0 + 1, nq // 2, lambda t, cr: step(2 * t + 1, 1, step(2 * t, 0, cr)), carries)
        dks, dvs = [], []
        for hh in range(HPS):
            dk, dv, dcs = carries[hh]
            dks.append(dk + _dot_nn(ds_s[1, hh], rows_of(q_ref, nq - 1, hh)))
            dvs.append(dv + _dot_nn(pt_s[1, hh], rows_of(do_ref, nq - 1, hh)))
            dq_acc[hh, nq - 1] += _dot_tn(kjs[hh], ds_s[1, hh])
            dcs_ref[:, LANES * hh:LANES * (hh + 1)] = jnp.broadcast_to(
                -jnp.sum(dcs, axis=-1, keepdims=True), (TK, LANES))
        dk_ref[...] = jnp.concatenate(dks, axis=-1)
        dv_ref[...] = jnp.concatenate(dvs, axis=-1)

        @pl.when(j == nkb - 1)
        def _():
            for i in range(nq):
                dq_ref[TQ * i:TQ * (i + 1), :] = jnp.concatenate([dq_acc[hh, i] for hh in range(HPS)], axis=0).T
            dr_ref[...] = dr_acc[...]

    kblk = pl.BlockSpec((TK, HW), lambda hp, j: (j, hp))
    full = pl.BlockSpec((S, HW), lambda hp, j: (0, hp))
    rows = pl.BlockSpec((HPS, nq, 1, TQ), lambda hp, j: (hp, 0, 0, 0))
    cblk = pl.BlockSpec((TK, HPS * LANES), lambda hp, j: (j, hp))
    return _call(
        body, name="fox_bwd", args=(qn, kn, vb, dob, lse, delta, crow, cbc), grid=(N_HEADS // HPS, nkb),
        in_specs=[full, kblk, kblk, full, rows, rows, rows, cblk],
        out_specs=[kblk, kblk, cblk, full, rows],
        out_shape=[_sds((S, D), F32), _sds((S, D), F32), _sds((S, N_HEADS * LANES), F32), _sds((S, D), F32),
                   _sds((N_HEADS, nq, 1, TQ), F32)],
        scratch_shapes=[pltpu.VMEM((2, HPS, TK, TQ), F32), pltpu.VMEM((2, HPS, TK, TQ), F32),
                        pltpu.VMEM((2, HPS, TK, TQ), BF16), pltpu.VMEM((2, HPS, TK, TQ), BF16),
                        pltpu.VMEM((HPS, nq, HD, TQ), F32), pltpu.VMEM((HPS, nq, 1, TQ), F32)], rider=rider)


def _prep_a_bwd(dq, dk, dv, dgate, drow, dcs, proj, b_pad, gq, gk):
    nt = S // TM

    def body(dq_ref, dk_ref, dv_ref, dgt_ref, dr_ref, dcs_ref, xq_ref, xk_ref, f_ref, b_ref, gq_ref, gk_ref,
             o_ref, dgq_ref, dgk_ref, db_ref, carry):
        @pl.when(pl.program_id(0) == 0)
        def _():
            carry[...] = jnp.zeros_like(carry)
            dgq_ref[...] = jnp.zeros_like(dgq_ref)
            dgk_ref[...] = jnp.zeros_like(dgk_ref)
            db_ref[...] = jnp.zeros_like(db_ref)

        lane = _lane_iota((TM, LANES))
        lo_half = lane < HD
        gq2, gk2 = _g2(gq_ref), _g2(gk_ref)
        dgq, dgk = jnp.zeros((1, LANES), F32), jnp.zeros((1, LANES), F32)
        for c in _pairs(D):
            dxq, dg = _head_norm_bwd(dq_ref[:, c] * QSCALE, xq_ref[:, c], gq2, lo_half)
            o_ref[:, c] = dxq.astype(BF16)
            dgq = dgq + dg
            dxk, dg = _head_norm_bwd(dk_ref[:, c], xk_ref[:, c], gk2, lo_half)
            o_ref[:, D + c.start:D + c.stop] = dxk.astype(BF16)
            dgk = dgk + dg
        dgq_ref[...] += dgq
        dgk_ref[...] += dgk
        o_ref[:, 2 * D:3 * D] = dv_ref[...].astype(BF16)
        o_ref[:, GOFF:GOFF + D] = dgt_ref[...]

        dc = dr_ref[...]
        for h in range(N_HEADS):
            dc = dc + jnp.where(lane == h, dcs_ref[:, LANES * h:LANES * h + 1], 0.0)
        r = lax.broadcasted_iota(jnp.int32, (TM, TM), 0)
        c = lax.broadcasted_iota(jnp.int32, (TM, TM), 1)
        tri = (c >= r).astype(F32)
        dlogf = jnp.dot(tri, dc, precision=lax.Precision.HIGHEST, preferred_element_type=F32) + carry[0:1, :]
        carry[0:1, :] = dlogf[0:1, :]
        df = dlogf * (1.0 / (1.0 + jnp.exp(f_ref[...] + b_ref[...])))
        db_ref[...] += jnp.sum(df, axis=0, keepdims=True)
        o_ref[:, FOFF:FOFF + LANES] = df.astype(BF16)
        o_ref[:, FOFF + LANES:NA] = jnp.zeros((TM, NA - FOFF - LANES), BF16)

    rev = lambda width, col: pl.BlockSpec((TM, width), lambda i: (nt - 1 - i, col))
    gspec = pl.BlockSpec((1, HD), lambda i: (0, 0))
    acc = pl.BlockSpec((1, LANES), lambda i: (0, 0))
    return pl.pallas_call(
        body, name="prep_a_bwd", grid=(nt,),
        in_specs=[rev(D, 0), rev(D, 0), rev(D, 0), rev(D, 0), rev(LANES, 0), rev(N_HEADS * LANES, 0),
                  rev(D, 0), rev(D, 1), rev(LANES, FOFF // LANES), acc, gspec, gspec],
        out_specs=[rev(NA, 0), acc, acc, acc],
        out_shape=[_sds((S, NA), BF16)] + [_sds((1, LANES), F32)] * 3,
        scratch_shapes=[pltpu.VMEM((8, LANES), F32)],
        compiler_params=_params())(dq, dk, dv, dgate, drow, dcs, proj, proj, proj, b_pad, gq, gk)


def _prep_b(pb, gq, cos2, sin2):
    def body(x_ref, g_ref, c_ref, s_ref, o_ref):
        lane = _lane_iota((TM, LANES))
        g2, cos, sin = _g2(g_ref), c_ref[...], s_ref[...]
        for c in _pairs(D):
            x = x_ref[:, c]
            xn = (x * _head_rinv(x, lane < HD)) * g2
            o_ref[:, c] = (_rope_fwd(xn, cos, sin, lane) * QSCALE).astype(BF16)

    blk = pl.BlockSpec((TM, D), lambda i: (i, 0))
    tab = pl.BlockSpec((TM, LANES), lambda i: (i, 0))
    return pl.pallas_call(
        body, name="prep_b", grid=(S // TM,),
        in_specs=[blk, pl.BlockSpec((1, HD), lambda i: (0, 0)), tab, tab], out_specs=blk,
        out_shape=_sds((S, D), BF16), compiler_params=_params())(pb, gq, cos2, sin2)


def _prep_kv(kv, gk, cos2, sin2):
    def body(k_ref, v_ref, g_ref, c_ref, s_ref, ko_ref, vo_ref):
        lane = _lane_iota((RB, LANES))
        g2, cos, sin = _g2(g_ref), c_ref[...], s_ref[...]
        for c in _pairs(CB):
            x = k_ref[:, c]
            xn = (x * _head_rinv(x, lane < HD)) * g2
            ko_ref[:, c] = _rope_fwd(xn, cos, sin, lane).astype(BF16)
        vo_ref[...] = v_ref[...].astype(BF16)

    blk = lambda off: pl.BlockSpec((RB, CB), lambda i: (i, off))
    tab = pl.BlockSpec((RB, LANES), lambda i: (i, 0))
    return pl.pallas_call(
        body, name="prep_kv", grid=(S // RB,),
        in_specs=[blk(0), blk(1), pl.BlockSpec((1, HD), lambda i: (0, 0)), tab, tab],
        out_specs=[blk(0), blk(0)], out_shape=[_sds((S, CB), BF16)] * 2,
        compiler_params=_params())(kv, kv, gk, cos2, sin2)


N_KV, GRP = 4, 4


def _swa_mask(n):
    r = lax.broadcasted_iota(jnp.int32, (2 * WIN, GRP * WIN), 0)
    q = lax.broadcasted_iota(jnp.int32, (2 * WIN, GRP * WIN), 1) & (WIN - 1)
    return (r > q) & (r <= q + WIN) & ((r >= WIN) | (n > 0))


def _stack4(ref_or_val, base):
    return jnp.concatenate([ref_or_val[:, base + HD * g: base + HD * (g + 1)] for g in range(GRP)], axis=0)


def _unstack4(xt):
    return jnp.concatenate([xt[:, WIN * g:WIN * (g + 1)] for g in range(GRP)], axis=0).T


def _band(prev_ref, cur_ref, kh):
    return jnp.concatenate([prev_ref[:, HD * kh:HD * (kh + 1)], cur_ref[:, HD * kh:HD * (kh + 1)]], axis=0)


def _sink_row(s_ref, first):
    lane = _lane_iota((1, GRP * WIN))
    row = jnp.full((1, GRP * WIN), s_ref[first + GRP - 1], F32)
    for g in range(GRP - 2, -1, -1):
        row = jnp.where(lane < WIN * (g + 1), s_ref[first + g], row)
    return row


def _swa_fwd(qb, ksh, vsh, pb, sinks):
    nb = S // WIN

    def body(q_ref, kp_ref, kc_ref, vp_ref, vc_ref, g_ref, s_ref, o_ref, z_ref, lse_ref):
        n = pl.program_id(0)
        valid = _swa_mask(n)
        outs = []
        for kh in range(N_KV):
            kb, vb = _band(kp_ref, kc_ref, kh), _band(vp_ref, vc_ref, kh)
            st = jnp.where(valid, _dot_nt(kb, _stack4(q_ref, GRP * HD * kh)), -jnp.inf)
            sink = _sink_row(s_ref, GRP * kh)
            m = jnp.maximum(jnp.max(st, axis=0, keepdims=True), sink)
            pt = jnp.exp(st - m)
            l = jnp.sum(pt, axis=0, keepdims=True) + jnp.exp(sink - m)
            outs.append(_unstack4(_dot_tn(vb, pt.astype(BF16)) / l))
            lse = m + jnp.log(l)
            for g in range(GRP):
                lse_ref[GRP * kh + g, 0] = lse[:, WIN * g:WIN * (g + 1)]
        o = jnp.concatenate(outs, axis=-1)
        o_ref[...] = o
        g = g_ref[...]
        z_ref[...] = (o * (g * _sigmoid(g))).astype(BF16)

    row = pl.BlockSpec((WIN, D), lambda n: (n, 0))
    prev = pl.BlockSpec((WIN, N_KV * HD), lambda n: (jnp.maximum(n - 1, 0), 0))
    cur = pl.BlockSpec((WIN, N_KV * HD), lambda n: (n, 0))
    return pl.pallas_call(
        body, name="swa_fwd", grid=(nb,),
        in_specs=[row, prev, cur, prev, cur, pl.BlockSpec((WIN, D), lambda n: (n, 1)),
                  pl.BlockSpec(memory_space=pltpu.SMEM)],
        out_specs=[row, row, pl.BlockSpec((N_HEADS, 1, 1, WIN), lambda n: (0, n, 0, 0))],
        out_shape=[_sds((S, D), F32), _sds((S, D), BF16), _sds((N_HEADS, nb, 1, WIN), F32)],
        compiler_params=_params())(qb, ksh, ksh, vsh, vsh, pb, sinks)


def _swa_bwd(qb, ksh, vsh, dz, o, lse, pb, sinks):
    nb = S // WIN

    def body(q_ref, kp_ref, kc_ref, vp_ref, vc_ref, dz_ref, o_ref, lse_ref, g_ref, s_ref,
             dq_ref, dg_ref, dka_ref, dkb_ref, dva_ref, dvb_ref, dsink_ref):
        n = pl.program_id(0)

        @pl.when(n == 0)
        def _():
            dsink_ref[...] = jnp.zeros_like(dsink_ref)

        valid = _swa_mask(n)
        g = g_ref[...]
        sg = _sigmoid(g)
        dzv = dz_ref[...]
        ov = o_ref[...]
        do = dzv * (g * sg)
        dg_ref[...] = (dzv * ov * (sg * (1.0 + g * (1.0 - sg)))).astype(BF16)
        prod_t = (do * ov).T
        lane1 = _lane_iota((1, LANES))
        dqs, dkas, dkbs, dvas, dvbs = [], [], [], [], []
        dsink = jnp.zeros((1, LANES), F32)
        for kh in range(N_KV):
            kb, vb = _band(kp_ref, kc_ref, kh), _band(vp_ref, vc_ref, kh)
            base = GRP * HD * kh
            qs = _stack4(q_ref, base)
            dos = _stack4(do, base).astype(BF16)
            delta = jnp.concatenate(
                [jnp.sum(prod_t[base + HD * gg:base + HD * (gg + 1), :], axis=0, keepdims=True)
                 for gg in range(GRP)], axis=1)
            lse = jnp.concatenate([lse_ref[GRP * kh + gg, 0] for gg in range(GRP)], axis=1)
            st = jnp.where(valid, _dot_nt(kb, qs), -jnp.inf)
            pt = jnp.exp(st - lse)
            dst = pt * (_dot_nt(vb, dos) - delta)
            dsb = dst.astype(BF16)
            dqs.append(_unstack4(_dot_tn(kb, dsb)))
            dkband = _dot_nn(dsb, qs)
            dvband = _dot_nn(pt.astype(BF16), dos)
            dkbs.append(dkband[0:WIN, :])
            dkas.append(dkband[WIN:2 * WIN, :])
            dvbs.append(dvband[0:WIN, :])
            dvas.append(dvband[WIN:2 * WIN, :])
            ps_delta = jnp.exp(_sink_row(s_ref, GRP * kh) - lse) * delta
            for gg in range(GRP):
                val = jnp.sum(ps_delta[:, WIN * gg:WIN * (gg + 1)], axis=1, keepdims=True)
                dsink = dsink - jnp.where(lane1 == GRP * kh + gg, val, 0.0)
        dq_ref[...] = jnp.concatenate(dqs, axis=-1)
        dka_ref[...] = jnp.concatenate(dkas, axis=-1)
        dkb_ref[...] = jnp.concatenate(dkbs, axis=-1)
        dva_ref[...] = jnp.concatenate(dvas, axis=-1)
        dvb_ref[...] = jnp.concatenate(dvbs, axis=-1)
        dsink_ref[...] += dsink

    row = pl.BlockSpec((WIN, D), lambda n: (n, 0))
    prev = pl.BlockSpec((WIN, N_KV * HD), lambda n: (jnp.maximum(n - 1, 0), 0))
    cur = pl.BlockSpec((WIN, N_KV * HD), lambda n: (n, 0))
    return pl.pallas_call(
        body, name="swa_bwd", grid=(nb,),
        in_specs=[row, prev, cur, prev, cur, row, row, pl.BlockSpec((N_HEADS, 1, 1, WIN), lambda n: (0, n, 0, 0)),
                  pl.BlockSpec((WIN, D), lambda n: (n, 1)), pl.BlockSpec(memory_space=pltpu.SMEM)],
        out_specs=[row, row, cur, cur, cur, cur, pl.BlockSpec((1, LANES), lambda n: (0, 0))],
        out_shape=[_sds((S, D), F32), _sds((S, D), BF16)] + [_sds((S, 256), F32)] * 4 + [_sds((1, LANES), F32)],
        compiler_params=_params())(qb, ksh, ksh, vsh, vsh, dz, o, lse, pb, sinks)


def _prep_b_bwd(dq, dgate, pb, gq, cos2, sin2):
    def body(dq_ref, dgt_ref, x_ref, g_ref, c_ref, s_ref, o_ref, dgq_ref):
        @pl.when(pl.program_id(0) == 0)
        def _():
            dgq_ref[...] = jnp.zeros_like(dgq_ref)

        lane = _lane_iota((TM, LANES))
        g2, cos, sin = _g2(g_ref), c_ref[...], s_ref[...]
        dg_tot = jnp.zeros((1, LANES), F32)
        for c in _pairs(D):
            dn = _rope_bwd(dq_ref[:, c] * QSCALE, cos, sin, lane)
            dx, dg = _head_norm_bwd(dn, x_ref[:, c], g2, lane < HD)
            o_ref[:, c] = dx.astype(BF16)
            dg_tot = dg_tot + dg
        dgq_ref[...] += dg_tot
        o_ref[:, D:2 * D] = dgt_ref[...]

    row = pl.BlockSpec((TM, D), lambda i: (i, 0))
    tab = pl.BlockSpec((TM, LANES), lambda i: (i, 0))
    return pl.pallas_call(
        body, name="prep_b_bwd", grid=(S // TM,),
        in_specs=[row, row, row, pl.BlockSpec((1, HD), lambda i: (0, 0)), tab, tab],
        out_specs=[pl.BlockSpec((TM, 2 * D), lambda i: (i, 0)), pl.BlockSpec((1, LANES), lambda i: (0, 0))],
        out_shape=[_sds((S, 2 * D), BF16), _sds((1, LANES), F32)],
        compiler_params=_params())(dq, dgate, pb, gq, cos2, sin2)


def _prep_kv_bwd(dka, dkb, dva, dvb, kv, gk, cos2, sin2):
    nt = S // RB
    per = RB // WIN

    def shifted(cur_ref, nxt_ref, has_next):
        return jnp.concatenate([cur_ref[WIN:RB, :], jnp.where(has_next, nxt_ref[...], 0.0)], axis=0)

    def body(dka_ref, dkb_ref, dkn_ref, dva_ref, dvb_ref, dvn_ref, x_ref, g_ref, c_ref, s_ref, o_ref, dgk_ref):
        i, j = pl.program_id(0), pl.program_id(1)

        @pl.when((i == 0) & (j == 0))
        def _():
            dgk_ref[...] = jnp.zeros_like(dgk_ref)

        has_next = i < nt - 1

        @pl.when(j == 0)
        def _():
            lane = _lane_iota((RB, LANES))
            g2, cos, sin = _g2(g_ref), c_ref[...], s_ref[...]
            dy_all = dka_ref[...] + shifted(dkb_ref, dkn_ref, has_next)
            dg_tot = jnp.zeros((1, LANES), F32)
            for c in _pairs(CB):
                dn = _rope_bwd(dy_all[:, c], cos, sin, lane)
                dx, dg = _head_norm_bwd(dn, x_ref[:, c], g2, lane < HD)
                o_ref[:, c] = dx.astype(BF16)
                dg_tot = dg_tot + dg
            dgk_ref[...] += dg_tot

        @pl.when(j == 1)
        def _():
            o_ref[...] = (dva_ref[...] + shifted(dvb_ref, dvn_ref, has_next)).astype(BF16)

    cur = pl.BlockSpec((RB, CB), lambda i, j: (i, 0))
    nxt = pl.BlockSpec((WIN, CB), lambda i, j: (jnp.minimum(per * (i + 1), S // WIN - 1), 0))
    tab = pl.BlockSpec((RB, LANES), lambda i, j: (i, 0))
    return pl.pallas_call(
        body, name="prep_kv_bwd", grid=(nt, 2),
        in_specs=[cur, cur, nxt, cur, cur, nxt, cur, pl.BlockSpec((1, HD), lambda i, j: (0, 0)), tab, tab],
        out_specs=[pl.BlockSpec((RB, CB), lambda i, j: (i, j)), pl.BlockSpec((1, LANES), lambda i, j: (0, 0))],
        out_shape=[_sds((S, 2 * CB), BF16), _sds((1, LANES), F32)],
        compiler_params=_params())(dka, dkb, dkb, dva, dvb, dvb, kv, gk, cos2, sin2)


def _loss_dy(y, tgt):
    def body(y_ref, t_ref, dy_ref, l_ref):
        @pl.when(pl.program_id(0) == 0)
        def _():
            l_ref[...] = jnp.zeros_like(l_ref)

        e = y_ref[...] - t_ref[...]
        dy_ref[...] = e * (1.0 / D)
        l_ref[...] += jnp.sum(jnp.sum(e * e, axis=-1, keepdims=True), axis=0, keepdims=True)

    row = pl.BlockSpec((TM, D), lambda i: (i, 0))
    return pl.pallas_call(
        body, name="loss_dy", grid=(S // TM,), in_specs=[row, row],
        out_specs=[row, pl.BlockSpec((1, LANES), lambda i: (0, 0))],
        out_shape=[_sds((S, D), F32), _sds((1, LANES), F32)], compiler_params=_params())(y, tgt)


def _rms2_bwd(du_b, du_kv, h1, g_b, g_kv, dy):
    def body(dub_ref, dukv_ref, x_ref, gb_ref, gkv_ref, dy_ref, dh_ref, dgb_ref, dgkv_ref):
        @pl.when(pl.program_id(0) == 0)
        def _():
            dgb_ref[...] = jnp.zeros_like(dgb_ref)
            dgkv_ref[...] = jnp.zeros_like(dgkv_ref)

        x = x_ref[...]
        dx1, dg1 = _rms_bwd_core(dub_ref[...], x, gb_ref[...])
        dx2, dg2 = _rms_bwd_core(dukv_ref[...], x, gkv_ref[...])
        dh_ref[...] = dy_ref[...] + dx1 + dx2
        dgb_ref[...] += dg1
        dgkv_ref[...] += dg2

    row = pl.BlockSpec((TM, D), lambda i: (i, 0))
    vec = pl.BlockSpec((1, D), lambda i: (0, 0))
    return pl.pallas_call(
        body, name="rms2_bwd", grid=(S // TM,), in_specs=[row, row, row, vec, vec, row],
        out_specs=[row, vec, vec], out_shape=[_sds((S, D), F32), _sds((1, D), F32), _sds((1, D), F32)],
        compiler_params=_params())(du_b, du_kv, h1, g_b, g_kv, dy)


def _rms_bwd(du, x, g, dres):
    def body(du_ref, x_ref, g_ref, dr_ref, dx_ref, dg_ref):
        @pl.when(pl.program_id(0) == 0)
        def _():
            dg_ref[...] = jnp.zeros_like(dg_ref)

        dx, dg = _rms_bwd_core(du_ref[...], x_ref[...], g_ref[...])
        dx_ref[...] = dr_ref[...] + dx
        dg_ref[...] += dg

    row = pl.BlockSpec((TM, D), lambda i: (i, 0))
    vec = pl.BlockSpec((1, D), lambda i: (0, 0))
    return pl.pallas_call(
        body, name="rms_bwd", grid=(S // TM,), in_specs=[row, row, vec, row], out_specs=[row, vec],
        out_shape=[_sds((S, D), F32), _sds((1, D), F32)], compiler_params=_params())(du, x, g, dres)


def _gather_first(w_in_a, w_out_a, w_kv, w_in_b, w_out_b, norm_a_g):
    def body(wia_ref, woa_ref, wkv_ref, wib_ref, wob_ref, ga_ref,
             wa_g, ga_g, woa_s, wkv_s, wib_s, wob_s, wa_s, *sems):
        wa_s[...] = wia_ref[0].astype(BF16)
        woa_s[...] = woa_ref[0].astype(BF16)
        wkv_s[...] = wkv_ref[...].astype(BF16)
        wib_s[...] = wib_ref[0].astype(BF16)
        wob_s[...] = wob_ref[0].astype(BF16)
        _gather_two_level([wa_s, ga_ref], [wa_g, ga_g], sems)

    vmem = pl.BlockSpec(memory_space=pltpu.VMEM)
    anyspec = pl.BlockSpec(memory_space=pl.ANY)
    shard = lambda w: _sds(w.shape[-2:], BF16)
    return pl.pallas_call(
        body, name="gather_first", in_specs=[vmem] * 6, out_specs=[anyspec, anyspec, vmem, vmem, vmem, vmem],
        out_shape=[_sds((N_DEV,) + w_in_a.shape[-2:], BF16), _sds((N_DEV,) + norm_a_g.shape, F32),
                   shard(w_out_a), shard(w_kv), shard(w_in_b), shard(w_out_b)],
        scratch_shapes=[pltpu.VMEM(w_in_a.shape[-2:], BF16)] + _exchange_sems(2),
        compiler_params=pltpu.CompilerParams(vmem_limit_bytes=VMEM_LIMIT, has_side_effects=True))(
            w_in_a, w_out_a, w_kv, w_in_b, w_out_b, norm_a_g)


def _pair_reduce(slots):
    n_chip = N_DEV // 2
    _, rows, cols = slots.shape

    def body(s_ref, o_ref, own_v, sib_v, send_sems, recv_sems, local_sems):
        x, y, c = lax.axis_index("x"), lax.axis_index("y"), lax.axis_index("c")
        copies = []
        for j in range(n_chip):
            own = pltpu.make_async_copy(s_ref.at[2 * j + c], own_v.at[j], local_sems.at[j])
            give = pltpu.make_async_remote_copy(
                src_ref=s_ref.at[2 * j + 1 - c], dst_ref=sib_v.at[j], send_sem=send_sems.at[j],
                recv_sem=recv_sems.at[j], device_id=(x, y, 1 - c), device_id_type=pl.DeviceIdType.MESH)
            own.start()
            give.start()
            copies.append((own, give))
        for j, (own, give) in enumerate(copies):
            own.wait()
            give.wait()
            o_ref[j] = (own_v[j].astype(F32) + sib_v[j].astype(F32)).astype(BF16)

    half = _sds((n_chip, rows, cols), slots.dtype)
    return pl.pallas_call(
        body, name="pair_reduce", in_specs=[pl.BlockSpec(memory_space=pl.ANY)],
        out_specs=pl.BlockSpec(memory_space=pltpu.VMEM), out_shape=half,
        scratch_shapes=[pltpu.VMEM(half.shape, half.dtype), pltpu.VMEM(half.shape, half.dtype),
                        pltpu.SemaphoreType.DMA((n_chip,)), pltpu.SemaphoreType.DMA((n_chip,)),
                        pltpu.SemaphoreType.DMA((n_chip,))],
        compiler_params=pltpu.CompilerParams(vmem_limit_bytes=VMEM_LIMIT, has_side_effects=True))(slots)


def _padded_col(c):
    if c < RAW_F:
        return c
    return FOFF + (c - RAW_F) if c < RAW_G else GOFF + (c - RAW_G)


def _shard_pieces():
    width = NA_RAW // N_DEV
    pieces = []
    for d in range(N_DEV):
        cuts = [width * d] + [c for c in (RAW_F, RAW_G) if width * d < c < width * (d + 1)] + [width * (d + 1)]
        for lo, hi in zip(cuts[:-1], cuts[1:]):
            pieces.append((d, lo - width * d, _padded_col(lo), hi - lo))
    return pieces


def _unshard_wa(wa_g):
    def body(w_ref, o_ref):
        o_ref[:, FOFF + N_HEADS:NA] = jnp.zeros((TM, NA - FOFF - N_HEADS), BF16)
        for d, src, dst, width in _shard_pieces():
            o_ref[:, dst:dst + width] = w_ref[d, :, src:src + width]

    return pl.pallas_call(
        body, name="unshard_wa", grid=(D // TM,),
        in_specs=[pl.BlockSpec((N_DEV, TM, NA_RAW // N_DEV), lambda i: (0, i, 0))],
        out_specs=pl.BlockSpec((TM, NA), lambda i: (i, 0)), out_shape=_sds((D, NA), BF16),
        compiler_params=_params())(wa_g)


def _reshard_dwa(dwa):
    def body(g_ref, o_ref):
        for d, src, dst, width in _shard_pieces():
            o_ref[d, :, src:src + width] = g_ref[:, dst:dst + width]

    return pl.pallas_call(
        body, name="reshard_dwa", grid=(D // TM,), in_specs=[pl.BlockSpec((TM, NA), lambda i: (i, 0))],
        out_specs=pl.BlockSpec((N_DEV, TM, NA_RAW // N_DEV), lambda i: (0, i, 0)),
        out_shape=_sds((N_DEV, D, NA_RAW // N_DEV), dwa.dtype), compiler_params=_params())(dwa)


def _gather_slab(slab):
    def body(s_ref, o_ref, *sems):
        _exchange_ops(["gather_rows"], [s_ref], [o_ref], sems, True, True)

    anyspec = pl.BlockSpec(memory_space=pl.ANY)
    return pl.pallas_call(
        body, name="gather_slab", in_specs=[anyspec], out_specs=anyspec,
        out_shape=_sds((N_DEV,) + slab.shape, slab.dtype), scratch_shapes=_exchange_sems(1),
        compiler_params=pltpu.CompilerParams(has_side_effects=True))(slab)


def _adamw(w, g, m, v):
    m = ADAM_B1 * m + (1.0 - ADAM_B1) * g
    v = ADAM_B2 * v + (1.0 - ADAM_B2) * (g * g)
    m_hat = m / (1.0 - ADAM_B1 ** ADAM_STEP)
    v_hat = v / (1.0 - ADAM_B2 ** ADAM_STEP)
    delta = -ADAM_LR * (m_hat / (jnp.sqrt(v_hat) + ADAM_EPS) + ADAM_WD * w)
    return delta, m, v


def _sum_adamw(recv, w, m, v, name):
    lead = w.ndim - 2
    rows, cols = w.shape[-2:]
    tr = 128

    n_slots = recv.shape[0]

    def body(r_ref, w_ref, m_ref, v_ref, g_ref, d_ref, nm_ref, nv_ref):
        g = r_ref[0].astype(F32)
        for slot in range(1, n_slots):
            g = g + r_ref[slot].astype(F32)
        g_ref[...] = g
        d_ref[...], nm_ref[...], nv_ref[...] = _adamw(w_ref[...], g, m_ref[...], v_ref[...])

    blk = pl.BlockSpec((None,) * lead + (tr, cols), lambda i: (0,) * lead + (i, 0))
    return pl.pallas_call(
        body, name=name, grid=(rows // tr,),
        in_specs=[pl.BlockSpec((n_slots, tr, cols), lambda i: (0, i, 0)), blk, blk, blk],
        out_specs=[blk] * 4, out_shape=[_sds(w.shape, F32)] * 4,
        compiler_params=_params())(recv, w, m, v)


SLAB_ROWS = 16
SLOT = {"kv_norm_g": (8, 0, D), "norm_b_g": (9, 0, D), "b_forget": (10, 0, 16), "qnorm_a_g": (10, 128, HD),
        "knorm_a_g": (10, 256, HD), "knorm_b_g": (10, 384, HD), "qnorm_b_g": (10, 512, HD), "sinks": (10, 640, 16)}
SMALL = ["norm_a_g", "b_forget", "qnorm_a_g", "knorm_a_g", "kv_norm_g", "knorm_b_g", "norm_b_g", "qnorm_b_g", "sinks"]


LOSS_ROW = 11


def _pack_small(dg_a, dg_kv, dg_b, db_f, dgq_a, dgk_a, dgk_b, dgq_b, dsinks, lsum):
    def fold(ref):
        return ref[:, 0:HD] + ref[:, HD:2 * HD]

    def body(dga_ref, dgkv_ref, dgb_ref, dbf_ref, dgqa_ref, dgka_ref, dgkb_ref, dgqb_ref, dsk_ref, ls_ref, slab_ref):
        slab_ref[...] = jnp.zeros_like(slab_ref)
        for r in range(N_DEV):
            slab_ref[r:r + 1, 0:LANES] = dga_ref[:, LANES * r:LANES * (r + 1)]
        slab_ref[8:9, :] = dgkv_ref[...]
        slab_ref[9:10, :] = dgb_ref[...]
        slab_ref[10:11, 0:LANES] = dbf_ref[...]
        slab_ref[10:11, 128:128 + HD] = fold(dgqa_ref)
        slab_ref[10:11, 256:256 + HD] = fold(dgka_ref)
        slab_ref[10:11, 384:384 + HD] = fold(dgkb_ref)
        slab_ref[10:11, 512:512 + HD] = fold(dgqb_ref)
        slab_ref[10:11, 640:640 + LANES] = dsk_ref[...]
        slab_ref[LOSS_ROW:LOSS_ROW + 1, 0:LANES] = ls_ref[...]

    return pl.pallas_call(body, name="pack_small", out_shape=_sds((SLAB_ROWS, D), F32), compiler_params=_params())(
        dg_a, dg_kv, dg_b, db_f, dgq_a, dgk_a, dgk_b, dgq_b, dsinks, lsum)


def _small_adamw(recv, ws, ms, vs):
    k = len(SMALL)

    def body(*refs):
        r_ref = refs[0]
        w_refs, m_refs, v_refs = refs[1:1 + k], refs[1 + k:1 + 2 * k], refs[1 + 2 * k:1 + 3 * k]
        outs = refs[1 + 3 * k:1 + 7 * k]
        loss_ref, tot = refs[1 + 7 * k], refs[2 + 7 * k]
        g = r_ref[0]
        for dev in range(1, N_DEV):
            g = g + r_ref[dev]
        tot[...] = g
        loss_ref[...] = tot[LOSS_ROW:LOSS_ROW + 1, 0:LANES] * (0.5 / D)
        me = 4 * lax.axis_index("x") + 2 * lax.axis_index("y") + lax.axis_index("c")
        for p, name in enumerate(SMALL):
            if name == "norm_a_g":
                mine = lax.broadcasted_iota(jnp.int32, (N_DEV, LANES), 0) == me
                gp = jnp.sum(jnp.where(mine, tot[0:N_DEV, 0:LANES], 0.0), axis=0, keepdims=True)
            else:
                row, lo, width = SLOT[name]
                gp = tot[row:row + 1, lo:lo + width]
            d, nm, nv = _adamw(w_refs[p][...], gp, m_refs[p][...], v_refs[p][...])
            outs[p][...] = gp
            outs[k + p][...] = d
            outs[2 * k + p][...] = nm
            outs[3 * k + p][...] = nv

    shapes = [_sds(w.shape, F32) for w in ws]
    return pl.pallas_call(body, name="small_adamw", out_shape=shapes * 4 + [_sds((1, LANES), F32)],
                          scratch_shapes=[pltpu.VMEM((SLAB_ROWS, D), F32)],
                          compiler_params=_params())(recv, *ws, *ms, *vs)


def _rope_tables(positions):
    inv_freq = jnp.power(jnp.float32(ROPE_THETA), -jnp.arange(0, ROT, 2, dtype=F32) / ROT)
    ang = positions.astype(F32)[:, None] * inv_freq[None, :]
    cos, sin = jnp.cos(ang), jnp.sin(ang)
    c64 = jnp.concatenate([cos, cos, jnp.ones((S, HD - ROT), F32)], axis=-1)
    s64 = jnp.concatenate([-sin, sin, jnp.zeros((S, HD - ROT), F32)], axis=-1)
    return jnp.tile(c64, (1, 2)), jnp.tile(s64, (1, 2))


def _local_step(x, tgt, positions, g_a, wa, b_forget, gq_a, gk_a, g_kv, gk_b, g_b, gq_b, sinks,
                woa_s, wkv_s, wib_s, wob_s):
    nq = S // TQ
    cos2, sin2 = _rope_tables(positions)
    b_pad = jnp.pad(b_forget, ((0, 0), (0, LANES - N_HEADS)))

    u_a = _rms_fwd(x, g_a, "rms_a_fwd")
    proj = _mm(u_a, wa, "nn", S, 256, D, name="mm_in_a")
    qn, kn, vb = _prep_a(proj, gq_a, gk_a)
    ccol, cbc = _fgate_fwd(proj, b_pad)
    crow = ccol[:, :N_HEADS].T.reshape(N_HEADS, nq, 1, TQ)
    o_a, z_a, lse_a, woa_g, wkv_g, w_in_b, wob_g = _fox_fwd(
        qn, kn, vb, proj, crow, cbc,
        rider=[("gather_rows", woa_s), ("gather_rows", wkv_s), ("gather_cols", wib_s), ("gather_rows", wob_s)])
    w_out_a, w_kv, w_out_b = woa_g.reshape(D, D), wkv_g.reshape(D, 512), wob_g.reshape(D, D)
    h1 = _mm(z_a, w_out_a, "nn", 1024, 512, D, add=x, name="mm_out_a")
    u_kv, u_b = _rms2_fwd(h1, g_kv, g_b)
    kv = _mm(u_kv, w_kv, "nn", 1024, 512, D, name="mm_kv")
    pb = _mm(u_b, w_in_b, "nn", 1024, 512, D, name="mm_in_b")
    qb = _prep_b(pb, gq_b, cos2, sin2)
    ksh, vsh = _prep_kv(kv, gk_b, cos2, sin2)
    sinks1 = sinks.reshape(N_HEADS)
    o_b, z_b, lse_b = _swa_fwd(qb, ksh, vsh, pb, sinks1)
    y = _mm(z_b, w_out_b, "nn", 1024, 512, D, add=h1, name="mm_out_b")
    dy, lsum = _loss_dy(y, tgt)
    dw_out_b = _mm(z_b, dy, "tn", 512, 512, S, out_dtype=BF16, name="mm_dw_out_b")
    dz_b = _mm(dy, w_out_b, "nt", 1024, 512, D, name="mm_dz_b")
    dq_b, dgate_b, dka, dkb, dva, dvb, dsinks = _swa_bwd(qb, ksh, vsh, dz_b, o_b, lse_b, pb, sinks1)
    dpb, dgq_b = _prep_b_bwd(dq_b, dgate_b, pb, gq_b, cos2, sin2)
    dkv, dgk_b = _prep_kv_bwd(dka, dkb, dva, dvb, kv, gk_b, cos2, sin2)
    dw_in_b = _mm(u_b, dpb, "tn", 512, 512, S, out_dtype=BF16, name="mm_dw_in_b")
    du_b = _mm(dpb, w_in_b, "nt", 1024, 512, 2048, name="mm_du_b")
    dw_kv = _mm(u_kv, dkv, "tn", 512, 512, S, out_dtype=BF16, name="mm_dw_kv")
    du_kv = _mm(dkv, w_kv, "nt", 1024, 512, 512, name="mm_du_kv")
    dh1, dg_b, dg_kv = _rms2_bwd(du_b, du_kv, h1, g_b, g_kv, dy)
    dw_out_a = _mm(z_a, dh1, "tn", 512, 512, S, out_dtype=BF16, name="mm_dw_out_a")
    dz_a = _mm(dh1, w_out_a, "nt", 1024, 512, D, name="mm_dz_a")
    do_a, dgate_a, delta_a = _fox_bwd_pre(dz_a, proj, o_a)
    dk_a, dv_a, dcs, dq_a, drow, r_wob, r_wib, r_wkv, r_woa = _fox_bwd(
        qn, kn, vb, do_a, lse_a, delta_a, crow, cbc,
        rider=[("a2a_rows", dw_out_b), ("a2a_cols", dw_in_b), ("a2a_rows", dw_kv), ("a2a_rows", dw_out_a)])
    drow_col = jnp.pad(drow.reshape(N_HEADS, S).T, ((0, 0), (0, LANES - N_HEADS)))
    dproj, dgq_a, dgk_a, db_f = _prep_a_bwd(dq_a, dk_a, dv_a, dgate_a, drow_col, dcs, proj, b_pad, gq_a, gk_a)
    dwa = _mm(u_a, dproj, "tn", 1024, 256, S, out_dtype=BF16, name="mm_dw_in_a")
    du_a, r_wa = _mm(dproj, wa, "nt", 1024, 512, NA // 2, name="mm_du_a",
                     rider=[("a2a_chips", _pair_reduce(_reshard_dwa(dwa)))])
    dx, dg_a = _rms_bwd(du_a, x, g_a, dh1)
    slab = _pack_small(dg_a, dg_kv, dg_b, db_f, dgq_a, dgk_a, dgk_b, dgq_b, dsinks, lsum)
    return dx, r_wa, r_woa, r_wkv, r_wib, r_wob, _gather_slab(slab)


def kernel(x, positions, norm_a_g, w_in_a, b_forget, qnorm_a_g, knorm_a_g, w_out_a, kv_norm_g, w_kv, knorm_b_g, norm_b_g, w_in_b, qnorm_b_g, sinks, w_out_b, loss_target, m_norm_a_g, m_w_in_a, m_b_forget, m_qnorm_a_g, m_knorm_a_g, m_w_out_a, m_kv_norm_g, m_w_kv, m_knorm_b_g, m_norm_b_g, m_w_in_b, m_qnorm_b_g, m_sinks, m_w_out_b, v_norm_a_g, v_w_in_a, v_b_forget, v_qnorm_a_g, v_knorm_a_g, v_w_out_a, v_kv_norm_g, v_w_kv, v_knorm_b_g, v_norm_b_g, v_w_in_b, v_qnorm_b_g, v_sinks, v_w_out_b):
    wa_g, ga_g, woa_s, wkv_s, wib_s, wob_s = _gather_first(w_in_a, w_out_a, w_kv, w_in_b, w_out_b, norm_a_g)
    dx, r_wa, r_woa, r_wkv, r_wib, r_wob, slab_g = _local_step(
        x[0], loss_target[0], positions, ga_g.reshape(1, D), _unshard_wa(wa_g), b_forget, qnorm_a_g, knorm_a_g,
        kv_norm_g.reshape(1, D), knorm_b_g.reshape(1, HD), norm_b_g, qnorm_b_g, sinks, woa_s, wkv_s, wib_s, wob_s)

    big = {}
    for name, recv, w, m, v in (
            ("w_in_a", r_wa, w_in_a, m_w_in_a, v_w_in_a), ("w_out_a", r_woa, w_out_a, m_w_out_a, v_w_out_a),
            ("w_kv", r_wkv, w_kv, m_w_kv, v_w_kv), ("w_in_b", r_wib, w_in_b, m_w_in_b, v_w_in_b),
            ("w_out_b", r_wob, w_out_b, m_w_out_b, v_w_out_b)):
        big[name] = _sum_adamw(recv, w, m, v, "adamw_" + name)

    r2 = lambda a: a.reshape(1, -1)
    small_w = dict(norm_a_g=norm_a_g, b_forget=b_forget, qnorm_a_g=qnorm_a_g, knorm_a_g=knorm_a_g,
                   kv_norm_g=kv_norm_g, knorm_b_g=knorm_b_g, norm_b_g=norm_b_g, qnorm_b_g=qnorm_b_g, sinks=sinks)
    small_m = dict(norm_a_g=m_norm_a_g, b_forget=m_b_forget, qnorm_a_g=m_qnorm_a_g, knorm_a_g=m_knorm_a_g,
                   kv_norm_g=m_kv_norm_g, knorm_b_g=m_knorm_b_g, norm_b_g=m_norm_b_g, qnorm_b_g=m_qnorm_b_g,
                   sinks=m_sinks)
    small_v = dict(norm_a_g=v_norm_a_g, b_forget=v_b_forget, qnorm_a_g=v_qnorm_a_g, knorm_a_g=v_knorm_a_g,
                   kv_norm_g=v_kv_norm_g, knorm_b_g=v_knorm_b_g, norm_b_g=v_norm_b_g, qnorm_b_g=v_qnorm_b_g,
                   sinks=v_sinks)
    res = _small_adamw(slab_g, [r2(small_w[n]) for n in SMALL], [r2(small_m[n]) for n in SMALL],
                       [r2(small_v[n]) for n in SMALL])
    k = len(SMALL)
    small = {n: [res[q * k + p].reshape(small_w[n].shape) for q in range(4)] for p, n in enumerate(SMALL)}
    loss = res[4 * k][0, 0]

    order = ["norm_a_g", "w_in_a", "b_forget", "qnorm_a_g", "knorm_a_g", "w_out_a", "kv_norm_g", "w_kv",
             "knorm_b_g", "norm_b_g", "w_in_b", "qnorm_b_g", "sinks", "w_out_b"]

    def leaf(n, q):
        return big[n][q] if n in big else small[n][q]

    outs = [loss, dx[None]]
    for q in range(4):
        outs.extend(leaf(n, q) for n in order)
    return tuple(outs)
```

```python
import jax
import jax.numpy as jnp
from jax import lax
from jax.experimental import pallas as pl
from jax.experimental.pallas import tpu as pltpu

F32, BF16 = jnp.float32, jnp.bfloat16

S = 2048
D = 1024
HD = 64
N_HEADS = 16
N_DEV = 8
NA = 4352
GOFF = 3072
FOFF = 4096
RAW_F = 3072
RAW_G = RAW_F + N_HEADS
NA_RAW = 4112
EPS = 1e-6
QSCALE = 0.125
ROPE_THETA = 500000.0
ROT = 16
WIN = 128
TQ = 256
TK = 256
KS = TQ // 2
HPS = 8
HW = HPS * HD
TM = 256
RB = 512
CB = 256
LANES = 128

ADAM_LR, ADAM_B1, ADAM_B2, ADAM_EPS, ADAM_WD, ADAM_STEP = 0.001, 0.9, 0.999, 1e-08, 0.01, 10

VMEM_LIMIT = 56 * 1024 * 1024


def _params():
    return pltpu.CompilerParams(vmem_limit_bytes=VMEM_LIMIT)


def _sds(shape, dtype):
    return jax.ShapeDtypeStruct(shape, dtype)


def _dot_nt(a, b):
    return lax.dot_general(a, b, (((1,), (1,)), ((), ())), preferred_element_type=F32)


def _dot_tn(a, b):
    return lax.dot_general(a, b, (((0,), (0,)), ((), ())), preferred_element_type=F32)


def _dot_nn(a, b):
    return lax.dot_general(a, b, (((1,), (0,)), ((), ())), preferred_element_type=F32)


def _sigmoid(g):
    return 1.0 / (1.0 + jnp.exp(-g))


def _lane_iota(shape):
    return lax.broadcasted_iota(jnp.int32, shape, len(shape) - 1)


def _flips(kind):
    return (2, 4, 6) if kind == "a2a_chips" else tuple(range(1, N_DEV))


def _send_view(kind, ref, dev):
    if kind in ("gather_rows", "gather_cols"):
        return ref
    if kind == "a2a_slots":
        return ref.at[dev]
    if kind == "a2a_chips":
        return ref.at[dev >> 1]
    if kind == "a2a_rows":
        rows = ref.shape[0] // N_DEV
        return ref.at[pl.ds(pl.multiple_of(dev * rows, rows), rows)]
    cols = ref.shape[1] // N_DEV
    return ref.at[:, pl.ds(pl.multiple_of(dev * cols, cols), cols)]


def _land_view(kind, ref, dev):
    if kind == "gather_cols":
        cols = ref.shape[1] // N_DEV
        return ref.at[:, pl.ds(pl.multiple_of(dev * cols, cols), cols)]
    if kind == "a2a_chips":
        return ref.at[dev >> 1]
    return ref.at[dev]


def _landing_sds(kind, arr):
    if kind == "gather_rows":
        return _sds((N_DEV,) + arr.shape, arr.dtype)
    if kind == "gather_cols":
        return _sds((arr.shape[0], N_DEV * arr.shape[1]), arr.dtype)
    if kind == "a2a_rows":
        return _sds((N_DEV, arr.shape[0] // N_DEV, arr.shape[1]), arr.dtype)
    if kind == "a2a_cols":
        return _sds((N_DEV, arr.shape[0], arr.shape[1] // N_DEV), arr.dtype)
    return _sds(arr.shape, arr.dtype)


def _exchange_sems(n_parts):
    n = n_parts * (N_DEV - 1)
    return [pltpu.SemaphoreType.DMA((n,)), pltpu.SemaphoreType.DMA((n,)), pltpu.SemaphoreType.DMA((n_parts,))]


def _exchange_ops(kinds, srcs, dsts, sems, start, wait):
    send_sems, recv_sems, local_sems = sems
    x, y, c = lax.axis_index("x"), lax.axis_index("y"), lax.axis_index("c")
    me = 4 * x + 2 * y + c

    def local(a):
        return pltpu.make_async_copy(_send_view(kinds[a], srcs[a], me), _land_view(kinds[a], dsts[a], me),
                                     local_sems.at[a])

    def remote(a, k, landing_dev):
        peer = (x ^ ((k >> 2) & 1), y ^ ((k >> 1) & 1), c ^ (k & 1))
        sem = a * (N_DEV - 1) + k - 1
        return pltpu.make_async_remote_copy(
            src_ref=_send_view(kinds[a], srcs[a], me ^ k), dst_ref=_land_view(kinds[a], dsts[a], landing_dev),
            send_sem=send_sems.at[sem], recv_sem=recv_sems.at[sem], device_id=peer,
            device_id_type=pl.DeviceIdType.MESH)

    pairs = [(a, k) for k in range(1, N_DEV) for a in range(len(kinds)) if k in _flips(kinds[a])]
    if start:
        for a in range(len(kinds)):
            local(a).start()
        for a, k in pairs:
            remote(a, k, me).start()
    if wait:
        for a, k in pairs:
            remote(a, k, me ^ k).wait_recv()
            remote(a, k, me).wait_send()
        for a in range(len(kinds)):
            local(a).wait()


def _gather_two_level(srcs, dsts, sems):
    send_sems, recv_sems, local_sems = sems
    x, y, c = lax.axis_index("x"), lax.axis_index("y"), lax.axis_index("c")
    me, sibling = (x, y, c), (x, y, 1 - c)
    chips = [(1 - x, y), (x, 1 - y), (1 - x, 1 - y)]

    def slot(ref, dev):
        return ref.at[4 * dev[0] + 2 * dev[1] + dev[2]]

    def copy(a, k, block, to, src=None):
        return pltpu.make_async_remote_copy(
            src_ref=slot(dsts[a], block) if src is None else src, dst_ref=slot(dsts[a], block),
            send_sem=send_sems.at[a * (N_DEV - 1) + k], recv_sem=recv_sems.at[a * (N_DEV - 1) + k],
            device_id=to, device_id_type=pl.DeviceIdType.MESH)

    parts = range(len(srcs))
    mine = [pltpu.make_async_copy(srcs[a], slot(dsts[a], me), local_sems.at[a]) for a in parts]
    first = [copy(a, 0, me, sibling, src=srcs[a]) for a in parts]
    first += [copy(a, 1 + j, me, (*chip, c), src=srcs[a]) for j, chip in enumerate(chips) for a in parts]
    for cp in mine + first:
        cp.start()
    passed = []
    for j, chip in enumerate(chips):
        for a in parts:
            copy(a, 1 + j, (*chip, c), me).wait_recv()
            fwd = copy(a, 4 + j, (*chip, c), sibling)
            fwd.start()
            passed.append(fwd)
    for a in parts:
        copy(a, 0, sibling, me).wait_recv()
        for j, chip in enumerate(chips):
            copy(a, 4 + j, (*chip, 1 - c), me).wait_recv()
    for cp in first + passed:
        cp.wait_send()
    for cp in mine:
        cp.wait()


def _call(body, *, name, args, in_specs, out_specs, out_shape, grid=(), scratch_shapes=(), aliases=None, rider=()):
    n_in, n_out, n_scr, n_r = len(in_specs), len(out_specs), len(scratch_shapes), len(rider)
    kinds = [kind for kind, _ in rider]

    def kernel_body(*refs):
        c_in, r_in = refs[:n_in], refs[n_in:n_in + n_r]
        c_out = refs[n_in + n_r:n_in + n_r + n_out]
        r_out = refs[n_in + n_r + n_out:n_in + 2 * n_r + n_out]
        rest = refs[n_in + 2 * n_r + n_out:]
        c_scr, sems = rest[:n_scr], rest[n_scr:]
        if n_r:
            assert grid, "a rider needs a gridded call"
            ids = [pl.program_id(ax) for ax in range(len(grid))]
            first, last = ids[0] == 0, ids[0] == grid[0] - 1
            for pid, size in zip(ids[1:], grid[1:]):
                first = first & (pid == 0)
                last = last & (pid == size - 1)
            pl.when(first)(lambda: _exchange_ops(kinds, r_in, r_out, sems, True, False))
        body(*c_in, *c_out, *c_scr)
        if n_r:
            pl.when(last)(lambda: _exchange_ops(kinds, r_in, r_out, sems, False, True))

    anyspec = pl.BlockSpec(memory_space=pl.ANY)
    params = pltpu.CompilerParams(vmem_limit_bytes=VMEM_LIMIT, has_side_effects=bool(n_r))
    outs = pl.pallas_call(
        kernel_body, name=name, grid=grid, in_specs=list(in_specs) + [anyspec] * n_r,
        out_specs=list(out_specs) + [anyspec] * n_r,
        out_shape=list(out_shape) + [_landing_sds(kind, arr) for kind, arr in rider],
        scratch_shapes=list(scratch_shapes) + (_exchange_sems(n_r) if n_r else []),
        input_output_aliases=aliases or {}, compiler_params=params)(*args, *[arr for _, arr in rider])
    return list(outs)


def _mm(a, b, mode, tm, tn, tk, out_dtype=F32, add=None, name="mm", rider=()):
    if mode == "nn":
        (m, k), n = a.shape, b.shape[1]
        a_spec = pl.BlockSpec((tm, tk), lambda i, j, kk: (i, kk))
        b_spec = pl.BlockSpec((tk, tn), lambda i, j, kk: (kk, j))
        dot = _dot_nn
    elif mode == "nt":
        (m, k), n = a.shape, b.shape[0]
        a_spec = pl.BlockSpec((tm, tk), lambda i, j, kk: (i, kk))
        b_spec = pl.BlockSpec((tn, tk), lambda i, j, kk: (j, kk))
        dot = _dot_nt
    else:
        (k, m), n = a.shape, b.shape[1]
        a_spec = pl.BlockSpec((tk, tm), lambda i, j, kk: (kk, i))
        b_spec = pl.BlockSpec((tk, tn), lambda i, j, kk: (kk, j))
        dot = _dot_tn
    assert m % tm == 0 and n % tn == 0 and k % tk == 0, (m, n, k, tm, tn, tk)
    nk = k // tk
    has_add = add is not None

    def body(*refs):
        if has_add:
            a_ref, b_ref, add_ref, o_ref, acc = refs
        else:
            a_ref, b_ref, o_ref, acc = refs
        p = dot(a_ref[...].astype(BF16), b_ref[...].astype(BF16))

        def finish(total):
            if has_add:
                total = add_ref[...] + total
            o_ref[...] = total.astype(out_dtype)

        if nk == 1:
            finish(p)
        else:
            kk = pl.program_id(2)

            @pl.when(kk == 0)
            def _():
                acc[...] = p

            @pl.when(kk > 0)
            def _():
                acc[...] += p

            @pl.when(kk == nk - 1)
            def _():
                finish(acc[...])

    in_specs = [a_spec, b_spec]
    args = [a, b]
    if has_add:
        in_specs.append(pl.BlockSpec((tm, tn), lambda i, j, kk: (i, j)))
        args.append(add)
    acc_shape = (tm, tn) if nk > 1 else (8, LANES)
    outs = _call(body, name=name, args=args, grid=(m // tm, n // tn, nk), in_specs=in_specs,
                 out_specs=[pl.BlockSpec((tm, tn), lambda i, j, kk: (i, j))], out_shape=[_sds((m, n), out_dtype)],
                 scratch_shapes=[pltpu.VMEM(acc_shape, F32)], rider=rider)
    return outs if rider else outs[0]


def _rms_rinv(x):
    return lax.rsqrt(jnp.mean(x * x, axis=-1, keepdims=True) + EPS)


def _rms_bwd_core(du, x, g):
    r = _rms_rinv(x)
    dug = du * g
    dx = r * (dug - x * ((r * r) * jnp.mean(dug * x, axis=-1, keepdims=True)))
    dg = jnp.sum(du * (x * r), axis=0, keepdims=True)
    return dx, dg


def _half_sum(v, lo_half):
    s0 = jnp.sum(jnp.where(lo_half, v, 0.0), axis=-1, keepdims=True)
    s1 = jnp.sum(jnp.where(lo_half, 0.0, v), axis=-1, keepdims=True)
    return jnp.where(lo_half, s0, s1)


def _head_rinv(x, lo_half):
    return lax.rsqrt(_half_sum(x * x, lo_half) * (1.0 / HD) + EPS)


def _head_norm_bwd(dn, x, g, lo_half):
    r = _head_rinv(x, lo_half)
    dng = dn * g
    dx = r * (dng - x * ((r * r) * (_half_sum(dng * x, lo_half) * (1.0 / HD))))
    dg = jnp.sum(dn * (x * r), axis=0, keepdims=True)
    return dx, dg


def _rope_swap(x, lane):
    l64 = lane & (HD - 1)
    return jnp.where(l64 < ROT // 2, pltpu.roll(x, LANES - ROT // 2, 1), pltpu.roll(x, ROT // 2, 1))


def _rope_fwd(x, cos, sin, lane):
    return x * cos + _rope_swap(x, lane) * sin


def _rope_bwd(dy, cos, sin, lane):
    return dy * cos + jnp.where((lane & (HD - 1)) < ROT, _rope_swap(dy * sin, lane), 0.0)


def _g2(g_ref):
    g = g_ref[...]
    return jnp.concatenate([g, g], axis=-1)


def _pairs(width):
    return [slice(LANES * c, LANES * (c + 1)) for c in range(width // LANES)]


def _rms_fwd(x, g, name):
    def body(x_ref, g_ref, u_ref):
        xv = x_ref[...]
        u_ref[...] = ((xv * _rms_rinv(xv)) * g_ref[...]).astype(BF16)

    return pl.pallas_call(
        body, name=name, grid=(S // TM,),
        in_specs=[pl.BlockSpec((TM, D), lambda i: (i, 0)), pl.BlockSpec((1, D), lambda i: (0, 0))],
        out_specs=pl.BlockSpec((TM, D), lambda i: (i, 0)), out_shape=_sds((S, D), BF16),
        compiler_params=_params())(x, g)


def _rms2_fwd(h, g1, g2):
    def body(x_ref, g1_ref, g2_ref, u1_ref, u2_ref):
        xv = x_ref[...]
        xn = xv * _rms_rinv(xv)
        u1_ref[...] = (xn * g1_ref[...]).astype(BF16)
        u2_ref[...] = (xn * g2_ref[...]).astype(BF16)

    row = pl.BlockSpec((TM, D), lambda i: (i, 0))
    vec = pl.BlockSpec((1, D), lambda i: (0, 0))
    return pl.pallas_call(
        body, name="rms2_fwd", grid=(S // TM,), in_specs=[row, vec, vec], out_specs=[row, row],
        out_shape=[_sds((S, D), BF16)] * 2, compiler_params=_params())(h, g1, g2)


def _prep_a(proj, gq, gk):
    def body(q_ref, k_ref, v_ref, gq_ref, gk_ref, qo_ref, ko_ref, vo_ref):
        lo_half = _lane_iota((TM, LANES)) < HD
        gq2, gk2 = _g2(gq_ref), _g2(gk_ref)
        for c in _pairs(D):
            q = q_ref[:, c]
            k = k_ref[:, c]
            qo_ref[:, c] = (((q * _head_rinv(q, lo_half)) * gq2) * QSCALE).astype(BF16)
            ko_ref[:, c] = ((k * _head_rinv(k, lo_half)) * gk2).astype(BF16)
        vo_ref[...] = v_ref[...].astype(BF16)

    blk = lambda off: pl.BlockSpec((TM, D), lambda i: (i, off))
    gspec = pl.BlockSpec((1, HD), lambda i: (0, 0))
    return pl.pallas_call(
        body, name="prep_a", grid=(S // TM,),
        in_specs=[blk(0), blk(1), blk(2), gspec, gspec], out_specs=[blk(0)] * 3,
        out_shape=[_sds((S, D), BF16)] * 3, compiler_params=_params())(proj, proj, proj, gq, gk)


def _pick_lane(block, lane, idx):
    return jnp.sum(jnp.where(lane == idx, block, 0.0), axis=-1, keepdims=True)


def _fgate_fwd(proj, b_pad):
    def body(f_ref, b_ref, c_ref, cbc_ref, carry):
        @pl.when(pl.program_id(0) == 0)
        def _():
            carry[...] = jnp.zeros_like(carry)

        z = f_ref[...] + b_ref[...]
        logf = jnp.minimum(z, 0.0) - jnp.log1p(jnp.exp(-jnp.abs(z)))
        r = lax.broadcasted_iota(jnp.int32, (TM, TM), 0)
        c = lax.broadcasted_iota(jnp.int32, (TM, TM), 1)
        tri = (r >= c).astype(F32)
        loc = jnp.dot(tri, logf, precision=lax.Precision.HIGHEST, preferred_element_type=F32) + carry[0:1, :]
        c_ref[...] = loc
        carry[0:1, :] = loc[TM - 1:TM, :]
        lane = _lane_iota((TM, LANES))
        for h in range(N_HEADS):
            cbc_ref[:, LANES * h:LANES * (h + 1)] = jnp.broadcast_to(_pick_lane(loc, lane, h), (TM, LANES))

    return pl.pallas_call(
        body, name="fgate_fwd", grid=(S // TM,),
        in_specs=[pl.BlockSpec((TM, LANES), lambda i: (i, FOFF // LANES)), pl.BlockSpec((1, LANES), lambda i: (0, 0))],
        out_specs=[pl.BlockSpec((TM, LANES), lambda i: (i, 0)), pl.BlockSpec((TM, N_HEADS * LANES), lambda i: (i, 0))],
        out_shape=[_sds((S, LANES), F32), _sds((S, N_HEADS * LANES), F32)],
        scratch_shapes=[pltpu.VMEM((8, LANES), F32)], compiler_params=_params())(proj, b_pad)


def _key_le_query(offset, keys=TK):
    r = lax.broadcasted_iota(jnp.int32, (keys, TQ), 0)
    c = lax.broadcasted_iota(jnp.int32, (keys, TQ), 1)
    return (r + offset) <= c


def _widen(tile):
    return jnp.concatenate([tile] * (TQ // LANES), axis=1)


def _fox_fwd(qn, kn, vb, proj, crow, cbc, rider=()):
    nq, per = S // TQ, TQ // TK

    def body(q_ref, k_ref, v_ref, g_ref, cq_ref, cbc_ref, o_ref, z_ref, lse_ref, st_s, pt_s):
        i = pl.program_id(1)
        qs = [q_ref[:, HD * hh:HD * (hh + 1)] for hh in range(HPS)]
        cqs = [cq_ref[hh, 0] for hh in range(HPS)]

        def scores(s, hh):
            off = pl.multiple_of(s * KS, KS)
            kj = k_ref[pl.ds(off, KS), HD * hh:HD * (hh + 1)]
            return (_dot_nt(kj, qs[hh]) + cqs[hh]) - _widen(cbc_ref[pl.ds(off, KS), LANES * hh:LANES * (hh + 1)])

        def values(s, hh, pt):
            off = pl.multiple_of(s * KS, KS)
            return _dot_tn(v_ref[pl.ds(off, KS), HD * hh:HD * (hh + 1)], pt)

        def step(s, slot, carries, mask=None, last=False):
            if not last:
                for hh in range(HPS):
                    st_s[1 - slot, hh] = scores(s + 1, hh)
            pvs = [values(jnp.maximum(s - 1, 0), hh, pt_s[1 - slot, hh]) for hh in range(HPS)]
            out = []
            for hh in range(HPS):
                m, l, acc = carries[hh]
                st = st_s[slot, hh]
                if mask is not None:
                    st = jnp.where(mask, st, -jnp.inf)
                m_new = jnp.maximum(m, jnp.max(st, axis=0, keepdims=True))
                pt = jnp.exp(st - m_new)
                alpha = jnp.exp(m - m_new)
                pt_s[slot, hh] = pt.astype(BF16)
                out.append((m_new, alpha * l + jnp.sum(pt, axis=0, keepdims=True), alpha * (acc + pvs[hh])))
            return tuple(out)

        for hh in range(HPS):
            st_s[0, hh] = scores(0, hh)
            pt_s[1, hh] = jnp.zeros((KS, TQ), BF16)
        one = (jnp.full((1, TQ), -jnp.inf, F32), jnp.zeros((1, TQ), F32), jnp.zeros((HD, TQ), F32))
        carries = lax.fori_loop(0, i, lambda t, cr: step(2 * t + 1, 1, step(2 * t, 0, cr)), (one,) * HPS)
        carries = step(2 * i, 0, carries, mask=_key_le_query(0, KS))
        carries = step(2 * i + 1, 1, carries, mask=_key_le_query(KS, KS), last=True)
        accs = []
        for hh in range(HPS):
            m, l, acc = carries[hh]
            acc = acc + values(2 * i + 1, hh, pt_s[1, hh])
            accs.append(acc / l)
            lse_ref[hh, 0] = m + jnp.log(l)
        o = jnp.concatenate(accs, axis=0).T
        o_ref[...] = o
        g = g_ref[...]
        z_ref[...] = (o * (g * _sigmoid(g))).astype(BF16)

    qblk = pl.BlockSpec((TQ, HW), lambda hp, i: (i, hp))
    full = pl.BlockSpec((S, HW), lambda hp, i: (0, hp))
    rows = pl.BlockSpec((HPS, 1, 1, TQ), lambda hp, i: (hp, i, 0, 0))
    return _call(
        body, name="fox_fwd", args=(qn, kn, vb, proj, crow, cbc), grid=(N_HEADS // HPS, nq),
        in_specs=[qblk, full, full,
                  pl.BlockSpec((TQ, HW), lambda hp, i: (i, GOFF // HW + hp)),
                  rows, pl.BlockSpec((S, HPS * LANES), lambda hp, i: (0, hp))],
        out_specs=[qblk, qblk, rows],
        out_shape=[_sds((S, D), F32), _sds((S, D), BF16), _sds((N_HEADS, nq, 1, TQ), F32)],
        scratch_shapes=[pltpu.VMEM((2, HPS, KS, TQ), F32), pltpu.VMEM((2, HPS, KS, TQ), BF16)], rider=rider)


def _fox_bwd_pre(dz, proj, o):
    nq = S // TQ

    def body(dz_ref, g_ref, o_ref, do_ref, dg_ref, delta_ref):
        g = g_ref[...]
        sg = _sigmoid(g)
        dzv = dz_ref[...]
        ov = o_ref[...]
        do = dzv * (g * sg)
        dg_ref[...] = (dzv * ov * (sg * (1.0 + g * (1.0 - sg)))).astype(BF16)
        do_ref[...] = do.astype(BF16)
        prod_t = (do * ov).T
        for h in range(N_HEADS):
            delta_ref[h, 0] = jnp.sum(prod_t[HD * h:HD * (h + 1), :], axis=0, keepdims=True)

    row = pl.BlockSpec((TQ, D), lambda i: (i, 0))
    return pl.pallas_call(
        body, name="fox_bwd_pre", grid=(nq,),
        in_specs=[row, pl.BlockSpec((TQ, D), lambda i: (i, GOFF // D)), row],
        out_specs=[row, row, pl.BlockSpec((N_HEADS, 1, 1, TQ), lambda i: (0, i, 0, 0))],
        out_shape=[_sds((S, D), BF16), _sds((S, D), BF16), _sds((N_HEADS, nq, 1, TQ), F32)],
        compiler_params=_params())(dz, proj, o)


def _fox_bwd(qn, kn, vb, dob, lse, delta, crow, cbc, rider=()):
    nq, nkb = S // TQ, S // TK

    def body(q_ref, k_ref, v_ref, do_ref, lse_ref, del_ref, cq_ref, cbc_ref,
             dk_ref, dv_ref, dcs_ref, dq_ref, dr_ref, st_s, dp_s, pt_s, ds_s, dq_acc, dr_acc):
        j = pl.program_id(1)

        @pl.when(j == 0)
        def _():
            dq_acc[...] = jnp.zeros_like(dq_acc)
            dr_acc[...] = jnp.zeros_like(dr_acc)

        kjs = [k_ref[:, HD * hh:HD * (hh + 1)] for hh in range(HPS)]
        vjs = [v_ref[:, HD * hh:HD * (hh + 1)] for hh in range(HPS)]

        def rows_of(ref, u, hh):
            off = pl.multiple_of(u * TQ, TQ)
            return ref[pl.ds(off, TQ), HD * hh:HD * (hh + 1)]

        def products(u, hh):
            st = (_dot_nt(kjs[hh], rows_of(q_ref, u, hh)) + cq_ref[hh, u]) - _widen(
                cbc_ref[:, LANES * hh:LANES * (hh + 1)])
            return st, _dot_nt(vjs[hh], rows_of(do_ref, u, hh))

        def step(u, slot, carries, masked=False):
            nxt = jnp.minimum(u + 1, nq - 1)
            for hh in range(HPS):
                st_s[1 - slot, hh], dp_s[1 - slot, hh] = products(nxt, hh)
            prev = jnp.maximum(u - 1, 0)
            dvs = [_dot_nn(pt_s[1 - slot, hh], rows_of(do_ref, prev, hh)) for hh in range(HPS)]
            dks = [_dot_nn(ds_s[1 - slot, hh], rows_of(q_ref, prev, hh)) for hh in range(HPS)]
            for hh in range(HPS):
                dq_acc[hh, prev] += _dot_tn(kjs[hh], ds_s[1 - slot, hh])
            out = []
            for hh in range(HPS):
                dk, dv, dcs = carries[hh]
                st = st_s[slot, hh]
                if masked:
                    st = jnp.where(_key_le_query((j - u) * TQ), st, -jnp.inf)
                pt = jnp.exp(st - lse_ref[hh, u])
                dst = pt * (dp_s[slot, hh] - del_ref[hh, u])
                pt_s[slot, hh] = pt.astype(BF16)
                ds_s[slot, hh] = dst.astype(BF16)
                dr_acc[hh, u] += jnp.sum(dst, axis=0, keepdims=True)
                out.append((dk + dks[hh], dv + dvs[hh], dcs + (dst[:, :LANES] + dst[:, LANES:])))
            return tuple(out)

        t0 = j // 2
        for hh in range(HPS):
            st_s[0, hh], dp_s[0, hh] = products(2 * t0, hh)
            pt_s[1, hh] = jnp.zeros((TK, TQ), BF16)
            ds_s[1, hh] = jnp.zeros((TK, TQ), BF16)
        one = (jnp.zeros((TK, HD), F32), jnp.zeros((TK, HD), F32), jnp.zeros((TK, LANES), F32))
        carries = step(2 * t0 + 1, 1, step(2 * t0, 0, (one,) * HPS, masked=True), masked=True)
        carries = lax.fori_loop(t0 + 1, nq // 2, lambda t, cr: step(2 * t + 1, 1, step(2 * t, 0, cr)), carries)
        dks, dvs = [], []
        for hh in range(HPS):
            dk, dv, dcs = carries[hh]
            dks.append(dk + _dot_nn(ds_s[1, hh], rows_of(q_ref, nq - 1, hh)))
            dvs.append(dv + _dot_nn(pt_s[1, hh], rows_of(do_ref, nq - 1, hh)))
            dq_acc[hh, nq - 1] += _dot_tn(kjs[hh], ds_s[1, hh])
            dcs_ref[:, LANES * hh:LANES * (hh + 1)] = jnp.broadcast_to(
                -jnp.sum(dcs, axis=-1, keepdims=True), (TK, LANES))
        dk_ref[...] = jnp.concatenate(dks, axis=-1)
        dv_ref[...] = jnp.concatenate(dvs, axis=-1)

        @pl.when(j == nkb - 1)
        def _():
            for i in range(nq):
                dq_ref[TQ * i:TQ * (i + 1), :] = jnp.concatenate([dq_acc[hh, i] for hh in range(HPS)], axis=0).T
            dr_ref[...] = dr_acc[...]

    kblk = pl.BlockSpec((TK, HW), lambda hp, j: (j, hp))
    full = pl.BlockSpec((S, HW), lambda hp, j: (0, hp))
    rows = pl.BlockSpec((HPS, nq, 1, TQ), lambda hp, j: (hp, 0, 0, 0))
    cblk = pl.BlockSpec((TK, HPS * LANES), lambda hp, j: (j, hp))
    return _call(
        body, name="fox_bwd", args=(qn, kn, vb, dob, lse, delta, crow, cbc), grid=(N_HEADS // HPS, nkb),
        in_specs=[full, kblk, kblk, full, rows, rows, rows, cblk],
        out_specs=[kblk, kblk, cblk, full, rows],
        out_shape=[_sds((S, D), F32), _sds((S, D), F32), _sds((S, N_HEADS * LANES), F32), _sds((S, D), F32),
                   _sds((N_HEADS, nq, 1, TQ), F32)],
        scratch_shapes=[pltpu.VMEM((2, HPS, TK, TQ), F32), pltpu.VMEM((2, HPS, TK, TQ), F32),
                        pltpu.VMEM((2, HPS, TK, TQ), BF16), pltpu.VMEM((2, HPS, TK, TQ), BF16),
                        pltpu.VMEM((HPS, nq, HD, TQ), F32), pltpu.VMEM((HPS, nq, 1, TQ), F32)], rider=rider)


def _prep_a_bwd(dq, dk, dv, dgate, drow, dcs, proj, b_pad, gq, gk):
    nt = S // TM

    def body(dq_ref, dk_ref, dv_ref, dgt_ref, dr_ref, dcs_ref, xq_ref, xk_ref, f_ref, b_ref, gq_ref, gk_ref,
             o_ref, dgq_ref, dgk_ref, db_ref, carry):
        @pl.when(pl.program_id(0) == 0)
        def _():
            carry[...] = jnp.zeros_like(carry)
            dgq_ref[...] = jnp.zeros_like(dgq_ref)
            dgk_ref[...] = jnp.zeros_like(dgk_ref)
            db_ref[...] = jnp.zeros_like(db_ref)

        lane = _lane_iota((TM, LANES))
        lo_half = lane < HD
        gq2, gk2 = _g2(gq_ref), _g2(gk_ref)
        dgq, dgk = jnp.zeros((1, LANES), F32), jnp.zeros((1, LANES), F32)
        for c in _pairs(D):
            dxq, dg = _head_norm_bwd(dq_ref[:, c] * QSCALE, xq_ref[:, c], gq2, lo_half)
            o_ref[:, c] = dxq.astype(BF16)
            dgq = dgq + dg
            dxk, dg = _head_norm_bwd(dk_ref[:, c], xk_ref[:, c], gk2, lo_half)
            o_ref[:, D + c.start:D + c.stop] = dxk.astype(BF16)
            dgk = dgk + dg
        dgq_ref[...] += dgq
        dgk_ref[...] += dgk
        o_ref[:, 2 * D:3 * D] = dv_ref[...].astype(BF16)
        o_ref[:, GOFF:GOFF + D] = dgt_ref[...]

        dc = dr_ref[...]
        for h in range(N_HEADS):
            dc = dc + jnp.where(lane == h, dcs_ref[:, LANES * h:LANES * h + 1], 0.0)
        r = lax.broadcasted_iota(jnp.int32, (TM, TM), 0)
        c = lax.broadcasted_iota(jnp.int32, (TM, TM), 1)
        tri = (c >= r).astype(F32)
        dlogf = jnp.dot(tri, dc, precision=lax.Precision.HIGHEST, preferred_element_type=F32) + carry[0:1, :]
        carry[0:1, :] = dlogf[0:1, :]
        df = dlogf * (1.0 / (1.0 + jnp.exp(f_ref[...] + b_ref[...])))
        db_ref[...] += jnp.sum(df, axis=0, keepdims=True)
        o_ref[:, FOFF:FOFF + LANES] = df.astype(BF16)
        o_ref[:, FOFF + LANES:NA] = jnp.zeros((TM, NA - FOFF - LANES), BF16)

    rev = lambda width, col: pl.BlockSpec((TM, width), lambda i: (nt - 1 - i, col))
    gspec = pl.BlockSpec((1, HD), lambda i: (0, 0))
    acc = pl.BlockSpec((1, LANES), lambda i: (0, 0))
    return pl.pallas_call(
        body, name="prep_a_bwd", grid=(nt,),
        in_specs=[rev(D, 0), rev(D, 0), rev(D, 0), rev(D, 0), rev(LANES, 0), rev(N_HEADS * LANES, 0),
                  rev(D, 0), rev(D, 1), rev(LANES, FOFF // LANES), acc, gspec, gspec],
        out_specs=[rev(NA, 0), acc, acc, acc],
        out_shape=[_sds((S, NA), BF16)] + [_sds((1, LANES), F32)] * 3,
        scratch_shapes=[pltpu.VMEM((8, LANES), F32)],
        compiler_params=_params())(dq, dk, dv, dgate, drow, dcs, proj, proj, proj, b_pad, gq, gk)


def _prep_b(pb, gq, cos2, sin2):
    def body(x_ref, g_ref, c_ref, s_ref, o_ref):
        lane = _lane_iota((TM, LANES))
        g2, cos, sin = _g2(g_ref), c_ref[...], s_ref[...]
        for c in _pairs(D):
            x = x_ref[:, c]
            xn = (x * _head_rinv(x, lane < HD)) * g2
            o_ref[:, c] = (_rope_fwd(xn, cos, sin, lane) * QSCALE).astype(BF16)

    blk = pl.BlockSpec((TM, D), lambda i: (i, 0))
    tab = pl.BlockSpec((TM, LANES), lambda i: (i, 0))
    return pl.pallas_call(
        body, name="prep_b", grid=(S // TM,),
        in_specs=[blk, pl.BlockSpec((1, HD), lambda i: (0, 0)), tab, tab], out_specs=blk,
        out_shape=_sds((S, D), BF16), compiler_params=_params())(pb, gq, cos2, sin2)


def _prep_kv(kv, gk, cos2, sin2):
    def body(k_ref, v_ref, g_ref, c_ref, s_ref, ko_ref, vo_ref):
        lane = _lane_iota((RB, LANES))
        g2, cos, sin = _g2(g_ref), c_ref[...], s_ref[...]
        for c in _pairs(CB):
            x = k_ref[:, c]
            xn = (x * _head_rinv(x, lane < HD)) * g2
            ko_ref[:, c] = _rope_fwd(xn, cos, sin, lane).astype(BF16)
        vo_ref[...] = v_ref[...].astype(BF16)

    blk = lambda off: pl.BlockSpec((RB, CB), lambda i: (i, off))
    tab = pl.BlockSpec((RB, LANES), lambda i: (i, 0))
    return pl.pallas_call(
        body, name="prep_kv", grid=(S // RB,),
        in_specs=[blk(0), blk(1), pl.BlockSpec((1, HD), lambda i: (0, 0)), tab, tab],
        out_specs=[blk(0), blk(0)], out_shape=[_sds((S, CB), BF16)] * 2,
        compiler_params=_params())(kv, kv, gk, cos2, sin2)


N_KV, GRP = 4, 4


def _swa_mask(n):
    r = lax.broadcasted_iota(jnp.int32, (2 * WIN, GRP * WIN), 0)
    q = lax.broadcasted_iota(jnp.int32, (2 * WIN, GRP * WIN), 1) & (WIN - 1)
    return (r > q) & (r <= q + WIN) & ((r >= WIN) | (n > 0))


def _stack4(ref_or_val, base):
    return jnp.concatenate([ref_or_val[:, base + HD * g: base + HD * (g + 1)] for g in range(GRP)], axis=0)


def _unstack4(xt):
    return jnp.concatenate([xt[:, WIN * g:WIN * (g + 1)] for g in range(GRP)], axis=0).T


def _band(prev_ref, cur_ref, kh):
    return jnp.concatenate([prev_ref[:, HD * kh:HD * (kh + 1)], cur_ref[:, HD * kh:HD * (kh + 1)]], axis=0)


def _sink_row(s_ref, first):
    lane = _lane_iota((1, GRP * WIN))
    row = jnp.full((1, GRP * WIN), s_ref[first + GRP - 1], F32)
    for g in range(GRP - 2, -1, -1):
        row = jnp.where(lane < WIN * (g + 1), s_ref[first + g], row)
    return row


def _swa_fwd(qb, ksh, vsh, pb, sinks):
    nb = S // WIN

    def body(q_ref, kp_ref, kc_ref, vp_ref, vc_ref, g_ref, s_ref, o_ref, z_ref, lse_ref):
        n = pl.program_id(0)
        valid = _swa_mask(n)
        outs = []
        for kh in range(N_KV):
            kb, vb = _band(kp_ref, kc_ref, kh), _band(vp_ref, vc_ref, kh)
            st = jnp.where(valid, _dot_nt(kb, _stack4(q_ref, GRP * HD * kh)), -jnp.inf)
            sink = _sink_row(s_ref, GRP * kh)
            m = jnp.maximum(jnp.max(st, axis=0, keepdims=True), sink)
            pt = jnp.exp(st - m)
            l = jnp.sum(pt, axis=0, keepdims=True) + jnp.exp(sink - m)
            outs.append(_unstack4(_dot_tn(vb, pt.astype(BF16)) / l))
            lse = m + jnp.log(l)
            for g in range(GRP):
                lse_ref[GRP * kh + g, 0] = lse[:, WIN * g:WIN * (g + 1)]
        o = jnp.concatenate(outs, axis=-1)
        o_ref[...] = o
        g = g_ref[...]
        z_ref[...] = (o * (g * _sigmoid(g))).astype(BF16)

    row = pl.BlockSpec((WIN, D), lambda n: (n, 0))
    prev = pl.BlockSpec((WIN, N_KV * HD), lambda n: (jnp.maximum(n - 1, 0), 0))
    cur = pl.BlockSpec((WIN, N_KV * HD), lambda n: (n, 0))
    return pl.pallas_call(
        body, name="swa_fwd", grid=(nb,),
        in_specs=[row, prev, cur, prev, cur, pl.BlockSpec((WIN, D), lambda n: (n, 1)),
                  pl.BlockSpec(memory_space=pltpu.SMEM)],
        out_specs=[row, row, pl.BlockSpec((N_HEADS, 1, 1, WIN), lambda n: (0, n, 0, 0))],
        out_shape=[_sds((S, D), F32), _sds((S, D), BF16), _sds((N_HEADS, nb, 1, WIN), F32)],
        compiler_params=_params())(qb, ksh, ksh, vsh, vsh, pb, sinks)


def _swa_bwd(qb, ksh, vsh, dz, o, lse, pb, sinks):
    nb = S // WIN

    def body(q_ref, kp_ref, kc_ref, vp_ref, vc_ref, dz_ref, o_ref, lse_ref, g_ref, s_ref,
             dq_ref, dg_ref, dka_ref, dkb_ref, dva_ref, dvb_ref, dsink_ref):
        n = pl.program_id(0)

        @pl.when(n == 0)
        def _():
            dsink_ref[...] = jnp.zeros_like(dsink_ref)

        valid = _swa_mask(n)
        g = g_ref[...]
        sg = _sigmoid(g)
        dzv = dz_ref[...]
        ov = o_ref[...]
        do = dzv * (g * sg)
        dg_ref[...] = (dzv * ov * (sg * (1.0 + g * (1.0 - sg)))).astype(BF16)
        prod_t = (do * ov).T
        lane1 = _lane_iota((1, LANES))
        dqs, dkas, dkbs, dvas, dvbs = [], [], [], [], []
        dsink = jnp.zeros((1, LANES), F32)
        for kh in range(N_KV):
            kb, vb = _band(kp_ref, kc_ref, kh), _band(vp_ref, vc_ref, kh)
            base = GRP * HD * kh
            qs = _stack4(q_ref, base)
            dos = _stack4(do, base).astype(BF16)
            delta = jnp.concatenate(
                [jnp.sum(prod_t[base + HD * gg:base + HD * (gg + 1), :], axis=0, keepdims=True)
                 for gg in range(GRP)], axis=1)
            lse = jnp.concatenate([lse_ref[GRP * kh + gg, 0] for gg in range(GRP)], axis=1)
            st = jnp.where(valid, _dot_nt(kb, qs), -jnp.inf)
            pt = jnp.exp(st - lse)
            dst = pt * (_dot_nt(vb, dos) - delta)
            dsb = dst.astype(BF16)
            dqs.append(_unstack4(_dot_tn(kb, dsb)))
            dkband = _dot_nn(dsb, qs)
            dvband = _dot_nn(pt.astype(BF16), dos)
            dkbs.append(dkband[0:WIN, :])
            dkas.append(dkband[WIN:2 * WIN, :])
            dvbs.append(dvband[0:WIN, :])
            dvas.append(dvband[WIN:2 * WIN, :])
            ps_delta = jnp.exp(_sink_row(s_ref, GRP * kh) - lse) * delta
            for gg in range(GRP):
                val = jnp.sum(ps_delta[:, WIN * gg:WIN * (gg + 1)], axis=1, keepdims=True)
                dsink = dsink - jnp.where(lane1 == GRP * kh + gg, val, 0.0)
        dq_ref[...] = jnp.concatenate(dqs, axis=-1)
        dka_ref[...] = jnp.concatenate(dkas, axis=-1)
        dkb_ref[...] = jnp.concatenate(dkbs, axis=-1)
        dva_ref[...] = jnp.concatenate(dvas, axis=-1)
        dvb_ref[...] = jnp.concatenate(dvbs, axis=-1)
        dsink_ref[...] += dsink

    row = pl.BlockSpec((WIN, D), lambda n: (n, 0))
    prev = pl.BlockSpec((WIN, N_KV * HD), lambda n: (jnp.maximum(n - 1, 0), 0))
    cur = pl.BlockSpec((WIN, N_KV * HD), lambda n: (n, 0))
    return pl.pallas_call(
        body, name="swa_bwd", grid=(nb,),
        in_specs=[row, prev, cur, prev, cur, row, row, pl.BlockSpec((N_HEADS, 1, 1, WIN), lambda n: (0, n, 0, 0)),
                  pl.BlockSpec((WIN, D), lambda n: (n, 1)), pl.BlockSpec(memory_space=pltpu.SMEM)],
        out_specs=[row, row, cur, cur, cur, cur, pl.BlockSpec((1, LANES), lambda n: (0, 0))],
        out_shape=[_sds((S, D), F32), _sds((S, D), BF16)] + [_sds((S, 256), F32)] * 4 + [_sds((1, LANES), F32)],
        compiler_params=_params())(qb, ksh, ksh, vsh, vsh, dz, o, lse, pb, sinks)


def _prep_b_bwd(dq, dgate, pb, gq, cos2, sin2):
    def body(dq_ref, dgt_ref, x_ref, g_ref, c_ref, s_ref, o_ref, dgq_ref):
        @pl.when(pl.program_id(0) == 0)
        def _():
            dgq_ref[...] = jnp.zeros_like(dgq_ref)

        lane = _lane_iota((TM, LANES))
        g2, cos, sin = _g2(g_ref), c_ref[...], s_ref[...]
        dg_tot = jnp.zeros((1, LANES), F32)
        for c in _pairs(D):
            dn = _rope_bwd(dq_ref[:, c] * QSCALE, cos, sin, lane)
            dx, dg = _head_norm_bwd(dn, x_ref[:, c], g2, lane < HD)
            o_ref[:, c] = dx.astype(BF16)
            dg_tot = dg_tot + dg
        dgq_ref[...] += dg_tot
        o_ref[:, D:2 * D] = dgt_ref[...]

    row = pl.BlockSpec((TM, D), lambda i: (i, 0))
    tab = pl.BlockSpec((TM, LANES), lambda i: (i, 0))
    return pl.pallas_call(
        body, name="prep_b_bwd", grid=(S // TM,),
        in_specs=[row, row, row, pl.BlockSpec((1, HD), lambda i: (0, 0)), tab, tab],
        out_specs=[pl.BlockSpec((TM, 2 * D), lambda i: (i, 0)), pl.BlockSpec((1, LANES), lambda i: (0, 0))],
        out_shape=[_sds((S, 2 * D), BF16), _sds((1, LANES), F32)],
        compiler_params=_params())(dq, dgate, pb, gq, cos2, sin2)


def _prep_kv_bwd(dka, dkb, dva, dvb, kv, gk, cos2, sin2):
    nt = S // RB
    per = RB // WIN

    def shifted(cur_ref, nxt_ref, has_next):
        return jnp.concatenate([cur_ref[WIN:RB, :], jnp.where(has_next, nxt_ref[...], 0.0)], axis=0)

    def body(dka_ref, dkb_ref, dkn_ref, dva_ref, dvb_ref, dvn_ref, x_ref, g_ref, c_ref, s_ref, o_ref, dgk_ref):
        i, j = pl.program_id(0), pl.program_id(1)

        @pl.when((i == 0) & (j == 0))
        def _():
            dgk_ref[...] = jnp.zeros_like(dgk_ref)

        has_next = i < nt - 1

        @pl.when(j == 0)
        def _():
            lane = _lane_iota((RB, LANES))
            g2, cos, sin = _g2(g_ref), c_ref[...], s_ref[...]
            dy_all = dka_ref[...] + shifted(dkb_ref, dkn_ref, has_next)
            dg_tot = jnp.zeros((1, LANES), F32)
            for c in _pairs(CB):
                dn = _rope_bwd(dy_all[:, c], cos, sin, lane)
                dx, dg = _head_norm_bwd(dn, x_ref[:, c], g2, lane < HD)
                o_ref[:, c] = dx.astype(BF16)
                dg_tot = dg_tot + dg
            dgk_ref[...] += dg_tot

        @pl.when(j == 1)
        def _():
            o_ref[...] = (dva_ref[...] + shifted(dvb_ref, dvn_ref, has_next)).astype(BF16)

    cur = pl.BlockSpec((RB, CB), lambda i, j: (i, 0))
    nxt = pl.BlockSpec((WIN, CB), lambda i, j: (jnp.minimum(per * (i + 1), S // WIN - 1), 0))
    tab = pl.BlockSpec((RB, LANES), lambda i, j: (i, 0))
    return pl.pallas_call(
        body, name="prep_kv_bwd", grid=(nt, 2),
        in_specs=[cur, cur, nxt, cur, cur, nxt, cur, pl.BlockSpec((1, HD), lambda i, j: (0, 0)), tab, tab],
        out_specs=[pl.BlockSpec((RB, CB), lambda i, j: (i, j)), pl.BlockSpec((1, LANES), lambda i, j: (0, 0))],
        out_shape=[_sds((S, 2 * CB), BF16), _sds((1, LANES), F32)],
        compiler_params=_params())(dka, dkb, dkb, dva, dvb, dvb, kv, gk, cos2, sin2)


def _loss_dy(y, tgt):
    def body(y_ref, t_ref, dy_ref, l_ref):
        @pl.when(pl.program_id(0) == 0)
        def _():
            l_ref[...] = jnp.zeros_like(l_ref)

        e = y_ref[...] - t_ref[...]
        dy_ref[...] = e * (1.0 / D)
        l_ref[...] += jnp.sum(jnp.sum(e * e, axis=-1, keepdims=True), axis=0, keepdims=True)

    row = pl.BlockSpec((TM, D), lambda i: (i, 0))
    return pl.pallas_call(
        body, name="loss_dy", grid=(S // TM,), in_specs=[row, row],
        out_specs=[row, pl.BlockSpec((1, LANES), lambda i: (0, 0))],
        out_shape=[_sds((S, D), F32), _sds((1, LANES), F32)], compiler_params=_params())(y, tgt)


def _du_a_rms_bwd(dproj, wa, x, g, dres, rider=()):
    tm, tk = 512, NA // 2
    nk = NA // tk

    def body(a_ref, b_ref, x_ref, g_ref, dr_ref, dx_ref, dg_ref, acc):
        i, kk = pl.program_id(0), pl.program_id(1)

        @pl.when((i == 0) & (kk == 0))
        def _():
            dg_ref[...] = jnp.zeros_like(dg_ref)

        p = _dot_nt(a_ref[...], b_ref[...])

        @pl.when(kk == 0)
        def _():
            acc[...] = p

        @pl.when(kk == nk - 1)
        def _():
            dx, dg = _rms_bwd_core(acc[...] + p, x_ref[...], g_ref[...])
            dx_ref[...] = dr_ref[...] + dx
            dg_ref[...] += dg

    assert nk == 2
    row = pl.BlockSpec((tm, D), lambda i, kk: (i, 0))
    vec = pl.BlockSpec((1, D), lambda i, kk: (0, 0))
    return _call(
        body, name="du_a_rms_bwd", args=(dproj, wa, x, g, dres), grid=(S // tm, nk),
        in_specs=[pl.BlockSpec((tm, tk), lambda i, kk: (i, kk)), pl.BlockSpec((D, tk), lambda i, kk: (0, kk)),
                  row, vec, row],
        out_specs=[row, vec], out_shape=[_sds((S, D), F32), _sds((1, D), F32)],
        scratch_shapes=[pltpu.VMEM((tm, D), F32)], rider=rider)


def _du_b_rms_bwd(dpb, w_in_b, dkv, w_kv, h1, g_b, g_kv, dy):
    tm = 512

    def body(ab_ref, wb_ref, akv_ref, wkv_ref, x_ref, gb_ref, gkv_ref, dy_ref, dh_ref, dgb_ref, dgkv_ref):
        @pl.when(pl.program_id(0) == 0)
        def _():
            dgb_ref[...] = jnp.zeros_like(dgb_ref)
            dgkv_ref[...] = jnp.zeros_like(dgkv_ref)

        x = x_ref[...]
        dx1, dg1 = _rms_bwd_core(_dot_nt(ab_ref[...], wb_ref[...]), x, gb_ref[...])
        dx2, dg2 = _rms_bwd_core(_dot_nt(akv_ref[...], wkv_ref[...]), x, gkv_ref[...])
        dh_ref[...] = dy_ref[...] + dx1 + dx2
        dgb_ref[...] += dg1
        dgkv_ref[...] += dg2

    row = lambda width: pl.BlockSpec((tm, width), lambda i: (i, 0))
    whole = lambda arr: pl.BlockSpec(arr.shape, lambda i: (0, 0))
    vec = pl.BlockSpec((1, D), lambda i: (0, 0))
    return pl.pallas_call(
        body, name="du_b_rms_bwd", grid=(S // tm,),
        in_specs=[row(dpb.shape[1]), whole(w_in_b), row(dkv.shape[1]), whole(w_kv), row(D), vec, vec, row(D)],
        out_specs=[row(D), vec, vec], out_shape=[_sds((S, D), F32), _sds((1, D), F32), _sds((1, D), F32)],
        compiler_params=_params())(dpb, w_in_b, dkv, w_kv, h1, g_b, g_kv, dy)


def _gather_first(w_in_a, w_out_a, w_kv, w_in_b, w_out_b, norm_a_g):
    def body(wia_ref, woa_ref, wkv_ref, wib_ref, wob_ref, ga_ref,
             wa_g, ga_g, woa_s, wkv_s, wib_s, wob_s, wa_s, *sems):
        wa_s[...] = wia_ref[0].astype(BF16)
        woa_s[...] = woa_ref[0].astype(BF16)
        wkv_s[...] = wkv_ref[...].astype(BF16)
        wib_s[...] = wib_ref[0].astype(BF16)
        wob_s[...] = wob_ref[0].astype(BF16)
        _gather_two_level([wa_s, ga_ref], [wa_g, ga_g], sems)

    vmem = pl.BlockSpec(memory_space=pltpu.VMEM)
    anyspec = pl.BlockSpec(memory_space=pl.ANY)
    shard = lambda w: _sds(w.shape[-2:], BF16)
    return pl.pallas_call(
        body, name="gather_first", in_specs=[vmem] * 6, out_specs=[anyspec, anyspec, vmem, vmem, vmem, vmem],
        out_shape=[_sds((N_DEV,) + w_in_a.shape[-2:], BF16), _sds((N_DEV,) + norm_a_g.shape, F32),
                   shard(w_out_a), shard(w_kv), shard(w_in_b), shard(w_out_b)],
        scratch_shapes=[pltpu.VMEM(w_in_a.shape[-2:], BF16)] + _exchange_sems(2),
        compiler_params=pltpu.CompilerParams(vmem_limit_bytes=VMEM_LIMIT, has_side_effects=True))(
            w_in_a, w_out_a, w_kv, w_in_b, w_out_b, norm_a_g)


def _pair_reduce(slots):
    n_chip = N_DEV // 2
    _, rows, cols = slots.shape

    def body(s_ref, o_ref, own_v, sib_v, send_sems, recv_sems, local_sems):
        x, y, c = lax.axis_index("x"), lax.axis_index("y"), lax.axis_index("c")
        copies = []
        for j in range(n_chip):
            own = pltpu.make_async_copy(s_ref.at[2 * j + c], own_v.at[j], local_sems.at[j])
            give = pltpu.make_async_remote_copy(
                src_ref=s_ref.at[2 * j + 1 - c], dst_ref=sib_v.at[j], send_sem=send_sems.at[j],
                recv_sem=recv_sems.at[j], device_id=(x, y, 1 - c), device_id_type=pl.DeviceIdType.MESH)
            own.start()
            give.start()
            copies.append((own, give))
        for j, (own, give) in enumerate(copies):
            own.wait()
            give.wait()
            o_ref[j] = (own_v[j].astype(F32) + sib_v[j].astype(F32)).astype(BF16)

    half = _sds((n_chip, rows, cols), slots.dtype)
    return pl.pallas_call(
        body, name="pair_reduce", in_specs=[pl.BlockSpec(memory_space=pl.ANY)],
        out_specs=pl.BlockSpec(memory_space=pltpu.VMEM), out_shape=half,
        scratch_shapes=[pltpu.VMEM(half.shape, half.dtype), pltpu.VMEM(half.shape, half.dtype),
                        pltpu.SemaphoreType.DMA((n_chip,)), pltpu.SemaphoreType.DMA((n_chip,)),
                        pltpu.SemaphoreType.DMA((n_chip,))],
        compiler_params=pltpu.CompilerParams(vmem_limit_bytes=VMEM_LIMIT, has_side_effects=True))(slots)


def _padded_col(c):
    if c < RAW_F:
        return c
    return FOFF + (c - RAW_F) if c < RAW_G else GOFF + (c - RAW_G)


def _shard_pieces():
    width = NA_RAW // N_DEV
    pieces = []
    for d in range(N_DEV):
        cuts = [width * d] + [c for c in (RAW_F, RAW_G) if width * d < c < width * (d + 1)] + [width * (d + 1)]
        for lo, hi in zip(cuts[:-1], cuts[1:]):
            pieces.append((d, lo - width * d, _padded_col(lo), hi - lo))
    return pieces


def _unshard_wa(wa_g):
    def body(w_ref, o_ref):
        o_ref[:, FOFF + N_HEADS:NA] = jnp.zeros((TM, NA - FOFF - N_HEADS), BF16)
        for d, src, dst, width in _shard_pieces():
            o_ref[:, dst:dst + width] = w_ref[d, :, src:src + width]

    return pl.pallas_call(
        body, name="unshard_wa", grid=(D // TM,),
        in_specs=[pl.BlockSpec((N_DEV, TM, NA_RAW // N_DEV), lambda i: (0, i, 0))],
        out_specs=pl.BlockSpec((TM, NA), lambda i: (i, 0)), out_shape=_sds((D, NA), BF16),
        compiler_params=_params())(wa_g)


def _reshard_dwa(dwa):
    def body(g_ref, o_ref):
        for d, src, dst, width in _shard_pieces():
            o_ref[d, :, src:src + width] = g_ref[:, dst:dst + width]

    return pl.pallas_call(
        body, name="reshard_dwa", grid=(D // TM,), in_specs=[pl.BlockSpec((TM, NA), lambda i: (i, 0))],
        out_specs=pl.BlockSpec((N_DEV, TM, NA_RAW // N_DEV), lambda i: (0, i, 0)),
        out_shape=_sds((N_DEV, D, NA_RAW // N_DEV), dwa.dtype), compiler_params=_params())(dwa)


def _gather_slab(slab):
    def body(s_ref, o_ref, *sems):
        _exchange_ops(["gather_rows"], [s_ref], [o_ref], sems, True, True)

    anyspec = pl.BlockSpec(memory_space=pl.ANY)
    return pl.pallas_call(
        body, name="gather_slab", in_specs=[anyspec], out_specs=anyspec,
        out_shape=_sds((N_DEV,) + slab.shape, slab.dtype), scratch_shapes=_exchange_sems(1),
        compiler_params=pltpu.CompilerParams(has_side_effects=True))(slab)


def _adamw(w, g, m, v):
    m = ADAM_B1 * m + (1.0 - ADAM_B1) * g
    v = ADAM_B2 * v + (1.0 - ADAM_B2) * (g * g)
    m_hat = m / (1.0 - ADAM_B1 ** ADAM_STEP)
    v_hat = v / (1.0 - ADAM_B2 ** ADAM_STEP)
    delta = -ADAM_LR * (m_hat / (jnp.sqrt(v_hat) + ADAM_EPS) + ADAM_WD * w)
    return delta, m, v


def _sum_adamw(recv, w, m, v, name):
    lead = w.ndim - 2
    rows, cols = w.shape[-2:]
    tr = 128

    n_slots = recv.shape[0]

    def body(r_ref, w_ref, m_ref, v_ref, g_ref, d_ref, nm_ref, nv_ref):
        g = r_ref[0].astype(F32)
        for slot in range(1, n_slots):
            g = g + r_ref[slot].astype(F32)
        g_ref[...] = g
        d_ref[...], nm_ref[...], nv_ref[...] = _adamw(w_ref[...], g, m_ref[...], v_ref[...])

    blk = pl.BlockSpec((None,) * lead + (tr, cols), lambda i: (0,) * lead + (i, 0))
    return pl.pallas_call(
        body, name=name, grid=(rows // tr,),
        in_specs=[pl.BlockSpec((n_slots, tr, cols), lambda i: (0, i, 0)), blk, blk, blk],
        out_specs=[blk] * 4, out_shape=[_sds(w.shape, F32)] * 4,
        compiler_params=_params())(recv, w, m, v)


SLAB_ROWS = 16
SLOT = {"kv_norm_g": (8, 0, D), "norm_b_g": (9, 0, D), "b_forget": (10, 0, 16), "qnorm_a_g": (10, 128, HD),
        "knorm_a_g": (10, 256, HD), "knorm_b_g": (10, 384, HD), "qnorm_b_g": (10, 512, HD), "sinks": (10, 640, 16)}
SMALL = ["norm_a_g", "b_forget", "qnorm_a_g", "knorm_a_g", "kv_norm_g", "knorm_b_g", "norm_b_g", "qnorm_b_g", "sinks"]


LOSS_ROW = 11


def _pack_small(dg_a, dg_kv, dg_b, db_f, dgq_a, dgk_a, dgk_b, dgq_b, dsinks, lsum):
    def fold(ref):
        return ref[:, 0:HD] + ref[:, HD:2 * HD]

    def body(dga_ref, dgkv_ref, dgb_ref, dbf_ref, dgqa_ref, dgka_ref, dgkb_ref, dgqb_ref, dsk_ref, ls_ref, slab_ref):
        slab_ref[...] = jnp.zeros_like(slab_ref)
        for r in range(N_DEV):
            slab_ref[r:r + 1, 0:LANES] = dga_ref[:, LANES * r:LANES * (r + 1)]
        slab_ref[8:9, :] = dgkv_ref[...]
        slab_ref[9:10, :] = dgb_ref[...]
        slab_ref[10:11, 0:LANES] = dbf_ref[...]
        slab_ref[10:11, 128:128 + HD] = fold(dgqa_ref)
        slab_ref[10:11, 256:256 + HD] = fold(dgka_ref)
        slab_ref[10:11, 384:384 + HD] = fold(dgkb_ref)
        slab_ref[10:11, 512:512 + HD] = fold(dgqb_ref)
        slab_ref[10:11, 640:640 + LANES] = dsk_ref[...]
        slab_ref[LOSS_ROW:LOSS_ROW + 1, 0:LANES] = ls_ref[...]

    return pl.pallas_call(body, name="pack_small", out_shape=_sds((SLAB_ROWS, D), F32), compiler_params=_params())(
        dg_a, dg_kv, dg_b, db_f, dgq_a, dgk_a, dgk_b, dgq_b, dsinks, lsum)


def _small_adamw(recv, ws, ms, vs):
    k = len(SMALL)

    def body(*refs):
        r_ref = refs[0]
        w_refs, m_refs, v_refs = refs[1:1 + k], refs[1 + k:1 + 2 * k], refs[1 + 2 * k:1 + 3 * k]
        outs = refs[1 + 3 * k:1 + 7 * k]
        loss_ref, tot = refs[1 + 7 * k], refs[2 + 7 * k]
        g = r_ref[0]
        for dev in range(1, N_DEV):
            g = g + r_ref[dev]
        tot[...] = g
        loss_ref[...] = tot[LOSS_ROW:LOSS_ROW + 1, 0:LANES] * (0.5 / D)
        me = 4 * lax.axis_index("x") + 2 * lax.axis_index("y") + lax.axis_index("c")
        for p, name in enumerate(SMALL):
            if name == "norm_a_g":
                mine = lax.broadcasted_iota(jnp.int32, (N_DEV, LANES), 0) == me
                gp = jnp.sum(jnp.where(mine, tot[0:N_DEV, 0:LANES], 0.0), axis=0, keepdims=True)
            else:
                row, lo, width = SLOT[name]
                gp = tot[row:row + 1, lo:lo + width]
            d, nm, nv = _adamw(w_refs[p][...], gp, m_refs[p][...], v_refs[p][...])
            outs[p][...] = gp
            outs[k + p][...] = d
            outs[2 * k + p][...] = nm
            outs[3 * k + p][...] = nv

    shapes = [_sds(w.shape, F32) for w in ws]
    return pl.pallas_call(body, name="small_adamw", out_shape=shapes * 4 + [_sds((1, LANES), F32)],
                          scratch_shapes=[pltpu.VMEM((SLAB_ROWS, D), F32)],
                          compiler_params=_params())(recv, *ws, *ms, *vs)


def _rope_tables(positions):
    inv_freq = jnp.power(jnp.float32(ROPE_THETA), -jnp.arange(0, ROT, 2, dtype=F32) / ROT)
    ang = positions.astype(F32)[:, None] * inv_freq[None, :]
    cos, sin = jnp.cos(ang), jnp.sin(ang)
    c64 = jnp.concatenate([cos, cos, jnp.ones((S, HD - ROT), F32)], axis=-1)
    s64 = jnp.concatenate([-sin, sin, jnp.zeros((S, HD - ROT), F32)], axis=-1)
    return jnp.tile(c64, (1, 2)), jnp.tile(s64, (1, 2))


def _local_step(x, tgt, positions, g_a, wa, b_forget, gq_a, gk_a, g_kv, gk_b, g_b, gq_b, sinks,
                woa_s, wkv_s, wib_s, wob_s):
    nq = S // TQ
    cos2, sin2 = _rope_tables(positions)
    b_pad = jnp.pad(b_forget, ((0, 0), (0, LANES - N_HEADS)))

    u_a = _rms_fwd(x, g_a, "rms_a_fwd")
    proj = _mm(u_a, wa, "nn", S, 256, D, name="mm_in_a")
    qn, kn, vb = _prep_a(proj, gq_a, gk_a)
    ccol, cbc = _fgate_fwd(proj, b_pad)
    crow = ccol[:, :N_HEADS].T.reshape(N_HEADS, nq, 1, TQ)
    o_a, z_a, lse_a, woa_g, wkv_g, w_in_b, wob_g = _fox_fwd(
        qn, kn, vb, proj, crow, cbc,
        rider=[("gather_rows", woa_s), ("gather_rows", wkv_s), ("gather_cols", wib_s), ("gather_rows", wob_s)])
    w_out_a, w_kv, w_out_b = woa_g.reshape(D, D), wkv_g.reshape(D, 512), wob_g.reshape(D, D)
    h1 = _mm(z_a, w_out_a, "nn", 1024, 512, D, add=x, name="mm_out_a")
    u_kv, u_b = _rms2_fwd(h1, g_kv, g_b)
    kv = _mm(u_kv, w_kv, "nn", 1024, 512, D, name="mm_kv")
    pb = _mm(u_b, w_in_b, "nn", 1024, 512, D, name="mm_in_b")
    qb = _prep_b(pb, gq_b, cos2, sin2)
    ksh, vsh = _prep_kv(kv, gk_b, cos2, sin2)
    sinks1 = sinks.reshape(N_HEADS)
    o_b, z_b, lse_b = _swa_fwd(qb, ksh, vsh, pb, sinks1)
    y = _mm(z_b, w_out_b, "nn", 1024, 512, D, add=h1, name="mm_out_b")
    dy, lsum = _loss_dy(y, tgt)
    dw_out_b = _mm(z_b, dy, "tn", 512, 512, S, out_dtype=BF16, name="mm_dw_out_b")
    dz_b = _mm(dy, w_out_b, "nt", 1024, 512, D, name="mm_dz_b")
    dq_b, dgate_b, dka, dkb, dva, dvb, dsinks = _swa_bwd(qb, ksh, vsh, dz_b, o_b, lse_b, pb, sinks1)
    dpb, dgq_b = _prep_b_bwd(dq_b, dgate_b, pb, gq_b, cos2, sin2)
    dkv, dgk_b = _prep_kv_bwd(dka, dkb, dva, dvb, kv, gk_b, cos2, sin2)
    dw_in_b = _mm(u_b, dpb, "tn", 512, 512, S, out_dtype=BF16, name="mm_dw_in_b")
    dw_kv = _mm(u_kv, dkv, "tn", 512, 512, S, out_dtype=BF16, name="mm_dw_kv")
    dh1, dg_b, dg_kv = _du_b_rms_bwd(dpb, w_in_b, dkv, w_kv, h1, g_b, g_kv, dy)
    dw_out_a = _mm(z_a, dh1, "tn", 512, 512, S, out_dtype=BF16, name="mm_dw_out_a")
    dz_a = _mm(dh1, w_out_a, "nt", 1024, 512, D, name="mm_dz_a")
    do_a, dgate_a, delta_a = _fox_bwd_pre(dz_a, proj, o_a)
    dk_a, dv_a, dcs, dq_a, drow, r_wob, r_wib, r_wkv, r_woa = _fox_bwd(
        qn, kn, vb, do_a, lse_a, delta_a, crow, cbc,
        rider=[("a2a_rows", dw_out_b), ("a2a_cols", dw_in_b), ("a2a_rows", dw_kv), ("a2a_rows", dw_out_a)])
    drow_col = jnp.pad(drow.reshape(N_HEADS, S).T, ((0, 0), (0, LANES - N_HEADS)))
    dproj, dgq_a, dgk_a, db_f = _prep_a_bwd(dq_a, dk_a, dv_a, dgate_a, drow_col, dcs, proj, b_pad, gq_a, gk_a)
    dwa = _mm(u_a, dproj, "tn", 1024, 256, S, out_dtype=BF16, name="mm_dw_in_a")
    dx, dg_a, r_wa = _du_a_rms_bwd(dproj, wa, x, g_a, dh1, rider=[("a2a_chips", _pair_reduce(_reshard_dwa(dwa)))])
    slab = _pack_small(dg_a, dg_kv, dg_b, db_f, dgq_a, dgk_a, dgk_b, dgq_b, dsinks, lsum)
    return dx, r_wa, r_woa, r_wkv, r_wib, r_wob, _gather_slab(slab)


def kernel(x, positions, norm_a_g, w_in_a, b_forget, qnorm_a_g, knorm_a_g, w_out_a, kv_norm_g, w_kv, knorm_b_g, norm_b_g, w_in_b, qnorm_b_g, sinks, w_out_b, loss_target, m_norm_a_g, m_w_in_a, m_b_forget, m_qnorm_a_g, m_knorm_a_g, m_w_out_a, m_kv_norm_g, m_w_kv, m_knorm_b_g, m_norm_b_g, m_w_in_b, m_qnorm_b_g, m_sinks, m_w_out_b, v_norm_a_g, v_w_in_a, v_b_forget, v_qnorm_a_g, v_knorm_a_g, v_w_out_a, v_kv_norm_g, v_w_kv, v_knorm_b_g, v_norm_b_g, v_w_in_b, v_qnorm_b_g, v_sinks, v_w_out_b):
    wa_g, ga_g, woa_s, wkv_s, wib_s, wob_s = _gather_first(w_in_a, w_out_a, w_kv, w_in_b, w_out_b, norm_a_g)
    dx, r_wa, r_woa, r_wkv, r_wib, r_wob, slab_g = _local_step(
        x[0], loss_target[0], positions, ga_g.reshape(1, D), _unshard_wa(wa_g), b_forget, qnorm_a_g, knorm_a_g,
        kv_norm_g.reshape(1, D), knorm_b_g.reshape(1, HD), norm_b_g, qnorm_b_g, sinks, woa_s, wkv_s, wib_s, wob_s)

    big = {}
    for name, recv, w, m, v in (
            ("w_in_a", r_wa, w_in_a, m_w_in_a, v_w_in_a), ("w_out_a", r_woa, w_out_a, m_w_out_a, v_w_out_a),
            ("w_kv", r_wkv, w_kv, m_w_kv, v_w_kv), ("w_in_b", r_wib, w_in_b, m_w_in_b, v_w_in_b),
            ("w_out_b", r_wob, w_out_b, m_w_out_b, v_w_out_b)):
        big[name] = _sum_adamw(recv, w, m, v, "adamw_" + name)

    r2 = lambda a: a.reshape(1, -1)
    small_w = dict(norm_a_g=norm_a_g, b_forget=b_forget, qnorm_a_g=qnorm_a_g, knorm_a_g=knorm_a_g,
                   kv_norm_g=kv_norm_g, knorm_b_g=knorm_b_g, norm_b_g=norm_b_g, qnorm_b_g=qnorm_b_g, sinks=sinks)
    small_m = dict(norm_a_g=m_norm_a_g, b_forget=m_b_forget, qnorm_a_g=m_qnorm_a_g, knorm_a_g=m_knorm_a_g,
                   kv_norm_g=m_kv_norm_g, knorm_b_g=m_knorm_b_g, norm_b_g=m_norm_b_g, qnorm_b_g=m_qnorm_b_g,
                   sinks=m_sinks)
    small_v = dict(norm_a_g=v_norm_a_g, b_forget=v_b_forget, qnorm_a_g=v_qnorm_a_g, knorm_a_g=v_knorm_a_g,
                   kv_norm_g=v_kv_norm_g, knorm_b_g=v_knorm_b_g, norm_b_g=v_norm_b_g, qnorm_b_g=v_qnorm_b_g,
                   sinks=v_sinks)
    res = _small_adamw(slab_g, [r2(small_w[n]) for n in SMALL], [r2(small_m[n]) for n in SMALL],
                       [r2(small_v[n]) for n in SMALL])
    k = len(SMALL)
    small = {n: [res[q * k + p].reshape(small_w[n].shape) for q in range(4)] for p, n in enumerate(SMALL)}
    loss = res[4 * k][0, 0]

    order = ["norm_a_g", "w_in_a", "b_forget", "qnorm_a_g", "knorm_a_g", "w_out_a", "kv_norm_g", "w_kv",
             "knorm_b_g", "norm_b_g", "w_in_b", "qnorm_b_g", "sinks", "w_out_b"]

    def leaf(n, q):
        return big[n][q] if n in big else small[n][q]

    outs = [loss, dx[None]]
    for q in range(4):
        outs.extend(leaf(n, q) for n in order)
    return tuple(outs)
```

```python
import jax
import jax.numpy as jnp
from jax import lax
from jax.experimental import pallas as pl
from jax.experimental.pallas import tpu as pltpu

F32, BF16 = jnp.float32, jnp.bfloat16

S = 2048
D = 1024
HD = 64
N_HEADS = 16
N_DEV = 8
NA = 4352
GOFF = 3072
FOFF = 4096
RAW_F = 3072
RAW_G = RAW_F + N_HEADS
NA_RAW = 4112
EPS = 1e-6
QSCALE = 0.125
ROPE_THETA = 500000.0
ROT = 16
WIN = 128
TQ = 256
TK = 256
KS = TQ // 2
HPS = 8
HW = HPS * HD
TM = 256
RB = 512
CB = 256
LANES = 128

ADAM_LR, ADAM_B1, ADAM_B2, ADAM_EPS, ADAM_WD, ADAM_STEP = 0.001, 0.9, 0.999, 1e-08, 0.01, 10

VMEM_LIMIT = 56 * 1024 * 1024


def _params():
    return pltpu.CompilerParams(vmem_limit_bytes=VMEM_LIMIT)


def _sds(shape, dtype):
    return jax.ShapeDtypeStruct(shape, dtype)


def _dot_nt(a, b):
    return lax.dot_general(a, b, (((1,), (1,)), ((), ())), preferred_element_type=F32)


def _dot_tn(a, b):
    return lax.dot_general(a, b, (((0,), (0,)), ((), ())), preferred_element_type=F32)


def _dot_nn(a, b):
    return lax.dot_general(a, b, (((1,), (0,)), ((), ())), preferred_element_type=F32)


def _sigmoid(g):
    return 1.0 / (1.0 + jnp.exp(-g))


def _lane_iota(shape):
    return lax.broadcasted_iota(jnp.int32, shape, len(shape) - 1)


def _flips(kind):
    return (2, 4, 6) if kind == "a2a_chips" else tuple(range(1, N_DEV))


def _send_view(kind, ref, dev):
    if kind in ("gather_rows", "gather_cols"):
        return ref
    if kind == "a2a_slots":
        return ref.at[dev]
    if kind == "a2a_chips":
        return ref.at[dev >> 1]
    if kind == "a2a_rows":
        rows = ref.shape[0] // N_DEV
        return ref.at[pl.ds(pl.multiple_of(dev * rows, rows), rows)]
    cols = ref.shape[1] // N_DEV
    return ref.at[:, pl.ds(pl.multiple_of(dev * cols, cols), cols)]


def _land_view(kind, ref, dev):
    if kind == "gather_cols":
        cols = ref.shape[1] // N_DEV
        return ref.at[:, pl.ds(pl.multiple_of(dev * cols, cols), cols)]
    if kind == "a2a_chips":
        return ref.at[dev >> 1]
    return ref.at[dev]


def _landing_sds(kind, arr):
    if kind == "gather_rows":
        return _sds((N_DEV,) + arr.shape, arr.dtype)
    if kind == "gather_cols":
        return _sds((arr.shape[0], N_DEV * arr.shape[1]), arr.dtype)
    if kind == "a2a_rows":
        return _sds((N_DEV, arr.shape[0] // N_DEV, arr.shape[1]), arr.dtype)
    if kind == "a2a_cols":
        return _sds((N_DEV, arr.shape[0], arr.shape[1] // N_DEV), arr.dtype)
    return _sds(arr.shape, arr.dtype)


def _exchange_sems(n_parts):
    n = n_parts * (N_DEV - 1)
    return [pltpu.SemaphoreType.DMA((n,)), pltpu.SemaphoreType.DMA((n,)), pltpu.SemaphoreType.DMA((n_parts,))]


def _exchange_ops(kinds, srcs, dsts, sems, start, wait):
    send_sems, recv_sems, local_sems = sems
    x, y, c = lax.axis_index("x"), lax.axis_index("y"), lax.axis_index("c")
    me = 4 * x + 2 * y + c

    def local(a):
        return pltpu.make_async_copy(_send_view(kinds[a], srcs[a], me), _land_view(kinds[a], dsts[a], me),
                                     local_sems.at[a])

    def remote(a, k, landing_dev):
        peer = (x ^ ((k >> 2) & 1), y ^ ((k >> 1) & 1), c ^ (k & 1))
        sem = a * (N_DEV - 1) + k - 1
        return pltpu.make_async_remote_copy(
            src_ref=_send_view(kinds[a], srcs[a], me ^ k), dst_ref=_land_view(kinds[a], dsts[a], landing_dev),
            send_sem=send_sems.at[sem], recv_sem=recv_sems.at[sem], device_id=peer,
            device_id_type=pl.DeviceIdType.MESH)

    pairs = [(a, k) for k in range(1, N_DEV) for a in range(len(kinds)) if k in _flips(kinds[a])]
    if start:
        for a in range(len(kinds)):
            local(a).start()
        for a, k in pairs:
            remote(a, k, me).start()
    if wait:
        for a, k in pairs:
            remote(a, k, me ^ k).wait_recv()
            remote(a, k, me).wait_send()
        for a in range(len(kinds)):
            local(a).wait()


def _gather_two_level(srcs, dsts, sems):
    send_sems, recv_sems, local_sems = sems
    x, y, c = lax.axis_index("x"), lax.axis_index("y"), lax.axis_index("c")
    me, sibling = (x, y, c), (x, y, 1 - c)
    chips = [(1 - x, y), (x, 1 - y), (1 - x, 1 - y)]

    def slot(ref, dev):
        return ref.at[4 * dev[0] + 2 * dev[1] + dev[2]]

    def copy(a, k, block, to, src=None):
        return pltpu.make_async_remote_copy(
            src_ref=slot(dsts[a], block) if src is None else src, dst_ref=slot(dsts[a], block),
            send_sem=send_sems.at[a * (N_DEV - 1) + k], recv_sem=recv_sems.at[a * (N_DEV - 1) + k],
            device_id=to, device_id_type=pl.DeviceIdType.MESH)

    parts = range(len(srcs))
    mine = [pltpu.make_async_copy(srcs[a], slot(dsts[a], me), local_sems.at[a]) for a in parts]
    first = [copy(a, 0, me, sibling, src=srcs[a]) for a in parts]
    first += [copy(a, 1 + j, me, (*chip, c), src=srcs[a]) for j, chip in enumerate(chips) for a in parts]
    for cp in mine + first:
        cp.start()
    passed = []
    for j, chip in enumerate(chips):
        for a in parts:
            copy(a, 1 + j, (*chip, c), me).wait_recv()
            fwd = copy(a, 4 + j, (*chip, c), sibling)
            fwd.start()
            passed.append(fwd)
    for a in parts:
        copy(a, 0, sibling, me).wait_recv()
        for j, chip in enumerate(chips):
            copy(a, 4 + j, (*chip, 1 - c), me).wait_recv()
    for cp in first + passed:
        cp.wait_send()
    for cp in mine:
        cp.wait()


def _call(body, *, name, args, in_specs, out_specs, out_shape, grid=(), scratch_shapes=(), aliases=None, rider=()):
    n_in, n_out, n_scr, n_r = len(in_specs), len(out_specs), len(scratch_shapes), len(rider)
    kinds = [kind for kind, _ in rider]

    def kernel_body(*refs):
        c_in, r_in = refs[:n_in], refs[n_in:n_in + n_r]
        c_out = refs[n_in + n_r:n_in + n_r + n_out]
        r_out = refs[n_in + n_r + n_out:n_in + 2 * n_r + n_out]
        rest = refs[n_in + 2 * n_r + n_out:]
        c_scr, sems = rest[:n_scr], rest[n_scr:]
        if n_r:
            assert grid, "a rider needs a gridded call"
            ids = [pl.program_id(ax) for ax in range(len(grid))]
            first, last = ids[0] == 0, ids[0] == grid[0] - 1
            for pid, size in zip(ids[1:], grid[1:]):
                first = first & (pid == 0)
                last = last & (pid == size - 1)
            pl.when(first)(lambda: _exchange_ops(kinds, r_in, r_out, sems, True, False))
        body(*c_in, *c_out, *c_scr)
        if n_r:
            pl.when(last)(lambda: _exchange_ops(kinds, r_in, r_out, sems, False, True))

    anyspec = pl.BlockSpec(memory_space=pl.ANY)
    params = pltpu.CompilerParams(vmem_limit_bytes=VMEM_LIMIT, has_side_effects=bool(n_r))
    outs = pl.pallas_call(
        kernel_body, name=name, grid=grid, in_specs=list(in_specs) + [anyspec] * n_r,
        out_specs=list(out_specs) + [anyspec] * n_r,
        out_shape=list(out_shape) + [_landing_sds(kind, arr) for kind, arr in rider],
        scratch_shapes=list(scratch_shapes) + (_exchange_sems(n_r) if n_r else []),
        input_output_aliases=aliases or {}, compiler_params=params)(*args, *[arr for _, arr in rider])
    return list(outs)


def _mm(a, b, mode, tm, tn, tk, out_dtype=F32, add=None, name="mm", rider=()):
    if mode == "nn":
        (m, k), n = a.shape, b.shape[1]
        a_spec = pl.BlockSpec((tm, tk), lambda i, j, kk: (i, kk))
        b_spec = pl.BlockSpec((tk, tn), lambda i, j, kk: (kk, j))
        dot = _dot_nn
    elif mode == "nt":
        (m, k), n = a.shape, b.shape[0]
        a_spec = pl.BlockSpec((tm, tk), lambda i, j, kk: (i, kk))
        b_spec = pl.BlockSpec((tn, tk), lambda i, j, kk: (j, kk))
        dot = _dot_nt
    else:
        (k, m), n = a.shape, b.shape[1]
        a_spec = pl.BlockSpec((tk, tm), lambda i, j, kk: (kk, i))
        b_spec = pl.BlockSpec((tk, tn), lambda i, j, kk: (kk, j))
        dot = _dot_tn
    assert m % tm == 0 and n % tn == 0 and k % tk == 0, (m, n, k, tm, tn, tk)
    nk = k // tk
    has_add = add is not None

    def body(*refs):
        if has_add:
            a_ref, b_ref, add_ref, o_ref, acc = refs
        else:
            a_ref, b_ref, o_ref, acc = refs
        p = dot(a_ref[...].astype(BF16), b_ref[...].astype(BF16))

        def finish(total):
            if has_add:
                total = add_ref[...] + total
            o_ref[...] = total.astype(out_dtype)

        if nk == 1:
            finish(p)
        else:
            kk = pl.program_id(2)

            @pl.when(kk == 0)
            def _():
                acc[...] = p

            @pl.when(kk > 0)
            def _():
                acc[...] += p

            @pl.when(kk == nk - 1)
            def _():
                finish(acc[...])

    in_specs = [a_spec, b_spec]
    args = [a, b]
    if has_add:
        in_specs.append(pl.BlockSpec((tm, tn), lambda i, j, kk: (i, j)))
        args.append(add)
    acc_shape = (tm, tn) if nk > 1 else (8, LANES)
    outs = _call(body, name=name, args=args, grid=(m // tm, n // tn, nk), in_specs=in_specs,
                 out_specs=[pl.BlockSpec((tm, tn), lambda i, j, kk: (i, j))], out_shape=[_sds((m, n), out_dtype)],
                 scratch_shapes=[pltpu.VMEM(acc_shape, F32)], rider=rider)
    return outs if rider else outs[0]


def _rms_rinv(x):
    return lax.rsqrt(jnp.mean(x * x, axis=-1, keepdims=True) + EPS)


def _rms_bwd_core(du, x, g):
    r = _rms_rinv(x)
    dug = du * g
    dx = r * (dug - x * ((r * r) * jnp.mean(dug * x, axis=-1, keepdims=True)))
    dg = jnp.sum(du * (x * r), axis=0, keepdims=True)
    return dx, dg


def _half_sum(v, lo_half):
    s0 = jnp.sum(jnp.where(lo_half, v, 0.0), axis=-1, keepdims=True)
    s1 = jnp.sum(jnp.where(lo_half, 0.0, v), axis=-1, keepdims=True)
    return jnp.where(lo_half, s0, s1)


def _head_rinv(x, lo_half):
    return lax.rsqrt(_half_sum(x * x, lo_half) * (1.0 / HD) + EPS)


def _head_norm_bwd(dn, x, g, lo_half):
    r = _head_rinv(x, lo_half)
    dng = dn * g
    dx = r * (dng - x * ((r * r) * (_half_sum(dng * x, lo_half) * (1.0 / HD))))
    dg = jnp.sum(dn * (x * r), axis=0, keepdims=True)
    return dx, dg


def _rope_swap(x, lane):
    l64 = lane & (HD - 1)
    return jnp.where(l64 < ROT // 2, pltpu.roll(x, LANES - ROT // 2, 1), pltpu.roll(x, ROT // 2, 1))


def _rope_fwd(x, cos, sin, lane):
    return x * cos + _rope_swap(x, lane) * sin


def _rope_bwd(dy, cos, sin, lane):
    return dy * cos + jnp.where((lane & (HD - 1)) < ROT, _rope_swap(dy * sin, lane), 0.0)


def _g2(g_ref):
    g = g_ref[...]
    return jnp.concatenate([g, g], axis=-1)


def _pairs(width):
    return [slice(LANES * c, LANES * (c + 1)) for c in range(width // LANES)]


def _pick_lane(block, lane, idx):
    return jnp.sum(jnp.where(lane == idx, block, 0.0), axis=-1, keepdims=True)


def _head_a(x, g, wa, gq, gk, b_pad):
    def body(x_ref, g_ref, w_ref, gq_ref, gk_ref, b_ref, u_ref, p_ref, qo_ref, ko_ref, vo_ref, c_ref, cbc_ref, carry):
        @pl.when(pl.program_id(0) == 0)
        def _():
            carry[...] = jnp.zeros_like(carry)

        xv = x_ref[...]
        u = ((xv * _rms_rinv(xv)) * g_ref[...]).astype(BF16)
        u_ref[...] = u
        for lo in range(0, NA, D):
            hi = min(lo + D, NA)
            p_ref[:, lo:hi] = _dot_nn(u, w_ref[:, lo:hi])
        lane = _lane_iota((TM, LANES))
        lo_half = lane < HD
        gq2, gk2 = _g2(gq_ref), _g2(gk_ref)
        for c in _pairs(D):
            q = p_ref[:, c]
            k = p_ref[:, D + c.start:D + c.stop]
            qo_ref[:, c] = (((q * _head_rinv(q, lo_half)) * gq2) * QSCALE).astype(BF16)
            ko_ref[:, c] = ((k * _head_rinv(k, lo_half)) * gk2).astype(BF16)
        vo_ref[...] = p_ref[:, 2 * D:3 * D].astype(BF16)

        z = p_ref[:, FOFF:FOFF + LANES] + b_ref[...]
        logf = jnp.minimum(z, 0.0) - jnp.log1p(jnp.exp(-jnp.abs(z)))
        r = lax.broadcasted_iota(jnp.int32, (TM, TM), 0)
        cc = lax.broadcasted_iota(jnp.int32, (TM, TM), 1)
        tri = (r >= cc).astype(F32)
        loc = jnp.dot(tri, logf, precision=lax.Precision.HIGHEST, preferred_element_type=F32) + carry[0:1, :]
        c_ref[...] = loc
        carry[0:1, :] = loc[TM - 1:TM, :]
        for h in range(N_HEADS):
            cbc_ref[:, LANES * h:LANES * (h + 1)] = jnp.broadcast_to(_pick_lane(loc, lane, h), (TM, LANES))

    row = lambda width: pl.BlockSpec((TM, width), lambda i: (i, 0))
    whole = lambda arr: pl.BlockSpec(arr.shape, lambda i: (0,) * arr.ndim)
    return pl.pallas_call(
        body, name="head_a", grid=(S // TM,),
        in_specs=[row(D), whole(g), whole(wa), whole(gq), whole(gk), whole(b_pad)],
        out_specs=[row(D), row(NA), row(D), row(D), row(D), row(LANES), row(N_HEADS * LANES)],
        out_shape=[_sds((S, D), BF16), _sds((S, NA), F32)] + [_sds((S, D), BF16)] * 3
        + [_sds((S, LANES), F32), _sds((S, N_HEADS * LANES), F32)],
        scratch_shapes=[pltpu.VMEM((8, LANES), F32)], compiler_params=_params())(x, g, wa, gq, gk, b_pad)


def _key_le_query(offset, keys=TK):
    r = lax.broadcasted_iota(jnp.int32, (keys, TQ), 0)
    c = lax.broadcasted_iota(jnp.int32, (keys, TQ), 1)
    return (r + offset) <= c


def _widen(tile):
    return jnp.concatenate([tile] * (TQ // LANES), axis=1)


def _fox_fwd(qn, kn, vb, proj, crow, cbc, rider=()):
    nq, per = S // TQ, TQ // TK

    def body(q_ref, k_ref, v_ref, g_ref, cq_ref, cbc_ref, o_ref, z_ref, lse_ref, st_s, pt_s):
        i = pl.program_id(1)
        qs = [q_ref[:, HD * hh:HD * (hh + 1)] for hh in range(HPS)]
        cqs = [cq_ref[hh, 0] for hh in range(HPS)]

        def scores(s, hh):
            off = pl.multiple_of(s * KS, KS)
            kj = k_ref[pl.ds(off, KS), HD * hh:HD * (hh + 1)]
            return (_dot_nt(kj, qs[hh]) + cqs[hh]) - _widen(cbc_ref[pl.ds(off, KS), LANES * hh:LANES * (hh + 1)])

        def values(s, hh, pt):
            off = pl.multiple_of(s * KS, KS)
            return _dot_tn(v_ref[pl.ds(off, KS), HD * hh:HD * (hh + 1)], pt)

        def step(s, slot, carries, mask=None, last=False):
            if not last:
                for hh in range(HPS):
                    st_s[1 - slot, hh] = scores(s + 1, hh)
            pvs = [values(jnp.maximum(s - 1, 0), hh, pt_s[1 - slot, hh]) for hh in range(HPS)]
            out = []
            for hh in range(HPS):
                m, l, acc = carries[hh]
                st = st_s[slot, hh]
                if mask is not None:
                    st = jnp.where(mask, st, -jnp.inf)
                m_new = jnp.maximum(m, jnp.max(st, axis=0, keepdims=True))
                pt = jnp.exp(st - m_new)
                alpha = jnp.exp(m - m_new)
                pt_s[slot, hh] = pt.astype(BF16)
                out.append((m_new, alpha * l + jnp.sum(pt, axis=0, keepdims=True), alpha * (acc + pvs[hh])))
            return tuple(out)

        for hh in range(HPS):
            st_s[0, hh] = scores(0, hh)
            pt_s[1, hh] = jnp.zeros((KS, TQ), BF16)
        one = (jnp.full((1, TQ), -jnp.inf, F32), jnp.zeros((1, TQ), F32), jnp.zeros((HD, TQ), F32))
        carries = lax.fori_loop(0, i, lambda t, cr: step(2 * t + 1, 1, step(2 * t, 0, cr)), (one,) * HPS)
        carries = step(2 * i, 0, carries, mask=_key_le_query(0, KS))
        carries = step(2 * i + 1, 1, carries, mask=_key_le_query(KS, KS), last=True)
        accs = []
        for hh in range(HPS):
            m, l, acc = carries[hh]
            acc = acc + values(2 * i + 1, hh, pt_s[1, hh])
            accs.append(acc / l)
            lse_ref[hh, 0] = m + jnp.log(l)
        o = jnp.concatenate(accs, axis=0).T
        o_ref[...] = o
        g = g_ref[...]
        z_ref[...] = (o * (g * _sigmoid(g))).astype(BF16)

    qblk = pl.BlockSpec((TQ, HW), lambda hp, i: (i, hp))
    full = pl.BlockSpec((S, HW), lambda hp, i: (0, hp))
    rows = pl.BlockSpec((HPS, 1, 1, TQ), lambda hp, i: (hp, i, 0, 0))
    return _call(
        body, name="fox_fwd", args=(qn, kn, vb, proj, crow, cbc), grid=(N_HEADS // HPS, nq),
        in_specs=[qblk, full, full,
                  pl.BlockSpec((TQ, HW), lambda hp, i: (i, GOFF // HW + hp)),
                  rows, pl.BlockSpec((S, HPS * LANES), lambda hp, i: (0, hp))],
        out_specs=[qblk, qblk, rows],
        out_shape=[_sds((S, D), F32), _sds((S, D), BF16), _sds((N_HEADS, nq, 1, TQ), F32)],
        scratch_shapes=[pltpu.VMEM((2, HPS, KS, TQ), F32), pltpu.VMEM((2, HPS, KS, TQ), BF16)], rider=rider)


def _fox_bwd_pre(dz, proj, o):
    nq = S // TQ

    def body(dz_ref, g_ref, o_ref, do_ref, dg_ref, delta_ref):
        g = g_ref[...]
        sg = _sigmoid(g)
        dzv = dz_ref[...]
        ov = o_ref[...]
        do = dzv * (g * sg)
        dg_ref[...] = (dzv * ov * (sg * (1.0 + g * (1.0 - sg)))).astype(BF16)
        do_ref[...] = do.astype(BF16)
        prod_t = (do * ov).T
        for h in range(N_HEADS):
            delta_ref[h, 0] = jnp.sum(prod_t[HD * h:HD * (h + 1), :], axis=0, keepdims=True)

    row = pl.BlockSpec((TQ, D), lambda i: (i, 0))
    return pl.pallas_call(
        body, name="fox_bwd_pre", grid=(nq,),
        in_specs=[row, pl.BlockSpec((TQ, D), lambda i: (i, GOFF // D)), row],
        out_specs=[row, row, pl.BlockSpec((N_HEADS, 1, 1, TQ), lambda i: (0, i, 0, 0))],
        out_shape=[_sds((S, D), BF16), _sds((S, D), BF16), _sds((N_HEADS, nq, 1, TQ), F32)],
        compiler_params=_params())(dz, proj, o)


def _fox_bwd(qn, kn, vb, dob, lse, delta, crow, cbc, rider=()):
    nq, nkb = S // TQ, S // TK

    def body(q_ref, k_ref, v_ref, do_ref, lse_ref, del_ref, cq_ref, cbc_ref,
             dk_ref, dv_ref, dcs_ref, dq_ref, dr_ref, st_s, dp_s, pt_s, ds_s, dq_acc, dr_acc):
        j = pl.program_id(1)

        @pl.when(j == 0)
        def _():
            dq_acc[...] = jnp.zeros_like(dq_acc)
            dr_acc[...] = jnp.zeros_like(dr_acc)

        kjs = [k_ref[:, HD * hh:HD * (hh + 1)] for hh in range(HPS)]
        vjs = [v_ref[:, HD * hh:HD * (hh + 1)] for hh in range(HPS)]

        def rows_of(ref, u, hh):
            off = pl.multiple_of(u * TQ, TQ)
            return ref[pl.ds(off, TQ), HD * hh:HD * (hh + 1)]

        def products(u, hh):
            st = (_dot_nt(kjs[hh], rows_of(q_ref, u, hh)) + cq_ref[hh, u]) - _widen(
                cbc_ref[:, LANES * hh:LANES * (hh + 1)])
            return st, _dot_nt(vjs[hh], rows_of(do_ref, u, hh))

        def step(u, slot, carries, masked=False):
            nxt = jnp.minimum(u + 1, nq - 1)
            for hh in range(HPS):
                st_s[1 - slot, hh], dp_s[1 - slot, hh] = products(nxt, hh)
            prev = jnp.maximum(u - 1, 0)
            dvs = [_dot_nn(pt_s[1 - slot, hh], rows_of(do_ref, prev, hh)) for hh in range(HPS)]
            dks = [_dot_nn(ds_s[1 - slot, hh], rows_of(q_ref, prev, hh)) for hh in range(HPS)]
            for hh in range(HPS):
                dq_acc[hh, prev] += _dot_tn(kjs[hh], ds_s[1 - slot, hh])
            out = []
            for hh in range(HPS):
                dk, dv, dcs = carries[hh]
                st = st_s[slot, hh]
                if masked:
                    st = jnp.where(_key_le_query((j - u) * TQ), st, -jnp.inf)
                pt = jnp.exp(st - lse_ref[hh, u])
                dst = pt * (dp_s[slot, hh] - del_ref[hh, u])
                pt_s[slot, hh] = pt.astype(BF16)
                ds_s[slot, hh] = dst.astype(BF16)
                dr_acc[hh, u] += jnp.sum(dst, axis=0, keepdims=True)
                out.append((dk + dks[hh], dv + dvs[hh], dcs + (dst[:, :LANES] + dst[:, LANES:])))
            return tuple(out)

        t0 = j // 2
        for hh in range(HPS):
            st_s[0, hh], dp_s[0, hh] = products(2 * t0, hh)
            pt_s[1, hh] = jnp.zeros((TK, TQ), BF16)
            ds_s[1, hh] = jnp.zeros((TK, TQ), BF16)
        one = (jnp.zeros((TK, HD), F32), jnp.zeros((TK, HD), F32), jnp.zeros((TK, LANES), F32))
        carries = step(2 * t0 + 1, 1, step(2 * t0, 0, (one,) * HPS, masked=True), masked=True)
        carries = lax.fori_loop(t0 + 1, nq // 2, lambda t, cr: step(2 * t + 1, 1, step(2 * t, 0, cr)), carries)
        dks, dvs = [], []
        for hh in range(HPS):
            dk, dv, dcs = carries[hh]
            dks.append(dk + _dot_nn(ds_s[1, hh], rows_of(q_ref, nq - 1, hh)))
            dvs.append(dv + _dot_nn(pt_s[1, hh], rows_of(do_ref, nq - 1, hh)))
            dq_acc[hh, nq - 1] += _dot_tn(kjs[hh], ds_s[1, hh])
            dcs_ref[:, LANES * hh:LANES * (hh + 1)] = jnp.broadcast_to(
                -jnp.sum(dcs, axis=-1, keepdims=True), (TK, LANES))
        dk_ref[...] = jnp.concatenate(dks, axis=-1)
        dv_ref[...] = jnp.concatenate(dvs, axis=-1)

        @pl.when(j == nkb - 1)
        def _():
            for i in range(nq):
                dq_ref[TQ * i:TQ * (i + 1), :] = jnp.concatenate([dq_acc[hh, i] for hh in range(HPS)], axis=0).T
            dr_ref[...] = dr_acc[...]

    kblk = pl.BlockSpec((TK, HW), lambda hp, j: (j, hp))
    full = pl.BlockSpec((S, HW), lambda hp, j: (0, hp))
    rows = pl.BlockSpec((HPS, nq, 1, TQ), lambda hp, j: (hp, 0, 0, 0))
    cblk = pl.BlockSpec((TK, HPS * LANES), lambda hp, j: (j, hp))
    return _call(
        body, name="fox_bwd", args=(qn, kn, vb, dob, lse, delta, crow, cbc), grid=(N_HEADS // HPS, nkb),
        in_specs=[full, kblk, kblk, full, rows, rows, rows, cblk],
        out_specs=[kblk, kblk, cblk, full, rows],
        out_shape=[_sds((S, D), F32), _sds((S, D), F32), _sds((S, N_HEADS * LANES), F32), _sds((S, D), F32),
                   _sds((N_HEADS, nq, 1, TQ), F32)],
        scratch_shapes=[pltpu.VMEM((2, HPS, TK, TQ), F32), pltpu.VMEM((2, HPS, TK, TQ), F32),
                        pltpu.VMEM((2, HPS, TK, TQ), BF16), pltpu.VMEM((2, HPS, TK, TQ), BF16),
                        pltpu.VMEM((HPS, nq, HD, TQ), F32), pltpu.VMEM((HPS, nq, 1, TQ), F32)], rider=rider)


def _prep_a_bwd(dq, dk, dv, dgate, drow, dcs, proj, b_pad, gq, gk):
    nt = S // TM

    def body(dq_ref, dk_ref, dv_ref, dgt_ref, dr_ref, dcs_ref, xq_ref, xk_ref, f_ref, b_ref, gq_ref, gk_ref,
             o_ref, dgq_ref, dgk_ref, db_ref, carry):
        @pl.when(pl.program_id(0) == 0)
        def _():
            carry[...] = jnp.zeros_like(carry)
            dgq_ref[...] = jnp.zeros_like(dgq_ref)
            dgk_ref[...] = jnp.zeros_like(dgk_ref)
            db_ref[...] = jnp.zeros_like(db_ref)

        lane = _lane_iota((TM, LANES))
        lo_half = lane < HD
        gq2, gk2 = _g2(gq_ref), _g2(gk_ref)
        dgq, dgk = jnp.zeros((1, LANES), F32), jnp.zeros((1, LANES), F32)
        for c in _pairs(D):
            dxq, dg = _head_norm_bwd(dq_ref[:, c] * QSCALE, xq_ref[:, c], gq2, lo_half)
            o_ref[:, c] = dxq.astype(BF16)
            dgq = dgq + dg
            dxk, dg = _head_norm_bwd(dk_ref[:, c], xk_ref[:, c], gk2, lo_half)
            o_ref[:, D + c.start:D + c.stop] = dxk.astype(BF16)
            dgk = dgk + dg
        dgq_ref[...] += dgq
        dgk_ref[...] += dgk
        o_ref[:, 2 * D:3 * D] = dv_ref[...].astype(BF16)
        o_ref[:, GOFF:GOFF + D] = dgt_ref[...]

        dc = dr_ref[...]
        for h in range(N_HEADS):
            dc = dc + jnp.where(lane == h, dcs_ref[:, LANES * h:LANES * h + 1], 0.0)
        r = lax.broadcasted_iota(jnp.int32, (TM, TM), 0)
        c = lax.broadcasted_iota(jnp.int32, (TM, TM), 1)
        tri = (c >= r).astype(F32)
        dlogf = jnp.dot(tri, dc, precision=lax.Precision.HIGHEST, preferred_element_type=F32) + carry[0:1, :]
        carry[0:1, :] = dlogf[0:1, :]
        df = dlogf * (1.0 / (1.0 + jnp.exp(f_ref[...] + b_ref[...])))
        db_ref[...] += jnp.sum(df, axis=0, keepdims=True)
        o_ref[:, FOFF:FOFF + LANES] = df.astype(BF16)
        o_ref[:, FOFF + LANES:NA] = jnp.zeros((TM, NA - FOFF - LANES), BF16)

    rev = lambda width, col: pl.BlockSpec((TM, width), lambda i: (nt - 1 - i, col))
    gspec = pl.BlockSpec((1, HD), lambda i: (0, 0))
    acc = pl.BlockSpec((1, LANES), lambda i: (0, 0))
    return pl.pallas_call(
        body, name="prep_a_bwd", grid=(nt,),
        in_specs=[rev(D, 0), rev(D, 0), rev(D, 0), rev(D, 0), rev(LANES, 0), rev(N_HEADS * LANES, 0),
                  rev(D, 0), rev(D, 1), rev(LANES, FOFF // LANES), acc, gspec, gspec],
        out_specs=[rev(NA, 0), acc, acc, acc],
        out_shape=[_sds((S, NA), BF16)] + [_sds((1, LANES), F32)] * 3,
        scratch_shapes=[pltpu.VMEM((8, LANES), F32)],
        compiler_params=_params())(dq, dk, dv, dgate, drow, dcs, proj, proj, proj, b_pad, gq, gk)


def _head_b(h1, g_kv, g_b, w_kv, w_in_b, gq, gk, cos2, sin2):
    nkv = w_kv.shape[1] // 2

    def body(x_ref, gkv_ref, gb_ref, wkv_ref, wb_ref, gq_ref, gk_ref, c_ref, s_ref,
             ukv_ref, ub_ref, kv_ref, pb_ref, qo_ref, ko_ref, vo_ref):
        xv = x_ref[...]
        xn = xv * _rms_rinv(xv)
        ukv = (xn * gkv_ref[...]).astype(BF16)
        ub = (xn * gb_ref[...]).astype(BF16)
        ukv_ref[...] = ukv
        ub_ref[...] = ub
        kv_ref[...] = _dot_nn(ukv, wkv_ref[...])
        for lo in range(0, 2 * D, D):
            pb_ref[:, lo:lo + D] = _dot_nn(ub, wb_ref[:, lo:lo + D])
        lane = _lane_iota((TM, LANES))
        lo_half = lane < HD
        cos, sin = c_ref[...], s_ref[...]
        gq2, gk2 = _g2(gq_ref), _g2(gk_ref)
        for c in _pairs(D):
            q = pb_ref[:, c]
            qo_ref[:, c] = (_rope_fwd((q * _head_rinv(q, lo_half)) * gq2, cos, sin, lane) * QSCALE).astype(BF16)
        for c in _pairs(nkv):
            k = kv_ref[:, c]
            ko_ref[:, c] = _rope_fwd((k * _head_rinv(k, lo_half)) * gk2, cos, sin, lane).astype(BF16)
        vo_ref[...] = kv_ref[:, nkv:2 * nkv].astype(BF16)

    row = lambda width: pl.BlockSpec((TM, width), lambda i: (i, 0))
    whole = lambda arr: pl.BlockSpec(arr.shape, lambda i: (0,) * arr.ndim)
    return pl.pallas_call(
        body, name="head_b", grid=(S // TM,),
        in_specs=[row(D), whole(g_kv), whole(g_b), whole(w_kv), whole(w_in_b), whole(gq), whole(gk),
                  row(LANES), row(LANES)],
        out_specs=[row(D), row(D), row(2 * nkv), row(2 * D), row(D), row(nkv), row(nkv)],
        out_shape=[_sds((S, D), BF16), _sds((S, D), BF16), _sds((S, 2 * nkv), F32), _sds((S, 2 * D), F32),
                   _sds((S, D), BF16), _sds((S, nkv), BF16), _sds((S, nkv), BF16)],
        compiler_params=_params())(h1, g_kv, g_b, w_kv, w_in_b, gq, gk, cos2, sin2)


N_KV, GRP = 4, 4


def _swa_mask(n):
    r = lax.broadcasted_iota(jnp.int32, (2 * WIN, GRP * WIN), 0)
    q = lax.broadcasted_iota(jnp.int32, (2 * WIN, GRP * WIN), 1) & (WIN - 1)
    return (r > q) & (r <= q + WIN) & ((r >= WIN) | (n > 0))


def _stack4(ref_or_val, base):
    return jnp.concatenate([ref_or_val[:, base + HD * g: base + HD * (g + 1)] for g in range(GRP)], axis=0)


def _unstack4(xt):
    return jnp.concatenate([xt[:, WIN * g:WIN * (g + 1)] for g in range(GRP)], axis=0).T


def _band(prev_ref, cur_ref, kh):
    return jnp.concatenate([prev_ref[:, HD * kh:HD * (kh + 1)], cur_ref[:, HD * kh:HD * (kh + 1)]], axis=0)


def _sink_row(s_ref, first):
    lane = _lane_iota((1, GRP * WIN))
    row = jnp.full((1, GRP * WIN), s_ref[first + GRP - 1], F32)
    for g in range(GRP - 2, -1, -1):
        row = jnp.where(lane < WIN * (g + 1), s_ref[first + g], row)
    return row


def _swa_fwd(qb, ksh, vsh, pb, sinks):
    nb = S // WIN

    def body(q_ref, kp_ref, kc_ref, vp_ref, vc_ref, g_ref, s_ref, o_ref, z_ref, lse_ref):
        n = pl.program_id(0)
        valid = _swa_mask(n)
        outs = []
        for kh in range(N_KV):
            kb, vb = _band(kp_ref, kc_ref, kh), _band(vp_ref, vc_ref, kh)
            st = jnp.where(valid, _dot_nt(kb, _stack4(q_ref, GRP * HD * kh)), -jnp.inf)
            sink = _sink_row(s_ref, GRP * kh)
            m = jnp.maximum(jnp.max(st, axis=0, keepdims=True), sink)
            pt = jnp.exp(st - m)
            l = jnp.sum(pt, axis=0, keepdims=True) + jnp.exp(sink - m)
            outs.append(_unstack4(_dot_tn(vb, pt.astype(BF16)) / l))
            lse = m + jnp.log(l)
            for g in range(GRP):
                lse_ref[GRP * kh + g, 0] = lse[:, WIN * g:WIN * (g + 1)]
        o = jnp.concatenate(outs, axis=-1)
        o_ref[...] = o
        g = g_ref[...]
        z_ref[...] = (o * (g * _sigmoid(g))).astype(BF16)

    row = pl.BlockSpec((WIN, D), lambda n: (n, 0))
    prev = pl.BlockSpec((WIN, N_KV * HD), lambda n: (jnp.maximum(n - 1, 0), 0))
    cur = pl.BlockSpec((WIN, N_KV * HD), lambda n: (n, 0))
    return pl.pallas_call(
        body, name="swa_fwd", grid=(nb,),
        in_specs=[row, prev, cur, prev, cur, pl.BlockSpec((WIN, D), lambda n: (n, 1)),
                  pl.BlockSpec(memory_space=pltpu.SMEM)],
        out_specs=[row, row, pl.BlockSpec((N_HEADS, 1, 1, WIN), lambda n: (0, n, 0, 0))],
        out_shape=[_sds((S, D), F32), _sds((S, D), BF16), _sds((N_HEADS, nb, 1, WIN), F32)],
        compiler_params=_params())(qb, ksh, ksh, vsh, vsh, pb, sinks)


def _swa_bwd(qb, ksh, vsh, dz, o, lse, pb, sinks):
    nb = S // WIN

    def body(q_ref, kp_ref, kc_ref, vp_ref, vc_ref, dz_ref, o_ref, lse_ref, g_ref, s_ref,
             dq_ref, dg_ref, dka_ref, dkb_ref, dva_ref, dvb_ref, dsink_ref):
        n = pl.program_id(0)

        @pl.when(n == 0)
        def _():
            dsink_ref[...] = jnp.zeros_like(dsink_ref)

        valid = _swa_mask(n)
        g = g_ref[...]
        sg = _sigmoid(g)
        dzv = dz_ref[...]
        ov = o_ref[...]
        do = dzv * (g * sg)
        dg_ref[...] = (dzv * ov * (sg * (1.0 + g * (1.0 - sg)))).astype(BF16)
        prod_t = (do * ov).T
        lane1 = _lane_iota((1, LANES))
        dqs, dkas, dkbs, dvas, dvbs = [], [], [], [], []
        dsink = jnp.zeros((1, LANES), F32)
        for kh in range(N_KV):
            kb, vb = _band(kp_ref, kc_ref, kh), _band(vp_ref, vc_ref, kh)
            base = GRP * HD * kh
            qs = _stack4(q_ref, base)
            dos = _stack4(do, base).astype(BF16)
            delta = jnp.concatenate(
                [jnp.sum(prod_t[base + HD * gg:base + HD * (gg + 1), :], axis=0, keepdims=True)
                 for gg in range(GRP)], axis=1)
            lse = jnp.concatenate([lse_ref[GRP * kh + gg, 0] for gg in range(GRP)], axis=1)
            st = jnp.where(valid, _dot_nt(kb, qs), -jnp.inf)
            pt = jnp.exp(st - lse)
            dst = pt * (_dot_nt(vb, dos) - delta)
            dsb = dst.astype(BF16)
            dqs.append(_unstack4(_dot_tn(kb, dsb)))
            dkband = _dot_nn(dsb, qs)
            dvband = _dot_nn(pt.astype(BF16), dos)
            dkbs.append(dkband[0:WIN, :])
            dkas.append(dkband[WIN:2 * WIN, :])
            dvbs.append(dvband[0:WIN, :])
            dvas.append(dvband[WIN:2 * WIN, :])
            ps_delta = jnp.exp(_sink_row(s_ref, GRP * kh) - lse) * delta
            for gg in range(GRP):
                val = jnp.sum(ps_delta[:, WIN * gg:WIN * (gg + 1)], axis=1, keepdims=True)
                dsink = dsink - jnp.where(lane1 == GRP * kh + gg, val, 0.0)
        dq_ref[...] = jnp.concatenate(dqs, axis=-1)
        dka_ref[...] = jnp.concatenate(dkas, axis=-1)
        dkb_ref[...] = jnp.concatenate(dkbs, axis=-1)
        dva_ref[...] = jnp.concatenate(dvas, axis=-1)
        dvb_ref[...] = jnp.concatenate(dvbs, axis=-1)
        dsink_ref[...] += dsink

    row = pl.BlockSpec((WIN, D), lambda n: (n, 0))
    prev = pl.BlockSpec((WIN, N_KV * HD), lambda n: (jnp.maximum(n - 1, 0), 0))
    cur = pl.BlockSpec((WIN, N_KV * HD), lambda n: (n, 0))
    return pl.pallas_call(
        body, name="swa_bwd", grid=(nb,),
        in_specs=[row, prev, cur, prev, cur, row, row, pl.BlockSpec((N_HEADS, 1, 1, WIN), lambda n: (0, n, 0, 0)),
                  pl.BlockSpec((WIN, D), lambda n: (n, 1)), pl.BlockSpec(memory_space=pltpu.SMEM)],
        out_specs=[row, row, cur, cur, cur, cur, pl.BlockSpec((1, LANES), lambda n: (0, 0))],
        out_shape=[_sds((S, D), F32), _sds((S, D), BF16)] + [_sds((S, 256), F32)] * 4 + [_sds((1, LANES), F32)],
        compiler_params=_params())(qb, ksh, ksh, vsh, vsh, dz, o, lse, pb, sinks)


def _prep_b_bwd(dq, dgate, pb, gq, cos2, sin2):
    def body(dq_ref, dgt_ref, x_ref, g_ref, c_ref, s_ref, o_ref, dgq_ref):
        @pl.when(pl.program_id(0) == 0)
        def _():
            dgq_ref[...] = jnp.zeros_like(dgq_ref)

        lane = _lane_iota((TM, LANES))
        g2, cos, sin = _g2(g_ref), c_ref[...], s_ref[...]
        dg_tot = jnp.zeros((1, LANES), F32)
        for c in _pairs(D):
            dn = _rope_bwd(dq_ref[:, c] * QSCALE, cos, sin, lane)
            dx, dg = _head_norm_bwd(dn, x_ref[:, c], g2, lane < HD)
            o_ref[:, c] = dx.astype(BF16)
            dg_tot = dg_tot + dg
        dgq_ref[...] += dg_tot
        o_ref[:, D:2 * D] = dgt_ref[...]

    row = pl.BlockSpec((TM, D), lambda i: (i, 0))
    tab = pl.BlockSpec((TM, LANES), lambda i: (i, 0))
    return pl.pallas_call(
        body, name="prep_b_bwd", grid=(S // TM,),
        in_specs=[row, row, row, pl.BlockSpec((1, HD), lambda i: (0, 0)), tab, tab],
        out_specs=[pl.BlockSpec((TM, 2 * D), lambda i: (i, 0)), pl.BlockSpec((1, LANES), lambda i: (0, 0))],
        out_shape=[_sds((S, 2 * D), BF16), _sds((1, LANES), F32)],
        compiler_params=_params())(dq, dgate, pb, gq, cos2, sin2)


def _prep_kv_bwd(dka, dkb, dva, dvb, kv, gk, cos2, sin2):
    nt = S // RB
    per = RB // WIN

    def shifted(cur_ref, nxt_ref, has_next):
        return jnp.concatenate([cur_ref[WIN:RB, :], jnp.where(has_next, nxt_ref[...], 0.0)], axis=0)

    def body(dka_ref, dkb_ref, dkn_ref, dva_ref, dvb_ref, dvn_ref, x_ref, g_ref, c_ref, s_ref, o_ref, dgk_ref):
        i, j = pl.program_id(0), pl.program_id(1)

        @pl.when((i == 0) & (j == 0))
        def _():
            dgk_ref[...] = jnp.zeros_like(dgk_ref)

        has_next = i < nt - 1

        @pl.when(j == 0)
        def _():
            lane = _lane_iota((RB, LANES))
            g2, cos, sin = _g2(g_ref), c_ref[...], s_ref[...]
            dy_all = dka_ref[...] + shifted(dkb_ref, dkn_ref, has_next)
            dg_tot = jnp.zeros((1, LANES), F32)
            for c in _pairs(CB):
                dn = _rope_bwd(dy_all[:, c], cos, sin, lane)
                dx, dg = _head_norm_bwd(dn, x_ref[:, c], g2, lane < HD)
                o_ref[:, c] = dx.astype(BF16)
                dg_tot = dg_tot + dg
            dgk_ref[...] += dg_tot

        @pl.when(j == 1)
        def _():
            o_ref[...] = (dva_ref[...] + shifted(dvb_ref, dvn_ref, has_next)).astype(BF16)

    cur = pl.BlockSpec((RB, CB), lambda i, j: (i, 0))
    nxt = pl.BlockSpec((WIN, CB), lambda i, j: (jnp.minimum(per * (i + 1), S // WIN - 1), 0))
    tab = pl.BlockSpec((RB, LANES), lambda i, j: (i, 0))
    return pl.pallas_call(
        body, name="prep_kv_bwd", grid=(nt, 2),
        in_specs=[cur, cur, nxt, cur, cur, nxt, cur, pl.BlockSpec((1, HD), lambda i, j: (0, 0)), tab, tab],
        out_specs=[pl.BlockSpec((RB, CB), lambda i, j: (i, j)), pl.BlockSpec((1, LANES), lambda i, j: (0, 0))],
        out_shape=[_sds((S, 2 * CB), BF16), _sds((1, LANES), F32)],
        compiler_params=_params())(dka, dkb, dkb, dva, dvb, dvb, kv, gk, cos2, sin2)


def _loss_dy(y, tgt):
    def body(y_ref, t_ref, dy_ref, l_ref):
        @pl.when(pl.program_id(0) == 0)
        def _():
            l_ref[...] = jnp.zeros_like(l_ref)

        e = y_ref[...] - t_ref[...]
        dy_ref[...] = e * (1.0 / D)
        l_ref[...] += jnp.sum(jnp.sum(e * e, axis=-1, keepdims=True), axis=0, keepdims=True)

    row = pl.BlockSpec((TM, D), lambda i: (i, 0))
    return pl.pallas_call(
        body, name="loss_dy", grid=(S // TM,), in_specs=[row, row],
        out_specs=[row, pl.BlockSpec((1, LANES), lambda i: (0, 0))],
        out_shape=[_sds((S, D), F32), _sds((1, LANES), F32)], compiler_params=_params())(y, tgt)


def _du_a_rms_bwd(dproj, wa, x, g, dres, rider=()):
    tm, tk = 512, NA // 2
    nk = NA // tk

    def body(a_ref, b_ref, x_ref, g_ref, dr_ref, dx_ref, dg_ref, acc):
        i, kk = pl.program_id(0), pl.program_id(1)

        @pl.when((i == 0) & (kk == 0))
        def _():
            dg_ref[...] = jnp.zeros_like(dg_ref)

        p = _dot_nt(a_ref[...], b_ref[...])

        @pl.when(kk == 0)
        def _():
            acc[...] = p

        @pl.when(kk == nk - 1)
        def _():
            dx, dg = _rms_bwd_core(acc[...] + p, x_ref[...], g_ref[...])
            dx_ref[...] = dr_ref[...] + dx
            dg_ref[...] += dg

    assert nk == 2
    row = pl.BlockSpec((tm, D), lambda i, kk: (i, 0))
    vec = pl.BlockSpec((1, D), lambda i, kk: (0, 0))
    return _call(
        body, name="du_a_rms_bwd", args=(dproj, wa, x, g, dres), grid=(S // tm, nk),
        in_specs=[pl.BlockSpec((tm, tk), lambda i, kk: (i, kk)), pl.BlockSpec((D, tk), lambda i, kk: (0, kk)),
                  row, vec, row],
        out_specs=[row, vec], out_shape=[_sds((S, D), F32), _sds((1, D), F32)],
        scratch_shapes=[pltpu.VMEM((tm, D), F32)], rider=rider)


def _du_b_rms_bwd(dpb, w_in_b, dkv, w_kv, h1, g_b, g_kv, dy):
    tm = 512

    def body(ab_ref, wb_ref, akv_ref, wkv_ref, x_ref, gb_ref, gkv_ref, dy_ref, dh_ref, dgb_ref, dgkv_ref):
        @pl.when(pl.program_id(0) == 0)
        def _():
            dgb_ref[...] = jnp.zeros_like(dgb_ref)
            dgkv_ref[...] = jnp.zeros_like(dgkv_ref)

        x = x_ref[...]
        dx1, dg1 = _rms_bwd_core(_dot_nt(ab_ref[...], wb_ref[...]), x, gb_ref[...])
        dx2, dg2 = _rms_bwd_core(_dot_nt(akv_ref[...], wkv_ref[...]), x, gkv_ref[...])
        dh_ref[...] = dy_ref[...] + dx1 + dx2
        dgb_ref[...] += dg1
        dgkv_ref[...] += dg2

    row = lambda width: pl.BlockSpec((tm, width), lambda i: (i, 0))
    whole = lambda arr: pl.BlockSpec(arr.shape, lambda i: (0, 0))
    vec = pl.BlockSpec((1, D), lambda i: (0, 0))
    return pl.pallas_call(
        body, name="du_b_rms_bwd", grid=(S // tm,),
        in_specs=[row(dpb.shape[1]), whole(w_in_b), row(dkv.shape[1]), whole(w_kv), row(D), vec, vec, row(D)],
        out_specs=[row(D), vec, vec], out_shape=[_sds((S, D), F32), _sds((1, D), F32), _sds((1, D), F32)],
        compiler_params=_params())(dpb, w_in_b, dkv, w_kv, h1, g_b, g_kv, dy)


def _gather_first(w_in_a, w_out_a, w_kv, w_in_b, w_out_b, norm_a_g):
    def body(wia_ref, woa_ref, wkv_ref, wib_ref, wob_ref, ga_ref,
             wa_g, ga_g, woa_s, wkv_s, wib_s, wob_s, wa_s, *sems):
        wa_s[...] = wia_ref[0].astype(BF16)
        woa_s[...] = woa_ref[0].astype(BF16)
        wkv_s[...] = wkv_ref[...].astype(BF16)
        wib_s[...] = wib_ref[0].astype(BF16)
        wob_s[...] = wob_ref[0].astype(BF16)
        _gather_two_level([wa_s, ga_ref], [wa_g, ga_g], sems)

    vmem = pl.BlockSpec(memory_space=pltpu.VMEM)
    anyspec = pl.BlockSpec(memory_space=pl.ANY)
    shard = lambda w: _sds(w.shape[-2:], BF16)
    return pl.pallas_call(
        body, name="gather_first", in_specs=[vmem] * 6, out_specs=[anyspec, anyspec, vmem, vmem, vmem, vmem],
        out_shape=[_sds((N_DEV,) + w_in_a.shape[-2:], BF16), _sds((N_DEV,) + norm_a_g.shape, F32),
                   shard(w_out_a), shard(w_kv), shard(w_in_b), shard(w_out_b)],
        scratch_shapes=[pltpu.VMEM(w_in_a.shape[-2:], BF16)] + _exchange_sems(2),
        compiler_params=pltpu.CompilerParams(vmem_limit_bytes=VMEM_LIMIT, has_side_effects=True))(
            w_in_a, w_out_a, w_kv, w_in_b, w_out_b, norm_a_g)


def _pair_reduce(slots):
    n_chip = N_DEV // 2
    _, rows, cols = slots.shape

    def body(s_ref, o_ref, own_v, sib_v, send_sems, recv_sems, local_sems):
        x, y, c = lax.axis_index("x"), lax.axis_index("y"), lax.axis_index("c")
        copies = []
        for j in range(n_chip):
            own = pltpu.make_async_copy(s_ref.at[2 * j + c], own_v.at[j], local_sems.at[j])
            give = pltpu.make_async_remote_copy(
                src_ref=s_ref.at[2 * j + 1 - c], dst_ref=sib_v.at[j], send_sem=send_sems.at[j],
                recv_sem=recv_sems.at[j], device_id=(x, y, 1 - c), device_id_type=pl.DeviceIdType.MESH)
            own.start()
            give.start()
            copies.append((own, give))
        for j, (own, give) in enumerate(copies):
            own.wait()
            give.wait()
            o_ref[j] = (own_v[j].astype(F32) + sib_v[j].astype(F32)).astype(BF16)

    half = _sds((n_chip, rows, cols), slots.dtype)
    return pl.pallas_call(
        body, name="pair_reduce", in_specs=[pl.BlockSpec(memory_space=pl.ANY)],
        out_specs=pl.BlockSpec(memory_space=pltpu.VMEM), out_shape=half,
        scratch_shapes=[pltpu.VMEM(half.shape, half.dtype), pltpu.VMEM(half.shape, half.dtype),
                        pltpu.SemaphoreType.DMA((n_chip,)), pltpu.SemaphoreType.DMA((n_chip,)),
                        pltpu.SemaphoreType.DMA((n_chip,))],
        compiler_params=pltpu.CompilerParams(vmem_limit_bytes=VMEM_LIMIT, has_side_effects=True))(slots)


def _padded_col(c):
    if c < RAW_F:
        return c
    return FOFF + (c - RAW_F) if c < RAW_G else GOFF + (c - RAW_G)


def _shard_pieces():
    width = NA_RAW // N_DEV
    pieces = []
    for d in range(N_DEV):
        cuts = [width * d] + [c for c in (RAW_F, RAW_G) if width * d < c < width * (d + 1)] + [width * (d + 1)]
        for lo, hi in zip(cuts[:-1], cuts[1:]):
            pieces.append((d, lo - width * d, _padded_col(lo), hi - lo))
    return pieces


def _unshard_wa(wa_g):
    def body(w_ref, o_ref):
        o_ref[:, FOFF + N_HEADS:NA] = jnp.zeros((TM, NA - FOFF - N_HEADS), BF16)
        for d, src, dst, width in _shard_pieces():
            o_ref[:, dst:dst + width] = w_ref[d, :, src:src + width]

    return pl.pallas_call(
        body, name="unshard_wa", grid=(D // TM,),
        in_specs=[pl.BlockSpec((N_DEV, TM, NA_RAW // N_DEV), lambda i: (0, i, 0))],
        out_specs=pl.BlockSpec((TM, NA), lambda i: (i, 0)), out_shape=_sds((D, NA), BF16),
        compiler_params=_params())(wa_g)


def _reshard_dwa(dwa):
    def body(g_ref, o_ref):
        for d, src, dst, width in _shard_pieces():
            o_ref[d, :, src:src + width] = g_ref[:, dst:dst + width]

    return pl.pallas_call(
        body, name="reshard_dwa", grid=(D // TM,), in_specs=[pl.BlockSpec((TM, NA), lambda i: (i, 0))],
        out_specs=pl.BlockSpec((N_DEV, TM, NA_RAW // N_DEV), lambda i: (0, i, 0)),
        out_shape=_sds((N_DEV, D, NA_RAW // N_DEV), dwa.dtype), compiler_params=_params())(dwa)


def _gather_slab(slab):
    def body(s_ref, o_ref, *sems):
        _exchange_ops(["gather_rows"], [s_ref], [o_ref], sems, True, True)

    anyspec = pl.BlockSpec(memory_space=pl.ANY)
    return pl.pallas_call(
        body, name="gather_slab", in_specs=[anyspec], out_specs=anyspec,
        out_shape=_sds((N_DEV,) + slab.shape, slab.dtype), scratch_shapes=_exchange_sems(1),
        compiler_params=pltpu.CompilerParams(has_side_effects=True))(slab)


def _adamw(w, g, m, v):
    m = ADAM_B1 * m + (1.0 - ADAM_B1) * g
    v = ADAM_B2 * v + (1.0 - ADAM_B2) * (g * g)
    m_hat = m / (1.0 - ADAM_B1 ** ADAM_STEP)
    v_hat = v / (1.0 - ADAM_B2 ** ADAM_STEP)
    delta = -ADAM_LR * (m_hat / (jnp.sqrt(v_hat) + ADAM_EPS) + ADAM_WD * w)
    return delta, m, v


def _sum_adamw(recv, w, m, v, name):
    lead = w.ndim - 2
    rows, cols = w.shape[-2:]
    tr = 128

    n_slots = recv.shape[0]

    def body(r_ref, w_ref, m_ref, v_ref, g_ref, d_ref, nm_ref, nv_ref):
        g = r_ref[0].astype(F32)
        for slot in range(1, n_slots):
            g = g + r_ref[slot].astype(F32)
        g_ref[...] = g
        d_ref[...], nm_ref[...], nv_ref[...] = _adamw(w_ref[...], g, m_ref[...], v_ref[...])

    blk = pl.BlockSpec((None,) * lead + (tr, cols), lambda i: (0,) * lead + (i, 0))
    return pl.pallas_call(
        body, name=name, grid=(rows // tr,),
        in_specs=[pl.BlockSpec((n_slots, tr, cols), lambda i: (0, i, 0)), blk, blk, blk],
        out_specs=[blk] * 4, out_shape=[_sds(w.shape, F32)] * 4,
        compiler_params=_params())(recv, w, m, v)


SLAB_ROWS = 16
SLOT = {"kv_norm_g": (8, 0, D), "norm_b_g": (9, 0, D), "b_forget": (10, 0, 16), "qnorm_a_g": (10, 128, HD),
        "knorm_a_g": (10, 256, HD), "knorm_b_g": (10, 384, HD), "qnorm_b_g": (10, 512, HD), "sinks": (10, 640, 16)}
SMALL = ["norm_a_g", "b_forget", "qnorm_a_g", "knorm_a_g", "kv_norm_g", "knorm_b_g", "norm_b_g", "qnorm_b_g", "sinks"]


LOSS_ROW = 11


def _pack_small(dg_a, dg_kv, dg_b, db_f, dgq_a, dgk_a, dgk_b, dgq_b, dsinks, lsum):
    def fold(ref):
        return ref[:, 0:HD] + ref[:, HD:2 * HD]

    def body(dga_ref, dgkv_ref, dgb_ref, dbf_ref, dgqa_ref, dgka_ref, dgkb_ref, dgqb_ref, dsk_ref, ls_ref, slab_ref):
        slab_ref[...] = jnp.zeros_like(slab_ref)
        for r in range(N_DEV):
            slab_ref[r:r + 1, 0:LANES] = dga_ref[:, LANES * r:LANES * (r + 1)]
        slab_ref[8:9, :] = dgkv_ref[...]
        slab_ref[9:10, :] = dgb_ref[...]
        slab_ref[10:11, 0:LANES] = dbf_ref[...]
        slab_ref[10:11, 128:128 + HD] = fold(dgqa_ref)
        slab_ref[10:11, 256:256 + HD] = fold(dgka_ref)
        slab_ref[10:11, 384:384 + HD] = fold(dgkb_ref)
        slab_ref[10:11, 512:512 + HD] = fold(dgqb_ref)
        slab_ref[10:11, 640:640 + LANES] = dsk_ref[...]
        slab_ref[LOSS_ROW:LOSS_ROW + 1, 0:LANES] = ls_ref[...]

    return pl.pallas_call(body, name="pack_small", out_shape=_sds((SLAB_ROWS, D), F32), compiler_params=_params())(
        dg_a, dg_kv, dg_b, db_f, dgq_a, dgk_a, dgk_b, dgq_b, dsinks, lsum)


def _small_adamw(recv, ws, ms, vs):
    k = len(SMALL)

    def body(*refs):
        r_ref = refs[0]
        w_refs, m_refs, v_refs = refs[1:1 + k], refs[1 + k:1 + 2 * k], refs[1 + 2 * k:1 + 3 * k]
        outs = refs[1 + 3 * k:1 + 7 * k]
        loss_ref, tot = refs[1 + 7 * k], refs[2 + 7 * k]
        g = r_ref[0]
        for dev in range(1, N_DEV):
            g = g + r_ref[dev]
        tot[...] = g
        loss_ref[...] = tot[LOSS_ROW:LOSS_ROW + 1, 0:LANES] * (0.5 / D)
        me = 4 * lax.axis_index("x") + 2 * lax.axis_index("y") + lax.axis_index("c")
        for p, name in enumerate(SMALL):
            if name == "norm_a_g":
                mine = lax.broadcasted_iota(jnp.int32, (N_DEV, LANES), 0) == me
                gp = jnp.sum(jnp.where(mine, tot[0:N_DEV, 0:LANES], 0.0), axis=0, keepdims=True)
            else:
                row, lo, width = SLOT[name]
                gp = tot[row:row + 1, lo:lo + width]
            d, nm, nv = _adamw(w_refs[p][...], gp, m_refs[p][...], v_refs[p][...])
            outs[p][...] = gp
            outs[k + p][...] = d
            outs[2 * k + p][...] = nm
            outs[3 * k + p][...] = nv

    shapes = [_sds(w.shape, F32) for w in ws]
    return pl.pallas_call(body, name="small_adamw", out_shape=shapes * 4 + [_sds((1, LANES), F32)],
                          scratch_shapes=[pltpu.VMEM((SLAB_ROWS, D), F32)],
                          compiler_params=_params())(recv, *ws, *ms, *vs)


def _rope_tables(positions):
    inv_freq = jnp.power(jnp.float32(ROPE_THETA), -jnp.arange(0, ROT, 2, dtype=F32) / ROT)
    ang = positions.astype(F32)[:, None] * inv_freq[None, :]
    cos, sin = jnp.cos(ang), jnp.sin(ang)
    c64 = jnp.concatenate([cos, cos, jnp.ones((S, HD - ROT), F32)], axis=-1)
    s64 = jnp.concatenate([-sin, sin, jnp.zeros((S, HD - ROT), F32)], axis=-1)
    return jnp.tile(c64, (1, 2)), jnp.tile(s64, (1, 2))


def _local_step(x, tgt, positions, g_a, wa, b_forget, gq_a, gk_a, g_kv, gk_b, g_b, gq_b, sinks,
                woa_s, wkv_s, wib_s, wob_s):
    nq = S // TQ
    cos2, sin2 = _rope_tables(positions)
    b_pad = jnp.pad(b_forget, ((0, 0), (0, LANES - N_HEADS)))

    u_a, proj, qn, kn, vb, ccol, cbc = _head_a(x, g_a, wa, gq_a, gk_a, b_pad)
    crow = ccol[:, :N_HEADS].T.reshape(N_HEADS, nq, 1, TQ)
    o_a, z_a, lse_a, woa_g, wkv_g, w_in_b, wob_g = _fox_fwd(
        qn, kn, vb, proj, crow, cbc,
        rider=[("gather_rows", woa_s), ("gather_rows", wkv_s), ("gather_cols", wib_s), ("gather_rows", wob_s)])
    w_out_a, w_kv, w_out_b = woa_g.reshape(D, D), wkv_g.reshape(D, 512), wob_g.reshape(D, D)
    h1 = _mm(z_a, w_out_a, "nn", 1024, 512, D, add=x, name="mm_out_a")
    u_kv, u_b, kv, pb, qb, ksh, vsh = _head_b(h1, g_kv, g_b, w_kv, w_in_b, gq_b, gk_b, cos2, sin2)
    sinks1 = sinks.reshape(N_HEADS)
    o_b, z_b, lse_b = _swa_fwd(qb, ksh, vsh, pb, sinks1)
    y = _mm(z_b, w_out_b, "nn", 1024, 512, D, add=h1, name="mm_out_b")
    dy, lsum = _loss_dy(y, tgt)
    dw_out_b = _mm(z_b, dy, "tn", 512, 512, S, out_dtype=BF16, name="mm_dw_out_b")
    dz_b = _mm(dy, w_out_b, "nt", 1024, 512, D, name="mm_dz_b")
    dq_b, dgate_b, dka, dkb, dva, dvb, dsinks = _swa_bwd(qb, ksh, vsh, dz_b, o_b, lse_b, pb, sinks1)
    dpb, dgq_b = _prep_b_bwd(dq_b, dgate_b, pb, gq_b, cos2, sin2)
    dkv, dgk_b = _prep_kv_bwd(dka, dkb, dva, dvb, kv, gk_b, cos2, sin2)
    dw_in_b = _mm(u_b, dpb, "tn", 512, 512, S, out_dtype=BF16, name="mm_dw_in_b")
    dw_kv = _mm(u_kv, dkv, "tn", 512, 512, S, out_dtype=BF16, name="mm_dw_kv")
    dh1, dg_b, dg_kv = _du_b_rms_bwd(dpb, w_in_b, dkv, w_kv, h1, g_b, g_kv, dy)
    dw_out_a = _mm(z_a, dh1, "tn", 512, 512, S, out_dtype=BF16, name="mm_dw_out_a")
    dz_a = _mm(dh1, w_out_a, "nt", 1024, 512, D, name="mm_dz_a")
    do_a, dgate_a, delta_a = _fox_bwd_pre(dz_a, proj, o_a)
    dk_a, dv_a, dcs, dq_a, drow, r_wob, r_wib, r_wkv, r_woa = _fox_bwd(
        qn, kn, vb, do_a, lse_a, delta_a, crow, cbc,
        rider=[("a2a_rows", dw_out_b), ("a2a_cols", dw_in_b), ("a2a_rows", dw_kv), ("a2a_rows", dw_out_a)])
    drow_col = jnp.pad(drow.reshape(N_HEADS, S).T, ((0, 0), (0, LANES - N_HEADS)))
    dproj, dgq_a, dgk_a, db_f = _prep_a_bwd(dq_a, dk_a, dv_a, dgate_a, drow_col, dcs, proj, b_pad, gq_a, gk_a)
    dwa = _mm(u_a, dproj, "tn", 1024, 256, S, out_dtype=BF16, name="mm_dw_in_a")
    dx, dg_a, r_wa = _du_a_rms_bwd(dproj, wa, x, g_a, dh1, rider=[("a2a_chips", _pair_reduce(_reshard_dwa(dwa)))])
    slab = _pack_small(dg_a, dg_kv, dg_b, db_f, dgq_a, dgk_a, dgk_b, dgq_b, dsinks, lsum)
    return dx, r_wa, r_woa, r_wkv, r_wib, r_wob, _gather_slab(slab)


def kernel(x, positions, norm_a_g, w_in_a, b_forget, qnorm_a_g, knorm_a_g, w_out_a, kv_norm_g, w_kv, knorm_b_g, norm_b_g, w_in_b, qnorm_b_g, sinks, w_out_b, loss_target, m_norm_a_g, m_w_in_a, m_b_forget, m_qnorm_a_g, m_knorm_a_g, m_w_out_a, m_kv_norm_g, m_w_kv, m_knorm_b_g, m_norm_b_g, m_w_in_b, m_qnorm_b_g, m_sinks, m_w_out_b, v_norm_a_g, v_w_in_a, v_b_forget, v_qnorm_a_g, v_knorm_a_g, v_w_out_a, v_kv_norm_g, v_w_kv, v_knorm_b_g, v_norm_b_g, v_w_in_b, v_qnorm_b_g, v_sinks, v_w_out_b):
    wa_g, ga_g, woa_s, wkv_s, wib_s, wob_s = _gather_first(w_in_a, w_out_a, w_kv, w_in_b, w_out_b, norm_a_g)
    dx, r_wa, r_woa, r_wkv, r_wib, r_wob, slab_g = _local_step(
        x[0], loss_target[0], positions, ga_g.reshape(1, D), _unshard_wa(wa_g), b_forget, qnorm_a_g, knorm_a_g,
        kv_norm_g.reshape(1, D), knorm_b_g.reshape(1, HD), norm_b_g, qnorm_b_g, sinks, woa_s, wkv_s, wib_s, wob_s)

    big = {}
    for name, recv, w, m, v in (
            ("w_in_a", r_wa, w_in_a, m_w_in_a, v_w_in_a), ("w_out_a", r_woa, w_out_a, m_w_out_a, v_w_out_a),
            ("w_kv", r_wkv, w_kv, m_w_kv, v_w_kv), ("w_in_b", r_wib, w_in_b, m_w_in_b, v_w_in_b),
            ("w_out_b", r_wob, w_out_b, m_w_out_b, v_w_out_b)):
        big[name] = _sum_adamw(recv, w, m, v, "adamw_" + name)

    r2 = lambda a: a.reshape(1, -1)
    small_w = dict(norm_a_g=norm_a_g, b_forget=b_forget, qnorm_a_g=qnorm_a_g, knorm_a_g=knorm_a_g,
                   kv_norm_g=kv_norm_g, knorm_b_g=knorm_b_g, norm_b_g=norm_b_g, qnorm_b_g=qnorm_b_g, sinks=sinks)
    small_m = dict(norm_a_g=m_norm_a_g, b_forget=m_b_forget, qnorm_a_g=m_qnorm_a_g, knorm_a_g=m_knorm_a_g,
                   kv_norm_g=m_kv_norm_g, knorm_b_g=m_knorm_b_g, norm_b_g=m_norm_b_g, qnorm_b_g=m_qnorm_b_g,
                   sinks=m_sinks)
    small_v = dict(norm_a_g=v_norm_a_g, b_forget=v_b_forget, qnorm_a_g=v_qnorm_a_g, knorm_a_g=v_knorm_a_g,
                   kv_norm_g=v_kv_norm_g, knorm_b_g=v_knorm_b_g, norm_b_g=v_norm_b_g, qnorm_b_g=v_qnorm_b_g,
                   sinks=v_sinks)
    res = _small_adamw(slab_g, [r2(small_w[n]) for n in SMALL], [r2(small_m[n]) for n in SMALL],
                       [r2(small_v[n]) for n in SMALL])
    k = len(SMALL)
    small = {n: [res[q * k + p].reshape(small_w[n].shape) for q in range(4)] for p, n in enumerate(SMALL)}
    loss = res[4 * k][0, 0]

    order = ["norm_a_g", "w_in_a", "b_forget", "qnorm_a_g", "knorm_a_g", "w_out_a", "kv_norm_g", "w_kv",
             "knorm_b_g", "norm_b_g", "w_in_b", "qnorm_b_g", "sinks", "w_out_b"]

    def leaf(n, q):
        return big[n][q] if n in big else small[n][q]

    outs = [loss, dx[None]]
    for q in range(4):
        outs.extend(leaf(n, q) for n in order)
    return tuple(outs)
```

```python
import jax
import jax.numpy as jnp
from jax import lax
from jax.experimental import pallas as pl
from jax.experimental.pallas import tpu as pltpu

F32, BF16 = jnp.float32, jnp.bfloat16

S = 2048
D = 1024
HD = 64
N_HEADS = 16
N_DEV = 8
NA = 4352
GOFF = 3072
FOFF = 4096
RAW_F = 3072
RAW_G = RAW_F + N_HEADS
NA_RAW = 4112
EPS = 1e-6
QSCALE = 0.125
ROPE_THETA = 500000.0
ROT = 16
WIN = 128
TQ = 256
TK = 256
KS = TQ // 2
HPS = 8
HW = HPS * HD
TM = 256
RB = 512
CB = 256
LANES = 128

ADAM_LR, ADAM_B1, ADAM_B2, ADAM_EPS, ADAM_WD, ADAM_STEP = 0.001, 0.9, 0.999, 1e-08, 0.01, 10

VMEM_LIMIT = 56 * 1024 * 1024


def _params():
    return pltpu.CompilerParams(vmem_limit_bytes=VMEM_LIMIT)


def _sds(shape, dtype):
    return jax.ShapeDtypeStruct(shape, dtype)


def _dot_nt(a, b):
    return lax.dot_general(a, b, (((1,), (1,)), ((), ())), preferred_element_type=F32)


def _dot_tn(a, b):
    return lax.dot_general(a, b, (((0,), (0,)), ((), ())), preferred_element_type=F32)


def _dot_nn(a, b):
    return lax.dot_general(a, b, (((1,), (0,)), ((), ())), preferred_element_type=F32)


def _sigmoid(g):
    return 1.0 / (1.0 + jnp.exp(-g))


def _lane_iota(shape):
    return lax.broadcasted_iota(jnp.int32, shape, len(shape) - 1)


def _flips(kind):
    return (2, 4, 6) if kind == "a2a_chips" else tuple(range(1, N_DEV))


def _send_view(kind, ref, dev):
    if kind in ("gather_rows", "gather_cols"):
        return ref
    if kind == "a2a_slots":
        return ref.at[dev]
    if kind == "a2a_chips":
        return ref.at[dev >> 1]
    if kind == "a2a_rows":
        rows = ref.shape[0] // N_DEV
        return ref.at[pl.ds(pl.multiple_of(dev * rows, rows), rows)]
    cols = ref.shape[1] // N_DEV
    return ref.at[:, pl.ds(pl.multiple_of(dev * cols, cols), cols)]


def _land_view(kind, ref, dev):
    if kind == "gather_cols":
        cols = ref.shape[1] // N_DEV
        return ref.at[:, pl.ds(pl.multiple_of(dev * cols, cols), cols)]
    if kind == "a2a_chips":
        return ref.at[dev >> 1]
    return ref.at[dev]


def _landing_sds(kind, arr):
    if kind == "gather_rows":
        return _sds((N_DEV,) + arr.shape, arr.dtype)
    if kind == "gather_cols":
        return _sds((arr.shape[0], N_DEV * arr.shape[1]), arr.dtype)
    if kind == "a2a_rows":
        return _sds((N_DEV, arr.shape[0] // N_DEV, arr.shape[1]), arr.dtype)
    if kind == "a2a_cols":
        return _sds((N_DEV, arr.shape[0], arr.shape[1] // N_DEV), arr.dtype)
    return _sds(arr.shape, arr.dtype)


def _exchange_sems(n_parts):
    n = n_parts * (N_DEV - 1)
    return [pltpu.SemaphoreType.DMA((n,)), pltpu.SemaphoreType.DMA((n,)), pltpu.SemaphoreType.DMA((n_parts,))]


def _exchange_ops(kinds, srcs, dsts, sems, start, wait):
    send_sems, recv_sems, local_sems = sems
    x, y, c = lax.axis_index("x"), lax.axis_index("y"), lax.axis_index("c")
    me = 4 * x + 2 * y + c

    def local(a):
        return pltpu.make_async_copy(_send_view(kinds[a], srcs[a], me), _land_view(kinds[a], dsts[a], me),
                                     local_sems.at[a])

    def remote(a, k, landing_dev):
        peer = (x ^ ((k >> 2) & 1), y ^ ((k >> 1) & 1), c ^ (k & 1))
        sem = a * (N_DEV - 1) + k - 1
        return pltpu.make_async_remote_copy(
            src_ref=_send_view(kinds[a], srcs[a], me ^ k), dst_ref=_land_view(kinds[a], dsts[a], landing_dev),
            send_sem=send_sems.at[sem], recv_sem=recv_sems.at[sem], device_id=peer,
            device_id_type=pl.DeviceIdType.MESH)

    pairs = [(a, k) for k in range(1, N_DEV) for a in range(len(kinds)) if k in _flips(kinds[a])]
    if start:
        for a in range(len(kinds)):
            local(a).start()
        for a, k in pairs:
            remote(a, k, me).start()
    if wait:
        for a, k in pairs:
            remote(a, k, me ^ k).wait_recv()
            remote(a, k, me).wait_send()
        for a in range(len(kinds)):
            local(a).wait()


def _gather_two_level(srcs, dsts, sems):
    send_sems, recv_sems, local_sems = sems
    x, y, c = lax.axis_index("x"), lax.axis_index("y"), lax.axis_index("c")
    me, sibling = (x, y, c), (x, y, 1 - c)
    chips = [(1 - x, y), (x, 1 - y), (1 - x, 1 - y)]

    def slot(ref, dev):
        return ref.at[4 * dev[0] + 2 * dev[1] + dev[2]]

    def copy(a, k, block, to, src=None):
        return pltpu.make_async_remote_copy(
            src_ref=slot(dsts[a], block) if src is None else src, dst_ref=slot(dsts[a], block),
            send_sem=send_sems.at[a * (N_DEV - 1) + k], recv_sem=recv_sems.at[a * (N_DEV - 1) + k],
            device_id=to, device_id_type=pl.DeviceIdType.MESH)

    parts = range(len(srcs))
    mine = [pltpu.make_async_copy(srcs[a], slot(dsts[a], me), local_sems.at[a]) for a in parts]
    first = [copy(a, 0, me, sibling, src=srcs[a]) for a in parts]
    first += [copy(a, 1 + j, me, (*chip, c), src=srcs[a]) for j, chip in enumerate(chips) for a in parts]
    for cp in mine + first:
        cp.start()
    passed = []
    for j, chip in enumerate(chips):
        for a in parts:
            copy(a, 1 + j, (*chip, c), me).wait_recv()
            fwd = copy(a, 4 + j, (*chip, c), sibling)
            fwd.start()
            passed.append(fwd)
    for a in parts:
        copy(a, 0, sibling, me).wait_recv()
        for j, chip in enumerate(chips):
            copy(a, 4 + j, (*chip, 1 - c), me).wait_recv()
    for cp in first + passed:
        cp.wait_send()
    for cp in mine:
        cp.wait()


def _call(body, *, name, args, in_specs, out_specs, out_shape, grid=(), scratch_shapes=(), aliases=None, rider=()):
    n_in, n_out, n_scr, n_r = len(in_specs), len(out_specs), len(scratch_shapes), len(rider)
    kinds = [kind for kind, _ in rider]

    def kernel_body(*refs):
        c_in, r_in = refs[:n_in], refs[n_in:n_in + n_r]
        c_out = refs[n_in + n_r:n_in + n_r + n_out]
        r_out = refs[n_in + n_r + n_out:n_in + 2 * n_r + n_out]
        rest = refs[n_in + 2 * n_r + n_out:]
        c_scr, sems = rest[:n_scr], rest[n_scr:]
        if n_r:
            assert grid, "a rider needs a gridded call"
            ids = [pl.program_id(ax) for ax in range(len(grid))]
            first, last = ids[0] == 0, ids[0] == grid[0] - 1
            for pid, size in zip(ids[1:], grid[1:]):
                first = first & (pid == 0)
                last = last & (pid == size - 1)
            pl.when(first)(lambda: _exchange_ops(kinds, r_in, r_out, sems, True, False))
        body(*c_in, *c_out, *c_scr)
        if n_r:
            pl.when(last)(lambda: _exchange_ops(kinds, r_in, r_out, sems, False, True))

    anyspec = pl.BlockSpec(memory_space=pl.ANY)
    params = pltpu.CompilerParams(vmem_limit_bytes=VMEM_LIMIT, has_side_effects=bool(n_r))
    outs = pl.pallas_call(
        kernel_body, name=name, grid=grid, in_specs=list(in_specs) + [anyspec] * n_r,
        out_specs=list(out_specs) + [anyspec] * n_r,
        out_shape=list(out_shape) + [_landing_sds(kind, arr) for kind, arr in rider],
        scratch_shapes=list(scratch_shapes) + (_exchange_sems(n_r) if n_r else []),
        input_output_aliases=aliases or {}, compiler_params=params)(*args, *[arr for _, arr in rider])
    return list(outs)


def _mm(a, b, mode, tm, tn, tk, out_dtype=F32, add=None, name="mm", rider=()):
    if mode == "nn":
        (m, k), n = a.shape, b.shape[1]
        a_spec = pl.BlockSpec((tm, tk), lambda i, j, kk: (i, kk))
        b_spec = pl.BlockSpec((tk, tn), lambda i, j, kk: (kk, j))
        dot = _dot_nn
    elif mode == "nt":
        (m, k), n = a.shape, b.shape[0]
        a_spec = pl.BlockSpec((tm, tk), lambda i, j, kk: (i, kk))
        b_spec = pl.BlockSpec((tn, tk), lambda i, j, kk: (j, kk))
        dot = _dot_nt
    else:
        (k, m), n = a.shape, b.shape[1]
        a_spec = pl.BlockSpec((tk, tm), lambda i, j, kk: (kk, i))
        b_spec = pl.BlockSpec((tk, tn), lambda i, j, kk: (kk, j))
        dot = _dot_tn
    assert m % tm == 0 and n % tn == 0 and k % tk == 0, (m, n, k, tm, tn, tk)
    nk = k // tk
    has_add = add is not None

    def body(*refs):
        if has_add:
            a_ref, b_ref, add_ref, o_ref, acc = refs
        else:
            a_ref, b_ref, o_ref, acc = refs
        p = dot(a_ref[...].astype(BF16), b_ref[...].astype(BF16))

        def finish(total):
            if has_add:
                total = add_ref[...] + total
            o_ref[...] = total.astype(out_dtype)

        if nk == 1:
            finish(p)
        else:
            kk = pl.program_id(2)

            @pl.when(kk == 0)
            def _():
                acc[...] = p

            @pl.when(kk > 0)
            def _():
                acc[...] += p

            @pl.when(kk == nk - 1)
            def _():
                finish(acc[...])

    in_specs = [a_spec, b_spec]
    args = [a, b]
    if has_add:
        in_specs.append(pl.BlockSpec((tm, tn), lambda i, j, kk: (i, j)))
        args.append(add)
    acc_shape = (tm, tn) if nk > 1 else (8, LANES)
    outs = _call(body, name=name, args=args, grid=(m // tm, n // tn, nk), in_specs=in_specs,
                 out_specs=[pl.BlockSpec((tm, tn), lambda i, j, kk: (i, j))], out_shape=[_sds((m, n), out_dtype)],
                 scratch_shapes=[pltpu.VMEM(acc_shape, F32)], rider=rider)
    return outs if rider else outs[0]


def _rms_rinv(x):
    return lax.rsqrt(jnp.mean(x * x, axis=-1, keepdims=True) + EPS)


def _rms_bwd_core(du, x, g):
    r = _rms_rinv(x)
    dug = du * g
    dx = r * (dug - x * ((r * r) * jnp.mean(dug * x, axis=-1, keepdims=True)))
    dg = jnp.sum(du * (x * r), axis=0, keepdims=True)
    return dx, dg


def _half_sum(v, lo_half):
    s0 = jnp.sum(jnp.where(lo_half, v, 0.0), axis=-1, keepdims=True)
    s1 = jnp.sum(jnp.where(lo_half, 0.0, v), axis=-1, keepdims=True)
    return jnp.where(lo_half, s0, s1)


def _head_rinv(x, lo_half):
    return lax.rsqrt(_half_sum(x * x, lo_half) * (1.0 / HD) + EPS)


def _head_norm_bwd(dn, x, g, lo_half):
    r = _head_rinv(x, lo_half)
    dng = dn * g
    dx = r * (dng - x * ((r * r) * (_half_sum(dng * x, lo_half) * (1.0 / HD))))
    dg = jnp.sum(dn * (x * r), axis=0, keepdims=True)
    return dx, dg


def _rope_swap(x, lane):
    l64 = lane & (HD - 1)
    return jnp.where(l64 < ROT // 2, pltpu.roll(x, LANES - ROT // 2, 1), pltpu.roll(x, ROT // 2, 1))


def _rope_fwd(x, cos, sin, lane):
    return x * cos + _rope_swap(x, lane) * sin


def _rope_bwd(dy, cos, sin, lane):
    return dy * cos + jnp.where((lane & (HD - 1)) < ROT, _rope_swap(dy * sin, lane), 0.0)


def _g2(g_ref):
    g = g_ref[...]
    return jnp.concatenate([g, g], axis=-1)


def _pairs(width):
    return [slice(LANES * c, LANES * (c + 1)) for c in range(width // LANES)]


def _pick_lane(block, lane, idx):
    return jnp.sum(jnp.where(lane == idx, block, 0.0), axis=-1, keepdims=True)


def _head_a(x, g, wa, gq, gk, b_pad):
    def body(x_ref, g_ref, w_ref, gq_ref, gk_ref, b_ref, u_ref, p_ref, qo_ref, ko_ref, vo_ref, c_ref, cbc_ref, carry):
        @pl.when(pl.program_id(0) == 0)
        def _():
            carry[...] = jnp.zeros_like(carry)

        xv = x_ref[...]
        u = ((xv * _rms_rinv(xv)) * g_ref[...]).astype(BF16)
        u_ref[...] = u
        for lo in range(0, NA, D):
            hi = min(lo + D, NA)
            p_ref[:, lo:hi] = _dot_nn(u, w_ref[:, lo:hi])
        lane = _lane_iota((TM, LANES))
        lo_half = lane < HD
        gq2, gk2 = _g2(gq_ref), _g2(gk_ref)
        for c in _pairs(D):
            q = p_ref[:, c]
            k = p_ref[:, D + c.start:D + c.stop]
            qo_ref[:, c] = (((q * _head_rinv(q, lo_half)) * gq2) * QSCALE).astype(BF16)
            ko_ref[:, c] = ((k * _head_rinv(k, lo_half)) * gk2).astype(BF16)
        vo_ref[...] = p_ref[:, 2 * D:3 * D].astype(BF16)

        z = p_ref[:, FOFF:FOFF + LANES] + b_ref[...]
        logf = jnp.minimum(z, 0.0) - jnp.log1p(jnp.exp(-jnp.abs(z)))
        r = lax.broadcasted_iota(jnp.int32, (TM, TM), 0)
        cc = lax.broadcasted_iota(jnp.int32, (TM, TM), 1)
        tri = (r >= cc).astype(F32)
        loc = jnp.dot(tri, logf, precision=lax.Precision.HIGHEST, preferred_element_type=F32) + carry[0:1, :]
        c_ref[...] = loc
        carry[0:1, :] = loc[TM - 1:TM, :]
        for h in range(N_HEADS):
            cbc_ref[:, LANES * h:LANES * (h + 1)] = jnp.broadcast_to(_pick_lane(loc, lane, h), (TM, LANES))

    row = lambda width: pl.BlockSpec((TM, width), lambda i: (i, 0))
    whole = lambda arr: pl.BlockSpec(arr.shape, lambda i: (0,) * arr.ndim)
    return pl.pallas_call(
        body, name="head_a", grid=(S // TM,),
        in_specs=[row(D), whole(g), whole(wa), whole(gq), whole(gk), whole(b_pad)],
        out_specs=[row(D), row(NA), row(D), row(D), row(D), row(LANES), row(N_HEADS * LANES)],
        out_shape=[_sds((S, D), BF16), _sds((S, NA), F32)] + [_sds((S, D), BF16)] * 3
        + [_sds((S, LANES), F32), _sds((S, N_HEADS * LANES), F32)],
        scratch_shapes=[pltpu.VMEM((8, LANES), F32)], compiler_params=_params())(x, g, wa, gq, gk, b_pad)


def _key_le_query(offset, keys=TK):
    r = lax.broadcasted_iota(jnp.int32, (keys, TQ), 0)
    c = lax.broadcasted_iota(jnp.int32, (keys, TQ), 1)
    return (r + offset) <= c


def _widen(tile):
    return jnp.concatenate([tile] * (TQ // LANES), axis=1)


def _fox_fwd(qn, kn, vb, proj, crow, cbc, rider=()):
    nq, per = S // TQ, TQ // TK

    def body(q_ref, k_ref, v_ref, g_ref, cq_ref, cbc_ref, o_ref, z_ref, lse_ref, st_s, pt_s):
        i = pl.program_id(1)
        qs = [q_ref[:, HD * hh:HD * (hh + 1)] for hh in range(HPS)]
        cqs = [cq_ref[hh, 0] for hh in range(HPS)]

        def scores(s, hh):
            off = pl.multiple_of(s * KS, KS)
            kj = k_ref[pl.ds(off, KS), HD * hh:HD * (hh + 1)]
            return (_dot_nt(kj, qs[hh]) + cqs[hh]) - _widen(cbc_ref[pl.ds(off, KS), LANES * hh:LANES * (hh + 1)])

        def values(s, hh, pt):
            off = pl.multiple_of(s * KS, KS)
            return _dot_tn(v_ref[pl.ds(off, KS), HD * hh:HD * (hh + 1)], pt)

        def step(s, slot, carries, mask=None, last=False):
            if not last:
                for hh in range(HPS):
                    st_s[1 - slot, hh] = scores(s + 1, hh)
            pvs = [values(jnp.maximum(s - 1, 0), hh, pt_s[1 - slot, hh]) for hh in range(HPS)]
            out = []
            for hh in range(HPS):
                m, l, acc = carries[hh]
                st = st_s[slot, hh]
                if mask is not None:
                    st = jnp.where(mask, st, -jnp.inf)
                m_new = jnp.maximum(m, jnp.max(st, axis=0, keepdims=True))
                pt = jnp.exp(st - m_new)
                alpha = jnp.exp(m - m_new)
                pt_s[slot, hh] = pt.astype(BF16)
                out.append((m_new, alpha * l + jnp.sum(pt, axis=0, keepdims=True), alpha * (acc + pvs[hh])))
            return tuple(out)

        for hh in range(HPS):
            st_s[0, hh] = scores(0, hh)
            pt_s[1, hh] = jnp.zeros((KS, TQ), BF16)
        one = (jnp.full((1, TQ), -jnp.inf, F32), jnp.zeros((1, TQ), F32), jnp.zeros((HD, TQ), F32))
        carries = lax.fori_loop(0, i, lambda t, cr: step(2 * t + 1, 1, step(2 * t, 0, cr)), (one,) * HPS)
        carries = step(2 * i, 0, carries, mask=_key_le_query(0, KS))
        carries = step(2 * i + 1, 1, carries, mask=_key_le_query(KS, KS), last=True)
        accs = []
        for hh in range(HPS):
            m, l, acc = carries[hh]
            acc = acc + values(2 * i + 1, hh, pt_s[1, hh])
            accs.append(acc / l)
            lse_ref[hh, 0] = m + jnp.log(l)
        o = jnp.concatenate(accs, axis=0).T
        o_ref[...] = o
        g = g_ref[...]
        z_ref[...] = (o * (g * _sigmoid(g))).astype(BF16)

    qblk = pl.BlockSpec((TQ, HW), lambda hp, i: (i, hp))
    full = pl.BlockSpec((S, HW), lambda hp, i: (0, hp))
    rows = pl.BlockSpec((HPS, 1, 1, TQ), lambda hp, i: (hp, i, 0, 0))
    return _call(
        body, name="fox_fwd", args=(qn, kn, vb, proj, crow, cbc), grid=(N_HEADS // HPS, nq),
        in_specs=[qblk, full, full,
                  pl.BlockSpec((TQ, HW), lambda hp, i: (i, GOFF // HW + hp)),
                  rows, pl.BlockSpec((S, HPS * LANES), lambda hp, i: (0, hp))],
        out_specs=[qblk, qblk, rows],
        out_shape=[_sds((S, D), F32), _sds((S, D), BF16), _sds((N_HEADS, nq, 1, TQ), F32)],
        scratch_shapes=[pltpu.VMEM((2, HPS, KS, TQ), F32), pltpu.VMEM((2, HPS, KS, TQ), BF16)], rider=rider)


def _fox_bwd_pre(dh, w_out, proj, o):
    nq = S // TQ

    def body(dh_ref, w_ref, g_ref, o_ref, do_ref, dg_ref, delta_ref):
        g = g_ref[...]
        sg = _sigmoid(g)
        dzv = _dot_nt(dh_ref[...].astype(BF16), w_ref[...])
        ov = o_ref[...]
        do = dzv * (g * sg)
        dg_ref[...] = (dzv * ov * (sg * (1.0 + g * (1.0 - sg)))).astype(BF16)
        do_ref[...] = do.astype(BF16)
        prod_t = (do * ov).T
        for h in range(N_HEADS):
            delta_ref[h, 0] = jnp.sum(prod_t[HD * h:HD * (h + 1), :], axis=0, keepdims=True)

    row = pl.BlockSpec((TQ, D), lambda i: (i, 0))
    return pl.pallas_call(
        body, name="fox_bwd_pre", grid=(nq,),
        in_specs=[row, pl.BlockSpec(w_out.shape, lambda i: (0, 0)), pl.BlockSpec((TQ, D), lambda i: (i, GOFF // D)), row],
        out_specs=[row, row, pl.BlockSpec((N_HEADS, 1, 1, TQ), lambda i: (0, i, 0, 0))],
        out_shape=[_sds((S, D), BF16), _sds((S, D), BF16), _sds((N_HEADS, nq, 1, TQ), F32)],
        compiler_params=_params())(dh, w_out, proj, o)


def _fox_bwd(qn, kn, vb, dob, lse, delta, crow, cbc, rider=()):
    nq, nkb = S // TQ, S // TK

    def body(q_ref, k_ref, v_ref, do_ref, lse_ref, del_ref, cq_ref, cbc_ref,
             dk_ref, dv_ref, dcs_ref, dq_ref, dr_ref, st_s, dp_s, pt_s, ds_s, dq_acc, dr_acc):
        j = pl.program_id(1)

        @pl.when(j == 0)
        def _():
            dq_acc[...] = jnp.zeros_like(dq_acc)
            dr_acc[...] = jnp.zeros_like(dr_acc)

        kjs = [k_ref[:, HD * hh:HD * (hh + 1)] for hh in range(HPS)]
        vjs = [v_ref[:, HD * hh:HD * (hh + 1)] for hh in range(HPS)]

        def rows_of(ref, u, hh):
            off = pl.multiple_of(u * TQ, TQ)
            return ref[pl.ds(off, TQ), HD * hh:HD * (hh + 1)]

        def products(u, hh):
            st = (_dot_nt(kjs[hh], rows_of(q_ref, u, hh)) + cq_ref[hh, u]) - _widen(
                cbc_ref[:, LANES * hh:LANES * (hh + 1)])
            return st, _dot_nt(vjs[hh], rows_of(do_ref, u, hh))

        def step(u, slot, carries, masked=False):
            nxt = jnp.minimum(u + 1, nq - 1)
            for hh in range(HPS):
                st_s[1 - slot, hh], dp_s[1 - slot, hh] = products(nxt, hh)
            prev = jnp.maximum(u - 1, 0)
            dvs = [_dot_nn(pt_s[1 - slot, hh], rows_of(do_ref, prev, hh)) for hh in range(HPS)]
            dks = [_dot_nn(ds_s[1 - slot, hh], rows_of(q_ref, prev, hh)) for hh in range(HPS)]
            for hh in range(HPS):
                dq_acc[hh, prev] += _dot_tn(kjs[hh], ds_s[1 - slot, hh])
            out = []
            for hh in range(HPS):
                dk, dv, dcs = carries[hh]
                st = st_s[slot, hh]
                if masked:
                    st = jnp.where(_key_le_query((j - u) * TQ), st, -jnp.inf)
                pt = jnp.exp(st - lse_ref[hh, u])
                dst = pt * (dp_s[slot, hh] - del_ref[hh, u])
                pt_s[slot, hh] = pt.astype(BF16)
                ds_s[slot, hh] = dst.astype(BF16)
                dr_acc[hh, u] += jnp.sum(dst, axis=0, keepdims=True)
                out.append((dk + dks[hh], dv + dvs[hh], dcs + (dst[:, :LANES] + dst[:, LANES:])))
            return tuple(out)

        t0 = j // 2
        for hh in range(HPS):
            st_s[0, hh], dp_s[0, hh] = products(2 * t0, hh)
            pt_s[1, hh] = jnp.zeros((TK, TQ), BF16)
            ds_s[1, hh] = jnp.zeros((TK, TQ), BF16)
        one = (jnp.zeros((TK, HD), F32), jnp.zeros((TK, HD), F32), jnp.zeros((TK, LANES), F32))
        carries = step(2 * t0 + 1, 1, step(2 * t0, 0, (one,) * HPS, masked=True), masked=True)
        carries = lax.fori_loop(t0 + 1, nq // 2, lambda t, cr: step(2 * t + 1, 1, step(2 * t, 0, cr)), carries)
        dks, dvs = [], []
        for hh in range(HPS):
            dk, dv, dcs = carries[hh]
            dks.append(dk + _dot_nn(ds_s[1, hh], rows_of(q_ref, nq - 1, hh)))
            dvs.append(dv + _dot_nn(pt_s[1, hh], rows_of(do_ref, nq - 1, hh)))
            dq_acc[hh, nq - 1] += _dot_tn(kjs[hh], ds_s[1, hh])
            dcs_ref[:, LANES * hh:LANES * (hh + 1)] = jnp.broadcast_to(
                -jnp.sum(dcs, axis=-1, keepdims=True), (TK, LANES))
        dk_ref[...] = jnp.concatenate(dks, axis=-1)
        dv_ref[...] = jnp.concatenate(dvs, axis=-1)

        @pl.when(j == nkb - 1)
        def _():
            for i in range(nq):
                dq_ref[TQ * i:TQ * (i + 1), :] = jnp.concatenate([dq_acc[hh, i] for hh in range(HPS)], axis=0).T
            dr_ref[...] = dr_acc[...]

    kblk = pl.BlockSpec((TK, HW), lambda hp, j: (j, hp))
    full = pl.BlockSpec((S, HW), lambda hp, j: (0, hp))
    rows = pl.BlockSpec((HPS, nq, 1, TQ), lambda hp, j: (hp, 0, 0, 0))
    cblk = pl.BlockSpec((TK, HPS * LANES), lambda hp, j: (j, hp))
    return _call(
        body, name="fox_bwd", args=(qn, kn, vb, dob, lse, delta, crow, cbc), grid=(N_HEADS // HPS, nkb),
        in_specs=[full, kblk, kblk, full, rows, rows, rows, cblk],
        out_specs=[kblk, kblk, cblk, full, rows],
        out_shape=[_sds((S, D), F32), _sds((S, D), F32), _sds((S, N_HEADS * LANES), F32), _sds((S, D), F32),
                   _sds((N_HEADS, nq, 1, TQ), F32)],
        scratch_shapes=[pltpu.VMEM((2, HPS, TK, TQ), F32), pltpu.VMEM((2, HPS, TK, TQ), F32),
                        pltpu.VMEM((2, HPS, TK, TQ), BF16), pltpu.VMEM((2, HPS, TK, TQ), BF16),
                        pltpu.VMEM((HPS, nq, HD, TQ), F32), pltpu.VMEM((HPS, nq, 1, TQ), F32)], rider=rider)


def _prep_a_bwd(dq, dk, dv, dgate, drow, dcs, proj, b_pad, gq, gk):
    nt = S // TM

    def body(dq_ref, dk_ref, dv_ref, dgt_ref, dr_ref, dcs_ref, xq_ref, xk_ref, f_ref, b_ref, gq_ref, gk_ref,
             o_ref, dgq_ref, dgk_ref, db_ref, carry):
        @pl.when(pl.program_id(0) == 0)
        def _():
            carry[...] = jnp.zeros_like(carry)
            dgq_ref[...] = jnp.zeros_like(dgq_ref)
            dgk_ref[...] = jnp.zeros_like(dgk_ref)
            db_ref[...] = jnp.zeros_like(db_ref)

        lane = _lane_iota((TM, LANES))
        lo_half = lane < HD
        gq2, gk2 = _g2(gq_ref), _g2(gk_ref)
        dgq, dgk = jnp.zeros((1, LANES), F32), jnp.zeros((1, LANES), F32)
        for c in _pairs(D):
            dxq, dg = _head_norm_bwd(dq_ref[:, c] * QSCALE, xq_ref[:, c], gq2, lo_half)
            o_ref[:, c] = dxq.astype(BF16)
            dgq = dgq + dg
            dxk, dg = _head_norm_bwd(dk_ref[:, c], xk_ref[:, c], gk2, lo_half)
            o_ref[:, D + c.start:D + c.stop] = dxk.astype(BF16)
            dgk = dgk + dg
        dgq_ref[...] += dgq
        dgk_ref[...] += dgk
        o_ref[:, 2 * D:3 * D] = dv_ref[...].astype(BF16)
        o_ref[:, GOFF:GOFF + D] = dgt_ref[...]

        dc = dr_ref[...]
        for h in range(N_HEADS):
            dc = dc + jnp.where(lane == h, dcs_ref[:, LANES * h:LANES * h + 1], 0.0)
        r = lax.broadcasted_iota(jnp.int32, (TM, TM), 0)
        c = lax.broadcasted_iota(jnp.int32, (TM, TM), 1)
        tri = (c >= r).astype(F32)
        dlogf = jnp.dot(tri, dc, precision=lax.Precision.HIGHEST, preferred_element_type=F32) + carry[0:1, :]
        carry[0:1, :] = dlogf[0:1, :]
        df = dlogf * (1.0 / (1.0 + jnp.exp(f_ref[...] + b_ref[...])))
        db_ref[...] += jnp.sum(df, axis=0, keepdims=True)
        o_ref[:, FOFF:FOFF + LANES] = df.astype(BF16)
        o_ref[:, FOFF + LANES:NA] = jnp.zeros((TM, NA - FOFF - LANES), BF16)

    rev = lambda width, col: pl.BlockSpec((TM, width), lambda i: (nt - 1 - i, col))
    gspec = pl.BlockSpec((1, HD), lambda i: (0, 0))
    acc = pl.BlockSpec((1, LANES), lambda i: (0, 0))
    return pl.pallas_call(
        body, name="prep_a_bwd", grid=(nt,),
        in_specs=[rev(D, 0), rev(D, 0), rev(D, 0), rev(D, 0), rev(LANES, 0), rev(N_HEADS * LANES, 0),
                  rev(D, 0), rev(D, 1), rev(LANES, FOFF // LANES), acc, gspec, gspec],
        out_specs=[rev(NA, 0), acc, acc, acc],
        out_shape=[_sds((S, NA), BF16)] + [_sds((1, LANES), F32)] * 3,
        scratch_shapes=[pltpu.VMEM((8, LANES), F32)],
        compiler_params=_params())(dq, dk, dv, dgate, drow, dcs, proj, proj, proj, b_pad, gq, gk)


def _head_b(x, z_a, w_out_a, g_kv, g_b, w_kv, w_in_b, gq, gk, cos2, sin2):
    nkv = w_kv.shape[1] // 2

    def body(x_ref, z_ref, wo_ref, gkv_ref, gb_ref, wkv_ref, wb_ref, gq_ref, gk_ref, c_ref, s_ref,
             h_ref, ukv_ref, ub_ref, kv_ref, pb_ref, qo_ref, ko_ref, vo_ref):
        xv = x_ref[...] + _dot_nn(z_ref[...], wo_ref[...])
        h_ref[...] = xv
        xn = xv * _rms_rinv(xv)
        ukv = (xn * gkv_ref[...]).astype(BF16)
        ub = (xn * gb_ref[...]).astype(BF16)
        ukv_ref[...] = ukv
        ub_ref[...] = ub
        kv_ref[...] = _dot_nn(ukv, wkv_ref[...])
        for lo in range(0, 2 * D, D):
            pb_ref[:, lo:lo + D] = _dot_nn(ub, wb_ref[:, lo:lo + D])
        lane = _lane_iota((TM, LANES))
        lo_half = lane < HD
        cos, sin = c_ref[...], s_ref[...]
        gq2, gk2 = _g2(gq_ref), _g2(gk_ref)
        for c in _pairs(D):
            q = pb_ref[:, c]
            qo_ref[:, c] = (_rope_fwd((q * _head_rinv(q, lo_half)) * gq2, cos, sin, lane) * QSCALE).astype(BF16)
        for c in _pairs(nkv):
            k = kv_ref[:, c]
            ko_ref[:, c] = _rope_fwd((k * _head_rinv(k, lo_half)) * gk2, cos, sin, lane).astype(BF16)
        vo_ref[...] = kv_ref[:, nkv:2 * nkv].astype(BF16)

    row = lambda width: pl.BlockSpec((TM, width), lambda i: (i, 0))
    whole = lambda arr: pl.BlockSpec(arr.shape, lambda i: (0,) * arr.ndim)
    return pl.pallas_call(
        body, name="head_b", grid=(S // TM,),
        in_specs=[row(D), row(D), whole(w_out_a), whole(g_kv), whole(g_b), whole(w_kv), whole(w_in_b), whole(gq),
                  whole(gk), row(LANES), row(LANES)],
        out_specs=[row(D), row(D), row(D), row(2 * nkv), row(2 * D), row(D), row(nkv), row(nkv)],
        out_shape=[_sds((S, D), F32), _sds((S, D), BF16), _sds((S, D), BF16), _sds((S, 2 * nkv), F32),
                   _sds((S, 2 * D), F32), _sds((S, D), BF16), _sds((S, nkv), BF16), _sds((S, nkv), BF16)],
        compiler_params=_params())(x, z_a, w_out_a, g_kv, g_b, w_kv, w_in_b, gq, gk, cos2, sin2)


N_KV, GRP = 4, 4


def _swa_mask(n):
    r = lax.broadcasted_iota(jnp.int32, (2 * WIN, GRP * WIN), 0)
    q = lax.broadcasted_iota(jnp.int32, (2 * WIN, GRP * WIN), 1) & (WIN - 1)
    return (r > q) & (r <= q + WIN) & ((r >= WIN) | (n > 0))


def _stack4(ref_or_val, base):
    return jnp.concatenate([ref_or_val[:, base + HD * g: base + HD * (g + 1)] for g in range(GRP)], axis=0)


def _unstack4(xt):
    return jnp.concatenate([xt[:, WIN * g:WIN * (g + 1)] for g in range(GRP)], axis=0).T


def _band(prev_ref, cur_ref, kh):
    return jnp.concatenate([prev_ref[:, HD * kh:HD * (kh + 1)], cur_ref[:, HD * kh:HD * (kh + 1)]], axis=0)


def _sink_row(s_ref, first):
    lane = _lane_iota((1, GRP * WIN))
    row = jnp.full((1, GRP * WIN), s_ref[first + GRP - 1], F32)
    for g in range(GRP - 2, -1, -1):
        row = jnp.where(lane < WIN * (g + 1), s_ref[first + g], row)
    return row


def _swa_fwd(qb, ksh, vsh, pb, sinks):
    nb = S // WIN

    def body(q_ref, kp_ref, kc_ref, vp_ref, vc_ref, g_ref, s_ref, o_ref, z_ref, lse_ref):
        n = pl.program_id(0)
        valid = _swa_mask(n)
        outs = []
        for kh in range(N_KV):
            kb, vb = _band(kp_ref, kc_ref, kh), _band(vp_ref, vc_ref, kh)
            st = jnp.where(valid, _dot_nt(kb, _stack4(q_ref, GRP * HD * kh)), -jnp.inf)
            sink = _sink_row(s_ref, GRP * kh)
            m = jnp.maximum(jnp.max(st, axis=0, keepdims=True), sink)
            pt = jnp.exp(st - m)
            l = jnp.sum(pt, axis=0, keepdims=True) + jnp.exp(sink - m)
            outs.append(_unstack4(_dot_tn(vb, pt.astype(BF16)) / l))
            lse = m + jnp.log(l)
            for g in range(GRP):
                lse_ref[GRP * kh + g, 0] = lse[:, WIN * g:WIN * (g + 1)]
        o = jnp.concatenate(outs, axis=-1)
        o_ref[...] = o
        g = g_ref[...]
        z_ref[...] = (o * (g * _sigmoid(g))).astype(BF16)

    row = pl.BlockSpec((WIN, D), lambda n: (n, 0))
    prev = pl.BlockSpec((WIN, N_KV * HD), lambda n: (jnp.maximum(n - 1, 0), 0))
    cur = pl.BlockSpec((WIN, N_KV * HD), lambda n: (n, 0))
    return pl.pallas_call(
        body, name="swa_fwd", grid=(nb,),
        in_specs=[row, prev, cur, prev, cur, pl.BlockSpec((WIN, D), lambda n: (n, 1)),
                  pl.BlockSpec(memory_space=pltpu.SMEM)],
        out_specs=[row, row, pl.BlockSpec((N_HEADS, 1, 1, WIN), lambda n: (0, n, 0, 0))],
        out_shape=[_sds((S, D), F32), _sds((S, D), BF16), _sds((N_HEADS, nb, 1, WIN), F32)],
        compiler_params=_params())(qb, ksh, ksh, vsh, vsh, pb, sinks)


def _swa_bwd(qb, ksh, vsh, dz, o, lse, pb, sinks, gq, cos2, sin2):
    nb = S // WIN

    def body(q_ref, kp_ref, kc_ref, vp_ref, vc_ref, dz_ref, o_ref, lse_ref, x_ref, g_ref, s_ref, gq_ref, c_ref, sn_ref,
             dpb_ref, dka_ref, dkb_ref, dva_ref, dvb_ref, dsink_ref, dgq_ref):
        n = pl.program_id(0)

        @pl.when(n == 0)
        def _():
            dsink_ref[...] = jnp.zeros_like(dsink_ref)
            dgq_ref[...] = jnp.zeros_like(dgq_ref)

        valid = _swa_mask(n)
        g = g_ref[...]
        sg = _sigmoid(g)
        dzv = dz_ref[...]
        ov = o_ref[...]
        do = dzv * (g * sg)
        dpb_ref[:, D:2 * D] = (dzv * ov * (sg * (1.0 + g * (1.0 - sg)))).astype(BF16)
        prod_t = (do * ov).T
        lane1 = _lane_iota((1, LANES))
        dqs, dkas, dkbs, dvas, dvbs = [], [], [], [], []
        dsink = jnp.zeros((1, LANES), F32)
        for kh in range(N_KV):
            kb, vb = _band(kp_ref, kc_ref, kh), _band(vp_ref, vc_ref, kh)
            base = GRP * HD * kh
            qs = _stack4(q_ref, base)
            dos = _stack4(do, base).astype(BF16)
            delta = jnp.concatenate(
                [jnp.sum(prod_t[base + HD * gg:base + HD * (gg + 1), :], axis=0, keepdims=True)
                 for gg in range(GRP)], axis=1)
            lse = jnp.concatenate([lse_ref[GRP * kh + gg, 0] for gg in range(GRP)], axis=1)
            st = jnp.where(valid, _dot_nt(kb, qs), -jnp.inf)
            pt = jnp.exp(st - lse)
            dst = pt * (_dot_nt(vb, dos) - delta)
            dsb = dst.astype(BF16)
            dqs.append(_unstack4(_dot_tn(kb, dsb)))
            dkband = _dot_nn(dsb, qs)
            dvband = _dot_nn(pt.astype(BF16), dos)
            dkbs.append(dkband[0:WIN, :])
            dkas.append(dkband[WIN:2 * WIN, :])
            dvbs.append(dvband[0:WIN, :])
            dvas.append(dvband[WIN:2 * WIN, :])
            ps_delta = jnp.exp(_sink_row(s_ref, GRP * kh) - lse) * delta
            for gg in range(GRP):
                val = jnp.sum(ps_delta[:, WIN * gg:WIN * (gg + 1)], axis=1, keepdims=True)
                dsink = dsink - jnp.where(lane1 == GRP * kh + gg, val, 0.0)
        dka_ref[...] = jnp.concatenate(dkas, axis=-1)
        dkb_ref[...] = jnp.concatenate(dkbs, axis=-1)
        dva_ref[...] = jnp.concatenate(dvas, axis=-1)
        dvb_ref[...] = jnp.concatenate(dvbs, axis=-1)
        dsink_ref[...] += dsink

        lane = _lane_iota((WIN, LANES))
        g2, cos, sin = _g2(gq_ref), c_ref[...], sn_ref[...]
        dg_tot = jnp.zeros((1, LANES), F32)
        for kh in range(N_KV):
            for c in _pairs(GRP * HD):
                cols = slice(GRP * HD * kh + c.start, GRP * HD * kh + c.stop)
                dn = _rope_bwd(dqs[kh][:, c] * QSCALE, cos, sin, lane)
                dx, dg = _head_norm_bwd(dn, x_ref[:, cols], g2, lane < HD)
                dpb_ref[:, cols] = dx.astype(BF16)
                dg_tot = dg_tot + dg
        dgq_ref[...] += dg_tot

    row = pl.BlockSpec((WIN, D), lambda n: (n, 0))
    prev = pl.BlockSpec((WIN, N_KV * HD), lambda n: (jnp.maximum(n - 1, 0), 0))
    cur = pl.BlockSpec((WIN, N_KV * HD), lambda n: (n, 0))
    acc = pl.BlockSpec((1, LANES), lambda n: (0, 0))
    tab = pl.BlockSpec((WIN, LANES), lambda n: (n, 0))
    return pl.pallas_call(
        body, name="swa_bwd", grid=(nb,),
        in_specs=[row, prev, cur, prev, cur, row, row, pl.BlockSpec((N_HEADS, 1, 1, WIN), lambda n: (0, n, 0, 0)),
                  row, pl.BlockSpec((WIN, D), lambda n: (n, 1)), pl.BlockSpec(memory_space=pltpu.SMEM),
                  pl.BlockSpec((1, HD), lambda n: (0, 0)), tab, tab],
        out_specs=[pl.BlockSpec((WIN, 2 * D), lambda n: (n, 0)), cur, cur, cur, cur, acc, acc],
        out_shape=[_sds((S, 2 * D), BF16)] + [_sds((S, 256), F32)] * 4 + [_sds((1, LANES), F32)] * 2,
        compiler_params=_params())(qb, ksh, ksh, vsh, vsh, dz, o, lse, pb, pb, sinks, gq, cos2, sin2)


def _prep_kv_bwd(dka, dkb, dva, dvb, kv, gk, cos2, sin2):
    nt = S // RB
    per = RB // WIN

    def shifted(cur_ref, nxt_ref, has_next):
        return jnp.concatenate([cur_ref[WIN:RB, :], jnp.where(has_next, nxt_ref[...], 0.0)], axis=0)

    def body(dka_ref, dkb_ref, dkn_ref, dva_ref, dvb_ref, dvn_ref, x_ref, g_ref, c_ref, s_ref, o_ref, dgk_ref):
        i, j = pl.program_id(0), pl.program_id(1)

        @pl.when((i == 0) & (j == 0))
        def _():
            dgk_ref[...] = jnp.zeros_like(dgk_ref)

        has_next = i < nt - 1

        @pl.when(j == 0)
        def _():
            lane = _lane_iota((RB, LANES))
            g2, cos, sin = _g2(g_ref), c_ref[...], s_ref[...]
            dy_all = dka_ref[...] + shifted(dkb_ref, dkn_ref, has_next)
            dg_tot = jnp.zeros((1, LANES), F32)
            for c in _pairs(CB):
                dn = _rope_bwd(dy_all[:, c], cos, sin, lane)
                dx, dg = _head_norm_bwd(dn, x_ref[:, c], g2, lane < HD)
                o_ref[:, c] = dx.astype(BF16)
                dg_tot = dg_tot + dg
            dgk_ref[...] += dg_tot

        @pl.when(j == 1)
        def _():
            o_ref[...] = (dva_ref[...] + shifted(dvb_ref, dvn_ref, has_next)).astype(BF16)

    cur = pl.BlockSpec((RB, CB), lambda i, j: (i, 0))
    nxt = pl.BlockSpec((WIN, CB), lambda i, j: (jnp.minimum(per * (i + 1), S // WIN - 1), 0))
    tab = pl.BlockSpec((RB, LANES), lambda i, j: (i, 0))
    return pl.pallas_call(
        body, name="prep_kv_bwd", grid=(nt, 2),
        in_specs=[cur, cur, nxt, cur, cur, nxt, cur, pl.BlockSpec((1, HD), lambda i, j: (0, 0)), tab, tab],
        out_specs=[pl.BlockSpec((RB, CB), lambda i, j: (i, j)), pl.BlockSpec((1, LANES), lambda i, j: (0, 0))],
        out_shape=[_sds((S, 2 * CB), BF16), _sds((1, LANES), F32)],
        compiler_params=_params())(dka, dkb, dkb, dva, dvb, dvb, kv, gk, cos2, sin2)


def _out_b_loss(z, w_out, h1, tgt):
    tm = 512

    def body(z_ref, w_ref, h_ref, t_ref, dy_ref, l_ref):
        @pl.when(pl.program_id(0) == 0)
        def _():
            l_ref[...] = jnp.zeros_like(l_ref)

        e = (h_ref[...] + _dot_nn(z_ref[...], w_ref[...])) - t_ref[...]
        dy_ref[...] = e * (1.0 / D)
        l_ref[...] += jnp.sum(jnp.sum(e * e, axis=-1, keepdims=True), axis=0, keepdims=True)

    row = pl.BlockSpec((tm, D), lambda i: (i, 0))
    return pl.pallas_call(
        body, name="out_b_loss", grid=(S // tm,),
        in_specs=[row, pl.BlockSpec(w_out.shape, lambda i: (0, 0)), row, row],
        out_specs=[row, pl.BlockSpec((1, LANES), lambda i: (0, 0))],
        out_shape=[_sds((S, D), F32), _sds((1, LANES), F32)], compiler_params=_params())(z, w_out, h1, tgt)


def _du_a_rms_bwd(dproj, wa, x, g, dres, rider=()):
    tm, tk = 512, NA // 2
    nk = NA // tk

    def body(a_ref, b_ref, x_ref, g_ref, dr_ref, dx_ref, dg_ref, acc):
        i, kk = pl.program_id(0), pl.program_id(1)

        @pl.when((i == 0) & (kk == 0))
        def _():
            dg_ref[...] = jnp.zeros_like(dg_ref)

        p = _dot_nt(a_ref[...], b_ref[...])

        @pl.when(kk == 0)
        def _():
            acc[...] = p

        @pl.when(kk == nk - 1)
        def _():
            dx, dg = _rms_bwd_core(acc[...] + p, x_ref[...], g_ref[...])
            dx_ref[...] = dr_ref[...] + dx
            dg_ref[...] += dg

    assert nk == 2
    row = pl.BlockSpec((tm, D), lambda i, kk: (i, 0))
    vec = pl.BlockSpec((1, D), lambda i, kk: (0, 0))
    return _call(
        body, name="du_a_rms_bwd", args=(dproj, wa, x, g, dres), grid=(S // tm, nk),
        in_specs=[pl.BlockSpec((tm, tk), lambda i, kk: (i, kk)), pl.BlockSpec((D, tk), lambda i, kk: (0, kk)),
                  row, vec, row],
        out_specs=[row, vec], out_shape=[_sds((S, D), F32), _sds((1, D), F32)],
        scratch_shapes=[pltpu.VMEM((tm, D), F32)], rider=rider)


def _du_b_rms_bwd(dpb, w_in_b, dkv, w_kv, h1, g_b, g_kv, dy):
    tm = 512

    def body(ab_ref, wb_ref, akv_ref, wkv_ref, x_ref, gb_ref, gkv_ref, dy_ref, dh_ref, dgb_ref, dgkv_ref):
        @pl.when(pl.program_id(0) == 0)
        def _():
            dgb_ref[...] = jnp.zeros_like(dgb_ref)
            dgkv_ref[...] = jnp.zeros_like(dgkv_ref)

        x = x_ref[...]
        dx1, dg1 = _rms_bwd_core(_dot_nt(ab_ref[...], wb_ref[...]), x, gb_ref[...])
        dx2, dg2 = _rms_bwd_core(_dot_nt(akv_ref[...], wkv_ref[...]), x, gkv_ref[...])
        dh_ref[...] = dy_ref[...] + dx1 + dx2
        dgb_ref[...] += dg1
        dgkv_ref[...] += dg2

    row = lambda width: pl.BlockSpec((tm, width), lambda i: (i, 0))
    whole = lambda arr: pl.BlockSpec(arr.shape, lambda i: (0, 0))
    vec = pl.BlockSpec((1, D), lambda i: (0, 0))
    return pl.pallas_call(
        body, name="du_b_rms_bwd", grid=(S // tm,),
        in_specs=[row(dpb.shape[1]), whole(w_in_b), row(dkv.shape[1]), whole(w_kv), row(D), vec, vec, row(D)],
        out_specs=[row(D), vec, vec], out_shape=[_sds((S, D), F32), _sds((1, D), F32), _sds((1, D), F32)],
        compiler_params=_params())(dpb, w_in_b, dkv, w_kv, h1, g_b, g_kv, dy)


def _gather_first(w_in_a, w_out_a, w_kv, w_in_b, w_out_b, norm_a_g):
    def body(wia_ref, woa_ref, wkv_ref, wib_ref, wob_ref, ga_ref,
             wa_g, ga_g, woa_s, wkv_s, wib_s, wob_s, wa_s, *sems):
        wa_s[...] = wia_ref[0].astype(BF16)
        woa_s[...] = woa_ref[0].astype(BF16)
        wkv_s[...] = wkv_ref[...].astype(BF16)
        wib_s[...] = wib_ref[0].astype(BF16)
        wob_s[...] = wob_ref[0].astype(BF16)
        _gather_two_level([wa_s, ga_ref], [wa_g, ga_g], sems)

    vmem = pl.BlockSpec(memory_space=pltpu.VMEM)
    anyspec = pl.BlockSpec(memory_space=pl.ANY)
    shard = lambda w: _sds(w.shape[-2:], BF16)
    return pl.pallas_call(
        body, name="gather_first", in_specs=[vmem] * 6, out_specs=[anyspec, anyspec, vmem, vmem, vmem, vmem],
        out_shape=[_sds((N_DEV,) + w_in_a.shape[-2:], BF16), _sds((N_DEV,) + norm_a_g.shape, F32),
                   shard(w_out_a), shard(w_kv), shard(w_in_b), shard(w_out_b)],
        scratch_shapes=[pltpu.VMEM(w_in_a.shape[-2:], BF16)] + _exchange_sems(2),
        compiler_params=pltpu.CompilerParams(vmem_limit_bytes=VMEM_LIMIT, has_side_effects=True))(
            w_in_a, w_out_a, w_kv, w_in_b, w_out_b, norm_a_g)


def _pair_reduce(slots):
    n_chip = N_DEV // 2
    _, rows, cols = slots.shape

    def body(s_ref, o_ref, own_v, sib_v, send_sems, recv_sems, local_sems):
        x, y, c = lax.axis_index("x"), lax.axis_index("y"), lax.axis_index("c")
        copies = []
        for j in range(n_chip):
            own = pltpu.make_async_copy(s_ref.at[2 * j + c], own_v.at[j], local_sems.at[j])
            give = pltpu.make_async_remote_copy(
                src_ref=s_ref.at[2 * j + 1 - c], dst_ref=sib_v.at[j], send_sem=send_sems.at[j],
                recv_sem=recv_sems.at[j], device_id=(x, y, 1 - c), device_id_type=pl.DeviceIdType.MESH)
            own.start()
            give.start()
            copies.append((own, give))
        for j, (own, give) in enumerate(copies):
            own.wait()
            give.wait()
            o_ref[j] = (own_v[j].astype(F32) + sib_v[j].astype(F32)).astype(BF16)

    half = _sds((n_chip, rows, cols), slots.dtype)
    return pl.pallas_call(
        body, name="pair_reduce", in_specs=[pl.BlockSpec(memory_space=pl.ANY)],
        out_specs=pl.BlockSpec(memory_space=pltpu.VMEM), out_shape=half,
        scratch_shapes=[pltpu.VMEM(half.shape, half.dtype), pltpu.VMEM(half.shape, half.dtype),
                        pltpu.SemaphoreType.DMA((n_chip,)), pltpu.SemaphoreType.DMA((n_chip,)),
                        pltpu.SemaphoreType.DMA((n_chip,))],
        compiler_params=pltpu.CompilerParams(vmem_limit_bytes=VMEM_LIMIT, has_side_effects=True))(slots)


def _padded_col(c):
    if c < RAW_F:
        return c
    return FOFF + (c - RAW_F) if c < RAW_G else GOFF + (c - RAW_G)


def _shard_pieces():
    width = NA_RAW // N_DEV
    pieces = []
    for d in range(N_DEV):
        cuts = [width * d] + [c for c in (RAW_F, RAW_G) if width * d < c < width * (d + 1)] + [width * (d + 1)]
        for lo, hi in zip(cuts[:-1], cuts[1:]):
            pieces.append((d, lo - width * d, _padded_col(lo), hi - lo))
    return pieces


def _unshard_wa(wa_g):
    def body(w_ref, o_ref):
        o_ref[:, FOFF + N_HEADS:NA] = jnp.zeros((TM, NA - FOFF - N_HEADS), BF16)
        for d, src, dst, width in _shard_pieces():
            o_ref[:, dst:dst + width] = w_ref[d, :, src:src + width]

    return pl.pallas_call(
        body, name="unshard_wa", grid=(D // TM,),
        in_specs=[pl.BlockSpec((N_DEV, TM, NA_RAW // N_DEV), lambda i: (0, i, 0))],
        out_specs=pl.BlockSpec((TM, NA), lambda i: (i, 0)), out_shape=_sds((D, NA), BF16),
        compiler_params=_params())(wa_g)


def _reshard_dwa(dwa):
    def body(g_ref, o_ref):
        for d, src, dst, width in _shard_pieces():
            o_ref[d, :, src:src + width] = g_ref[:, dst:dst + width]

    return pl.pallas_call(
        body, name="reshard_dwa", grid=(D // TM,), in_specs=[pl.BlockSpec((TM, NA), lambda i: (i, 0))],
        out_specs=pl.BlockSpec((N_DEV, TM, NA_RAW // N_DEV), lambda i: (0, i, 0)),
        out_shape=_sds((N_DEV, D, NA_RAW // N_DEV), dwa.dtype), compiler_params=_params())(dwa)


def _gather_slab(slab):
    def body(s_ref, o_ref, *sems):
        _exchange_ops(["gather_rows"], [s_ref], [o_ref], sems, True, True)

    anyspec = pl.BlockSpec(memory_space=pl.ANY)
    return pl.pallas_call(
        body, name="gather_slab", in_specs=[anyspec], out_specs=anyspec,
        out_shape=_sds((N_DEV,) + slab.shape, slab.dtype), scratch_shapes=_exchange_sems(1),
        compiler_params=pltpu.CompilerParams(has_side_effects=True))(slab)


def _adamw(w, g, m, v):
    m = ADAM_B1 * m + (1.0 - ADAM_B1) * g
    v = ADAM_B2 * v + (1.0 - ADAM_B2) * (g * g)
    m_hat = m / (1.0 - ADAM_B1 ** ADAM_STEP)
    v_hat = v / (1.0 - ADAM_B2 ** ADAM_STEP)
    delta = -ADAM_LR * (m_hat / (jnp.sqrt(v_hat) + ADAM_EPS) + ADAM_WD * w)
    return delta, m, v


def _sum_adamw(recv, w, m, v, name):
    lead = w.ndim - 2
    rows, cols = w.shape[-2:]
    tr = 128

    n_slots = recv.shape[0]

    def body(r_ref, w_ref, m_ref, v_ref, g_ref, d_ref, nm_ref, nv_ref):
        g = r_ref[0].astype(F32)
        for slot in range(1, n_slots):
            g = g + r_ref[slot].astype(F32)
        g_ref[...] = g
        d_ref[...], nm_ref[...], nv_ref[...] = _adamw(w_ref[...], g, m_ref[...], v_ref[...])

    blk = pl.BlockSpec((None,) * lead + (tr, cols), lambda i: (0,) * lead + (i, 0))
    return pl.pallas_call(
        body, name=name, grid=(rows // tr,),
        in_specs=[pl.BlockSpec((n_slots, tr, cols), lambda i: (0, i, 0)), blk, blk, blk],
        out_specs=[blk] * 4, out_shape=[_sds(w.shape, F32)] * 4,
        compiler_params=_params())(recv, w, m, v)


SLAB_ROWS = 16
SLOT = {"kv_norm_g": (8, 0, D), "norm_b_g": (9, 0, D), "b_forget": (10, 0, 16), "qnorm_a_g": (10, 128, HD),
        "knorm_a_g": (10, 256, HD), "knorm_b_g": (10, 384, HD), "qnorm_b_g": (10, 512, HD), "sinks": (10, 640, 16)}
SMALL = ["norm_a_g", "b_forget", "qnorm_a_g", "knorm_a_g", "kv_norm_g", "knorm_b_g", "norm_b_g", "qnorm_b_g", "sinks"]


LOSS_ROW = 11


def _pack_small(dg_a, dg_kv, dg_b, db_f, dgq_a, dgk_a, dgk_b, dgq_b, dsinks, lsum):
    def fold(ref):
        return ref[:, 0:HD] + ref[:, HD:2 * HD]

    def body(dga_ref, dgkv_ref, dgb_ref, dbf_ref, dgqa_ref, dgka_ref, dgkb_ref, dgqb_ref, dsk_ref, ls_ref, slab_ref):
        slab_ref[...] = jnp.zeros_like(slab_ref)
        for r in range(N_DEV):
            slab_ref[r:r + 1, 0:LANES] = dga_ref[:, LANES * r:LANES * (r + 1)]
        slab_ref[8:9, :] = dgkv_ref[...]
        slab_ref[9:10, :] = dgb_ref[...]
        slab_ref[10:11, 0:LANES] = dbf_ref[...]
        slab_ref[10:11, 128:128 + HD] = fold(dgqa_ref)
        slab_ref[10:11, 256:256 + HD] = fold(dgka_ref)
        slab_ref[10:11, 384:384 + HD] = fold(dgkb_ref)
        slab_ref[10:11, 512:512 + HD] = fold(dgqb_ref)
        slab_ref[10:11, 640:640 + LANES] = dsk_ref[...]
        slab_ref[LOSS_ROW:LOSS_ROW + 1, 0:LANES] = ls_ref[...]

    return pl.pallas_call(body, name="pack_small", out_shape=_sds((SLAB_ROWS, D), F32), compiler_params=_params())(
        dg_a, dg_kv, dg_b, db_f, dgq_a, dgk_a, dgk_b, dgq_b, dsinks, lsum)


def _small_adamw(recv, ws, ms, vs):
    k = len(SMALL)

    def body(*refs):
        r_ref = refs[0]
        w_refs, m_refs, v_refs = refs[1:1 + k], refs[1 + k:1 + 2 * k], refs[1 + 2 * k:1 + 3 * k]
        outs = refs[1 + 3 * k:1 + 7 * k]
        loss_ref, tot = refs[1 + 7 * k], refs[2 + 7 * k]
        g = r_ref[0]
        for dev in range(1, N_DEV):
            g = g + r_ref[dev]
        tot[...] = g
        loss_ref[...] = tot[LOSS_ROW:LOSS_ROW + 1, 0:LANES] * (0.5 / D)
        me = 4 * lax.axis_index("x") + 2 * lax.axis_index("y") + lax.axis_index("c")
        for p, name in enumerate(SMALL):
            if name == "norm_a_g":
                mine = lax.broadcasted_iota(jnp.int32, (N_DEV, LANES), 0) == me
                gp = jnp.sum(jnp.where(mine, tot[0:N_DEV, 0:LANES], 0.0), axis=0, keepdims=True)
            else:
                row, lo, width = SLOT[name]
                gp = tot[row:row + 1, lo:lo + width]
            d, nm, nv = _adamw(w_refs[p][...], gp, m_refs[p][...], v_refs[p][...])
            outs[p][...] = gp
            outs[k + p][...] = d
            outs[2 * k + p][...] = nm
            outs[3 * k + p][...] = nv

    shapes = [_sds(w.shape, F32) for w in ws]
    return pl.pallas_call(body, name="small_adamw", out_shape=shapes * 4 + [_sds((1, LANES), F32)],
                          scratch_shapes=[pltpu.VMEM((SLAB_ROWS, D), F32)],
                          compiler_params=_params())(recv, *ws, *ms, *vs)


def _rope_tables(positions):
    inv_freq = jnp.power(jnp.float32(ROPE_THETA), -jnp.arange(0, ROT, 2, dtype=F32) / ROT)
    ang = positions.astype(F32)[:, None] * inv_freq[None, :]
    cos, sin = jnp.cos(ang), jnp.sin(ang)
    c64 = jnp.concatenate([cos, cos, jnp.ones((S, HD - ROT), F32)], axis=-1)
    s64 = jnp.concatenate([-sin, sin, jnp.zeros((S, HD - ROT), F32)], axis=-1)
    return jnp.tile(c64, (1, 2)), jnp.tile(s64, (1, 2))


def _local_step(x, tgt, positions, g_a, wa, b_forget, gq_a, gk_a, g_kv, gk_b, g_b, gq_b, sinks,
                woa_s, wkv_s, wib_s, wob_s):
    nq = S // TQ
    cos2, sin2 = _rope_tables(positions)
    b_pad = jnp.pad(b_forget, ((0, 0), (0, LANES - N_HEADS)))

    u_a, proj, qn, kn, vb, ccol, cbc = _head_a(x, g_a, wa, gq_a, gk_a, b_pad)
    crow = ccol[:, :N_HEADS].T.reshape(N_HEADS, nq, 1, TQ)
    o_a, z_a, lse_a, woa_g, wkv_g, w_in_b, wob_g = _fox_fwd(
        qn, kn, vb, proj, crow, cbc,
        rider=[("gather_rows", woa_s), ("gather_rows", wkv_s), ("gather_cols", wib_s), ("gather_rows", wob_s)])
    w_out_a, w_kv, w_out_b = woa_g.reshape(D, D), wkv_g.reshape(D, 512), wob_g.reshape(D, D)
    h1, u_kv, u_b, kv, pb, qb, ksh, vsh = _head_b(x, z_a, w_out_a, g_kv, g_b, w_kv, w_in_b, gq_b, gk_b, cos2, sin2)
    sinks1 = sinks.reshape(N_HEADS)
    o_b, z_b, lse_b = _swa_fwd(qb, ksh, vsh, pb, sinks1)
    dy, lsum = _out_b_loss(z_b, w_out_b, h1, tgt)
    dw_out_b = _mm(z_b, dy, "tn", 512, 512, S, out_dtype=BF16, name="mm_dw_out_b")
    dz_b = _mm(dy, w_out_b, "nt", 1024, 512, D, name="mm_dz_b")
    dpb, dka, dkb, dva, dvb, dsinks, dgq_b = _swa_bwd(qb, ksh, vsh, dz_b, o_b, lse_b, pb, sinks1, gq_b, cos2, sin2)
    dkv, dgk_b = _prep_kv_bwd(dka, dkb, dva, dvb, kv, gk_b, cos2, sin2)
    dw_in_b = _mm(u_b, dpb, "tn", 512, 512, S, out_dtype=BF16, name="mm_dw_in_b")
    dw_kv = _mm(u_kv, dkv, "tn", 512, 512, S, out_dtype=BF16, name="mm_dw_kv")
    dh1, dg_b, dg_kv = _du_b_rms_bwd(dpb, w_in_b, dkv, w_kv, h1, g_b, g_kv, dy)
    dw_out_a = _mm(z_a, dh1, "tn", 512, 512, S, out_dtype=BF16, name="mm_dw_out_a")
    do_a, dgate_a, delta_a = _fox_bwd_pre(dh1, w_out_a, proj, o_a)
    dk_a, dv_a, dcs, dq_a, drow, r_wob, r_wib, r_wkv, r_woa = _fox_bwd(
        qn, kn, vb, do_a, lse_a, delta_a, crow, cbc,
        rider=[("a2a_rows", dw_out_b), ("a2a_cols", dw_in_b), ("a2a_rows", dw_kv), ("a2a_rows", dw_out_a)])
    drow_col = jnp.pad(drow.reshape(N_HEADS, S).T, ((0, 0), (0, LANES - N_HEADS)))
    dproj, dgq_a, dgk_a, db_f = _prep_a_bwd(dq_a, dk_a, dv_a, dgate_a, drow_col, dcs, proj, b_pad, gq_a, gk_a)
    dwa = _mm(u_a, dproj, "tn", 1024, 256, S, out_dtype=BF16, name="mm_dw_in_a")
    dx, dg_a, r_wa = _du_a_rms_bwd(dproj, wa, x, g_a, dh1, rider=[("a2a_chips", _pair_reduce(_reshard_dwa(dwa)))])
    slab = _pack_small(dg_a, dg_kv, dg_b, db_f, dgq_a, dgk_a, dgk_b, dgq_b, dsinks, lsum)
    return dx, r_wa, r_woa, r_wkv, r_wib, r_wob, _gather_slab(slab)


def kernel(x, positions, norm_a_g, w_in_a, b_forget, qnorm_a_g, knorm_a_g, w_out_a, kv_norm_g, w_kv, knorm_b_g, norm_b_g, w_in_b, qnorm_b_g, sinks, w_out_b, loss_target, m_norm_a_g, m_w_in_a, m_b_forget, m_qnorm_a_g, m_knorm_a_g, m_w_out_a, m_kv_norm_g, m_w_kv, m_knorm_b_g, m_norm_b_g, m_w_in_b, m_qnorm_b_g, m_sinks, m_w_out_b, v_norm_a_g, v_w_in_a, v_b_forget, v_qnorm_a_g, v_knorm_a_g, v_w_out_a, v_kv_norm_g, v_w_kv, v_knorm_b_g, v_norm_b_g, v_w_in_b, v_qnorm_b_g, v_sinks, v_w_out_b):
    wa_g, ga_g, woa_s, wkv_s, wib_s, wob_s = _gather_first(w_in_a, w_out_a, w_kv, w_in_b, w_out_b, norm_a_g)
    dx, r_wa, r_woa, r_wkv, r_wib, r_wob, slab_g = _local_step(
        x[0], loss_target[0], positions, ga_g.reshape(1, D), _unshard_wa(wa_g), b_forget, qnorm_a_g, knorm_a_g,
        kv_norm_g.reshape(1, D), knorm_b_g.reshape(1, HD), norm_b_g, qnorm_b_g, sinks, woa_s, wkv_s, wib_s, wob_s)

    big = {}
    for name, recv, w, m, v in (
            ("w_in_a", r_wa, w_in_a, m_w_in_a, v_w_in_a), ("w_out_a", r_woa, w_out_a, m_w_out_a, v_w_out_a),
            ("w_kv", r_wkv, w_kv, m_w_kv, v_w_kv), ("w_in_b", r_wib, w_in_b, m_w_in_b, v_w_in_b),
            ("w_out_b", r_wob, w_out_b, m_w_out_b, v_w_out_b)):
        big[name] = _sum_adamw(recv, w, m, v, "adamw_" + name)

    r2 = lambda a: a.reshape(1, -1)
    small_w = dict(norm_a_g=norm_a_g, b_forget=b_forget, qnorm_a_g=qnorm_a_g, knorm_a_g=knorm_a_g,
                   kv_norm_g=kv_norm_g, knorm_b_g=knorm_b_g, norm_b_g=norm_b_g, qnorm_b_g=qnorm_b_g, sinks=sinks)
    small_m = dict(norm_a_g=m_norm_a_g, b_forget=m_b_forget, qnorm_a_g=m_qnorm_a_g, knorm_a_g=m_knorm_a_g,
                   kv_norm_g=m_kv_norm_g, knorm_b_g=m_knorm_b_g, norm_b_g=m_norm_b_g, qnorm_b_g=m_qnorm_b_g,
                   sinks=m_sinks)
    small_v = dict(norm_a_g=v_norm_a_g, b_forget=v_b_forget, qnorm_a_g=v_qnorm_a_g, knorm_a_g=v_knorm_a_g,
                   kv_norm_g=v_kv_norm_g, knorm_b_g=v_knorm_b_g, norm_b_g=v_norm_b_g, qnorm_b_g=v_qnorm_b_g,
                   sinks=v_sinks)
    res = _small_adamw(slab_g, [r2(small_w[n]) for n in SMALL], [r2(small_m[n]) for n in SMALL],
                       [r2(small_v[n]) for n in SMALL])
    k = len(SMALL)
    small = {n: [res[q * k + p].reshape(small_w[n].shape) for q in range(4)] for p, n in enumerate(SMALL)}
    loss = res[4 * k][0, 0]

    order = ["norm_a_g", "w_in_a", "b_forget", "qnorm_a_g", "knorm_a_g", "w_out_a", "kv_norm_g", "w_kv",
             "knorm_b_g", "norm_b_g", "w_in_b", "qnorm_b_g", "sinks", "w_out_b"]

    def leaf(n, q):
        return big[n][q] if n in big else small[n][q]

    outs = [loss, dx[None]]
    for q in range(4):
        outs.extend(leaf(n, q) for n in order)
    return tuple(outs)
```

```python
import jax
import jax.numpy as jnp
from jax import lax
from jax.experimental import pallas as pl
from jax.experimental.pallas import tpu as pltpu

F32, BF16 = jnp.float32, jnp.bfloat16

S = 2048
D = 1024
HD = 64
N_HEADS = 16
N_DEV = 8
NA = 4352
GOFF = 3072
FOFF = 4096
RAW_F = 3072
RAW_G = RAW_F + N_HEADS
NA_RAW = 4112
EPS = 1e-6
QSCALE = 0.125
ROPE_THETA = 500000.0
ROT = 16
WIN = 128
TQ = 256
TK = 256
KS = TQ // 2
HPS = 8
HW = HPS * HD
TM = 256
RB = 512
CB = 256
LANES = 128

ADAM_LR, ADAM_B1, ADAM_B2, ADAM_EPS, ADAM_WD, ADAM_STEP = 0.001, 0.9, 0.999, 1e-08, 0.01, 10

VMEM_LIMIT = 56 * 1024 * 1024


def _params():
    return pltpu.CompilerParams(vmem_limit_bytes=VMEM_LIMIT)


def _sds(shape, dtype):
    return jax.ShapeDtypeStruct(shape, dtype)


def _dot_nt(a, b):
    return lax.dot_general(a, b, (((1,), (1,)), ((), ())), preferred_element_type=F32)


def _dot_tn(a, b):
    return lax.dot_general(a, b, (((0,), (0,)), ((), ())), preferred_element_type=F32)


def _dot_nn(a, b):
    return lax.dot_general(a, b, (((1,), (0,)), ((), ())), preferred_element_type=F32)


def _sigmoid(g):
    return 1.0 / (1.0 + jnp.exp(-g))


def _lane_iota(shape):
    return lax.broadcasted_iota(jnp.int32, shape, len(shape) - 1)


def _flips(kind):
    return (2, 4, 6) if kind == "a2a_chips" else tuple(range(1, N_DEV))


def _send_view(kind, ref, dev):
    if kind in ("gather_rows", "gather_cols"):
        return ref
    if kind == "a2a_slots":
        return ref.at[dev]
    if kind == "a2a_chips":
        return ref.at[dev >> 1]
    if kind == "a2a_rows":
        rows = ref.shape[0] // N_DEV
        return ref.at[pl.ds(pl.multiple_of(dev * rows, rows), rows)]
    cols = ref.shape[1] // N_DEV
    return ref.at[:, pl.ds(pl.multiple_of(dev * cols, cols), cols)]


def _land_view(kind, ref, dev):
    if kind == "gather_cols":
        cols = ref.shape[1] // N_DEV
        return ref.at[:, pl.ds(pl.multiple_of(dev * cols, cols), cols)]
    if kind == "a2a_chips":
        return ref.at[dev >> 1]
    return ref.at[dev]


def _landing_sds(kind, arr):
    if kind == "gather_rows":
        return _sds((N_DEV,) + arr.shape, arr.dtype)
    if kind == "gather_cols":
        return _sds((arr.shape[0], N_DEV * arr.shape[1]), arr.dtype)
    if kind == "a2a_rows":
        return _sds((N_DEV, arr.shape[0] // N_DEV, arr.shape[1]), arr.dtype)
    if kind == "a2a_cols":
        return _sds((N_DEV, arr.shape[0], arr.shape[1] // N_DEV), arr.dtype)
    return _sds(arr.shape, arr.dtype)


def _exchange_sems(n_parts):
    n = n_parts * (N_DEV - 1)
    return [pltpu.SemaphoreType.DMA((n,)), pltpu.SemaphoreType.DMA((n,)), pltpu.SemaphoreType.DMA((n_parts,))]


def _exchange_ops(kinds, srcs, dsts, sems, start, wait):
    send_sems, recv_sems, local_sems = sems
    x, y, c = lax.axis_index("x"), lax.axis_index("y"), lax.axis_index("c")
    me = 4 * x + 2 * y + c

    def local(a):
        return pltpu.make_async_copy(_send_view(kinds[a], srcs[a], me), _land_view(kinds[a], dsts[a], me),
                                     local_sems.at[a])

    def remote(a, k, landing_dev):
        peer = (x ^ ((k >> 2) & 1), y ^ ((k >> 1) & 1), c ^ (k & 1))
        sem = a * (N_DEV - 1) + k - 1
        return pltpu.make_async_remote_copy(
            src_ref=_send_view(kinds[a], srcs[a], me ^ k), dst_ref=_land_view(kinds[a], dsts[a], landing_dev),
            send_sem=send_sems.at[sem], recv_sem=recv_sems.at[sem], device_id=peer,
            device_id_type=pl.DeviceIdType.MESH)

    pairs = [(a, k) for k in range(1, N_DEV) for a in range(len(kinds)) if k in _flips(kinds[a])]
    if start:
        for a in range(len(kinds)):
            local(a).start()
        for a, k in pairs:
            remote(a, k, me).start()
    if wait:
        for a, k in pairs:
            remote(a, k, me ^ k).wait_recv()
            remote(a, k, me).wait_send()
        for a in range(len(kinds)):
            local(a).wait()


def _gather_two_level(srcs, dsts, sems):
    send_sems, recv_sems, local_sems = sems
    x, y, c = lax.axis_index("x"), lax.axis_index("y"), lax.axis_index("c")
    me, sibling = (x, y, c), (x, y, 1 - c)
    chips = [(1 - x, y), (x, 1 - y), (1 - x, 1 - y)]

    def slot(ref, dev):
        return ref.at[4 * dev[0] + 2 * dev[1] + dev[2]]

    def copy(a, k, block, to, src=None):
        return pltpu.make_async_remote_copy(
            src_ref=slot(dsts[a], block) if src is None else src, dst_ref=slot(dsts[a], block),
            send_sem=send_sems.at[a * (N_DEV - 1) + k], recv_sem=recv_sems.at[a * (N_DEV - 1) + k],
            device_id=to, device_id_type=pl.DeviceIdType.MESH)

    parts = range(len(srcs))
    mine = [pltpu.make_async_copy(srcs[a], slot(dsts[a], me), local_sems.at[a]) for a in parts]
    first = [copy(a, 0, me, sibling, src=srcs[a]) for a in parts]
    first += [copy(a, 1 + j, me, (*chip, c), src=srcs[a]) for j, chip in enumerate(chips) for a in parts]
    for cp in mine + first:
        cp.start()
    passed = []
    for j, chip in enumerate(chips):
        for a in parts:
            copy(a, 1 + j, (*chip, c), me).wait_recv()
            fwd = copy(a, 4 + j, (*chip, c), sibling)
            fwd.start()
            passed.append(fwd)
    for a in parts:
        copy(a, 0, sibling, me).wait_recv()
        for j, chip in enumerate(chips):
            copy(a, 4 + j, (*chip, 1 - c), me).wait_recv()
    for cp in first + passed:
        cp.wait_send()
    for cp in mine:
        cp.wait()


def _call(body, *, name, args, in_specs, out_specs, out_shape, grid=(), scratch_shapes=(), aliases=None, rider=()):
    n_in, n_out, n_scr, n_r = len(in_specs), len(out_specs), len(scratch_shapes), len(rider)
    kinds = [kind for kind, _ in rider]

    def kernel_body(*refs):
        c_in, r_in = refs[:n_in], refs[n_in:n_in + n_r]
        c_out = refs[n_in + n_r:n_in + n_r + n_out]
        r_out = refs[n_in + n_r + n_out:n_in + 2 * n_r + n_out]
        rest = refs[n_in + 2 * n_r + n_out:]
        c_scr, sems = rest[:n_scr], rest[n_scr:]
        if n_r:
            assert grid, "a rider needs a gridded call"
            ids = [pl.program_id(ax) for ax in range(len(grid))]
            first, last = ids[0] == 0, ids[0] == grid[0] - 1
            for pid, size in zip(ids[1:], grid[1:]):
                first = first & (pid == 0)
                last = last & (pid == size - 1)
            pl.when(first)(lambda: _exchange_ops(kinds, r_in, r_out, sems, True, False))
        body(*c_in, *c_out, *c_scr)
        if n_r:
            pl.when(last)(lambda: _exchange_ops(kinds, r_in, r_out, sems, False, True))

    anyspec = pl.BlockSpec(memory_space=pl.ANY)
    params = pltpu.CompilerParams(vmem_limit_bytes=VMEM_LIMIT, has_side_effects=bool(n_r))
    outs = pl.pallas_call(
        kernel_body, name=name, grid=grid, in_specs=list(in_specs) + [anyspec] * n_r,
        out_specs=list(out_specs) + [anyspec] * n_r,
        out_shape=list(out_shape) + [_landing_sds(kind, arr) for kind, arr in rider],
        scratch_shapes=list(scratch_shapes) + (_exchange_sems(n_r) if n_r else []),
        input_output_aliases=aliases or {}, compiler_params=params)(*args, *[arr for _, arr in rider])
    return list(outs)


def _mm(a, b, mode, tm, tn, tk, out_dtype=F32, add=None, name="mm", rider=()):
    if mode == "nn":
        (m, k), n = a.shape, b.shape[1]
        a_spec = pl.BlockSpec((tm, tk), lambda i, j, kk: (i, kk))
        b_spec = pl.BlockSpec((tk, tn), lambda i, j, kk: (kk, j))
        dot = _dot_nn
    elif mode == "nt":
        (m, k), n = a.shape, b.shape[0]
        a_spec = pl.BlockSpec((tm, tk), lambda i, j, kk: (i, kk))
        b_spec = pl.BlockSpec((tn, tk), lambda i, j, kk: (j, kk))
        dot = _dot_nt
    else:
        (k, m), n = a.shape, b.shape[1]
        a_spec = pl.BlockSpec((tk, tm), lambda i, j, kk: (kk, i))
        b_spec = pl.BlockSpec((tk, tn), lambda i, j, kk: (kk, j))
        dot = _dot_tn
    assert m % tm == 0 and n % tn == 0 and k % tk == 0, (m, n, k, tm, tn, tk)
    nk = k // tk
    has_add = add is not None

    def body(*refs):
        if has_add:
            a_ref, b_ref, add_ref, o_ref, acc = refs
        else:
            a_ref, b_ref, o_ref, acc = refs
        p = dot(a_ref[...].astype(BF16), b_ref[...].astype(BF16))

        def finish(total):
            if has_add:
                total = add_ref[...] + total
            o_ref[...] = total.astype(out_dtype)

        if nk == 1:
            finish(p)
        else:
            kk = pl.program_id(2)

            @pl.when(kk == 0)
            def _():
                acc[...] = p

            @pl.when(kk > 0)
            def _():
                acc[...] += p

            @pl.when(kk == nk - 1)
            def _():
                finish(acc[...])

    in_specs = [a_spec, b_spec]
    args = [a, b]
    if has_add:
        in_specs.append(pl.BlockSpec((tm, tn), lambda i, j, kk: (i, j)))
        args.append(add)
    acc_shape = (tm, tn) if nk > 1 else (8, LANES)
    outs = _call(body, name=name, args=args, grid=(m // tm, n // tn, nk), in_specs=in_specs,
                 out_specs=[pl.BlockSpec((tm, tn), lambda i, j, kk: (i, j))], out_shape=[_sds((m, n), out_dtype)],
                 scratch_shapes=[pltpu.VMEM(acc_shape, F32)], rider=rider)
    return outs if rider else outs[0]


def _rms_rinv(x):
    return lax.rsqrt(jnp.mean(x * x, axis=-1, keepdims=True) + EPS)


def _rms_bwd_core(du, x, g):
    r = _rms_rinv(x)
    dug = du * g
    dx = r * (dug - x * ((r * r) * jnp.mean(dug * x, axis=-1, keepdims=True)))
    dg = jnp.sum(du * (x * r), axis=0, keepdims=True)
    return dx, dg


def _half_sum(v, lo_half):
    s0 = jnp.sum(jnp.where(lo_half, v, 0.0), axis=-1, keepdims=True)
    s1 = jnp.sum(jnp.where(lo_half, 0.0, v), axis=-1, keepdims=True)
    return jnp.where(lo_half, s0, s1)


def _head_rinv(x, lo_half):
    return lax.rsqrt(_half_sum(x * x, lo_half) * (1.0 / HD) + EPS)


def _head_norm_bwd(dn, x, g, lo_half):
    r = _head_rinv(x, lo_half)
    dng = dn * g
    dx = r * (dng - x * ((r * r) * (_half_sum(dng * x, lo_half) * (1.0 / HD))))
    dg = jnp.sum(dn * (x * r), axis=0, keepdims=True)
    return dx, dg


def _rope_swap(x, lane):
    l64 = lane & (HD - 1)
    return jnp.where(l64 < ROT // 2, pltpu.roll(x, LANES - ROT // 2, 1), pltpu.roll(x, ROT // 2, 1))


def _rope_fwd(x, cos, sin, lane):
    return x * cos + _rope_swap(x, lane) * sin


def _rope_bwd(dy, cos, sin, lane):
    return dy * cos + jnp.where((lane & (HD - 1)) < ROT, _rope_swap(dy * sin, lane), 0.0)


def _g2(g_ref):
    g = g_ref[...]
    return jnp.concatenate([g, g], axis=-1)


def _pairs(width):
    return [slice(LANES * c, LANES * (c + 1)) for c in range(width // LANES)]


def _pick_lane(block, lane, idx):
    return jnp.sum(jnp.where(lane == idx, block, 0.0), axis=-1, keepdims=True)


def _head_a(x, g, wa, gq, gk, b_pad):
    def body(x_ref, g_ref, w_ref, gq_ref, gk_ref, b_ref, u_ref, p_ref, qo_ref, ko_ref, vo_ref, c_ref, cbc_ref, carry):
        @pl.when(pl.program_id(0) == 0)
        def _():
            carry[...] = jnp.zeros_like(carry)

        xv = x_ref[...]
        u = ((xv * _rms_rinv(xv)) * g_ref[...]).astype(BF16)
        u_ref[...] = u
        for lo in range(0, NA, D):
            hi = min(lo + D, NA)
            p_ref[:, lo:hi] = _dot_nn(u, w_ref[:, lo:hi])
        lane = _lane_iota((TM, LANES))
        lo_half = lane < HD
        gq2, gk2 = _g2(gq_ref), _g2(gk_ref)
        for c in _pairs(D):
            q = p_ref[:, c]
            k = p_ref[:, D + c.start:D + c.stop]
            qo_ref[:, c] = (((q * _head_rinv(q, lo_half)) * gq2) * QSCALE).astype(BF16)
            ko_ref[:, c] = ((k * _head_rinv(k, lo_half)) * gk2).astype(BF16)
        vo_ref[...] = p_ref[:, 2 * D:3 * D].astype(BF16)

        z = p_ref[:, FOFF:FOFF + LANES] + b_ref[...]
        logf = jnp.minimum(z, 0.0) - jnp.log1p(jnp.exp(-jnp.abs(z)))
        r = lax.broadcasted_iota(jnp.int32, (TM, TM), 0)
        cc = lax.broadcasted_iota(jnp.int32, (TM, TM), 1)
        tri = (r >= cc).astype(F32)
        loc = jnp.dot(tri, logf, precision=lax.Precision.HIGHEST, preferred_element_type=F32) + carry[0:1, :]
        c_ref[...] = loc
        carry[0:1, :] = loc[TM - 1:TM, :]
        for h in range(N_HEADS):
            cbc_ref[:, LANES * h:LANES * (h + 1)] = jnp.broadcast_to(_pick_lane(loc, lane, h), (TM, LANES))

    row = lambda width: pl.BlockSpec((TM, width), lambda i: (i, 0))
    whole = lambda arr: pl.BlockSpec(arr.shape, lambda i: (0,) * arr.ndim)
    return pl.pallas_call(
        body, name="head_a", grid=(S // TM,),
        in_specs=[row(D), whole(g), whole(wa), whole(gq), whole(gk), whole(b_pad)],
        out_specs=[row(D), row(NA), row(D), row(D), row(D), row(LANES), row(N_HEADS * LANES)],
        out_shape=[_sds((S, D), BF16), _sds((S, NA), F32)] + [_sds((S, D), BF16)] * 3
        + [_sds((S, LANES), F32), _sds((S, N_HEADS * LANES), F32)],
        scratch_shapes=[pltpu.VMEM((8, LANES), F32)], compiler_params=_params())(x, g, wa, gq, gk, b_pad)


def _key_le_query(offset, keys=TK):
    r = lax.broadcasted_iota(jnp.int32, (keys, TQ), 0)
    c = lax.broadcasted_iota(jnp.int32, (keys, TQ), 1)
    return (r + offset) <= c


def _widen(tile):
    return jnp.concatenate([tile] * (TQ // LANES), axis=1)


def _fox_fwd(qn, kn, vb, proj, crow, cbc, rider=()):
    nq, per = S // TQ, TQ // TK

    def body(q_ref, k_ref, v_ref, g_ref, cq_ref, cbc_ref, o_ref, z_ref, lse_ref, st_s, pt_s):
        i = pl.program_id(1)
        qs = [q_ref[:, HD * hh:HD * (hh + 1)] for hh in range(HPS)]
        cqs = [cq_ref[hh, 0] for hh in range(HPS)]

        def scores(s, hh):
            off = pl.multiple_of(s * KS, KS)
            kj = k_ref[pl.ds(off, KS), HD * hh:HD * (hh + 1)]
            return (_dot_nt(kj, qs[hh]) + cqs[hh]) - _widen(cbc_ref[pl.ds(off, KS), LANES * hh:LANES * (hh + 1)])

        def values(s, hh, pt):
            off = pl.multiple_of(s * KS, KS)
            return _dot_tn(v_ref[pl.ds(off, KS), HD * hh:HD * (hh + 1)], pt)

        def step(s, slot, carries, mask=None, last=False):
            if not last:
                for hh in range(HPS):
                    st_s[1 - slot, hh] = scores(s + 1, hh)
            pvs = [values(jnp.maximum(s - 1, 0), hh, pt_s[1 - slot, hh]) for hh in range(HPS)]
            out = []
            for hh in range(HPS):
                m, l, acc = carries[hh]
                st = st_s[slot, hh]
                if mask is not None:
                    st = jnp.where(mask, st, -jnp.inf)
                m_new = jnp.maximum(m, jnp.max(st, axis=0, keepdims=True))
                pt = jnp.exp(st - m_new)
                alpha = jnp.exp(m - m_new)
                pt_s[slot, hh] = pt.astype(BF16)
                out.append((m_new, alpha * l + jnp.sum(pt, axis=0, keepdims=True), alpha * (acc + pvs[hh])))
            return tuple(out)

        for hh in range(HPS):
            st_s[0, hh] = scores(0, hh)
            pt_s[1, hh] = jnp.zeros((KS, TQ), BF16)
        one = (jnp.full((1, TQ), -jnp.inf, F32), jnp.zeros((1, TQ), F32), jnp.zeros((HD, TQ), F32))
        carries = lax.fori_loop(0, i, lambda t, cr: step(2 * t + 1, 1, step(2 * t, 0, cr)), (one,) * HPS)
        carries = step(2 * i, 0, carries, mask=_key_le_query(0, KS))
        carries = step(2 * i + 1, 1, carries, mask=_key_le_query(KS, KS), last=True)
        accs = []
        for hh in range(HPS):
            m, l, acc = carries[hh]
            acc = acc + values(2 * i + 1, hh, pt_s[1, hh])
            accs.append(acc / l)
            lse_ref[hh, 0] = m + jnp.log(l)
        o = jnp.concatenate(accs, axis=0).T
        o_ref[...] = o
        g = g_ref[...]
        z_ref[...] = (o * (g * _sigmoid(g))).astype(BF16)

    qblk = pl.BlockSpec((TQ, HW), lambda hp, i: (i, hp))
    full = pl.BlockSpec((S, HW), lambda hp, i: (0, hp))
    rows = pl.BlockSpec((HPS, 1, 1, TQ), lambda hp, i: (hp, i, 0, 0))
    return _call(
        body, name="fox_fwd", args=(qn, kn, vb, proj, crow, cbc), grid=(N_HEADS // HPS, nq),
        in_specs=[qblk, full, full,
                  pl.BlockSpec((TQ, HW), lambda hp, i: (i, GOFF // HW + hp)),
                  rows, pl.BlockSpec((S, HPS * LANES), lambda hp, i: (0, hp))],
        out_specs=[qblk, qblk, rows],
        out_shape=[_sds((S, D), F32), _sds((S, D), BF16), _sds((N_HEADS, nq, 1, TQ), F32)],
        scratch_shapes=[pltpu.VMEM((2, HPS, KS, TQ), F32), pltpu.VMEM((2, HPS, KS, TQ), BF16)], rider=rider)


def _fox_bwd_pre(dh, w_out, proj, o):
    nq = S // TQ

    def body(dh_ref, w_ref, g_ref, o_ref, do_ref, dg_ref, delta_ref):
        g = g_ref[...]
        sg = _sigmoid(g)
        dzv = _dot_nt(dh_ref[...].astype(BF16), w_ref[...])
        ov = o_ref[...]
        do = dzv * (g * sg)
        dg_ref[...] = (dzv * ov * (sg * (1.0 + g * (1.0 - sg)))).astype(BF16)
        do_ref[...] = do.astype(BF16)
        prod_t = (do * ov).T
        for h in range(N_HEADS):
            delta_ref[h, 0] = jnp.sum(prod_t[HD * h:HD * (h + 1), :], axis=0, keepdims=True)

    row = pl.BlockSpec((TQ, D), lambda i: (i, 0))
    return pl.pallas_call(
        body, name="fox_bwd_pre", grid=(nq,),
        in_specs=[row, pl.BlockSpec(w_out.shape, lambda i: (0, 0)), pl.BlockSpec((TQ, D), lambda i: (i, GOFF // D)), row],
        out_specs=[row, row, pl.BlockSpec((N_HEADS, 1, 1, TQ), lambda i: (0, i, 0, 0))],
        out_shape=[_sds((S, D), BF16), _sds((S, D), BF16), _sds((N_HEADS, nq, 1, TQ), F32)],
        compiler_params=_params())(dh, w_out, proj, o)


def _fox_bwd(qn, kn, vb, dob, lse, delta, crow, cbc, rider=()):
    nq, nkb = S // TQ, S // TK

    def body(q_ref, k_ref, v_ref, do_ref, lse_ref, del_ref, cq_ref, cbc_ref,
             dk_ref, dv_ref, dcs_ref, dq_ref, dr_ref, st_s, dp_s, pt_s, ds_s, dq_acc, dr_acc):
        j = pl.program_id(1)

        @pl.when(j == 0)
        def _():
            dq_acc[...] = jnp.zeros_like(dq_acc)
            dr_acc[...] = jnp.zeros_like(dr_acc)

        kjs = [k_ref[:, HD * hh:HD * (hh + 1)] for hh in range(HPS)]
        vjs = [v_ref[:, HD * hh:HD * (hh + 1)] for hh in range(HPS)]

        def rows_of(ref, u, hh):
            off = pl.multiple_of(u * TQ, TQ)
            return ref[pl.ds(off, TQ), HD * hh:HD * (hh + 1)]

        def products(u, hh):
            st = (_dot_nt(kjs[hh], rows_of(q_ref, u, hh)) + cq_ref[hh, u]) - _widen(
                cbc_ref[:, LANES * hh:LANES * (hh + 1)])
            return st, _dot_nt(vjs[hh], rows_of(do_ref, u, hh))

        def step(u, slot, carries, masked=False):
            nxt = jnp.minimum(u + 1, nq - 1)
            for hh in range(HPS):
                st_s[1 - slot, hh], dp_s[1 - slot, hh] = products(nxt, hh)
            prev = jnp.maximum(u - 1, 0)
            dvs = [_dot_nn(pt_s[1 - slot, hh], rows_of(do_ref, prev, hh)) for hh in range(HPS)]
            dks = [_dot_nn(ds_s[1 - slot, hh], rows_of(q_ref, prev, hh)) for hh in range(HPS)]
            for hh in range(HPS):
                dq_acc[hh, prev] += _dot_tn(kjs[hh], ds_s[1 - slot, hh])
            out = []
            for hh in range(HPS):
                dk, dv, dcs = carries[hh]
                st = st_s[slot, hh]
                if masked:
                    st = jnp.where(_key_le_query((j - u) * TQ), st, -jnp.inf)
                pt = jnp.exp(st - lse_ref[hh, u])
                dst = pt * (dp_s[slot, hh] - del_ref[hh, u])
                pt_s[slot, hh] = pt.astype(BF16)
                ds_s[slot, hh] = dst.astype(BF16)
                dr_acc[hh, u] += jnp.sum(dst, axis=0, keepdims=True)
                out.append((dk + dks[hh], dv + dvs[hh], dcs + (dst[:, :LANES] + dst[:, LANES:])))
            return tuple(out)

        t0 = j // 2
        for hh in range(HPS):
            st_s[0, hh], dp_s[0, hh] = products(2 * t0, hh)
            pt_s[1, hh] = jnp.zeros((TK, TQ), BF16)
            ds_s[1, hh] = jnp.zeros((TK, TQ), BF16)
        one = (jnp.zeros((TK, HD), F32), jnp.zeros((TK, HD), F32), jnp.zeros((TK, LANES), F32))
        carries = step(2 * t0 + 1, 1, step(2 * t0, 0, (one,) * HPS, masked=True), masked=True)
        carries = lax.fori_loop(t0 + 1, nq // 2, lambda t, cr: step(2 * t + 1, 1, step(2 * t, 0, cr)), carries)
        dks, dvs = [], []
        for hh in range(HPS):
            dk, dv, dcs = carries[hh]
            dks.append(dk + _dot_nn(ds_s[1, hh], rows_of(q_ref, nq - 1, hh)))
            dvs.append(dv + _dot_nn(pt_s[1, hh], rows_of(do_ref, nq - 1, hh)))
            dq_acc[hh, nq - 1] += _dot_tn(kjs[hh], ds_s[1, hh])
            dcs_ref[:, LANES * hh:LANES * (hh + 1)] = jnp.broadcast_to(
                -jnp.sum(dcs, axis=-1, keepdims=True), (TK, LANES))
        dk_ref[...] = jnp.concatenate(dks, axis=-1)
        dv_ref[...] = jnp.concatenate(dvs, axis=-1)

        @pl.when(j == nkb - 1)
        def _():
            for i in range(nq):
                dq_ref[TQ * i:TQ * (i + 1), :] = jnp.concatenate([dq_acc[hh, i] for hh in range(HPS)], axis=0).T
            dr_ref[...] = dr_acc[...]

    kblk = pl.BlockSpec((TK, HW), lambda hp, j: (j, hp))
    full = pl.BlockSpec((S, HW), lambda hp, j: (0, hp))
    rows = pl.BlockSpec((HPS, nq, 1, TQ), lambda hp, j: (hp, 0, 0, 0))
    cblk = pl.BlockSpec((TK, HPS * LANES), lambda hp, j: (j, hp))
    return _call(
        body, name="fox_bwd", args=(qn, kn, vb, dob, lse, delta, crow, cbc), grid=(N_HEADS // HPS, nkb),
        in_specs=[full, kblk, kblk, full, rows, rows, rows, cblk],
        out_specs=[kblk, kblk, cblk, full, rows],
        out_shape=[_sds((S, D), F32), _sds((S, D), F32), _sds((S, N_HEADS * LANES), F32), _sds((S, D), F32),
                   _sds((N_HEADS, nq, 1, TQ), F32)],
        scratch_shapes=[pltpu.VMEM((2, HPS, TK, TQ), F32), pltpu.VMEM((2, HPS, TK, TQ), F32),
                        pltpu.VMEM((2, HPS, TK, TQ), BF16), pltpu.VMEM((2, HPS, TK, TQ), BF16),
                        pltpu.VMEM((HPS, nq, HD, TQ), F32), pltpu.VMEM((HPS, nq, 1, TQ), F32)], rider=rider)


def _prep_a_bwd(dq, dk, dv, dgate, drow, dcs, proj, b_pad, gq, gk):
    nt = S // TM

    def body(dq_ref, dk_ref, dv_ref, dgt_ref, dr_ref, dcs_ref, xq_ref, xk_ref, f_ref, b_ref, gq_ref, gk_ref,
             o_ref, dgq_ref, dgk_ref, db_ref, carry):
        @pl.when(pl.program_id(0) == 0)
        def _():
            carry[...] = jnp.zeros_like(carry)
            dgq_ref[...] = jnp.zeros_like(dgq_ref)
            dgk_ref[...] = jnp.zeros_like(dgk_ref)
            db_ref[...] = jnp.zeros_like(db_ref)

        lane = _lane_iota((TM, LANES))
        lo_half = lane < HD
        gq2, gk2 = _g2(gq_ref), _g2(gk_ref)
        dgq, dgk = jnp.zeros((1, LANES), F32), jnp.zeros((1, LANES), F32)
        for c in _pairs(D):
            dxq, dg = _head_norm_bwd(dq_ref[:, c] * QSCALE, xq_ref[:, c], gq2, lo_half)
            o_ref[:, c] = dxq.astype(BF16)
            dgq = dgq + dg
            dxk, dg = _head_norm_bwd(dk_ref[:, c], xk_ref[:, c], gk2, lo_half)
            o_ref[:, D + c.start:D + c.stop] = dxk.astype(BF16)
            dgk = dgk + dg
        dgq_ref[...] += dgq
        dgk_ref[...] += dgk
        o_ref[:, 2 * D:3 * D] = dv_ref[...].astype(BF16)
        o_ref[:, GOFF:GOFF + D] = dgt_ref[...]

        dc = dr_ref[...]
        for h in range(N_HEADS):
            dc = dc + jnp.where(lane == h, dcs_ref[:, LANES * h:LANES * h + 1], 0.0)
        r = lax.broadcasted_iota(jnp.int32, (TM, TM), 0)
        c = lax.broadcasted_iota(jnp.int32, (TM, TM), 1)
        tri = (c >= r).astype(F32)
        dlogf = jnp.dot(tri, dc, precision=lax.Precision.HIGHEST, preferred_element_type=F32) + carry[0:1, :]
        carry[0:1, :] = dlogf[0:1, :]
        df = dlogf * (1.0 / (1.0 + jnp.exp(f_ref[...] + b_ref[...])))
        db_ref[...] += jnp.sum(df, axis=0, keepdims=True)
        o_ref[:, FOFF:FOFF + LANES] = df.astype(BF16)
        o_ref[:, FOFF + LANES:NA] = jnp.zeros((TM, NA - FOFF - LANES), BF16)

    rev = lambda width, col: pl.BlockSpec((TM, width), lambda i: (nt - 1 - i, col))
    gspec = pl.BlockSpec((1, HD), lambda i: (0, 0))
    acc = pl.BlockSpec((1, LANES), lambda i: (0, 0))
    return pl.pallas_call(
        body, name="prep_a_bwd", grid=(nt,),
        in_specs=[rev(D, 0), rev(D, 0), rev(D, 0), rev(D, 0), rev(LANES, 0), rev(N_HEADS * LANES, 0),
                  rev(D, 0), rev(D, 1), rev(LANES, FOFF // LANES), acc, gspec, gspec],
        out_specs=[rev(NA, 0), acc, acc, acc],
        out_shape=[_sds((S, NA), BF16)] + [_sds((1, LANES), F32)] * 3,
        scratch_shapes=[pltpu.VMEM((8, LANES), F32)],
        compiler_params=_params())(dq, dk, dv, dgate, drow, dcs, proj, proj, proj, b_pad, gq, gk)


def _head_b(x, z_a, w_out_a, g_kv, g_b, w_kv, w_in_b, gq, gk, cos2, sin2):
    nkv = w_kv.shape[1] // 2

    def body(x_ref, z_ref, wo_ref, gkv_ref, gb_ref, wkv_ref, wb_ref, gq_ref, gk_ref, c_ref, s_ref,
             h_ref, ukv_ref, ub_ref, kv_ref, pb_ref, qo_ref, ko_ref, vo_ref):
        xv = x_ref[...] + _dot_nn(z_ref[...], wo_ref[...])
        h_ref[...] = xv
        xn = xv * _rms_rinv(xv)
        ukv = (xn * gkv_ref[...]).astype(BF16)
        ub = (xn * gb_ref[...]).astype(BF16)
        ukv_ref[...] = ukv
        ub_ref[...] = ub
        kv_ref[...] = _dot_nn(ukv, wkv_ref[...])
        for lo in range(0, 2 * D, D):
            pb_ref[:, lo:lo + D] = _dot_nn(ub, wb_ref[:, lo:lo + D])
        lane = _lane_iota((TM, LANES))
        lo_half = lane < HD
        cos, sin = c_ref[...], s_ref[...]
        gq2, gk2 = _g2(gq_ref), _g2(gk_ref)
        for c in _pairs(D):
            q = pb_ref[:, c]
            qo_ref[:, c] = (_rope_fwd((q * _head_rinv(q, lo_half)) * gq2, cos, sin, lane) * QSCALE).astype(BF16)
        for c in _pairs(nkv):
            k = kv_ref[:, c]
            ko_ref[:, c] = _rope_fwd((k * _head_rinv(k, lo_half)) * gk2, cos, sin, lane).astype(BF16)
        vo_ref[...] = kv_ref[:, nkv:2 * nkv].astype(BF16)

    row = lambda width: pl.BlockSpec((TM, width), lambda i: (i, 0))
    whole = lambda arr: pl.BlockSpec(arr.shape, lambda i: (0,) * arr.ndim)
    return pl.pallas_call(
        body, name="head_b", grid=(S // TM,),
        in_specs=[row(D), row(D), whole(w_out_a), whole(g_kv), whole(g_b), whole(w_kv), whole(w_in_b), whole(gq),
                  whole(gk), row(LANES), row(LANES)],
        out_specs=[row(D), row(D), row(D), row(2 * nkv), row(2 * D), row(D), row(nkv), row(nkv)],
        out_shape=[_sds((S, D), F32), _sds((S, D), BF16), _sds((S, D), BF16), _sds((S, 2 * nkv), F32),
                   _sds((S, 2 * D), F32), _sds((S, D), BF16), _sds((S, nkv), BF16), _sds((S, nkv), BF16)],
        compiler_params=_params())(x, z_a, w_out_a, g_kv, g_b, w_kv, w_in_b, gq, gk, cos2, sin2)


N_KV, GRP = 4, 4


def _swa_mask(n):
    r = lax.broadcasted_iota(jnp.int32, (2 * WIN, GRP * WIN), 0)
    q = lax.broadcasted_iota(jnp.int32, (2 * WIN, GRP * WIN), 1) & (WIN - 1)
    return (r > q) & (r <= q + WIN) & ((r >= WIN) | (n > 0))


def _stack4(ref_or_val, base):
    return jnp.concatenate([ref_or_val[:, base + HD * g: base + HD * (g + 1)] for g in range(GRP)], axis=0)


def _unstack4(xt):
    return jnp.concatenate([xt[:, WIN * g:WIN * (g + 1)] for g in range(GRP)], axis=0).T


def _band(prev_ref, cur_ref, kh):
    return jnp.concatenate([prev_ref[:, HD * kh:HD * (kh + 1)], cur_ref[:, HD * kh:HD * (kh + 1)]], axis=0)


def _sink_row(s_ref, first):
    lane = _lane_iota((1, GRP * WIN))
    row = jnp.full((1, GRP * WIN), s_ref[first + GRP - 1], F32)
    for g in range(GRP - 2, -1, -1):
        row = jnp.where(lane < WIN * (g + 1), s_ref[first + g], row)
    return row


def _swa_fwd(qb, ksh, vsh, pb, sinks):
    nb = S // WIN

    def body(q_ref, kp_ref, kc_ref, vp_ref, vc_ref, g_ref, s_ref, o_ref, z_ref, lse_ref):
        n = pl.program_id(0)
        valid = _swa_mask(n)
        outs = []
        for kh in range(N_KV):
            kb, vb = _band(kp_ref, kc_ref, kh), _band(vp_ref, vc_ref, kh)
            st = jnp.where(valid, _dot_nt(kb, _stack4(q_ref, GRP * HD * kh)), -jnp.inf)
            sink = _sink_row(s_ref, GRP * kh)
            m = jnp.maximum(jnp.max(st, axis=0, keepdims=True), sink)
            pt = jnp.exp(st - m)
            l = jnp.sum(pt, axis=0, keepdims=True) + jnp.exp(sink - m)
            outs.append(_unstack4(_dot_tn(vb, pt.astype(BF16)) / l))
            lse = m + jnp.log(l)
            for g in range(GRP):
                lse_ref[GRP * kh + g, 0] = lse[:, WIN * g:WIN * (g + 1)]
        o = jnp.concatenate(outs, axis=-1)
        o_ref[...] = o
        g = g_ref[...]
        z_ref[...] = (o * (g * _sigmoid(g))).astype(BF16)

    row = pl.BlockSpec((WIN, D), lambda n: (n, 0))
    prev = pl.BlockSpec((WIN, N_KV * HD), lambda n: (jnp.maximum(n - 1, 0), 0))
    cur = pl.BlockSpec((WIN, N_KV * HD), lambda n: (n, 0))
    return pl.pallas_call(
        body, name="swa_fwd", grid=(nb,),
        in_specs=[row, prev, cur, prev, cur, pl.BlockSpec((WIN, D), lambda n: (n, 1)),
                  pl.BlockSpec(memory_space=pltpu.SMEM)],
        out_specs=[row, row, pl.BlockSpec((N_HEADS, 1, 1, WIN), lambda n: (0, n, 0, 0))],
        out_shape=[_sds((S, D), F32), _sds((S, D), BF16), _sds((N_HEADS, nb, 1, WIN), F32)],
        compiler_params=_params())(qb, ksh, ksh, vsh, vsh, pb, sinks)


def _swa_bwd(qb, ksh, vsh, dz, o, lse, pb, sinks, gq, cos2, sin2):
    nb = S // WIN

    def body(q_ref, kp_ref, kc_ref, vp_ref, vc_ref, dz_ref, o_ref, lse_ref, x_ref, g_ref, s_ref, gq_ref, c_ref, sn_ref,
             dpb_ref, dka_ref, dkb_ref, dva_ref, dvb_ref, dsink_ref, dgq_ref):
        n = pl.program_id(0)

        @pl.when(n == 0)
        def _():
            dsink_ref[...] = jnp.zeros_like(dsink_ref)
            dgq_ref[...] = jnp.zeros_like(dgq_ref)

        valid = _swa_mask(n)
        g = g_ref[...]
        sg = _sigmoid(g)
        dzv = dz_ref[...]
        ov = o_ref[...]
        do = dzv * (g * sg)
        dpb_ref[:, D:2 * D] = (dzv * ov * (sg * (1.0 + g * (1.0 - sg)))).astype(BF16)
        prod_t = (do * ov).T
        lane1 = _lane_iota((1, LANES))
        dqs, dkas, dkbs, dvas, dvbs = [], [], [], [], []
        dsink = jnp.zeros((1, LANES), F32)
        for kh in range(N_KV):
            kb, vb = _band(kp_ref, kc_ref, kh), _band(vp_ref, vc_ref, kh)
            base = GRP * HD * kh
            qs = _stack4(q_ref, base)
            dos = _stack4(do, base).astype(BF16)
            delta = jnp.concatenate(
                [jnp.sum(prod_t[base + HD * gg:base + HD * (gg + 1), :], axis=0, keepdims=True)
                 for gg in range(GRP)], axis=1)
            lse = jnp.concatenate([lse_ref[GRP * kh + gg, 0] for gg in range(GRP)], axis=1)
            st = jnp.where(valid, _dot_nt(kb, qs), -jnp.inf)
            pt = jnp.exp(st - lse)
            dst = pt * (_dot_nt(vb, dos) - delta)
            dsb = dst.astype(BF16)
            dqs.append(_unstack4(_dot_tn(kb, dsb)))
            dkband = _dot_nn(dsb, qs)
            dvband = _dot_nn(pt.astype(BF16), dos)
            dkbs.append(dkband[0:WIN, :])
            dkas.append(dkband[WIN:2 * WIN, :])
            dvbs.append(dvband[0:WIN, :])
            dvas.append(dvband[WIN:2 * WIN, :])
            ps_delta = jnp.exp(_sink_row(s_ref, GRP * kh) - lse) * delta
            for gg in range(GRP):
                val = jnp.sum(ps_delta[:, WIN * gg:WIN * (gg + 1)], axis=1, keepdims=True)
                dsink = dsink - jnp.where(lane1 == GRP * kh + gg, val, 0.0)
        dka_ref[...] = jnp.concatenate(dkas, axis=-1)
        dkb_ref[...] = jnp.concatenate(dkbs, axis=-1)
        dva_ref[...] = jnp.concatenate(dvas, axis=-1)
        dvb_ref[...] = jnp.concatenate(dvbs, axis=-1)
        dsink_ref[...] += dsink

        lane = _lane_iota((WIN, LANES))
        g2, cos, sin = _g2(gq_ref), c_ref[...], sn_ref[...]
        dg_tot = jnp.zeros((1, LANES), F32)
        for kh in range(N_KV):
            for c in _pairs(GRP * HD):
                cols = slice(GRP * HD * kh + c.start, GRP * HD * kh + c.stop)
                dn = _rope_bwd(dqs[kh][:, c] * QSCALE, cos, sin, lane)
                dx, dg = _head_norm_bwd(dn, x_ref[:, cols], g2, lane < HD)
                dpb_ref[:, cols] = dx.astype(BF16)
                dg_tot = dg_tot + dg
        dgq_ref[...] += dg_tot

    row = pl.BlockSpec((WIN, D), lambda n: (n, 0))
    prev = pl.BlockSpec((WIN, N_KV * HD), lambda n: (jnp.maximum(n - 1, 0), 0))
    cur = pl.BlockSpec((WIN, N_KV * HD), lambda n: (n, 0))
    acc = pl.BlockSpec((1, LANES), lambda n: (0, 0))
    tab = pl.BlockSpec((WIN, LANES), lambda n: (n, 0))
    return pl.pallas_call(
        body, name="swa_bwd", grid=(nb,),
        in_specs=[row, prev, cur, prev, cur, row, row, pl.BlockSpec((N_HEADS, 1, 1, WIN), lambda n: (0, n, 0, 0)),
                  row, pl.BlockSpec((WIN, D), lambda n: (n, 1)), pl.BlockSpec(memory_space=pltpu.SMEM),
                  pl.BlockSpec((1, HD), lambda n: (0, 0)), tab, tab],
        out_specs=[pl.BlockSpec((WIN, 2 * D), lambda n: (n, 0)), cur, cur, cur, cur, acc, acc],
        out_shape=[_sds((S, 2 * D), BF16)] + [_sds((S, 256), F32)] * 4 + [_sds((1, LANES), F32)] * 2,
        compiler_params=_params())(qb, ksh, ksh, vsh, vsh, dz, o, lse, pb, pb, sinks, gq, cos2, sin2)


def _prep_kv_bwd(dka, dkb, dva, dvb, kv, gk, cos2, sin2):
    nt = S // RB
    per = RB // WIN

    def shifted(cur_ref, nxt_ref, has_next):
        return jnp.concatenate([cur_ref[WIN:RB, :], jnp.where(has_next, nxt_ref[...], 0.0)], axis=0)

    def body(dka_ref, dkb_ref, dkn_ref, dva_ref, dvb_ref, dvn_ref, x_ref, g_ref, c_ref, s_ref, o_ref, dgk_ref):
        i, j = pl.program_id(0), pl.program_id(1)

        @pl.when((i == 0) & (j == 0))
        def _():
            dgk_ref[...] = jnp.zeros_like(dgk_ref)

        has_next = i < nt - 1

        @pl.when(j == 0)
        def _():
            lane = _lane_iota((RB, LANES))
            g2, cos, sin = _g2(g_ref), c_ref[...], s_ref[...]
            dy_all = dka_ref[...] + shifted(dkb_ref, dkn_ref, has_next)
            dg_tot = jnp.zeros((1, LANES), F32)
            for c in _pairs(CB):
                dn = _rope_bwd(dy_all[:, c], cos, sin, lane)
                dx, dg = _head_norm_bwd(dn, x_ref[:, c], g2, lane < HD)
                o_ref[:, c] = dx.astype(BF16)
                dg_tot = dg_tot + dg
            dgk_ref[...] += dg_tot

        @pl.when(j == 1)
        def _():
            o_ref[...] = (dva_ref[...] + shifted(dvb_ref, dvn_ref, has_next)).astype(BF16)

    cur = pl.BlockSpec((RB, CB), lambda i, j: (i, 0))
    nxt = pl.BlockSpec((WIN, CB), lambda i, j: (jnp.minimum(per * (i + 1), S // WIN - 1), 0))
    tab = pl.BlockSpec((RB, LANES), lambda i, j: (i, 0))
    return pl.pallas_call(
        body, name="prep_kv_bwd", grid=(nt, 2),
        in_specs=[cur, cur, nxt, cur, cur, nxt, cur, pl.BlockSpec((1, HD), lambda i, j: (0, 0)), tab, tab],
        out_specs=[pl.BlockSpec((RB, CB), lambda i, j: (i, j)), pl.BlockSpec((1, LANES), lambda i, j: (0, 0))],
        out_shape=[_sds((S, 2 * CB), BF16), _sds((1, LANES), F32)],
        compiler_params=_params())(dka, dkb, dkb, dva, dvb, dvb, kv, gk, cos2, sin2)


def _out_b_loss(z, w_out, h1, tgt):
    tm = 512

    def body(z_ref, w_ref, h_ref, t_ref, dy_ref, l_ref):
        @pl.when(pl.program_id(0) == 0)
        def _():
            l_ref[...] = jnp.zeros_like(l_ref)

        e = (h_ref[...] + _dot_nn(z_ref[...], w_ref[...])) - t_ref[...]
        dy_ref[...] = e * (1.0 / D)
        l_ref[...] += jnp.sum(jnp.sum(e * e, axis=-1, keepdims=True), axis=0, keepdims=True)

    row = pl.BlockSpec((tm, D), lambda i: (i, 0))
    return pl.pallas_call(
        body, name="out_b_loss", grid=(S // tm,),
        in_specs=[row, pl.BlockSpec(w_out.shape, lambda i: (0, 0)), row, row],
        out_specs=[row, pl.BlockSpec((1, LANES), lambda i: (0, 0))],
        out_shape=[_sds((S, D), F32), _sds((1, LANES), F32)], compiler_params=_params())(z, w_out, h1, tgt)


def _du_a_rms_bwd(dproj, wa, x, g, dres, after):
    tm, tk = 512, NA // 2
    nk = NA // tk

    def body(a_ref, b_ref, x_ref, g_ref, dr_ref, after_ref, dx_ref, dg_ref, acc):
        i, kk = pl.program_id(0), pl.program_id(1)

        @pl.when((i == 0) & (kk == 0))
        def _():
            dg_ref[...] = jnp.zeros_like(dg_ref)

        p = _dot_nt(a_ref[...], b_ref[...])

        @pl.when(kk == 0)
        def _():
            acc[...] = p

        @pl.when(kk == nk - 1)
        def _():
            dx, dg = _rms_bwd_core(acc[...] + p, x_ref[...], g_ref[...])
            dx_ref[...] = dr_ref[...] + dx
            dg_ref[...] += dg

    assert nk == 2
    row = pl.BlockSpec((tm, D), lambda i, kk: (i, 0))
    vec = pl.BlockSpec((1, D), lambda i, kk: (0, 0))
    return pl.pallas_call(
        body, name="du_a_rms_bwd", grid=(S // tm, nk),
        in_specs=[pl.BlockSpec((tm, tk), lambda i, kk: (i, kk)), pl.BlockSpec((D, tk), lambda i, kk: (0, kk)),
                  row, vec, row, pl.BlockSpec(after.shape, lambda i, kk: (0, 0))],
        out_specs=[row, vec], out_shape=[_sds((S, D), F32), _sds((1, D), F32)],
        scratch_shapes=[pltpu.VMEM((tm, D), F32)], compiler_params=_params())(dproj, wa, x, g, dres, after)


def _du_b_rms_bwd(dpb, w_in_b, dkv, w_kv, h1, g_b, g_kv, dy):
    tm = 512

    def body(ab_ref, wb_ref, akv_ref, wkv_ref, x_ref, gb_ref, gkv_ref, dy_ref, dh_ref, dgb_ref, dgkv_ref):
        @pl.when(pl.program_id(0) == 0)
        def _():
            dgb_ref[...] = jnp.zeros_like(dgb_ref)
            dgkv_ref[...] = jnp.zeros_like(dgkv_ref)

        x = x_ref[...]
        dx1, dg1 = _rms_bwd_core(_dot_nt(ab_ref[...], wb_ref[...]), x, gb_ref[...])
        dx2, dg2 = _rms_bwd_core(_dot_nt(akv_ref[...], wkv_ref[...]), x, gkv_ref[...])
        dh_ref[...] = dy_ref[...] + dx1 + dx2
        dgb_ref[...] += dg1
        dgkv_ref[...] += dg2

    row = lambda width: pl.BlockSpec((tm, width), lambda i: (i, 0))
    whole = lambda arr: pl.BlockSpec(arr.shape, lambda i: (0, 0))
    vec = pl.BlockSpec((1, D), lambda i: (0, 0))
    return pl.pallas_call(
        body, name="du_b_rms_bwd", grid=(S // tm,),
        in_specs=[row(dpb.shape[1]), whole(w_in_b), row(dkv.shape[1]), whole(w_kv), row(D), vec, vec, row(D)],
        out_specs=[row(D), vec, vec], out_shape=[_sds((S, D), F32), _sds((1, D), F32), _sds((1, D), F32)],
        compiler_params=_params())(dpb, w_in_b, dkv, w_kv, h1, g_b, g_kv, dy)


def _gather_first(w_in_a, w_out_a, w_kv, w_in_b, w_out_b, norm_a_g):
    def body(wia_ref, woa_ref, wkv_ref, wib_ref, wob_ref, ga_ref,
             wa_g, ga_g, woa_s, wkv_s, wib_s, wob_s, wa_s, *sems):
        wa_s[...] = wia_ref[0].astype(BF16)
        woa_s[...] = woa_ref[0].astype(BF16)
        wkv_s[...] = wkv_ref[...].astype(BF16)
        wib_s[...] = wib_ref[0].astype(BF16)
        wob_s[...] = wob_ref[0].astype(BF16)
        _gather_two_level([wa_s, ga_ref], [wa_g, ga_g], sems)

    vmem = pl.BlockSpec(memory_space=pltpu.VMEM)
    anyspec = pl.BlockSpec(memory_space=pl.ANY)
    shard = lambda w: _sds(w.shape[-2:], BF16)
    return pl.pallas_call(
        body, name="gather_first", in_specs=[vmem] * 6, out_specs=[anyspec, anyspec, vmem, vmem, vmem, vmem],
        out_shape=[_sds((N_DEV,) + w_in_a.shape[-2:], BF16), _sds((N_DEV,) + norm_a_g.shape, F32),
                   shard(w_out_a), shard(w_kv), shard(w_in_b), shard(w_out_b)],
        scratch_shapes=[pltpu.VMEM(w_in_a.shape[-2:], BF16)] + _exchange_sems(2),
        compiler_params=pltpu.CompilerParams(vmem_limit_bytes=VMEM_LIMIT, has_side_effects=True))(
            w_in_a, w_out_a, w_kv, w_in_b, w_out_b, norm_a_g)


def _pair_reduce(slots):
    n_chip = N_DEV // 2
    _, rows, cols = slots.shape

    def body(s_ref, o_ref, own_v, sib_v, send_sems, recv_sems, local_sems):
        x, y, c = lax.axis_index("x"), lax.axis_index("y"), lax.axis_index("c")
        copies = []
        for j in range(n_chip):
            own = pltpu.make_async_copy(s_ref.at[2 * j + c], own_v.at[j], local_sems.at[j])
            give = pltpu.make_async_remote_copy(
                src_ref=s_ref.at[2 * j + 1 - c], dst_ref=sib_v.at[j], send_sem=send_sems.at[j],
                recv_sem=recv_sems.at[j], device_id=(x, y, 1 - c), device_id_type=pl.DeviceIdType.MESH)
            own.start()
            give.start()
            copies.append((own, give))
        for j, (own, give) in enumerate(copies):
            own.wait()
            give.wait()
            o_ref[j] = (own_v[j].astype(F32) + sib_v[j].astype(F32)).astype(BF16)

    half = _sds((n_chip, rows, cols), slots.dtype)
    return pl.pallas_call(
        body, name="pair_reduce", in_specs=[pl.BlockSpec(memory_space=pl.ANY)],
        out_specs=pl.BlockSpec(memory_space=pltpu.VMEM), out_shape=half,
        scratch_shapes=[pltpu.VMEM(half.shape, half.dtype), pltpu.VMEM(half.shape, half.dtype),
                        pltpu.SemaphoreType.DMA((n_chip,)), pltpu.SemaphoreType.DMA((n_chip,)),
                        pltpu.SemaphoreType.DMA((n_chip,))],
        compiler_params=pltpu.CompilerParams(vmem_limit_bytes=VMEM_LIMIT, has_side_effects=True))(slots)


def _padded_col(c):
    if c < RAW_F:
        return c
    return FOFF + (c - RAW_F) if c < RAW_G else GOFF + (c - RAW_G)


def _shard_pieces():
    width = NA_RAW // N_DEV
    pieces = []
    for d in range(N_DEV):
        cuts = [width * d] + [c for c in (RAW_F, RAW_G) if width * d < c < width * (d + 1)] + [width * (d + 1)]
        for lo, hi in zip(cuts[:-1], cuts[1:]):
            pieces.append((d, lo - width * d, _padded_col(lo), hi - lo))
    return pieces


def _unshard_wa(wa_g):
    def body(w_ref, o_ref):
        o_ref[:, FOFF + N_HEADS:NA] = jnp.zeros((TM, NA - FOFF - N_HEADS), BF16)
        for d, src, dst, width in _shard_pieces():
            o_ref[:, dst:dst + width] = w_ref[d, :, src:src + width]

    return pl.pallas_call(
        body, name="unshard_wa", grid=(D // TM,),
        in_specs=[pl.BlockSpec((N_DEV, TM, NA_RAW // N_DEV), lambda i: (0, i, 0))],
        out_specs=pl.BlockSpec((TM, NA), lambda i: (i, 0)), out_shape=_sds((D, NA), BF16),
        compiler_params=_params())(wa_g)


def _reshard_dwa(dwa):
    def body(g_ref, o_ref):
        for d, src, dst, width in _shard_pieces():
            o_ref[d, :, src:src + width] = g_ref[:, dst:dst + width]

    return pl.pallas_call(
        body, name="reshard_dwa", grid=(D // TM,), in_specs=[pl.BlockSpec((TM, NA), lambda i: (i, 0))],
        out_specs=pl.BlockSpec((N_DEV, TM, NA_RAW // N_DEV), lambda i: (0, i, 0)),
        out_shape=_sds((N_DEV, D, NA_RAW // N_DEV), dwa.dtype), compiler_params=_params())(dwa)


CHIP_FLIPS = (2, 4, 6)


def _chip_exchange_start(partial):
    n = len(CHIP_FLIPS)

    def body(p_ref, land_ref, *rest):
        sends, recvs, token = rest[:n], rest[n:2 * n], rest[2 * n + 2]
        x, y, c = lax.axis_index("x"), lax.axis_index("y"), lax.axis_index("c")
        me = 4 * x + 2 * y + c
        for idx, k in enumerate(CHIP_FLIPS):
            pltpu.make_async_remote_copy(
                src_ref=p_ref.at[(me ^ k) >> 1], dst_ref=land_ref.at[me >> 1], send_sem=sends[idx],
                recv_sem=recvs[idx], device_id=(x ^ ((k >> 2) & 1), y ^ ((k >> 1) & 1), c),
                device_id_type=pl.DeviceIdType.MESH).start()
        token[...] = jnp.zeros_like(token)

    hbm = pl.BlockSpec(memory_space=pltpu.HBM)
    sem = pl.BlockSpec(memory_space=pltpu.SEMAPHORE)
    buf = pltpu.HBM(partial.shape, partial.dtype)
    return pl.pallas_call(
        body, name="chip_exchange_start",
        out_shape=(pltpu.SemaphoreType.DMA(()),) * (2 * n) + (buf, buf, _sds((8, LANES), F32)),
        in_specs=(hbm, hbm), out_specs=(sem,) * (2 * n) + (hbm, hbm, pl.BlockSpec(memory_space=pltpu.VMEM)),
        input_output_aliases={0: 2 * n, 1: 2 * n + 1},
        compiler_params=pltpu.CompilerParams(has_side_effects=pltpu.SideEffectType.DATAFLOW_SIDE_EFFECTING))(
            pltpu.with_memory_space_constraint(partial, pltpu.HBM),
            pltpu.with_memory_space_constraint(lax.empty(partial.shape, partial.dtype), pltpu.HBM))


def _chip_exchange_wait(started, after):
    n = len(CHIP_FLIPS)
    sems, (p_thru, land_thru) = started[:2 * n], started[2 * n:2 * n + 2]

    def body(p_ref, land_ref, *rest):
        sends, recvs = rest[:n], rest[n:2 * n]
        x, y, c = lax.axis_index("x"), lax.axis_index("y"), lax.axis_index("c")
        me = 4 * x + 2 * y + c
        for idx, k in enumerate(CHIP_FLIPS):
            copy = pltpu.make_async_remote_copy(
                src_ref=p_ref.at[(me ^ k) >> 1], dst_ref=land_ref.at[(me ^ k) >> 1], send_sem=sends[idx],
                recv_sem=recvs[idx], device_id=(x ^ ((k >> 2) & 1), y ^ ((k >> 1) & 1), c),
                device_id_type=pl.DeviceIdType.MESH)
            copy.wait_send()
            copy.wait_recv()

    hbm = pl.BlockSpec(memory_space=pltpu.HBM)
    sem = pl.BlockSpec(memory_space=pltpu.SEMAPHORE)
    buf = pltpu.HBM(p_thru.shape, p_thru.dtype)
    return pl.pallas_call(
        body, name="chip_exchange_wait", out_shape=(buf, buf),
        in_specs=(hbm, hbm) + (sem,) * (2 * n) + (pl.BlockSpec(memory_space=pl.ANY),), out_specs=(hbm, hbm),
        input_output_aliases={0: 0, 1: 1},
        compiler_params=pltpu.CompilerParams(has_side_effects=pltpu.SideEffectType.DATAFLOW_SIDE_EFFECTING))(
            p_thru, land_thru, *sems, after)[1]


def _gather_slab(slab):
    def body(s_ref, o_ref, *sems):
        _exchange_ops(["gather_rows"], [s_ref], [o_ref], sems, True, True)

    anyspec = pl.BlockSpec(memory_space=pl.ANY)
    return pl.pallas_call(
        body, name="gather_slab", in_specs=[anyspec], out_specs=anyspec,
        out_shape=_sds((N_DEV,) + slab.shape, slab.dtype), scratch_shapes=_exchange_sems(1),
        compiler_params=pltpu.CompilerParams(has_side_effects=True))(slab)


def _adamw(w, g, m, v):
    m = ADAM_B1 * m + (1.0 - ADAM_B1) * g
    v = ADAM_B2 * v + (1.0 - ADAM_B2) * (g * g)
    m_hat = m / (1.0 - ADAM_B1 ** ADAM_STEP)
    v_hat = v / (1.0 - ADAM_B2 ** ADAM_STEP)
    delta = -ADAM_LR * (m_hat / (jnp.sqrt(v_hat) + ADAM_EPS) + ADAM_WD * w)
    return delta, m, v


def _sum_adamw(recv, w, m, v, name):
    lead = w.ndim - 2
    rows, cols = w.shape[-2:]
    tr = 128
    slabs = list(recv) if isinstance(recv, tuple) else [recv]
    n_slots = slabs[0].shape[0]

    def body(*refs):
        r_ref, own_ref = refs[0], refs[len(slabs) - 1]
        w_ref, m_ref, v_ref, g_ref, d_ref, nm_ref, nv_ref = refs[len(slabs):]
        chip = (4 * lax.axis_index("x") + 2 * lax.axis_index("y") + lax.axis_index("c")) >> 1
        g = None
        for slot in range(n_slots):
            part = r_ref[slot]
            if len(slabs) == 2:
                part = jnp.where(chip == slot, own_ref[slot], part)
            g = part.astype(F32) if g is None else g + part.astype(F32)
        g_ref[...] = g
        d_ref[...], nm_ref[...], nv_ref[...] = _adamw(w_ref[...], g, m_ref[...], v_ref[...])

    blk = pl.BlockSpec((None,) * lead + (tr, cols), lambda i: (0,) * lead + (i, 0))
    slots = pl.BlockSpec((n_slots, tr, cols), lambda i: (0, i, 0))
    return pl.pallas_call(
        body, name=name, grid=(rows // tr,), in_specs=[slots] * len(slabs) + [blk, blk, blk],
        out_specs=[blk] * 4, out_shape=[_sds(w.shape, F32)] * 4, compiler_params=_params())(*slabs, w, m, v)


SLAB_ROWS = 16
SLOT = {"kv_norm_g": (8, 0, D), "norm_b_g": (9, 0, D), "b_forget": (10, 0, 16), "qnorm_a_g": (10, 128, HD),
        "knorm_a_g": (10, 256, HD), "knorm_b_g": (10, 384, HD), "qnorm_b_g": (10, 512, HD), "sinks": (10, 640, 16)}
SMALL = ["norm_a_g", "b_forget", "qnorm_a_g", "knorm_a_g", "kv_norm_g", "knorm_b_g", "norm_b_g", "qnorm_b_g", "sinks"]


LOSS_ROW = 11


def _pack_small(dg_a, dg_kv, dg_b, db_f, dgq_a, dgk_a, dgk_b, dgq_b, dsinks, lsum):
    def fold(ref):
        return ref[:, 0:HD] + ref[:, HD:2 * HD]

    def body(dga_ref, dgkv_ref, dgb_ref, dbf_ref, dgqa_ref, dgka_ref, dgkb_ref, dgqb_ref, dsk_ref, ls_ref, slab_ref):
        slab_ref[...] = jnp.zeros_like(slab_ref)
        for r in range(N_DEV):
            slab_ref[r:r + 1, 0:LANES] = dga_ref[:, LANES * r:LANES * (r + 1)]
        slab_ref[8:9, :] = dgkv_ref[...]
        slab_ref[9:10, :] = dgb_ref[...]
        slab_ref[10:11, 0:LANES] = dbf_ref[...]
        slab_ref[10:11, 128:128 + HD] = fold(dgqa_ref)
        slab_ref[10:11, 256:256 + HD] = fold(dgka_ref)
        slab_ref[10:11, 384:384 + HD] = fold(dgkb_ref)
        slab_ref[10:11, 512:512 + HD] = fold(dgqb_ref)
        slab_ref[10:11, 640:640 + LANES] = dsk_ref[...]
        slab_ref[LOSS_ROW:LOSS_ROW + 1, 0:LANES] = ls_ref[...]

    return pl.pallas_call(body, name="pack_small", out_shape=_sds((SLAB_ROWS, D), F32), compiler_params=_params())(
        dg_a, dg_kv, dg_b, db_f, dgq_a, dgk_a, dgk_b, dgq_b, dsinks, lsum)


def _small_adamw(recv, ws, ms, vs):
    k = len(SMALL)

    def body(*refs):
        r_ref = refs[0]
        w_refs, m_refs, v_refs = refs[1:1 + k], refs[1 + k:1 + 2 * k], refs[1 + 2 * k:1 + 3 * k]
        outs = refs[1 + 3 * k:1 + 7 * k]
        loss_ref, tot = refs[1 + 7 * k], refs[2 + 7 * k]
        g = r_ref[0]
        for dev in range(1, N_DEV):
            g = g + r_ref[dev]
        tot[...] = g
        loss_ref[...] = tot[LOSS_ROW:LOSS_ROW + 1, 0:LANES] * (0.5 / D)
        me = 4 * lax.axis_index("x") + 2 * lax.axis_index("y") + lax.axis_index("c")
        for p, name in enumerate(SMALL):
            if name == "norm_a_g":
                mine = lax.broadcasted_iota(jnp.int32, (N_DEV, LANES), 0) == me
                gp = jnp.sum(jnp.where(mine, tot[0:N_DEV, 0:LANES], 0.0), axis=0, keepdims=True)
            else:
                row, lo, width = SLOT[name]
                gp = tot[row:row + 1, lo:lo + width]
            d, nm, nv = _adamw(w_refs[p][...], gp, m_refs[p][...], v_refs[p][...])
            outs[p][...] = gp
            outs[k + p][...] = d
            outs[2 * k + p][...] = nm
            outs[3 * k + p][...] = nv

    shapes = [_sds(w.shape, F32) for w in ws]
    return pl.pallas_call(body, name="small_adamw", out_shape=shapes * 4 + [_sds((1, LANES), F32)],
                          scratch_shapes=[pltpu.VMEM((SLAB_ROWS, D), F32)],
                          compiler_params=_params())(recv, *ws, *ms, *vs)


def _rope_tables(positions):
    inv_freq = jnp.power(jnp.float32(ROPE_THETA), -jnp.arange(0, ROT, 2, dtype=F32) / ROT)
    ang = positions.astype(F32)[:, None] * inv_freq[None, :]
    cos, sin = jnp.cos(ang), jnp.sin(ang)
    c64 = jnp.concatenate([cos, cos, jnp.ones((S, HD - ROT), F32)], axis=-1)
    s64 = jnp.concatenate([-sin, sin, jnp.zeros((S, HD - ROT), F32)], axis=-1)
    return jnp.tile(c64, (1, 2)), jnp.tile(s64, (1, 2))


def _local_step(x, tgt, positions, g_a, wa, b_forget, gq_a, gk_a, g_kv, gk_b, g_b, gq_b, sinks,
                woa_s, wkv_s, wib_s, wob_s):
    nq = S // TQ
    cos2, sin2 = _rope_tables(positions)
    b_pad = jnp.pad(b_forget, ((0, 0), (0, LANES - N_HEADS)))

    u_a, proj, qn, kn, vb, ccol, cbc = _head_a(x, g_a, wa, gq_a, gk_a, b_pad)
    crow = ccol[:, :N_HEADS].T.reshape(N_HEADS, nq, 1, TQ)
    o_a, z_a, lse_a, woa_g, wkv_g, w_in_b, wob_g = _fox_fwd(
        qn, kn, vb, proj, crow, cbc,
        rider=[("gather_rows", woa_s), ("gather_rows", wkv_s), ("gather_cols", wib_s), ("gather_rows", wob_s)])
    w_out_a, w_kv, w_out_b = woa_g.reshape(D, D), wkv_g.reshape(D, 512), wob_g.reshape(D, D)
    h1, u_kv, u_b, kv, pb, qb, ksh, vsh = _head_b(x, z_a, w_out_a, g_kv, g_b, w_kv, w_in_b, gq_b, gk_b, cos2, sin2)
    sinks1 = sinks.reshape(N_HEADS)
    o_b, z_b, lse_b = _swa_fwd(qb, ksh, vsh, pb, sinks1)
    dy, lsum = _out_b_loss(z_b, w_out_b, h1, tgt)
    dw_out_b = _mm(z_b, dy, "tn", 512, 512, S, out_dtype=BF16, name="mm_dw_out_b")
    dz_b = _mm(dy, w_out_b, "nt", 1024, 512, D, name="mm_dz_b")
    dpb, dka, dkb, dva, dvb, dsinks, dgq_b = _swa_bwd(qb, ksh, vsh, dz_b, o_b, lse_b, pb, sinks1, gq_b, cos2, sin2)
    dkv, dgk_b = _prep_kv_bwd(dka, dkb, dva, dvb, kv, gk_b, cos2, sin2)
    dw_in_b = _mm(u_b, dpb, "tn", 512, 512, S, out_dtype=BF16, name="mm_dw_in_b")
    dw_kv = _mm(u_kv, dkv, "tn", 512, 512, S, out_dtype=BF16, name="mm_dw_kv")
    dh1, dg_b, dg_kv = _du_b_rms_bwd(dpb, w_in_b, dkv, w_kv, h1, g_b, g_kv, dy)
    dw_out_a = _mm(z_a, dh1, "tn", 512, 512, S, out_dtype=BF16, name="mm_dw_out_a")
    do_a, dgate_a, delta_a = _fox_bwd_pre(dh1, w_out_a, proj, o_a)
    dk_a, dv_a, dcs, dq_a, drow, r_wob, r_wib, r_wkv, r_woa = _fox_bwd(
        qn, kn, vb, do_a, lse_a, delta_a, crow, cbc,
        rider=[("a2a_rows", dw_out_b), ("a2a_cols", dw_in_b), ("a2a_rows", dw_kv), ("a2a_rows", dw_out_a)])
    drow_col = jnp.pad(drow.reshape(N_HEADS, S).T, ((0, 0), (0, LANES - N_HEADS)))
    dproj, dgq_a, dgk_a, db_f = _prep_a_bwd(dq_a, dk_a, dv_a, dgate_a, drow_col, dcs, proj, b_pad, gq_a, gk_a)
    dwa = _mm(u_a, dproj, "tn", 1024, 256, S, out_dtype=BF16, name="mm_dw_in_a")
    partial = _pair_reduce(_reshard_dwa(dwa))
    started = _chip_exchange_start(partial)
    dx, dg_a = _du_a_rms_bwd(dproj, wa, x, g_a, dh1, after=started[-1])
    slab = _pack_small(dg_a, dg_kv, dg_b, db_f, dgq_a, dgk_a, dgk_b, dgq_b, dsinks, lsum)
    slab_g = _gather_slab(slab)
    r_wa = _chip_exchange_wait(started, slab_g)
    return dx, (r_wa, partial), r_woa, r_wkv, r_wib, r_wob, slab_g


def kernel(x, positions, norm_a_g, w_in_a, b_forget, qnorm_a_g, knorm_a_g, w_out_a, kv_norm_g, w_kv, knorm_b_g, norm_b_g, w_in_b, qnorm_b_g, sinks, w_out_b, loss_target, m_norm_a_g, m_w_in_a, m_b_forget, m_qnorm_a_g, m_knorm_a_g, m_w_out_a, m_kv_norm_g, m_w_kv, m_knorm_b_g, m_norm_b_g, m_w_in_b, m_qnorm_b_g, m_sinks, m_w_out_b, v_norm_a_g, v_w_in_a, v_b_forget, v_qnorm_a_g, v_knorm_a_g, v_w_out_a, v_kv_norm_g, v_w_kv, v_knorm_b_g, v_norm_b_g, v_w_in_b, v_qnorm_b_g, v_sinks, v_w_out_b):
    wa_g, ga_g, woa_s, wkv_s, wib_s, wob_s = _gather_first(w_in_a, w_out_a, w_kv, w_in_b, w_out_b, norm_a_g)
    dx, r_wa, r_woa, r_wkv, r_wib, r_wob, slab_g = _local_step(
        x[0], loss_target[0], positions, ga_g.reshape(1, D), _unshard_wa(wa_g), b_forget, qnorm_a_g, knorm_a_g,
        kv_norm_g.reshape(1, D), knorm_b_g.reshape(1, HD), norm_b_g, qnorm_b_g, sinks, woa_s, wkv_s, wib_s, wob_s)

    big = {}
    for name, recv, w, m, v in (
            ("w_in_a", r_wa, w_in_a, m_w_in_a, v_w_in_a), ("w_out_a", r_woa, w_out_a, m_w_out_a, v_w_out_a),
            ("w_kv", r_wkv, w_kv, m_w_kv, v_w_kv), ("w_in_b", r_wib, w_in_b, m_w_in_b, v_w_in_b),
            ("w_out_b", r_wob, w_out_b, m_w_out_b, v_w_out_b)):
        big[name] = _sum_adamw(recv, w, m, v, "adamw_" + name)

    r2 = lambda a: a.reshape(1, -1)
    small_w = dict(norm_a_g=norm_a_g, b_forget=b_forget, qnorm_a_g=qnorm_a_g, knorm_a_g=knorm_a_g,
                   kv_norm_g=kv_norm_g, knorm_b_g=knorm_b_g, norm_b_g=norm_b_g, qnorm_b_g=qnorm_b_g, sinks=sinks)
    small_m = dict(norm_a_g=m_norm_a_g, b_forget=m_b_forget, qnorm_a_g=m_qnorm_a_g, knorm_a_g=m_knorm_a_g,
                   kv_norm_g=m_kv_norm_g, knorm_b_g=m_knorm_b_g, norm_b_g=m_norm_b_g, qnorm_b_g=m_qnorm_b_g,
                   sinks=m_sinks)
    small_v = dict(norm_a_g=v_norm_a_g, b_forget=v_b_forget, qnorm_a_g=v_qnorm_a_g, knorm_a_g=v_knorm_a_g,
                   kv_norm_g=v_kv_norm_g, knorm_b_g=v_knorm_b_g, norm_b_g=v_norm_b_g, qnorm_b_g=v_qnorm_b_g,
                   sinks=v_sinks)
    res = _small_adamw(slab_g, [r2(small_w[n]) for n in SMALL], [r2(small_m[n]) for n in SMALL],
                       [r2(small_v[n]) for n in SMALL])
    k = len(SMALL)
    small = {n: [res[q * k + p].reshape(small_w[n].shape) for q in range(4)] for p, n in enumerate(SMALL)}
    loss = res[4 * k][0, 0]

    order = ["norm_a_g", "w_in_a", "b_forget", "qnorm_a_g", "knorm_a_g", "w_out_a", "kv_norm_g", "w_kv",
             "knorm_b_g", "norm_b_g", "w_in_b", "qnorm_b_g", "sinks", "w_out_b"]

    def leaf(n, q):
        return big[n][q] if n in big else small[n][q]

    outs = [loss, dx[None]]
    for q in range(4):
        outs.extend(leaf(n, q) for n in order)
    return tuple(outs)
```

```python
import jax
import jax.numpy as jnp
from jax import lax
from jax.experimental import pallas as pl
from jax.experimental.pallas import tpu as pltpu

F32, BF16 = jnp.float32, jnp.bfloat16

S = 2048
D = 1024
HD = 64
N_HEADS = 16
N_DEV = 8
NA = 4352
GOFF = 3072
FOFF = 4096
RAW_F = 3072
RAW_G = RAW_F + N_HEADS
NA_RAW = 4112
EPS = 1e-6
QSCALE = 0.125
ROPE_THETA = 500000.0
ROT = 16
WIN = 128
TQ = 256
TK = 256
KS = TQ // 2
HPS = 8
HW = HPS * HD
TM = 256
RB = 512
CB = 256
LANES = 128

ADAM_LR, ADAM_B1, ADAM_B2, ADAM_EPS, ADAM_WD, ADAM_STEP = 0.001, 0.9, 0.999, 1e-08, 0.01, 10

VMEM_LIMIT = 56 * 1024 * 1024


def _params():
    return pltpu.CompilerParams(vmem_limit_bytes=VMEM_LIMIT)


def _sds(shape, dtype):
    return jax.ShapeDtypeStruct(shape, dtype)


def _dot_nt(a, b):
    return lax.dot_general(a, b, (((1,), (1,)), ((), ())), preferred_element_type=F32)


def _dot_tn(a, b):
    return lax.dot_general(a, b, (((0,), (0,)), ((), ())), preferred_element_type=F32)


def _dot_nn(a, b):
    return lax.dot_general(a, b, (((1,), (0,)), ((), ())), preferred_element_type=F32)


def _sigmoid(g):
    return 1.0 / (1.0 + jnp.exp(-g))


def _lane_iota(shape):
    return lax.broadcasted_iota(jnp.int32, shape, len(shape) - 1)


def _flips(kind):
    return (2, 4, 6) if kind == "a2a_chips" else tuple(range(1, N_DEV))


def _send_view(kind, ref, dev):
    if kind in ("gather_rows", "gather_cols"):
        return ref
    if kind == "a2a_slots":
        return ref.at[dev]
    if kind == "a2a_chips":
        return ref.at[dev >> 1]
    if kind == "a2a_rows":
        rows = ref.shape[0] // N_DEV
        return ref.at[pl.ds(pl.multiple_of(dev * rows, rows), rows)]
    cols = ref.shape[1] // N_DEV
    return ref.at[:, pl.ds(pl.multiple_of(dev * cols, cols), cols)]


def _land_view(kind, ref, dev):
    if kind == "gather_cols":
        cols = ref.shape[1] // N_DEV
        return ref.at[:, pl.ds(pl.multiple_of(dev * cols, cols), cols)]
    if kind == "a2a_chips":
        return ref.at[dev >> 1]
    return ref.at[dev]


def _landing_sds(kind, arr):
    if kind == "gather_rows":
        return _sds((N_DEV,) + arr.shape, arr.dtype)
    if kind == "gather_cols":
        return _sds((arr.shape[0], N_DEV * arr.shape[1]), arr.dtype)
    if kind == "a2a_rows":
        return _sds((N_DEV, arr.shape[0] // N_DEV, arr.shape[1]), arr.dtype)
    if kind == "a2a_cols":
        return _sds((N_DEV, arr.shape[0], arr.shape[1] // N_DEV), arr.dtype)
    return _sds(arr.shape, arr.dtype)


def _exchange_sems(n_parts):
    n = n_parts * (N_DEV - 1)
    return [pltpu.SemaphoreType.DMA((n,)), pltpu.SemaphoreType.DMA((n,)), pltpu.SemaphoreType.DMA((n_parts,))]


def _exchange_ops(kinds, srcs, dsts, sems, start, wait):
    send_sems, recv_sems, local_sems = sems
    x, y, c = lax.axis_index("x"), lax.axis_index("y"), lax.axis_index("c")
    me = 4 * x + 2 * y + c

    def local(a):
        return pltpu.make_async_copy(_send_view(kinds[a], srcs[a], me), _land_view(kinds[a], dsts[a], me),
                                     local_sems.at[a])

    def remote(a, k, landing_dev):
        peer = (x ^ ((k >> 2) & 1), y ^ ((k >> 1) & 1), c ^ (k & 1))
        sem = a * (N_DEV - 1) + k - 1
        return pltpu.make_async_remote_copy(
            src_ref=_send_view(kinds[a], srcs[a], me ^ k), dst_ref=_land_view(kinds[a], dsts[a], landing_dev),
            send_sem=send_sems.at[sem], recv_sem=recv_sems.at[sem], device_id=peer,
            device_id_type=pl.DeviceIdType.MESH)

    pairs = [(a, k) for k in range(1, N_DEV) for a in range(len(kinds)) if k in _flips(kinds[a])]
    if start:
        for a in range(len(kinds)):
            local(a).start()
        for a, k in pairs:
            remote(a, k, me).start()
    if wait:
        for a, k in pairs:
            remote(a, k, me ^ k).wait_recv()
            remote(a, k, me).wait_send()
        for a in range(len(kinds)):
            local(a).wait()


def _gather_two_level(srcs, dsts, sems, meanwhile=None):
    send_sems, recv_sems, local_sems = sems
    x, y, c = lax.axis_index("x"), lax.axis_index("y"), lax.axis_index("c")
    me, sibling = (x, y, c), (x, y, 1 - c)
    chips = [(1 - x, y), (x, 1 - y), (1 - x, 1 - y)]

    def slot(ref, dev):
        return ref.at[4 * dev[0] + 2 * dev[1] + dev[2]]

    def copy(a, k, block, to, src=None):
        return pltpu.make_async_remote_copy(
            src_ref=slot(dsts[a], block) if src is None else src, dst_ref=slot(dsts[a], block),
            send_sem=send_sems.at[a * (N_DEV - 1) + k], recv_sem=recv_sems.at[a * (N_DEV - 1) + k],
            device_id=to, device_id_type=pl.DeviceIdType.MESH)

    parts = range(len(srcs))
    mine = [pltpu.make_async_copy(srcs[a], slot(dsts[a], me), local_sems.at[a]) for a in parts]
    first = [copy(a, 0, me, sibling, src=srcs[a]) for a in parts]
    first += [copy(a, 1 + j, me, (*chip, c), src=srcs[a]) for j, chip in enumerate(chips) for a in parts]
    for cp in mine + first:
        cp.start()
    if meanwhile is not None:
        meanwhile()
    passed = []
    for j, chip in enumerate(chips):
        for a in parts:
            copy(a, 1 + j, (*chip, c), me).wait_recv()
            fwd = copy(a, 4 + j, (*chip, c), sibling)
            fwd.start()
            passed.append(fwd)
    for a in parts:
        copy(a, 0, sibling, me).wait_recv()
        for j, chip in enumerate(chips):
            copy(a, 4 + j, (*chip, 1 - c), me).wait_recv()
    for cp in first + passed:
        cp.wait_send()
    for cp in mine:
        cp.wait()


def _call(body, *, name, args, in_specs, out_specs, out_shape, grid=(), scratch_shapes=(), aliases=None, rider=()):
    n_in, n_out, n_scr, n_r = len(in_specs), len(out_specs), len(scratch_shapes), len(rider)
    kinds = [kind for kind, _ in rider]

    def kernel_body(*refs):
        c_in, r_in = refs[:n_in], refs[n_in:n_in + n_r]
        c_out = refs[n_in + n_r:n_in + n_r + n_out]
        r_out = refs[n_in + n_r + n_out:n_in + 2 * n_r + n_out]
        rest = refs[n_in + 2 * n_r + n_out:]
        c_scr, sems = rest[:n_scr], rest[n_scr:]
        if n_r:
            assert grid, "a rider needs a gridded call"
            ids = [pl.program_id(ax) for ax in range(len(grid))]
            first, last = ids[0] == 0, ids[0] == grid[0] - 1
            for pid, size in zip(ids[1:], grid[1:]):
                first = first & (pid == 0)
                last = last & (pid == size - 1)
            pl.when(first)(lambda: _exchange_ops(kinds, r_in, r_out, sems, True, False))
        body(*c_in, *c_out, *c_scr)
        if n_r:
            pl.when(last)(lambda: _exchange_ops(kinds, r_in, r_out, sems, False, True))

    anyspec = pl.BlockSpec(memory_space=pl.ANY)
    params = pltpu.CompilerParams(vmem_limit_bytes=VMEM_LIMIT, has_side_effects=bool(n_r))
    outs = pl.pallas_call(
        kernel_body, name=name, grid=grid, in_specs=list(in_specs) + [anyspec] * n_r,
        out_specs=list(out_specs) + [anyspec] * n_r,
        out_shape=list(out_shape) + [_landing_sds(kind, arr) for kind, arr in rider],
        scratch_shapes=list(scratch_shapes) + (_exchange_sems(n_r) if n_r else []),
        input_output_aliases=aliases or {}, compiler_params=params)(*args, *[arr for _, arr in rider])
    return list(outs)


def _mm(a, b, mode, tm, tn, tk, out_dtype=F32, add=None, name="mm", rider=()):
    if mode == "nn":
        (m, k), n = a.shape, b.shape[1]
        a_spec = pl.BlockSpec((tm, tk), lambda i, j, kk: (i, kk))
        b_spec = pl.BlockSpec((tk, tn), lambda i, j, kk: (kk, j))
        dot = _dot_nn
    elif mode == "nt":
        (m, k), n = a.shape, b.shape[0]
        a_spec = pl.BlockSpec((tm, tk), lambda i, j, kk: (i, kk))
        b_spec = pl.BlockSpec((tn, tk), lambda i, j, kk: (j, kk))
        dot = _dot_nt
    else:
        (k, m), n = a.shape, b.shape[1]
        a_spec = pl.BlockSpec((tk, tm), lambda i, j, kk: (kk, i))
        b_spec = pl.BlockSpec((tk, tn), lambda i, j, kk: (kk, j))
        dot = _dot_tn
    assert m % tm == 0 and n % tn == 0 and k % tk == 0, (m, n, k, tm, tn, tk)
    nk = k // tk
    has_add = add is not None

    def body(*refs):
        if has_add:
            a_ref, b_ref, add_ref, o_ref, acc = refs
        else:
            a_ref, b_ref, o_ref, acc = refs
        p = dot(a_ref[...].astype(BF16), b_ref[...].astype(BF16))

        def finish(total):
            if has_add:
                total = add_ref[...] + total
            o_ref[...] = total.astype(out_dtype)

        if nk == 1:
            finish(p)
        else:
            kk = pl.program_id(2)

            @pl.when(kk == 0)
            def _():
                acc[...] = p

            @pl.when(kk > 0)
            def _():
                acc[...] += p

            @pl.when(kk == nk - 1)
            def _():
                finish(acc[...])

    in_specs = [a_spec, b_spec]
    args = [a, b]
    if has_add:
        in_specs.append(pl.BlockSpec((tm, tn), lambda i, j, kk: (i, j)))
        args.append(add)
    acc_shape = (tm, tn) if nk > 1 else (8, LANES)
    outs = _call(body, name=name, args=args, grid=(m // tm, n // tn, nk), in_specs=in_specs,
                 out_specs=[pl.BlockSpec((tm, tn), lambda i, j, kk: (i, j))], out_shape=[_sds((m, n), out_dtype)],
                 scratch_shapes=[pltpu.VMEM(acc_shape, F32)], rider=rider)
    return outs if rider else outs[0]


def _rms_rinv(x):
    return lax.rsqrt(jnp.mean(x * x, axis=-1, keepdims=True) + EPS)


def _rms_bwd_core(du, x, g):
    r = _rms_rinv(x)
    dug = du * g
    dx = r * (dug - x * ((r * r) * jnp.mean(dug * x, axis=-1, keepdims=True)))
    dg = jnp.sum(du * (x * r), axis=0, keepdims=True)
    return dx, dg


def _half_sum(v, lo_half):
    s0 = jnp.sum(jnp.where(lo_half, v, 0.0), axis=-1, keepdims=True)
    s1 = jnp.sum(jnp.where(lo_half, 0.0, v), axis=-1, keepdims=True)
    return jnp.where(lo_half, s0, s1)


def _head_rinv(x, lo_half):
    return lax.rsqrt(_half_sum(x * x, lo_half) * (1.0 / HD) + EPS)


def _head_norm_bwd(dn, x, g, lo_half):
    r = _head_rinv(x, lo_half)
    dng = dn * g
    dx = r * (dng - x * ((r * r) * (_half_sum(dng * x, lo_half) * (1.0 / HD))))
    dg = jnp.sum(dn * (x * r), axis=0, keepdims=True)
    return dx, dg


def _rope_swap(x, lane):
    l64 = lane & (HD - 1)
    return jnp.where(l64 < ROT // 2, pltpu.roll(x, LANES - ROT // 2, 1), pltpu.roll(x, ROT // 2, 1))


def _rope_fwd(x, cos, sin, lane):
    return x * cos + _rope_swap(x, lane) * sin


def _rope_bwd(dy, cos, sin, lane):
    return dy * cos + jnp.where((lane & (HD - 1)) < ROT, _rope_swap(dy * sin, lane), 0.0)


def _g2(g_ref):
    g = g_ref[...]
    return jnp.concatenate([g, g], axis=-1)


def _pairs(width):
    return [slice(LANES * c, LANES * (c + 1)) for c in range(width // LANES)]


def _pick_lane(block, lane, idx):
    return jnp.sum(jnp.where(lane == idx, block, 0.0), axis=-1, keepdims=True)


def _head_a(x, g, wa, gq, gk, b_pad):
    def body(x_ref, g_ref, w_ref, gq_ref, gk_ref, b_ref, u_ref, p_ref, qo_ref, ko_ref, vo_ref, c_ref, cbc_ref, carry):
        @pl.when(pl.program_id(0) == 0)
        def _():
            carry[...] = jnp.zeros_like(carry)

        xv = x_ref[...]
        u = ((xv * _rms_rinv(xv)) * g_ref[...]).astype(BF16)
        u_ref[...] = u
        for lo in range(0, NA, D):
            hi = min(lo + D, NA)
            p_ref[:, lo:hi] = _dot_nn(u, w_ref[:, lo:hi])
        lane = _lane_iota((TM, LANES))
        lo_half = lane < HD
        gq2, gk2 = _g2(gq_ref), _g2(gk_ref)
        for c in _pairs(D):
            q = p_ref[:, c]
            k = p_ref[:, D + c.start:D + c.stop]
            qo_ref[:, c] = (((q * _head_rinv(q, lo_half)) * gq2) * QSCALE).astype(BF16)
            ko_ref[:, c] = ((k * _head_rinv(k, lo_half)) * gk2).astype(BF16)
        vo_ref[...] = p_ref[:, 2 * D:3 * D].astype(BF16)

        z = p_ref[:, FOFF:FOFF + LANES] + b_ref[...]
        logf = jnp.minimum(z, 0.0) - jnp.log1p(jnp.exp(-jnp.abs(z)))
        r = lax.broadcasted_iota(jnp.int32, (TM, TM), 0)
        cc = lax.broadcasted_iota(jnp.int32, (TM, TM), 1)
        tri = (r >= cc).astype(F32)
        loc = jnp.dot(tri, logf, precision=lax.Precision.HIGHEST, preferred_element_type=F32) + carry[0:1, :]
        c_ref[...] = loc
        carry[0:1, :] = loc[TM - 1:TM, :]
        for h in range(N_HEADS):
            cbc_ref[:, LANES * h:LANES * (h + 1)] = jnp.broadcast_to(_pick_lane(loc, lane, h), (TM, LANES))

    row = lambda width: pl.BlockSpec((TM, width), lambda i: (i, 0))
    whole = lambda arr: pl.BlockSpec(arr.shape, lambda i: (0,) * arr.ndim)
    return pl.pallas_call(
        body, name="head_a", grid=(S // TM,),
        in_specs=[row(D), whole(g), whole(wa), whole(gq), whole(gk), whole(b_pad)],
        out_specs=[row(D), row(NA), row(D), row(D), row(D), row(LANES), row(N_HEADS * LANES)],
        out_shape=[_sds((S, D), BF16), _sds((S, NA), F32)] + [_sds((S, D), BF16)] * 3
        + [_sds((S, LANES), F32), _sds((S, N_HEADS * LANES), F32)],
        scratch_shapes=[pltpu.VMEM((8, LANES), F32)], compiler_params=_params())(x, g, wa, gq, gk, b_pad)


def _key_le_query(offset, keys=TK):
    r = lax.broadcasted_iota(jnp.int32, (keys, TQ), 0)
    c = lax.broadcasted_iota(jnp.int32, (keys, TQ), 1)
    return (r + offset) <= c


def _widen(tile):
    return jnp.concatenate([tile] * (TQ // LANES), axis=1)


def _fox_fwd(qn, kn, vb, proj, crow, cbc, rider=()):
    nq, per = S // TQ, TQ // TK

    def body(q_ref, k_ref, v_ref, g_ref, cq_ref, cbc_ref, o_ref, z_ref, lse_ref, st_s, pt_s):
        i = pl.program_id(1)
        qs = [q_ref[:, HD * hh:HD * (hh + 1)] for hh in range(HPS)]
        cqs = [cq_ref[hh, 0] for hh in range(HPS)]

        def scores(s, hh):
            off = pl.multiple_of(s * KS, KS)
            kj = k_ref[pl.ds(off, KS), HD * hh:HD * (hh + 1)]
            return (_dot_nt(kj, qs[hh]) + cqs[hh]) - _widen(cbc_ref[pl.ds(off, KS), LANES * hh:LANES * (hh + 1)])

        def values(s, hh, pt):
            off = pl.multiple_of(s * KS, KS)
            return _dot_tn(v_ref[pl.ds(off, KS), HD * hh:HD * (hh + 1)], pt)

        def step(s, slot, carries, mask=None, last=False):
            if not last:
                for hh in range(HPS):
                    st_s[1 - slot, hh] = scores(s + 1, hh)
            pvs = [values(jnp.maximum(s - 1, 0), hh, pt_s[1 - slot, hh]) for hh in range(HPS)]
            out = []
            for hh in range(HPS):
                m, l, acc = carries[hh]
                st = st_s[slot, hh]
                if mask is not None:
                    st = jnp.where(mask, st, -jnp.inf)
                m_new = jnp.maximum(m, jnp.max(st, axis=0, keepdims=True))
                pt = jnp.exp(st - m_new)
                alpha = jnp.exp(m - m_new)
                pt_s[slot, hh] = pt.astype(BF16)
                out.append((m_new, alpha * l + jnp.sum(pt, axis=0, keepdims=True), alpha * (acc + pvs[hh])))
            return tuple(out)

        for hh in range(HPS):
            st_s[0, hh] = scores(0, hh)
            pt_s[1, hh] = jnp.zeros((KS, TQ), BF16)
        one = (jnp.full((1, TQ), -jnp.inf, F32), jnp.zeros((1, TQ), F32), jnp.zeros((HD, TQ), F32))
        carries = lax.fori_loop(0, i, lambda t, cr: step(2 * t + 1, 1, step(2 * t, 0, cr)), (one,) * HPS)
        carries = step(2 * i, 0, carries, mask=_key_le_query(0, KS))
        carries = step(2 * i + 1, 1, carries, mask=_key_le_query(KS, KS), last=True)
        accs = []
        for hh in range(HPS):
            m, l, acc = carries[hh]
            acc = acc + values(2 * i + 1, hh, pt_s[1, hh])
            accs.append(acc / l)
            lse_ref[hh, 0] = m + jnp.log(l)
        o = jnp.concatenate(accs, axis=0).T
        o_ref[...] = o
        g = g_ref[...]
        z_ref[...] = (o * (g * _sigmoid(g))).astype(BF16)

    qblk = pl.BlockSpec((TQ, HW), lambda hp, i: (i, hp))
    full = pl.BlockSpec((S, HW), lambda hp, i: (0, hp))
    rows = pl.BlockSpec((HPS, 1, 1, TQ), lambda hp, i: (hp, i, 0, 0))
    return _call(
        body, name="fox_fwd", args=(qn, kn, vb, proj, crow, cbc), grid=(N_HEADS // HPS, nq),
        in_specs=[qblk, full, full,
                  pl.BlockSpec((TQ, HW), lambda hp, i: (i, GOFF // HW + hp)),
                  rows, pl.BlockSpec((S, HPS * LANES), lambda hp, i: (0, hp))],
        out_specs=[qblk, qblk, rows],
        out_shape=[_sds((S, D), F32), _sds((S, D), BF16), _sds((N_HEADS, nq, 1, TQ), F32)],
        scratch_shapes=[pltpu.VMEM((2, HPS, KS, TQ), F32), pltpu.VMEM((2, HPS, KS, TQ), BF16)], rider=rider)


def _fox_bwd_pre(dh, w_out, proj, o):
    nq = S // TQ

    def body(dh_ref, w_ref, g_ref, o_ref, do_ref, dg_ref, delta_ref):
        g = g_ref[...]
        sg = _sigmoid(g)
        dzv = _dot_nt(dh_ref[...].astype(BF16), w_ref[...])
        ov = o_ref[...]
        do = dzv * (g * sg)
        dg_ref[...] = (dzv * ov * (sg * (1.0 + g * (1.0 - sg)))).astype(BF16)
        do_ref[...] = do.astype(BF16)
        prod_t = (do * ov).T
        for h in range(N_HEADS):
            delta_ref[h, 0] = jnp.sum(prod_t[HD * h:HD * (h + 1), :], axis=0, keepdims=True)

    row = pl.BlockSpec((TQ, D), lambda i: (i, 0))
    return pl.pallas_call(
        body, name="fox_bwd_pre", grid=(nq,),
        in_specs=[row, pl.BlockSpec(w_out.shape, lambda i: (0, 0)), pl.BlockSpec((TQ, D), lambda i: (i, GOFF // D)), row],
        out_specs=[row, row, pl.BlockSpec((N_HEADS, 1, 1, TQ), lambda i: (0, i, 0, 0))],
        out_shape=[_sds((S, D), BF16), _sds((S, D), BF16), _sds((N_HEADS, nq, 1, TQ), F32)],
        compiler_params=_params())(dh, w_out, proj, o)


def _fox_bwd(qn, kn, vb, dob, lse, delta, crow, cbc, rider=()):
    nq, nkb = S // TQ, S // TK

    def body(q_ref, k_ref, v_ref, do_ref, lse_ref, del_ref, cq_ref, cbc_ref,
             dk_ref, dv_ref, dcs_ref, dq_ref, dr_ref, st_s, dp_s, pt_s, ds_s, dq_acc, dr_acc):
        j = pl.program_id(1)

        @pl.when(j == 0)
        def _():
            dq_acc[...] = jnp.zeros_like(dq_acc)
            dr_acc[...] = jnp.zeros_like(dr_acc)

        kjs = [k_ref[:, HD * hh:HD * (hh + 1)] for hh in range(HPS)]
        vjs = [v_ref[:, HD * hh:HD * (hh + 1)] for hh in range(HPS)]

        def rows_of(ref, u, hh):
            off = pl.multiple_of(u * TQ, TQ)
            return ref[pl.ds(off, TQ), HD * hh:HD * (hh + 1)]

        def products(u, hh):
            st = (_dot_nt(kjs[hh], rows_of(q_ref, u, hh)) + cq_ref[hh, u]) - _widen(
                cbc_ref[:, LANES * hh:LANES * (hh + 1)])
            return st, _dot_nt(vjs[hh], rows_of(do_ref, u, hh))

        def step(u, slot, carries, masked=False):
            nxt = jnp.minimum(u + 1, nq - 1)
            for hh in range(HPS):
                st_s[1 - slot, hh], dp_s[1 - slot, hh] = products(nxt, hh)
            prev = jnp.maximum(u - 1, 0)
            dvs = [_dot_nn(pt_s[1 - slot, hh], rows_of(do_ref, prev, hh)) for hh in range(HPS)]
            dks = [_dot_nn(ds_s[1 - slot, hh], rows_of(q_ref, prev, hh)) for hh in range(HPS)]
            for hh in range(HPS):
                dq_acc[hh, prev] += _dot_tn(kjs[hh], ds_s[1 - slot, hh])
            out = []
            for hh in range(HPS):
                dk, dv, dcs = carries[hh]
                st = st_s[slot, hh]
                if masked:
                    st = jnp.where(_key_le_query((j - u) * TQ), st, -jnp.inf)
                pt = jnp.exp(st - lse_ref[hh, u])
                dst = pt * (dp_s[slot, hh] - del_ref[hh, u])
                pt_s[slot, hh] = pt.astype(BF16)
                ds_s[slot, hh] = dst.astype(BF16)
                dr_acc[hh, u] += jnp.sum(dst, axis=0, keepdims=True)
                out.append((dk + dks[hh], dv + dvs[hh], dcs + (dst[:, :LANES] + dst[:, LANES:])))
            return tuple(out)

        t0 = j // 2
        for hh in range(HPS):
            st_s[0, hh], dp_s[0, hh] = products(2 * t0, hh)
            pt_s[1, hh] = jnp.zeros((TK, TQ), BF16)
            ds_s[1, hh] = jnp.zeros((TK, TQ), BF16)
        one = (jnp.zeros((TK, HD), F32), jnp.zeros((TK, HD), F32), jnp.zeros((TK, LANES), F32))
        carries = step(2 * t0 + 1, 1, step(2 * t0, 0, (one,) * HPS, masked=True), masked=True)
        carries = lax.fori_loop(t0 + 1, nq // 2, lambda t, cr: step(2 * t + 1, 1, step(2 * t, 0, cr)), carries)
        dks, dvs = [], []
        lane = _lane_iota((TK, LANES))
        dcs_all = jnp.zeros((TK, LANES), F32)
        for hh in range(HPS):
            dk, dv, dcs = carries[hh]
            dks.append(dk + _dot_nn(ds_s[1, hh], rows_of(q_ref, nq - 1, hh)))
            dvs.append(dv + _dot_nn(pt_s[1, hh], rows_of(do_ref, nq - 1, hh)))
            dq_acc[hh, nq - 1] += _dot_tn(kjs[hh], ds_s[1, hh])
            dcs_all = jnp.where(lane == HPS * pl.program_id(0) + hh, -jnp.sum(dcs, axis=-1, keepdims=True), dcs_all)
        dcs_ref[0] = dcs_all
        dk_ref[...] = jnp.concatenate(dks, axis=-1)
        dv_ref[...] = jnp.concatenate(dvs, axis=-1).astype(BF16)

        @pl.when(j == nkb - 1)
        def _():
            for i in range(nq):
                dq_ref[TQ * i:TQ * (i + 1), :] = jnp.concatenate([dq_acc[hh, i] for hh in range(HPS)], axis=0).T
            dr_ref[...] = dr_acc[...]

    kblk = pl.BlockSpec((TK, HW), lambda hp, j: (j, hp))
    full = pl.BlockSpec((S, HW), lambda hp, j: (0, hp))
    rows = pl.BlockSpec((HPS, nq, 1, TQ), lambda hp, j: (hp, 0, 0, 0))
    cblk = pl.BlockSpec((TK, HPS * LANES), lambda hp, j: (j, hp))
    return _call(
        body, name="fox_bwd", args=(qn, kn, vb, dob, lse, delta, crow, cbc), grid=(N_HEADS // HPS, nkb),
        in_specs=[full, kblk, kblk, full, rows, rows, rows, cblk],
        out_specs=[kblk, kblk, pl.BlockSpec((1, TK, LANES), lambda hp, j: (hp, j, 0)), full, rows],
        out_shape=[_sds((S, D), F32), _sds((S, D), BF16), _sds((N_HEADS // HPS, S, LANES), F32), _sds((S, D), F32),
                   _sds((N_HEADS, nq, 1, TQ), F32)],
        scratch_shapes=[pltpu.VMEM((2, HPS, TK, TQ), F32), pltpu.VMEM((2, HPS, TK, TQ), F32),
                        pltpu.VMEM((2, HPS, TK, TQ), BF16), pltpu.VMEM((2, HPS, TK, TQ), BF16),
                        pltpu.VMEM((HPS, nq, HD, TQ), F32), pltpu.VMEM((HPS, nq, 1, TQ), F32)], rider=rider)


def _prep_a_bwd(dq, dk, dv, dgate, drow, dcs, proj, b_pad, gq, gk):
    nt = S // TM

    def body(dq_ref, dk_ref, dv_ref, dgt_ref, dr_ref, dcs_ref, xq_ref, xk_ref, f_ref, b_ref, gq_ref, gk_ref,
             o_ref, dgq_ref, dgk_ref, db_ref, carry):
        @pl.when(pl.program_id(0) == 0)
        def _():
            carry[...] = jnp.zeros_like(carry)
            dgq_ref[...] = jnp.zeros_like(dgq_ref)
            dgk_ref[...] = jnp.zeros_like(dgk_ref)
            db_ref[...] = jnp.zeros_like(db_ref)

        lane = _lane_iota((TM, LANES))
        lo_half = lane < HD
        gq2, gk2 = _g2(gq_ref), _g2(gk_ref)
        dgq, dgk = jnp.zeros((1, LANES), F32), jnp.zeros((1, LANES), F32)
        for c in _pairs(D):
            dxq, dg = _head_norm_bwd(dq_ref[:, c] * QSCALE, xq_ref[:, c], gq2, lo_half)
            o_ref[:, c] = dxq.astype(BF16)
            dgq = dgq + dg
            dxk, dg = _head_norm_bwd(dk_ref[:, c], xk_ref[:, c], gk2, lo_half)
            o_ref[:, D + c.start:D + c.stop] = dxk.astype(BF16)
            dgk = dgk + dg
        dgq_ref[...] += dgq
        dgk_ref[...] += dgk
        o_ref[:, 2 * D:3 * D] = dv_ref[...]
        o_ref[:, GOFF:GOFF + D] = dgt_ref[...]

        dc = dr_ref[...]
        for group in range(N_HEADS // HPS):
            dc = dc + dcs_ref[group]
        r = lax.broadcasted_iota(jnp.int32, (TM, TM), 0)
        c = lax.broadcasted_iota(jnp.int32, (TM, TM), 1)
        tri = (c >= r).astype(F32)
        dlogf = jnp.dot(tri, dc, precision=lax.Precision.HIGHEST, preferred_element_type=F32) + carry[0:1, :]
        carry[0:1, :] = dlogf[0:1, :]
        df = dlogf * (1.0 / (1.0 + jnp.exp(f_ref[...] + b_ref[...])))
        db_ref[...] += jnp.sum(df, axis=0, keepdims=True)
        o_ref[:, FOFF:FOFF + LANES] = df.astype(BF16)
        o_ref[:, FOFF + LANES:NA] = jnp.zeros((TM, NA - FOFF - LANES), BF16)

    rev = lambda width, col: pl.BlockSpec((TM, width), lambda i: (nt - 1 - i, col))
    gspec = pl.BlockSpec((1, HD), lambda i: (0, 0))
    acc = pl.BlockSpec((1, LANES), lambda i: (0, 0))
    return pl.pallas_call(
        body, name="prep_a_bwd", grid=(nt,),
        in_specs=[rev(D, 0), rev(D, 0), rev(D, 0), rev(D, 0), rev(LANES, 0),
                  pl.BlockSpec((N_HEADS // HPS, TM, LANES), lambda i: (0, nt - 1 - i, 0)),
                  rev(D, 0), rev(D, 1), rev(LANES, FOFF // LANES), acc, gspec, gspec],
        out_specs=[rev(NA, 0), acc, acc, acc],
        out_shape=[_sds((S, NA), BF16)] + [_sds((1, LANES), F32)] * 3,
        scratch_shapes=[pltpu.VMEM((8, LANES), F32)],
        compiler_params=_params())(dq, dk, dv, dgate, drow, dcs, proj, proj, proj, b_pad, gq, gk)


def _head_b(x, z_a, w_out_a, g_kv, g_b, w_kv, w_in_b, gq, gk, cos2, sin2):
    nkv = w_kv.shape[1] // 2

    def body(x_ref, z_ref, wo_ref, gkv_ref, gb_ref, wkv_ref, wb_ref, gq_ref, gk_ref, c_ref, s_ref,
             h_ref, ukv_ref, ub_ref, kv_ref, pb_ref, qo_ref, ko_ref, vo_ref):
        xv = x_ref[...] + _dot_nn(z_ref[...], wo_ref[...])
        h_ref[...] = xv
        xn = xv * _rms_rinv(xv)
        ukv = (xn * gkv_ref[...]).astype(BF16)
        ub = (xn * gb_ref[...]).astype(BF16)
        ukv_ref[...] = ukv
        ub_ref[...] = ub
        kv_ref[...] = _dot_nn(ukv, wkv_ref[...])
        for lo in range(0, 2 * D, D):
            pb_ref[:, lo:lo + D] = _dot_nn(ub, wb_ref[:, lo:lo + D])
        lane = _lane_iota((TM, LANES))
        lo_half = lane < HD
        cos, sin = c_ref[...], s_ref[...]
        gq2, gk2 = _g2(gq_ref), _g2(gk_ref)
        for c in _pairs(D):
            q = pb_ref[:, c]
            qo_ref[:, c] = (_rope_fwd((q * _head_rinv(q, lo_half)) * gq2, cos, sin, lane) * QSCALE).astype(BF16)
        for c in _pairs(nkv):
            k = kv_ref[:, c]
            ko_ref[:, c] = _rope_fwd((k * _head_rinv(k, lo_half)) * gk2, cos, sin, lane).astype(BF16)
        vo_ref[...] = kv_ref[:, nkv:2 * nkv].astype(BF16)

    row = lambda width: pl.BlockSpec((TM, width), lambda i: (i, 0))
    whole = lambda arr: pl.BlockSpec(arr.shape, lambda i: (0,) * arr.ndim)
    return pl.pallas_call(
        body, name="head_b", grid=(S // TM,),
        in_specs=[row(D), row(D), whole(w_out_a), whole(g_kv), whole(g_b), whole(w_kv), whole(w_in_b), whole(gq),
                  whole(gk), row(LANES), row(LANES)],
        out_specs=[row(D), row(D), row(D), row(2 * nkv), row(2 * D), row(D), row(nkv), row(nkv)],
        out_shape=[_sds((S, D), F32), _sds((S, D), BF16), _sds((S, D), BF16), _sds((S, 2 * nkv), F32),
                   _sds((S, 2 * D), F32), _sds((S, D), BF16), _sds((S, nkv), BF16), _sds((S, nkv), BF16)],
        compiler_params=_params())(x, z_a, w_out_a, g_kv, g_b, w_kv, w_in_b, gq, gk, cos2, sin2)


N_KV, GRP = 4, 4


def _swa_mask(n):
    r = lax.broadcasted_iota(jnp.int32, (2 * WIN, GRP * WIN), 0)
    q = lax.broadcasted_iota(jnp.int32, (2 * WIN, GRP * WIN), 1) & (WIN - 1)
    return (r > q) & (r <= q + WIN) & ((r >= WIN) | (n > 0))


def _stack4(ref_or_val, base):
    return jnp.concatenate([ref_or_val[:, base + HD * g: base + HD * (g + 1)] for g in range(GRP)], axis=0)


def _unstack4(xt):
    return jnp.concatenate([xt[:, WIN * g:WIN * (g + 1)] for g in range(GRP)], axis=0).T


def _band(prev_ref, cur_ref, kh):
    return jnp.concatenate([prev_ref[:, HD * kh:HD * (kh + 1)], cur_ref[:, HD * kh:HD * (kh + 1)]], axis=0)


def _sink_row(s_ref, first):
    lane = _lane_iota((1, GRP * WIN))
    row = jnp.full((1, GRP * WIN), s_ref[first + GRP - 1], F32)
    for g in range(GRP - 2, -1, -1):
        row = jnp.where(lane < WIN * (g + 1), s_ref[first + g], row)
    return row


def _swa_fwd(qb, ksh, vsh, pb, sinks):
    nb = S // WIN

    def body(q_ref, kp_ref, kc_ref, vp_ref, vc_ref, g_ref, s_ref, o_ref, z_ref, lse_ref):
        n = pl.program_id(0)
        valid = _swa_mask(n)
        outs = []
        for kh in range(N_KV):
            kb, vb = _band(kp_ref, kc_ref, kh), _band(vp_ref, vc_ref, kh)
            st = jnp.where(valid, _dot_nt(kb, _stack4(q_ref, GRP * HD * kh)), -jnp.inf)
            sink = _sink_row(s_ref, GRP * kh)
            m = jnp.maximum(jnp.max(st, axis=0, keepdims=True), sink)
            pt = jnp.exp(st - m)
            l = jnp.sum(pt, axis=0, keepdims=True) + jnp.exp(sink - m)
            outs.append(_unstack4(_dot_tn(vb, pt.astype(BF16)) / l))
            lse = m + jnp.log(l)
            for g in range(GRP):
                lse_ref[GRP * kh + g, 0] = lse[:, WIN * g:WIN * (g + 1)]
        o = jnp.concatenate(outs, axis=-1)
        o_ref[...] = o
        g = g_ref[...]
        z_ref[...] = (o * (g * _sigmoid(g))).astype(BF16)

    row = pl.BlockSpec((WIN, D), lambda n: (n, 0))
    prev = pl.BlockSpec((WIN, N_KV * HD), lambda n: (jnp.maximum(n - 1, 0), 0))
    cur = pl.BlockSpec((WIN, N_KV * HD), lambda n: (n, 0))
    return pl.pallas_call(
        body, name="swa_fwd", grid=(nb,),
        in_specs=[row, prev, cur, prev, cur, pl.BlockSpec((WIN, D), lambda n: (n, 1)),
                  pl.BlockSpec(memory_space=pltpu.SMEM)],
        out_specs=[row, row, pl.BlockSpec((N_HEADS, 1, 1, WIN), lambda n: (0, n, 0, 0))],
        out_shape=[_sds((S, D), F32), _sds((S, D), BF16), _sds((N_HEADS, nb, 1, WIN), F32)],
        compiler_params=_params())(qb, ksh, ksh, vsh, vsh, pb, sinks)


def _swa_bwd(qb, ksh, vsh, dz, o, lse, pb, sinks, gq, cos2, sin2):
    nb = S // WIN

    def body(q_ref, kp_ref, kc_ref, vp_ref, vc_ref, dz_ref, o_ref, lse_ref, x_ref, g_ref, s_ref, gq_ref, c_ref, sn_ref,
             dpb_ref, dka_ref, dkb_ref, dva_ref, dvb_ref, dsink_ref, dgq_ref):
        n = pl.program_id(0)

        @pl.when(n == 0)
        def _():
            dsink_ref[...] = jnp.zeros_like(dsink_ref)
            dgq_ref[...] = jnp.zeros_like(dgq_ref)

        valid = _swa_mask(n)
        g = g_ref[...]
        sg = _sigmoid(g)
        dzv = dz_ref[...]
        ov = o_ref[...]
        do = dzv * (g * sg)
        dpb_ref[:, D:2 * D] = (dzv * ov * (sg * (1.0 + g * (1.0 - sg)))).astype(BF16)
        prod_t = (do * ov).T
        lane1 = _lane_iota((1, LANES))
        dqs, dkas, dkbs, dvas, dvbs = [], [], [], [], []
        dsink = jnp.zeros((1, LANES), F32)
        for kh in range(N_KV):
            kb, vb = _band(kp_ref, kc_ref, kh), _band(vp_ref, vc_ref, kh)
            base = GRP * HD * kh
            qs = _stack4(q_ref, base)
            dos = _stack4(do, base).astype(BF16)
            delta = jnp.concatenate(
                [jnp.sum(prod_t[base + HD * gg:base + HD * (gg + 1), :], axis=0, keepdims=True)
                 for gg in range(GRP)], axis=1)
            lse = jnp.concatenate([lse_ref[GRP * kh + gg, 0] for gg in range(GRP)], axis=1)
            st = jnp.where(valid, _dot_nt(kb, qs), -jnp.inf)
            pt = jnp.exp(st - lse)
            dst = pt * (_dot_nt(vb, dos) - delta)
            dsb = dst.astype(BF16)
            dqs.append(_unstack4(_dot_tn(kb, dsb)))
            dkband = _dot_nn(dsb, qs)
            dvband = _dot_nn(pt.astype(BF16), dos)
            dkbs.append(dkband[0:WIN, :])
            dkas.append(dkband[WIN:2 * WIN, :])
            dvbs.append(dvband[0:WIN, :])
            dvas.append(dvband[WIN:2 * WIN, :])
            ps_delta = jnp.exp(_sink_row(s_ref, GRP * kh) - lse) * delta
            for gg in range(GRP):
                val = jnp.sum(ps_delta[:, WIN * gg:WIN * (gg + 1)], axis=1, keepdims=True)
                dsink = dsink - jnp.where(lane1 == GRP * kh + gg, val, 0.0)
        dka_ref[...] = jnp.concatenate(dkas, axis=-1)
        dkb_ref[...] = jnp.concatenate(dkbs, axis=-1)
        dva_ref[...] = jnp.concatenate(dvas, axis=-1)
        dvb_ref[...] = jnp.concatenate(dvbs, axis=-1)
        dsink_ref[...] += dsink

        lane = _lane_iota((WIN, LANES))
        g2, cos, sin = _g2(gq_ref), c_ref[...], sn_ref[...]
        dg_tot = jnp.zeros((1, LANES), F32)
        for kh in range(N_KV):
            for c in _pairs(GRP * HD):
                cols = slice(GRP * HD * kh + c.start, GRP * HD * kh + c.stop)
                dn = _rope_bwd(dqs[kh][:, c] * QSCALE, cos, sin, lane)
                dx, dg = _head_norm_bwd(dn, x_ref[:, cols], g2, lane < HD)
                dpb_ref[:, cols] = dx.astype(BF16)
                dg_tot = dg_tot + dg
        dgq_ref[...] += dg_tot

    row = pl.BlockSpec((WIN, D), lambda n: (n, 0))
    prev = pl.BlockSpec((WIN, N_KV * HD), lambda n: (jnp.maximum(n - 1, 0), 0))
    cur = pl.BlockSpec((WIN, N_KV * HD), lambda n: (n, 0))
    acc = pl.BlockSpec((1, LANES), lambda n: (0, 0))
    tab = pl.BlockSpec((WIN, LANES), lambda n: (n, 0))
    return pl.pallas_call(
        body, name="swa_bwd", grid=(nb,),
        in_specs=[row, prev, cur, prev, cur, row, row, pl.BlockSpec((N_HEADS, 1, 1, WIN), lambda n: (0, n, 0, 0)),
                  row, pl.BlockSpec((WIN, D), lambda n: (n, 1)), pl.BlockSpec(memory_space=pltpu.SMEM),
                  pl.BlockSpec((1, HD), lambda n: (0, 0)), tab, tab],
        out_specs=[pl.BlockSpec((WIN, 2 * D), lambda n: (n, 0)), cur, cur, cur, cur, acc, acc],
        out_shape=[_sds((S, 2 * D), BF16)] + [_sds((S, 256), F32)] * 4 + [_sds((1, LANES), F32)] * 2,
        compiler_params=_params())(qb, ksh, ksh, vsh, vsh, dz, o, lse, pb, pb, sinks, gq, cos2, sin2)


def _prep_kv_bwd(dka, dkb, dva, dvb, kv, gk, cos2, sin2):
    nt = S // RB
    per = RB // WIN

    def shifted(cur_ref, nxt_ref, has_next):
        return jnp.concatenate([cur_ref[WIN:RB, :], jnp.where(has_next, nxt_ref[...], 0.0)], axis=0)

    def body(dka_ref, dkb_ref, dkn_ref, dva_ref, dvb_ref, dvn_ref, x_ref, g_ref, c_ref, s_ref, o_ref, dgk_ref):
        i, j = pl.program_id(0), pl.program_id(1)

        @pl.when((i == 0) & (j == 0))
        def _():
            dgk_ref[...] = jnp.zeros_like(dgk_ref)

        has_next = i < nt - 1

        @pl.when(j == 0)
        def _():
            lane = _lane_iota((RB, LANES))
            g2, cos, sin = _g2(g_ref), c_ref[...], s_ref[...]
            dy_all = dka_ref[...] + shifted(dkb_ref, dkn_ref, has_next)
            dg_tot = jnp.zeros((1, LANES), F32)
            for c in _pairs(CB):
                dn = _rope_bwd(dy_all[:, c], cos, sin, lane)
                dx, dg = _head_norm_bwd(dn, x_ref[:, c], g2, lane < HD)
                o_ref[:, c] = dx.astype(BF16)
                dg_tot = dg_tot + dg
            dgk_ref[...] += dg_tot

        @pl.when(j == 1)
        def _():
            o_ref[...] = (dva_ref[...] + shifted(dvb_ref, dvn_ref, has_next)).astype(BF16)

    cur = pl.BlockSpec((RB, CB), lambda i, j: (i, 0))
    nxt = pl.BlockSpec((WIN, CB), lambda i, j: (jnp.minimum(per * (i + 1), S // WIN - 1), 0))
    tab = pl.BlockSpec((RB, LANES), lambda i, j: (i, 0))
    return pl.pallas_call(
        body, name="prep_kv_bwd", grid=(nt, 2),
        in_specs=[cur, cur, nxt, cur, cur, nxt, cur, pl.BlockSpec((1, HD), lambda i, j: (0, 0)), tab, tab],
        out_specs=[pl.BlockSpec((RB, CB), lambda i, j: (i, j)), pl.BlockSpec((1, LANES), lambda i, j: (0, 0))],
        out_shape=[_sds((S, 2 * CB), BF16), _sds((1, LANES), F32)],
        compiler_params=_params())(dka, dkb, dkb, dva, dvb, dvb, kv, gk, cos2, sin2)


def _out_b_loss(z, w_out, h1, tgt):
    tm = 512

    def body(z_ref, w_ref, h_ref, t_ref, dy_ref, l_ref):
        @pl.when(pl.program_id(0) == 0)
        def _():
            l_ref[...] = jnp.zeros_like(l_ref)

        e = (h_ref[...] + _dot_nn(z_ref[...], w_ref[...])) - t_ref[...]
        dy_ref[...] = e * (1.0 / D)
        l_ref[...] += jnp.sum(jnp.sum(e * e, axis=-1, keepdims=True), axis=0, keepdims=True)

    row = pl.BlockSpec((tm, D), lambda i: (i, 0))
    return pl.pallas_call(
        body, name="out_b_loss", grid=(S // tm,),
        in_specs=[row, pl.BlockSpec(w_out.shape, lambda i: (0, 0)), row, row],
        out_specs=[row, pl.BlockSpec((1, LANES), lambda i: (0, 0))],
        out_shape=[_sds((S, D), F32), _sds((1, LANES), F32)], compiler_params=_params())(z, w_out, h1, tgt)


def _du_a_rms_bwd(dproj, wa, x, g, dres, after):
    tm, tk = 512, NA // 2
    nk = NA // tk

    def body(a_ref, b_ref, x_ref, g_ref, dr_ref, after_ref, dx_ref, dg_ref, acc):
        i, kk = pl.program_id(0), pl.program_id(1)

        @pl.when((i == 0) & (kk == 0))
        def _():
            dg_ref[...] = jnp.zeros_like(dg_ref)

        p = _dot_nt(a_ref[...], b_ref[...])

        @pl.when(kk == 0)
        def _():
            acc[...] = p

        @pl.when(kk == nk - 1)
        def _():
            dx, dg = _rms_bwd_core(acc[...] + p, x_ref[...], g_ref[...])
            dx_ref[...] = dr_ref[...] + dx
            dg_ref[...] += dg

    assert nk == 2
    row = pl.BlockSpec((tm, D), lambda i, kk: (i, 0))
    vec = pl.BlockSpec((1, D), lambda i, kk: (0, 0))
    return pl.pallas_call(
        body, name="du_a_rms_bwd", grid=(S // tm, nk),
        in_specs=[pl.BlockSpec((tm, tk), lambda i, kk: (i, kk)), pl.BlockSpec((D, tk), lambda i, kk: (0, kk)),
                  row, vec, row, pl.BlockSpec(after.shape, lambda i, kk: (0, 0))],
        out_specs=[row, vec], out_shape=[_sds((S, D), F32), _sds((1, D), F32)],
        scratch_shapes=[pltpu.VMEM((tm, D), F32)], compiler_params=_params())(dproj, wa, x, g, dres, after)


def _du_b_rms_bwd(dpb, w_in_b, dkv, w_kv, h1, g_b, g_kv, dy):
    tm = 512

    def body(ab_ref, wb_ref, akv_ref, wkv_ref, x_ref, gb_ref, gkv_ref, dy_ref, dh_ref, dgb_ref, dgkv_ref):
        @pl.when(pl.program_id(0) == 0)
        def _():
            dgb_ref[...] = jnp.zeros_like(dgb_ref)
            dgkv_ref[...] = jnp.zeros_like(dgkv_ref)

        x = x_ref[...]
        dx1, dg1 = _rms_bwd_core(_dot_nt(ab_ref[...], wb_ref[...]), x, gb_ref[...])
        dx2, dg2 = _rms_bwd_core(_dot_nt(akv_ref[...], wkv_ref[...]), x, gkv_ref[...])
        dh_ref[...] = dy_ref[...] + dx1 + dx2
        dgb_ref[...] += dg1
        dgkv_ref[...] += dg2

    row = lambda width: pl.BlockSpec((tm, width), lambda i: (i, 0))
    whole = lambda arr: pl.BlockSpec(arr.shape, lambda i: (0, 0))
    vec = pl.BlockSpec((1, D), lambda i: (0, 0))
    return pl.pallas_call(
        body, name="du_b_rms_bwd", grid=(S // tm,),
        in_specs=[row(dpb.shape[1]), whole(w_in_b), row(dkv.shape[1]), whole(w_kv), row(D), vec, vec, row(D)],
        out_specs=[row(D), vec, vec], out_shape=[_sds((S, D), F32), _sds((1, D), F32), _sds((1, D), F32)],
        compiler_params=_params())(dpb, w_in_b, dkv, w_kv, h1, g_b, g_kv, dy)


def _gather_first(w_in_a, w_out_a, w_kv, w_in_b, w_out_b, norm_a_g):
    def body(wia_ref, woa_ref, wkv_ref, wib_ref, wob_ref, ga_ref,
             wa_g, ga_g, woa_s, wkv_s, wib_s, wob_s, wa_s, st_a, st_oa, st_kv, st_ib, st_ob, load_sems, *sems):
        sources = [wia_ref.at[0], woa_ref.at[0], wkv_ref, wib_ref.at[0], wob_ref.at[0]]
        stages = [st_a, st_oa, st_kv, st_ib, st_ob]
        loads = [pltpu.make_async_copy(src, dst, load_sems.at[i]) for i, (src, dst) in enumerate(zip(sources, stages))]
        for cp in loads:
            cp.start()
        loads[0].wait()
        wa_s[...] = st_a[...].astype(BF16)

        def cast_the_rest():
            for cp, stage, out in zip(loads[1:], stages[1:], [woa_s, wkv_s, wib_s, wob_s]):
                cp.wait()
                out[...] = stage[...].astype(BF16)

        _gather_two_level([wa_s, ga_ref], [wa_g, ga_g], sems, meanwhile=cast_the_rest)

    vmem = pl.BlockSpec(memory_space=pltpu.VMEM)
    anyspec = pl.BlockSpec(memory_space=pl.ANY)
    shard = lambda w: _sds(w.shape[-2:], BF16)
    stage = lambda w: pltpu.VMEM(w.shape[-2:], F32)
    return pl.pallas_call(
        body, name="gather_first", in_specs=[anyspec] * 5 + [vmem],
        out_specs=[anyspec, anyspec, vmem, vmem, vmem, vmem],
        out_shape=[_sds((N_DEV,) + w_in_a.shape[-2:], BF16), _sds((N_DEV,) + norm_a_g.shape, F32),
                   shard(w_out_a), shard(w_kv), shard(w_in_b), shard(w_out_b)],
        scratch_shapes=[pltpu.VMEM(w_in_a.shape[-2:], BF16), stage(w_in_a), stage(w_out_a), stage(w_kv),
                        stage(w_in_b), stage(w_out_b), pltpu.SemaphoreType.DMA((5,))] + _exchange_sems(2),
        compiler_params=pltpu.CompilerParams(vmem_limit_bytes=VMEM_LIMIT, has_side_effects=True))(
            w_in_a, w_out_a, w_kv, w_in_b, w_out_b, norm_a_g)


def _pair_reduce(slots):
    n_chip = N_DEV // 2
    _, rows, cols = slots.shape

    def body(s_ref, o_ref, own_v, sib_v, send_sems, recv_sems, local_sems):
        x, y, c = lax.axis_index("x"), lax.axis_index("y"), lax.axis_index("c")
        copies = []
        for j in range(n_chip):
            own = pltpu.make_async_copy(s_ref.at[2 * j + c], own_v.at[j], local_sems.at[j])
            give = pltpu.make_async_remote_copy(
                src_ref=s_ref.at[2 * j + 1 - c], dst_ref=sib_v.at[j], send_sem=send_sems.at[j],
                recv_sem=recv_sems.at[j], device_id=(x, y, 1 - c), device_id_type=pl.DeviceIdType.MESH)
            own.start()
            give.start()
            copies.append((own, give))
        for j, (own, give) in enumerate(copies):
            own.wait()
            give.wait()
            o_ref[j] = (own_v[j].astype(F32) + sib_v[j].astype(F32)).astype(BF16)

    half = _sds((n_chip, rows, cols), slots.dtype)
    return pl.pallas_call(
        body, name="pair_reduce", in_specs=[pl.BlockSpec(memory_space=pl.ANY)],
        out_specs=pl.BlockSpec(memory_space=pltpu.VMEM), out_shape=half,
        scratch_shapes=[pltpu.VMEM(half.shape, half.dtype), pltpu.VMEM(half.shape, half.dtype),
                        pltpu.SemaphoreType.DMA((n_chip,)), pltpu.SemaphoreType.DMA((n_chip,)),
                        pltpu.SemaphoreType.DMA((n_chip,))],
        compiler_params=pltpu.CompilerParams(vmem_limit_bytes=VMEM_LIMIT, has_side_effects=True))(slots)


def _padded_col(c):
    if c < RAW_F:
        return c
    return FOFF + (c - RAW_F) if c < RAW_G else GOFF + (c - RAW_G)


def _shard_pieces():
    width = NA_RAW // N_DEV
    pieces = []
    for d in range(N_DEV):
        cuts = [width * d] + [c for c in (RAW_F, RAW_G) if width * d < c < width * (d + 1)] + [width * (d + 1)]
        for lo, hi in zip(cuts[:-1], cuts[1:]):
            pieces.append((d, lo - width * d, _padded_col(lo), hi - lo))
    return pieces


def _unshard_wa(wa_g):
    def body(w_ref, o_ref):
        o_ref[:, FOFF + N_HEADS:NA] = jnp.zeros((TM, NA - FOFF - N_HEADS), BF16)
        for d, src, dst, width in _shard_pieces():
            o_ref[:, dst:dst + width] = w_ref[d, :, src:src + width]

    return pl.pallas_call(
        body, name="unshard_wa", grid=(D // TM,),
        in_specs=[pl.BlockSpec((N_DEV, TM, NA_RAW // N_DEV), lambda i: (0, i, 0))],
        out_specs=pl.BlockSpec((TM, NA), lambda i: (i, 0)), out_shape=_sds((D, NA), BF16),
        compiler_params=_params())(wa_g)


def _reshard_dwa(dwa):
    def body(g_ref, o_ref):
        for d, src, dst, width in _shard_pieces():
            o_ref[d, :, src:src + width] = g_ref[:, dst:dst + width]

    return pl.pallas_call(
        body, name="reshard_dwa", grid=(D // TM,), in_specs=[pl.BlockSpec((TM, NA), lambda i: (i, 0))],
        out_specs=pl.BlockSpec((N_DEV, TM, NA_RAW // N_DEV), lambda i: (0, i, 0)),
        out_shape=_sds((N_DEV, D, NA_RAW // N_DEV), dwa.dtype), compiler_params=_params())(dwa)


CHIP_FLIPS = (2, 4, 6)


def _chip_exchange_start(partial):
    n = len(CHIP_FLIPS)

    def body(p_ref, land_ref, *rest):
        sends, recvs, token = rest[:n], rest[n:2 * n], rest[2 * n + 2]
        x, y, c = lax.axis_index("x"), lax.axis_index("y"), lax.axis_index("c")
        me = 4 * x + 2 * y + c
        for idx, k in enumerate(CHIP_FLIPS):
            pltpu.make_async_remote_copy(
                src_ref=p_ref.at[(me ^ k) >> 1], dst_ref=land_ref.at[me >> 1], send_sem=sends[idx],
                recv_sem=recvs[idx], device_id=(x ^ ((k >> 2) & 1), y ^ ((k >> 1) & 1), c),
                device_id_type=pl.DeviceIdType.MESH).start()
        token[...] = jnp.zeros_like(token)

    hbm = pl.BlockSpec(memory_space=pltpu.HBM)
    sem = pl.BlockSpec(memory_space=pltpu.SEMAPHORE)
    buf = pltpu.HBM(partial.shape, partial.dtype)
    return pl.pallas_call(
        body, name="chip_exchange_start",
        out_shape=(pltpu.SemaphoreType.DMA(()),) * (2 * n) + (buf, buf, _sds((8, LANES), F32)),
        in_specs=(hbm, hbm), out_specs=(sem,) * (2 * n) + (hbm, hbm, pl.BlockSpec(memory_space=pltpu.VMEM)),
        input_output_aliases={0: 2 * n, 1: 2 * n + 1},
        compiler_params=pltpu.CompilerParams(has_side_effects=pltpu.SideEffectType.DATAFLOW_SIDE_EFFECTING))(
            pltpu.with_memory_space_constraint(partial, pltpu.HBM),
            pltpu.with_memory_space_constraint(lax.empty(partial.shape, partial.dtype), pltpu.HBM))


def _chip_exchange_wait(started, after):
    n = len(CHIP_FLIPS)
    sems, (p_thru, land_thru) = started[:2 * n], started[2 * n:2 * n + 2]

    def body(p_ref, land_ref, *rest):
        sends, recvs = rest[:n], rest[n:2 * n]
        x, y, c = lax.axis_index("x"), lax.axis_index("y"), lax.axis_index("c")
        me = 4 * x + 2 * y + c
        for idx, k in enumerate(CHIP_FLIPS):
            copy = pltpu.make_async_remote_copy(
                src_ref=p_ref.at[(me ^ k) >> 1], dst_ref=land_ref.at[(me ^ k) >> 1], send_sem=sends[idx],
                recv_sem=recvs[idx], device_id=(x ^ ((k >> 2) & 1), y ^ ((k >> 1) & 1), c),
                device_id_type=pl.DeviceIdType.MESH)
            copy.wait_send()
            copy.wait_recv()

    hbm = pl.BlockSpec(memory_space=pltpu.HBM)
    sem = pl.BlockSpec(memory_space=pltpu.SEMAPHORE)
    buf = pltpu.HBM(p_thru.shape, p_thru.dtype)
    return pl.pallas_call(
        body, name="chip_exchange_wait", out_shape=(buf, buf),
        in_specs=(hbm, hbm) + (sem,) * (2 * n) + (pl.BlockSpec(memory_space=pl.ANY),), out_specs=(hbm, hbm),
        input_output_aliases={0: 0, 1: 1},
        compiler_params=pltpu.CompilerParams(has_side_effects=pltpu.SideEffectType.DATAFLOW_SIDE_EFFECTING))(
            p_thru, land_thru, *sems, after)


def _gather_slab(slab):
    def body(s_ref, o_ref, *sems):
        _exchange_ops(["gather_rows"], [s_ref], [o_ref], sems, True, True)

    anyspec = pl.BlockSpec(memory_space=pl.ANY)
    return pl.pallas_call(
        body, name="gather_slab", in_specs=[anyspec], out_specs=anyspec,
        out_shape=_sds((N_DEV,) + slab.shape, slab.dtype), scratch_shapes=_exchange_sems(1),
        compiler_params=pltpu.CompilerParams(has_side_effects=True))(slab)


def _adamw(w, g, m, v):
    m = ADAM_B1 * m + (1.0 - ADAM_B1) * g
    v = ADAM_B2 * v + (1.0 - ADAM_B2) * (g * g)
    m_hat = m / (1.0 - ADAM_B1 ** ADAM_STEP)
    v_hat = v / (1.0 - ADAM_B2 ** ADAM_STEP)
    delta = -ADAM_LR * (m_hat / (jnp.sqrt(v_hat) + ADAM_EPS) + ADAM_WD * w)
    return delta, m, v


def _sum_adamw(recv, w, m, v, name):
    lead = w.ndim - 2
    rows, cols = w.shape[-2:]
    tr = 128
    slabs = list(recv) if isinstance(recv, tuple) else [recv]
    n_slots = slabs[0].shape[0]

    def body(*refs):
        r_ref, own_ref = refs[0], refs[len(slabs) - 1]
        w_ref, m_ref, v_ref, g_ref, d_ref, nm_ref, nv_ref = refs[len(slabs):]
        chip = (4 * lax.axis_index("x") + 2 * lax.axis_index("y") + lax.axis_index("c")) >> 1
        g = None
        for slot in range(n_slots):
            part = r_ref[slot]
            if len(slabs) == 2:
                part = jnp.where(chip == slot, own_ref[slot], part)
            g = part.astype(F32) if g is None else g + part.astype(F32)
        g_ref[...] = g
        d_ref[...], nm_ref[...], nv_ref[...] = _adamw(w_ref[...], g, m_ref[...], v_ref[...])

    blk = pl.BlockSpec((None,) * lead + (tr, cols), lambda i: (0,) * lead + (i, 0))
    slots = pl.BlockSpec((n_slots, tr, cols), lambda i: (0, i, 0))
    return pl.pallas_call(
        body, name=name, grid=(rows // tr,), in_specs=[slots] * len(slabs) + [blk, blk, blk],
        out_specs=[blk] * 4, out_shape=[_sds(w.shape, F32)] * 4, compiler_params=_params())(*slabs, w, m, v)


SLAB_ROWS = 16
SLOT = {"kv_norm_g": (8, 0, D), "norm_b_g": (9, 0, D), "b_forget": (10, 0, 16), "qnorm_a_g": (10, 128, HD),
        "knorm_a_g": (10, 256, HD), "knorm_b_g": (10, 384, HD), "qnorm_b_g": (10, 512, HD), "sinks": (10, 640, 16)}
SMALL = ["norm_a_g", "b_forget", "qnorm_a_g", "knorm_a_g", "kv_norm_g", "knorm_b_g", "norm_b_g", "qnorm_b_g", "sinks"]


LOSS_ROW = 11


def _pack_small(dg_a, dg_kv, dg_b, db_f, dgq_a, dgk_a, dgk_b, dgq_b, dsinks, lsum):
    def fold(ref):
        return ref[:, 0:HD] + ref[:, HD:2 * HD]

    def body(dga_ref, dgkv_ref, dgb_ref, dbf_ref, dgqa_ref, dgka_ref, dgkb_ref, dgqb_ref, dsk_ref, ls_ref, slab_ref):
        slab_ref[...] = jnp.zeros_like(slab_ref)
        for r in range(N_DEV):
            slab_ref[r:r + 1, 0:LANES] = dga_ref[:, LANES * r:LANES * (r + 1)]
        slab_ref[8:9, :] = dgkv_ref[...]
        slab_ref[9:10, :] = dgb_ref[...]
        slab_ref[10:11, 0:LANES] = dbf_ref[...]
        slab_ref[10:11, 128:128 + HD] = fold(dgqa_ref)
        slab_ref[10:11, 256:256 + HD] = fold(dgka_ref)
        slab_ref[10:11, 384:384 + HD] = fold(dgkb_ref)
        slab_ref[10:11, 512:512 + HD] = fold(dgqb_ref)
        slab_ref[10:11, 640:640 + LANES] = dsk_ref[...]
        slab_ref[LOSS_ROW:LOSS_ROW + 1, 0:LANES] = ls_ref[...]

    return pl.pallas_call(body, name="pack_small", out_shape=_sds((SLAB_ROWS, D), F32), compiler_params=_params())(
        dg_a, dg_kv, dg_b, db_f, dgq_a, dgk_a, dgk_b, dgq_b, dsinks, lsum)


def _small_adamw(recv, ws, ms, vs):
    k = len(SMALL)

    def body(*refs):
        r_ref = refs[0]
        w_refs, m_refs, v_refs = refs[1:1 + k], refs[1 + k:1 + 2 * k], refs[1 + 2 * k:1 + 3 * k]
        outs = refs[1 + 3 * k:1 + 7 * k]
        loss_ref, tot = refs[1 + 7 * k], refs[2 + 7 * k]
        g = r_ref[0]
        for dev in range(1, N_DEV):
            g = g + r_ref[dev]
        tot[...] = g
        loss_ref[...] = tot[LOSS_ROW:LOSS_ROW + 1, 0:LANES] * (0.5 / D)
        me = 4 * lax.axis_index("x") + 2 * lax.axis_index("y") + lax.axis_index("c")
        for p, name in enumerate(SMALL):
            if name == "norm_a_g":
                mine = lax.broadcasted_iota(jnp.int32, (N_DEV, LANES), 0) == me
                gp = jnp.sum(jnp.where(mine, tot[0:N_DEV, 0:LANES], 0.0), axis=0, keepdims=True)
            else:
                row, lo, width = SLOT[name]
                gp = tot[row:row + 1, lo:lo + width]
            d, nm, nv = _adamw(w_refs[p][...], gp, m_refs[p][...], v_refs[p][...])
            outs[p][...] = gp
            outs[k + p][...] = d
            outs[2 * k + p][...] = nm
            outs[3 * k + p][...] = nv

    shapes = [_sds(w.shape, F32) for w in ws]
    return pl.pallas_call(body, name="small_adamw", out_shape=shapes * 4 + [_sds((1, LANES), F32)],
                          scratch_shapes=[pltpu.VMEM((SLAB_ROWS, D), F32)],
                          compiler_params=_params())(recv, *ws, *ms, *vs)


def _rope_tables(positions):
    inv_freq = jnp.power(jnp.float32(ROPE_THETA), -jnp.arange(0, ROT, 2, dtype=F32) / ROT)
    ang = positions.astype(F32)[:, None] * inv_freq[None, :]
    cos, sin = jnp.cos(ang), jnp.sin(ang)
    c64 = jnp.concatenate([cos, cos, jnp.ones((S, HD - ROT), F32)], axis=-1)
    s64 = jnp.concatenate([-sin, sin, jnp.zeros((S, HD - ROT), F32)], axis=-1)
    return jnp.tile(c64, (1, 2)), jnp.tile(s64, (1, 2))


def _local_step(x, tgt, positions, g_a, wa, b_forget, gq_a, gk_a, g_kv, gk_b, g_b, gq_b, sinks,
                woa_s, wkv_s, wib_s, wob_s):
    nq = S // TQ
    cos2, sin2 = _rope_tables(positions)
    b_pad = jnp.pad(b_forget, ((0, 0), (0, LANES - N_HEADS)))

    u_a, proj, qn, kn, vb, ccol, cbc = _head_a(x, g_a, wa, gq_a, gk_a, b_pad)
    crow = ccol[:, :N_HEADS].T.reshape(N_HEADS, nq, 1, TQ)
    o_a, z_a, lse_a, woa_g, wkv_g, w_in_b, wob_g = _fox_fwd(
        qn, kn, vb, proj, crow, cbc,
        rider=[("gather_rows", woa_s), ("gather_rows", wkv_s), ("gather_cols", wib_s), ("gather_rows", wob_s)])
    w_out_a, w_kv, w_out_b = woa_g.reshape(D, D), wkv_g.reshape(D, 512), wob_g.reshape(D, D)
    h1, u_kv, u_b, kv, pb, qb, ksh, vsh = _head_b(x, z_a, w_out_a, g_kv, g_b, w_kv, w_in_b, gq_b, gk_b, cos2, sin2)
    sinks1 = sinks.reshape(N_HEADS)
    o_b, z_b, lse_b = _swa_fwd(qb, ksh, vsh, pb, sinks1)
    dy, lsum = _out_b_loss(z_b, w_out_b, h1, tgt)
    dw_out_b = _mm(z_b, dy, "tn", 512, 512, S, out_dtype=BF16, name="mm_dw_out_b")
    dz_b = _mm(dy, w_out_b, "nt", 1024, 512, D, name="mm_dz_b")
    dpb, dka, dkb, dva, dvb, dsinks, dgq_b = _swa_bwd(qb, ksh, vsh, dz_b, o_b, lse_b, pb, sinks1, gq_b, cos2, sin2)
    dkv, dgk_b = _prep_kv_bwd(dka, dkb, dva, dvb, kv, gk_b, cos2, sin2)
    dw_in_b = _mm(u_b, dpb, "tn", 512, 512, S, out_dtype=BF16, name="mm_dw_in_b")
    dw_kv = _mm(u_kv, dkv, "tn", 512, 512, S, out_dtype=BF16, name="mm_dw_kv")
    dh1, dg_b, dg_kv = _du_b_rms_bwd(dpb, w_in_b, dkv, w_kv, h1, g_b, g_kv, dy)
    dw_out_a = _mm(z_a, dh1, "tn", 512, 512, S, out_dtype=BF16, name="mm_dw_out_a")
    do_a, dgate_a, delta_a = _fox_bwd_pre(dh1, w_out_a, proj, o_a)
    dk_a, dv_a, dcs, dq_a, drow, r_wob, r_wib, r_wkv, r_woa = _fox_bwd(
        qn, kn, vb, do_a, lse_a, delta_a, crow, cbc,
        rider=[("a2a_rows", dw_out_b), ("a2a_cols", dw_in_b), ("a2a_rows", dw_kv), ("a2a_rows", dw_out_a)])
    drow_col = jnp.pad(drow.reshape(N_HEADS, S).T, ((0, 0), (0, LANES - N_HEADS)))
    dproj, dgq_a, dgk_a, db_f = _prep_a_bwd(dq_a, dk_a, dv_a, dgate_a, drow_col, dcs, proj, b_pad, gq_a, gk_a)
    dwa = _mm(u_a, dproj, "tn", 1024, 256, S, out_dtype=BF16, name="mm_dw_in_a")
    partial = _pair_reduce(_reshard_dwa(dwa))
    started = _chip_exchange_start(partial)
    dx, dg_a = _du_a_rms_bwd(dproj, wa, x, g_a, dh1, after=started[-1])
    slab = _pack_small(dg_a, dg_kv, dg_b, db_f, dgq_a, dgk_a, dgk_b, dgq_b, dsinks, lsum)
    slab_g = _gather_slab(slab)
    partial, landed = _chip_exchange_wait(started, slab_g)
    return dx, (landed, partial), r_woa, r_wkv, r_wib, r_wob, slab_g


def kernel(x, positions, norm_a_g, w_in_a, b_forget, qnorm_a_g, knorm_a_g, w_out_a, kv_norm_g, w_kv, knorm_b_g, norm_b_g, w_in_b, qnorm_b_g, sinks, w_out_b, loss_target, m_norm_a_g, m_w_in_a, m_b_forget, m_qnorm_a_g, m_knorm_a_g, m_w_out_a, m_kv_norm_g, m_w_kv, m_knorm_b_g, m_norm_b_g, m_w_in_b, m_qnorm_b_g, m_sinks, m_w_out_b, v_norm_a_g, v_w_in_a, v_b_forget, v_qnorm_a_g, v_knorm_a_g, v_w_out_a, v_kv_norm_g, v_w_kv, v_knorm_b_g, v_norm_b_g, v_w_in_b, v_qnorm_b_g, v_sinks, v_w_out_b):
    wa_g, ga_g, woa_s, wkv_s, wib_s, wob_s = _gather_first(w_in_a, w_out_a, w_kv, w_in_b, w_out_b, norm_a_g)
    dx, r_wa, r_woa, r_wkv, r_wib, r_wob, slab_g = _local_step(
        x[0], loss_target[0], positions, ga_g.reshape(1, D), _unshard_wa(wa_g), b_forget, qnorm_a_g, knorm_a_g,
        kv_norm_g.reshape(1, D), knorm_b_g.reshape(1, HD), norm_b_g, qnorm_b_g, sinks, woa_s, wkv_s, wib_s, wob_s)

    big = {}
    for name, recv, w, m, v in (
            ("w_in_a", r_wa, w_in_a, m_w_in_a, v_w_in_a), ("w_out_a", r_woa, w_out_a, m_w_out_a, v_w_out_a),
            ("w_kv", r_wkv, w_kv, m_w_kv, v_w_kv), ("w_in_b", r_wib, w_in_b, m_w_in_b, v_w_in_b),
            ("w_out_b", r_wob, w_out_b, m_w_out_b, v_w_out_b)):
        big[name] = _sum_adamw(recv, w, m, v, "adamw_" + name)

    r2 = lambda a: a.reshape(1, -1)
    small_w = dict(norm_a_g=norm_a_g, b_forget=b_forget, qnorm_a_g=qnorm_a_g, knorm_a_g=knorm_a_g,
                   kv_norm_g=kv_norm_g, knorm_b_g=knorm_b_g, norm_b_g=norm_b_g, qnorm_b_g=qnorm_b_g, sinks=sinks)
    small_m = dict(norm_a_g=m_norm_a_g, b_forget=m_b_forget, qnorm_a_g=m_qnorm_a_g, knorm_a_g=m_knorm_a_g,
                   kv_norm_g=m_kv_norm_g, knorm_b_g=m_knorm_b_g, norm_b_g=m_norm_b_g, qnorm_b_g=m_qnorm_b_g,
                   sinks=m_sinks)
    small_v = dict(norm_a_g=v_norm_a_g, b_forget=v_b_forget, qnorm_a_g=v_qnorm_a_g, knorm_a_g=v_knorm_a_g,
                   kv_norm_g=v_kv_norm_g, knorm_b_g=v_knorm_b_g, norm_b_g=v_norm_b_g, qnorm_b_g=v_qnorm_b_g,
                   sinks=v_sinks)
    res = _small_adamw(slab_g, [r2(small_w[n]) for n in SMALL], [r2(small_m[n]) for n in SMALL],
                       [r2(small_v[n]) for n in SMALL])
    k = len(SMALL)
    small = {n: [res[q * k + p].reshape(small_w[n].shape) for q in range(4)] for p, n in enumerate(SMALL)}
    loss = res[4 * k][0, 0]

    order = ["norm_a_g", "w_in_a", "b_forget", "qnorm_a_g", "knorm_a_g", "w_out_a", "kv_norm_g", "w_kv",
             "knorm_b_g", "norm_b_g", "w_in_b", "qnorm_b_g", "sinks", "w_out_b"]

    def leaf(n, q):
        return big[n][q] if n in big else small[n][q]

    outs = [loss, dx[None]]
    for q in range(4):
        outs.extend(leaf(n, q) for n in order)
    return tuple(outs)
```

```python
import jax
import jax.numpy as jnp
from jax import lax
from jax.experimental import pallas as pl
from jax.experimental.pallas import tpu as pltpu

F32, BF16 = jnp.float32, jnp.bfloat16

S = 2048
D = 1024
HD = 64
N_HEADS = 16
N_DEV = 8
NA = 4352
GOFF = 3072
FOFF = 4096
RAW_F = 3072
RAW_G = RAW_F + N_HEADS
NA_RAW = 4112
EPS = 1e-6
QSCALE = 0.125
ROPE_THETA = 500000.0
ROT = 16
WIN = 128
TQ = 256
TK = 256
KS = TQ // 2
HPS = 8
HW = HPS * HD
TM = 256
RB = 512
CB = 256
LANES = 128

ADAM_LR, ADAM_B1, ADAM_B2, ADAM_EPS, ADAM_WD, ADAM_STEP = 0.001, 0.9, 0.999, 1e-08, 0.01, 10

VMEM_LIMIT = 56 * 1024 * 1024


def _params():
    return pltpu.CompilerParams(vmem_limit_bytes=VMEM_LIMIT)


def _sds(shape, dtype):
    return jax.ShapeDtypeStruct(shape, dtype)


def _dot_nt(a, b):
    return lax.dot_general(a, b, (((1,), (1,)), ((), ())), preferred_element_type=F32)


def _dot_tn(a, b):
    return lax.dot_general(a, b, (((0,), (0,)), ((), ())), preferred_element_type=F32)


def _dot_nn(a, b):
    return lax.dot_general(a, b, (((1,), (0,)), ((), ())), preferred_element_type=F32)


def _sigmoid(g):
    return 1.0 / (1.0 + jnp.exp(-g))


def _lane_iota(shape):
    return lax.broadcasted_iota(jnp.int32, shape, len(shape) - 1)


def _flips(kind):
    return (2, 4, 6) if kind == "a2a_chips" else tuple(range(1, N_DEV))


def _send_view(kind, ref, dev):
    if kind in ("gather_rows", "gather_cols"):
        return ref
    if kind == "a2a_slots":
        return ref.at[dev]
    if kind == "a2a_chips":
        return ref.at[dev >> 1]
    if kind == "a2a_rows":
        rows = ref.shape[0] // N_DEV
        return ref.at[pl.ds(pl.multiple_of(dev * rows, rows), rows)]
    cols = ref.shape[1] // N_DEV
    return ref.at[:, pl.ds(pl.multiple_of(dev * cols, cols), cols)]


def _land_view(kind, ref, dev):
    if kind == "gather_cols":
        cols = ref.shape[1] // N_DEV
        return ref.at[:, pl.ds(pl.multiple_of(dev * cols, cols), cols)]
    if kind == "a2a_chips":
        return ref.at[dev >> 1]
    return ref.at[dev]


def _landing_sds(kind, arr):
    if kind == "gather_rows":
        return _sds((N_DEV,) + arr.shape, arr.dtype)
    if kind == "gather_cols":
        return _sds((arr.shape[0], N_DEV * arr.shape[1]), arr.dtype)
    if kind == "a2a_rows":
        return _sds((N_DEV, arr.shape[0] // N_DEV, arr.shape[1]), arr.dtype)
    if kind == "a2a_cols":
        return _sds((N_DEV, arr.shape[0], arr.shape[1] // N_DEV), arr.dtype)
    return _sds(arr.shape, arr.dtype)


def _exchange_sems(n_parts):
    n = n_parts * (N_DEV - 1)
    return [pltpu.SemaphoreType.DMA((n,)), pltpu.SemaphoreType.DMA((n,)), pltpu.SemaphoreType.DMA((n_parts,))]


def _exchange_ops(kinds, srcs, dsts, sems, start, wait):
    send_sems, recv_sems, local_sems = sems
    x, y, c = lax.axis_index("x"), lax.axis_index("y"), lax.axis_index("c")
    me = 4 * x + 2 * y + c

    def local(a):
        return pltpu.make_async_copy(_send_view(kinds[a], srcs[a], me), _land_view(kinds[a], dsts[a], me),
                                     local_sems.at[a])

    def remote(a, k, landing_dev):
        peer = (x ^ ((k >> 2) & 1), y ^ ((k >> 1) & 1), c ^ (k & 1))
        sem = a * (N_DEV - 1) + k - 1
        return pltpu.make_async_remote_copy(
            src_ref=_send_view(kinds[a], srcs[a], me ^ k), dst_ref=_land_view(kinds[a], dsts[a], landing_dev),
            send_sem=send_sems.at[sem], recv_sem=recv_sems.at[sem], device_id=peer,
            device_id_type=pl.DeviceIdType.MESH)

    pairs = [(a, k) for k in range(1, N_DEV) for a in range(len(kinds)) if k in _flips(kinds[a])]
    if start:
        for a in range(len(kinds)):
            local(a).start()
        for a, k in pairs:
            remote(a, k, me).start()
    if wait:
        for a, k in pairs:
            remote(a, k, me ^ k).wait_recv()
            remote(a, k, me).wait_send()
        for a in range(len(kinds)):
            local(a).wait()


def _gather_two_level(srcs, dsts, sems, meanwhile=None):
    send_sems, recv_sems, local_sems = sems
    x, y, c = lax.axis_index("x"), lax.axis_index("y"), lax.axis_index("c")
    me, sibling = (x, y, c), (x, y, 1 - c)
    chips = [(1 - x, y), (x, 1 - y), (1 - x, 1 - y)]

    def slot(ref, dev):
        return ref.at[4 * dev[0] + 2 * dev[1] + dev[2]]

    def copy(a, k, block, to, src=None):
        return pltpu.make_async_remote_copy(
            src_ref=slot(dsts[a], block) if src is None else src, dst_ref=slot(dsts[a], block),
            send_sem=send_sems.at[a * (N_DEV - 1) + k], recv_sem=recv_sems.at[a * (N_DEV - 1) + k],
            device_id=to, device_id_type=pl.DeviceIdType.MESH)

    parts = range(len(srcs))
    mine = [pltpu.make_async_copy(srcs[a], slot(dsts[a], me), local_sems.at[a]) for a in parts]
    first = [copy(a, 0, me, sibling, src=srcs[a]) for a in parts]
    first += [copy(a, 1 + j, me, (*chip, c), src=srcs[a]) for j, chip in enumerate(chips) for a in parts]
    for cp in mine + first:
        cp.start()
    if meanwhile is not None:
        meanwhile()
    passed = []
    for j, chip in enumerate(chips):
        for a in parts:
            copy(a, 1 + j, (*chip, c), me).wait_recv()
            fwd = copy(a, 4 + j, (*chip, c), sibling)
            fwd.start()
            passed.append(fwd)
    for a in parts:
        copy(a, 0, sibling, me).wait_recv()
        for j, chip in enumerate(chips):
            copy(a, 4 + j, (*chip, 1 - c), me).wait_recv()
    for cp in first + passed:
        cp.wait_send()
    for cp in mine:
        cp.wait()


def _call(body, *, name, args, in_specs, out_specs, out_shape, grid=(), scratch_shapes=(), aliases=None, rider=()):
    n_in, n_out, n_scr, n_r = len(in_specs), len(out_specs), len(scratch_shapes), len(rider)
    kinds = [kind for kind, _ in rider]

    def kernel_body(*refs):
        c_in, r_in = refs[:n_in], refs[n_in:n_in + n_r]
        c_out = refs[n_in + n_r:n_in + n_r + n_out]
        r_out = refs[n_in + n_r + n_out:n_in + 2 * n_r + n_out]
        rest = refs[n_in + 2 * n_r + n_out:]
        c_scr, sems = rest[:n_scr], rest[n_scr:]
        if n_r:
            assert grid, "a rider needs a gridded call"
            ids = [pl.program_id(ax) for ax in range(len(grid))]
            first, last = ids[0] == 0, ids[0] == grid[0] - 1
            for pid, size in zip(ids[1:], grid[1:]):
                first = first & (pid == 0)
                last = last & (pid == size - 1)
            pl.when(first)(lambda: _exchange_ops(kinds, r_in, r_out, sems, True, False))
        body(*c_in, *c_out, *c_scr)
        if n_r:
            pl.when(last)(lambda: _exchange_ops(kinds, r_in, r_out, sems, False, True))

    anyspec = pl.BlockSpec(memory_space=pl.ANY)
    params = pltpu.CompilerParams(vmem_limit_bytes=VMEM_LIMIT, has_side_effects=bool(n_r))
    outs = pl.pallas_call(
        kernel_body, name=name, grid=grid, in_specs=list(in_specs) + [anyspec] * n_r,
        out_specs=list(out_specs) + [anyspec] * n_r,
        out_shape=list(out_shape) + [_landing_sds(kind, arr) for kind, arr in rider],
        scratch_shapes=list(scratch_shapes) + (_exchange_sems(n_r) if n_r else []),
        input_output_aliases=aliases or {}, compiler_params=params)(*args, *[arr for _, arr in rider])
    return list(outs)


def _mm(a, b, mode, tm, tn, tk, out_dtype=F32, add=None, name="mm", rider=()):
    if mode == "nn":
        (m, k), n = a.shape, b.shape[1]
        a_spec = pl.BlockSpec((tm, tk), lambda i, j, kk: (i, kk))
        b_spec = pl.BlockSpec((tk, tn), lambda i, j, kk: (kk, j))
        dot = _dot_nn
    elif mode == "nt":
        (m, k), n = a.shape, b.shape[0]
        a_spec = pl.BlockSpec((tm, tk), lambda i, j, kk: (i, kk))
        b_spec = pl.BlockSpec((tn, tk), lambda i, j, kk: (j, kk))
        dot = _dot_nt
    else:
        (k, m), n = a.shape, b.shape[1]
        a_spec = pl.BlockSpec((tk, tm), lambda i, j, kk: (kk, i))
        b_spec = pl.BlockSpec((tk, tn), lambda i, j, kk: (kk, j))
        dot = _dot_tn
    assert m % tm == 0 and n % tn == 0 and k % tk == 0, (m, n, k, tm, tn, tk)
    nk = k // tk
    has_add = add is not None

    def body(*refs):
        if has_add:
            a_ref, b_ref, add_ref, o_ref, acc = refs
        else:
            a_ref, b_ref, o_ref, acc = refs
        p = dot(a_ref[...].astype(BF16), b_ref[...].astype(BF16))

        def finish(total):
            if has_add:
                total = add_ref[...] + total
            o_ref[...] = total.astype(out_dtype)

        if nk == 1:
            finish(p)
        else:
            kk = pl.program_id(2)

            @pl.when(kk == 0)
            def _():
                acc[...] = p

            @pl.when(kk > 0)
            def _():
                acc[...] += p

            @pl.when(kk == nk - 1)
            def _():
                finish(acc[...])

    in_specs = [a_spec, b_spec]
    args = [a, b]
    if has_add:
        in_specs.append(pl.BlockSpec((tm, tn), lambda i, j, kk: (i, j)))
        args.append(add)
    acc_shape = (tm, tn) if nk > 1 else (8, LANES)
    outs = _call(body, name=name, args=args, grid=(m // tm, n // tn, nk), in_specs=in_specs,
                 out_specs=[pl.BlockSpec((tm, tn), lambda i, j, kk: (i, j))], out_shape=[_sds((m, n), out_dtype)],
                 scratch_shapes=[pltpu.VMEM(acc_shape, F32)], rider=rider)
    return outs if rider else outs[0]


def _rms_rinv(x):
    return lax.rsqrt(jnp.mean(x * x, axis=-1, keepdims=True) + EPS)


def _rms_bwd_core(du, x, g):
    r = _rms_rinv(x)
    dug = du * g
    dx = r * (dug - x * ((r * r) * jnp.mean(dug * x, axis=-1, keepdims=True)))
    dg = jnp.sum(du * (x * r), axis=0, keepdims=True)
    return dx, dg


def _half_sum(v, lo_half):
    s0 = jnp.sum(jnp.where(lo_half, v, 0.0), axis=-1, keepdims=True)
    s1 = jnp.sum(jnp.where(lo_half, 0.0, v), axis=-1, keepdims=True)
    return jnp.where(lo_half, s0, s1)


def _head_rinv(x, lo_half):
    return lax.rsqrt(_half_sum(x * x, lo_half) * (1.0 / HD) + EPS)


def _head_norm_bwd(dn, x, g, lo_half):
    r = _head_rinv(x, lo_half)
    dng = dn * g
    dx = r * (dng - x * ((r * r) * (_half_sum(dng * x, lo_half) * (1.0 / HD))))
    dg = jnp.sum(dn * (x * r), axis=0, keepdims=True)
    return dx, dg


def _rope_swap(x, lane):
    l64 = lane & (HD - 1)
    return jnp.where(l64 < ROT // 2, pltpu.roll(x, LANES - ROT // 2, 1), pltpu.roll(x, ROT // 2, 1))


def _rope_fwd(x, cos, sin, lane):
    return x * cos + _rope_swap(x, lane) * sin


def _rope_bwd(dy, cos, sin, lane):
    return dy * cos + jnp.where((lane & (HD - 1)) < ROT, _rope_swap(dy * sin, lane), 0.0)


def _g2(g_ref):
    g = g_ref[...]
    return jnp.concatenate([g, g], axis=-1)


def _pairs(width):
    return [slice(LANES * c, LANES * (c + 1)) for c in range(width // LANES)]


def _pick_lane(block, lane, idx):
    return jnp.sum(jnp.where(lane == idx, block, 0.0), axis=-1, keepdims=True)


AW = 2 * HD


def _bias_lanes(c_col, lane64, ones_first):
    hi = c_col.astype(BF16).astype(F32)
    rest = c_col - hi
    mid = rest.astype(BF16).astype(F32)
    pieces = [hi, mid, rest - mid]
    out = jnp.zeros(lane64.shape, F32)
    for j, piece in enumerate(pieces):
        out = jnp.where(lane64 == (3 + j if ones_first else j), piece if ones_first else -piece, out)
    ones = (lane64 < 3) if ones_first else ((lane64 >= 3) & (lane64 < 6))
    return jnp.where(ones, 1.0, out)


def _head_a(x, g, wa, gq, gk, b_pad):
    def body(x_ref, g_ref, w_ref, gq_ref, gk_ref, b_ref, u_ref, p_ref, qa_ref, ka_ref, vo_ref, carry):
        @pl.when(pl.program_id(0) == 0)
        def _():
            carry[...] = jnp.zeros_like(carry)

        xv = x_ref[...]
        u = ((xv * _rms_rinv(xv)) * g_ref[...]).astype(BF16)
        u_ref[...] = u
        for lo in range(0, NA, D):
            hi = min(lo + D, NA)
            p_ref[:, lo:hi] = _dot_nn(u, w_ref[:, lo:hi])
        vo_ref[...] = p_ref[:, 2 * D:3 * D].astype(BF16)

        z = p_ref[:, FOFF:FOFF + LANES] + b_ref[...]
        logf = jnp.minimum(z, 0.0) - jnp.log1p(jnp.exp(-jnp.abs(z)))
        r = lax.broadcasted_iota(jnp.int32, (TM, TM), 0)
        cc = lax.broadcasted_iota(jnp.int32, (TM, TM), 1)
        tri = (r >= cc).astype(F32)
        loc = jnp.dot(tri, logf, precision=lax.Precision.HIGHEST, preferred_element_type=F32) + carry[0:1, :]
        carry[0:1, :] = loc[TM - 1:TM, :]

        lane = _lane_iota((TM, LANES))
        lane64 = _lane_iota((TM, HD))
        lo_half = lane < HD
        gq2, gk2 = _g2(gq_ref), _g2(gk_ref)
        for pair, c in enumerate(_pairs(D)):
            q = p_ref[:, c]
            k = p_ref[:, D + c.start:D + c.stop]
            qn = ((q * _head_rinv(q, lo_half)) * gq2) * QSCALE
            kn = (k * _head_rinv(k, lo_half)) * gk2
            for half in range(2):
                h = 2 * pair + half
                c_col = _pick_lane(loc, lane, h)
                feat = slice(HD * half, HD * (half + 1))
                qa_ref[:, AW * h:AW * (h + 1)] = jnp.concatenate(
                    [qn[:, feat], _bias_lanes(c_col, lane64, True)], axis=-1).astype(BF16)
                ka_ref[:, AW * h:AW * (h + 1)] = jnp.concatenate(
                    [kn[:, feat], _bias_lanes(c_col, lane64, False)], axis=-1).astype(BF16)

    row = lambda width: pl.BlockSpec((TM, width), lambda i: (i, 0))
    whole = lambda arr: pl.BlockSpec(arr.shape, lambda i: (0,) * arr.ndim)
    return pl.pallas_call(
        body, name="head_a", grid=(S // TM,),
        in_specs=[row(D), whole(g), whole(wa), whole(gq), whole(gk), whole(b_pad)],
        out_specs=[row(D), row(NA), row(N_HEADS * AW), row(N_HEADS * AW), row(D)],
        out_shape=[_sds((S, D), BF16), _sds((S, NA), F32), _sds((S, N_HEADS * AW), BF16),
                   _sds((S, N_HEADS * AW), BF16), _sds((S, D), BF16)],
        scratch_shapes=[pltpu.VMEM((8, LANES), F32)], compiler_params=_params())(x, g, wa, gq, gk, b_pad)


def _key_le_query(offset, keys=TK):
    r = lax.broadcasted_iota(jnp.int32, (keys, TQ), 0)
    c = lax.broadcasted_iota(jnp.int32, (keys, TQ), 1)
    return (r + offset) <= c


def _fox_fwd(qa, ka, vb, proj, rider=()):
    nq = S // TQ

    def body(q_ref, k_ref, v_ref, g_ref, o_ref, z_ref, lse_ref, st_s, pt_s):
        i = pl.program_id(1)
        qs = [q_ref[:, AW * hh:AW * (hh + 1)] for hh in range(HPS)]

        def scores(s, hh):
            off = pl.multiple_of(s * KS, KS)
            return _dot_nt(k_ref[pl.ds(off, KS), AW * hh:AW * (hh + 1)], qs[hh])

        def values(s, hh, pt):
            off = pl.multiple_of(s * KS, KS)
            return _dot_tn(v_ref[pl.ds(off, KS), HD * hh:HD * (hh + 1)], pt)

        def step(s, slot, carries, mask=None, last=False):
            if not last:
                for hh in range(HPS):
                    st_s[1 - slot, hh] = scores(s + 1, hh)
            pvs = [values(jnp.maximum(s - 1, 0), hh, pt_s[1 - slot, hh]) for hh in range(HPS)]
            out = []
            for hh in range(HPS):
                m, l, acc = carries[hh]
                st = st_s[slot, hh]
                if mask is not None:
                    st = jnp.where(mask, st, -jnp.inf)
                m_new = jnp.maximum(m, jnp.max(st, axis=0, keepdims=True))
                pt = jnp.exp(st - m_new)
                alpha = jnp.exp(m - m_new)
                pt_s[slot, hh] = pt.astype(BF16)
                out.append((m_new, alpha * l + jnp.sum(pt, axis=0, keepdims=True), alpha * (acc + pvs[hh])))
            return tuple(out)

        for hh in range(HPS):
            st_s[0, hh] = scores(0, hh)
            pt_s[1, hh] = jnp.zeros((KS, TQ), BF16)
        one = (jnp.full((1, TQ), -jnp.inf, F32), jnp.zeros((1, TQ), F32), jnp.zeros((HD, TQ), F32))
        carries = lax.fori_loop(0, i, lambda t, cr: step(2 * t + 1, 1, step(2 * t, 0, cr)), (one,) * HPS)
        carries = step(2 * i, 0, carries, mask=_key_le_query(0, KS))
        carries = step(2 * i + 1, 1, carries, mask=_key_le_query(KS, KS), last=True)
        accs = []
        for hh in range(HPS):
            m, l, acc = carries[hh]
            acc = acc + values(2 * i + 1, hh, pt_s[1, hh])
            accs.append(acc / l)
            lse_ref[hh, 0] = m + jnp.log(l)
        o = jnp.concatenate(accs, axis=0).T
        o_ref[...] = o
        g = g_ref[...]
        z_ref[...] = (o * (g * _sigmoid(g))).astype(BF16)

    qblk = pl.BlockSpec((TQ, HW), lambda hp, i: (i, hp))
    full = pl.BlockSpec((S, HW), lambda hp, i: (0, hp))
    rows = pl.BlockSpec((HPS, 1, 1, TQ), lambda hp, i: (hp, i, 0, 0))
    return _call(
        body, name="fox_fwd", args=(qa, ka, vb, proj), grid=(N_HEADS // HPS, nq),
        in_specs=[pl.BlockSpec((TQ, HPS * AW), lambda hp, i: (i, hp)),
                  pl.BlockSpec((S, HPS * AW), lambda hp, i: (0, hp)), full,
                  pl.BlockSpec((TQ, HW), lambda hp, i: (i, GOFF // HW + hp))],
        out_specs=[qblk, qblk, rows],
        out_shape=[_sds((S, D), F32), _sds((S, D), BF16), _sds((N_HEADS, nq, 1, TQ), F32)],
        scratch_shapes=[pltpu.VMEM((2, HPS, KS, TQ), F32), pltpu.VMEM((2, HPS, KS, TQ), BF16)], rider=rider)


def _fox_bwd_pre(dh, w_out, proj, o):
    nq = S // TQ

    def body(dh_ref, w_ref, g_ref, o_ref, do_ref, dg_ref, delta_ref):
        g = g_ref[...]
        sg = _sigmoid(g)
        dzv = _dot_nt(dh_ref[...].astype(BF16), w_ref[...])
        ov = o_ref[...]
        do = dzv * (g * sg)
        dg_ref[...] = (dzv * ov * (sg * (1.0 + g * (1.0 - sg)))).astype(BF16)
        do_ref[...] = do.astype(BF16)
        prod_t = (do * ov).T
        for h in range(N_HEADS):
            delta_ref[h, 0] = jnp.sum(prod_t[HD * h:HD * (h + 1), :], axis=0, keepdims=True)

    row = pl.BlockSpec((TQ, D), lambda i: (i, 0))
    return pl.pallas_call(
        body, name="fox_bwd_pre", grid=(nq,),
        in_specs=[row, pl.BlockSpec(w_out.shape, lambda i: (0, 0)), pl.BlockSpec((TQ, D), lambda i: (i, GOFF // D)), row],
        out_specs=[row, row, pl.BlockSpec((N_HEADS, 1, 1, TQ), lambda i: (0, i, 0, 0))],
        out_shape=[_sds((S, D), BF16), _sds((S, D), BF16), _sds((N_HEADS, nq, 1, TQ), F32)],
        compiler_params=_params())(dh, w_out, proj, o)


def _fox_bwd(qa, ka, vb, dob, lse, delta, rider=()):
    nq, nkb = S // TQ, S // TK

    def body(q_ref, k_ref, v_ref, do_ref, lse_ref, del_ref,
             dk_ref, dv_ref, dcs_ref, dq_ref, dr_ref, st_s, dp_s, pt_s, ds_s, dq_acc, dr_acc):
        j = pl.program_id(1)

        @pl.when(j == 0)
        def _():
            dq_acc[...] = jnp.zeros_like(dq_acc)
            dr_acc[...] = jnp.zeros_like(dr_acc)

        kas = [k_ref[:, AW * hh:AW * (hh + 1)] for hh in range(HPS)]
        kjs = [k_ref[:, AW * hh:AW * hh + HD] for hh in range(HPS)]
        vjs = [v_ref[:, HD * hh:HD * (hh + 1)] for hh in range(HPS)]

        def rows_of(ref, u, hh, stride=HD, width=HD):
            off = pl.multiple_of(u * TQ, TQ)
            return ref[pl.ds(off, TQ), stride * hh:stride * hh + width]

        def products(u, hh):
            return (_dot_nt(kas[hh], rows_of(q_ref, u, hh, AW, AW)),
                    _dot_nt(vjs[hh], rows_of(do_ref, u, hh)))

        def step(u, slot, carries, masked=False):
            nxt = jnp.minimum(u + 1, nq - 1)
            for hh in range(HPS):
                st_s[1 - slot, hh], dp_s[1 - slot, hh] = products(nxt, hh)
            prev = jnp.maximum(u - 1, 0)
            dvs = [_dot_nn(pt_s[1 - slot, hh], rows_of(do_ref, prev, hh)) for hh in range(HPS)]
            dks = [_dot_nn(ds_s[1 - slot, hh], rows_of(q_ref, prev, hh, AW)) for hh in range(HPS)]
            for hh in range(HPS):
                dq_acc[hh, prev] += _dot_tn(kjs[hh], ds_s[1 - slot, hh])
            out = []
            for hh in range(HPS):
                dk, dv, dcs = carries[hh]
                st = st_s[slot, hh]
                if masked:
                    st = jnp.where(_key_le_query((j - u) * TQ), st, -jnp.inf)
                pt = jnp.exp(st - lse_ref[hh, u])
                dst = pt * (dp_s[slot, hh] - del_ref[hh, u])
                pt_s[slot, hh] = pt.astype(BF16)
                ds_s[slot, hh] = dst.astype(BF16)
                dr_acc[hh, u] += jnp.sum(dst, axis=0, keepdims=True)
                out.append((dk + dks[hh], dv + dvs[hh], dcs + (dst[:, :LANES] + dst[:, LANES:])))
            return tuple(out)

        t0 = j // 2
        for hh in range(HPS):
            st_s[0, hh], dp_s[0, hh] = products(2 * t0, hh)
            pt_s[1, hh] = jnp.zeros((TK, TQ), BF16)
            ds_s[1, hh] = jnp.zeros((TK, TQ), BF16)
        one = (jnp.zeros((TK, HD), F32), jnp.zeros((TK, HD), F32), jnp.zeros((TK, LANES), F32))
        carries = step(2 * t0 + 1, 1, step(2 * t0, 0, (one,) * HPS, masked=True), masked=True)
        carries = lax.fori_loop(t0 + 1, nq // 2, lambda t, cr: step(2 * t + 1, 1, step(2 * t, 0, cr)), carries)
        dks, dvs = [], []
        lane = _lane_iota((TK, LANES))
        dcs_all = jnp.zeros((TK, LANES), F32)
        for hh in range(HPS):
            dk, dv, dcs = carries[hh]
            dks.append(dk + _dot_nn(ds_s[1, hh], rows_of(q_ref, nq - 1, hh, AW)))
            dvs.append(dv + _dot_nn(pt_s[1, hh], rows_of(do_ref, nq - 1, hh)))
            dq_acc[hh, nq - 1] += _dot_tn(kjs[hh], ds_s[1, hh])
            dcs_all = jnp.where(lane == HPS * pl.program_id(0) + hh, -jnp.sum(dcs, axis=-1, keepdims=True), dcs_all)
        dcs_ref[0] = dcs_all
        dk_ref[...] = jnp.concatenate(dks, axis=-1)
        dv_ref[...] = jnp.concatenate(dvs, axis=-1).astype(BF16)

        @pl.when(j == nkb - 1)
        def _():
            for i in range(nq):
                dq_ref[TQ * i:TQ * (i + 1), :] = jnp.concatenate([dq_acc[hh, i] for hh in range(HPS)], axis=0).T
            dr_ref[...] = dr_acc[...]

    kblk = pl.BlockSpec((TK, HW), lambda hp, j: (j, hp))
    full = pl.BlockSpec((S, HW), lambda hp, j: (0, hp))
    rows = pl.BlockSpec((HPS, nq, 1, TQ), lambda hp, j: (hp, 0, 0, 0))
    return _call(
        body, name="fox_bwd", args=(qa, ka, vb, dob, lse, delta), grid=(N_HEADS // HPS, nkb),
        in_specs=[pl.BlockSpec((S, HPS * AW), lambda hp, j: (0, hp)),
                  pl.BlockSpec((TK, HPS * AW), lambda hp, j: (j, hp)), kblk, full, rows, rows],
        out_specs=[kblk, kblk, pl.BlockSpec((1, TK, LANES), lambda hp, j: (hp, j, 0)), full, rows],
        out_shape=[_sds((S, D), F32), _sds((S, D), BF16), _sds((N_HEADS // HPS, S, LANES), F32), _sds((S, D), F32),
                   _sds((N_HEADS, nq, 1, TQ), F32)],
        scratch_shapes=[pltpu.VMEM((2, HPS, TK, TQ), F32), pltpu.VMEM((2, HPS, TK, TQ), F32),
                        pltpu.VMEM((2, HPS, TK, TQ), BF16), pltpu.VMEM((2, HPS, TK, TQ), BF16),
                        pltpu.VMEM((HPS, nq, HD, TQ), F32), pltpu.VMEM((HPS, nq, 1, TQ), F32)], rider=rider)


def _prep_a_bwd(dq, dk, dv, dgate, drow, dcs, proj, b_pad, gq, gk):
    nt = S // TM

    def body(dq_ref, dk_ref, dv_ref, dgt_ref, dr_ref, dcs_ref, xq_ref, xk_ref, f_ref, b_ref, gq_ref, gk_ref,
             o_ref, dgq_ref, dgk_ref, db_ref, carry):
        @pl.when(pl.program_id(0) == 0)
        def _():
            carry[...] = jnp.zeros_like(carry)
            dgq_ref[...] = jnp.zeros_like(dgq_ref)
            dgk_ref[...] = jnp.zeros_like(dgk_ref)
            db_ref[...] = jnp.zeros_like(db_ref)

        lane = _lane_iota((TM, LANES))
        lo_half = lane < HD
        gq2, gk2 = _g2(gq_ref), _g2(gk_ref)
        dgq, dgk = jnp.zeros((1, LANES), F32), jnp.zeros((1, LANES), F32)
        for c in _pairs(D):
            dxq, dg = _head_norm_bwd(dq_ref[:, c] * QSCALE, xq_ref[:, c], gq2, lo_half)
            o_ref[:, c] = dxq.astype(BF16)
            dgq = dgq + dg
            dxk, dg = _head_norm_bwd(dk_ref[:, c], xk_ref[:, c], gk2, lo_half)
            o_ref[:, D + c.start:D + c.stop] = dxk.astype(BF16)
            dgk = dgk + dg
        dgq_ref[...] += dgq
        dgk_ref[...] += dgk
        o_ref[:, 2 * D:3 * D] = dv_ref[...]
        o_ref[:, GOFF:GOFF + D] = dgt_ref[...]

        dc = dr_ref[...]
        for group in range(N_HEADS // HPS):
            dc = dc + dcs_ref[group]
        r = lax.broadcasted_iota(jnp.int32, (TM, TM), 0)
        c = lax.broadcasted_iota(jnp.int32, (TM, TM), 1)
        tri = (c >= r).astype(F32)
        dlogf = jnp.dot(tri, dc, precision=lax.Precision.HIGHEST, preferred_element_type=F32) + carry[0:1, :]
        carry[0:1, :] = dlogf[0:1, :]
        df = dlogf * (1.0 / (1.0 + jnp.exp(f_ref[...] + b_ref[...])))
        db_ref[...] += jnp.sum(df, axis=0, keepdims=True)
        o_ref[:, FOFF:FOFF + LANES] = df.astype(BF16)
        o_ref[:, FOFF + LANES:NA] = jnp.zeros((TM, NA - FOFF - LANES), BF16)

    rev = lambda width, col: pl.BlockSpec((TM, width), lambda i: (nt - 1 - i, col))
    gspec = pl.BlockSpec((1, HD), lambda i: (0, 0))
    acc = pl.BlockSpec((1, LANES), lambda i: (0, 0))
    return pl.pallas_call(
        body, name="prep_a_bwd", grid=(nt,),
        in_specs=[rev(D, 0), rev(D, 0), rev(D, 0), rev(D, 0), rev(LANES, 0),
                  pl.BlockSpec((N_HEADS // HPS, TM, LANES), lambda i: (0, nt - 1 - i, 0)),
                  rev(D, 0), rev(D, 1), rev(LANES, FOFF // LANES), acc, gspec, gspec],
        out_specs=[rev(NA, 0), acc, acc, acc],
        out_shape=[_sds((S, NA), BF16)] + [_sds((1, LANES), F32)] * 3,
        scratch_shapes=[pltpu.VMEM((8, LANES), F32)],
        compiler_params=_params())(dq, dk, dv, dgate, drow, dcs, proj, proj, proj, b_pad, gq, gk)


def _head_b(x, z_a, w_out_a, g_kv, g_b, w_kv, w_in_b, gq, gk, cos2, sin2):
    nkv = w_kv.shape[1] // 2

    def body(x_ref, z_ref, wo_ref, gkv_ref, gb_ref, wkv_ref, wb_ref, gq_ref, gk_ref, c_ref, s_ref,
             h_ref, ukv_ref, ub_ref, kv_ref, pb_ref, qo_ref, ko_ref, vo_ref):
        xv = x_ref[...] + _dot_nn(z_ref[...], wo_ref[...])
        h_ref[...] = xv
        xn = xv * _rms_rinv(xv)
        ukv = (xn * gkv_ref[...]).astype(BF16)
        ub = (xn * gb_ref[...]).astype(BF16)
        ukv_ref[...] = ukv
        ub_ref[...] = ub
        kv_ref[...] = _dot_nn(ukv, wkv_ref[...])
        for lo in range(0, 2 * D, D):
            pb_ref[:, lo:lo + D] = _dot_nn(ub, wb_ref[:, lo:lo + D])
        lane = _lane_iota((TM, LANES))
        lo_half = lane < HD
        cos, sin = c_ref[...], s_ref[...]
        gq2, gk2 = _g2(gq_ref), _g2(gk_ref)
        for c in _pairs(D):
            q = pb_ref[:, c]
            qo_ref[:, c] = (_rope_fwd((q * _head_rinv(q, lo_half)) * gq2, cos, sin, lane) * QSCALE).astype(BF16)
        for c in _pairs(nkv):
            k = kv_ref[:, c]
            ko_ref[:, c] = _rope_fwd((k * _head_rinv(k, lo_half)) * gk2, cos, sin, lane).astype(BF16)
        vo_ref[...] = kv_ref[:, nkv:2 * nkv].astype(BF16)

    row = lambda width: pl.BlockSpec((TM, width), lambda i: (i, 0))
    whole = lambda arr: pl.BlockSpec(arr.shape, lambda i: (0,) * arr.ndim)
    return pl.pallas_call(
        body, name="head_b", grid=(S // TM,),
        in_specs=[row(D), row(D), whole(w_out_a), whole(g_kv), whole(g_b), whole(w_kv), whole(w_in_b), whole(gq),
                  whole(gk), row(LANES), row(LANES)],
        out_specs=[row(D), row(D), row(D), row(2 * nkv), row(2 * D), row(D), row(nkv), row(nkv)],
        out_shape=[_sds((S, D), F32), _sds((S, D), BF16), _sds((S, D), BF16), _sds((S, 2 * nkv), F32),
                   _sds((S, 2 * D), F32), _sds((S, D), BF16), _sds((S, nkv), BF16), _sds((S, nkv), BF16)],
        compiler_params=_params())(x, z_a, w_out_a, g_kv, g_b, w_kv, w_in_b, gq, gk, cos2, sin2)


N_KV, GRP = 4, 4


def _swa_mask(n):
    r = lax.broadcasted_iota(jnp.int32, (2 * WIN, GRP * WIN), 0)
    q = lax.broadcasted_iota(jnp.int32, (2 * WIN, GRP * WIN), 1) & (WIN - 1)
    return (r > q) & (r <= q + WIN) & ((r >= WIN) | (n > 0))


def _stack4(ref_or_val, base):
    return jnp.concatenate([ref_or_val[:, base + HD * g: base + HD * (g + 1)] for g in range(GRP)], axis=0)


def _unstack4(xt):
    return jnp.concatenate([xt[:, WIN * g:WIN * (g + 1)] for g in range(GRP)], axis=0).T


def _band(prev_ref, cur_ref, kh):
    return jnp.concatenate([prev_ref[:, HD * kh:HD * (kh + 1)], cur_ref[:, HD * kh:HD * (kh + 1)]], axis=0)


def _sink_row(s_ref, first):
    lane = _lane_iota((1, GRP * WIN))
    row = jnp.full((1, GRP * WIN), s_ref[first + GRP - 1], F32)
    for g in range(GRP - 2, -1, -1):
        row = jnp.where(lane < WIN * (g + 1), s_ref[first + g], row)
    return row


def _swa_fwd(qb, ksh, vsh, pb, sinks):
    nb = S // WIN

    def body(q_ref, kp_ref, kc_ref, vp_ref, vc_ref, g_ref, s_ref, o_ref, z_ref, lse_ref):
        n = pl.program_id(0)
        valid = _swa_mask(n)
        outs = []
        for kh in range(N_KV):
            kb, vb = _band(kp_ref, kc_ref, kh), _band(vp_ref, vc_ref, kh)
            st = jnp.where(valid, _dot_nt(kb, _stack4(q_ref, GRP * HD * kh)), -jnp.inf)
            sink = _sink_row(s_ref, GRP * kh)
            m = jnp.maximum(jnp.max(st, axis=0, keepdims=True), sink)
            pt = jnp.exp(st - m)
            l = jnp.sum(pt, axis=0, keepdims=True) + jnp.exp(sink - m)
            outs.append(_unstack4(_dot_tn(vb, pt.astype(BF16)) / l))
            lse = m + jnp.log(l)
            for g in range(GRP):
                lse_ref[GRP * kh + g, 0] = lse[:, WIN * g:WIN * (g + 1)]
        o = jnp.concatenate(outs, axis=-1)
        o_ref[...] = o
        g = g_ref[...]
        z_ref[...] = (o * (g * _sigmoid(g))).astype(BF16)

    row = pl.BlockSpec((WIN, D), lambda n: (n, 0))
    prev = pl.BlockSpec((WIN, N_KV * HD), lambda n: (jnp.maximum(n - 1, 0), 0))
    cur = pl.BlockSpec((WIN, N_KV * HD), lambda n: (n, 0))
    return pl.pallas_call(
        body, name="swa_fwd", grid=(nb,),
        in_specs=[row, prev, cur, prev, cur, pl.BlockSpec((WIN, D), lambda n: (n, 1)),
                  pl.BlockSpec(memory_space=pltpu.SMEM)],
        out_specs=[row, row, pl.BlockSpec((N_HEADS, 1, 1, WIN), lambda n: (0, n, 0, 0))],
        out_shape=[_sds((S, D), F32), _sds((S, D), BF16), _sds((N_HEADS, nb, 1, WIN), F32)],
        compiler_params=_params())(qb, ksh, ksh, vsh, vsh, pb, sinks)


def _swa_bwd(qb, ksh, vsh, dz, o, lse, pb, sinks, gq, cos2, sin2):
    nb = S // WIN

    def body(q_ref, kp_ref, kc_ref, vp_ref, vc_ref, dz_ref, o_ref, lse_ref, x_ref, g_ref, s_ref, gq_ref, c_ref, sn_ref,
             dpb_ref, dka_ref, dkb_ref, dva_ref, dvb_ref, dsink_ref, dgq_ref):
        n = pl.program_id(0)

        @pl.when(n == 0)
        def _():
            dsink_ref[...] = jnp.zeros_like(dsink_ref)
            dgq_ref[...] = jnp.zeros_like(dgq_ref)

        valid = _swa_mask(n)
        g = g_ref[...]
        sg = _sigmoid(g)
        dzv = dz_ref[...]
        ov = o_ref[...]
        do = dzv * (g * sg)
        dpb_ref[:, D:2 * D] = (dzv * ov * (sg * (1.0 + g * (1.0 - sg)))).astype(BF16)
        prod_t = (do * ov).T
        lane1 = _lane_iota((1, LANES))
        dqs, dkas, dkbs, dvas, dvbs = [], [], [], [], []
        dsink = jnp.zeros((1, LANES), F32)
        for kh in range(N_KV):
            kb, vb = _band(kp_ref, kc_ref, kh), _band(vp_ref, vc_ref, kh)
            base = GRP * HD * kh
            qs = _stack4(q_ref, base)
            dos = _stack4(do, base).astype(BF16)
            delta = jnp.concatenate(
                [jnp.sum(prod_t[base + HD * gg:base + HD * (gg + 1), :], axis=0, keepdims=True)
                 for gg in range(GRP)], axis=1)
            lse = jnp.concatenate([lse_ref[GRP * kh + gg, 0] for gg in range(GRP)], axis=1)
            st = jnp.where(valid, _dot_nt(kb, qs), -jnp.inf)
            pt = jnp.exp(st - lse)
            dst = pt * (_dot_nt(vb, dos) - delta)
            dsb = dst.astype(BF16)
            dqs.append(_unstack4(_dot_tn(kb, dsb)))
            dkband = _dot_nn(dsb, qs)
            dvband = _dot_nn(pt.astype(BF16), dos)
            dkbs.append(dkband[0:WIN, :])
            dkas.append(dkband[WIN:2 * WIN, :])
            dvbs.append(dvband[0:WIN, :])
            dvas.append(dvband[WIN:2 * WIN, :])
            ps_delta = jnp.exp(_sink_row(s_ref, GRP * kh) - lse) * delta
            for gg in range(GRP):
                val = jnp.sum(ps_delta[:, WIN * gg:WIN * (gg + 1)], axis=1, keepdims=True)
                dsink = dsink - jnp.where(lane1 == GRP * kh + gg, val, 0.0)
        dka_ref[...] = jnp.concatenate(dkas, axis=-1)
        dkb_ref[...] = jnp.concatenate(dkbs, axis=-1)
        dva_ref[...] = jnp.concatenate(dvas, axis=-1)
        dvb_ref[...] = jnp.concatenate(dvbs, axis=-1)
        dsink_ref[...] += dsink

        lane = _lane_iota((WIN, LANES))
        g2, cos, sin = _g2(gq_ref), c_ref[...], sn_ref[...]
        dg_tot = jnp.zeros((1, LANES), F32)
        for kh in range(N_KV):
            for c in _pairs(GRP * HD):
                cols = slice(GRP * HD * kh + c.start, GRP * HD * kh + c.stop)
                dn = _rope_bwd(dqs[kh][:, c] * QSCALE, cos, sin, lane)
                dx, dg = _head_norm_bwd(dn, x_ref[:, cols], g2, lane < HD)
                dpb_ref[:, cols] = dx.astype(BF16)
                dg_tot = dg_tot + dg
        dgq_ref[...] += dg_tot

    row = pl.BlockSpec((WIN, D), lambda n: (n, 0))
    prev = pl.BlockSpec((WIN, N_KV * HD), lambda n: (jnp.maximum(n - 1, 0), 0))
    cur = pl.BlockSpec((WIN, N_KV * HD), lambda n: (n, 0))
    acc = pl.BlockSpec((1, LANES), lambda n: (0, 0))
    tab = pl.BlockSpec((WIN, LANES), lambda n: (n, 0))
    return pl.pallas_call(
        body, name="swa_bwd", grid=(nb,),
        in_specs=[row, prev, cur, prev, cur, row, row, pl.BlockSpec((N_HEADS, 1, 1, WIN), lambda n: (0, n, 0, 0)),
                  row, pl.BlockSpec((WIN, D), lambda n: (n, 1)), pl.BlockSpec(memory_space=pltpu.SMEM),
                  pl.BlockSpec((1, HD), lambda n: (0, 0)), tab, tab],
        out_specs=[pl.BlockSpec((WIN, 2 * D), lambda n: (n, 0)), cur, cur, cur, cur, acc, acc],
        out_shape=[_sds((S, 2 * D), BF16)] + [_sds((S, 256), F32)] * 4 + [_sds((1, LANES), F32)] * 2,
        compiler_params=_params())(qb, ksh, ksh, vsh, vsh, dz, o, lse, pb, pb, sinks, gq, cos2, sin2)


def _prep_kv_bwd(dka, dkb, dva, dvb, kv, gk, cos2, sin2):
    nt = S // RB
    per = RB // WIN

    def shifted(cur_ref, nxt_ref, has_next):
        return jnp.concatenate([cur_ref[WIN:RB, :], jnp.where(has_next, nxt_ref[...], 0.0)], axis=0)

    def body(dka_ref, dkb_ref, dkn_ref, dva_ref, dvb_ref, dvn_ref, x_ref, g_ref, c_ref, s_ref, o_ref, dgk_ref):
        i, j = pl.program_id(0), pl.program_id(1)

        @pl.when((i == 0) & (j == 0))
        def _():
            dgk_ref[...] = jnp.zeros_like(dgk_ref)

        has_next = i < nt - 1

        @pl.when(j == 0)
        def _():
            lane = _lane_iota((RB, LANES))
            g2, cos, sin = _g2(g_ref), c_ref[...], s_ref[...]
            dy_all = dka_ref[...] + shifted(dkb_ref, dkn_ref, has_next)
            dg_tot = jnp.zeros((1, LANES), F32)
            for c in _pairs(CB):
                dn = _rope_bwd(dy_all[:, c], cos, sin, lane)
                dx, dg = _head_norm_bwd(dn, x_ref[:, c], g2, lane < HD)
                o_ref[:, c] = dx.astype(BF16)
                dg_tot = dg_tot + dg
            dgk_ref[...] += dg_tot

        @pl.when(j == 1)
        def _():
            o_ref[...] = (dva_ref[...] + shifted(dvb_ref, dvn_ref, has_next)).astype(BF16)

    cur = pl.BlockSpec((RB, CB), lambda i, j: (i, 0))
    nxt = pl.BlockSpec((WIN, CB), lambda i, j: (jnp.minimum(per * (i + 1), S // WIN - 1), 0))
    tab = pl.BlockSpec((RB, LANES), lambda i, j: (i, 0))
    return pl.pallas_call(
        body, name="prep_kv_bwd", grid=(nt, 2),
        in_specs=[cur, cur, nxt, cur, cur, nxt, cur, pl.BlockSpec((1, HD), lambda i, j: (0, 0)), tab, tab],
        out_specs=[pl.BlockSpec((RB, CB), lambda i, j: (i, j)), pl.BlockSpec((1, LANES), lambda i, j: (0, 0))],
        out_shape=[_sds((S, 2 * CB), BF16), _sds((1, LANES), F32)],
        compiler_params=_params())(dka, dkb, dkb, dva, dvb, dvb, kv, gk, cos2, sin2)


def _out_b_loss(z, w_out, h1, tgt):
    tm = 512

    def body(z_ref, w_ref, h_ref, t_ref, dy_ref, l_ref):
        @pl.when(pl.program_id(0) == 0)
        def _():
            l_ref[...] = jnp.zeros_like(l_ref)

        e = (h_ref[...] + _dot_nn(z_ref[...], w_ref[...])) - t_ref[...]
        dy_ref[...] = e * (1.0 / D)
        l_ref[...] += jnp.sum(jnp.sum(e * e, axis=-1, keepdims=True), axis=0, keepdims=True)

    row = pl.BlockSpec((tm, D), lambda i: (i, 0))
    return pl.pallas_call(
        body, name="out_b_loss", grid=(S // tm,),
        in_specs=[row, pl.BlockSpec(w_out.shape, lambda i: (0, 0)), row, row],
        out_specs=[row, pl.BlockSpec((1, LANES), lambda i: (0, 0))],
        out_shape=[_sds((S, D), F32), _sds((1, LANES), F32)], compiler_params=_params())(z, w_out, h1, tgt)


def _du_a_rms_bwd(dproj, wa, x, g, dres, after):
    tm, tk = 512, NA // 2
    nk = NA // tk

    def body(a_ref, b_ref, x_ref, g_ref, dr_ref, after_ref, dx_ref, dg_ref, acc):
        i, kk = pl.program_id(0), pl.program_id(1)

        @pl.when((i == 0) & (kk == 0))
        def _():
            dg_ref[...] = jnp.zeros_like(dg_ref)

        p = _dot_nt(a_ref[...], b_ref[...])

        @pl.when(kk == 0)
        def _():
            acc[...] = p

        @pl.when(kk == nk - 1)
        def _():
            dx, dg = _rms_bwd_core(acc[...] + p, x_ref[...], g_ref[...])
            dx_ref[...] = dr_ref[...] + dx
            dg_ref[...] += dg

    assert nk == 2
    row = pl.BlockSpec((tm, D), lambda i, kk: (i, 0))
    vec = pl.BlockSpec((1, D), lambda i, kk: (0, 0))
    return pl.pallas_call(
        body, name="du_a_rms_bwd", grid=(S // tm, nk),
        in_specs=[pl.BlockSpec((tm, tk), lambda i, kk: (i, kk)), pl.BlockSpec((D, tk), lambda i, kk: (0, kk)),
                  row, vec, row, pl.BlockSpec(after.shape, lambda i, kk: (0, 0))],
        out_specs=[row, vec], out_shape=[_sds((S, D), F32), _sds((1, D), F32)],
        scratch_shapes=[pltpu.VMEM((tm, D), F32)], compiler_params=_params())(dproj, wa, x, g, dres, after)


def _du_b_rms_bwd(dpb, w_in_b, dkv, w_kv, h1, g_b, g_kv, dy):
    tm = 512

    def body(ab_ref, wb_ref, akv_ref, wkv_ref, x_ref, gb_ref, gkv_ref, dy_ref, dh_ref, dgb_ref, dgkv_ref):
        @pl.when(pl.program_id(0) == 0)
        def _():
            dgb_ref[...] = jnp.zeros_like(dgb_ref)
            dgkv_ref[...] = jnp.zeros_like(dgkv_ref)

        x = x_ref[...]
        dx1, dg1 = _rms_bwd_core(_dot_nt(ab_ref[...], wb_ref[...]), x, gb_ref[...])
        dx2, dg2 = _rms_bwd_core(_dot_nt(akv_ref[...], wkv_ref[...]), x, gkv_ref[...])
        dh_ref[...] = dy_ref[...] + dx1 + dx2
        dgb_ref[...] += dg1
        dgkv_ref[...] += dg2

    row = lambda width: pl.BlockSpec((tm, width), lambda i: (i, 0))
    whole = lambda arr: pl.BlockSpec(arr.shape, lambda i: (0, 0))
    vec = pl.BlockSpec((1, D), lambda i: (0, 0))
    return pl.pallas_call(
        body, name="du_b_rms_bwd", grid=(S // tm,),
        in_specs=[row(dpb.shape[1]), whole(w_in_b), row(dkv.shape[1]), whole(w_kv), row(D), vec, vec, row(D)],
        out_specs=[row(D), vec, vec], out_shape=[_sds((S, D), F32), _sds((1, D), F32), _sds((1, D), F32)],
        compiler_params=_params())(dpb, w_in_b, dkv, w_kv, h1, g_b, g_kv, dy)


def _gather_first(w_in_a, w_out_a, w_kv, w_in_b, w_out_b, norm_a_g):
    def body(wia_ref, woa_ref, wkv_ref, wib_ref, wob_ref, ga_ref,
             wa_g, ga_g, woa_s, wkv_s, wib_s, wob_s, wa_s, st_a, st_oa, st_kv, st_ib, st_ob, load_sems, *sems):
        sources = [wia_ref.at[0], woa_ref.at[0], wkv_ref, wib_ref.at[0], wob_ref.at[0]]
        stages = [st_a, st_oa, st_kv, st_ib, st_ob]
        loads = [pltpu.make_async_copy(src, dst, load_sems.at[i]) for i, (src, dst) in enumerate(zip(sources, stages))]
        for cp in loads:
            cp.start()
        loads[0].wait()
        wa_s[...] = st_a[...].astype(BF16)

        def cast_the_rest():
            for cp, stage, out in zip(loads[1:], stages[1:], [woa_s, wkv_s, wib_s, wob_s]):
                cp.wait()
                out[...] = stage[...].astype(BF16)

        _gather_two_level([wa_s, ga_ref], [wa_g, ga_g], sems, meanwhile=cast_the_rest)

    vmem = pl.BlockSpec(memory_space=pltpu.VMEM)
    anyspec = pl.BlockSpec(memory_space=pl.ANY)
    shard = lambda w: _sds(w.shape[-2:], BF16)
    stage = lambda w: pltpu.VMEM(w.shape[-2:], F32)
    return pl.pallas_call(
        body, name="gather_first", in_specs=[anyspec] * 5 + [vmem],
        out_specs=[anyspec, anyspec, vmem, vmem, vmem, vmem],
        out_shape=[_sds((N_DEV,) + w_in_a.shape[-2:], BF16), _sds((N_DEV,) + norm_a_g.shape, F32),
                   shard(w_out_a), shard(w_kv), shard(w_in_b), shard(w_out_b)],
        scratch_shapes=[pltpu.VMEM(w_in_a.shape[-2:], BF16), stage(w_in_a), stage(w_out_a), stage(w_kv),
                        stage(w_in_b), stage(w_out_b), pltpu.SemaphoreType.DMA((5,))] + _exchange_sems(2),
        compiler_params=pltpu.CompilerParams(vmem_limit_bytes=VMEM_LIMIT, has_side_effects=True))(
            w_in_a, w_out_a, w_kv, w_in_b, w_out_b, norm_a_g)


def _pair_reduce(slots):
    n_chip = N_DEV // 2
    _, rows, cols = slots.shape

    def body(s_ref, o_ref, own_v, sib_v, send_sems, recv_sems, local_sems):
        x, y, c = lax.axis_index("x"), lax.axis_index("y"), lax.axis_index("c")
        copies = []
        for j in range(n_chip):
            own = pltpu.make_async_copy(s_ref.at[2 * j + c], own_v.at[j], local_sems.at[j])
            give = pltpu.make_async_remote_copy(
                src_ref=s_ref.at[2 * j + 1 - c], dst_ref=sib_v.at[j], send_sem=send_sems.at[j],
                recv_sem=recv_sems.at[j], device_id=(x, y, 1 - c), device_id_type=pl.DeviceIdType.MESH)
            own.start()
            give.start()
            copies.append((own, give))
        for j, (own, give) in enumerate(copies):
            own.wait()
            give.wait()
            o_ref[j] = (own_v[j].astype(F32) + sib_v[j].astype(F32)).astype(BF16)

    half = _sds((n_chip, rows, cols), slots.dtype)
    return pl.pallas_call(
        body, name="pair_reduce", in_specs=[pl.BlockSpec(memory_space=pl.ANY)],
        out_specs=pl.BlockSpec(memory_space=pltpu.VMEM), out_shape=half,
        scratch_shapes=[pltpu.VMEM(half.shape, half.dtype), pltpu.VMEM(half.shape, half.dtype),
                        pltpu.SemaphoreType.DMA((n_chip,)), pltpu.SemaphoreType.DMA((n_chip,)),
                        pltpu.SemaphoreType.DMA((n_chip,))],
        compiler_params=pltpu.CompilerParams(vmem_limit_bytes=VMEM_LIMIT, has_side_effects=True))(slots)


def _padded_col(c):
    if c < RAW_F:
        return c
    return FOFF + (c - RAW_F) if c < RAW_G else GOFF + (c - RAW_G)


def _shard_pieces():
    width = NA_RAW // N_DEV
    pieces = []
    for d in range(N_DEV):
        cuts = [width * d] + [c for c in (RAW_F, RAW_G) if width * d < c < width * (d + 1)] + [width * (d + 1)]
        for lo, hi in zip(cuts[:-1], cuts[1:]):
            pieces.append((d, lo - width * d, _padded_col(lo), hi - lo))
    return pieces


def _unshard_wa(wa_g):
    def body(w_ref, o_ref):
        o_ref[:, FOFF + N_HEADS:NA] = jnp.zeros((TM, NA - FOFF - N_HEADS), BF16)
        for d, src, dst, width in _shard_pieces():
            o_ref[:, dst:dst + width] = w_ref[d, :, src:src + width]

    return pl.pallas_call(
        body, name="unshard_wa", grid=(D // TM,),
        in_specs=[pl.BlockSpec((N_DEV, TM, NA_RAW // N_DEV), lambda i: (0, i, 0))],
        out_specs=pl.BlockSpec((TM, NA), lambda i: (i, 0)), out_shape=_sds((D, NA), BF16),
        compiler_params=_params())(wa_g)


def _reshard_dwa(dwa):
    def body(g_ref, o_ref):
        for d, src, dst, width in _shard_pieces():
            o_ref[d, :, src:src + width] = g_ref[:, dst:dst + width]

    return pl.pallas_call(
        body, name="reshard_dwa", grid=(D // TM,), in_specs=[pl.BlockSpec((TM, NA), lambda i: (i, 0))],
        out_specs=pl.BlockSpec((N_DEV, TM, NA_RAW // N_DEV), lambda i: (0, i, 0)),
        out_shape=_sds((N_DEV, D, NA_RAW // N_DEV), dwa.dtype), compiler_params=_params())(dwa)


CHIP_FLIPS = (2, 4, 6)


def _chip_exchange_start(partial):
    n = len(CHIP_FLIPS)

    def body(p_ref, land_ref, *rest):
        sends, recvs, token = rest[:n], rest[n:2 * n], rest[2 * n + 2]
        x, y, c = lax.axis_index("x"), lax.axis_index("y"), lax.axis_index("c")
        me = 4 * x + 2 * y + c
        for idx, k in enumerate(CHIP_FLIPS):
            pltpu.make_async_remote_copy(
                src_ref=p_ref.at[(me ^ k) >> 1], dst_ref=land_ref.at[me >> 1], send_sem=sends[idx],
                recv_sem=recvs[idx], device_id=(x ^ ((k >> 2) & 1), y ^ ((k >> 1) & 1), c),
                device_id_type=pl.DeviceIdType.MESH).start()
        token[...] = jnp.zeros_like(token)

    hbm = pl.BlockSpec(memory_space=pltpu.HBM)
    sem = pl.BlockSpec(memory_space=pltpu.SEMAPHORE)
    buf = pltpu.HBM(partial.shape, partial.dtype)
    return pl.pallas_call(
        body, name="chip_exchange_start",
        out_shape=(pltpu.SemaphoreType.DMA(()),) * (2 * n) + (buf, buf, _sds((8, LANES), F32)),
        in_specs=(hbm, hbm), out_specs=(sem,) * (2 * n) + (hbm, hbm, pl.BlockSpec(memory_space=pltpu.VMEM)),
        input_output_aliases={0: 2 * n, 1: 2 * n + 1},
        compiler_params=pltpu.CompilerParams(has_side_effects=pltpu.SideEffectType.DATAFLOW_SIDE_EFFECTING))(
            pltpu.with_memory_space_constraint(partial, pltpu.HBM),
            pltpu.with_memory_space_constraint(lax.empty(partial.shape, partial.dtype), pltpu.HBM))


def _chip_exchange_wait(started, after):
    n = len(CHIP_FLIPS)
    sems, (p_thru, land_thru) = started[:2 * n], started[2 * n:2 * n + 2]

    def body(p_ref, land_ref, *rest):
        sends, recvs = rest[:n], rest[n:2 * n]
        x, y, c = lax.axis_index("x"), lax.axis_index("y"), lax.axis_index("c")
        me = 4 * x + 2 * y + c
        for idx, k in enumerate(CHIP_FLIPS):
            copy = pltpu.make_async_remote_copy(
                src_ref=p_ref.at[(me ^ k) >> 1], dst_ref=land_ref.at[(me ^ k) >> 1], send_sem=sends[idx],
                recv_sem=recvs[idx], device_id=(x ^ ((k >> 2) & 1), y ^ ((k >> 1) & 1), c),
                device_id_type=pl.DeviceIdType.MESH)
            copy.wait_send()
            copy.wait_recv()

    hbm = pl.BlockSpec(memory_space=pltpu.HBM)
    sem = pl.BlockSpec(memory_space=pltpu.SEMAPHORE)
    buf = pltpu.HBM(p_thru.shape, p_thru.dtype)
    return pl.pallas_call(
        body, name="chip_exchange_wait", out_shape=(buf, buf),
        in_specs=(hbm, hbm) + (sem,) * (2 * n) + (pl.BlockSpec(memory_space=pl.ANY),), out_specs=(hbm, hbm),
        input_output_aliases={0: 0, 1: 1},
        compiler_params=pltpu.CompilerParams(has_side_effects=pltpu.SideEffectType.DATAFLOW_SIDE_EFFECTING))(
            p_thru, land_thru, *sems, after)


def _gather_slab(slab):
    def body(s_ref, o_ref, *sems):
        _exchange_ops(["gather_rows"], [s_ref], [o_ref], sems, True, True)

    anyspec = pl.BlockSpec(memory_space=pl.ANY)
    return pl.pallas_call(
        body, name="gather_slab", in_specs=[anyspec], out_specs=anyspec,
        out_shape=_sds((N_DEV,) + slab.shape, slab.dtype), scratch_shapes=_exchange_sems(1),
        compiler_params=pltpu.CompilerParams(has_side_effects=True))(slab)


def _adamw(w, g, m, v):
    m = ADAM_B1 * m + (1.0 - ADAM_B1) * g
    v = ADAM_B2 * v + (1.0 - ADAM_B2) * (g * g)
    m_hat = m / (1.0 - ADAM_B1 ** ADAM_STEP)
    v_hat = v / (1.0 - ADAM_B2 ** ADAM_STEP)
    delta = -ADAM_LR * (m_hat / (jnp.sqrt(v_hat) + ADAM_EPS) + ADAM_WD * w)
    return delta, m, v


def _sum_adamw(recv, w, m, v, name):
    lead = w.ndim - 2
    rows, cols = w.shape[-2:]
    tr = 128
    slabs = list(recv) if isinstance(recv, tuple) else [recv]
    n_slots = slabs[0].shape[0]

    def body(*refs):
        r_ref, own_ref = refs[0], refs[len(slabs) - 1]
        w_ref, m_ref, v_ref, g_ref, d_ref, nm_ref, nv_ref = refs[len(slabs):]
        chip = (4 * lax.axis_index("x") + 2 * lax.axis_index("y") + lax.axis_index("c")) >> 1
        g = None
        for slot in range(n_slots):
            part = r_ref[slot]
            if len(slabs) == 2:
                part = jnp.where(chip == slot, own_ref[slot], part)
            g = part.astype(F32) if g is None else g + part.astype(F32)
        g_ref[...] = g
        d_ref[...], nm_ref[...], nv_ref[...] = _adamw(w_ref[...], g, m_ref[...], v_ref[...])

    blk = pl.BlockSpec((None,) * lead + (tr, cols), lambda i: (0,) * lead + (i, 0))
    slots = pl.BlockSpec((n_slots, tr, cols), lambda i: (0, i, 0))
    return pl.pallas_call(
        body, name=name, grid=(rows // tr,), in_specs=[slots] * len(slabs) + [blk, blk, blk],
        out_specs=[blk] * 4, out_shape=[_sds(w.shape, F32)] * 4, compiler_params=_params())(*slabs, w, m, v)


SLAB_ROWS = 16
SLOT = {"kv_norm_g": (8, 0, D), "norm_b_g": (9, 0, D), "b_forget": (10, 0, 16), "qnorm_a_g": (10, 128, HD),
        "knorm_a_g": (10, 256, HD), "knorm_b_g": (10, 384, HD), "qnorm_b_g": (10, 512, HD), "sinks": (10, 640, 16)}
SMALL = ["norm_a_g", "b_forget", "qnorm_a_g", "knorm_a_g", "kv_norm_g", "knorm_b_g", "norm_b_g", "qnorm_b_g", "sinks"]


LOSS_ROW = 11


def _pack_small(dg_a, dg_kv, dg_b, db_f, dgq_a, dgk_a, dgk_b, dgq_b, dsinks, lsum):
    def fold(ref):
        return ref[:, 0:HD] + ref[:, HD:2 * HD]

    def body(dga_ref, dgkv_ref, dgb_ref, dbf_ref, dgqa_ref, dgka_ref, dgkb_ref, dgqb_ref, dsk_ref, ls_ref, slab_ref):
        slab_ref[...] = jnp.zeros_like(slab_ref)
        for r in range(N_DEV):
            slab_ref[r:r + 1, 0:LANES] = dga_ref[:, LANES * r:LANES * (r + 1)]
        slab_ref[8:9, :] = dgkv_ref[...]
        slab_ref[9:10, :] = dgb_ref[...]
        slab_ref[10:11, 0:LANES] = dbf_ref[...]
        slab_ref[10:11, 128:128 + HD] = fold(dgqa_ref)
        slab_ref[10:11, 256:256 + HD] = fold(dgka_ref)
        slab_ref[10:11, 384:384 + HD] = fold(dgkb_ref)
        slab_ref[10:11, 512:512 + HD] = fold(dgqb_ref)
        slab_ref[10:11, 640:640 + LANES] = dsk_ref[...]
        slab_ref[LOSS_ROW:LOSS_ROW + 1, 0:LANES] = ls_ref[...]

    return pl.pallas_call(body, name="pack_small", out_shape=_sds((SLAB_ROWS, D), F32), compiler_params=_params())(
        dg_a, dg_kv, dg_b, db_f, dgq_a, dgk_a, dgk_b, dgq_b, dsinks, lsum)


def _small_adamw(recv, ws, ms, vs):
    k = len(SMALL)

    def body(*refs):
        r_ref = refs[0]
        w_refs, m_refs, v_refs = refs[1:1 + k], refs[1 + k:1 + 2 * k], refs[1 + 2 * k:1 + 3 * k]
        outs = refs[1 + 3 * k:1 + 7 * k]
        loss_ref, tot = refs[1 + 7 * k], refs[2 + 7 * k]
        g = r_ref[0]
        for dev in range(1, N_DEV):
            g = g + r_ref[dev]
        tot[...] = g
        loss_ref[...] = tot[LOSS_ROW:LOSS_ROW + 1, 0:LANES] * (0.5 / D)
        me = 4 * lax.axis_index("x") + 2 * lax.axis_index("y") + lax.axis_index("c")
        for p, name in enumerate(SMALL):
            if name == "norm_a_g":
                mine = lax.broadcasted_iota(jnp.int32, (N_DEV, LANES), 0) == me
                gp = jnp.sum(jnp.where(mine, tot[0:N_DEV, 0:LANES], 0.0), axis=0, keepdims=True)
            else:
                row, lo, width = SLOT[name]
                gp = tot[row:row + 1, lo:lo + width]
            d, nm, nv = _adamw(w_refs[p][...], gp, m_refs[p][...], v_refs[p][...])
            outs[p][...] = gp
            outs[k + p][...] = d
            outs[2 * k + p][...] = nm
            outs[3 * k + p][...] = nv

    shapes = [_sds(w.shape, F32) for w in ws]
    return pl.pallas_call(body, name="small_adamw", out_shape=shapes * 4 + [_sds((1, LANES), F32)],
                          scratch_shapes=[pltpu.VMEM((SLAB_ROWS, D), F32)],
                          compiler_params=_params())(recv, *ws, *ms, *vs)


def _rope_tables(positions):
    inv_freq = jnp.power(jnp.float32(ROPE_THETA), -jnp.arange(0, ROT, 2, dtype=F32) / ROT)
    ang = positions.astype(F32)[:, None] * inv_freq[None, :]
    cos, sin = jnp.cos(ang), jnp.sin(ang)
    c64 = jnp.concatenate([cos, cos, jnp.ones((S, HD - ROT), F32)], axis=-1)
    s64 = jnp.concatenate([-sin, sin, jnp.zeros((S, HD - ROT), F32)], axis=-1)
    return jnp.tile(c64, (1, 2)), jnp.tile(s64, (1, 2))


def _local_step(x, tgt, positions, g_a, wa, b_forget, gq_a, gk_a, g_kv, gk_b, g_b, gq_b, sinks,
                woa_s, wkv_s, wib_s, wob_s):
    nq = S // TQ
    cos2, sin2 = _rope_tables(positions)
    b_pad = jnp.pad(b_forget, ((0, 0), (0, LANES - N_HEADS)))

    u_a, proj, qa, ka, vb = _head_a(x, g_a, wa, gq_a, gk_a, b_pad)
    o_a, z_a, lse_a, woa_g, wkv_g, w_in_b, wob_g = _fox_fwd(
        qa, ka, vb, proj,
        rider=[("gather_rows", woa_s), ("gather_rows", wkv_s), ("gather_cols", wib_s), ("gather_rows", wob_s)])
    w_out_a, w_kv, w_out_b = woa_g.reshape(D, D), wkv_g.reshape(D, 512), wob_g.reshape(D, D)
    h1, u_kv, u_b, kv, pb, qb, ksh, vsh = _head_b(x, z_a, w_out_a, g_kv, g_b, w_kv, w_in_b, gq_b, gk_b, cos2, sin2)
    sinks1 = sinks.reshape(N_HEADS)
    o_b, z_b, lse_b = _swa_fwd(qb, ksh, vsh, pb, sinks1)
    dy, lsum = _out_b_loss(z_b, w_out_b, h1, tgt)
    dw_out_b = _mm(z_b, dy, "tn", 512, 512, S, out_dtype=BF16, name="mm_dw_out_b")
    dz_b = _mm(dy, w_out_b, "nt", 1024, 512, D, name="mm_dz_b")
    dpb, dka, dkb, dva, dvb, dsinks, dgq_b = _swa_bwd(qb, ksh, vsh, dz_b, o_b, lse_b, pb, sinks1, gq_b, cos2, sin2)
    dkv, dgk_b = _prep_kv_bwd(dka, dkb, dva, dvb, kv, gk_b, cos2, sin2)
    dw_in_b = _mm(u_b, dpb, "tn", 512, 512, S, out_dtype=BF16, name="mm_dw_in_b")
    dw_kv = _mm(u_kv, dkv, "tn", 512, 512, S, out_dtype=BF16, name="mm_dw_kv")
    dh1, dg_b, dg_kv = _du_b_rms_bwd(dpb, w_in_b, dkv, w_kv, h1, g_b, g_kv, dy)
    dw_out_a = _mm(z_a, dh1, "tn", 512, 512, S, out_dtype=BF16, name="mm_dw_out_a")
    do_a, dgate_a, delta_a = _fox_bwd_pre(dh1, w_out_a, proj, o_a)
    dk_a, dv_a, dcs, dq_a, drow, r_wob, r_wib, r_wkv, r_woa = _fox_bwd(
        qa, ka, vb, do_a, lse_a, delta_a,
        rider=[("a2a_rows", dw_out_b), ("a2a_cols", dw_in_b), ("a2a_rows", dw_kv), ("a2a_rows", dw_out_a)])
    drow_col = jnp.pad(drow.reshape(N_HEADS, S).T, ((0, 0), (0, LANES - N_HEADS)))
    dproj, dgq_a, dgk_a, db_f = _prep_a_bwd(dq_a, dk_a, dv_a, dgate_a, drow_col, dcs, proj, b_pad, gq_a, gk_a)
    dwa = _mm(u_a, dproj, "tn", 1024, 256, S, out_dtype=BF16, name="mm_dw_in_a")
    partial = _pair_reduce(_reshard_dwa(dwa))
    started = _chip_exchange_start(partial)
    dx, dg_a = _du_a_rms_bwd(dproj, wa, x, g_a, dh1, after=started[-1])
    slab = _pack_small(dg_a, dg_kv, dg_b, db_f, dgq_a, dgk_a, dgk_b, dgq_b, dsinks, lsum)
    slab_g = _gather_slab(slab)
    partial, landed = _chip_exchange_wait(started, slab_g)
    return dx, (landed, partial), r_woa, r_wkv, r_wib, r_wob, slab_g


def kernel(x, positions, norm_a_g, w_in_a, b_forget, qnorm_a_g, knorm_a_g, w_out_a, kv_norm_g, w_kv, knorm_b_g, norm_b_g, w_in_b, qnorm_b_g, sinks, w_out_b, loss_target, m_norm_a_g, m_w_in_a, m_b_forget, m_qnorm_a_g, m_knorm_a_g, m_w_out_a, m_kv_norm_g, m_w_kv, m_knorm_b_g, m_norm_b_g, m_w_in_b, m_qnorm_b_g, m_sinks, m_w_out_b, v_norm_a_g, v_w_in_a, v_b_forget, v_qnorm_a_g, v_knorm_a_g, v_w_out_a, v_kv_norm_g, v_w_kv, v_knorm_b_g, v_norm_b_g, v_w_in_b, v_qnorm_b_g, v_sinks, v_w_out_b):
    wa_g, ga_g, woa_s, wkv_s, wib_s, wob_s = _gather_first(w_in_a, w_out_a, w_kv, w_in_b, w_out_b, norm_a_g)
    dx, r_wa, r_woa, r_wkv, r_wib, r_wob, slab_g = _local_step(
        x[0], loss_target[0], positions, ga_g.reshape(1, D), _unshard_wa(wa_g), b_forget, qnorm_a_g, knorm_a_g,
        kv_norm_g.reshape(1, D), knorm_b_g.reshape(1, HD), norm_b_g, qnorm_b_g, sinks, woa_s, wkv_s, wib_s, wob_s)

    big = {}
    for name, recv, w, m, v in (
            ("w_in_a", r_wa, w_in_a, m_w_in_a, v_w_in_a), ("w_out_a", r_woa, w_out_a, m_w_out_a, v_w_out_a),
            ("w_kv", r_wkv, w_kv, m_w_kv, v_w_kv), ("w_in_b", r_wib, w_in_b, m_w_in_b, v_w_in_b),
            ("w_out_b", r_wob, w_out_b, m_w_out_b, v_w_out_b)):
        big[name] = _sum_adamw(recv, w, m, v, "adamw_" + name)

    r2 = lambda a: a.reshape(1, -1)
    small_w = dict(norm_a_g=norm_a_g, b_forget=b_forget, qnorm_a_g=qnorm_a_g, knorm_a_g=knorm_a_g,
                   kv_norm_g=kv_norm_g, knorm_b_g=knorm_b_g, norm_b_g=norm_b_g, qnorm_b_g=qnorm_b_g, sinks=sinks)
    small_m = dict(norm_a_g=m_norm_a_g, b_forget=m_b_forget, qnorm_a_g=m_qnorm_a_g, knorm_a_g=m_knorm_a_g,
                   kv_norm_g=m_kv_norm_g, knorm_b_g=m_knorm_b_g, norm_b_g=m_norm_b_g, qnorm_b_g=m_qnorm_b_g,
                   sinks=m_sinks)
    small_v = dict(norm_a_g=v_norm_a_g, b_forget=v_b_forget, qnorm_a_g=v_qnorm_a_g, knorm_a_g=v_knorm_a_g,
                   kv_norm_g=v_kv_norm_g, knorm_b_g=v_knorm_b_g, norm_b_g=v_norm_b_g, qnorm_b_g=v_qnorm_b_g,
                   sinks=v_sinks)
    res = _small_adamw(slab_g, [r2(small_w[n]) for n in SMALL], [r2(small_m[n]) for n in SMALL],
                       [r2(small_v[n]) for n in SMALL])
    k = len(SMALL)
    small = {n: [res[q * k + p].reshape(small_w[n].shape) for q in range(4)] for p, n in enumerate(SMALL)}
    loss = res[4 * k][0, 0]

    order = ["norm_a_g", "w_in_a", "b_forget", "qnorm_a_g", "knorm_a_g", "w_out_a", "kv_norm_g", "w_kv",
             "knorm_b_g", "norm_b_g", "w_in_b", "qnorm_b_g", "sinks", "w_out_b"]

    def leaf(n, q):
        return big[n][q] if n in big else small[n][q]

    outs = [loss, dx[None]]
    for q in range(4):
        outs.extend(leaf(n, q) for n in order)
    return tuple(outs)
```

```python
import jax
import jax.numpy as jnp
from jax import lax
from jax.experimental import pallas as pl
from jax.experimental.pallas import tpu as pltpu

F32, BF16 = jnp.float32, jnp.bfloat16

S = 2048
D = 1024
HD = 64
N_HEADS = 16
N_DEV = 8
NA = 4352
GOFF = 3072
FOFF = 4096
RAW_F = 3072
RAW_G = RAW_F + N_HEADS
NA_RAW = 4112
EPS = 1e-6
QSCALE = 0.125
ROPE_THETA = 500000.0
ROT = 16
WIN = 128
TQ = 256
TK = 256
KS = TQ // 2
HPS = 8
HW = HPS * HD
TM = 256
RB = 512
CB = 256
LANES = 128

ADAM_LR, ADAM_B1, ADAM_B2, ADAM_EPS, ADAM_WD, ADAM_STEP = 0.001, 0.9, 0.999, 1e-08, 0.01, 10

VMEM_LIMIT = 56 * 1024 * 1024


def _params():
    return pltpu.CompilerParams(vmem_limit_bytes=VMEM_LIMIT)


def _sds(shape, dtype):
    return jax.ShapeDtypeStruct(shape, dtype)


def _dot_nt(a, b):
    return lax.dot_general(a, b, (((1,), (1,)), ((), ())), preferred_element_type=F32)


def _dot_tn(a, b):
    return lax.dot_general(a, b, (((0,), (0,)), ((), ())), preferred_element_type=F32)


def _dot_nn(a, b):
    return lax.dot_general(a, b, (((1,), (0,)), ((), ())), preferred_element_type=F32)


def _sigmoid(g):
    return 1.0 / (1.0 + jnp.exp(-g))


def _lane_iota(shape):
    return lax.broadcasted_iota(jnp.int32, shape, len(shape) - 1)


def _flips(kind):
    return (2, 4, 6) if kind == "a2a_chips" else tuple(range(1, N_DEV))


def _send_view(kind, ref, dev):
    if kind in ("gather_rows", "gather_cols"):
        return ref
    if kind == "a2a_slots":
        return ref.at[dev]
    if kind == "a2a_chips":
        return ref.at[dev >> 1]
    if kind == "a2a_rows":
        rows = ref.shape[0] // N_DEV
        return ref.at[pl.ds(pl.multiple_of(dev * rows, rows), rows)]
    cols = ref.shape[1] // N_DEV
    return ref.at[:, pl.ds(pl.multiple_of(dev * cols, cols), cols)]


def _land_view(kind, ref, dev):
    if kind == "gather_cols":
        cols = ref.shape[1] // N_DEV
        return ref.at[:, pl.ds(pl.multiple_of(dev * cols, cols), cols)]
    if kind == "a2a_chips":
        return ref.at[dev >> 1]
    return ref.at[dev]


def _landing_sds(kind, arr):
    if kind == "gather_rows":
        return _sds((N_DEV,) + arr.shape, arr.dtype)
    if kind == "gather_cols":
        return _sds((arr.shape[0], N_DEV * arr.shape[1]), arr.dtype)
    if kind == "a2a_rows":
        return _sds((N_DEV, arr.shape[0] // N_DEV, arr.shape[1]), arr.dtype)
    if kind == "a2a_cols":
        return _sds((N_DEV, arr.shape[0], arr.shape[1] // N_DEV), arr.dtype)
    return _sds(arr.shape, arr.dtype)


def _exchange_sems(n_parts):
    n = n_parts * (N_DEV - 1)
    return [pltpu.SemaphoreType.DMA((n,)), pltpu.SemaphoreType.DMA((n,)), pltpu.SemaphoreType.DMA((n_parts,))]


def _exchange_ops(kinds, srcs, dsts, sems, start, wait):
    send_sems, recv_sems, local_sems = sems
    x, y, c = lax.axis_index("x"), lax.axis_index("y"), lax.axis_index("c")
    me = 4 * x + 2 * y + c

    def local(a):
        return pltpu.make_async_copy(_send_view(kinds[a], srcs[a], me), _land_view(kinds[a], dsts[a], me),
                                     local_sems.at[a])

    def remote(a, k, landing_dev):
        peer = (x ^ ((k >> 2) & 1), y ^ ((k >> 1) & 1), c ^ (k & 1))
        sem = a * (N_DEV - 1) + k - 1
        return pltpu.make_async_remote_copy(
            src_ref=_send_view(kinds[a], srcs[a], me ^ k), dst_ref=_land_view(kinds[a], dsts[a], landing_dev),
            send_sem=send_sems.at[sem], recv_sem=recv_sems.at[sem], device_id=peer,
            device_id_type=pl.DeviceIdType.MESH)

    pairs = [(a, k) for k in range(1, N_DEV) for a in range(len(kinds)) if k in _flips(kinds[a])]
    if start:
        for a in range(len(kinds)):
            local(a).start()
        for a, k in pairs:
            remote(a, k, me).start()
    if wait:
        for a, k in pairs:
            remote(a, k, me ^ k).wait_recv()
            remote(a, k, me).wait_send()
        for a in range(len(kinds)):
            local(a).wait()


def _gather_two_level(srcs, dsts, sems, meanwhile=None):
    send_sems, recv_sems, local_sems = sems
    x, y, c = lax.axis_index("x"), lax.axis_index("y"), lax.axis_index("c")
    me, sibling = (x, y, c), (x, y, 1 - c)
    chips = [(1 - x, y), (x, 1 - y), (1 - x, 1 - y)]

    def slot(ref, dev):
        return ref.at[4 * dev[0] + 2 * dev[1] + dev[2]]

    def copy(a, k, block, to, src=None):
        return pltpu.make_async_remote_copy(
            src_ref=slot(dsts[a], block) if src is None else src, dst_ref=slot(dsts[a], block),
            send_sem=send_sems.at[a * (N_DEV - 1) + k], recv_sem=recv_sems.at[a * (N_DEV - 1) + k],
            device_id=to, device_id_type=pl.DeviceIdType.MESH)

    parts = range(len(srcs))
    mine = [pltpu.make_async_copy(srcs[a], slot(dsts[a], me), local_sems.at[a]) for a in parts]
    first = [copy(a, 0, me, sibling, src=srcs[a]) for a in parts]
    first += [copy(a, 1 + j, me, (*chip, c), src=srcs[a]) for j, chip in enumerate(chips) for a in parts]
    for cp in mine + first:
        cp.start()
    if meanwhile is not None:
        meanwhile()
    passed = []
    for j, chip in enumerate(chips):
        for a in parts:
            copy(a, 1 + j, (*chip, c), me).wait_recv()
            fwd = copy(a, 4 + j, (*chip, c), sibling)
            fwd.start()
            passed.append(fwd)
    for a in parts:
        copy(a, 0, sibling, me).wait_recv()
        for j, chip in enumerate(chips):
            copy(a, 4 + j, (*chip, 1 - c), me).wait_recv()
    for cp in first + passed:
        cp.wait_send()
    for cp in mine:
        cp.wait()


def _call(body, *, name, args, in_specs, out_specs, out_shape, grid=(), scratch_shapes=(), aliases=None, rider=()):
    n_in, n_out, n_scr, n_r = len(in_specs), len(out_specs), len(scratch_shapes), len(rider)
    kinds = [kind for kind, _ in rider]

    def kernel_body(*refs):
        c_in, r_in = refs[:n_in], refs[n_in:n_in + n_r]
        c_out = refs[n_in + n_r:n_in + n_r + n_out]
        r_out = refs[n_in + n_r + n_out:n_in + 2 * n_r + n_out]
        rest = refs[n_in + 2 * n_r + n_out:]
        c_scr, sems = rest[:n_scr], rest[n_scr:]
        if n_r:
            assert grid, "a rider needs a gridded call"
            ids = [pl.program_id(ax) for ax in range(len(grid))]
            first, last = ids[0] == 0, ids[0] == grid[0] - 1
            for pid, size in zip(ids[1:], grid[1:]):
                first = first & (pid == 0)
                last = last & (pid == size - 1)
            pl.when(first)(lambda: _exchange_ops(kinds, r_in, r_out, sems, True, False))
        body(*c_in, *c_out, *c_scr)
        if n_r:
            pl.when(last)(lambda: _exchange_ops(kinds, r_in, r_out, sems, False, True))

    anyspec = pl.BlockSpec(memory_space=pl.ANY)
    params = pltpu.CompilerParams(vmem_limit_bytes=VMEM_LIMIT, has_side_effects=bool(n_r))
    outs = pl.pallas_call(
        kernel_body, name=name, grid=grid, in_specs=list(in_specs) + [anyspec] * n_r,
        out_specs=list(out_specs) + [anyspec] * n_r,
        out_shape=list(out_shape) + [_landing_sds(kind, arr) for kind, arr in rider],
        scratch_shapes=list(scratch_shapes) + (_exchange_sems(n_r) if n_r else []),
        input_output_aliases=aliases or {}, compiler_params=params)(*args, *[arr for _, arr in rider])
    return list(outs)


def _mm(a, b, mode, tm, tn, tk, out_dtype=F32, add=None, name="mm", rider=()):
    if mode == "nn":
        (m, k), n = a.shape, b.shape[1]
        a_spec = pl.BlockSpec((tm, tk), lambda i, j, kk: (i, kk))
        b_spec = pl.BlockSpec((tk, tn), lambda i, j, kk: (kk, j))
        dot = _dot_nn
    elif mode == "nt":
        (m, k), n = a.shape, b.shape[0]
        a_spec = pl.BlockSpec((tm, tk), lambda i, j, kk: (i, kk))
        b_spec = pl.BlockSpec((tn, tk), lambda i, j, kk: (j, kk))
        dot = _dot_nt
    else:
        (k, m), n = a.shape, b.shape[1]
        a_spec = pl.BlockSpec((tk, tm), lambda i, j, kk: (kk, i))
        b_spec = pl.BlockSpec((tk, tn), lambda i, j, kk: (kk, j))
        dot = _dot_tn
    assert m % tm == 0 and n % tn == 0 and k % tk == 0, (m, n, k, tm, tn, tk)
    nk = k // tk
    has_add = add is not None

    def body(*refs):
        if has_add:
            a_ref, b_ref, add_ref, o_ref, acc = refs
        else:
            a_ref, b_ref, o_ref, acc = refs
        p = dot(a_ref[...].astype(BF16), b_ref[...].astype(BF16))

        def finish(total):
            if has_add:
                total = add_ref[...] + total
            o_ref[...] = total.astype(out_dtype)

        if nk == 1:
            finish(p)
        else:
            kk = pl.program_id(2)

            @pl.when(kk == 0)
            def _():
                acc[...] = p

            @pl.when(kk > 0)
            def _():
                acc[...] += p

            @pl.when(kk == nk - 1)
            def _():
                finish(acc[...])

    in_specs = [a_spec, b_spec]
    args = [a, b]
    if has_add:
        in_specs.append(pl.BlockSpec((tm, tn), lambda i, j, kk: (i, j)))
        args.append(add)
    acc_shape = (tm, tn) if nk > 1 else (8, LANES)
    outs = _call(body, name=name, args=args, grid=(m // tm, n // tn, nk), in_specs=in_specs,
                 out_specs=[pl.BlockSpec((tm, tn), lambda i, j, kk: (i, j))], out_shape=[_sds((m, n), out_dtype)],
                 scratch_shapes=[pltpu.VMEM(acc_shape, F32)], rider=rider)
    return outs if rider else outs[0]


def _rms_rinv(x):
    return lax.rsqrt(jnp.mean(x * x, axis=-1, keepdims=True) + EPS)


def _rms_bwd_core(du, x, g):
    r = _rms_rinv(x)
    dug = du * g
    dx = r * (dug - x * ((r * r) * jnp.mean(dug * x, axis=-1, keepdims=True)))
    dg = jnp.sum(du * (x * r), axis=0, keepdims=True)
    return dx, dg


def _half_sum(v, lo_half):
    s0 = jnp.sum(jnp.where(lo_half, v, 0.0), axis=-1, keepdims=True)
    s1 = jnp.sum(jnp.where(lo_half, 0.0, v), axis=-1, keepdims=True)
    return jnp.where(lo_half, s0, s1)


def _head_rinv(x, lo_half):
    return lax.rsqrt(_half_sum(x * x, lo_half) * (1.0 / HD) + EPS)


def _head_norm_bwd(dn, x, g, lo_half):
    r = _head_rinv(x, lo_half)
    dng = dn * g
    dx = r * (dng - x * ((r * r) * (_half_sum(dng * x, lo_half) * (1.0 / HD))))
    dg = jnp.sum(dn * (x * r), axis=0, keepdims=True)
    return dx, dg


def _rope_swap(x, lane):
    l64 = lane & (HD - 1)
    return jnp.where(l64 < ROT // 2, pltpu.roll(x, LANES - ROT // 2, 1), pltpu.roll(x, ROT // 2, 1))


def _rope_fwd(x, cos, sin, lane):
    return x * cos + _rope_swap(x, lane) * sin


def _rope_bwd(dy, cos, sin, lane):
    return dy * cos + jnp.where((lane & (HD - 1)) < ROT, _rope_swap(dy * sin, lane), 0.0)


def _g2(g_ref):
    g = g_ref[...]
    return jnp.concatenate([g, g], axis=-1)


def _pairs(width):
    return [slice(LANES * c, LANES * (c + 1)) for c in range(width // LANES)]


def _pick_lane(block, lane, idx):
    return jnp.sum(jnp.where(lane == idx, block, 0.0), axis=-1, keepdims=True)


def _head_a(x, g, wa, gq, gk, b_pad):
    def body(x_ref, g_ref, w_ref, gq_ref, gk_ref, b_ref, u_ref, p_ref, qo_ref, ko_ref, vo_ref, c_ref, cbc_ref, carry):
        @pl.when(pl.program_id(0) == 0)
        def _():
            carry[...] = jnp.zeros_like(carry)

        xv = x_ref[...]
        u = ((xv * _rms_rinv(xv)) * g_ref[...]).astype(BF16)
        u_ref[...] = u
        for lo in range(0, NA, D):
            hi = min(lo + D, NA)
            p_ref[:, lo:hi] = _dot_nn(u, w_ref[:, lo:hi])
        lane = _lane_iota((TM, LANES))
        lo_half = lane < HD
        gq2, gk2 = _g2(gq_ref), _g2(gk_ref)
        for c in _pairs(D):
            q = p_ref[:, c]
            k = p_ref[:, D + c.start:D + c.stop]
            qo_ref[:, c] = (((q * _head_rinv(q, lo_half)) * gq2) * QSCALE).astype(BF16)
            ko_ref[:, c] = ((k * _head_rinv(k, lo_half)) * gk2).astype(BF16)
        vo_ref[...] = p_ref[:, 2 * D:3 * D].astype(BF16)

        z = p_ref[:, FOFF:FOFF + LANES] + b_ref[...]
        logf = jnp.minimum(z, 0.0) - jnp.log1p(jnp.exp(-jnp.abs(z)))
        r = lax.broadcasted_iota(jnp.int32, (TM, TM), 0)
        cc = lax.broadcasted_iota(jnp.int32, (TM, TM), 1)
        tri = (r >= cc).astype(F32)
        loc = jnp.dot(tri, logf, precision=lax.Precision.HIGHEST, preferred_element_type=F32) + carry[0:1, :]
        c_ref[...] = loc
        carry[0:1, :] = loc[TM - 1:TM, :]
        for h in range(N_HEADS):
            cbc_ref[:, LANES * h:LANES * (h + 1)] = jnp.broadcast_to(_pick_lane(loc, lane, h), (TM, LANES))

    row = lambda width: pl.BlockSpec((TM, width), lambda i: (i, 0))
    whole = lambda arr: pl.BlockSpec(arr.shape, lambda i: (0,) * arr.ndim)
    return pl.pallas_call(
        body, name="head_a", grid=(S // TM,),
        in_specs=[row(D), whole(g), whole(wa), whole(gq), whole(gk), whole(b_pad)],
        out_specs=[row(D), row(NA), row(D), row(D), row(D), row(LANES), row(N_HEADS * LANES)],
        out_shape=[_sds((S, D), BF16), _sds((S, NA), F32)] + [_sds((S, D), BF16)] * 3
        + [_sds((S, LANES), F32), _sds((S, N_HEADS * LANES), F32)],
        scratch_shapes=[pltpu.VMEM((8, LANES), F32)], compiler_params=_params())(x, g, wa, gq, gk, b_pad)


def _key_le_query(offset, keys=TK):
    r = lax.broadcasted_iota(jnp.int32, (keys, TQ), 0)
    c = lax.broadcasted_iota(jnp.int32, (keys, TQ), 1)
    return (r + offset) <= c


def _widen(tile):
    return jnp.concatenate([tile] * (TQ // LANES), axis=1)


def _fox_fwd(qn, kn, vb, proj, crow, cbc, rider=()):
    nq = S // TQ

    def body(q_ref, k_ref, v_ref, g_ref, cq_ref, cbc_ref, o_ref, z_ref, lse_ref, st_s, pt_s):
        i = pl.program_id(1)
        qs = [q_ref[:, HD * hh:HD * (hh + 1)] for hh in range(HPS)]
        cqs = [cq_ref[hh, 0] for hh in range(HPS)]

        def scores(s, hh):
            off = pl.multiple_of(s * KS, KS)
            kj = k_ref[pl.ds(off, KS), HD * hh:HD * (hh + 1)]
            return (_dot_nt(kj, qs[hh]) + cqs[hh]) - _widen(cbc_ref[pl.ds(off, KS), LANES * hh:LANES * (hh + 1)])

        def values(s, hh, pt):
            off = pl.multiple_of(s * KS, KS)
            return _dot_tn(v_ref[pl.ds(off, KS), HD * hh:HD * (hh + 1)], pt)

        def step(s, slot, carries, mask=None, last=False):
            if not last:
                for hh in range(HPS):
                    st_s[1 - slot, hh] = scores(s + 1, hh)
            pvs = [values(jnp.maximum(s - 1, 0), hh, pt_s[1 - slot, hh]) for hh in range(HPS)]
            out = []
            for hh in range(HPS):
                m, l, acc = carries[hh]
                st = st_s[slot, hh]
                if mask is not None:
                    st = jnp.where(mask, st, -jnp.inf)
                m_new = jnp.maximum(m, jnp.max(st, axis=0, keepdims=True))
                pt = jnp.exp(st - m_new)
                alpha = jnp.exp(m - m_new)
                pt_s[slot, hh] = pt.astype(BF16)
                out.append((m_new, alpha * l + jnp.sum(pt, axis=0, keepdims=True), alpha * (acc + pvs[hh])))
            return tuple(out)

        for hh in range(HPS):
            st_s[0, hh] = scores(0, hh)
            pt_s[1, hh] = jnp.zeros((KS, TQ), BF16)
        one = (jnp.full((1, TQ), -jnp.inf, F32), jnp.zeros((1, TQ), F32), jnp.zeros((HD, TQ), F32))
        carries = lax.fori_loop(0, i, lambda t, cr: step(2 * t + 1, 1, step(2 * t, 0, cr)), (one,) * HPS)
        carries = step(2 * i, 0, carries, mask=_key_le_query(0, KS))
        carries = step(2 * i + 1, 1, carries, mask=_key_le_query(KS, KS), last=True)
        accs = []
        for hh in range(HPS):
            m, l, acc = carries[hh]
            acc = acc + values(2 * i + 1, hh, pt_s[1, hh])
            accs.append(acc / l)
            lse_ref[hh, 0] = m + jnp.log(l)
        o = jnp.concatenate(accs, axis=0).T
        o_ref[...] = o
        g = g_ref[...]
        z_ref[...] = (o * (g * _sigmoid(g))).astype(BF16)

    qblk = pl.BlockSpec((TQ, HW), lambda hp, i: (i, hp))
    full = pl.BlockSpec((S, HW), lambda hp, i: (0, hp))
    rows = pl.BlockSpec((HPS, 1, 1, TQ), lambda hp, i: (hp, i, 0, 0))
    return _call(
        body, name="fox_fwd", args=(qn, kn, vb, proj, crow, cbc), grid=(N_HEADS // HPS, nq),
        in_specs=[qblk, full, full,
                  pl.BlockSpec((TQ, HW), lambda hp, i: (i, GOFF // HW + hp)),
                  rows, pl.BlockSpec((S, HPS * LANES), lambda hp, i: (0, hp))],
        out_specs=[qblk, qblk, rows],
        out_shape=[_sds((S, D), F32), _sds((S, D), BF16), _sds((N_HEADS, nq, 1, TQ), F32)],
        scratch_shapes=[pltpu.VMEM((2, HPS, KS, TQ), F32), pltpu.VMEM((2, HPS, KS, TQ), BF16)], rider=rider)


def _fox_bwd_pre(dh, w_out, proj, o):
    nq = S // TQ

    def body(dh_ref, w_ref, g_ref, o_ref, do_ref, dg_ref, delta_ref):
        g = g_ref[...]
        sg = _sigmoid(g)
        dzv = _dot_nt(dh_ref[...].astype(BF16), w_ref[...])
        ov = o_ref[...]
        do = dzv * (g * sg)
        dg_ref[...] = (dzv * ov * (sg * (1.0 + g * (1.0 - sg)))).astype(BF16)
        do_ref[...] = do.astype(BF16)
        prod_t = (do * ov).T
        for h in range(N_HEADS):
            delta_ref[h, 0] = jnp.sum(prod_t[HD * h:HD * (h + 1), :], axis=0, keepdims=True)

    row = pl.BlockSpec((TQ, D), lambda i: (i, 0))
    return pl.pallas_call(
        body, name="fox_bwd_pre", grid=(nq,),
        in_specs=[row, pl.BlockSpec(w_out.shape, lambda i: (0, 0)), pl.BlockSpec((TQ, D), lambda i: (i, GOFF // D)), row],
        out_specs=[row, row, pl.BlockSpec((N_HEADS, 1, 1, TQ), lambda i: (0, i, 0, 0))],
        out_shape=[_sds((S, D), BF16), _sds((S, D), BF16), _sds((N_HEADS, nq, 1, TQ), F32)],
        compiler_params=_params())(dh, w_out, proj, o)


def _fox_bwd(qn, kn, vb, dob, lse, delta, crow, cbc, rider=()):
    nq, nkb = S // TQ, S // TK

    def body(q_ref, k_ref, v_ref, do_ref, lse_ref, del_ref, cq_ref, cbc_ref,
             dk_ref, dv_ref, dcs_ref, dq_ref, dr_ref, st_s, dp_s, pt_s, ds_s, dq_acc, dr_acc):
        j = pl.program_id(1)

        @pl.when(j == 0)
        def _():
            dq_acc[...] = jnp.zeros_like(dq_acc)
            dr_acc[...] = jnp.zeros_like(dr_acc)

        kjs = [k_ref[:, HD * hh:HD * (hh + 1)] for hh in range(HPS)]
        vjs = [v_ref[:, HD * hh:HD * (hh + 1)] for hh in range(HPS)]

        def rows_of(ref, u, hh):
            off = pl.multiple_of(u * TQ, TQ)
            return ref[pl.ds(off, TQ), HD * hh:HD * (hh + 1)]

        def products(u, hh):
            st = (_dot_nt(kjs[hh], rows_of(q_ref, u, hh)) + cq_ref[hh, u]) - _widen(
                cbc_ref[:, LANES * hh:LANES * (hh + 1)])
            return st, _dot_nt(vjs[hh], rows_of(do_ref, u, hh))

        def step(u, slot, carries, masked=False):
            nxt = jnp.minimum(u + 1, nq - 1)
            for hh in range(HPS):
                st_s[1 - slot, hh], dp_s[1 - slot, hh] = products(nxt, hh)
            prev = jnp.maximum(u - 1, 0)
            dvs = [_dot_nn(pt_s[1 - slot, hh], rows_of(do_ref, prev, hh)) for hh in range(HPS)]
            dks = [_dot_nn(ds_s[1 - slot, hh], rows_of(q_ref, prev, hh)) for hh in range(HPS)]
            for hh in range(HPS):
                dq_acc[hh, prev] += _dot_tn(kjs[hh], ds_s[1 - slot, hh])
            out = []
            for hh in range(HPS):
                dk, dv, dcs = carries[hh]
                st = st_s[slot, hh]
                if masked:
                    st = jnp.where(_key_le_query((j - u) * TQ), st, -jnp.inf)
                pt = jnp.exp(st - lse_ref[hh, u])
                dst = pt * (dp_s[slot, hh] - del_ref[hh, u])
                pt_s[slot, hh] = pt.astype(BF16)
                ds_s[slot, hh] = dst.astype(BF16)
                dr_acc[hh, u] += jnp.sum(dst, axis=0, keepdims=True)
                out.append((dk + dks[hh], dv + dvs[hh], dcs + (dst[:, :LANES] + dst[:, LANES:])))
            return tuple(out)

        t0 = j // 2
        for hh in range(HPS):
            st_s[0, hh], dp_s[0, hh] = products(2 * t0, hh)
            pt_s[1, hh] = jnp.zeros((TK, TQ), BF16)
            ds_s[1, hh] = jnp.zeros((TK, TQ), BF16)
        one = (jnp.zeros((TK, HD), F32), jnp.zeros((TK, HD), F32), jnp.zeros((TK, LANES), F32))
        carries = step(2 * t0 + 1, 1, step(2 * t0, 0, (one,) * HPS, masked=True), masked=True)
        carries = lax.fori_loop(t0 + 1, nq // 2, lambda t, cr: step(2 * t + 1, 1, step(2 * t, 0, cr)), carries)
        dks, dvs = [], []
        lane = _lane_iota((TK, LANES))
        dcs_all = jnp.zeros((TK, LANES), F32)
        for hh in range(HPS):
            dk, dv, dcs = carries[hh]
            dks.append(dk + _dot_nn(ds_s[1, hh], rows_of(q_ref, nq - 1, hh)))
            dvs.append(dv + _dot_nn(pt_s[1, hh], rows_of(do_ref, nq - 1, hh)))
            dq_acc[hh, nq - 1] += _dot_tn(kjs[hh], ds_s[1, hh])
            dcs_all = jnp.where(lane == HPS * pl.program_id(0) + hh, -jnp.sum(dcs, axis=-1, keepdims=True), dcs_all)
        dcs_ref[0] = dcs_all
        dk_ref[...] = jnp.concatenate(dks, axis=-1)
        dv_ref[...] = jnp.concatenate(dvs, axis=-1).astype(BF16)

        @pl.when(j == nkb - 1)
        def _():
            for i in range(nq):
                dq_ref[TQ * i:TQ * (i + 1), :] = jnp.concatenate([dq_acc[hh, i] for hh in range(HPS)], axis=0).T
            dr_ref[...] = dr_acc[...]

    kblk = pl.BlockSpec((TK, HW), lambda hp, j: (j, hp))
    full = pl.BlockSpec((S, HW), lambda hp, j: (0, hp))
    rows = pl.BlockSpec((HPS, nq, 1, TQ), lambda hp, j: (hp, 0, 0, 0))
    cblk = pl.BlockSpec((TK, HPS * LANES), lambda hp, j: (j, hp))
    return _call(
        body, name="fox_bwd", args=(qn, kn, vb, dob, lse, delta, crow, cbc), grid=(N_HEADS // HPS, nkb),
        in_specs=[full, kblk, kblk, full, rows, rows, rows, cblk],
        out_specs=[kblk, kblk, pl.BlockSpec((1, TK, LANES), lambda hp, j: (hp, j, 0)), full, rows],
        out_shape=[_sds((S, D), F32), _sds((S, D), BF16), _sds((N_HEADS // HPS, S, LANES), F32), _sds((S, D), F32),
                   _sds((N_HEADS, nq, 1, TQ), F32)],
        scratch_shapes=[pltpu.VMEM((2, HPS, TK, TQ), F32), pltpu.VMEM((2, HPS, TK, TQ), F32),
                        pltpu.VMEM((2, HPS, TK, TQ), BF16), pltpu.VMEM((2, HPS, TK, TQ), BF16),
                        pltpu.VMEM((HPS, nq, HD, TQ), F32), pltpu.VMEM((HPS, nq, 1, TQ), F32)], rider=rider)


def _prep_a_bwd(dq, dk, dv, dgate, drow, dcs, proj, b_pad, gq, gk):
    nt = S // TM

    def body(dq_ref, dk_ref, dv_ref, dgt_ref, dr_ref, dcs_ref, xq_ref, xk_ref, f_ref, b_ref, gq_ref, gk_ref,
             o_ref, dgq_ref, dgk_ref, db_ref, carry):
        @pl.when(pl.program_id(0) == 0)
        def _():
            carry[...] = jnp.zeros_like(carry)
            dgq_ref[...] = jnp.zeros_like(dgq_ref)
            dgk_ref[...] = jnp.zeros_like(dgk_ref)
            db_ref[...] = jnp.zeros_like(db_ref)

        lane = _lane_iota((TM, LANES))
        lo_half = lane < HD
        gq2, gk2 = _g2(gq_ref), _g2(gk_ref)
        dgq, dgk = jnp.zeros((1, LANES), F32), jnp.zeros((1, LANES), F32)
        for c in _pairs(D):
            dxq, dg = _head_norm_bwd(dq_ref[:, c] * QSCALE, xq_ref[:, c], gq2, lo_half)
            o_ref[:, c] = dxq.astype(BF16)
            dgq = dgq + dg
            dxk, dg = _head_norm_bwd(dk_ref[:, c], xk_ref[:, c], gk2, lo_half)
            o_ref[:, D + c.start:D + c.stop] = dxk.astype(BF16)
            dgk = dgk + dg
        dgq_ref[...] += dgq
        dgk_ref[...] += dgk
        o_ref[:, 2 * D:3 * D] = dv_ref[...]
        o_ref[:, GOFF:GOFF + D] = dgt_ref[...]

        dc = dr_ref[...]
        for group in range(N_HEADS // HPS):
            dc = dc + dcs_ref[group]
        r = lax.broadcasted_iota(jnp.int32, (TM, TM), 0)
        c = lax.broadcasted_iota(jnp.int32, (TM, TM), 1)
        tri = (c >= r).astype(F32)
        dlogf = jnp.dot(tri, dc, precision=lax.Precision.HIGHEST, preferred_element_type=F32) + carry[0:1, :]
        carry[0:1, :] = dlogf[0:1, :]
        df = dlogf * (1.0 / (1.0 + jnp.exp(f_ref[...] + b_ref[...])))
        db_ref[...] += jnp.sum(df, axis=0, keepdims=True)
        o_ref[:, FOFF:FOFF + LANES] = df.astype(BF16)
        o_ref[:, FOFF + LANES:NA] = jnp.zeros((TM, NA - FOFF - LANES), BF16)

    rev = lambda width, col: pl.BlockSpec((TM, width), lambda i: (nt - 1 - i, col))
    gspec = pl.BlockSpec((1, HD), lambda i: (0, 0))
    acc = pl.BlockSpec((1, LANES), lambda i: (0, 0))
    return pl.pallas_call(
        body, name="prep_a_bwd", grid=(nt,),
        in_specs=[rev(D, 0), rev(D, 0), rev(D, 0), rev(D, 0), rev(LANES, 0),
                  pl.BlockSpec((N_HEADS // HPS, TM, LANES), lambda i: (0, nt - 1 - i, 0)),
                  rev(D, 0), rev(D, 1), rev(LANES, FOFF // LANES), acc, gspec, gspec],
        out_specs=[rev(NA, 0), acc, acc, acc],
        out_shape=[_sds((S, NA), BF16)] + [_sds((1, LANES), F32)] * 3,
        scratch_shapes=[pltpu.VMEM((8, LANES), F32)],
        compiler_params=_params())(dq, dk, dv, dgate, drow, dcs, proj, proj, proj, b_pad, gq, gk)


def _head_b(x, z_a, w_out_a, g_kv, g_b, w_kv, w_in_b, gq, gk, cos2, sin2):
    nkv = w_kv.shape[1] // 2

    def body(x_ref, z_ref, wo_ref, gkv_ref, gb_ref, wkv_ref, wb_ref, gq_ref, gk_ref, c_ref, s_ref,
             h_ref, ukv_ref, ub_ref, kv_ref, pb_ref, qo_ref, ko_ref, vo_ref):
        xv = x_ref[...] + _dot_nn(z_ref[...], wo_ref[...])
        h_ref[...] = xv
        xn = xv * _rms_rinv(xv)
        ukv = (xn * gkv_ref[...]).astype(BF16)
        ub = (xn * gb_ref[...]).astype(BF16)
        ukv_ref[...] = ukv
        ub_ref[...] = ub
        kv_ref[...] = _dot_nn(ukv, wkv_ref[...])
        for lo in range(0, 2 * D, D):
            pb_ref[:, lo:lo + D] = _dot_nn(ub, wb_ref[:, lo:lo + D])
        lane = _lane_iota((TM, LANES))
        lo_half = lane < HD
        cos, sin = c_ref[...], s_ref[...]
        gq2, gk2 = _g2(gq_ref), _g2(gk_ref)
        for c in _pairs(D):
            q = pb_ref[:, c]
            qo_ref[:, c] = (_rope_fwd((q * _head_rinv(q, lo_half)) * gq2, cos, sin, lane) * QSCALE).astype(BF16)
        for c in _pairs(nkv):
            k = kv_ref[:, c]
            ko_ref[:, c] = _rope_fwd((k * _head_rinv(k, lo_half)) * gk2, cos, sin, lane).astype(BF16)
        vo_ref[...] = kv_ref[:, nkv:2 * nkv].astype(BF16)

    row = lambda width: pl.BlockSpec((TM, width), lambda i: (i, 0))
    whole = lambda arr: pl.BlockSpec(arr.shape, lambda i: (0,) * arr.ndim)
    return pl.pallas_call(
        body, name="head_b", grid=(S // TM,),
        in_specs=[row(D), row(D), whole(w_out_a), whole(g_kv), whole(g_b), whole(w_kv), whole(w_in_b), whole(gq),
                  whole(gk), row(LANES), row(LANES)],
        out_specs=[row(D), row(D), row(D), row(2 * nkv), row(2 * D), row(D), row(nkv), row(nkv)],
        out_shape=[_sds((S, D), F32), _sds((S, D), BF16), _sds((S, D), BF16), _sds((S, 2 * nkv), F32),
                   _sds((S, 2 * D), F32), _sds((S, D), BF16), _sds((S, nkv), BF16), _sds((S, nkv), BF16)],
        compiler_params=_params())(x, z_a, w_out_a, g_kv, g_b, w_kv, w_in_b, gq, gk, cos2, sin2)


N_KV, GRP = 4, 4


def _swa_mask(n):
    r = lax.broadcasted_iota(jnp.int32, (2 * WIN, GRP * WIN), 0)
    q = lax.broadcasted_iota(jnp.int32, (2 * WIN, GRP * WIN), 1) & (WIN - 1)
    return (r > q) & (r <= q + WIN) & ((r >= WIN) | (n > 0))


def _stack4(ref_or_val, base):
    return jnp.concatenate([ref_or_val[:, base + HD * g: base + HD * (g + 1)] for g in range(GRP)], axis=0)


def _unstack4(xt):
    return jnp.concatenate([xt[:, WIN * g:WIN * (g + 1)] for g in range(GRP)], axis=0).T


def _band(prev_ref, cur_ref, kh):
    return jnp.concatenate([prev_ref[:, HD * kh:HD * (kh + 1)], cur_ref[:, HD * kh:HD * (kh + 1)]], axis=0)


def _sink_row(s_ref, first):
    lane = _lane_iota((1, GRP * WIN))
    row = jnp.full((1, GRP * WIN), s_ref[first + GRP - 1], F32)
    for g in range(GRP - 2, -1, -1):
        row = jnp.where(lane < WIN * (g + 1), s_ref[first + g], row)
    return row


def _swa_fwd(qb, ksh, vsh, pb, sinks):
    nb = S // WIN

    def body(q_ref, kp_ref, kc_ref, vp_ref, vc_ref, g_ref, s_ref, o_ref, z_ref, lse_ref):
        n = pl.program_id(0)
        valid = _swa_mask(n)
        outs = []
        for kh in range(N_KV):
            kb, vb = _band(kp_ref, kc_ref, kh), _band(vp_ref, vc_ref, kh)
            st = jnp.where(valid, _dot_nt(kb, _stack4(q_ref, GRP * HD * kh)), -jnp.inf)
            sink = _sink_row(s_ref, GRP * kh)
            m = jnp.maximum(jnp.max(st, axis=0, keepdims=True), sink)
            pt = jnp.exp(st - m)
            l = jnp.sum(pt, axis=0, keepdims=True) + jnp.exp(sink - m)
            outs.append(_unstack4(_dot_tn(vb, pt.astype(BF16)) / l))
            lse = m + jnp.log(l)
            for g in range(GRP):
                lse_ref[GRP * kh + g, 0] = lse[:, WIN * g:WIN * (g + 1)]
        o = jnp.concatenate(outs, axis=-1)
        o_ref[...] = o
        g = g_ref[...]
        z_ref[...] = (o * (g * _sigmoid(g))).astype(BF16)

    row = pl.BlockSpec((WIN, D), lambda n: (n, 0))
    prev = pl.BlockSpec((WIN, N_KV * HD), lambda n: (jnp.maximum(n - 1, 0), 0))
    cur = pl.BlockSpec((WIN, N_KV * HD), lambda n: (n, 0))
    return pl.pallas_call(
        body, name="swa_fwd", grid=(nb,),
        in_specs=[row, prev, cur, prev, cur, pl.BlockSpec((WIN, D), lambda n: (n, 1)),
                  pl.BlockSpec(memory_space=pltpu.SMEM)],
        out_specs=[row, row, pl.BlockSpec((N_HEADS, 1, 1, WIN), lambda n: (0, n, 0, 0))],
        out_shape=[_sds((S, D), F32), _sds((S, D), BF16), _sds((N_HEADS, nb, 1, WIN), F32)],
        compiler_params=_params())(qb, ksh, ksh, vsh, vsh, pb, sinks)


def _swa_bwd(qb, ksh, vsh, dz, o, lse, pb, sinks, gq, cos2, sin2):
    nb = S // WIN

    def body(q_ref, kp_ref, kc_ref, vp_ref, vc_ref, dz_ref, o_ref, lse_ref, x_ref, g_ref, s_ref, gq_ref, c_ref, sn_ref,
             dpb_ref, dka_ref, dkb_ref, dva_ref, dvb_ref, dsink_ref, dgq_ref):
        n = pl.program_id(0)

        @pl.when(n == 0)
        def _():
            dsink_ref[...] = jnp.zeros_like(dsink_ref)
            dgq_ref[...] = jnp.zeros_like(dgq_ref)

        valid = _swa_mask(n)
        g = g_ref[...]
        sg = _sigmoid(g)
        dzv = dz_ref[...]
        ov = o_ref[...]
        do = dzv * (g * sg)
        dpb_ref[:, D:2 * D] = (dzv * ov * (sg * (1.0 + g * (1.0 - sg)))).astype(BF16)
        prod_t = (do * ov).T
        lane1 = _lane_iota((1, LANES))
        dqs, dkas, dkbs, dvas, dvbs = [], [], [], [], []
        dsink = jnp.zeros((1, LANES), F32)
        for kh in range(N_KV):
            kb, vb = _band(kp_ref, kc_ref, kh), _band(vp_ref, vc_ref, kh)
            base = GRP * HD * kh
            qs = _stack4(q_ref, base)
            dos = _stack4(do, base).astype(BF16)
            delta = jnp.concatenate(
                [jnp.sum(prod_t[base + HD * gg:base + HD * (gg + 1), :], axis=0, keepdims=True)
                 for gg in range(GRP)], axis=1)
            lse = jnp.concatenate([lse_ref[GRP * kh + gg, 0] for gg in range(GRP)], axis=1)
            st = jnp.where(valid, _dot_nt(kb, qs), -jnp.inf)
            pt = jnp.exp(st - lse)
            dst = pt * (_dot_nt(vb, dos) - delta)
            dsb = dst.astype(BF16)
            dqs.append(_unstack4(_dot_tn(kb, dsb)))
            dkband = _dot_nn(dsb, qs)
            dvband = _dot_nn(pt.astype(BF16), dos)
            dkbs.append(dkband[0:WIN, :])
            dkas.append(dkband[WIN:2 * WIN, :])
            dvbs.append(dvband[0:WIN, :])
            dvas.append(dvband[WIN:2 * WIN, :])
            ps_delta = jnp.exp(_sink_row(s_ref, GRP * kh) - lse) * delta
            for gg in range(GRP):
                val = jnp.sum(ps_delta[:, WIN * gg:WIN * (gg + 1)], axis=1, keepdims=True)
                dsink = dsink - jnp.where(lane1 == GRP * kh + gg, val, 0.0)
        dka_ref[...] = jnp.concatenate(dkas, axis=-1)
        dkb_ref[...] = jnp.concatenate(dkbs, axis=-1)
        dva_ref[...] = jnp.concatenate(dvas, axis=-1)
        dvb_ref[...] = jnp.concatenate(dvbs, axis=-1)
        dsink_ref[...] += dsink

        lane = _lane_iota((WIN, LANES))
        g2, cos, sin = _g2(gq_ref), c_ref[...], sn_ref[...]
        dg_tot = jnp.zeros((1, LANES), F32)
        for kh in range(N_KV):
            for c in _pairs(GRP * HD):
                cols = slice(GRP * HD * kh + c.start, GRP * HD * kh + c.stop)
                dn = _rope_bwd(dqs[kh][:, c] * QSCALE, cos, sin, lane)
                dx, dg = _head_norm_bwd(dn, x_ref[:, cols], g2, lane < HD)
                dpb_ref[:, cols] = dx.astype(BF16)
                dg_tot = dg_tot + dg
        dgq_ref[...] += dg_tot

    row = pl.BlockSpec((WIN, D), lambda n: (n, 0))
    prev = pl.BlockSpec((WIN, N_KV * HD), lambda n: (jnp.maximum(n - 1, 0), 0))
    cur = pl.BlockSpec((WIN, N_KV * HD), lambda n: (n, 0))
    acc = pl.BlockSpec((1, LANES), lambda n: (0, 0))
    tab = pl.BlockSpec((WIN, LANES), lambda n: (n, 0))
    return pl.pallas_call(
        body, name="swa_bwd", grid=(nb,),
        in_specs=[row, prev, cur, prev, cur, row, row, pl.BlockSpec((N_HEADS, 1, 1, WIN), lambda n: (0, n, 0, 0)),
                  row, pl.BlockSpec((WIN, D), lambda n: (n, 1)), pl.BlockSpec(memory_space=pltpu.SMEM),
                  pl.BlockSpec((1, HD), lambda n: (0, 0)), tab, tab],
        out_specs=[pl.BlockSpec((WIN, 2 * D), lambda n: (n, 0)), cur, cur, cur, cur, acc, acc],
        out_shape=[_sds((S, 2 * D), BF16)] + [_sds((S, 256), F32)] * 4 + [_sds((1, LANES), F32)] * 2,
        compiler_params=_params())(qb, ksh, ksh, vsh, vsh, dz, o, lse, pb, pb, sinks, gq, cos2, sin2)


def _prep_kv_bwd(dka, dkb, dva, dvb, kv, gk, cos2, sin2):
    nt = S // RB
    per = RB // WIN

    def shifted(cur_ref, nxt_ref, has_next):
        return jnp.concatenate([cur_ref[WIN:RB, :], jnp.where(has_next, nxt_ref[...], 0.0)], axis=0)

    def body(dka_ref, dkb_ref, dkn_ref, dva_ref, dvb_ref, dvn_ref, x_ref, g_ref, c_ref, s_ref, o_ref, dgk_ref):
        i, j = pl.program_id(0), pl.program_id(1)

        @pl.when((i == 0) & (j == 0))
        def _():
            dgk_ref[...] = jnp.zeros_like(dgk_ref)

        has_next = i < nt - 1

        @pl.when(j == 0)
        def _():
            lane = _lane_iota((RB, LANES))
            g2, cos, sin = _g2(g_ref), c_ref[...], s_ref[...]
            dy_all = dka_ref[...] + shifted(dkb_ref, dkn_ref, has_next)
            dg_tot = jnp.zeros((1, LANES), F32)
            for c in _pairs(CB):
                dn = _rope_bwd(dy_all[:, c], cos, sin, lane)
                dx, dg = _head_norm_bwd(dn, x_ref[:, c], g2, lane < HD)
                o_ref[:, c] = dx.astype(BF16)
                dg_tot = dg_tot + dg
            dgk_ref[...] += dg_tot

        @pl.when(j == 1)
        def _():
            o_ref[...] = (dva_ref[...] + shifted(dvb_ref, dvn_ref, has_next)).astype(BF16)

    cur = pl.BlockSpec((RB, CB), lambda i, j: (i, 0))
    nxt = pl.BlockSpec((WIN, CB), lambda i, j: (jnp.minimum(per * (i + 1), S // WIN - 1), 0))
    tab = pl.BlockSpec((RB, LANES), lambda i, j: (i, 0))
    return pl.pallas_call(
        body, name="prep_kv_bwd", grid=(nt, 2),
        in_specs=[cur, cur, nxt, cur, cur, nxt, cur, pl.BlockSpec((1, HD), lambda i, j: (0, 0)), tab, tab],
        out_specs=[pl.BlockSpec((RB, CB), lambda i, j: (i, j)), pl.BlockSpec((1, LANES), lambda i, j: (0, 0))],
        out_shape=[_sds((S, 2 * CB), BF16), _sds((1, LANES), F32)],
        compiler_params=_params())(dka, dkb, dkb, dva, dvb, dvb, kv, gk, cos2, sin2)


def _out_b_loss(z, w_out, h1, tgt):
    tm = 512

    def body(z_ref, w_ref, h_ref, t_ref, dy_ref, l_ref):
        @pl.when(pl.program_id(0) == 0)
        def _():
            l_ref[...] = jnp.zeros_like(l_ref)

        e = (h_ref[...] + _dot_nn(z_ref[...], w_ref[...])) - t_ref[...]
        dy_ref[...] = e * (1.0 / D)
        l_ref[...] += jnp.sum(jnp.sum(e * e, axis=-1, keepdims=True), axis=0, keepdims=True)

    row = pl.BlockSpec((tm, D), lambda i: (i, 0))
    return pl.pallas_call(
        body, name="out_b_loss", grid=(S // tm,),
        in_specs=[row, pl.BlockSpec(w_out.shape, lambda i: (0, 0)), row, row],
        out_specs=[row, pl.BlockSpec((1, LANES), lambda i: (0, 0))],
        out_shape=[_sds((S, D), F32), _sds((1, LANES), F32)], compiler_params=_params())(z, w_out, h1, tgt)


def _du_a_rms_bwd(dproj, wa, x, g, dres, after):
    tm, tk = 512, NA // 2
    nk = NA // tk

    def body(a_ref, b_ref, x_ref, g_ref, dr_ref, after_ref, dx_ref, dg_ref, acc):
        i, kk = pl.program_id(0), pl.program_id(1)

        @pl.when((i == 0) & (kk == 0))
        def _():
            dg_ref[...] = jnp.zeros_like(dg_ref)

        p = _dot_nt(a_ref[...], b_ref[...])

        @pl.when(kk == 0)
        def _():
            acc[...] = p

        @pl.when(kk == nk - 1)
        def _():
            dx, dg = _rms_bwd_core(acc[...] + p, x_ref[...], g_ref[...])
            dx_ref[...] = dr_ref[...] + dx
            dg_ref[...] += dg

    assert nk == 2
    row = pl.BlockSpec((tm, D), lambda i, kk: (i, 0))
    vec = pl.BlockSpec((1, D), lambda i, kk: (0, 0))
    return pl.pallas_call(
        body, name="du_a_rms_bwd", grid=(S // tm, nk),
        in_specs=[pl.BlockSpec((tm, tk), lambda i, kk: (i, kk)), pl.BlockSpec((D, tk), lambda i, kk: (0, kk)),
                  row, vec, row, pl.BlockSpec(after.shape, lambda i, kk: (0, 0))],
        out_specs=[row, vec], out_shape=[_sds((S, D), F32), _sds((1, D), F32)],
        scratch_shapes=[pltpu.VMEM((tm, D), F32)], compiler_params=_params())(dproj, wa, x, g, dres, after)


def _du_b_rms_bwd(dpb, w_in_b, dkv, w_kv, h1, g_b, g_kv, dy):
    tm = 512

    def body(ab_ref, wb_ref, akv_ref, wkv_ref, x_ref, gb_ref, gkv_ref, dy_ref, dh_ref, dgb_ref, dgkv_ref):
        @pl.when(pl.program_id(0) == 0)
        def _():
            dgb_ref[...] = jnp.zeros_like(dgb_ref)
            dgkv_ref[...] = jnp.zeros_like(dgkv_ref)

        x = x_ref[...]
        dx1, dg1 = _rms_bwd_core(_dot_nt(ab_ref[...], wb_ref[...]), x, gb_ref[...])
        dx2, dg2 = _rms_bwd_core(_dot_nt(akv_ref[...], wkv_ref[...]), x, gkv_ref[...])
        dh_ref[...] = dy_ref[...] + dx1 + dx2
        dgb_ref[...] += dg1
        dgkv_ref[...] += dg2

    row = lambda width: pl.BlockSpec((tm, width), lambda i: (i, 0))
    whole = lambda arr: pl.BlockSpec(arr.shape, lambda i: (0, 0))
    vec = pl.BlockSpec((1, D), lambda i: (0, 0))
    return pl.pallas_call(
        body, name="du_b_rms_bwd", grid=(S // tm,),
        in_specs=[row(dpb.shape[1]), whole(w_in_b), row(dkv.shape[1]), whole(w_kv), row(D), vec, vec, row(D)],
        out_specs=[row(D), vec, vec], out_shape=[_sds((S, D), F32), _sds((1, D), F32), _sds((1, D), F32)],
        compiler_params=_params())(dpb, w_in_b, dkv, w_kv, h1, g_b, g_kv, dy)


def _gather_first(w_in_a, w_out_a, w_kv, w_in_b, w_out_b, norm_a_g):
    def body(wia_ref, woa_ref, wkv_ref, wib_ref, wob_ref, ga_ref,
             wa_g, ga_g, woa_s, wkv_s, wib_s, wob_s, wa_s, st_a, st_oa, st_kv, st_ib, st_ob, load_sems, *sems):
        sources = [wia_ref.at[0], woa_ref.at[0], wkv_ref, wib_ref.at[0], wob_ref.at[0]]
        stages = [st_a, st_oa, st_kv, st_ib, st_ob]
        loads = [pltpu.make_async_copy(src, dst, load_sems.at[i]) for i, (src, dst) in enumerate(zip(sources, stages))]
        for cp in loads:
            cp.start()
        loads[0].wait()
        wa_s[...] = st_a[...].astype(BF16)

        def cast_the_rest():
            for cp, stage, out in zip(loads[1:], stages[1:], [woa_s, wkv_s, wib_s, wob_s]):
                cp.wait()
                out[...] = stage[...].astype(BF16)

        _gather_two_level([wa_s, ga_ref], [wa_g, ga_g], sems, meanwhile=cast_the_rest)

    vmem = pl.BlockSpec(memory_space=pltpu.VMEM)
    anyspec = pl.BlockSpec(memory_space=pl.ANY)
    shard = lambda w: _sds(w.shape[-2:], BF16)
    stage = lambda w: pltpu.VMEM(w.shape[-2:], F32)
    return pl.pallas_call(
        body, name="gather_first", in_specs=[anyspec] * 5 + [vmem],
        out_specs=[anyspec, anyspec, vmem, vmem, vmem, vmem],
        out_shape=[_sds((N_DEV,) + w_in_a.shape[-2:], BF16), _sds((N_DEV,) + norm_a_g.shape, F32),
                   shard(w_out_a), shard(w_kv), shard(w_in_b), shard(w_out_b)],
        scratch_shapes=[pltpu.VMEM(w_in_a.shape[-2:], BF16), stage(w_in_a), stage(w_out_a), stage(w_kv),
                        stage(w_in_b), stage(w_out_b), pltpu.SemaphoreType.DMA((5,))] + _exchange_sems(2),
        compiler_params=pltpu.CompilerParams(vmem_limit_bytes=VMEM_LIMIT, has_side_effects=True))(
            w_in_a, w_out_a, w_kv, w_in_b, w_out_b, norm_a_g)


def _pair_reduce(slots):
    n_chip = N_DEV // 2
    _, rows, cols = slots.shape

    def body(s_ref, o_ref, own_v, sib_v, send_sems, recv_sems, local_sems):
        x, y, c = lax.axis_index("x"), lax.axis_index("y"), lax.axis_index("c")
        copies = []
        for j in range(n_chip):
            own = pltpu.make_async_copy(s_ref.at[2 * j + c], own_v.at[j], local_sems.at[j])
            give = pltpu.make_async_remote_copy(
                src_ref=s_ref.at[2 * j + 1 - c], dst_ref=sib_v.at[j], send_sem=send_sems.at[j],
                recv_sem=recv_sems.at[j], device_id=(x, y, 1 - c), device_id_type=pl.DeviceIdType.MESH)
            own.start()
            give.start()
            copies.append((own, give))
        for j, (own, give) in enumerate(copies):
            own.wait()
            give.wait()
            o_ref[j] = (own_v[j].astype(F32) + sib_v[j].astype(F32)).astype(BF16)

    half = _sds((n_chip, rows, cols), slots.dtype)
    return pl.pallas_call(
        body, name="pair_reduce", in_specs=[pl.BlockSpec(memory_space=pl.ANY)],
        out_specs=pl.BlockSpec(memory_space=pltpu.VMEM), out_shape=half,
        scratch_shapes=[pltpu.VMEM(half.shape, half.dtype), pltpu.VMEM(half.shape, half.dtype),
                        pltpu.SemaphoreType.DMA((n_chip,)), pltpu.SemaphoreType.DMA((n_chip,)),
                        pltpu.SemaphoreType.DMA((n_chip,))],
        compiler_params=pltpu.CompilerParams(vmem_limit_bytes=VMEM_LIMIT, has_side_effects=True))(slots)


def _padded_col(c):
    if c < RAW_F:
        return c
    return FOFF + (c - RAW_F) if c < RAW_G else GOFF + (c - RAW_G)


def _shard_pieces():
    width = NA_RAW // N_DEV
    pieces = []
    for d in range(N_DEV):
        cuts = [width * d] + [c for c in (RAW_F, RAW_G) if width * d < c < width * (d + 1)] + [width * (d + 1)]
        for lo, hi in zip(cuts[:-1], cuts[1:]):
            pieces.append((d, lo - width * d, _padded_col(lo), hi - lo))
    return pieces


def _unshard_wa(wa_g):
    def body(w_ref, o_ref):
        o_ref[:, FOFF + N_HEADS:NA] = jnp.zeros((TM, NA - FOFF - N_HEADS), BF16)
        for d, src, dst, width in _shard_pieces():
            o_ref[:, dst:dst + width] = w_ref[d, :, src:src + width]

    return pl.pallas_call(
        body, name="unshard_wa", grid=(D // TM,),
        in_specs=[pl.BlockSpec((N_DEV, TM, NA_RAW // N_DEV), lambda i: (0, i, 0))],
        out_specs=pl.BlockSpec((TM, NA), lambda i: (i, 0)), out_shape=_sds((D, NA), BF16),
        compiler_params=_params())(wa_g)


def _reshard_dwa(dwa):
    def body(g_ref, o_ref):
        for d, src, dst, width in _shard_pieces():
            o_ref[d, :, src:src + width] = g_ref[:, dst:dst + width]

    return pl.pallas_call(
        body, name="reshard_dwa", grid=(D // TM,), in_specs=[pl.BlockSpec((TM, NA), lambda i: (i, 0))],
        out_specs=pl.BlockSpec((N_DEV, TM, NA_RAW // N_DEV), lambda i: (0, i, 0)),
        out_shape=_sds((N_DEV, D, NA_RAW // N_DEV), dwa.dtype), compiler_params=_params())(dwa)


CHIP_FLIPS = (2, 4, 6)


def _chip_exchange_start(partial):
    n = len(CHIP_FLIPS)

    def body(p_ref, land_ref, *rest):
        sends, recvs, token = rest[:n], rest[n:2 * n], rest[2 * n + 2]
        x, y, c = lax.axis_index("x"), lax.axis_index("y"), lax.axis_index("c")
        me = 4 * x + 2 * y + c
        for idx, k in enumerate(CHIP_FLIPS):
            pltpu.make_async_remote_copy(
                src_ref=p_ref.at[(me ^ k) >> 1], dst_ref=land_ref.at[me >> 1], send_sem=sends[idx],
                recv_sem=recvs[idx], device_id=(x ^ ((k >> 2) & 1), y ^ ((k >> 1) & 1), c),
                device_id_type=pl.DeviceIdType.MESH).start()
        token[...] = jnp.zeros_like(token)

    hbm = pl.BlockSpec(memory_space=pltpu.HBM)
    sem = pl.BlockSpec(memory_space=pltpu.SEMAPHORE)
    buf = pltpu.HBM(partial.shape, partial.dtype)
    return pl.pallas_call(
        body, name="chip_exchange_start",
        out_shape=(pltpu.SemaphoreType.DMA(()),) * (2 * n) + (buf, buf, _sds((8, LANES), F32)),
        in_specs=(hbm, hbm), out_specs=(sem,) * (2 * n) + (hbm, hbm, pl.BlockSpec(memory_space=pltpu.VMEM)),
        input_output_aliases={0: 2 * n, 1: 2 * n + 1},
        compiler_params=pltpu.CompilerParams(has_side_effects=pltpu.SideEffectType.DATAFLOW_SIDE_EFFECTING))(
            pltpu.with_memory_space_constraint(partial, pltpu.HBM),
            pltpu.with_memory_space_constraint(lax.empty(partial.shape, partial.dtype), pltpu.HBM))


def _chip_exchange_wait(started, after):
    n = len(CHIP_FLIPS)
    sems, (p_thru, land_thru) = started[:2 * n], started[2 * n:2 * n + 2]

    def body(p_ref, land_ref, *rest):
        sends, recvs = rest[:n], rest[n:2 * n]
        x, y, c = lax.axis_index("x"), lax.axis_index("y"), lax.axis_index("c")
        me = 4 * x + 2 * y + c
        for idx, k in enumerate(CHIP_FLIPS):
            copy = pltpu.make_async_remote_copy(
                src_ref=p_ref.at[(me ^ k) >> 1], dst_ref=land_ref.at[(me ^ k) >> 1], send_sem=sends[idx],
                recv_sem=recvs[idx], device_id=(x ^ ((k >> 2) & 1), y ^ ((k >> 1) & 1), c),
                device_id_type=pl.DeviceIdType.MESH)
            copy.wait_send()
            copy.wait_recv()

    hbm = pl.BlockSpec(memory_space=pltpu.HBM)
    sem = pl.BlockSpec(memory_space=pltpu.SEMAPHORE)
    buf = pltpu.HBM(p_thru.shape, p_thru.dtype)
    return pl.pallas_call(
        body, name="chip_exchange_wait", out_shape=(buf, buf),
        in_specs=(hbm, hbm) + (sem,) * (2 * n) + (pl.BlockSpec(memory_space=pl.ANY),), out_specs=(hbm, hbm),
        input_output_aliases={0: 0, 1: 1},
        compiler_params=pltpu.CompilerParams(has_side_effects=pltpu.SideEffectType.DATAFLOW_SIDE_EFFECTING))(
            p_thru, land_thru, *sems, after)


def _gather_slab(slab, after):
    def body(s_ref, after_ref, o_ref, *sems):
        _exchange_ops(["gather_rows"], [s_ref], [o_ref], sems, True, True)

    anyspec = pl.BlockSpec(memory_space=pl.ANY)
    return pl.pallas_call(
        body, name="gather_slab", in_specs=[anyspec, anyspec], out_specs=anyspec,
        out_shape=_sds((N_DEV,) + slab.shape, slab.dtype), scratch_shapes=_exchange_sems(1),
        compiler_params=pltpu.CompilerParams(has_side_effects=True))(slab, after)


def _adamw(w, g, m, v):
    m = ADAM_B1 * m + (1.0 - ADAM_B1) * g
    v = ADAM_B2 * v + (1.0 - ADAM_B2) * (g * g)
    m_hat = m / (1.0 - ADAM_B1 ** ADAM_STEP)
    v_hat = v / (1.0 - ADAM_B2 ** ADAM_STEP)
    delta = -ADAM_LR * (m_hat / (jnp.sqrt(v_hat) + ADAM_EPS) + ADAM_WD * w)
    return delta, m, v


def _sum_adamw(recv, w, m, v, name, after=None):
    lead = w.ndim - 2
    rows, cols = w.shape[-2:]
    tr = 256 if rows % 256 == 0 else 128
    slabs = list(recv) if isinstance(recv, tuple) else [recv]
    n_slots = slabs[0].shape[0]
    extra = [] if after is None else [after]

    def body(*refs):
        r_ref, own_ref = refs[0], refs[len(slabs) - 1]
        w_ref, m_ref, v_ref, g_ref, d_ref, nm_ref, nv_ref = refs[len(slabs) + len(extra):]
        chip = (4 * lax.axis_index("x") + 2 * lax.axis_index("y") + lax.axis_index("c")) >> 1
        g = None
        for slot in range(n_slots):
            part = r_ref[slot]
            if len(slabs) == 2:
                part = jnp.where(chip == slot, own_ref[slot], part)
            g = part.astype(F32) if g is None else g + part.astype(F32)
        g_ref[...] = g
        d_ref[...], nm_ref[...], nv_ref[...] = _adamw(w_ref[...], g, m_ref[...], v_ref[...])

    blk = pl.BlockSpec((None,) * lead + (tr, cols), lambda i: (0,) * lead + (i, 0))
    slots = pl.BlockSpec((n_slots, tr, cols), lambda i: (0, i, 0))
    return pl.pallas_call(
        body, name=name, grid=(rows // tr,),
        in_specs=[slots] * len(slabs) + [pl.BlockSpec(a.shape, lambda i: (0, 0)) for a in extra] + [blk, blk, blk],
        out_specs=[blk] * 4, out_shape=[_sds(w.shape, F32)] * 4, compiler_params=_params())(*slabs, *extra, w, m, v)


SLAB_ROWS = 16
SLOT = {"kv_norm_g": (8, 0, D), "norm_b_g": (9, 0, D), "b_forget": (10, 0, 16), "qnorm_a_g": (10, 128, HD),
        "knorm_a_g": (10, 256, HD), "knorm_b_g": (10, 384, HD), "qnorm_b_g": (10, 512, HD), "sinks": (10, 640, 16)}
SMALL = ["norm_a_g", "b_forget", "qnorm_a_g", "knorm_a_g", "kv_norm_g", "knorm_b_g", "norm_b_g", "qnorm_b_g", "sinks"]


LOSS_ROW = 11


def _pack_small(dg_a, dg_kv, dg_b, db_f, dgq_a, dgk_a, dgk_b, dgq_b, dsinks, lsum):
    def fold(ref):
        return ref[:, 0:HD] + ref[:, HD:2 * HD]

    def body(dga_ref, dgkv_ref, dgb_ref, dbf_ref, dgqa_ref, dgka_ref, dgkb_ref, dgqb_ref, dsk_ref, ls_ref, slab_ref):
        slab_ref[...] = jnp.zeros_like(slab_ref)
        for r in range(N_DEV):
            slab_ref[r:r + 1, 0:LANES] = dga_ref[:, LANES * r:LANES * (r + 1)]
        slab_ref[8:9, :] = dgkv_ref[...]
        slab_ref[9:10, :] = dgb_ref[...]
        slab_ref[10:11, 0:LANES] = dbf_ref[...]
        slab_ref[10:11, 128:128 + HD] = fold(dgqa_ref)
        slab_ref[10:11, 256:256 + HD] = fold(dgka_ref)
        slab_ref[10:11, 384:384 + HD] = fold(dgkb_ref)
        slab_ref[10:11, 512:512 + HD] = fold(dgqb_ref)
        slab_ref[10:11, 640:640 + LANES] = dsk_ref[...]
        slab_ref[LOSS_ROW:LOSS_ROW + 1, 0:LANES] = ls_ref[...]

    return pl.pallas_call(body, name="pack_small", out_shape=_sds((SLAB_ROWS, D), F32), compiler_params=_params())(
        dg_a, dg_kv, dg_b, db_f, dgq_a, dgk_a, dgk_b, dgq_b, dsinks, lsum)


def _small_adamw(recv, ws, ms, vs):
    k = len(SMALL)

    def body(*refs):
        r_ref = refs[0]
        w_refs, m_refs, v_refs = refs[1:1 + k], refs[1 + k:1 + 2 * k], refs[1 + 2 * k:1 + 3 * k]
        outs = refs[1 + 3 * k:1 + 7 * k]
        loss_ref, tot = refs[1 + 7 * k], refs[2 + 7 * k]
        g = r_ref[0]
        for dev in range(1, N_DEV):
            g = g + r_ref[dev]
        tot[...] = g
        loss_ref[...] = tot[LOSS_ROW:LOSS_ROW + 1, 0:LANES] * (0.5 / D)
        me = 4 * lax.axis_index("x") + 2 * lax.axis_index("y") + lax.axis_index("c")
        for p, name in enumerate(SMALL):
            if name == "norm_a_g":
                mine = lax.broadcasted_iota(jnp.int32, (N_DEV, LANES), 0) == me
                gp = jnp.sum(jnp.where(mine, tot[0:N_DEV, 0:LANES], 0.0), axis=0, keepdims=True)
            else:
                row, lo, width = SLOT[name]
                gp = tot[row:row + 1, lo:lo + width]
            d, nm, nv = _adamw(w_refs[p][...], gp, m_refs[p][...], v_refs[p][...])
            outs[p][...] = gp
            outs[k + p][...] = d
            outs[2 * k + p][...] = nm
            outs[3 * k + p][...] = nv

    shapes = [_sds(w.shape, F32) for w in ws]
    return pl.pallas_call(body, name="small_adamw", out_shape=shapes * 4 + [_sds((1, LANES), F32)],
                          scratch_shapes=[pltpu.VMEM((SLAB_ROWS, D), F32)],
                          compiler_params=_params())(recv, *ws, *ms, *vs)


def _rope_tables(positions):
    inv_freq = jnp.power(jnp.float32(ROPE_THETA), -jnp.arange(0, ROT, 2, dtype=F32) / ROT)
    ang = positions.astype(F32)[:, None] * inv_freq[None, :]
    cos, sin = jnp.cos(ang), jnp.sin(ang)
    c64 = jnp.concatenate([cos, cos, jnp.ones((S, HD - ROT), F32)], axis=-1)
    s64 = jnp.concatenate([-sin, sin, jnp.zeros((S, HD - ROT), F32)], axis=-1)
    return jnp.tile(c64, (1, 2)), jnp.tile(s64, (1, 2))


def _local_step(x, tgt, positions, g_a, wa, b_forget, gq_a, gk_a, g_kv, gk_b, g_b, gq_b, sinks,
                woa_s, wkv_s, wib_s, wob_s, adamw_others):
    nq = S // TQ
    cos2, sin2 = _rope_tables(positions)
    b_pad = jnp.pad(b_forget, ((0, 0), (0, LANES - N_HEADS)))

    u_a, proj, qn, kn, vb, ccol, cbc = _head_a(x, g_a, wa, gq_a, gk_a, b_pad)
    crow = ccol[:, :N_HEADS].T.reshape(N_HEADS, nq, 1, TQ)
    o_a, z_a, lse_a, woa_g, wkv_g, w_in_b, wob_g = _fox_fwd(
        qn, kn, vb, proj, crow, cbc,
        rider=[("gather_rows", woa_s), ("gather_rows", wkv_s), ("gather_cols", wib_s), ("gather_rows", wob_s)])
    w_out_a, w_kv, w_out_b = woa_g.reshape(D, D), wkv_g.reshape(D, 512), wob_g.reshape(D, D)
    h1, u_kv, u_b, kv, pb, qb, ksh, vsh = _head_b(x, z_a, w_out_a, g_kv, g_b, w_kv, w_in_b, gq_b, gk_b, cos2, sin2)
    sinks1 = sinks.reshape(N_HEADS)
    o_b, z_b, lse_b = _swa_fwd(qb, ksh, vsh, pb, sinks1)
    dy, lsum = _out_b_loss(z_b, w_out_b, h1, tgt)
    dw_out_b = _mm(z_b, dy, "tn", 512, 512, S, out_dtype=BF16, name="mm_dw_out_b")
    dz_b = _mm(dy, w_out_b, "nt", 1024, 512, D, name="mm_dz_b")
    dpb, dka, dkb, dva, dvb, dsinks, dgq_b = _swa_bwd(qb, ksh, vsh, dz_b, o_b, lse_b, pb, sinks1, gq_b, cos2, sin2)
    dkv, dgk_b = _prep_kv_bwd(dka, dkb, dva, dvb, kv, gk_b, cos2, sin2)
    dw_in_b = _mm(u_b, dpb, "tn", 512, 512, S, out_dtype=BF16, name="mm_dw_in_b")
    dw_kv = _mm(u_kv, dkv, "tn", 512, 512, S, out_dtype=BF16, name="mm_dw_kv")
    dh1, dg_b, dg_kv = _du_b_rms_bwd(dpb, w_in_b, dkv, w_kv, h1, g_b, g_kv, dy)
    dw_out_a = _mm(z_a, dh1, "tn", 512, 512, S, out_dtype=BF16, name="mm_dw_out_a")
    do_a, dgate_a, delta_a = _fox_bwd_pre(dh1, w_out_a, proj, o_a)
    dk_a, dv_a, dcs, dq_a, drow, r_wob, r_wib, r_wkv, r_woa = _fox_bwd(
        qn, kn, vb, do_a, lse_a, delta_a, crow, cbc,
        rider=[("a2a_rows", dw_out_b), ("a2a_cols", dw_in_b), ("a2a_rows", dw_kv), ("a2a_rows", dw_out_a)])
    drow_col = jnp.pad(drow.reshape(N_HEADS, S).T, ((0, 0), (0, LANES - N_HEADS)))
    dproj, dgq_a, dgk_a, db_f = _prep_a_bwd(dq_a, dk_a, dv_a, dgate_a, drow_col, dcs, proj, b_pad, gq_a, gk_a)
    dwa = _mm(u_a, dproj, "tn", 1024, 256, S, out_dtype=BF16, name="mm_dw_in_a")
    partial = _pair_reduce(_reshard_dwa(dwa))
    started = _chip_exchange_start(partial)
    dx, dg_a = _du_a_rms_bwd(dproj, wa, x, g_a, dh1, after=started[-1])
    others = adamw_others(dict(w_out_a=r_woa, w_kv=r_wkv, w_in_b=r_wib, w_out_b=r_wob), dg_a)
    partial, landed = _chip_exchange_wait(started, others["w_out_b"][0])
    slab = _pack_small(dg_a, dg_kv, dg_b, db_f, dgq_a, dgk_a, dgk_b, dgq_b, dsinks, lsum)
    return dx, (landed, partial), others, _gather_slab(slab, landed)


def kernel(x, positions, norm_a_g, w_in_a, b_forget, qnorm_a_g, knorm_a_g, w_out_a, kv_norm_g, w_kv, knorm_b_g, norm_b_g, w_in_b, qnorm_b_g, sinks, w_out_b, loss_target, m_norm_a_g, m_w_in_a, m_b_forget, m_qnorm_a_g, m_knorm_a_g, m_w_out_a, m_kv_norm_g, m_w_kv, m_knorm_b_g, m_norm_b_g, m_w_in_b, m_qnorm_b_g, m_sinks, m_w_out_b, v_norm_a_g, v_w_in_a, v_b_forget, v_qnorm_a_g, v_knorm_a_g, v_w_out_a, v_kv_norm_g, v_w_kv, v_knorm_b_g, v_norm_b_g, v_w_in_b, v_qnorm_b_g, v_sinks, v_w_out_b):
    wa_g, ga_g, woa_s, wkv_s, wib_s, wob_s = _gather_first(w_in_a, w_out_a, w_kv, w_in_b, w_out_b, norm_a_g)
    state = dict(w_in_a=(w_in_a, m_w_in_a, v_w_in_a), w_out_a=(w_out_a, m_w_out_a, v_w_out_a),
                 w_kv=(w_kv, m_w_kv, v_w_kv), w_in_b=(w_in_b, m_w_in_b, v_w_in_b),
                 w_out_b=(w_out_b, m_w_out_b, v_w_out_b))

    def adamw_others(landed, after):
        return {n: _sum_adamw(r, *state[n], "adamw_" + n, after=after) for n, r in landed.items()}

    dx, r_wa, big, slab_g = _local_step(
        x[0], loss_target[0], positions, ga_g.reshape(1, D), _unshard_wa(wa_g), b_forget, qnorm_a_g, knorm_a_g,
        kv_norm_g.reshape(1, D), knorm_b_g.reshape(1, HD), norm_b_g, qnorm_b_g, sinks, woa_s, wkv_s, wib_s, wob_s,
        adamw_others)
    big["w_in_a"] = _sum_adamw(r_wa, *state["w_in_a"], "adamw_w_in_a")

    r2 = lambda a: a.reshape(1, -1)
    small_w = dict(norm_a_g=norm_a_g, b_forget=b_forget, qnorm_a_g=qnorm_a_g, knorm_a_g=knorm_a_g,
                   kv_norm_g=kv_norm_g, knorm_b_g=knorm_b_g, norm_b_g=norm_b_g, qnorm_b_g=qnorm_b_g, sinks=sinks)
    small_m = dict(norm_a_g=m_norm_a_g, b_forget=m_b_forget, qnorm_a_g=m_qnorm_a_g, knorm_a_g=m_knorm_a_g,
                   kv_norm_g=m_kv_norm_g, knorm_b_g=m_knorm_b_g, norm_b_g=m_norm_b_g, qnorm_b_g=m_qnorm_b_g,
                   sinks=m_sinks)
    small_v = dict(norm_a_g=v_norm_a_g, b_forget=v_b_forget, qnorm_a_g=v_qnorm_a_g, knorm_a_g=v_knorm_a_g,
                   kv_norm_g=v_kv_norm_g, knorm_b_g=v_knorm_b_g, norm_b_g=v_norm_b_g, qnorm_b_g=v_qnorm_b_g,
                   sinks=v_sinks)
    res = _small_adamw(slab_g, [r2(small_w[n]) for n in SMALL], [r2(small_m[n]) for n in SMALL],
                       [r2(small_v[n]) for n in SMALL])
    k = len(SMALL)
    small = {n: [res[q * k + p].reshape(small_w[n].shape) for q in range(4)] for p, n in enumerate(SMALL)}
    loss = res[4 * k][0, 0]

    order = ["norm_a_g", "w_in_a", "b_forget", "qnorm_a_g", "knorm_a_g", "w_out_a", "kv_norm_g", "w_kv",
             "knorm_b_g", "norm_b_g", "w_in_b", "qnorm_b_g", "sinks", "w_out_b"]

    def leaf(n, q):
        return big[n][q] if n in big else small[n][q]

    outs = [loss, dx[None]]
    for q in range(4):
        outs.extend(leaf(n, q) for n in order)
    return tuple(outs)
```

```python
import jax
import jax.numpy as jnp
from jax import lax
from jax.experimental import pallas as pl
from jax.experimental.pallas import tpu as pltpu

F32, BF16 = jnp.float32, jnp.bfloat16

S = 2048
D = 1024
HD = 64
N_HEADS = 16
N_DEV = 8
NA = 4352
GOFF = 3072
FOFF = 4096
RAW_F = 3072
RAW_G = RAW_F + N_HEADS
NA_RAW = 4112
EPS = 1e-6
QSCALE = 0.125
ROPE_THETA = 500000.0
ROT = 16
WIN = 128
TQ = 256
TK = 256
KS = TQ // 2
HPS = 8
HW = HPS * HD
TM = 256
RB = 512
CB = 256
LANES = 128

ADAM_LR, ADAM_B1, ADAM_B2, ADAM_EPS, ADAM_WD, ADAM_STEP = 0.001, 0.9, 0.999, 1e-08, 0.01, 10

VMEM_LIMIT = 56 * 1024 * 1024


def _params():
    return pltpu.CompilerParams(vmem_limit_bytes=VMEM_LIMIT)


def _sds(shape, dtype):
    return jax.ShapeDtypeStruct(shape, dtype)


def _dot_nt(a, b):
    return lax.dot_general(a, b, (((1,), (1,)), ((), ())), preferred_element_type=F32)


def _dot_tn(a, b):
    return lax.dot_general(a, b, (((0,), (0,)), ((), ())), preferred_element_type=F32)


def _dot_nn(a, b):
    return lax.dot_general(a, b, (((1,), (0,)), ((), ())), preferred_element_type=F32)


def _sigmoid(g):
    return 1.0 / (1.0 + jnp.exp(-g))


def _lane_iota(shape):
    return lax.broadcasted_iota(jnp.int32, shape, len(shape) - 1)


def _flips(kind):
    return (2, 4, 6) if kind == "a2a_chips" else tuple(range(1, N_DEV))


def _send_view(kind, ref, dev):
    if kind in ("gather_rows", "gather_cols"):
        return ref
    if kind == "a2a_slots":
        return ref.at[dev]
    if kind == "a2a_chips":
        return ref.at[dev >> 1]
    if kind == "a2a_rows":
        rows = ref.shape[0] // N_DEV
        return ref.at[pl.ds(pl.multiple_of(dev * rows, rows), rows)]
    cols = ref.shape[1] // N_DEV
    return ref.at[:, pl.ds(pl.multiple_of(dev * cols, cols), cols)]


def _land_view(kind, ref, dev):
    if kind == "gather_cols":
        cols = ref.shape[1] // N_DEV
        return ref.at[:, pl.ds(pl.multiple_of(dev * cols, cols), cols)]
    if kind == "a2a_chips":
        return ref.at[dev >> 1]
    return ref.at[dev]


def _landing_sds(kind, arr):
    if kind == "gather_rows":
        return _sds((N_DEV,) + arr.shape, arr.dtype)
    if kind == "gather_cols":
        return _sds((arr.shape[0], N_DEV * arr.shape[1]), arr.dtype)
    if kind == "a2a_rows":
        return _sds((N_DEV, arr.shape[0] // N_DEV, arr.shape[1]), arr.dtype)
    if kind == "a2a_cols":
        return _sds((N_DEV, arr.shape[0], arr.shape[1] // N_DEV), arr.dtype)
    return _sds(arr.shape, arr.dtype)


def _exchange_sems(n_parts):
    n = n_parts * (N_DEV - 1)
    return [pltpu.SemaphoreType.DMA((n,)), pltpu.SemaphoreType.DMA((n,)), pltpu.SemaphoreType.DMA((n_parts,))]


def _exchange_ops(kinds, srcs, dsts, sems, start, wait):
    send_sems, recv_sems, local_sems = sems
    x, y, c = lax.axis_index("x"), lax.axis_index("y"), lax.axis_index("c")
    me = 4 * x + 2 * y + c

    def local(a):
        return pltpu.make_async_copy(_send_view(kinds[a], srcs[a], me), _land_view(kinds[a], dsts[a], me),
                                     local_sems.at[a])

    def remote(a, k, landing_dev):
        peer = (x ^ ((k >> 2) & 1), y ^ ((k >> 1) & 1), c ^ (k & 1))
        sem = a * (N_DEV - 1) + k - 1
        return pltpu.make_async_remote_copy(
            src_ref=_send_view(kinds[a], srcs[a], me ^ k), dst_ref=_land_view(kinds[a], dsts[a], landing_dev),
            send_sem=send_sems.at[sem], recv_sem=recv_sems.at[sem], device_id=peer,
            device_id_type=pl.DeviceIdType.MESH)

    pairs = [(a, k) for k in range(1, N_DEV) for a in range(len(kinds)) if k in _flips(kinds[a])]
    if start:
        for a in range(len(kinds)):
            local(a).start()
        for a, k in pairs:
            remote(a, k, me).start()
    if wait:
        for a, k in pairs:
            remote(a, k, me ^ k).wait_recv()
            remote(a, k, me).wait_send()
        for a in range(len(kinds)):
            local(a).wait()


def _gather_two_level(srcs, dsts, sems, meanwhile=None):
    send_sems, recv_sems, local_sems = sems
    x, y, c = lax.axis_index("x"), lax.axis_index("y"), lax.axis_index("c")
    me, sibling = (x, y, c), (x, y, 1 - c)
    chips = [(1 - x, y), (x, 1 - y), (1 - x, 1 - y)]

    def slot(ref, dev):
        return ref.at[4 * dev[0] + 2 * dev[1] + dev[2]]

    def copy(a, k, block, to, src=None):
        return pltpu.make_async_remote_copy(
            src_ref=slot(dsts[a], block) if src is None else src, dst_ref=slot(dsts[a], block),
            send_sem=send_sems.at[a * (N_DEV - 1) + k], recv_sem=recv_sems.at[a * (N_DEV - 1) + k],
            device_id=to, device_id_type=pl.DeviceIdType.MESH)

    parts = range(len(srcs))
    mine = [pltpu.make_async_copy(srcs[a], slot(dsts[a], me), local_sems.at[a]) for a in parts]
    first = [copy(a, 0, me, sibling, src=srcs[a]) for a in parts]
    first += [copy(a, 1 + j, me, (*chip, c), src=srcs[a]) for j, chip in enumerate(chips) for a in parts]
    for cp in mine + first:
        cp.start()
    if meanwhile is not None:
        meanwhile()
    passed = []
    for j, chip in enumerate(chips):
        for a in parts:
            copy(a, 1 + j, (*chip, c), me).wait_recv()
            fwd = copy(a, 4 + j, (*chip, c), sibling)
            fwd.start()
            passed.append(fwd)
    for a in parts:
        copy(a, 0, sibling, me).wait_recv()
        for j, chip in enumerate(chips):
            copy(a, 4 + j, (*chip, 1 - c), me).wait_recv()
    for cp in first + passed:
        cp.wait_send()
    for cp in mine:
        cp.wait()


def _call(body, *, name, args, in_specs, out_specs, out_shape, grid=(), scratch_shapes=(), aliases=None, rider=()):
    n_in, n_out, n_scr, n_r = len(in_specs), len(out_specs), len(scratch_shapes), len(rider)
    kinds = [kind for kind, _ in rider]

    def kernel_body(*refs):
        c_in, r_in = refs[:n_in], refs[n_in:n_in + n_r]
        c_out = refs[n_in + n_r:n_in + n_r + n_out]
        r_out = refs[n_in + n_r + n_out:n_in + 2 * n_r + n_out]
        rest = refs[n_in + 2 * n_r + n_out:]
        c_scr, sems = rest[:n_scr], rest[n_scr:]
        if n_r:
            assert grid, "a rider needs a gridded call"
            ids = [pl.program_id(ax) for ax in range(len(grid))]
            first, last = ids[0] == 0, ids[0] == grid[0] - 1
            for pid, size in zip(ids[1:], grid[1:]):
                first = first & (pid == 0)
                last = last & (pid == size - 1)
            pl.when(first)(lambda: _exchange_ops(kinds, r_in, r_out, sems, True, False))
        body(*c_in, *c_out, *c_scr)
        if n_r:
            pl.when(last)(lambda: _exchange_ops(kinds, r_in, r_out, sems, False, True))

    anyspec = pl.BlockSpec(memory_space=pl.ANY)
    params = pltpu.CompilerParams(vmem_limit_bytes=VMEM_LIMIT, has_side_effects=bool(n_r))
    outs = pl.pallas_call(
        kernel_body, name=name, grid=grid, in_specs=list(in_specs) + [anyspec] * n_r,
        out_specs=list(out_specs) + [anyspec] * n_r,
        out_shape=list(out_shape) + [_landing_sds(kind, arr) for kind, arr in rider],
        scratch_shapes=list(scratch_shapes) + (_exchange_sems(n_r) if n_r else []),
        input_output_aliases=aliases or {}, compiler_params=params)(*args, *[arr for _, arr in rider])
    return list(outs)


def _mm(a, b, mode, tm, tn, tk, out_dtype=F32, add=None, name="mm", rider=()):
    if mode == "nn":
        (m, k), n = a.shape, b.shape[1]
        a_spec = pl.BlockSpec((tm, tk), lambda i, j, kk: (i, kk))
        b_spec = pl.BlockSpec((tk, tn), lambda i, j, kk: (kk, j))
        dot = _dot_nn
    elif mode == "nt":
        (m, k), n = a.shape, b.shape[0]
        a_spec = pl.BlockSpec((tm, tk), lambda i, j, kk: (i, kk))
        b_spec = pl.BlockSpec((tn, tk), lambda i, j, kk: (j, kk))
        dot = _dot_nt
    else:
        (k, m), n = a.shape, b.shape[1]
        a_spec = pl.BlockSpec((tk, tm), lambda i, j, kk: (kk, i))
        b_spec = pl.BlockSpec((tk, tn), lambda i, j, kk: (kk, j))
        dot = _dot_tn
    assert m % tm == 0 and n % tn == 0 and k % tk == 0, (m, n, k, tm, tn, tk)
    nk = k // tk
    has_add = add is not None

    def body(*refs):
        if has_add:
            a_ref, b_ref, add_ref, o_ref, acc = refs
        else:
            a_ref, b_ref, o_ref, acc = refs
        p = dot(a_ref[...].astype(BF16), b_ref[...].astype(BF16))

        def finish(total):
            if has_add:
                total = add_ref[...] + total
            o_ref[...] = total.astype(out_dtype)

        if nk == 1:
            finish(p)
        else:
            kk = pl.program_id(2)

            @pl.when(kk == 0)
            def _():
                acc[...] = p

            @pl.when(kk > 0)
            def _():
                acc[...] += p

            @pl.when(kk == nk - 1)
            def _():
                finish(acc[...])

    in_specs = [a_spec, b_spec]
    args = [a, b]
    if has_add:
        in_specs.append(pl.BlockSpec((tm, tn), lambda i, j, kk: (i, j)))
        args.append(add)
    acc_shape = (tm, tn) if nk > 1 else (8, LANES)
    outs = _call(body, name=name, args=args, grid=(m // tm, n // tn, nk), in_specs=in_specs,
                 out_specs=[pl.BlockSpec((tm, tn), lambda i, j, kk: (i, j))], out_shape=[_sds((m, n), out_dtype)],
                 scratch_shapes=[pltpu.VMEM(acc_shape, F32)], rider=rider)
    return outs if rider else outs[0]


def _rms_rinv(x):
    return lax.rsqrt(jnp.mean(x * x, axis=-1, keepdims=True) + EPS)


def _rms_bwd_core(du, x, g):
    r = _rms_rinv(x)
    dug = du * g
    dx = r * (dug - x * ((r * r) * jnp.mean(dug * x, axis=-1, keepdims=True)))
    dg = jnp.sum(du * (x * r), axis=0, keepdims=True)
    return dx, dg


def _half_sum(v, lo_half):
    s0 = jnp.sum(jnp.where(lo_half, v, 0.0), axis=-1, keepdims=True)
    s1 = jnp.sum(jnp.where(lo_half, 0.0, v), axis=-1, keepdims=True)
    return jnp.where(lo_half, s0, s1)


def _head_rinv(x, lo_half):
    return lax.rsqrt(_half_sum(x * x, lo_half) * (1.0 / HD) + EPS)


def _head_norm_bwd(dn, x, g, lo_half):
    r = _head_rinv(x, lo_half)
    dng = dn * g
    dx = r * (dng - x * ((r * r) * (_half_sum(dng * x, lo_half) * (1.0 / HD))))
    dg = jnp.sum(dn * (x * r), axis=0, keepdims=True)
    return dx, dg


def _rope_swap(x, lane):
    l64 = lane & (HD - 1)
    return jnp.where(l64 < ROT // 2, pltpu.roll(x, LANES - ROT // 2, 1), pltpu.roll(x, ROT // 2, 1))


def _rope_fwd(x, cos, sin, lane):
    return x * cos + _rope_swap(x, lane) * sin


def _rope_bwd(dy, cos, sin, lane):
    return dy * cos + jnp.where((lane & (HD - 1)) < ROT, _rope_swap(dy * sin, lane), 0.0)


def _g2(g_ref):
    g = g_ref[...]
    return jnp.concatenate([g, g], axis=-1)


def _pairs(width):
    return [slice(LANES * c, LANES * (c + 1)) for c in range(width // LANES)]


def _pick_lane(block, lane, idx):
    return jnp.sum(jnp.where(lane == idx, block, 0.0), axis=-1, keepdims=True)


def _head_a(x, g, wa, gq, gk, b_pad):
    def body(x_ref, g_ref, w_ref, gq_ref, gk_ref, b_ref, u_ref, p_ref, qo_ref, ko_ref, vo_ref, c_ref, cbc_ref, carry):
        @pl.when(pl.program_id(0) == 0)
        def _():
            carry[...] = jnp.zeros_like(carry)

        xv = x_ref[...]
        u = ((xv * _rms_rinv(xv)) * g_ref[...]).astype(BF16)
        u_ref[...] = u
        for lo in range(0, NA, D):
            hi = min(lo + D, NA)
            p_ref[:, lo:hi] = _dot_nn(u, w_ref[:, lo:hi])
        lane = _lane_iota((TM, LANES))
        lo_half = lane < HD
        gq2, gk2 = _g2(gq_ref), _g2(gk_ref)
        for c in _pairs(D):
            q = p_ref[:, c]
            k = p_ref[:, D + c.start:D + c.stop]
            qo_ref[:, c] = (((q * _head_rinv(q, lo_half)) * gq2) * QSCALE).astype(BF16)
            ko_ref[:, c] = ((k * _head_rinv(k, lo_half)) * gk2).astype(BF16)
        vo_ref[...] = p_ref[:, 2 * D:3 * D].astype(BF16)

        z = p_ref[:, FOFF:FOFF + LANES] + b_ref[...]
        logf = jnp.minimum(z, 0.0) - jnp.log1p(jnp.exp(-jnp.abs(z)))
        r = lax.broadcasted_iota(jnp.int32, (TM, TM), 0)
        cc = lax.broadcasted_iota(jnp.int32, (TM, TM), 1)
        tri = (r >= cc).astype(F32)
        loc = jnp.dot(tri, logf, precision=lax.Precision.HIGHEST, preferred_element_type=F32) + carry[0:1, :]
        c_ref[...] = loc
        carry[0:1, :] = loc[TM - 1:TM, :]
        for h in range(N_HEADS):
            cbc_ref[:, LANES * h:LANES * (h + 1)] = jnp.broadcast_to(_pick_lane(loc, lane, h), (TM, LANES))

    row = lambda width: pl.BlockSpec((TM, width), lambda i: (i, 0))
    whole = lambda arr: pl.BlockSpec(arr.shape, lambda i: (0,) * arr.ndim)
    return pl.pallas_call(
        body, name="head_a", grid=(S // TM,),
        in_specs=[row(D), whole(g), whole(wa), whole(gq), whole(gk), whole(b_pad)],
        out_specs=[row(D), row(NA), row(D), row(D), row(D), row(LANES), row(N_HEADS * LANES)],
        out_shape=[_sds((S, D), BF16), _sds((S, NA), F32)] + [_sds((S, D), BF16)] * 3
        + [_sds((S, LANES), F32), _sds((S, N_HEADS * LANES), F32)],
        scratch_shapes=[pltpu.VMEM((8, LANES), F32)], compiler_params=_params())(x, g, wa, gq, gk, b_pad)


def _key_le_query(offset, keys=TK):
    r = lax.broadcasted_iota(jnp.int32, (keys, TQ), 0)
    c = lax.broadcasted_iota(jnp.int32, (keys, TQ), 1)
    return (r + offset) <= c


def _widen(tile):
    return jnp.concatenate([tile] * (TQ // LANES), axis=1)


def _fox_fwd(qn, kn, vb, proj, crow, cbc, rider=()):
    nq = S // TQ

    def body(q_ref, k_ref, v_ref, g_ref, cq_ref, cbc_ref, o_ref, z_ref, lse_ref, st_s, pt_s):
        i = pl.program_id(1)
        qs = [q_ref[:, HD * hh:HD * (hh + 1)] for hh in range(HPS)]
        cqs = [cq_ref[hh, 0] for hh in range(HPS)]

        def scores(s, hh):
            off = pl.multiple_of(s * KS, KS)
            kj = k_ref[pl.ds(off, KS), HD * hh:HD * (hh + 1)]
            return (_dot_nt(kj, qs[hh]) + cqs[hh]) - _widen(cbc_ref[pl.ds(off, KS), LANES * hh:LANES * (hh + 1)])

        def values(s, hh, pt):
            off = pl.multiple_of(s * KS, KS)
            return _dot_tn(v_ref[pl.ds(off, KS), HD * hh:HD * (hh + 1)], pt)

        def step(s, slot, carries, mask=None, last=False):
            if not last:
                for hh in range(HPS):
                    st_s[1 - slot, hh] = scores(s + 1, hh)
            pvs = [values(jnp.maximum(s - 1, 0), hh, pt_s[1 - slot, hh]) for hh in range(HPS)]
            out = []
            for hh in range(HPS):
                m, l, acc = carries[hh]
                st = st_s[slot, hh]
                if mask is not None:
                    st = jnp.where(mask, st, -jnp.inf)
                m_new = jnp.maximum(m, jnp.max(st, axis=0, keepdims=True))
                pt = jnp.exp(st - m_new)
                alpha = jnp.exp(m - m_new)
                pt_s[slot, hh] = pt.astype(BF16)
                out.append((m_new, alpha * l + jnp.sum(pt, axis=0, keepdims=True), alpha * (acc + pvs[hh])))
            return tuple(out)

        for hh in range(HPS):
            st_s[0, hh] = scores(0, hh)
            pt_s[1, hh] = jnp.zeros((KS, TQ), BF16)
        one = (jnp.full((1, TQ), -jnp.inf, F32), jnp.zeros((1, TQ), F32), jnp.zeros((HD, TQ), F32))
        carries = lax.fori_loop(0, i, lambda t, cr: step(2 * t + 1, 1, step(2 * t, 0, cr)), (one,) * HPS)
        carries = step(2 * i, 0, carries, mask=_key_le_query(0, KS))
        carries = step(2 * i + 1, 1, carries, mask=_key_le_query(KS, KS), last=True)
        accs = []
        for hh in range(HPS):
            m, l, acc = carries[hh]
            acc = acc + values(2 * i + 1, hh, pt_s[1, hh])
            accs.append(acc / l)
            lse_ref[hh, 0] = m + jnp.log(l)
        o = jnp.concatenate(accs, axis=0).T
        o_ref[...] = o
        g = g_ref[...]
        z_ref[...] = (o * (g * _sigmoid(g))).astype(BF16)

    qblk = pl.BlockSpec((TQ, HW), lambda hp, i: (i, hp))
    full = pl.BlockSpec((S, HW), lambda hp, i: (0, hp))
    rows = pl.BlockSpec((HPS, 1, 1, TQ), lambda hp, i: (hp, i, 0, 0))
    return _call(
        body, name="fox_fwd", args=(qn, kn, vb, proj, crow, cbc), grid=(N_HEADS // HPS, nq),
        in_specs=[qblk, full, full,
                  pl.BlockSpec((TQ, HW), lambda hp, i: (i, GOFF // HW + hp)),
                  rows, pl.BlockSpec((S, HPS * LANES), lambda hp, i: (0, hp))],
        out_specs=[qblk, qblk, rows],
        out_shape=[_sds((S, D), F32), _sds((S, D), BF16), _sds((N_HEADS, nq, 1, TQ), F32)],
        scratch_shapes=[pltpu.VMEM((2, HPS, KS, TQ), F32), pltpu.VMEM((2, HPS, KS, TQ), BF16)], rider=rider)


def _fox_bwd_pre(dh, w_out, proj, o):
    nq = S // TQ

    def body(dh_ref, w_ref, g_ref, o_ref, do_ref, dg_ref, delta_ref):
        g = g_ref[...]
        sg = _sigmoid(g)
        dzv = _dot_nt(dh_ref[...].astype(BF16), w_ref[...])
        ov = o_ref[...]
        do = dzv * (g * sg)
        dg_ref[...] = (dzv * ov * (sg * (1.0 + g * (1.0 - sg)))).astype(BF16)
        do_ref[...] = do.astype(BF16)
        prod_t = (do * ov).T
        for h in range(N_HEADS):
            delta_ref[h, 0] = jnp.sum(prod_t[HD * h:HD * (h + 1), :], axis=0, keepdims=True)

    row = pl.BlockSpec((TQ, D), lambda i: (i, 0))
    return pl.pallas_call(
        body, name="fox_bwd_pre", grid=(nq,),
        in_specs=[row, pl.BlockSpec(w_out.shape, lambda i: (0, 0)), pl.BlockSpec((TQ, D), lambda i: (i, GOFF // D)), row],
        out_specs=[row, row, pl.BlockSpec((N_HEADS, 1, 1, TQ), lambda i: (0, i, 0, 0))],
        out_shape=[_sds((S, D), BF16), _sds((S, D), BF16), _sds((N_HEADS, nq, 1, TQ), F32)],
        compiler_params=_params())(dh, w_out, proj, o)


def _fox_bwd(qn, kn, vb, dob, lse, delta, crow, cbc, rider=()):
    nq, nkb = S // TQ, S // TK

    def body(q_ref, k_ref, v_ref, do_ref, lse_ref, del_ref, cq_ref, cbc_ref,
             dk_ref, dv_ref, dcs_ref, dq_ref, dr_ref, st_s, dp_s, pt_s, ds_s, dq_acc, dr_acc):
        j = pl.program_id(1)

        @pl.when(j == 0)
        def _():
            dq_acc[...] = jnp.zeros_like(dq_acc)
            dr_acc[...] = jnp.zeros_like(dr_acc)

        kjs = [k_ref[:, HD * hh:HD * (hh + 1)] for hh in range(HPS)]
        vjs = [v_ref[:, HD * hh:HD * (hh + 1)] for hh in range(HPS)]

        def rows_of(ref, u, hh):
            off = pl.multiple_of(u * TQ, TQ)
            return ref[pl.ds(off, TQ), HD * hh:HD * (hh + 1)]

        def products(u, hh):
            st = (_dot_nt(kjs[hh], rows_of(q_ref, u, hh)) + cq_ref[hh, u]) - _widen(
                cbc_ref[:, LANES * hh:LANES * (hh + 1)])
            return st, _dot_nt(vjs[hh], rows_of(do_ref, u, hh))

        def step(u, slot, carries, masked=False):
            nxt = jnp.minimum(u + 1, nq - 1)
            for hh in range(HPS):
                st_s[1 - slot, hh], dp_s[1 - slot, hh] = products(nxt, hh)
            prev = jnp.maximum(u - 1, 0)
            dvs = [_dot_nn(pt_s[1 - slot, hh], rows_of(do_ref, prev, hh)) for hh in range(HPS)]
            dks = [_dot_nn(ds_s[1 - slot, hh], rows_of(q_ref, prev, hh)) for hh in range(HPS)]
            for hh in range(HPS):
                dq_acc[hh, prev] += _dot_tn(kjs[hh], ds_s[1 - slot, hh])
            out = []
            for hh in range(HPS):
                dk, dv, dcs = carries[hh]
                st = st_s[slot, hh]
                if masked:
                    st = jnp.where(_key_le_query(0), st, -jnp.inf)
                pt = jnp.exp(st - lse_ref[hh, u])
                dst = pt * (dp_s[slot, hh] - del_ref[hh, u])
                pt_s[slot, hh] = pt.astype(BF16)
                ds_s[slot, hh] = dst.astype(BF16)
                dr_acc[hh, u] += jnp.sum(dst, axis=0, keepdims=True)
                out.append((dk + dks[hh], dv + dvs[hh], dcs + (dst[:, :LANES] + dst[:, LANES:])))
            return tuple(out)

        for hh in range(HPS):
            st_s[0, hh], dp_s[0, hh] = products(j, hh)
            pt_s[1, hh] = jnp.zeros((TK, TQ), BF16)
            ds_s[1, hh] = jnp.zeros((TK, TQ), BF16)
        one = (jnp.zeros((TK, HD), F32), jnp.zeros((TK, HD), F32), jnp.zeros((TK, LANES), F32))
        rest = nq - 1 - j
        carries = step(j, 0, (one,) * HPS, masked=True)
        carries = lax.fori_loop(
            0, rest // 2, lambda t, cr: step(j + 2 + 2 * t, 0, step(j + 1 + 2 * t, 1, cr)), carries)
        carries = lax.cond(rest % 2 == 1, lambda cr: step(nq - 1, 1, cr), lambda cr: cr, carries)
        last_in_0 = rest % 2 == 0
        dks, dvs = [], []
        lane = _lane_iota((TK, LANES))
        dcs_all = jnp.zeros((TK, LANES), F32)
        for hh in range(HPS):
            dk, dv, dcs = carries[hh]
            ds_last = jnp.where(last_in_0, ds_s[0, hh], ds_s[1, hh])
            pt_last = jnp.where(last_in_0, pt_s[0, hh], pt_s[1, hh])
            dks.append(dk + _dot_nn(ds_last, rows_of(q_ref, nq - 1, hh)))
            dvs.append(dv + _dot_nn(pt_last, rows_of(do_ref, nq - 1, hh)))
            dq_acc[hh, nq - 1] += _dot_tn(kjs[hh], ds_last)
            dcs_all = jnp.where(lane == HPS * pl.program_id(0) + hh, -jnp.sum(dcs, axis=-1, keepdims=True), dcs_all)
        dcs_ref[0] = dcs_all
        dk_ref[...] = jnp.concatenate(dks, axis=-1)
        dv_ref[...] = jnp.concatenate(dvs, axis=-1).astype(BF16)

        @pl.when(j == nkb - 1)
        def _():
            for i in range(nq):
                dq_ref[TQ * i:TQ * (i + 1), :] = jnp.concatenate([dq_acc[hh, i] for hh in range(HPS)], axis=0).T
            dr_ref[...] = dr_acc[...]

    kblk = pl.BlockSpec((TK, HW), lambda hp, j: (j, hp))
    full = pl.BlockSpec((S, HW), lambda hp, j: (0, hp))
    rows = pl.BlockSpec((HPS, nq, 1, TQ), lambda hp, j: (hp, 0, 0, 0))
    cblk = pl.BlockSpec((TK, HPS * LANES), lambda hp, j: (j, hp))
    return _call(
        body, name="fox_bwd", args=(qn, kn, vb, dob, lse, delta, crow, cbc), grid=(N_HEADS // HPS, nkb),
        in_specs=[full, kblk, kblk, full, rows, rows, rows, cblk],
        out_specs=[kblk, kblk, pl.BlockSpec((1, TK, LANES), lambda hp, j: (hp, j, 0)), full, rows],
        out_shape=[_sds((S, D), F32), _sds((S, D), BF16), _sds((N_HEADS // HPS, S, LANES), F32), _sds((S, D), F32),
                   _sds((N_HEADS, nq, 1, TQ), F32)],
        scratch_shapes=[pltpu.VMEM((2, HPS, TK, TQ), F32), pltpu.VMEM((2, HPS, TK, TQ), F32),
                        pltpu.VMEM((2, HPS, TK, TQ), BF16), pltpu.VMEM((2, HPS, TK, TQ), BF16),
                        pltpu.VMEM((HPS, nq, HD, TQ), F32), pltpu.VMEM((HPS, nq, 1, TQ), F32)], rider=rider)


def _prep_a_bwd(dq, dk, dv, dgate, drow, dcs, proj, b_pad, gq, gk):
    nt = S // TM

    def body(dq_ref, dk_ref, dv_ref, dgt_ref, dr_ref, dcs_ref, xq_ref, xk_ref, f_ref, b_ref, gq_ref, gk_ref,
             o_ref, dgq_ref, dgk_ref, db_ref, carry):
        @pl.when(pl.program_id(0) == 0)
        def _():
            carry[...] = jnp.zeros_like(carry)
            dgq_ref[...] = jnp.zeros_like(dgq_ref)
            dgk_ref[...] = jnp.zeros_like(dgk_ref)
            db_ref[...] = jnp.zeros_like(db_ref)

        lane = _lane_iota((TM, LANES))
        lo_half = lane < HD
        gq2, gk2 = _g2(gq_ref), _g2(gk_ref)
        dgq, dgk = jnp.zeros((1, LANES), F32), jnp.zeros((1, LANES), F32)
        for c in _pairs(D):
            dxq, dg = _head_norm_bwd(dq_ref[:, c] * QSCALE, xq_ref[:, c], gq2, lo_half)
            o_ref[:, c] = dxq.astype(BF16)
            dgq = dgq + dg
            dxk, dg = _head_norm_bwd(dk_ref[:, c], xk_ref[:, c], gk2, lo_half)
            o_ref[:, D + c.start:D + c.stop] = dxk.astype(BF16)
            dgk = dgk + dg
        dgq_ref[...] += dgq
        dgk_ref[...] += dgk
        o_ref[:, 2 * D:3 * D] = dv_ref[...]
        o_ref[:, GOFF:GOFF + D] = dgt_ref[...]

        dc = dr_ref[...]
        for group in range(N_HEADS // HPS):
            dc = dc + dcs_ref[group]
        r = lax.broadcasted_iota(jnp.int32, (TM, TM), 0)
        c = lax.broadcasted_iota(jnp.int32, (TM, TM), 1)
        tri = (c >= r).astype(F32)
        dlogf = jnp.dot(tri, dc, precision=lax.Precision.HIGHEST, preferred_element_type=F32) + carry[0:1, :]
        carry[0:1, :] = dlogf[0:1, :]
        df = dlogf * (1.0 / (1.0 + jnp.exp(f_ref[...] + b_ref[...])))
        db_ref[...] += jnp.sum(df, axis=0, keepdims=True)
        o_ref[:, FOFF:FOFF + LANES] = df.astype(BF16)
        o_ref[:, FOFF + LANES:NA] = jnp.zeros((TM, NA - FOFF - LANES), BF16)

    rev = lambda width, col: pl.BlockSpec((TM, width), lambda i: (nt - 1 - i, col))
    gspec = pl.BlockSpec((1, HD), lambda i: (0, 0))
    acc = pl.BlockSpec((1, LANES), lambda i: (0, 0))
    return pl.pallas_call(
        body, name="prep_a_bwd", grid=(nt,),
        in_specs=[rev(D, 0), rev(D, 0), rev(D, 0), rev(D, 0), rev(LANES, 0),
                  pl.BlockSpec((N_HEADS // HPS, TM, LANES), lambda i: (0, nt - 1 - i, 0)),
                  rev(D, 0), rev(D, 1), rev(LANES, FOFF // LANES), acc, gspec, gspec],
        out_specs=[rev(NA, 0), acc, acc, acc],
        out_shape=[_sds((S, NA), BF16)] + [_sds((1, LANES), F32)] * 3,
        scratch_shapes=[pltpu.VMEM((8, LANES), F32)],
        compiler_params=_params())(dq, dk, dv, dgate, drow, dcs, proj, proj, proj, b_pad, gq, gk)


def _head_b(x, z_a, w_out_a, g_kv, g_b, w_kv, w_in_b, gq, gk, cos2, sin2):
    nkv = w_kv.shape[1] // 2

    def body(x_ref, z_ref, wo_ref, gkv_ref, gb_ref, wkv_ref, wb_ref, gq_ref, gk_ref, c_ref, s_ref,
             h_ref, ukv_ref, ub_ref, kv_ref, pb_ref, qo_ref, ko_ref, vo_ref):
        xv = x_ref[...] + _dot_nn(z_ref[...], wo_ref[...])
        h_ref[...] = xv
        xn = xv * _rms_rinv(xv)
        ukv = (xn * gkv_ref[...]).astype(BF16)
        ub = (xn * gb_ref[...]).astype(BF16)
        ukv_ref[...] = ukv
        ub_ref[...] = ub
        kv_ref[...] = _dot_nn(ukv, wkv_ref[...])
        for lo in range(0, 2 * D, D):
            pb_ref[:, lo:lo + D] = _dot_nn(ub, wb_ref[:, lo:lo + D])
        lane = _lane_iota((TM, LANES))
        lo_half = lane < HD
        cos, sin = c_ref[...], s_ref[...]
        gq2, gk2 = _g2(gq_ref), _g2(gk_ref)
        for c in _pairs(D):
            q = pb_ref[:, c]
            qo_ref[:, c] = (_rope_fwd((q * _head_rinv(q, lo_half)) * gq2, cos, sin, lane) * QSCALE).astype(BF16)
        for c in _pairs(nkv):
            k = kv_ref[:, c]
            ko_ref[:, c] = _rope_fwd((k * _head_rinv(k, lo_half)) * gk2, cos, sin, lane).astype(BF16)
        vo_ref[...] = kv_ref[:, nkv:2 * nkv].astype(BF16)

    row = lambda width: pl.BlockSpec((TM, width), lambda i: (i, 0))
    whole = lambda arr: pl.BlockSpec(arr.shape, lambda i: (0,) * arr.ndim)
    return pl.pallas_call(
        body, name="head_b", grid=(S // TM,),
        in_specs=[row(D), row(D), whole(w_out_a), whole(g_kv), whole(g_b), whole(w_kv), whole(w_in_b), whole(gq),
                  whole(gk), row(LANES), row(LANES)],
        out_specs=[row(D), row(D), row(D), row(2 * nkv), row(2 * D), row(D), row(nkv), row(nkv)],
        out_shape=[_sds((S, D), F32), _sds((S, D), BF16), _sds((S, D), BF16), _sds((S, 2 * nkv), F32),
                   _sds((S, 2 * D), F32), _sds((S, D), BF16), _sds((S, nkv), BF16), _sds((S, nkv), BF16)],
        compiler_params=_params())(x, z_a, w_out_a, g_kv, g_b, w_kv, w_in_b, gq, gk, cos2, sin2)


N_KV, GRP = 4, 4


def _swa_mask(n):
    r = lax.broadcasted_iota(jnp.int32, (2 * WIN, GRP * WIN), 0)
    q = lax.broadcasted_iota(jnp.int32, (2 * WIN, GRP * WIN), 1) & (WIN - 1)
    return (r > q) & (r <= q + WIN) & ((r >= WIN) | (n > 0))


def _stack4(ref_or_val, base):
    return jnp.concatenate([ref_or_val[:, base + HD * g: base + HD * (g + 1)] for g in range(GRP)], axis=0)


def _unstack4(xt):
    return jnp.concatenate([xt[:, WIN * g:WIN * (g + 1)] for g in range(GRP)], axis=0).T


def _band(prev_ref, cur_ref, kh):
    return jnp.concatenate([prev_ref[:, HD * kh:HD * (kh + 1)], cur_ref[:, HD * kh:HD * (kh + 1)]], axis=0)


def _sink_row(s_ref, first):
    lane = _lane_iota((1, GRP * WIN))
    row = jnp.full((1, GRP * WIN), s_ref[first + GRP - 1], F32)
    for g in range(GRP - 2, -1, -1):
        row = jnp.where(lane < WIN * (g + 1), s_ref[first + g], row)
    return row


def _swa_fwd(qb, ksh, vsh, pb, sinks):
    nb = S // WIN

    def body(q_ref, kp_ref, kc_ref, vp_ref, vc_ref, g_ref, s_ref, o_ref, z_ref, lse_ref):
        n = pl.program_id(0)
        valid = _swa_mask(n)
        outs = []
        for kh in range(N_KV):
            kb, vb = _band(kp_ref, kc_ref, kh), _band(vp_ref, vc_ref, kh)
            st = jnp.where(valid, _dot_nt(kb, _stack4(q_ref, GRP * HD * kh)), -jnp.inf)
            sink = _sink_row(s_ref, GRP * kh)
            m = jnp.maximum(jnp.max(st, axis=0, keepdims=True), sink)
            pt = jnp.exp(st - m)
            l = jnp.sum(pt, axis=0, keepdims=True) + jnp.exp(sink - m)
            outs.append(_unstack4(_dot_tn(vb, pt.astype(BF16)) / l))
            lse = m + jnp.log(l)
            for g in range(GRP):
                lse_ref[GRP * kh + g, 0] = lse[:, WIN * g:WIN * (g + 1)]
        o = jnp.concatenate(outs, axis=-1)
        o_ref[...] = o
        g = g_ref[...]
        z_ref[...] = (o * (g * _sigmoid(g))).astype(BF16)

    row = pl.BlockSpec((WIN, D), lambda n: (n, 0))
    prev = pl.BlockSpec((WIN, N_KV * HD), lambda n: (jnp.maximum(n - 1, 0), 0))
    cur = pl.BlockSpec((WIN, N_KV * HD), lambda n: (n, 0))
    return pl.pallas_call(
        body, name="swa_fwd", grid=(nb,),
        in_specs=[row, prev, cur, prev, cur, pl.BlockSpec((WIN, D), lambda n: (n, 1)),
                  pl.BlockSpec(memory_space=pltpu.SMEM)],
        out_specs=[row, row, pl.BlockSpec((N_HEADS, 1, 1, WIN), lambda n: (0, n, 0, 0))],
        out_shape=[_sds((S, D), F32), _sds((S, D), BF16), _sds((N_HEADS, nb, 1, WIN), F32)],
        compiler_params=_params())(qb, ksh, ksh, vsh, vsh, pb, sinks)


def _swa_bwd(qb, ksh, vsh, dz, o, lse, pb, sinks, gq, cos2, sin2):
    nb = S // WIN

    def body(q_ref, kp_ref, kc_ref, vp_ref, vc_ref, dz_ref, o_ref, lse_ref, x_ref, g_ref, s_ref, gq_ref, c_ref, sn_ref,
             dpb_ref, dka_ref, dkb_ref, dva_ref, dvb_ref, dsink_ref, dgq_ref):
        n = pl.program_id(0)

        @pl.when(n == 0)
        def _():
            dsink_ref[...] = jnp.zeros_like(dsink_ref)
            dgq_ref[...] = jnp.zeros_like(dgq_ref)

        valid = _swa_mask(n)
        g = g_ref[...]
        sg = _sigmoid(g)
        dzv = dz_ref[...]
        ov = o_ref[...]
        do = dzv * (g * sg)
        dpb_ref[:, D:2 * D] = (dzv * ov * (sg * (1.0 + g * (1.0 - sg)))).astype(BF16)
        prod_t = (do * ov).T
        lane1 = _lane_iota((1, LANES))
        dqs, dkas, dkbs, dvas, dvbs = [], [], [], [], []
        dsink = jnp.zeros((1, LANES), F32)
        for kh in range(N_KV):
            kb, vb = _band(kp_ref, kc_ref, kh), _band(vp_ref, vc_ref, kh)
            base = GRP * HD * kh
            qs = _stack4(q_ref, base)
            dos = _stack4(do, base).astype(BF16)
            delta = jnp.concatenate(
                [jnp.sum(prod_t[base + HD * gg:base + HD * (gg + 1), :], axis=0, keepdims=True)
                 for gg in range(GRP)], axis=1)
            lse = jnp.concatenate([lse_ref[GRP * kh + gg, 0] for gg in range(GRP)], axis=1)
            st = jnp.where(valid, _dot_nt(kb, qs), -jnp.inf)
            pt = jnp.exp(st - lse)
            dst = pt * (_dot_nt(vb, dos) - delta)
            dsb = dst.astype(BF16)
            dqs.append(_unstack4(_dot_tn(kb, dsb)))
            dkband = _dot_nn(dsb, qs)
            dvband = _dot_nn(pt.astype(BF16), dos)
            dkbs.append(dkband[0:WIN, :])
            dkas.append(dkband[WIN:2 * WIN, :])
            dvbs.append(dvband[0:WIN, :])
            dvas.append(dvband[WIN:2 * WIN, :])
            ps_delta = jnp.exp(_sink_row(s_ref, GRP * kh) - lse) * delta
            for gg in range(GRP):
                val = jnp.sum(ps_delta[:, WIN * gg:WIN * (gg + 1)], axis=1, keepdims=True)
                dsink = dsink - jnp.where(lane1 == GRP * kh + gg, val, 0.0)
        dka_ref[...] = jnp.concatenate(dkas, axis=-1)
        dkb_ref[...] = jnp.concatenate(dkbs, axis=-1)
        dva_ref[...] = jnp.concatenate(dvas, axis=-1)
        dvb_ref[...] = jnp.concatenate(dvbs, axis=-1)
        dsink_ref[...] += dsink

        lane = _lane_iota((WIN, LANES))
        g2, cos, sin = _g2(gq_ref), c_ref[...], sn_ref[...]
        dg_tot = jnp.zeros((1, LANES), F32)
        for kh in range(N_KV):
            for c in _pairs(GRP * HD):
                cols = slice(GRP * HD * kh + c.start, GRP * HD * kh + c.stop)
                dn = _rope_bwd(dqs[kh][:, c] * QSCALE, cos, sin, lane)
                dx, dg = _head_norm_bwd(dn, x_ref[:, cols], g2, lane < HD)
                dpb_ref[:, cols] = dx.astype(BF16)
                dg_tot = dg_tot + dg
        dgq_ref[...] += dg_tot

    row = pl.BlockSpec((WIN, D), lambda n: (n, 0))
    prev = pl.BlockSpec((WIN, N_KV * HD), lambda n: (jnp.maximum(n - 1, 0), 0))
    cur = pl.BlockSpec((WIN, N_KV * HD), lambda n: (n, 0))
    acc = pl.BlockSpec((1, LANES), lambda n: (0, 0))
    tab = pl.BlockSpec((WIN, LANES), lambda n: (n, 0))
    return pl.pallas_call(
        body, name="swa_bwd", grid=(nb,),
        in_specs=[row, prev, cur, prev, cur, row, row, pl.BlockSpec((N_HEADS, 1, 1, WIN), lambda n: (0, n, 0, 0)),
                  row, pl.BlockSpec((WIN, D), lambda n: (n, 1)), pl.BlockSpec(memory_space=pltpu.SMEM),
                  pl.BlockSpec((1, HD), lambda n: (0, 0)), tab, tab],
        out_specs=[pl.BlockSpec((WIN, 2 * D), lambda n: (n, 0)), cur, cur, cur, cur, acc, acc],
        out_shape=[_sds((S, 2 * D), BF16)] + [_sds((S, 256), F32)] * 4 + [_sds((1, LANES), F32)] * 2,
        compiler_params=_params())(qb, ksh, ksh, vsh, vsh, dz, o, lse, pb, pb, sinks, gq, cos2, sin2)


def _prep_kv_bwd(dka, dkb, dva, dvb, kv, gk, cos2, sin2):
    nt = S // RB
    per = RB // WIN

    def shifted(cur_ref, nxt_ref, has_next):
        return jnp.concatenate([cur_ref[WIN:RB, :], jnp.where(has_next, nxt_ref[...], 0.0)], axis=0)

    def body(dka_ref, dkb_ref, dkn_ref, dva_ref, dvb_ref, dvn_ref, x_ref, g_ref, c_ref, s_ref, o_ref, dgk_ref):
        i, j = pl.program_id(0), pl.program_id(1)

        @pl.when((i == 0) & (j == 0))
        def _():
            dgk_ref[...] = jnp.zeros_like(dgk_ref)

        has_next = i < nt - 1

        @pl.when(j == 0)
        def _():
            lane = _lane_iota((RB, LANES))
            g2, cos, sin = _g2(g_ref), c_ref[...], s_ref[...]
            dy_all = dka_ref[...] + shifted(dkb_ref, dkn_ref, has_next)
            dg_tot = jnp.zeros((1, LANES), F32)
            for c in _pairs(CB):
                dn = _rope_bwd(dy_all[:, c], cos, sin, lane)
                dx, dg = _head_norm_bwd(dn, x_ref[:, c], g2, lane < HD)
                o_ref[:, c] = dx.astype(BF16)
                dg_tot = dg_tot + dg
            dgk_ref[...] += dg_tot

        @pl.when(j == 1)
        def _():
            o_ref[...] = (dva_ref[...] + shifted(dvb_ref, dvn_ref, has_next)).astype(BF16)

    cur = pl.BlockSpec((RB, CB), lambda i, j: (i, 0))
    nxt = pl.BlockSpec((WIN, CB), lambda i, j: (jnp.minimum(per * (i + 1), S // WIN - 1), 0))
    tab = pl.BlockSpec((RB, LANES), lambda i, j: (i, 0))
    return pl.pallas_call(
        body, name="prep_kv_bwd", grid=(nt, 2),
        in_specs=[cur, cur, nxt, cur, cur, nxt, cur, pl.BlockSpec((1, HD), lambda i, j: (0, 0)), tab, tab],
        out_specs=[pl.BlockSpec((RB, CB), lambda i, j: (i, j)), pl.BlockSpec((1, LANES), lambda i, j: (0, 0))],
        out_shape=[_sds((S, 2 * CB), BF16), _sds((1, LANES), F32)],
        compiler_params=_params())(dka, dkb, dkb, dva, dvb, dvb, kv, gk, cos2, sin2)


def _out_b_loss(z, w_out, h1, tgt):
    tm = 512

    def body(z_ref, w_ref, h_ref, t_ref, dy_ref, l_ref):
        @pl.when(pl.program_id(0) == 0)
        def _():
            l_ref[...] = jnp.zeros_like(l_ref)

        e = (h_ref[...] + _dot_nn(z_ref[...], w_ref[...])) - t_ref[...]
        dy_ref[...] = e * (1.0 / D)
        l_ref[...] += jnp.sum(jnp.sum(e * e, axis=-1, keepdims=True), axis=0, keepdims=True)

    row = pl.BlockSpec((tm, D), lambda i: (i, 0))
    return pl.pallas_call(
        body, name="out_b_loss", grid=(S // tm,),
        in_specs=[row, pl.BlockSpec(w_out.shape, lambda i: (0, 0)), row, row],
        out_specs=[row, pl.BlockSpec((1, LANES), lambda i: (0, 0))],
        out_shape=[_sds((S, D), F32), _sds((1, LANES), F32)], compiler_params=_params())(z, w_out, h1, tgt)


def _du_a_rms_bwd(dproj, wa, x, g, dres, after):
    tm, tk = 512, NA // 2
    nk = NA // tk

    def body(a_ref, b_ref, x_ref, g_ref, dr_ref, after_ref, dx_ref, dg_ref, acc):
        i, kk = pl.program_id(0), pl.program_id(1)

        @pl.when((i == 0) & (kk == 0))
        def _():
            dg_ref[...] = jnp.zeros_like(dg_ref)

        p = _dot_nt(a_ref[...], b_ref[...])

        @pl.when(kk == 0)
        def _():
            acc[...] = p

        @pl.when(kk == nk - 1)
        def _():
            dx, dg = _rms_bwd_core(acc[...] + p, x_ref[...], g_ref[...])
            dx_ref[...] = dr_ref[...] + dx
            dg_ref[...] += dg

    assert nk == 2
    row = pl.BlockSpec((tm, D), lambda i, kk: (i, 0))
    vec = pl.BlockSpec((1, D), lambda i, kk: (0, 0))
    return pl.pallas_call(
        body, name="du_a_rms_bwd", grid=(S // tm, nk),
        in_specs=[pl.BlockSpec((tm, tk), lambda i, kk: (i, kk)), pl.BlockSpec((D, tk), lambda i, kk: (0, kk)),
                  row, vec, row, pl.BlockSpec(after.shape, lambda i, kk: (0, 0))],
        out_specs=[row, vec], out_shape=[_sds((S, D), F32), _sds((1, D), F32)],
        scratch_shapes=[pltpu.VMEM((tm, D), F32)], compiler_params=_params())(dproj, wa, x, g, dres, after)


def _du_b_rms_bwd(dpb, w_in_b, dkv, w_kv, h1, g_b, g_kv, dy):
    tm = 512

    def body(ab_ref, wb_ref, akv_ref, wkv_ref, x_ref, gb_ref, gkv_ref, dy_ref, dh_ref, dgb_ref, dgkv_ref):
        @pl.when(pl.program_id(0) == 0)
        def _():
            dgb_ref[...] = jnp.zeros_like(dgb_ref)
            dgkv_ref[...] = jnp.zeros_like(dgkv_ref)

        x = x_ref[...]
        dx1, dg1 = _rms_bwd_core(_dot_nt(ab_ref[...], wb_ref[...]), x, gb_ref[...])
        dx2, dg2 = _rms_bwd_core(_dot_nt(akv_ref[...], wkv_ref[...]), x, gkv_ref[...])
        dh_ref[...] = dy_ref[...] + dx1 + dx2
        dgb_ref[...] += dg1
        dgkv_ref[...] += dg2

    row = lambda width: pl.BlockSpec((tm, width), lambda i: (i, 0))
    whole = lambda arr: pl.BlockSpec(arr.shape, lambda i: (0, 0))
    vec = pl.BlockSpec((1, D), lambda i: (0, 0))
    return pl.pallas_call(
        body, name="du_b_rms_bwd", grid=(S // tm,),
        in_specs=[row(dpb.shape[1]), whole(w_in_b), row(dkv.shape[1]), whole(w_kv), row(D), vec, vec, row(D)],
        out_specs=[row(D), vec, vec], out_shape=[_sds((S, D), F32), _sds((1, D), F32), _sds((1, D), F32)],
        compiler_params=_params())(dpb, w_in_b, dkv, w_kv, h1, g_b, g_kv, dy)


def _gather_first(w_in_a, w_out_a, w_kv, w_in_b, w_out_b, norm_a_g):
    def body(wia_ref, woa_ref, wkv_ref, wib_ref, wob_ref, ga_ref,
             wa_g, ga_g, woa_s, wkv_s, wib_s, wob_s, wa_s, st_a, st_oa, st_kv, st_ib, st_ob, load_sems, *sems):
        sources = [wia_ref.at[0], woa_ref.at[0], wkv_ref, wib_ref.at[0], wob_ref.at[0]]
        stages = [st_a, st_oa, st_kv, st_ib, st_ob]
        loads = [pltpu.make_async_copy(src, dst, load_sems.at[i]) for i, (src, dst) in enumerate(zip(sources, stages))]
        for cp in loads:
            cp.start()
        loads[0].wait()
        wa_s[...] = st_a[...].astype(BF16)

        def cast_the_rest():
            for cp, stage, out in zip(loads[1:], stages[1:], [woa_s, wkv_s, wib_s, wob_s]):
                cp.wait()
                out[...] = stage[...].astype(BF16)

        _gather_two_level([wa_s, ga_ref], [wa_g, ga_g], sems, meanwhile=cast_the_rest)

    vmem = pl.BlockSpec(memory_space=pltpu.VMEM)
    anyspec = pl.BlockSpec(memory_space=pl.ANY)
    shard = lambda w: _sds(w.shape[-2:], BF16)
    stage = lambda w: pltpu.VMEM(w.shape[-2:], F32)
    return pl.pallas_call(
        body, name="gather_first", in_specs=[anyspec] * 5 + [vmem],
        out_specs=[anyspec, anyspec, vmem, vmem, vmem, vmem],
        out_shape=[_sds((N_DEV,) + w_in_a.shape[-2:], BF16), _sds((N_DEV,) + norm_a_g.shape, F32),
                   shard(w_out_a), shard(w_kv), shard(w_in_b), shard(w_out_b)],
        scratch_shapes=[pltpu.VMEM(w_in_a.shape[-2:], BF16), stage(w_in_a), stage(w_out_a), stage(w_kv),
                        stage(w_in_b), stage(w_out_b), pltpu.SemaphoreType.DMA((5,))] + _exchange_sems(2),
        compiler_params=pltpu.CompilerParams(vmem_limit_bytes=VMEM_LIMIT, has_side_effects=True))(
            w_in_a, w_out_a, w_kv, w_in_b, w_out_b, norm_a_g)


def _pair_reduce(slots):
    n_chip = N_DEV // 2
    _, rows, cols = slots.shape

    def body(s_ref, o_ref, own_v, sib_v, send_sems, recv_sems, local_sems):
        x, y, c = lax.axis_index("x"), lax.axis_index("y"), lax.axis_index("c")
        copies = []
        for j in range(n_chip):
            own = pltpu.make_async_copy(s_ref.at[2 * j + c], own_v.at[j], local_sems.at[j])
            give = pltpu.make_async_remote_copy(
                src_ref=s_ref.at[2 * j + 1 - c], dst_ref=sib_v.at[j], send_sem=send_sems.at[j],
                recv_sem=recv_sems.at[j], device_id=(x, y, 1 - c), device_id_type=pl.DeviceIdType.MESH)
            own.start()
            give.start()
            copies.append((own, give))
        for j, (own, give) in enumerate(copies):
            own.wait()
            give.wait()
            o_ref[j] = (own_v[j].astype(F32) + sib_v[j].astype(F32)).astype(BF16)

    half = _sds((n_chip, rows, cols), slots.dtype)
    return pl.pallas_call(
        body, name="pair_reduce", in_specs=[pl.BlockSpec(memory_space=pl.ANY)],
        out_specs=pl.BlockSpec(memory_space=pltpu.VMEM), out_shape=half,
        scratch_shapes=[pltpu.VMEM(half.shape, half.dtype), pltpu.VMEM(half.shape, half.dtype),
                        pltpu.SemaphoreType.DMA((n_chip,)), pltpu.SemaphoreType.DMA((n_chip,)),
                        pltpu.SemaphoreType.DMA((n_chip,))],
        compiler_params=pltpu.CompilerParams(vmem_limit_bytes=VMEM_LIMIT, has_side_effects=True))(slots)


def _padded_col(c):
    if c < RAW_F:
        return c
    return FOFF + (c - RAW_F) if c < RAW_G else GOFF + (c - RAW_G)


def _shard_pieces():
    width = NA_RAW // N_DEV
    pieces = []
    for d in range(N_DEV):
        cuts = [width * d] + [c for c in (RAW_F, RAW_G) if width * d < c < width * (d + 1)] + [width * (d + 1)]
        for lo, hi in zip(cuts[:-1], cuts[1:]):
            pieces.append((d, lo - width * d, _padded_col(lo), hi - lo))
    return pieces


def _unshard_wa(wa_g):
    def body(w_ref, o_ref):
        o_ref[:, FOFF + N_HEADS:NA] = jnp.zeros((TM, NA - FOFF - N_HEADS), BF16)
        for d, src, dst, width in _shard_pieces():
            o_ref[:, dst:dst + width] = w_ref[d, :, src:src + width]

    return pl.pallas_call(
        body, name="unshard_wa", grid=(D // TM,),
        in_specs=[pl.BlockSpec((N_DEV, TM, NA_RAW // N_DEV), lambda i: (0, i, 0))],
        out_specs=pl.BlockSpec((TM, NA), lambda i: (i, 0)), out_shape=_sds((D, NA), BF16),
        compiler_params=_params())(wa_g)


def _reshard_dwa(dwa):
    def body(g_ref, o_ref):
        for d, src, dst, width in _shard_pieces():
            o_ref[d, :, src:src + width] = g_ref[:, dst:dst + width]

    return pl.pallas_call(
        body, name="reshard_dwa", grid=(D // TM,), in_specs=[pl.BlockSpec((TM, NA), lambda i: (i, 0))],
        out_specs=pl.BlockSpec((N_DEV, TM, NA_RAW // N_DEV), lambda i: (0, i, 0)),
        out_shape=_sds((N_DEV, D, NA_RAW // N_DEV), dwa.dtype), compiler_params=_params())(dwa)


CHIP_FLIPS = (2, 4, 6)


def _chip_exchange_start(partial):
    n = len(CHIP_FLIPS)

    def body(p_ref, land_ref, *rest):
        sends, recvs, token = rest[:n], rest[n:2 * n], rest[2 * n + 2]
        x, y, c = lax.axis_index("x"), lax.axis_index("y"), lax.axis_index("c")
        me = 4 * x + 2 * y + c
        for idx, k in enumerate(CHIP_FLIPS):
            pltpu.make_async_remote_copy(
                src_ref=p_ref.at[(me ^ k) >> 1], dst_ref=land_ref.at[me >> 1], send_sem=sends[idx],
                recv_sem=recvs[idx], device_id=(x ^ ((k >> 2) & 1), y ^ ((k >> 1) & 1), c),
                device_id_type=pl.DeviceIdType.MESH).start()
        token[...] = jnp.zeros_like(token)

    hbm = pl.BlockSpec(memory_space=pltpu.HBM)
    sem = pl.BlockSpec(memory_space=pltpu.SEMAPHORE)
    buf = pltpu.HBM(partial.shape, partial.dtype)
    return pl.pallas_call(
        body, name="chip_exchange_start",
        out_shape=(pltpu.SemaphoreType.DMA(()),) * (2 * n) + (buf, buf, _sds((8, LANES), F32)),
        in_specs=(hbm, hbm), out_specs=(sem,) * (2 * n) + (hbm, hbm, pl.BlockSpec(memory_space=pltpu.VMEM)),
        input_output_aliases={0: 2 * n, 1: 2 * n + 1},
        compiler_params=pltpu.CompilerParams(has_side_effects=pltpu.SideEffectType.DATAFLOW_SIDE_EFFECTING))(
            pltpu.with_memory_space_constraint(partial, pltpu.HBM),
            pltpu.with_memory_space_constraint(lax.empty(partial.shape, partial.dtype), pltpu.HBM))


def _chip_exchange_wait(started, after):
    n = len(CHIP_FLIPS)
    sems, (p_thru, land_thru) = started[:2 * n], started[2 * n:2 * n + 2]

    def body(p_ref, land_ref, *rest):
        sends, recvs = rest[:n], rest[n:2 * n]
        x, y, c = lax.axis_index("x"), lax.axis_index("y"), lax.axis_index("c")
        me = 4 * x + 2 * y + c
        for idx, k in enumerate(CHIP_FLIPS):
            copy = pltpu.make_async_remote_copy(
                src_ref=p_ref.at[(me ^ k) >> 1], dst_ref=land_ref.at[(me ^ k) >> 1], send_sem=sends[idx],
                recv_sem=recvs[idx], device_id=(x ^ ((k >> 2) & 1), y ^ ((k >> 1) & 1), c),
                device_id_type=pl.DeviceIdType.MESH)
            copy.wait_send()
            copy.wait_recv()

    hbm = pl.BlockSpec(memory_space=pltpu.HBM)
    sem = pl.BlockSpec(memory_space=pltpu.SEMAPHORE)
    buf = pltpu.HBM(p_thru.shape, p_thru.dtype)
    return pl.pallas_call(
        body, name="chip_exchange_wait", out_shape=(buf, buf),
        in_specs=(hbm, hbm) + (sem,) * (2 * n) + (pl.BlockSpec(memory_space=pl.ANY),), out_specs=(hbm, hbm),
        input_output_aliases={0: 0, 1: 1},
        compiler_params=pltpu.CompilerParams(has_side_effects=pltpu.SideEffectType.DATAFLOW_SIDE_EFFECTING))(
            p_thru, land_thru, *sems, after)


def _gather_slab(slab, after):
    def body(s_ref, after_ref, o_ref, *sems):
        _exchange_ops(["gather_rows"], [s_ref], [o_ref], sems, True, True)

    anyspec = pl.BlockSpec(memory_space=pl.ANY)
    return pl.pallas_call(
        body, name="gather_slab", in_specs=[anyspec, anyspec], out_specs=anyspec,
        out_shape=_sds((N_DEV,) + slab.shape, slab.dtype), scratch_shapes=_exchange_sems(1),
        compiler_params=pltpu.CompilerParams(has_side_effects=True))(slab, after)


def _adamw(w, g, m, v):
    m = ADAM_B1 * m + (1.0 - ADAM_B1) * g
    v = ADAM_B2 * v + (1.0 - ADAM_B2) * (g * g)
    m_hat = m / (1.0 - ADAM_B1 ** ADAM_STEP)
    v_hat = v / (1.0 - ADAM_B2 ** ADAM_STEP)
    delta = -ADAM_LR * (m_hat / (jnp.sqrt(v_hat) + ADAM_EPS) + ADAM_WD * w)
    return delta, m, v


def _sum_adamw(recv, w, m, v, name, after=None):
    lead = w.ndim - 2
    rows, cols = w.shape[-2:]
    tr = 256 if rows % 256 == 0 else 128
    slabs = list(recv) if isinstance(recv, tuple) else [recv]
    n_slots = slabs[0].shape[0]
    extra = [] if after is None else [after]

    def body(*refs):
        r_ref, own_ref = refs[0], refs[len(slabs) - 1]
        w_ref, m_ref, v_ref, g_ref, d_ref, nm_ref, nv_ref = refs[len(slabs) + len(extra):]
        chip = (4 * lax.axis_index("x") + 2 * lax.axis_index("y") + lax.axis_index("c")) >> 1
        g = None
        for slot in range(n_slots):
            part = r_ref[slot]
            if len(slabs) == 2:
                part = jnp.where(chip == slot, own_ref[slot], part)
            g = part.astype(F32) if g is None else g + part.astype(F32)
        g_ref[...] = g
        d_ref[...], nm_ref[...], nv_ref[...] = _adamw(w_ref[...], g, m_ref[...], v_ref[...])

    blk = pl.BlockSpec((None,) * lead + (tr, cols), lambda i: (0,) * lead + (i, 0))
    slots = pl.BlockSpec((n_slots, tr, cols), lambda i: (0, i, 0))
    return pl.pallas_call(
        body, name=name, grid=(rows // tr,),
        in_specs=[slots] * len(slabs) + [pl.BlockSpec(a.shape, lambda i: (0, 0)) for a in extra] + [blk, blk, blk],
        out_specs=[blk] * 4, out_shape=[_sds(w.shape, F32)] * 4, compiler_params=_params())(*slabs, *extra, w, m, v)


SLAB_ROWS = 16
SLOT = {"kv_norm_g": (8, 0, D), "norm_b_g": (9, 0, D), "b_forget": (10, 0, 16), "qnorm_a_g": (10, 128, HD),
        "knorm_a_g": (10, 256, HD), "knorm_b_g": (10, 384, HD), "qnorm_b_g": (10, 512, HD), "sinks": (10, 640, 16)}
SMALL = ["norm_a_g", "b_forget", "qnorm_a_g", "knorm_a_g", "kv_norm_g", "knorm_b_g", "norm_b_g", "qnorm_b_g", "sinks"]


LOSS_ROW = 11


def _pack_small(dg_a, dg_kv, dg_b, db_f, dgq_a, dgk_a, dgk_b, dgq_b, dsinks, lsum):
    def fold(ref):
        return ref[:, 0:HD] + ref[:, HD:2 * HD]

    def body(dga_ref, dgkv_ref, dgb_ref, dbf_ref, dgqa_ref, dgka_ref, dgkb_ref, dgqb_ref, dsk_ref, ls_ref, slab_ref):
        slab_ref[...] = jnp.zeros_like(slab_ref)
        for r in range(N_DEV):
            slab_ref[r:r + 1, 0:LANES] = dga_ref[:, LANES * r:LANES * (r + 1)]
        slab_ref[8:9, :] = dgkv_ref[...]
        slab_ref[9:10, :] = dgb_ref[...]
        slab_ref[10:11, 0:LANES] = dbf_ref[...]
        slab_ref[10:11, 128:128 + HD] = fold(dgqa_ref)
        slab_ref[10:11, 256:256 + HD] = fold(dgka_ref)
        slab_ref[10:11, 384:384 + HD] = fold(dgkb_ref)
        slab_ref[10:11, 512:512 + HD] = fold(dgqb_ref)
        slab_ref[10:11, 640:640 + LANES] = dsk_ref[...]
        slab_ref[LOSS_ROW:LOSS_ROW + 1, 0:LANES] = ls_ref[...]

    return pl.pallas_call(body, name="pack_small", out_shape=_sds((SLAB_ROWS, D), F32), compiler_params=_params())(
        dg_a, dg_kv, dg_b, db_f, dgq_a, dgk_a, dgk_b, dgq_b, dsinks, lsum)


def _small_adamw(recv, ws, ms, vs):
    k = len(SMALL)

    def body(*refs):
        r_ref = refs[0]
        w_refs, m_refs, v_refs = refs[1:1 + k], refs[1 + k:1 + 2 * k], refs[1 + 2 * k:1 + 3 * k]
        outs = refs[1 + 3 * k:1 + 7 * k]
        loss_ref, tot = refs[1 + 7 * k], refs[2 + 7 * k]
        g = r_ref[0]
        for dev in range(1, N_DEV):
            g = g + r_ref[dev]
        tot[...] = g
        loss_ref[...] = tot[LOSS_ROW:LOSS_ROW + 1, 0:LANES] * (0.5 / D)
        me = 4 * lax.axis_index("x") + 2 * lax.axis_index("y") + lax.axis_index("c")
        for p, name in enumerate(SMALL):
            if name == "norm_a_g":
                mine = lax.broadcasted_iota(jnp.int32, (N_DEV, LANES), 0) == me
                gp = jnp.sum(jnp.where(mine, tot[0:N_DEV, 0:LANES], 0.0), axis=0, keepdims=True)
            else:
                row, lo, width = SLOT[name]
                gp = tot[row:row + 1, lo:lo + width]
            d, nm, nv = _adamw(w_refs[p][...], gp, m_refs[p][...], v_refs[p][...])
            outs[p][...] = gp
            outs[k + p][...] = d
            outs[2 * k + p][...] = nm
            outs[3 * k + p][...] = nv

    shapes = [_sds(w.shape, F32) for w in ws]
    return pl.pallas_call(body, name="small_adamw", out_shape=shapes * 4 + [_sds((1, LANES), F32)],
                          scratch_shapes=[pltpu.VMEM((SLAB_ROWS, D), F32)],
                          compiler_params=_params())(recv, *ws, *ms, *vs)


def _rope_tables(positions):
    inv_freq = jnp.power(jnp.float32(ROPE_THETA), -jnp.arange(0, ROT, 2, dtype=F32) / ROT)
    ang = positions.astype(F32)[:, None] * inv_freq[None, :]
    cos, sin = jnp.cos(ang), jnp.sin(ang)
    c64 = jnp.concatenate([cos, cos, jnp.ones((S, HD - ROT), F32)], axis=-1)
    s64 = jnp.concatenate([-sin, sin, jnp.zeros((S, HD - ROT), F32)], axis=-1)
    return jnp.tile(c64, (1, 2)), jnp.tile(s64, (1, 2))


def _local_step(x, tgt, positions, g_a, wa, b_forget, gq_a, gk_a, g_kv, gk_b, g_b, gq_b, sinks,
                woa_s, wkv_s, wib_s, wob_s, adamw_others):
    nq = S // TQ
    cos2, sin2 = _rope_tables(positions)
    b_pad = jnp.pad(b_forget, ((0, 0), (0, LANES - N_HEADS)))

    u_a, proj, qn, kn, vb, ccol, cbc = _head_a(x, g_a, wa, gq_a, gk_a, b_pad)
    crow = ccol[:, :N_HEADS].T.reshape(N_HEADS, nq, 1, TQ)
    o_a, z_a, lse_a, woa_g, wkv_g, w_in_b, wob_g = _fox_fwd(
        qn, kn, vb, proj, crow, cbc,
        rider=[("gather_rows", woa_s), ("gather_rows", wkv_s), ("gather_cols", wib_s), ("gather_rows", wob_s)])
    w_out_a, w_kv, w_out_b = woa_g.reshape(D, D), wkv_g.reshape(D, 512), wob_g.reshape(D, D)
    h1, u_kv, u_b, kv, pb, qb, ksh, vsh = _head_b(x, z_a, w_out_a, g_kv, g_b, w_kv, w_in_b, gq_b, gk_b, cos2, sin2)
    sinks1 = sinks.reshape(N_HEADS)
    o_b, z_b, lse_b = _swa_fwd(qb, ksh, vsh, pb, sinks1)
    dy, lsum = _out_b_loss(z_b, w_out_b, h1, tgt)
    dw_out_b = _mm(z_b, dy, "tn", 512, 512, S, out_dtype=BF16, name="mm_dw_out_b")
    dz_b = _mm(dy, w_out_b, "nt", 1024, 512, D, name="mm_dz_b")
    dpb, dka, dkb, dva, dvb, dsinks, dgq_b = _swa_bwd(qb, ksh, vsh, dz_b, o_b, lse_b, pb, sinks1, gq_b, cos2, sin2)
    dkv, dgk_b = _prep_kv_bwd(dka, dkb, dva, dvb, kv, gk_b, cos2, sin2)
    dw_in_b = _mm(u_b, dpb, "tn", 512, 512, S, out_dtype=BF16, name="mm_dw_in_b")
    dw_kv = _mm(u_kv, dkv, "tn", 512, 512, S, out_dtype=BF16, name="mm_dw_kv")
    dh1, dg_b, dg_kv = _du_b_rms_bwd(dpb, w_in_b, dkv, w_kv, h1, g_b, g_kv, dy)
    dw_out_a = _mm(z_a, dh1, "tn", 512, 512, S, out_dtype=BF16, name="mm_dw_out_a")
    do_a, dgate_a, delta_a = _fox_bwd_pre(dh1, w_out_a, proj, o_a)
    dk_a, dv_a, dcs, dq_a, drow, r_wob, r_wib, r_wkv, r_woa = _fox_bwd(
        qn, kn, vb, do_a, lse_a, delta_a, crow, cbc,
        rider=[("a2a_rows", dw_out_b), ("a2a_cols", dw_in_b), ("a2a_rows", dw_kv), ("a2a_rows", dw_out_a)])
    drow_col = jnp.pad(drow.reshape(N_HEADS, S).T, ((0, 0), (0, LANES - N_HEADS)))
    dproj, dgq_a, dgk_a, db_f = _prep_a_bwd(dq_a, dk_a, dv_a, dgate_a, drow_col, dcs, proj, b_pad, gq_a, gk_a)
    dwa = _mm(u_a, dproj, "tn", 1024, 256, S, out_dtype=BF16, name="mm_dw_in_a")
    partial = _pair_reduce(_reshard_dwa(dwa))
    started = _chip_exchange_start(partial)
    dx, dg_a = _du_a_rms_bwd(dproj, wa, x, g_a, dh1, after=started[-1])
    others = adamw_others(dict(w_out_a=r_woa, w_kv=r_wkv, w_in_b=r_wib, w_out_b=r_wob), dg_a)
    partial, landed = _chip_exchange_wait(started, others["w_out_b"][0])
    slab = _pack_small(dg_a, dg_kv, dg_b, db_f, dgq_a, dgk_a, dgk_b, dgq_b, dsinks, lsum)
    return dx, (landed, partial), others, _gather_slab(slab, landed)


def kernel(x, positions, norm_a_g, w_in_a, b_forget, qnorm_a_g, knorm_a_g, w_out_a, kv_norm_g, w_kv, knorm_b_g, norm_b_g, w_in_b, qnorm_b_g, sinks, w_out_b, loss_target, m_norm_a_g, m_w_in_a, m_b_forget, m_qnorm_a_g, m_knorm_a_g, m_w_out_a, m_kv_norm_g, m_w_kv, m_knorm_b_g, m_norm_b_g, m_w_in_b, m_qnorm_b_g, m_sinks, m_w_out_b, v_norm_a_g, v_w_in_a, v_b_forget, v_qnorm_a_g, v_knorm_a_g, v_w_out_a, v_kv_norm_g, v_w_kv, v_knorm_b_g, v_norm_b_g, v_w_in_b, v_qnorm_b_g, v_sinks, v_w_out_b):
    wa_g, ga_g, woa_s, wkv_s, wib_s, wob_s = _gather_first(w_in_a, w_out_a, w_kv, w_in_b, w_out_b, norm_a_g)
    state = dict(w_in_a=(w_in_a, m_w_in_a, v_w_in_a), w_out_a=(w_out_a, m_w_out_a, v_w_out_a),
                 w_kv=(w_kv, m_w_kv, v_w_kv), w_in_b=(w_in_b, m_w_in_b, v_w_in_b),
                 w_out_b=(w_out_b, m_w_out_b, v_w_out_b))

    def adamw_others(landed, after):
        return {n: _sum_adamw(r, *state[n], "adamw_" + n, after=after) for n, r in landed.items()}

    dx, r_wa, big, slab_g = _local_step(
        x[0], loss_target[0], positions, ga_g.reshape(1, D), _unshard_wa(wa_g), b_forget, qnorm_a_g, knorm_a_g,
        kv_norm_g.reshape(1, D), knorm_b_g.reshape(1, HD), norm_b_g, qnorm_b_g, sinks, woa_s, wkv_s, wib_s, wob_s,
        adamw_others)
    big["w_in_a"] = _sum_adamw(r_wa, *state["w_in_a"], "adamw_w_in_a")

    r2 = lambda a: a.reshape(1, -1)
    small_w = dict(norm_a_g=norm_a_g, b_forget=b_forget, qnorm_a_g=qnorm_a_g, knorm_a_g=knorm_a_g,
                   kv_norm_g=kv_norm_g, knorm_b_g=knorm_b_g, norm_b_g=norm_b_g, qnorm_b_g=qnorm_b_g, sinks=sinks)
    small_m = dict(norm_a_g=m_norm_a_g, b_forget=m_b_forget, qnorm_a_g=m_qnorm_a_g, knorm_a_g=m_knorm_a_g,
                   kv_norm_g=m_kv_norm_g, knorm_b_g=m_knorm_b_g, norm_b_g=m_norm_b_g, qnorm_b_g=m_qnorm_b_g,
                   sinks=m_sinks)
    small_v = dict(norm_a_g=v_norm_a_g, b_forget=v_b_forget, qnorm_a_g=v_qnorm_a_g, knorm_a_g=v_knorm_a_g,
                   kv_norm_g=v_kv_norm_g, knorm_b_g=v_knorm_b_g, norm_b_g=v_norm_b_g, qnorm_b_g=v_qnorm_b_g,
                   sinks=v_sinks)
    res = _small_adamw(slab_g, [r2(small_w[n]) for n in SMALL], [r2(small_m[n]) for n in SMALL],
                       [r2(small_v[n]) for n in SMALL])
    k = len(SMALL)
    small = {n: [res[q * k + p].reshape(small_w[n].shape) for q in range(4)] for p, n in enumerate(SMALL)}
    loss = res[4 * k][0, 0]

    order = ["norm_a_g", "w_in_a", "b_forget", "qnorm_a_g", "knorm_a_g", "w_out_a", "kv_norm_g", "w_kv",
             "knorm_b_g", "norm_b_g", "w_in_b", "qnorm_b_g", "sinks", "w_out_b"]

    def leaf(n, q):
        return big[n][q] if n in big else small[n][q]

    outs = [loss, dx[None]]
    for q in range(4):
        outs.extend(leaf(n, q) for n in order)
    return tuple(outs)
```

```python
import jax
import jax.numpy as jnp
from jax import lax
from jax.experimental import pallas as pl
from jax.experimental.pallas import tpu as pltpu

F32, BF16 = jnp.float32, jnp.bfloat16

S = 2048
D = 1024
HD = 64
N_HEADS = 16
N_DEV = 8
NA = 4352
GOFF = 3072
FOFF = 4096
RAW_F = 3072
RAW_G = RAW_F + N_HEADS
NA_RAW = 4112
EPS = 1e-6
QSCALE = 0.125
ROPE_THETA = 500000.0
ROT = 16
WIN = 128
TQ = 256
TK = 256
KS = TQ // 2
HPS = 8
HW = HPS * HD
TM = 256
RT = 512
RB = 512
CB = 256
LANES = 128

ADAM_LR, ADAM_B1, ADAM_B2, ADAM_EPS, ADAM_WD, ADAM_STEP = 0.001, 0.9, 0.999, 1e-08, 0.01, 10

VMEM_LIMIT = 56 * 1024 * 1024


def _params():
    return pltpu.CompilerParams(vmem_limit_bytes=VMEM_LIMIT)


def _sds(shape, dtype):
    return jax.ShapeDtypeStruct(shape, dtype)


def _dot_nt(a, b):
    return lax.dot_general(a, b, (((1,), (1,)), ((), ())), preferred_element_type=F32)


def _dot_tn(a, b):
    return lax.dot_general(a, b, (((0,), (0,)), ((), ())), preferred_element_type=F32)


def _dot_nn(a, b):
    return lax.dot_general(a, b, (((1,), (0,)), ((), ())), preferred_element_type=F32)


def _sigmoid(g):
    return 1.0 / (1.0 + jnp.exp(-g))


def _lane_iota(shape):
    return lax.broadcasted_iota(jnp.int32, shape, len(shape) - 1)


def _flips(kind):
    return (2, 4, 6) if kind == "a2a_chips" else tuple(range(1, N_DEV))


def _send_view(kind, ref, dev):
    if kind in ("gather_rows", "gather_cols"):
        return ref
    if kind == "a2a_slots":
        return ref.at[dev]
    if kind == "a2a_chips":
        return ref.at[dev >> 1]
    if kind == "a2a_rows":
        rows = ref.shape[0] // N_DEV
        return ref.at[pl.ds(pl.multiple_of(dev * rows, rows), rows)]
    cols = ref.shape[1] // N_DEV
    return ref.at[:, pl.ds(pl.multiple_of(dev * cols, cols), cols)]


def _land_view(kind, ref, dev):
    if kind == "gather_cols":
        cols = ref.shape[1] // N_DEV
        return ref.at[:, pl.ds(pl.multiple_of(dev * cols, cols), cols)]
    if kind == "a2a_chips":
        return ref.at[dev >> 1]
    return ref.at[dev]


def _landing_sds(kind, arr):
    if kind == "gather_rows":
        return _sds((N_DEV,) + arr.shape, arr.dtype)
    if kind == "gather_cols":
        return _sds((arr.shape[0], N_DEV * arr.shape[1]), arr.dtype)
    if kind == "a2a_rows":
        return _sds((N_DEV, arr.shape[0] // N_DEV, arr.shape[1]), arr.dtype)
    if kind == "a2a_cols":
        return _sds((N_DEV, arr.shape[0], arr.shape[1] // N_DEV), arr.dtype)
    return _sds(arr.shape, arr.dtype)


def _exchange_sems(n_parts):
    n = n_parts * (N_DEV - 1)
    return [pltpu.SemaphoreType.DMA((n,)), pltpu.SemaphoreType.DMA((n,)), pltpu.SemaphoreType.DMA((n_parts,))]


def _exchange_ops(kinds, srcs, dsts, sems, start, wait):
    send_sems, recv_sems, local_sems = sems
    x, y, c = lax.axis_index("x"), lax.axis_index("y"), lax.axis_index("c")
    me = 4 * x + 2 * y + c

    def local(a):
        return pltpu.make_async_copy(_send_view(kinds[a], srcs[a], me), _land_view(kinds[a], dsts[a], me),
                                     local_sems.at[a])

    def remote(a, k, landing_dev):
        peer = (x ^ ((k >> 2) & 1), y ^ ((k >> 1) & 1), c ^ (k & 1))
        sem = a * (N_DEV - 1) + k - 1
        return pltpu.make_async_remote_copy(
            src_ref=_send_view(kinds[a], srcs[a], me ^ k), dst_ref=_land_view(kinds[a], dsts[a], landing_dev),
            send_sem=send_sems.at[sem], recv_sem=recv_sems.at[sem], device_id=peer,
            device_id_type=pl.DeviceIdType.MESH)

    pairs = [(a, k) for k in range(1, N_DEV) for a in range(len(kinds)) if k in _flips(kinds[a])]
    if start:
        for a in range(len(kinds)):
            local(a).start()
        for a, k in pairs:
            remote(a, k, me).start()
    if wait:
        for a, k in pairs:
            remote(a, k, me ^ k).wait_recv()
            remote(a, k, me).wait_send()
        for a in range(len(kinds)):
            local(a).wait()


def _gather_two_level(srcs, dsts, sems, meanwhile=None):
    send_sems, recv_sems, local_sems = sems
    x, y, c = lax.axis_index("x"), lax.axis_index("y"), lax.axis_index("c")
    me, sibling = (x, y, c), (x, y, 1 - c)
    chips = [(1 - x, y), (x, 1 - y), (1 - x, 1 - y)]

    def slot(ref, dev):
        return ref.at[4 * dev[0] + 2 * dev[1] + dev[2]]

    def copy(a, k, block, to, src=None):
        return pltpu.make_async_remote_copy(
            src_ref=slot(dsts[a], block) if src is None else src, dst_ref=slot(dsts[a], block),
            send_sem=send_sems.at[a * (N_DEV - 1) + k], recv_sem=recv_sems.at[a * (N_DEV - 1) + k],
            device_id=to, device_id_type=pl.DeviceIdType.MESH)

    parts = range(len(srcs))
    mine = [pltpu.make_async_copy(srcs[a], slot(dsts[a], me), local_sems.at[a]) for a in parts]
    first = [copy(a, 0, me, sibling, src=srcs[a]) for a in parts]
    first += [copy(a, 1 + j, me, (*chip, c), src=srcs[a]) for j, chip in enumerate(chips) for a in parts]
    for cp in mine + first:
        cp.start()
    if meanwhile is not None:
        meanwhile()
    passed = []
    for j, chip in enumerate(chips):
        for a in parts:
            copy(a, 1 + j, (*chip, c), me).wait_recv()
            fwd = copy(a, 4 + j, (*chip, c), sibling)
            fwd.start()
            passed.append(fwd)
    for a in parts:
        copy(a, 0, sibling, me).wait_recv()
        for j, chip in enumerate(chips):
            copy(a, 4 + j, (*chip, 1 - c), me).wait_recv()
    for cp in first + passed:
        cp.wait_send()
    for cp in mine:
        cp.wait()


def _call(body, *, name, args, in_specs, out_specs, out_shape, grid=(), scratch_shapes=(), aliases=None, rider=()):
    n_in, n_out, n_scr, n_r = len(in_specs), len(out_specs), len(scratch_shapes), len(rider)
    kinds = [kind for kind, _ in rider]

    def kernel_body(*refs):
        c_in, r_in = refs[:n_in], refs[n_in:n_in + n_r]
        c_out = refs[n_in + n_r:n_in + n_r + n_out]
        r_out = refs[n_in + n_r + n_out:n_in + 2 * n_r + n_out]
        rest = refs[n_in + 2 * n_r + n_out:]
        c_scr, sems = rest[:n_scr], rest[n_scr:]
        if n_r:
            assert grid, "a rider needs a gridded call"
            ids = [pl.program_id(ax) for ax in range(len(grid))]
            first, last = ids[0] == 0, ids[0] == grid[0] - 1
            for pid, size in zip(ids[1:], grid[1:]):
                first = first & (pid == 0)
                last = last & (pid == size - 1)
            pl.when(first)(lambda: _exchange_ops(kinds, r_in, r_out, sems, True, False))
        body(*c_in, *c_out, *c_scr)
        if n_r:
            pl.when(last)(lambda: _exchange_ops(kinds, r_in, r_out, sems, False, True))

    anyspec = pl.BlockSpec(memory_space=pl.ANY)
    params = pltpu.CompilerParams(vmem_limit_bytes=VMEM_LIMIT, has_side_effects=bool(n_r))
    outs = pl.pallas_call(
        kernel_body, name=name, grid=grid, in_specs=list(in_specs) + [anyspec] * n_r,
        out_specs=list(out_specs) + [anyspec] * n_r,
        out_shape=list(out_shape) + [_landing_sds(kind, arr) for kind, arr in rider],
        scratch_shapes=list(scratch_shapes) + (_exchange_sems(n_r) if n_r else []),
        input_output_aliases=aliases or {}, compiler_params=params)(*args, *[arr for _, arr in rider])
    return list(outs)


def _mm(a, b, mode, tm, tn, tk, out_dtype=F32, add=None, name="mm", rider=()):
    if mode == "nn":
        (m, k), n = a.shape, b.shape[1]
        a_spec = pl.BlockSpec((tm, tk), lambda i, j, kk: (i, kk))
        b_spec = pl.BlockSpec((tk, tn), lambda i, j, kk: (kk, j))
        dot = _dot_nn
    elif mode == "nt":
        (m, k), n = a.shape, b.shape[0]
        a_spec = pl.BlockSpec((tm, tk), lambda i, j, kk: (i, kk))
        b_spec = pl.BlockSpec((tn, tk), lambda i, j, kk: (j, kk))
        dot = _dot_nt
    else:
        (k, m), n = a.shape, b.shape[1]
        a_spec = pl.BlockSpec((tk, tm), lambda i, j, kk: (kk, i))
        b_spec = pl.BlockSpec((tk, tn), lambda i, j, kk: (kk, j))
        dot = _dot_tn
    assert m % tm == 0 and n % tn == 0 and k % tk == 0, (m, n, k, tm, tn, tk)
    nk = k // tk
    has_add = add is not None

    def body(*refs):
        if has_add:
            a_ref, b_ref, add_ref, o_ref, acc = refs
        else:
            a_ref, b_ref, o_ref, acc = refs
        p = dot(a_ref[...].astype(BF16), b_ref[...].astype(BF16))

        def finish(total):
            if has_add:
                total = add_ref[...] + total
            o_ref[...] = total.astype(out_dtype)

        if nk == 1:
            finish(p)
        else:
            kk = pl.program_id(2)

            @pl.when(kk == 0)
            def _():
                acc[...] = p

            @pl.when(kk > 0)
            def _():
                acc[...] += p

            @pl.when(kk == nk - 1)
            def _():
                finish(acc[...])

    in_specs = [a_spec, b_spec]
    args = [a, b]
    if has_add:
        in_specs.append(pl.BlockSpec((tm, tn), lambda i, j, kk: (i, j)))
        args.append(add)
    acc_shape = (tm, tn) if nk > 1 else (8, LANES)
    outs = _call(body, name=name, args=args, grid=(m // tm, n // tn, nk), in_specs=in_specs,
                 out_specs=[pl.BlockSpec((tm, tn), lambda i, j, kk: (i, j))], out_shape=[_sds((m, n), out_dtype)],
                 scratch_shapes=[pltpu.VMEM(acc_shape, F32)], rider=rider)
    return outs if rider else outs[0]


def _rms_rinv(x):
    return lax.rsqrt(jnp.mean(x * x, axis=-1, keepdims=True) + EPS)


def _rms_bwd_core(du, x, g):
    r = _rms_rinv(x)
    dug = du * g
    dx = r * (dug - x * ((r * r) * jnp.mean(dug * x, axis=-1, keepdims=True)))
    dg = jnp.sum(du * (x * r), axis=0, keepdims=True)
    return dx, dg


def _half_sum(v, lo_half):
    s0 = jnp.sum(jnp.where(lo_half, v, 0.0), axis=-1, keepdims=True)
    s1 = jnp.sum(jnp.where(lo_half, 0.0, v), axis=-1, keepdims=True)
    return jnp.where(lo_half, s0, s1)


def _head_rinv(x, lo_half):
    return lax.rsqrt(_half_sum(x * x, lo_half) * (1.0 / HD) + EPS)


def _head_norm_bwd(dn, x, g, lo_half):
    r = _head_rinv(x, lo_half)
    dng = dn * g
    dx = r * (dng - x * ((r * r) * (_half_sum(dng * x, lo_half) * (1.0 / HD))))
    dg = jnp.sum(dn * (x * r), axis=0, keepdims=True)
    return dx, dg


def _rope_swap(x, lane):
    l64 = lane & (HD - 1)
    return jnp.where(l64 < ROT // 2, pltpu.roll(x, LANES - ROT // 2, 1), pltpu.roll(x, ROT // 2, 1))


def _rope_fwd(x, cos, sin, lane):
    return x * cos + _rope_swap(x, lane) * sin


def _rope_bwd(dy, cos, sin, lane):
    return dy * cos + jnp.where((lane & (HD - 1)) < ROT, _rope_swap(dy * sin, lane), 0.0)


def _g2(g_ref):
    g = g_ref[...]
    return jnp.concatenate([g, g], axis=-1)


def _pairs(width):
    return [slice(LANES * c, LANES * (c + 1)) for c in range(width // LANES)]


def _pick_lane(block, lane, idx):
    return jnp.sum(jnp.where(lane == idx, block, 0.0), axis=-1, keepdims=True)


def _head_a(x, g, wa, gq, gk, b_pad):
    def body(x_ref, g_ref, w_ref, gq_ref, gk_ref, b_ref, u_ref, p_ref, qo_ref, ko_ref, vo_ref, c_ref, cbc_ref, carry):
        @pl.when(pl.program_id(0) == 0)
        def _():
            carry[...] = jnp.zeros_like(carry)

        xv = x_ref[...]
        u = ((xv * _rms_rinv(xv)) * g_ref[...]).astype(BF16)
        u_ref[...] = u
        for lo in range(0, NA, D):
            hi = min(lo + D, NA)
            p_ref[:, lo:hi] = _dot_nn(u, w_ref[:, lo:hi])
        lane = _lane_iota((RT, LANES))
        lo_half = lane < HD
        gq2, gk2 = _g2(gq_ref), _g2(gk_ref)
        for c in _pairs(D):
            q = p_ref[:, c]
            k = p_ref[:, D + c.start:D + c.stop]
            qo_ref[:, c] = (((q * _head_rinv(q, lo_half)) * gq2) * QSCALE).astype(BF16)
            ko_ref[:, c] = ((k * _head_rinv(k, lo_half)) * gk2).astype(BF16)
        vo_ref[...] = p_ref[:, 2 * D:3 * D].astype(BF16)

        z = p_ref[:, FOFF:FOFF + LANES] + b_ref[...]
        logf = jnp.minimum(z, 0.0) - jnp.log1p(jnp.exp(-jnp.abs(z)))
        r = lax.broadcasted_iota(jnp.int32, (RT, RT), 0)
        cc = lax.broadcasted_iota(jnp.int32, (RT, RT), 1)
        tri = (r >= cc).astype(F32)
        loc = jnp.dot(tri, logf, precision=lax.Precision.HIGHEST, preferred_element_type=F32) + carry[0:1, :]
        c_ref[...] = loc
        carry[0:1, :] = loc[RT - 1:RT, :]
        for h in range(N_HEADS):
            cbc_ref[:, LANES * h:LANES * (h + 1)] = jnp.broadcast_to(_pick_lane(loc, lane, h), (RT, LANES))

    row = lambda width: pl.BlockSpec((RT, width), lambda i: (i, 0))
    whole = lambda arr: pl.BlockSpec(arr.shape, lambda i: (0,) * arr.ndim, pipeline_mode=pl.Buffered(1))
    return pl.pallas_call(
        body, name="head_a", grid=(S // RT,),
        in_specs=[row(D), whole(g), whole(wa), whole(gq), whole(gk), whole(b_pad)],
        out_specs=[row(D), row(NA), row(D), row(D), row(D), row(LANES), row(N_HEADS * LANES)],
        out_shape=[_sds((S, D), BF16), _sds((S, NA), F32)] + [_sds((S, D), BF16)] * 3
        + [_sds((S, LANES), F32), _sds((S, N_HEADS * LANES), F32)],
        scratch_shapes=[pltpu.VMEM((8, LANES), F32)], compiler_params=_params())(x, g, wa, gq, gk, b_pad)


def _key_le_query(offset, keys=TK):
    r = lax.broadcasted_iota(jnp.int32, (keys, TQ), 0)
    c = lax.broadcasted_iota(jnp.int32, (keys, TQ), 1)
    return (r + offset) <= c


def _widen(tile):
    return jnp.concatenate([tile] * (TQ // LANES), axis=1)


def _fox_fwd(qn, kn, vb, proj, crow, cbc, rider=()):
    nq = S // TQ

    def body(q_ref, k_ref, v_ref, g_ref, cq_ref, cbc_ref, o_ref, z_ref, lse_ref, st_s, pt_s):
        i = pl.program_id(1)
        qs = [q_ref[:, HD * hh:HD * (hh + 1)] for hh in range(HPS)]
        cqs = [cq_ref[hh, 0] for hh in range(HPS)]

        def scores(s, hh):
            off = pl.multiple_of(s * KS, KS)
            kj = k_ref[pl.ds(off, KS), HD * hh:HD * (hh + 1)]
            return (_dot_nt(kj, qs[hh]) + cqs[hh]) - _widen(cbc_ref[pl.ds(off, KS), LANES * hh:LANES * (hh + 1)])

        def values(s, hh, pt):
            off = pl.multiple_of(s * KS, KS)
            return _dot_tn(v_ref[pl.ds(off, KS), HD * hh:HD * (hh + 1)], pt)

        def step(s, slot, carries, mask=None, last=False):
            if not last:
                for hh in range(HPS):
                    st_s[1 - slot, hh] = scores(s + 1, hh)
            pvs = [values(jnp.maximum(s - 1, 0), hh, pt_s[1 - slot, hh]) for hh in range(HPS)]
            out = []
            for hh in range(HPS):
                m, l, acc = carries[hh]
                st = st_s[slot, hh]
                if mask is not None:
                    st = jnp.where(mask, st, -jnp.inf)
                m_new = jnp.maximum(m, jnp.max(st, axis=0, keepdims=True))
                pt = jnp.exp(st - m_new)
                alpha = jnp.exp(m - m_new)
                pt_s[slot, hh] = pt.astype(BF16)
                out.append((m_new, alpha * l + jnp.sum(pt, axis=0, keepdims=True), alpha * (acc + pvs[hh])))
            return tuple(out)

        for hh in range(HPS):
            st_s[0, hh] = scores(0, hh)
            pt_s[1, hh] = jnp.zeros((KS, TQ), BF16)
        one = (jnp.full((1, TQ), -jnp.inf, F32), jnp.zeros((1, TQ), F32), jnp.zeros((HD, TQ), F32))
        carries = lax.fori_loop(0, i, lambda t, cr: step(2 * t + 1, 1, step(2 * t, 0, cr)), (one,) * HPS)
        carries = step(2 * i, 0, carries, mask=_key_le_query(0, KS))
        carries = step(2 * i + 1, 1, carries, mask=_key_le_query(KS, KS), last=True)
        accs = []
        for hh in range(HPS):
            m, l, acc = carries[hh]
            acc = acc + values(2 * i + 1, hh, pt_s[1, hh])
            accs.append(acc / l)
            lse_ref[hh, 0] = m + jnp.log(l)
        o = jnp.concatenate(accs, axis=0).T
        o_ref[...] = o
        g = g_ref[...]
        z_ref[...] = (o * (g * _sigmoid(g))).astype(BF16)

    qblk = pl.BlockSpec((TQ, HW), lambda hp, i: (i, hp))
    full = pl.BlockSpec((S, HW), lambda hp, i: (0, hp))
    rows = pl.BlockSpec((HPS, 1, 1, TQ), lambda hp, i: (hp, i, 0, 0))
    return _call(
        body, name="fox_fwd", args=(qn, kn, vb, proj, crow, cbc), grid=(N_HEADS // HPS, nq),
        in_specs=[qblk, full, full,
                  pl.BlockSpec((TQ, HW), lambda hp, i: (i, GOFF // HW + hp)),
                  rows, pl.BlockSpec((S, HPS * LANES), lambda hp, i: (0, hp))],
        out_specs=[qblk, qblk, rows],
        out_shape=[_sds((S, D), F32), _sds((S, D), BF16), _sds((N_HEADS, nq, 1, TQ), F32)],
        scratch_shapes=[pltpu.VMEM((2, HPS, KS, TQ), F32), pltpu.VMEM((2, HPS, KS, TQ), BF16)], rider=rider)


def _fox_bwd_pre(dh, w_out, proj, o):
    nq, per = S // TQ, RT // TQ

    def body(dh_ref, w_ref, g_ref, o_ref, do_ref, dg_ref, delta_ref):
        g = g_ref[...]
        sg = _sigmoid(g)
        dzv = _dot_nt(dh_ref[...].astype(BF16), w_ref[...])
        ov = o_ref[...]
        do = dzv * (g * sg)
        dg_ref[...] = (dzv * ov * (sg * (1.0 + g * (1.0 - sg)))).astype(BF16)
        do_ref[...] = do.astype(BF16)
        prod_t = (do * ov).T
        for h in range(N_HEADS):
            for b in range(per):
                delta_ref[h, b] = jnp.sum(prod_t[HD * h:HD * (h + 1), TQ * b:TQ * (b + 1)], axis=0, keepdims=True)

    row = pl.BlockSpec((RT, D), lambda i: (i, 0))
    return pl.pallas_call(
        body, name="fox_bwd_pre", grid=(S // RT,),
        in_specs=[row, pl.BlockSpec(w_out.shape, lambda i: (0, 0), pipeline_mode=pl.Buffered(1)),
                  pl.BlockSpec((RT, D), lambda i: (i, GOFF // D)), row],
        out_specs=[row, row, pl.BlockSpec((N_HEADS, per, 1, TQ), lambda i: (0, i, 0, 0))],
        out_shape=[_sds((S, D), BF16), _sds((S, D), BF16), _sds((N_HEADS, nq, 1, TQ), F32)],
        compiler_params=_params())(dh, w_out, proj, o)


def _fox_bwd(qn, kn, vb, dob, lse, delta, crow, cbc, rider=()):
    nq, nkb = S // TQ, S // TK

    def body(q_ref, k_ref, v_ref, do_ref, lse_ref, del_ref, cq_ref, cbc_ref,
             dk_ref, dv_ref, dcs_ref, dq_ref, dr_ref, st_s, dp_s, pt_s, ds_s, dq_acc, dr_acc):
        j = pl.program_id(1)

        @pl.when(j == 0)
        def _():
            dq_acc[...] = jnp.zeros_like(dq_acc)
            dr_acc[...] = jnp.zeros_like(dr_acc)

        kjs = [k_ref[:, HD * hh:HD * (hh + 1)] for hh in range(HPS)]
        vjs = [v_ref[:, HD * hh:HD * (hh + 1)] for hh in range(HPS)]

        def rows_of(ref, u, hh):
            off = pl.multiple_of(u * TQ, TQ)
            return ref[pl.ds(off, TQ), HD * hh:HD * (hh + 1)]

        def products(u, hh):
            st = (_dot_nt(kjs[hh], rows_of(q_ref, u, hh)) + cq_ref[hh, u]) - _widen(
                cbc_ref[:, LANES * hh:LANES * (hh + 1)])
            return st, _dot_nt(vjs[hh], rows_of(do_ref, u, hh))

        def step(u, slot, carries, masked=False):
            nxt = jnp.minimum(u + 1, nq - 1)
            for hh in range(HPS):
                st_s[1 - slot, hh], dp_s[1 - slot, hh] = products(nxt, hh)
            prev = jnp.maximum(u - 1, 0)
            dvs = [_dot_nn(pt_s[1 - slot, hh], rows_of(do_ref, prev, hh)) for hh in range(HPS)]
            dks = [_dot_nn(ds_s[1 - slot, hh], rows_of(q_ref, prev, hh)) for hh in range(HPS)]
            for hh in range(HPS):
                dq_acc[hh, prev] += _dot_tn(kjs[hh], ds_s[1 - slot, hh])
            out = []
            for hh in range(HPS):
                dk, dv, dcs = carries[hh]
                st = st_s[slot, hh]
                if masked:
                    st = jnp.where(_key_le_query((j - u) * TQ), st, -jnp.inf)
                pt = jnp.exp(st - lse_ref[hh, u])
                dst = pt * (dp_s[slot, hh] - del_ref[hh, u])
                pt_s[slot, hh] = pt.astype(BF16)
                ds_s[slot, hh] = dst.astype(BF16)
                dr_acc[hh, u] += jnp.sum(dst, axis=0, keepdims=True)
                out.append((dk + dks[hh], dv + dvs[hh], dcs + (dst[:, :LANES] + dst[:, LANES:])))
            return tuple(out)

        t0 = j // 2
        for hh in range(HPS):
            st_s[0, hh], dp_s[0, hh] = products(2 * t0, hh)
            pt_s[1, hh] = jnp.zeros((TK, TQ), BF16)
            ds_s[1, hh] = jnp.zeros((TK, TQ), BF16)
        one = (jnp.zeros((TK, HD), F32), jnp.zeros((TK, HD), F32), jnp.zeros((TK, LANES), F32))
        carries = step(2 * t0 + 1, 1, step(2 * t0, 0, (one,) * HPS, masked=True), masked=True)
        carries = lax.fori_loop(t0 + 1, nq // 2, lambda t, cr: step(2 * t + 1, 1, step(2 * t, 0, cr)), carries)
        dks, dvs = [], []
        lane = _lane_iota((TK, LANES))
        dcs_all = jnp.zeros((TK, LANES), F32)
        for hh in range(HPS):
            dk, dv, dcs = carries[hh]
            dks.append(dk + _dot_nn(ds_s[1, hh], rows_of(q_ref, nq - 1, hh)))
            dvs.append(dv + _dot_nn(pt_s[1, hh], rows_of(do_ref, nq - 1, hh)))
            dq_acc[hh, nq - 1] += _dot_tn(kjs[hh], ds_s[1, hh])
            dcs_all = jnp.where(lane == HPS * pl.program_id(0) + hh, -jnp.sum(dcs, axis=-1, keepdims=True), dcs_all)
        dcs_ref[0] = dcs_all
        dk_ref[...] = jnp.concatenate(dks, axis=-1)
        dv_ref[...] = jnp.concatenate(dvs, axis=-1).astype(BF16)

        @pl.when(j == nkb - 1)
        def _():
            for i in range(nq):
                dq_ref[TQ * i:TQ * (i + 1), :] = jnp.concatenate([dq_acc[hh, i] for hh in range(HPS)], axis=0).T
            dr_ref[...] = dr_acc[...]

    kblk = pl.BlockSpec((TK, HW), lambda hp, j: (j, hp))
    full = pl.BlockSpec((S, HW), lambda hp, j: (0, hp))
    rows = pl.BlockSpec((HPS, nq, 1, TQ), lambda hp, j: (hp, 0, 0, 0))
    cblk = pl.BlockSpec((TK, HPS * LANES), lambda hp, j: (j, hp))
    return _call(
        body, name="fox_bwd", args=(qn, kn, vb, dob, lse, delta, crow, cbc), grid=(N_HEADS // HPS, nkb),
        in_specs=[full, kblk, kblk, full, rows, rows, rows, cblk],
        out_specs=[kblk, kblk, pl.BlockSpec((1, TK, LANES), lambda hp, j: (hp, j, 0)), full, rows],
        out_shape=[_sds((S, D), F32), _sds((S, D), BF16), _sds((N_HEADS // HPS, S, LANES), F32), _sds((S, D), F32),
                   _sds((N_HEADS, nq, 1, TQ), F32)],
        scratch_shapes=[pltpu.VMEM((2, HPS, TK, TQ), F32), pltpu.VMEM((2, HPS, TK, TQ), F32),
                        pltpu.VMEM((2, HPS, TK, TQ), BF16), pltpu.VMEM((2, HPS, TK, TQ), BF16),
                        pltpu.VMEM((HPS, nq, HD, TQ), F32), pltpu.VMEM((HPS, nq, 1, TQ), F32)], rider=rider)


def _prep_a_bwd(dq, dk, dv, dgate, drow, dcs, proj, b_pad, gq, gk):
    nt = S // TM

    def body(dq_ref, dk_ref, dv_ref, dgt_ref, dr_ref, dcs_ref, xq_ref, xk_ref, f_ref, b_ref, gq_ref, gk_ref,
             o_ref, dgq_ref, dgk_ref, db_ref, carry):
        @pl.when(pl.program_id(0) == 0)
        def _():
            carry[...] = jnp.zeros_like(carry)
            dgq_ref[...] = jnp.zeros_like(dgq_ref)
            dgk_ref[...] = jnp.zeros_like(dgk_ref)
            db_ref[...] = jnp.zeros_like(db_ref)

        lane = _lane_iota((TM, LANES))
        lo_half = lane < HD
        gq2, gk2 = _g2(gq_ref), _g2(gk_ref)
        dgq, dgk = jnp.zeros((1, LANES), F32), jnp.zeros((1, LANES), F32)
        for c in _pairs(D):
            dxq, dg = _head_norm_bwd(dq_ref[:, c] * QSCALE, xq_ref[:, c], gq2, lo_half)
            o_ref[:, c] = dxq.astype(BF16)
            dgq = dgq + dg
            dxk, dg = _head_norm_bwd(dk_ref[:, c], xk_ref[:, c], gk2, lo_half)
            o_ref[:, D + c.start:D + c.stop] = dxk.astype(BF16)
            dgk = dgk + dg
        dgq_ref[...] += dgq
        dgk_ref[...] += dgk
        o_ref[:, 2 * D:3 * D] = dv_ref[...]
        o_ref[:, GOFF:GOFF + D] = dgt_ref[...]

        dc = dr_ref[...]
        for group in range(N_HEADS // HPS):
            dc = dc + dcs_ref[group]
        r = lax.broadcasted_iota(jnp.int32, (TM, TM), 0)
        c = lax.broadcasted_iota(jnp.int32, (TM, TM), 1)
        tri = (c >= r).astype(F32)
        dlogf = jnp.dot(tri, dc, precision=lax.Precision.HIGHEST, preferred_element_type=F32) + carry[0:1, :]
        carry[0:1, :] = dlogf[0:1, :]
        df = dlogf * (1.0 / (1.0 + jnp.exp(f_ref[...] + b_ref[...])))
        db_ref[...] += jnp.sum(df, axis=0, keepdims=True)
        o_ref[:, FOFF:FOFF + LANES] = df.astype(BF16)
        o_ref[:, FOFF + LANES:NA] = jnp.zeros((TM, NA - FOFF - LANES), BF16)

    rev = lambda width, col: pl.BlockSpec((TM, width), lambda i: (nt - 1 - i, col))
    gspec = pl.BlockSpec((1, HD), lambda i: (0, 0))
    acc = pl.BlockSpec((1, LANES), lambda i: (0, 0))
    return pl.pallas_call(
        body, name="prep_a_bwd", grid=(nt,),
        in_specs=[rev(D, 0), rev(D, 0), rev(D, 0), rev(D, 0), rev(LANES, 0),
                  pl.BlockSpec((N_HEADS // HPS, TM, LANES), lambda i: (0, nt - 1 - i, 0)),
                  rev(D, 0), rev(D, 1), rev(LANES, FOFF // LANES), acc, gspec, gspec],
        out_specs=[rev(NA, 0), acc, acc, acc],
        out_shape=[_sds((S, NA), BF16)] + [_sds((1, LANES), F32)] * 3,
        scratch_shapes=[pltpu.VMEM((8, LANES), F32)],
        compiler_params=_params())(dq, dk, dv, dgate, drow, dcs, proj, proj, proj, b_pad, gq, gk)


def _head_b(x, z_a, w_out_a, g_kv, g_b, w_kv, w_in_b, gq, gk, cos2, sin2):
    nkv = w_kv.shape[1] // 2

    def body(x_ref, z_ref, wo_ref, gkv_ref, gb_ref, wkv_ref, wb_ref, gq_ref, gk_ref, c_ref, s_ref,
             h_ref, ukv_ref, ub_ref, kv_ref, pb_ref, qo_ref, ko_ref, vo_ref):
        xv = x_ref[...] + _dot_nn(z_ref[...], wo_ref[...])
        h_ref[...] = xv
        xn = xv * _rms_rinv(xv)
        ukv = (xn * gkv_ref[...]).astype(BF16)
        ub = (xn * gb_ref[...]).astype(BF16)
        ukv_ref[...] = ukv
        ub_ref[...] = ub
        kv_ref[...] = _dot_nn(ukv, wkv_ref[...])
        for lo in range(0, 2 * D, D):
            pb_ref[:, lo:lo + D] = _dot_nn(ub, wb_ref[:, lo:lo + D])
        lane = _lane_iota((RT, LANES))
        lo_half = lane < HD
        cos, sin = c_ref[...], s_ref[...]
        gq2, gk2 = _g2(gq_ref), _g2(gk_ref)
        for c in _pairs(D):
            q = pb_ref[:, c]
            qo_ref[:, c] = (_rope_fwd((q * _head_rinv(q, lo_half)) * gq2, cos, sin, lane) * QSCALE).astype(BF16)
        for c in _pairs(nkv):
            k = kv_ref[:, c]
            ko_ref[:, c] = _rope_fwd((k * _head_rinv(k, lo_half)) * gk2, cos, sin, lane).astype(BF16)
        vo_ref[...] = kv_ref[:, nkv:2 * nkv].astype(BF16)

    row = lambda width: pl.BlockSpec((RT, width), lambda i: (i, 0))
    whole = lambda arr: pl.BlockSpec(arr.shape, lambda i: (0,) * arr.ndim, pipeline_mode=pl.Buffered(1))
    return pl.pallas_call(
        body, name="head_b", grid=(S // RT,),
        in_specs=[row(D), row(D), whole(w_out_a), whole(g_kv), whole(g_b), whole(w_kv), whole(w_in_b), whole(gq),
                  whole(gk), row(LANES), row(LANES)],
        out_specs=[row(D), row(D), row(D), row(2 * nkv), row(2 * D), row(D), row(nkv), row(nkv)],
        out_shape=[_sds((S, D), F32), _sds((S, D), BF16), _sds((S, D), BF16), _sds((S, 2 * nkv), F32),
                   _sds((S, 2 * D), F32), _sds((S, D), BF16), _sds((S, nkv), BF16), _sds((S, nkv), BF16)],
        compiler_params=_params())(x, z_a, w_out_a, g_kv, g_b, w_kv, w_in_b, gq, gk, cos2, sin2)


N_KV, GRP = 4, 4


def _swa_mask(n):
    r = lax.broadcasted_iota(jnp.int32, (2 * WIN, GRP * WIN), 0)
    q = lax.broadcasted_iota(jnp.int32, (2 * WIN, GRP * WIN), 1) & (WIN - 1)
    return (r > q) & (r <= q + WIN) & ((r >= WIN) | (n > 0))


def _stack4(ref_or_val, base):
    return jnp.concatenate([ref_or_val[:, base + HD * g: base + HD * (g + 1)] for g in range(GRP)], axis=0)


def _unstack4(xt):
    return jnp.concatenate([xt[:, WIN * g:WIN * (g + 1)] for g in range(GRP)], axis=0).T


def _band(prev_ref, cur_ref, kh):
    return jnp.concatenate([prev_ref[:, HD * kh:HD * (kh + 1)], cur_ref[:, HD * kh:HD * (kh + 1)]], axis=0)


def _sink_row(s_ref, first):
    lane = _lane_iota((1, GRP * WIN))
    row = jnp.full((1, GRP * WIN), s_ref[first + GRP - 1], F32)
    for g in range(GRP - 2, -1, -1):
        row = jnp.where(lane < WIN * (g + 1), s_ref[first + g], row)
    return row


def _swa_fwd(qb, ksh, vsh, pb, sinks):
    nb = S // WIN

    def body(q_ref, kp_ref, kc_ref, vp_ref, vc_ref, g_ref, s_ref, o_ref, z_ref, lse_ref):
        n = pl.program_id(0)
        valid = _swa_mask(n)
        outs = []
        for kh in range(N_KV):
            kb, vb = _band(kp_ref, kc_ref, kh), _band(vp_ref, vc_ref, kh)
            st = jnp.where(valid, _dot_nt(kb, _stack4(q_ref, GRP * HD * kh)), -jnp.inf)
            sink = _sink_row(s_ref, GRP * kh)
            m = jnp.maximum(jnp.max(st, axis=0, keepdims=True), sink)
            pt = jnp.exp(st - m)
            l = jnp.sum(pt, axis=0, keepdims=True) + jnp.exp(sink - m)
            outs.append(_unstack4(_dot_tn(vb, pt.astype(BF16)) / l))
            lse = m + jnp.log(l)
            for g in range(GRP):
                lse_ref[GRP * kh + g, 0] = lse[:, WIN * g:WIN * (g + 1)]
        o = jnp.concatenate(outs, axis=-1)
        o_ref[...] = o
        g = g_ref[...]
        z_ref[...] = (o * (g * _sigmoid(g))).astype(BF16)

    row = pl.BlockSpec((WIN, D), lambda n: (n, 0))
    prev = pl.BlockSpec((WIN, N_KV * HD), lambda n: (jnp.maximum(n - 1, 0), 0))
    cur = pl.BlockSpec((WIN, N_KV * HD), lambda n: (n, 0))
    return pl.pallas_call(
        body, name="swa_fwd", grid=(nb,),
        in_specs=[row, prev, cur, prev, cur, pl.BlockSpec((WIN, D), lambda n: (n, 1)),
                  pl.BlockSpec(memory_space=pltpu.SMEM)],
        out_specs=[row, row, pl.BlockSpec((N_HEADS, 1, 1, WIN), lambda n: (0, n, 0, 0))],
        out_shape=[_sds((S, D), F32), _sds((S, D), BF16), _sds((N_HEADS, nb, 1, WIN), F32)],
        compiler_params=_params())(qb, ksh, ksh, vsh, vsh, pb, sinks)


def _swa_bwd(qb, ksh, vsh, dz, o, lse, pb, sinks, gq, cos2, sin2):
    nb = S // WIN

    def body(q_ref, kp_ref, kc_ref, vp_ref, vc_ref, dz_ref, o_ref, lse_ref, x_ref, g_ref, s_ref, gq_ref, c_ref, sn_ref,
             dpb_ref, dka_ref, dkb_ref, dva_ref, dvb_ref, dsink_ref, dgq_ref):
        n = pl.program_id(0)

        @pl.when(n == 0)
        def _():
            dsink_ref[...] = jnp.zeros_like(dsink_ref)
            dgq_ref[...] = jnp.zeros_like(dgq_ref)

        valid = _swa_mask(n)
        g = g_ref[...]
        sg = _sigmoid(g)
        dzv = dz_ref[...]
        ov = o_ref[...]
        do = dzv * (g * sg)
        dpb_ref[:, D:2 * D] = (dzv * ov * (sg * (1.0 + g * (1.0 - sg)))).astype(BF16)
        prod_t = (do * ov).T
        lane1 = _lane_iota((1, LANES))
        dqs, dkas, dkbs, dvas, dvbs = [], [], [], [], []
        dsink = jnp.zeros((1, LANES), F32)
        for kh in range(N_KV):
            kb, vb = _band(kp_ref, kc_ref, kh), _band(vp_ref, vc_ref, kh)
            base = GRP * HD * kh
            qs = _stack4(q_ref, base)
            dos = _stack4(do, base).astype(BF16)
            delta = jnp.concatenate(
                [jnp.sum(prod_t[base + HD * gg:base + HD * (gg + 1), :], axis=0, keepdims=True)
                 for gg in range(GRP)], axis=1)
            lse = jnp.concatenate([lse_ref[GRP * kh + gg, 0] for gg in range(GRP)], axis=1)
            st = jnp.where(valid, _dot_nt(kb, qs), -jnp.inf)
            pt = jnp.exp(st - lse)
            dst = pt * (_dot_nt(vb, dos) - delta)
            dsb = dst.astype(BF16)
            dqs.append(_unstack4(_dot_tn(kb, dsb)))
            dkband = _dot_nn(dsb, qs)
            dvband = _dot_nn(pt.astype(BF16), dos)
            dkbs.append(dkband[0:WIN, :])
            dkas.append(dkband[WIN:2 * WIN, :])
            dvbs.append(dvband[0:WIN, :])
            dvas.append(dvband[WIN:2 * WIN, :])
            ps_delta = jnp.exp(_sink_row(s_ref, GRP * kh) - lse) * delta
            for gg in range(GRP):
                val = jnp.sum(ps_delta[:, WIN * gg:WIN * (gg + 1)], axis=1, keepdims=True)
                dsink = dsink - jnp.where(lane1 == GRP * kh + gg, val, 0.0)
        dka_ref[...] = jnp.concatenate(dkas, axis=-1)
        dkb_ref[...] = jnp.concatenate(dkbs, axis=-1)
        dva_ref[...] = jnp.concatenate(dvas, axis=-1)
        dvb_ref[...] = jnp.concatenate(dvbs, axis=-1)
        dsink_ref[...] += dsink

        lane = _lane_iota((WIN, LANES))
        g2, cos, sin = _g2(gq_ref), c_ref[...], sn_ref[...]
        dg_tot = jnp.zeros((1, LANES), F32)
        for kh in range(N_KV):
            for c in _pairs(GRP * HD):
                cols = slice(GRP * HD * kh + c.start, GRP * HD * kh + c.stop)
                dn = _rope_bwd(dqs[kh][:, c] * QSCALE, cos, sin, lane)
                dx, dg = _head_norm_bwd(dn, x_ref[:, cols], g2, lane < HD)
                dpb_ref[:, cols] = dx.astype(BF16)
                dg_tot = dg_tot + dg
        dgq_ref[...] += dg_tot

    row = pl.BlockSpec((WIN, D), lambda n: (n, 0))
    prev = pl.BlockSpec((WIN, N_KV * HD), lambda n: (jnp.maximum(n - 1, 0), 0))
    cur = pl.BlockSpec((WIN, N_KV * HD), lambda n: (n, 0))
    acc = pl.BlockSpec((1, LANES), lambda n: (0, 0))
    tab = pl.BlockSpec((WIN, LANES), lambda n: (n, 0))
    return pl.pallas_call(
        body, name="swa_bwd", grid=(nb,),
        in_specs=[row, prev, cur, prev, cur, row, row, pl.BlockSpec((N_HEADS, 1, 1, WIN), lambda n: (0, n, 0, 0)),
                  row, pl.BlockSpec((WIN, D), lambda n: (n, 1)), pl.BlockSpec(memory_space=pltpu.SMEM),
                  pl.BlockSpec((1, HD), lambda n: (0, 0)), tab, tab],
        out_specs=[pl.BlockSpec((WIN, 2 * D), lambda n: (n, 0)), cur, cur, cur, cur, acc, acc],
        out_shape=[_sds((S, 2 * D), BF16)] + [_sds((S, 256), F32)] * 4 + [_sds((1, LANES), F32)] * 2,
        compiler_params=_params())(qb, ksh, ksh, vsh, vsh, dz, o, lse, pb, pb, sinks, gq, cos2, sin2)


def _prep_kv_bwd(dka, dkb, dva, dvb, kv, gk, cos2, sin2):
    nt = S // RB
    per = RB // WIN

    def shifted(cur_ref, nxt_ref, has_next):
        return jnp.concatenate([cur_ref[WIN:RB, :], jnp.where(has_next, nxt_ref[...], 0.0)], axis=0)

    def body(dka_ref, dkb_ref, dkn_ref, dva_ref, dvb_ref, dvn_ref, x_ref, g_ref, c_ref, s_ref, o_ref, dgk_ref):
        i, j = pl.program_id(0), pl.program_id(1)

        @pl.when((i == 0) & (j == 0))
        def _():
            dgk_ref[...] = jnp.zeros_like(dgk_ref)

        has_next = i < nt - 1

        @pl.when(j == 0)
        def _():
            lane = _lane_iota((RB, LANES))
            g2, cos, sin = _g2(g_ref), c_ref[...], s_ref[...]
            dy_all = dka_ref[...] + shifted(dkb_ref, dkn_ref, has_next)
            dg_tot = jnp.zeros((1, LANES), F32)
            for c in _pairs(CB):
                dn = _rope_bwd(dy_all[:, c], cos, sin, lane)
                dx, dg = _head_norm_bwd(dn, x_ref[:, c], g2, lane < HD)
                o_ref[:, c] = dx.astype(BF16)
                dg_tot = dg_tot + dg
            dgk_ref[...] += dg_tot

        @pl.when(j == 1)
        def _():
            o_ref[...] = (dva_ref[...] + shifted(dvb_ref, dvn_ref, has_next)).astype(BF16)

    cur = pl.BlockSpec((RB, CB), lambda i, j: (i, 0))
    nxt = pl.BlockSpec((WIN, CB), lambda i, j: (jnp.minimum(per * (i + 1), S // WIN - 1), 0))
    tab = pl.BlockSpec((RB, LANES), lambda i, j: (i, 0))
    return pl.pallas_call(
        body, name="prep_kv_bwd", grid=(nt, 2),
        in_specs=[cur, cur, nxt, cur, cur, nxt, cur, pl.BlockSpec((1, HD), lambda i, j: (0, 0)), tab, tab],
        out_specs=[pl.BlockSpec((RB, CB), lambda i, j: (i, j)), pl.BlockSpec((1, LANES), lambda i, j: (0, 0))],
        out_shape=[_sds((S, 2 * CB), BF16), _sds((1, LANES), F32)],
        compiler_params=_params())(dka, dkb, dkb, dva, dvb, dvb, kv, gk, cos2, sin2)


def _out_b_loss(z, w_out, h1, tgt):
    tm = 2 * RT

    def body(z_ref, w_ref, h_ref, t_ref, dy_ref, l_ref):
        @pl.when(pl.program_id(0) == 0)
        def _():
            l_ref[...] = jnp.zeros_like(l_ref)

        e = (h_ref[...] + _dot_nn(z_ref[...], w_ref[...])) - t_ref[...]
        dy_ref[...] = e * (1.0 / D)
        l_ref[...] += jnp.sum(jnp.sum(e * e, axis=-1, keepdims=True), axis=0, keepdims=True)

    row = pl.BlockSpec((tm, D), lambda i: (i, 0))
    return pl.pallas_call(
        body, name="out_b_loss", grid=(S // tm,),
        in_specs=[row, pl.BlockSpec(w_out.shape, lambda i: (0, 0), pipeline_mode=pl.Buffered(1)), row, row],
        out_specs=[row, pl.BlockSpec((1, LANES), lambda i: (0, 0))],
        out_shape=[_sds((S, D), F32), _sds((1, LANES), F32)], compiler_params=_params())(z, w_out, h1, tgt)


def _du_a_rms_bwd(dproj, wa, x, g, dres, after):
    tm, tk = 2 * RT, NA // 2
    nk = NA // tk

    def body(a_ref, b_ref, x_ref, g_ref, dr_ref, after_ref, dx_ref, dg_ref, acc):
        i, kk = pl.program_id(0), pl.program_id(1)

        @pl.when((i == 0) & (kk == 0))
        def _():
            dg_ref[...] = jnp.zeros_like(dg_ref)

        p = _dot_nt(a_ref[...], b_ref[...])

        @pl.when(kk == 0)
        def _():
            acc[...] = p

        @pl.when(kk == nk - 1)
        def _():
            dx, dg = _rms_bwd_core(acc[...] + p, x_ref[...], g_ref[...])
            dx_ref[...] = dr_ref[...] + dx
            dg_ref[...] += dg

    assert nk == 2
    row = pl.BlockSpec((tm, D), lambda i, kk: (i, 0))
    vec = pl.BlockSpec((1, D), lambda i, kk: (0, 0))
    return pl.pallas_call(
        body, name="du_a_rms_bwd", grid=(S // tm, nk),
        in_specs=[pl.BlockSpec((tm, tk), lambda i, kk: (i, kk)), pl.BlockSpec((D, tk), lambda i, kk: (0, kk)),
                  row, vec, row, pl.BlockSpec(after.shape, lambda i, kk: (0, 0))],
        out_specs=[row, vec], out_shape=[_sds((S, D), F32), _sds((1, D), F32)],
        scratch_shapes=[pltpu.VMEM((tm, D), F32)], compiler_params=_params())(dproj, wa, x, g, dres, after)


def _du_b_rms_bwd(dpb, w_in_b, dkv, w_kv, h1, g_b, g_kv, dy):
    tm = 2 * RT

    def body(ab_ref, wb_ref, akv_ref, wkv_ref, x_ref, gb_ref, gkv_ref, dy_ref, dh_ref, dgb_ref, dgkv_ref):
        @pl.when(pl.program_id(0) == 0)
        def _():
            dgb_ref[...] = jnp.zeros_like(dgb_ref)
            dgkv_ref[...] = jnp.zeros_like(dgkv_ref)

        x = x_ref[...]
        dx1, dg1 = _rms_bwd_core(_dot_nt(ab_ref[...], wb_ref[...]), x, gb_ref[...])
        dx2, dg2 = _rms_bwd_core(_dot_nt(akv_ref[...], wkv_ref[...]), x, gkv_ref[...])
        dh_ref[...] = dy_ref[...] + dx1 + dx2
        dgb_ref[...] += dg1
        dgkv_ref[...] += dg2

    row = lambda width: pl.BlockSpec((tm, width), lambda i: (i, 0))
    whole = lambda arr: pl.BlockSpec(arr.shape, lambda i: (0, 0), pipeline_mode=pl.Buffered(1))
    vec = pl.BlockSpec((1, D), lambda i: (0, 0))
    return pl.pallas_call(
        body, name="du_b_rms_bwd", grid=(S // tm,),
        in_specs=[row(dpb.shape[1]), whole(w_in_b), row(dkv.shape[1]), whole(w_kv), row(D), vec, vec, row(D)],
        out_specs=[row(D), vec, vec], out_shape=[_sds((S, D), F32), _sds((1, D), F32), _sds((1, D), F32)],
        compiler_params=_params())(dpb, w_in_b, dkv, w_kv, h1, g_b, g_kv, dy)


def _gather_first(w_in_a, w_out_a, w_kv, w_in_b, w_out_b, norm_a_g):
    def body(wia_ref, woa_ref, wkv_ref, wib_ref, wob_ref, ga_ref,
             wa_g, ga_g, woa_s, wkv_s, wib_s, wob_s, wa_s, st_a, st_oa, st_kv, st_ib, st_ob, load_sems, *sems):
        sources = [wia_ref.at[0], woa_ref.at[0], wkv_ref, wib_ref.at[0], wob_ref.at[0]]
        stages = [st_a, st_oa, st_kv, st_ib, st_ob]
        loads = [pltpu.make_async_copy(src, dst, load_sems.at[i]) for i, (src, dst) in enumerate(zip(sources, stages))]
        for cp in loads:
            cp.start()
        loads[0].wait()
        wa_s[...] = st_a[...].astype(BF16)

        def cast_the_rest():
            for cp, stage, out in zip(loads[1:], stages[1:], [woa_s, wkv_s, wib_s, wob_s]):
                cp.wait()
                out[...] = stage[...].astype(BF16)

        _gather_two_level([wa_s, ga_ref], [wa_g, ga_g], sems, meanwhile=cast_the_rest)

    vmem = pl.BlockSpec(memory_space=pltpu.VMEM)
    anyspec = pl.BlockSpec(memory_space=pl.ANY)
    shard = lambda w: _sds(w.shape[-2:], BF16)
    stage = lambda w: pltpu.VMEM(w.shape[-2:], F32)
    return pl.pallas_call(
        body, name="gather_first", in_specs=[anyspec] * 5 + [vmem],
        out_specs=[anyspec, anyspec, vmem, vmem, vmem, vmem],
        out_shape=[_sds((N_DEV,) + w_in_a.shape[-2:], BF16), _sds((N_DEV,) + norm_a_g.shape, F32),
                   shard(w_out_a), shard(w_kv), shard(w_in_b), shard(w_out_b)],
        scratch_shapes=[pltpu.VMEM(w_in_a.shape[-2:], BF16), stage(w_in_a), stage(w_out_a), stage(w_kv),
                        stage(w_in_b), stage(w_out_b), pltpu.SemaphoreType.DMA((5,))] + _exchange_sems(2),
        compiler_params=pltpu.CompilerParams(vmem_limit_bytes=VMEM_LIMIT, has_side_effects=True))(
            w_in_a, w_out_a, w_kv, w_in_b, w_out_b, norm_a_g)


def _pair_reduce(slots):
    n_chip = N_DEV // 2
    _, rows, cols = slots.shape

    def body(s_ref, o_ref, own_v, sib_v, send_sems, recv_sems, local_sems):
        x, y, c = lax.axis_index("x"), lax.axis_index("y"), lax.axis_index("c")
        copies = []
        for j in range(n_chip):
            own = pltpu.make_async_copy(s_ref.at[2 * j + c], own_v.at[j], local_sems.at[j])
            give = pltpu.make_async_remote_copy(
                src_ref=s_ref.at[2 * j + 1 - c], dst_ref=sib_v.at[j], send_sem=send_sems.at[j],
                recv_sem=recv_sems.at[j], device_id=(x, y, 1 - c), device_id_type=pl.DeviceIdType.MESH)
            own.start()
            give.start()
            copies.append((own, give))
        for j, (own, give) in enumerate(copies):
            own.wait()
            give.wait()
            o_ref[j] = (own_v[j].astype(F32) + sib_v[j].astype(F32)).astype(BF16)

    half = _sds((n_chip, rows, cols), slots.dtype)
    return pl.pallas_call(
        body, name="pair_reduce", in_specs=[pl.BlockSpec(memory_space=pl.ANY)],
        out_specs=pl.BlockSpec(memory_space=pltpu.VMEM), out_shape=half,
        scratch_shapes=[pltpu.VMEM(half.shape, half.dtype), pltpu.VMEM(half.shape, half.dtype),
                        pltpu.SemaphoreType.DMA((n_chip,)), pltpu.SemaphoreType.DMA((n_chip,)),
                        pltpu.SemaphoreType.DMA((n_chip,))],
        compiler_params=pltpu.CompilerParams(vmem_limit_bytes=VMEM_LIMIT, has_side_effects=True))(slots)


def _padded_col(c):
    if c < RAW_F:
        return c
    return FOFF + (c - RAW_F) if c < RAW_G else GOFF + (c - RAW_G)


def _shard_pieces():
    width = NA_RAW // N_DEV
    pieces = []
    for d in range(N_DEV):
        cuts = [width * d] + [c for c in (RAW_F, RAW_G) if width * d < c < width * (d + 1)] + [width * (d + 1)]
        for lo, hi in zip(cuts[:-1], cuts[1:]):
            pieces.append((d, lo - width * d, _padded_col(lo), hi - lo))
    return pieces


def _unshard_wa(wa_g):
    def body(w_ref, o_ref):
        o_ref[:, FOFF + N_HEADS:NA] = jnp.zeros((TM, NA - FOFF - N_HEADS), BF16)
        for d, src, dst, width in _shard_pieces():
            o_ref[:, dst:dst + width] = w_ref[d, :, src:src + width]

    return pl.pallas_call(
        body, name="unshard_wa", grid=(D // TM,),
        in_specs=[pl.BlockSpec((N_DEV, TM, NA_RAW // N_DEV), lambda i: (0, i, 0))],
        out_specs=pl.BlockSpec((TM, NA), lambda i: (i, 0)), out_shape=_sds((D, NA), BF16),
        compiler_params=_params())(wa_g)


def _reshard_dwa(dwa):
    def body(g_ref, o_ref):
        for d, src, dst, width in _shard_pieces():
            o_ref[d, :, src:src + width] = g_ref[:, dst:dst + width]

    return pl.pallas_call(
        body, name="reshard_dwa", grid=(D // TM,), in_specs=[pl.BlockSpec((TM, NA), lambda i: (i, 0))],
        out_specs=pl.BlockSpec((N_DEV, TM, NA_RAW // N_DEV), lambda i: (0, i, 0)),
        out_shape=_sds((N_DEV, D, NA_RAW // N_DEV), dwa.dtype), compiler_params=_params())(dwa)


CHIP_FLIPS = (2, 4, 6)


def _chip_exchange_start(partial):
    n = len(CHIP_FLIPS)

    def body(p_ref, land_ref, *rest):
        sends, recvs, token = rest[:n], rest[n:2 * n], rest[2 * n + 2]
        x, y, c = lax.axis_index("x"), lax.axis_index("y"), lax.axis_index("c")
        me = 4 * x + 2 * y + c
        for idx, k in enumerate(CHIP_FLIPS):
            pltpu.make_async_remote_copy(
                src_ref=p_ref.at[(me ^ k) >> 1], dst_ref=land_ref.at[me >> 1], send_sem=sends[idx],
                recv_sem=recvs[idx], device_id=(x ^ ((k >> 2) & 1), y ^ ((k >> 1) & 1), c),
                device_id_type=pl.DeviceIdType.MESH).start()
        token[...] = jnp.zeros_like(token)

    hbm = pl.BlockSpec(memory_space=pltpu.HBM)
    sem = pl.BlockSpec(memory_space=pltpu.SEMAPHORE)
    buf = pltpu.HBM(partial.shape, partial.dtype)
    return pl.pallas_call(
        body, name="chip_exchange_start",
        out_shape=(pltpu.SemaphoreType.DMA(()),) * (2 * n) + (buf, buf, _sds((8, LANES), F32)),
        in_specs=(hbm, hbm), out_specs=(sem,) * (2 * n) + (hbm, hbm, pl.BlockSpec(memory_space=pltpu.VMEM)),
        input_output_aliases={0: 2 * n, 1: 2 * n + 1},
        compiler_params=pltpu.CompilerParams(has_side_effects=pltpu.SideEffectType.DATAFLOW_SIDE_EFFECTING))(
            pltpu.with_memory_space_constraint(partial, pltpu.HBM),
            pltpu.with_memory_space_constraint(lax.empty(partial.shape, partial.dtype), pltpu.HBM))


def _chip_exchange_wait(started, after):
    n = len(CHIP_FLIPS)
    sems, (p_thru, land_thru) = started[:2 * n], started[2 * n:2 * n + 2]

    def body(p_ref, land_ref, *rest):
        sends, recvs = rest[:n], rest[n:2 * n]
        x, y, c = lax.axis_index("x"), lax.axis_index("y"), lax.axis_index("c")
        me = 4 * x + 2 * y + c
        for idx, k in enumerate(CHIP_FLIPS):
            copy = pltpu.make_async_remote_copy(
                src_ref=p_ref.at[(me ^ k) >> 1], dst_ref=land_ref.at[(me ^ k) >> 1], send_sem=sends[idx],
                recv_sem=recvs[idx], device_id=(x ^ ((k >> 2) & 1), y ^ ((k >> 1) & 1), c),
                device_id_type=pl.DeviceIdType.MESH)
            copy.wait_send()
            copy.wait_recv()

    hbm = pl.BlockSpec(memory_space=pltpu.HBM)
    sem = pl.BlockSpec(memory_space=pltpu.SEMAPHORE)
    buf = pltpu.HBM(p_thru.shape, p_thru.dtype)
    return pl.pallas_call(
        body, name="chip_exchange_wait", out_shape=(buf, buf),
        in_specs=(hbm, hbm) + (sem,) * (2 * n) + (pl.BlockSpec(memory_space=pl.ANY),), out_specs=(hbm, hbm),
        input_output_aliases={0: 0, 1: 1},
        compiler_params=pltpu.CompilerParams(has_side_effects=pltpu.SideEffectType.DATAFLOW_SIDE_EFFECTING))(
            p_thru, land_thru, *sems, after)


def _gather_slab(slab, after):
    def body(s_ref, after_ref, o_ref, *sems):
        _exchange_ops(["gather_rows"], [s_ref], [o_ref], sems, True, True)

    anyspec = pl.BlockSpec(memory_space=pl.ANY)
    return pl.pallas_call(
        body, name="gather_slab", in_specs=[anyspec, anyspec], out_specs=anyspec,
        out_shape=_sds((N_DEV,) + slab.shape, slab.dtype), scratch_shapes=_exchange_sems(1),
        compiler_params=pltpu.CompilerParams(has_side_effects=True))(slab, after)


def _adamw(w, g, m, v):
    m = ADAM_B1 * m + (1.0 - ADAM_B1) * g
    v = ADAM_B2 * v + (1.0 - ADAM_B2) * (g * g)
    m_hat = m / (1.0 - ADAM_B1 ** ADAM_STEP)
    v_hat = v / (1.0 - ADAM_B2 ** ADAM_STEP)
    delta = -ADAM_LR * (m_hat / (jnp.sqrt(v_hat) + ADAM_EPS) + ADAM_WD * w)
    return delta, m, v


def _sum_adamw(recv, w, m, v, name, after=None):
    lead = w.ndim - 2
    rows, cols = w.shape[-2:]
    tr = 256 if rows % 256 == 0 else 128
    slabs = list(recv) if isinstance(recv, tuple) else [recv]
    n_slots = slabs[0].shape[0]
    extra = [] if after is None else [after]

    def body(*refs):
        r_ref, own_ref = refs[0], refs[len(slabs) - 1]
        w_ref, m_ref, v_ref, g_ref, d_ref, nm_ref, nv_ref = refs[len(slabs) + len(extra):]
        chip = (4 * lax.axis_index("x") + 2 * lax.axis_index("y") + lax.axis_index("c")) >> 1
        g = None
        for slot in range(n_slots):
            part = r_ref[slot]
            if len(slabs) == 2:
                part = jnp.where(chip == slot, own_ref[slot], part)
            g = part.astype(F32) if g is None else g + part.astype(F32)
        g_ref[...] = g
        d_ref[...], nm_ref[...], nv_ref[...] = _adamw(w_ref[...], g, m_ref[...], v_ref[...])

    blk = pl.BlockSpec((None,) * lead + (tr, cols), lambda i: (0,) * lead + (i, 0))
    slots = pl.BlockSpec((n_slots, tr, cols), lambda i: (0, i, 0))
    return pl.pallas_call(
        body, name=name, grid=(rows // tr,),
        in_specs=[slots] * len(slabs) + [pl.BlockSpec(a.shape, lambda i: (0, 0)) for a in extra] + [blk, blk, blk],
        out_specs=[blk] * 4, out_shape=[_sds(w.shape, F32)] * 4, compiler_params=_params())(*slabs, *extra, w, m, v)


SLAB_ROWS = 16
SLOT = {"kv_norm_g": (8, 0, D), "norm_b_g": (9, 0, D), "b_forget": (10, 0, 16), "qnorm_a_g": (10, 128, HD),
        "knorm_a_g": (10, 256, HD), "knorm_b_g": (10, 384, HD), "qnorm_b_g": (10, 512, HD), "sinks": (10, 640, 16)}
SMALL = ["norm_a_g", "b_forget", "qnorm_a_g", "knorm_a_g", "kv_norm_g", "knorm_b_g", "norm_b_g", "qnorm_b_g", "sinks"]


LOSS_ROW = 11


def _pack_small(dg_a, dg_kv, dg_b, db_f, dgq_a, dgk_a, dgk_b, dgq_b, dsinks, lsum):
    def fold(ref):
        return ref[:, 0:HD] + ref[:, HD:2 * HD]

    def body(dga_ref, dgkv_ref, dgb_ref, dbf_ref, dgqa_ref, dgka_ref, dgkb_ref, dgqb_ref, dsk_ref, ls_ref, slab_ref):
        slab_ref[...] = jnp.zeros_like(slab_ref)
        for r in range(N_DEV):
            slab_ref[r:r + 1, 0:LANES] = dga_ref[:, LANES * r:LANES * (r + 1)]
        slab_ref[8:9, :] = dgkv_ref[...]
        slab_ref[9:10, :] = dgb_ref[...]
        slab_ref[10:11, 0:LANES] = dbf_ref[...]
        slab_ref[10:11, 128:128 + HD] = fold(dgqa_ref)
        slab_ref[10:11, 256:256 + HD] = fold(dgka_ref)
        slab_ref[10:11, 384:384 + HD] = fold(dgkb_ref)
        slab_ref[10:11, 512:512 + HD] = fold(dgqb_ref)
        slab_ref[10:11, 640:640 + LANES] = dsk_ref[...]
        slab_ref[LOSS_ROW:LOSS_ROW + 1, 0:LANES] = ls_ref[...]

    return pl.pallas_call(body, name="pack_small", out_shape=_sds((SLAB_ROWS, D), F32), compiler_params=_params())(
        dg_a, dg_kv, dg_b, db_f, dgq_a, dgk_a, dgk_b, dgq_b, dsinks, lsum)


def _small_adamw(recv, ws, ms, vs):
    k = len(SMALL)

    def body(*refs):
        r_ref = refs[0]
        w_refs, m_refs, v_refs = refs[1:1 + k], refs[1 + k:1 + 2 * k], refs[1 + 2 * k:1 + 3 * k]
        outs = refs[1 + 3 * k:1 + 7 * k]
        loss_ref, tot = refs[1 + 7 * k], refs[2 + 7 * k]
        g = r_ref[0]
        for dev in range(1, N_DEV):
            g = g + r_ref[dev]
        tot[...] = g
        loss_ref[...] = tot[LOSS_ROW:LOSS_ROW + 1, 0:LANES] * (0.5 / D)
        me = 4 * lax.axis_index("x") + 2 * lax.axis_index("y") + lax.axis_index("c")
        for p, name in enumerate(SMALL):
            if name == "norm_a_g":
                mine = lax.broadcasted_iota(jnp.int32, (N_DEV, LANES), 0) == me
                gp = jnp.sum(jnp.where(mine, tot[0:N_DEV, 0:LANES], 0.0), axis=0, keepdims=True)
            else:
                row, lo, width = SLOT[name]
                gp = tot[row:row + 1, lo:lo + width]
            d, nm, nv = _adamw(w_refs[p][...], gp, m_refs[p][...], v_refs[p][...])
            outs[p][...] = gp
            outs[k + p][...] = d
            outs[2 * k + p][...] = nm
            outs[3 * k + p][...] = nv

    shapes = [_sds(w.shape, F32) for w in ws]
    return pl.pallas_call(body, name="small_adamw", out_shape=shapes * 4 + [_sds((1, LANES), F32)],
                          scratch_shapes=[pltpu.VMEM((SLAB_ROWS, D), F32)],
                          compiler_params=_params())(recv, *ws, *ms, *vs)


def _rope_tables(positions):
    inv_freq = jnp.power(jnp.float32(ROPE_THETA), -jnp.arange(0, ROT, 2, dtype=F32) / ROT)
    ang = positions.astype(F32)[:, None] * inv_freq[None, :]
    cos, sin = jnp.cos(ang), jnp.sin(ang)
    c64 = jnp.concatenate([cos, cos, jnp.ones((S, HD - ROT), F32)], axis=-1)
    s64 = jnp.concatenate([-sin, sin, jnp.zeros((S, HD - ROT), F32)], axis=-1)
    return jnp.tile(c64, (1, 2)), jnp.tile(s64, (1, 2))


def _local_step(x, tgt, positions, g_a, wa, b_forget, gq_a, gk_a, g_kv, gk_b, g_b, gq_b, sinks,
                woa_s, wkv_s, wib_s, wob_s, adamw_others):
    nq = S // TQ
    cos2, sin2 = _rope_tables(positions)
    b_pad = jnp.pad(b_forget, ((0, 0), (0, LANES - N_HEADS)))

    u_a, proj, qn, kn, vb, ccol, cbc = _head_a(x, g_a, wa, gq_a, gk_a, b_pad)
    crow = ccol[:, :N_HEADS].T.reshape(N_HEADS, nq, 1, TQ)
    o_a, z_a, lse_a, woa_g, wkv_g, w_in_b, wob_g = _fox_fwd(
        qn, kn, vb, proj, crow, cbc,
        rider=[("gather_rows", woa_s), ("gather_rows", wkv_s), ("gather_cols", wib_s), ("gather_rows", wob_s)])
    w_out_a, w_kv, w_out_b = woa_g.reshape(D, D), wkv_g.reshape(D, 512), wob_g.reshape(D, D)
    h1, u_kv, u_b, kv, pb, qb, ksh, vsh = _head_b(x, z_a, w_out_a, g_kv, g_b, w_kv, w_in_b, gq_b, gk_b, cos2, sin2)
    sinks1 = sinks.reshape(N_HEADS)
    o_b, z_b, lse_b = _swa_fwd(qb, ksh, vsh, pb, sinks1)
    dy, lsum = _out_b_loss(z_b, w_out_b, h1, tgt)
    dw_out_b = _mm(z_b, dy, "tn", 512, 512, S, out_dtype=BF16, name="mm_dw_out_b")
    dz_b = _mm(dy, w_out_b, "nt", 1024, 512, D, name="mm_dz_b")
    dpb, dka, dkb, dva, dvb, dsinks, dgq_b = _swa_bwd(qb, ksh, vsh, dz_b, o_b, lse_b, pb, sinks1, gq_b, cos2, sin2)
    dkv, dgk_b = _prep_kv_bwd(dka, dkb, dva, dvb, kv, gk_b, cos2, sin2)
    dw_in_b = _mm(u_b, dpb, "tn", 512, 512, S, out_dtype=BF16, name="mm_dw_in_b")
    dw_kv = _mm(u_kv, dkv, "tn", 512, 512, S, out_dtype=BF16, name="mm_dw_kv")
    dh1, dg_b, dg_kv = _du_b_rms_bwd(dpb, w_in_b, dkv, w_kv, h1, g_b, g_kv, dy)
    dw_out_a = _mm(z_a, dh1, "tn", 512, 512, S, out_dtype=BF16, name="mm_dw_out_a")
    do_a, dgate_a, delta_a = _fox_bwd_pre(dh1, w_out_a, proj, o_a)
    dk_a, dv_a, dcs, dq_a, drow, r_wob, r_wib, r_wkv, r_woa = _fox_bwd(
        qn, kn, vb, do_a, lse_a, delta_a, crow, cbc,
        rider=[("a2a_rows", dw_out_b), ("a2a_cols", dw_in_b), ("a2a_rows", dw_kv), ("a2a_rows", dw_out_a)])
    drow_col = jnp.pad(drow.reshape(N_HEADS, S).T, ((0, 0), (0, LANES - N_HEADS)))
    dproj, dgq_a, dgk_a, db_f = _prep_a_bwd(dq_a, dk_a, dv_a, dgate_a, drow_col, dcs, proj, b_pad, gq_a, gk_a)
    dwa = _mm(u_a, dproj, "tn", 1024, 256, S, out_dtype=BF16, name="mm_dw_in_a")
    partial = _pair_reduce(_reshard_dwa(dwa))
    started = _chip_exchange_start(partial)
    dx, dg_a = _du_a_rms_bwd(dproj, wa, x, g_a, dh1, after=started[-1])
    others = adamw_others(dict(w_out_a=r_woa, w_kv=r_wkv, w_in_b=r_wib, w_out_b=r_wob), dg_a)
    partial, landed = _chip_exchange_wait(started, others["w_out_b"][0])
    slab = _pack_small(dg_a, dg_kv, dg_b, db_f, dgq_a, dgk_a, dgk_b, dgq_b, dsinks, lsum)
    return dx, (landed, partial), others, _gather_slab(slab, landed)


def kernel(x, positions, norm_a_g, w_in_a, b_forget, qnorm_a_g, knorm_a_g, w_out_a, kv_norm_g, w_kv, knorm_b_g, norm_b_g, w_in_b, qnorm_b_g, sinks, w_out_b, loss_target, m_norm_a_g, m_w_in_a, m_b_forget, m_qnorm_a_g, m_knorm_a_g, m_w_out_a, m_kv_norm_g, m_w_kv, m_knorm_b_g, m_norm_b_g, m_w_in_b, m_qnorm_b_g, m_sinks, m_w_out_b, v_norm_a_g, v_w_in_a, v_b_forget, v_qnorm_a_g, v_knorm_a_g, v_w_out_a, v_kv_norm_g, v_w_kv, v_knorm_b_g, v_norm_b_g, v_w_in_b, v_qnorm_b_g, v_sinks, v_w_out_b):
    wa_g, ga_g, woa_s, wkv_s, wib_s, wob_s = _gather_first(w_in_a, w_out_a, w_kv, w_in_b, w_out_b, norm_a_g)
    state = dict(w_in_a=(w_in_a, m_w_in_a, v_w_in_a), w_out_a=(w_out_a, m_w_out_a, v_w_out_a),
                 w_kv=(w_kv, m_w_kv, v_w_kv), w_in_b=(w_in_b, m_w_in_b, v_w_in_b),
                 w_out_b=(w_out_b, m_w_out_b, v_w_out_b))

    def adamw_others(landed, after):
        return {n: _sum_adamw(r, *state[n], "adamw_" + n, after=after) for n, r in landed.items()}

    dx, r_wa, big, slab_g = _local_step(
        x[0], loss_target[0], positions, ga_g.reshape(1, D), _unshard_wa(wa_g), b_forget, qnorm_a_g, knorm_a_g,
        kv_norm_g.reshape(1, D), knorm_b_g.reshape(1, HD), norm_b_g, qnorm_b_g, sinks, woa_s, wkv_s, wib_s, wob_s,
        adamw_others)
    big["w_in_a"] = _sum_adamw(r_wa, *state["w_in_a"], "adamw_w_in_a")

    r2 = lambda a: a.reshape(1, -1)
    small_w = dict(norm_a_g=norm_a_g, b_forget=b_forget, qnorm_a_g=qnorm_a_g, knorm_a_g=knorm_a_g,
                   kv_norm_g=kv_norm_g, knorm_b_g=knorm_b_g, norm_b_g=norm_b_g, qnorm_b_g=qnorm_b_g, sinks=sinks)
    small_m = dict(norm_a_g=m_norm_a_g, b_forget=m_b_forget, qnorm_a_g=m_qnorm_a_g, knorm_a_g=m_knorm_a_g,
                   kv_norm_g=m_kv_norm_g, knorm_b_g=m_knorm_b_g, norm_b_g=m_norm_b_g, qnorm_b_g=m_qnorm_b_g,
                   sinks=m_sinks)
    small_v = dict(norm_a_g=v_norm_a_g, b_forget=v_b_forget, qnorm_a_g=v_qnorm_a_g, knorm_a_g=v_knorm_a_g,
                   kv_norm_g=v_kv_norm_g, knorm_b_g=v_knorm_b_g, norm_b_g=v_norm_b_g, qnorm_b_g=v_qnorm_b_g,
                   sinks=v_sinks)
    res = _small_adamw(slab_g, [r2(small_w[n]) for n in SMALL], [r2(small_m[n]) for n in SMALL],
                       [r2(small_v[n]) for n in SMALL])
    k = len(SMALL)
    small = {n: [res[q * k + p].reshape(small_w[n].shape) for q in range(4)] for p, n in enumerate(SMALL)}
    loss = res[4 * k][0, 0]

    order = ["norm_a_g", "w_in_a", "b_forget", "qnorm_a_g", "knorm_a_g", "w_out_a", "kv_norm_g", "w_kv",
             "knorm_b_g", "norm_b_g", "w_in_b", "qnorm_b_g", "sinks", "w_out_b"]

    def leaf(n, q):
        return big[n][q] if n in big else small[n][q]

    outs = [loss, dx[None]]
    for q in range(4):
        outs.extend(leaf(n, q) for n in order)
    return tuple(outs)
```

```python
import jax
import jax.numpy as jnp
from jax import lax
from jax.experimental import pallas as pl
from jax.experimental.pallas import tpu as pltpu

F32, BF16 = jnp.float32, jnp.bfloat16

S = 2048
D = 1024
HD = 64
N_HEADS = 16
N_DEV = 8
NA = 4352
GOFF = 3072
FOFF = 4096
RAW_F = 3072
RAW_G = RAW_F + N_HEADS
NA_RAW = 4112
EPS = 1e-6
QSCALE = 0.125
ROPE_THETA = 500000.0
ROT = 16
WIN = 128
TQ = 256
TK = 256
KS = TQ // 2
HPS = 8
HW = HPS * HD
TM = 256
RT = 128
RB = 512
CB = 256
LANES = 128

ADAM_LR, ADAM_B1, ADAM_B2, ADAM_EPS, ADAM_WD, ADAM_STEP = 0.001, 0.9, 0.999, 1e-08, 0.01, 10

VMEM_LIMIT = 56 * 1024 * 1024


def _params():
    return pltpu.CompilerParams(vmem_limit_bytes=VMEM_LIMIT)


def _sds(shape, dtype):
    return jax.ShapeDtypeStruct(shape, dtype)


def _dot_nt(a, b):
    return lax.dot_general(a, b, (((1,), (1,)), ((), ())), preferred_element_type=F32)


def _dot_tn(a, b):
    return lax.dot_general(a, b, (((0,), (0,)), ((), ())), preferred_element_type=F32)


def _dot_nn(a, b):
    return lax.dot_general(a, b, (((1,), (0,)), ((), ())), preferred_element_type=F32)


def _sigmoid(g):
    return 1.0 / (1.0 + jnp.exp(-g))


def _lane_iota(shape):
    return lax.broadcasted_iota(jnp.int32, shape, len(shape) - 1)


def _flips(kind):
    return (2, 4, 6) if kind == "a2a_chips" else tuple(range(1, N_DEV))


def _send_view(kind, ref, dev):
    if kind in ("gather_rows", "gather_cols"):
        return ref
    if kind == "a2a_slots":
        return ref.at[dev]
    if kind == "a2a_chips":
        return ref.at[dev >> 1]
    if kind == "a2a_rows":
        rows = ref.shape[0] // N_DEV
        return ref.at[pl.ds(pl.multiple_of(dev * rows, rows), rows)]
    cols = ref.shape[1] // N_DEV
    return ref.at[:, pl.ds(pl.multiple_of(dev * cols, cols), cols)]


def _land_view(kind, ref, dev):
    if kind == "gather_cols":
        cols = ref.shape[1] // N_DEV
        return ref.at[:, pl.ds(pl.multiple_of(dev * cols, cols), cols)]
    if kind == "a2a_chips":
        return ref.at[dev >> 1]
    return ref.at[dev]


def _landing_sds(kind, arr):
    if kind == "gather_rows":
        return _sds((N_DEV,) + arr.shape, arr.dtype)
    if kind == "gather_cols":
        return _sds((arr.shape[0], N_DEV * arr.shape[1]), arr.dtype)
    if kind == "a2a_rows":
        return _sds((N_DEV, arr.shape[0] // N_DEV, arr.shape[1]), arr.dtype)
    if kind == "a2a_cols":
        return _sds((N_DEV, arr.shape[0], arr.shape[1] // N_DEV), arr.dtype)
    return _sds(arr.shape, arr.dtype)


def _exchange_sems(n_parts):
    n = n_parts * (N_DEV - 1)
    return [pltpu.SemaphoreType.DMA((n,)), pltpu.SemaphoreType.DMA((n,)), pltpu.SemaphoreType.DMA((n_parts,))]


def _exchange_ops(kinds, srcs, dsts, sems, start, wait):
    send_sems, recv_sems, local_sems = sems
    x, y, c = lax.axis_index("x"), lax.axis_index("y"), lax.axis_index("c")
    me = 4 * x + 2 * y + c

    def local(a):
        return pltpu.make_async_copy(_send_view(kinds[a], srcs[a], me), _land_view(kinds[a], dsts[a], me),
                                     local_sems.at[a])

    def remote(a, k, landing_dev):
        peer = (x ^ ((k >> 2) & 1), y ^ ((k >> 1) & 1), c ^ (k & 1))
        sem = a * (N_DEV - 1) + k - 1
        return pltpu.make_async_remote_copy(
            src_ref=_send_view(kinds[a], srcs[a], me ^ k), dst_ref=_land_view(kinds[a], dsts[a], landing_dev),
            send_sem=send_sems.at[sem], recv_sem=recv_sems.at[sem], device_id=peer,
            device_id_type=pl.DeviceIdType.MESH)

    pairs = [(a, k) for k in range(1, N_DEV) for a in range(len(kinds)) if k in _flips(kinds[a])]
    if start:
        for a in range(len(kinds)):
            local(a).start()
        for a, k in pairs:
            remote(a, k, me).start()
    if wait:
        for a, k in pairs:
            remote(a, k, me ^ k).wait_recv()
            remote(a, k, me).wait_send()
        for a in range(len(kinds)):
            local(a).wait()


def _gather_two_level(srcs, dsts, sems, meanwhile=None):
    send_sems, recv_sems, local_sems = sems
    x, y, c = lax.axis_index("x"), lax.axis_index("y"), lax.axis_index("c")
    me, sibling = (x, y, c), (x, y, 1 - c)
    chips = [(1 - x, y), (x, 1 - y), (1 - x, 1 - y)]

    def slot(ref, dev):
        return ref.at[4 * dev[0] + 2 * dev[1] + dev[2]]

    def copy(a, k, block, to, src=None):
        return pltpu.make_async_remote_copy(
            src_ref=slot(dsts[a], block) if src is None else src, dst_ref=slot(dsts[a], block),
            send_sem=send_sems.at[a * (N_DEV - 1) + k], recv_sem=recv_sems.at[a * (N_DEV - 1) + k],
            device_id=to, device_id_type=pl.DeviceIdType.MESH)

    parts = range(len(srcs))
    mine = [pltpu.make_async_copy(srcs[a], slot(dsts[a], me), local_sems.at[a]) for a in parts]
    first = [copy(a, 0, me, sibling, src=srcs[a]) for a in parts]
    first += [copy(a, 1 + j, me, (*chip, c), src=srcs[a]) for j, chip in enumerate(chips) for a in parts]
    for cp in mine + first:
        cp.start()
    if meanwhile is not None:
        meanwhile()
    passed = []
    for j, chip in enumerate(chips):
        for a in parts:
            copy(a, 1 + j, (*chip, c), me).wait_recv()
            fwd = copy(a, 4 + j, (*chip, c), sibling)
            fwd.start()
            passed.append(fwd)
    for a in parts:
        copy(a, 0, sibling, me).wait_recv()
        for j, chip in enumerate(chips):
            copy(a, 4 + j, (*chip, 1 - c), me).wait_recv()
    for cp in first + passed:
        cp.wait_send()
    for cp in mine:
        cp.wait()


def _call(body, *, name, args, in_specs, out_specs, out_shape, grid=(), scratch_shapes=(), aliases=None, rider=()):
    n_in, n_out, n_scr, n_r = len(in_specs), len(out_specs), len(scratch_shapes), len(rider)
    kinds = [kind for kind, _ in rider]

    def kernel_body(*refs):
        c_in, r_in = refs[:n_in], refs[n_in:n_in + n_r]
        c_out = refs[n_in + n_r:n_in + n_r + n_out]
        r_out = refs[n_in + n_r + n_out:n_in + 2 * n_r + n_out]
        rest = refs[n_in + 2 * n_r + n_out:]
        c_scr, sems = rest[:n_scr], rest[n_scr:]
        if n_r:
            assert grid, "a rider needs a gridded call"
            ids = [pl.program_id(ax) for ax in range(len(grid))]
            first, last = ids[0] == 0, ids[0] == grid[0] - 1
            for pid, size in zip(ids[1:], grid[1:]):
                first = first & (pid == 0)
                last = last & (pid == size - 1)
            pl.when(first)(lambda: _exchange_ops(kinds, r_in, r_out, sems, True, False))
        body(*c_in, *c_out, *c_scr)
        if n_r:
            pl.when(last)(lambda: _exchange_ops(kinds, r_in, r_out, sems, False, True))

    anyspec = pl.BlockSpec(memory_space=pl.ANY)
    params = pltpu.CompilerParams(vmem_limit_bytes=VMEM_LIMIT, has_side_effects=bool(n_r))
    outs = pl.pallas_call(
        kernel_body, name=name, grid=grid, in_specs=list(in_specs) + [anyspec] * n_r,
        out_specs=list(out_specs) + [anyspec] * n_r,
        out_shape=list(out_shape) + [_landing_sds(kind, arr) for kind, arr in rider],
        scratch_shapes=list(scratch_shapes) + (_exchange_sems(n_r) if n_r else []),
        input_output_aliases=aliases or {}, compiler_params=params)(*args, *[arr for _, arr in rider])
    return list(outs)


def _mm(a, b, mode, tm, tn, tk, out_dtype=F32, add=None, name="mm", rider=()):
    if mode == "nn":
        (m, k), n = a.shape, b.shape[1]
        a_spec = pl.BlockSpec((tm, tk), lambda i, j, kk: (i, kk))
        b_spec = pl.BlockSpec((tk, tn), lambda i, j, kk: (kk, j))
        dot = _dot_nn
    elif mode == "nt":
        (m, k), n = a.shape, b.shape[0]
        a_spec = pl.BlockSpec((tm, tk), lambda i, j, kk: (i, kk))
        b_spec = pl.BlockSpec((tn, tk), lambda i, j, kk: (j, kk))
        dot = _dot_nt
    else:
        (k, m), n = a.shape, b.shape[1]
        a_spec = pl.BlockSpec((tk, tm), lambda i, j, kk: (kk, i))
        b_spec = pl.BlockSpec((tk, tn), lambda i, j, kk: (kk, j))
        dot = _dot_tn
    assert m % tm == 0 and n % tn == 0 and k % tk == 0, (m, n, k, tm, tn, tk)
    nk = k // tk
    has_add = add is not None

    def body(*refs):
        if has_add:
            a_ref, b_ref, add_ref, o_ref, acc = refs
        else:
            a_ref, b_ref, o_ref, acc = refs
        p = dot(a_ref[...].astype(BF16), b_ref[...].astype(BF16))

        def finish(total):
            if has_add:
                total = add_ref[...] + total
            o_ref[...] = total.astype(out_dtype)

        if nk == 1:
            finish(p)
        else:
            kk = pl.program_id(2)

            @pl.when(kk == 0)
            def _():
                acc[...] = p

            @pl.when(kk > 0)
            def _():
                acc[...] += p

            @pl.when(kk == nk - 1)
            def _():
                finish(acc[...])

    in_specs = [a_spec, b_spec]
    args = [a, b]
    if has_add:
        in_specs.append(pl.BlockSpec((tm, tn), lambda i, j, kk: (i, j)))
        args.append(add)
    acc_shape = (tm, tn) if nk > 1 else (8, LANES)
    outs = _call(body, name=name, args=args, grid=(m // tm, n // tn, nk), in_specs=in_specs,
                 out_specs=[pl.BlockSpec((tm, tn), lambda i, j, kk: (i, j))], out_shape=[_sds((m, n), out_dtype)],
                 scratch_shapes=[pltpu.VMEM(acc_shape, F32)], rider=rider)
    return outs if rider else outs[0]


def _rms_rinv(x):
    return lax.rsqrt(jnp.mean(x * x, axis=-1, keepdims=True) + EPS)


def _rms_bwd_core(du, x, g):
    r = _rms_rinv(x)
    dug = du * g
    dx = r * (dug - x * ((r * r) * jnp.mean(dug * x, axis=-1, keepdims=True)))
    dg = jnp.sum(du * (x * r), axis=0, keepdims=True)
    return dx, dg


def _half_sum(v, lo_half):
    s0 = jnp.sum(jnp.where(lo_half, v, 0.0), axis=-1, keepdims=True)
    s1 = jnp.sum(jnp.where(lo_half, 0.0, v), axis=-1, keepdims=True)
    return jnp.where(lo_half, s0, s1)


def _head_rinv(x, lo_half):
    return lax.rsqrt(_half_sum(x * x, lo_half) * (1.0 / HD) + EPS)


def _head_norm_bwd(dn, x, g, lo_half):
    r = _head_rinv(x, lo_half)
    dng = dn * g
    dx = r * (dng - x * ((r * r) * (_half_sum(dng * x, lo_half) * (1.0 / HD))))
    dg = jnp.sum(dn * (x * r), axis=0, keepdims=True)
    return dx, dg


def _rope_swap(x, lane):
    l64 = lane & (HD - 1)
    return jnp.where(l64 < ROT // 2, pltpu.roll(x, LANES - ROT // 2, 1), pltpu.roll(x, ROT // 2, 1))


def _rope_fwd(x, cos, sin, lane):
    return x * cos + _rope_swap(x, lane) * sin


def _rope_bwd(dy, cos, sin, lane):
    return dy * cos + jnp.where((lane & (HD - 1)) < ROT, _rope_swap(dy * sin, lane), 0.0)


def _g2(g_ref):
    g = g_ref[...]
    return jnp.concatenate([g, g], axis=-1)


def _pairs(width):
    return [slice(LANES * c, LANES * (c + 1)) for c in range(width // LANES)]


def _pick_lane(block, lane, idx):
    return jnp.sum(jnp.where(lane == idx, block, 0.0), axis=-1, keepdims=True)


def _head_a(x, g, wa, gq, gk, b_pad):
    def body(x_ref, g_ref, w_ref, gq_ref, gk_ref, b_ref, u_ref, p_ref, qo_ref, ko_ref, vo_ref, c_ref, cbc_ref, carry):
        @pl.when(pl.program_id(0) == 0)
        def _():
            carry[...] = jnp.zeros_like(carry)

        xv = x_ref[...]
        u = ((xv * _rms_rinv(xv)) * g_ref[...]).astype(BF16)
        u_ref[...] = u
        for lo in range(0, NA, D):
            hi = min(lo + D, NA)
            p_ref[:, lo:hi] = _dot_nn(u, w_ref[:, lo:hi])
        lane = _lane_iota((RT, LANES))
        lo_half = lane < HD
        gq2, gk2 = _g2(gq_ref), _g2(gk_ref)
        for c in _pairs(D):
            q = p_ref[:, c]
            k = p_ref[:, D + c.start:D + c.stop]
            qo_ref[:, c] = (((q * _head_rinv(q, lo_half)) * gq2) * QSCALE).astype(BF16)
            ko_ref[:, c] = ((k * _head_rinv(k, lo_half)) * gk2).astype(BF16)
        vo_ref[...] = p_ref[:, 2 * D:3 * D].astype(BF16)

        z = p_ref[:, FOFF:FOFF + LANES] + b_ref[...]
        logf = jnp.minimum(z, 0.0) - jnp.log1p(jnp.exp(-jnp.abs(z)))
        r = lax.broadcasted_iota(jnp.int32, (RT, RT), 0)
        cc = lax.broadcasted_iota(jnp.int32, (RT, RT), 1)
        tri = (r >= cc).astype(F32)
        loc = jnp.dot(tri, logf, precision=lax.Precision.HIGHEST, preferred_element_type=F32) + carry[0:1, :]
        c_ref[...] = loc
        carry[0:1, :] = loc[RT - 1:RT, :]
        for h in range(N_HEADS):
            cbc_ref[:, LANES * h:LANES * (h + 1)] = jnp.broadcast_to(_pick_lane(loc, lane, h), (RT, LANES))

    row = lambda width: pl.BlockSpec((RT, width), lambda i: (i, 0))
    whole = lambda arr: pl.BlockSpec(arr.shape, lambda i: (0,) * arr.ndim, pipeline_mode=pl.Buffered(1))
    return pl.pallas_call(
        body, name="head_a", grid=(S // RT,),
        in_specs=[row(D), whole(g), whole(wa), whole(gq), whole(gk), whole(b_pad)],
        out_specs=[row(D), row(NA), row(D), row(D), row(D), row(LANES), row(N_HEADS * LANES)],
        out_shape=[_sds((S, D), BF16), _sds((S, NA), F32)] + [_sds((S, D), BF16)] * 3
        + [_sds((S, LANES), F32), _sds((S, N_HEADS * LANES), F32)],
        scratch_shapes=[pltpu.VMEM((8, LANES), F32)], compiler_params=_params())(x, g, wa, gq, gk, b_pad)


def _key_le_query(offset, keys=TK):
    r = lax.broadcasted_iota(jnp.int32, (keys, TQ), 0)
    c = lax.broadcasted_iota(jnp.int32, (keys, TQ), 1)
    return (r + offset) <= c


def _widen(tile):
    return jnp.concatenate([tile] * (TQ // LANES), axis=1)


def _fox_fwd(qn, kn, vb, proj, crow, cbc, rider=()):
    nq = S // TQ

    def body(q_ref, k_ref, v_ref, g_ref, cq_ref, cbc_ref, o_ref, z_ref, lse_ref, st_s, pt_s):
        i = pl.program_id(1)
        qs = [q_ref[:, HD * hh:HD * (hh + 1)] for hh in range(HPS)]
        cqs = [cq_ref[hh, 0] for hh in range(HPS)]

        def scores(s, hh):
            off = pl.multiple_of(s * KS, KS)
            kj = k_ref[pl.ds(off, KS), HD * hh:HD * (hh + 1)]
            return (_dot_nt(kj, qs[hh]) + cqs[hh]) - _widen(cbc_ref[pl.ds(off, KS), LANES * hh:LANES * (hh + 1)])

        def values(s, hh, pt):
            off = pl.multiple_of(s * KS, KS)
            return _dot_tn(v_ref[pl.ds(off, KS), HD * hh:HD * (hh + 1)], pt)

        def step(s, slot, carries, mask=None, last=False):
            if not last:
                for hh in range(HPS):
                    st_s[1 - slot, hh] = scores(s + 1, hh)
            pvs = [values(jnp.maximum(s - 1, 0), hh, pt_s[1 - slot, hh]) for hh in range(HPS)]
            out = []
            for hh in range(HPS):
                m, l, acc = carries[hh]
                st = st_s[slot, hh]
                if mask is not None:
                    st = jnp.where(mask, st, -jnp.inf)
                m_new = jnp.maximum(m, jnp.max(st, axis=0, keepdims=True))
                pt = jnp.exp(st - m_new)
                alpha = jnp.exp(m - m_new)
                pt_s[slot, hh] = pt.astype(BF16)
                out.append((m_new, alpha * l + jnp.sum(pt, axis=0, keepdims=True), alpha * (acc + pvs[hh])))
            return tuple(out)

        for hh in range(HPS):
            st_s[0, hh] = scores(0, hh)
            pt_s[1, hh] = jnp.zeros((KS, TQ), BF16)
        one = (jnp.full((1, TQ), -jnp.inf, F32), jnp.zeros((1, TQ), F32), jnp.zeros((HD, TQ), F32))
        carries = lax.fori_loop(0, i, lambda t, cr: step(2 * t + 1, 1, step(2 * t, 0, cr)), (one,) * HPS)
        carries = step(2 * i, 0, carries, mask=_key_le_query(0, KS))
        carries = step(2 * i + 1, 1, carries, mask=_key_le_query(KS, KS), last=True)
        accs = []
        for hh in range(HPS):
            m, l, acc = carries[hh]
            acc = acc + values(2 * i + 1, hh, pt_s[1, hh])
            accs.append(acc / l)
            lse_ref[hh, 0] = m + jnp.log(l)
        o = jnp.concatenate(accs, axis=0).T
        o_ref[...] = o
        g = g_ref[...]
        z_ref[...] = (o * (g * _sigmoid(g))).astype(BF16)

    qblk = pl.BlockSpec((TQ, HW), lambda hp, i: (i, hp))
    full = pl.BlockSpec((S, HW), lambda hp, i: (0, hp))
    rows = pl.BlockSpec((HPS, 1, 1, TQ), lambda hp, i: (hp, i, 0, 0))
    return _call(
        body, name="fox_fwd", args=(qn, kn, vb, proj, crow, cbc), grid=(N_HEADS // HPS, nq),
        in_specs=[qblk, full, full,
                  pl.BlockSpec((TQ, HW), lambda hp, i: (i, GOFF // HW + hp)),
                  rows, pl.BlockSpec((S, HPS * LANES), lambda hp, i: (0, hp))],
        out_specs=[qblk, qblk, rows],
        out_shape=[_sds((S, D), F32), _sds((S, D), BF16), _sds((N_HEADS, nq, 1, TQ), F32)],
        scratch_shapes=[pltpu.VMEM((2, HPS, KS, TQ), F32), pltpu.VMEM((2, HPS, KS, TQ), BF16)], rider=rider)


def _fox_bwd_pre(dh, w_out, proj, o):
    nq, rows = S // TQ, 2 * TQ
    per = rows // TQ

    def body(dh_ref, w_ref, g_ref, o_ref, do_ref, dg_ref, delta_ref):
        g = g_ref[...]
        sg = _sigmoid(g)
        dzv = _dot_nt(dh_ref[...].astype(BF16), w_ref[...])
        ov = o_ref[...]
        do = dzv * (g * sg)
        dg_ref[...] = (dzv * ov * (sg * (1.0 + g * (1.0 - sg)))).astype(BF16)
        do_ref[...] = do.astype(BF16)
        prod_t = (do * ov).T
        for h in range(N_HEADS):
            for b in range(per):
                delta_ref[h, b] = jnp.sum(prod_t[HD * h:HD * (h + 1), TQ * b:TQ * (b + 1)], axis=0, keepdims=True)

    row = pl.BlockSpec((rows, D), lambda i: (i, 0))
    return pl.pallas_call(
        body, name="fox_bwd_pre", grid=(S // rows,),
        in_specs=[row, pl.BlockSpec(w_out.shape, lambda i: (0, 0), pipeline_mode=pl.Buffered(1)),
                  pl.BlockSpec((rows, D), lambda i: (i, GOFF // D)), row],
        out_specs=[row, row, pl.BlockSpec((N_HEADS, per, 1, TQ), lambda i: (0, i, 0, 0))],
        out_shape=[_sds((S, D), BF16), _sds((S, D), BF16), _sds((N_HEADS, nq, 1, TQ), F32)],
        compiler_params=_params())(dh, w_out, proj, o)


def _fox_bwd(qn, kn, vb, dob, lse, delta, crow, cbc, rider=()):
    nq, nkb = S // TQ, S // TK

    def body(q_ref, k_ref, v_ref, do_ref, lse_ref, del_ref, cq_ref, cbc_ref,
             dk_ref, dv_ref, dcs_ref, dq_ref, dr_ref, st_s, dp_s, pt_s, ds_s, dq_acc, dr_acc):
        j = pl.program_id(1)

        @pl.when(j == 0)
        def _():
            dq_acc[...] = jnp.zeros_like(dq_acc)
            dr_acc[...] = jnp.zeros_like(dr_acc)

        kjs = [k_ref[:, HD * hh:HD * (hh + 1)] for hh in range(HPS)]
        vjs = [v_ref[:, HD * hh:HD * (hh + 1)] for hh in range(HPS)]

        def rows_of(ref, u, hh):
            off = pl.multiple_of(u * TQ, TQ)
            return ref[pl.ds(off, TQ), HD * hh:HD * (hh + 1)]

        def products(u, hh):
            st = (_dot_nt(kjs[hh], rows_of(q_ref, u, hh)) + cq_ref[hh, u]) - _widen(
                cbc_ref[:, LANES * hh:LANES * (hh + 1)])
            return st, _dot_nt(vjs[hh], rows_of(do_ref, u, hh))

        def step(u, slot, carries, masked=False):
            nxt = jnp.minimum(u + 1, nq - 1)
            for hh in range(HPS):
                st_s[1 - slot, hh], dp_s[1 - slot, hh] = products(nxt, hh)
            prev = jnp.maximum(u - 1, 0)
            dvs = [_dot_nn(pt_s[1 - slot, hh], rows_of(do_ref, prev, hh)) for hh in range(HPS)]
            dks = [_dot_nn(ds_s[1 - slot, hh], rows_of(q_ref, prev, hh)) for hh in range(HPS)]
            for hh in range(HPS):
                dq_acc[hh, prev] += _dot_tn(kjs[hh], ds_s[1 - slot, hh])
            out = []
            for hh in range(HPS):
                dk, dv, dcs = carries[hh]
                st = st_s[slot, hh]
                if masked:
                    st = jnp.where(_key_le_query((j - u) * TQ), st, -jnp.inf)
                pt = jnp.exp(st - lse_ref[hh, u])
                dst = pt * (dp_s[slot, hh] - del_ref[hh, u])
                pt_s[slot, hh] = pt.astype(BF16)
                ds_s[slot, hh] = dst.astype(BF16)
                dr_acc[hh, u] += jnp.sum(dst, axis=0, keepdims=True)
                out.append((dk + dks[hh], dv + dvs[hh], dcs + (dst[:, :LANES] + dst[:, LANES:])))
            return tuple(out)

        t0 = j // 2
        for hh in range(HPS):
            st_s[0, hh], dp_s[0, hh] = products(2 * t0, hh)
            pt_s[1, hh] = jnp.zeros((TK, TQ), BF16)
            ds_s[1, hh] = jnp.zeros((TK, TQ), BF16)
        one = (jnp.zeros((TK, HD), F32), jnp.zeros((TK, HD), F32), jnp.zeros((TK, LANES), F32))
        carries = step(2 * t0 + 1, 1, step(2 * t0, 0, (one,) * HPS, masked=True), masked=True)
        carries = lax.fori_loop(t0 + 1, nq // 2, lambda t, cr: step(2 * t + 1, 1, step(2 * t, 0, cr)), carries)
        dks, dvs = [], []
        lane = _lane_iota((TK, LANES))
        dcs_all = jnp.zeros((TK, LANES), F32)
        for hh in range(HPS):
            dk, dv, dcs = carries[hh]
            dks.append(dk + _dot_nn(ds_s[1, hh], rows_of(q_ref, nq - 1, hh)))
            dvs.append(dv + _dot_nn(pt_s[1, hh], rows_of(do_ref, nq - 1, hh)))
            dq_acc[hh, nq - 1] += _dot_tn(kjs[hh], ds_s[1, hh])
            dcs_all = jnp.where(lane == HPS * pl.program_id(0) + hh, -jnp.sum(dcs, axis=-1, keepdims=True), dcs_all)
        dcs_ref[0] = dcs_all
        dk_ref[...] = jnp.concatenate(dks, axis=-1)
        dv_ref[...] = jnp.concatenate(dvs, axis=-1).astype(BF16)

        @pl.when(j == nkb - 1)
        def _():
            for i in range(nq):
                dq_ref[TQ * i:TQ * (i + 1), :] = jnp.concatenate([dq_acc[hh, i] for hh in range(HPS)], axis=0).T
            dr_ref[...] = dr_acc[...]

    kblk = pl.BlockSpec((TK, HW), lambda hp, j: (j, hp))
    full = pl.BlockSpec((S, HW), lambda hp, j: (0, hp))
    rows = pl.BlockSpec((HPS, nq, 1, TQ), lambda hp, j: (hp, 0, 0, 0))
    cblk = pl.BlockSpec((TK, HPS * LANES), lambda hp, j: (j, hp))
    return _call(
        body, name="fox_bwd", args=(qn, kn, vb, dob, lse, delta, crow, cbc), grid=(N_HEADS // HPS, nkb),
        in_specs=[full, kblk, kblk, full, rows, rows, rows, cblk],
        out_specs=[kblk, kblk, pl.BlockSpec((1, TK, LANES), lambda hp, j: (hp, j, 0)), full, rows],
        out_shape=[_sds((S, D), F32), _sds((S, D), BF16), _sds((N_HEADS // HPS, S, LANES), F32), _sds((S, D), F32),
                   _sds((N_HEADS, nq, 1, TQ), F32)],
        scratch_shapes=[pltpu.VMEM((2, HPS, TK, TQ), F32), pltpu.VMEM((2, HPS, TK, TQ), F32),
                        pltpu.VMEM((2, HPS, TK, TQ), BF16), pltpu.VMEM((2, HPS, TK, TQ), BF16),
                        pltpu.VMEM((HPS, nq, HD, TQ), F32), pltpu.VMEM((HPS, nq, 1, TQ), F32)], rider=rider)


def _prep_a_bwd(dq, dk, dv, dgate, drow, dcs, proj, b_pad, gq, gk):
    nt = S // TM

    def body(dq_ref, dk_ref, dv_ref, dgt_ref, dr_ref, dcs_ref, xq_ref, xk_ref, f_ref, b_ref, gq_ref, gk_ref,
             o_ref, dgq_ref, dgk_ref, db_ref, carry):
        @pl.when(pl.program_id(0) == 0)
        def _():
            carry[...] = jnp.zeros_like(carry)
            dgq_ref[...] = jnp.zeros_like(dgq_ref)
            dgk_ref[...] = jnp.zeros_like(dgk_ref)
            db_ref[...] = jnp.zeros_like(db_ref)

        lane = _lane_iota((TM, LANES))
        lo_half = lane < HD
        gq2, gk2 = _g2(gq_ref), _g2(gk_ref)
        dgq, dgk = jnp.zeros((1, LANES), F32), jnp.zeros((1, LANES), F32)
        for c in _pairs(D):
            dxq, dg = _head_norm_bwd(dq_ref[:, c] * QSCALE, xq_ref[:, c], gq2, lo_half)
            o_ref[:, c] = dxq.astype(BF16)
            dgq = dgq + dg
            dxk, dg = _head_norm_bwd(dk_ref[:, c], xk_ref[:, c], gk2, lo_half)
            o_ref[:, D + c.start:D + c.stop] = dxk.astype(BF16)
            dgk = dgk + dg
        dgq_ref[...] += dgq
        dgk_ref[...] += dgk
        o_ref[:, 2 * D:3 * D] = dv_ref[...]
        o_ref[:, GOFF:GOFF + D] = dgt_ref[...]

        dc = dr_ref[...]
        for group in range(N_HEADS // HPS):
            dc = dc + dcs_ref[group]
        r = lax.broadcasted_iota(jnp.int32, (TM, TM), 0)
        c = lax.broadcasted_iota(jnp.int32, (TM, TM), 1)
        tri = (c >= r).astype(F32)
        dlogf = jnp.dot(tri, dc, precision=lax.Precision.HIGHEST, preferred_element_type=F32) + carry[0:1, :]
        carry[0:1, :] = dlogf[0:1, :]
        df = dlogf * (1.0 / (1.0 + jnp.exp(f_ref[...] + b_ref[...])))
        db_ref[...] += jnp.sum(df, axis=0, keepdims=True)
        o_ref[:, FOFF:FOFF + LANES] = df.astype(BF16)
        o_ref[:, FOFF + LANES:NA] = jnp.zeros((TM, NA - FOFF - LANES), BF16)

    rev = lambda width, col: pl.BlockSpec((TM, width), lambda i: (nt - 1 - i, col))
    gspec = pl.BlockSpec((1, HD), lambda i: (0, 0))
    acc = pl.BlockSpec((1, LANES), lambda i: (0, 0))
    return pl.pallas_call(
        body, name="prep_a_bwd", grid=(nt,),
        in_specs=[rev(D, 0), rev(D, 0), rev(D, 0), rev(D, 0), rev(LANES, 0),
                  pl.BlockSpec((N_HEADS // HPS, TM, LANES), lambda i: (0, nt - 1 - i, 0)),
                  rev(D, 0), rev(D, 1), rev(LANES, FOFF // LANES), acc, gspec, gspec],
        out_specs=[rev(NA, 0), acc, acc, acc],
        out_shape=[_sds((S, NA), BF16)] + [_sds((1, LANES), F32)] * 3,
        scratch_shapes=[pltpu.VMEM((8, LANES), F32)],
        compiler_params=_params())(dq, dk, dv, dgate, drow, dcs, proj, proj, proj, b_pad, gq, gk)


def _head_b(x, z_a, w_out_a, g_kv, g_b, w_kv, w_in_b, gq, gk, cos2, sin2):
    nkv = w_kv.shape[1] // 2

    def body(x_ref, z_ref, wo_ref, gkv_ref, gb_ref, wkv_ref, wb_ref, gq_ref, gk_ref, c_ref, s_ref,
             h_ref, ukv_ref, ub_ref, kv_ref, pb_ref, qo_ref, ko_ref, vo_ref):
        xv = x_ref[...] + _dot_nn(z_ref[...], wo_ref[...])
        h_ref[...] = xv
        xn = xv * _rms_rinv(xv)
        ukv = (xn * gkv_ref[...]).astype(BF16)
        ub = (xn * gb_ref[...]).astype(BF16)
        ukv_ref[...] = ukv
        ub_ref[...] = ub
        kv_ref[...] = _dot_nn(ukv, wkv_ref[...])
        for lo in range(0, 2 * D, D):
            pb_ref[:, lo:lo + D] = _dot_nn(ub, wb_ref[:, lo:lo + D])
        lane = _lane_iota((RT, LANES))
        lo_half = lane < HD
        cos, sin = c_ref[...], s_ref[...]
        gq2, gk2 = _g2(gq_ref), _g2(gk_ref)
        for c in _pairs(D):
            q = pb_ref[:, c]
            qo_ref[:, c] = (_rope_fwd((q * _head_rinv(q, lo_half)) * gq2, cos, sin, lane) * QSCALE).astype(BF16)
        for c in _pairs(nkv):
            k = kv_ref[:, c]
            ko_ref[:, c] = _rope_fwd((k * _head_rinv(k, lo_half)) * gk2, cos, sin, lane).astype(BF16)
        vo_ref[...] = kv_ref[:, nkv:2 * nkv].astype(BF16)

    row = lambda width: pl.BlockSpec((RT, width), lambda i: (i, 0))
    whole = lambda arr: pl.BlockSpec(arr.shape, lambda i: (0,) * arr.ndim, pipeline_mode=pl.Buffered(1))
    return pl.pallas_call(
        body, name="head_b", grid=(S // RT,),
        in_specs=[row(D), row(D), whole(w_out_a), whole(g_kv), whole(g_b), whole(w_kv), whole(w_in_b), whole(gq),
                  whole(gk), row(LANES), row(LANES)],
        out_specs=[row(D), row(D), row(D), row(2 * nkv), row(2 * D), row(D), row(nkv), row(nkv)],
        out_shape=[_sds((S, D), F32), _sds((S, D), BF16), _sds((S, D), BF16), _sds((S, 2 * nkv), F32),
                   _sds((S, 2 * D), F32), _sds((S, D), BF16), _sds((S, nkv), BF16), _sds((S, nkv), BF16)],
        compiler_params=_params())(x, z_a, w_out_a, g_kv, g_b, w_kv, w_in_b, gq, gk, cos2, sin2)


N_KV, GRP = 4, 4


def _swa_mask(n):
    r = lax.broadcasted_iota(jnp.int32, (2 * WIN, GRP * WIN), 0)
    q = lax.broadcasted_iota(jnp.int32, (2 * WIN, GRP * WIN), 1) & (WIN - 1)
    return (r > q) & (r <= q + WIN) & ((r >= WIN) | (n > 0))


def _stack4(ref_or_val, base):
    return jnp.concatenate([ref_or_val[:, base + HD * g: base + HD * (g + 1)] for g in range(GRP)], axis=0)


def _unstack4(xt):
    return jnp.concatenate([xt[:, WIN * g:WIN * (g + 1)] for g in range(GRP)], axis=0).T


def _band(prev_ref, cur_ref, kh):
    return jnp.concatenate([prev_ref[:, HD * kh:HD * (kh + 1)], cur_ref[:, HD * kh:HD * (kh + 1)]], axis=0)


def _sink_row(s_ref, first):
    lane = _lane_iota((1, GRP * WIN))
    row = jnp.full((1, GRP * WIN), s_ref[first + GRP - 1], F32)
    for g in range(GRP - 2, -1, -1):
        row = jnp.where(lane < WIN * (g + 1), s_ref[first + g], row)
    return row


def _swa_fwd(qb, ksh, vsh, pb, sinks):
    nb = S // WIN

    def body(q_ref, kp_ref, kc_ref, vp_ref, vc_ref, g_ref, s_ref, o_ref, z_ref, lse_ref):
        n = pl.program_id(0)
        valid = _swa_mask(n)
        outs = []
        for kh in range(N_KV):
            kb, vb = _band(kp_ref, kc_ref, kh), _band(vp_ref, vc_ref, kh)
            st = jnp.where(valid, _dot_nt(kb, _stack4(q_ref, GRP * HD * kh)), -jnp.inf)
            sink = _sink_row(s_ref, GRP * kh)
            m = jnp.maximum(jnp.max(st, axis=0, keepdims=True), sink)
            pt = jnp.exp(st - m)
            l = jnp.sum(pt, axis=0, keepdims=True) + jnp.exp(sink - m)
            outs.append(_unstack4(_dot_tn(vb, pt.astype(BF16)) / l))
            lse = m + jnp.log(l)
            for g in range(GRP):
                lse_ref[GRP * kh + g, 0] = lse[:, WIN * g:WIN * (g + 1)]
        o = jnp.concatenate(outs, axis=-1)
        o_ref[...] = o
        g = g_ref[...]
        z_ref[...] = (o * (g * _sigmoid(g))).astype(BF16)

    row = pl.BlockSpec((WIN, D), lambda n: (n, 0))
    prev = pl.BlockSpec((WIN, N_KV * HD), lambda n: (jnp.maximum(n - 1, 0), 0))
    cur = pl.BlockSpec((WIN, N_KV * HD), lambda n: (n, 0))
    return pl.pallas_call(
        body, name="swa_fwd", grid=(nb,),
        in_specs=[row, prev, cur, prev, cur, pl.BlockSpec((WIN, D), lambda n: (n, 1)),
                  pl.BlockSpec(memory_space=pltpu.SMEM)],
        out_specs=[row, row, pl.BlockSpec((N_HEADS, 1, 1, WIN), lambda n: (0, n, 0, 0))],
        out_shape=[_sds((S, D), F32), _sds((S, D), BF16), _sds((N_HEADS, nb, 1, WIN), F32)],
        compiler_params=_params())(qb, ksh, ksh, vsh, vsh, pb, sinks)


def _swa_bwd(qb, ksh, vsh, dz, o, lse, pb, sinks, gq, cos2, sin2):
    nb = S // WIN

    def body(q_ref, kp_ref, kc_ref, vp_ref, vc_ref, dz_ref, o_ref, lse_ref, x_ref, g_ref, s_ref, gq_ref, c_ref, sn_ref,
             dpb_ref, dka_ref, dkb_ref, dva_ref, dvb_ref, dsink_ref, dgq_ref):
        n = pl.program_id(0)

        @pl.when(n == 0)
        def _():
            dsink_ref[...] = jnp.zeros_like(dsink_ref)
            dgq_ref[...] = jnp.zeros_like(dgq_ref)

        valid = _swa_mask(n)
        g = g_ref[...]
        sg = _sigmoid(g)
        dzv = dz_ref[...]
        ov = o_ref[...]
        do = dzv * (g * sg)
        dpb_ref[:, D:2 * D] = (dzv * ov * (sg * (1.0 + g * (1.0 - sg)))).astype(BF16)
        prod_t = (do * ov).T
        lane1 = _lane_iota((1, LANES))
        dqs, dkas, dkbs, dvas, dvbs = [], [], [], [], []
        dsink = jnp.zeros((1, LANES), F32)
        for kh in range(N_KV):
            kb, vb = _band(kp_ref, kc_ref, kh), _band(vp_ref, vc_ref, kh)
            base = GRP * HD * kh
            qs = _stack4(q_ref, base)
            dos = _stack4(do, base).astype(BF16)
            delta = jnp.concatenate(
                [jnp.sum(prod_t[base + HD * gg:base + HD * (gg + 1), :], axis=0, keepdims=True)
                 for gg in range(GRP)], axis=1)
            lse = jnp.concatenate([lse_ref[GRP * kh + gg, 0] for gg in range(GRP)], axis=1)
            st = jnp.where(valid, _dot_nt(kb, qs), -jnp.inf)
            pt = jnp.exp(st - lse)
            dst = pt * (_dot_nt(vb, dos) - delta)
            dsb = dst.astype(BF16)
            dqs.append(_unstack4(_dot_tn(kb, dsb)))
            dkband = _dot_nn(dsb, qs)
            dvband = _dot_nn(pt.astype(BF16), dos)
            dkbs.append(dkband[0:WIN, :])
            dkas.append(dkband[WIN:2 * WIN, :])
            dvbs.append(dvband[0:WIN, :])
            dvas.append(dvband[WIN:2 * WIN, :])
            ps_delta = jnp.exp(_sink_row(s_ref, GRP * kh) - lse) * delta
            for gg in range(GRP):
                val = jnp.sum(ps_delta[:, WIN * gg:WIN * (gg + 1)], axis=1, keepdims=True)
                dsink = dsink - jnp.where(lane1 == GRP * kh + gg, val, 0.0)
        dka_ref[...] = jnp.concatenate(dkas, axis=-1)
        dkb_ref[...] = jnp.concatenate(dkbs, axis=-1)
        dva_ref[...] = jnp.concatenate(dvas, axis=-1)
        dvb_ref[...] = jnp.concatenate(dvbs, axis=-1)
        dsink_ref[...] += dsink

        lane = _lane_iota((WIN, LANES))
        g2, cos, sin = _g2(gq_ref), c_ref[...], sn_ref[...]
        dg_tot = jnp.zeros((1, LANES), F32)
        for kh in range(N_KV):
            for c in _pairs(GRP * HD):
                cols = slice(GRP * HD * kh + c.start, GRP * HD * kh + c.stop)
                dn = _rope_bwd(dqs[kh][:, c] * QSCALE, cos, sin, lane)
                dx, dg = _head_norm_bwd(dn, x_ref[:, cols], g2, lane < HD)
                dpb_ref[:, cols] = dx.astype(BF16)
                dg_tot = dg_tot + dg
        dgq_ref[...] += dg_tot

    row = pl.BlockSpec((WIN, D), lambda n: (n, 0))
    prev = pl.BlockSpec((WIN, N_KV * HD), lambda n: (jnp.maximum(n - 1, 0), 0))
    cur = pl.BlockSpec((WIN, N_KV * HD), lambda n: (n, 0))
    acc = pl.BlockSpec((1, LANES), lambda n: (0, 0))
    tab = pl.BlockSpec((WIN, LANES), lambda n: (n, 0))
    return pl.pallas_call(
        body, name="swa_bwd", grid=(nb,),
        in_specs=[row, prev, cur, prev, cur, row, row, pl.BlockSpec((N_HEADS, 1, 1, WIN), lambda n: (0, n, 0, 0)),
                  row, pl.BlockSpec((WIN, D), lambda n: (n, 1)), pl.BlockSpec(memory_space=pltpu.SMEM),
                  pl.BlockSpec((1, HD), lambda n: (0, 0)), tab, tab],
        out_specs=[pl.BlockSpec((WIN, 2 * D), lambda n: (n, 0)), cur, cur, cur, cur, acc, acc],
        out_shape=[_sds((S, 2 * D), BF16)] + [_sds((S, 256), F32)] * 4 + [_sds((1, LANES), F32)] * 2,
        compiler_params=_params())(qb, ksh, ksh, vsh, vsh, dz, o, lse, pb, pb, sinks, gq, cos2, sin2)


def _prep_kv_bwd(dka, dkb, dva, dvb, kv, gk, cos2, sin2):
    nt = S // RB
    per = RB // WIN

    def shifted(cur_ref, nxt_ref, has_next):
        return jnp.concatenate([cur_ref[WIN:RB, :], jnp.where(has_next, nxt_ref[...], 0.0)], axis=0)

    def body(dka_ref, dkb_ref, dkn_ref, dva_ref, dvb_ref, dvn_ref, x_ref, g_ref, c_ref, s_ref, o_ref, dgk_ref):
        i, j = pl.program_id(0), pl.program_id(1)

        @pl.when((i == 0) & (j == 0))
        def _():
            dgk_ref[...] = jnp.zeros_like(dgk_ref)

        has_next = i < nt - 1

        @pl.when(j == 0)
        def _():
            lane = _lane_iota((RB, LANES))
            g2, cos, sin = _g2(g_ref), c_ref[...], s_ref[...]
            dy_all = dka_ref[...] + shifted(dkb_ref, dkn_ref, has_next)
            dg_tot = jnp.zeros((1, LANES), F32)
            for c in _pairs(CB):
                dn = _rope_bwd(dy_all[:, c], cos, sin, lane)
                dx, dg = _head_norm_bwd(dn, x_ref[:, c], g2, lane < HD)
                o_ref[:, c] = dx.astype(BF16)
                dg_tot = dg_tot + dg
            dgk_ref[...] += dg_tot

        @pl.when(j == 1)
        def _():
            o_ref[...] = (dva_ref[...] + shifted(dvb_ref, dvn_ref, has_next)).astype(BF16)

    cur = pl.BlockSpec((RB, CB), lambda i, j: (i, 0))
    nxt = pl.BlockSpec((WIN, CB), lambda i, j: (jnp.minimum(per * (i + 1), S // WIN - 1), 0))
    tab = pl.BlockSpec((RB, LANES), lambda i, j: (i, 0))
    return pl.pallas_call(
        body, name="prep_kv_bwd", grid=(nt, 2),
        in_specs=[cur, cur, nxt, cur, cur, nxt, cur, pl.BlockSpec((1, HD), lambda i, j: (0, 0)), tab, tab],
        out_specs=[pl.BlockSpec((RB, CB), lambda i, j: (i, j)), pl.BlockSpec((1, LANES), lambda i, j: (0, 0))],
        out_shape=[_sds((S, 2 * CB), BF16), _sds((1, LANES), F32)],
        compiler_params=_params())(dka, dkb, dkb, dva, dvb, dvb, kv, gk, cos2, sin2)


def _out_b_loss(z, w_out, h1, tgt):
    tm = 2 * RT

    def body(z_ref, w_ref, h_ref, t_ref, dy_ref, l_ref):
        @pl.when(pl.program_id(0) == 0)
        def _():
            l_ref[...] = jnp.zeros_like(l_ref)

        e = (h_ref[...] + _dot_nn(z_ref[...], w_ref[...])) - t_ref[...]
        dy_ref[...] = e * (1.0 / D)
        l_ref[...] += jnp.sum(jnp.sum(e * e, axis=-1, keepdims=True), axis=0, keepdims=True)

    row = pl.BlockSpec((tm, D), lambda i: (i, 0))
    return pl.pallas_call(
        body, name="out_b_loss", grid=(S // tm,),
        in_specs=[row, pl.BlockSpec(w_out.shape, lambda i: (0, 0), pipeline_mode=pl.Buffered(1)), row, row],
        out_specs=[row, pl.BlockSpec((1, LANES), lambda i: (0, 0))],
        out_shape=[_sds((S, D), F32), _sds((1, LANES), F32)], compiler_params=_params())(z, w_out, h1, tgt)


def _du_a_rms_bwd(dproj, wa, x, g, dres, after):
    tm, tk = 1024, NA // 2
    nk = NA // tk

    def body(a_ref, b_ref, x_ref, g_ref, dr_ref, after_ref, dx_ref, dg_ref, acc):
        i, kk = pl.program_id(0), pl.program_id(1)

        @pl.when((i == 0) & (kk == 0))
        def _():
            dg_ref[...] = jnp.zeros_like(dg_ref)

        p = _dot_nt(a_ref[...], b_ref[...])

        @pl.when(kk == 0)
        def _():
            acc[...] = p

        @pl.when(kk == nk - 1)
        def _():
            dx, dg = _rms_bwd_core(acc[...] + p, x_ref[...], g_ref[...])
            dx_ref[...] = dr_ref[...] + dx
            dg_ref[...] += dg

    assert nk == 2
    row = pl.BlockSpec((tm, D), lambda i, kk: (i, 0))
    vec = pl.BlockSpec((1, D), lambda i, kk: (0, 0))
    return pl.pallas_call(
        body, name="du_a_rms_bwd", grid=(S // tm, nk),
        in_specs=[pl.BlockSpec((tm, tk), lambda i, kk: (i, kk)), pl.BlockSpec((D, tk), lambda i, kk: (0, kk)),
                  row, vec, row, pl.BlockSpec(after.shape, lambda i, kk: (0, 0))],
        out_specs=[row, vec], out_shape=[_sds((S, D), F32), _sds((1, D), F32)],
        scratch_shapes=[pltpu.VMEM((tm, D), F32)], compiler_params=_params())(dproj, wa, x, g, dres, after)


def _du_b_rms_bwd(dpb, w_in_b, dkv, w_kv, h1, g_b, g_kv, dy):
    tm = 2 * RT

    def body(ab_ref, wb_ref, akv_ref, wkv_ref, x_ref, gb_ref, gkv_ref, dy_ref, dh_ref, dgb_ref, dgkv_ref):
        @pl.when(pl.program_id(0) == 0)
        def _():
            dgb_ref[...] = jnp.zeros_like(dgb_ref)
            dgkv_ref[...] = jnp.zeros_like(dgkv_ref)

        x = x_ref[...]
        dx1, dg1 = _rms_bwd_core(_dot_nt(ab_ref[...], wb_ref[...]), x, gb_ref[...])
        dx2, dg2 = _rms_bwd_core(_dot_nt(akv_ref[...], wkv_ref[...]), x, gkv_ref[...])
        dh_ref[...] = dy_ref[...] + dx1 + dx2
        dgb_ref[...] += dg1
        dgkv_ref[...] += dg2

    row = lambda width: pl.BlockSpec((tm, width), lambda i: (i, 0))
    whole = lambda arr: pl.BlockSpec(arr.shape, lambda i: (0, 0), pipeline_mode=pl.Buffered(1))
    vec = pl.BlockSpec((1, D), lambda i: (0, 0))
    return pl.pallas_call(
        body, name="du_b_rms_bwd", grid=(S // tm,),
        in_specs=[row(dpb.shape[1]), whole(w_in_b), row(dkv.shape[1]), whole(w_kv), row(D), vec, vec, row(D)],
        out_specs=[row(D), vec, vec], out_shape=[_sds((S, D), F32), _sds((1, D), F32), _sds((1, D), F32)],
        compiler_params=_params())(dpb, w_in_b, dkv, w_kv, h1, g_b, g_kv, dy)


def _gather_first(w_in_a, w_out_a, w_kv, w_in_b, w_out_b, norm_a_g):
    def body(wia_ref, woa_ref, wkv_ref, wib_ref, wob_ref, ga_ref,
             wa_g, ga_g, woa_s, wkv_s, wib_s, wob_s, wa_s, st_a, st_oa, st_kv, st_ib, st_ob, load_sems, *sems):
        sources = [wia_ref.at[0], woa_ref.at[0], wkv_ref, wib_ref.at[0], wob_ref.at[0]]
        stages = [st_a, st_oa, st_kv, st_ib, st_ob]
        loads = [pltpu.make_async_copy(src, dst, load_sems.at[i]) for i, (src, dst) in enumerate(zip(sources, stages))]
        for cp in loads:
            cp.start()
        loads[0].wait()
        wa_s[...] = st_a[...].astype(BF16)

        def cast_the_rest():
            for cp, stage, out in zip(loads[1:], stages[1:], [woa_s, wkv_s, wib_s, wob_s]):
                cp.wait()
                out[...] = stage[...].astype(BF16)

        _gather_two_level([wa_s, ga_ref], [wa_g, ga_g], sems, meanwhile=cast_the_rest)

    vmem = pl.BlockSpec(memory_space=pltpu.VMEM)
    anyspec = pl.BlockSpec(memory_space=pl.ANY)
    shard = lambda w: _sds(w.shape[-2:], BF16)
    stage = lambda w: pltpu.VMEM(w.shape[-2:], F32)
    return pl.pallas_call(
        body, name="gather_first", in_specs=[anyspec] * 5 + [vmem],
        out_specs=[anyspec, anyspec, vmem, vmem, vmem, vmem],
        out_shape=[_sds((N_DEV,) + w_in_a.shape[-2:], BF16), _sds((N_DEV,) + norm_a_g.shape, F32),
                   shard(w_out_a), shard(w_kv), shard(w_in_b), shard(w_out_b)],
        scratch_shapes=[pltpu.VMEM(w_in_a.shape[-2:], BF16), stage(w_in_a), stage(w_out_a), stage(w_kv),
                        stage(w_in_b), stage(w_out_b), pltpu.SemaphoreType.DMA((5,))] + _exchange_sems(2),
        compiler_params=pltpu.CompilerParams(vmem_limit_bytes=VMEM_LIMIT, has_side_effects=True))(
            w_in_a, w_out_a, w_kv, w_in_b, w_out_b, norm_a_g)


def _pair_reduce(slots):
    n_chip = N_DEV // 2
    _, rows, cols = slots.shape

    def body(s_ref, o_ref, own_v, sib_v, send_sems, recv_sems, local_sems):
        x, y, c = lax.axis_index("x"), lax.axis_index("y"), lax.axis_index("c")
        copies = []
        for j in range(n_chip):
            own = pltpu.make_async_copy(s_ref.at[2 * j + c], own_v.at[j], local_sems.at[j])
            give = pltpu.make_async_remote_copy(
                src_ref=s_ref.at[2 * j + 1 - c], dst_ref=sib_v.at[j], send_sem=send_sems.at[j],
                recv_sem=recv_sems.at[j], device_id=(x, y, 1 - c), device_id_type=pl.DeviceIdType.MESH)
            own.start()
            give.start()
            copies.append((own, give))
        for j, (own, give) in enumerate(copies):
            own.wait()
            give.wait()
            o_ref[j] = (own_v[j].astype(F32) + sib_v[j].astype(F32)).astype(BF16)

    half = _sds((n_chip, rows, cols), slots.dtype)
    return pl.pallas_call(
        body, name="pair_reduce", in_specs=[pl.BlockSpec(memory_space=pl.ANY)],
        out_specs=pl.BlockSpec(memory_space=pltpu.VMEM), out_shape=half,
        scratch_shapes=[pltpu.VMEM(half.shape, half.dtype), pltpu.VMEM(half.shape, half.dtype),
                        pltpu.SemaphoreType.DMA((n_chip,)), pltpu.SemaphoreType.DMA((n_chip,)),
                        pltpu.SemaphoreType.DMA((n_chip,))],
        compiler_params=pltpu.CompilerParams(vmem_limit_bytes=VMEM_LIMIT, has_side_effects=True))(slots)


def _padded_col(c):
    if c < RAW_F:
        return c
    return FOFF + (c - RAW_F) if c < RAW_G else GOFF + (c - RAW_G)


def _shard_pieces():
    width = NA_RAW // N_DEV
    pieces = []
    for d in range(N_DEV):
        cuts = [width * d] + [c for c in (RAW_F, RAW_G) if width * d < c < width * (d + 1)] + [width * (d + 1)]
        for lo, hi in zip(cuts[:-1], cuts[1:]):
            pieces.append((d, lo - width * d, _padded_col(lo), hi - lo))
    return pieces


def _unshard_wa(wa_g):
    def body(w_ref, o_ref):
        o_ref[:, FOFF + N_HEADS:NA] = jnp.zeros((TM, NA - FOFF - N_HEADS), BF16)
        for d, src, dst, width in _shard_pieces():
            o_ref[:, dst:dst + width] = w_ref[d, :, src:src + width]

    return pl.pallas_call(
        body, name="unshard_wa", grid=(D // TM,),
        in_specs=[pl.BlockSpec((N_DEV, TM, NA_RAW // N_DEV), lambda i: (0, i, 0))],
        out_specs=pl.BlockSpec((TM, NA), lambda i: (i, 0)), out_shape=_sds((D, NA), BF16),
        compiler_params=_params())(wa_g)


def _reshard_dwa(dwa):
    def body(g_ref, o_ref):
        for d, src, dst, width in _shard_pieces():
            o_ref[d, :, src:src + width] = g_ref[:, dst:dst + width]

    return pl.pallas_call(
        body, name="reshard_dwa", grid=(D // TM,), in_specs=[pl.BlockSpec((TM, NA), lambda i: (i, 0))],
        out_specs=pl.BlockSpec((N_DEV, TM, NA_RAW // N_DEV), lambda i: (0, i, 0)),
        out_shape=_sds((N_DEV, D, NA_RAW // N_DEV), dwa.dtype), compiler_params=_params())(dwa)


CHIP_FLIPS = (2, 4, 6)


def _chip_exchange_start(partial):
    n = len(CHIP_FLIPS)

    def body(p_ref, land_ref, *rest):
        sends, recvs, token = rest[:n], rest[n:2 * n], rest[2 * n + 2]
        x, y, c = lax.axis_index("x"), lax.axis_index("y"), lax.axis_index("c")
        me = 4 * x + 2 * y + c
        for idx, k in enumerate(CHIP_FLIPS):
            pltpu.make_async_remote_copy(
                src_ref=p_ref.at[(me ^ k) >> 1], dst_ref=land_ref.at[me >> 1], send_sem=sends[idx],
                recv_sem=recvs[idx], device_id=(x ^ ((k >> 2) & 1), y ^ ((k >> 1) & 1), c),
                device_id_type=pl.DeviceIdType.MESH).start()
        token[...] = jnp.zeros_like(token)

    hbm = pl.BlockSpec(memory_space=pltpu.HBM)
    sem = pl.BlockSpec(memory_space=pltpu.SEMAPHORE)
    buf = pltpu.HBM(partial.shape, partial.dtype)
    return pl.pallas_call(
        body, name="chip_exchange_start",
        out_shape=(pltpu.SemaphoreType.DMA(()),) * (2 * n) + (buf, buf, _sds((8, LANES), F32)),
        in_specs=(hbm, hbm), out_specs=(sem,) * (2 * n) + (hbm, hbm, pl.BlockSpec(memory_space=pltpu.VMEM)),
        input_output_aliases={0: 2 * n, 1: 2 * n + 1},
        compiler_params=pltpu.CompilerParams(has_side_effects=pltpu.SideEffectType.DATAFLOW_SIDE_EFFECTING))(
            pltpu.with_memory_space_constraint(partial, pltpu.HBM),
            pltpu.with_memory_space_constraint(lax.empty(partial.shape, partial.dtype), pltpu.HBM))


def _chip_exchange_wait(started, after):
    n = len(CHIP_FLIPS)
    sems, (p_thru, land_thru) = started[:2 * n], started[2 * n:2 * n + 2]

    def body(p_ref, land_ref, *rest):
        sends, recvs = rest[:n], rest[n:2 * n]
        x, y, c = lax.axis_index("x"), lax.axis_index("y"), lax.axis_index("c")
        me = 4 * x + 2 * y + c
        for idx, k in enumerate(CHIP_FLIPS):
            copy = pltpu.make_async_remote_copy(
                src_ref=p_ref.at[(me ^ k) >> 1], dst_ref=land_ref.at[(me ^ k) >> 1], send_sem=sends[idx],
                recv_sem=recvs[idx], device_id=(x ^ ((k >> 2) & 1), y ^ ((k >> 1) & 1), c),
                device_id_type=pl.DeviceIdType.MESH)
            copy.wait_send()
            copy.wait_recv()

    hbm = pl.BlockSpec(memory_space=pltpu.HBM)
    sem = pl.BlockSpec(memory_space=pltpu.SEMAPHORE)
    buf = pltpu.HBM(p_thru.shape, p_thru.dtype)
    return pl.pallas_call(
        body, name="chip_exchange_wait", out_shape=(buf, buf),
        in_specs=(hbm, hbm) + (sem,) * (2 * n) + (pl.BlockSpec(memory_space=pl.ANY),), out_specs=(hbm, hbm),
        input_output_aliases={0: 0, 1: 1},
        compiler_params=pltpu.CompilerParams(has_side_effects=pltpu.SideEffectType.DATAFLOW_SIDE_EFFECTING))(
            p_thru, land_thru, *sems, after)


def _gather_slab(slab, after):
    def body(s_ref, after_ref, o_ref, *sems):
        _exchange_ops(["gather_rows"], [s_ref], [o_ref], sems, True, True)

    anyspec = pl.BlockSpec(memory_space=pl.ANY)
    return pl.pallas_call(
        body, name="gather_slab", in_specs=[anyspec, anyspec], out_specs=anyspec,
        out_shape=_sds((N_DEV,) + slab.shape, slab.dtype), scratch_shapes=_exchange_sems(1),
        compiler_params=pltpu.CompilerParams(has_side_effects=True))(slab, after)


def _adamw(w, g, m, v):
    m = ADAM_B1 * m + (1.0 - ADAM_B1) * g
    v = ADAM_B2 * v + (1.0 - ADAM_B2) * (g * g)
    m_hat = m / (1.0 - ADAM_B1 ** ADAM_STEP)
    v_hat = v / (1.0 - ADAM_B2 ** ADAM_STEP)
    delta = -ADAM_LR * (m_hat / (jnp.sqrt(v_hat) + ADAM_EPS) + ADAM_WD * w)
    return delta, m, v


def _sum_adamw(recv, w, m, v, name, after=None):
    lead = w.ndim - 2
    rows, cols = w.shape[-2:]
    tr = 256 if rows % 256 == 0 else 128
    slabs = list(recv) if isinstance(recv, tuple) else [recv]
    n_slots = slabs[0].shape[0]
    extra = [] if after is None else [after]

    def body(*refs):
        r_ref, own_ref = refs[0], refs[len(slabs) - 1]
        w_ref, m_ref, v_ref, g_ref, d_ref, nm_ref, nv_ref = refs[len(slabs) + len(extra):]
        chip = (4 * lax.axis_index("x") + 2 * lax.axis_index("y") + lax.axis_index("c")) >> 1
        g = None
        for slot in range(n_slots):
            part = r_ref[slot]
            if len(slabs) == 2:
                part = jnp.where(chip == slot, own_ref[slot], part)
            g = part.astype(F32) if g is None else g + part.astype(F32)
        g_ref[...] = g
        d_ref[...], nm_ref[...], nv_ref[...] = _adamw(w_ref[...], g, m_ref[...], v_ref[...])

    blk = pl.BlockSpec((None,) * lead + (tr, cols), lambda i: (0,) * lead + (i, 0))
    slots = pl.BlockSpec((n_slots, tr, cols), lambda i: (0, i, 0))
    return pl.pallas_call(
        body, name=name, grid=(rows // tr,),
        in_specs=[slots] * len(slabs) + [pl.BlockSpec(a.shape, lambda i: (0, 0)) for a in extra] + [blk, blk, blk],
        out_specs=[blk] * 4, out_shape=[_sds(w.shape, F32)] * 4, compiler_params=_params())(*slabs, *extra, w, m, v)


SLAB_ROWS = 16
SLOT = {"kv_norm_g": (8, 0, D), "norm_b_g": (9, 0, D), "b_forget": (10, 0, 16), "qnorm_a_g": (10, 128, HD),
        "knorm_a_g": (10, 256, HD), "knorm_b_g": (10, 384, HD), "qnorm_b_g": (10, 512, HD), "sinks": (10, 640, 16)}
SMALL = ["norm_a_g", "b_forget", "qnorm_a_g", "knorm_a_g", "kv_norm_g", "knorm_b_g", "norm_b_g", "qnorm_b_g", "sinks"]


LOSS_ROW = 11


def _pack_small(dg_a, dg_kv, dg_b, db_f, dgq_a, dgk_a, dgk_b, dgq_b, dsinks, lsum):
    def fold(ref):
        return ref[:, 0:HD] + ref[:, HD:2 * HD]

    def body(dga_ref, dgkv_ref, dgb_ref, dbf_ref, dgqa_ref, dgka_ref, dgkb_ref, dgqb_ref, dsk_ref, ls_ref, slab_ref):
        slab_ref[...] = jnp.zeros_like(slab_ref)
        for r in range(N_DEV):
            slab_ref[r:r + 1, 0:LANES] = dga_ref[:, LANES * r:LANES * (r + 1)]
        slab_ref[8:9, :] = dgkv_ref[...]
        slab_ref[9:10, :] = dgb_ref[...]
        slab_ref[10:11, 0:LANES] = dbf_ref[...]
        slab_ref[10:11, 128:128 + HD] = fold(dgqa_ref)
        slab_ref[10:11, 256:256 + HD] = fold(dgka_ref)
        slab_ref[10:11, 384:384 + HD] = fold(dgkb_ref)
        slab_ref[10:11, 512:512 + HD] = fold(dgqb_ref)
        slab_ref[10:11, 640:640 + LANES] = dsk_ref[...]
        slab_ref[LOSS_ROW:LOSS_ROW + 1, 0:LANES] = ls_ref[...]

    return pl.pallas_call(body, name="pack_small", out_shape=_sds((SLAB_ROWS, D), F32), compiler_params=_params())(
        dg_a, dg_kv, dg_b, db_f, dgq_a, dgk_a, dgk_b, dgq_b, dsinks, lsum)


def _small_adamw(recv, ws, ms, vs):
    k = len(SMALL)

    def body(*refs):
        r_ref = refs[0]
        w_refs, m_refs, v_refs = refs[1:1 + k], refs[1 + k:1 + 2 * k], refs[1 + 2 * k:1 + 3 * k]
        outs = refs[1 + 3 * k:1 + 7 * k]
        loss_ref, tot = refs[1 + 7 * k], refs[2 + 7 * k]
        g = r_ref[0]
        for dev in range(1, N_DEV):
            g = g + r_ref[dev]
        tot[...] = g
        loss_ref[...] = tot[LOSS_ROW:LOSS_ROW + 1, 0:LANES] * (0.5 / D)
        me = 4 * lax.axis_index("x") + 2 * lax.axis_index("y") + lax.axis_index("c")
        for p, name in enumerate(SMALL):
            if name == "norm_a_g":
                mine = lax.broadcasted_iota(jnp.int32, (N_DEV, LANES), 0) == me
                gp = jnp.sum(jnp.where(mine, tot[0:N_DEV, 0:LANES], 0.0), axis=0, keepdims=True)
            else:
                row, lo, width = SLOT[name]
                gp = tot[row:row + 1, lo:lo + width]
            d, nm, nv = _adamw(w_refs[p][...], gp, m_refs[p][...], v_refs[p][...])
            outs[p][...] = gp
            outs[k + p][...] = d
            outs[2 * k + p][...] = nm
            outs[3 * k + p][...] = nv

    shapes = [_sds(w.shape, F32) for w in ws]
    return pl.pallas_call(body, name="small_adamw", out_shape=shapes * 4 + [_sds((1, LANES), F32)],
                          scratch_shapes=[pltpu.VMEM((SLAB_ROWS, D), F32)],
                          compiler_params=_params())(recv, *ws, *ms, *vs)


def _rope_tables(positions):
    inv_freq = jnp.power(jnp.float32(ROPE_THETA), -jnp.arange(0, ROT, 2, dtype=F32) / ROT)
    ang = positions.astype(F32)[:, None] * inv_freq[None, :]
    cos, sin = jnp.cos(ang), jnp.sin(ang)
    c64 = jnp.concatenate([cos, cos, jnp.ones((S, HD - ROT), F32)], axis=-1)
    s64 = jnp.concatenate([-sin, sin, jnp.zeros((S, HD - ROT), F32)], axis=-1)
    return jnp.tile(c64, (1, 2)), jnp.tile(s64, (1, 2))


def _local_step(x, tgt, positions, g_a, wa, b_forget, gq_a, gk_a, g_kv, gk_b, g_b, gq_b, sinks,
                woa_s, wkv_s, wib_s, wob_s, adamw_others):
    nq = S // TQ
    cos2, sin2 = _rope_tables(positions)
    b_pad = jnp.pad(b_forget, ((0, 0), (0, LANES - N_HEADS)))

    u_a, proj, qn, kn, vb, ccol, cbc = _head_a(x, g_a, wa, gq_a, gk_a, b_pad)
    crow = ccol[:, :N_HEADS].T.reshape(N_HEADS, nq, 1, TQ)
    o_a, z_a, lse_a, woa_g, wkv_g, w_in_b, wob_g = _fox_fwd(
        qn, kn, vb, proj, crow, cbc,
        rider=[("gather_rows", woa_s), ("gather_rows", wkv_s), ("gather_cols", wib_s), ("gather_rows", wob_s)])
    w_out_a, w_kv, w_out_b = woa_g.reshape(D, D), wkv_g.reshape(D, 512), wob_g.reshape(D, D)
    h1, u_kv, u_b, kv, pb, qb, ksh, vsh = _head_b(x, z_a, w_out_a, g_kv, g_b, w_kv, w_in_b, gq_b, gk_b, cos2, sin2)
    sinks1 = sinks.reshape(N_HEADS)
    o_b, z_b, lse_b = _swa_fwd(qb, ksh, vsh, pb, sinks1)
    dy, lsum = _out_b_loss(z_b, w_out_b, h1, tgt)
    dw_out_b = _mm(z_b, dy, "tn", 512, 512, S, out_dtype=BF16, name="mm_dw_out_b")
    dz_b = _mm(dy, w_out_b, "nt", 1024, 512, D, name="mm_dz_b")
    dpb, dka, dkb, dva, dvb, dsinks, dgq_b = _swa_bwd(qb, ksh, vsh, dz_b, o_b, lse_b, pb, sinks1, gq_b, cos2, sin2)
    dkv, dgk_b = _prep_kv_bwd(dka, dkb, dva, dvb, kv, gk_b, cos2, sin2)
    dw_in_b = _mm(u_b, dpb, "tn", 512, 512, S, out_dtype=BF16, name="mm_dw_in_b")
    dw_kv = _mm(u_kv, dkv, "tn", 512, 512, S, out_dtype=BF16, name="mm_dw_kv")
    dh1, dg_b, dg_kv = _du_b_rms_bwd(dpb, w_in_b, dkv, w_kv, h1, g_b, g_kv, dy)
    dw_out_a = _mm(z_a, dh1, "tn", 512, 512, S, out_dtype=BF16, name="mm_dw_out_a")
    do_a, dgate_a, delta_a = _fox_bwd_pre(dh1, w_out_a, proj, o_a)
    dk_a, dv_a, dcs, dq_a, drow, r_wob, r_wib, r_wkv, r_woa = _fox_bwd(
        qn, kn, vb, do_a, lse_a, delta_a, crow, cbc,
        rider=[("a2a_rows", dw_out_b), ("a2a_cols", dw_in_b), ("a2a_rows", dw_kv), ("a2a_rows", dw_out_a)])
    drow_col = jnp.pad(drow.reshape(N_HEADS, S).T, ((0, 0), (0, LANES - N_HEADS)))
    dproj, dgq_a, dgk_a, db_f = _prep_a_bwd(dq_a, dk_a, dv_a, dgate_a, drow_col, dcs, proj, b_pad, gq_a, gk_a)
    dwa = _mm(u_a, dproj, "tn", 1024, 256, S, out_dtype=BF16, name="mm_dw_in_a")
    partial = _pair_reduce(_reshard_dwa(dwa))
    started = _chip_exchange_start(partial)
    dx, dg_a = _du_a_rms_bwd(dproj, wa, x, g_a, dh1, after=started[-1])
    others = adamw_others(dict(w_out_a=r_woa, w_kv=r_wkv, w_in_b=r_wib, w_out_b=r_wob), dg_a)
    partial, landed = _chip_exchange_wait(started, others["w_out_b"][0])
    slab = _pack_small(dg_a, dg_kv, dg_b, db_f, dgq_a, dgk_a, dgk_b, dgq_b, dsinks, lsum)
    return dx, (landed, partial), others, _gather_slab(slab, landed)


def kernel(x, positions, norm_a_g, w_in_a, b_forget, qnorm_a_g, knorm_a_g, w_out_a, kv_norm_g, w_kv, knorm_b_g, norm_b_g, w_in_b, qnorm_b_g, sinks, w_out_b, loss_target, m_norm_a_g, m_w_in_a, m_b_forget, m_qnorm_a_g, m_knorm_a_g, m_w_out_a, m_kv_norm_g, m_w_kv, m_knorm_b_g, m_norm_b_g, m_w_in_b, m_qnorm_b_g, m_sinks, m_w_out_b, v_norm_a_g, v_w_in_a, v_b_forget, v_qnorm_a_g, v_knorm_a_g, v_w_out_a, v_kv_norm_g, v_w_kv, v_knorm_b_g, v_norm_b_g, v_w_in_b, v_qnorm_b_g, v_sinks, v_w_out_b):
    wa_g, ga_g, woa_s, wkv_s, wib_s, wob_s = _gather_first(w_in_a, w_out_a, w_kv, w_in_b, w_out_b, norm_a_g)
    state = dict(w_in_a=(w_in_a, m_w_in_a, v_w_in_a), w_out_a=(w_out_a, m_w_out_a, v_w_out_a),
                 w_kv=(w_kv, m_w_kv, v_w_kv), w_in_b=(w_in_b, m_w_in_b, v_w_in_b),
                 w_out_b=(w_out_b, m_w_out_b, v_w_out_b))

    def adamw_others(landed, after):
        return {n: _sum_adamw(r, *state[n], "adamw_" + n, after=after) for n, r in landed.items()}

    dx, r_wa, big, slab_g = _local_step(
        x[0], loss_target[0], positions, ga_g.reshape(1, D), _unshard_wa(wa_g), b_forget, qnorm_a_g, knorm_a_g,
        kv_norm_g.reshape(1, D), knorm_b_g.reshape(1, HD), norm_b_g, qnorm_b_g, sinks, woa_s, wkv_s, wib_s, wob_s,
        adamw_others)
    big["w_in_a"] = _sum_adamw(r_wa, *state["w_in_a"], "adamw_w_in_a")

    r2 = lambda a: a.reshape(1, -1)
    small_w = dict(norm_a_g=norm_a_g, b_forget=b_forget, qnorm_a_g=qnorm_a_g, knorm_a_g=knorm_a_g,
                   kv_norm_g=kv_norm_g, knorm_b_g=knorm_b_g, norm_b_g=norm_b_g, qnorm_b_g=qnorm_b_g, sinks=sinks)
    small_m = dict(norm_a_g=m_norm_a_g, b_forget=m_b_forget, qnorm_a_g=m_qnorm_a_g, knorm_a_g=m_knorm_a_g,
                   kv_norm_g=m_kv_norm_g, knorm_b_g=m_knorm_b_g, norm_b_g=m_norm_b_g, qnorm_b_g=m_qnorm_b_g,
                   sinks=m_sinks)
    small_v = dict(norm_a_g=v_norm_a_g, b_forget=v_b_forget, qnorm_a_g=v_qnorm_a_g, knorm_a_g=v_knorm_a_g,
                   kv_norm_g=v_kv_norm_g, knorm_b_g=v_knorm_b_g, norm_b_g=v_norm_b_g, qnorm_b_g=v_qnorm_b_g,
                   sinks=v_sinks)
    res = _small_adamw(slab_g, [r2(small_w[n]) for n in SMALL], [r2(small_m[n]) for n in SMALL],
                       [r2(small_v[n]) for n in SMALL])
    k = len(SMALL)
    small = {n: [res[q * k + p].reshape(small_w[n].shape) for q in range(4)] for p, n in enumerate(SMALL)}
    loss = res[4 * k][0, 0]

    order = ["norm_a_g", "w_in_a", "b_forget", "qnorm_a_g", "knorm_a_g", "w_out_a", "kv_norm_g", "w_kv",
             "knorm_b_g", "norm_b_g", "w_in_b", "qnorm_b_g", "sinks", "w_out_b"]

    def leaf(n, q):
        return big[n][q] if n in big else small[n][q]

    outs = [loss, dx[None]]
    for q in range(4):
        outs.extend(leaf(n, q) for n in order)
    return tuple(outs)
```

```python
import jax
import jax.numpy as jnp
from jax import lax
from jax.experimental import pallas as pl
from jax.experimental.pallas import tpu as pltpu

F32, BF16 = jnp.float32, jnp.bfloat16

S = 2048
D = 1024
HD = 64
N_HEADS = 16
N_DEV = 8
NA = 4352
GOFF = 3072
FOFF = 4096
RAW_F = 3072
RAW_G = RAW_F + N_HEADS
NA_RAW = 4112
EPS = 1e-6
QSCALE = 0.125
ROPE_THETA = 500000.0
ROT = 16
WIN = 128
TQ = 256
TK = 256
KS = TQ // 2
HPS = 8
HW = HPS * HD
TM = 256
RT = 256
RB = 512
CB = 256
LANES = 128

ADAM_LR, ADAM_B1, ADAM_B2, ADAM_EPS, ADAM_WD, ADAM_STEP = 0.001, 0.9, 0.999, 1e-08, 0.01, 10

VMEM_LIMIT = 56 * 1024 * 1024


def _params():
    return pltpu.CompilerParams(vmem_limit_bytes=VMEM_LIMIT)


def _sds(shape, dtype):
    return jax.ShapeDtypeStruct(shape, dtype)


def _dot_nt(a, b):
    return lax.dot_general(a, b, (((1,), (1,)), ((), ())), preferred_element_type=F32)


def _dot_tn(a, b):
    return lax.dot_general(a, b, (((0,), (0,)), ((), ())), preferred_element_type=F32)


def _dot_nn(a, b):
    return lax.dot_general(a, b, (((1,), (0,)), ((), ())), preferred_element_type=F32)


def _sigmoid(g):
    return 1.0 / (1.0 + jnp.exp(-g))


def _lane_iota(shape):
    return lax.broadcasted_iota(jnp.int32, shape, len(shape) - 1)


def _flips(kind):
    return (2, 4, 6) if kind == "a2a_chips" else tuple(range(1, N_DEV))


def _send_view(kind, ref, dev):
    if kind in ("gather_rows", "gather_cols"):
        return ref
    if kind == "a2a_slots":
        return ref.at[dev]
    if kind == "a2a_chips":
        return ref.at[dev >> 1]
    if kind == "a2a_rows":
        rows = ref.shape[0] // N_DEV
        return ref.at[pl.ds(pl.multiple_of(dev * rows, rows), rows)]
    cols = ref.shape[1] // N_DEV
    return ref.at[:, pl.ds(pl.multiple_of(dev * cols, cols), cols)]


def _land_view(kind, ref, dev):
    if kind == "gather_cols":
        cols = ref.shape[1] // N_DEV
        return ref.at[:, pl.ds(pl.multiple_of(dev * cols, cols), cols)]
    if kind == "a2a_chips":
        return ref.at[dev >> 1]
    return ref.at[dev]


def _landing_sds(kind, arr):
    if kind == "gather_rows":
        return _sds((N_DEV,) + arr.shape, arr.dtype)
    if kind == "gather_cols":
        return _sds((arr.shape[0], N_DEV * arr.shape[1]), arr.dtype)
    if kind == "a2a_rows":
        return _sds((N_DEV, arr.shape[0] // N_DEV, arr.shape[1]), arr.dtype)
    if kind == "a2a_cols":
        return _sds((N_DEV, arr.shape[0], arr.shape[1] // N_DEV), arr.dtype)
    return _sds(arr.shape, arr.dtype)


def _exchange_sems(n_parts):
    n = n_parts * (N_DEV - 1)
    return [pltpu.SemaphoreType.DMA((n,)), pltpu.SemaphoreType.DMA((n,)), pltpu.SemaphoreType.DMA((n_parts,))]


def _exchange_ops(kinds, srcs, dsts, sems, start, wait):
    send_sems, recv_sems, local_sems = sems
    x, y, c = lax.axis_index("x"), lax.axis_index("y"), lax.axis_index("c")
    me = 4 * x + 2 * y + c

    def local(a):
        return pltpu.make_async_copy(_send_view(kinds[a], srcs[a], me), _land_view(kinds[a], dsts[a], me),
                                     local_sems.at[a])

    def remote(a, k, landing_dev):
        peer = (x ^ ((k >> 2) & 1), y ^ ((k >> 1) & 1), c ^ (k & 1))
        sem = a * (N_DEV - 1) + k - 1
        return pltpu.make_async_remote_copy(
            src_ref=_send_view(kinds[a], srcs[a], me ^ k), dst_ref=_land_view(kinds[a], dsts[a], landing_dev),
            send_sem=send_sems.at[sem], recv_sem=recv_sems.at[sem], device_id=peer,
            device_id_type=pl.DeviceIdType.MESH)

    pairs = [(a, k) for k in range(1, N_DEV) for a in range(len(kinds)) if k in _flips(kinds[a])]
    if start:
        for a in range(len(kinds)):
            local(a).start()
        for a, k in pairs:
            remote(a, k, me).start()
    if wait:
        for a, k in pairs:
            remote(a, k, me ^ k).wait_recv()
            remote(a, k, me).wait_send()
        for a in range(len(kinds)):
            local(a).wait()


def _gather_two_level(srcs, dsts, sems, meanwhile=None):
    send_sems, recv_sems, local_sems = sems
    x, y, c = lax.axis_index("x"), lax.axis_index("y"), lax.axis_index("c")
    me, sibling = (x, y, c), (x, y, 1 - c)
    chips = [(1 - x, y), (x, 1 - y), (1 - x, 1 - y)]

    def slot(ref, dev):
        return ref.at[4 * dev[0] + 2 * dev[1] + dev[2]]

    def copy(a, k, block, to, src=None):
        return pltpu.make_async_remote_copy(
            src_ref=slot(dsts[a], block) if src is None else src, dst_ref=slot(dsts[a], block),
            send_sem=send_sems.at[a * (N_DEV - 1) + k], recv_sem=recv_sems.at[a * (N_DEV - 1) + k],
            device_id=to, device_id_type=pl.DeviceIdType.MESH)

    parts = range(len(srcs))
    mine = [pltpu.make_async_copy(srcs[a], slot(dsts[a], me), local_sems.at[a]) for a in parts]
    first = [copy(a, 0, me, sibling, src=srcs[a]) for a in parts]
    first += [copy(a, 1 + j, me, (*chip, c), src=srcs[a]) for j, chip in enumerate(chips) for a in parts]
    for cp in mine + first:
        cp.start()
    if meanwhile is not None:
        meanwhile()
    passed = []
    for j, chip in enumerate(chips):
        for a in parts:
            copy(a, 1 + j, (*chip, c), me).wait_recv()
            fwd = copy(a, 4 + j, (*chip, c), sibling)
            fwd.start()
            passed.append(fwd)
    for a in parts:
        copy(a, 0, sibling, me).wait_recv()
        for j, chip in enumerate(chips):
            copy(a, 4 + j, (*chip, 1 - c), me).wait_recv()
    for cp in first + passed:
        cp.wait_send()
    for cp in mine:
        cp.wait()


def _call(body, *, name, args, in_specs, out_specs, out_shape, grid=(), scratch_shapes=(), aliases=None, rider=()):
    n_in, n_out, n_scr, n_r = len(in_specs), len(out_specs), len(scratch_shapes), len(rider)
    kinds = [kind for kind, _ in rider]

    def kernel_body(*refs):
        c_in, r_in = refs[:n_in], refs[n_in:n_in + n_r]
        c_out = refs[n_in + n_r:n_in + n_r + n_out]
        r_out = refs[n_in + n_r + n_out:n_in + 2 * n_r + n_out]
        rest = refs[n_in + 2 * n_r + n_out:]
        c_scr, sems = rest[:n_scr], rest[n_scr:]
        if n_r:
            assert grid, "a rider needs a gridded call"
            ids = [pl.program_id(ax) for ax in range(len(grid))]
            first, last = ids[0] == 0, ids[0] == grid[0] - 1
            for pid, size in zip(ids[1:], grid[1:]):
                first = first & (pid == 0)
                last = last & (pid == size - 1)
            pl.when(first)(lambda: _exchange_ops(kinds, r_in, r_out, sems, True, False))
        body(*c_in, *c_out, *c_scr)
        if n_r:
            pl.when(last)(lambda: _exchange_ops(kinds, r_in, r_out, sems, False, True))

    anyspec = pl.BlockSpec(memory_space=pl.ANY)
    params = pltpu.CompilerParams(vmem_limit_bytes=VMEM_LIMIT, has_side_effects=bool(n_r))
    outs = pl.pallas_call(
        kernel_body, name=name, grid=grid, in_specs=list(in_specs) + [anyspec] * n_r,
        out_specs=list(out_specs) + [anyspec] * n_r,
        out_shape=list(out_shape) + [_landing_sds(kind, arr) for kind, arr in rider],
        scratch_shapes=list(scratch_shapes) + (_exchange_sems(n_r) if n_r else []),
        input_output_aliases=aliases or {}, compiler_params=params)(*args, *[arr for _, arr in rider])
    return list(outs)


def _mm(a, b, mode, tm, tn, tk, out_dtype=F32, add=None, name="mm", rider=()):
    if mode == "nn":
        (m, k), n = a.shape, b.shape[1]
        a_spec = pl.BlockSpec((tm, tk), lambda i, j, kk: (i, kk))
        b_spec = pl.BlockSpec((tk, tn), lambda i, j, kk: (kk, j))
        dot = _dot_nn
    elif mode == "nt":
        (m, k), n = a.shape, b.shape[0]
        a_spec = pl.BlockSpec((tm, tk), lambda i, j, kk: (i, kk))
        b_spec = pl.BlockSpec((tn, tk), lambda i, j, kk: (j, kk))
        dot = _dot_nt
    else:
        (k, m), n = a.shape, b.shape[1]
        a_spec = pl.BlockSpec((tk, tm), lambda i, j, kk: (kk, i))
        b_spec = pl.BlockSpec((tk, tn), lambda i, j, kk: (kk, j))
        dot = _dot_tn
    assert m % tm == 0 and n % tn == 0 and k % tk == 0, (m, n, k, tm, tn, tk)
    nk = k // tk
    has_add = add is not None

    def body(*refs):
        if has_add:
            a_ref, b_ref, add_ref, o_ref, acc = refs
        else:
            a_ref, b_ref, o_ref, acc = refs
        p = dot(a_ref[...].astype(BF16), b_ref[...].astype(BF16))

        def finish(total):
            if has_add:
                total = add_ref[...] + total
            o_ref[...] = total.astype(out_dtype)

        if nk == 1:
            finish(p)
        else:
            kk = pl.program_id(2)

            @pl.when(kk == 0)
            def _():
                acc[...] = p

            @pl.when(kk > 0)
            def _():
                acc[...] += p

            @pl.when(kk == nk - 1)
            def _():
                finish(acc[...])

    in_specs = [a_spec, b_spec]
    args = [a, b]
    if has_add:
        in_specs.append(pl.BlockSpec((tm, tn), lambda i, j, kk: (i, j)))
        args.append(add)
    acc_shape = (tm, tn) if nk > 1 else (8, LANES)
    outs = _call(body, name=name, args=args, grid=(m // tm, n // tn, nk), in_specs=in_specs,
                 out_specs=[pl.BlockSpec((tm, tn), lambda i, j, kk: (i, j))], out_shape=[_sds((m, n), out_dtype)],
                 scratch_shapes=[pltpu.VMEM(acc_shape, F32)], rider=rider)
    return outs if rider else outs[0]


def _rms_rinv(x):
    return lax.rsqrt(jnp.mean(x * x, axis=-1, keepdims=True) + EPS)


def _rms_bwd_core(du, x, g):
    r = _rms_rinv(x)
    dug = du * g
    dx = r * (dug - x * ((r * r) * jnp.mean(dug * x, axis=-1, keepdims=True)))
    dg = jnp.sum(du * (x * r), axis=0, keepdims=True)
    return dx, dg


def _half_sum(v, lo_half):
    s0 = jnp.sum(jnp.where(lo_half, v, 0.0), axis=-1, keepdims=True)
    s1 = jnp.sum(jnp.where(lo_half, 0.0, v), axis=-1, keepdims=True)
    return jnp.where(lo_half, s0, s1)


def _head_rinv(x, lo_half):
    return lax.rsqrt(_half_sum(x * x, lo_half) * (1.0 / HD) + EPS)


def _head_norm_bwd(dn, x, g, lo_half):
    r = _head_rinv(x, lo_half)
    dng = dn * g
    dx = r * (dng - x * ((r * r) * (_half_sum(dng * x, lo_half) * (1.0 / HD))))
    dg = jnp.sum(dn * (x * r), axis=0, keepdims=True)
    return dx, dg


def _rope_swap(x, lane):
    l64 = lane & (HD - 1)
    return jnp.where(l64 < ROT // 2, pltpu.roll(x, LANES - ROT // 2, 1), pltpu.roll(x, ROT // 2, 1))


def _rope_fwd(x, cos, sin, lane):
    return x * cos + _rope_swap(x, lane) * sin


def _rope_bwd(dy, cos, sin, lane):
    return dy * cos + jnp.where((lane & (HD - 1)) < ROT, _rope_swap(dy * sin, lane), 0.0)


def _g2(g_ref):
    g = g_ref[...]
    return jnp.concatenate([g, g], axis=-1)


def _pairs(width):
    return [slice(LANES * c, LANES * (c + 1)) for c in range(width // LANES)]


def _pick_lane(block, lane, idx):
    return jnp.sum(jnp.where(lane == idx, block, 0.0), axis=-1, keepdims=True)


def _head_a(x, g, wa, gq, gk, b_pad):
    def body(x_ref, g_ref, w_ref, gq_ref, gk_ref, b_ref, u_ref, p_ref, qo_ref, ko_ref, vo_ref, c_ref, cbc_ref, carry):
        @pl.when(pl.program_id(0) == 0)
        def _():
            carry[...] = jnp.zeros_like(carry)

        xv = x_ref[...]
        u = ((xv * _rms_rinv(xv)) * g_ref[...]).astype(BF16)
        u_ref[...] = u
        for lo in range(0, NA, D):
            hi = min(lo + D, NA)
            p_ref[:, lo:hi] = _dot_nn(u, w_ref[:, lo:hi])
        lane = _lane_iota((RT, LANES))
        lo_half = lane < HD
        gq2, gk2 = _g2(gq_ref), _g2(gk_ref)
        for c in _pairs(D):
            q = p_ref[:, c]
            k = p_ref[:, D + c.start:D + c.stop]
            qo_ref[:, c] = (((q * _head_rinv(q, lo_half)) * gq2) * QSCALE).astype(BF16)
            ko_ref[:, c] = ((k * _head_rinv(k, lo_half)) * gk2).astype(BF16)
        vo_ref[...] = p_ref[:, 2 * D:3 * D].astype(BF16)

        z = p_ref[:, FOFF:FOFF + LANES] + b_ref[...]
        logf = jnp.minimum(z, 0.0) - jnp.log1p(jnp.exp(-jnp.abs(z)))
        r = lax.broadcasted_iota(jnp.int32, (RT, RT), 0)
        cc = lax.broadcasted_iota(jnp.int32, (RT, RT), 1)
        tri = (r >= cc).astype(F32)
        loc = jnp.dot(tri, logf, precision=lax.Precision.HIGHEST, preferred_element_type=F32) + carry[0:1, :]
        c_ref[...] = loc
        carry[0:1, :] = loc[RT - 1:RT, :]
        for h in range(N_HEADS):
            cbc_ref[:, LANES * h:LANES * (h + 1)] = jnp.broadcast_to(_pick_lane(loc, lane, h), (RT, LANES))

    row = lambda width: pl.BlockSpec((RT, width), lambda i: (i, 0))
    whole = lambda arr: pl.BlockSpec(arr.shape, lambda i: (0,) * arr.ndim, pipeline_mode=pl.Buffered(1))
    return pl.pallas_call(
        body, name="head_a", grid=(S // RT,),
        in_specs=[row(D), whole(g), whole(wa), whole(gq), whole(gk), whole(b_pad)],
        out_specs=[row(D), row(NA), row(D), row(D), row(D), row(LANES), row(N_HEADS * LANES)],
        out_shape=[_sds((S, D), BF16), _sds((S, NA), F32)] + [_sds((S, D), BF16)] * 3
        + [_sds((S, LANES), F32), _sds((S, N_HEADS * LANES), F32)],
        scratch_shapes=[pltpu.VMEM((8, LANES), F32)], compiler_params=_params())(x, g, wa, gq, gk, b_pad)


def _key_le_query(offset, keys=TK):
    r = lax.broadcasted_iota(jnp.int32, (keys, TQ), 0)
    c = lax.broadcasted_iota(jnp.int32, (keys, TQ), 1)
    return (r + offset) <= c


def _widen(tile):
    return jnp.concatenate([tile] * (TQ // LANES), axis=1)


def _fox_fwd(qn, kn, vb, proj, crow, cbc, rider=()):
    nq = S // TQ

    def body(q_ref, k_ref, v_ref, g_ref, cq_ref, cbc_ref, o_ref, z_ref, lse_ref, st_s, pt_s):
        i = pl.program_id(1)
        qs = [q_ref[:, HD * hh:HD * (hh + 1)] for hh in range(HPS)]
        cqs = [cq_ref[hh, 0] for hh in range(HPS)]

        def scores(s, hh):
            off = pl.multiple_of(s * KS, KS)
            kj = k_ref[pl.ds(off, KS), HD * hh:HD * (hh + 1)]
            return (_dot_nt(kj, qs[hh]) + cqs[hh]) - _widen(cbc_ref[pl.ds(off, KS), LANES * hh:LANES * (hh + 1)])

        def values(s, hh, pt):
            off = pl.multiple_of(s * KS, KS)
            return _dot_tn(v_ref[pl.ds(off, KS), HD * hh:HD * (hh + 1)], pt)

        def step(s, slot, carries, mask=None, last=False):
            if not last:
                for hh in range(HPS):
                    st_s[1 - slot, hh] = scores(s + 1, hh)
            pvs = [values(jnp.maximum(s - 1, 0), hh, pt_s[1 - slot, hh]) for hh in range(HPS)]
            out = []
            for hh in range(HPS):
                m, l, acc = carries[hh]
                st = st_s[slot, hh]
                if mask is not None:
                    st = jnp.where(mask, st, -jnp.inf)
                m_new = jnp.maximum(m, jnp.max(st, axis=0, keepdims=True))
                pt = jnp.exp(st - m_new)
                alpha = jnp.exp(m - m_new)
                pt_s[slot, hh] = pt.astype(BF16)
                out.append((m_new, alpha * l + jnp.sum(pt, axis=0, keepdims=True), alpha * (acc + pvs[hh])))
            return tuple(out)

        for hh in range(HPS):
            st_s[0, hh] = scores(0, hh)
            pt_s[1, hh] = jnp.zeros((KS, TQ), BF16)
        one = (jnp.full((1, TQ), -jnp.inf, F32), jnp.zeros((1, TQ), F32), jnp.zeros((HD, TQ), F32))
        carries = lax.fori_loop(0, i, lambda t, cr: step(2 * t + 1, 1, step(2 * t, 0, cr)), (one,) * HPS)
        carries = step(2 * i, 0, carries, mask=_key_le_query(0, KS))
        carries = step(2 * i + 1, 1, carries, mask=_key_le_query(KS, KS), last=True)
        accs = []
        for hh in range(HPS):
            m, l, acc = carries[hh]
            acc = acc + values(2 * i + 1, hh, pt_s[1, hh])
            accs.append(acc / l)
            lse_ref[hh, 0] = m + jnp.log(l)
        o = jnp.concatenate(accs, axis=0).T
        o_ref[...] = o
        g = g_ref[...]
        z_ref[...] = (o * (g * _sigmoid(g))).astype(BF16)

    qblk = pl.BlockSpec((TQ, HW), lambda hp, i: (i, hp))
    full = pl.BlockSpec((S, HW), lambda hp, i: (0, hp))
    rows = pl.BlockSpec((HPS, 1, 1, TQ), lambda hp, i: (hp, i, 0, 0))
    return _call(
        body, name="fox_fwd", args=(qn, kn, vb, proj, crow, cbc), grid=(N_HEADS // HPS, nq),
        in_specs=[qblk, full, full,
                  pl.BlockSpec((TQ, HW), lambda hp, i: (i, GOFF // HW + hp)),
                  rows, pl.BlockSpec((S, HPS * LANES), lambda hp, i: (0, hp))],
        out_specs=[qblk, qblk, rows],
        out_shape=[_sds((S, D), F32), _sds((S, D), BF16), _sds((N_HEADS, nq, 1, TQ), F32)],
        scratch_shapes=[pltpu.VMEM((2, HPS, KS, TQ), F32), pltpu.VMEM((2, HPS, KS, TQ), BF16)], rider=rider)


def _fox_bwd_pre(dh, w_out, proj, o):
    nq, rows = S // TQ, 2 * TQ
    per = rows // TQ

    def body(dh_ref, w_ref, g_ref, o_ref, do_ref, dg_ref, delta_ref):
        g = g_ref[...]
        sg = _sigmoid(g)
        dzv = _dot_nt(dh_ref[...].astype(BF16), w_ref[...])
        ov = o_ref[...]
        do = dzv * (g * sg)
        dg_ref[...] = (dzv * ov * (sg * (1.0 + g * (1.0 - sg)))).astype(BF16)
        do_ref[...] = do.astype(BF16)
        prod_t = (do * ov).T
        for h in range(N_HEADS):
            for b in range(per):
                delta_ref[h, b] = jnp.sum(prod_t[HD * h:HD * (h + 1), TQ * b:TQ * (b + 1)], axis=0, keepdims=True)

    row = pl.BlockSpec((rows, D), lambda i: (i, 0))
    return pl.pallas_call(
        body, name="fox_bwd_pre", grid=(S // rows,),
        in_specs=[row, pl.BlockSpec(w_out.shape, lambda i: (0, 0), pipeline_mode=pl.Buffered(1)),
                  pl.BlockSpec((rows, D), lambda i: (i, GOFF // D)), row],
        out_specs=[row, row, pl.BlockSpec((N_HEADS, per, 1, TQ), lambda i: (0, i, 0, 0))],
        out_shape=[_sds((S, D), BF16), _sds((S, D), BF16), _sds((N_HEADS, nq, 1, TQ), F32)],
        compiler_params=_params())(dh, w_out, proj, o)


def _fox_bwd(qn, kn, vb, dob, lse, delta, crow, cbc, rider=()):
    nq, nkb = S // TQ, S // TK

    def body(q_ref, k_ref, v_ref, do_ref, lse_ref, del_ref, cq_ref, cbc_ref,
             dk_ref, dv_ref, dcs_ref, dq_ref, dr_ref, st_s, dp_s, pt_s, ds_s, dq_acc, dr_acc):
        j = pl.program_id(1)

        @pl.when(j == 0)
        def _():
            dq_acc[...] = jnp.zeros_like(dq_acc)
            dr_acc[...] = jnp.zeros_like(dr_acc)

        kjs = [k_ref[:, HD * hh:HD * (hh + 1)] for hh in range(HPS)]
        vjs = [v_ref[:, HD * hh:HD * (hh + 1)] for hh in range(HPS)]

        def rows_of(ref, u, hh):
            off = pl.multiple_of(u * TQ, TQ)
            return ref[pl.ds(off, TQ), HD * hh:HD * (hh + 1)]

        def products(u, hh):
            st = (_dot_nt(kjs[hh], rows_of(q_ref, u, hh)) + cq_ref[hh, u]) - _widen(
                cbc_ref[:, LANES * hh:LANES * (hh + 1)])
            return st, _dot_nt(vjs[hh], rows_of(do_ref, u, hh))

        def step(u, slot, carries, masked=False):
            nxt = jnp.minimum(u + 1, nq - 1)
            for hh in range(HPS):
                st_s[1 - slot, hh], dp_s[1 - slot, hh] = products(nxt, hh)
            prev = jnp.maximum(u - 1, 0)
            dvs = [_dot_nn(pt_s[1 - slot, hh], rows_of(do_ref, prev, hh)) for hh in range(HPS)]
            dks = [_dot_nn(ds_s[1 - slot, hh], rows_of(q_ref, prev, hh)) for hh in range(HPS)]
            for hh in range(HPS):
                dq_acc[hh, prev] += _dot_tn(kjs[hh], ds_s[1 - slot, hh])
            out = []
            for hh in range(HPS):
                dk, dv, dcs = carries[hh]
                st = st_s[slot, hh]
                if masked:
                    st = jnp.where(_key_le_query((j - u) * TQ), st, -jnp.inf)
                pt = jnp.exp(st - lse_ref[hh, u])
                dst = pt * (dp_s[slot, hh] - del_ref[hh, u])
                pt_s[slot, hh] = pt.astype(BF16)
                ds_s[slot, hh] = dst.astype(BF16)
                dr_acc[hh, u] += jnp.sum(dst, axis=0, keepdims=True)
                out.append((dk + dks[hh], dv + dvs[hh], dcs + (dst[:, :LANES] + dst[:, LANES:])))
            return tuple(out)

        t0 = j // 2
        for hh in range(HPS):
            st_s[0, hh], dp_s[0, hh] = products(2 * t0, hh)
            pt_s[1, hh] = jnp.zeros((TK, TQ), BF16)
            ds_s[1, hh] = jnp.zeros((TK, TQ), BF16)
        one = (jnp.zeros((TK, HD), F32), jnp.zeros((TK, HD), F32), jnp.zeros((TK, LANES), F32))
        carries = step(2 * t0 + 1, 1, step(2 * t0, 0, (one,) * HPS, masked=True), masked=True)
        carries = lax.fori_loop(t0 + 1, nq // 2, lambda t, cr: step(2 * t + 1, 1, step(2 * t, 0, cr)), carries)
        dks, dvs = [], []
        lane = _lane_iota((TK, LANES))
        dcs_all = jnp.zeros((TK, LANES), F32)
        for hh in range(HPS):
            dk, dv, dcs = carries[hh]
            dks.append(dk + _dot_nn(ds_s[1, hh], rows_of(q_ref, nq - 1, hh)))
            dvs.append(dv + _dot_nn(pt_s[1, hh], rows_of(do_ref, nq - 1, hh)))
            dq_acc[hh, nq - 1] += _dot_tn(kjs[hh], ds_s[1, hh])
            dcs_all = jnp.where(lane == HPS * pl.program_id(0) + hh, -jnp.sum(dcs, axis=-1, keepdims=True), dcs_all)
        dcs_ref[0] = dcs_all
        dk_ref[...] = jnp.concatenate(dks, axis=-1)
        dv_ref[...] = jnp.concatenate(dvs, axis=-1).astype(BF16)

        @pl.when(j == nkb - 1)
        def _():
            for i in range(nq):
                dq_ref[TQ * i:TQ * (i + 1), :] = jnp.concatenate([dq_acc[hh, i] for hh in range(HPS)], axis=0).T
            dr_ref[...] = dr_acc[...]

    kblk = pl.BlockSpec((TK, HW), lambda hp, j: (j, hp))
    full = pl.BlockSpec((S, HW), lambda hp, j: (0, hp))
    rows = pl.BlockSpec((HPS, nq, 1, TQ), lambda hp, j: (hp, 0, 0, 0))
    cblk = pl.BlockSpec((TK, HPS * LANES), lambda hp, j: (j, hp))
    return _call(
        body, name="fox_bwd", args=(qn, kn, vb, dob, lse, delta, crow, cbc), grid=(N_HEADS // HPS, nkb),
        in_specs=[full, kblk, kblk, full, rows, rows, rows, cblk],
        out_specs=[kblk, kblk, pl.BlockSpec((1, TK, LANES), lambda hp, j: (hp, j, 0)), full, rows],
        out_shape=[_sds((S, D), F32), _sds((S, D), BF16), _sds((N_HEADS // HPS, S, LANES), F32), _sds((S, D), F32),
                   _sds((N_HEADS, nq, 1, TQ), F32)],
        scratch_shapes=[pltpu.VMEM((2, HPS, TK, TQ), F32), pltpu.VMEM((2, HPS, TK, TQ), F32),
                        pltpu.VMEM((2, HPS, TK, TQ), BF16), pltpu.VMEM((2, HPS, TK, TQ), BF16),
                        pltpu.VMEM((HPS, nq, HD, TQ), F32), pltpu.VMEM((HPS, nq, 1, TQ), F32)], rider=rider)


def _prep_a_bwd(dq, dk, dv, dgate, drow, dcs, proj, b_pad, gq, gk):
    nt = S // TM

    def body(dq_ref, dk_ref, dv_ref, dgt_ref, dr_ref, dcs_ref, xq_ref, xk_ref, f_ref, b_ref, gq_ref, gk_ref,
             o_ref, dgq_ref, dgk_ref, db_ref, carry):
        @pl.when(pl.program_id(0) == 0)
        def _():
            carry[...] = jnp.zeros_like(carry)
            dgq_ref[...] = jnp.zeros_like(dgq_ref)
            dgk_ref[...] = jnp.zeros_like(dgk_ref)
            db_ref[...] = jnp.zeros_like(db_ref)

        lane = _lane_iota((TM, LANES))
        lo_half = lane < HD
        gq2, gk2 = _g2(gq_ref), _g2(gk_ref)
        dgq, dgk = jnp.zeros((1, LANES), F32), jnp.zeros((1, LANES), F32)
        for c in _pairs(D):
            dxq, dg = _head_norm_bwd(dq_ref[:, c] * QSCALE, xq_ref[:, c], gq2, lo_half)
            o_ref[:, c] = dxq.astype(BF16)
            dgq = dgq + dg
            dxk, dg = _head_norm_bwd(dk_ref[:, c], xk_ref[:, c], gk2, lo_half)
            o_ref[:, D + c.start:D + c.stop] = dxk.astype(BF16)
            dgk = dgk + dg
        dgq_ref[...] += dgq
        dgk_ref[...] += dgk
        o_ref[:, 2 * D:3 * D] = dv_ref[...]
        o_ref[:, GOFF:GOFF + D] = dgt_ref[...]

        dc = dr_ref[...]
        for group in range(N_HEADS // HPS):
            dc = dc + dcs_ref[group]
        r = lax.broadcasted_iota(jnp.int32, (TM, TM), 0)
        c = lax.broadcasted_iota(jnp.int32, (TM, TM), 1)
        tri = (c >= r).astype(F32)
        dlogf = jnp.dot(tri, dc, precision=lax.Precision.HIGHEST, preferred_element_type=F32) + carry[0:1, :]
        carry[0:1, :] = dlogf[0:1, :]
        df = dlogf * (1.0 / (1.0 + jnp.exp(f_ref[...] + b_ref[...])))
        db_ref[...] += jnp.sum(df, axis=0, keepdims=True)
        o_ref[:, FOFF:FOFF + LANES] = df.astype(BF16)
        o_ref[:, FOFF + LANES:NA] = jnp.zeros((TM, NA - FOFF - LANES), BF16)

    rev = lambda width, col: pl.BlockSpec((TM, width), lambda i: (nt - 1 - i, col))
    gspec = pl.BlockSpec((1, HD), lambda i: (0, 0))
    acc = pl.BlockSpec((1, LANES), lambda i: (0, 0))
    return pl.pallas_call(
        body, name="prep_a_bwd", grid=(nt,),
        in_specs=[rev(D, 0), rev(D, 0), rev(D, 0), rev(D, 0), rev(LANES, 0),
                  pl.BlockSpec((N_HEADS // HPS, TM, LANES), lambda i: (0, nt - 1 - i, 0)),
                  rev(D, 0), rev(D, 1), rev(LANES, FOFF // LANES), acc, gspec, gspec],
        out_specs=[rev(NA, 0), acc, acc, acc],
        out_shape=[_sds((S, NA), BF16)] + [_sds((1, LANES), F32)] * 3,
        scratch_shapes=[pltpu.VMEM((8, LANES), F32)],
        compiler_params=_params())(dq, dk, dv, dgate, drow, dcs, proj, proj, proj, b_pad, gq, gk)


def _head_b(x, z_a, w_out_a, g_kv, g_b, w_kv, w_in_b, gq, gk, cos2, sin2):
    nkv = w_kv.shape[1] // 2

    def body(x_ref, z_ref, wo_ref, gkv_ref, gb_ref, wkv_ref, wb_ref, gq_ref, gk_ref, c_ref, s_ref,
             h_ref, ukv_ref, ub_ref, kv_ref, pb_ref, qo_ref, ko_ref, vo_ref):
        xv = x_ref[...] + _dot_nn(z_ref[...], wo_ref[...])
        h_ref[...] = xv
        xn = xv * _rms_rinv(xv)
        ukv = (xn * gkv_ref[...]).astype(BF16)
        ub = (xn * gb_ref[...]).astype(BF16)
        ukv_ref[...] = ukv
        ub_ref[...] = ub
        kv_ref[...] = _dot_nn(ukv, wkv_ref[...])
        for lo in range(0, 2 * D, D):
            pb_ref[:, lo:lo + D] = _dot_nn(ub, wb_ref[:, lo:lo + D])
        lane = _lane_iota((RT, LANES))
        lo_half = lane < HD
        cos, sin = c_ref[...], s_ref[...]
        gq2, gk2 = _g2(gq_ref), _g2(gk_ref)
        for c in _pairs(D):
            q = pb_ref[:, c]
            qo_ref[:, c] = (_rope_fwd((q * _head_rinv(q, lo_half)) * gq2, cos, sin, lane) * QSCALE).astype(BF16)
        for c in _pairs(nkv):
            k = kv_ref[:, c]
            ko_ref[:, c] = _rope_fwd((k * _head_rinv(k, lo_half)) * gk2, cos, sin, lane).astype(BF16)
        vo_ref[...] = kv_ref[:, nkv:2 * nkv].astype(BF16)

    row = lambda width: pl.BlockSpec((RT, width), lambda i: (i, 0))
    whole = lambda arr: pl.BlockSpec(arr.shape, lambda i: (0,) * arr.ndim, pipeline_mode=pl.Buffered(1))
    return pl.pallas_call(
        body, name="head_b", grid=(S // RT,),
        in_specs=[row(D), row(D), whole(w_out_a), whole(g_kv), whole(g_b), whole(w_kv), whole(w_in_b), whole(gq),
                  whole(gk), row(LANES), row(LANES)],
        out_specs=[row(D), row(D), row(D), row(2 * nkv), row(2 * D), row(D), row(nkv), row(nkv)],
        out_shape=[_sds((S, D), F32), _sds((S, D), BF16), _sds((S, D), BF16), _sds((S, 2 * nkv), F32),
                   _sds((S, 2 * D), F32), _sds((S, D), BF16), _sds((S, nkv), BF16), _sds((S, nkv), BF16)],
        compiler_params=_params())(x, z_a, w_out_a, g_kv, g_b, w_kv, w_in_b, gq, gk, cos2, sin2)


N_KV, GRP = 4, 4


def _swa_mask(n):
    r = lax.broadcasted_iota(jnp.int32, (2 * WIN, GRP * WIN), 0)
    q = lax.broadcasted_iota(jnp.int32, (2 * WIN, GRP * WIN), 1) & (WIN - 1)
    return (r > q) & (r <= q + WIN) & ((r >= WIN) | (n > 0))


def _stack4(ref_or_val, base):
    return jnp.concatenate([ref_or_val[:, base + HD * g: base + HD * (g + 1)] for g in range(GRP)], axis=0)


def _unstack4(xt):
    return jnp.concatenate([xt[:, WIN * g:WIN * (g + 1)] for g in range(GRP)], axis=0).T


def _band(prev_ref, cur_ref, kh):
    return jnp.concatenate([prev_ref[:, HD * kh:HD * (kh + 1)], cur_ref[:, HD * kh:HD * (kh + 1)]], axis=0)


def _sink_row(s_ref, first):
    lane = _lane_iota((1, GRP * WIN))
    row = jnp.full((1, GRP * WIN), s_ref[first + GRP - 1], F32)
    for g in range(GRP - 2, -1, -1):
        row = jnp.where(lane < WIN * (g + 1), s_ref[first + g], row)
    return row


def _swa_fwd(qb, ksh, vsh, pb, sinks):
    nb = S // WIN

    def body(q_ref, kp_ref, kc_ref, vp_ref, vc_ref, g_ref, s_ref, o_ref, z_ref, lse_ref):
        n = pl.program_id(0)
        valid = _swa_mask(n)
        outs = []
        for kh in range(N_KV):
            kb, vb = _band(kp_ref, kc_ref, kh), _band(vp_ref, vc_ref, kh)
            st = jnp.where(valid, _dot_nt(kb, _stack4(q_ref, GRP * HD * kh)), -jnp.inf)
            sink = _sink_row(s_ref, GRP * kh)
            m = jnp.maximum(jnp.max(st, axis=0, keepdims=True), sink)
            pt = jnp.exp(st - m)
            l = jnp.sum(pt, axis=0, keepdims=True) + jnp.exp(sink - m)
            outs.append(_unstack4(_dot_tn(vb, pt.astype(BF16)) / l))
            lse = m + jnp.log(l)
            for g in range(GRP):
                lse_ref[GRP * kh + g, 0] = lse[:, WIN * g:WIN * (g + 1)]
        o = jnp.concatenate(outs, axis=-1)
        o_ref[...] = o
        g = g_ref[...]
        z_ref[...] = (o * (g * _sigmoid(g))).astype(BF16)

    row = pl.BlockSpec((WIN, D), lambda n: (n, 0))
    prev = pl.BlockSpec((WIN, N_KV * HD), lambda n: (jnp.maximum(n - 1, 0), 0))
    cur = pl.BlockSpec((WIN, N_KV * HD), lambda n: (n, 0))
    return pl.pallas_call(
        body, name="swa_fwd", grid=(nb,),
        in_specs=[row, prev, cur, prev, cur, pl.BlockSpec((WIN, D), lambda n: (n, 1)),
                  pl.BlockSpec(memory_space=pltpu.SMEM)],
        out_specs=[row, row, pl.BlockSpec((N_HEADS, 1, 1, WIN), lambda n: (0, n, 0, 0))],
        out_shape=[_sds((S, D), F32), _sds((S, D), BF16), _sds((N_HEADS, nb, 1, WIN), F32)],
        compiler_params=_params())(qb, ksh, ksh, vsh, vsh, pb, sinks)


def _swa_bwd(qb, ksh, vsh, dz, o, lse, pb, sinks, gq, cos2, sin2):
    nb = S // WIN

    def body(q_ref, kp_ref, kc_ref, vp_ref, vc_ref, dz_ref, o_ref, lse_ref, x_ref, g_ref, s_ref, gq_ref, c_ref, sn_ref,
             dpb_ref, dka_ref, dkb_ref, dva_ref, dvb_ref, dsink_ref, dgq_ref):
        n = pl.program_id(0)

        @pl.when(n == 0)
        def _():
            dsink_ref[...] = jnp.zeros_like(dsink_ref)
            dgq_ref[...] = jnp.zeros_like(dgq_ref)

        valid = _swa_mask(n)
        g = g_ref[...]
        sg = _sigmoid(g)
        dzv = dz_ref[...]
        ov = o_ref[...]
        do = dzv * (g * sg)
        dpb_ref[:, D:2 * D] = (dzv * ov * (sg * (1.0 + g * (1.0 - sg)))).astype(BF16)
        prod_t = (do * ov).T
        lane1 = _lane_iota((1, LANES))
        dqs, dkas, dkbs, dvas, dvbs = [], [], [], [], []
        dsink = jnp.zeros((1, LANES), F32)
        for kh in range(N_KV):
            kb, vb = _band(kp_ref, kc_ref, kh), _band(vp_ref, vc_ref, kh)
            base = GRP * HD * kh
            qs = _stack4(q_ref, base)
            dos = _stack4(do, base).astype(BF16)
            delta = jnp.concatenate(
                [jnp.sum(prod_t[base + HD * gg:base + HD * (gg + 1), :], axis=0, keepdims=True)
                 for gg in range(GRP)], axis=1)
            lse = jnp.concatenate([lse_ref[GRP * kh + gg, 0] for gg in range(GRP)], axis=1)
            st = jnp.where(valid, _dot_nt(kb, qs), -jnp.inf)
            pt = jnp.exp(st - lse)
            dst = pt * (_dot_nt(vb, dos) - delta)
            dsb = dst.astype(BF16)
            dqs.append(_unstack4(_dot_tn(kb, dsb)))
            dkband = _dot_nn(dsb, qs)
            dvband = _dot_nn(pt.astype(BF16), dos)
            dkbs.append(dkband[0:WIN, :])
            dkas.append(dkband[WIN:2 * WIN, :])
            dvbs.append(dvband[0:WIN, :])
            dvas.append(dvband[WIN:2 * WIN, :])
            ps_delta = jnp.exp(_sink_row(s_ref, GRP * kh) - lse) * delta
            for gg in range(GRP):
                val = jnp.sum(ps_delta[:, WIN * gg:WIN * (gg + 1)], axis=1, keepdims=True)
                dsink = dsink - jnp.where(lane1 == GRP * kh + gg, val, 0.0)
        dka_ref[...] = jnp.concatenate(dkas, axis=-1)
        dkb_ref[...] = jnp.concatenate(dkbs, axis=-1)
        dva_ref[...] = jnp.concatenate(dvas, axis=-1)
        dvb_ref[...] = jnp.concatenate(dvbs, axis=-1)
        dsink_ref[...] += dsink

        lane = _lane_iota((WIN, LANES))
        g2, cos, sin = _g2(gq_ref), c_ref[...], sn_ref[...]
        dg_tot = jnp.zeros((1, LANES), F32)
        for kh in range(N_KV):
            for c in _pairs(GRP * HD):
                cols = slice(GRP * HD * kh + c.start, GRP * HD * kh + c.stop)
                dn = _rope_bwd(dqs[kh][:, c] * QSCALE, cos, sin, lane)
                dx, dg = _head_norm_bwd(dn, x_ref[:, cols], g2, lane < HD)
                dpb_ref[:, cols] = dx.astype(BF16)
                dg_tot = dg_tot + dg
        dgq_ref[...] += dg_tot

    row = pl.BlockSpec((WIN, D), lambda n: (n, 0))
    prev = pl.BlockSpec((WIN, N_KV * HD), lambda n: (jnp.maximum(n - 1, 0), 0))
    cur = pl.BlockSpec((WIN, N_KV * HD), lambda n: (n, 0))
    acc = pl.BlockSpec((1, LANES), lambda n: (0, 0))
    tab = pl.BlockSpec((WIN, LANES), lambda n: (n, 0))
    return pl.pallas_call(
        body, name="swa_bwd", grid=(nb,),
        in_specs=[row, prev, cur, prev, cur, row, row, pl.BlockSpec((N_HEADS, 1, 1, WIN), lambda n: (0, n, 0, 0)),
                  row, pl.BlockSpec((WIN, D), lambda n: (n, 1)), pl.BlockSpec(memory_space=pltpu.SMEM),
                  pl.BlockSpec((1, HD), lambda n: (0, 0)), tab, tab],
        out_specs=[pl.BlockSpec((WIN, 2 * D), lambda n: (n, 0)), cur, cur, cur, cur, acc, acc],
        out_shape=[_sds((S, 2 * D), BF16)] + [_sds((S, 256), F32)] * 4 + [_sds((1, LANES), F32)] * 2,
        compiler_params=_params())(qb, ksh, ksh, vsh, vsh, dz, o, lse, pb, pb, sinks, gq, cos2, sin2)


def _prep_kv_bwd(dka, dkb, dva, dvb, kv, gk, cos2, sin2):
    nt = S // RB
    per = RB // WIN

    def shifted(cur_ref, nxt_ref, has_next):
        return jnp.concatenate([cur_ref[WIN:RB, :], jnp.where(has_next, nxt_ref[...], 0.0)], axis=0)

    def body(dka_ref, dkb_ref, dkn_ref, dva_ref, dvb_ref, dvn_ref, x_ref, g_ref, c_ref, s_ref, o_ref, dgk_ref):
        i, j = pl.program_id(0), pl.program_id(1)

        @pl.when((i == 0) & (j == 0))
        def _():
            dgk_ref[...] = jnp.zeros_like(dgk_ref)

        has_next = i < nt - 1

        @pl.when(j == 0)
        def _():
            lane = _lane_iota((RB, LANES))
            g2, cos, sin = _g2(g_ref), c_ref[...], s_ref[...]
            dy_all = dka_ref[...] + shifted(dkb_ref, dkn_ref, has_next)
            dg_tot = jnp.zeros((1, LANES), F32)
            for c in _pairs(CB):
                dn = _rope_bwd(dy_all[:, c], cos, sin, lane)
                dx, dg = _head_norm_bwd(dn, x_ref[:, c], g2, lane < HD)
                o_ref[:, c] = dx.astype(BF16)
                dg_tot = dg_tot + dg
            dgk_ref[...] += dg_tot

        @pl.when(j == 1)
        def _():
            o_ref[...] = (dva_ref[...] + shifted(dvb_ref, dvn_ref, has_next)).astype(BF16)

    cur = pl.BlockSpec((RB, CB), lambda i, j: (i, 0))
    nxt = pl.BlockSpec((WIN, CB), lambda i, j: (jnp.minimum(per * (i + 1), S // WIN - 1), 0))
    tab = pl.BlockSpec((RB, LANES), lambda i, j: (i, 0))
    return pl.pallas_call(
        body, name="prep_kv_bwd", grid=(nt, 2),
        in_specs=[cur, cur, nxt, cur, cur, nxt, cur, pl.BlockSpec((1, HD), lambda i, j: (0, 0)), tab, tab],
        out_specs=[pl.BlockSpec((RB, CB), lambda i, j: (i, j)), pl.BlockSpec((1, LANES), lambda i, j: (0, 0))],
        out_shape=[_sds((S, 2 * CB), BF16), _sds((1, LANES), F32)],
        compiler_params=_params())(dka, dkb, dkb, dva, dvb, dvb, kv, gk, cos2, sin2)


def _out_b_loss(z, w_out, h1, tgt):
    tm = 2 * RT

    def body(z_ref, w_ref, h_ref, t_ref, dy_ref, l_ref):
        @pl.when(pl.program_id(0) == 0)
        def _():
            l_ref[...] = jnp.zeros_like(l_ref)

        e = (h_ref[...] + _dot_nn(z_ref[...], w_ref[...])) - t_ref[...]
        dy_ref[...] = e * (1.0 / D)
        l_ref[...] += jnp.sum(jnp.sum(e * e, axis=-1, keepdims=True), axis=0, keepdims=True)

    row = pl.BlockSpec((tm, D), lambda i: (i, 0))
    return pl.pallas_call(
        body, name="out_b_loss", grid=(S // tm,),
        in_specs=[row, pl.BlockSpec(w_out.shape, lambda i: (0, 0), pipeline_mode=pl.Buffered(1)), row, row],
        out_specs=[row, pl.BlockSpec((1, LANES), lambda i: (0, 0))],
        out_shape=[_sds((S, D), F32), _sds((1, LANES), F32)], compiler_params=_params())(z, w_out, h1, tgt)


def _du_a_rms_bwd(dproj, wa, x, g, dres, after):
    tm, tk = 1024, NA // 2
    nk = NA // tk

    def body(a_ref, b_ref, x_ref, g_ref, dr_ref, after_ref, dx_ref, dg_ref, acc):
        i, kk = pl.program_id(0), pl.program_id(1)

        @pl.when((i == 0) & (kk == 0))
        def _():
            dg_ref[...] = jnp.zeros_like(dg_ref)

        p = _dot_nt(a_ref[...], b_ref[...])

        @pl.when(kk == 0)
        def _():
            acc[...] = p

        @pl.when(kk == nk - 1)
        def _():
            dx, dg = _rms_bwd_core(acc[...] + p, x_ref[...], g_ref[...])
            dx_ref[...] = dr_ref[...] + dx
            dg_ref[...] += dg

    assert nk == 2
    row = pl.BlockSpec((tm, D), lambda i, kk: (i, 0))
    vec = pl.BlockSpec((1, D), lambda i, kk: (0, 0))
    return pl.pallas_call(
        body, name="du_a_rms_bwd", grid=(S // tm, nk),
        in_specs=[pl.BlockSpec((tm, tk), lambda i, kk: (i, kk)), pl.BlockSpec((D, tk), lambda i, kk: (0, kk)),
                  row, vec, row, pl.BlockSpec(after.shape, lambda i, kk: (0, 0))],
        out_specs=[row, vec], out_shape=[_sds((S, D), F32), _sds((1, D), F32)],
        scratch_shapes=[pltpu.VMEM((tm, D), F32)], compiler_params=_params())(dproj, wa, x, g, dres, after)


def _du_b_rms_bwd(dpb, w_in_b, dkv, w_kv, h1, g_b, g_kv, dy):
    tm = 2 * RT

    def body(ab_ref, wb_ref, akv_ref, wkv_ref, x_ref, gb_ref, gkv_ref, dy_ref, dh_ref, dgb_ref, dgkv_ref):
        @pl.when(pl.program_id(0) == 0)
        def _():
            dgb_ref[...] = jnp.zeros_like(dgb_ref)
            dgkv_ref[...] = jnp.zeros_like(dgkv_ref)

        x = x_ref[...]
        dx1, dg1 = _rms_bwd_core(_dot_nt(ab_ref[...], wb_ref[...]), x, gb_ref[...])
        dx2, dg2 = _rms_bwd_core(_dot_nt(akv_ref[...], wkv_ref[...]), x, gkv_ref[...])
        dh_ref[...] = dy_ref[...] + dx1 + dx2
        dgb_ref[...] += dg1
        dgkv_ref[...] += dg2

    row = lambda width: pl.BlockSpec((tm, width), lambda i: (i, 0))
    whole = lambda arr: pl.BlockSpec(arr.shape, lambda i: (0, 0), pipeline_mode=pl.Buffered(1))
    vec = pl.BlockSpec((1, D), lambda i: (0, 0))
    return pl.pallas_call(
        body, name="du_b_rms_bwd", grid=(S // tm,),
        in_specs=[row(dpb.shape[1]), whole(w_in_b), row(dkv.shape[1]), whole(w_kv), row(D), vec, vec, row(D)],
        out_specs=[row(D), vec, vec], out_shape=[_sds((S, D), F32), _sds((1, D), F32), _sds((1, D), F32)],
        compiler_params=_params())(dpb, w_in_b, dkv, w_kv, h1, g_b, g_kv, dy)


def _gather_first(w_in_a, w_out_a, w_kv, w_in_b, w_out_b, norm_a_g):
    def body(wia_ref, woa_ref, wkv_ref, wib_ref, wob_ref, ga_ref,
             wa_g, ga_g, woa_s, wkv_s, wib_s, wob_s, wa_s, st_a, st_oa, st_kv, st_ib, st_ob, load_sems, *sems):
        sources = [wia_ref.at[0], woa_ref.at[0], wkv_ref, wib_ref.at[0], wob_ref.at[0]]
        stages = [st_a, st_oa, st_kv, st_ib, st_ob]
        loads = [pltpu.make_async_copy(src, dst, load_sems.at[i]) for i, (src, dst) in enumerate(zip(sources, stages))]
        for cp in loads:
            cp.start()
        loads[0].wait()
        wa_s[...] = st_a[...].astype(BF16)

        def cast_the_rest():
            for cp, stage, out in zip(loads[1:], stages[1:], [woa_s, wkv_s, wib_s, wob_s]):
                cp.wait()
                out[...] = stage[...].astype(BF16)

        _gather_two_level([wa_s, ga_ref], [wa_g, ga_g], sems, meanwhile=cast_the_rest)

    vmem = pl.BlockSpec(memory_space=pltpu.VMEM)
    anyspec = pl.BlockSpec(memory_space=pl.ANY)
    shard = lambda w: _sds(w.shape[-2:], BF16)
    stage = lambda w: pltpu.VMEM(w.shape[-2:], F32)
    return pl.pallas_call(
        body, name="gather_first", in_specs=[anyspec] * 5 + [vmem],
        out_specs=[anyspec, anyspec, vmem, vmem, vmem, vmem],
        out_shape=[_sds((N_DEV,) + w_in_a.shape[-2:], BF16), _sds((N_DEV,) + norm_a_g.shape, F32),
                   shard(w_out_a), shard(w_kv), shard(w_in_b), shard(w_out_b)],
        scratch_shapes=[pltpu.VMEM(w_in_a.shape[-2:], BF16), stage(w_in_a), stage(w_out_a), stage(w_kv),
                        stage(w_in_b), stage(w_out_b), pltpu.SemaphoreType.DMA((5,))] + _exchange_sems(2),
        compiler_params=pltpu.CompilerParams(vmem_limit_bytes=VMEM_LIMIT, has_side_effects=True))(
            w_in_a, w_out_a, w_kv, w_in_b, w_out_b, norm_a_g)


def _pair_reduce(slots):
    n_chip = N_DEV // 2
    _, rows, cols = slots.shape

    def body(s_ref, o_ref, own_v, sib_v, send_sems, recv_sems, local_sems):
        x, y, c = lax.axis_index("x"), lax.axis_index("y"), lax.axis_index("c")
        copies = []
        for j in range(n_chip):
            own = pltpu.make_async_copy(s_ref.at[2 * j + c], own_v.at[j], local_sems.at[j])
            give = pltpu.make_async_remote_copy(
                src_ref=s_ref.at[2 * j + 1 - c], dst_ref=sib_v.at[j], send_sem=send_sems.at[j],
                recv_sem=recv_sems.at[j], device_id=(x, y, 1 - c), device_id_type=pl.DeviceIdType.MESH)
            own.start()
            give.start()
            copies.append((own, give))
        for j, (own, give) in enumerate(copies):
            own.wait()
            give.wait()
            o_ref[j] = (own_v[j].astype(F32) + sib_v[j].astype(F32)).astype(BF16)

    half = _sds((n_chip, rows, cols), slots.dtype)
    return pl.pallas_call(
        body, name="pair_reduce", in_specs=[pl.BlockSpec(memory_space=pl.ANY)],
        out_specs=pl.BlockSpec(memory_space=pltpu.VMEM), out_shape=half,
        scratch_shapes=[pltpu.VMEM(half.shape, half.dtype), pltpu.VMEM(half.shape, half.dtype),
                        pltpu.SemaphoreType.DMA((n_chip,)), pltpu.SemaphoreType.DMA((n_chip,)),
                        pltpu.SemaphoreType.DMA((n_chip,))],
        compiler_params=pltpu.CompilerParams(vmem_limit_bytes=VMEM_LIMIT, has_side_effects=True))(slots)


def _padded_col(c):
    if c < RAW_F:
        return c
    return FOFF + (c - RAW_F) if c < RAW_G else GOFF + (c - RAW_G)


def _shard_pieces():
    width = NA_RAW // N_DEV
    pieces = []
    for d in range(N_DEV):
        cuts = [width * d] + [c for c in (RAW_F, RAW_G) if width * d < c < width * (d + 1)] + [width * (d + 1)]
        for lo, hi in zip(cuts[:-1], cuts[1:]):
            pieces.append((d, lo - width * d, _padded_col(lo), hi - lo))
    return pieces


def _unshard_wa(wa_g):
    def body(w_ref, o_ref):
        o_ref[:, FOFF + N_HEADS:NA] = jnp.zeros((TM, NA - FOFF - N_HEADS), BF16)
        for d, src, dst, width in _shard_pieces():
            o_ref[:, dst:dst + width] = w_ref[d, :, src:src + width]

    return pl.pallas_call(
        body, name="unshard_wa", grid=(D // TM,),
        in_specs=[pl.BlockSpec((N_DEV, TM, NA_RAW // N_DEV), lambda i: (0, i, 0))],
        out_specs=pl.BlockSpec((TM, NA), lambda i: (i, 0)), out_shape=_sds((D, NA), BF16),
        compiler_params=_params())(wa_g)


def _reshard_dwa(dwa):
    def body(g_ref, o_ref):
        for d, src, dst, width in _shard_pieces():
            o_ref[d, :, src:src + width] = g_ref[:, dst:dst + width]

    return pl.pallas_call(
        body, name="reshard_dwa", grid=(D // TM,), in_specs=[pl.BlockSpec((TM, NA), lambda i: (i, 0))],
        out_specs=pl.BlockSpec((N_DEV, TM, NA_RAW // N_DEV), lambda i: (0, i, 0)),
        out_shape=_sds((N_DEV, D, NA_RAW // N_DEV), dwa.dtype), compiler_params=_params())(dwa)


CHIP_FLIPS = (2, 4, 6)


def _chip_exchange_start(partial):
    n = len(CHIP_FLIPS)

    def body(p_ref, land_ref, *rest):
        sends, recvs, token = rest[:n], rest[n:2 * n], rest[2 * n + 2]
        x, y, c = lax.axis_index("x"), lax.axis_index("y"), lax.axis_index("c")
        me = 4 * x + 2 * y + c
        for idx, k in enumerate(CHIP_FLIPS):
            pltpu.make_async_remote_copy(
                src_ref=p_ref.at[(me ^ k) >> 1], dst_ref=land_ref.at[me >> 1], send_sem=sends[idx],
                recv_sem=recvs[idx], device_id=(x ^ ((k >> 2) & 1), y ^ ((k >> 1) & 1), c),
                device_id_type=pl.DeviceIdType.MESH).start()
        token[...] = jnp.zeros_like(token)

    hbm = pl.BlockSpec(memory_space=pltpu.HBM)
    sem = pl.BlockSpec(memory_space=pltpu.SEMAPHORE)
    buf = pltpu.HBM(partial.shape, partial.dtype)
    return pl.pallas_call(
        body, name="chip_exchange_start",
        out_shape=(pltpu.SemaphoreType.DMA(()),) * (2 * n) + (buf, buf, _sds((8, LANES), F32)),
        in_specs=(hbm, hbm), out_specs=(sem,) * (2 * n) + (hbm, hbm, pl.BlockSpec(memory_space=pltpu.VMEM)),
        input_output_aliases={0: 2 * n, 1: 2 * n + 1},
        compiler_params=pltpu.CompilerParams(has_side_effects=pltpu.SideEffectType.DATAFLOW_SIDE_EFFECTING))(
            pltpu.with_memory_space_constraint(partial, pltpu.HBM),
            pltpu.with_memory_space_constraint(lax.empty(partial.shape, partial.dtype), pltpu.HBM))


def _chip_exchange_wait(started, after):
    n = len(CHIP_FLIPS)
    sems, (p_thru, land_thru) = started[:2 * n], started[2 * n:2 * n + 2]

    def body(p_ref, land_ref, *rest):
        sends, recvs = rest[:n], rest[n:2 * n]
        x, y, c = lax.axis_index("x"), lax.axis_index("y"), lax.axis_index("c")
        me = 4 * x + 2 * y + c
        for idx, k in enumerate(CHIP_FLIPS):
            copy = pltpu.make_async_remote_copy(
                src_ref=p_ref.at[(me ^ k) >> 1], dst_ref=land_ref.at[(me ^ k) >> 1], send_sem=sends[idx],
                recv_sem=recvs[idx], device_id=(x ^ ((k >> 2) & 1), y ^ ((k >> 1) & 1), c),
                device_id_type=pl.DeviceIdType.MESH)
            copy.wait_send()
            copy.wait_recv()

    hbm = pl.BlockSpec(memory_space=pltpu.HBM)
    sem = pl.BlockSpec(memory_space=pltpu.SEMAPHORE)
    buf = pltpu.HBM(p_thru.shape, p_thru.dtype)
    return pl.pallas_call(
        body, name="chip_exchange_wait", out_shape=(buf, buf),
        in_specs=(hbm, hbm) + (sem,) * (2 * n) + (pl.BlockSpec(memory_space=pl.ANY),), out_specs=(hbm, hbm),
        input_output_aliases={0: 0, 1: 1},
        compiler_params=pltpu.CompilerParams(has_side_effects=pltpu.SideEffectType.DATAFLOW_SIDE_EFFECTING))(
            p_thru, land_thru, *sems, after)


def _gather_slab(slab, after):
    def body(s_ref, after_ref, o_ref, *sems):
        _exchange_ops(["gather_rows"], [s_ref], [o_ref], sems, True, True)

    anyspec = pl.BlockSpec(memory_space=pl.ANY)
    return pl.pallas_call(
        body, name="gather_slab", in_specs=[anyspec, anyspec], out_specs=anyspec,
        out_shape=_sds((N_DEV,) + slab.shape, slab.dtype), scratch_shapes=_exchange_sems(1),
        compiler_params=pltpu.CompilerParams(has_side_effects=True))(slab, after)


def _adamw(w, g, m, v):
    m = ADAM_B1 * m + (1.0 - ADAM_B1) * g
    v = ADAM_B2 * v + (1.0 - ADAM_B2) * (g * g)
    m_hat = m / (1.0 - ADAM_B1 ** ADAM_STEP)
    v_hat = v / (1.0 - ADAM_B2 ** ADAM_STEP)
    delta = -ADAM_LR * (m_hat / (jnp.sqrt(v_hat) + ADAM_EPS) + ADAM_WD * w)
    return delta, m, v


def _sum_adamw(recv, w, m, v, name, after=None):
    lead = w.ndim - 2
    rows, cols = w.shape[-2:]
    tr = 256 if rows % 256 == 0 else 128
    slabs = list(recv) if isinstance(recv, tuple) else [recv]
    n_slots = slabs[0].shape[0]
    extra = [] if after is None else [after]

    def body(*refs):
        r_ref, own_ref = refs[0], refs[len(slabs) - 1]
        w_ref, m_ref, v_ref, g_ref, d_ref, nm_ref, nv_ref = refs[len(slabs) + len(extra):]
        chip = (4 * lax.axis_index("x") + 2 * lax.axis_index("y") + lax.axis_index("c")) >> 1
        g = None
        for slot in range(n_slots):
            part = r_ref[slot]
            if len(slabs) == 2:
                part = jnp.where(chip == slot, own_ref[slot], part)
            g = part.astype(F32) if g is None else g + part.astype(F32)
        g_ref[...] = g
        d_ref[...], nm_ref[...], nv_ref[...] = _adamw(w_ref[...], g, m_ref[...], v_ref[...])

    blk = pl.BlockSpec((None,) * lead + (tr, cols), lambda i: (0,) * lead + (i, 0))
    slots = pl.BlockSpec((n_slots, tr, cols), lambda i: (0, i, 0))
    return pl.pallas_call(
        body, name=name, grid=(rows // tr,),
        in_specs=[slots] * len(slabs) + [pl.BlockSpec(a.shape, lambda i: (0, 0)) for a in extra] + [blk, blk, blk],
        out_specs=[blk] * 4, out_shape=[_sds(w.shape, F32)] * 4, compiler_params=_params())(*slabs, *extra, w, m, v)


SLAB_ROWS = 16
SLOT = {"kv_norm_g": (8, 0, D), "norm_b_g": (9, 0, D), "b_forget": (10, 0, 16), "qnorm_a_g": (10, 128, HD),
        "knorm_a_g": (10, 256, HD), "knorm_b_g": (10, 384, HD), "qnorm_b_g": (10, 512, HD), "sinks": (10, 640, 16)}
SMALL = ["norm_a_g", "b_forget", "qnorm_a_g", "knorm_a_g", "kv_norm_g", "knorm_b_g", "norm_b_g", "qnorm_b_g", "sinks"]


LOSS_ROW = 11


def _pack_small(dg_a, dg_kv, dg_b, db_f, dgq_a, dgk_a, dgk_b, dgq_b, dsinks, lsum):
    def fold(ref):
        return ref[:, 0:HD] + ref[:, HD:2 * HD]

    def body(dga_ref, dgkv_ref, dgb_ref, dbf_ref, dgqa_ref, dgka_ref, dgkb_ref, dgqb_ref, dsk_ref, ls_ref, slab_ref):
        slab_ref[...] = jnp.zeros_like(slab_ref)
        for r in range(N_DEV):
            slab_ref[r:r + 1, 0:LANES] = dga_ref[:, LANES * r:LANES * (r + 1)]
        slab_ref[8:9, :] = dgkv_ref[...]
        slab_ref[9:10, :] = dgb_ref[...]
        slab_ref[10:11, 0:LANES] = dbf_ref[...]
        slab_ref[10:11, 128:128 + HD] = fold(dgqa_ref)
        slab_ref[10:11, 256:256 + HD] = fold(dgka_ref)
        slab_ref[10:11, 384:384 + HD] = fold(dgkb_ref)
        slab_ref[10:11, 512:512 + HD] = fold(dgqb_ref)
        slab_ref[10:11, 640:640 + LANES] = dsk_ref[...]
        slab_ref[LOSS_ROW:LOSS_ROW + 1, 0:LANES] = ls_ref[...]

    return pl.pallas_call(body, name="pack_small", out_shape=_sds((SLAB_ROWS, D), F32), compiler_params=_params())(
        dg_a, dg_kv, dg_b, db_f, dgq_a, dgk_a, dgk_b, dgq_b, dsinks, lsum)


def _small_adamw(recv, ws, ms, vs):
    k = len(SMALL)

    def body(*refs):
        r_ref = refs[0]
        w_refs, m_refs, v_refs = refs[1:1 + k], refs[1 + k:1 + 2 * k], refs[1 + 2 * k:1 + 3 * k]
        outs = refs[1 + 3 * k:1 + 7 * k]
        loss_ref, tot = refs[1 + 7 * k], refs[2 + 7 * k]
        g = r_ref[0]
        for dev in range(1, N_DEV):
            g = g + r_ref[dev]
        tot[...] = g
        loss_ref[...] = tot[LOSS_ROW:LOSS_ROW + 1, 0:LANES] * (0.5 / D)
        me = 4 * lax.axis_index("x") + 2 * lax.axis_index("y") + lax.axis_index("c")
        for p, name in enumerate(SMALL):
            if name == "norm_a_g":
                mine = lax.broadcasted_iota(jnp.int32, (N_DEV, LANES), 0) == me
                gp = jnp.sum(jnp.where(mine, tot[0:N_DEV, 0:LANES], 0.0), axis=0, keepdims=True)
            else:
                row, lo, width = SLOT[name]
                gp = tot[row:row + 1, lo:lo + width]
            d, nm, nv = _adamw(w_refs[p][...], gp, m_refs[p][...], v_refs[p][...])
            outs[p][...] = gp
            outs[k + p][...] = d
            outs[2 * k + p][...] = nm
            outs[3 * k + p][...] = nv

    shapes = [_sds(w.shape, F32) for w in ws]
    return pl.pallas_call(body, name="small_adamw", out_shape=shapes * 4 + [_sds((1, LANES), F32)],
                          scratch_shapes=[pltpu.VMEM((SLAB_ROWS, D), F32)],
                          compiler_params=_params())(recv, *ws, *ms, *vs)


def _rope_tables(positions):
    inv_freq = jnp.power(jnp.float32(ROPE_THETA), -jnp.arange(0, ROT, 2, dtype=F32) / ROT)
    ang = positions.astype(F32)[:, None] * inv_freq[None, :]
    cos, sin = jnp.cos(ang), jnp.sin(ang)
    c64 = jnp.concatenate([cos, cos, jnp.ones((S, HD - ROT), F32)], axis=-1)
    s64 = jnp.concatenate([-sin, sin, jnp.zeros((S, HD - ROT), F32)], axis=-1)
    return jnp.tile(c64, (1, 2)), jnp.tile(s64, (1, 2))


def _local_step(x, tgt, positions, g_a, wa, b_forget, gq_a, gk_a, g_kv, gk_b, g_b, gq_b, sinks,
                woa_s, wkv_s, wib_s, wob_s, adamw_others):
    nq = S // TQ
    cos2, sin2 = _rope_tables(positions)
    b_pad = jnp.pad(b_forget, ((0, 0), (0, LANES - N_HEADS)))

    u_a, proj, qn, kn, vb, ccol, cbc = _head_a(x, g_a, wa, gq_a, gk_a, b_pad)
    crow = ccol[:, :N_HEADS].T.reshape(N_HEADS, nq, 1, TQ)
    o_a, z_a, lse_a, woa_g, wkv_g, w_in_b, wob_g = _fox_fwd(
        qn, kn, vb, proj, crow, cbc,
        rider=[("gather_rows", woa_s), ("gather_rows", wkv_s), ("gather_cols", wib_s), ("gather_rows", wob_s)])
    w_out_a, w_kv, w_out_b = woa_g.reshape(D, D), wkv_g.reshape(D, 512), wob_g.reshape(D, D)
    h1, u_kv, u_b, kv, pb, qb, ksh, vsh = _head_b(x, z_a, w_out_a, g_kv, g_b, w_kv, w_in_b, gq_b, gk_b, cos2, sin2)
    sinks1 = sinks.reshape(N_HEADS)
    o_b, z_b, lse_b = _swa_fwd(qb, ksh, vsh, pb, sinks1)
    dy, lsum = _out_b_loss(z_b, w_out_b, h1, tgt)
    dw_out_b = _mm(z_b, dy, "tn", 512, 512, S, out_dtype=BF16, name="mm_dw_out_b")
    dz_b = _mm(dy, w_out_b, "nt", 1024, 512, D, name="mm_dz_b")
    dpb, dka, dkb, dva, dvb, dsinks, dgq_b = _swa_bwd(qb, ksh, vsh, dz_b, o_b, lse_b, pb, sinks1, gq_b, cos2, sin2)
    dkv, dgk_b = _prep_kv_bwd(dka, dkb, dva, dvb, kv, gk_b, cos2, sin2)
    dw_in_b = _mm(u_b, dpb, "tn", 512, 512, S, out_dtype=BF16, name="mm_dw_in_b")
    dw_kv = _mm(u_kv, dkv, "tn", 512, 512, S, out_dtype=BF16, name="mm_dw_kv")
    dh1, dg_b, dg_kv = _du_b_rms_bwd(dpb, w_in_b, dkv, w_kv, h1, g_b, g_kv, dy)
    dw_out_a = _mm(z_a, dh1, "tn", 512, 512, S, out_dtype=BF16, name="mm_dw_out_a")
    do_a, dgate_a, delta_a = _fox_bwd_pre(dh1, w_out_a, proj, o_a)
    dk_a, dv_a, dcs, dq_a, drow, r_wob, r_wib, r_wkv, r_woa = _fox_bwd(
        qn, kn, vb, do_a, lse_a, delta_a, crow, cbc,
        rider=[("a2a_rows", dw_out_b), ("a2a_cols", dw_in_b), ("a2a_rows", dw_kv), ("a2a_rows", dw_out_a)])
    drow_col = jnp.pad(drow.reshape(N_HEADS, S).T, ((0, 0), (0, LANES - N_HEADS)))
    dproj, dgq_a, dgk_a, db_f = _prep_a_bwd(dq_a, dk_a, dv_a, dgate_a, drow_col, dcs, proj, b_pad, gq_a, gk_a)
    dwa = _mm(u_a, dproj, "tn", 1024, 256, S, out_dtype=BF16, name="mm_dw_in_a")
    partial = _pair_reduce(_reshard_dwa(dwa))
    started = _chip_exchange_start(partial)
    dx, dg_a = _du_a_rms_bwd(dproj, wa, x, g_a, dh1, after=started[-1])
    others = adamw_others(dict(w_out_a=r_woa, w_kv=r_wkv, w_in_b=r_wib, w_out_b=r_wob), dg_a)
    partial, landed = _chip_exchange_wait(started, others["w_out_b"][0])
    slab = _pack_small(dg_a, dg_kv, dg_b, db_f, dgq_a, dgk_a, dgk_b, dgq_b, dsinks, lsum)
    return dx, (landed, partial), others, _gather_slab(slab, landed)


def kernel(x, positions, norm_a_g, w_in_a, b_forget, qnorm_a_g, knorm_a_g, w_out_a, kv_norm_g, w_kv, knorm_b_g, norm_b_g, w_in_b, qnorm_b_g, sinks, w_out_b, loss_target, m_norm_a_g, m_w_in_a, m_b_forget, m_qnorm_a_g, m_knorm_a_g, m_w_out_a, m_kv_norm_g, m_w_kv, m_knorm_b_g, m_norm_b_g, m_w_in_b, m_qnorm_b_g, m_sinks, m_w_out_b, v_norm_a_g, v_w_in_a, v_b_forget, v_qnorm_a_g, v_knorm_a_g, v_w_out_a, v_kv_norm_g, v_w_kv, v_knorm_b_g, v_norm_b_g, v_w_in_b, v_qnorm_b_g, v_sinks, v_w_out_b):
    wa_g, ga_g, woa_s, wkv_s, wib_s, wob_s = _gather_first(w_in_a, w_out_a, w_kv, w_in_b, w_out_b, norm_a_g)
    state = dict(w_in_a=(w_in_a, m_w_in_a, v_w_in_a), w_out_a=(w_out_a, m_w_out_a, v_w_out_a),
                 w_kv=(w_kv, m_w_kv, v_w_kv), w_in_b=(w_in_b, m_w_in_b, v_w_in_b),
                 w_out_b=(w_out_b, m_w_out_b, v_w_out_b))

    def adamw_others(landed, after):
        return {n: _sum_adamw(r, *state[n], "adamw_" + n, after=after) for n, r in landed.items()}

    dx, r_wa, big, slab_g = _local_step(
        x[0], loss_target[0], positions, ga_g.reshape(1, D), _unshard_wa(wa_g), b_forget, qnorm_a_g, knorm_a_g,
        kv_norm_g.reshape(1, D), knorm_b_g.reshape(1, HD), norm_b_g, qnorm_b_g, sinks, woa_s, wkv_s, wib_s, wob_s,
        adamw_others)
    big["w_in_a"] = _sum_adamw(r_wa, *state["w_in_a"], "adamw_w_in_a")

    r2 = lambda a: a.reshape(1, -1)
    small_w = dict(norm_a_g=norm_a_g, b_forget=b_forget, qnorm_a_g=qnorm_a_g, knorm_a_g=knorm_a_g,
                   kv_norm_g=kv_norm_g, knorm_b_g=knorm_b_g, norm_b_g=norm_b_g, qnorm_b_g=qnorm_b_g, sinks=sinks)
    small_m = dict(norm_a_g=m_norm_a_g, b_forget=m_b_forget, qnorm_a_g=m_qnorm_a_g, knorm_a_g=m_knorm_a_g,
                   kv_norm_g=m_kv_norm_g, knorm_b_g=m_knorm_b_g, norm_b_g=m_norm_b_g, qnorm_b_g=m_qnorm_b_g,
                   sinks=m_sinks)
    small_v = dict(norm_a_g=v_norm_a_g, b_forget=v_b_forget, qnorm_a_g=v_qnorm_a_g, knorm_a_g=v_knorm_a_g,
                   kv_norm_g=v_kv_norm_g, knorm_b_g=v_knorm_b_g, norm_b_g=v_norm_b_g, qnorm_b_g=v_qnorm_b_g,
                   sinks=v_sinks)
    res = _small_adamw(slab_g, [r2(small_w[n]) for n in SMALL], [r2(small_m[n]) for n in SMALL],
                       [r2(small_v[n]) for n in SMALL])
    k = len(SMALL)
    small = {n: [res[q * k + p].reshape(small_w[n].shape) for q in range(4)] for p, n in enumerate(SMALL)}
    loss = res[4 * k][0, 0]

    order = ["norm_a_g", "w_in_a", "b_forget", "qnorm_a_g", "knorm_a_g", "w_out_a", "kv_norm_g", "w_kv",
             "knorm_b_g", "norm_b_g", "w_in_b", "qnorm_b_g", "sinks", "w_out_b"]

    def leaf(n, q):
        return big[n][q] if n in big else small[n][q]

    outs = [loss, dx[None]]
    for q in range(4):
        outs.extend(leaf(n, q) for n in order)
    return tuple(outs)
```

```python
import jax
import jax.numpy as jnp
from jax import lax
from jax.experimental import pallas as pl
from jax.experimental.pallas import tpu as pltpu

F32, BF16 = jnp.float32, jnp.bfloat16

S = 2048
D = 1024
HD = 64
N_HEADS = 16
N_DEV = 8
NA = 4352
GOFF = 3072
FOFF = 4096
RAW_F = 3072
RAW_G = RAW_F + N_HEADS
NA_RAW = 4112
EPS = 1e-6
QSCALE = 0.125
ROPE_THETA = 500000.0
ROT = 16
WIN = 128
TQ = 256
TK = 256
KS = TQ // 2
HPS = 8
HW = HPS * HD
TM = 256
RT = 256
RB = 512
CB = 256
LANES = 128

ADAM_LR, ADAM_B1, ADAM_B2, ADAM_EPS, ADAM_WD, ADAM_STEP = 0.001, 0.9, 0.999, 1e-08, 0.01, 10

VMEM_LIMIT = 56 * 1024 * 1024


def _params():
    return pltpu.CompilerParams(vmem_limit_bytes=VMEM_LIMIT)


def _sds(shape, dtype):
    return jax.ShapeDtypeStruct(shape, dtype)


def _dot_nt(a, b):
    return lax.dot_general(a, b, (((1,), (1,)), ((), ())), preferred_element_type=F32)


def _dot_tn(a, b):
    return lax.dot_general(a, b, (((0,), (0,)), ((), ())), preferred_element_type=F32)


def _dot_nn(a, b):
    return lax.dot_general(a, b, (((1,), (0,)), ((), ())), preferred_element_type=F32)


def _sigmoid(g):
    return 1.0 / (1.0 + jnp.exp(-g))


def _lane_iota(shape):
    return lax.broadcasted_iota(jnp.int32, shape, len(shape) - 1)


def _flips(kind):
    return (2, 4, 6) if kind == "a2a_chips" else tuple(range(1, N_DEV))


def _send_view(kind, ref, dev):
    if kind in ("gather_rows", "gather_cols"):
        return ref
    if kind == "a2a_slots":
        return ref.at[dev]
    if kind == "a2a_chips":
        return ref.at[dev >> 1]
    if kind == "a2a_rows":
        rows = ref.shape[0] // N_DEV
        return ref.at[pl.ds(pl.multiple_of(dev * rows, rows), rows)]
    cols = ref.shape[1] // N_DEV
    return ref.at[:, pl.ds(pl.multiple_of(dev * cols, cols), cols)]


def _land_view(kind, ref, dev):
    if kind == "gather_cols":
        cols = ref.shape[1] // N_DEV
        return ref.at[:, pl.ds(pl.multiple_of(dev * cols, cols), cols)]
    if kind == "a2a_chips":
        return ref.at[dev >> 1]
    return ref.at[dev]


def _landing_sds(kind, arr):
    if kind == "gather_rows":
        return _sds((N_DEV,) + arr.shape, arr.dtype)
    if kind == "gather_cols":
        return _sds((arr.shape[0], N_DEV * arr.shape[1]), arr.dtype)
    if kind == "a2a_rows":
        return _sds((N_DEV, arr.shape[0] // N_DEV, arr.shape[1]), arr.dtype)
    if kind == "a2a_cols":
        return _sds((N_DEV, arr.shape[0], arr.shape[1] // N_DEV), arr.dtype)
    return _sds(arr.shape, arr.dtype)


def _exchange_sems(n_parts):
    n = n_parts * (N_DEV - 1)
    return [pltpu.SemaphoreType.DMA((n,)), pltpu.SemaphoreType.DMA((n,)), pltpu.SemaphoreType.DMA((n_parts,))]


def _exchange_ops(kinds, srcs, dsts, sems, start, wait):
    send_sems, recv_sems, local_sems = sems
    x, y, c = lax.axis_index("x"), lax.axis_index("y"), lax.axis_index("c")
    me = 4 * x + 2 * y + c

    def local(a):
        return pltpu.make_async_copy(_send_view(kinds[a], srcs[a], me), _land_view(kinds[a], dsts[a], me),
                                     local_sems.at[a])

    def remote(a, k, landing_dev):
        peer = (x ^ ((k >> 2) & 1), y ^ ((k >> 1) & 1), c ^ (k & 1))
        sem = a * (N_DEV - 1) + k - 1
        return pltpu.make_async_remote_copy(
            src_ref=_send_view(kinds[a], srcs[a], me ^ k), dst_ref=_land_view(kinds[a], dsts[a], landing_dev),
            send_sem=send_sems.at[sem], recv_sem=recv_sems.at[sem], device_id=peer,
            device_id_type=pl.DeviceIdType.MESH)

    pairs = [(a, k) for k in range(1, N_DEV) for a in range(len(kinds)) if k in _flips(kinds[a])]
    if start:
        for a in range(len(kinds)):
            local(a).start()
        for a, k in pairs:
            remote(a, k, me).start()
    if wait:
        for a, k in pairs:
            remote(a, k, me ^ k).wait_recv()
            remote(a, k, me).wait_send()
        for a in range(len(kinds)):
            local(a).wait()


def _gather_two_level(srcs, dsts, sems, meanwhile=None):
    send_sems, recv_sems, local_sems = sems
    x, y, c = lax.axis_index("x"), lax.axis_index("y"), lax.axis_index("c")
    me, sibling = (x, y, c), (x, y, 1 - c)
    chips = [(1 - x, y), (x, 1 - y), (1 - x, 1 - y)]

    def slot(ref, dev):
        return ref.at[4 * dev[0] + 2 * dev[1] + dev[2]]

    def copy(a, k, block, to, src=None):
        return pltpu.make_async_remote_copy(
            src_ref=slot(dsts[a], block) if src is None else src, dst_ref=slot(dsts[a], block),
            send_sem=send_sems.at[a * (N_DEV - 1) + k], recv_sem=recv_sems.at[a * (N_DEV - 1) + k],
            device_id=to, device_id_type=pl.DeviceIdType.MESH)

    parts = range(len(srcs))
    mine = [pltpu.make_async_copy(srcs[a], slot(dsts[a], me), local_sems.at[a]) for a in parts]
    first = [copy(a, 0, me, sibling, src=srcs[a]) for a in parts]
    first += [copy(a, 1 + j, me, (*chip, c), src=srcs[a]) for j, chip in enumerate(chips) for a in parts]
    for cp in mine + first:
        cp.start()
    if meanwhile is not None:
        meanwhile()
    passed = []
    for j, chip in enumerate(chips):
        for a in parts:
            copy(a, 1 + j, (*chip, c), me).wait_recv()
            fwd = copy(a, 4 + j, (*chip, c), sibling)
            fwd.start()
            passed.append(fwd)
    for a in parts:
        copy(a, 0, sibling, me).wait_recv()
        for j, chip in enumerate(chips):
            copy(a, 4 + j, (*chip, 1 - c), me).wait_recv()
    for cp in first + passed:
        cp.wait_send()
    for cp in mine:
        cp.wait()


def _call(body, *, name, args, in_specs, out_specs, out_shape, grid=(), scratch_shapes=(), aliases=None, rider=()):
    n_in, n_out, n_scr, n_r = len(in_specs), len(out_specs), len(scratch_shapes), len(rider)
    kinds = [kind for kind, _ in rider]

    def kernel_body(*refs):
        c_in, r_in = refs[:n_in], refs[n_in:n_in + n_r]
        c_out = refs[n_in + n_r:n_in + n_r + n_out]
        r_out = refs[n_in + n_r + n_out:n_in + 2 * n_r + n_out]
        rest = refs[n_in + 2 * n_r + n_out:]
        c_scr, sems = rest[:n_scr], rest[n_scr:]
        if n_r:
            assert grid, "a rider needs a gridded call"
            ids = [pl.program_id(ax) for ax in range(len(grid))]
            first, last = ids[0] == 0, ids[0] == grid[0] - 1
            for pid, size in zip(ids[1:], grid[1:]):
                first = first & (pid == 0)
                last = last & (pid == size - 1)
            pl.when(first)(lambda: _exchange_ops(kinds, r_in, r_out, sems, True, False))
        body(*c_in, *c_out, *c_scr)
        if n_r:
            pl.when(last)(lambda: _exchange_ops(kinds, r_in, r_out, sems, False, True))

    anyspec = pl.BlockSpec(memory_space=pl.ANY)
    params = pltpu.CompilerParams(vmem_limit_bytes=VMEM_LIMIT, has_side_effects=bool(n_r))
    outs = pl.pallas_call(
        kernel_body, name=name, grid=grid, in_specs=list(in_specs) + [anyspec] * n_r,
        out_specs=list(out_specs) + [anyspec] * n_r,
        out_shape=list(out_shape) + [_landing_sds(kind, arr) for kind, arr in rider],
        scratch_shapes=list(scratch_shapes) + (_exchange_sems(n_r) if n_r else []),
        input_output_aliases=aliases or {}, compiler_params=params)(*args, *[arr for _, arr in rider])
    return list(outs)


def _mm(a, b, mode, tm, tn, tk, out_dtype=F32, add=None, name="mm", rider=()):
    if mode == "nn":
        (m, k), n = a.shape, b.shape[1]
        a_spec = pl.BlockSpec((tm, tk), lambda i, j, kk: (i, kk))
        b_spec = pl.BlockSpec((tk, tn), lambda i, j, kk: (kk, j))
        dot = _dot_nn
    elif mode == "nt":
        (m, k), n = a.shape, b.shape[0]
        a_spec = pl.BlockSpec((tm, tk), lambda i, j, kk: (i, kk))
        b_spec = pl.BlockSpec((tn, tk), lambda i, j, kk: (j, kk))
        dot = _dot_nt
    else:
        (k, m), n = a.shape, b.shape[1]
        a_spec = pl.BlockSpec((tk, tm), lambda i, j, kk: (kk, i))
        b_spec = pl.BlockSpec((tk, tn), lambda i, j, kk: (kk, j))
        dot = _dot_tn
    assert m % tm == 0 and n % tn == 0 and k % tk == 0, (m, n, k, tm, tn, tk)
    nk = k // tk
    has_add = add is not None

    def body(*refs):
        if has_add:
            a_ref, b_ref, add_ref, o_ref, acc = refs
        else:
            a_ref, b_ref, o_ref, acc = refs
        p = dot(a_ref[...].astype(BF16), b_ref[...].astype(BF16))

        def finish(total):
            if has_add:
                total = add_ref[...] + total
            o_ref[...] = total.astype(out_dtype)

        if nk == 1:
            finish(p)
        else:
            kk = pl.program_id(2)

            @pl.when(kk == 0)
            def _():
                acc[...] = p

            @pl.when(kk > 0)
            def _():
                acc[...] += p

            @pl.when(kk == nk - 1)
            def _():
                finish(acc[...])

    in_specs = [a_spec, b_spec]
    args = [a, b]
    if has_add:
        in_specs.append(pl.BlockSpec((tm, tn), lambda i, j, kk: (i, j)))
        args.append(add)
    acc_shape = (tm, tn) if nk > 1 else (8, LANES)
    outs = _call(body, name=name, args=args, grid=(m // tm, n // tn, nk), in_specs=in_specs,
                 out_specs=[pl.BlockSpec((tm, tn), lambda i, j, kk: (i, j))], out_shape=[_sds((m, n), out_dtype)],
                 scratch_shapes=[pltpu.VMEM(acc_shape, F32)], rider=rider)
    return outs if rider else outs[0]


def _rms_rinv(x):
    return lax.rsqrt(jnp.mean(x * x, axis=-1, keepdims=True) + EPS)


def _rms_bwd_core(du, x, g):
    r = _rms_rinv(x)
    dug = du * g
    dx = r * (dug - x * ((r * r) * jnp.mean(dug * x, axis=-1, keepdims=True)))
    dg = jnp.sum(du * (x * r), axis=0, keepdims=True)
    return dx, dg


def _half_ones():
    r = lax.broadcasted_iota(jnp.int32, (LANES, LANES), 0)
    c = lax.broadcasted_iota(jnp.int32, (LANES, LANES), 1)
    return ((r < HD) == (c < HD)).astype(BF16)


def _half_sum(v, lo_half):
    hi = v.astype(BF16)
    lo = (v - hi.astype(F32)).astype(BF16)
    return _dot_nn(hi, lo_half) + _dot_nn(lo, lo_half)


def _head_rinv(x, lo_half):
    return lax.rsqrt(_half_sum(x * x, lo_half) * (1.0 / HD) + EPS)


def _head_norm_bwd(dn, x, g, lo_half):
    r = _head_rinv(x, lo_half)
    dng = dn * g
    dx = r * (dng - x * ((r * r) * (_half_sum(dng * x, lo_half) * (1.0 / HD))))
    dg = jnp.sum(dn * (x * r), axis=0, keepdims=True)
    return dx, dg


def _rope_swap(x, lane):
    l64 = lane & (HD - 1)
    return jnp.where(l64 < ROT // 2, pltpu.roll(x, LANES - ROT // 2, 1), pltpu.roll(x, ROT // 2, 1))


def _rope_fwd(x, cos, sin, lane):
    return x * cos + _rope_swap(x, lane) * sin


def _rope_bwd(dy, cos, sin, lane):
    return dy * cos + jnp.where((lane & (HD - 1)) < ROT, _rope_swap(dy * sin, lane), 0.0)


def _g2(g_ref):
    g = g_ref[...]
    return jnp.concatenate([g, g], axis=-1)


def _pairs(width):
    return [slice(LANES * c, LANES * (c + 1)) for c in range(width // LANES)]


def _pick_lane(block, lane, idx):
    return jnp.sum(jnp.where(lane == idx, block, 0.0), axis=-1, keepdims=True)


def _head_a(x, g, wa, gq, gk, b_pad):
    def body(x_ref, g_ref, w_ref, gq_ref, gk_ref, b_ref, u_ref, p_ref, qo_ref, ko_ref, vo_ref, c_ref, cbc_ref, carry):
        @pl.when(pl.program_id(0) == 0)
        def _():
            carry[...] = jnp.zeros_like(carry)

        xv = x_ref[...]
        u = ((xv * _rms_rinv(xv)) * g_ref[...]).astype(BF16)
        u_ref[...] = u
        for lo in range(0, NA, D):
            hi = min(lo + D, NA)
            p_ref[:, lo:hi] = _dot_nn(u, w_ref[:, lo:hi])
        lane = _lane_iota((RT, LANES))
        lo_half = _half_ones()
        gq2, gk2 = _g2(gq_ref), _g2(gk_ref)
        for c in _pairs(D):
            q = p_ref[:, c]
            k = p_ref[:, D + c.start:D + c.stop]
            qo_ref[:, c] = (((q * _head_rinv(q, lo_half)) * gq2) * QSCALE).astype(BF16)
            ko_ref[:, c] = ((k * _head_rinv(k, lo_half)) * gk2).astype(BF16)
        vo_ref[...] = p_ref[:, 2 * D:3 * D].astype(BF16)

        z = p_ref[:, FOFF:FOFF + LANES] + b_ref[...]
        logf = jnp.minimum(z, 0.0) - jnp.log1p(jnp.exp(-jnp.abs(z)))
        r = lax.broadcasted_iota(jnp.int32, (RT, RT), 0)
        cc = lax.broadcasted_iota(jnp.int32, (RT, RT), 1)
        tri = (r >= cc).astype(F32)
        loc = jnp.dot(tri, logf, precision=lax.Precision.HIGHEST, preferred_element_type=F32) + carry[0:1, :]
        c_ref[...] = loc
        carry[0:1, :] = loc[RT - 1:RT, :]
        for h in range(N_HEADS):
            cbc_ref[:, LANES * h:LANES * (h + 1)] = jnp.broadcast_to(_pick_lane(loc, lane, h), (RT, LANES))

    row = lambda width: pl.BlockSpec((RT, width), lambda i: (i, 0))
    whole = lambda arr: pl.BlockSpec(arr.shape, lambda i: (0,) * arr.ndim, pipeline_mode=pl.Buffered(1))
    return pl.pallas_call(
        body, name="head_a", grid=(S // RT,),
        in_specs=[row(D), whole(g), whole(wa), whole(gq), whole(gk), whole(b_pad)],
        out_specs=[row(D), row(NA), row(D), row(D), row(D), row(LANES), row(N_HEADS * LANES)],
        out_shape=[_sds((S, D), BF16), _sds((S, NA), F32)] + [_sds((S, D), BF16)] * 3
        + [_sds((S, LANES), F32), _sds((S, N_HEADS * LANES), F32)],
        scratch_shapes=[pltpu.VMEM((8, LANES), F32)], compiler_params=_params())(x, g, wa, gq, gk, b_pad)


def _key_le_query(offset, keys=TK):
    r = lax.broadcasted_iota(jnp.int32, (keys, TQ), 0)
    c = lax.broadcasted_iota(jnp.int32, (keys, TQ), 1)
    return (r + offset) <= c


def _widen(tile):
    return jnp.concatenate([tile] * (TQ // LANES), axis=1)


def _fox_fwd(qn, kn, vb, proj, crow, cbc, rider=()):
    nq = S // TQ

    def body(q_ref, k_ref, v_ref, g_ref, cq_ref, cbc_ref, o_ref, z_ref, lse_ref, st_s, pt_s):
        i = pl.program_id(1)
        qs = [q_ref[:, HD * hh:HD * (hh + 1)] for hh in range(HPS)]
        cqs = [cq_ref[hh, 0] for hh in range(HPS)]

        def scores(s, hh):
            off = pl.multiple_of(s * KS, KS)
            kj = k_ref[pl.ds(off, KS), HD * hh:HD * (hh + 1)]
            return (_dot_nt(kj, qs[hh]) + cqs[hh]) - _widen(cbc_ref[pl.ds(off, KS), LANES * hh:LANES * (hh + 1)])

        def values(s, hh, pt):
            off = pl.multiple_of(s * KS, KS)
            return _dot_tn(v_ref[pl.ds(off, KS), HD * hh:HD * (hh + 1)], pt)

        def step(s, slot, carries, mask=None, last=False):
            if not last:
                for hh in range(HPS):
                    st_s[1 - slot, hh] = scores(s + 1, hh)
            pvs = [values(jnp.maximum(s - 1, 0), hh, pt_s[1 - slot, hh]) for hh in range(HPS)]
            out = []
            for hh in range(HPS):
                m, l, acc = carries[hh]
                st = st_s[slot, hh]
                if mask is not None:
                    st = jnp.where(mask, st, -jnp.inf)
                m_new = jnp.maximum(m, jnp.max(st, axis=0, keepdims=True))
                pt = jnp.exp(st - m_new)
                alpha = jnp.exp(m - m_new)
                pt_s[slot, hh] = pt.astype(BF16)
                out.append((m_new, alpha * l + jnp.sum(pt, axis=0, keepdims=True), alpha * (acc + pvs[hh])))
            return tuple(out)

        for hh in range(HPS):
            st_s[0, hh] = scores(0, hh)
            pt_s[1, hh] = jnp.zeros((KS, TQ), BF16)
        one = (jnp.full((1, TQ), -jnp.inf, F32), jnp.zeros((1, TQ), F32), jnp.zeros((HD, TQ), F32))
        carries = lax.fori_loop(0, i, lambda t, cr: step(2 * t + 1, 1, step(2 * t, 0, cr)), (one,) * HPS)
        carries = step(2 * i, 0, carries, mask=_key_le_query(0, KS))
        carries = step(2 * i + 1, 1, carries, mask=_key_le_query(KS, KS), last=True)
        accs = []
        for hh in range(HPS):
            m, l, acc = carries[hh]
            acc = acc + values(2 * i + 1, hh, pt_s[1, hh])
            accs.append(acc / l)
            lse_ref[hh, 0] = m + jnp.log(l)
        o = jnp.concatenate(accs, axis=0).T
        o_ref[...] = o
        g = g_ref[...]
        z_ref[...] = (o * (g * _sigmoid(g))).astype(BF16)

    qblk = pl.BlockSpec((TQ, HW), lambda hp, i: (i, hp))
    full = pl.BlockSpec((S, HW), lambda hp, i: (0, hp))
    rows = pl.BlockSpec((HPS, 1, 1, TQ), lambda hp, i: (hp, i, 0, 0))
    return _call(
        body, name="fox_fwd", args=(qn, kn, vb, proj, crow, cbc), grid=(N_HEADS // HPS, nq),
        in_specs=[qblk, full, full,
                  pl.BlockSpec((TQ, HW), lambda hp, i: (i, GOFF // HW + hp)),
                  rows, pl.BlockSpec((S, HPS * LANES), lambda hp, i: (0, hp))],
        out_specs=[qblk, qblk, rows],
        out_shape=[_sds((S, D), F32), _sds((S, D), BF16), _sds((N_HEADS, nq, 1, TQ), F32)],
        scratch_shapes=[pltpu.VMEM((2, HPS, KS, TQ), F32), pltpu.VMEM((2, HPS, KS, TQ), BF16)], rider=rider)


def _fox_bwd_pre(dh, w_out, proj, o):
    nq, rows = S // TQ, 2 * TQ
    per = rows // TQ

    def body(dh_ref, w_ref, g_ref, o_ref, do_ref, dg_ref, delta_ref):
        g = g_ref[...]
        sg = _sigmoid(g)
        dzv = _dot_nt(dh_ref[...].astype(BF16), w_ref[...])
        ov = o_ref[...]
        do = dzv * (g * sg)
        dg_ref[...] = (dzv * ov * (sg * (1.0 + g * (1.0 - sg)))).astype(BF16)
        do_ref[...] = do.astype(BF16)
        prod_t = (do * ov).T
        for h in range(N_HEADS):
            for b in range(per):
                delta_ref[h, b] = jnp.sum(prod_t[HD * h:HD * (h + 1), TQ * b:TQ * (b + 1)], axis=0, keepdims=True)

    row = pl.BlockSpec((rows, D), lambda i: (i, 0))
    return pl.pallas_call(
        body, name="fox_bwd_pre", grid=(S // rows,),
        in_specs=[row, pl.BlockSpec(w_out.shape, lambda i: (0, 0), pipeline_mode=pl.Buffered(1)),
                  pl.BlockSpec((rows, D), lambda i: (i, GOFF // D)), row],
        out_specs=[row, row, pl.BlockSpec((N_HEADS, per, 1, TQ), lambda i: (0, i, 0, 0))],
        out_shape=[_sds((S, D), BF16), _sds((S, D), BF16), _sds((N_HEADS, nq, 1, TQ), F32)],
        compiler_params=_params())(dh, w_out, proj, o)


def _fox_bwd(qn, kn, vb, dob, lse, delta, crow, cbc, rider=()):
    nq, nkb = S // TQ, S // TK

    def body(q_ref, k_ref, v_ref, do_ref, lse_ref, del_ref, cq_ref, cbc_ref,
             dk_ref, dv_ref, dcs_ref, dq_ref, dr_ref, st_s, dp_s, pt_s, ds_s, dq_acc, dr_acc):
        j = pl.program_id(1)

        @pl.when(j == 0)
        def _():
            dq_acc[...] = jnp.zeros_like(dq_acc)
            dr_acc[...] = jnp.zeros_like(dr_acc)

        kjs = [k_ref[:, HD * hh:HD * (hh + 1)] for hh in range(HPS)]
        vjs = [v_ref[:, HD * hh:HD * (hh + 1)] for hh in range(HPS)]

        def rows_of(ref, u, hh):
            off = pl.multiple_of(u * TQ, TQ)
            return ref[pl.ds(off, TQ), HD * hh:HD * (hh + 1)]

        def products(u, hh):
            st = (_dot_nt(kjs[hh], rows_of(q_ref, u, hh)) + cq_ref[hh, u]) - _widen(
                cbc_ref[:, LANES * hh:LANES * (hh + 1)])
            return st, _dot_nt(vjs[hh], rows_of(do_ref, u, hh))

        def step(u, slot, carries, masked=False):
            nxt = jnp.minimum(u + 1, nq - 1)
            for hh in range(HPS):
                st_s[1 - slot, hh], dp_s[1 - slot, hh] = products(nxt, hh)
            prev = jnp.maximum(u - 1, 0)
            dvs = [_dot_nn(pt_s[1 - slot, hh], rows_of(do_ref, prev, hh)) for hh in range(HPS)]
            dks = [_dot_nn(ds_s[1 - slot, hh], rows_of(q_ref, prev, hh)) for hh in range(HPS)]
            for hh in range(HPS):
                dq_acc[hh, prev] += _dot_tn(kjs[hh], ds_s[1 - slot, hh])
            out = []
            for hh in range(HPS):
                dk, dv, dcs = carries[hh]
                st = st_s[slot, hh]
                if masked:
                    st = jnp.where(_key_le_query((j - u) * TQ), st, -jnp.inf)
                pt = jnp.exp(st - lse_ref[hh, u])
                dst = pt * (dp_s[slot, hh] - del_ref[hh, u])
                pt_s[slot, hh] = pt.astype(BF16)
                ds_s[slot, hh] = dst.astype(BF16)
                dr_acc[hh, u] += jnp.sum(dst, axis=0, keepdims=True)
                out.append((dk + dks[hh], dv + dvs[hh], dcs + (dst[:, :LANES] + dst[:, LANES:])))
            return tuple(out)

        t0 = j // 2
        for hh in range(HPS):
            st_s[0, hh], dp_s[0, hh] = products(2 * t0, hh)
            pt_s[1, hh] = jnp.zeros((TK, TQ), BF16)
            ds_s[1, hh] = jnp.zeros((TK, TQ), BF16)
        one = (jnp.zeros((TK, HD), F32), jnp.zeros((TK, HD), F32), jnp.zeros((TK, LANES), F32))
        carries = step(2 * t0 + 1, 1, step(2 * t0, 0, (one,) * HPS, masked=True), masked=True)
        carries = lax.fori_loop(t0 + 1, nq // 2, lambda t, cr: step(2 * t + 1, 1, step(2 * t, 0, cr)), carries)
        dks, dvs = [], []
        lane = _lane_iota((TK, LANES))
        dcs_all = jnp.zeros((TK, LANES), F32)
        for hh in range(HPS):
            dk, dv, dcs = carries[hh]
            dks.append(dk + _dot_nn(ds_s[1, hh], rows_of(q_ref, nq - 1, hh)))
            dvs.append(dv + _dot_nn(pt_s[1, hh], rows_of(do_ref, nq - 1, hh)))
            dq_acc[hh, nq - 1] += _dot_tn(kjs[hh], ds_s[1, hh])
            dcs_all = jnp.where(lane == HPS * pl.program_id(0) + hh, -jnp.sum(dcs, axis=-1, keepdims=True), dcs_all)
        dcs_ref[0] = dcs_all
        dk_ref[...] = jnp.concatenate(dks, axis=-1)
        dv_ref[...] = jnp.concatenate(dvs, axis=-1).astype(BF16)

        @pl.when(j == nkb - 1)
        def _():
            for i in range(nq):
                dq_ref[TQ * i:TQ * (i + 1), :] = jnp.concatenate([dq_acc[hh, i] for hh in range(HPS)], axis=0).T
            dr_ref[...] = dr_acc[...]

    kblk = pl.BlockSpec((TK, HW), lambda hp, j: (j, hp))
    full = pl.BlockSpec((S, HW), lambda hp, j: (0, hp))
    rows = pl.BlockSpec((HPS, nq, 1, TQ), lambda hp, j: (hp, 0, 0, 0))
    cblk = pl.BlockSpec((TK, HPS * LANES), lambda hp, j: (j, hp))
    return _call(
        body, name="fox_bwd", args=(qn, kn, vb, dob, lse, delta, crow, cbc), grid=(N_HEADS // HPS, nkb),
        in_specs=[full, kblk, kblk, full, rows, rows, rows, cblk],
        out_specs=[kblk, kblk, pl.BlockSpec((1, TK, LANES), lambda hp, j: (hp, j, 0)), full, rows],
        out_shape=[_sds((S, D), F32), _sds((S, D), BF16), _sds((N_HEADS // HPS, S, LANES), F32), _sds((S, D), F32),
                   _sds((N_HEADS, nq, 1, TQ), F32)],
        scratch_shapes=[pltpu.VMEM((2, HPS, TK, TQ), F32), pltpu.VMEM((2, HPS, TK, TQ), F32),
                        pltpu.VMEM((2, HPS, TK, TQ), BF16), pltpu.VMEM((2, HPS, TK, TQ), BF16),
                        pltpu.VMEM((HPS, nq, HD, TQ), F32), pltpu.VMEM((HPS, nq, 1, TQ), F32)], rider=rider)


def _prep_a_bwd(dq, dk, dv, dgate, drow, dcs, proj, b_pad, gq, gk):
    nt = S // TM

    def body(dq_ref, dk_ref, dv_ref, dgt_ref, dr_ref, dcs_ref, xq_ref, xk_ref, f_ref, b_ref, gq_ref, gk_ref,
             o_ref, dgq_ref, dgk_ref, db_ref, carry):
        @pl.when(pl.program_id(0) == 0)
        def _():
            carry[...] = jnp.zeros_like(carry)
            dgq_ref[...] = jnp.zeros_like(dgq_ref)
            dgk_ref[...] = jnp.zeros_like(dgk_ref)
            db_ref[...] = jnp.zeros_like(db_ref)

        lane = _lane_iota((TM, LANES))
        lo_half = _half_ones()
        gq2, gk2 = _g2(gq_ref), _g2(gk_ref)
        dgq, dgk = jnp.zeros((1, LANES), F32), jnp.zeros((1, LANES), F32)
        for c in _pairs(D):
            dxq, dg = _head_norm_bwd(dq_ref[:, c] * QSCALE, xq_ref[:, c], gq2, lo_half)
            o_ref[:, c] = dxq.astype(BF16)
            dgq = dgq + dg
            dxk, dg = _head_norm_bwd(dk_ref[:, c], xk_ref[:, c], gk2, lo_half)
            o_ref[:, D + c.start:D + c.stop] = dxk.astype(BF16)
            dgk = dgk + dg
        dgq_ref[...] += dgq
        dgk_ref[...] += dgk
        o_ref[:, 2 * D:3 * D] = dv_ref[...]
        o_ref[:, GOFF:GOFF + D] = dgt_ref[...]

        dc = dr_ref[...]
        for group in range(N_HEADS // HPS):
            dc = dc + dcs_ref[group]
        r = lax.broadcasted_iota(jnp.int32, (TM, TM), 0)
        c = lax.broadcasted_iota(jnp.int32, (TM, TM), 1)
        tri = (c >= r).astype(F32)
        dlogf = jnp.dot(tri, dc, precision=lax.Precision.HIGHEST, preferred_element_type=F32) + carry[0:1, :]
        carry[0:1, :] = dlogf[0:1, :]
        df = dlogf * (1.0 / (1.0 + jnp.exp(f_ref[...] + b_ref[...])))
        db_ref[...] += jnp.sum(df, axis=0, keepdims=True)
        o_ref[:, FOFF:FOFF + LANES] = df.astype(BF16)
        o_ref[:, FOFF + LANES:NA] = jnp.zeros((TM, NA - FOFF - LANES), BF16)

    rev = lambda width, col: pl.BlockSpec((TM, width), lambda i: (nt - 1 - i, col))
    gspec = pl.BlockSpec((1, HD), lambda i: (0, 0))
    acc = pl.BlockSpec((1, LANES), lambda i: (0, 0))
    return pl.pallas_call(
        body, name="prep_a_bwd", grid=(nt,),
        in_specs=[rev(D, 0), rev(D, 0), rev(D, 0), rev(D, 0), rev(LANES, 0),
                  pl.BlockSpec((N_HEADS // HPS, TM, LANES), lambda i: (0, nt - 1 - i, 0)),
                  rev(D, 0), rev(D, 1), rev(LANES, FOFF // LANES), acc, gspec, gspec],
        out_specs=[rev(NA, 0), acc, acc, acc],
        out_shape=[_sds((S, NA), BF16)] + [_sds((1, LANES), F32)] * 3,
        scratch_shapes=[pltpu.VMEM((8, LANES), F32)],
        compiler_params=_params())(dq, dk, dv, dgate, drow, dcs, proj, proj, proj, b_pad, gq, gk)


def _head_b(x, z_a, w_out_a, g_kv, g_b, w_kv, w_in_b, gq, gk, cos2, sin2):
    nkv = w_kv.shape[1] // 2

    def body(x_ref, z_ref, wo_ref, gkv_ref, gb_ref, wkv_ref, wb_ref, gq_ref, gk_ref, c_ref, s_ref,
             h_ref, ukv_ref, ub_ref, kv_ref, pb_ref, qo_ref, ko_ref, vo_ref):
        xv = x_ref[...] + _dot_nn(z_ref[...], wo_ref[...])
        h_ref[...] = xv
        xn = xv * _rms_rinv(xv)
        ukv = (xn * gkv_ref[...]).astype(BF16)
        ub = (xn * gb_ref[...]).astype(BF16)
        ukv_ref[...] = ukv
        ub_ref[...] = ub
        kv_ref[...] = _dot_nn(ukv, wkv_ref[...])
        for lo in range(0, 2 * D, D):
            pb_ref[:, lo:lo + D] = _dot_nn(ub, wb_ref[:, lo:lo + D])
        lane = _lane_iota((RT, LANES))
        lo_half = _half_ones()
        cos, sin = c_ref[...], s_ref[...]
        gq2, gk2 = _g2(gq_ref), _g2(gk_ref)
        for c in _pairs(D):
            q = pb_ref[:, c]
            qo_ref[:, c] = (_rope_fwd((q * _head_rinv(q, lo_half)) * gq2, cos, sin, lane) * QSCALE).astype(BF16)
        for c in _pairs(nkv):
            k = kv_ref[:, c]
            ko_ref[:, c] = _rope_fwd((k * _head_rinv(k, lo_half)) * gk2, cos, sin, lane).astype(BF16)
        vo_ref[...] = kv_ref[:, nkv:2 * nkv].astype(BF16)

    row = lambda width: pl.BlockSpec((RT, width), lambda i: (i, 0))
    whole = lambda arr: pl.BlockSpec(arr.shape, lambda i: (0,) * arr.ndim, pipeline_mode=pl.Buffered(1))
    return pl.pallas_call(
        body, name="head_b", grid=(S // RT,),
        in_specs=[row(D), row(D), whole(w_out_a), whole(g_kv), whole(g_b), whole(w_kv), whole(w_in_b), whole(gq),
                  whole(gk), row(LANES), row(LANES)],
        out_specs=[row(D), row(D), row(D), row(2 * nkv), row(2 * D), row(D), row(nkv), row(nkv)],
        out_shape=[_sds((S, D), F32), _sds((S, D), BF16), _sds((S, D), BF16), _sds((S, 2 * nkv), F32),
                   _sds((S, 2 * D), F32), _sds((S, D), BF16), _sds((S, nkv), BF16), _sds((S, nkv), BF16)],
        compiler_params=_params())(x, z_a, w_out_a, g_kv, g_b, w_kv, w_in_b, gq, gk, cos2, sin2)


N_KV, GRP = 4, 4


def _swa_mask(n):
    r = lax.broadcasted_iota(jnp.int32, (2 * WIN, GRP * WIN), 0)
    q = lax.broadcasted_iota(jnp.int32, (2 * WIN, GRP * WIN), 1) & (WIN - 1)
    return (r > q) & (r <= q + WIN) & ((r >= WIN) | (n > 0))


def _stack4(ref_or_val, base):
    return jnp.concatenate([ref_or_val[:, base + HD * g: base + HD * (g + 1)] for g in range(GRP)], axis=0)


def _unstack4(xt):
    return jnp.concatenate([xt[:, WIN * g:WIN * (g + 1)] for g in range(GRP)], axis=0).T


def _band(prev_ref, cur_ref, kh):
    return jnp.concatenate([prev_ref[:, HD * kh:HD * (kh + 1)], cur_ref[:, HD * kh:HD * (kh + 1)]], axis=0)


def _sink_row(s_ref, first):
    lane = _lane_iota((1, GRP * WIN))
    row = jnp.full((1, GRP * WIN), s_ref[first + GRP - 1], F32)
    for g in range(GRP - 2, -1, -1):
        row = jnp.where(lane < WIN * (g + 1), s_ref[first + g], row)
    return row


def _swa_fwd(qb, ksh, vsh, pb, sinks):
    nb = S // WIN

    def body(q_ref, kp_ref, kc_ref, vp_ref, vc_ref, g_ref, s_ref, o_ref, z_ref, lse_ref):
        n = pl.program_id(0)
        valid = _swa_mask(n)
        outs = []
        for kh in range(N_KV):
            kb, vb = _band(kp_ref, kc_ref, kh), _band(vp_ref, vc_ref, kh)
            st = jnp.where(valid, _dot_nt(kb, _stack4(q_ref, GRP * HD * kh)), -jnp.inf)
            sink = _sink_row(s_ref, GRP * kh)
            m = jnp.maximum(jnp.max(st, axis=0, keepdims=True), sink)
            pt = jnp.exp(st - m)
            l = jnp.sum(pt, axis=0, keepdims=True) + jnp.exp(sink - m)
            outs.append(_unstack4(_dot_tn(vb, pt.astype(BF16)) / l))
            lse = m + jnp.log(l)
            for g in range(GRP):
                lse_ref[GRP * kh + g, 0] = lse[:, WIN * g:WIN * (g + 1)]
        o = jnp.concatenate(outs, axis=-1)
        o_ref[...] = o
        g = g_ref[...]
        z_ref[...] = (o * (g * _sigmoid(g))).astype(BF16)

    row = pl.BlockSpec((WIN, D), lambda n: (n, 0))
    prev = pl.BlockSpec((WIN, N_KV * HD), lambda n: (jnp.maximum(n - 1, 0), 0))
    cur = pl.BlockSpec((WIN, N_KV * HD), lambda n: (n, 0))
    return pl.pallas_call(
        body, name="swa_fwd", grid=(nb,),
        in_specs=[row, prev, cur, prev, cur, pl.BlockSpec((WIN, D), lambda n: (n, 1)),
                  pl.BlockSpec(memory_space=pltpu.SMEM)],
        out_specs=[row, row, pl.BlockSpec((N_HEADS, 1, 1, WIN), lambda n: (0, n, 0, 0))],
        out_shape=[_sds((S, D), F32), _sds((S, D), BF16), _sds((N_HEADS, nb, 1, WIN), F32)],
        compiler_params=_params())(qb, ksh, ksh, vsh, vsh, pb, sinks)


def _swa_bwd(qb, ksh, vsh, dz, o, lse, pb, sinks, gq, cos2, sin2):
    nb = S // WIN

    def body(q_ref, kp_ref, kc_ref, vp_ref, vc_ref, dz_ref, o_ref, lse_ref, x_ref, g_ref, s_ref, gq_ref, c_ref, sn_ref,
             dpb_ref, dka_ref, dkb_ref, dva_ref, dvb_ref, dsink_ref, dgq_ref):
        n = pl.program_id(0)

        @pl.when(n == 0)
        def _():
            dsink_ref[...] = jnp.zeros_like(dsink_ref)
            dgq_ref[...] = jnp.zeros_like(dgq_ref)

        valid = _swa_mask(n)
        g = g_ref[...]
        sg = _sigmoid(g)
        dzv = dz_ref[...]
        ov = o_ref[...]
        do = dzv * (g * sg)
        dpb_ref[:, D:2 * D] = (dzv * ov * (sg * (1.0 + g * (1.0 - sg)))).astype(BF16)
        prod_t = (do * ov).T
        lane1 = _lane_iota((1, LANES))
        dqs, dkas, dkbs, dvas, dvbs = [], [], [], [], []
        dsink = jnp.zeros((1, LANES), F32)
        for kh in range(N_KV):
            kb, vb = _band(kp_ref, kc_ref, kh), _band(vp_ref, vc_ref, kh)
            base = GRP * HD * kh
            qs = _stack4(q_ref, base)
            dos = _stack4(do, base).astype(BF16)
            delta = jnp.concatenate(
                [jnp.sum(prod_t[base + HD * gg:base + HD * (gg + 1), :], axis=0, keepdims=True)
                 for gg in range(GRP)], axis=1)
            lse = jnp.concatenate([lse_ref[GRP * kh + gg, 0] for gg in range(GRP)], axis=1)
            st = jnp.where(valid, _dot_nt(kb, qs), -jnp.inf)
            pt = jnp.exp(st - lse)
            dst = pt * (_dot_nt(vb, dos) - delta)
            dsb = dst.astype(BF16)
            dqs.append(_unstack4(_dot_tn(kb, dsb)))
            dkband = _dot_nn(dsb, qs)
            dvband = _dot_nn(pt.astype(BF16), dos)
            dkbs.append(dkband[0:WIN, :])
            dkas.append(dkband[WIN:2 * WIN, :])
            dvbs.append(dvband[0:WIN, :])
            dvas.append(dvband[WIN:2 * WIN, :])
            ps_delta = jnp.exp(_sink_row(s_ref, GRP * kh) - lse) * delta
            for gg in range(GRP):
                val = jnp.sum(ps_delta[:, WIN * gg:WIN * (gg + 1)], axis=1, keepdims=True)
                dsink = dsink - jnp.where(lane1 == GRP * kh + gg, val, 0.0)
        dka_ref[...] = jnp.concatenate(dkas, axis=-1)
        dkb_ref[...] = jnp.concatenate(dkbs, axis=-1)
        dva_ref[...] = jnp.concatenate(dvas, axis=-1)
        dvb_ref[...] = jnp.concatenate(dvbs, axis=-1)
        dsink_ref[...] += dsink

        lane = _lane_iota((WIN, LANES))
        g2, cos, sin = _g2(gq_ref), c_ref[...], sn_ref[...]
        lo_half = _half_ones()
        dg_tot = jnp.zeros((1, LANES), F32)
        for kh in range(N_KV):
            for c in _pairs(GRP * HD):
                cols = slice(GRP * HD * kh + c.start, GRP * HD * kh + c.stop)
                dn = _rope_bwd(dqs[kh][:, c] * QSCALE, cos, sin, lane)
                dx, dg = _head_norm_bwd(dn, x_ref[:, cols], g2, lo_half)
                dpb_ref[:, cols] = dx.astype(BF16)
                dg_tot = dg_tot + dg
        dgq_ref[...] += dg_tot

    row = pl.BlockSpec((WIN, D), lambda n: (n, 0))
    prev = pl.BlockSpec((WIN, N_KV * HD), lambda n: (jnp.maximum(n - 1, 0), 0))
    cur = pl.BlockSpec((WIN, N_KV * HD), lambda n: (n, 0))
    acc = pl.BlockSpec((1, LANES), lambda n: (0, 0))
    tab = pl.BlockSpec((WIN, LANES), lambda n: (n, 0))
    return pl.pallas_call(
        body, name="swa_bwd", grid=(nb,),
        in_specs=[row, prev, cur, prev, cur, row, row, pl.BlockSpec((N_HEADS, 1, 1, WIN), lambda n: (0, n, 0, 0)),
                  row, pl.BlockSpec((WIN, D), lambda n: (n, 1)), pl.BlockSpec(memory_space=pltpu.SMEM),
                  pl.BlockSpec((1, HD), lambda n: (0, 0)), tab, tab],
        out_specs=[pl.BlockSpec((WIN, 2 * D), lambda n: (n, 0)), cur, cur, cur, cur, acc, acc],
        out_shape=[_sds((S, 2 * D), BF16)] + [_sds((S, 256), F32)] * 4 + [_sds((1, LANES), F32)] * 2,
        compiler_params=_params())(qb, ksh, ksh, vsh, vsh, dz, o, lse, pb, pb, sinks, gq, cos2, sin2)


def _prep_kv_bwd(dka, dkb, dva, dvb, kv, gk, cos2, sin2):
    nt = S // RB
    per = RB // WIN

    def shifted(cur_ref, nxt_ref, has_next):
        return jnp.concatenate([cur_ref[WIN:RB, :], jnp.where(has_next, nxt_ref[...], 0.0)], axis=0)

    def body(dka_ref, dkb_ref, dkn_ref, dva_ref, dvb_ref, dvn_ref, x_ref, g_ref, c_ref, s_ref, o_ref, dgk_ref):
        i, j = pl.program_id(0), pl.program_id(1)

        @pl.when((i == 0) & (j == 0))
        def _():
            dgk_ref[...] = jnp.zeros_like(dgk_ref)

        has_next = i < nt - 1

        @pl.when(j == 0)
        def _():
            lane = _lane_iota((RB, LANES))
            g2, cos, sin = _g2(g_ref), c_ref[...], s_ref[...]
            dy_all = dka_ref[...] + shifted(dkb_ref, dkn_ref, has_next)
            dg_tot = jnp.zeros((1, LANES), F32)
            lo_half = _half_ones()
            for c in _pairs(CB):
                dn = _rope_bwd(dy_all[:, c], cos, sin, lane)
                dx, dg = _head_norm_bwd(dn, x_ref[:, c], g2, lo_half)
                o_ref[:, c] = dx.astype(BF16)
                dg_tot = dg_tot + dg
            dgk_ref[...] += dg_tot

        @pl.when(j == 1)
        def _():
            o_ref[...] = (dva_ref[...] + shifted(dvb_ref, dvn_ref, has_next)).astype(BF16)

    cur = pl.BlockSpec((RB, CB), lambda i, j: (i, 0))
    nxt = pl.BlockSpec((WIN, CB), lambda i, j: (jnp.minimum(per * (i + 1), S // WIN - 1), 0))
    tab = pl.BlockSpec((RB, LANES), lambda i, j: (i, 0))
    return pl.pallas_call(
        body, name="prep_kv_bwd", grid=(nt, 2),
        in_specs=[cur, cur, nxt, cur, cur, nxt, cur, pl.BlockSpec((1, HD), lambda i, j: (0, 0)), tab, tab],
        out_specs=[pl.BlockSpec((RB, CB), lambda i, j: (i, j)), pl.BlockSpec((1, LANES), lambda i, j: (0, 0))],
        out_shape=[_sds((S, 2 * CB), BF16), _sds((1, LANES), F32)],
        compiler_params=_params())(dka, dkb, dkb, dva, dvb, dvb, kv, gk, cos2, sin2)


def _out_b_loss(z, w_out, h1, tgt):
    tm = 2 * RT

    def body(z_ref, w_ref, h_ref, t_ref, dy_ref, l_ref):
        @pl.when(pl.program_id(0) == 0)
        def _():
            l_ref[...] = jnp.zeros_like(l_ref)

        e = (h_ref[...] + _dot_nn(z_ref[...], w_ref[...])) - t_ref[...]
        dy_ref[...] = e * (1.0 / D)
        l_ref[...] += jnp.sum(jnp.sum(e * e, axis=-1, keepdims=True), axis=0, keepdims=True)

    row = pl.BlockSpec((tm, D), lambda i: (i, 0))
    return pl.pallas_call(
        body, name="out_b_loss", grid=(S // tm,),
        in_specs=[row, pl.BlockSpec(w_out.shape, lambda i: (0, 0), pipeline_mode=pl.Buffered(1)), row, row],
        out_specs=[row, pl.BlockSpec((1, LANES), lambda i: (0, 0))],
        out_shape=[_sds((S, D), F32), _sds((1, LANES), F32)], compiler_params=_params())(z, w_out, h1, tgt)


def _du_a_rms_bwd(dproj, wa, x, g, dres, after):
    tm, tk = 1024, NA // 2
    nk = NA // tk

    def body(a_ref, b_ref, x_ref, g_ref, dr_ref, after_ref, dx_ref, dg_ref, acc):
        i, kk = pl.program_id(0), pl.program_id(1)

        @pl.when((i == 0) & (kk == 0))
        def _():
            dg_ref[...] = jnp.zeros_like(dg_ref)

        p = _dot_nt(a_ref[...], b_ref[...])

        @pl.when(kk == 0)
        def _():
            acc[...] = p

        @pl.when(kk == nk - 1)
        def _():
            dx, dg = _rms_bwd_core(acc[...] + p, x_ref[...], g_ref[...])
            dx_ref[...] = dr_ref[...] + dx
            dg_ref[...] += dg

    assert nk == 2
    row = pl.BlockSpec((tm, D), lambda i, kk: (i, 0))
    vec = pl.BlockSpec((1, D), lambda i, kk: (0, 0))
    return pl.pallas_call(
        body, name="du_a_rms_bwd", grid=(S // tm, nk),
        in_specs=[pl.BlockSpec((tm, tk), lambda i, kk: (i, kk)), pl.BlockSpec((D, tk), lambda i, kk: (0, kk)),
                  row, vec, row, pl.BlockSpec(after.shape, lambda i, kk: (0, 0))],
        out_specs=[row, vec], out_shape=[_sds((S, D), F32), _sds((1, D), F32)],
        scratch_shapes=[pltpu.VMEM((tm, D), F32)], compiler_params=_params())(dproj, wa, x, g, dres, after)


def _du_b_rms_bwd(dpb, w_in_b, dkv, w_kv, h1, g_b, g_kv, dy):
    tm = 2 * RT

    def body(ab_ref, wb_ref, akv_ref, wkv_ref, x_ref, gb_ref, gkv_ref, dy_ref, dh_ref, dgb_ref, dgkv_ref):
        @pl.when(pl.program_id(0) == 0)
        def _():
            dgb_ref[...] = jnp.zeros_like(dgb_ref)
            dgkv_ref[...] = jnp.zeros_like(dgkv_ref)

        x = x_ref[...]
        dx1, dg1 = _rms_bwd_core(_dot_nt(ab_ref[...], wb_ref[...]), x, gb_ref[...])
        dx2, dg2 = _rms_bwd_core(_dot_nt(akv_ref[...], wkv_ref[...]), x, gkv_ref[...])
        dh_ref[...] = dy_ref[...] + dx1 + dx2
        dgb_ref[...] += dg1
        dgkv_ref[...] += dg2

    row = lambda width: pl.BlockSpec((tm, width), lambda i: (i, 0))
    whole = lambda arr: pl.BlockSpec(arr.shape, lambda i: (0, 0), pipeline_mode=pl.Buffered(1))
    vec = pl.BlockSpec((1, D), lambda i: (0, 0))
    return pl.pallas_call(
        body, name="du_b_rms_bwd", grid=(S // tm,),
        in_specs=[row(dpb.shape[1]), whole(w_in_b), row(dkv.shape[1]), whole(w_kv), row(D), vec, vec, row(D)],
        out_specs=[row(D), vec, vec], out_shape=[_sds((S, D), F32), _sds((1, D), F32), _sds((1, D), F32)],
        compiler_params=_params())(dpb, w_in_b, dkv, w_kv, h1, g_b, g_kv, dy)


def _gather_first(w_in_a, w_out_a, w_kv, w_in_b, w_out_b, norm_a_g):
    def body(wia_ref, woa_ref, wkv_ref, wib_ref, wob_ref, ga_ref,
             wa_g, ga_g, woa_s, wkv_s, wib_s, wob_s, wa_s, st_a, st_oa, st_kv, st_ib, st_ob, load_sems, *sems):
        sources = [wia_ref.at[0], woa_ref.at[0], wkv_ref, wib_ref.at[0], wob_ref.at[0]]
        stages = [st_a, st_oa, st_kv, st_ib, st_ob]
        loads = [pltpu.make_async_copy(src, dst, load_sems.at[i]) for i, (src, dst) in enumerate(zip(sources, stages))]
        for cp in loads:
            cp.start()
        loads[0].wait()
        wa_s[...] = st_a[...].astype(BF16)

        def cast_the_rest():
            for cp, stage, out in zip(loads[1:], stages[1:], [woa_s, wkv_s, wib_s, wob_s]):
                cp.wait()
                out[...] = stage[...].astype(BF16)

        _gather_two_level([wa_s, ga_ref], [wa_g, ga_g], sems, meanwhile=cast_the_rest)

    vmem = pl.BlockSpec(memory_space=pltpu.VMEM)
    anyspec = pl.BlockSpec(memory_space=pl.ANY)
    shard = lambda w: _sds(w.shape[-2:], BF16)
    stage = lambda w: pltpu.VMEM(w.shape[-2:], F32)
    return pl.pallas_call(
        body, name="gather_first", in_specs=[anyspec] * 5 + [vmem],
        out_specs=[anyspec, anyspec, vmem, vmem, vmem, vmem],
        out_shape=[_sds((N_DEV,) + w_in_a.shape[-2:], BF16), _sds((N_DEV,) + norm_a_g.shape, F32),
                   shard(w_out_a), shard(w_kv), shard(w_in_b), shard(w_out_b)],
        scratch_shapes=[pltpu.VMEM(w_in_a.shape[-2:], BF16), stage(w_in_a), stage(w_out_a), stage(w_kv),
                        stage(w_in_b), stage(w_out_b), pltpu.SemaphoreType.DMA((5,))] + _exchange_sems(2),
        compiler_params=pltpu.CompilerParams(vmem_limit_bytes=VMEM_LIMIT, has_side_effects=True))(
            w_in_a, w_out_a, w_kv, w_in_b, w_out_b, norm_a_g)


def _pair_reduce(slots):
    n_chip = N_DEV // 2
    _, rows, cols = slots.shape

    def body(s_ref, o_ref, own_v, sib_v, send_sems, recv_sems, local_sems):
        x, y, c = lax.axis_index("x"), lax.axis_index("y"), lax.axis_index("c")
        copies = []
        for j in range(n_chip):
            own = pltpu.make_async_copy(s_ref.at[2 * j + c], own_v.at[j], local_sems.at[j])
            give = pltpu.make_async_remote_copy(
                src_ref=s_ref.at[2 * j + 1 - c], dst_ref=sib_v.at[j], send_sem=send_sems.at[j],
                recv_sem=recv_sems.at[j], device_id=(x, y, 1 - c), device_id_type=pl.DeviceIdType.MESH)
            own.start()
            give.start()
            copies.append((own, give))
        for j, (own, give) in enumerate(copies):
            own.wait()
            give.wait()
            o_ref[j] = (own_v[j].astype(F32) + sib_v[j].astype(F32)).astype(BF16)

    half = _sds((n_chip, rows, cols), slots.dtype)
    return pl.pallas_call(
        body, name="pair_reduce", in_specs=[pl.BlockSpec(memory_space=pl.ANY)],
        out_specs=pl.BlockSpec(memory_space=pltpu.VMEM), out_shape=half,
        scratch_shapes=[pltpu.VMEM(half.shape, half.dtype), pltpu.VMEM(half.shape, half.dtype),
                        pltpu.SemaphoreType.DMA((n_chip,)), pltpu.SemaphoreType.DMA((n_chip,)),
                        pltpu.SemaphoreType.DMA((n_chip,))],
        compiler_params=pltpu.CompilerParams(vmem_limit_bytes=VMEM_LIMIT, has_side_effects=True))(slots)


def _padded_col(c):
    if c < RAW_F:
        return c
    return FOFF + (c - RAW_F) if c < RAW_G else GOFF + (c - RAW_G)


def _shard_pieces():
    width = NA_RAW // N_DEV
    pieces = []
    for d in range(N_DEV):
        cuts = [width * d] + [c for c in (RAW_F, RAW_G) if width * d < c < width * (d + 1)] + [width * (d + 1)]
        for lo, hi in zip(cuts[:-1], cuts[1:]):
            pieces.append((d, lo - width * d, _padded_col(lo), hi - lo))
    return pieces


def _unshard_wa(wa_g):
    def body(w_ref, o_ref):
        o_ref[:, FOFF + N_HEADS:NA] = jnp.zeros((TM, NA - FOFF - N_HEADS), BF16)
        for d, src, dst, width in _shard_pieces():
            o_ref[:, dst:dst + width] = w_ref[d, :, src:src + width]

    return pl.pallas_call(
        body, name="unshard_wa", grid=(D // TM,),
        in_specs=[pl.BlockSpec((N_DEV, TM, NA_RAW // N_DEV), lambda i: (0, i, 0))],
        out_specs=pl.BlockSpec((TM, NA), lambda i: (i, 0)), out_shape=_sds((D, NA), BF16),
        compiler_params=_params())(wa_g)


def _reshard_dwa(dwa):
    def body(g_ref, o_ref):
        for d, src, dst, width in _shard_pieces():
            o_ref[d, :, src:src + width] = g_ref[:, dst:dst + width]

    return pl.pallas_call(
        body, name="reshard_dwa", grid=(D // TM,), in_specs=[pl.BlockSpec((TM, NA), lambda i: (i, 0))],
        out_specs=pl.BlockSpec((N_DEV, TM, NA_RAW // N_DEV), lambda i: (0, i, 0)),
        out_shape=_sds((N_DEV, D, NA_RAW // N_DEV), dwa.dtype), compiler_params=_params())(dwa)


CHIP_FLIPS = (2, 4, 6)


def _chip_exchange_start(partial):
    n = len(CHIP_FLIPS)

    def body(p_ref, land_ref, *rest):
        sends, recvs, token = rest[:n], rest[n:2 * n], rest[2 * n + 2]
        x, y, c = lax.axis_index("x"), lax.axis_index("y"), lax.axis_index("c")
        me = 4 * x + 2 * y + c
        for idx, k in enumerate(CHIP_FLIPS):
            pltpu.make_async_remote_copy(
                src_ref=p_ref.at[(me ^ k) >> 1], dst_ref=land_ref.at[me >> 1], send_sem=sends[idx],
                recv_sem=recvs[idx], device_id=(x ^ ((k >> 2) & 1), y ^ ((k >> 1) & 1), c),
                device_id_type=pl.DeviceIdType.MESH).start()
        token[...] = jnp.zeros_like(token)

    hbm = pl.BlockSpec(memory_space=pltpu.HBM)
    sem = pl.BlockSpec(memory_space=pltpu.SEMAPHORE)
    buf = pltpu.HBM(partial.shape, partial.dtype)
    return pl.pallas_call(
        body, name="chip_exchange_start",
        out_shape=(pltpu.SemaphoreType.DMA(()),) * (2 * n) + (buf, buf, _sds((8, LANES), F32)),
        in_specs=(hbm, hbm), out_specs=(sem,) * (2 * n) + (hbm, hbm, pl.BlockSpec(memory_space=pltpu.VMEM)),
        input_output_aliases={0: 2 * n, 1: 2 * n + 1},
        compiler_params=pltpu.CompilerParams(has_side_effects=pltpu.SideEffectType.DATAFLOW_SIDE_EFFECTING))(
            pltpu.with_memory_space_constraint(partial, pltpu.HBM),
            pltpu.with_memory_space_constraint(lax.empty(partial.shape, partial.dtype), pltpu.HBM))


def _chip_exchange_wait(started, after):
    n = len(CHIP_FLIPS)
    sems, (p_thru, land_thru) = started[:2 * n], started[2 * n:2 * n + 2]

    def body(p_ref, land_ref, *rest):
        sends, recvs = rest[:n], rest[n:2 * n]
        x, y, c = lax.axis_index("x"), lax.axis_index("y"), lax.axis_index("c")
        me = 4 * x + 2 * y + c
        for idx, k in enumerate(CHIP_FLIPS):
            copy = pltpu.make_async_remote_copy(
                src_ref=p_ref.at[(me ^ k) >> 1], dst_ref=land_ref.at[(me ^ k) >> 1], send_sem=sends[idx],
                recv_sem=recvs[idx], device_id=(x ^ ((k >> 2) & 1), y ^ ((k >> 1) & 1), c),
                device_id_type=pl.DeviceIdType.MESH)
            copy.wait_send()
            copy.wait_recv()

    hbm = pl.BlockSpec(memory_space=pltpu.HBM)
    sem = pl.BlockSpec(memory_space=pltpu.SEMAPHORE)
    buf = pltpu.HBM(p_thru.shape, p_thru.dtype)
    return pl.pallas_call(
        body, name="chip_exchange_wait", out_shape=(buf, buf),
        in_specs=(hbm, hbm) + (sem,) * (2 * n) + (pl.BlockSpec(memory_space=pl.ANY),), out_specs=(hbm, hbm),
        input_output_aliases={0: 0, 1: 1},
        compiler_params=pltpu.CompilerParams(has_side_effects=pltpu.SideEffectType.DATAFLOW_SIDE_EFFECTING))(
            p_thru, land_thru, *sems, after)


def _gather_slab(slab, after):
    def body(s_ref, after_ref, o_ref, *sems):
        _exchange_ops(["gather_rows"], [s_ref], [o_ref], sems, True, True)

    anyspec = pl.BlockSpec(memory_space=pl.ANY)
    return pl.pallas_call(
        body, name="gather_slab", in_specs=[anyspec, anyspec], out_specs=anyspec,
        out_shape=_sds((N_DEV,) + slab.shape, slab.dtype), scratch_shapes=_exchange_sems(1),
        compiler_params=pltpu.CompilerParams(has_side_effects=True))(slab, after)


def _adamw(w, g, m, v):
    m = ADAM_B1 * m + (1.0 - ADAM_B1) * g
    v = ADAM_B2 * v + (1.0 - ADAM_B2) * (g * g)
    m_hat = m / (1.0 - ADAM_B1 ** ADAM_STEP)
    v_hat = v / (1.0 - ADAM_B2 ** ADAM_STEP)
    delta = -ADAM_LR * (m_hat / (jnp.sqrt(v_hat) + ADAM_EPS) + ADAM_WD * w)
    return delta, m, v


def _sum_adamw(recv, w, m, v, name, after=None):
    lead = w.ndim - 2
    rows, cols = w.shape[-2:]
    tr = 256 if rows % 256 == 0 else 128
    slabs = list(recv) if isinstance(recv, tuple) else [recv]
    n_slots = slabs[0].shape[0]
    extra = [] if after is None else [after]

    def body(*refs):
        r_ref, own_ref = refs[0], refs[len(slabs) - 1]
        w_ref, m_ref, v_ref, g_ref, d_ref, nm_ref, nv_ref = refs[len(slabs) + len(extra):]
        chip = (4 * lax.axis_index("x") + 2 * lax.axis_index("y") + lax.axis_index("c")) >> 1
        g = None
        for slot in range(n_slots):
            part = r_ref[slot]
            if len(slabs) == 2:
                part = jnp.where(chip == slot, own_ref[slot], part)
            g = part.astype(F32) if g is None else g + part.astype(F32)
        g_ref[...] = g
        d_ref[...], nm_ref[...], nv_ref[...] = _adamw(w_ref[...], g, m_ref[...], v_ref[...])

    blk = pl.BlockSpec((None,) * lead + (tr, cols), lambda i: (0,) * lead + (i, 0))
    slots = pl.BlockSpec((n_slots, tr, cols), lambda i: (0, i, 0))
    return pl.pallas_call(
        body, name=name, grid=(rows // tr,),
        in_specs=[slots] * len(slabs) + [pl.BlockSpec(a.shape, lambda i: (0, 0)) for a in extra] + [blk, blk, blk],
        out_specs=[blk] * 4, out_shape=[_sds(w.shape, F32)] * 4, compiler_params=_params())(*slabs, *extra, w, m, v)


SLAB_ROWS = 16
SLOT = {"kv_norm_g": (8, 0, D), "norm_b_g": (9, 0, D), "b_forget": (10, 0, 16), "qnorm_a_g": (10, 128, HD),
        "knorm_a_g": (10, 256, HD), "knorm_b_g": (10, 384, HD), "qnorm_b_g": (10, 512, HD), "sinks": (10, 640, 16)}
SMALL = ["norm_a_g", "b_forget", "qnorm_a_g", "knorm_a_g", "kv_norm_g", "knorm_b_g", "norm_b_g", "qnorm_b_g", "sinks"]


LOSS_ROW = 11


def _pack_small(dg_a, dg_kv, dg_b, db_f, dgq_a, dgk_a, dgk_b, dgq_b, dsinks, lsum):
    def fold(ref):
        return ref[:, 0:HD] + ref[:, HD:2 * HD]

    def body(dga_ref, dgkv_ref, dgb_ref, dbf_ref, dgqa_ref, dgka_ref, dgkb_ref, dgqb_ref, dsk_ref, ls_ref, slab_ref):
        slab_ref[...] = jnp.zeros_like(slab_ref)
        for r in range(N_DEV):
            slab_ref[r:r + 1, 0:LANES] = dga_ref[:, LANES * r:LANES * (r + 1)]
        slab_ref[8:9, :] = dgkv_ref[...]
        slab_ref[9:10, :] = dgb_ref[...]
        slab_ref[10:11, 0:LANES] = dbf_ref[...]
        slab_ref[10:11, 128:128 + HD] = fold(dgqa_ref)
        slab_ref[10:11, 256:256 + HD] = fold(dgka_ref)
        slab_ref[10:11, 384:384 + HD] = fold(dgkb_ref)
        slab_ref[10:11, 512:512 + HD] = fold(dgqb_ref)
        slab_ref[10:11, 640:640 + LANES] = dsk_ref[...]
        slab_ref[LOSS_ROW:LOSS_ROW + 1, 0:LANES] = ls_ref[...]

    return pl.pallas_call(body, name="pack_small", out_shape=_sds((SLAB_ROWS, D), F32), compiler_params=_params())(
        dg_a, dg_kv, dg_b, db_f, dgq_a, dgk_a, dgk_b, dgq_b, dsinks, lsum)


def _small_adamw(recv, ws, ms, vs):
    k = len(SMALL)

    def body(*refs):
        r_ref = refs[0]
        w_refs, m_refs, v_refs = refs[1:1 + k], refs[1 + k:1 + 2 * k], refs[1 + 2 * k:1 + 3 * k]
        outs = refs[1 + 3 * k:1 + 7 * k]
        loss_ref, tot = refs[1 + 7 * k], refs[2 + 7 * k]
        g = r_ref[0]
        for dev in range(1, N_DEV):
            g = g + r_ref[dev]
        tot[...] = g
        loss_ref[...] = tot[LOSS_ROW:LOSS_ROW + 1, 0:LANES] * (0.5 / D)
        me = 4 * lax.axis_index("x") + 2 * lax.axis_index("y") + lax.axis_index("c")
        for p, name in enumerate(SMALL):
            if name == "norm_a_g":
                mine = lax.broadcasted_iota(jnp.int32, (N_DEV, LANES), 0) == me
                gp = jnp.sum(jnp.where(mine, tot[0:N_DEV, 0:LANES], 0.0), axis=0, keepdims=True)
            else:
                row, lo, width = SLOT[name]
                gp = tot[row:row + 1, lo:lo + width]
            d, nm, nv = _adamw(w_refs[p][...], gp, m_refs[p][...], v_refs[p][...])
            outs[p][...] = gp
            outs[k + p][...] = d
            outs[2 * k + p][...] = nm
            outs[3 * k + p][...] = nv

    shapes = [_sds(w.shape, F32) for w in ws]
    return pl.pallas_call(body, name="small_adamw", out_shape=shapes * 4 + [_sds((1, LANES), F32)],
                          scratch_shapes=[pltpu.VMEM((SLAB_ROWS, D), F32)],
                          compiler_params=_params())(recv, *ws, *ms, *vs)


def _rope_tables(positions):
    inv_freq = jnp.power(jnp.float32(ROPE_THETA), -jnp.arange(0, ROT, 2, dtype=F32) / ROT)
    ang = positions.astype(F32)[:, None] * inv_freq[None, :]
    cos, sin = jnp.cos(ang), jnp.sin(ang)
    c64 = jnp.concatenate([cos, cos, jnp.ones((S, HD - ROT), F32)], axis=-1)
    s64 = jnp.concatenate([-sin, sin, jnp.zeros((S, HD - ROT), F32)], axis=-1)
    return jnp.tile(c64, (1, 2)), jnp.tile(s64, (1, 2))


def _local_step(x, tgt, positions, g_a, wa, b_forget, gq_a, gk_a, g_kv, gk_b, g_b, gq_b, sinks,
                woa_s, wkv_s, wib_s, wob_s, adamw_others):
    nq = S // TQ
    cos2, sin2 = _rope_tables(positions)
    b_pad = jnp.pad(b_forget, ((0, 0), (0, LANES - N_HEADS)))

    u_a, proj, qn, kn, vb, ccol, cbc = _head_a(x, g_a, wa, gq_a, gk_a, b_pad)
    crow = ccol[:, :N_HEADS].T.reshape(N_HEADS, nq, 1, TQ)
    o_a, z_a, lse_a, woa_g, wkv_g, w_in_b, wob_g = _fox_fwd(
        qn, kn, vb, proj, crow, cbc,
        rider=[("gather_rows", woa_s), ("gather_rows", wkv_s), ("gather_cols", wib_s), ("gather_rows", wob_s)])
    w_out_a, w_kv, w_out_b = woa_g.reshape(D, D), wkv_g.reshape(D, 512), wob_g.reshape(D, D)
    h1, u_kv, u_b, kv, pb, qb, ksh, vsh = _head_b(x, z_a, w_out_a, g_kv, g_b, w_kv, w_in_b, gq_b, gk_b, cos2, sin2)
    sinks1 = sinks.reshape(N_HEADS)
    o_b, z_b, lse_b = _swa_fwd(qb, ksh, vsh, pb, sinks1)
    dy, lsum = _out_b_loss(z_b, w_out_b, h1, tgt)
    dw_out_b = _mm(z_b, dy, "tn", 512, 512, S, out_dtype=BF16, name="mm_dw_out_b")
    dz_b = _mm(dy, w_out_b, "nt", 1024, 512, D, name="mm_dz_b")
    dpb, dka, dkb, dva, dvb, dsinks, dgq_b = _swa_bwd(qb, ksh, vsh, dz_b, o_b, lse_b, pb, sinks1, gq_b, cos2, sin2)
    dkv, dgk_b = _prep_kv_bwd(dka, dkb, dva, dvb, kv, gk_b, cos2, sin2)
    dw_in_b = _mm(u_b, dpb, "tn", 512, 512, S, out_dtype=BF16, name="mm_dw_in_b")
    dw_kv = _mm(u_kv, dkv, "tn", 512, 512, S, out_dtype=BF16, name="mm_dw_kv")
    dh1, dg_b, dg_kv = _du_b_rms_bwd(dpb, w_in_b, dkv, w_kv, h1, g_b, g_kv, dy)
    dw_out_a = _mm(z_a, dh1, "tn", 512, 512, S, out_dtype=BF16, name="mm_dw_out_a")
    do_a, dgate_a, delta_a = _fox_bwd_pre(dh1, w_out_a, proj, o_a)
    dk_a, dv_a, dcs, dq_a, drow, r_wob, r_wib, r_wkv, r_woa = _fox_bwd(
        qn, kn, vb, do_a, lse_a, delta_a, crow, cbc,
        rider=[("a2a_rows", dw_out_b), ("a2a_cols", dw_in_b), ("a2a_rows", dw_kv), ("a2a_rows", dw_out_a)])
    drow_col = jnp.pad(drow.reshape(N_HEADS, S).T, ((0, 0), (0, LANES - N_HEADS)))
    dproj, dgq_a, dgk_a, db_f = _prep_a_bwd(dq_a, dk_a, dv_a, dgate_a, drow_col, dcs, proj, b_pad, gq_a, gk_a)
    dwa = _mm(u_a, dproj, "tn", 1024, 256, S, out_dtype=BF16, name="mm_dw_in_a")
    partial = _pair_reduce(_reshard_dwa(dwa))
    started = _chip_exchange_start(partial)
    dx, dg_a = _du_a_rms_bwd(dproj, wa, x, g_a, dh1, after=started[-1])
    others = adamw_others(dict(w_out_a=r_woa, w_kv=r_wkv, w_in_b=r_wib, w_out_b=r_wob), dg_a)
    partial, landed = _chip_exchange_wait(started, others["w_out_b"][0])
    slab = _pack_small(dg_a, dg_kv, dg_b, db_f, dgq_a, dgk_a, dgk_b, dgq_b, dsinks, lsum)
    return dx, (landed, partial), others, _gather_slab(slab, landed)


def kernel(x, positions, norm_a_g, w_in_a, b_forget, qnorm_a_g, knorm_a_g, w_out_a, kv_norm_g, w_kv, knorm_b_g, norm_b_g, w_in_b, qnorm_b_g, sinks, w_out_b, loss_target, m_norm_a_g, m_w_in_a, m_b_forget, m_qnorm_a_g, m_knorm_a_g, m_w_out_a, m_kv_norm_g, m_w_kv, m_knorm_b_g, m_norm_b_g, m_w_in_b, m_qnorm_b_g, m_sinks, m_w_out_b, v_norm_a_g, v_w_in_a, v_b_forget, v_qnorm_a_g, v_knorm_a_g, v_w_out_a, v_kv_norm_g, v_w_kv, v_knorm_b_g, v_norm_b_g, v_w_in_b, v_qnorm_b_g, v_sinks, v_w_out_b):
    wa_g, ga_g, woa_s, wkv_s, wib_s, wob_s = _gather_first(w_in_a, w_out_a, w_kv, w_in_b, w_out_b, norm_a_g)
    state = dict(w_in_a=(w_in_a, m_w_in_a, v_w_in_a), w_out_a=(w_out_a, m_w_out_a, v_w_out_a),
                 w_kv=(w_kv, m_w_kv, v_w_kv), w_in_b=(w_in_b, m_w_in_b, v_w_in_b),
                 w_out_b=(w_out_b, m_w_out_b, v_w_out_b))

    def adamw_others(landed, after):
        return {n: _sum_adamw(r, *state[n], "adamw_" + n, after=after) for n, r in landed.items()}

    dx, r_wa, big, slab_g = _local_step(
        x[0], loss_target[0], positions, ga_g.reshape(1, D), _unshard_wa(wa_g), b_forget, qnorm_a_g, knorm_a_g,
        kv_norm_g.reshape(1, D), knorm_b_g.reshape(1, HD), norm_b_g, qnorm_b_g, sinks, woa_s, wkv_s, wib_s, wob_s,
        adamw_others)
    big["w_in_a"] = _sum_adamw(r_wa, *state["w_in_a"], "adamw_w_in_a")

    r2 = lambda a: a.reshape(1, -1)
    small_w = dict(norm_a_g=norm_a_g, b_forget=b_forget, qnorm_a_g=qnorm_a_g, knorm_a_g=knorm_a_g,
                   kv_norm_g=kv_norm_g, knorm_b_g=knorm_b_g, norm_b_g=norm_b_g, qnorm_b_g=qnorm_b_g, sinks=sinks)
    small_m = dict(norm_a_g=m_norm_a_g, b_forget=m_b_forget, qnorm_a_g=m_qnorm_a_g, knorm_a_g=m_knorm_a_g,
                   kv_norm_g=m_kv_norm_g, knorm_b_g=m_knorm_b_g, norm_b_g=m_norm_b_g, qnorm_b_g=m_qnorm_b_g,
                   sinks=m_sinks)
    small_v = dict(norm_a_g=v_norm_a_g, b_forget=v_b_forget, qnorm_a_g=v_qnorm_a_g, knorm_a_g=v_knorm_a_g,
                   kv_norm_g=v_kv_norm_g, knorm_b_g=v_knorm_b_g, norm_b_g=v_norm_b_g, qnorm_b_g=v_qnorm_b_g,
                   sinks=v_sinks)
    res = _small_adamw(slab_g, [r2(small_w[n]) for n in SMALL], [r2(small_m[n]) for n in SMALL],
                       [r2(small_v[n]) for n in SMALL])
    k = len(SMALL)
    small = {n: [res[q * k + p].reshape(small_w[n].shape) for q in range(4)] for p, n in enumerate(SMALL)}
    loss = res[4 * k][0, 0]

    order = ["norm_a_g", "w_in_a", "b_forget", "qnorm_a_g", "knorm_a_g", "w_out_a", "kv_norm_g", "w_kv",
             "knorm_b_g", "norm_b_g", "w_in_b", "qnorm_b_g", "sinks", "w_out_b"]

    def leaf(n, q):
        return big[n][q] if n in big else small[n][q]

    outs = [loss, dx[None]]
    for q in range(4):
        outs.extend(leaf(n, q) for n in order)
    return tuple(outs)
```

```python
import jax
import jax.numpy as jnp
from jax import lax
from jax.experimental import pallas as pl
from jax.experimental.pallas import tpu as pltpu

F32, BF16 = jnp.float32, jnp.bfloat16

S = 2048
D = 1024
HD = 64
N_HEADS = 16
N_DEV = 8
NA = 4352
GOFF = 3072
FOFF = 4096
RAW_F = 3072
RAW_G = RAW_F + N_HEADS
NA_RAW = 4112
EPS = 1e-6
QSCALE = 0.125
ROPE_THETA = 500000.0
ROT = 16
WIN = 128
TQ = 256
TK = 256
KS = TQ // 2
HPS = 8
HW = HPS * HD
TM = 256
RT = 256
RB = 512
CB = 256
LANES = 128

ADAM_LR, ADAM_B1, ADAM_B2, ADAM_EPS, ADAM_WD, ADAM_STEP = 0.001, 0.9, 0.999, 1e-08, 0.01, 10

VMEM_LIMIT = 56 * 1024 * 1024


def _params():
    return pltpu.CompilerParams(vmem_limit_bytes=VMEM_LIMIT)


def _sds(shape, dtype):
    return jax.ShapeDtypeStruct(shape, dtype)


def _dot_nt(a, b):
    return lax.dot_general(a, b, (((1,), (1,)), ((), ())), preferred_element_type=F32)


def _dot_tn(a, b):
    return lax.dot_general(a, b, (((0,), (0,)), ((), ())), preferred_element_type=F32)


def _dot_nn(a, b):
    return lax.dot_general(a, b, (((1,), (0,)), ((), ())), preferred_element_type=F32)


def _sigmoid(g):
    return 1.0 / (1.0 + jnp.exp(-g))


def _lane_iota(shape):
    return lax.broadcasted_iota(jnp.int32, shape, len(shape) - 1)


def _flips(kind):
    return (2, 4, 6) if kind == "a2a_chips" else tuple(range(1, N_DEV))


def _send_view(kind, ref, dev):
    if kind in ("gather_rows", "gather_cols"):
        return ref
    if kind == "a2a_slots":
        return ref.at[dev]
    if kind == "a2a_chips":
        return ref.at[dev >> 1]
    if kind == "a2a_rows":
        rows = ref.shape[0] // N_DEV
        return ref.at[pl.ds(pl.multiple_of(dev * rows, rows), rows)]
    cols = ref.shape[1] // N_DEV
    return ref.at[:, pl.ds(pl.multiple_of(dev * cols, cols), cols)]


def _land_view(kind, ref, dev):
    if kind == "gather_cols":
        cols = ref.shape[1] // N_DEV
        return ref.at[:, pl.ds(pl.multiple_of(dev * cols, cols), cols)]
    if kind == "a2a_chips":
        return ref.at[dev >> 1]
    return ref.at[dev]


def _landing_sds(kind, arr):
    if kind == "gather_rows":
        return _sds((N_DEV,) + arr.shape, arr.dtype)
    if kind == "gather_cols":
        return _sds((arr.shape[0], N_DEV * arr.shape[1]), arr.dtype)
    if kind == "a2a_rows":
        return _sds((N_DEV, arr.shape[0] // N_DEV, arr.shape[1]), arr.dtype)
    if kind == "a2a_cols":
        return _sds((N_DEV, arr.shape[0], arr.shape[1] // N_DEV), arr.dtype)
    return _sds(arr.shape, arr.dtype)


def _exchange_sems(n_parts):
    n = n_parts * (N_DEV - 1)
    return [pltpu.SemaphoreType.DMA((n,)), pltpu.SemaphoreType.DMA((n,)), pltpu.SemaphoreType.DMA((n_parts,))]


def _exchange_ops(kinds, srcs, dsts, sems, start, wait):
    send_sems, recv_sems, local_sems = sems
    x, y, c = lax.axis_index("x"), lax.axis_index("y"), lax.axis_index("c")
    me = 4 * x + 2 * y + c

    def local(a):
        return pltpu.make_async_copy(_send_view(kinds[a], srcs[a], me), _land_view(kinds[a], dsts[a], me),
                                     local_sems.at[a])

    def remote(a, k, landing_dev):
        peer = (x ^ ((k >> 2) & 1), y ^ ((k >> 1) & 1), c ^ (k & 1))
        sem = a * (N_DEV - 1) + k - 1
        return pltpu.make_async_remote_copy(
            src_ref=_send_view(kinds[a], srcs[a], me ^ k), dst_ref=_land_view(kinds[a], dsts[a], landing_dev),
            send_sem=send_sems.at[sem], recv_sem=recv_sems.at[sem], device_id=peer,
            device_id_type=pl.DeviceIdType.MESH)

    pairs = [(a, k) for k in range(1, N_DEV) for a in range(len(kinds)) if k in _flips(kinds[a])]
    if start:
        for a in range(len(kinds)):
            local(a).start()
        for a, k in pairs:
            remote(a, k, me).start()
    if wait:
        for a, k in pairs:
            remote(a, k, me ^ k).wait_recv()
            remote(a, k, me).wait_send()
        for a in range(len(kinds)):
            local(a).wait()


def _gather_two_level(srcs, dsts, sems, meanwhile=None):
    send_sems, recv_sems, local_sems = sems
    x, y, c = lax.axis_index("x"), lax.axis_index("y"), lax.axis_index("c")
    me, sibling = (x, y, c), (x, y, 1 - c)
    chips = [(1 - x, y), (x, 1 - y), (1 - x, 1 - y)]

    def slot(ref, dev):
        return ref.at[4 * dev[0] + 2 * dev[1] + dev[2]]

    def copy(a, k, block, to, src=None):
        return pltpu.make_async_remote_copy(
            src_ref=slot(dsts[a], block) if src is None else src, dst_ref=slot(dsts[a], block),
            send_sem=send_sems.at[a * (N_DEV - 1) + k], recv_sem=recv_sems.at[a * (N_DEV - 1) + k],
            device_id=to, device_id_type=pl.DeviceIdType.MESH)

    parts = range(len(srcs))
    mine = [pltpu.make_async_copy(srcs[a], slot(dsts[a], me), local_sems.at[a]) for a in parts]
    first = [copy(a, 0, me, sibling, src=srcs[a]) for a in parts]
    first += [copy(a, 1 + j, me, (*chip, c), src=srcs[a]) for j, chip in enumerate(chips) for a in parts]
    for cp in mine + first:
        cp.start()
    if meanwhile is not None:
        meanwhile()
    passed = []
    for j, chip in enumerate(chips):
        for a in parts:
            copy(a, 1 + j, (*chip, c), me).wait_recv()
            fwd = copy(a, 4 + j, (*chip, c), sibling)
            fwd.start()
            passed.append(fwd)
    for a in parts:
        copy(a, 0, sibling, me).wait_recv()
        for j, chip in enumerate(chips):
            copy(a, 4 + j, (*chip, 1 - c), me).wait_recv()
    for cp in first + passed:
        cp.wait_send()
    for cp in mine:
        cp.wait()


def _call(body, *, name, args, in_specs, out_specs, out_shape, grid=(), scratch_shapes=(), aliases=None, rider=()):
    n_in, n_out, n_scr, n_r = len(in_specs), len(out_specs), len(scratch_shapes), len(rider)
    kinds = [kind for kind, _ in rider]

    def kernel_body(*refs):
        c_in, r_in = refs[:n_in], refs[n_in:n_in + n_r]
        c_out = refs[n_in + n_r:n_in + n_r + n_out]
        r_out = refs[n_in + n_r + n_out:n_in + 2 * n_r + n_out]
        rest = refs[n_in + 2 * n_r + n_out:]
        c_scr, sems = rest[:n_scr], rest[n_scr:]
        if n_r:
            assert grid, "a rider needs a gridded call"
            ids = [pl.program_id(ax) for ax in range(len(grid))]
            first, last = ids[0] == 0, ids[0] == grid[0] - 1
            for pid, size in zip(ids[1:], grid[1:]):
                first = first & (pid == 0)
                last = last & (pid == size - 1)
            pl.when(first)(lambda: _exchange_ops(kinds, r_in, r_out, sems, True, False))
        body(*c_in, *c_out, *c_scr)
        if n_r:
            pl.when(last)(lambda: _exchange_ops(kinds, r_in, r_out, sems, False, True))

    anyspec = pl.BlockSpec(memory_space=pl.ANY)
    params = pltpu.CompilerParams(vmem_limit_bytes=VMEM_LIMIT, has_side_effects=bool(n_r))
    outs = pl.pallas_call(
        kernel_body, name=name, grid=grid, in_specs=list(in_specs) + [anyspec] * n_r,
        out_specs=list(out_specs) + [anyspec] * n_r,
        out_shape=list(out_shape) + [_landing_sds(kind, arr) for kind, arr in rider],
        scratch_shapes=list(scratch_shapes) + (_exchange_sems(n_r) if n_r else []),
        input_output_aliases=aliases or {}, compiler_params=params)(*args, *[arr for _, arr in rider])
    return list(outs)


def _mm(a, b, mode, tm, tn, tk, out_dtype=F32, add=None, name="mm", rider=()):
    if mode == "nn":
        (m, k), n = a.shape, b.shape[1]
        a_spec = pl.BlockSpec((tm, tk), lambda i, j, kk: (i, kk))
        b_spec = pl.BlockSpec((tk, tn), lambda i, j, kk: (kk, j))
        dot = _dot_nn
    elif mode == "nt":
        (m, k), n = a.shape, b.shape[0]
        a_spec = pl.BlockSpec((tm, tk), lambda i, j, kk: (i, kk))
        b_spec = pl.BlockSpec((tn, tk), lambda i, j, kk: (j, kk))
        dot = _dot_nt
    else:
        (k, m), n = a.shape, b.shape[1]
        a_spec = pl.BlockSpec((tk, tm), lambda i, j, kk: (kk, i))
        b_spec = pl.BlockSpec((tk, tn), lambda i, j, kk: (kk, j))
        dot = _dot_tn
    assert m % tm == 0 and n % tn == 0 and k % tk == 0, (m, n, k, tm, tn, tk)
    nk = k // tk
    has_add = add is not None

    def body(*refs):
        if has_add:
            a_ref, b_ref, add_ref, o_ref, acc = refs
        else:
            a_ref, b_ref, o_ref, acc = refs
        p = dot(a_ref[...].astype(BF16), b_ref[...].astype(BF16))

        def finish(total):
            if has_add:
                total = add_ref[...] + total
            o_ref[...] = total.astype(out_dtype)

        if nk == 1:
            finish(p)
        else:
            kk = pl.program_id(2)

            @pl.when(kk == 0)
            def _():
                acc[...] = p

            @pl.when(kk > 0)
            def _():
                acc[...] += p

            @pl.when(kk == nk - 1)
            def _():
                finish(acc[...])

    in_specs = [a_spec, b_spec]
    args = [a, b]
    if has_add:
        in_specs.append(pl.BlockSpec((tm, tn), lambda i, j, kk: (i, j)))
        args.append(add)
    acc_shape = (tm, tn) if nk > 1 else (8, LANES)
    outs = _call(body, name=name, args=args, grid=(m // tm, n // tn, nk), in_specs=in_specs,
                 out_specs=[pl.BlockSpec((tm, tn), lambda i, j, kk: (i, j))], out_shape=[_sds((m, n), out_dtype)],
                 scratch_shapes=[pltpu.VMEM(acc_shape, F32)], rider=rider)
    return outs if rider else outs[0]


def _rms_rinv(x):
    return lax.rsqrt(jnp.mean(x * x, axis=-1, keepdims=True) + EPS)


def _rms_bwd_core(du, x, g):
    r = _rms_rinv(x)
    dug = du * g
    dx = r * (dug - x * ((r * r) * jnp.mean(dug * x, axis=-1, keepdims=True)))
    dg = jnp.sum(du * (x * r), axis=0, keepdims=True)
    return dx, dg


def _half_ones():
    r = lax.broadcasted_iota(jnp.int32, (LANES, LANES), 0)
    c = lax.broadcasted_iota(jnp.int32, (LANES, LANES), 1)
    return ((r < HD) == (c < HD)).astype(BF16)


def _half_sum(v, lo_half):
    if lo_half.dtype == jnp.bool_:
        s0 = jnp.sum(jnp.where(lo_half, v, 0.0), axis=-1, keepdims=True)
        s1 = jnp.sum(jnp.where(lo_half, 0.0, v), axis=-1, keepdims=True)
        return jnp.where(lo_half, s0, s1)
    hi = v.astype(BF16)
    lo = (v - hi.astype(F32)).astype(BF16)
    return _dot_nn(hi, lo_half) + _dot_nn(lo, lo_half)


def _head_rinv(x, lo_half):
    return lax.rsqrt(_half_sum(x * x, lo_half) * (1.0 / HD) + EPS)


def _head_norm_bwd(dn, x, g, lo_half):
    r = _head_rinv(x, lo_half)
    dng = dn * g
    dx = r * (dng - x * ((r * r) * (_half_sum(dng * x, lo_half) * (1.0 / HD))))
    dg = jnp.sum(dn * (x * r), axis=0, keepdims=True)
    return dx, dg


def _rope_swap(x, lane):
    l64 = lane & (HD - 1)
    return jnp.where(l64 < ROT // 2, pltpu.roll(x, LANES - ROT // 2, 1), pltpu.roll(x, ROT // 2, 1))


def _rope_fwd(x, cos, sin, lane):
    return x * cos + _rope_swap(x, lane) * sin


def _rope_bwd(dy, cos, sin, lane):
    return dy * cos + jnp.where((lane & (HD - 1)) < ROT, _rope_swap(dy * sin, lane), 0.0)


def _g2(g_ref):
    g = g_ref[...]
    return jnp.concatenate([g, g], axis=-1)


def _pairs(width):
    return [slice(LANES * c, LANES * (c + 1)) for c in range(width // LANES)]


def _pick_lane(block, lane, idx):
    return jnp.sum(jnp.where(lane == idx, block, 0.0), axis=-1, keepdims=True)


def _head_a(x, g, wa, gq, gk, b_pad):
    def body(x_ref, g_ref, w_ref, gq_ref, gk_ref, b_ref, u_ref, p_ref, qo_ref, ko_ref, vo_ref, c_ref, cbc_ref, carry):
        @pl.when(pl.program_id(0) == 0)
        def _():
            carry[...] = jnp.zeros_like(carry)

        xv = x_ref[...]
        u = ((xv * _rms_rinv(xv)) * g_ref[...]).astype(BF16)
        u_ref[...] = u
        for lo in range(0, NA, D):
            hi = min(lo + D, NA)
            p_ref[:, lo:hi] = _dot_nn(u, w_ref[:, lo:hi])
        lane = _lane_iota((RT, LANES))
        lo_half = lane < HD
        gq2, gk2 = _g2(gq_ref), _g2(gk_ref)
        for c in _pairs(D):
            q = p_ref[:, c]
            k = p_ref[:, D + c.start:D + c.stop]
            qo_ref[:, c] = (((q * _head_rinv(q, lo_half)) * gq2) * QSCALE).astype(BF16)
            ko_ref[:, c] = ((k * _head_rinv(k, lo_half)) * gk2).astype(BF16)
        vo_ref[...] = p_ref[:, 2 * D:3 * D].astype(BF16)

        z = p_ref[:, FOFF:FOFF + LANES] + b_ref[...]
        logf = jnp.minimum(z, 0.0) - jnp.log1p(jnp.exp(-jnp.abs(z)))
        r = lax.broadcasted_iota(jnp.int32, (RT, RT), 0)
        cc = lax.broadcasted_iota(jnp.int32, (RT, RT), 1)
        tri = (r >= cc).astype(F32)
        loc = jnp.dot(tri, logf, precision=lax.Precision.HIGHEST, preferred_element_type=F32) + carry[0:1, :]
        c_ref[...] = loc
        carry[0:1, :] = loc[RT - 1:RT, :]
        for h in range(N_HEADS):
            cbc_ref[:, LANES * h:LANES * (h + 1)] = jnp.broadcast_to(_pick_lane(loc, lane, h), (RT, LANES))

    row = lambda width: pl.BlockSpec((RT, width), lambda i: (i, 0))
    whole = lambda arr: pl.BlockSpec(arr.shape, lambda i: (0,) * arr.ndim, pipeline_mode=pl.Buffered(1))
    return pl.pallas_call(
        body, name="head_a", grid=(S // RT,),
        in_specs=[row(D), whole(g), whole(wa), whole(gq), whole(gk), whole(b_pad)],
        out_specs=[row(D), row(NA), row(D), row(D), row(D), row(LANES), row(N_HEADS * LANES)],
        out_shape=[_sds((S, D), BF16), _sds((S, NA), F32)] + [_sds((S, D), BF16)] * 3
        + [_sds((S, LANES), F32), _sds((S, N_HEADS * LANES), F32)],
        scratch_shapes=[pltpu.VMEM((8, LANES), F32)], compiler_params=_params())(x, g, wa, gq, gk, b_pad)


def _key_le_query(offset, keys=TK):
    r = lax.broadcasted_iota(jnp.int32, (keys, TQ), 0)
    c = lax.broadcasted_iota(jnp.int32, (keys, TQ), 1)
    return (r + offset) <= c


def _widen(tile):
    return jnp.concatenate([tile] * (TQ // LANES), axis=1)


def _fox_fwd(qn, kn, vb, proj, crow, cbc, rider=()):
    nq = S // TQ

    def body(q_ref, k_ref, v_ref, g_ref, cq_ref, cbc_ref, o_ref, z_ref, lse_ref, st_s, pt_s):
        i = pl.program_id(1)
        qs = [q_ref[:, HD * hh:HD * (hh + 1)] for hh in range(HPS)]
        cqs = [cq_ref[hh, 0] for hh in range(HPS)]

        def scores(s, hh):
            off = pl.multiple_of(s * KS, KS)
            kj = k_ref[pl.ds(off, KS), HD * hh:HD * (hh + 1)]
            return (_dot_nt(kj, qs[hh]) + cqs[hh]) - _widen(cbc_ref[pl.ds(off, KS), LANES * hh:LANES * (hh + 1)])

        def values(s, hh, pt):
            off = pl.multiple_of(s * KS, KS)
            return _dot_tn(v_ref[pl.ds(off, KS), HD * hh:HD * (hh + 1)], pt)

        def step(s, slot, carries, mask=None, last=False):
            if not last:
                for hh in range(HPS):
                    st_s[1 - slot, hh] = scores(s + 1, hh)
            pvs = [values(jnp.maximum(s - 1, 0), hh, pt_s[1 - slot, hh]) for hh in range(HPS)]
            out = []
            for hh in range(HPS):
                m, l, acc = carries[hh]
                st = st_s[slot, hh]
                if mask is not None:
                    st = jnp.where(mask, st, -jnp.inf)
                m_new = jnp.maximum(m, jnp.max(st, axis=0, keepdims=True))
                pt = jnp.exp(st - m_new)
                alpha = jnp.exp(m - m_new)
                pt_s[slot, hh] = pt.astype(BF16)
                out.append((m_new, alpha * l + jnp.sum(pt, axis=0, keepdims=True), alpha * (acc + pvs[hh])))
            return tuple(out)

        for hh in range(HPS):
            st_s[0, hh] = scores(0, hh)
            pt_s[1, hh] = jnp.zeros((KS, TQ), BF16)
        one = (jnp.full((1, TQ), -jnp.inf, F32), jnp.zeros((1, TQ), F32), jnp.zeros((HD, TQ), F32))
        carries = lax.fori_loop(0, i, lambda t, cr: step(2 * t + 1, 1, step(2 * t, 0, cr)), (one,) * HPS)
        carries = step(2 * i, 0, carries, mask=_key_le_query(0, KS))
        carries = step(2 * i + 1, 1, carries, mask=_key_le_query(KS, KS), last=True)
        accs = []
        for hh in range(HPS):
            m, l, acc = carries[hh]
            acc = acc + values(2 * i + 1, hh, pt_s[1, hh])
            accs.append(acc / l)
            lse_ref[hh, 0] = m + jnp.log(l)
        o = jnp.concatenate(accs, axis=0).T
        o_ref[...] = o
        g = g_ref[...]
        z_ref[...] = (o * (g * _sigmoid(g))).astype(BF16)

    qblk = pl.BlockSpec((TQ, HW), lambda hp, i: (i, hp))
    full = pl.BlockSpec((S, HW), lambda hp, i: (0, hp))
    rows = pl.BlockSpec((HPS, 1, 1, TQ), lambda hp, i: (hp, i, 0, 0))
    return _call(
        body, name="fox_fwd", args=(qn, kn, vb, proj, crow, cbc), grid=(N_HEADS // HPS, nq),
        in_specs=[qblk, full, full,
                  pl.BlockSpec((TQ, HW), lambda hp, i: (i, GOFF // HW + hp)),
                  rows, pl.BlockSpec((S, HPS * LANES), lambda hp, i: (0, hp))],
        out_specs=[qblk, qblk, rows],
        out_shape=[_sds((S, D), F32), _sds((S, D), BF16), _sds((N_HEADS, nq, 1, TQ), F32)],
        scratch_shapes=[pltpu.VMEM((2, HPS, KS, TQ), F32), pltpu.VMEM((2, HPS, KS, TQ), BF16)], rider=rider)


def _fox_bwd_pre(dh, w_out, proj, o):
    nq, rows = S // TQ, 2 * TQ
    per = rows // TQ

    def body(dh_ref, w_ref, g_ref, o_ref, do_ref, dg_ref, delta_ref):
        g = g_ref[...]
        sg = _sigmoid(g)
        dzv = _dot_nt(dh_ref[...].astype(BF16), w_ref[...])
        ov = o_ref[...]
        do = dzv * (g * sg)
        dg_ref[...] = (dzv * ov * (sg * (1.0 + g * (1.0 - sg)))).astype(BF16)
        do_ref[...] = do.astype(BF16)
        prod_t = (do * ov).T
        for h in range(N_HEADS):
            for b in range(per):
                delta_ref[h, b] = jnp.sum(prod_t[HD * h:HD * (h + 1), TQ * b:TQ * (b + 1)], axis=0, keepdims=True)

    row = pl.BlockSpec((rows, D), lambda i: (i, 0))
    return pl.pallas_call(
        body, name="fox_bwd_pre", grid=(S // rows,),
        in_specs=[row, pl.BlockSpec(w_out.shape, lambda i: (0, 0), pipeline_mode=pl.Buffered(1)),
                  pl.BlockSpec((rows, D), lambda i: (i, GOFF // D)), row],
        out_specs=[row, row, pl.BlockSpec((N_HEADS, per, 1, TQ), lambda i: (0, i, 0, 0))],
        out_shape=[_sds((S, D), BF16), _sds((S, D), BF16), _sds((N_HEADS, nq, 1, TQ), F32)],
        compiler_params=_params())(dh, w_out, proj, o)


def _fox_bwd(qn, kn, vb, dob, lse, delta, crow, cbc, rider=()):
    nq, nkb = S // TQ, S // TK

    def body(q_ref, k_ref, v_ref, do_ref, lse_ref, del_ref, cq_ref, cbc_ref,
             dk_ref, dv_ref, dcs_ref, dq_ref, dr_ref, st_s, dp_s, pt_s, ds_s, dq_acc, dr_acc):
        j = pl.program_id(1)

        @pl.when(j == 0)
        def _():
            dq_acc[...] = jnp.zeros_like(dq_acc)
            dr_acc[...] = jnp.zeros_like(dr_acc)

        kjs = [k_ref[:, HD * hh:HD * (hh + 1)] for hh in range(HPS)]
        vjs = [v_ref[:, HD * hh:HD * (hh + 1)] for hh in range(HPS)]

        def rows_of(ref, u, hh):
            off = pl.multiple_of(u * TQ, TQ)
            return ref[pl.ds(off, TQ), HD * hh:HD * (hh + 1)]

        def products(u, hh):
            st = (_dot_nt(kjs[hh], rows_of(q_ref, u, hh)) + cq_ref[hh, u]) - _widen(
                cbc_ref[:, LANES * hh:LANES * (hh + 1)])
            return st, _dot_nt(vjs[hh], rows_of(do_ref, u, hh))

        def step(u, slot, carries, masked=False):
            nxt = jnp.minimum(u + 1, nq - 1)
            for hh in range(HPS):
                st_s[1 - slot, hh], dp_s[1 - slot, hh] = products(nxt, hh)
            prev = jnp.maximum(u - 1, 0)
            dvs = [_dot_nn(pt_s[1 - slot, hh], rows_of(do_ref, prev, hh)) for hh in range(HPS)]
            dks = [_dot_nn(ds_s[1 - slot, hh], rows_of(q_ref, prev, hh)) for hh in range(HPS)]
            for hh in range(HPS):
                dq_acc[hh, prev] += _dot_tn(kjs[hh], ds_s[1 - slot, hh])
            out = []
            for hh in range(HPS):
                dk, dv, dcs = carries[hh]
                st = st_s[slot, hh]
                if masked:
                    st = jnp.where(_key_le_query((j - u) * TQ), st, -jnp.inf)
                pt = jnp.exp(st - lse_ref[hh, u])
                dst = pt * (dp_s[slot, hh] - del_ref[hh, u])
                pt_s[slot, hh] = pt.astype(BF16)
                ds_s[slot, hh] = dst.astype(BF16)
                dr_acc[hh, u] += jnp.sum(dst, axis=0, keepdims=True)
                out.append((dk + dks[hh], dv + dvs[hh], dcs + (dst[:, :LANES] + dst[:, LANES:])))
            return tuple(out)

        t0 = j // 2
        for hh in range(HPS):
            st_s[0, hh], dp_s[0, hh] = products(2 * t0, hh)
            pt_s[1, hh] = jnp.zeros((TK, TQ), BF16)
            ds_s[1, hh] = jnp.zeros((TK, TQ), BF16)
        one = (jnp.zeros((TK, HD), F32), jnp.zeros((TK, HD), F32), jnp.zeros((TK, LANES), F32))
        carries = step(2 * t0 + 1, 1, step(2 * t0, 0, (one,) * HPS, masked=True), masked=True)
        carries = lax.fori_loop(t0 + 1, nq // 2, lambda t, cr: step(2 * t + 1, 1, step(2 * t, 0, cr)), carries)
        dks, dvs = [], []
        lane = _lane_iota((TK, LANES))
        dcs_all = jnp.zeros((TK, LANES), F32)
        for hh in range(HPS):
            dk, dv, dcs = carries[hh]
            dks.append(dk + _dot_nn(ds_s[1, hh], rows_of(q_ref, nq - 1, hh)))
            dvs.append(dv + _dot_nn(pt_s[1, hh], rows_of(do_ref, nq - 1, hh)))
            dq_acc[hh, nq - 1] += _dot_tn(kjs[hh], ds_s[1, hh])
            dcs_all = jnp.where(lane == HPS * pl.program_id(0) + hh, -jnp.sum(dcs, axis=-1, keepdims=True), dcs_all)
        dcs_ref[0] = dcs_all
        dk_ref[...] = jnp.concatenate(dks, axis=-1)
        dv_ref[...] = jnp.concatenate(dvs, axis=-1).astype(BF16)

        @pl.when(j == nkb - 1)
        def _():
            for i in range(nq):
                dq_ref[TQ * i:TQ * (i + 1), :] = jnp.concatenate([dq_acc[hh, i] for hh in range(HPS)], axis=0).T
            dr_ref[...] = dr_acc[...]

    kblk = pl.BlockSpec((TK, HW), lambda hp, j: (j, hp))
    full = pl.BlockSpec((S, HW), lambda hp, j: (0, hp))
    rows = pl.BlockSpec((HPS, nq, 1, TQ), lambda hp, j: (hp, 0, 0, 0))
    cblk = pl.BlockSpec((TK, HPS * LANES), lambda hp, j: (j, hp))
    return _call(
        body, name="fox_bwd", args=(qn, kn, vb, dob, lse, delta, crow, cbc), grid=(N_HEADS // HPS, nkb),
        in_specs=[full, kblk, kblk, full, rows, rows, rows, cblk],
        out_specs=[kblk, kblk, pl.BlockSpec((1, TK, LANES), lambda hp, j: (hp, j, 0)), full, rows],
        out_shape=[_sds((S, D), F32), _sds((S, D), BF16), _sds((N_HEADS // HPS, S, LANES), F32), _sds((S, D), F32),
                   _sds((N_HEADS, nq, 1, TQ), F32)],
        scratch_shapes=[pltpu.VMEM((2, HPS, TK, TQ), F32), pltpu.VMEM((2, HPS, TK, TQ), F32),
                        pltpu.VMEM((2, HPS, TK, TQ), BF16), pltpu.VMEM((2, HPS, TK, TQ), BF16),
                        pltpu.VMEM((HPS, nq, HD, TQ), F32), pltpu.VMEM((HPS, nq, 1, TQ), F32)], rider=rider)


def _prep_a_bwd(dq, dk, dv, dgate, drow, dcs, proj, b_pad, gq, gk):
    nt = S // TM

    def body(dq_ref, dk_ref, dv_ref, dgt_ref, dr_ref, dcs_ref, xq_ref, xk_ref, f_ref, b_ref, gq_ref, gk_ref,
             o_ref, dgq_ref, dgk_ref, db_ref, carry):
        @pl.when(pl.program_id(0) == 0)
        def _():
            carry[...] = jnp.zeros_like(carry)
            dgq_ref[...] = jnp.zeros_like(dgq_ref)
            dgk_ref[...] = jnp.zeros_like(dgk_ref)
            db_ref[...] = jnp.zeros_like(db_ref)

        lane = _lane_iota((TM, LANES))
        lo_half = _half_ones()
        gq2, gk2 = _g2(gq_ref), _g2(gk_ref)
        dgq, dgk = jnp.zeros((1, LANES), F32), jnp.zeros((1, LANES), F32)
        for c in _pairs(D):
            dxq, dg = _head_norm_bwd(dq_ref[:, c] * QSCALE, xq_ref[:, c], gq2, lo_half)
            o_ref[:, c] = dxq.astype(BF16)
            dgq = dgq + dg
            dxk, dg = _head_norm_bwd(dk_ref[:, c], xk_ref[:, c], gk2, lo_half)
            o_ref[:, D + c.start:D + c.stop] = dxk.astype(BF16)
            dgk = dgk + dg
        dgq_ref[...] += dgq
        dgk_ref[...] += dgk
        o_ref[:, 2 * D:3 * D] = dv_ref[...]
        o_ref[:, GOFF:GOFF + D] = dgt_ref[...]

        dc = dr_ref[...]
        for group in range(N_HEADS // HPS):
            dc = dc + dcs_ref[group]
        r = lax.broadcasted_iota(jnp.int32, (TM, TM), 0)
        c = lax.broadcasted_iota(jnp.int32, (TM, TM), 1)
        tri = (c >= r).astype(F32)
        dlogf = jnp.dot(tri, dc, precision=lax.Precision.HIGHEST, preferred_element_type=F32) + carry[0:1, :]
        carry[0:1, :] = dlogf[0:1, :]
        df = dlogf * (1.0 / (1.0 + jnp.exp(f_ref[...] + b_ref[...])))
        db_ref[...] += jnp.sum(df, axis=0, keepdims=True)
        o_ref[:, FOFF:FOFF + LANES] = df.astype(BF16)
        o_ref[:, FOFF + LANES:NA] = jnp.zeros((TM, NA - FOFF - LANES), BF16)

    rev = lambda width, col: pl.BlockSpec((TM, width), lambda i: (nt - 1 - i, col))
    gspec = pl.BlockSpec((1, HD), lambda i: (0, 0))
    acc = pl.BlockSpec((1, LANES), lambda i: (0, 0))
    return pl.pallas_call(
        body, name="prep_a_bwd", grid=(nt,),
        in_specs=[rev(D, 0), rev(D, 0), rev(D, 0), rev(D, 0), rev(LANES, 0),
                  pl.BlockSpec((N_HEADS // HPS, TM, LANES), lambda i: (0, nt - 1 - i, 0)),
                  rev(D, 0), rev(D, 1), rev(LANES, FOFF // LANES), acc, gspec, gspec],
        out_specs=[rev(NA, 0), acc, acc, acc],
        out_shape=[_sds((S, NA), BF16)] + [_sds((1, LANES), F32)] * 3,
        scratch_shapes=[pltpu.VMEM((8, LANES), F32)],
        compiler_params=_params())(dq, dk, dv, dgate, drow, dcs, proj, proj, proj, b_pad, gq, gk)


def _head_b(x, z_a, w_out_a, g_kv, g_b, w_kv, w_in_b, gq, gk, cos2, sin2):
    nkv = w_kv.shape[1] // 2

    def body(x_ref, z_ref, wo_ref, gkv_ref, gb_ref, wkv_ref, wb_ref, gq_ref, gk_ref, c_ref, s_ref,
             h_ref, ukv_ref, ub_ref, kv_ref, pb_ref, qo_ref, ko_ref, vo_ref):
        xv = x_ref[...] + _dot_nn(z_ref[...], wo_ref[...])
        h_ref[...] = xv
        xn = xv * _rms_rinv(xv)
        ukv = (xn * gkv_ref[...]).astype(BF16)
        ub = (xn * gb_ref[...]).astype(BF16)
        ukv_ref[...] = ukv
        ub_ref[...] = ub
        kv_ref[...] = _dot_nn(ukv, wkv_ref[...])
        for lo in range(0, 2 * D, D):
            pb_ref[:, lo:lo + D] = _dot_nn(ub, wb_ref[:, lo:lo + D])
        lane = _lane_iota((RT, LANES))
        lo_half = lane < HD
        cos, sin = c_ref[...], s_ref[...]
        gq2, gk2 = _g2(gq_ref), _g2(gk_ref)
        for c in _pairs(D):
            q = pb_ref[:, c]
            qo_ref[:, c] = (_rope_fwd((q * _head_rinv(q, lo_half)) * gq2, cos, sin, lane) * QSCALE).astype(BF16)
        for c in _pairs(nkv):
            k = kv_ref[:, c]
            ko_ref[:, c] = _rope_fwd((k * _head_rinv(k, lo_half)) * gk2, cos, sin, lane).astype(BF16)
        vo_ref[...] = kv_ref[:, nkv:2 * nkv].astype(BF16)

    row = lambda width: pl.BlockSpec((RT, width), lambda i: (i, 0))
    whole = lambda arr: pl.BlockSpec(arr.shape, lambda i: (0,) * arr.ndim, pipeline_mode=pl.Buffered(1))
    return pl.pallas_call(
        body, name="head_b", grid=(S // RT,),
        in_specs=[row(D), row(D), whole(w_out_a), whole(g_kv), whole(g_b), whole(w_kv), whole(w_in_b), whole(gq),
                  whole(gk), row(LANES), row(LANES)],
        out_specs=[row(D), row(D), row(D), row(2 * nkv), row(2 * D), row(D), row(nkv), row(nkv)],
        out_shape=[_sds((S, D), F32), _sds((S, D), BF16), _sds((S, D), BF16), _sds((S, 2 * nkv), F32),
                   _sds((S, 2 * D), F32), _sds((S, D), BF16), _sds((S, nkv), BF16), _sds((S, nkv), BF16)],
        compiler_params=_params())(x, z_a, w_out_a, g_kv, g_b, w_kv, w_in_b, gq, gk, cos2, sin2)


N_KV, GRP = 4, 4


def _swa_mask(n):
    r = lax.broadcasted_iota(jnp.int32, (2 * WIN, GRP * WIN), 0)
    q = lax.broadcasted_iota(jnp.int32, (2 * WIN, GRP * WIN), 1) & (WIN - 1)
    return (r > q) & (r <= q + WIN) & ((r >= WIN) | (n > 0))


def _stack4(ref_or_val, base):
    return jnp.concatenate([ref_or_val[:, base + HD * g: base + HD * (g + 1)] for g in range(GRP)], axis=0)


def _unstack4(xt):
    return jnp.concatenate([xt[:, WIN * g:WIN * (g + 1)] for g in range(GRP)], axis=0).T


def _band(prev_ref, cur_ref, kh):
    return jnp.concatenate([prev_ref[:, HD * kh:HD * (kh + 1)], cur_ref[:, HD * kh:HD * (kh + 1)]], axis=0)


def _sink_row(s_ref, first):
    lane = _lane_iota((1, GRP * WIN))
    row = jnp.full((1, GRP * WIN), s_ref[first + GRP - 1], F32)
    for g in range(GRP - 2, -1, -1):
        row = jnp.where(lane < WIN * (g + 1), s_ref[first + g], row)
    return row


def _swa_fwd(qb, ksh, vsh, pb, sinks):
    nb = S // WIN

    def body(q_ref, kp_ref, kc_ref, vp_ref, vc_ref, g_ref, s_ref, o_ref, z_ref, lse_ref):
        n = pl.program_id(0)
        valid = _swa_mask(n)
        outs = []
        for kh in range(N_KV):
            kb, vb = _band(kp_ref, kc_ref, kh), _band(vp_ref, vc_ref, kh)
            st = jnp.where(valid, _dot_nt(kb, _stack4(q_ref, GRP * HD * kh)), -jnp.inf)
            sink = _sink_row(s_ref, GRP * kh)
            m = jnp.maximum(jnp.max(st, axis=0, keepdims=True), sink)
            pt = jnp.exp(st - m)
            l = jnp.sum(pt, axis=0, keepdims=True) + jnp.exp(sink - m)
            outs.append(_unstack4(_dot_tn(vb, pt.astype(BF16)) / l))
            lse = m + jnp.log(l)
            for g in range(GRP):
                lse_ref[GRP * kh + g, 0] = lse[:, WIN * g:WIN * (g + 1)]
        o = jnp.concatenate(outs, axis=-1)
        o_ref[...] = o
        g = g_ref[...]
        z_ref[...] = (o * (g * _sigmoid(g))).astype(BF16)

    row = pl.BlockSpec((WIN, D), lambda n: (n, 0))
    prev = pl.BlockSpec((WIN, N_KV * HD), lambda n: (jnp.maximum(n - 1, 0), 0))
    cur = pl.BlockSpec((WIN, N_KV * HD), lambda n: (n, 0))
    return pl.pallas_call(
        body, name="swa_fwd", grid=(nb,),
        in_specs=[row, prev, cur, prev, cur, pl.BlockSpec((WIN, D), lambda n: (n, 1)),
                  pl.BlockSpec(memory_space=pltpu.SMEM)],
        out_specs=[row, row, pl.BlockSpec((N_HEADS, 1, 1, WIN), lambda n: (0, n, 0, 0))],
        out_shape=[_sds((S, D), F32), _sds((S, D), BF16), _sds((N_HEADS, nb, 1, WIN), F32)],
        compiler_params=_params())(qb, ksh, ksh, vsh, vsh, pb, sinks)


def _swa_bwd(qb, ksh, vsh, dz, o, lse, pb, sinks, gq, cos2, sin2):
    nb = S // WIN

    def body(q_ref, kp_ref, kc_ref, vp_ref, vc_ref, dz_ref, o_ref, lse_ref, x_ref, g_ref, s_ref, gq_ref, c_ref, sn_ref,
             dpb_ref, dka_ref, dkb_ref, dva_ref, dvb_ref, dsink_ref, dgq_ref):
        n = pl.program_id(0)

        @pl.when(n == 0)
        def _():
            dsink_ref[...] = jnp.zeros_like(dsink_ref)
            dgq_ref[...] = jnp.zeros_like(dgq_ref)

        valid = _swa_mask(n)
        g = g_ref[...]
        sg = _sigmoid(g)
        dzv = dz_ref[...]
        ov = o_ref[...]
        do = dzv * (g * sg)
        dpb_ref[:, D:2 * D] = (dzv * ov * (sg * (1.0 + g * (1.0 - sg)))).astype(BF16)
        prod_t = (do * ov).T
        lane1 = _lane_iota((1, LANES))
        dqs, dkas, dkbs, dvas, dvbs = [], [], [], [], []
        dsink = jnp.zeros((1, LANES), F32)
        for kh in range(N_KV):
            kb, vb = _band(kp_ref, kc_ref, kh), _band(vp_ref, vc_ref, kh)
            base = GRP * HD * kh
            qs = _stack4(q_ref, base)
            dos = _stack4(do, base).astype(BF16)
            delta = jnp.concatenate(
                [jnp.sum(prod_t[base + HD * gg:base + HD * (gg + 1), :], axis=0, keepdims=True)
                 for gg in range(GRP)], axis=1)
            lse = jnp.concatenate([lse_ref[GRP * kh + gg, 0] for gg in range(GRP)], axis=1)
            st = jnp.where(valid, _dot_nt(kb, qs), -jnp.inf)
            pt = jnp.exp(st - lse)
            dst = pt * (_dot_nt(vb, dos) - delta)
            dsb = dst.astype(BF16)
            dqs.append(_unstack4(_dot_tn(kb, dsb)))
            dkband = _dot_nn(dsb, qs)
            dvband = _dot_nn(pt.astype(BF16), dos)
            dkbs.append(dkband[0:WIN, :])
            dkas.append(dkband[WIN:2 * WIN, :])
            dvbs.append(dvband[0:WIN, :])
            dvas.append(dvband[WIN:2 * WIN, :])
            ps_delta = jnp.exp(_sink_row(s_ref, GRP * kh) - lse) * delta
            for gg in range(GRP):
                val = jnp.sum(ps_delta[:, WIN * gg:WIN * (gg + 1)], axis=1, keepdims=True)
                dsink = dsink - jnp.where(lane1 == GRP * kh + gg, val, 0.0)
        dka_ref[...] = jnp.concatenate(dkas, axis=-1)
        dkb_ref[...] = jnp.concatenate(dkbs, axis=-1)
        dva_ref[...] = jnp.concatenate(dvas, axis=-1)
        dvb_ref[...] = jnp.concatenate(dvbs, axis=-1)
        dsink_ref[...] += dsink

        lane = _lane_iota((WIN, LANES))
        g2, cos, sin = _g2(gq_ref), c_ref[...], sn_ref[...]
        lo_half = _half_ones()
        dg_tot = jnp.zeros((1, LANES), F32)
        for kh in range(N_KV):
            for c in _pairs(GRP * HD):
                cols = slice(GRP * HD * kh + c.start, GRP * HD * kh + c.stop)
                dn = _rope_bwd(dqs[kh][:, c] * QSCALE, cos, sin, lane)
                dx, dg = _head_norm_bwd(dn, x_ref[:, cols], g2, lo_half)
                dpb_ref[:, cols] = dx.astype(BF16)
                dg_tot = dg_tot + dg
        dgq_ref[...] += dg_tot

    row = pl.BlockSpec((WIN, D), lambda n: (n, 0))
    prev = pl.BlockSpec((WIN, N_KV * HD), lambda n: (jnp.maximum(n - 1, 0), 0))
    cur = pl.BlockSpec((WIN, N_KV * HD), lambda n: (n, 0))
    acc = pl.BlockSpec((1, LANES), lambda n: (0, 0))
    tab = pl.BlockSpec((WIN, LANES), lambda n: (n, 0))
    return pl.pallas_call(
        body, name="swa_bwd", grid=(nb,),
        in_specs=[row, prev, cur, prev, cur, row, row, pl.BlockSpec((N_HEADS, 1, 1, WIN), lambda n: (0, n, 0, 0)),
                  row, pl.BlockSpec((WIN, D), lambda n: (n, 1)), pl.BlockSpec(memory_space=pltpu.SMEM),
                  pl.BlockSpec((1, HD), lambda n: (0, 0)), tab, tab],
        out_specs=[pl.BlockSpec((WIN, 2 * D), lambda n: (n, 0)), cur, cur, cur, cur, acc, acc],
        out_shape=[_sds((S, 2 * D), BF16)] + [_sds((S, 256), F32)] * 4 + [_sds((1, LANES), F32)] * 2,
        compiler_params=_params())(qb, ksh, ksh, vsh, vsh, dz, o, lse, pb, pb, sinks, gq, cos2, sin2)


def _prep_kv_bwd(dka, dkb, dva, dvb, kv, gk, cos2, sin2):
    nt = S // RB
    per = RB // WIN

    def shifted(cur_ref, nxt_ref, has_next):
        return jnp.concatenate([cur_ref[WIN:RB, :], jnp.where(has_next, nxt_ref[...], 0.0)], axis=0)

    def body(dka_ref, dkb_ref, dkn_ref, dva_ref, dvb_ref, dvn_ref, x_ref, g_ref, c_ref, s_ref, o_ref, dgk_ref):
        i, j = pl.program_id(0), pl.program_id(1)

        @pl.when((i == 0) & (j == 0))
        def _():
            dgk_ref[...] = jnp.zeros_like(dgk_ref)

        has_next = i < nt - 1

        @pl.when(j == 0)
        def _():
            lane = _lane_iota((RB, LANES))
            g2, cos, sin = _g2(g_ref), c_ref[...], s_ref[...]
            dy_all = dka_ref[...] + shifted(dkb_ref, dkn_ref, has_next)
            dg_tot = jnp.zeros((1, LANES), F32)
            lo_half = _half_ones()
            for c in _pairs(CB):
                dn = _rope_bwd(dy_all[:, c], cos, sin, lane)
                dx, dg = _head_norm_bwd(dn, x_ref[:, c], g2, lo_half)
                o_ref[:, c] = dx.astype(BF16)
                dg_tot = dg_tot + dg
            dgk_ref[...] += dg_tot

        @pl.when(j == 1)
        def _():
            o_ref[...] = (dva_ref[...] + shifted(dvb_ref, dvn_ref, has_next)).astype(BF16)

    cur = pl.BlockSpec((RB, CB), lambda i, j: (i, 0))
    nxt = pl.BlockSpec((WIN, CB), lambda i, j: (jnp.minimum(per * (i + 1), S // WIN - 1), 0))
    tab = pl.BlockSpec((RB, LANES), lambda i, j: (i, 0))
    return pl.pallas_call(
        body, name="prep_kv_bwd", grid=(nt, 2),
        in_specs=[cur, cur, nxt, cur, cur, nxt, cur, pl.BlockSpec((1, HD), lambda i, j: (0, 0)), tab, tab],
        out_specs=[pl.BlockSpec((RB, CB), lambda i, j: (i, j)), pl.BlockSpec((1, LANES), lambda i, j: (0, 0))],
        out_shape=[_sds((S, 2 * CB), BF16), _sds((1, LANES), F32)],
        compiler_params=_params())(dka, dkb, dkb, dva, dvb, dvb, kv, gk, cos2, sin2)


def _out_b_loss(z, w_out, h1, tgt):
    tm = 2 * RT

    def body(z_ref, w_ref, h_ref, t_ref, dy_ref, l_ref):
        @pl.when(pl.program_id(0) == 0)
        def _():
            l_ref[...] = jnp.zeros_like(l_ref)

        e = (h_ref[...] + _dot_nn(z_ref[...], w_ref[...])) - t_ref[...]
        dy_ref[...] = e * (1.0 / D)
        l_ref[...] += jnp.sum(jnp.sum(e * e, axis=-1, keepdims=True), axis=0, keepdims=True)

    row = pl.BlockSpec((tm, D), lambda i: (i, 0))
    return pl.pallas_call(
        body, name="out_b_loss", grid=(S // tm,),
        in_specs=[row, pl.BlockSpec(w_out.shape, lambda i: (0, 0), pipeline_mode=pl.Buffered(1)), row, row],
        out_specs=[row, pl.BlockSpec((1, LANES), lambda i: (0, 0))],
        out_shape=[_sds((S, D), F32), _sds((1, LANES), F32)], compiler_params=_params())(z, w_out, h1, tgt)


def _du_a_rms_bwd(dproj, wa, x, g, dres, after):
    tm, tk = 1024, NA // 2
    nk = NA // tk

    def body(a_ref, b_ref, x_ref, g_ref, dr_ref, after_ref, dx_ref, dg_ref, acc):
        i, kk = pl.program_id(0), pl.program_id(1)

        @pl.when((i == 0) & (kk == 0))
        def _():
            dg_ref[...] = jnp.zeros_like(dg_ref)

        p = _dot_nt(a_ref[...], b_ref[...])

        @pl.when(kk == 0)
        def _():
            acc[...] = p

        @pl.when(kk == nk - 1)
        def _():
            dx, dg = _rms_bwd_core(acc[...] + p, x_ref[...], g_ref[...])
            dx_ref[...] = dr_ref[...] + dx
            dg_ref[...] += dg

    assert nk == 2
    row = pl.BlockSpec((tm, D), lambda i, kk: (i, 0))
    vec = pl.BlockSpec((1, D), lambda i, kk: (0, 0))
    return pl.pallas_call(
        body, name="du_a_rms_bwd", grid=(S // tm, nk),
        in_specs=[pl.BlockSpec((tm, tk), lambda i, kk: (i, kk)), pl.BlockSpec((D, tk), lambda i, kk: (0, kk)),
                  row, vec, row, pl.BlockSpec(after.shape, lambda i, kk: (0, 0))],
        out_specs=[row, vec], out_shape=[_sds((S, D), F32), _sds((1, D), F32)],
        scratch_shapes=[pltpu.VMEM((tm, D), F32)], compiler_params=_params())(dproj, wa, x, g, dres, after)


def _du_b_rms_bwd(dpb, w_in_b, dkv, w_kv, h1, g_b, g_kv, dy):
    tm = 2 * RT

    def body(ab_ref, wb_ref, akv_ref, wkv_ref, x_ref, gb_ref, gkv_ref, dy_ref, dh_ref, dgb_ref, dgkv_ref):
        @pl.when(pl.program_id(0) == 0)
        def _():
            dgb_ref[...] = jnp.zeros_like(dgb_ref)
            dgkv_ref[...] = jnp.zeros_like(dgkv_ref)

        x = x_ref[...]
        dx1, dg1 = _rms_bwd_core(_dot_nt(ab_ref[...], wb_ref[...]), x, gb_ref[...])
        dx2, dg2 = _rms_bwd_core(_dot_nt(akv_ref[...], wkv_ref[...]), x, gkv_ref[...])
        dh_ref[...] = dy_ref[...] + dx1 + dx2
        dgb_ref[...] += dg1
        dgkv_ref[...] += dg2

    row = lambda width: pl.BlockSpec((tm, width), lambda i: (i, 0))
    whole = lambda arr: pl.BlockSpec(arr.shape, lambda i: (0, 0), pipeline_mode=pl.Buffered(1))
    vec = pl.BlockSpec((1, D), lambda i: (0, 0))
    return pl.pallas_call(
        body, name="du_b_rms_bwd", grid=(S // tm,),
        in_specs=[row(dpb.shape[1]), whole(w_in_b), row(dkv.shape[1]), whole(w_kv), row(D), vec, vec, row(D)],
        out_specs=[row(D), vec, vec], out_shape=[_sds((S, D), F32), _sds((1, D), F32), _sds((1, D), F32)],
        compiler_params=_params())(dpb, w_in_b, dkv, w_kv, h1, g_b, g_kv, dy)


def _gather_first(w_in_a, w_out_a, w_kv, w_in_b, w_out_b, norm_a_g):
    def body(wia_ref, woa_ref, wkv_ref, wib_ref, wob_ref, ga_ref,
             wa_g, ga_g, woa_s, wkv_s, wib_s, wob_s, wa_s, st_a, st_oa, st_kv, st_ib, st_ob, load_sems, *sems):
        sources = [wia_ref.at[0], woa_ref.at[0], wkv_ref, wib_ref.at[0], wob_ref.at[0]]
        stages = [st_a, st_oa, st_kv, st_ib, st_ob]
        loads = [pltpu.make_async_copy(src, dst, load_sems.at[i]) for i, (src, dst) in enumerate(zip(sources, stages))]
        for cp in loads:
            cp.start()
        loads[0].wait()
        wa_s[...] = st_a[...].astype(BF16)

        def cast_the_rest():
            for cp, stage, out in zip(loads[1:], stages[1:], [woa_s, wkv_s, wib_s, wob_s]):
                cp.wait()
                out[...] = stage[...].astype(BF16)

        _gather_two_level([wa_s, ga_ref], [wa_g, ga_g], sems, meanwhile=cast_the_rest)

    vmem = pl.BlockSpec(memory_space=pltpu.VMEM)
    anyspec = pl.BlockSpec(memory_space=pl.ANY)
    shard = lambda w: _sds(w.shape[-2:], BF16)
    stage = lambda w: pltpu.VMEM(w.shape[-2:], F32)
    return pl.pallas_call(
        body, name="gather_first", in_specs=[anyspec] * 5 + [vmem],
        out_specs=[anyspec, anyspec, vmem, vmem, vmem, vmem],
        out_shape=[_sds((N_DEV,) + w_in_a.shape[-2:], BF16), _sds((N_DEV,) + norm_a_g.shape, F32),
                   shard(w_out_a), shard(w_kv), shard(w_in_b), shard(w_out_b)],
        scratch_shapes=[pltpu.VMEM(w_in_a.shape[-2:], BF16), stage(w_in_a), stage(w_out_a), stage(w_kv),
                        stage(w_in_b), stage(w_out_b), pltpu.SemaphoreType.DMA((5,))] + _exchange_sems(2),
        compiler_params=pltpu.CompilerParams(vmem_limit_bytes=VMEM_LIMIT, has_side_effects=True))(
            w_in_a, w_out_a, w_kv, w_in_b, w_out_b, norm_a_g)


def _pair_reduce(slots):
    n_chip = N_DEV // 2
    _, rows, cols = slots.shape

    def body(s_ref, o_ref, own_v, sib_v, send_sems, recv_sems, local_sems):
        x, y, c = lax.axis_index("x"), lax.axis_index("y"), lax.axis_index("c")
        copies = []
        for j in range(n_chip):
            own = pltpu.make_async_copy(s_ref.at[2 * j + c], own_v.at[j], local_sems.at[j])
            give = pltpu.make_async_remote_copy(
                src_ref=s_ref.at[2 * j + 1 - c], dst_ref=sib_v.at[j], send_sem=send_sems.at[j],
                recv_sem=recv_sems.at[j], device_id=(x, y, 1 - c), device_id_type=pl.DeviceIdType.MESH)
            own.start()
            give.start()
            copies.append((own, give))
        for j, (own, give) in enumerate(copies):
            own.wait()
            give.wait()
            o_ref[j] = (own_v[j].astype(F32) + sib_v[j].astype(F32)).astype(BF16)

    half = _sds((n_chip, rows, cols), slots.dtype)
    return pl.pallas_call(
        body, name="pair_reduce", in_specs=[pl.BlockSpec(memory_space=pl.ANY)],
        out_specs=pl.BlockSpec(memory_space=pltpu.VMEM), out_shape=half,
        scratch_shapes=[pltpu.VMEM(half.shape, half.dtype), pltpu.VMEM(half.shape, half.dtype),
                        pltpu.SemaphoreType.DMA((n_chip,)), pltpu.SemaphoreType.DMA((n_chip,)),
                        pltpu.SemaphoreType.DMA((n_chip,))],
        compiler_params=pltpu.CompilerParams(vmem_limit_bytes=VMEM_LIMIT, has_side_effects=True))(slots)


def _padded_col(c):
    if c < RAW_F:
        return c
    return FOFF + (c - RAW_F) if c < RAW_G else GOFF + (c - RAW_G)


def _shard_pieces():
    width = NA_RAW // N_DEV
    pieces = []
    for d in range(N_DEV):
        cuts = [width * d] + [c for c in (RAW_F, RAW_G) if width * d < c < width * (d + 1)] + [width * (d + 1)]
        for lo, hi in zip(cuts[:-1], cuts[1:]):
            pieces.append((d, lo - width * d, _padded_col(lo), hi - lo))
    return pieces


def _unshard_wa(wa_g):
    def body(w_ref, o_ref):
        o_ref[:, FOFF + N_HEADS:NA] = jnp.zeros((TM, NA - FOFF - N_HEADS), BF16)
        for d, src, dst, width in _shard_pieces():
            o_ref[:, dst:dst + width] = w_ref[d, :, src:src + width]

    return pl.pallas_call(
        body, name="unshard_wa", grid=(D // TM,),
        in_specs=[pl.BlockSpec((N_DEV, TM, NA_RAW // N_DEV), lambda i: (0, i, 0))],
        out_specs=pl.BlockSpec((TM, NA), lambda i: (i, 0)), out_shape=_sds((D, NA), BF16),
        compiler_params=_params())(wa_g)


def _reshard_dwa(dwa):
    def body(g_ref, o_ref):
        for d, src, dst, width in _shard_pieces():
            o_ref[d, :, src:src + width] = g_ref[:, dst:dst + width]

    return pl.pallas_call(
        body, name="reshard_dwa", grid=(D // TM,), in_specs=[pl.BlockSpec((TM, NA), lambda i: (i, 0))],
        out_specs=pl.BlockSpec((N_DEV, TM, NA_RAW // N_DEV), lambda i: (0, i, 0)),
        out_shape=_sds((N_DEV, D, NA_RAW // N_DEV), dwa.dtype), compiler_params=_params())(dwa)


CHIP_FLIPS = (2, 4, 6)


def _chip_exchange_start(partial):
    n = len(CHIP_FLIPS)

    def body(p_ref, land_ref, *rest):
        sends, recvs, token = rest[:n], rest[n:2 * n], rest[2 * n + 2]
        x, y, c = lax.axis_index("x"), lax.axis_index("y"), lax.axis_index("c")
        me = 4 * x + 2 * y + c
        for idx, k in enumerate(CHIP_FLIPS):
            pltpu.make_async_remote_copy(
                src_ref=p_ref.at[(me ^ k) >> 1], dst_ref=land_ref.at[me >> 1], send_sem=sends[idx],
                recv_sem=recvs[idx], device_id=(x ^ ((k >> 2) & 1), y ^ ((k >> 1) & 1), c),
                device_id_type=pl.DeviceIdType.MESH).start()
        token[...] = jnp.zeros_like(token)

    hbm = pl.BlockSpec(memory_space=pltpu.HBM)
    sem = pl.BlockSpec(memory_space=pltpu.SEMAPHORE)
    buf = pltpu.HBM(partial.shape, partial.dtype)
    return pl.pallas_call(
        body, name="chip_exchange_start",
        out_shape=(pltpu.SemaphoreType.DMA(()),) * (2 * n) + (buf, buf, _sds((8, LANES), F32)),
        in_specs=(hbm, hbm), out_specs=(sem,) * (2 * n) + (hbm, hbm, pl.BlockSpec(memory_space=pltpu.VMEM)),
        input_output_aliases={0: 2 * n, 1: 2 * n + 1},
        compiler_params=pltpu.CompilerParams(has_side_effects=pltpu.SideEffectType.DATAFLOW_SIDE_EFFECTING))(
            pltpu.with_memory_space_constraint(partial, pltpu.HBM),
            pltpu.with_memory_space_constraint(lax.empty(partial.shape, partial.dtype), pltpu.HBM))


def _chip_exchange_wait(started, after):
    n = len(CHIP_FLIPS)
    sems, (p_thru, land_thru) = started[:2 * n], started[2 * n:2 * n + 2]

    def body(p_ref, land_ref, *rest):
        sends, recvs = rest[:n], rest[n:2 * n]
        x, y, c = lax.axis_index("x"), lax.axis_index("y"), lax.axis_index("c")
        me = 4 * x + 2 * y + c
        for idx, k in enumerate(CHIP_FLIPS):
            copy = pltpu.make_async_remote_copy(
                src_ref=p_ref.at[(me ^ k) >> 1], dst_ref=land_ref.at[(me ^ k) >> 1], send_sem=sends[idx],
                recv_sem=recvs[idx], device_id=(x ^ ((k >> 2) & 1), y ^ ((k >> 1) & 1), c),
                device_id_type=pl.DeviceIdType.MESH)
            copy.wait_send()
            copy.wait_recv()

    hbm = pl.BlockSpec(memory_space=pltpu.HBM)
    sem = pl.BlockSpec(memory_space=pltpu.SEMAPHORE)
    buf = pltpu.HBM(p_thru.shape, p_thru.dtype)
    return pl.pallas_call(
        body, name="chip_exchange_wait", out_shape=(buf, buf),
        in_specs=(hbm, hbm) + (sem,) * (2 * n) + (pl.BlockSpec(memory_space=pl.ANY),), out_specs=(hbm, hbm),
        input_output_aliases={0: 0, 1: 1},
        compiler_params=pltpu.CompilerParams(has_side_effects=pltpu.SideEffectType.DATAFLOW_SIDE_EFFECTING))(
            p_thru, land_thru, *sems, after)


def _gather_slab(slab, after):
    def body(s_ref, after_ref, o_ref, *sems):
        _exchange_ops(["gather_rows"], [s_ref], [o_ref], sems, True, True)

    anyspec = pl.BlockSpec(memory_space=pl.ANY)
    return pl.pallas_call(
        body, name="gather_slab", in_specs=[anyspec, anyspec], out_specs=anyspec,
        out_shape=_sds((N_DEV,) + slab.shape, slab.dtype), scratch_shapes=_exchange_sems(1),
        compiler_params=pltpu.CompilerParams(has_side_effects=True))(slab, after)


def _adamw(w, g, m, v):
    m = ADAM_B1 * m + (1.0 - ADAM_B1) * g
    v = ADAM_B2 * v + (1.0 - ADAM_B2) * (g * g)
    m_hat = m / (1.0 - ADAM_B1 ** ADAM_STEP)
    v_hat = v / (1.0 - ADAM_B2 ** ADAM_STEP)
    delta = -ADAM_LR * (m_hat / (jnp.sqrt(v_hat) + ADAM_EPS) + ADAM_WD * w)
    return delta, m, v


def _sum_adamw(recv, w, m, v, name, after=None):
    lead = w.ndim - 2
    rows, cols = w.shape[-2:]
    tr = 256 if rows % 256 == 0 else 128
    slabs = list(recv) if isinstance(recv, tuple) else [recv]
    n_slots = slabs[0].shape[0]
    extra = [] if after is None else [after]

    def body(*refs):
        r_ref, own_ref = refs[0], refs[len(slabs) - 1]
        w_ref, m_ref, v_ref, g_ref, d_ref, nm_ref, nv_ref = refs[len(slabs) + len(extra):]
        chip = (4 * lax.axis_index("x") + 2 * lax.axis_index("y") + lax.axis_index("c")) >> 1
        g = None
        for slot in range(n_slots):
            part = r_ref[slot]
            if len(slabs) == 2:
                part = jnp.where(chip == slot, own_ref[slot], part)
            g = part.astype(F32) if g is None else g + part.astype(F32)
        g_ref[...] = g
        d_ref[...], nm_ref[...], nv_ref[...] = _adamw(w_ref[...], g, m_ref[...], v_ref[...])

    blk = pl.BlockSpec((None,) * lead + (tr, cols), lambda i: (0,) * lead + (i, 0))
    slots = pl.BlockSpec((n_slots, tr, cols), lambda i: (0, i, 0))
    return pl.pallas_call(
        body, name=name, grid=(rows // tr,),
        in_specs=[slots] * len(slabs) + [pl.BlockSpec(a.shape, lambda i: (0, 0)) for a in extra] + [blk, blk, blk],
        out_specs=[blk] * 4, out_shape=[_sds(w.shape, F32)] * 4, compiler_params=_params())(*slabs, *extra, w, m, v)


SLAB_ROWS = 16
SLOT = {"kv_norm_g": (8, 0, D), "norm_b_g": (9, 0, D), "b_forget": (10, 0, 16), "qnorm_a_g": (10, 128, HD),
        "knorm_a_g": (10, 256, HD), "knorm_b_g": (10, 384, HD), "qnorm_b_g": (10, 512, HD), "sinks": (10, 640, 16)}
SMALL = ["norm_a_g", "b_forget", "qnorm_a_g", "knorm_a_g", "kv_norm_g", "knorm_b_g", "norm_b_g", "qnorm_b_g", "sinks"]


LOSS_ROW = 11


def _pack_small(dg_a, dg_kv, dg_b, db_f, dgq_a, dgk_a, dgk_b, dgq_b, dsinks, lsum):
    def fold(ref):
        return ref[:, 0:HD] + ref[:, HD:2 * HD]

    def body(dga_ref, dgkv_ref, dgb_ref, dbf_ref, dgqa_ref, dgka_ref, dgkb_ref, dgqb_ref, dsk_ref, ls_ref, slab_ref):
        slab_ref[...] = jnp.zeros_like(slab_ref)
        for r in range(N_DEV):
            slab_ref[r:r + 1, 0:LANES] = dga_ref[:, LANES * r:LANES * (r + 1)]
        slab_ref[8:9, :] = dgkv_ref[...]
        slab_ref[9:10, :] = dgb_ref[...]
        slab_ref[10:11, 0:LANES] = dbf_ref[...]
        slab_ref[10:11, 128:128 + HD] = fold(dgqa_ref)
        slab_ref[10:11, 256:256 + HD] = fold(dgka_ref)
        slab_ref[10:11, 384:384 + HD] = fold(dgkb_ref)
        slab_ref[10:11, 512:512 + HD] = fold(dgqb_ref)
        slab_ref[10:11, 640:640 + LANES] = dsk_ref[...]
        slab_ref[LOSS_ROW:LOSS_ROW + 1, 0:LANES] = ls_ref[...]

    return pl.pallas_call(body, name="pack_small", out_shape=_sds((SLAB_ROWS, D), F32), compiler_params=_params())(
        dg_a, dg_kv, dg_b, db_f, dgq_a, dgk_a, dgk_b, dgq_b, dsinks, lsum)


def _small_adamw(recv, ws, ms, vs):
    k = len(SMALL)

    def body(*refs):
        r_ref = refs[0]
        w_refs, m_refs, v_refs = refs[1:1 + k], refs[1 + k:1 + 2 * k], refs[1 + 2 * k:1 + 3 * k]
        outs = refs[1 + 3 * k:1 + 7 * k]
        loss_ref, tot = refs[1 + 7 * k], refs[2 + 7 * k]
        g = r_ref[0]
        for dev in range(1, N_DEV):
            g = g + r_ref[dev]
        tot[...] = g
        loss_ref[...] = tot[LOSS_ROW:LOSS_ROW + 1, 0:LANES] * (0.5 / D)
        me = 4 * lax.axis_index("x") + 2 * lax.axis_index("y") + lax.axis_index("c")
        for p, name in enumerate(SMALL):
            if name == "norm_a_g":
                mine = lax.broadcasted_iota(jnp.int32, (N_DEV, LANES), 0) == me
                gp = jnp.sum(jnp.where(mine, tot[0:N_DEV, 0:LANES], 0.0), axis=0, keepdims=True)
            else:
                row, lo, width = SLOT[name]
                gp = tot[row:row + 1, lo:lo + width]
            d, nm, nv = _adamw(w_refs[p][...], gp, m_refs[p][...], v_refs[p][...])
            outs[p][...] = gp
            outs[k + p][...] = d
            outs[2 * k + p][...] = nm
            outs[3 * k + p][...] = nv

    shapes = [_sds(w.shape, F32) for w in ws]
    return pl.pallas_call(body, name="small_adamw", out_shape=shapes * 4 + [_sds((1, LANES), F32)],
                          scratch_shapes=[pltpu.VMEM((SLAB_ROWS, D), F32)],
                          compiler_params=_params())(recv, *ws, *ms, *vs)


def _rope_tables(positions):
    inv_freq = jnp.power(jnp.float32(ROPE_THETA), -jnp.arange(0, ROT, 2, dtype=F32) / ROT)
    ang = positions.astype(F32)[:, None] * inv_freq[None, :]
    cos, sin = jnp.cos(ang), jnp.sin(ang)
    c64 = jnp.concatenate([cos, cos, jnp.ones((S, HD - ROT), F32)], axis=-1)
    s64 = jnp.concatenate([-sin, sin, jnp.zeros((S, HD - ROT), F32)], axis=-1)
    return jnp.tile(c64, (1, 2)), jnp.tile(s64, (1, 2))


def _local_step(x, tgt, positions, g_a, wa, b_forget, gq_a, gk_a, g_kv, gk_b, g_b, gq_b, sinks,
                woa_s, wkv_s, wib_s, wob_s, adamw_others):
    nq = S // TQ
    cos2, sin2 = _rope_tables(positions)
    b_pad = jnp.pad(b_forget, ((0, 0), (0, LANES - N_HEADS)))

    u_a, proj, qn, kn, vb, ccol, cbc = _head_a(x, g_a, wa, gq_a, gk_a, b_pad)
    crow = ccol[:, :N_HEADS].T.reshape(N_HEADS, nq, 1, TQ)
    o_a, z_a, lse_a, woa_g, wkv_g, w_in_b, wob_g = _fox_fwd(
        qn, kn, vb, proj, crow, cbc,
        rider=[("gather_rows", woa_s), ("gather_rows", wkv_s), ("gather_cols", wib_s), ("gather_rows", wob_s)])
    w_out_a, w_kv, w_out_b = woa_g.reshape(D, D), wkv_g.reshape(D, 512), wob_g.reshape(D, D)
    h1, u_kv, u_b, kv, pb, qb, ksh, vsh = _head_b(x, z_a, w_out_a, g_kv, g_b, w_kv, w_in_b, gq_b, gk_b, cos2, sin2)
    sinks1 = sinks.reshape(N_HEADS)
    o_b, z_b, lse_b = _swa_fwd(qb, ksh, vsh, pb, sinks1)
    dy, lsum = _out_b_loss(z_b, w_out_b, h1, tgt)
    dw_out_b = _mm(z_b, dy, "tn", 512, 512, S, out_dtype=BF16, name="mm_dw_out_b")
    dz_b = _mm(dy, w_out_b, "nt", 1024, 512, D, name="mm_dz_b")
    dpb, dka, dkb, dva, dvb, dsinks, dgq_b = _swa_bwd(qb, ksh, vsh, dz_b, o_b, lse_b, pb, sinks1, gq_b, cos2, sin2)
    dkv, dgk_b = _prep_kv_bwd(dka, dkb, dva, dvb, kv, gk_b, cos2, sin2)
    dw_in_b = _mm(u_b, dpb, "tn", 512, 512, S, out_dtype=BF16, name="mm_dw_in_b")
    dw_kv = _mm(u_kv, dkv, "tn", 512, 512, S, out_dtype=BF16, name="mm_dw_kv")
    dh1, dg_b, dg_kv = _du_b_rms_bwd(dpb, w_in_b, dkv, w_kv, h1, g_b, g_kv, dy)
    dw_out_a = _mm(z_a, dh1, "tn", 512, 512, S, out_dtype=BF16, name="mm_dw_out_a")
    do_a, dgate_a, delta_a = _fox_bwd_pre(dh1, w_out_a, proj, o_a)
    dk_a, dv_a, dcs, dq_a, drow, r_wob, r_wib, r_wkv, r_woa = _fox_bwd(
        qn, kn, vb, do_a, lse_a, delta_a, crow, cbc,
        rider=[("a2a_rows", dw_out_b), ("a2a_cols", dw_in_b), ("a2a_rows", dw_kv), ("a2a_rows", dw_out_a)])
    drow_col = jnp.pad(drow.reshape(N_HEADS, S).T, ((0, 0), (0, LANES - N_HEADS)))
    dproj, dgq_a, dgk_a, db_f = _prep_a_bwd(dq_a, dk_a, dv_a, dgate_a, drow_col, dcs, proj, b_pad, gq_a, gk_a)
    dwa = _mm(u_a, dproj, "tn", 1024, 256, S, out_dtype=BF16, name="mm_dw_in_a")
    partial = _pair_reduce(_reshard_dwa(dwa))
    started = _chip_exchange_start(partial)
    dx, dg_a = _du_a_rms_bwd(dproj, wa, x, g_a, dh1, after=started[-1])
    others = adamw_others(dict(w_out_a=r_woa, w_kv=r_wkv, w_in_b=r_wib, w_out_b=r_wob), dg_a)
    partial, landed = _chip_exchange_wait(started, others["w_out_b"][0])
    slab = _pack_small(dg_a, dg_kv, dg_b, db_f, dgq_a, dgk_a, dgk_b, dgq_b, dsinks, lsum)
    return dx, (landed, partial), others, _gather_slab(slab, landed)


def kernel(x, positions, norm_a_g, w_in_a, b_forget, qnorm_a_g, knorm_a_g, w_out_a, kv_norm_g, w_kv, knorm_b_g, norm_b_g, w_in_b, qnorm_b_g, sinks, w_out_b, loss_target, m_norm_a_g, m_w_in_a, m_b_forget, m_qnorm_a_g, m_knorm_a_g, m_w_out_a, m_kv_norm_g, m_w_kv, m_knorm_b_g, m_norm_b_g, m_w_in_b, m_qnorm_b_g, m_sinks, m_w_out_b, v_norm_a_g, v_w_in_a, v_b_forget, v_qnorm_a_g, v_knorm_a_g, v_w_out_a, v_kv_norm_g, v_w_kv, v_knorm_b_g, v_norm_b_g, v_w_in_b, v_qnorm_b_g, v_sinks, v_w_out_b):
    wa_g, ga_g, woa_s, wkv_s, wib_s, wob_s = _gather_first(w_in_a, w_out_a, w_kv, w_in_b, w_out_b, norm_a_g)
    state = dict(w_in_a=(w_in_a, m_w_in_a, v_w_in_a), w_out_a=(w_out_a, m_w_out_a, v_w_out_a),
                 w_kv=(w_kv, m_w_kv, v_w_kv), w_in_b=(w_in_b, m_w_in_b, v_w_in_b),
                 w_out_b=(w_out_b, m_w_out_b, v_w_out_b))

    def adamw_others(landed, after):
        return {n: _sum_adamw(r, *state[n], "adamw_" + n, after=after) for n, r in landed.items()}

    dx, r_wa, big, slab_g = _local_step(
        x[0], loss_target[0], positions, ga_g.reshape(1, D), _unshard_wa(wa_g), b_forget, qnorm_a_g, knorm_a_g,
        kv_norm_g.reshape(1, D), knorm_b_g.reshape(1, HD), norm_b_g, qnorm_b_g, sinks, woa_s, wkv_s, wib_s, wob_s,
        adamw_others)
    big["w_in_a"] = _sum_adamw(r_wa, *state["w_in_a"], "adamw_w_in_a")

    r2 = lambda a: a.reshape(1, -1)
    small_w = dict(norm_a_g=norm_a_g, b_forget=b_forget, qnorm_a_g=qnorm_a_g, knorm_a_g=knorm_a_g,
                   kv_norm_g=kv_norm_g, knorm_b_g=knorm_b_g, norm_b_g=norm_b_g, qnorm_b_g=qnorm_b_g, sinks=sinks)
    small_m = dict(norm_a_g=m_norm_a_g, b_forget=m_b_forget, qnorm_a_g=m_qnorm_a_g, knorm_a_g=m_knorm_a_g,
                   kv_norm_g=m_kv_norm_g, knorm_b_g=m_knorm_b_g, norm_b_g=m_norm_b_g, qnorm_b_g=m_qnorm_b_g,
                   sinks=m_sinks)
    small_v = dict(norm_a_g=v_norm_a_g, b_forget=v_b_forget, qnorm_a_g=v_qnorm_a_g, knorm_a_g=v_knorm_a_g,
                   kv_norm_g=v_kv_norm_g, knorm_b_g=v_knorm_b_g, norm_b_g=v_norm_b_g, qnorm_b_g=v_qnorm_b_g,
                   sinks=v_sinks)
    res = _small_adamw(slab_g, [r2(small_w[n]) for n in SMALL], [r2(small_m[n]) for n in SMALL],
                       [r2(small_v[n]) for n in SMALL])
    k = len(SMALL)
    small = {n: [res[q * k + p].reshape(small_w[n].shape) for q in range(4)] for p, n in enumerate(SMALL)}
    loss = res[4 * k][0, 0]

    order = ["norm_a_g", "w_in_a", "b_forget", "qnorm_a_g", "knorm_a_g", "w_out_a", "kv_norm_g", "w_kv",
             "knorm_b_g", "norm_b_g", "w_in_b", "qnorm_b_g", "sinks", "w_out_b"]

    def leaf(n, q):
        return big[n][q] if n in big else small[n][q]

    outs = [loss, dx[None]]
    for q in range(4):
        outs.extend(leaf(n, q) for n in order)
    return tuple(outs)
```

```python
import jax
import jax.numpy as jnp
from jax import lax
from jax.experimental import pallas as pl
from jax.experimental.pallas import tpu as pltpu

F32, BF16 = jnp.float32, jnp.bfloat16

S = 2048
D = 1024
HD = 64
N_HEADS = 16
N_DEV = 8
NA = 4352
GOFF = 3072
FOFF = 4096
RAW_F = 3072
RAW_G = RAW_F + N_HEADS
NA_RAW = 4112
EPS = 1e-6
QSCALE = 0.125
ROPE_THETA = 500000.0
ROT = 16
WIN = 128
TQ = 256
TK = 256
KS = TQ // 2
HPS = 8
HW = HPS * HD
TM = 256
RT = 256
RB = 512
CB = 256
LANES = 128

ADAM_LR, ADAM_B1, ADAM_B2, ADAM_EPS, ADAM_WD, ADAM_STEP = 0.001, 0.9, 0.999, 1e-08, 0.01, 10

VMEM_LIMIT = 56 * 1024 * 1024


def _params():
    return pltpu.CompilerParams(vmem_limit_bytes=VMEM_LIMIT)


def _sds(shape, dtype):
    return jax.ShapeDtypeStruct(shape, dtype)


def _dot_nt(a, b):
    return lax.dot_general(a, b, (((1,), (1,)), ((), ())), preferred_element_type=F32)


def _dot_tn(a, b):
    return lax.dot_general(a, b, (((0,), (0,)), ((), ())), preferred_element_type=F32)


def _dot_nn(a, b):
    return lax.dot_general(a, b, (((1,), (0,)), ((), ())), preferred_element_type=F32)


def _sigmoid(g):
    return 1.0 / (1.0 + jnp.exp(-g))


def _lane_iota(shape):
    return lax.broadcasted_iota(jnp.int32, shape, len(shape) - 1)


def _flips(kind):
    return (2, 4, 6) if kind == "a2a_chips" else tuple(range(1, N_DEV))


def _send_view(kind, ref, dev):
    if kind in ("gather_rows", "gather_cols"):
        return ref
    if kind == "a2a_slots":
        return ref.at[dev]
    if kind == "a2a_chips":
        return ref.at[dev >> 1]
    if kind == "a2a_rows":
        rows = ref.shape[0] // N_DEV
        return ref.at[pl.ds(pl.multiple_of(dev * rows, rows), rows)]
    cols = ref.shape[1] // N_DEV
    return ref.at[:, pl.ds(pl.multiple_of(dev * cols, cols), cols)]


def _land_view(kind, ref, dev):
    if kind == "gather_cols":
        cols = ref.shape[1] // N_DEV
        return ref.at[:, pl.ds(pl.multiple_of(dev * cols, cols), cols)]
    if kind == "a2a_chips":
        return ref.at[dev >> 1]
    return ref.at[dev]


def _landing_sds(kind, arr):
    if kind == "gather_rows":
        return _sds((N_DEV,) + arr.shape, arr.dtype)
    if kind == "gather_cols":
        return _sds((arr.shape[0], N_DEV * arr.shape[1]), arr.dtype)
    if kind == "a2a_rows":
        return _sds((N_DEV, arr.shape[0] // N_DEV, arr.shape[1]), arr.dtype)
    if kind == "a2a_cols":
        return _sds((N_DEV, arr.shape[0], arr.shape[1] // N_DEV), arr.dtype)
    return _sds(arr.shape, arr.dtype)


def _exchange_sems(n_parts):
    n = n_parts * (N_DEV - 1)
    return [pltpu.SemaphoreType.DMA((n,)), pltpu.SemaphoreType.DMA((n,)), pltpu.SemaphoreType.DMA((n_parts,))]


def _exchange_ops(kinds, srcs, dsts, sems, start, wait):
    send_sems, recv_sems, local_sems = sems
    x, y, c = lax.axis_index("x"), lax.axis_index("y"), lax.axis_index("c")
    me = 4 * x + 2 * y + c

    def local(a):
        return pltpu.make_async_copy(_send_view(kinds[a], srcs[a], me), _land_view(kinds[a], dsts[a], me),
                                     local_sems.at[a])

    def remote(a, k, landing_dev):
        peer = (x ^ ((k >> 2) & 1), y ^ ((k >> 1) & 1), c ^ (k & 1))
        sem = a * (N_DEV - 1) + k - 1
        return pltpu.make_async_remote_copy(
            src_ref=_send_view(kinds[a], srcs[a], me ^ k), dst_ref=_land_view(kinds[a], dsts[a], landing_dev),
            send_sem=send_sems.at[sem], recv_sem=recv_sems.at[sem], device_id=peer,
            device_id_type=pl.DeviceIdType.MESH)

    pairs = [(a, k) for k in range(1, N_DEV) for a in range(len(kinds)) if k in _flips(kinds[a])]
    if start:
        for a in range(len(kinds)):
            local(a).start()
        for a, k in pairs:
            remote(a, k, me).start()
    if wait:
        for a, k in pairs:
            remote(a, k, me ^ k).wait_recv()
            remote(a, k, me).wait_send()
        for a in range(len(kinds)):
            local(a).wait()


def _gather_two_level(srcs, dsts, sems, meanwhile=None):
    send_sems, recv_sems, local_sems = sems
    x, y, c = lax.axis_index("x"), lax.axis_index("y"), lax.axis_index("c")
    me, sibling = (x, y, c), (x, y, 1 - c)
    chips = [(1 - x, y), (x, 1 - y), (1 - x, 1 - y)]

    def slot(ref, dev):
        return ref.at[4 * dev[0] + 2 * dev[1] + dev[2]]

    def copy(a, k, block, to, src=None):
        return pltpu.make_async_remote_copy(
            src_ref=slot(dsts[a], block) if src is None else src, dst_ref=slot(dsts[a], block),
            send_sem=send_sems.at[a * (N_DEV - 1) + k], recv_sem=recv_sems.at[a * (N_DEV - 1) + k],
            device_id=to, device_id_type=pl.DeviceIdType.MESH)

    parts = range(len(srcs))
    mine = [pltpu.make_async_copy(srcs[a], slot(dsts[a], me), local_sems.at[a]) for a in parts]
    first = [copy(a, 0, me, sibling, src=srcs[a]) for a in parts]
    first += [copy(a, 1 + j, me, (*chip, c), src=srcs[a]) for j, chip in enumerate(chips) for a in parts]
    for cp in mine + first:
        cp.start()
    if meanwhile is not None:
        meanwhile()
    passed = []
    for j, chip in enumerate(chips):
        for a in parts:
            copy(a, 1 + j, (*chip, c), me).wait_recv()
            fwd = copy(a, 4 + j, (*chip, c), sibling)
            fwd.start()
            passed.append(fwd)
    for a in parts:
        copy(a, 0, sibling, me).wait_recv()
        for j, chip in enumerate(chips):
            copy(a, 4 + j, (*chip, 1 - c), me).wait_recv()
    for cp in first + passed:
        cp.wait_send()
    for cp in mine:
        cp.wait()


def _call(body, *, name, args, in_specs, out_specs, out_shape, grid=(), scratch_shapes=(), aliases=None, rider=()):
    n_in, n_out, n_scr, n_r = len(in_specs), len(out_specs), len(scratch_shapes), len(rider)
    kinds = [kind for kind, _ in rider]

    def kernel_body(*refs):
        c_in, r_in = refs[:n_in], refs[n_in:n_in + n_r]
        c_out = refs[n_in + n_r:n_in + n_r + n_out]
        r_out = refs[n_in + n_r + n_out:n_in + 2 * n_r + n_out]
        rest = refs[n_in + 2 * n_r + n_out:]
        c_scr, sems = rest[:n_scr], rest[n_scr:]
        if n_r:
            assert grid, "a rider needs a gridded call"
            ids = [pl.program_id(ax) for ax in range(len(grid))]
            first, last = ids[0] == 0, ids[0] == grid[0] - 1
            for pid, size in zip(ids[1:], grid[1:]):
                first = first & (pid == 0)
                last = last & (pid == size - 1)
            pl.when(first)(lambda: _exchange_ops(kinds, r_in, r_out, sems, True, False))
        body(*c_in, *c_out, *c_scr)
        if n_r:
            pl.when(last)(lambda: _exchange_ops(kinds, r_in, r_out, sems, False, True))

    anyspec = pl.BlockSpec(memory_space=pl.ANY)
    params = pltpu.CompilerParams(vmem_limit_bytes=VMEM_LIMIT, has_side_effects=bool(n_r))
    outs = pl.pallas_call(
        kernel_body, name=name, grid=grid, in_specs=list(in_specs) + [anyspec] * n_r,
        out_specs=list(out_specs) + [anyspec] * n_r,
        out_shape=list(out_shape) + [_landing_sds(kind, arr) for kind, arr in rider],
        scratch_shapes=list(scratch_shapes) + (_exchange_sems(n_r) if n_r else []),
        input_output_aliases=aliases or {}, compiler_params=params)(*args, *[arr for _, arr in rider])
    return list(outs)


def _mm(a, b, mode, tm, tn, tk, out_dtype=F32, add=None, name="mm", rider=()):
    if mode == "nn":
        (m, k), n = a.shape, b.shape[1]
        a_spec = pl.BlockSpec((tm, tk), lambda i, j, kk: (i, kk))
        b_spec = pl.BlockSpec((tk, tn), lambda i, j, kk: (kk, j))
        dot = _dot_nn
    elif mode == "nt":
        (m, k), n = a.shape, b.shape[0]
        a_spec = pl.BlockSpec((tm, tk), lambda i, j, kk: (i, kk))
        b_spec = pl.BlockSpec((tn, tk), lambda i, j, kk: (j, kk))
        dot = _dot_nt
    else:
        (k, m), n = a.shape, b.shape[1]
        a_spec = pl.BlockSpec((tk, tm), lambda i, j, kk: (kk, i))
        b_spec = pl.BlockSpec((tk, tn), lambda i, j, kk: (kk, j))
        dot = _dot_tn
    assert m % tm == 0 and n % tn == 0 and k % tk == 0, (m, n, k, tm, tn, tk)
    nk = k // tk
    has_add = add is not None

    def body(*refs):
        if has_add:
            a_ref, b_ref, add_ref, o_ref, acc = refs
        else:
            a_ref, b_ref, o_ref, acc = refs
        p = dot(a_ref[...].astype(BF16), b_ref[...].astype(BF16))

        def finish(total):
            if has_add:
                total = add_ref[...] + total
            o_ref[...] = total.astype(out_dtype)

        if nk == 1:
            finish(p)
        else:
            kk = pl.program_id(2)

            @pl.when(kk == 0)
            def _():
                acc[...] = p

            @pl.when(kk > 0)
            def _():
                acc[...] += p

            @pl.when(kk == nk - 1)
            def _():
                finish(acc[...])

    in_specs = [a_spec, b_spec]
    args = [a, b]
    if has_add:
        in_specs.append(pl.BlockSpec((tm, tn), lambda i, j, kk: (i, j)))
        args.append(add)
    acc_shape = (tm, tn) if nk > 1 else (8, LANES)
    outs = _call(body, name=name, args=args, grid=(m // tm, n // tn, nk), in_specs=in_specs,
                 out_specs=[pl.BlockSpec((tm, tn), lambda i, j, kk: (i, j))], out_shape=[_sds((m, n), out_dtype)],
                 scratch_shapes=[pltpu.VMEM(acc_shape, F32)], rider=rider)
    return outs if rider else outs[0]


def _rms_rinv(x):
    return lax.rsqrt(jnp.mean(x * x, axis=-1, keepdims=True) + EPS)


def _rms_bwd_core(du, x, g):
    r = _rms_rinv(x)
    dug = du * g
    dx = r * (dug - x * ((r * r) * jnp.mean(dug * x, axis=-1, keepdims=True)))
    dg = jnp.sum(du * (x * r), axis=0, keepdims=True)
    return dx, dg


def _half_ones():
    r = lax.broadcasted_iota(jnp.int32, (LANES, LANES), 0)
    c = lax.broadcasted_iota(jnp.int32, (LANES, LANES), 1)
    return ((r < HD) == (c < HD)).astype(BF16)


def _half_sum(v, lo_half):
    if lo_half.dtype == jnp.bool_:
        s0 = jnp.sum(jnp.where(lo_half, v, 0.0), axis=-1, keepdims=True)
        s1 = jnp.sum(jnp.where(lo_half, 0.0, v), axis=-1, keepdims=True)
        return jnp.where(lo_half, s0, s1)
    hi = v.astype(BF16)
    lo = (v - hi.astype(F32)).astype(BF16)
    return _dot_nn(hi, lo_half) + _dot_nn(lo, lo_half)


def _head_rinv(x, lo_half):
    return lax.rsqrt(_half_sum(x * x, lo_half) * (1.0 / HD) + EPS)


def _head_norm_bwd(dn, x, g, lo_half):
    r = _head_rinv(x, lo_half)
    dng = dn * g
    dx = r * (dng - x * ((r * r) * (_half_sum(dng * x, lo_half) * (1.0 / HD))))
    dg = jnp.sum(dn * (x * r), axis=0, keepdims=True)
    return dx, dg


def _rope_swap(x, lane):
    l64 = lane & (HD - 1)
    return jnp.where(l64 < ROT // 2, pltpu.roll(x, LANES - ROT // 2, 1), pltpu.roll(x, ROT // 2, 1))


def _rope_fwd(x, cos, sin, lane):
    return x * cos + _rope_swap(x, lane) * sin


def _rope_bwd(dy, cos, sin, lane):
    return dy * cos + jnp.where((lane & (HD - 1)) < ROT, _rope_swap(dy * sin, lane), 0.0)


def _g2(g_ref):
    g = g_ref[...]
    return jnp.concatenate([g, g], axis=-1)


def _pairs(width):
    return [slice(LANES * c, LANES * (c + 1)) for c in range(width // LANES)]


def _pick_lane(block, lane, idx):
    return jnp.sum(jnp.where(lane == idx, block, 0.0), axis=-1, keepdims=True)


def _head_a(x, g, wa, gq, gk, b_pad):
    def body(x_ref, g_ref, w_ref, gq_ref, gk_ref, b_ref, u_ref, p_ref, qo_ref, ko_ref, vo_ref, c_ref, cbc_ref, carry):
        @pl.when(pl.program_id(0) == 0)
        def _():
            carry[...] = jnp.zeros_like(carry)

        xv = x_ref[...]
        u = ((xv * _rms_rinv(xv)) * g_ref[...]).astype(BF16)
        u_ref[...] = u
        for lo in range(0, NA, D):
            hi = min(lo + D, NA)
            p_ref[:, lo:hi] = _dot_nn(u, w_ref[:, lo:hi])
        lane = _lane_iota((RT, LANES))
        lo_half = lane < HD
        gq2, gk2 = _g2(gq_ref), _g2(gk_ref)
        for c in _pairs(D):
            q = p_ref[:, c]
            k = p_ref[:, D + c.start:D + c.stop]
            qo_ref[:, c] = (((q * _head_rinv(q, lo_half)) * gq2) * QSCALE).astype(BF16)
            ko_ref[:, c] = ((k * _head_rinv(k, lo_half)) * gk2).astype(BF16)
        vo_ref[...] = p_ref[:, 2 * D:3 * D].astype(BF16)

        z = p_ref[:, FOFF:FOFF + LANES] + b_ref[...]
        logf = jnp.minimum(z, 0.0) - jnp.log1p(jnp.exp(-jnp.abs(z)))
        r = lax.broadcasted_iota(jnp.int32, (RT, RT), 0)
        cc = lax.broadcasted_iota(jnp.int32, (RT, RT), 1)
        tri = (r >= cc).astype(F32)
        loc = jnp.dot(tri, logf, precision=lax.Precision.HIGHEST, preferred_element_type=F32) + carry[0:1, :]
        c_ref[...] = loc
        carry[0:1, :] = loc[RT - 1:RT, :]
        for h in range(N_HEADS):
            cbc_ref[:, LANES * h:LANES * (h + 1)] = jnp.broadcast_to(_pick_lane(loc, lane, h), (RT, LANES))

    row = lambda width: pl.BlockSpec((RT, width), lambda i: (i, 0))
    whole = lambda arr: pl.BlockSpec(arr.shape, lambda i: (0,) * arr.ndim, pipeline_mode=pl.Buffered(1))
    return pl.pallas_call(
        body, name="head_a", grid=(S // RT,),
        in_specs=[row(D), whole(g), whole(wa), whole(gq), whole(gk), whole(b_pad)],
        out_specs=[row(D), row(NA), row(D), row(D), row(D), row(LANES), row(N_HEADS * LANES)],
        out_shape=[_sds((S, D), BF16), _sds((S, NA), F32)] + [_sds((S, D), BF16)] * 3
        + [_sds((S, LANES), F32), _sds((S, N_HEADS * LANES), F32)],
        scratch_shapes=[pltpu.VMEM((8, LANES), F32)], compiler_params=_params())(x, g, wa, gq, gk, b_pad)


def _key_le_query(offset, keys=TK):
    r = lax.broadcasted_iota(jnp.int32, (keys, TQ), 0)
    c = lax.broadcasted_iota(jnp.int32, (keys, TQ), 1)
    return (r + offset) <= c


def _widen(tile):
    return jnp.concatenate([tile] * (TQ // LANES), axis=1)


def _fox_fwd(qn, kn, vb, proj, crow, cbc, rider=()):
    nq = S // TQ

    def body(q_ref, k_ref, v_ref, g_ref, cq_ref, cbc_ref, o_ref, z_ref, lse_ref, st_s, pt_s):
        i = pl.program_id(1)
        qs = [q_ref[:, HD * hh:HD * (hh + 1)] for hh in range(HPS)]
        cqs = [cq_ref[hh, 0] for hh in range(HPS)]

        def scores(s, hh):
            off = pl.multiple_of(s * KS, KS)
            kj = k_ref[pl.ds(off, KS), HD * hh:HD * (hh + 1)]
            return (_dot_nt(kj, qs[hh]) + cqs[hh]) - _widen(cbc_ref[pl.ds(off, KS), LANES * hh:LANES * (hh + 1)])

        def values(s, hh, pt):
            off = pl.multiple_of(s * KS, KS)
            return _dot_tn(v_ref[pl.ds(off, KS), HD * hh:HD * (hh + 1)], pt)

        def step(s, slot, carries, mask=None, last=False):
            if not last:
                for hh in range(HPS):
                    st_s[1 - slot, hh] = scores(s + 1, hh)
            pvs = [values(jnp.maximum(s - 1, 0), hh, pt_s[1 - slot, hh]) for hh in range(HPS)]
            out = []
            for hh in range(HPS):
                m, l, acc = carries[hh]
                st = st_s[slot, hh]
                if mask is not None:
                    st = jnp.where(mask, st, -jnp.inf)
                m_new = jnp.maximum(m, jnp.max(st, axis=0, keepdims=True))
                pt = jnp.exp(st - m_new)
                alpha = jnp.exp(m - m_new)
                pt_s[slot, hh] = pt.astype(BF16)
                out.append((m_new, alpha * l + jnp.sum(pt, axis=0, keepdims=True), alpha * (acc + pvs[hh])))
            return tuple(out)

        for hh in range(HPS):
            st_s[0, hh] = scores(0, hh)
            pt_s[1, hh] = jnp.zeros((KS, TQ), BF16)
        one = (jnp.full((1, TQ), -jnp.inf, F32), jnp.zeros((1, TQ), F32), jnp.zeros((HD, TQ), F32))
        carries = lax.fori_loop(0, i, lambda t, cr: step(2 * t + 1, 1, step(2 * t, 0, cr)), (one,) * HPS)
        carries = step(2 * i, 0, carries, mask=_key_le_query(0, KS))
        carries = step(2 * i + 1, 1, carries, mask=_key_le_query(KS, KS), last=True)
        accs = []
        for hh in range(HPS):
            m, l, acc = carries[hh]
            acc = acc + values(2 * i + 1, hh, pt_s[1, hh])
            accs.append(acc / l)
            lse_ref[hh, 0] = m + jnp.log(l)
        o = jnp.concatenate(accs, axis=0).T
        o_ref[...] = o
        g = g_ref[...]
        z_ref[...] = (o * (g * _sigmoid(g))).astype(BF16)

    qblk = pl.BlockSpec((TQ, HW), lambda hp, i: (i, hp))
    full = pl.BlockSpec((S, HW), lambda hp, i: (0, hp))
    rows = pl.BlockSpec((HPS, 1, 1, TQ), lambda hp, i: (hp, i, 0, 0))
    return _call(
        body, name="fox_fwd", args=(qn, kn, vb, proj, crow, cbc), grid=(N_HEADS // HPS, nq),
        in_specs=[qblk, full, full,
                  pl.BlockSpec((TQ, HW), lambda hp, i: (i, GOFF // HW + hp)),
                  rows, pl.BlockSpec((S, HPS * LANES), lambda hp, i: (0, hp))],
        out_specs=[qblk, qblk, rows],
        out_shape=[_sds((S, D), F32), _sds((S, D), BF16), _sds((N_HEADS, nq, 1, TQ), F32)],
        scratch_shapes=[pltpu.VMEM((2, HPS, KS, TQ), F32), pltpu.VMEM((2, HPS, KS, TQ), BF16)], rider=rider)


def _fox_bwd_pre(dh, w_out, proj, o):
    nq, rows = S // TQ, 2 * TQ
    per = rows // TQ

    def body(dh_ref, w_ref, g_ref, o_ref, do_ref, dg_ref, delta_ref):
        g = g_ref[...]
        sg = _sigmoid(g)
        dzv = _dot_nt(dh_ref[...].astype(BF16), w_ref[...])
        ov = o_ref[...]
        do = dzv * (g * sg)
        dg_ref[...] = (dzv * ov * (sg * (1.0 + g * (1.0 - sg)))).astype(BF16)
        do_ref[...] = do.astype(BF16)
        prod_t = (do * ov).T
        for h in range(N_HEADS):
            for b in range(per):
                delta_ref[h, b] = jnp.sum(prod_t[HD * h:HD * (h + 1), TQ * b:TQ * (b + 1)], axis=0, keepdims=True)

    row = pl.BlockSpec((rows, D), lambda i: (i, 0))
    return pl.pallas_call(
        body, name="fox_bwd_pre", grid=(S // rows,),
        in_specs=[row, pl.BlockSpec(w_out.shape, lambda i: (0, 0), pipeline_mode=pl.Buffered(1)),
                  pl.BlockSpec((rows, D), lambda i: (i, GOFF // D)), row],
        out_specs=[row, row, pl.BlockSpec((N_HEADS, per, 1, TQ), lambda i: (0, i, 0, 0))],
        out_shape=[_sds((S, D), BF16), _sds((S, D), BF16), _sds((N_HEADS, nq, 1, TQ), F32)],
        compiler_params=_params())(dh, w_out, proj, o)


def _fox_bwd(qn, kn, vb, dob, lse, delta, crow, cbc, rider=()):
    nq, nkb = S // TQ, S // TK

    def body(q_ref, k_ref, v_ref, do_ref, lse_ref, del_ref, cq_ref, cbc_ref,
             dk_ref, dv_ref, dcs_ref, dq_ref, dr_ref, st_s, dp_s, pt_s, ds_s, dq_acc, dr_acc):
        j = pl.program_id(1)

        @pl.when(j == 0)
        def _():
            dq_acc[...] = jnp.zeros_like(dq_acc)
            dr_acc[...] = jnp.zeros_like(dr_acc)

        kjs = [k_ref[:, HD * hh:HD * (hh + 1)] for hh in range(HPS)]
        vjs = [v_ref[:, HD * hh:HD * (hh + 1)] for hh in range(HPS)]

        def rows_of(ref, u, hh):
            off = pl.multiple_of(u * TQ, TQ)
            return ref[pl.ds(off, TQ), HD * hh:HD * (hh + 1)]

        def products(u, hh):
            st = (_dot_nt(kjs[hh], rows_of(q_ref, u, hh)) + cq_ref[hh, u]) - _widen(
                cbc_ref[:, LANES * hh:LANES * (hh + 1)])
            return st, _dot_nt(vjs[hh], rows_of(do_ref, u, hh))

        def step(u, slot, carries, masked=False):
            nxt = jnp.minimum(u + 1, nq - 1)
            for hh in range(HPS):
                st_s[1 - slot, hh], dp_s[1 - slot, hh] = products(nxt, hh)
            prev = jnp.maximum(u - 1, 0)
            dvs = [_dot_nn(pt_s[1 - slot, hh], rows_of(do_ref, prev, hh)) for hh in range(HPS)]
            dks = [_dot_nn(ds_s[1 - slot, hh], rows_of(q_ref, prev, hh)) for hh in range(HPS)]
            for hh in range(HPS):
                dq_acc[hh, prev] += _dot_tn(kjs[hh], ds_s[1 - slot, hh])
            out = []
            for hh in range(HPS):
                dk, dv, dcs = carries[hh]
                st = st_s[slot, hh]
                if masked:
                    st = jnp.where(_key_le_query((j - u) * TQ), st, -jnp.inf)
                pt = jnp.exp(st - lse_ref[hh, u])
                dst = pt * (dp_s[slot, hh] - del_ref[hh, u])
                pt_s[slot, hh] = pt.astype(BF16)
                ds_s[slot, hh] = dst.astype(BF16)
                dr_acc[hh, u] += jnp.sum(dst, axis=0, keepdims=True)
                out.append((dk + dks[hh], dv + dvs[hh], dcs + (dst[:, :LANES] + dst[:, LANES:])))
            return tuple(out)

        t0 = j // 2
        for hh in range(HPS):
            st_s[0, hh], dp_s[0, hh] = products(2 * t0, hh)
            pt_s[1, hh] = jnp.zeros((TK, TQ), BF16)
            ds_s[1, hh] = jnp.zeros((TK, TQ), BF16)
        one = (jnp.zeros((TK, HD), F32), jnp.zeros((TK, HD), F32), jnp.zeros((TK, LANES), F32))
        carries = step(2 * t0 + 1, 1, step(2 * t0, 0, (one,) * HPS, masked=True), masked=True)
        carries = lax.fori_loop(t0 + 1, nq // 2, lambda t, cr: step(2 * t + 1, 1, step(2 * t, 0, cr)), carries)
        dks, dvs = [], []
        lane = _lane_iota((TK, LANES))
        dcs_all = jnp.zeros((TK, LANES), F32)
        for hh in range(HPS):
            dk, dv, dcs = carries[hh]
            dks.append(dk + _dot_nn(ds_s[1, hh], rows_of(q_ref, nq - 1, hh)))
            dvs.append(dv + _dot_nn(pt_s[1, hh], rows_of(do_ref, nq - 1, hh)))
            dq_acc[hh, nq - 1] += _dot_tn(kjs[hh], ds_s[1, hh])
            dcs_all = jnp.where(lane == HPS * pl.program_id(0) + hh, -jnp.sum(dcs, axis=-1, keepdims=True), dcs_all)
        dcs_ref[0] = dcs_all
        dk_ref[...] = jnp.concatenate(dks, axis=-1)
        dv_ref[...] = jnp.concatenate(dvs, axis=-1).astype(BF16)

        @pl.when(j == nkb - 1)
        def _():
            for i in range(nq):
                dq_ref[TQ * i:TQ * (i + 1), :] = jnp.concatenate([dq_acc[hh, i] for hh in range(HPS)], axis=0).T
            dr_ref[...] = dr_acc[...]

    kblk = pl.BlockSpec((TK, HW), lambda hp, j: (j, hp))
    full = pl.BlockSpec((S, HW), lambda hp, j: (0, hp))
    rows = pl.BlockSpec((HPS, nq, 1, TQ), lambda hp, j: (hp, 0, 0, 0))
    cblk = pl.BlockSpec((TK, HPS * LANES), lambda hp, j: (j, hp))
    return _call(
        body, name="fox_bwd", args=(qn, kn, vb, dob, lse, delta, crow, cbc), grid=(N_HEADS // HPS, nkb),
        in_specs=[full, kblk, kblk, full, rows, rows, rows, cblk],
        out_specs=[kblk, kblk, pl.BlockSpec((1, TK, LANES), lambda hp, j: (hp, j, 0)), full, rows],
        out_shape=[_sds((S, D), F32), _sds((S, D), BF16), _sds((N_HEADS // HPS, S, LANES), F32), _sds((S, D), F32),
                   _sds((N_HEADS, nq, 1, TQ), F32)],
        scratch_shapes=[pltpu.VMEM((2, HPS, TK, TQ), F32), pltpu.VMEM((2, HPS, TK, TQ), F32),
                        pltpu.VMEM((2, HPS, TK, TQ), BF16), pltpu.VMEM((2, HPS, TK, TQ), BF16),
                        pltpu.VMEM((HPS, nq, HD, TQ), F32), pltpu.VMEM((HPS, nq, 1, TQ), F32)], rider=rider)


def _prep_a_bwd(dq, dk, dv, dgate, drow, dcs, proj, b_pad, gq, gk):
    nt = S // TM

    def body(dq_ref, dk_ref, dv_ref, dgt_ref, dr_ref, dcs_ref, xq_ref, xk_ref, f_ref, b_ref, gq_ref, gk_ref,
             o_ref, dgq_ref, dgk_ref, db_ref, carry):
        @pl.when(pl.program_id(0) == 0)
        def _():
            carry[...] = jnp.zeros_like(carry)
            dgq_ref[...] = jnp.zeros_like(dgq_ref)
            dgk_ref[...] = jnp.zeros_like(dgk_ref)
            db_ref[...] = jnp.zeros_like(db_ref)

        lane = _lane_iota((TM, LANES))
        lo_half = _half_ones()
        gq2, gk2 = _g2(gq_ref), _g2(gk_ref)
        dgq, dgk = jnp.zeros((1, LANES), F32), jnp.zeros((1, LANES), F32)
        for c in _pairs(D):
            dxq, dg = _head_norm_bwd(dq_ref[:, c] * QSCALE, xq_ref[:, c], gq2, lo_half)
            o_ref[:, c] = dxq.astype(BF16)
            dgq = dgq + dg
            dxk, dg = _head_norm_bwd(dk_ref[:, c], xk_ref[:, c], gk2, lo_half)
            o_ref[:, D + c.start:D + c.stop] = dxk.astype(BF16)
            dgk = dgk + dg
        dgq_ref[...] += dgq
        dgk_ref[...] += dgk
        o_ref[:, 2 * D:3 * D] = dv_ref[...]
        o_ref[:, GOFF:GOFF + D] = dgt_ref[...]

        dc = dr_ref[...]
        for group in range(N_HEADS // HPS):
            dc = dc + dcs_ref[group]
        r = lax.broadcasted_iota(jnp.int32, (TM, TM), 0)
        c = lax.broadcasted_iota(jnp.int32, (TM, TM), 1)
        tri = (c >= r).astype(F32)
        dlogf = jnp.dot(tri, dc, precision=lax.Precision.HIGHEST, preferred_element_type=F32) + carry[0:1, :]
        carry[0:1, :] = dlogf[0:1, :]
        df = dlogf * (1.0 / (1.0 + jnp.exp(f_ref[...] + b_ref[...])))
        db_ref[...] += jnp.sum(df, axis=0, keepdims=True)
        o_ref[:, FOFF:FOFF + LANES] = df.astype(BF16)
        o_ref[:, FOFF + LANES:NA] = jnp.zeros((TM, NA - FOFF - LANES), BF16)

    rev = lambda width, col: pl.BlockSpec((TM, width), lambda i: (nt - 1 - i, col))
    gspec = pl.BlockSpec((1, HD), lambda i: (0, 0))
    acc = pl.BlockSpec((1, LANES), lambda i: (0, 0))
    return pl.pallas_call(
        body, name="prep_a_bwd", grid=(nt,),
        in_specs=[rev(D, 0), rev(D, 0), rev(D, 0), rev(D, 0), rev(LANES, 0),
                  pl.BlockSpec((N_HEADS // HPS, TM, LANES), lambda i: (0, nt - 1 - i, 0)),
                  rev(D, 0), rev(D, 1), rev(LANES, FOFF // LANES), acc, gspec, gspec],
        out_specs=[rev(NA, 0), acc, acc, acc],
        out_shape=[_sds((S, NA), BF16)] + [_sds((1, LANES), F32)] * 3,
        scratch_shapes=[pltpu.VMEM((8, LANES), F32)],
        compiler_params=_params())(dq, dk, dv, dgate, drow, dcs, proj, proj, proj, b_pad, gq, gk)


def _head_b(x, z_a, w_out_a, g_kv, g_b, w_kv, w_in_b, gq, gk, cos2, sin2):
    nkv = w_kv.shape[1] // 2

    def body(x_ref, z_ref, wo_ref, gkv_ref, gb_ref, wkv_ref, wb_ref, gq_ref, gk_ref, c_ref, s_ref,
             h_ref, ukv_ref, ub_ref, kv_ref, pb_ref, qo_ref, ko_ref, vo_ref):
        xv = x_ref[...] + _dot_nn(z_ref[...], wo_ref[...])
        h_ref[...] = xv
        xn = xv * _rms_rinv(xv)
        ukv = (xn * gkv_ref[...]).astype(BF16)
        ub = (xn * gb_ref[...]).astype(BF16)
        ukv_ref[...] = ukv
        ub_ref[...] = ub
        kv_ref[...] = _dot_nn(ukv, wkv_ref[...])
        for lo in range(0, 2 * D, D):
            pb_ref[:, lo:lo + D] = _dot_nn(ub, wb_ref[:, lo:lo + D])
        lane = _lane_iota((RT, LANES))
        lo_half = lane < HD
        cos, sin = c_ref[...], s_ref[...]
        gq2, gk2 = _g2(gq_ref), _g2(gk_ref)
        for c in _pairs(D):
            q = pb_ref[:, c]
            qo_ref[:, c] = (_rope_fwd((q * _head_rinv(q, lo_half)) * gq2, cos, sin, lane) * QSCALE).astype(BF16)
        for c in _pairs(nkv):
            k = kv_ref[:, c]
            ko_ref[:, c] = _rope_fwd((k * _head_rinv(k, lo_half)) * gk2, cos, sin, lane).astype(BF16)
        vo_ref[...] = kv_ref[:, nkv:2 * nkv].astype(BF16)

    row = lambda width: pl.BlockSpec((RT, width), lambda i: (i, 0))
    whole = lambda arr: pl.BlockSpec(arr.shape, lambda i: (0,) * arr.ndim, pipeline_mode=pl.Buffered(1))
    return pl.pallas_call(
        body, name="head_b", grid=(S // RT,),
        in_specs=[row(D), row(D), whole(w_out_a), whole(g_kv), whole(g_b), whole(w_kv), whole(w_in_b), whole(gq),
                  whole(gk), row(LANES), row(LANES)],
        out_specs=[row(D), row(D), row(D), row(2 * nkv), row(2 * D), row(D), row(nkv), row(nkv)],
        out_shape=[_sds((S, D), F32), _sds((S, D), BF16), _sds((S, D), BF16), _sds((S, 2 * nkv), F32),
                   _sds((S, 2 * D), F32), _sds((S, D), BF16), _sds((S, nkv), BF16), _sds((S, nkv), BF16)],
        compiler_params=_params())(x, z_a, w_out_a, g_kv, g_b, w_kv, w_in_b, gq, gk, cos2, sin2)


N_KV, GRP = 4, 4


def _swa_mask(n):
    r = lax.broadcasted_iota(jnp.int32, (2 * WIN, GRP * WIN), 0)
    q = lax.broadcasted_iota(jnp.int32, (2 * WIN, GRP * WIN), 1) & (WIN - 1)
    return (r > q) & (r <= q + WIN) & ((r >= WIN) | (n > 0))


def _stack4(ref_or_val, base):
    return jnp.concatenate([ref_or_val[:, base + HD * g: base + HD * (g + 1)] for g in range(GRP)], axis=0)


def _unstack4(xt):
    return jnp.concatenate([xt[:, WIN * g:WIN * (g + 1)] for g in range(GRP)], axis=0).T


def _band(prev_ref, cur_ref, kh):
    return jnp.concatenate([prev_ref[:, HD * kh:HD * (kh + 1)], cur_ref[:, HD * kh:HD * (kh + 1)]], axis=0)


def _sink_row(s_ref, first):
    lane = _lane_iota((1, GRP * WIN))
    row = jnp.full((1, GRP * WIN), s_ref[first + GRP - 1], F32)
    for g in range(GRP - 2, -1, -1):
        row = jnp.where(lane < WIN * (g + 1), s_ref[first + g], row)
    return row


def _swa_fwd(qb, ksh, vsh, pb, sinks):
    nb = S // WIN

    def body(q_ref, kp_ref, kc_ref, vp_ref, vc_ref, g_ref, s_ref, o_ref, z_ref, lse_ref):
        n = pl.program_id(0)
        valid = _swa_mask(n)
        outs = []
        for kh in range(N_KV):
            kb, vb = _band(kp_ref, kc_ref, kh), _band(vp_ref, vc_ref, kh)
            st = jnp.where(valid, _dot_nt(kb, _stack4(q_ref, GRP * HD * kh)), -jnp.inf)
            sink = _sink_row(s_ref, GRP * kh)
            m = jnp.maximum(jnp.max(st, axis=0, keepdims=True), sink)
            pt = jnp.exp(st - m)
            l = jnp.sum(pt, axis=0, keepdims=True) + jnp.exp(sink - m)
            outs.append(_unstack4(_dot_tn(vb, pt.astype(BF16)) / l))
            lse = m + jnp.log(l)
            for g in range(GRP):
                lse_ref[GRP * kh + g, 0] = lse[:, WIN * g:WIN * (g + 1)]
        o = jnp.concatenate(outs, axis=-1)
        o_ref[...] = o
        g = g_ref[...]
        z_ref[...] = (o * (g * _sigmoid(g))).astype(BF16)

    row = pl.BlockSpec((WIN, D), lambda n: (n, 0))
    prev = pl.BlockSpec((WIN, N_KV * HD), lambda n: (jnp.maximum(n - 1, 0), 0))
    cur = pl.BlockSpec((WIN, N_KV * HD), lambda n: (n, 0))
    return pl.pallas_call(
        body, name="swa_fwd", grid=(nb,),
        in_specs=[row, prev, cur, prev, cur, pl.BlockSpec((WIN, D), lambda n: (n, 1)),
                  pl.BlockSpec(memory_space=pltpu.SMEM)],
        out_specs=[row, row, pl.BlockSpec((N_HEADS, 1, 1, WIN), lambda n: (0, n, 0, 0))],
        out_shape=[_sds((S, D), F32), _sds((S, D), BF16), _sds((N_HEADS, nb, 1, WIN), F32)],
        compiler_params=_params())(qb, ksh, ksh, vsh, vsh, pb, sinks)


def _swa_bwd(qb, ksh, vsh, dz, o, lse, pb, sinks, gq, cos2, sin2):
    nb = S // WIN

    def body(q_ref, kp_ref, kc_ref, vp_ref, vc_ref, dz_ref, o_ref, lse_ref, x_ref, g_ref, s_ref, gq_ref, c_ref, sn_ref,
             dpb_ref, dka_ref, dkb_ref, dva_ref, dvb_ref, dsink_ref, dgq_ref):
        n = pl.program_id(0)

        @pl.when(n == 0)
        def _():
            dsink_ref[...] = jnp.zeros_like(dsink_ref)
            dgq_ref[...] = jnp.zeros_like(dgq_ref)

        valid = _swa_mask(n)
        g = g_ref[...]
        sg = _sigmoid(g)
        dzv = dz_ref[...]
        ov = o_ref[...]
        do = dzv * (g * sg)
        dpb_ref[:, D:2 * D] = (dzv * ov * (sg * (1.0 + g * (1.0 - sg)))).astype(BF16)
        prod_t = (do * ov).T
        lane1 = _lane_iota((1, LANES))
        dqs, dkas, dkbs, dvas, dvbs = [], [], [], [], []
        dsink = jnp.zeros((1, LANES), F32)
        for kh in range(N_KV):
            kb, vb = _band(kp_ref, kc_ref, kh), _band(vp_ref, vc_ref, kh)
            base = GRP * HD * kh
            qs = _stack4(q_ref, base)
            dos = _stack4(do, base).astype(BF16)
            delta = jnp.concatenate(
                [jnp.sum(prod_t[base + HD * gg:base + HD * (gg + 1), :], axis=0, keepdims=True)
                 for gg in range(GRP)], axis=1)
            lse = jnp.concatenate([lse_ref[GRP * kh + gg, 0] for gg in range(GRP)], axis=1)
            st = jnp.where(valid, _dot_nt(kb, qs), -jnp.inf)
            pt = jnp.exp(st - lse)
            dst = pt * (_dot_nt(vb, dos) - delta)
            dsb = dst.astype(BF16)
            dqs.append(_unstack4(_dot_tn(kb, dsb)))
            dkband = _dot_nn(dsb, qs)
            dvband = _dot_nn(pt.astype(BF16), dos)
            dkbs.append(dkband[0:WIN, :])
            dkas.append(dkband[WIN:2 * WIN, :])
            dvbs.append(dvband[0:WIN, :])
            dvas.append(dvband[WIN:2 * WIN, :])
            ps_delta = jnp.exp(_sink_row(s_ref, GRP * kh) - lse) * delta
            for gg in range(GRP):
                val = jnp.sum(ps_delta[:, WIN * gg:WIN * (gg + 1)], axis=1, keepdims=True)
                dsink = dsink - jnp.where(lane1 == GRP * kh + gg, val, 0.0)
        dka_ref[...] = jnp.concatenate(dkas, axis=-1)
        dkb_ref[...] = jnp.concatenate(dkbs, axis=-1)
        dva_ref[...] = jnp.concatenate(dvas, axis=-1)
        dvb_ref[...] = jnp.concatenate(dvbs, axis=-1)
        dsink_ref[...] += dsink

        lane = _lane_iota((WIN, LANES))
        g2, cos, sin = _g2(gq_ref), c_ref[...], sn_ref[...]
        lo_half = _half_ones()
        dg_tot = jnp.zeros((1, LANES), F32)
        for kh in range(N_KV):
            for c in _pairs(GRP * HD):
                cols = slice(GRP * HD * kh + c.start, GRP * HD * kh + c.stop)
                dn = _rope_bwd(dqs[kh][:, c] * QSCALE, cos, sin, lane)
                dx, dg = _head_norm_bwd(dn, x_ref[:, cols], g2, lo_half)
                dpb_ref[:, cols] = dx.astype(BF16)
                dg_tot = dg_tot + dg
        dgq_ref[...] += dg_tot

    row = pl.BlockSpec((WIN, D), lambda n: (n, 0))
    prev = pl.BlockSpec((WIN, N_KV * HD), lambda n: (jnp.maximum(n - 1, 0), 0))
    cur = pl.BlockSpec((WIN, N_KV * HD), lambda n: (n, 0))
    acc = pl.BlockSpec((1, LANES), lambda n: (0, 0))
    tab = pl.BlockSpec((WIN, LANES), lambda n: (n, 0))
    return pl.pallas_call(
        body, name="swa_bwd", grid=(nb,),
        in_specs=[row, prev, cur, prev, cur, row, row, pl.BlockSpec((N_HEADS, 1, 1, WIN), lambda n: (0, n, 0, 0)),
                  row, pl.BlockSpec((WIN, D), lambda n: (n, 1)), pl.BlockSpec(memory_space=pltpu.SMEM),
                  pl.BlockSpec((1, HD), lambda n: (0, 0)), tab, tab],
        out_specs=[pl.BlockSpec((WIN, 2 * D), lambda n: (n, 0)), cur, cur, cur, cur, acc, acc],
        out_shape=[_sds((S, 2 * D), BF16)] + [_sds((S, 256), F32)] * 4 + [_sds((1, LANES), F32)] * 2,
        compiler_params=_params())(qb, ksh, ksh, vsh, vsh, dz, o, lse, pb, pb, sinks, gq, cos2, sin2)


def _prep_kv_bwd(dka, dkb, dva, dvb, kv, gk, cos2, sin2):
    nt = S // RB
    per = RB // WIN

    def shifted(cur_ref, nxt_ref, has_next):
        return jnp.concatenate([cur_ref[WIN:RB, :], jnp.where(has_next, nxt_ref[...], 0.0)], axis=0)

    def body(dka_ref, dkb_ref, dkn_ref, dva_ref, dvb_ref, dvn_ref, x_ref, g_ref, c_ref, s_ref, o_ref, dgk_ref):
        i, j = pl.program_id(0), pl.program_id(1)

        @pl.when((i == 0) & (j == 0))
        def _():
            dgk_ref[...] = jnp.zeros_like(dgk_ref)

        has_next = i < nt - 1

        @pl.when(j == 0)
        def _():
            lane = _lane_iota((RB, LANES))
            g2, cos, sin = _g2(g_ref), c_ref[...], s_ref[...]
            dy_all = dka_ref[...] + shifted(dkb_ref, dkn_ref, has_next)
            dg_tot = jnp.zeros((1, LANES), F32)
            lo_half = _half_ones()
            for c in _pairs(CB):
                dn = _rope_bwd(dy_all[:, c], cos, sin, lane)
                dx, dg = _head_norm_bwd(dn, x_ref[:, c], g2, lo_half)
                o_ref[:, c] = dx.astype(BF16)
                dg_tot = dg_tot + dg
            dgk_ref[...] += dg_tot

        @pl.when(j == 1)
        def _():
            o_ref[...] = (dva_ref[...] + shifted(dvb_ref, dvn_ref, has_next)).astype(BF16)

    cur = pl.BlockSpec((RB, CB), lambda i, j: (i, 0))
    nxt = pl.BlockSpec((WIN, CB), lambda i, j: (jnp.minimum(per * (i + 1), S // WIN - 1), 0))
    tab = pl.BlockSpec((RB, LANES), lambda i, j: (i, 0))
    return pl.pallas_call(
        body, name="prep_kv_bwd", grid=(nt, 2),
        in_specs=[cur, cur, nxt, cur, cur, nxt, cur, pl.BlockSpec((1, HD), lambda i, j: (0, 0)), tab, tab],
        out_specs=[pl.BlockSpec((RB, CB), lambda i, j: (i, j)), pl.BlockSpec((1, LANES), lambda i, j: (0, 0))],
        out_shape=[_sds((S, 2 * CB), BF16), _sds((1, LANES), F32)],
        compiler_params=_params())(dka, dkb, dkb, dva, dvb, dvb, kv, gk, cos2, sin2)


def _out_b_loss(z, w_out, h1, tgt):
    tm = 2 * RT

    def body(z_ref, w_ref, h_ref, t_ref, dy_ref, l_ref):
        @pl.when(pl.program_id(0) == 0)
        def _():
            l_ref[...] = jnp.zeros_like(l_ref)

        e = (h_ref[...] + _dot_nn(z_ref[...], w_ref[...])) - t_ref[...]
        dy_ref[...] = e * (1.0 / D)
        l_ref[...] += jnp.sum(jnp.sum(e * e, axis=-1, keepdims=True), axis=0, keepdims=True)

    row = pl.BlockSpec((tm, D), lambda i: (i, 0))
    return pl.pallas_call(
        body, name="out_b_loss", grid=(S // tm,),
        in_specs=[row, pl.BlockSpec(w_out.shape, lambda i: (0, 0), pipeline_mode=pl.Buffered(1)), row, row],
        out_specs=[row, pl.BlockSpec((1, LANES), lambda i: (0, 0))],
        out_shape=[_sds((S, D), F32), _sds((1, LANES), F32)], compiler_params=_params())(z, w_out, h1, tgt)


def _du_a_rms_bwd(dproj, wa, x, g, dres, after):
    tm, tk = 1024, NA // 2
    nk = NA // tk

    def body(a_ref, b_ref, x_ref, g_ref, dr_ref, after_ref, dx_ref, dg_ref, acc):
        i, kk = pl.program_id(0), pl.program_id(1)

        @pl.when((i == 0) & (kk == 0))
        def _():
            dg_ref[...] = jnp.zeros_like(dg_ref)

        p = _dot_nt(a_ref[...], b_ref[...])

        @pl.when(kk == 0)
        def _():
            acc[...] = p

        @pl.when(kk == nk - 1)
        def _():
            dx, dg = _rms_bwd_core(acc[...] + p, x_ref[...], g_ref[...])
            dx_ref[...] = dr_ref[...] + dx
            dg_ref[...] += dg

    assert nk == 2
    row = pl.BlockSpec((tm, D), lambda i, kk: (i, 0))
    vec = pl.BlockSpec((1, D), lambda i, kk: (0, 0))
    return pl.pallas_call(
        body, name="du_a_rms_bwd", grid=(S // tm, nk),
        in_specs=[pl.BlockSpec((tm, tk), lambda i, kk: (i, kk)), pl.BlockSpec((D, tk), lambda i, kk: (0, kk)),
                  row, vec, row, pl.BlockSpec(after.shape, lambda i, kk: (0, 0))],
        out_specs=[row, vec], out_shape=[_sds((S, D), F32), _sds((1, D), F32)],
        scratch_shapes=[pltpu.VMEM((tm, D), F32)], compiler_params=_params())(dproj, wa, x, g, dres, after)


def _du_b_rms_bwd(dpb, w_in_b, dkv, w_kv, h1, g_b, g_kv, dy):
    tm = 2 * RT

    def body(ab_ref, wb_ref, akv_ref, wkv_ref, x_ref, gb_ref, gkv_ref, dy_ref, dh_ref, dgb_ref, dgkv_ref):
        @pl.when(pl.program_id(0) == 0)
        def _():
            dgb_ref[...] = jnp.zeros_like(dgb_ref)
            dgkv_ref[...] = jnp.zeros_like(dgkv_ref)

        x = x_ref[...]
        dx1, dg1 = _rms_bwd_core(_dot_nt(ab_ref[...], wb_ref[...]), x, gb_ref[...])
        dx2, dg2 = _rms_bwd_core(_dot_nt(akv_ref[...], wkv_ref[...]), x, gkv_ref[...])
        dh_ref[...] = dy_ref[...] + dx1 + dx2
        dgb_ref[...] += dg1
        dgkv_ref[...] += dg2

    row = lambda width: pl.BlockSpec((tm, width), lambda i: (i, 0))
    whole = lambda arr: pl.BlockSpec(arr.shape, lambda i: (0, 0), pipeline_mode=pl.Buffered(1))
    vec = pl.BlockSpec((1, D), lambda i: (0, 0))
    return pl.pallas_call(
        body, name="du_b_rms_bwd", grid=(S // tm,),
        in_specs=[row(dpb.shape[1]), whole(w_in_b), row(dkv.shape[1]), whole(w_kv), row(D), vec, vec, row(D)],
        out_specs=[row(D), vec, vec], out_shape=[_sds((S, D), F32), _sds((1, D), F32), _sds((1, D), F32)],
        compiler_params=_params())(dpb, w_in_b, dkv, w_kv, h1, g_b, g_kv, dy)


def _gather_first(w_in_a, w_out_a, w_kv, w_in_b, w_out_b, norm_a_g):
    def body(wia_ref, woa_ref, wkv_ref, wib_ref, wob_ref, ga_ref,
             wa_g, ga_g, woa_s, wkv_s, wib_s, wob_s, wa_s, st_a, st_oa, st_kv, st_ib, st_ob, load_sems, *sems):
        sources = [wia_ref.at[0], woa_ref.at[0], wkv_ref, wib_ref.at[0], wob_ref.at[0]]
        stages = [st_a, st_oa, st_kv, st_ib, st_ob]
        loads = [pltpu.make_async_copy(src, dst, load_sems.at[i]) for i, (src, dst) in enumerate(zip(sources, stages))]
        for cp in loads:
            cp.start()
        loads[0].wait()
        wa_s[...] = st_a[...].astype(BF16)

        def cast_the_rest():
            for cp, stage, out in zip(loads[1:], stages[1:], [woa_s, wkv_s, wib_s, wob_s]):
                cp.wait()
                out[...] = stage[...].astype(BF16)

        _gather_two_level([wa_s, ga_ref], [wa_g, ga_g], sems, meanwhile=cast_the_rest)

    vmem = pl.BlockSpec(memory_space=pltpu.VMEM)
    anyspec = pl.BlockSpec(memory_space=pl.ANY)
    shard = lambda w: _sds(w.shape[-2:], BF16)
    stage = lambda w: pltpu.VMEM(w.shape[-2:], F32)
    return pl.pallas_call(
        body, name="gather_first", in_specs=[anyspec] * 5 + [vmem],
        out_specs=[anyspec, anyspec, vmem, vmem, vmem, vmem],
        out_shape=[_sds((N_DEV,) + w_in_a.shape[-2:], BF16), _sds((N_DEV,) + norm_a_g.shape, F32),
                   shard(w_out_a), shard(w_kv), shard(w_in_b), shard(w_out_b)],
        scratch_shapes=[pltpu.VMEM(w_in_a.shape[-2:], BF16), stage(w_in_a), stage(w_out_a), stage(w_kv),
                        stage(w_in_b), stage(w_out_b), pltpu.SemaphoreType.DMA((5,))] + _exchange_sems(2),
        compiler_params=pltpu.CompilerParams(vmem_limit_bytes=VMEM_LIMIT, has_side_effects=True))(
            w_in_a, w_out_a, w_kv, w_in_b, w_out_b, norm_a_g)


def _pair_reduce(slots):
    n_chip = N_DEV // 2
    _, rows, cols = slots.shape

    def body(s_ref, o_ref, own_v, sib_v, send_sems, recv_sems, local_sems):
        x, y, c = lax.axis_index("x"), lax.axis_index("y"), lax.axis_index("c")
        copies = []
        for j in range(n_chip):
            own = pltpu.make_async_copy(s_ref.at[2 * j + c], own_v.at[j], local_sems.at[j])
            give = pltpu.make_async_remote_copy(
                src_ref=s_ref.at[2 * j + 1 - c], dst_ref=sib_v.at[j], send_sem=send_sems.at[j],
                recv_sem=recv_sems.at[j], device_id=(x, y, 1 - c), device_id_type=pl.DeviceIdType.MESH)
            own.start()
            give.start()
            copies.append((own, give))
        for j, (own, give) in enumerate(copies):
            own.wait()
            give.wait()
            o_ref[j] = (own_v[j].astype(F32) + sib_v[j].astype(F32)).astype(BF16)

    half = _sds((n_chip, rows, cols), slots.dtype)
    return pl.pallas_call(
        body, name="pair_reduce", in_specs=[pl.BlockSpec(memory_space=pl.ANY)],
        out_specs=pl.BlockSpec(memory_space=pltpu.VMEM), out_shape=half,
        scratch_shapes=[pltpu.VMEM(half.shape, half.dtype), pltpu.VMEM(half.shape, half.dtype),
                        pltpu.SemaphoreType.DMA((n_chip,)), pltpu.SemaphoreType.DMA((n_chip,)),
                        pltpu.SemaphoreType.DMA((n_chip,))],
        compiler_params=pltpu.CompilerParams(vmem_limit_bytes=VMEM_LIMIT, has_side_effects=True))(slots)


def _padded_col(c):
    if c < RAW_F:
        return c
    return FOFF + (c - RAW_F) if c < RAW_G else GOFF + (c - RAW_G)


def _shard_pieces():
    width = NA_RAW // N_DEV
    pieces = []
    for d in range(N_DEV):
        cuts = [width * d] + [c for c in (RAW_F, RAW_G) if width * d < c < width * (d + 1)] + [width * (d + 1)]
        for lo, hi in zip(cuts[:-1], cuts[1:]):
            pieces.append((d, lo - width * d, _padded_col(lo), hi - lo))
    return pieces


def _unshard_wa(wa_g):
    def body(w_ref, o_ref):
        o_ref[:, FOFF + N_HEADS:NA] = jnp.zeros((TM, NA - FOFF - N_HEADS), BF16)
        for d, src, dst, width in _shard_pieces():
            o_ref[:, dst:dst + width] = w_ref[d, :, src:src + width]

    return pl.pallas_call(
        body, name="unshard_wa", grid=(D // TM,),
        in_specs=[pl.BlockSpec((N_DEV, TM, NA_RAW // N_DEV), lambda i: (0, i, 0))],
        out_specs=pl.BlockSpec((TM, NA), lambda i: (i, 0)), out_shape=_sds((D, NA), BF16),
        compiler_params=_params())(wa_g)


def _reshard_dwa(dwa):
    def body(g_ref, o_ref):
        for d, src, dst, width in _shard_pieces():
            o_ref[d, :, src:src + width] = g_ref[:, dst:dst + width]

    return pl.pallas_call(
        body, name="reshard_dwa", grid=(D // TM,), in_specs=[pl.BlockSpec((TM, NA), lambda i: (i, 0))],
        out_specs=pl.BlockSpec((N_DEV, TM, NA_RAW // N_DEV), lambda i: (0, i, 0)),
        out_shape=_sds((N_DEV, D, NA_RAW // N_DEV), dwa.dtype), compiler_params=_params())(dwa)


CHIP_FLIPS = (2, 4, 6)


def _chip_exchange_start(partial):
    n = len(CHIP_FLIPS)

    def body(p_ref, land_ref, *rest):
        sends, recvs, token = rest[:n], rest[n:2 * n], rest[2 * n + 2]
        x, y, c = lax.axis_index("x"), lax.axis_index("y"), lax.axis_index("c")
        me = 4 * x + 2 * y + c
        for idx, k in enumerate(CHIP_FLIPS):
            pltpu.make_async_remote_copy(
                src_ref=p_ref.at[(me ^ k) >> 1], dst_ref=land_ref.at[me >> 1], send_sem=sends[idx],
                recv_sem=recvs[idx], device_id=(x ^ ((k >> 2) & 1), y ^ ((k >> 1) & 1), c),
                device_id_type=pl.DeviceIdType.MESH).start()
        token[...] = jnp.zeros_like(token)

    hbm = pl.BlockSpec(memory_space=pltpu.HBM)
    sem = pl.BlockSpec(memory_space=pltpu.SEMAPHORE)
    buf = pltpu.HBM(partial.shape, partial.dtype)
    return pl.pallas_call(
        body, name="chip_exchange_start",
        out_shape=(pltpu.SemaphoreType.DMA(()),) * (2 * n) + (buf, buf, _sds((8, LANES), F32)),
        in_specs=(hbm, hbm), out_specs=(sem,) * (2 * n) + (hbm, hbm, pl.BlockSpec(memory_space=pltpu.VMEM)),
        input_output_aliases={0: 2 * n, 1: 2 * n + 1},
        compiler_params=pltpu.CompilerParams(has_side_effects=pltpu.SideEffectType.DATAFLOW_SIDE_EFFECTING))(
            pltpu.with_memory_space_constraint(partial, pltpu.HBM),
            pltpu.with_memory_space_constraint(lax.empty(partial.shape, partial.dtype), pltpu.HBM))


def _chip_exchange_wait(started, after):
    n = len(CHIP_FLIPS)
    sems, (p_thru, land_thru) = started[:2 * n], started[2 * n:2 * n + 2]

    def body(p_ref, land_ref, *rest):
        sends, recvs = rest[:n], rest[n:2 * n]
        x, y, c = lax.axis_index("x"), lax.axis_index("y"), lax.axis_index("c")
        me = 4 * x + 2 * y + c
        for idx, k in enumerate(CHIP_FLIPS):
            copy = pltpu.make_async_remote_copy(
                src_ref=p_ref.at[(me ^ k) >> 1], dst_ref=land_ref.at[(me ^ k) >> 1], send_sem=sends[idx],
                recv_sem=recvs[idx], device_id=(x ^ ((k >> 2) & 1), y ^ ((k >> 1) & 1), c),
                device_id_type=pl.DeviceIdType.MESH)
            copy.wait_send()
            copy.wait_recv()

    hbm = pl.BlockSpec(memory_space=pltpu.HBM)
    sem = pl.BlockSpec(memory_space=pltpu.SEMAPHORE)
    buf = pltpu.HBM(p_thru.shape, p_thru.dtype)
    return pl.pallas_call(
        body, name="chip_exchange_wait", out_shape=(buf, buf),
        in_specs=(hbm, hbm) + (sem,) * (2 * n) + (pl.BlockSpec(memory_space=pl.ANY),), out_specs=(hbm, hbm),
        input_output_aliases={0: 0, 1: 1},
        compiler_params=pltpu.CompilerParams(has_side_effects=pltpu.SideEffectType.DATAFLOW_SIDE_EFFECTING))(
            p_thru, land_thru, *sems, after)


def _gather_slab(slab, after):
    def body(s_ref, after_ref, o_ref, *sems):
        _exchange_ops(["gather_rows"], [s_ref], [o_ref], sems, True, True)

    anyspec = pl.BlockSpec(memory_space=pl.ANY)
    return pl.pallas_call(
        body, name="gather_slab", in_specs=[anyspec, anyspec], out_specs=anyspec,
        out_shape=_sds((N_DEV,) + slab.shape, slab.dtype), scratch_shapes=_exchange_sems(1),
        compiler_params=pltpu.CompilerParams(has_side_effects=True))(slab, after)


def _adamw(w, g, m, v):
    m = ADAM_B1 * m + (1.0 - ADAM_B1) * g
    v = ADAM_B2 * v + (1.0 - ADAM_B2) * (g * g)
    m_hat = m / (1.0 - ADAM_B1 ** ADAM_STEP)
    v_hat = v / (1.0 - ADAM_B2 ** ADAM_STEP)
    delta = -ADAM_LR * (m_hat / (jnp.sqrt(v_hat) + ADAM_EPS) + ADAM_WD * w)
    return delta, m, v


def _sum_adamw(recv, w, m, v, name, after=None):
    lead = w.ndim - 2
    rows, cols = w.shape[-2:]
    tr = 256 if rows % 256 == 0 else 128
    slabs = list(recv) if isinstance(recv, tuple) else [recv]
    n_slots = slabs[0].shape[0]
    extra = [] if after is None else [after]

    def body(*refs):
        r_ref, own_ref = refs[0], refs[len(slabs) - 1]
        w_ref, m_ref, v_ref, g_ref, d_ref, nm_ref, nv_ref = refs[len(slabs) + len(extra):]
        chip = (4 * lax.axis_index("x") + 2 * lax.axis_index("y") + lax.axis_index("c")) >> 1
        g = None
        for slot in range(n_slots):
            part = r_ref[slot]
            if len(slabs) == 2:
                part = jnp.where(chip == slot, own_ref[slot], part)
            g = part.astype(F32) if g is None else g + part.astype(F32)
        g_ref[...] = g
        d_ref[...], nm_ref[...], nv_ref[...] = _adamw(w_ref[...], g, m_ref[...], v_ref[...])

    blk = pl.BlockSpec((None,) * lead + (tr, cols), lambda i: (0,) * lead + (i, 0))
    slots = pl.BlockSpec((n_slots, tr, cols), lambda i: (0, i, 0))
    return pl.pallas_call(
        body, name=name, grid=(rows // tr,),
        in_specs=[slots] * len(slabs) + [pl.BlockSpec(a.shape, lambda i: (0, 0)) for a in extra] + [blk, blk, blk],
        out_specs=[blk] * 4, out_shape=[_sds(w.shape, F32)] * 4, compiler_params=_params())(*slabs, *extra, w, m, v)


def _entry_view(a):
    _, rows, cols = a.shape
    return jnp.transpose(a, (2, 0, 1)).reshape(cols, rows // LANES, LANES)


def _from_entry_view(a):
    cols, groups, lanes = a.shape
    return jnp.transpose(a, (1, 2, 0)).reshape(1, groups * lanes, cols)


def _sum_adamw_entry_view(landing, own, w, m, v, name):
    n_slots, rows, cols = landing.shape
    groups = rows // LANES
    cols_pad = -(-cols // LANES) * LANES

    def body(r_ref, own_ref, w_ref, m_ref, v_ref, g_ref, d_ref, nm_ref, nv_ref, pad_ref):
        j = pl.program_id(0)
        chip = (4 * lax.axis_index("x") + 2 * lax.axis_index("y") + lax.axis_index("c")) >> 1
        g = None
        for slot in range(n_slots):
            part = jnp.where(chip == slot, own_ref[slot], r_ref[slot])
            g = part.astype(F32) if g is None else g + part.astype(F32)
        pad_ref[:, cols_pad - LANES:] = jnp.zeros((LANES, LANES), F32)
        pad_ref[:, :cols] = g
        gt = pad_ref[...].T[:cols]
        g_ref[:, j, :] = gt
        d_ref[:, j, :], nm_ref[:, j, :], nv_ref[:, j, :] = _adamw(w_ref[:, j, :], gt, m_ref[:, j, :], v_ref[:, j, :])

    whole = pl.BlockSpec((cols, groups, LANES), lambda j: (0, 0, 0))
    slots = pl.BlockSpec((n_slots, LANES, cols), lambda j: (0, j, 0))
    outs = pl.pallas_call(
        body, name=name, grid=(groups,), in_specs=[slots, slots, whole, whole, whole], out_specs=[whole] * 4,
        out_shape=[_sds((cols, groups, LANES), F32)] * 4, scratch_shapes=[pltpu.VMEM((LANES, cols_pad), F32)],
        compiler_params=_params())(landing, own, _entry_view(w), _entry_view(m), _entry_view(v))
    return [_from_entry_view(o) for o in outs]


SLAB_ROWS = 16
SLOT = {"kv_norm_g": (8, 0, D), "norm_b_g": (9, 0, D), "b_forget": (10, 0, 16), "qnorm_a_g": (10, 128, HD),
        "knorm_a_g": (10, 256, HD), "knorm_b_g": (10, 384, HD), "qnorm_b_g": (10, 512, HD), "sinks": (10, 640, 16)}
SMALL = ["norm_a_g", "b_forget", "qnorm_a_g", "knorm_a_g", "kv_norm_g", "knorm_b_g", "norm_b_g", "qnorm_b_g", "sinks"]


LOSS_ROW = 11


def _pack_small(dg_a, dg_kv, dg_b, db_f, dgq_a, dgk_a, dgk_b, dgq_b, dsinks, lsum):
    def fold(ref):
        return ref[:, 0:HD] + ref[:, HD:2 * HD]

    def body(dga_ref, dgkv_ref, dgb_ref, dbf_ref, dgqa_ref, dgka_ref, dgkb_ref, dgqb_ref, dsk_ref, ls_ref, slab_ref):
        slab_ref[...] = jnp.zeros_like(slab_ref)
        for r in range(N_DEV):
            slab_ref[r:r + 1, 0:LANES] = dga_ref[:, LANES * r:LANES * (r + 1)]
        slab_ref[8:9, :] = dgkv_ref[...]
        slab_ref[9:10, :] = dgb_ref[...]
        slab_ref[10:11, 0:LANES] = dbf_ref[...]
        slab_ref[10:11, 128:128 + HD] = fold(dgqa_ref)
        slab_ref[10:11, 256:256 + HD] = fold(dgka_ref)
        slab_ref[10:11, 384:384 + HD] = fold(dgkb_ref)
        slab_ref[10:11, 512:512 + HD] = fold(dgqb_ref)
        slab_ref[10:11, 640:640 + LANES] = dsk_ref[...]
        slab_ref[LOSS_ROW:LOSS_ROW + 1, 0:LANES] = ls_ref[...]

    return pl.pallas_call(body, name="pack_small", out_shape=_sds((SLAB_ROWS, D), F32), compiler_params=_params())(
        dg_a, dg_kv, dg_b, db_f, dgq_a, dgk_a, dgk_b, dgq_b, dsinks, lsum)


def _small_adamw(recv, ws, ms, vs):
    k = len(SMALL)

    def body(*refs):
        r_ref = refs[0]
        w_refs, m_refs, v_refs = refs[1:1 + k], refs[1 + k:1 + 2 * k], refs[1 + 2 * k:1 + 3 * k]
        outs = refs[1 + 3 * k:1 + 7 * k]
        loss_ref, tot = refs[1 + 7 * k], refs[2 + 7 * k]
        g = r_ref[0]
        for dev in range(1, N_DEV):
            g = g + r_ref[dev]
        tot[...] = g
        loss_ref[...] = tot[LOSS_ROW:LOSS_ROW + 1, 0:LANES] * (0.5 / D)
        me = 4 * lax.axis_index("x") + 2 * lax.axis_index("y") + lax.axis_index("c")
        for p, name in enumerate(SMALL):
            if name == "norm_a_g":
                mine = lax.broadcasted_iota(jnp.int32, (N_DEV, LANES), 0) == me
                gp = jnp.sum(jnp.where(mine, tot[0:N_DEV, 0:LANES], 0.0), axis=0, keepdims=True)
            else:
                row, lo, width = SLOT[name]
                gp = tot[row:row + 1, lo:lo + width]
            d, nm, nv = _adamw(w_refs[p][...], gp, m_refs[p][...], v_refs[p][...])
            outs[p][...] = gp
            outs[k + p][...] = d
            outs[2 * k + p][...] = nm
            outs[3 * k + p][...] = nv

    shapes = [_sds(w.shape, F32) for w in ws]
    return pl.pallas_call(body, name="small_adamw", out_shape=shapes * 4 + [_sds((1, LANES), F32)],
                          scratch_shapes=[pltpu.VMEM((SLAB_ROWS, D), F32)],
                          compiler_params=_params())(recv, *ws, *ms, *vs)


def _rope_tables(positions):
    inv_freq = jnp.power(jnp.float32(ROPE_THETA), -jnp.arange(0, ROT, 2, dtype=F32) / ROT)
    ang = positions.astype(F32)[:, None] * inv_freq[None, :]
    cos, sin = jnp.cos(ang), jnp.sin(ang)
    c64 = jnp.concatenate([cos, cos, jnp.ones((S, HD - ROT), F32)], axis=-1)
    s64 = jnp.concatenate([-sin, sin, jnp.zeros((S, HD - ROT), F32)], axis=-1)
    return jnp.tile(c64, (1, 2)), jnp.tile(s64, (1, 2))


def _local_step(x, tgt, positions, g_a, wa, b_forget, gq_a, gk_a, g_kv, gk_b, g_b, gq_b, sinks,
                woa_s, wkv_s, wib_s, wob_s, adamw_others):
    nq = S // TQ
    cos2, sin2 = _rope_tables(positions)
    b_pad = jnp.pad(b_forget, ((0, 0), (0, LANES - N_HEADS)))

    u_a, proj, qn, kn, vb, ccol, cbc = _head_a(x, g_a, wa, gq_a, gk_a, b_pad)
    crow = ccol[:, :N_HEADS].T.reshape(N_HEADS, nq, 1, TQ)
    o_a, z_a, lse_a, woa_g, wkv_g, w_in_b, wob_g = _fox_fwd(
        qn, kn, vb, proj, crow, cbc,
        rider=[("gather_rows", woa_s), ("gather_rows", wkv_s), ("gather_cols", wib_s), ("gather_rows", wob_s)])
    w_out_a, w_kv, w_out_b = woa_g.reshape(D, D), wkv_g.reshape(D, 512), wob_g.reshape(D, D)
    h1, u_kv, u_b, kv, pb, qb, ksh, vsh = _head_b(x, z_a, w_out_a, g_kv, g_b, w_kv, w_in_b, gq_b, gk_b, cos2, sin2)
    sinks1 = sinks.reshape(N_HEADS)
    o_b, z_b, lse_b = _swa_fwd(qb, ksh, vsh, pb, sinks1)
    dy, lsum = _out_b_loss(z_b, w_out_b, h1, tgt)
    dw_out_b = _mm(z_b, dy, "tn", 512, 512, S, out_dtype=BF16, name="mm_dw_out_b")
    dz_b = _mm(dy, w_out_b, "nt", 1024, 512, D, name="mm_dz_b")
    dpb, dka, dkb, dva, dvb, dsinks, dgq_b = _swa_bwd(qb, ksh, vsh, dz_b, o_b, lse_b, pb, sinks1, gq_b, cos2, sin2)
    dkv, dgk_b = _prep_kv_bwd(dka, dkb, dva, dvb, kv, gk_b, cos2, sin2)
    dw_in_b = _mm(u_b, dpb, "tn", 512, 512, S, out_dtype=BF16, name="mm_dw_in_b")
    dw_kv = _mm(u_kv, dkv, "tn", 512, 512, S, out_dtype=BF16, name="mm_dw_kv")
    dh1, dg_b, dg_kv = _du_b_rms_bwd(dpb, w_in_b, dkv, w_kv, h1, g_b, g_kv, dy)
    dw_out_a = _mm(z_a, dh1, "tn", 512, 512, S, out_dtype=BF16, name="mm_dw_out_a")
    do_a, dgate_a, delta_a = _fox_bwd_pre(dh1, w_out_a, proj, o_a)
    dk_a, dv_a, dcs, dq_a, drow, r_wob, r_wib, r_wkv, r_woa = _fox_bwd(
        qn, kn, vb, do_a, lse_a, delta_a, crow, cbc,
        rider=[("a2a_rows", dw_out_b), ("a2a_cols", dw_in_b), ("a2a_rows", dw_kv), ("a2a_rows", dw_out_a)])
    drow_col = jnp.pad(drow.reshape(N_HEADS, S).T, ((0, 0), (0, LANES - N_HEADS)))
    dproj, dgq_a, dgk_a, db_f = _prep_a_bwd(dq_a, dk_a, dv_a, dgate_a, drow_col, dcs, proj, b_pad, gq_a, gk_a)
    dwa = _mm(u_a, dproj, "tn", 1024, 256, S, out_dtype=BF16, name="mm_dw_in_a")
    partial = _pair_reduce(_reshard_dwa(dwa))
    started = _chip_exchange_start(partial)
    dx, dg_a = _du_a_rms_bwd(dproj, wa, x, g_a, dh1, after=started[-1])
    others = adamw_others(dict(w_out_a=r_woa, w_kv=r_wkv, w_in_b=r_wib, w_out_b=r_wob), dg_a)
    partial, landed = _chip_exchange_wait(started, others["w_out_b"][0])
    slab = _pack_small(dg_a, dg_kv, dg_b, db_f, dgq_a, dgk_a, dgk_b, dgq_b, dsinks, lsum)
    return dx, (landed, partial), others, _gather_slab(slab, landed)


def kernel(x, positions, norm_a_g, w_in_a, b_forget, qnorm_a_g, knorm_a_g, w_out_a, kv_norm_g, w_kv, knorm_b_g, norm_b_g, w_in_b, qnorm_b_g, sinks, w_out_b, loss_target, m_norm_a_g, m_w_in_a, m_b_forget, m_qnorm_a_g, m_knorm_a_g, m_w_out_a, m_kv_norm_g, m_w_kv, m_knorm_b_g, m_norm_b_g, m_w_in_b, m_qnorm_b_g, m_sinks, m_w_out_b, v_norm_a_g, v_w_in_a, v_b_forget, v_qnorm_a_g, v_knorm_a_g, v_w_out_a, v_kv_norm_g, v_w_kv, v_knorm_b_g, v_norm_b_g, v_w_in_b, v_qnorm_b_g, v_sinks, v_w_out_b):
    wa_g, ga_g, woa_s, wkv_s, wib_s, wob_s = _gather_first(w_in_a, w_out_a, w_kv, w_in_b, w_out_b, norm_a_g)
    state = dict(w_in_a=(w_in_a, m_w_in_a, v_w_in_a), w_out_a=(w_out_a, m_w_out_a, v_w_out_a),
                 w_kv=(w_kv, m_w_kv, v_w_kv), w_in_b=(w_in_b, m_w_in_b, v_w_in_b),
                 w_out_b=(w_out_b, m_w_out_b, v_w_out_b))

    def adamw_others(landed, after):
        return {n: _sum_adamw(r, *state[n], "adamw_" + n, after=after) for n, r in landed.items()}

    dx, r_wa, big, slab_g = _local_step(
        x[0], loss_target[0], positions, ga_g.reshape(1, D), _unshard_wa(wa_g), b_forget, qnorm_a_g, knorm_a_g,
        kv_norm_g.reshape(1, D), knorm_b_g.reshape(1, HD), norm_b_g, qnorm_b_g, sinks, woa_s, wkv_s, wib_s, wob_s,
        adamw_others)
    big["w_in_a"] = _sum_adamw_entry_view(*r_wa, *state["w_in_a"], "adamw_w_in_a")

    r2 = lambda a: a.reshape(1, -1)
    small_w = dict(norm_a_g=norm_a_g, b_forget=b_forget, qnorm_a_g=qnorm_a_g, knorm_a_g=knorm_a_g,
                   kv_norm_g=kv_norm_g, knorm_b_g=knorm_b_g, norm_b_g=norm_b_g, qnorm_b_g=qnorm_b_g, sinks=sinks)
    small_m = dict(norm_a_g=m_norm_a_g, b_forget=m_b_forget, qnorm_a_g=m_qnorm_a_g, knorm_a_g=m_knorm_a_g,
                   kv_norm_g=m_kv_norm_g, knorm_b_g=m_knorm_b_g, norm_b_g=m_norm_b_g, qnorm_b_g=m_qnorm_b_g,
                   sinks=m_sinks)
    small_v = dict(norm_a_g=v_norm_a_g, b_forget=v_b_forget, qnorm_a_g=v_qnorm_a_g, knorm_a_g=v_knorm_a_g,
                   kv_norm_g=v_kv_norm_g, knorm_b_g=v_knorm_b_g, norm_b_g=v_norm_b_g, qnorm_b_g=v_qnorm_b_g,
                   sinks=v_sinks)
    res = _small_adamw(slab_g, [r2(small_w[n]) for n in SMALL], [r2(small_m[n]) for n in SMALL],
                       [r2(small_v[n]) for n in SMALL])
    k = len(SMALL)
    small = {n: [res[q * k + p].reshape(small_w[n].shape) for q in range(4)] for p, n in enumerate(SMALL)}
    loss = res[4 * k][0, 0]

    order = ["norm_a_g", "w_in_a", "b_forget", "qnorm_a_g", "knorm_a_g", "w_out_a", "kv_norm_g", "w_kv",
             "knorm_b_g", "norm_b_g", "w_in_b", "qnorm_b_g", "sinks", "w_out_b"]

    def leaf(n, q):
        return big[n][q] if n in big else small[n][q]

    outs = [loss, dx[None]]
    for q in range(4):
        outs.extend(leaf(n, q) for n in order)
    return tuple(outs)
```

```python
import jax
import jax.numpy as jnp
from jax import lax
from jax.experimental import pallas as pl
from jax.experimental.pallas import tpu as pltpu

F32, BF16 = jnp.float32, jnp.bfloat16

S = 2048
D = 1024
HD = 64
N_HEADS = 16
N_DEV = 8
NA = 4352
GOFF = 3072
FOFF = 4096
RAW_F = 3072
RAW_G = RAW_F + N_HEADS
NA_RAW = 4112
EPS = 1e-6
QSCALE = 0.125
ROPE_THETA = 500000.0
ROT = 16
WIN = 128
TQ = 256
TK = 256
KS = TQ // 2
HPS = 8
HW = HPS * HD
TM = 256
RT = 256
RB = 512
CB = 256
LANES = 128

ADAM_LR, ADAM_B1, ADAM_B2, ADAM_EPS, ADAM_WD, ADAM_STEP = 0.001, 0.9, 0.999, 1e-08, 0.01, 10

VMEM_LIMIT = 56 * 1024 * 1024


def _params():
    return pltpu.CompilerParams(vmem_limit_bytes=VMEM_LIMIT)


def _sds(shape, dtype):
    return jax.ShapeDtypeStruct(shape, dtype)


def _dot_nt(a, b):
    return lax.dot_general(a, b, (((1,), (1,)), ((), ())), preferred_element_type=F32)


def _dot_tn(a, b):
    return lax.dot_general(a, b, (((0,), (0,)), ((), ())), preferred_element_type=F32)


def _dot_nn(a, b):
    return lax.dot_general(a, b, (((1,), (0,)), ((), ())), preferred_element_type=F32)


def _sigmoid(g):
    return 1.0 / (1.0 + jnp.exp(-g))


def _lane_iota(shape):
    return lax.broadcasted_iota(jnp.int32, shape, len(shape) - 1)


def _flips(kind):
    return (2, 4, 6) if kind == "a2a_chips" else tuple(range(1, N_DEV))


def _send_view(kind, ref, dev):
    if kind in ("gather_rows", "gather_cols"):
        return ref
    if kind == "a2a_slots":
        return ref.at[dev]
    if kind == "a2a_chips":
        return ref.at[dev >> 1]
    if kind == "a2a_rows":
        rows = ref.shape[0] // N_DEV
        return ref.at[pl.ds(pl.multiple_of(dev * rows, rows), rows)]
    cols = ref.shape[1] // N_DEV
    return ref.at[:, pl.ds(pl.multiple_of(dev * cols, cols), cols)]


def _land_view(kind, ref, dev):
    if kind == "gather_cols":
        cols = ref.shape[1] // N_DEV
        return ref.at[:, pl.ds(pl.multiple_of(dev * cols, cols), cols)]
    if kind == "a2a_chips":
        return ref.at[dev >> 1]
    return ref.at[dev]


def _landing_sds(kind, arr):
    if kind == "gather_rows":
        return _sds((N_DEV,) + arr.shape, arr.dtype)
    if kind == "gather_cols":
        return _sds((arr.shape[0], N_DEV * arr.shape[1]), arr.dtype)
    if kind == "a2a_rows":
        return _sds((N_DEV, arr.shape[0] // N_DEV, arr.shape[1]), arr.dtype)
    if kind == "a2a_cols":
        return _sds((N_DEV, arr.shape[0], arr.shape[1] // N_DEV), arr.dtype)
    return _sds(arr.shape, arr.dtype)


def _exchange_sems(n_parts):
    n = n_parts * (N_DEV - 1)
    return [pltpu.SemaphoreType.DMA((n,)), pltpu.SemaphoreType.DMA((n,)), pltpu.SemaphoreType.DMA((n_parts,))]


def _exchange_ops(kinds, srcs, dsts, sems, start, wait):
    send_sems, recv_sems, local_sems = sems
    x, y, c = lax.axis_index("x"), lax.axis_index("y"), lax.axis_index("c")
    me = 4 * x + 2 * y + c

    def local(a):
        return pltpu.make_async_copy(_send_view(kinds[a], srcs[a], me), _land_view(kinds[a], dsts[a], me),
                                     local_sems.at[a])

    def remote(a, k, landing_dev):
        peer = (x ^ ((k >> 2) & 1), y ^ ((k >> 1) & 1), c ^ (k & 1))
        sem = a * (N_DEV - 1) + k - 1
        return pltpu.make_async_remote_copy(
            src_ref=_send_view(kinds[a], srcs[a], me ^ k), dst_ref=_land_view(kinds[a], dsts[a], landing_dev),
            send_sem=send_sems.at[sem], recv_sem=recv_sems.at[sem], device_id=peer,
            device_id_type=pl.DeviceIdType.MESH)

    pairs = [(a, k) for k in range(1, N_DEV) for a in range(len(kinds)) if k in _flips(kinds[a])]
    if start:
        for a in range(len(kinds)):
            local(a).start()
        for a, k in pairs:
            remote(a, k, me).start()
    if wait:
        for a, k in pairs:
            remote(a, k, me ^ k).wait_recv()
            remote(a, k, me).wait_send()
        for a in range(len(kinds)):
            local(a).wait()


def _gather_two_level(srcs, dsts, sems, meanwhile=None):
    send_sems, recv_sems, local_sems = sems
    x, y, c = lax.axis_index("x"), lax.axis_index("y"), lax.axis_index("c")
    me, sibling = (x, y, c), (x, y, 1 - c)
    chips = [(1 - x, y), (x, 1 - y), (1 - x, 1 - y)]

    def slot(ref, dev):
        return ref.at[4 * dev[0] + 2 * dev[1] + dev[2]]

    def copy(a, k, block, to, src=None):
        return pltpu.make_async_remote_copy(
            src_ref=slot(dsts[a], block) if src is None else src, dst_ref=slot(dsts[a], block),
            send_sem=send_sems.at[a * (N_DEV - 1) + k], recv_sem=recv_sems.at[a * (N_DEV - 1) + k],
            device_id=to, device_id_type=pl.DeviceIdType.MESH)

    parts = range(len(srcs))
    mine = [pltpu.make_async_copy(srcs[a], slot(dsts[a], me), local_sems.at[a]) for a in parts]
    first = [copy(a, 0, me, sibling, src=srcs[a]) for a in parts]
    first += [copy(a, 1 + j, me, (*chip, c), src=srcs[a]) for j, chip in enumerate(chips) for a in parts]
    for cp in mine + first:
        cp.start()
    if meanwhile is not None:
        meanwhile()
    passed = []
    for j, chip in enumerate(chips):
        for a in parts:
            copy(a, 1 + j, (*chip, c), me).wait_recv()
            fwd = copy(a, 4 + j, (*chip, c), sibling)
            fwd.start()
            passed.append(fwd)
    for a in parts:
        copy(a, 0, sibling, me).wait_recv()
        for j, chip in enumerate(chips):
            copy(a, 4 + j, (*chip, 1 - c), me).wait_recv()
    for cp in first + passed:
        cp.wait_send()
    for cp in mine:
        cp.wait()


def _call(body, *, name, args, in_specs, out_specs, out_shape, grid=(), scratch_shapes=(), aliases=None, rider=()):
    n_in, n_out, n_scr, n_r = len(in_specs), len(out_specs), len(scratch_shapes), len(rider)
    kinds = [kind for kind, _ in rider]

    def kernel_body(*refs):
        c_in, r_in = refs[:n_in], refs[n_in:n_in + n_r]
        c_out = refs[n_in + n_r:n_in + n_r + n_out]
        r_out = refs[n_in + n_r + n_out:n_in + 2 * n_r + n_out]
        rest = refs[n_in + 2 * n_r + n_out:]
        c_scr, sems = rest[:n_scr], rest[n_scr:]
        if n_r:
            assert grid, "a rider needs a gridded call"
            ids = [pl.program_id(ax) for ax in range(len(grid))]
            first, last = ids[0] == 0, ids[0] == grid[0] - 1
            for pid, size in zip(ids[1:], grid[1:]):
                first = first & (pid == 0)
                last = last & (pid == size - 1)
            pl.when(first)(lambda: _exchange_ops(kinds, r_in, r_out, sems, True, False))
        body(*c_in, *c_out, *c_scr)
        if n_r:
            pl.when(last)(lambda: _exchange_ops(kinds, r_in, r_out, sems, False, True))

    anyspec = pl.BlockSpec(memory_space=pl.ANY)
    params = pltpu.CompilerParams(vmem_limit_bytes=VMEM_LIMIT, has_side_effects=bool(n_r))
    outs = pl.pallas_call(
        kernel_body, name=name, grid=grid, in_specs=list(in_specs) + [anyspec] * n_r,
        out_specs=list(out_specs) + [anyspec] * n_r,
        out_shape=list(out_shape) + [_landing_sds(kind, arr) for kind, arr in rider],
        scratch_shapes=list(scratch_shapes) + (_exchange_sems(n_r) if n_r else []),
        input_output_aliases=aliases or {}, compiler_params=params)(*args, *[arr for _, arr in rider])
    return list(outs)


def _mm(a, b, mode, tm, tn, tk, out_dtype=F32, add=None, name="mm", rider=()):
    if mode == "nn":
        (m, k), n = a.shape, b.shape[1]
        a_spec = pl.BlockSpec((tm, tk), lambda i, j, kk: (i, kk))
        b_spec = pl.BlockSpec((tk, tn), lambda i, j, kk: (kk, j))
        dot = _dot_nn
    elif mode == "nt":
        (m, k), n = a.shape, b.shape[0]
        a_spec = pl.BlockSpec((tm, tk), lambda i, j, kk: (i, kk))
        b_spec = pl.BlockSpec((tn, tk), lambda i, j, kk: (j, kk))
        dot = _dot_nt
    else:
        (k, m), n = a.shape, b.shape[1]
        a_spec = pl.BlockSpec((tk, tm), lambda i, j, kk: (kk, i))
        b_spec = pl.BlockSpec((tk, tn), lambda i, j, kk: (kk, j))
        dot = _dot_tn
    assert m % tm == 0 and n % tn == 0 and k % tk == 0, (m, n, k, tm, tn, tk)
    nk = k // tk
    has_add = add is not None

    def body(*refs):
        if has_add:
            a_ref, b_ref, add_ref, o_ref, acc = refs
        else:
            a_ref, b_ref, o_ref, acc = refs
        p = dot(a_ref[...].astype(BF16), b_ref[...].astype(BF16))

        def finish(total):
            if has_add:
                total = add_ref[...] + total
            o_ref[...] = total.astype(out_dtype)

        if nk == 1:
            finish(p)
        else:
            kk = pl.program_id(2)

            @pl.when(kk == 0)
            def _():
                acc[...] = p

            @pl.when(kk > 0)
            def _():
                acc[...] += p

            @pl.when(kk == nk - 1)
            def _():
                finish(acc[...])

    in_specs = [a_spec, b_spec]
    args = [a, b]
    if has_add:
        in_specs.append(pl.BlockSpec((tm, tn), lambda i, j, kk: (i, j)))
        args.append(add)
    acc_shape = (tm, tn) if nk > 1 else (8, LANES)
    outs = _call(body, name=name, args=args, grid=(m // tm, n // tn, nk), in_specs=in_specs,
                 out_specs=[pl.BlockSpec((tm, tn), lambda i, j, kk: (i, j))], out_shape=[_sds((m, n), out_dtype)],
                 scratch_shapes=[pltpu.VMEM(acc_shape, F32)], rider=rider)
    return outs if rider else outs[0]


def _rms_rinv(x):
    return lax.rsqrt(jnp.mean(x * x, axis=-1, keepdims=True) + EPS)


def _rms_bwd_core(du, x, g):
    r = _rms_rinv(x)
    dug = du * g
    dx = r * (dug - x * ((r * r) * jnp.mean(dug * x, axis=-1, keepdims=True)))
    dg = jnp.sum(du * (x * r), axis=0, keepdims=True)
    return dx, dg


def _half_ones():
    r = lax.broadcasted_iota(jnp.int32, (LANES, LANES), 0)
    c = lax.broadcasted_iota(jnp.int32, (LANES, LANES), 1)
    return ((r < HD) == (c < HD)).astype(BF16)


def _half_sum(v, lo_half):
    if lo_half.dtype == jnp.bool_:
        s0 = jnp.sum(jnp.where(lo_half, v, 0.0), axis=-1, keepdims=True)
        s1 = jnp.sum(jnp.where(lo_half, 0.0, v), axis=-1, keepdims=True)
        return jnp.where(lo_half, s0, s1)
    hi = v.astype(BF16)
    lo = (v - hi.astype(F32)).astype(BF16)
    return _dot_nn(hi, lo_half) + _dot_nn(lo, lo_half)


def _head_rinv(x, lo_half):
    return lax.rsqrt(_half_sum(x * x, lo_half) * (1.0 / HD) + EPS)


def _head_norm_bwd(dn, x, g, lo_half):
    r = _head_rinv(x, lo_half)
    dng = dn * g
    dx = r * (dng - x * ((r * r) * (_half_sum(dng * x, lo_half) * (1.0 / HD))))
    dg = jnp.sum(dn * (x * r), axis=0, keepdims=True)
    return dx, dg


def _rope_swap(x, lane):
    l64 = lane & (HD - 1)
    return jnp.where(l64 < ROT // 2, pltpu.roll(x, LANES - ROT // 2, 1), pltpu.roll(x, ROT // 2, 1))


def _rope_fwd(x, cos, sin, lane):
    return x * cos + _rope_swap(x, lane) * sin


def _rope_bwd(dy, cos, sin, lane):
    return dy * cos + jnp.where((lane & (HD - 1)) < ROT, _rope_swap(dy * sin, lane), 0.0)


def _g2(g_ref):
    g = g_ref[...]
    return jnp.concatenate([g, g], axis=-1)


def _pairs(width):
    return [slice(LANES * c, LANES * (c + 1)) for c in range(width // LANES)]


def _pick_lane(block, lane, idx):
    return jnp.sum(jnp.where(lane == idx, block, 0.0), axis=-1, keepdims=True)


def _head_a(x, g, wa, gq, gk, b_pad):
    def body(x_ref, g_ref, w_ref, gq_ref, gk_ref, b_ref, u_ref, p_ref, qo_ref, ko_ref, vo_ref, c_ref, cbc_ref, carry):
        @pl.when(pl.program_id(0) == 0)
        def _():
            carry[...] = jnp.zeros_like(carry)

        xv = x_ref[...]
        u = ((xv * _rms_rinv(xv)) * g_ref[...]).astype(BF16)
        u_ref[...] = u
        for lo in range(0, NA, D):
            hi = min(lo + D, NA)
            p_ref[:, lo:hi] = _dot_nn(u, w_ref[:, lo:hi])
        lane = _lane_iota((RT, LANES))
        lo_half = lane < HD
        gq2, gk2 = _g2(gq_ref), _g2(gk_ref)
        for c in _pairs(D):
            q = p_ref[:, c]
            k = p_ref[:, D + c.start:D + c.stop]
            qo_ref[:, c] = (((q * _head_rinv(q, lo_half)) * gq2) * QSCALE).astype(BF16)
            ko_ref[:, c] = ((k * _head_rinv(k, lo_half)) * gk2).astype(BF16)
        vo_ref[...] = p_ref[:, 2 * D:3 * D].astype(BF16)

        z = p_ref[:, FOFF:FOFF + LANES] + b_ref[...]
        logf = jnp.minimum(z, 0.0) - jnp.log1p(jnp.exp(-jnp.abs(z)))
        r = lax.broadcasted_iota(jnp.int32, (RT, RT), 0)
        cc = lax.broadcasted_iota(jnp.int32, (RT, RT), 1)
        tri = (r >= cc).astype(F32)
        loc = jnp.dot(tri, logf, precision=lax.Precision.HIGHEST, preferred_element_type=F32) + carry[0:1, :]
        c_ref[...] = loc
        carry[0:1, :] = loc[RT - 1:RT, :]
        for h in range(N_HEADS):
            cbc_ref[:, LANES * h:LANES * (h + 1)] = jnp.broadcast_to(_pick_lane(loc, lane, h), (RT, LANES))

    row = lambda width: pl.BlockSpec((RT, width), lambda i: (i, 0))
    whole = lambda arr: pl.BlockSpec(arr.shape, lambda i: (0,) * arr.ndim, pipeline_mode=pl.Buffered(1))
    return pl.pallas_call(
        body, name="head_a", grid=(S // RT,),
        in_specs=[row(D), whole(g), whole(wa), whole(gq), whole(gk), whole(b_pad)],
        out_specs=[row(D), row(NA), row(D), row(D), row(D), row(LANES), row(N_HEADS * LANES)],
        out_shape=[_sds((S, D), BF16), _sds((S, NA), F32)] + [_sds((S, D), BF16)] * 3
        + [_sds((S, LANES), F32), _sds((S, N_HEADS * LANES), F32)],
        scratch_shapes=[pltpu.VMEM((8, LANES), F32)], compiler_params=_params())(x, g, wa, gq, gk, b_pad)


def _key_le_query(offset, keys=TK):
    r = lax.broadcasted_iota(jnp.int32, (keys, TQ), 0)
    c = lax.broadcasted_iota(jnp.int32, (keys, TQ), 1)
    return (r + offset) <= c


def _widen(tile):
    return jnp.concatenate([tile] * (TQ // LANES), axis=1)


def _fox_fwd(qn, kn, vb, proj, crow, cbc, rider=()):
    nq = S // TQ

    def body(q_ref, k_ref, v_ref, g_ref, cq_ref, cbc_ref, o_ref, z_ref, lse_ref, st_s, pt_s):
        i = pl.program_id(1)
        qs = [q_ref[:, HD * hh:HD * (hh + 1)] for hh in range(HPS)]
        cqs = [cq_ref[hh, 0] for hh in range(HPS)]

        def scores(s, hh):
            off = pl.multiple_of(s * KS, KS)
            kj = k_ref[pl.ds(off, KS), HD * hh:HD * (hh + 1)]
            return (_dot_nt(kj, qs[hh]) + cqs[hh]) - _widen(cbc_ref[pl.ds(off, KS), LANES * hh:LANES * (hh + 1)])

        def values(s, hh, pt):
            off = pl.multiple_of(s * KS, KS)
            return _dot_tn(v_ref[pl.ds(off, KS), HD * hh:HD * (hh + 1)], pt)

        def step(s, slot, carries, mask=None, last=False):
            if not last:
                for hh in range(HPS):
                    st_s[1 - slot, hh] = scores(s + 1, hh)
            pvs = [values(jnp.maximum(s - 1, 0), hh, pt_s[1 - slot, hh]) for hh in range(HPS)]
            out = []
            for hh in range(HPS):
                m, l, acc = carries[hh]
                st = st_s[slot, hh]
                if mask is not None:
                    st = jnp.where(mask, st, -jnp.inf)
                m_new = jnp.maximum(m, jnp.max(st, axis=0, keepdims=True))
                pt = jnp.exp(st - m_new)
                alpha = jnp.exp(m - m_new)
                pt_s[slot, hh] = pt.astype(BF16)
                out.append((m_new, alpha * l + jnp.sum(pt, axis=0, keepdims=True), alpha * (acc + pvs[hh])))
            return tuple(out)

        for hh in range(HPS):
            st_s[0, hh] = scores(0, hh)
            pt_s[1, hh] = jnp.zeros((KS, TQ), BF16)
        one = (jnp.full((1, TQ), -jnp.inf, F32), jnp.zeros((1, TQ), F32), jnp.zeros((HD, TQ), F32))
        carries = lax.fori_loop(0, i, lambda t, cr: step(2 * t + 1, 1, step(2 * t, 0, cr)), (one,) * HPS)
        carries = step(2 * i, 0, carries, mask=_key_le_query(0, KS))
        carries = step(2 * i + 1, 1, carries, mask=_key_le_query(KS, KS), last=True)
        accs = []
        for hh in range(HPS):
            m, l, acc = carries[hh]
            acc = acc + values(2 * i + 1, hh, pt_s[1, hh])
            accs.append(acc / l)
            lse_ref[hh, 0] = m + jnp.log(l)
        o = jnp.concatenate(accs, axis=0).T
        o_ref[...] = o
        g = g_ref[...]
        z_ref[...] = (o * (g * _sigmoid(g))).astype(BF16)

    qblk = pl.BlockSpec((TQ, HW), lambda hp, i: (i, hp))
    full = pl.BlockSpec((S, HW), lambda hp, i: (0, hp))
    rows = pl.BlockSpec((HPS, 1, 1, TQ), lambda hp, i: (hp, i, 0, 0))
    return _call(
        body, name="fox_fwd", args=(qn, kn, vb, proj, crow, cbc), grid=(N_HEADS // HPS, nq),
        in_specs=[qblk, full, full,
                  pl.BlockSpec((TQ, HW), lambda hp, i: (i, GOFF // HW + hp)),
                  rows, pl.BlockSpec((S, HPS * LANES), lambda hp, i: (0, hp))],
        out_specs=[qblk, qblk, rows],
        out_shape=[_sds((S, D), F32), _sds((S, D), BF16), _sds((N_HEADS, nq, 1, TQ), F32)],
        scratch_shapes=[pltpu.VMEM((2, HPS, KS, TQ), F32), pltpu.VMEM((2, HPS, KS, TQ), BF16)], rider=rider)


def _fox_bwd_pre(dh, w_out, proj, o):
    nq, rows = S // TQ, 2 * TQ
    per = rows // TQ

    def body(dh_ref, w_ref, g_ref, o_ref, do_ref, dg_ref, delta_ref):
        g = g_ref[...]
        sg = _sigmoid(g)
        dzv = _dot_nt(dh_ref[...].astype(BF16), w_ref[...])
        ov = o_ref[...]
        do = dzv * (g * sg)
        dg_ref[...] = (dzv * ov * (sg * (1.0 + g * (1.0 - sg)))).astype(BF16)
        do_ref[...] = do.astype(BF16)
        prod_t = (do * ov).T
        for h in range(N_HEADS):
            for b in range(per):
                delta_ref[h, b] = jnp.sum(prod_t[HD * h:HD * (h + 1), TQ * b:TQ * (b + 1)], axis=0, keepdims=True)

    row = pl.BlockSpec((rows, D), lambda i: (i, 0))
    return pl.pallas_call(
        body, name="fox_bwd_pre", grid=(S // rows,),
        in_specs=[row, pl.BlockSpec(w_out.shape, lambda i: (0, 0), pipeline_mode=pl.Buffered(1)),
                  pl.BlockSpec((rows, D), lambda i: (i, GOFF // D)), row],
        out_specs=[row, row, pl.BlockSpec((N_HEADS, per, 1, TQ), lambda i: (0, i, 0, 0))],
        out_shape=[_sds((S, D), BF16), _sds((S, D), BF16), _sds((N_HEADS, nq, 1, TQ), F32)],
        compiler_params=_params())(dh, w_out, proj, o)


def _fox_bwd(qn, kn, vb, dob, lse, delta, crow, cbc, rider=()):
    nq, nkb = S // TQ, S // TK

    def body(q_ref, k_ref, v_ref, do_ref, lse_ref, del_ref, cq_ref, cbc_ref,
             dk_ref, dv_ref, dcs_ref, dq_ref, dr_ref, st_s, dp_s, pt_s, ds_s, dq_acc, dr_acc):
        j = pl.program_id(1)

        @pl.when(j == 0)
        def _():
            dq_acc[...] = jnp.zeros_like(dq_acc)
            dr_acc[...] = jnp.zeros_like(dr_acc)

        kjs = [k_ref[:, HD * hh:HD * (hh + 1)] for hh in range(HPS)]
        vjs = [v_ref[:, HD * hh:HD * (hh + 1)] for hh in range(HPS)]

        def rows_of(ref, u, hh):
            off = pl.multiple_of(u * TQ, TQ)
            return ref[pl.ds(off, TQ), HD * hh:HD * (hh + 1)]

        def products(u, hh):
            st = (_dot_nt(kjs[hh], rows_of(q_ref, u, hh)) + cq_ref[hh, u]) - _widen(
                cbc_ref[:, LANES * hh:LANES * (hh + 1)])
            return st, _dot_nt(vjs[hh], rows_of(do_ref, u, hh))

        def step(u, slot, carries, masked=False):
            nxt = jnp.minimum(u + 1, nq - 1)
            for hh in range(HPS):
                st_s[1 - slot, hh], dp_s[1 - slot, hh] = products(nxt, hh)
            prev = jnp.maximum(u - 1, 0)
            dvs = [_dot_nn(pt_s[1 - slot, hh], rows_of(do_ref, prev, hh)) for hh in range(HPS)]
            dks = [_dot_nn(ds_s[1 - slot, hh], rows_of(q_ref, prev, hh)) for hh in range(HPS)]
            for hh in range(HPS):
                dq_acc[hh, prev] += _dot_tn(kjs[hh], ds_s[1 - slot, hh])
            out = []
            for hh in range(HPS):
                dk, dv, dcs = carries[hh]
                st = st_s[slot, hh]
                if masked:
                    st = jnp.where(_key_le_query((j - u) * TQ), st, -jnp.inf)
                pt = jnp.exp(st - lse_ref[hh, u])
                dst = pt * (dp_s[slot, hh] - del_ref[hh, u])
                pt_s[slot, hh] = pt.astype(BF16)
                ds_s[slot, hh] = dst.astype(BF16)
                dr_acc[hh, u] += jnp.sum(dst, axis=0, keepdims=True)
                out.append((dk + dks[hh], dv + dvs[hh], dcs + (dst[:, :LANES] + dst[:, LANES:])))
            return tuple(out)

        t0 = j // 2
        for hh in range(HPS):
            st_s[0, hh], dp_s[0, hh] = products(2 * t0, hh)
            pt_s[1, hh] = jnp.zeros((TK, TQ), BF16)
            ds_s[1, hh] = jnp.zeros((TK, TQ), BF16)
        one = (jnp.zeros((TK, HD), F32), jnp.zeros((TK, HD), F32), jnp.zeros((TK, LANES), F32))
        carries = step(2 * t0 + 1, 1, step(2 * t0, 0, (one,) * HPS, masked=True), masked=True)
        carries = lax.fori_loop(t0 + 1, nq // 2, lambda t, cr: step(2 * t + 1, 1, step(2 * t, 0, cr)), carries)
        dks, dvs = [], []
        lane = _lane_iota((TK, LANES))
        dcs_all = jnp.zeros((TK, LANES), F32)
        for hh in range(HPS):
            dk, dv, dcs = carries[hh]
            dks.append(dk + _dot_nn(ds_s[1, hh], rows_of(q_ref, nq - 1, hh)))
            dvs.append(dv + _dot_nn(pt_s[1, hh], rows_of(do_ref, nq - 1, hh)))
            dq_acc[hh, nq - 1] += _dot_tn(kjs[hh], ds_s[1, hh])
            dcs_all = jnp.where(lane == HPS * pl.program_id(0) + hh, -jnp.sum(dcs, axis=-1, keepdims=True), dcs_all)
        dcs_ref[0] = dcs_all
        dk_ref[...] = jnp.concatenate(dks, axis=-1)
        dv_ref[...] = jnp.concatenate(dvs, axis=-1).astype(BF16)

        @pl.when(j == nkb - 1)
        def _():
            for i in range(nq):
                dq_ref[TQ * i:TQ * (i + 1), :] = jnp.concatenate([dq_acc[hh, i] for hh in range(HPS)], axis=0).T
            dr_ref[...] = dr_acc[...]

    kblk = pl.BlockSpec((TK, HW), lambda hp, j: (j, hp))
    full = pl.BlockSpec((S, HW), lambda hp, j: (0, hp))
    rows = pl.BlockSpec((HPS, nq, 1, TQ), lambda hp, j: (hp, 0, 0, 0))
    cblk = pl.BlockSpec((TK, HPS * LANES), lambda hp, j: (j, hp))
    return _call(
        body, name="fox_bwd", args=(qn, kn, vb, dob, lse, delta, crow, cbc), grid=(N_HEADS // HPS, nkb),
        in_specs=[full, kblk, kblk, full, rows, rows, rows, cblk],
        out_specs=[kblk, kblk, pl.BlockSpec((1, TK, LANES), lambda hp, j: (hp, j, 0)), full, rows],
        out_shape=[_sds((S, D), F32), _sds((S, D), BF16), _sds((N_HEADS // HPS, S, LANES), F32), _sds((S, D), F32),
                   _sds((N_HEADS, nq, 1, TQ), F32)],
        scratch_shapes=[pltpu.VMEM((2, HPS, TK, TQ), F32), pltpu.VMEM((2, HPS, TK, TQ), F32),
                        pltpu.VMEM((2, HPS, TK, TQ), BF16), pltpu.VMEM((2, HPS, TK, TQ), BF16),
                        pltpu.VMEM((HPS, nq, HD, TQ), F32), pltpu.VMEM((HPS, nq, 1, TQ), F32)], rider=rider)


def _prep_a_bwd(dq, dk, dv, dgate, drow, dcs, proj, b_pad, gq, gk):
    nt = S // TM

    def body(dq_ref, dk_ref, dv_ref, dgt_ref, dr_ref, dcs_ref, xq_ref, xk_ref, f_ref, b_ref, gq_ref, gk_ref,
             o_ref, dgq_ref, dgk_ref, db_ref, carry):
        @pl.when(pl.program_id(0) == 0)
        def _():
            carry[...] = jnp.zeros_like(carry)
            dgq_ref[...] = jnp.zeros_like(dgq_ref)
            dgk_ref[...] = jnp.zeros_like(dgk_ref)
            db_ref[...] = jnp.zeros_like(db_ref)

        lane = _lane_iota((TM, LANES))
        lo_half = _half_ones()
        gq2, gk2 = _g2(gq_ref), _g2(gk_ref)
        dgq, dgk = jnp.zeros((1, LANES), F32), jnp.zeros((1, LANES), F32)
        for c in _pairs(D):
            dxq, dg = _head_norm_bwd(dq_ref[:, c] * QSCALE, xq_ref[:, c], gq2, lo_half)
            o_ref[:, c] = dxq.astype(BF16)
            dgq = dgq + dg
            dxk, dg = _head_norm_bwd(dk_ref[:, c], xk_ref[:, c], gk2, lo_half)
            o_ref[:, D + c.start:D + c.stop] = dxk.astype(BF16)
            dgk = dgk + dg
        dgq_ref[...] += dgq
        dgk_ref[...] += dgk
        o_ref[:, 2 * D:3 * D] = dv_ref[...]
        o_ref[:, GOFF:GOFF + D] = dgt_ref[...]

        dc = dr_ref[...]
        for group in range(N_HEADS // HPS):
            dc = dc + dcs_ref[group]
        r = lax.broadcasted_iota(jnp.int32, (TM, TM), 0)
        c = lax.broadcasted_iota(jnp.int32, (TM, TM), 1)
        tri = (c >= r).astype(F32)
        dlogf = jnp.dot(tri, dc, precision=lax.Precision.HIGHEST, preferred_element_type=F32) + carry[0:1, :]
        carry[0:1, :] = dlogf[0:1, :]
        df = dlogf * (1.0 / (1.0 + jnp.exp(f_ref[...] + b_ref[...])))
        db_ref[...] += jnp.sum(df, axis=0, keepdims=True)
        o_ref[:, FOFF:FOFF + LANES] = df.astype(BF16)
        o_ref[:, FOFF + LANES:NA] = jnp.zeros((TM, NA - FOFF - LANES), BF16)

    rev = lambda width, col: pl.BlockSpec((TM, width), lambda i: (nt - 1 - i, col))
    gspec = pl.BlockSpec((1, HD), lambda i: (0, 0))
    acc = pl.BlockSpec((1, LANES), lambda i: (0, 0))
    return pl.pallas_call(
        body, name="prep_a_bwd", grid=(nt,),
        in_specs=[rev(D, 0), rev(D, 0), rev(D, 0), rev(D, 0), rev(LANES, 0),
                  pl.BlockSpec((N_HEADS // HPS, TM, LANES), lambda i: (0, nt - 1 - i, 0)),
                  rev(D, 0), rev(D, 1), rev(LANES, FOFF // LANES), acc, gspec, gspec],
        out_specs=[rev(NA, 0), acc, acc, acc],
        out_shape=[_sds((S, NA), BF16)] + [_sds((1, LANES), F32)] * 3,
        scratch_shapes=[pltpu.VMEM((8, LANES), F32)],
        compiler_params=_params())(dq, dk, dv, dgate, drow, dcs, proj, proj, proj, b_pad, gq, gk)


def _head_b(x, z_a, w_out_a, g_kv, g_b, w_kv, w_in_b, gq, gk, cos2, sin2):
    nkv = w_kv.shape[1] // 2

    def body(x_ref, z_ref, wo_ref, gkv_ref, gb_ref, wkv_ref, wb_ref, gq_ref, gk_ref, c_ref, s_ref,
             h_ref, ukv_ref, ub_ref, kv_ref, pb_ref, qo_ref, ko_ref, vo_ref):
        xv = x_ref[...] + _dot_nn(z_ref[...], wo_ref[...])
        h_ref[...] = xv
        xn = xv * _rms_rinv(xv)
        ukv = (xn * gkv_ref[...]).astype(BF16)
        ub = (xn * gb_ref[...]).astype(BF16)
        ukv_ref[...] = ukv
        ub_ref[...] = ub
        kv_ref[...] = _dot_nn(ukv, wkv_ref[...])
        for lo in range(0, 2 * D, D):
            pb_ref[:, lo:lo + D] = _dot_nn(ub, wb_ref[:, lo:lo + D])
        lane = _lane_iota((RT, LANES))
        lo_half = lane < HD
        cos, sin = c_ref[...], s_ref[...]
        gq2, gk2 = _g2(gq_ref), _g2(gk_ref)
        for c in _pairs(D):
            q = pb_ref[:, c]
            qo_ref[:, c] = (_rope_fwd((q * _head_rinv(q, lo_half)) * gq2, cos, sin, lane) * QSCALE).astype(BF16)
        for c in _pairs(nkv):
            k = kv_ref[:, c]
            ko_ref[:, c] = _rope_fwd((k * _head_rinv(k, lo_half)) * gk2, cos, sin, lane).astype(BF16)
        vo_ref[...] = kv_ref[:, nkv:2 * nkv].astype(BF16)

    row = lambda width: pl.BlockSpec((RT, width), lambda i: (i, 0))
    whole = lambda arr: pl.BlockSpec(arr.shape, lambda i: (0,) * arr.ndim, pipeline_mode=pl.Buffered(1))
    return pl.pallas_call(
        body, name="head_b", grid=(S // RT,),
        in_specs=[row(D), row(D), whole(w_out_a), whole(g_kv), whole(g_b), whole(w_kv), whole(w_in_b), whole(gq),
                  whole(gk), row(LANES), row(LANES)],
        out_specs=[row(D), row(D), row(D), row(2 * nkv), row(2 * D), row(D), row(nkv), row(nkv)],
        out_shape=[_sds((S, D), F32), _sds((S, D), BF16), _sds((S, D), BF16), _sds((S, 2 * nkv), F32),
                   _sds((S, 2 * D), F32), _sds((S, D), BF16), _sds((S, nkv), BF16), _sds((S, nkv), BF16)],
        compiler_params=_params())(x, z_a, w_out_a, g_kv, g_b, w_kv, w_in_b, gq, gk, cos2, sin2)


N_KV, GRP = 4, 4


def _swa_mask(n):
    r = lax.broadcasted_iota(jnp.int32, (2 * WIN, GRP * WIN), 0)
    q = lax.broadcasted_iota(jnp.int32, (2 * WIN, GRP * WIN), 1) & (WIN - 1)
    return (r > q) & (r <= q + WIN) & ((r >= WIN) | (n > 0))


def _stack4(ref_or_val, base):
    return jnp.concatenate([ref_or_val[:, base + HD * g: base + HD * (g + 1)] for g in range(GRP)], axis=0)


def _unstack4(xt):
    return jnp.concatenate([xt[:, WIN * g:WIN * (g + 1)] for g in range(GRP)], axis=0).T


def _band(prev_ref, cur_ref, kh):
    return jnp.concatenate([prev_ref[:, HD * kh:HD * (kh + 1)], cur_ref[:, HD * kh:HD * (kh + 1)]], axis=0)


def _sink_row(s_ref, first):
    lane = _lane_iota((1, GRP * WIN))
    row = jnp.full((1, GRP * WIN), s_ref[first + GRP - 1], F32)
    for g in range(GRP - 2, -1, -1):
        row = jnp.where(lane < WIN * (g + 1), s_ref[first + g], row)
    return row


def _swa_fwd(qb, ksh, vsh, pb, sinks):
    nb = S // WIN

    def body(q_ref, kp_ref, kc_ref, vp_ref, vc_ref, g_ref, s_ref, o_ref, z_ref, lse_ref):
        n = pl.program_id(0)
        valid = _swa_mask(n)
        outs = []
        for kh in range(N_KV):
            kb, vb = _band(kp_ref, kc_ref, kh), _band(vp_ref, vc_ref, kh)
            st = jnp.where(valid, _dot_nt(kb, _stack4(q_ref, GRP * HD * kh)), -jnp.inf)
            sink = _sink_row(s_ref, GRP * kh)
            m = jnp.maximum(jnp.max(st, axis=0, keepdims=True), sink)
            pt = jnp.exp(st - m)
            l = jnp.sum(pt, axis=0, keepdims=True) + jnp.exp(sink - m)
            outs.append(_unstack4(_dot_tn(vb, pt.astype(BF16)) / l))
            lse = m + jnp.log(l)
            for g in range(GRP):
                lse_ref[GRP * kh + g, 0] = lse[:, WIN * g:WIN * (g + 1)]
        o = jnp.concatenate(outs, axis=-1)
        o_ref[...] = o
        g = g_ref[...]
        z_ref[...] = (o * (g * _sigmoid(g))).astype(BF16)

    row = pl.BlockSpec((WIN, D), lambda n: (n, 0))
    prev = pl.BlockSpec((WIN, N_KV * HD), lambda n: (jnp.maximum(n - 1, 0), 0))
    cur = pl.BlockSpec((WIN, N_KV * HD), lambda n: (n, 0))
    return pl.pallas_call(
        body, name="swa_fwd", grid=(nb,),
        in_specs=[row, prev, cur, prev, cur, pl.BlockSpec((WIN, D), lambda n: (n, 1)),
                  pl.BlockSpec(memory_space=pltpu.SMEM)],
        out_specs=[row, row, pl.BlockSpec((N_HEADS, 1, 1, WIN), lambda n: (0, n, 0, 0))],
        out_shape=[_sds((S, D), F32), _sds((S, D), BF16), _sds((N_HEADS, nb, 1, WIN), F32)],
        compiler_params=_params())(qb, ksh, ksh, vsh, vsh, pb, sinks)


def _swa_bwd(qb, ksh, vsh, dz, o, lse, pb, sinks, gq, cos2, sin2):
    nb = S // WIN

    def body(q_ref, kp_ref, kc_ref, vp_ref, vc_ref, dz_ref, o_ref, lse_ref, x_ref, g_ref, s_ref, gq_ref, c_ref, sn_ref,
             dpb_ref, dka_ref, dkb_ref, dva_ref, dvb_ref, dsink_ref, dgq_ref):
        n = pl.program_id(0)

        @pl.when(n == 0)
        def _():
            dsink_ref[...] = jnp.zeros_like(dsink_ref)
            dgq_ref[...] = jnp.zeros_like(dgq_ref)

        valid = _swa_mask(n)
        g = g_ref[...]
        sg = _sigmoid(g)
        dzv = dz_ref[...]
        ov = o_ref[...]
        do = dzv * (g * sg)
        dpb_ref[:, D:2 * D] = (dzv * ov * (sg * (1.0 + g * (1.0 - sg)))).astype(BF16)
        prod_t = (do * ov).T
        lane1 = _lane_iota((1, LANES))
        dqs, dkas, dkbs, dvas, dvbs = [], [], [], [], []
        dsink = jnp.zeros((1, LANES), F32)
        for kh in range(N_KV):
            kb, vb = _band(kp_ref, kc_ref, kh), _band(vp_ref, vc_ref, kh)
            base = GRP * HD * kh
            qs = _stack4(q_ref, base)
            dos = _stack4(do, base).astype(BF16)
            delta = jnp.concatenate(
                [jnp.sum(prod_t[base + HD * gg:base + HD * (gg + 1), :], axis=0, keepdims=True)
                 for gg in range(GRP)], axis=1)
            lse = jnp.concatenate([lse_ref[GRP * kh + gg, 0] for gg in range(GRP)], axis=1)
            st = jnp.where(valid, _dot_nt(kb, qs), -jnp.inf)
            pt = jnp.exp(st - lse)
            dst = pt * (_dot_nt(vb, dos) - delta)
            dsb = dst.astype(BF16)
            dqs.append(_unstack4(_dot_tn(kb, dsb)))
            dkband = _dot_nn(dsb, qs)
            dvband = _dot_nn(pt.astype(BF16), dos)
            dkbs.append(dkband[0:WIN, :])
            dkas.append(dkband[WIN:2 * WIN, :])
            dvbs.append(dvband[0:WIN, :])
            dvas.append(dvband[WIN:2 * WIN, :])
            ps_delta = jnp.exp(_sink_row(s_ref, GRP * kh) - lse) * delta
            for gg in range(GRP):
                val = jnp.sum(ps_delta[:, WIN * gg:WIN * (gg + 1)], axis=1, keepdims=True)
                dsink = dsink - jnp.where(lane1 == GRP * kh + gg, val, 0.0)
        dka_ref[...] = jnp.concatenate(dkas, axis=-1)
        dkb_ref[...] = jnp.concatenate(dkbs, axis=-1)
        dva_ref[...] = jnp.concatenate(dvas, axis=-1)
        dvb_ref[...] = jnp.concatenate(dvbs, axis=-1)
        dsink_ref[...] += dsink

        lane = _lane_iota((WIN, LANES))
        g2, cos, sin = _g2(gq_ref), c_ref[...], sn_ref[...]
        lo_half = _half_ones()
        dg_tot = jnp.zeros((1, LANES), F32)
        for kh in range(N_KV):
            for c in _pairs(GRP * HD):
                cols = slice(GRP * HD * kh + c.start, GRP * HD * kh + c.stop)
                dn = _rope_bwd(dqs[kh][:, c] * QSCALE, cos, sin, lane)
                dx, dg = _head_norm_bwd(dn, x_ref[:, cols], g2, lo_half)
                dpb_ref[:, cols] = dx.astype(BF16)
                dg_tot = dg_tot + dg
        dgq_ref[...] += dg_tot

    row = pl.BlockSpec((WIN, D), lambda n: (n, 0))
    prev = pl.BlockSpec((WIN, N_KV * HD), lambda n: (jnp.maximum(n - 1, 0), 0))
    cur = pl.BlockSpec((WIN, N_KV * HD), lambda n: (n, 0))
    acc = pl.BlockSpec((1, LANES), lambda n: (0, 0))
    tab = pl.BlockSpec((WIN, LANES), lambda n: (n, 0))
    return pl.pallas_call(
        body, name="swa_bwd", grid=(nb,),
        in_specs=[row, prev, cur, prev, cur, row, row, pl.BlockSpec((N_HEADS, 1, 1, WIN), lambda n: (0, n, 0, 0)),
                  row, pl.BlockSpec((WIN, D), lambda n: (n, 1)), pl.BlockSpec(memory_space=pltpu.SMEM),
                  pl.BlockSpec((1, HD), lambda n: (0, 0)), tab, tab],
        out_specs=[pl.BlockSpec((WIN, 2 * D), lambda n: (n, 0)), cur, cur, cur, cur, acc, acc],
        out_shape=[_sds((S, 2 * D), BF16)] + [_sds((S, 256), F32)] * 4 + [_sds((1, LANES), F32)] * 2,
        compiler_params=_params())(qb, ksh, ksh, vsh, vsh, dz, o, lse, pb, pb, sinks, gq, cos2, sin2)


def _prep_kv_bwd(dka, dkb, dva, dvb, kv, gk, cos2, sin2):
    nt = S // RB
    per = RB // WIN

    def shifted(cur_ref, nxt_ref, has_next):
        return jnp.concatenate([cur_ref[WIN:RB, :], jnp.where(has_next, nxt_ref[...], 0.0)], axis=0)

    def body(dka_ref, dkb_ref, dkn_ref, dva_ref, dvb_ref, dvn_ref, x_ref, g_ref, c_ref, s_ref, o_ref, dgk_ref):
        i, j = pl.program_id(0), pl.program_id(1)

        @pl.when((i == 0) & (j == 0))
        def _():
            dgk_ref[...] = jnp.zeros_like(dgk_ref)

        has_next = i < nt - 1

        @pl.when(j == 0)
        def _():
            lane = _lane_iota((RB, LANES))
            g2, cos, sin = _g2(g_ref), c_ref[...], s_ref[...]
            dy_all = dka_ref[...] + shifted(dkb_ref, dkn_ref, has_next)
            dg_tot = jnp.zeros((1, LANES), F32)
            lo_half = _half_ones()
            for c in _pairs(CB):
                dn = _rope_bwd(dy_all[:, c], cos, sin, lane)
                dx, dg = _head_norm_bwd(dn, x_ref[:, c], g2, lo_half)
                o_ref[:, c] = dx.astype(BF16)
                dg_tot = dg_tot + dg
            dgk_ref[...] += dg_tot

        @pl.when(j == 1)
        def _():
            o_ref[...] = (dva_ref[...] + shifted(dvb_ref, dvn_ref, has_next)).astype(BF16)

    cur = pl.BlockSpec((RB, CB), lambda i, j: (i, 0))
    nxt = pl.BlockSpec((WIN, CB), lambda i, j: (jnp.minimum(per * (i + 1), S // WIN - 1), 0))
    tab = pl.BlockSpec((RB, LANES), lambda i, j: (i, 0))
    return pl.pallas_call(
        body, name="prep_kv_bwd", grid=(nt, 2),
        in_specs=[cur, cur, nxt, cur, cur, nxt, cur, pl.BlockSpec((1, HD), lambda i, j: (0, 0)), tab, tab],
        out_specs=[pl.BlockSpec((RB, CB), lambda i, j: (i, j)), pl.BlockSpec((1, LANES), lambda i, j: (0, 0))],
        out_shape=[_sds((S, 2 * CB), BF16), _sds((1, LANES), F32)],
        compiler_params=_params())(dka, dkb, dkb, dva, dvb, dvb, kv, gk, cos2, sin2)


def _out_b_loss(z, w_out, h1, tgt):
    tm = 2 * RT

    def body(z_ref, w_ref, h_ref, t_ref, dy_ref, l_ref):
        @pl.when(pl.program_id(0) == 0)
        def _():
            l_ref[...] = jnp.zeros_like(l_ref)

        e = (h_ref[...] + _dot_nn(z_ref[...], w_ref[...])) - t_ref[...]
        dy_ref[...] = e * (1.0 / D)
        l_ref[...] += jnp.sum(jnp.sum(e * e, axis=-1, keepdims=True), axis=0, keepdims=True)

    row = pl.BlockSpec((tm, D), lambda i: (i, 0))
    return pl.pallas_call(
        body, name="out_b_loss", grid=(S // tm,),
        in_specs=[row, pl.BlockSpec(w_out.shape, lambda i: (0, 0), pipeline_mode=pl.Buffered(1)), row, row],
        out_specs=[row, pl.BlockSpec((1, LANES), lambda i: (0, 0))],
        out_shape=[_sds((S, D), F32), _sds((1, LANES), F32)], compiler_params=_params())(z, w_out, h1, tgt)


def _du_a_rms_bwd(dproj, wa, x, g, dres, after):
    tm, tk = 1024, NA // 2
    nk = NA // tk

    def body(a_ref, b_ref, x_ref, g_ref, dr_ref, after_ref, dx_ref, dg_ref, acc):
        i, kk = pl.program_id(0), pl.program_id(1)

        @pl.when((i == 0) & (kk == 0))
        def _():
            dg_ref[...] = jnp.zeros_like(dg_ref)

        p = _dot_nt(a_ref[...], b_ref[...])

        @pl.when(kk == 0)
        def _():
            acc[...] = p

        @pl.when(kk == nk - 1)
        def _():
            dx, dg = _rms_bwd_core(acc[...] + p, x_ref[...], g_ref[...])
            dx_ref[...] = dr_ref[...] + dx
            dg_ref[...] += dg

    assert nk == 2
    row = pl.BlockSpec((tm, D), lambda i, kk: (i, 0))
    vec = pl.BlockSpec((1, D), lambda i, kk: (0, 0))
    return pl.pallas_call(
        body, name="du_a_rms_bwd", grid=(S // tm, nk),
        in_specs=[pl.BlockSpec((tm, tk), lambda i, kk: (i, kk)), pl.BlockSpec((D, tk), lambda i, kk: (0, kk)),
                  row, vec, row, pl.BlockSpec(after.shape, lambda i, kk: (0, 0))],
        out_specs=[row, vec], out_shape=[_sds((S, D), F32), _sds((1, D), F32)],
        scratch_shapes=[pltpu.VMEM((tm, D), F32)], compiler_params=_params())(dproj, wa, x, g, dres, after)


def _du_b_rms_bwd(dpb, w_in_b, dkv, w_kv, h1, g_b, g_kv, dy):
    tm = 2 * RT

    def body(ab_ref, wb_ref, akv_ref, wkv_ref, x_ref, gb_ref, gkv_ref, dy_ref, dh_ref, dgb_ref, dgkv_ref):
        @pl.when(pl.program_id(0) == 0)
        def _():
            dgb_ref[...] = jnp.zeros_like(dgb_ref)
            dgkv_ref[...] = jnp.zeros_like(dgkv_ref)

        x = x_ref[...]
        dx1, dg1 = _rms_bwd_core(_dot_nt(ab_ref[...], wb_ref[...]), x, gb_ref[...])
        dx2, dg2 = _rms_bwd_core(_dot_nt(akv_ref[...], wkv_ref[...]), x, gkv_ref[...])
        dh_ref[...] = dy_ref[...] + dx1 + dx2
        dgb_ref[...] += dg1
        dgkv_ref[...] += dg2

    row = lambda width: pl.BlockSpec((tm, width), lambda i: (i, 0))
    whole = lambda arr: pl.BlockSpec(arr.shape, lambda i: (0, 0), pipeline_mode=pl.Buffered(1))
    vec = pl.BlockSpec((1, D), lambda i: (0, 0))
    return pl.pallas_call(
        body, name="du_b_rms_bwd", grid=(S // tm,),
        in_specs=[row(dpb.shape[1]), whole(w_in_b), row(dkv.shape[1]), whole(w_kv), row(D), vec, vec, row(D)],
        out_specs=[row(D), vec, vec], out_shape=[_sds((S, D), F32), _sds((1, D), F32), _sds((1, D), F32)],
        compiler_params=_params())(dpb, w_in_b, dkv, w_kv, h1, g_b, g_kv, dy)


def _gather_first(w_in_a, w_out_a, w_kv, w_in_b, w_out_b, norm_a_g):
    def body(wia_ref, woa_ref, wkv_ref, wib_ref, wob_ref, ga_ref,
             wa_g, ga_g, woa_s, wkv_s, wib_s, wob_s, wa_s, st_a, st_oa, st_kv, st_ib, st_ob, load_sems, *sems):
        sources = [wia_ref.at[0], woa_ref.at[0], wkv_ref, wib_ref.at[0], wob_ref.at[0]]
        stages = [st_a, st_oa, st_kv, st_ib, st_ob]
        loads = [pltpu.make_async_copy(src, dst, load_sems.at[i]) for i, (src, dst) in enumerate(zip(sources, stages))]
        for cp in loads:
            cp.start()
        loads[0].wait()
        wa_s[...] = st_a[...].astype(BF16)

        def cast_the_rest():
            for cp, stage, out in zip(loads[1:], stages[1:], [woa_s, wkv_s, wib_s, wob_s]):
                cp.wait()
                out[...] = stage[...].astype(BF16)

        _gather_two_level([wa_s, ga_ref], [wa_g, ga_g], sems, meanwhile=cast_the_rest)

    vmem = pl.BlockSpec(memory_space=pltpu.VMEM)
    anyspec = pl.BlockSpec(memory_space=pl.ANY)
    shard = lambda w: _sds(w.shape[-2:], BF16)
    stage = lambda w: pltpu.VMEM(w.shape[-2:], F32)
    return pl.pallas_call(
        body, name="gather_first", in_specs=[anyspec] * 5 + [vmem],
        out_specs=[anyspec, anyspec, vmem, vmem, vmem, vmem],
        out_shape=[_sds((N_DEV,) + w_in_a.shape[-2:], BF16), _sds((N_DEV,) + norm_a_g.shape, F32),
                   shard(w_out_a), shard(w_kv), shard(w_in_b), shard(w_out_b)],
        scratch_shapes=[pltpu.VMEM(w_in_a.shape[-2:], BF16), stage(w_in_a), stage(w_out_a), stage(w_kv),
                        stage(w_in_b), stage(w_out_b), pltpu.SemaphoreType.DMA((5,))] + _exchange_sems(2),
        compiler_params=pltpu.CompilerParams(vmem_limit_bytes=VMEM_LIMIT, has_side_effects=True))(
            w_in_a, w_out_a, w_kv, w_in_b, w_out_b, norm_a_g)


def _pair_reduce(slots):
    n_chip = N_DEV // 2
    _, rows, cols = slots.shape

    def body(s_ref, o_ref, own_v, sib_v, send_sems, recv_sems, local_sems):
        x, y, c = lax.axis_index("x"), lax.axis_index("y"), lax.axis_index("c")
        copies = []
        for j in range(n_chip):
            own = pltpu.make_async_copy(s_ref.at[2 * j + c], own_v.at[j], local_sems.at[j])
            give = pltpu.make_async_remote_copy(
                src_ref=s_ref.at[2 * j + 1 - c], dst_ref=sib_v.at[j], send_sem=send_sems.at[j],
                recv_sem=recv_sems.at[j], device_id=(x, y, 1 - c), device_id_type=pl.DeviceIdType.MESH)
            own.start()
            give.start()
            copies.append((own, give))
        for j, (own, give) in enumerate(copies):
            own.wait()
            give.wait()
            o_ref[j] = (own_v[j].astype(F32) + sib_v[j].astype(F32)).astype(BF16)

    half = _sds((n_chip, rows, cols), slots.dtype)
    return pl.pallas_call(
        body, name="pair_reduce", in_specs=[pl.BlockSpec(memory_space=pl.ANY)],
        out_specs=pl.BlockSpec(memory_space=pltpu.VMEM), out_shape=half,
        scratch_shapes=[pltpu.VMEM(half.shape, half.dtype), pltpu.VMEM(half.shape, half.dtype),
                        pltpu.SemaphoreType.DMA((n_chip,)), pltpu.SemaphoreType.DMA((n_chip,)),
                        pltpu.SemaphoreType.DMA((n_chip,))],
        compiler_params=pltpu.CompilerParams(vmem_limit_bytes=VMEM_LIMIT, has_side_effects=True))(slots)


def _padded_col(c):
    if c < RAW_F:
        return c
    return FOFF + (c - RAW_F) if c < RAW_G else GOFF + (c - RAW_G)


def _shard_pieces():
    width = NA_RAW // N_DEV
    pieces = []
    for d in range(N_DEV):
        cuts = [width * d] + [c for c in (RAW_F, RAW_G) if width * d < c < width * (d + 1)] + [width * (d + 1)]
        for lo, hi in zip(cuts[:-1], cuts[1:]):
            pieces.append((d, lo - width * d, _padded_col(lo), hi - lo))
    return pieces


def _unshard_wa(wa_g):
    def body(w_ref, o_ref):
        o_ref[:, FOFF + N_HEADS:NA] = jnp.zeros((TM, NA - FOFF - N_HEADS), BF16)
        for d, src, dst, width in _shard_pieces():
            o_ref[:, dst:dst + width] = w_ref[d, :, src:src + width]

    return pl.pallas_call(
        body, name="unshard_wa", grid=(D // TM,),
        in_specs=[pl.BlockSpec((N_DEV, TM, NA_RAW // N_DEV), lambda i: (0, i, 0))],
        out_specs=pl.BlockSpec((TM, NA), lambda i: (i, 0)), out_shape=_sds((D, NA), BF16),
        compiler_params=_params())(wa_g)


def _reshard_dwa(dwa):
    def body(g_ref, o_ref):
        for d, src, dst, width in _shard_pieces():
            o_ref[d, :, src:src + width] = g_ref[:, dst:dst + width]

    return pl.pallas_call(
        body, name="reshard_dwa", grid=(D // TM,), in_specs=[pl.BlockSpec((TM, NA), lambda i: (i, 0))],
        out_specs=pl.BlockSpec((N_DEV, TM, NA_RAW // N_DEV), lambda i: (0, i, 0)),
        out_shape=_sds((N_DEV, D, NA_RAW // N_DEV), dwa.dtype), compiler_params=_params())(dwa)


CHIP_FLIPS = (2, 4, 6)


def _chip_exchange_start(partial):
    n = len(CHIP_FLIPS)

    def body(p_ref, land_ref, *rest):
        sends, recvs, token = rest[:n], rest[n:2 * n], rest[2 * n + 2]
        x, y, c = lax.axis_index("x"), lax.axis_index("y"), lax.axis_index("c")
        me = 4 * x + 2 * y + c
        for idx, k in enumerate(CHIP_FLIPS):
            pltpu.make_async_remote_copy(
                src_ref=p_ref.at[(me ^ k) >> 1], dst_ref=land_ref.at[me >> 1], send_sem=sends[idx],
                recv_sem=recvs[idx], device_id=(x ^ ((k >> 2) & 1), y ^ ((k >> 1) & 1), c),
                device_id_type=pl.DeviceIdType.MESH).start()
        token[...] = jnp.zeros_like(token)

    hbm = pl.BlockSpec(memory_space=pltpu.HBM)
    sem = pl.BlockSpec(memory_space=pltpu.SEMAPHORE)
    buf = pltpu.HBM(partial.shape, partial.dtype)
    return pl.pallas_call(
        body, name="chip_exchange_start",
        out_shape=(pltpu.SemaphoreType.DMA(()),) * (2 * n) + (buf, buf, _sds((8, LANES), F32)),
        in_specs=(hbm, hbm), out_specs=(sem,) * (2 * n) + (hbm, hbm, pl.BlockSpec(memory_space=pltpu.VMEM)),
        input_output_aliases={0: 2 * n, 1: 2 * n + 1},
        compiler_params=pltpu.CompilerParams(has_side_effects=pltpu.SideEffectType.DATAFLOW_SIDE_EFFECTING))(
            pltpu.with_memory_space_constraint(partial, pltpu.HBM),
            pltpu.with_memory_space_constraint(lax.empty(partial.shape, partial.dtype), pltpu.HBM))


def _chip_exchange_wait(started, after):
    n = len(CHIP_FLIPS)
    sems, (p_thru, land_thru) = started[:2 * n], started[2 * n:2 * n + 2]

    def body(p_ref, land_ref, *rest):
        sends, recvs = rest[:n], rest[n:2 * n]
        x, y, c = lax.axis_index("x"), lax.axis_index("y"), lax.axis_index("c")
        me = 4 * x + 2 * y + c
        for idx, k in enumerate(CHIP_FLIPS):
            copy = pltpu.make_async_remote_copy(
                src_ref=p_ref.at[(me ^ k) >> 1], dst_ref=land_ref.at[(me ^ k) >> 1], send_sem=sends[idx],
                recv_sem=recvs[idx], device_id=(x ^ ((k >> 2) & 1), y ^ ((k >> 1) & 1), c),
                device_id_type=pl.DeviceIdType.MESH)
            copy.wait_send()
            copy.wait_recv()

    hbm = pl.BlockSpec(memory_space=pltpu.HBM)
    sem = pl.BlockSpec(memory_space=pltpu.SEMAPHORE)
    buf = pltpu.HBM(p_thru.shape, p_thru.dtype)
    return pl.pallas_call(
        body, name="chip_exchange_wait", out_shape=(buf, buf),
        in_specs=(hbm, hbm) + (sem,) * (2 * n) + (pl.BlockSpec(memory_space=pl.ANY),), out_specs=(hbm, hbm),
        input_output_aliases={0: 0, 1: 1},
        compiler_params=pltpu.CompilerParams(has_side_effects=pltpu.SideEffectType.DATAFLOW_SIDE_EFFECTING))(
            p_thru, land_thru, *sems, after)


def _gather_slab(slab, after):
    def body(s_ref, after_ref, o_ref, *sems):
        _exchange_ops(["gather_rows"], [s_ref], [o_ref], sems, True, True)

    anyspec = pl.BlockSpec(memory_space=pl.ANY)
    return pl.pallas_call(
        body, name="gather_slab", in_specs=[anyspec, anyspec], out_specs=anyspec,
        out_shape=_sds((N_DEV,) + slab.shape, slab.dtype), scratch_shapes=_exchange_sems(1),
        compiler_params=pltpu.CompilerParams(has_side_effects=True))(slab, after)


def _adamw(w, g, m, v):
    m = ADAM_B1 * m + (1.0 - ADAM_B1) * g
    v = ADAM_B2 * v + (1.0 - ADAM_B2) * (g * g)
    m_hat = m / (1.0 - ADAM_B1 ** ADAM_STEP)
    v_hat = v / (1.0 - ADAM_B2 ** ADAM_STEP)
    delta = -ADAM_LR * (m_hat / (jnp.sqrt(v_hat) + ADAM_EPS) + ADAM_WD * w)
    return delta, m, v


def _sum_adamw(recv, w, m, v, name, after=None):
    lead = w.ndim - 2
    rows, cols = w.shape[-2:]
    tr = 256 if rows % 256 == 0 else 128
    slabs = list(recv) if isinstance(recv, tuple) else [recv]
    n_slots = slabs[0].shape[0]
    extra = [] if after is None else [after]

    def body(*refs):
        r_ref, own_ref = refs[0], refs[len(slabs) - 1]
        w_ref, m_ref, v_ref, g_ref, d_ref, nm_ref, nv_ref = refs[len(slabs) + len(extra):]
        chip = (4 * lax.axis_index("x") + 2 * lax.axis_index("y") + lax.axis_index("c")) >> 1
        g = None
        for slot in range(n_slots):
            part = r_ref[slot]
            if len(slabs) == 2:
                part = jnp.where(chip == slot, own_ref[slot], part)
            g = part.astype(F32) if g is None else g + part.astype(F32)
        g_ref[...] = g
        d_ref[...], nm_ref[...], nv_ref[...] = _adamw(w_ref[...], g, m_ref[...], v_ref[...])

    blk = pl.BlockSpec((None,) * lead + (tr, cols), lambda i: (0,) * lead + (i, 0))
    slots = pl.BlockSpec((n_slots, tr, cols), lambda i: (0, i, 0))
    return pl.pallas_call(
        body, name=name, grid=(rows // tr,),
        in_specs=[slots] * len(slabs) + [pl.BlockSpec(a.shape, lambda i: (0, 0)) for a in extra] + [blk, blk, blk],
        out_specs=[blk] * 4, out_shape=[_sds(w.shape, F32)] * 4, compiler_params=_params())(*slabs, *extra, w, m, v)


def _entry_view(a):
    _, rows, cols = a.shape
    return jnp.transpose(a, (2, 0, 1)).reshape(cols, rows // LANES, LANES)


def _from_entry_view(a):
    cols, groups, lanes = a.shape
    return jnp.transpose(a, (1, 2, 0)).reshape(1, groups * lanes, cols)


def _sum_adamw_entry_view(landing, own, w, m, v, name):
    n_slots, rows, cols = landing.shape
    groups = rows // LANES
    cols_pad = -(-cols // LANES) * LANES

    def body(r_ref, own_ref, w_ref, m_ref, v_ref, g_ref, d_ref, nm_ref, nv_ref, pad_ref, gt_ref):
        j = pl.program_id(0)
        chip = (4 * lax.axis_index("x") + 2 * lax.axis_index("y") + lax.axis_index("c")) >> 1
        pad_ref[:, cols_pad - LANES:] = jnp.zeros((LANES, LANES), F32)
        for r0 in range(0, LANES, 32):
            g = None
            for slot in range(n_slots):
                part = jnp.where(chip == slot, own_ref[slot, r0:r0 + 32], r_ref[slot, r0:r0 + 32])
                g = part.astype(F32) if g is None else g + part.astype(F32)
            pad_ref[r0:r0 + 32, :cols] = g
        for c0 in range(0, cols_pad, LANES):
            gt_ref[c0:c0 + LANES, :] = pad_ref[:, c0:c0 + LANES].T
        def update_plane(plane):
            for c0 in range(0, cols, 64):
                n = min(64, cols - c0)
                at = pl.ds(c0 * groups + plane, n, stride=groups)
                gt = gt_ref[c0:c0 + n, :]
                g_ref[at, :] = gt
                d_ref[at, :], nm_ref[at, :], nv_ref[at, :] = _adamw(w_ref[at, :], gt, m_ref[at, :], v_ref[at, :])

        for plane in range(groups):
            pl.when(j == plane)(lambda plane=plane: update_plane(plane))

    flat = lambda a: _entry_view(a).reshape(cols * groups, LANES)
    whole = pl.BlockSpec((cols * groups, LANES), lambda j: (0, 0))
    slots = pl.BlockSpec((n_slots, LANES, cols), lambda j: (0, j, 0))
    outs = pl.pallas_call(
        body, name=name, grid=(groups,), in_specs=[slots, slots, whole, whole, whole], out_specs=[whole] * 4,
        out_shape=[_sds((cols * groups, LANES), F32)] * 4,
        scratch_shapes=[pltpu.VMEM((LANES, cols_pad), F32), pltpu.VMEM((cols_pad, LANES), F32)],
        compiler_params=_params())(landing, own, flat(w), flat(m), flat(v))
    return [_from_entry_view(o.reshape(cols, groups, LANES)) for o in outs]


SLAB_ROWS = 16
SLOT = {"kv_norm_g": (8, 0, D), "norm_b_g": (9, 0, D), "b_forget": (10, 0, 16), "qnorm_a_g": (10, 128, HD),
        "knorm_a_g": (10, 256, HD), "knorm_b_g": (10, 384, HD), "qnorm_b_g": (10, 512, HD), "sinks": (10, 640, 16)}
SMALL = ["norm_a_g", "b_forget", "qnorm_a_g", "knorm_a_g", "kv_norm_g", "knorm_b_g", "norm_b_g", "qnorm_b_g", "sinks"]


LOSS_ROW = 11


def _pack_small(dg_a, dg_kv, dg_b, db_f, dgq_a, dgk_a, dgk_b, dgq_b, dsinks, lsum):
    def fold(ref):
        return ref[:, 0:HD] + ref[:, HD:2 * HD]

    def body(dga_ref, dgkv_ref, dgb_ref, dbf_ref, dgqa_ref, dgka_ref, dgkb_ref, dgqb_ref, dsk_ref, ls_ref, slab_ref):
        slab_ref[...] = jnp.zeros_like(slab_ref)
        for r in range(N_DEV):
            slab_ref[r:r + 1, 0:LANES] = dga_ref[:, LANES * r:LANES * (r + 1)]
        slab_ref[8:9, :] = dgkv_ref[...]
        slab_ref[9:10, :] = dgb_ref[...]
        slab_ref[10:11, 0:LANES] = dbf_ref[...]
        slab_ref[10:11, 128:128 + HD] = fold(dgqa_ref)
        slab_ref[10:11, 256:256 + HD] = fold(dgka_ref)
        slab_ref[10:11, 384:384 + HD] = fold(dgkb_ref)
        slab_ref[10:11, 512:512 + HD] = fold(dgqb_ref)
        slab_ref[10:11, 640:640 + LANES] = dsk_ref[...]
        slab_ref[LOSS_ROW:LOSS_ROW + 1, 0:LANES] = ls_ref[...]

    return pl.pallas_call(body, name="pack_small", out_shape=_sds((SLAB_ROWS, D), F32), compiler_params=_params())(
        dg_a, dg_kv, dg_b, db_f, dgq_a, dgk_a, dgk_b, dgq_b, dsinks, lsum)


def _small_adamw(recv, ws, ms, vs):
    k = len(SMALL)

    def body(*refs):
        r_ref = refs[0]
        w_refs, m_refs, v_refs = refs[1:1 + k], refs[1 + k:1 + 2 * k], refs[1 + 2 * k:1 + 3 * k]
        outs = refs[1 + 3 * k:1 + 7 * k]
        loss_ref, tot = refs[1 + 7 * k], refs[2 + 7 * k]
        g = r_ref[0]
        for dev in range(1, N_DEV):
            g = g + r_ref[dev]
        tot[...] = g
        loss_ref[...] = tot[LOSS_ROW:LOSS_ROW + 1, 0:LANES] * (0.5 / D)
        me = 4 * lax.axis_index("x") + 2 * lax.axis_index("y") + lax.axis_index("c")
        for p, name in enumerate(SMALL):
            if name == "norm_a_g":
                mine = lax.broadcasted_iota(jnp.int32, (N_DEV, LANES), 0) == me
                gp = jnp.sum(jnp.where(mine, tot[0:N_DEV, 0:LANES], 0.0), axis=0, keepdims=True)
            else:
                row, lo, width = SLOT[name]
                gp = tot[row:row + 1, lo:lo + width]
            d, nm, nv = _adamw(w_refs[p][...], gp, m_refs[p][...], v_refs[p][...])
            outs[p][...] = gp
            outs[k + p][...] = d
            outs[2 * k + p][...] = nm
            outs[3 * k + p][...] = nv

    shapes = [_sds(w.shape, F32) for w in ws]
    return pl.pallas_call(body, name="small_adamw", out_shape=shapes * 4 + [_sds((1, LANES), F32)],
                          scratch_shapes=[pltpu.VMEM((SLAB_ROWS, D), F32)],
                          compiler_params=_params())(recv, *ws, *ms, *vs)


def _rope_tables(positions):
    inv_freq = jnp.power(jnp.float32(ROPE_THETA), -jnp.arange(0, ROT, 2, dtype=F32) / ROT)
    ang = positions.astype(F32)[:, None] * inv_freq[None, :]
    cos, sin = jnp.cos(ang), jnp.sin(ang)
    c64 = jnp.concatenate([cos, cos, jnp.ones((S, HD - ROT), F32)], axis=-1)
    s64 = jnp.concatenate([-sin, sin, jnp.zeros((S, HD - ROT), F32)], axis=-1)
    return jnp.tile(c64, (1, 2)), jnp.tile(s64, (1, 2))


def _local_step(x, tgt, positions, g_a, wa, b_forget, gq_a, gk_a, g_kv, gk_b, g_b, gq_b, sinks,
                woa_s, wkv_s, wib_s, wob_s, adamw_others):
    nq = S // TQ
    cos2, sin2 = _rope_tables(positions)
    b_pad = jnp.pad(b_forget, ((0, 0), (0, LANES - N_HEADS)))

    u_a, proj, qn, kn, vb, ccol, cbc = _head_a(x, g_a, wa, gq_a, gk_a, b_pad)
    crow = ccol[:, :N_HEADS].T.reshape(N_HEADS, nq, 1, TQ)
    o_a, z_a, lse_a, woa_g, wkv_g, w_in_b, wob_g = _fox_fwd(
        qn, kn, vb, proj, crow, cbc,
        rider=[("gather_rows", woa_s), ("gather_rows", wkv_s), ("gather_cols", wib_s), ("gather_rows", wob_s)])
    w_out_a, w_kv, w_out_b = woa_g.reshape(D, D), wkv_g.reshape(D, 512), wob_g.reshape(D, D)
    h1, u_kv, u_b, kv, pb, qb, ksh, vsh = _head_b(x, z_a, w_out_a, g_kv, g_b, w_kv, w_in_b, gq_b, gk_b, cos2, sin2)
    sinks1 = sinks.reshape(N_HEADS)
    o_b, z_b, lse_b = _swa_fwd(qb, ksh, vsh, pb, sinks1)
    dy, lsum = _out_b_loss(z_b, w_out_b, h1, tgt)
    dw_out_b = _mm(z_b, dy, "tn", 512, 512, S, out_dtype=BF16, name="mm_dw_out_b")
    dz_b = _mm(dy, w_out_b, "nt", 1024, 512, D, name="mm_dz_b")
    dpb, dka, dkb, dva, dvb, dsinks, dgq_b = _swa_bwd(qb, ksh, vsh, dz_b, o_b, lse_b, pb, sinks1, gq_b, cos2, sin2)
    dkv, dgk_b = _prep_kv_bwd(dka, dkb, dva, dvb, kv, gk_b, cos2, sin2)
    dw_in_b = _mm(u_b, dpb, "tn", 512, 512, S, out_dtype=BF16, name="mm_dw_in_b")
    dw_kv = _mm(u_kv, dkv, "tn", 512, 512, S, out_dtype=BF16, name="mm_dw_kv")
    dh1, dg_b, dg_kv = _du_b_rms_bwd(dpb, w_in_b, dkv, w_kv, h1, g_b, g_kv, dy)
    dw_out_a = _mm(z_a, dh1, "tn", 512, 512, S, out_dtype=BF16, name="mm_dw_out_a")
    do_a, dgate_a, delta_a = _fox_bwd_pre(dh1, w_out_a, proj, o_a)
    dk_a, dv_a, dcs, dq_a, drow, r_wob, r_wib, r_wkv, r_woa = _fox_bwd(
        qn, kn, vb, do_a, lse_a, delta_a, crow, cbc,
        rider=[("a2a_rows", dw_out_b), ("a2a_cols", dw_in_b), ("a2a_rows", dw_kv), ("a2a_rows", dw_out_a)])
    drow_col = jnp.pad(drow.reshape(N_HEADS, S).T, ((0, 0), (0, LANES - N_HEADS)))
    dproj, dgq_a, dgk_a, db_f = _prep_a_bwd(dq_a, dk_a, dv_a, dgate_a, drow_col, dcs, proj, b_pad, gq_a, gk_a)
    dwa = _mm(u_a, dproj, "tn", 1024, 256, S, out_dtype=BF16, name="mm_dw_in_a")
    partial = _pair_reduce(_reshard_dwa(dwa))
    started = _chip_exchange_start(partial)
    dx, dg_a = _du_a_rms_bwd(dproj, wa, x, g_a, dh1, after=started[-1])
    others = adamw_others(dict(w_out_a=r_woa, w_kv=r_wkv, w_in_b=r_wib, w_out_b=r_wob), dg_a)
    partial, landed = _chip_exchange_wait(started, others["w_out_b"][0])
    slab = _pack_small(dg_a, dg_kv, dg_b, db_f, dgq_a, dgk_a, dgk_b, dgq_b, dsinks, lsum)
    return dx, (landed, partial), others, _gather_slab(slab, landed)


def kernel(x, positions, norm_a_g, w_in_a, b_forget, qnorm_a_g, knorm_a_g, w_out_a, kv_norm_g, w_kv, knorm_b_g, norm_b_g, w_in_b, qnorm_b_g, sinks, w_out_b, loss_target, m_norm_a_g, m_w_in_a, m_b_forget, m_qnorm_a_g, m_knorm_a_g, m_w_out_a, m_kv_norm_g, m_w_kv, m_knorm_b_g, m_norm_b_g, m_w_in_b, m_qnorm_b_g, m_sinks, m_w_out_b, v_norm_a_g, v_w_in_a, v_b_forget, v_qnorm_a_g, v_knorm_a_g, v_w_out_a, v_kv_norm_g, v_w_kv, v_knorm_b_g, v_norm_b_g, v_w_in_b, v_qnorm_b_g, v_sinks, v_w_out_b):
    wa_g, ga_g, woa_s, wkv_s, wib_s, wob_s = _gather_first(w_in_a, w_out_a, w_kv, w_in_b, w_out_b, norm_a_g)
    state = dict(w_in_a=(w_in_a, m_w_in_a, v_w_in_a), w_out_a=(w_out_a, m_w_out_a, v_w_out_a),
                 w_kv=(w_kv, m_w_kv, v_w_kv), w_in_b=(w_in_b, m_w_in_b, v_w_in_b),
                 w_out_b=(w_out_b, m_w_out_b, v_w_out_b))

    def adamw_others(landed, after):
        return {n: _sum_adamw(r, *state[n], "adamw_" + n, after=after) for n, r in landed.items()}

    dx, r_wa, big, slab_g = _local_step(
        x[0], loss_target[0], positions, ga_g.reshape(1, D), _unshard_wa(wa_g), b_forget, qnorm_a_g, knorm_a_g,
        kv_norm_g.reshape(1, D), knorm_b_g.reshape(1, HD), norm_b_g, qnorm_b_g, sinks, woa_s, wkv_s, wib_s, wob_s,
        adamw_others)
    big["w_in_a"] = _sum_adamw_entry_view(*r_wa, *state["w_in_a"], "adamw_w_in_a")

    r2 = lambda a: a.reshape(1, -1)
    small_w = dict(norm_a_g=norm_a_g, b_forget=b_forget, qnorm_a_g=qnorm_a_g, knorm_a_g=knorm_a_g,
                   kv_norm_g=kv_norm_g, knorm_b_g=knorm_b_g, norm_b_g=norm_b_g, qnorm_b_g=qnorm_b_g, sinks=sinks)
    small_m = dict(norm_a_g=m_norm_a_g, b_forget=m_b_forget, qnorm_a_g=m_qnorm_a_g, knorm_a_g=m_knorm_a_g,
                   kv_norm_g=m_kv_norm_g, knorm_b_g=m_knorm_b_g, norm_b_g=m_norm_b_g, qnorm_b_g=m_qnorm_b_g,
                   sinks=m_sinks)
    small_v = dict(norm_a_g=v_norm_a_g, b_forget=v_b_forget, qnorm_a_g=v_qnorm_a_g, knorm_a_g=v_knorm_a_g,
                   kv_norm_g=v_kv_norm_g, knorm_b_g=v_knorm_b_g, norm_b_g=v_norm_b_g, qnorm_b_g=v_qnorm_b_g,
                   sinks=v_sinks)
    res = _small_adamw(slab_g, [r2(small_w[n]) for n in SMALL], [r2(small_m[n]) for n in SMALL],
                       [r2(small_v[n]) for n in SMALL])
    k = len(SMALL)
    small = {n: [res[q * k + p].reshape(small_w[n].shape) for q in range(4)] for p, n in enumerate(SMALL)}
    loss = res[4 * k][0, 0]

    order = ["norm_a_g", "w_in_a", "b_forget", "qnorm_a_g", "knorm_a_g", "w_out_a", "kv_norm_g", "w_kv",
             "knorm_b_g", "norm_b_g", "w_in_b", "qnorm_b_g", "sinks", "w_out_b"]

    def leaf(n, q):
        return big[n][q] if n in big else small[n][q]

    outs = [loss, dx[None]]
    for q in range(4):
        outs.extend(leaf(n, q) for n in order)
    return tuple(outs)
```

```python
import jax
import jax.numpy as jnp
from jax import lax
from jax.experimental import pallas as pl
from jax.experimental.pallas import tpu as pltpu

F32, BF16 = jnp.float32, jnp.bfloat16

S = 2048
D = 1024
HD = 64
N_HEADS = 16
N_DEV = 8
NA = 4352
GOFF = 3072
FOFF = 4096
RAW_F = 3072
RAW_G = RAW_F + N_HEADS
NA_RAW = 4112
EPS = 1e-6
QSCALE = 0.125
ROPE_THETA = 500000.0
ROT = 16
WIN = 128
TQ = 256
TK = 256
KS = TQ // 2
HPS = 8
HW = HPS * HD
TM = 256
RT = 256
RB = 512
CB = 256
LANES = 128

ADAM_LR, ADAM_B1, ADAM_B2, ADAM_EPS, ADAM_WD, ADAM_STEP = 0.001, 0.9, 0.999, 1e-08, 0.01, 10

VMEM_LIMIT = 56 * 1024 * 1024


def _params():
    return pltpu.CompilerParams(vmem_limit_bytes=VMEM_LIMIT)


def _sds(shape, dtype):
    return jax.ShapeDtypeStruct(shape, dtype)


def _dot_nt(a, b):
    return lax.dot_general(a, b, (((1,), (1,)), ((), ())), preferred_element_type=F32)


def _dot_tn(a, b):
    return lax.dot_general(a, b, (((0,), (0,)), ((), ())), preferred_element_type=F32)


def _dot_nn(a, b):
    return lax.dot_general(a, b, (((1,), (0,)), ((), ())), preferred_element_type=F32)


def _sigmoid(g):
    return 1.0 / (1.0 + jnp.exp(-g))


def _lane_iota(shape):
    return lax.broadcasted_iota(jnp.int32, shape, len(shape) - 1)


def _flips(kind):
    return (2, 4, 6) if kind == "a2a_chips" else tuple(range(1, N_DEV))


def _send_view(kind, ref, dev):
    if kind in ("gather_rows", "gather_cols"):
        return ref
    if kind == "a2a_slots":
        return ref.at[dev]
    if kind == "a2a_chips":
        return ref.at[dev >> 1]
    if kind == "a2a_rows":
        rows = ref.shape[0] // N_DEV
        return ref.at[pl.ds(pl.multiple_of(dev * rows, rows), rows)]
    cols = ref.shape[1] // N_DEV
    return ref.at[:, pl.ds(pl.multiple_of(dev * cols, cols), cols)]


def _land_view(kind, ref, dev):
    if kind == "gather_cols":
        cols = ref.shape[1] // N_DEV
        return ref.at[:, pl.ds(pl.multiple_of(dev * cols, cols), cols)]
    if kind == "a2a_chips":
        return ref.at[dev >> 1]
    return ref.at[dev]


def _landing_sds(kind, arr):
    if kind == "gather_rows":
        return _sds((N_DEV,) + arr.shape, arr.dtype)
    if kind == "gather_cols":
        return _sds((arr.shape[0], N_DEV * arr.shape[1]), arr.dtype)
    if kind == "a2a_rows":
        return _sds((N_DEV, arr.shape[0] // N_DEV, arr.shape[1]), arr.dtype)
    if kind == "a2a_cols":
        return _sds((N_DEV, arr.shape[0], arr.shape[1] // N_DEV), arr.dtype)
    return _sds(arr.shape, arr.dtype)


def _exchange_sems(n_parts):
    n = n_parts * (N_DEV - 1)
    return [pltpu.SemaphoreType.DMA((n,)), pltpu.SemaphoreType.DMA((n,)), pltpu.SemaphoreType.DMA((n_parts,))]


def _exchange_ops(kinds, srcs, dsts, sems, start, wait):
    send_sems, recv_sems, local_sems = sems
    x, y, c = lax.axis_index("x"), lax.axis_index("y"), lax.axis_index("c")
    me = 4 * x + 2 * y + c

    def local(a):
        return pltpu.make_async_copy(_send_view(kinds[a], srcs[a], me), _land_view(kinds[a], dsts[a], me),
                                     local_sems.at[a])

    def remote(a, k, landing_dev):
        peer = (x ^ ((k >> 2) & 1), y ^ ((k >> 1) & 1), c ^ (k & 1))
        sem = a * (N_DEV - 1) + k - 1
        return pltpu.make_async_remote_copy(
            src_ref=_send_view(kinds[a], srcs[a], me ^ k), dst_ref=_land_view(kinds[a], dsts[a], landing_dev),
            send_sem=send_sems.at[sem], recv_sem=recv_sems.at[sem], device_id=peer,
            device_id_type=pl.DeviceIdType.MESH)

    pairs = [(a, k) for k in range(1, N_DEV) for a in range(len(kinds)) if k in _flips(kinds[a])]
    if start:
        for a in range(len(kinds)):
            local(a).start()
        for a, k in pairs:
            remote(a, k, me).start()
    if wait:
        for a, k in pairs:
            remote(a, k, me ^ k).wait_recv()
            remote(a, k, me).wait_send()
        for a in range(len(kinds)):
            local(a).wait()


def _gather_two_level(srcs, dsts, sems, meanwhile=None):
    send_sems, recv_sems, local_sems = sems
    x, y, c = lax.axis_index("x"), lax.axis_index("y"), lax.axis_index("c")
    me, sibling = (x, y, c), (x, y, 1 - c)
    chips = [(1 - x, y), (x, 1 - y), (1 - x, 1 - y)]

    def slot(ref, dev):
        return ref.at[4 * dev[0] + 2 * dev[1] + dev[2]]

    def copy(a, k, block, to, src=None):
        return pltpu.make_async_remote_copy(
            src_ref=slot(dsts[a], block) if src is None else src, dst_ref=slot(dsts[a], block),
            send_sem=send_sems.at[a * (N_DEV - 1) + k], recv_sem=recv_sems.at[a * (N_DEV - 1) + k],
            device_id=to, device_id_type=pl.DeviceIdType.MESH)

    parts = range(len(srcs))
    mine = [pltpu.make_async_copy(srcs[a], slot(dsts[a], me), local_sems.at[a]) for a in parts]
    first = [copy(a, 0, me, sibling, src=srcs[a]) for a in parts]
    first += [copy(a, 1 + j, me, (*chip, c), src=srcs[a]) for j, chip in enumerate(chips) for a in parts]
    for cp in mine + first:
        cp.start()
    if meanwhile is not None:
        meanwhile()
    passed = []
    for j, chip in enumerate(chips):
        for a in parts:
            copy(a, 1 + j, (*chip, c), me).wait_recv()
            fwd = copy(a, 4 + j, (*chip, c), sibling)
            fwd.start()
            passed.append(fwd)
    for a in parts:
        copy(a, 0, sibling, me).wait_recv()
        for j, chip in enumerate(chips):
            copy(a, 4 + j, (*chip, 1 - c), me).wait_recv()
    for cp in first + passed:
        cp.wait_send()
    for cp in mine:
        cp.wait()


def _call(body, *, name, args, in_specs, out_specs, out_shape, grid=(), scratch_shapes=(), aliases=None, rider=()):
    n_in, n_out, n_scr, n_r = len(in_specs), len(out_specs), len(scratch_shapes), len(rider)
    kinds = [kind for kind, _ in rider]

    def kernel_body(*refs):
        c_in, r_in = refs[:n_in], refs[n_in:n_in + n_r]
        c_out = refs[n_in + n_r:n_in + n_r + n_out]
        r_out = refs[n_in + n_r + n_out:n_in + 2 * n_r + n_out]
        rest = refs[n_in + 2 * n_r + n_out:]
        c_scr, sems = rest[:n_scr], rest[n_scr:]
        if n_r:
            assert grid, "a rider needs a gridded call"
            ids = [pl.program_id(ax) for ax in range(len(grid))]
            first, last = ids[0] == 0, ids[0] == grid[0] - 1
            for pid, size in zip(ids[1:], grid[1:]):
                first = first & (pid == 0)
                last = last & (pid == size - 1)
            pl.when(first)(lambda: _exchange_ops(kinds, r_in, r_out, sems, True, False))
        body(*c_in, *c_out, *c_scr)
        if n_r:
            pl.when(last)(lambda: _exchange_ops(kinds, r_in, r_out, sems, False, True))

    anyspec = pl.BlockSpec(memory_space=pl.ANY)
    params = pltpu.CompilerParams(vmem_limit_bytes=VMEM_LIMIT, has_side_effects=bool(n_r))
    outs = pl.pallas_call(
        kernel_body, name=name, grid=grid, in_specs=list(in_specs) + [anyspec] * n_r,
        out_specs=list(out_specs) + [anyspec] * n_r,
        out_shape=list(out_shape) + [_landing_sds(kind, arr) for kind, arr in rider],
        scratch_shapes=list(scratch_shapes) + (_exchange_sems(n_r) if n_r else []),
        input_output_aliases=aliases or {}, compiler_params=params)(*args, *[arr for _, arr in rider])
    return list(outs)


def _mm(a, b, mode, tm, tn, tk, out_dtype=F32, add=None, name="mm", rider=()):
    if mode == "nn":
        (m, k), n = a.shape, b.shape[1]
        a_spec = pl.BlockSpec((tm, tk), lambda i, j, kk: (i, kk))
        b_spec = pl.BlockSpec((tk, tn), lambda i, j, kk: (kk, j))
        dot = _dot_nn
    elif mode == "nt":
        (m, k), n = a.shape, b.shape[0]
        a_spec = pl.BlockSpec((tm, tk), lambda i, j, kk: (i, kk))
        b_spec = pl.BlockSpec((tn, tk), lambda i, j, kk: (j, kk))
        dot = _dot_nt
    else:
        (k, m), n = a.shape, b.shape[1]
        a_spec = pl.BlockSpec((tk, tm), lambda i, j, kk: (kk, i))
        b_spec = pl.BlockSpec((tk, tn), lambda i, j, kk: (kk, j))
        dot = _dot_tn
    assert m % tm == 0 and n % tn == 0 and k % tk == 0, (m, n, k, tm, tn, tk)
    nk = k // tk
    has_add = add is not None

    def body(*refs):
        if has_add:
            a_ref, b_ref, add_ref, o_ref, acc = refs
        else:
            a_ref, b_ref, o_ref, acc = refs
        p = dot(a_ref[...].astype(BF16), b_ref[...].astype(BF16))

        def finish(total):
            if has_add:
                total = add_ref[...] + total
            o_ref[...] = total.astype(out_dtype)

        if nk == 1:
            finish(p)
        else:
            kk = pl.program_id(2)

            @pl.when(kk == 0)
            def _():
                acc[...] = p

            @pl.when(kk > 0)
            def _():
                acc[...] += p

            @pl.when(kk == nk - 1)
            def _():
                finish(acc[...])

    in_specs = [a_spec, b_spec]
    args = [a, b]
    if has_add:
        in_specs.append(pl.BlockSpec((tm, tn), lambda i, j, kk: (i, j)))
        args.append(add)
    acc_shape = (tm, tn) if nk > 1 else (8, LANES)
    outs = _call(body, name=name, args=args, grid=(m // tm, n // tn, nk), in_specs=in_specs,
                 out_specs=[pl.BlockSpec((tm, tn), lambda i, j, kk: (i, j))], out_shape=[_sds((m, n), out_dtype)],
                 scratch_shapes=[pltpu.VMEM(acc_shape, F32)], rider=rider)
    return outs if rider else outs[0]


def _rms_rinv(x):
    return lax.rsqrt(jnp.mean(x * x, axis=-1, keepdims=True) + EPS)


def _rms_bwd_core(du, x, g):
    r = _rms_rinv(x)
    dug = du * g
    dx = r * (dug - x * ((r * r) * jnp.mean(dug * x, axis=-1, keepdims=True)))
    dg = jnp.sum(du * (x * r), axis=0, keepdims=True)
    return dx, dg


def _half_ones():
    r = lax.broadcasted_iota(jnp.int32, (LANES, LANES), 0)
    c = lax.broadcasted_iota(jnp.int32, (LANES, LANES), 1)
    return ((r < HD) == (c < HD)).astype(BF16)


def _half_sum(v, lo_half):
    if lo_half.dtype == jnp.bool_:
        s0 = jnp.sum(jnp.where(lo_half, v, 0.0), axis=-1, keepdims=True)
        s1 = jnp.sum(jnp.where(lo_half, 0.0, v), axis=-1, keepdims=True)
        return jnp.where(lo_half, s0, s1)
    hi = v.astype(BF16)
    lo = (v - hi.astype(F32)).astype(BF16)
    return _dot_nn(hi, lo_half) + _dot_nn(lo, lo_half)


def _head_rinv(x, lo_half):
    return lax.rsqrt(_half_sum(x * x, lo_half) * (1.0 / HD) + EPS)


def _head_norm_bwd(dn, x, g, lo_half):
    r = _head_rinv(x, lo_half)
    dng = dn * g
    dx = r * (dng - x * ((r * r) * (_half_sum(dng * x, lo_half) * (1.0 / HD))))
    dg = jnp.sum(dn * (x * r), axis=0, keepdims=True)
    return dx, dg


def _rope_swap(x, lane):
    l64 = lane & (HD - 1)
    return jnp.where(l64 < ROT // 2, pltpu.roll(x, LANES - ROT // 2, 1), pltpu.roll(x, ROT // 2, 1))


def _rope_fwd(x, cos, sin, lane):
    return x * cos + _rope_swap(x, lane) * sin


def _rope_bwd(dy, cos, sin, lane):
    return dy * cos + jnp.where((lane & (HD - 1)) < ROT, _rope_swap(dy * sin, lane), 0.0)


def _g2(g_ref):
    g = g_ref[...]
    return jnp.concatenate([g, g], axis=-1)


def _pairs(width):
    return [slice(LANES * c, LANES * (c + 1)) for c in range(width // LANES)]


def _pick_lane(block, lane, idx):
    return jnp.sum(jnp.where(lane == idx, block, 0.0), axis=-1, keepdims=True)


def _head_a(x, g, wa, gq, gk, b_pad):
    def body(x_ref, g_ref, w_ref, gq_ref, gk_ref, b_ref, u_ref, p_ref, qo_ref, ko_ref, vo_ref, c_ref, cbc_ref, carry):
        @pl.when(pl.program_id(0) == 0)
        def _():
            carry[...] = jnp.zeros_like(carry)

        xv = x_ref[...]
        u = ((xv * _rms_rinv(xv)) * g_ref[...]).astype(BF16)
        u_ref[...] = u
        for lo in range(0, NA, D):
            hi = min(lo + D, NA)
            p_ref[:, lo:hi] = _dot_nn(u, w_ref[:, lo:hi])
        lane = _lane_iota((RT, LANES))
        lo_half = lane < HD
        gq2, gk2 = _g2(gq_ref), _g2(gk_ref)
        for c in _pairs(D):
            q = p_ref[:, c]
            k = p_ref[:, D + c.start:D + c.stop]
            qo_ref[:, c] = (((q * _head_rinv(q, lo_half)) * gq2) * QSCALE).astype(BF16)
            ko_ref[:, c] = ((k * _head_rinv(k, lo_half)) * gk2).astype(BF16)
        vo_ref[...] = p_ref[:, 2 * D:3 * D].astype(BF16)

        z = p_ref[:, FOFF:FOFF + LANES] + b_ref[...]
        logf = jnp.minimum(z, 0.0) - jnp.log1p(jnp.exp(-jnp.abs(z)))
        r = lax.broadcasted_iota(jnp.int32, (RT, RT), 0)
        cc = lax.broadcasted_iota(jnp.int32, (RT, RT), 1)
        tri = (r >= cc).astype(F32)
        loc = jnp.dot(tri, logf, precision=lax.Precision.HIGHEST, preferred_element_type=F32) + carry[0:1, :]
        c_ref[...] = loc
        carry[0:1, :] = loc[RT - 1:RT, :]
        for h in range(N_HEADS):
            cbc_ref[:, LANES * h:LANES * (h + 1)] = jnp.broadcast_to(_pick_lane(loc, lane, h), (RT, LANES))

    row = lambda width: pl.BlockSpec((RT, width), lambda i: (i, 0))
    whole = lambda arr: pl.BlockSpec(arr.shape, lambda i: (0,) * arr.ndim, pipeline_mode=pl.Buffered(1))
    return pl.pallas_call(
        body, name="head_a", grid=(S // RT,),
        in_specs=[row(D), whole(g), whole(wa), whole(gq), whole(gk), whole(b_pad)],
        out_specs=[row(D), row(NA), row(D), row(D), row(D), row(LANES), row(N_HEADS * LANES)],
        out_shape=[_sds((S, D), BF16), _sds((S, NA), F32)] + [_sds((S, D), BF16)] * 3
        + [_sds((S, LANES), F32), _sds((S, N_HEADS * LANES), F32)],
        scratch_shapes=[pltpu.VMEM((8, LANES), F32)], compiler_params=_params())(x, g, wa, gq, gk, b_pad)


def _key_le_query(offset, keys=TK):
    r = lax.broadcasted_iota(jnp.int32, (keys, TQ), 0)
    c = lax.broadcasted_iota(jnp.int32, (keys, TQ), 1)
    return (r + offset) <= c


def _widen(tile):
    return jnp.concatenate([tile] * (TQ // LANES), axis=1)


def _fox_fwd(qn, kn, vb, proj, crow, cbc, rider=()):
    nq = S // TQ

    def body(q_ref, k_ref, v_ref, g_ref, cq_ref, cbc_ref, o_ref, z_ref, lse_ref, st_s, pt_s):
        i = pl.program_id(1)
        qs = [q_ref[:, HD * hh:HD * (hh + 1)] for hh in range(HPS)]
        cqs = [cq_ref[hh, 0] for hh in range(HPS)]

        def scores(s, hh):
            off = pl.multiple_of(s * KS, KS)
            kj = k_ref[pl.ds(off, KS), HD * hh:HD * (hh + 1)]
            return (_dot_nt(kj, qs[hh]) + cqs[hh]) - _widen(cbc_ref[pl.ds(off, KS), LANES * hh:LANES * (hh + 1)])

        def values(s, hh, pt):
            off = pl.multiple_of(s * KS, KS)
            return _dot_tn(v_ref[pl.ds(off, KS), HD * hh:HD * (hh + 1)], pt)

        def step(s, slot, carries, mask=None, last=False):
            if not last:
                for hh in range(HPS):
                    st_s[1 - slot, hh] = scores(s + 1, hh)
            pvs = [values(jnp.maximum(s - 1, 0), hh, pt_s[1 - slot, hh]) for hh in range(HPS)]
            out = []
            for hh in range(HPS):
                m, l, acc = carries[hh]
                st = st_s[slot, hh]
                if mask is not None:
                    st = jnp.where(mask, st, -jnp.inf)
                m_new = jnp.maximum(m, jnp.max(st, axis=0, keepdims=True))
                pt = jnp.exp(st - m_new)
                alpha = jnp.exp(m - m_new)
                pt_s[slot, hh] = pt.astype(BF16)
                out.append((m_new, alpha * l + jnp.sum(pt, axis=0, keepdims=True), alpha * (acc + pvs[hh])))
            return tuple(out)

        for hh in range(HPS):
            st_s[0, hh] = scores(0, hh)
            pt_s[1, hh] = jnp.zeros((KS, TQ), BF16)
        one = (jnp.full((1, TQ), -jnp.inf, F32), jnp.zeros((1, TQ), F32), jnp.zeros((HD, TQ), F32))
        carries = lax.fori_loop(0, i, lambda t, cr: step(2 * t + 1, 1, step(2 * t, 0, cr)), (one,) * HPS)
        carries = step(2 * i, 0, carries, mask=_key_le_query(0, KS))
        carries = step(2 * i + 1, 1, carries, mask=_key_le_query(KS, KS), last=True)
        accs = []
        for hh in range(HPS):
            m, l, acc = carries[hh]
            acc = acc + values(2 * i + 1, hh, pt_s[1, hh])
            accs.append(acc / l)
            lse_ref[hh, 0] = m + jnp.log(l)
        o = jnp.concatenate(accs, axis=0).T
        o_ref[...] = o
        g = g_ref[...]
        z_ref[...] = (o * (g * _sigmoid(g))).astype(BF16)

    qblk = pl.BlockSpec((TQ, HW), lambda hp, i: (i, hp))
    full = pl.BlockSpec((S, HW), lambda hp, i: (0, hp))
    rows = pl.BlockSpec((HPS, 1, 1, TQ), lambda hp, i: (hp, i, 0, 0))
    return _call(
        body, name="fox_fwd", args=(qn, kn, vb, proj, crow, cbc), grid=(N_HEADS // HPS, nq),
        in_specs=[qblk, full, full,
                  pl.BlockSpec((TQ, HW), lambda hp, i: (i, GOFF // HW + hp)),
                  rows, pl.BlockSpec((S, HPS * LANES), lambda hp, i: (0, hp))],
        out_specs=[qblk, qblk, rows],
        out_shape=[_sds((S, D), F32), _sds((S, D), BF16), _sds((N_HEADS, nq, 1, TQ), F32)],
        scratch_shapes=[pltpu.VMEM((2, HPS, KS, TQ), F32), pltpu.VMEM((2, HPS, KS, TQ), BF16)], rider=rider)


def _fox_bwd_pre(dh, w_out, proj, o):
    nq, rows = S // TQ, 2 * TQ
    per = rows // TQ

    def body(dh_ref, w_ref, g_ref, o_ref, do_ref, dg_ref, delta_ref):
        g = g_ref[...]
        sg = _sigmoid(g)
        dzv = _dot_nt(dh_ref[...].astype(BF16), w_ref[...])
        ov = o_ref[...]
        do = dzv * (g * sg)
        dg_ref[...] = (dzv * ov * (sg * (1.0 + g * (1.0 - sg)))).astype(BF16)
        do_ref[...] = do.astype(BF16)
        prod_t = (do * ov).T
        for h in range(N_HEADS):
            for b in range(per):
                delta_ref[h, b] = jnp.sum(prod_t[HD * h:HD * (h + 1), TQ * b:TQ * (b + 1)], axis=0, keepdims=True)

    row = pl.BlockSpec((rows, D), lambda i: (i, 0))
    return pl.pallas_call(
        body, name="fox_bwd_pre", grid=(S // rows,),
        in_specs=[row, pl.BlockSpec(w_out.shape, lambda i: (0, 0), pipeline_mode=pl.Buffered(1)),
                  pl.BlockSpec((rows, D), lambda i: (i, GOFF // D)), row],
        out_specs=[row, row, pl.BlockSpec((N_HEADS, per, 1, TQ), lambda i: (0, i, 0, 0))],
        out_shape=[_sds((S, D), BF16), _sds((S, D), BF16), _sds((N_HEADS, nq, 1, TQ), F32)],
        compiler_params=_params())(dh, w_out, proj, o)


def _fox_bwd(qn, kn, vb, dob, lse, delta, crow, cbc, rider=()):
    nq, nkb = S // TQ, S // TK

    def body(q_ref, k_ref, v_ref, do_ref, lse_ref, del_ref, cq_ref, cbc_ref,
             dk_ref, dv_ref, dcs_ref, dq_ref, dr_ref, st_s, dp_s, pt_s, ds_s, dq_acc, dr_acc):
        j = pl.program_id(1)

        @pl.when(j == 0)
        def _():
            dq_acc[...] = jnp.zeros_like(dq_acc)
            dr_acc[...] = jnp.zeros_like(dr_acc)

        kjs = [k_ref[:, HD * hh:HD * (hh + 1)] for hh in range(HPS)]
        vjs = [v_ref[:, HD * hh:HD * (hh + 1)] for hh in range(HPS)]

        def rows_of(ref, u, hh):
            off = pl.multiple_of(u * TQ, TQ)
            return ref[pl.ds(off, TQ), HD * hh:HD * (hh + 1)]

        def products(u, hh):
            st = (_dot_nt(kjs[hh], rows_of(q_ref, u, hh)) + cq_ref[hh, u]) - _widen(
                cbc_ref[:, LANES * hh:LANES * (hh + 1)])
            return st, _dot_nt(vjs[hh], rows_of(do_ref, u, hh))

        def step(u, slot, carries, masked=False):
            nxt = jnp.minimum(u + 1, nq - 1)
            for hh in range(HPS):
                st_s[1 - slot, hh], dp_s[1 - slot, hh] = products(nxt, hh)
            prev = jnp.maximum(u - 1, 0)
            dvs = [_dot_nn(pt_s[1 - slot, hh], rows_of(do_ref, prev, hh)) for hh in range(HPS)]
            dks = [_dot_nn(ds_s[1 - slot, hh], rows_of(q_ref, prev, hh)) for hh in range(HPS)]
            for hh in range(HPS):
                dq_acc[hh, prev] += _dot_tn(kjs[hh], ds_s[1 - slot, hh])
            out = []
            for hh in range(HPS):
                dk, dv, dcs = carries[hh]
                st = st_s[slot, hh]
                if masked:
                    st = jnp.where(_key_le_query((j - u) * TQ), st, -jnp.inf)
                pt = jnp.exp(st - lse_ref[hh, u])
                dst = pt * (dp_s[slot, hh] - del_ref[hh, u])
                pt_s[slot, hh] = pt.astype(BF16)
                ds_s[slot, hh] = dst.astype(BF16)
                dr_acc[hh, u] += jnp.sum(dst, axis=0, keepdims=True)
                out.append((dk + dks[hh], dv + dvs[hh], dcs + (dst[:, :LANES] + dst[:, LANES:])))
            return tuple(out)

        t0 = j // 2
        for hh in range(HPS):
            st_s[0, hh], dp_s[0, hh] = products(2 * t0, hh)
            pt_s[1, hh] = jnp.zeros((TK, TQ), BF16)
            ds_s[1, hh] = jnp.zeros((TK, TQ), BF16)
        one = (jnp.zeros((TK, HD), F32), jnp.zeros((TK, HD), F32), jnp.zeros((TK, LANES), F32))
        carries = step(2 * t0 + 1, 1, step(2 * t0, 0, (one,) * HPS, masked=True), masked=True)
        carries = lax.fori_loop(t0 + 1, nq // 2, lambda t, cr: step(2 * t + 1, 1, step(2 * t, 0, cr)), carries)
        dks, dvs = [], []
        lane = _lane_iota((TK, LANES))
        dcs_all = jnp.zeros((TK, LANES), F32)
        for hh in range(HPS):
            dk, dv, dcs = carries[hh]
            dks.append(dk + _dot_nn(ds_s[1, hh], rows_of(q_ref, nq - 1, hh)))
            dvs.append(dv + _dot_nn(pt_s[1, hh], rows_of(do_ref, nq - 1, hh)))
            dq_acc[hh, nq - 1] += _dot_tn(kjs[hh], ds_s[1, hh])
            dcs_all = jnp.where(lane == HPS * pl.program_id(0) + hh, -jnp.sum(dcs, axis=-1, keepdims=True), dcs_all)
        dcs_ref[0] = dcs_all
        dk_ref[...] = jnp.concatenate(dks, axis=-1)
        dv_ref[...] = jnp.concatenate(dvs, axis=-1).astype(BF16)

        @pl.when(j == nkb - 1)
        def _():
            for i in range(nq):
                dq_ref[TQ * i:TQ * (i + 1), :] = jnp.concatenate([dq_acc[hh, i] for hh in range(HPS)], axis=0).T
            dr_ref[...] = dr_acc[...]

    kblk = pl.BlockSpec((TK, HW), lambda hp, j: (j, hp))
    full = pl.BlockSpec((S, HW), lambda hp, j: (0, hp))
    rows = pl.BlockSpec((HPS, nq, 1, TQ), lambda hp, j: (hp, 0, 0, 0))
    cblk = pl.BlockSpec((TK, HPS * LANES), lambda hp, j: (j, hp))
    return _call(
        body, name="fox_bwd", args=(qn, kn, vb, dob, lse, delta, crow, cbc), grid=(N_HEADS // HPS, nkb),
        in_specs=[full, kblk, kblk, full, rows, rows, rows, cblk],
        out_specs=[kblk, kblk, pl.BlockSpec((1, TK, LANES), lambda hp, j: (hp, j, 0)), full, rows],
        out_shape=[_sds((S, D), F32), _sds((S, D), BF16), _sds((N_HEADS // HPS, S, LANES), F32), _sds((S, D), F32),
                   _sds((N_HEADS, nq, 1, TQ), F32)],
        scratch_shapes=[pltpu.VMEM((2, HPS, TK, TQ), F32), pltpu.VMEM((2, HPS, TK, TQ), F32),
                        pltpu.VMEM((2, HPS, TK, TQ), BF16), pltpu.VMEM((2, HPS, TK, TQ), BF16),
                        pltpu.VMEM((HPS, nq, HD, TQ), F32), pltpu.VMEM((HPS, nq, 1, TQ), F32)], rider=rider)


def _prep_a_bwd(dq, dk, dv, dgate, drow, dcs, proj, b_pad, gq, gk):
    nt = S // TM

    def body(dq_ref, dk_ref, dv_ref, dgt_ref, dr_ref, dcs_ref, xq_ref, xk_ref, f_ref, b_ref, gq_ref, gk_ref,
             o_ref, dgq_ref, dgk_ref, db_ref, carry):
        @pl.when(pl.program_id(0) == 0)
        def _():
            carry[...] = jnp.zeros_like(carry)
            dgq_ref[...] = jnp.zeros_like(dgq_ref)
            dgk_ref[...] = jnp.zeros_like(dgk_ref)
            db_ref[...] = jnp.zeros_like(db_ref)

        lane = _lane_iota((TM, LANES))
        lo_half = _half_ones()
        gq2, gk2 = _g2(gq_ref), _g2(gk_ref)
        dgq, dgk = jnp.zeros((1, LANES), F32), jnp.zeros((1, LANES), F32)
        for c in _pairs(D):
            dxq, dg = _head_norm_bwd(dq_ref[:, c] * QSCALE, xq_ref[:, c], gq2, lo_half)
            o_ref[:, c] = dxq.astype(BF16)
            dgq = dgq + dg
            dxk, dg = _head_norm_bwd(dk_ref[:, c], xk_ref[:, c], gk2, lo_half)
            o_ref[:, D + c.start:D + c.stop] = dxk.astype(BF16)
            dgk = dgk + dg
        dgq_ref[...] += dgq
        dgk_ref[...] += dgk
        o_ref[:, 2 * D:3 * D] = dv_ref[...]
        o_ref[:, GOFF:GOFF + D] = dgt_ref[...]

        dc = dr_ref[...]
        for group in range(N_HEADS // HPS):
            dc = dc + dcs_ref[group]
        r = lax.broadcasted_iota(jnp.int32, (TM, TM), 0)
        c = lax.broadcasted_iota(jnp.int32, (TM, TM), 1)
        tri = (c >= r).astype(F32)
        dlogf = jnp.dot(tri, dc, precision=lax.Precision.HIGHEST, preferred_element_type=F32) + carry[0:1, :]
        carry[0:1, :] = dlogf[0:1, :]
        df = dlogf * (1.0 / (1.0 + jnp.exp(f_ref[...] + b_ref[...])))
        db_ref[...] += jnp.sum(df, axis=0, keepdims=True)
        o_ref[:, FOFF:FOFF + LANES] = df.astype(BF16)
        o_ref[:, FOFF + LANES:NA] = jnp.zeros((TM, NA - FOFF - LANES), BF16)

    rev = lambda width, col: pl.BlockSpec((TM, width), lambda i: (nt - 1 - i, col))
    gspec = pl.BlockSpec((1, HD), lambda i: (0, 0))
    acc = pl.BlockSpec((1, LANES), lambda i: (0, 0))
    return pl.pallas_call(
        body, name="prep_a_bwd", grid=(nt,),
        in_specs=[rev(D, 0), rev(D, 0), rev(D, 0), rev(D, 0), rev(LANES, 0),
                  pl.BlockSpec((N_HEADS // HPS, TM, LANES), lambda i: (0, nt - 1 - i, 0)),
                  rev(D, 0), rev(D, 1), rev(LANES, FOFF // LANES), acc, gspec, gspec],
        out_specs=[rev(NA, 0), acc, acc, acc],
        out_shape=[_sds((S, NA), BF16)] + [_sds((1, LANES), F32)] * 3,
        scratch_shapes=[pltpu.VMEM((8, LANES), F32)],
        compiler_params=_params())(dq, dk, dv, dgate, drow, dcs, proj, proj, proj, b_pad, gq, gk)


def _head_b(x, z_a, w_out_a, g_kv, g_b, w_kv, w_in_b, gq, gk, cos2, sin2):
    nkv = w_kv.shape[1] // 2

    def body(x_ref, z_ref, wo_ref, gkv_ref, gb_ref, wkv_ref, wb_ref, gq_ref, gk_ref, c_ref, s_ref,
             h_ref, ukv_ref, ub_ref, kv_ref, pb_ref, qo_ref, ko_ref, vo_ref):
        xv = x_ref[...] + _dot_nn(z_ref[...], wo_ref[...])
        h_ref[...] = xv
        xn = xv * _rms_rinv(xv)
        ukv = (xn * gkv_ref[...]).astype(BF16)
        ub = (xn * gb_ref[...]).astype(BF16)
        ukv_ref[...] = ukv
        ub_ref[...] = ub
        kv_ref[...] = _dot_nn(ukv, wkv_ref[...])
        for lo in range(0, 2 * D, D):
            pb_ref[:, lo:lo + D] = _dot_nn(ub, wb_ref[:, lo:lo + D])
        lane = _lane_iota((RT, LANES))
        lo_half = lane < HD
        cos, sin = c_ref[...], s_ref[...]
        gq2, gk2 = _g2(gq_ref), _g2(gk_ref)
        for c in _pairs(D):
            q = pb_ref[:, c]
            qo_ref[:, c] = (_rope_fwd((q * _head_rinv(q, lo_half)) * gq2, cos, sin, lane) * QSCALE).astype(BF16)
        for c in _pairs(nkv):
            k = kv_ref[:, c]
            ko_ref[:, c] = _rope_fwd((k * _head_rinv(k, lo_half)) * gk2, cos, sin, lane).astype(BF16)
        vo_ref[...] = kv_ref[:, nkv:2 * nkv].astype(BF16)

    row = lambda width: pl.BlockSpec((RT, width), lambda i: (i, 0))
    whole = lambda arr: pl.BlockSpec(arr.shape, lambda i: (0,) * arr.ndim, pipeline_mode=pl.Buffered(1))
    return pl.pallas_call(
        body, name="head_b", grid=(S // RT,),
        in_specs=[row(D), row(D), whole(w_out_a), whole(g_kv), whole(g_b), whole(w_kv), whole(w_in_b), whole(gq),
                  whole(gk), row(LANES), row(LANES)],
        out_specs=[row(D), row(D), row(D), row(2 * nkv), row(2 * D), row(D), row(nkv), row(nkv)],
        out_shape=[_sds((S, D), F32), _sds((S, D), BF16), _sds((S, D), BF16), _sds((S, 2 * nkv), F32),
                   _sds((S, 2 * D), F32), _sds((S, D), BF16), _sds((S, nkv), BF16), _sds((S, nkv), BF16)],
        compiler_params=_params())(x, z_a, w_out_a, g_kv, g_b, w_kv, w_in_b, gq, gk, cos2, sin2)


N_KV, GRP = 4, 4


def _swa_mask(n):
    r = lax.broadcasted_iota(jnp.int32, (2 * WIN, GRP * WIN), 0)
    q = lax.broadcasted_iota(jnp.int32, (2 * WIN, GRP * WIN), 1) & (WIN - 1)
    return (r > q) & (r <= q + WIN) & ((r >= WIN) | (n > 0))


def _stack4(ref_or_val, base):
    return jnp.concatenate([ref_or_val[:, base + HD * g: base + HD * (g + 1)] for g in range(GRP)], axis=0)


def _unstack4(xt):
    return jnp.concatenate([xt[:, WIN * g:WIN * (g + 1)] for g in range(GRP)], axis=0).T


def _band(prev_ref, cur_ref, kh):
    return jnp.concatenate([prev_ref[:, HD * kh:HD * (kh + 1)], cur_ref[:, HD * kh:HD * (kh + 1)]], axis=0)


def _sink_row(s_ref, first):
    lane = _lane_iota((1, GRP * WIN))
    row = jnp.full((1, GRP * WIN), s_ref[first + GRP - 1], F32)
    for g in range(GRP - 2, -1, -1):
        row = jnp.where(lane < WIN * (g + 1), s_ref[first + g], row)
    return row


def _swa_fwd(qb, ksh, vsh, pb, sinks):
    nb = S // WIN

    def body(q_ref, kp_ref, kc_ref, vp_ref, vc_ref, g_ref, s_ref, o_ref, z_ref, lse_ref):
        n = pl.program_id(0)
        valid = _swa_mask(n)
        outs = []
        for kh in range(N_KV):
            kb, vb = _band(kp_ref, kc_ref, kh), _band(vp_ref, vc_ref, kh)
            st = jnp.where(valid, _dot_nt(kb, _stack4(q_ref, GRP * HD * kh)), -jnp.inf)
            sink = _sink_row(s_ref, GRP * kh)
            m = jnp.maximum(jnp.max(st, axis=0, keepdims=True), sink)
            pt = jnp.exp(st - m)
            l = jnp.sum(pt, axis=0, keepdims=True) + jnp.exp(sink - m)
            outs.append(_unstack4(_dot_tn(vb, pt.astype(BF16)) / l))
            lse = m + jnp.log(l)
            for g in range(GRP):
                lse_ref[GRP * kh + g, 0] = lse[:, WIN * g:WIN * (g + 1)]
        o = jnp.concatenate(outs, axis=-1)
        o_ref[...] = o
        g = g_ref[...]
        z_ref[...] = (o * (g * _sigmoid(g))).astype(BF16)

    row = pl.BlockSpec((WIN, D), lambda n: (n, 0))
    prev = pl.BlockSpec((WIN, N_KV * HD), lambda n: (jnp.maximum(n - 1, 0), 0))
    cur = pl.BlockSpec((WIN, N_KV * HD), lambda n: (n, 0))
    return pl.pallas_call(
        body, name="swa_fwd", grid=(nb,),
        in_specs=[row, prev, cur, prev, cur, pl.BlockSpec((WIN, D), lambda n: (n, 1)),
                  pl.BlockSpec(memory_space=pltpu.SMEM)],
        out_specs=[row, row, pl.BlockSpec((N_HEADS, 1, 1, WIN), lambda n: (0, n, 0, 0))],
        out_shape=[_sds((S, D), F32), _sds((S, D), BF16), _sds((N_HEADS, nb, 1, WIN), F32)],
        compiler_params=_params())(qb, ksh, ksh, vsh, vsh, pb, sinks)


def _swa_bwd(qb, ksh, vsh, dz, o, lse, pb, sinks, gq, cos2, sin2):
    nb = S // WIN

    def body(q_ref, kp_ref, kc_ref, vp_ref, vc_ref, dz_ref, o_ref, lse_ref, x_ref, g_ref, s_ref, gq_ref, c_ref, sn_ref,
             dpb_ref, dka_ref, dkb_ref, dva_ref, dvb_ref, dsink_ref, dgq_ref):
        n = pl.program_id(0)

        @pl.when(n == 0)
        def _():
            dsink_ref[...] = jnp.zeros_like(dsink_ref)
            dgq_ref[...] = jnp.zeros_like(dgq_ref)

        valid = _swa_mask(n)
        g = g_ref[...]
        sg = _sigmoid(g)
        dzv = dz_ref[...]
        ov = o_ref[...]
        do = dzv * (g * sg)
        dpb_ref[:, D:2 * D] = (dzv * ov * (sg * (1.0 + g * (1.0 - sg)))).astype(BF16)
        prod_t = (do * ov).T
        lane1 = _lane_iota((1, LANES))
        dqs, dkas, dkbs, dvas, dvbs = [], [], [], [], []
        dsink = jnp.zeros((1, LANES), F32)
        for kh in range(N_KV):
            kb, vb = _band(kp_ref, kc_ref, kh), _band(vp_ref, vc_ref, kh)
            base = GRP * HD * kh
            qs = _stack4(q_ref, base)
            dos = _stack4(do, base).astype(BF16)
            delta = jnp.concatenate(
                [jnp.sum(prod_t[base + HD * gg:base + HD * (gg + 1), :], axis=0, keepdims=True)
                 for gg in range(GRP)], axis=1)
            lse = jnp.concatenate([lse_ref[GRP * kh + gg, 0] for gg in range(GRP)], axis=1)
            st = jnp.where(valid, _dot_nt(kb, qs), -jnp.inf)
            pt = jnp.exp(st - lse)
            dst = pt * (_dot_nt(vb, dos) - delta)
            dsb = dst.astype(BF16)
            dqs.append(_unstack4(_dot_tn(kb, dsb)))
            dkband = _dot_nn(dsb, qs)
            dvband = _dot_nn(pt.astype(BF16), dos)
            dkbs.append(dkband[0:WIN, :])
            dkas.append(dkband[WIN:2 * WIN, :])
            dvbs.append(dvband[0:WIN, :])
            dvas.append(dvband[WIN:2 * WIN, :])
            ps_delta = jnp.exp(_sink_row(s_ref, GRP * kh) - lse) * delta
            for gg in range(GRP):
                val = jnp.sum(ps_delta[:, WIN * gg:WIN * (gg + 1)], axis=1, keepdims=True)
                dsink = dsink - jnp.where(lane1 == GRP * kh + gg, val, 0.0)
        dka_ref[...] = jnp.concatenate(dkas, axis=-1)
        dkb_ref[...] = jnp.concatenate(dkbs, axis=-1)
        dva_ref[...] = jnp.concatenate(dvas, axis=-1)
        dvb_ref[...] = jnp.concatenate(dvbs, axis=-1)
        dsink_ref[...] += dsink

        lane = _lane_iota((WIN, LANES))
        g2, cos, sin = _g2(gq_ref), c_ref[...], sn_ref[...]
        lo_half = _half_ones()
        dg_tot = jnp.zeros((1, LANES), F32)
        for kh in range(N_KV):
            for c in _pairs(GRP * HD):
                cols = slice(GRP * HD * kh + c.start, GRP * HD * kh + c.stop)
                dn = _rope_bwd(dqs[kh][:, c] * QSCALE, cos, sin, lane)
                dx, dg = _head_norm_bwd(dn, x_ref[:, cols], g2, lo_half)
                dpb_ref[:, cols] = dx.astype(BF16)
                dg_tot = dg_tot + dg
        dgq_ref[...] += dg_tot

    row = pl.BlockSpec((WIN, D), lambda n: (n, 0))
    prev = pl.BlockSpec((WIN, N_KV * HD), lambda n: (jnp.maximum(n - 1, 0), 0))
    cur = pl.BlockSpec((WIN, N_KV * HD), lambda n: (n, 0))
    acc = pl.BlockSpec((1, LANES), lambda n: (0, 0))
    tab = pl.BlockSpec((WIN, LANES), lambda n: (n, 0))
    return pl.pallas_call(
        body, name="swa_bwd", grid=(nb,),
        in_specs=[row, prev, cur, prev, cur, row, row, pl.BlockSpec((N_HEADS, 1, 1, WIN), lambda n: (0, n, 0, 0)),
                  row, pl.BlockSpec((WIN, D), lambda n: (n, 1)), pl.BlockSpec(memory_space=pltpu.SMEM),
                  pl.BlockSpec((1, HD), lambda n: (0, 0)), tab, tab],
        out_specs=[pl.BlockSpec((WIN, 2 * D), lambda n: (n, 0)), cur, cur, cur, cur, acc, acc],
        out_shape=[_sds((S, 2 * D), BF16)] + [_sds((S, 256), F32)] * 4 + [_sds((1, LANES), F32)] * 2,
        compiler_params=_params())(qb, ksh, ksh, vsh, vsh, dz, o, lse, pb, pb, sinks, gq, cos2, sin2)


def _prep_kv_bwd(dka, dkb, dva, dvb, kv, gk, cos2, sin2):
    nt = S // RB
    per = RB // WIN

    def shifted(cur_ref, nxt_ref, has_next):
        return jnp.concatenate([cur_ref[WIN:RB, :], jnp.where(has_next, nxt_ref[...], 0.0)], axis=0)

    def body(dka_ref, dkb_ref, dkn_ref, dva_ref, dvb_ref, dvn_ref, x_ref, g_ref, c_ref, s_ref, o_ref, dgk_ref):
        i, j = pl.program_id(0), pl.program_id(1)

        @pl.when((i == 0) & (j == 0))
        def _():
            dgk_ref[...] = jnp.zeros_like(dgk_ref)

        has_next = i < nt - 1

        @pl.when(j == 0)
        def _():
            lane = _lane_iota((RB, LANES))
            g2, cos, sin = _g2(g_ref), c_ref[...], s_ref[...]
            dy_all = dka_ref[...] + shifted(dkb_ref, dkn_ref, has_next)
            dg_tot = jnp.zeros((1, LANES), F32)
            lo_half = _half_ones()
            for c in _pairs(CB):
                dn = _rope_bwd(dy_all[:, c], cos, sin, lane)
                dx, dg = _head_norm_bwd(dn, x_ref[:, c], g2, lo_half)
                o_ref[:, c] = dx.astype(BF16)
                dg_tot = dg_tot + dg
            dgk_ref[...] += dg_tot

        @pl.when(j == 1)
        def _():
            o_ref[...] = (dva_ref[...] + shifted(dvb_ref, dvn_ref, has_next)).astype(BF16)

    cur = pl.BlockSpec((RB, CB), lambda i, j: (i, 0))
    nxt = pl.BlockSpec((WIN, CB), lambda i, j: (jnp.minimum(per * (i + 1), S // WIN - 1), 0))
    tab = pl.BlockSpec((RB, LANES), lambda i, j: (i, 0))
    return pl.pallas_call(
        body, name="prep_kv_bwd", grid=(nt, 2),
        in_specs=[cur, cur, nxt, cur, cur, nxt, cur, pl.BlockSpec((1, HD), lambda i, j: (0, 0)), tab, tab],
        out_specs=[pl.BlockSpec((RB, CB), lambda i, j: (i, j)), pl.BlockSpec((1, LANES), lambda i, j: (0, 0))],
        out_shape=[_sds((S, 2 * CB), BF16), _sds((1, LANES), F32)],
        compiler_params=_params())(dka, dkb, dkb, dva, dvb, dvb, kv, gk, cos2, sin2)


def _out_b_loss(z, w_out, h1, tgt):
    tm = 2 * RT

    def body(z_ref, w_ref, h_ref, t_ref, dy_ref, l_ref):
        @pl.when(pl.program_id(0) == 0)
        def _():
            l_ref[...] = jnp.zeros_like(l_ref)

        e = (h_ref[...] + _dot_nn(z_ref[...], w_ref[...])) - t_ref[...]
        dy_ref[...] = e * (1.0 / D)
        l_ref[...] += jnp.sum(jnp.sum(e * e, axis=-1, keepdims=True), axis=0, keepdims=True)

    row = pl.BlockSpec((tm, D), lambda i: (i, 0))
    return pl.pallas_call(
        body, name="out_b_loss", grid=(S // tm,),
        in_specs=[row, pl.BlockSpec(w_out.shape, lambda i: (0, 0), pipeline_mode=pl.Buffered(1)), row, row],
        out_specs=[row, pl.BlockSpec((1, LANES), lambda i: (0, 0))],
        out_shape=[_sds((S, D), F32), _sds((1, LANES), F32)], compiler_params=_params())(z, w_out, h1, tgt)


def _du_a_rms_bwd(dproj, wa, x, g, dres, after):
    tm, tk = 1024, NA // 2
    nk = NA // tk

    def body(a_ref, b_ref, x_ref, g_ref, dr_ref, after_ref, dx_ref, dg_ref, acc):
        i, kk = pl.program_id(0), pl.program_id(1)

        @pl.when((i == 0) & (kk == 0))
        def _():
            dg_ref[...] = jnp.zeros_like(dg_ref)

        p = _dot_nt(a_ref[...], b_ref[...])

        @pl.when(kk == 0)
        def _():
            acc[...] = p

        @pl.when(kk == nk - 1)
        def _():
            dx, dg = _rms_bwd_core(acc[...] + p, x_ref[...], g_ref[...])
            dx_ref[...] = dr_ref[...] + dx
            dg_ref[...] += dg

    assert nk == 2
    row = pl.BlockSpec((tm, D), lambda i, kk: (i, 0))
    vec = pl.BlockSpec((1, D), lambda i, kk: (0, 0))
    return pl.pallas_call(
        body, name="du_a_rms_bwd", grid=(S // tm, nk),
        in_specs=[pl.BlockSpec((tm, tk), lambda i, kk: (i, kk)), pl.BlockSpec((D, tk), lambda i, kk: (0, kk)),
                  row, vec, row, pl.BlockSpec(after.shape, lambda i, kk: (0, 0))],
        out_specs=[row, vec], out_shape=[_sds((S, D), F32), _sds((1, D), F32)],
        scratch_shapes=[pltpu.VMEM((tm, D), F32)], compiler_params=_params())(dproj, wa, x, g, dres, after)


def _du_b_rms_bwd(dpb, w_in_b, dkv, w_kv, h1, g_b, g_kv, dy):
    tm = 2 * RT

    def body(ab_ref, wb_ref, akv_ref, wkv_ref, x_ref, gb_ref, gkv_ref, dy_ref, dh_ref, dgb_ref, dgkv_ref):
        @pl.when(pl.program_id(0) == 0)
        def _():
            dgb_ref[...] = jnp.zeros_like(dgb_ref)
            dgkv_ref[...] = jnp.zeros_like(dgkv_ref)

        x = x_ref[...]
        dx1, dg1 = _rms_bwd_core(_dot_nt(ab_ref[...], wb_ref[...]), x, gb_ref[...])
        dx2, dg2 = _rms_bwd_core(_dot_nt(akv_ref[...], wkv_ref[...]), x, gkv_ref[...])
        dh_ref[...] = dy_ref[...] + dx1 + dx2
        dgb_ref[...] += dg1
        dgkv_ref[...] += dg2

    row = lambda width: pl.BlockSpec((tm, width), lambda i: (i, 0))
    whole = lambda arr: pl.BlockSpec(arr.shape, lambda i: (0, 0), pipeline_mode=pl.Buffered(1))
    vec = pl.BlockSpec((1, D), lambda i: (0, 0))
    return pl.pallas_call(
        body, name="du_b_rms_bwd", grid=(S // tm,),
        in_specs=[row(dpb.shape[1]), whole(w_in_b), row(dkv.shape[1]), whole(w_kv), row(D), vec, vec, row(D)],
        out_specs=[row(D), vec, vec], out_shape=[_sds((S, D), F32), _sds((1, D), F32), _sds((1, D), F32)],
        compiler_params=_params())(dpb, w_in_b, dkv, w_kv, h1, g_b, g_kv, dy)


def _gather_first(w_in_a, w_out_a, w_kv, w_in_b, w_out_b, norm_a_g):
    rows, cols = w_in_a.shape[-2:]
    groups = rows // LANES
    cols_pad = -(-cols // LANES) * LANES

    def body(wia_ref, woa_ref, wkv_ref, wib_ref, wob_ref, ga_ref,
             wa_g, ga_g, woa_s, wkv_s, wib_s, wob_s, wa_s, st_a, st_oa, st_kv, st_ib, st_ob, plane_t, load_sems, *sems):
        sources = [wia_ref, woa_ref.at[0], wkv_ref, wib_ref.at[0], wob_ref.at[0]]
        stages = [st_a, st_oa, st_kv, st_ib, st_ob]
        loads = [pltpu.make_async_copy(src, dst, load_sems.at[i]) for i, (src, dst) in enumerate(zip(sources, stages))]
        for cp in loads:
            cp.start()
        loads[0].wait()
        plane_t[cols_pad - LANES:, :] = jnp.zeros((LANES, LANES), F32)
        for j in range(groups):
            for c0 in range(0, cols, 64):
                n = min(64, cols - c0)
                plane_t[c0:c0 + n, :] = st_a[pl.ds(c0 * groups + j, n, stride=groups), :]
            for c0 in range(0, cols, LANES):
                n = min(LANES, cols - c0)
                wa_s[j * LANES:(j + 1) * LANES, c0:c0 + n] = plane_t[c0:c0 + LANES, :].T[:, :n].astype(BF16)

        def cast_the_rest():
            for cp, stage, out in zip(loads[1:], stages[1:], [woa_s, wkv_s, wib_s, wob_s]):
                cp.wait()
                out[...] = stage[...].astype(BF16)

        _gather_two_level([wa_s, ga_ref], [wa_g, ga_g], sems, meanwhile=cast_the_rest)

    vmem = pl.BlockSpec(memory_space=pltpu.VMEM)
    anyspec = pl.BlockSpec(memory_space=pl.ANY)
    shard = lambda w: _sds(w.shape[-2:], BF16)
    stage = lambda w: pltpu.VMEM(w.shape[-2:], F32)
    return pl.pallas_call(
        body, name="gather_first", in_specs=[anyspec] * 5 + [vmem],
        out_specs=[anyspec, anyspec, vmem, vmem, vmem, vmem],
        out_shape=[_sds((N_DEV,) + w_in_a.shape[-2:], BF16), _sds((N_DEV,) + norm_a_g.shape, F32),
                   shard(w_out_a), shard(w_kv), shard(w_in_b), shard(w_out_b)],
        scratch_shapes=[pltpu.VMEM(w_in_a.shape[-2:], BF16), pltpu.VMEM((cols * groups, LANES), F32), stage(w_out_a),
                        stage(w_kv), stage(w_in_b), stage(w_out_b), pltpu.VMEM((cols_pad, LANES), F32),
                        pltpu.SemaphoreType.DMA((5,))] + _exchange_sems(2),
        compiler_params=pltpu.CompilerParams(vmem_limit_bytes=VMEM_LIMIT, has_side_effects=True))(
            _entry_view(w_in_a).reshape(cols * groups, LANES), w_out_a, w_kv, w_in_b, w_out_b, norm_a_g)


def _pair_reduce(slots):
    n_chip = N_DEV // 2
    _, rows, cols = slots.shape

    def body(s_ref, o_ref, own_v, sib_v, send_sems, recv_sems, local_sems):
        x, y, c = lax.axis_index("x"), lax.axis_index("y"), lax.axis_index("c")
        copies = []
        for j in range(n_chip):
            own = pltpu.make_async_copy(s_ref.at[2 * j + c], own_v.at[j], local_sems.at[j])
            give = pltpu.make_async_remote_copy(
                src_ref=s_ref.at[2 * j + 1 - c], dst_ref=sib_v.at[j], send_sem=send_sems.at[j],
                recv_sem=recv_sems.at[j], device_id=(x, y, 1 - c), device_id_type=pl.DeviceIdType.MESH)
            own.start()
            give.start()
            copies.append((own, give))
        for j, (own, give) in enumerate(copies):
            own.wait()
            give.wait()
            o_ref[j] = (own_v[j].astype(F32) + sib_v[j].astype(F32)).astype(BF16)

    half = _sds((n_chip, rows, cols), slots.dtype)
    return pl.pallas_call(
        body, name="pair_reduce", in_specs=[pl.BlockSpec(memory_space=pl.ANY)],
        out_specs=pl.BlockSpec(memory_space=pltpu.VMEM), out_shape=half,
        scratch_shapes=[pltpu.VMEM(half.shape, half.dtype), pltpu.VMEM(half.shape, half.dtype),
                        pltpu.SemaphoreType.DMA((n_chip,)), pltpu.SemaphoreType.DMA((n_chip,)),
                        pltpu.SemaphoreType.DMA((n_chip,))],
        compiler_params=pltpu.CompilerParams(vmem_limit_bytes=VMEM_LIMIT, has_side_effects=True))(slots)


def _padded_col(c):
    if c < RAW_F:
        return c
    return FOFF + (c - RAW_F) if c < RAW_G else GOFF + (c - RAW_G)


def _shard_pieces():
    width = NA_RAW // N_DEV
    pieces = []
    for d in range(N_DEV):
        cuts = [width * d] + [c for c in (RAW_F, RAW_G) if width * d < c < width * (d + 1)] + [width * (d + 1)]
        for lo, hi in zip(cuts[:-1], cuts[1:]):
            pieces.append((d, lo - width * d, _padded_col(lo), hi - lo))
    return pieces


def _unshard_wa(wa_g):
    def body(w_ref, o_ref):
        o_ref[:, FOFF + N_HEADS:NA] = jnp.zeros((TM, NA - FOFF - N_HEADS), BF16)
        for d, src, dst, width in _shard_pieces():
            o_ref[:, dst:dst + width] = w_ref[d, :, src:src + width]

    return pl.pallas_call(
        body, name="unshard_wa", grid=(D // TM,),
        in_specs=[pl.BlockSpec((N_DEV, TM, NA_RAW // N_DEV), lambda i: (0, i, 0))],
        out_specs=pl.BlockSpec((TM, NA), lambda i: (i, 0)), out_shape=_sds((D, NA), BF16),
        compiler_params=_params())(wa_g)


def _reshard_dwa(dwa):
    def body(g_ref, o_ref):
        for d, src, dst, width in _shard_pieces():
            o_ref[d, :, src:src + width] = g_ref[:, dst:dst + width]

    return pl.pallas_call(
        body, name="reshard_dwa", grid=(D // TM,), in_specs=[pl.BlockSpec((TM, NA), lambda i: (i, 0))],
        out_specs=pl.BlockSpec((N_DEV, TM, NA_RAW // N_DEV), lambda i: (0, i, 0)),
        out_shape=_sds((N_DEV, D, NA_RAW // N_DEV), dwa.dtype), compiler_params=_params())(dwa)


CHIP_FLIPS = (2, 4, 6)


def _chip_exchange_start(partial):
    n = len(CHIP_FLIPS)

    def body(p_ref, land_ref, *rest):
        sends, recvs, token = rest[:n], rest[n:2 * n], rest[2 * n + 2]
        x, y, c = lax.axis_index("x"), lax.axis_index("y"), lax.axis_index("c")
        me = 4 * x + 2 * y + c
        for idx, k in enumerate(CHIP_FLIPS):
            pltpu.make_async_remote_copy(
                src_ref=p_ref.at[(me ^ k) >> 1], dst_ref=land_ref.at[me >> 1], send_sem=sends[idx],
                recv_sem=recvs[idx], device_id=(x ^ ((k >> 2) & 1), y ^ ((k >> 1) & 1), c),
                device_id_type=pl.DeviceIdType.MESH).start()
        token[...] = jnp.zeros_like(token)

    hbm = pl.BlockSpec(memory_space=pltpu.HBM)
    sem = pl.BlockSpec(memory_space=pltpu.SEMAPHORE)
    buf = pltpu.HBM(partial.shape, partial.dtype)
    return pl.pallas_call(
        body, name="chip_exchange_start",
        out_shape=(pltpu.SemaphoreType.DMA(()),) * (2 * n) + (buf, buf, _sds((8, LANES), F32)),
        in_specs=(hbm, hbm), out_specs=(sem,) * (2 * n) + (hbm, hbm, pl.BlockSpec(memory_space=pltpu.VMEM)),
        input_output_aliases={0: 2 * n, 1: 2 * n + 1},
        compiler_params=pltpu.CompilerParams(has_side_effects=pltpu.SideEffectType.DATAFLOW_SIDE_EFFECTING))(
            pltpu.with_memory_space_constraint(partial, pltpu.HBM),
            pltpu.with_memory_space_constraint(lax.empty(partial.shape, partial.dtype), pltpu.HBM))


def _chip_exchange_wait(started, after):
    n = len(CHIP_FLIPS)
    sems, (p_thru, land_thru) = started[:2 * n], started[2 * n:2 * n + 2]

    def body(p_ref, land_ref, *rest):
        sends, recvs = rest[:n], rest[n:2 * n]
        x, y, c = lax.axis_index("x"), lax.axis_index("y"), lax.axis_index("c")
        me = 4 * x + 2 * y + c
        for idx, k in enumerate(CHIP_FLIPS):
            copy = pltpu.make_async_remote_copy(
                src_ref=p_ref.at[(me ^ k) >> 1], dst_ref=land_ref.at[(me ^ k) >> 1], send_sem=sends[idx],
                recv_sem=recvs[idx], device_id=(x ^ ((k >> 2) & 1), y ^ ((k >> 1) & 1), c),
                device_id_type=pl.DeviceIdType.MESH)
            copy.wait_send()
            copy.wait_recv()

    hbm = pl.BlockSpec(memory_space=pltpu.HBM)
    sem = pl.BlockSpec(memory_space=pltpu.SEMAPHORE)
    buf = pltpu.HBM(p_thru.shape, p_thru.dtype)
    return pl.pallas_call(
        body, name="chip_exchange_wait", out_shape=(buf, buf),
        in_specs=(hbm, hbm) + (sem,) * (2 * n) + (pl.BlockSpec(memory_space=pl.ANY),), out_specs=(hbm, hbm),
        input_output_aliases={0: 0, 1: 1},
        compiler_params=pltpu.CompilerParams(has_side_effects=pltpu.SideEffectType.DATAFLOW_SIDE_EFFECTING))(
            p_thru, land_thru, *sems, after)


def _gather_slab(slab, after):
    def body(s_ref, after_ref, o_ref, *sems):
        _exchange_ops(["gather_rows"], [s_ref], [o_ref], sems, True, True)

    anyspec = pl.BlockSpec(memory_space=pl.ANY)
    return pl.pallas_call(
        body, name="gather_slab", in_specs=[anyspec, anyspec], out_specs=anyspec,
        out_shape=_sds((N_DEV,) + slab.shape, slab.dtype), scratch_shapes=_exchange_sems(1),
        compiler_params=pltpu.CompilerParams(has_side_effects=True))(slab, after)


def _adamw(w, g, m, v):
    m = ADAM_B1 * m + (1.0 - ADAM_B1) * g
    v = ADAM_B2 * v + (1.0 - ADAM_B2) * (g * g)
    m_hat = m / (1.0 - ADAM_B1 ** ADAM_STEP)
    v_hat = v / (1.0 - ADAM_B2 ** ADAM_STEP)
    delta = -ADAM_LR * (m_hat / (jnp.sqrt(v_hat) + ADAM_EPS) + ADAM_WD * w)
    return delta, m, v


def _sum_adamw(recv, w, m, v, name, after=None):
    lead = w.ndim - 2
    rows, cols = w.shape[-2:]
    tr = 256 if rows % 256 == 0 else 128
    slabs = list(recv) if isinstance(recv, tuple) else [recv]
    n_slots = slabs[0].shape[0]
    extra = [] if after is None else [after]

    def body(*refs):
        r_ref, own_ref = refs[0], refs[len(slabs) - 1]
        w_ref, m_ref, v_ref, g_ref, d_ref, nm_ref, nv_ref = refs[len(slabs) + len(extra):]
        chip = (4 * lax.axis_index("x") + 2 * lax.axis_index("y") + lax.axis_index("c")) >> 1
        g = None
        for slot in range(n_slots):
            part = r_ref[slot]
            if len(slabs) == 2:
                part = jnp.where(chip == slot, own_ref[slot], part)
            g = part.astype(F32) if g is None else g + part.astype(F32)
        g_ref[...] = g
        d_ref[...], nm_ref[...], nv_ref[...] = _adamw(w_ref[...], g, m_ref[...], v_ref[...])

    blk = pl.BlockSpec((None,) * lead + (tr, cols), lambda i: (0,) * lead + (i, 0))
    slots = pl.BlockSpec((n_slots, tr, cols), lambda i: (0, i, 0))
    return pl.pallas_call(
        body, name=name, grid=(rows // tr,),
        in_specs=[slots] * len(slabs) + [pl.BlockSpec(a.shape, lambda i: (0, 0)) for a in extra] + [blk, blk, blk],
        out_specs=[blk] * 4, out_shape=[_sds(w.shape, F32)] * 4, compiler_params=_params())(*slabs, *extra, w, m, v)


def _entry_view(a):
    _, rows, cols = a.shape
    return jnp.transpose(a, (2, 0, 1)).reshape(cols, rows // LANES, LANES)


def _from_entry_view(a):
    cols, groups, lanes = a.shape
    return jnp.transpose(a, (1, 2, 0)).reshape(1, groups * lanes, cols)


def _sum_adamw_entry_view(landing, own, w, m, v, name):
    n_slots, rows, cols = landing.shape
    groups = rows // LANES
    cols_pad = -(-cols // LANES) * LANES

    def body(r_ref, own_ref, w_ref, m_ref, v_ref, g_ref, d_ref, nm_ref, nv_ref, pad_ref, gt_ref):
        j = pl.program_id(0)
        chip = (4 * lax.axis_index("x") + 2 * lax.axis_index("y") + lax.axis_index("c")) >> 1
        pad_ref[:, cols_pad - LANES:] = jnp.zeros((LANES, LANES), F32)
        for r0 in range(0, LANES, 32):
            g = None
            for slot in range(n_slots):
                part = jnp.where(chip == slot, own_ref[slot, r0:r0 + 32], r_ref[slot, r0:r0 + 32])
                g = part.astype(F32) if g is None else g + part.astype(F32)
            pad_ref[r0:r0 + 32, :cols] = g
        for c0 in range(0, cols_pad, LANES):
            gt_ref[c0:c0 + LANES, :] = pad_ref[:, c0:c0 + LANES].T
        def update_plane(plane):
            for c0 in range(0, cols, 64):
                n = min(64, cols - c0)
                at = pl.ds(c0 * groups + plane, n, stride=groups)
                gt = gt_ref[c0:c0 + n, :]
                g_ref[at, :] = gt
                d_ref[at, :], nm_ref[at, :], nv_ref[at, :] = _adamw(w_ref[at, :], gt, m_ref[at, :], v_ref[at, :])

        for plane in range(groups):
            pl.when(j == plane)(lambda plane=plane: update_plane(plane))

    flat = lambda a: _entry_view(a).reshape(cols * groups, LANES)
    whole = pl.BlockSpec((cols * groups, LANES), lambda j: (0, 0))
    slots = pl.BlockSpec((n_slots, LANES, cols), lambda j: (0, j, 0))
    outs = pl.pallas_call(
        body, name=name, grid=(groups,), in_specs=[slots, slots, whole, whole, whole], out_specs=[whole] * 4,
        out_shape=[_sds((cols * groups, LANES), F32)] * 4,
        scratch_shapes=[pltpu.VMEM((LANES, cols_pad), F32), pltpu.VMEM((cols_pad, LANES), F32)],
        compiler_params=_params())(landing, own, flat(w), flat(m), flat(v))
    return [_from_entry_view(o.reshape(cols, groups, LANES)) for o in outs]


SLAB_ROWS = 16
SLOT = {"kv_norm_g": (8, 0, D), "norm_b_g": (9, 0, D), "b_forget": (10, 0, 16), "qnorm_a_g": (10, 128, HD),
        "knorm_a_g": (10, 256, HD), "knorm_b_g": (10, 384, HD), "qnorm_b_g": (10, 512, HD), "sinks": (10, 640, 16)}
SMALL = ["norm_a_g", "b_forget", "qnorm_a_g", "knorm_a_g", "kv_norm_g", "knorm_b_g", "norm_b_g", "qnorm_b_g", "sinks"]


LOSS_ROW = 11


def _pack_small(dg_a, dg_kv, dg_b, db_f, dgq_a, dgk_a, dgk_b, dgq_b, dsinks, lsum):
    def fold(ref):
        return ref[:, 0:HD] + ref[:, HD:2 * HD]

    def body(dga_ref, dgkv_ref, dgb_ref, dbf_ref, dgqa_ref, dgka_ref, dgkb_ref, dgqb_ref, dsk_ref, ls_ref, slab_ref):
        slab_ref[...] = jnp.zeros_like(slab_ref)
        for r in range(N_DEV):
            slab_ref[r:r + 1, 0:LANES] = dga_ref[:, LANES * r:LANES * (r + 1)]
        slab_ref[8:9, :] = dgkv_ref[...]
        slab_ref[9:10, :] = dgb_ref[...]
        slab_ref[10:11, 0:LANES] = dbf_ref[...]
        slab_ref[10:11, 128:128 + HD] = fold(dgqa_ref)
        slab_ref[10:11, 256:256 + HD] = fold(dgka_ref)
        slab_ref[10:11, 384:384 + HD] = fold(dgkb_ref)
        slab_ref[10:11, 512:512 + HD] = fold(dgqb_ref)
        slab_ref[10:11, 640:640 + LANES] = dsk_ref[...]
        slab_ref[LOSS_ROW:LOSS_ROW + 1, 0:LANES] = ls_ref[...]

    return pl.pallas_call(body, name="pack_small", out_shape=_sds((SLAB_ROWS, D), F32), compiler_params=_params())(
        dg_a, dg_kv, dg_b, db_f, dgq_a, dgk_a, dgk_b, dgq_b, dsinks, lsum)


def _small_adamw(recv, ws, ms, vs):
    k = len(SMALL)

    def body(*refs):
        r_ref = refs[0]
        w_refs, m_refs, v_refs = refs[1:1 + k], refs[1 + k:1 + 2 * k], refs[1 + 2 * k:1 + 3 * k]
        outs = refs[1 + 3 * k:1 + 7 * k]
        loss_ref, tot = refs[1 + 7 * k], refs[2 + 7 * k]
        g = r_ref[0]
        for dev in range(1, N_DEV):
            g = g + r_ref[dev]
        tot[...] = g
        loss_ref[...] = tot[LOSS_ROW:LOSS_ROW + 1, 0:LANES] * (0.5 / D)
        me = 4 * lax.axis_index("x") + 2 * lax.axis_index("y") + lax.axis_index("c")
        for p, name in enumerate(SMALL):
            if name == "norm_a_g":
                mine = lax.broadcasted_iota(jnp.int32, (N_DEV, LANES), 0) == me
                gp = jnp.sum(jnp.where(mine, tot[0:N_DEV, 0:LANES], 0.0), axis=0, keepdims=True)
            else:
                row, lo, width = SLOT[name]
                gp = tot[row:row + 1, lo:lo + width]
            d, nm, nv = _adamw(w_refs[p][...], gp, m_refs[p][...], v_refs[p][...])
            outs[p][...] = gp
            outs[k + p][...] = d
            outs[2 * k + p][...] = nm
            outs[3 * k + p][...] = nv

    shapes = [_sds(w.shape, F32) for w in ws]
    return pl.pallas_call(body, name="small_adamw", out_shape=shapes * 4 + [_sds((1, LANES), F32)],
                          scratch_shapes=[pltpu.VMEM((SLAB_ROWS, D), F32)],
                          compiler_params=_params())(recv, *ws, *ms, *vs)


def _rope_tables(positions):
    inv_freq = jnp.power(jnp.float32(ROPE_THETA), -jnp.arange(0, ROT, 2, dtype=F32) / ROT)
    ang = positions.astype(F32)[:, None] * inv_freq[None, :]
    cos, sin = jnp.cos(ang), jnp.sin(ang)
    c64 = jnp.concatenate([cos, cos, jnp.ones((S, HD - ROT), F32)], axis=-1)
    s64 = jnp.concatenate([-sin, sin, jnp.zeros((S, HD - ROT), F32)], axis=-1)
    return jnp.tile(c64, (1, 2)), jnp.tile(s64, (1, 2))


def _local_step(x, tgt, positions, g_a, wa, b_forget, gq_a, gk_a, g_kv, gk_b, g_b, gq_b, sinks,
                woa_s, wkv_s, wib_s, wob_s, adamw_others):
    nq = S // TQ
    cos2, sin2 = _rope_tables(positions)
    b_pad = jnp.pad(b_forget, ((0, 0), (0, LANES - N_HEADS)))

    u_a, proj, qn, kn, vb, ccol, cbc = _head_a(x, g_a, wa, gq_a, gk_a, b_pad)
    crow = ccol[:, :N_HEADS].T.reshape(N_HEADS, nq, 1, TQ)
    o_a, z_a, lse_a, woa_g, wkv_g, w_in_b, wob_g = _fox_fwd(
        qn, kn, vb, proj, crow, cbc,
        rider=[("gather_rows", woa_s), ("gather_rows", wkv_s), ("gather_cols", wib_s), ("gather_rows", wob_s)])
    w_out_a, w_kv, w_out_b = woa_g.reshape(D, D), wkv_g.reshape(D, 512), wob_g.reshape(D, D)
    h1, u_kv, u_b, kv, pb, qb, ksh, vsh = _head_b(x, z_a, w_out_a, g_kv, g_b, w_kv, w_in_b, gq_b, gk_b, cos2, sin2)
    sinks1 = sinks.reshape(N_HEADS)
    o_b, z_b, lse_b = _swa_fwd(qb, ksh, vsh, pb, sinks1)
    dy, lsum = _out_b_loss(z_b, w_out_b, h1, tgt)
    dw_out_b = _mm(z_b, dy, "tn", 512, 512, S, out_dtype=BF16, name="mm_dw_out_b")
    dz_b = _mm(dy, w_out_b, "nt", 1024, 512, D, name="mm_dz_b")
    dpb, dka, dkb, dva, dvb, dsinks, dgq_b = _swa_bwd(qb, ksh, vsh, dz_b, o_b, lse_b, pb, sinks1, gq_b, cos2, sin2)
    dkv, dgk_b = _prep_kv_bwd(dka, dkb, dva, dvb, kv, gk_b, cos2, sin2)
    dw_in_b = _mm(u_b, dpb, "tn", 512, 512, S, out_dtype=BF16, name="mm_dw_in_b")
    dw_kv = _mm(u_kv, dkv, "tn", 512, 512, S, out_dtype=BF16, name="mm_dw_kv")
    dh1, dg_b, dg_kv = _du_b_rms_bwd(dpb, w_in_b, dkv, w_kv, h1, g_b, g_kv, dy)
    dw_out_a = _mm(z_a, dh1, "tn", 512, 512, S, out_dtype=BF16, name="mm_dw_out_a")
    do_a, dgate_a, delta_a = _fox_bwd_pre(dh1, w_out_a, proj, o_a)
    dk_a, dv_a, dcs, dq_a, drow, r_wob, r_wib, r_wkv, r_woa = _fox_bwd(
        qn, kn, vb, do_a, lse_a, delta_a, crow, cbc,
        rider=[("a2a_rows", dw_out_b), ("a2a_cols", dw_in_b), ("a2a_rows", dw_kv), ("a2a_rows", dw_out_a)])
    drow_col = jnp.pad(drow.reshape(N_HEADS, S).T, ((0, 0), (0, LANES - N_HEADS)))
    dproj, dgq_a, dgk_a, db_f = _prep_a_bwd(dq_a, dk_a, dv_a, dgate_a, drow_col, dcs, proj, b_pad, gq_a, gk_a)
    dwa = _mm(u_a, dproj, "tn", 1024, 256, S, out_dtype=BF16, name="mm_dw_in_a")
    partial = _pair_reduce(_reshard_dwa(dwa))
    started = _chip_exchange_start(partial)
    dx, dg_a = _du_a_rms_bwd(dproj, wa, x, g_a, dh1, after=started[-1])
    others = adamw_others(dict(w_out_a=r_woa, w_kv=r_wkv, w_in_b=r_wib, w_out_b=r_wob), dg_a)
    partial, landed = _chip_exchange_wait(started, others["w_out_b"][0])
    slab = _pack_small(dg_a, dg_kv, dg_b, db_f, dgq_a, dgk_a, dgk_b, dgq_b, dsinks, lsum)
    return dx, (landed, partial), others, _gather_slab(slab, landed)


def kernel(x, positions, norm_a_g, w_in_a, b_forget, qnorm_a_g, knorm_a_g, w_out_a, kv_norm_g, w_kv, knorm_b_g, norm_b_g, w_in_b, qnorm_b_g, sinks, w_out_b, loss_target, m_norm_a_g, m_w_in_a, m_b_forget, m_qnorm_a_g, m_knorm_a_g, m_w_out_a, m_kv_norm_g, m_w_kv, m_knorm_b_g, m_norm_b_g, m_w_in_b, m_qnorm_b_g, m_sinks, m_w_out_b, v_norm_a_g, v_w_in_a, v_b_forget, v_qnorm_a_g, v_knorm_a_g, v_w_out_a, v_kv_norm_g, v_w_kv, v_knorm_b_g, v_norm_b_g, v_w_in_b, v_qnorm_b_g, v_sinks, v_w_out_b):
    wa_g, ga_g, woa_s, wkv_s, wib_s, wob_s = _gather_first(w_in_a, w_out_a, w_kv, w_in_b, w_out_b, norm_a_g)
    state = dict(w_in_a=(w_in_a, m_w_in_a, v_w_in_a), w_out_a=(w_out_a, m_w_out_a, v_w_out_a),
                 w_kv=(w_kv, m_w_kv, v_w_kv), w_in_b=(w_in_b, m_w_in_b, v_w_in_b),
                 w_out_b=(w_out_b, m_w_out_b, v_w_out_b))

    def adamw_others(landed, after):
        return {n: _sum_adamw(r, *state[n], "adamw_" + n, after=after) for n, r in landed.items()}

    dx, r_wa, big, slab_g = _local_step(
        x[0], loss_target[0], positions, ga_g.reshape(1, D), _unshard_wa(wa_g), b_forget, qnorm_a_g, knorm_a_g,
        kv_norm_g.reshape(1, D), knorm_b_g.reshape(1, HD), norm_b_g, qnorm_b_g, sinks, woa_s, wkv_s, wib_s, wob_s,
        adamw_others)
    big["w_in_a"] = _sum_adamw_entry_view(*r_wa, *state["w_in_a"], "adamw_w_in_a")

    r2 = lambda a: a.reshape(1, -1)
    small_w = dict(norm_a_g=norm_a_g, b_forget=b_forget, qnorm_a_g=qnorm_a_g, knorm_a_g=knorm_a_g,
                   kv_norm_g=kv_norm_g, knorm_b_g=knorm_b_g, norm_b_g=norm_b_g, qnorm_b_g=qnorm_b_g, sinks=sinks)
    small_m = dict(norm_a_g=m_norm_a_g, b_forget=m_b_forget, qnorm_a_g=m_qnorm_a_g, knorm_a_g=m_knorm_a_g,
                   kv_norm_g=m_kv_norm_g, knorm_b_g=m_knorm_b_g, norm_b_g=m_norm_b_g, qnorm_b_g=m_qnorm_b_g,
                   sinks=m_sinks)
    small_v = dict(norm_a_g=v_norm_a_g, b_forget=v_b_forget, qnorm_a_g=v_qnorm_a_g, knorm_a_g=v_knorm_a_g,
                   kv_norm_g=v_kv_norm_g, knorm_b_g=v_knorm_b_g, norm_b_g=v_norm_b_g, qnorm_b_g=v_qnorm_b_g,
                   sinks=v_sinks)
    res = _small_adamw(slab_g, [r2(small_w[n]) for n in SMALL], [r2(small_m[n]) for n in SMALL],
                       [r2(small_v[n]) for n in SMALL])
    k = len(SMALL)
    small = {n: [res[q * k + p].reshape(small_w[n].shape) for q in range(4)] for p, n in enumerate(SMALL)}
    loss = res[4 * k][0, 0]

    order = ["norm_a_g", "w_in_a", "b_forget", "qnorm_a_g", "knorm_a_g", "w_out_a", "kv_norm_g", "w_kv",
             "knorm_b_g", "norm_b_g", "w_in_b", "qnorm_b_g", "sinks", "w_out_b"]

    def leaf(n, q):
        return big[n][q] if n in big else small[n][q]

    outs = [loss, dx[None]]
    for q in range(4):
        outs.extend(leaf(n, q) for n in order)
    return tuple(outs)
```

```python
import jax
import jax.numpy as jnp
from jax import lax
from jax.experimental import pallas as pl
from jax.experimental.pallas import tpu as pltpu

F32, BF16 = jnp.float32, jnp.bfloat16

S = 2048
D = 1024
HD = 64
N_HEADS = 16
N_DEV = 8
NA = 4352
GOFF = 3072
FOFF = 4096
RAW_F = 3072
RAW_G = RAW_F + N_HEADS
NA_RAW = 4112
EPS = 1e-6
QSCALE = 0.125
ROPE_THETA = 500000.0
ROT = 16
WIN = 128
TQ = 256
TK = 256
KS = TQ // 2
HPS = 8
HW = HPS * HD
TM = 256
RT = 256
RB = 512
CB = 256
LANES = 128

ADAM_LR, ADAM_B1, ADAM_B2, ADAM_EPS, ADAM_WD, ADAM_STEP = 0.001, 0.9, 0.999, 1e-08, 0.01, 10

VMEM_LIMIT = 56 * 1024 * 1024


def _params():
    return pltpu.CompilerParams(vmem_limit_bytes=VMEM_LIMIT)


def _sds(shape, dtype):
    return jax.ShapeDtypeStruct(shape, dtype)


def _dot_nt(a, b):
    return lax.dot_general(a, b, (((1,), (1,)), ((), ())), preferred_element_type=F32)


def _dot_tn(a, b):
    return lax.dot_general(a, b, (((0,), (0,)), ((), ())), preferred_element_type=F32)


def _dot_nn(a, b):
    return lax.dot_general(a, b, (((1,), (0,)), ((), ())), preferred_element_type=F32)


def _sigmoid(g):
    return 1.0 / (1.0 + jnp.exp(-g))


def _lane_iota(shape):
    return lax.broadcasted_iota(jnp.int32, shape, len(shape) - 1)


def _flips(kind):
    return (2, 4, 6) if kind == "a2a_chips" else tuple(range(1, N_DEV))


def _send_view(kind, ref, dev):
    if kind in ("gather_rows", "gather_cols"):
        return ref
    if kind == "a2a_slots":
        return ref.at[dev]
    if kind == "a2a_chips":
        return ref.at[dev >> 1]
    if kind == "a2a_rows":
        rows = ref.shape[0] // N_DEV
        return ref.at[pl.ds(pl.multiple_of(dev * rows, rows), rows)]
    cols = ref.shape[1] // N_DEV
    return ref.at[:, pl.ds(pl.multiple_of(dev * cols, cols), cols)]


def _land_view(kind, ref, dev):
    if kind == "gather_cols":
        cols = ref.shape[1] // N_DEV
        return ref.at[:, pl.ds(pl.multiple_of(dev * cols, cols), cols)]
    if kind == "a2a_chips":
        return ref.at[dev >> 1]
    return ref.at[dev]


def _landing_sds(kind, arr):
    if kind == "gather_rows":
        return _sds((N_DEV,) + arr.shape, arr.dtype)
    if kind == "gather_cols":
        return _sds((arr.shape[0], N_DEV * arr.shape[1]), arr.dtype)
    if kind == "a2a_rows":
        return _sds((N_DEV, arr.shape[0] // N_DEV, arr.shape[1]), arr.dtype)
    if kind == "a2a_cols":
        return _sds((N_DEV, arr.shape[0], arr.shape[1] // N_DEV), arr.dtype)
    return _sds(arr.shape, arr.dtype)


def _exchange_sems(n_parts):
    n = n_parts * (N_DEV - 1)
    return [pltpu.SemaphoreType.DMA((n,)), pltpu.SemaphoreType.DMA((n,)), pltpu.SemaphoreType.DMA((n_parts,))]


def _exchange_ops(kinds, srcs, dsts, sems, start, wait):
    send_sems, recv_sems, local_sems = sems
    x, y, c = lax.axis_index("x"), lax.axis_index("y"), lax.axis_index("c")
    me = 4 * x + 2 * y + c

    def local(a):
        return pltpu.make_async_copy(_send_view(kinds[a], srcs[a], me), _land_view(kinds[a], dsts[a], me),
                                     local_sems.at[a])

    def remote(a, k, landing_dev):
        peer = (x ^ ((k >> 2) & 1), y ^ ((k >> 1) & 1), c ^ (k & 1))
        sem = a * (N_DEV - 1) + k - 1
        return pltpu.make_async_remote_copy(
            src_ref=_send_view(kinds[a], srcs[a], me ^ k), dst_ref=_land_view(kinds[a], dsts[a], landing_dev),
            send_sem=send_sems.at[sem], recv_sem=recv_sems.at[sem], device_id=peer,
            device_id_type=pl.DeviceIdType.MESH)

    pairs = [(a, k) for k in range(1, N_DEV) for a in range(len(kinds)) if k in _flips(kinds[a])]
    if start:
        for a in range(len(kinds)):
            local(a).start()
        for a, k in pairs:
            remote(a, k, me).start()
    if wait:
        for a, k in pairs:
            remote(a, k, me ^ k).wait_recv()
            remote(a, k, me).wait_send()
        for a in range(len(kinds)):
            local(a).wait()


def _gather_two_level(srcs, dsts, sems, meanwhile=None):
    send_sems, recv_sems, local_sems = sems
    x, y, c = lax.axis_index("x"), lax.axis_index("y"), lax.axis_index("c")
    me, sibling = (x, y, c), (x, y, 1 - c)
    chips = [(1 - x, y), (x, 1 - y), (1 - x, 1 - y)]

    def slot(ref, dev):
        return ref.at[4 * dev[0] + 2 * dev[1] + dev[2]]

    def copy(a, k, block, to, src=None):
        return pltpu.make_async_remote_copy(
            src_ref=slot(dsts[a], block) if src is None else src, dst_ref=slot(dsts[a], block),
            send_sem=send_sems.at[a * (N_DEV - 1) + k], recv_sem=recv_sems.at[a * (N_DEV - 1) + k],
            device_id=to, device_id_type=pl.DeviceIdType.MESH)

    parts = range(len(srcs))
    mine = [pltpu.make_async_copy(srcs[a], slot(dsts[a], me), local_sems.at[a]) for a in parts]
    first = [copy(a, 0, me, sibling, src=srcs[a]) for a in parts]
    first += [copy(a, 1 + j, me, (*chip, c), src=srcs[a]) for j, chip in enumerate(chips) for a in parts]
    for cp in mine + first:
        cp.start()
    if meanwhile is not None:
        meanwhile()
    passed = []
    for j, chip in enumerate(chips):
        for a in parts:
            copy(a, 1 + j, (*chip, c), me).wait_recv()
            fwd = copy(a, 4 + j, (*chip, c), sibling)
            fwd.start()
            passed.append(fwd)
    for a in parts:
        copy(a, 0, sibling, me).wait_recv()
        for j, chip in enumerate(chips):
            copy(a, 4 + j, (*chip, 1 - c), me).wait_recv()
    for cp in first + passed:
        cp.wait_send()
    for cp in mine:
        cp.wait()


def _call(body, *, name, args, in_specs, out_specs, out_shape, grid=(), scratch_shapes=(), aliases=None, rider=()):
    n_in, n_out, n_scr, n_r = len(in_specs), len(out_specs), len(scratch_shapes), len(rider)
    kinds = [kind for kind, _ in rider]

    def kernel_body(*refs):
        c_in, r_in = refs[:n_in], refs[n_in:n_in + n_r]
        c_out = refs[n_in + n_r:n_in + n_r + n_out]
        r_out = refs[n_in + n_r + n_out:n_in + 2 * n_r + n_out]
        rest = refs[n_in + 2 * n_r + n_out:]
        c_scr, sems = rest[:n_scr], rest[n_scr:]
        if n_r:
            assert grid, "a rider needs a gridded call"
            ids = [pl.program_id(ax) for ax in range(len(grid))]
            first, last = ids[0] == 0, ids[0] == grid[0] - 1
            for pid, size in zip(ids[1:], grid[1:]):
                first = first & (pid == 0)
                last = last & (pid == size - 1)
            pl.when(first)(lambda: _exchange_ops(kinds, r_in, r_out, sems, True, False))
        body(*c_in, *c_out, *c_scr)
        if n_r:
            pl.when(last)(lambda: _exchange_ops(kinds, r_in, r_out, sems, False, True))

    anyspec = pl.BlockSpec(memory_space=pl.ANY)
    params = pltpu.CompilerParams(vmem_limit_bytes=VMEM_LIMIT, has_side_effects=bool(n_r))
    outs = pl.pallas_call(
        kernel_body, name=name, grid=grid, in_specs=list(in_specs) + [anyspec] * n_r,
        out_specs=list(out_specs) + [anyspec] * n_r,
        out_shape=list(out_shape) + [_landing_sds(kind, arr) for kind, arr in rider],
        scratch_shapes=list(scratch_shapes) + (_exchange_sems(n_r) if n_r else []),
        input_output_aliases=aliases or {}, compiler_params=params)(*args, *[arr for _, arr in rider])
    return list(outs)


def _mm(a, b, mode, tm, tn, tk, out_dtype=F32, add=None, name="mm", rider=()):
    if mode == "nn":
        (m, k), n = a.shape, b.shape[1]
        a_spec = pl.BlockSpec((tm, tk), lambda i, j, kk: (i, kk))
        b_spec = pl.BlockSpec((tk, tn), lambda i, j, kk: (kk, j))
        dot = _dot_nn
    elif mode == "nt":
        (m, k), n = a.shape, b.shape[0]
        a_spec = pl.BlockSpec((tm, tk), lambda i, j, kk: (i, kk))
        b_spec = pl.BlockSpec((tn, tk), lambda i, j, kk: (j, kk))
        dot = _dot_nt
    else:
        (k, m), n = a.shape, b.shape[1]
        a_spec = pl.BlockSpec((tk, tm), lambda i, j, kk: (kk, i))
        b_spec = pl.BlockSpec((tk, tn), lambda i, j, kk: (kk, j))
        dot = _dot_tn
    assert m % tm == 0 and n % tn == 0 and k % tk == 0, (m, n, k, tm, tn, tk)
    nk = k // tk
    has_add = add is not None

    def body(*refs):
        if has_add:
            a_ref, b_ref, add_ref, o_ref, acc = refs
        else:
            a_ref, b_ref, o_ref, acc = refs
        p = dot(a_ref[...].astype(BF16), b_ref[...].astype(BF16))

        def finish(total):
            if has_add:
                total = add_ref[...] + total
            o_ref[...] = total.astype(out_dtype)

        if nk == 1:
            finish(p)
        else:
            kk = pl.program_id(2)

            @pl.when(kk == 0)
            def _():
                acc[...] = p

            @pl.when(kk > 0)
            def _():
                acc[...] += p

            @pl.when(kk == nk - 1)
            def _():
                finish(acc[...])

    in_specs = [a_spec, b_spec]
    args = [a, b]
    if has_add:
        in_specs.append(pl.BlockSpec((tm, tn), lambda i, j, kk: (i, j)))
        args.append(add)
    acc_shape = (tm, tn) if nk > 1 else (8, LANES)
    outs = _call(body, name=name, args=args, grid=(m // tm, n // tn, nk), in_specs=in_specs,
                 out_specs=[pl.BlockSpec((tm, tn), lambda i, j, kk: (i, j))], out_shape=[_sds((m, n), out_dtype)],
                 scratch_shapes=[pltpu.VMEM(acc_shape, F32)], rider=rider)
    return outs if rider else outs[0]


def _rms_rinv(x):
    return lax.rsqrt(jnp.mean(x * x, axis=-1, keepdims=True) + EPS)


def _rms_bwd_core(du, x, g):
    r = _rms_rinv(x)
    dug = du * g
    dx = r * (dug - x * ((r * r) * jnp.mean(dug * x, axis=-1, keepdims=True)))
    dg = jnp.sum(du * (x * r), axis=0, keepdims=True)
    return dx, dg


def _half_ones():
    r = lax.broadcasted_iota(jnp.int32, (LANES, LANES), 0)
    c = lax.broadcasted_iota(jnp.int32, (LANES, LANES), 1)
    return ((r < HD) == (c < HD)).astype(BF16)


def _half_sum(v, lo_half):
    if lo_half.dtype == jnp.bool_:
        s0 = jnp.sum(jnp.where(lo_half, v, 0.0), axis=-1, keepdims=True)
        s1 = jnp.sum(jnp.where(lo_half, 0.0, v), axis=-1, keepdims=True)
        return jnp.where(lo_half, s0, s1)
    hi = v.astype(BF16)
    lo = (v - hi.astype(F32)).astype(BF16)
    return _dot_nn(hi, lo_half) + _dot_nn(lo, lo_half)


def _head_rinv(x, lo_half):
    return lax.rsqrt(_half_sum(x * x, lo_half) * (1.0 / HD) + EPS)


def _head_norm_bwd(dn, x, g, lo_half):
    r = _head_rinv(x, lo_half)
    dng = dn * g
    dx = r * (dng - x * ((r * r) * (_half_sum(dng * x, lo_half) * (1.0 / HD))))
    dg = jnp.sum(dn * (x * r), axis=0, keepdims=True)
    return dx, dg


def _rope_swap(x, lane):
    l64 = lane & (HD - 1)
    return jnp.where(l64 < ROT // 2, pltpu.roll(x, LANES - ROT // 2, 1), pltpu.roll(x, ROT // 2, 1))


def _rope_fwd(x, cos, sin, lane):
    return x * cos + _rope_swap(x, lane) * sin


def _rope_bwd(dy, cos, sin, lane):
    return dy * cos + jnp.where((lane & (HD - 1)) < ROT, _rope_swap(dy * sin, lane), 0.0)


def _g2(g_ref):
    g = g_ref[...]
    return jnp.concatenate([g, g], axis=-1)


def _pairs(width):
    return [slice(LANES * c, LANES * (c + 1)) for c in range(width // LANES)]


def _pick_lane(block, lane, idx):
    return jnp.sum(jnp.where(lane == idx, block, 0.0), axis=-1, keepdims=True)


def _head_a(x, g, wa, gq, gk, b_pad):
    def body(x_ref, g_ref, w_ref, gq_ref, gk_ref, b_ref, u_ref, p_ref, qo_ref, ko_ref, vo_ref, c_ref, cbc_ref, carry):
        @pl.when(pl.program_id(0) == 0)
        def _():
            carry[...] = jnp.zeros_like(carry)

        xv = x_ref[...]
        u = ((xv * _rms_rinv(xv)) * g_ref[...]).astype(BF16)
        u_ref[...] = u
        for lo in range(0, NA, D):
            hi = min(lo + D, NA)
            p_ref[:, lo:hi] = _dot_nn(u, w_ref[:, lo:hi])
        lane = _lane_iota((RT, LANES))
        lo_half = lane < HD
        gq2, gk2 = _g2(gq_ref), _g2(gk_ref)
        for c in _pairs(D):
            q = p_ref[:, c]
            k = p_ref[:, D + c.start:D + c.stop]
            qo_ref[:, c] = (((q * _head_rinv(q, lo_half)) * gq2) * QSCALE).astype(BF16)
            ko_ref[:, c] = ((k * _head_rinv(k, lo_half)) * gk2).astype(BF16)
        vo_ref[...] = p_ref[:, 2 * D:3 * D].astype(BF16)

        z = p_ref[:, FOFF:FOFF + LANES] + b_ref[...]
        logf = jnp.minimum(z, 0.0) - jnp.log1p(jnp.exp(-jnp.abs(z)))
        r = lax.broadcasted_iota(jnp.int32, (RT, RT), 0)
        cc = lax.broadcasted_iota(jnp.int32, (RT, RT), 1)
        tri = (r >= cc).astype(F32)
        loc = jnp.dot(tri, logf, precision=lax.Precision.HIGHEST, preferred_element_type=F32) + carry[0:1, :]
        c_ref[...] = loc
        carry[0:1, :] = loc[RT - 1:RT, :]
        for h in range(N_HEADS):
            cbc_ref[:, LANES * h:LANES * (h + 1)] = jnp.broadcast_to(_pick_lane(loc, lane, h), (RT, LANES))

    row = lambda width: pl.BlockSpec((RT, width), lambda i: (i, 0))
    whole = lambda arr: pl.BlockSpec(arr.shape, lambda i: (0,) * arr.ndim, pipeline_mode=pl.Buffered(1))
    return pl.pallas_call(
        body, name="head_a", grid=(S // RT,),
        in_specs=[row(D), whole(g), whole(wa), whole(gq), whole(gk), whole(b_pad)],
        out_specs=[row(D), row(NA), row(D), row(D), row(D), row(LANES), row(N_HEADS * LANES)],
        out_shape=[_sds((S, D), BF16), _sds((S, NA), F32)] + [_sds((S, D), BF16)] * 3
        + [_sds((S, LANES), F32), _sds((S, N_HEADS * LANES), F32)],
        scratch_shapes=[pltpu.VMEM((8, LANES), F32)], compiler_params=_params())(x, g, wa, gq, gk, b_pad)


def _key_le_query(offset, keys=TK):
    r = lax.broadcasted_iota(jnp.int32, (keys, TQ), 0)
    c = lax.broadcasted_iota(jnp.int32, (keys, TQ), 1)
    return (r + offset) <= c


def _widen(tile):
    return jnp.concatenate([tile] * (TQ // LANES), axis=1)


def _fox_fwd(qn, kn, vb, proj, crow, cbc, rider=()):
    nq = S // TQ

    def body(q_ref, k_ref, v_ref, g_ref, cq_ref, cbc_ref, o_ref, z_ref, lse_ref, st_s, pt_s):
        i = pl.program_id(1)
        qs = [q_ref[:, HD * hh:HD * (hh + 1)] for hh in range(HPS)]
        cqs = [cq_ref[hh, 0] for hh in range(HPS)]

        def scores(s, hh):
            off = pl.multiple_of(s * KS, KS)
            kj = k_ref[pl.ds(off, KS), HD * hh:HD * (hh + 1)]
            return (_dot_nt(kj, qs[hh]) + cqs[hh]) - _widen(cbc_ref[pl.ds(off, KS), LANES * hh:LANES * (hh + 1)])

        def values(s, hh, pt):
            off = pl.multiple_of(s * KS, KS)
            return _dot_tn(v_ref[pl.ds(off, KS), HD * hh:HD * (hh + 1)], pt)

        def step(s, slot, carries, mask=None, last=False):
            if not last:
                for hh in range(HPS):
                    st_s[1 - slot, hh] = scores(s + 1, hh)
            pvs = [values(jnp.maximum(s - 1, 0), hh, pt_s[1 - slot, hh]) for hh in range(HPS)]
            out = []
            for hh in range(HPS):
                m, l, acc = carries[hh]
                st = st_s[slot, hh]
                if mask is not None:
                    st = jnp.where(mask, st, -jnp.inf)
                m_new = jnp.maximum(m, jnp.max(st, axis=0, keepdims=True))
                pt = jnp.exp(st - m_new)
                alpha = jnp.exp(m - m_new)
                pt_s[slot, hh] = pt.astype(BF16)
                out.append((m_new, alpha * l + jnp.sum(pt, axis=0, keepdims=True), alpha * (acc + pvs[hh])))
            return tuple(out)

        for hh in range(HPS):
            st_s[0, hh] = scores(0, hh)
            pt_s[1, hh] = jnp.zeros((KS, TQ), BF16)
        one = (jnp.full((1, TQ), -jnp.inf, F32), jnp.zeros((1, TQ), F32), jnp.zeros((HD, TQ), F32))
        carries = lax.fori_loop(0, i, lambda t, cr: step(2 * t + 1, 1, step(2 * t, 0, cr)), (one,) * HPS)
        carries = step(2 * i, 0, carries, mask=_key_le_query(0, KS))
        carries = step(2 * i + 1, 1, carries, mask=_key_le_query(KS, KS), last=True)
        accs = []
        for hh in range(HPS):
            m, l, acc = carries[hh]
            acc = acc + values(2 * i + 1, hh, pt_s[1, hh])
            accs.append(acc / l)
            lse_ref[hh, 0] = m + jnp.log(l)
        o = jnp.concatenate(accs, axis=0).T
        o_ref[...] = o
        g = g_ref[...]
        z_ref[...] = (o * (g * _sigmoid(g))).astype(BF16)

    qblk = pl.BlockSpec((TQ, HW), lambda hp, i: (i, hp))
    full = pl.BlockSpec((S, HW), lambda hp, i: (0, hp))
    rows = pl.BlockSpec((HPS, 1, 1, TQ), lambda hp, i: (hp, i, 0, 0))
    return _call(
        body, name="fox_fwd", args=(qn, kn, vb, proj, crow, cbc), grid=(N_HEADS // HPS, nq),
        in_specs=[qblk, full, full,
                  pl.BlockSpec((TQ, HW), lambda hp, i: (i, GOFF // HW + hp)),
                  rows, pl.BlockSpec((S, HPS * LANES), lambda hp, i: (0, hp))],
        out_specs=[qblk, qblk, rows],
        out_shape=[_sds((S, D), F32), _sds((S, D), BF16), _sds((N_HEADS, nq, 1, TQ), F32)],
        scratch_shapes=[pltpu.VMEM((2, HPS, KS, TQ), F32), pltpu.VMEM((2, HPS, KS, TQ), BF16)], rider=rider)


def _fox_bwd_pre(dh, w_out, proj, o):
    nq, rows = S // TQ, 2 * TQ
    per = rows // TQ

    def body(dh_ref, w_ref, g_ref, o_ref, do_ref, dg_ref, delta_ref):
        g = g_ref[...]
        sg = _sigmoid(g)
        dzv = _dot_nt(dh_ref[...].astype(BF16), w_ref[...])
        ov = o_ref[...]
        do = dzv * (g * sg)
        dg_ref[...] = (dzv * ov * (sg * (1.0 + g * (1.0 - sg)))).astype(BF16)
        do_ref[...] = do.astype(BF16)
        prod_t = (do * ov).T
        for h in range(N_HEADS):
            for b in range(per):
                delta_ref[h, b] = jnp.sum(prod_t[HD * h:HD * (h + 1), TQ * b:TQ * (b + 1)], axis=0, keepdims=True)

    row = pl.BlockSpec((rows, D), lambda i: (i, 0))
    return pl.pallas_call(
        body, name="fox_bwd_pre", grid=(S // rows,),
        in_specs=[row, pl.BlockSpec(w_out.shape, lambda i: (0, 0), pipeline_mode=pl.Buffered(1)),
                  pl.BlockSpec((rows, D), lambda i: (i, GOFF // D)), row],
        out_specs=[row, row, pl.BlockSpec((N_HEADS, per, 1, TQ), lambda i: (0, i, 0, 0))],
        out_shape=[_sds((S, D), BF16), _sds((S, D), BF16), _sds((N_HEADS, nq, 1, TQ), F32)],
        compiler_params=_params())(dh, w_out, proj, o)


def _fox_bwd(qn, kn, vb, dob, lse, delta, crow, cbc, rider=()):
    nq, nkb = S // TQ, S // TK

    def body(q_ref, k_ref, v_ref, do_ref, lse_ref, del_ref, cq_ref, cbc_ref,
             dk_ref, dv_ref, dcs_ref, dq_ref, dr_ref, st_s, dp_s, pt_s, ds_s, dq_acc, dr_acc):
        j = pl.program_id(1)

        @pl.when(j == 0)
        def _():
            dq_acc[...] = jnp.zeros_like(dq_acc)
            dr_acc[...] = jnp.zeros_like(dr_acc)

        kjs = [k_ref[:, HD * hh:HD * (hh + 1)] for hh in range(HPS)]
        vjs = [v_ref[:, HD * hh:HD * (hh + 1)] for hh in range(HPS)]

        def rows_of(ref, u, hh):
            off = pl.multiple_of(u * TQ, TQ)
            return ref[pl.ds(off, TQ), HD * hh:HD * (hh + 1)]

        def products(u, hh):
            st = (_dot_nt(kjs[hh], rows_of(q_ref, u, hh)) + cq_ref[hh, u]) - _widen(
                cbc_ref[:, LANES * hh:LANES * (hh + 1)])
            return st, _dot_nt(vjs[hh], rows_of(do_ref, u, hh))

        def step(u, slot, carries, masked=False):
            nxt = jnp.minimum(u + 1, nq - 1)
            for hh in range(HPS):
                st_s[1 - slot, hh], dp_s[1 - slot, hh] = products(nxt, hh)
            prev = jnp.maximum(u - 1, 0)
            dvs = [_dot_nn(pt_s[1 - slot, hh], rows_of(do_ref, prev, hh)) for hh in range(HPS)]
            dks = [_dot_nn(ds_s[1 - slot, hh], rows_of(q_ref, prev, hh)) for hh in range(HPS)]
            for hh in range(HPS):
                dq_acc[hh, prev] += _dot_tn(kjs[hh], ds_s[1 - slot, hh])
            out = []
            for hh in range(HPS):
                dk, dv, dcs = carries[hh]
                st = st_s[slot, hh]
                if masked:
                    st = jnp.where(_key_le_query((j - u) * TQ), st, -jnp.inf)
                pt = jnp.exp(st - lse_ref[hh, u])
                dst = pt * (dp_s[slot, hh] - del_ref[hh, u])
                pt_s[slot, hh] = pt.astype(BF16)
                ds_s[slot, hh] = dst.astype(BF16)
                dr_acc[hh, u] += jnp.sum(dst, axis=0, keepdims=True)
                out.append((dk + dks[hh], dv + dvs[hh], dcs + (dst[:, :LANES] + dst[:, LANES:])))
            return tuple(out)

        t0 = j // 2
        for hh in range(HPS):
            st_s[0, hh], dp_s[0, hh] = products(2 * t0, hh)
            pt_s[1, hh] = jnp.zeros((TK, TQ), BF16)
            ds_s[1, hh] = jnp.zeros((TK, TQ), BF16)
        one = (jnp.zeros((TK, HD), F32), jnp.zeros((TK, HD), F32), jnp.zeros((TK, LANES), F32))
        carries = step(2 * t0 + 1, 1, step(2 * t0, 0, (one,) * HPS, masked=True), masked=True)
        carries = lax.fori_loop(t0 + 1, nq // 2, lambda t, cr: step(2 * t + 1, 1, step(2 * t, 0, cr)), carries)
        dks, dvs = [], []
        lane = _lane_iota((TK, LANES))
        dcs_all = jnp.zeros((TK, LANES), F32)
        for hh in range(HPS):
            dk, dv, dcs = carries[hh]
            dks.append(dk + _dot_nn(ds_s[1, hh], rows_of(q_ref, nq - 1, hh)))
            dvs.append(dv + _dot_nn(pt_s[1, hh], rows_of(do_ref, nq - 1, hh)))
            dq_acc[hh, nq - 1] += _dot_tn(kjs[hh], ds_s[1, hh])
            dcs_all = jnp.where(lane == HPS * pl.program_id(0) + hh, -jnp.sum(dcs, axis=-1, keepdims=True), dcs_all)
        dcs_ref[0] = dcs_all
        dk_ref[...] = jnp.concatenate(dks, axis=-1)
        dv_ref[...] = jnp.concatenate(dvs, axis=-1).astype(BF16)

        @pl.when(j == nkb - 1)
        def _():
            for i in range(nq):
                dq_ref[TQ * i:TQ * (i + 1), :] = jnp.concatenate([dq_acc[hh, i] for hh in range(HPS)], axis=0).T
            dr_ref[...] = dr_acc[...]

    kblk = pl.BlockSpec((TK, HW), lambda hp, j: (j, hp))
    full = pl.BlockSpec((S, HW), lambda hp, j: (0, hp))
    rows = pl.BlockSpec((HPS, nq, 1, TQ), lambda hp, j: (hp, 0, 0, 0))
    cblk = pl.BlockSpec((TK, HPS * LANES), lambda hp, j: (j, hp))
    return _call(
        body, name="fox_bwd", args=(qn, kn, vb, dob, lse, delta, crow, cbc), grid=(N_HEADS // HPS, nkb),
        in_specs=[full, kblk, kblk, full, rows, rows, rows, cblk],
        out_specs=[kblk, kblk, pl.BlockSpec((1, TK, LANES), lambda hp, j: (hp, j, 0)), full, rows],
        out_shape=[_sds((S, D), F32), _sds((S, D), BF16), _sds((N_HEADS // HPS, S, LANES), F32), _sds((S, D), F32),
                   _sds((N_HEADS, nq, 1, TQ), F32)],
        scratch_shapes=[pltpu.VMEM((2, HPS, TK, TQ), F32), pltpu.VMEM((2, HPS, TK, TQ), F32),
                        pltpu.VMEM((2, HPS, TK, TQ), BF16), pltpu.VMEM((2, HPS, TK, TQ), BF16),
                        pltpu.VMEM((HPS, nq, HD, TQ), F32), pltpu.VMEM((HPS, nq, 1, TQ), F32)], rider=rider)


def _prep_a_bwd(dq, dk, dv, dgate, drow, dcs, proj, b_pad, gq, gk):
    nt = S // TM

    def body(dq_ref, dk_ref, dv_ref, dgt_ref, dr_ref, dcs_ref, xq_ref, xk_ref, f_ref, b_ref, gq_ref, gk_ref,
             o_ref, dgq_ref, dgk_ref, db_ref, carry):
        @pl.when(pl.program_id(0) == 0)
        def _():
            carry[...] = jnp.zeros_like(carry)
            dgq_ref[...] = jnp.zeros_like(dgq_ref)
            dgk_ref[...] = jnp.zeros_like(dgk_ref)
            db_ref[...] = jnp.zeros_like(db_ref)

        lane = _lane_iota((TM, LANES))
        lo_half = _half_ones()
        gq2, gk2 = _g2(gq_ref), _g2(gk_ref)
        dgq, dgk = jnp.zeros((1, LANES), F32), jnp.zeros((1, LANES), F32)
        for c in _pairs(D):
            dxq, dg = _head_norm_bwd(dq_ref[:, c] * QSCALE, xq_ref[:, c], gq2, lo_half)
            o_ref[:, c] = dxq.astype(BF16)
            dgq = dgq + dg
            dxk, dg = _head_norm_bwd(dk_ref[:, c], xk_ref[:, c], gk2, lo_half)
            o_ref[:, D + c.start:D + c.stop] = dxk.astype(BF16)
            dgk = dgk + dg
        dgq_ref[...] += dgq
        dgk_ref[...] += dgk
        o_ref[:, 2 * D:3 * D] = dv_ref[...]
        o_ref[:, GOFF:GOFF + D] = dgt_ref[...]

        dc = dr_ref[...]
        for group in range(N_HEADS // HPS):
            dc = dc + dcs_ref[group]
        r = lax.broadcasted_iota(jnp.int32, (TM, TM), 0)
        c = lax.broadcasted_iota(jnp.int32, (TM, TM), 1)
        tri = (c >= r).astype(F32)
        dlogf = jnp.dot(tri, dc, precision=lax.Precision.HIGHEST, preferred_element_type=F32) + carry[0:1, :]
        carry[0:1, :] = dlogf[0:1, :]
        df = dlogf * (1.0 / (1.0 + jnp.exp(f_ref[...] + b_ref[...])))
        db_ref[...] += jnp.sum(df, axis=0, keepdims=True)
        o_ref[:, FOFF:FOFF + LANES] = df.astype(BF16)
        o_ref[:, FOFF + LANES:NA] = jnp.zeros((TM, NA - FOFF - LANES), BF16)

    rev = lambda width, col: pl.BlockSpec((TM, width), lambda i: (nt - 1 - i, col))
    gspec = pl.BlockSpec((1, HD), lambda i: (0, 0))
    acc = pl.BlockSpec((1, LANES), lambda i: (0, 0))
    return pl.pallas_call(
        body, name="prep_a_bwd", grid=(nt,),
        in_specs=[rev(D, 0), rev(D, 0), rev(D, 0), rev(D, 0), rev(LANES, 0),
                  pl.BlockSpec((N_HEADS // HPS, TM, LANES), lambda i: (0, nt - 1 - i, 0)),
                  rev(D, 0), rev(D, 1), rev(LANES, FOFF // LANES), acc, gspec, gspec],
        out_specs=[rev(NA, 0), acc, acc, acc],
        out_shape=[_sds((S, NA), BF16)] + [_sds((1, LANES), F32)] * 3,
        scratch_shapes=[pltpu.VMEM((8, LANES), F32)],
        compiler_params=_params())(dq, dk, dv, dgate, drow, dcs, proj, proj, proj, b_pad, gq, gk)


def _head_b(x, z_a, w_out_a, g_kv, g_b, w_kv, w_in_b, gq, gk, cos2, sin2):
    nkv = w_kv.shape[1] // 2

    def body(x_ref, z_ref, wo_ref, gkv_ref, gb_ref, wkv_ref, wb_ref, gq_ref, gk_ref, c_ref, s_ref,
             h_ref, ukv_ref, ub_ref, kv_ref, pb_ref, qo_ref, ko_ref, vo_ref):
        xv = x_ref[...] + _dot_nn(z_ref[...], wo_ref[...])
        h_ref[...] = xv
        xn = xv * _rms_rinv(xv)
        ukv = (xn * gkv_ref[...]).astype(BF16)
        ub = (xn * gb_ref[...]).astype(BF16)
        ukv_ref[...] = ukv
        ub_ref[...] = ub
        kv_ref[...] = _dot_nn(ukv, wkv_ref[...])
        for lo in range(0, 2 * D, D):
            pb_ref[:, lo:lo + D] = _dot_nn(ub, wb_ref[:, lo:lo + D])
        lane = _lane_iota((RT, LANES))
        lo_half = lane < HD
        cos, sin = c_ref[...], s_ref[...]
        gq2, gk2 = _g2(gq_ref), _g2(gk_ref)
        for c in _pairs(D):
            q = pb_ref[:, c]
            qo_ref[:, c] = (_rope_fwd((q * _head_rinv(q, lo_half)) * gq2, cos, sin, lane) * QSCALE).astype(BF16)
        for c in _pairs(nkv):
            k = kv_ref[:, c]
            ko_ref[:, c] = _rope_fwd((k * _head_rinv(k, lo_half)) * gk2, cos, sin, lane).astype(BF16)
        vo_ref[...] = kv_ref[:, nkv:2 * nkv].astype(BF16)

    row = lambda width: pl.BlockSpec((RT, width), lambda i: (i, 0))
    whole = lambda arr: pl.BlockSpec(arr.shape, lambda i: (0,) * arr.ndim, pipeline_mode=pl.Buffered(1))
    return pl.pallas_call(
        body, name="head_b", grid=(S // RT,),
        in_specs=[row(D), row(D), whole(w_out_a), whole(g_kv), whole(g_b), whole(w_kv), whole(w_in_b), whole(gq),
                  whole(gk), row(LANES), row(LANES)],
        out_specs=[row(D), row(D), row(D), row(2 * nkv), row(2 * D), row(D), row(nkv), row(nkv)],
        out_shape=[_sds((S, D), F32), _sds((S, D), BF16), _sds((S, D), BF16), _sds((S, 2 * nkv), F32),
                   _sds((S, 2 * D), F32), _sds((S, D), BF16), _sds((S, nkv), BF16), _sds((S, nkv), BF16)],
        compiler_params=_params())(x, z_a, w_out_a, g_kv, g_b, w_kv, w_in_b, gq, gk, cos2, sin2)


N_KV, GRP = 4, 4


def _swa_mask(n):
    r = lax.broadcasted_iota(jnp.int32, (2 * WIN, GRP * WIN), 0)
    q = lax.broadcasted_iota(jnp.int32, (2 * WIN, GRP * WIN), 1) & (WIN - 1)
    return (r > q) & (r <= q + WIN) & ((r >= WIN) | (n > 0))


def _stack4(ref_or_val, base):
    return jnp.concatenate([ref_or_val[:, base + HD * g: base + HD * (g + 1)] for g in range(GRP)], axis=0)


def _unstack4(xt):
    return jnp.concatenate([xt[:, WIN * g:WIN * (g + 1)] for g in range(GRP)], axis=0).T


def _band(prev_ref, cur_ref, kh):
    return jnp.concatenate([prev_ref[:, HD * kh:HD * (kh + 1)], cur_ref[:, HD * kh:HD * (kh + 1)]], axis=0)


def _sink_row(s_ref, first):
    lane = _lane_iota((1, GRP * WIN))
    row = jnp.full((1, GRP * WIN), s_ref[first + GRP - 1], F32)
    for g in range(GRP - 2, -1, -1):
        row = jnp.where(lane < WIN * (g + 1), s_ref[first + g], row)
    return row


def _swa_fwd(qb, ksh, vsh, pb, sinks):
    nb = S // WIN

    def body(q_ref, kp_ref, kc_ref, vp_ref, vc_ref, g_ref, s_ref, o_ref, z_ref, lse_ref):
        n = pl.program_id(0)
        valid = _swa_mask(n)
        outs = []
        for kh in range(N_KV):
            kb, vb = _band(kp_ref, kc_ref, kh), _band(vp_ref, vc_ref, kh)
            st = jnp.where(valid, _dot_nt(kb, _stack4(q_ref, GRP * HD * kh)), -jnp.inf)
            sink = _sink_row(s_ref, GRP * kh)
            m = jnp.maximum(jnp.max(st, axis=0, keepdims=True), sink)
            pt = jnp.exp(st - m)
            l = jnp.sum(pt, axis=0, keepdims=True) + jnp.exp(sink - m)
            outs.append(_unstack4(_dot_tn(vb, pt.astype(BF16)) / l))
            lse = m + jnp.log(l)
            for g in range(GRP):
                lse_ref[GRP * kh + g, 0] = lse[:, WIN * g:WIN * (g + 1)]
        o = jnp.concatenate(outs, axis=-1)
        o_ref[...] = o
        g = g_ref[...]
        z_ref[...] = (o * (g * _sigmoid(g))).astype(BF16)

    row = pl.BlockSpec((WIN, D), lambda n: (n, 0))
    prev = pl.BlockSpec((WIN, N_KV * HD), lambda n: (jnp.maximum(n - 1, 0), 0))
    cur = pl.BlockSpec((WIN, N_KV * HD), lambda n: (n, 0))
    return pl.pallas_call(
        body, name="swa_fwd", grid=(nb,),
        in_specs=[row, prev, cur, prev, cur, pl.BlockSpec((WIN, D), lambda n: (n, 1)),
                  pl.BlockSpec(memory_space=pltpu.SMEM)],
        out_specs=[row, row, pl.BlockSpec((N_HEADS, 1, 1, WIN), lambda n: (0, n, 0, 0))],
        out_shape=[_sds((S, D), F32), _sds((S, D), BF16), _sds((N_HEADS, nb, 1, WIN), F32)],
        compiler_params=_params())(qb, ksh, ksh, vsh, vsh, pb, sinks)


def _swa_bwd(qb, ksh, vsh, dz, o, lse, pb, sinks, gq, cos2, sin2):
    nb = S // WIN

    def body(q_ref, kp_ref, kc_ref, vp_ref, vc_ref, dz_ref, o_ref, lse_ref, x_ref, g_ref, s_ref, gq_ref, c_ref, sn_ref,
             dpb_ref, dka_ref, dkb_ref, dva_ref, dvb_ref, dsink_ref, dgq_ref):
        n = pl.program_id(0)

        @pl.when(n == 0)
        def _():
            dsink_ref[...] = jnp.zeros_like(dsink_ref)
            dgq_ref[...] = jnp.zeros_like(dgq_ref)

        valid = _swa_mask(n)
        g = g_ref[...]
        sg = _sigmoid(g)
        dzv = dz_ref[...]
        ov = o_ref[...]
        do = dzv * (g * sg)
        dpb_ref[:, D:2 * D] = (dzv * ov * (sg * (1.0 + g * (1.0 - sg)))).astype(BF16)
        prod_t = (do * ov).T
        lane1 = _lane_iota((1, LANES))
        dqs, dkas, dkbs, dvas, dvbs = [], [], [], [], []
        dsink = jnp.zeros((1, LANES), F32)
        for kh in range(N_KV):
            kb, vb = _band(kp_ref, kc_ref, kh), _band(vp_ref, vc_ref, kh)
            base = GRP * HD * kh
            qs = _stack4(q_ref, base)
            dos = _stack4(do, base).astype(BF16)
            delta = jnp.concatenate(
                [jnp.sum(prod_t[base + HD * gg:base + HD * (gg + 1), :], axis=0, keepdims=True)
                 for gg in range(GRP)], axis=1)
            lse = jnp.concatenate([lse_ref[GRP * kh + gg, 0] for gg in range(GRP)], axis=1)
            st = jnp.where(valid, _dot_nt(kb, qs), -jnp.inf)
            pt = jnp.exp(st - lse)
            dst = pt * (_dot_nt(vb, dos) - delta)
            dsb = dst.astype(BF16)
            dqs.append(_unstack4(_dot_tn(kb, dsb)))
            dkband = _dot_nn(dsb, qs)
            dvband = _dot_nn(pt.astype(BF16), dos)
            dkbs.append(dkband[0:WIN, :])
            dkas.append(dkband[WIN:2 * WIN, :])
            dvbs.append(dvband[0:WIN, :])
            dvas.append(dvband[WIN:2 * WIN, :])
            ps_delta = jnp.exp(_sink_row(s_ref, GRP * kh) - lse) * delta
            for gg in range(GRP):
                val = jnp.sum(ps_delta[:, WIN * gg:WIN * (gg + 1)], axis=1, keepdims=True)
                dsink = dsink - jnp.where(lane1 == GRP * kh + gg, val, 0.0)
        dka_ref[...] = jnp.concatenate(dkas, axis=-1)
        dkb_ref[...] = jnp.concatenate(dkbs, axis=-1)
        dva_ref[...] = jnp.concatenate(dvas, axis=-1)
        dvb_ref[...] = jnp.concatenate(dvbs, axis=-1)
        dsink_ref[...] += dsink

        lane = _lane_iota((WIN, LANES))
        g2, cos, sin = _g2(gq_ref), c_ref[...], sn_ref[...]
        lo_half = _half_ones()
        dg_tot = jnp.zeros((1, LANES), F32)
        for kh in range(N_KV):
            for c in _pairs(GRP * HD):
                cols = slice(GRP * HD * kh + c.start, GRP * HD * kh + c.stop)
                dn = _rope_bwd(dqs[kh][:, c] * QSCALE, cos, sin, lane)
                dx, dg = _head_norm_bwd(dn, x_ref[:, cols], g2, lo_half)
                dpb_ref[:, cols] = dx.astype(BF16)
                dg_tot = dg_tot + dg
        dgq_ref[...] += dg_tot

    row = pl.BlockSpec((WIN, D), lambda n: (n, 0))
    prev = pl.BlockSpec((WIN, N_KV * HD), lambda n: (jnp.maximum(n - 1, 0), 0))
    cur = pl.BlockSpec((WIN, N_KV * HD), lambda n: (n, 0))
    acc = pl.BlockSpec((1, LANES), lambda n: (0, 0))
    tab = pl.BlockSpec((WIN, LANES), lambda n: (n, 0))
    return pl.pallas_call(
        body, name="swa_bwd", grid=(nb,),
        in_specs=[row, prev, cur, prev, cur, row, row, pl.BlockSpec((N_HEADS, 1, 1, WIN), lambda n: (0, n, 0, 0)),
                  row, pl.BlockSpec((WIN, D), lambda n: (n, 1)), pl.BlockSpec(memory_space=pltpu.SMEM),
                  pl.BlockSpec((1, HD), lambda n: (0, 0)), tab, tab],
        out_specs=[pl.BlockSpec((WIN, 2 * D), lambda n: (n, 0)), cur, cur, cur, cur, acc, acc],
        out_shape=[_sds((S, 2 * D), BF16)] + [_sds((S, 256), F32)] * 4 + [_sds((1, LANES), F32)] * 2,
        compiler_params=_params())(qb, ksh, ksh, vsh, vsh, dz, o, lse, pb, pb, sinks, gq, cos2, sin2)


def _prep_kv_bwd(dka, dkb, dva, dvb, kv, gk, cos2, sin2):
    nt = S // RB
    per = RB // WIN

    def shifted(cur_ref, nxt_ref, has_next):
        return jnp.concatenate([cur_ref[WIN:RB, :], jnp.where(has_next, nxt_ref[...], 0.0)], axis=0)

    def body(dka_ref, dkb_ref, dkn_ref, dva_ref, dvb_ref, dvn_ref, x_ref, g_ref, c_ref, s_ref, o_ref, dgk_ref):
        i, j = pl.program_id(0), pl.program_id(1)

        @pl.when((i == 0) & (j == 0))
        def _():
            dgk_ref[...] = jnp.zeros_like(dgk_ref)

        has_next = i < nt - 1

        @pl.when(j == 0)
        def _():
            lane = _lane_iota((RB, LANES))
            g2, cos, sin = _g2(g_ref), c_ref[...], s_ref[...]
            dy_all = dka_ref[...] + shifted(dkb_ref, dkn_ref, has_next)
            dg_tot = jnp.zeros((1, LANES), F32)
            lo_half = _half_ones()
            for c in _pairs(CB):
                dn = _rope_bwd(dy_all[:, c], cos, sin, lane)
                dx, dg = _head_norm_bwd(dn, x_ref[:, c], g2, lo_half)
                o_ref[:, c] = dx.astype(BF16)
                dg_tot = dg_tot + dg
            dgk_ref[...] += dg_tot

        @pl.when(j == 1)
        def _():
            o_ref[...] = (dva_ref[...] + shifted(dvb_ref, dvn_ref, has_next)).astype(BF16)

    cur = pl.BlockSpec((RB, CB), lambda i, j: (i, 0))
    nxt = pl.BlockSpec((WIN, CB), lambda i, j: (jnp.minimum(per * (i + 1), S // WIN - 1), 0))
    tab = pl.BlockSpec((RB, LANES), lambda i, j: (i, 0))
    return pl.pallas_call(
        body, name="prep_kv_bwd", grid=(nt, 2),
        in_specs=[cur, cur, nxt, cur, cur, nxt, cur, pl.BlockSpec((1, HD), lambda i, j: (0, 0)), tab, tab],
        out_specs=[pl.BlockSpec((RB, CB), lambda i, j: (i, j)), pl.BlockSpec((1, LANES), lambda i, j: (0, 0))],
        out_shape=[_sds((S, 2 * CB), BF16), _sds((1, LANES), F32)],
        compiler_params=_params())(dka, dkb, dkb, dva, dvb, dvb, kv, gk, cos2, sin2)


def _out_b_loss(z, w_out, h1, tgt):
    tm = 2 * RT

    def body(z_ref, w_ref, h_ref, t_ref, dy_ref, l_ref):
        @pl.when(pl.program_id(0) == 0)
        def _():
            l_ref[...] = jnp.zeros_like(l_ref)

        e = (h_ref[...] + _dot_nn(z_ref[...], w_ref[...])) - t_ref[...]
        dy_ref[...] = e * (1.0 / D)
        l_ref[...] += jnp.sum(jnp.sum(e * e, axis=-1, keepdims=True), axis=0, keepdims=True)

    row = pl.BlockSpec((tm, D), lambda i: (i, 0))
    return pl.pallas_call(
        body, name="out_b_loss", grid=(S // tm,),
        in_specs=[row, pl.BlockSpec(w_out.shape, lambda i: (0, 0), pipeline_mode=pl.Buffered(1)), row, row],
        out_specs=[row, pl.BlockSpec((1, LANES), lambda i: (0, 0))],
        out_shape=[_sds((S, D), F32), _sds((1, LANES), F32)], compiler_params=_params())(z, w_out, h1, tgt)


def _du_a_rms_bwd(dproj, wa, x, g, dres, after):
    tm, tk = 1024, NA // 2
    nk = NA // tk

    def body(a_ref, b_ref, x_ref, g_ref, dr_ref, after_ref, dx_ref, dg_ref, acc):
        i, kk = pl.program_id(0), pl.program_id(1)

        @pl.when((i == 0) & (kk == 0))
        def _():
            dg_ref[...] = jnp.zeros_like(dg_ref)

        p = _dot_nt(a_ref[...], b_ref[...])

        @pl.when(kk == 0)
        def _():
            acc[...] = p

        @pl.when(kk == nk - 1)
        def _():
            dx, dg = _rms_bwd_core(acc[...] + p, x_ref[...], g_ref[...])
            dx_ref[...] = dr_ref[...] + dx
            dg_ref[...] += dg

    assert nk == 2
    row = pl.BlockSpec((tm, D), lambda i, kk: (i, 0))
    vec = pl.BlockSpec((1, D), lambda i, kk: (0, 0))
    return pl.pallas_call(
        body, name="du_a_rms_bwd", grid=(S // tm, nk),
        in_specs=[pl.BlockSpec((tm, tk), lambda i, kk: (i, kk)), pl.BlockSpec((D, tk), lambda i, kk: (0, kk)),
                  row, vec, row, pl.BlockSpec(after.shape, lambda i, kk: (0, 0))],
        out_specs=[row, vec], out_shape=[_sds((S, D), F32), _sds((1, D), F32)],
        scratch_shapes=[pltpu.VMEM((tm, D), F32)], compiler_params=_params())(dproj, wa, x, g, dres, after)


def _du_b_rms_bwd(dpb, w_in_b, dkv, w_kv, h1, g_b, g_kv, dy):
    tm = 2 * RT

    def body(ab_ref, wb_ref, akv_ref, wkv_ref, x_ref, gb_ref, gkv_ref, dy_ref, dh_ref, dgb_ref, dgkv_ref):
        @pl.when(pl.program_id(0) == 0)
        def _():
            dgb_ref[...] = jnp.zeros_like(dgb_ref)
            dgkv_ref[...] = jnp.zeros_like(dgkv_ref)

        x = x_ref[...]
        dx1, dg1 = _rms_bwd_core(_dot_nt(ab_ref[...], wb_ref[...]), x, gb_ref[...])
        dx2, dg2 = _rms_bwd_core(_dot_nt(akv_ref[...], wkv_ref[...]), x, gkv_ref[...])
        dh_ref[...] = dy_ref[...] + dx1 + dx2
        dgb_ref[...] += dg1
        dgkv_ref[...] += dg2

    row = lambda width: pl.BlockSpec((tm, width), lambda i: (i, 0))
    whole = lambda arr: pl.BlockSpec(arr.shape, lambda i: (0, 0), pipeline_mode=pl.Buffered(1))
    vec = pl.BlockSpec((1, D), lambda i: (0, 0))
    return pl.pallas_call(
        body, name="du_b_rms_bwd", grid=(S // tm,),
        in_specs=[row(dpb.shape[1]), whole(w_in_b), row(dkv.shape[1]), whole(w_kv), row(D), vec, vec, row(D)],
        out_specs=[row(D), vec, vec], out_shape=[_sds((S, D), F32), _sds((1, D), F32), _sds((1, D), F32)],
        compiler_params=_params())(dpb, w_in_b, dkv, w_kv, h1, g_b, g_kv, dy)


def _gather_first(w_in_a, w_out_a, w_kv, w_in_b, w_out_b, norm_a_g):
    rows, cols = w_in_a.shape[-2:]
    groups = rows // LANES
    cols_pad = -(-cols // LANES) * LANES

    def body(wia_ref, woa_ref, wkv_ref, wib_ref, wob_ref, ga_ref,
             wa_g, ga_g, woa_s, wkv_s, wib_s, wob_s, wa_s, st_a, st_oa, st_kv, st_ib, st_ob, plane_t, load_sems, *sems):
        sources = [wia_ref, woa_ref.at[0], wkv_ref, wib_ref.at[0], wob_ref.at[0]]
        stages = [st_a, st_oa, st_kv, st_ib, st_ob]
        loads = [pltpu.make_async_copy(src, dst, load_sems.at[i]) for i, (src, dst) in enumerate(zip(sources, stages))]
        for cp in loads:
            cp.start()
        loads[0].wait()
        plane_t[cols_pad - LANES:, :] = jnp.zeros((LANES, LANES), F32)
        for j in range(groups):
            for c0 in range(0, cols, 64):
                n = min(64, cols - c0)
                plane_t[c0:c0 + n, :] = st_a[pl.ds(c0 * groups + j, n, stride=groups), :]
            for c0 in range(0, cols, LANES):
                n = min(LANES, cols - c0)
                wa_s[j * LANES:(j + 1) * LANES, c0:c0 + n] = plane_t[c0:c0 + LANES, :].T[:, :n].astype(BF16)

        def cast_the_rest():
            for cp, stage, out in zip(loads[1:], stages[1:], [woa_s, wkv_s, wib_s, wob_s]):
                cp.wait()
                out[...] = stage[...].astype(BF16)

        _gather_two_level([wa_s, ga_ref], [wa_g, ga_g], sems, meanwhile=cast_the_rest)

    vmem = pl.BlockSpec(memory_space=pltpu.VMEM)
    anyspec = pl.BlockSpec(memory_space=pl.ANY)
    shard = lambda w: _sds(w.shape[-2:], BF16)
    stage = lambda w: pltpu.VMEM(w.shape[-2:], F32)
    return pl.pallas_call(
        body, name="gather_first", in_specs=[anyspec] * 5 + [vmem],
        out_specs=[anyspec, anyspec, vmem, vmem, vmem, vmem],
        out_shape=[_sds((N_DEV,) + w_in_a.shape[-2:], BF16), _sds((N_DEV,) + norm_a_g.shape, F32),
                   shard(w_out_a), shard(w_kv), shard(w_in_b), shard(w_out_b)],
        scratch_shapes=[pltpu.VMEM(w_in_a.shape[-2:], BF16), pltpu.VMEM((cols * groups, LANES), F32), stage(w_out_a),
                        stage(w_kv), stage(w_in_b), stage(w_out_b), pltpu.VMEM((cols_pad, LANES), F32),
                        pltpu.SemaphoreType.DMA((5,))] + _exchange_sems(2),
        compiler_params=pltpu.CompilerParams(vmem_limit_bytes=VMEM_LIMIT, has_side_effects=True))(
            _entry_view(w_in_a).reshape(cols * groups, LANES), w_out_a, w_kv, w_in_b, w_out_b, norm_a_g)


def _pair_reduce(slots):
    n_chip = N_DEV // 2
    _, rows, cols = slots.shape

    def body(s_ref, o_ref, own_v, sib_v, send_sems, recv_sems, local_sems):
        x, y, c = lax.axis_index("x"), lax.axis_index("y"), lax.axis_index("c")
        copies = []
        for j in range(n_chip):
            own = pltpu.make_async_copy(s_ref.at[2 * j + c], own_v.at[j], local_sems.at[j])
            give = pltpu.make_async_remote_copy(
                src_ref=s_ref.at[2 * j + 1 - c], dst_ref=sib_v.at[j], send_sem=send_sems.at[j],
                recv_sem=recv_sems.at[j], device_id=(x, y, 1 - c), device_id_type=pl.DeviceIdType.MESH)
            own.start()
            give.start()
            copies.append((own, give))
        for j, (own, give) in enumerate(copies):
            own.wait()
            give.wait()
            o_ref[j] = (own_v[j].astype(F32) + sib_v[j].astype(F32)).astype(BF16)

    half = _sds((n_chip, rows, cols), slots.dtype)
    return pl.pallas_call(
        body, name="pair_reduce", in_specs=[pl.BlockSpec(memory_space=pl.ANY)],
        out_specs=pl.BlockSpec(memory_space=pltpu.VMEM), out_shape=half,
        scratch_shapes=[pltpu.VMEM(half.shape, half.dtype), pltpu.VMEM(half.shape, half.dtype),
                        pltpu.SemaphoreType.DMA((n_chip,)), pltpu.SemaphoreType.DMA((n_chip,)),
                        pltpu.SemaphoreType.DMA((n_chip,))],
        compiler_params=pltpu.CompilerParams(vmem_limit_bytes=VMEM_LIMIT, has_side_effects=True))(slots)


def _padded_col(c):
    if c < RAW_F:
        return c
    return FOFF + (c - RAW_F) if c < RAW_G else GOFF + (c - RAW_G)


def _shard_pieces():
    width = NA_RAW // N_DEV
    pieces = []
    for d in range(N_DEV):
        cuts = [width * d] + [c for c in (RAW_F, RAW_G) if width * d < c < width * (d + 1)] + [width * (d + 1)]
        for lo, hi in zip(cuts[:-1], cuts[1:]):
            pieces.append((d, lo - width * d, _padded_col(lo), hi - lo))
    return pieces


def _unshard_wa(wa_g):
    def body(w_ref, o_ref):
        o_ref[:, FOFF + N_HEADS:NA] = jnp.zeros((TM, NA - FOFF - N_HEADS), BF16)
        for d, src, dst, width in _shard_pieces():
            o_ref[:, dst:dst + width] = w_ref[d, :, src:src + width]

    return pl.pallas_call(
        body, name="unshard_wa", grid=(D // TM,),
        in_specs=[pl.BlockSpec((N_DEV, TM, NA_RAW // N_DEV), lambda i: (0, i, 0))],
        out_specs=pl.BlockSpec((TM, NA), lambda i: (i, 0)), out_shape=_sds((D, NA), BF16),
        compiler_params=_params())(wa_g)


def _reshard_dwa(dwa):
    def body(g_ref, o_ref):
        for d, src, dst, width in _shard_pieces():
            o_ref[d, :, src:src + width] = g_ref[:, dst:dst + width]

    return pl.pallas_call(
        body, name="reshard_dwa", grid=(D // TM,), in_specs=[pl.BlockSpec((TM, NA), lambda i: (i, 0))],
        out_specs=pl.BlockSpec((N_DEV, TM, NA_RAW // N_DEV), lambda i: (0, i, 0)),
        out_shape=_sds((N_DEV, D, NA_RAW // N_DEV), dwa.dtype), compiler_params=_params())(dwa)


CHIP_FLIPS = (2, 4, 6)


def _chip_exchange_start(partial):
    n = len(CHIP_FLIPS)

    def body(p_ref, land_ref, *rest):
        sends, recvs, token = rest[:n], rest[n:2 * n], rest[2 * n + 2]
        x, y, c = lax.axis_index("x"), lax.axis_index("y"), lax.axis_index("c")
        me = 4 * x + 2 * y + c
        for idx, k in enumerate(CHIP_FLIPS):
            pltpu.make_async_remote_copy(
                src_ref=p_ref.at[(me ^ k) >> 1], dst_ref=land_ref.at[me >> 1], send_sem=sends[idx],
                recv_sem=recvs[idx], device_id=(x ^ ((k >> 2) & 1), y ^ ((k >> 1) & 1), c),
                device_id_type=pl.DeviceIdType.MESH).start()
        token[...] = jnp.zeros_like(token)

    hbm = pl.BlockSpec(memory_space=pltpu.HBM)
    sem = pl.BlockSpec(memory_space=pltpu.SEMAPHORE)
    buf = pltpu.HBM(partial.shape, partial.dtype)
    return pl.pallas_call(
        body, name="chip_exchange_start",
        out_shape=(pltpu.SemaphoreType.DMA(()),) * (2 * n) + (buf, buf, _sds((8, LANES), F32)),
        in_specs=(hbm, hbm), out_specs=(sem,) * (2 * n) + (hbm, hbm, pl.BlockSpec(memory_space=pltpu.VMEM)),
        input_output_aliases={0: 2 * n, 1: 2 * n + 1},
        compiler_params=pltpu.CompilerParams(has_side_effects=pltpu.SideEffectType.DATAFLOW_SIDE_EFFECTING))(
            pltpu.with_memory_space_constraint(partial, pltpu.HBM),
            pltpu.with_memory_space_constraint(lax.empty(partial.shape, partial.dtype), pltpu.HBM))


def _chip_exchange_wait(started, after):
    n = len(CHIP_FLIPS)
    sems, (p_thru, land_thru) = started[:2 * n], started[2 * n:2 * n + 2]

    def body(p_ref, land_ref, *rest):
        sends, recvs = rest[:n], rest[n:2 * n]
        x, y, c = lax.axis_index("x"), lax.axis_index("y"), lax.axis_index("c")
        me = 4 * x + 2 * y + c
        for idx, k in enumerate(CHIP_FLIPS):
            copy = pltpu.make_async_remote_copy(
                src_ref=p_ref.at[(me ^ k) >> 1], dst_ref=land_ref.at[(me ^ k) >> 1], send_sem=sends[idx],
                recv_sem=recvs[idx], device_id=(x ^ ((k >> 2) & 1), y ^ ((k >> 1) & 1), c),
                device_id_type=pl.DeviceIdType.MESH)
            copy.wait_send()
            copy.wait_recv()

    hbm = pl.BlockSpec(memory_space=pltpu.HBM)
    sem = pl.BlockSpec(memory_space=pltpu.SEMAPHORE)
    buf = pltpu.HBM(p_thru.shape, p_thru.dtype)
    return pl.pallas_call(
        body, name="chip_exchange_wait", out_shape=(buf, buf),
        in_specs=(hbm, hbm) + (sem,) * (2 * n) + (pl.BlockSpec(memory_space=pl.ANY),), out_specs=(hbm, hbm),
        input_output_aliases={0: 0, 1: 1},
        compiler_params=pltpu.CompilerParams(has_side_effects=pltpu.SideEffectType.DATAFLOW_SIDE_EFFECTING))(
            p_thru, land_thru, *sems, after)


def _adamw(w, g, m, v):
    m = ADAM_B1 * m + (1.0 - ADAM_B1) * g
    v = ADAM_B2 * v + (1.0 - ADAM_B2) * (g * g)
    m_hat = m / (1.0 - ADAM_B1 ** ADAM_STEP)
    v_hat = v / (1.0 - ADAM_B2 ** ADAM_STEP)
    delta = -ADAM_LR * (m_hat / (jnp.sqrt(v_hat) + ADAM_EPS) + ADAM_WD * w)
    return delta, m, v


def _sum_adamw(recv, w, m, v, name, after=None):
    lead = w.ndim - 2
    rows, cols = w.shape[-2:]
    tr = 256 if rows % 256 == 0 else 128
    n_slots = recv.shape[0]
    extra = [] if after is None else [after]

    def body(*refs):
        r_ref = refs[0]
        w_ref, m_ref, v_ref, g_ref, d_ref, nm_ref, nv_ref = refs[1 + len(extra):]
        g = None
        for slot in range(n_slots):
            g = r_ref[slot].astype(F32) if g is None else g + r_ref[slot].astype(F32)
        g_ref[...] = g
        d_ref[...], nm_ref[...], nv_ref[...] = _adamw(w_ref[...], g, m_ref[...], v_ref[...])

    blk = pl.BlockSpec((None,) * lead + (tr, cols), lambda i: (0,) * lead + (i, 0))
    slots = pl.BlockSpec((n_slots, tr, cols), lambda i: (0, i, 0))
    return pl.pallas_call(
        body, name=name, grid=(rows // tr,),
        in_specs=[slots] + [pl.BlockSpec(a.shape, lambda i: (0, 0)) for a in extra] + [blk, blk, blk],
        out_specs=[blk] * 4, out_shape=[_sds(w.shape, F32)] * 4, compiler_params=_params())(recv, *extra, w, m, v)


def _entry_view(a):
    _, rows, cols = a.shape
    return jnp.transpose(a, (2, 0, 1)).reshape(cols, rows // LANES, LANES)


def _from_entry_view(a):
    cols, groups, lanes = a.shape
    return jnp.transpose(a, (1, 2, 0)).reshape(1, groups * lanes, cols)


def _sum_adamw_entry_view(landing, own, w, m, v, slab, name):
    n_slots, rows, cols = landing.shape
    groups = rows // LANES
    cols_pad = -(-cols // LANES) * LANES

    def body(r_ref, own_ref, w_ref, m_ref, v_ref, g_ref, d_ref, nm_ref, nv_ref, pad_ref, gt_ref):
        j = pl.program_id(0)
        chip = (4 * lax.axis_index("x") + 2 * lax.axis_index("y") + lax.axis_index("c")) >> 1
        pad_ref[:, cols_pad - LANES:] = jnp.zeros((LANES, LANES), F32)
        for r0 in range(0, LANES, 32):
            g = None
            for slot in range(n_slots):
                part = jnp.where(chip == slot, own_ref[slot, r0:r0 + 32], r_ref[slot, r0:r0 + 32])
                g = part.astype(F32) if g is None else g + part.astype(F32)
            pad_ref[r0:r0 + 32, :cols] = g
        for c0 in range(0, cols_pad, LANES):
            gt_ref[c0:c0 + LANES, :] = pad_ref[:, c0:c0 + LANES].T
        def update_plane(plane):
            for c0 in range(0, cols, 64):
                n = min(64, cols - c0)
                at = pl.ds(c0 * groups + plane, n, stride=groups)
                gt = gt_ref[c0:c0 + n, :]
                g_ref[at, :] = gt
                d_ref[at, :], nm_ref[at, :], nv_ref[at, :] = _adamw(w_ref[at, :], gt, m_ref[at, :], v_ref[at, :])

        for plane in range(groups):
            pl.when(j == plane)(lambda plane=plane: update_plane(plane))

    flat = lambda a: _entry_view(a).reshape(cols * groups, LANES)
    whole = pl.BlockSpec((cols * groups, LANES), lambda j: (0, 0))
    slots = pl.BlockSpec((n_slots, LANES, cols), lambda j: (0, j, 0))
    *outs, slab_g = _call(
        body, name=name, args=(landing, own, flat(w), flat(m), flat(v)), grid=(groups,),
        in_specs=[slots, slots, whole, whole, whole], out_specs=[whole] * 4,
        out_shape=[_sds((cols * groups, LANES), F32)] * 4,
        scratch_shapes=[pltpu.VMEM((LANES, cols_pad), F32), pltpu.VMEM((cols_pad, LANES), F32)],
        rider=[("gather_rows", slab)])
    return [_from_entry_view(o.reshape(cols, groups, LANES)) for o in outs], slab_g


SLAB_ROWS = 16
SLOT = {"kv_norm_g": (8, 0, D), "norm_b_g": (9, 0, D), "b_forget": (10, 0, 16), "qnorm_a_g": (10, 128, HD),
        "knorm_a_g": (10, 256, HD), "knorm_b_g": (10, 384, HD), "qnorm_b_g": (10, 512, HD), "sinks": (10, 640, 16)}
SMALL = ["norm_a_g", "b_forget", "qnorm_a_g", "knorm_a_g", "kv_norm_g", "knorm_b_g", "norm_b_g", "qnorm_b_g", "sinks"]


LOSS_ROW = 11


def _pack_small(dg_a, dg_kv, dg_b, db_f, dgq_a, dgk_a, dgk_b, dgq_b, dsinks, lsum):
    def fold(ref):
        return ref[:, 0:HD] + ref[:, HD:2 * HD]

    def body(dga_ref, dgkv_ref, dgb_ref, dbf_ref, dgqa_ref, dgka_ref, dgkb_ref, dgqb_ref, dsk_ref, ls_ref, slab_ref):
        slab_ref[...] = jnp.zeros_like(slab_ref)
        for r in range(N_DEV):
            slab_ref[r:r + 1, 0:LANES] = dga_ref[:, LANES * r:LANES * (r + 1)]
        slab_ref[8:9, :] = dgkv_ref[...]
        slab_ref[9:10, :] = dgb_ref[...]
        slab_ref[10:11, 0:LANES] = dbf_ref[...]
        slab_ref[10:11, 128:128 + HD] = fold(dgqa_ref)
        slab_ref[10:11, 256:256 + HD] = fold(dgka_ref)
        slab_ref[10:11, 384:384 + HD] = fold(dgkb_ref)
        slab_ref[10:11, 512:512 + HD] = fold(dgqb_ref)
        slab_ref[10:11, 640:640 + LANES] = dsk_ref[...]
        slab_ref[LOSS_ROW:LOSS_ROW + 1, 0:LANES] = ls_ref[...]

    return pl.pallas_call(body, name="pack_small", out_shape=_sds((SLAB_ROWS, D), F32), compiler_params=_params())(
        dg_a, dg_kv, dg_b, db_f, dgq_a, dgk_a, dgk_b, dgq_b, dsinks, lsum)


def _small_adamw(recv, ws, ms, vs):
    k = len(SMALL)

    def body(*refs):
        r_ref = refs[0]
        w_refs, m_refs, v_refs = refs[1:1 + k], refs[1 + k:1 + 2 * k], refs[1 + 2 * k:1 + 3 * k]
        outs = refs[1 + 3 * k:1 + 7 * k]
        loss_ref, tot = refs[1 + 7 * k], refs[2 + 7 * k]
        g = r_ref[0]
        for dev in range(1, N_DEV):
            g = g + r_ref[dev]
        tot[...] = g
        loss_ref[...] = tot[LOSS_ROW:LOSS_ROW + 1, 0:LANES] * (0.5 / D)
        me = 4 * lax.axis_index("x") + 2 * lax.axis_index("y") + lax.axis_index("c")
        for p, name in enumerate(SMALL):
            if name == "norm_a_g":
                mine = lax.broadcasted_iota(jnp.int32, (N_DEV, LANES), 0) == me
                gp = jnp.sum(jnp.where(mine, tot[0:N_DEV, 0:LANES], 0.0), axis=0, keepdims=True)
            else:
                row, lo, width = SLOT[name]
                gp = tot[row:row + 1, lo:lo + width]
            d, nm, nv = _adamw(w_refs[p][...], gp, m_refs[p][...], v_refs[p][...])
            outs[p][...] = gp
            outs[k + p][...] = d
            outs[2 * k + p][...] = nm
            outs[3 * k + p][...] = nv

    shapes = [_sds(w.shape, F32) for w in ws]
    return pl.pallas_call(body, name="small_adamw", out_shape=shapes * 4 + [_sds((1, LANES), F32)],
                          scratch_shapes=[pltpu.VMEM((SLAB_ROWS, D), F32)],
                          compiler_params=_params())(recv, *ws, *ms, *vs)


def _rope_tables(positions):
    inv_freq = jnp.power(jnp.float32(ROPE_THETA), -jnp.arange(0, ROT, 2, dtype=F32) / ROT)
    ang = positions.astype(F32)[:, None] * inv_freq[None, :]
    cos, sin = jnp.cos(ang), jnp.sin(ang)
    c64 = jnp.concatenate([cos, cos, jnp.ones((S, HD - ROT), F32)], axis=-1)
    s64 = jnp.concatenate([-sin, sin, jnp.zeros((S, HD - ROT), F32)], axis=-1)
    return jnp.tile(c64, (1, 2)), jnp.tile(s64, (1, 2))


def _local_step(x, tgt, positions, g_a, wa, b_forget, gq_a, gk_a, g_kv, gk_b, g_b, gq_b, sinks,
                woa_s, wkv_s, wib_s, wob_s, adamw_others):
    nq = S // TQ
    cos2, sin2 = _rope_tables(positions)
    b_pad = jnp.pad(b_forget, ((0, 0), (0, LANES - N_HEADS)))

    u_a, proj, qn, kn, vb, ccol, cbc = _head_a(x, g_a, wa, gq_a, gk_a, b_pad)
    crow = ccol[:, :N_HEADS].T.reshape(N_HEADS, nq, 1, TQ)
    o_a, z_a, lse_a, woa_g, wkv_g, w_in_b, wob_g = _fox_fwd(
        qn, kn, vb, proj, crow, cbc,
        rider=[("gather_rows", woa_s), ("gather_rows", wkv_s), ("gather_cols", wib_s), ("gather_rows", wob_s)])
    w_out_a, w_kv, w_out_b = woa_g.reshape(D, D), wkv_g.reshape(D, 512), wob_g.reshape(D, D)
    h1, u_kv, u_b, kv, pb, qb, ksh, vsh = _head_b(x, z_a, w_out_a, g_kv, g_b, w_kv, w_in_b, gq_b, gk_b, cos2, sin2)
    sinks1 = sinks.reshape(N_HEADS)
    o_b, z_b, lse_b = _swa_fwd(qb, ksh, vsh, pb, sinks1)
    dy, lsum = _out_b_loss(z_b, w_out_b, h1, tgt)
    dw_out_b = _mm(z_b, dy, "tn", 512, 512, S, out_dtype=BF16, name="mm_dw_out_b")
    dz_b = _mm(dy, w_out_b, "nt", 1024, 512, D, name="mm_dz_b")
    dpb, dka, dkb, dva, dvb, dsinks, dgq_b = _swa_bwd(qb, ksh, vsh, dz_b, o_b, lse_b, pb, sinks1, gq_b, cos2, sin2)
    dkv, dgk_b = _prep_kv_bwd(dka, dkb, dva, dvb, kv, gk_b, cos2, sin2)
    dw_in_b = _mm(u_b, dpb, "tn", 512, 512, S, out_dtype=BF16, name="mm_dw_in_b")
    dw_kv = _mm(u_kv, dkv, "tn", 512, 512, S, out_dtype=BF16, name="mm_dw_kv")
    dh1, dg_b, dg_kv = _du_b_rms_bwd(dpb, w_in_b, dkv, w_kv, h1, g_b, g_kv, dy)
    dw_out_a = _mm(z_a, dh1, "tn", 512, 512, S, out_dtype=BF16, name="mm_dw_out_a")
    do_a, dgate_a, delta_a = _fox_bwd_pre(dh1, w_out_a, proj, o_a)
    dk_a, dv_a, dcs, dq_a, drow, r_wob, r_wib, r_wkv, r_woa = _fox_bwd(
        qn, kn, vb, do_a, lse_a, delta_a, crow, cbc,
        rider=[("a2a_rows", dw_out_b), ("a2a_cols", dw_in_b), ("a2a_rows", dw_kv), ("a2a_rows", dw_out_a)])
    drow_col = jnp.pad(drow.reshape(N_HEADS, S).T, ((0, 0), (0, LANES - N_HEADS)))
    dproj, dgq_a, dgk_a, db_f = _prep_a_bwd(dq_a, dk_a, dv_a, dgate_a, drow_col, dcs, proj, b_pad, gq_a, gk_a)
    dwa = _mm(u_a, dproj, "tn", 1024, 256, S, out_dtype=BF16, name="mm_dw_in_a")
    partial = _pair_reduce(_reshard_dwa(dwa))
    started = _chip_exchange_start(partial)
    dx, dg_a = _du_a_rms_bwd(dproj, wa, x, g_a, dh1, after=started[-1])
    others = adamw_others(dict(w_out_a=r_woa, w_kv=r_wkv, w_in_b=r_wib, w_out_b=r_wob), dg_a)
    partial, landed = _chip_exchange_wait(started, others["w_out_b"][0])
    slab = _pack_small(dg_a, dg_kv, dg_b, db_f, dgq_a, dgk_a, dgk_b, dgq_b, dsinks, lsum)
    return dx, (landed, partial), others, slab


def kernel(x, positions, norm_a_g, w_in_a, b_forget, qnorm_a_g, knorm_a_g, w_out_a, kv_norm_g, w_kv, knorm_b_g, norm_b_g, w_in_b, qnorm_b_g, sinks, w_out_b, loss_target, m_norm_a_g, m_w_in_a, m_b_forget, m_qnorm_a_g, m_knorm_a_g, m_w_out_a, m_kv_norm_g, m_w_kv, m_knorm_b_g, m_norm_b_g, m_w_in_b, m_qnorm_b_g, m_sinks, m_w_out_b, v_norm_a_g, v_w_in_a, v_b_forget, v_qnorm_a_g, v_knorm_a_g, v_w_out_a, v_kv_norm_g, v_w_kv, v_knorm_b_g, v_norm_b_g, v_w_in_b, v_qnorm_b_g, v_sinks, v_w_out_b):
    wa_g, ga_g, woa_s, wkv_s, wib_s, wob_s = _gather_first(w_in_a, w_out_a, w_kv, w_in_b, w_out_b, norm_a_g)
    state = dict(w_in_a=(w_in_a, m_w_in_a, v_w_in_a), w_out_a=(w_out_a, m_w_out_a, v_w_out_a),
                 w_kv=(w_kv, m_w_kv, v_w_kv), w_in_b=(w_in_b, m_w_in_b, v_w_in_b),
                 w_out_b=(w_out_b, m_w_out_b, v_w_out_b))

    def adamw_others(landed, after):
        return {n: _sum_adamw(r, *state[n], "adamw_" + n, after=after) for n, r in landed.items()}

    dx, r_wa, big, slab = _local_step(
        x[0], loss_target[0], positions, ga_g.reshape(1, D), _unshard_wa(wa_g), b_forget, qnorm_a_g, knorm_a_g,
        kv_norm_g.reshape(1, D), knorm_b_g.reshape(1, HD), norm_b_g, qnorm_b_g, sinks, woa_s, wkv_s, wib_s, wob_s,
        adamw_others)
    big["w_in_a"], slab_g = _sum_adamw_entry_view(*r_wa, *state["w_in_a"], slab, "adamw_w_in_a")

    r2 = lambda a: a.reshape(1, -1)
    small_w = dict(norm_a_g=norm_a_g, b_forget=b_forget, qnorm_a_g=qnorm_a_g, knorm_a_g=knorm_a_g,
                   kv_norm_g=kv_norm_g, knorm_b_g=knorm_b_g, norm_b_g=norm_b_g, qnorm_b_g=qnorm_b_g, sinks=sinks)
    small_m = dict(norm_a_g=m_norm_a_g, b_forget=m_b_forget, qnorm_a_g=m_qnorm_a_g, knorm_a_g=m_knorm_a_g,
                   kv_norm_g=m_kv_norm_g, knorm_b_g=m_knorm_b_g, norm_b_g=m_norm_b_g, qnorm_b_g=m_qnorm_b_g,
                   sinks=m_sinks)
    small_v = dict(norm_a_g=v_norm_a_g, b_forget=v_b_forget, qnorm_a_g=v_qnorm_a_g, knorm_a_g=v_knorm_a_g,
                   kv_norm_g=v_kv_norm_g, knorm_b_g=v_knorm_b_g, norm_b_g=v_norm_b_g, qnorm_b_g=v_qnorm_b_g,
                   sinks=v_sinks)
    res = _small_adamw(slab_g, [r2(small_w[n]) for n in SMALL], [r2(small_m[n]) for n in SMALL],
                       [r2(small_v[n]) for n in SMALL])
    k = len(SMALL)
    small = {n: [res[q * k + p].reshape(small_w[n].shape) for q in range(4)] for p, n in enumerate(SMALL)}
    loss = res[4 * k][0, 0]

    order = ["norm_a_g", "w_in_a", "b_forget", "qnorm_a_g", "knorm_a_g", "w_out_a", "kv_norm_g", "w_kv",
             "knorm_b_g", "norm_b_g", "w_in_b", "qnorm_b_g", "sinks", "w_out_b"]

    def leaf(n, q):
        return big[n][q] if n in big else small[n][q]

    outs = [loss, dx[None]]
    for q in range(4):
        outs.extend(leaf(n, q) for n in order)
    return tuple(outs)
```

```python
import jax
import jax.numpy as jnp
from jax import lax
from jax.experimental import pallas as pl
from jax.experimental.pallas import tpu as pltpu

F32, BF16 = jnp.float32, jnp.bfloat16

S = 2048
D = 1024
HD = 64
N_HEADS = 16
N_DEV = 8
NA = 4352
GOFF = 3072
FOFF = 4096
RAW_F = 3072
RAW_G = RAW_F + N_HEADS
NA_RAW = 4112
EPS = 1e-6
QSCALE = 0.125
ROPE_THETA = 500000.0
ROT = 16
WIN = 128
TQ = 256
TK = 256
KS = TQ // 2
HPS = 8
HW = HPS * HD
TM = 256
RT = 256
RB = 512
CB = 256
LANES = 128

ADAM_LR, ADAM_B1, ADAM_B2, ADAM_EPS, ADAM_WD, ADAM_STEP = 0.001, 0.9, 0.999, 1e-08, 0.01, 10

VMEM_LIMIT = 56 * 1024 * 1024


def _params():
    return pltpu.CompilerParams(vmem_limit_bytes=VMEM_LIMIT)


def _sds(shape, dtype):
    return jax.ShapeDtypeStruct(shape, dtype)


def _dot_nt(a, b):
    return lax.dot_general(a, b, (((1,), (1,)), ((), ())), preferred_element_type=F32)


def _dot_tn(a, b):
    return lax.dot_general(a, b, (((0,), (0,)), ((), ())), preferred_element_type=F32)


def _dot_nn(a, b):
    return lax.dot_general(a, b, (((1,), (0,)), ((), ())), preferred_element_type=F32)


def _sigmoid(g):
    return 1.0 / (1.0 + jnp.exp(-g))


def _lane_iota(shape):
    return lax.broadcasted_iota(jnp.int32, shape, len(shape) - 1)


def _flips(kind):
    return (2, 4, 6) if kind == "a2a_chips" else tuple(range(1, N_DEV))


def _send_view(kind, ref, dev):
    if kind in ("gather_rows", "gather_cols"):
        return ref
    if kind == "a2a_slots":
        return ref.at[dev]
    if kind == "a2a_chips":
        return ref.at[dev >> 1]
    if kind == "a2a_rows":
        rows = ref.shape[0] // N_DEV
        return ref.at[pl.ds(pl.multiple_of(dev * rows, rows), rows)]
    cols = ref.shape[1] // N_DEV
    return ref.at[:, pl.ds(pl.multiple_of(dev * cols, cols), cols)]


def _land_view(kind, ref, dev):
    if kind == "gather_cols":
        cols = ref.shape[1] // N_DEV
        return ref.at[:, pl.ds(pl.multiple_of(dev * cols, cols), cols)]
    if kind == "a2a_chips":
        return ref.at[dev >> 1]
    return ref.at[dev]


def _landing_sds(kind, arr):
    if kind == "gather_rows":
        return _sds((N_DEV,) + arr.shape, arr.dtype)
    if kind == "gather_cols":
        return _sds((arr.shape[0], N_DEV * arr.shape[1]), arr.dtype)
    if kind == "a2a_rows":
        return _sds((N_DEV, arr.shape[0] // N_DEV, arr.shape[1]), arr.dtype)
    if kind == "a2a_cols":
        return _sds((N_DEV, arr.shape[0], arr.shape[1] // N_DEV), arr.dtype)
    return _sds(arr.shape, arr.dtype)


def _exchange_sems(n_parts):
    n = n_parts * (N_DEV - 1)
    return [pltpu.SemaphoreType.DMA((n,)), pltpu.SemaphoreType.DMA((n,)), pltpu.SemaphoreType.DMA((n_parts,))]


def _exchange_ops(kinds, srcs, dsts, sems, start, wait):
    send_sems, recv_sems, local_sems = sems
    x, y, c = lax.axis_index("x"), lax.axis_index("y"), lax.axis_index("c")
    me = 4 * x + 2 * y + c

    def local(a):
        return pltpu.make_async_copy(_send_view(kinds[a], srcs[a], me), _land_view(kinds[a], dsts[a], me),
                                     local_sems.at[a])

    def remote(a, k, landing_dev):
        peer = (x ^ ((k >> 2) & 1), y ^ ((k >> 1) & 1), c ^ (k & 1))
        sem = a * (N_DEV - 1) + k - 1
        return pltpu.make_async_remote_copy(
            src_ref=_send_view(kinds[a], srcs[a], me ^ k), dst_ref=_land_view(kinds[a], dsts[a], landing_dev),
            send_sem=send_sems.at[sem], recv_sem=recv_sems.at[sem], device_id=peer,
            device_id_type=pl.DeviceIdType.MESH)

    pairs = [(a, k) for k in range(1, N_DEV) for a in range(len(kinds)) if k in _flips(kinds[a])]
    if start:
        for a in range(len(kinds)):
            local(a).start()
        for a, k in pairs:
            remote(a, k, me).start()
    if wait:
        for a, k in pairs:
            remote(a, k, me ^ k).wait_recv()
            remote(a, k, me).wait_send()
        for a in range(len(kinds)):
            local(a).wait()


def _gather_two_level(srcs, dsts, sems, meanwhile=None):
    send_sems, recv_sems, local_sems = sems
    x, y, c = lax.axis_index("x"), lax.axis_index("y"), lax.axis_index("c")
    me, sibling = (x, y, c), (x, y, 1 - c)
    chips = [(1 - x, y), (x, 1 - y), (1 - x, 1 - y)]

    def slot(ref, dev):
        return ref.at[4 * dev[0] + 2 * dev[1] + dev[2]]

    def copy(a, k, block, to, src=None):
        return pltpu.make_async_remote_copy(
            src_ref=slot(dsts[a], block) if src is None else src, dst_ref=slot(dsts[a], block),
            send_sem=send_sems.at[a * (N_DEV - 1) + k], recv_sem=recv_sems.at[a * (N_DEV - 1) + k],
            device_id=to, device_id_type=pl.DeviceIdType.MESH)

    parts = range(len(srcs))
    mine = [pltpu.make_async_copy(srcs[a], slot(dsts[a], me), local_sems.at[a]) for a in parts]
    first = [copy(a, 0, me, sibling, src=srcs[a]) for a in parts]
    first += [copy(a, 1 + j, me, (*chip, c), src=srcs[a]) for j, chip in enumerate(chips) for a in parts]
    for cp in mine + first:
        cp.start()
    if meanwhile is not None:
        meanwhile()
    passed = []
    for j, chip in enumerate(chips):
        for a in parts:
            copy(a, 1 + j, (*chip, c), me).wait_recv()
            fwd = copy(a, 4 + j, (*chip, c), sibling)
            fwd.start()
            passed.append(fwd)
    for a in parts:
        copy(a, 0, sibling, me).wait_recv()
        for j, chip in enumerate(chips):
            copy(a, 4 + j, (*chip, 1 - c), me).wait_recv()
    for cp in first + passed:
        cp.wait_send()
    for cp in mine:
        cp.wait()


def _call(body, *, name, args, in_specs, out_specs, out_shape, grid=(), scratch_shapes=(), aliases=None, rider=()):
    n_in, n_out, n_scr, n_r = len(in_specs), len(out_specs), len(scratch_shapes), len(rider)
    kinds = [kind for kind, _ in rider]

    def kernel_body(*refs):
        c_in, r_in = refs[:n_in], refs[n_in:n_in + n_r]
        c_out = refs[n_in + n_r:n_in + n_r + n_out]
        r_out = refs[n_in + n_r + n_out:n_in + 2 * n_r + n_out]
        rest = refs[n_in + 2 * n_r + n_out:]
        c_scr, sems = rest[:n_scr], rest[n_scr:]
        if n_r:
            assert grid, "a rider needs a gridded call"
            ids = [pl.program_id(ax) for ax in range(len(grid))]
            first, last = ids[0] == 0, ids[0] == grid[0] - 1
            for pid, size in zip(ids[1:], grid[1:]):
                first = first & (pid == 0)
                last = last & (pid == size - 1)
            pl.when(first)(lambda: _exchange_ops(kinds, r_in, r_out, sems, True, False))
        body(*c_in, *c_out, *c_scr)
        if n_r:
            pl.when(last)(lambda: _exchange_ops(kinds, r_in, r_out, sems, False, True))

    anyspec = pl.BlockSpec(memory_space=pl.ANY)
    params = pltpu.CompilerParams(vmem_limit_bytes=VMEM_LIMIT, has_side_effects=bool(n_r))
    outs = pl.pallas_call(
        kernel_body, name=name, grid=grid, in_specs=list(in_specs) + [anyspec] * n_r,
        out_specs=list(out_specs) + [anyspec] * n_r,
        out_shape=list(out_shape) + [_landing_sds(kind, arr) for kind, arr in rider],
        scratch_shapes=list(scratch_shapes) + (_exchange_sems(n_r) if n_r else []),
        input_output_aliases=aliases or {}, compiler_params=params)(*args, *[arr for _, arr in rider])
    return list(outs)


def _mm(a, b, mode, tm, tn, tk, out_dtype=F32, add=None, name="mm", rider=()):
    if mode == "nn":
        (m, k), n = a.shape, b.shape[1]
        a_spec = pl.BlockSpec((tm, tk), lambda i, j, kk: (i, kk))
        b_spec = pl.BlockSpec((tk, tn), lambda i, j, kk: (kk, j))
        dot = _dot_nn
    elif mode == "nt":
        (m, k), n = a.shape, b.shape[0]
        a_spec = pl.BlockSpec((tm, tk), lambda i, j, kk: (i, kk))
        b_spec = pl.BlockSpec((tn, tk), lambda i, j, kk: (j, kk))
        dot = _dot_nt
    else:
        (k, m), n = a.shape, b.shape[1]
        a_spec = pl.BlockSpec((tk, tm), lambda i, j, kk: (kk, i))
        b_spec = pl.BlockSpec((tk, tn), lambda i, j, kk: (kk, j))
        dot = _dot_tn
    assert m % tm == 0 and n % tn == 0 and k % tk == 0, (m, n, k, tm, tn, tk)
    nk = k // tk
    has_add = add is not None

    def body(*refs):
        if has_add:
            a_ref, b_ref, add_ref, o_ref, acc = refs
        else:
            a_ref, b_ref, o_ref, acc = refs
        p = dot(a_ref[...].astype(BF16), b_ref[...].astype(BF16))

        def finish(total):
            if has_add:
                total = add_ref[...] + total
            o_ref[...] = total.astype(out_dtype)

        if nk == 1:
            finish(p)
        else:
            kk = pl.program_id(2)

            @pl.when(kk == 0)
            def _():
                acc[...] = p

            @pl.when(kk > 0)
            def _():
                acc[...] += p

            @pl.when(kk == nk - 1)
            def _():
                finish(acc[...])

    in_specs = [a_spec, b_spec]
    args = [a, b]
    if has_add:
        in_specs.append(pl.BlockSpec((tm, tn), lambda i, j, kk: (i, j)))
        args.append(add)
    acc_shape = (tm, tn) if nk > 1 else (8, LANES)
    outs = _call(body, name=name, args=args, grid=(m // tm, n // tn, nk), in_specs=in_specs,
                 out_specs=[pl.BlockSpec((tm, tn), lambda i, j, kk: (i, j))], out_shape=[_sds((m, n), out_dtype)],
                 scratch_shapes=[pltpu.VMEM(acc_shape, F32)], rider=rider)
    return outs if rider else outs[0]


def _rms_rinv(x):
    return lax.rsqrt(jnp.mean(x * x, axis=-1, keepdims=True) + EPS)


def _rms_bwd_core(du, x, g):
    r = _rms_rinv(x)
    dug = du * g
    dx = r * (dug - x * ((r * r) * jnp.mean(dug * x, axis=-1, keepdims=True)))
    dg = jnp.sum(du * (x * r), axis=0, keepdims=True)
    return dx, dg


def _half_ones():
    r = lax.broadcasted_iota(jnp.int32, (LANES, LANES), 0)
    c = lax.broadcasted_iota(jnp.int32, (LANES, LANES), 1)
    return ((r < HD) == (c < HD)).astype(BF16)


def _half_sum(v, lo_half):
    if lo_half.dtype == jnp.bool_:
        s0 = jnp.sum(jnp.where(lo_half, v, 0.0), axis=-1, keepdims=True)
        s1 = jnp.sum(jnp.where(lo_half, 0.0, v), axis=-1, keepdims=True)
        return jnp.where(lo_half, s0, s1)
    hi = v.astype(BF16)
    lo = (v - hi.astype(F32)).astype(BF16)
    return _dot_nn(hi, lo_half) + _dot_nn(lo, lo_half)


def _head_rinv(x, lo_half):
    return lax.rsqrt(_half_sum(x * x, lo_half) * (1.0 / HD) + EPS)


def _head_norm_bwd(dn, x, g, lo_half):
    r = _head_rinv(x, lo_half)
    dng = dn * g
    dx = r * (dng - x * ((r * r) * (_half_sum(dng * x, lo_half) * (1.0 / HD))))
    dg = jnp.sum(dn * (x * r), axis=0, keepdims=True)
    return dx, dg


def _rope_swap(x, lane):
    l64 = lane & (HD - 1)
    return jnp.where(l64 < ROT // 2, pltpu.roll(x, LANES - ROT // 2, 1), pltpu.roll(x, ROT // 2, 1))


def _rope_fwd(x, cos, sin, lane):
    return x * cos + _rope_swap(x, lane) * sin


def _rope_bwd(dy, cos, sin, lane):
    return dy * cos + jnp.where((lane & (HD - 1)) < ROT, _rope_swap(dy * sin, lane), 0.0)


def _g2(g_ref):
    g = g_ref[...]
    return jnp.concatenate([g, g], axis=-1)


def _pairs(width):
    return [slice(LANES * c, LANES * (c + 1)) for c in range(width // LANES)]


def _pick_lane(block, lane, idx):
    return jnp.sum(jnp.where(lane == idx, block, 0.0), axis=-1, keepdims=True)


def _head_a(x, g, wa, gq, gk, b_pad):
    def body(x_ref, g_ref, w_ref, gq_ref, gk_ref, b_ref, u_ref, p_ref, qo_ref, ko_ref, vo_ref, c_ref, cbc_ref, carry):
        @pl.when(pl.program_id(0) == 0)
        def _():
            carry[...] = jnp.zeros_like(carry)

        xv = x_ref[...]
        u = ((xv * _rms_rinv(xv)) * g_ref[...]).astype(BF16)
        u_ref[...] = u
        for lo in range(0, NA, D):
            hi = min(lo + D, NA)
            p_ref[:, lo:hi] = _dot_nn(u, w_ref[:, lo:hi])
        lane = _lane_iota((RT, LANES))
        lo_half = lane < HD
        gq2, gk2 = _g2(gq_ref), _g2(gk_ref)
        for c in _pairs(D):
            q = p_ref[:, c]
            k = p_ref[:, D + c.start:D + c.stop]
            qo_ref[:, c] = (((q * _head_rinv(q, lo_half)) * gq2) * QSCALE).astype(BF16)
            ko_ref[:, c] = ((k * _head_rinv(k, lo_half)) * gk2).astype(BF16)
        vo_ref[...] = p_ref[:, 2 * D:3 * D].astype(BF16)

        z = p_ref[:, FOFF:FOFF + LANES] + b_ref[...]
        logf = jnp.minimum(z, 0.0) - jnp.log1p(jnp.exp(-jnp.abs(z)))
        r = lax.broadcasted_iota(jnp.int32, (RT, RT), 0)
        cc = lax.broadcasted_iota(jnp.int32, (RT, RT), 1)
        tri = (r >= cc).astype(F32)
        loc = jnp.dot(tri, logf, precision=lax.Precision.HIGHEST, preferred_element_type=F32) + carry[0:1, :]
        c_ref[...] = loc
        carry[0:1, :] = loc[RT - 1:RT, :]
        for h in range(N_HEADS):
            cbc_ref[:, LANES * h:LANES * (h + 1)] = jnp.broadcast_to(_pick_lane(loc, lane, h), (RT, LANES))

    row = lambda width: pl.BlockSpec((RT, width), lambda i: (i, 0))
    whole = lambda arr: pl.BlockSpec(arr.shape, lambda i: (0,) * arr.ndim, pipeline_mode=pl.Buffered(1))
    return pl.pallas_call(
        body, name="head_a", grid=(S // RT,),
        in_specs=[row(D), whole(g), whole(wa), whole(gq), whole(gk), whole(b_pad)],
        out_specs=[row(D), row(NA), row(D), row(D), row(D), row(LANES), row(N_HEADS * LANES)],
        out_shape=[_sds((S, D), BF16), _sds((S, NA), F32)] + [_sds((S, D), BF16)] * 3
        + [_sds((S, LANES), F32), _sds((S, N_HEADS * LANES), F32)],
        scratch_shapes=[pltpu.VMEM((8, LANES), F32)], compiler_params=_params())(x, g, wa, gq, gk, b_pad)


def _key_le_query(offset, keys=TK):
    r = lax.broadcasted_iota(jnp.int32, (keys, TQ), 0)
    c = lax.broadcasted_iota(jnp.int32, (keys, TQ), 1)
    return (r + offset) <= c


def _widen(tile):
    return jnp.concatenate([tile] * (TQ // LANES), axis=1)


def _fox_fwd(qn, kn, vb, proj, crow, cbc, rider=()):
    nq = S // TQ

    def body(q_ref, k_ref, v_ref, g_ref, cq_ref, cbc_ref, o_ref, z_ref, lse_ref, st_s, pt_s):
        i = pl.program_id(1)
        qs = [q_ref[:, HD * hh:HD * (hh + 1)] for hh in range(HPS)]
        cqs = [cq_ref[hh, 0] for hh in range(HPS)]

        def scores(s, hh):
            off = pl.multiple_of(s * KS, KS)
            kj = k_ref[pl.ds(off, KS), HD * hh:HD * (hh + 1)]
            return (_dot_nt(kj, qs[hh]) + cqs[hh]) - _widen(cbc_ref[pl.ds(off, KS), LANES * hh:LANES * (hh + 1)])

        def values(s, hh, pt):
            off = pl.multiple_of(s * KS, KS)
            return _dot_tn(v_ref[pl.ds(off, KS), HD * hh:HD * (hh + 1)], pt)

        def step(s, slot, carries, mask=None, last=False):
            if not last:
                for hh in range(HPS):
                    st_s[1 - slot, hh] = scores(s + 1, hh)
            pvs = [values(jnp.maximum(s - 1, 0), hh, pt_s[1 - slot, hh]) for hh in range(HPS)]
            out = []
            for hh in range(HPS):
                m, l, acc = carries[hh]
                st = st_s[slot, hh]
                if mask is not None:
                    st = jnp.where(mask, st, -jnp.inf)
                m_new = jnp.maximum(m, jnp.max(st, axis=0, keepdims=True))
                pt = jnp.exp(st - m_new)
                alpha = jnp.exp(m - m_new)
                pt_s[slot, hh] = pt.astype(BF16)
                out.append((m_new, alpha * l + jnp.sum(pt, axis=0, keepdims=True), alpha * (acc + pvs[hh])))
            return tuple(out)

        for hh in range(HPS):
            st_s[0, hh] = scores(0, hh)
            pt_s[1, hh] = jnp.zeros((KS, TQ), BF16)
        one = (jnp.full((1, TQ), -jnp.inf, F32), jnp.zeros((1, TQ), F32), jnp.zeros((HD, TQ), F32))
        carries = lax.fori_loop(0, i, lambda t, cr: step(2 * t + 1, 1, step(2 * t, 0, cr)), (one,) * HPS)
        carries = step(2 * i, 0, carries, mask=_key_le_query(0, KS))
        carries = step(2 * i + 1, 1, carries, mask=_key_le_query(KS, KS), last=True)
        accs = []
        for hh in range(HPS):
            m, l, acc = carries[hh]
            acc = acc + values(2 * i + 1, hh, pt_s[1, hh])
            accs.append(acc / l)
            lse_ref[hh, 0] = m + jnp.log(l)
        o = jnp.concatenate(accs, axis=0).T
        o_ref[...] = o
        g = g_ref[...]
        z_ref[...] = (o * (g * _sigmoid(g))).astype(BF16)

    qblk = pl.BlockSpec((TQ, HW), lambda hp, i: (i, hp))
    full = pl.BlockSpec((S, HW), lambda hp, i: (0, hp))
    rows = pl.BlockSpec((HPS, 1, 1, TQ), lambda hp, i: (hp, i, 0, 0))
    return _call(
        body, name="fox_fwd", args=(qn, kn, vb, proj, crow, cbc), grid=(N_HEADS // HPS, nq),
        in_specs=[qblk, full, full,
                  pl.BlockSpec((TQ, HW), lambda hp, i: (i, GOFF // HW + hp)),
                  rows, pl.BlockSpec((S, HPS * LANES), lambda hp, i: (0, hp))],
        out_specs=[qblk, qblk, rows],
        out_shape=[_sds((S, D), F32), _sds((S, D), BF16), _sds((N_HEADS, nq, 1, TQ), F32)],
        scratch_shapes=[pltpu.VMEM((2, HPS, KS, TQ), F32), pltpu.VMEM((2, HPS, KS, TQ), BF16)], rider=rider)


def _fox_bwd_pre(dh, w_out, proj, o):
    nq, rows = S // TQ, 2 * TQ
    per = rows // TQ

    def body(dh_ref, w_ref, g_ref, o_ref, do_ref, dg_ref, delta_ref):
        g = g_ref[...]
        sg = _sigmoid(g)
        dzv = _dot_nt(dh_ref[...].astype(BF16), w_ref[...])
        ov = o_ref[...]
        do = dzv * (g * sg)
        dg_ref[...] = (dzv * ov * (sg * (1.0 + g * (1.0 - sg)))).astype(BF16)
        do_ref[...] = do.astype(BF16)
        prod_t = (do * ov).T
        for h in range(N_HEADS):
            for b in range(per):
                delta_ref[h, b] = jnp.sum(prod_t[HD * h:HD * (h + 1), TQ * b:TQ * (b + 1)], axis=0, keepdims=True)

    row = pl.BlockSpec((rows, D), lambda i: (i, 0))
    return pl.pallas_call(
        body, name="fox_bwd_pre", grid=(S // rows,),
        in_specs=[row, pl.BlockSpec(w_out.shape, lambda i: (0, 0), pipeline_mode=pl.Buffered(1)),
                  pl.BlockSpec((rows, D), lambda i: (i, GOFF // D)), row],
        out_specs=[row, row, pl.BlockSpec((N_HEADS, per, 1, TQ), lambda i: (0, i, 0, 0))],
        out_shape=[_sds((S, D), BF16), _sds((S, D), BF16), _sds((N_HEADS, nq, 1, TQ), F32)],
        compiler_params=_params())(dh, w_out, proj, o)


def _fox_bwd(qn, kn, vb, dob, lse, delta, crow, cbc, rider=()):
    nq, nkb = S // TQ, S // TK

    def body(q_ref, k_ref, v_ref, do_ref, lse_ref, del_ref, cq_ref, cbc_ref,
             dk_ref, dv_ref, dcs_ref, dq_ref, dr_ref, st_s, dp_s, pt_s, ds_s, dq_acc, dr_acc):
        j = pl.program_id(1)

        @pl.when(j == 0)
        def _():
            dq_acc[...] = jnp.zeros_like(dq_acc)
            dr_acc[...] = jnp.zeros_like(dr_acc)

        kjs = [k_ref[:, HD * hh:HD * (hh + 1)] for hh in range(HPS)]
        vjs = [v_ref[:, HD * hh:HD * (hh + 1)] for hh in range(HPS)]

        def rows_of(ref, u, hh):
            off = pl.multiple_of(u * TQ, TQ)
            return ref[pl.ds(off, TQ), HD * hh:HD * (hh + 1)]

        def products(u, hh):
            st = (_dot_nt(kjs[hh], rows_of(q_ref, u, hh)) + cq_ref[hh, u]) - _widen(
                cbc_ref[:, LANES * hh:LANES * (hh + 1)])
            return st, _dot_nt(vjs[hh], rows_of(do_ref, u, hh))

        def step(u, slot, carries, masked=False):
            nxt = jnp.minimum(u + 1, nq - 1)
            for hh in range(HPS):
                st_s[1 - slot, hh], dp_s[1 - slot, hh] = products(nxt, hh)
            prev = jnp.maximum(u - 1, 0)
            dvs = [_dot_nn(pt_s[1 - slot, hh], rows_of(do_ref, prev, hh)) for hh in range(HPS)]
            dks = [_dot_nn(ds_s[1 - slot, hh], rows_of(q_ref, prev, hh)) for hh in range(HPS)]
            for hh in range(HPS):
                dq_acc[hh, prev] += _dot_tn(kjs[hh], ds_s[1 - slot, hh])
            out = []
            for hh in range(HPS):
                dk, dv, dcs = carries[hh]
                st = st_s[slot, hh]
                if masked:
                    st = jnp.where(_key_le_query((j - u) * TQ), st, -jnp.inf)
                pt = jnp.exp(st - lse_ref[hh, u])
                dst = pt * (dp_s[slot, hh] - del_ref[hh, u])
                pt_s[slot, hh] = pt.astype(BF16)
                ds_s[slot, hh] = dst.astype(BF16)
                dr_acc[hh, u] += jnp.sum(dst, axis=0, keepdims=True)
                out.append((dk + dks[hh], dv + dvs[hh], dcs + (dst[:, :LANES] + dst[:, LANES:])))
            return tuple(out)

        t0 = j // 2
        for hh in range(HPS):
            st_s[0, hh], dp_s[0, hh] = products(2 * t0, hh)
            pt_s[1, hh] = jnp.zeros((TK, TQ), BF16)
            ds_s[1, hh] = jnp.zeros((TK, TQ), BF16)
        one = (jnp.zeros((TK, HD), F32), jnp.zeros((TK, HD), F32), jnp.zeros((TK, LANES), F32))
        carries = step(2 * t0 + 1, 1, step(2 * t0, 0, (one,) * HPS, masked=True), masked=True)
        carries = lax.fori_loop(t0 + 1, nq // 2, lambda t, cr: step(2 * t + 1, 1, step(2 * t, 0, cr)), carries)
        dks, dvs = [], []
        lane = _lane_iota((TK, LANES))
        dcs_all = jnp.zeros((TK, LANES), F32)
        for hh in range(HPS):
            dk, dv, dcs = carries[hh]
            dks.append(dk + _dot_nn(ds_s[1, hh], rows_of(q_ref, nq - 1, hh)))
            dvs.append(dv + _dot_nn(pt_s[1, hh], rows_of(do_ref, nq - 1, hh)))
            dq_acc[hh, nq - 1] += _dot_tn(kjs[hh], ds_s[1, hh])
            dcs_all = jnp.where(lane == HPS * pl.program_id(0) + hh, -jnp.sum(dcs, axis=-1, keepdims=True), dcs_all)
        dcs_ref[0] = dcs_all
        dk_ref[...] = jnp.concatenate(dks, axis=-1)
        dv_ref[...] = jnp.concatenate(dvs, axis=-1).astype(BF16)

        @pl.when(j == nkb - 1)
        def _():
            for i in range(nq):
                dq_ref[TQ * i:TQ * (i + 1), :] = jnp.concatenate([dq_acc[hh, i] for hh in range(HPS)], axis=0).T
            dr_ref[...] = dr_acc[...]

    kblk = pl.BlockSpec((TK, HW), lambda hp, j: (j, hp))
    full = pl.BlockSpec((S, HW), lambda hp, j: (0, hp))
    rows = pl.BlockSpec((HPS, nq, 1, TQ), lambda hp, j: (hp, 0, 0, 0))
    cblk = pl.BlockSpec((TK, HPS * LANES), lambda hp, j: (j, hp))
    return _call(
        body, name="fox_bwd", args=(qn, kn, vb, dob, lse, delta, crow, cbc), grid=(N_HEADS // HPS, nkb),
        in_specs=[full, kblk, kblk, full, rows, rows, rows, cblk],
        out_specs=[kblk, kblk, pl.BlockSpec((1, TK, LANES), lambda hp, j: (hp, j, 0)), full, rows],
        out_shape=[_sds((S, D), F32), _sds((S, D), BF16), _sds((N_HEADS // HPS, S, LANES), F32), _sds((S, D), F32),
                   _sds((N_HEADS, nq, 1, TQ), F32)],
        scratch_shapes=[pltpu.VMEM((2, HPS, TK, TQ), F32), pltpu.VMEM((2, HPS, TK, TQ), F32),
                        pltpu.VMEM((2, HPS, TK, TQ), BF16), pltpu.VMEM((2, HPS, TK, TQ), BF16),
                        pltpu.VMEM((HPS, nq, HD, TQ), F32), pltpu.VMEM((HPS, nq, 1, TQ), F32)], rider=rider)


def _prep_a_bwd(dq, dk, dv, dgate, drow, dcs, proj, b_pad, gq, gk):
    nt = S // TM

    def body(dq_ref, dk_ref, dv_ref, dgt_ref, dr_ref, dcs_ref, xq_ref, xk_ref, f_ref, b_ref, gq_ref, gk_ref,
             o_ref, dgq_ref, dgk_ref, db_ref, carry):
        @pl.when(pl.program_id(0) == 0)
        def _():
            carry[...] = jnp.zeros_like(carry)
            dgq_ref[...] = jnp.zeros_like(dgq_ref)
            dgk_ref[...] = jnp.zeros_like(dgk_ref)
            db_ref[...] = jnp.zeros_like(db_ref)

        lane = _lane_iota((TM, LANES))
        lo_half = _half_ones()
        gq2, gk2 = _g2(gq_ref), _g2(gk_ref)
        dgq, dgk = jnp.zeros((1, LANES), F32), jnp.zeros((1, LANES), F32)
        for c in _pairs(D):
            dxq, dg = _head_norm_bwd(dq_ref[:, c] * QSCALE, xq_ref[:, c], gq2, lo_half)
            o_ref[:, c] = dxq.astype(BF16)
            dgq = dgq + dg
            dxk, dg = _head_norm_bwd(dk_ref[:, c], xk_ref[:, c], gk2, lo_half)
            o_ref[:, D + c.start:D + c.stop] = dxk.astype(BF16)
            dgk = dgk + dg
        dgq_ref[...] += dgq
        dgk_ref[...] += dgk
        o_ref[:, 2 * D:3 * D] = dv_ref[...]
        o_ref[:, GOFF:GOFF + D] = dgt_ref[...]

        dc = dr_ref[...]
        for group in range(N_HEADS // HPS):
            dc = dc + dcs_ref[group]
        r = lax.broadcasted_iota(jnp.int32, (TM, TM), 0)
        c = lax.broadcasted_iota(jnp.int32, (TM, TM), 1)
        tri = (c >= r).astype(F32)
        dlogf = jnp.dot(tri, dc, precision=lax.Precision.HIGHEST, preferred_element_type=F32) + carry[0:1, :]
        carry[0:1, :] = dlogf[0:1, :]
        df = dlogf * (1.0 / (1.0 + jnp.exp(f_ref[...] + b_ref[...])))
        db_ref[...] += jnp.sum(df, axis=0, keepdims=True)
        o_ref[:, FOFF:FOFF + LANES] = df.astype(BF16)
        o_ref[:, FOFF + LANES:NA] = jnp.zeros((TM, NA - FOFF - LANES), BF16)

    rev = lambda width, col: pl.BlockSpec((TM, width), lambda i: (nt - 1 - i, col))
    gspec = pl.BlockSpec((1, HD), lambda i: (0, 0))
    acc = pl.BlockSpec((1, LANES), lambda i: (0, 0))
    return pl.pallas_call(
        body, name="prep_a_bwd", grid=(nt,),
        in_specs=[rev(D, 0), rev(D, 0), rev(D, 0), rev(D, 0), rev(LANES, 0),
                  pl.BlockSpec((N_HEADS // HPS, TM, LANES), lambda i: (0, nt - 1 - i, 0)),
                  rev(D, 0), rev(D, 1), rev(LANES, FOFF // LANES), acc, gspec, gspec],
        out_specs=[rev(NA, 0), acc, acc, acc],
        out_shape=[_sds((S, NA), BF16)] + [_sds((1, LANES), F32)] * 3,
        scratch_shapes=[pltpu.VMEM((8, LANES), F32)],
        compiler_params=_params())(dq, dk, dv, dgate, drow, dcs, proj, proj, proj, b_pad, gq, gk)


def _head_b(x, z_a, w_out_a, g_kv, g_b, w_kv, w_in_b, gq, gk, cos2, sin2):
    nkv = w_kv.shape[1] // 2

    def body(x_ref, z_ref, wo_ref, gkv_ref, gb_ref, wkv_ref, wb_ref, gq_ref, gk_ref, c_ref, s_ref,
             h_ref, ukv_ref, ub_ref, kv_ref, pb_ref, qo_ref, ko_ref, vo_ref):
        xv = x_ref[...] + _dot_nn(z_ref[...], wo_ref[...])
        h_ref[...] = xv
        xn = xv * _rms_rinv(xv)
        ukv = (xn * gkv_ref[...]).astype(BF16)
        ub = (xn * gb_ref[...]).astype(BF16)
        ukv_ref[...] = ukv
        ub_ref[...] = ub
        kv_ref[...] = _dot_nn(ukv, wkv_ref[...])
        for lo in range(0, 2 * D, D):
            pb_ref[:, lo:lo + D] = _dot_nn(ub, wb_ref[:, lo:lo + D])
        lane = _lane_iota((RT, LANES))
        lo_half = lane < HD
        cos, sin = c_ref[...], s_ref[...]
        gq2, gk2 = _g2(gq_ref), _g2(gk_ref)
        for c in _pairs(D):
            q = pb_ref[:, c]
            qo_ref[:, c] = (_rope_fwd((q * _head_rinv(q, lo_half)) * gq2, cos, sin, lane) * QSCALE).astype(BF16)
        for c in _pairs(nkv):
            k = kv_ref[:, c]
            ko_ref[:, c] = _rope_fwd((k * _head_rinv(k, lo_half)) * gk2, cos, sin, lane).astype(BF16)
        vo_ref[...] = kv_ref[:, nkv:2 * nkv].astype(BF16)

    row = lambda width: pl.BlockSpec((RT, width), lambda i: (i, 0))
    whole = lambda arr: pl.BlockSpec(arr.shape, lambda i: (0,) * arr.ndim, pipeline_mode=pl.Buffered(1))
    return pl.pallas_call(
        body, name="head_b", grid=(S // RT,),
        in_specs=[row(D), row(D), whole(w_out_a), whole(g_kv), whole(g_b), whole(w_kv), whole(w_in_b), whole(gq),
                  whole(gk), row(LANES), row(LANES)],
        out_specs=[row(D), row(D), row(D), row(2 * nkv), row(2 * D), row(D), row(nkv), row(nkv)],
        out_shape=[_sds((S, D), F32), _sds((S, D), BF16), _sds((S, D), BF16), _sds((S, 2 * nkv), F32),
                   _sds((S, 2 * D), F32), _sds((S, D), BF16), _sds((S, nkv), BF16), _sds((S, nkv), BF16)],
        compiler_params=_params())(x, z_a, w_out_a, g_kv, g_b, w_kv, w_in_b, gq, gk, cos2, sin2)


N_KV, GRP = 4, 4


def _swa_mask(n):
    r = lax.broadcasted_iota(jnp.int32, (2 * WIN, GRP * WIN), 0)
    q = lax.broadcasted_iota(jnp.int32, (2 * WIN, GRP * WIN), 1) & (WIN - 1)
    return (r > q) & (r <= q + WIN) & ((r >= WIN) | (n > 0))


def _stack4(ref_or_val, base):
    return jnp.concatenate([ref_or_val[:, base + HD * g: base + HD * (g + 1)] for g in range(GRP)], axis=0)


def _unstack4(xt):
    return jnp.concatenate([xt[:, WIN * g:WIN * (g + 1)] for g in range(GRP)], axis=0).T


def _band(prev_ref, cur_ref, kh):
    return jnp.concatenate([prev_ref[:, HD * kh:HD * (kh + 1)], cur_ref[:, HD * kh:HD * (kh + 1)]], axis=0)


def _sink_row(s_ref, first):
    lane = _lane_iota((1, GRP * WIN))
    row = jnp.full((1, GRP * WIN), s_ref[first + GRP - 1], F32)
    for g in range(GRP - 2, -1, -1):
        row = jnp.where(lane < WIN * (g + 1), s_ref[first + g], row)
    return row


def _swa_fwd(qb, ksh, vsh, pb, sinks):
    nb = S // WIN

    def body(q_ref, kp_ref, kc_ref, vp_ref, vc_ref, g_ref, s_ref, o_ref, z_ref, lse_ref):
        n = pl.program_id(0)
        valid = _swa_mask(n)
        outs = []
        for kh in range(N_KV):
            kb, vb = _band(kp_ref, kc_ref, kh), _band(vp_ref, vc_ref, kh)
            st = jnp.where(valid, _dot_nt(kb, _stack4(q_ref, GRP * HD * kh)), -jnp.inf)
            sink = _sink_row(s_ref, GRP * kh)
            m = jnp.maximum(jnp.max(st, axis=0, keepdims=True), sink)
            pt = jnp.exp(st - m)
            l = jnp.sum(pt, axis=0, keepdims=True) + jnp.exp(sink - m)
            outs.append(_unstack4(_dot_tn(vb, pt.astype(BF16)) / l))
            lse = m + jnp.log(l)
            for g in range(GRP):
                lse_ref[GRP * kh + g, 0] = lse[:, WIN * g:WIN * (g + 1)]
        o = jnp.concatenate(outs, axis=-1)
        o_ref[...] = o
        g = g_ref[...]
        z_ref[...] = (o * (g * _sigmoid(g))).astype(BF16)

    row = pl.BlockSpec((WIN, D), lambda n: (n, 0))
    prev = pl.BlockSpec((WIN, N_KV * HD), lambda n: (jnp.maximum(n - 1, 0), 0))
    cur = pl.BlockSpec((WIN, N_KV * HD), lambda n: (n, 0))
    return pl.pallas_call(
        body, name="swa_fwd", grid=(nb,),
        in_specs=[row, prev, cur, prev, cur, pl.BlockSpec((WIN, D), lambda n: (n, 1)),
                  pl.BlockSpec(memory_space=pltpu.SMEM)],
        out_specs=[row, row, pl.BlockSpec((N_HEADS, 1, 1, WIN), lambda n: (0, n, 0, 0))],
        out_shape=[_sds((S, D), F32), _sds((S, D), BF16), _sds((N_HEADS, nb, 1, WIN), F32)],
        compiler_params=_params())(qb, ksh, ksh, vsh, vsh, pb, sinks)


def _swa_bwd(qb, ksh, vsh, dz, o, lse, pb, sinks, gq, cos2, sin2):
    nb = S // WIN

    def body(q_ref, kp_ref, kc_ref, vp_ref, vc_ref, dz_ref, o_ref, lse_ref, x_ref, g_ref, s_ref, gq_ref, c_ref, sn_ref,
             dpb_ref, dka_ref, dkb_ref, dva_ref, dvb_ref, dsink_ref, dgq_ref):
        n = pl.program_id(0)

        @pl.when(n == 0)
        def _():
            dsink_ref[...] = jnp.zeros_like(dsink_ref)
            dgq_ref[...] = jnp.zeros_like(dgq_ref)

        valid = _swa_mask(n)
        g = g_ref[...]
        sg = _sigmoid(g)
        dzv = dz_ref[...]
        ov = o_ref[...]
        do = dzv * (g * sg)
        dpb_ref[:, D:2 * D] = (dzv * ov * (sg * (1.0 + g * (1.0 - sg)))).astype(BF16)
        prod_t = (do * ov).T
        lane1 = _lane_iota((1, LANES))
        dqs, dkas, dkbs, dvas, dvbs = [], [], [], [], []
        dsink = jnp.zeros((1, LANES), F32)
        for kh in range(N_KV):
            kb, vb = _band(kp_ref, kc_ref, kh), _band(vp_ref, vc_ref, kh)
            base = GRP * HD * kh
            qs = _stack4(q_ref, base)
            dos = _stack4(do, base).astype(BF16)
            delta = jnp.concatenate(
                [jnp.sum(prod_t[base + HD * gg:base + HD * (gg + 1), :], axis=0, keepdims=True)
                 for gg in range(GRP)], axis=1)
            lse = jnp.concatenate([lse_ref[GRP * kh + gg, 0] for gg in range(GRP)], axis=1)
            st = jnp.where(valid, _dot_nt(kb, qs), -jnp.inf)
            pt = jnp.exp(st - lse)
            dst = pt * (_dot_nt(vb, dos) - delta)
            dsb = dst.astype(BF16)
            dqs.append(_unstack4(_dot_tn(kb, dsb)))
            dkband = _dot_nn(dsb, qs)
            dvband = _dot_nn(pt.astype(BF16), dos)
            dkbs.append(dkband[0:WIN, :])
            dkas.append(dkband[WIN:2 * WIN, :])
            dvbs.append(dvband[0:WIN, :])
            dvas.append(dvband[WIN:2 * WIN, :])
            ps_delta = jnp.exp(_sink_row(s_ref, GRP * kh) - lse) * delta
            for gg in range(GRP):
                val = jnp.sum(ps_delta[:, WIN * gg:WIN * (gg + 1)], axis=1, keepdims=True)
                dsink = dsink - jnp.where(lane1 == GRP * kh + gg, val, 0.0)
        dka_ref[...] = jnp.concatenate(dkas, axis=-1)
        dkb_ref[...] = jnp.concatenate(dkbs, axis=-1)
        dva_ref[...] = jnp.concatenate(dvas, axis=-1)
        dvb_ref[...] = jnp.concatenate(dvbs, axis=-1)
        dsink_ref[...] += dsink

        lane = _lane_iota((WIN, LANES))
        g2, cos, sin = _g2(gq_ref), c_ref[...], sn_ref[...]
        lo_half = _half_ones()
        dg_tot = jnp.zeros((1, LANES), F32)
        for kh in range(N_KV):
            for c in _pairs(GRP * HD):
                cols = slice(GRP * HD * kh + c.start, GRP * HD * kh + c.stop)
                dn = _rope_bwd(dqs[kh][:, c] * QSCALE, cos, sin, lane)
                dx, dg = _head_norm_bwd(dn, x_ref[:, cols], g2, lo_half)
                dpb_ref[:, cols] = dx.astype(BF16)
                dg_tot = dg_tot + dg
        dgq_ref[...] += dg_tot

    row = pl.BlockSpec((WIN, D), lambda n: (n, 0))
    prev = pl.BlockSpec((WIN, N_KV * HD), lambda n: (jnp.maximum(n - 1, 0), 0))
    cur = pl.BlockSpec((WIN, N_KV * HD), lambda n: (n, 0))
    acc = pl.BlockSpec((1, LANES), lambda n: (0, 0))
    tab = pl.BlockSpec((WIN, LANES), lambda n: (n, 0))
    return pl.pallas_call(
        body, name="swa_bwd", grid=(nb,),
        in_specs=[row, prev, cur, prev, cur, row, row, pl.BlockSpec((N_HEADS, 1, 1, WIN), lambda n: (0, n, 0, 0)),
                  row, pl.BlockSpec((WIN, D), lambda n: (n, 1)), pl.BlockSpec(memory_space=pltpu.SMEM),
                  pl.BlockSpec((1, HD), lambda n: (0, 0)), tab, tab],
        out_specs=[pl.BlockSpec((WIN, 2 * D), lambda n: (n, 0)), cur, cur, cur, cur, acc, acc],
        out_shape=[_sds((S, 2 * D), BF16)] + [_sds((S, 256), F32)] * 4 + [_sds((1, LANES), F32)] * 2,
        compiler_params=_params())(qb, ksh, ksh, vsh, vsh, dz, o, lse, pb, pb, sinks, gq, cos2, sin2)


def _prep_kv_bwd(dka, dkb, dva, dvb, kv, gk, cos2, sin2):
    nt = S // RB
    per = RB // WIN

    def shifted(cur_ref, nxt_ref, has_next):
        return jnp.concatenate([cur_ref[WIN:RB, :], jnp.where(has_next, nxt_ref[...], 0.0)], axis=0)

    def body(dka_ref, dkb_ref, dkn_ref, dva_ref, dvb_ref, dvn_ref, x_ref, g_ref, c_ref, s_ref, o_ref, dgk_ref):
        i, j = pl.program_id(0), pl.program_id(1)

        @pl.when((i == 0) & (j == 0))
        def _():
            dgk_ref[...] = jnp.zeros_like(dgk_ref)

        has_next = i < nt - 1

        @pl.when(j == 0)
        def _():
            lane = _lane_iota((RB, LANES))
            g2, cos, sin = _g2(g_ref), c_ref[...], s_ref[...]
            dy_all = dka_ref[...] + shifted(dkb_ref, dkn_ref, has_next)
            dg_tot = jnp.zeros((1, LANES), F32)
            lo_half = _half_ones()
            for c in _pairs(CB):
                dn = _rope_bwd(dy_all[:, c], cos, sin, lane)
                dx, dg = _head_norm_bwd(dn, x_ref[:, c], g2, lo_half)
                o_ref[:, c] = dx.astype(BF16)
                dg_tot = dg_tot + dg
            dgk_ref[...] += dg_tot

        @pl.when(j == 1)
        def _():
            o_ref[...] = (dva_ref[...] + shifted(dvb_ref, dvn_ref, has_next)).astype(BF16)

    cur = pl.BlockSpec((RB, CB), lambda i, j: (i, 0))
    nxt = pl.BlockSpec((WIN, CB), lambda i, j: (jnp.minimum(per * (i + 1), S // WIN - 1), 0))
    tab = pl.BlockSpec((RB, LANES), lambda i, j: (i, 0))
    return pl.pallas_call(
        body, name="prep_kv_bwd", grid=(nt, 2),
        in_specs=[cur, cur, nxt, cur, cur, nxt, cur, pl.BlockSpec((1, HD), lambda i, j: (0, 0)), tab, tab],
        out_specs=[pl.BlockSpec((RB, CB), lambda i, j: (i, j)), pl.BlockSpec((1, LANES), lambda i, j: (0, 0))],
        out_shape=[_sds((S, 2 * CB), BF16), _sds((1, LANES), F32)],
        compiler_params=_params())(dka, dkb, dkb, dva, dvb, dvb, kv, gk, cos2, sin2)


def _out_b_loss(z, w_out, h1, tgt):
    tm = 2 * RT

    def body(z_ref, w_ref, h_ref, t_ref, dy_ref, l_ref):
        @pl.when(pl.program_id(0) == 0)
        def _():
            l_ref[...] = jnp.zeros_like(l_ref)

        e = (h_ref[...] + _dot_nn(z_ref[...], w_ref[...])) - t_ref[...]
        dy_ref[...] = e * (1.0 / D)
        l_ref[...] += jnp.sum(jnp.sum(e * e, axis=-1, keepdims=True), axis=0, keepdims=True)

    row = pl.BlockSpec((tm, D), lambda i: (i, 0))
    return pl.pallas_call(
        body, name="out_b_loss", grid=(S // tm,),
        in_specs=[row, pl.BlockSpec(w_out.shape, lambda i: (0, 0), pipeline_mode=pl.Buffered(1)), row, row],
        out_specs=[row, pl.BlockSpec((1, LANES), lambda i: (0, 0))],
        out_shape=[_sds((S, D), F32), _sds((1, LANES), F32)], compiler_params=_params())(z, w_out, h1, tgt)


def _du_a_rms_bwd(dproj, wa, x, g, dres, after):
    tm, tk = 1024, NA // 2
    nk = NA // tk

    def body(a_ref, b_ref, x_ref, g_ref, dr_ref, after_ref, dx_ref, dg_ref, acc):
        i, kk = pl.program_id(0), pl.program_id(1)

        @pl.when((i == 0) & (kk == 0))
        def _():
            dg_ref[...] = jnp.zeros_like(dg_ref)

        p = _dot_nt(a_ref[...], b_ref[...])

        @pl.when(kk == 0)
        def _():
            acc[...] = p

        @pl.when(kk == nk - 1)
        def _():
            dx, dg = _rms_bwd_core(acc[...] + p, x_ref[...], g_ref[...])
            dx_ref[...] = dr_ref[...] + dx
            dg_ref[...] += dg

    assert nk == 2
    row = pl.BlockSpec((tm, D), lambda i, kk: (i, 0))
    vec = pl.BlockSpec((1, D), lambda i, kk: (0, 0))
    return pl.pallas_call(
        body, name="du_a_rms_bwd", grid=(S // tm, nk),
        in_specs=[pl.BlockSpec((tm, tk), lambda i, kk: (i, kk)), pl.BlockSpec((D, tk), lambda i, kk: (0, kk)),
                  row, vec, row, pl.BlockSpec(after.shape, lambda i, kk: (0, 0))],
        out_specs=[row, vec], out_shape=[_sds((S, D), F32), _sds((1, D), F32)],
        scratch_shapes=[pltpu.VMEM((tm, D), F32)], compiler_params=_params())(dproj, wa, x, g, dres, after)


def _du_b_rms_bwd(dpb, w_in_b, dkv, w_kv, h1, g_b, g_kv, dy):
    tm = 2 * RT

    def body(ab_ref, wb_ref, akv_ref, wkv_ref, x_ref, gb_ref, gkv_ref, dy_ref, dh_ref, dgb_ref, dgkv_ref):
        @pl.when(pl.program_id(0) == 0)
        def _():
            dgb_ref[...] = jnp.zeros_like(dgb_ref)
            dgkv_ref[...] = jnp.zeros_like(dgkv_ref)

        x = x_ref[...]
        dx1, dg1 = _rms_bwd_core(_dot_nt(ab_ref[...], wb_ref[...]), x, gb_ref[...])
        dx2, dg2 = _rms_bwd_core(_dot_nt(akv_ref[...], wkv_ref[...]), x, gkv_ref[...])
        dh_ref[...] = dy_ref[...] + dx1 + dx2
        dgb_ref[...] += dg1
        dgkv_ref[...] += dg2

    row = lambda width: pl.BlockSpec((tm, width), lambda i: (i, 0))
    whole = lambda arr: pl.BlockSpec(arr.shape, lambda i: (0, 0), pipeline_mode=pl.Buffered(1))
    vec = pl.BlockSpec((1, D), lambda i: (0, 0))
    return pl.pallas_call(
        body, name="du_b_rms_bwd", grid=(S // tm,),
        in_specs=[row(dpb.shape[1]), whole(w_in_b), row(dkv.shape[1]), whole(w_kv), row(D), vec, vec, row(D)],
        out_specs=[row(D), vec, vec], out_shape=[_sds((S, D), F32), _sds((1, D), F32), _sds((1, D), F32)],
        compiler_params=_params())(dpb, w_in_b, dkv, w_kv, h1, g_b, g_kv, dy)


def _gather_first(w_in_a, w_out_a, w_kv, w_in_b, w_out_b, norm_a_g):
    rows, cols = w_in_a.shape[-2:]
    groups = rows // LANES
    cols_pad = -(-cols // LANES) * LANES

    def body(wia_ref, woa_ref, wkv_ref, wib_ref, wob_ref, ga_ref,
             wa_g, ga_g, woa_s, wkv_s, wib_s, wob_s, wa_s, st_a, st_oa, st_kv, st_ib, st_ob, plane_t, load_sems, *sems):
        sources = [wia_ref, woa_ref.at[0], wkv_ref, wib_ref.at[0], wob_ref.at[0]]
        stages = [st_a, st_oa, st_kv, st_ib, st_ob]
        loads = [pltpu.make_async_copy(src, dst, load_sems.at[i]) for i, (src, dst) in enumerate(zip(sources, stages))]
        for cp in loads:
            cp.start()
        loads[0].wait()
        plane_t[cols_pad - LANES:, :] = jnp.zeros((LANES, LANES), F32)
        for j in range(groups):
            for c0 in range(0, cols, 64):
                n = min(64, cols - c0)
                plane_t[c0:c0 + n, :] = st_a[pl.ds(c0 * groups + j, n, stride=groups), :]
            for c0 in range(0, cols, LANES):
                n = min(LANES, cols - c0)
                wa_s[j * LANES:(j + 1) * LANES, c0:c0 + n] = plane_t[c0:c0 + LANES, :].T[:, :n].astype(BF16)

        def cast_the_rest():
            for cp, stage, out in zip(loads[1:], stages[1:], [woa_s, wkv_s, wib_s, wob_s]):
                cp.wait()
                out[...] = stage[...].astype(BF16)

        _gather_two_level([wa_s, ga_ref], [wa_g, ga_g], sems, meanwhile=cast_the_rest)

    vmem = pl.BlockSpec(memory_space=pltpu.VMEM)
    anyspec = pl.BlockSpec(memory_space=pl.ANY)
    shard = lambda w: _sds(w.shape[-2:], BF16)
    stage = lambda w: pltpu.VMEM(w.shape[-2:], F32)
    return pl.pallas_call(
        body, name="gather_first", in_specs=[anyspec] * 5 + [vmem],
        out_specs=[anyspec, anyspec, vmem, vmem, vmem, vmem],
        out_shape=[_sds((N_DEV,) + w_in_a.shape[-2:], BF16), _sds((N_DEV,) + norm_a_g.shape, F32),
                   shard(w_out_a), shard(w_kv), shard(w_in_b), shard(w_out_b)],
        scratch_shapes=[pltpu.VMEM(w_in_a.shape[-2:], BF16), pltpu.VMEM((cols * groups, LANES), F32), stage(w_out_a),
                        stage(w_kv), stage(w_in_b), stage(w_out_b), pltpu.VMEM((cols_pad, LANES), F32),
                        pltpu.SemaphoreType.DMA((5,))] + _exchange_sems(2),
        compiler_params=pltpu.CompilerParams(vmem_limit_bytes=VMEM_LIMIT, has_side_effects=True))(
            _entry_view(w_in_a).reshape(cols * groups, LANES), w_out_a, w_kv, w_in_b, w_out_b, norm_a_g)


def _pair_reduce(slots):
    n_chip = N_DEV // 2
    _, rows, cols = slots.shape

    def body(s_ref, o_ref, own_v, sib_v, send_sems, recv_sems, local_sems):
        x, y, c = lax.axis_index("x"), lax.axis_index("y"), lax.axis_index("c")
        copies = []
        for j in range(n_chip):
            own = pltpu.make_async_copy(s_ref.at[2 * j + c], own_v.at[j], local_sems.at[j])
            give = pltpu.make_async_remote_copy(
                src_ref=s_ref.at[2 * j + 1 - c], dst_ref=sib_v.at[j], send_sem=send_sems.at[j],
                recv_sem=recv_sems.at[j], device_id=(x, y, 1 - c), device_id_type=pl.DeviceIdType.MESH)
            own.start()
            give.start()
            copies.append((own, give))
        for j, (own, give) in enumerate(copies):
            own.wait()
            give.wait()
            o_ref[j] = (own_v[j].astype(F32) + sib_v[j].astype(F32)).astype(BF16)

    half = _sds((n_chip, rows, cols), slots.dtype)
    return pl.pallas_call(
        body, name="pair_reduce", in_specs=[pl.BlockSpec(memory_space=pl.ANY)],
        out_specs=pl.BlockSpec(memory_space=pltpu.VMEM), out_shape=half,
        scratch_shapes=[pltpu.VMEM(half.shape, half.dtype), pltpu.VMEM(half.shape, half.dtype),
                        pltpu.SemaphoreType.DMA((n_chip,)), pltpu.SemaphoreType.DMA((n_chip,)),
                        pltpu.SemaphoreType.DMA((n_chip,))],
        compiler_params=pltpu.CompilerParams(vmem_limit_bytes=VMEM_LIMIT, has_side_effects=True))(slots)


def _padded_col(c):
    if c < RAW_F:
        return c
    return FOFF + (c - RAW_F) if c < RAW_G else GOFF + (c - RAW_G)


def _shard_pieces():
    width = NA_RAW // N_DEV
    pieces = []
    for d in range(N_DEV):
        cuts = [width * d] + [c for c in (RAW_F, RAW_G) if width * d < c < width * (d + 1)] + [width * (d + 1)]
        for lo, hi in zip(cuts[:-1], cuts[1:]):
            pieces.append((d, lo - width * d, _padded_col(lo), hi - lo))
    return pieces


def _unshard_wa(wa_g):
    def body(w_ref, o_ref):
        o_ref[:, FOFF + N_HEADS:NA] = jnp.zeros((TM, NA - FOFF - N_HEADS), BF16)
        for d, src, dst, width in _shard_pieces():
            o_ref[:, dst:dst + width] = w_ref[d, :, src:src + width]

    return pl.pallas_call(
        body, name="unshard_wa", grid=(D // TM,),
        in_specs=[pl.BlockSpec((N_DEV, TM, NA_RAW // N_DEV), lambda i: (0, i, 0))],
        out_specs=pl.BlockSpec((TM, NA), lambda i: (i, 0)), out_shape=_sds((D, NA), BF16),
        compiler_params=_params())(wa_g)


def _reshard_dwa(dwa):
    def body(g_ref, o_ref):
        for d, src, dst, width in _shard_pieces():
            o_ref[d, :, src:src + width] = g_ref[:, dst:dst + width]

    return pl.pallas_call(
        body, name="reshard_dwa", grid=(D // TM,), in_specs=[pl.BlockSpec((TM, NA), lambda i: (i, 0))],
        out_specs=pl.BlockSpec((N_DEV, TM, NA_RAW // N_DEV), lambda i: (0, i, 0)),
        out_shape=_sds((N_DEV, D, NA_RAW // N_DEV), dwa.dtype), compiler_params=_params())(dwa)


CHIP_FLIPS = (2, 4, 6)


def _chip_exchange_start(partial):
    n = len(CHIP_FLIPS)

    def body(p_ref, land_ref, *rest):
        sends, recvs, token = rest[:n], rest[n:2 * n], rest[2 * n + 2]
        x, y, c = lax.axis_index("x"), lax.axis_index("y"), lax.axis_index("c")
        me = 4 * x + 2 * y + c
        for idx, k in enumerate(CHIP_FLIPS):
            pltpu.make_async_remote_copy(
                src_ref=p_ref.at[(me ^ k) >> 1], dst_ref=land_ref.at[me >> 1], send_sem=sends[idx],
                recv_sem=recvs[idx], device_id=(x ^ ((k >> 2) & 1), y ^ ((k >> 1) & 1), c),
                device_id_type=pl.DeviceIdType.MESH).start()
        token[...] = jnp.zeros_like(token)

    hbm = pl.BlockSpec(memory_space=pltpu.HBM)
    sem = pl.BlockSpec(memory_space=pltpu.SEMAPHORE)
    buf = pltpu.HBM(partial.shape, partial.dtype)
    return pl.pallas_call(
        body, name="chip_exchange_start",
        out_shape=(pltpu.SemaphoreType.DMA(()),) * (2 * n) + (buf, buf, _sds((8, LANES), F32)),
        in_specs=(hbm, hbm), out_specs=(sem,) * (2 * n) + (hbm, hbm, pl.BlockSpec(memory_space=pltpu.VMEM)),
        input_output_aliases={0: 2 * n, 1: 2 * n + 1},
        compiler_params=pltpu.CompilerParams(has_side_effects=pltpu.SideEffectType.DATAFLOW_SIDE_EFFECTING))(
            pltpu.with_memory_space_constraint(partial, pltpu.HBM),
            pltpu.with_memory_space_constraint(lax.empty(partial.shape, partial.dtype), pltpu.HBM))


def _chip_exchange_wait(started, after):
    n = len(CHIP_FLIPS)
    sems, (p_thru, land_thru) = started[:2 * n], started[2 * n:2 * n + 2]

    def body(p_ref, land_ref, *rest):
        sends, recvs = rest[:n], rest[n:2 * n]
        x, y, c = lax.axis_index("x"), lax.axis_index("y"), lax.axis_index("c")
        me = 4 * x + 2 * y + c
        for idx, k in enumerate(CHIP_FLIPS):
            copy = pltpu.make_async_remote_copy(
                src_ref=p_ref.at[(me ^ k) >> 1], dst_ref=land_ref.at[(me ^ k) >> 1], send_sem=sends[idx],
                recv_sem=recvs[idx], device_id=(x ^ ((k >> 2) & 1), y ^ ((k >> 1) & 1), c),
                device_id_type=pl.DeviceIdType.MESH)
            copy.wait_send()
            copy.wait_recv()

    hbm = pl.BlockSpec(memory_space=pltpu.HBM)
    sem = pl.BlockSpec(memory_space=pltpu.SEMAPHORE)
    buf = pltpu.HBM(p_thru.shape, p_thru.dtype)
    return pl.pallas_call(
        body, name="chip_exchange_wait", out_shape=(buf, buf),
        in_specs=(hbm, hbm) + (sem,) * (2 * n) + (pl.BlockSpec(memory_space=pl.ANY),) * len(after),
        out_specs=(hbm, hbm), input_output_aliases={0: 0, 1: 1},
        compiler_params=pltpu.CompilerParams(has_side_effects=pltpu.SideEffectType.DATAFLOW_SIDE_EFFECTING))(
            p_thru, land_thru, *sems, *after)


def _adamw(w, g, m, v):
    m = ADAM_B1 * m + (1.0 - ADAM_B1) * g
    v = ADAM_B2 * v + (1.0 - ADAM_B2) * (g * g)
    m_hat = m / (1.0 - ADAM_B1 ** ADAM_STEP)
    v_hat = v / (1.0 - ADAM_B2 ** ADAM_STEP)
    delta = -ADAM_LR * (m_hat / (jnp.sqrt(v_hat) + ADAM_EPS) + ADAM_WD * w)
    return delta, m, v


def _sum_adamw(recv, w, m, v, name, after=None):
    lead = w.ndim - 2
    rows, cols = w.shape[-2:]
    tr = 256 if rows % 256 == 0 else 128
    n_slots = recv.shape[0]
    extra = [] if after is None else [after]

    def body(*refs):
        r_ref = refs[0]
        w_ref, m_ref, v_ref, g_ref, d_ref, nm_ref, nv_ref = refs[1 + len(extra):]
        g = None
        for slot in range(n_slots):
            g = r_ref[slot].astype(F32) if g is None else g + r_ref[slot].astype(F32)
        g_ref[...] = g
        d_ref[...], nm_ref[...], nv_ref[...] = _adamw(w_ref[...], g, m_ref[...], v_ref[...])

    blk = pl.BlockSpec((None,) * lead + (tr, cols), lambda i: (0,) * lead + (i, 0))
    slots = pl.BlockSpec((n_slots, tr, cols), lambda i: (0, i, 0))
    return pl.pallas_call(
        body, name=name, grid=(rows // tr,),
        in_specs=[slots] + [pl.BlockSpec(a.shape, lambda i: (0, 0)) for a in extra] + [blk, blk, blk],
        out_specs=[blk] * 4, out_shape=[_sds(w.shape, F32)] * 4, compiler_params=_params())(recv, *extra, w, m, v)


def _entry_view(a):
    _, rows, cols = a.shape
    return jnp.transpose(a, (2, 0, 1)).reshape(cols, rows // LANES, LANES)


def _from_entry_view(a):
    cols, groups, lanes = a.shape
    return jnp.transpose(a, (1, 2, 0)).reshape(1, groups * lanes, cols)


def _sum_adamw_entry_view(landing, own, w, m, v, slab, name):
    n_slots, rows, cols = landing.shape
    groups = rows // LANES
    cols_pad = -(-cols // LANES) * LANES

    def body(r_ref, own_ref, w_ref, m_ref, v_ref, g_ref, d_ref, nm_ref, nv_ref, pad_ref, gt_ref):
        j = pl.program_id(0)
        chip = (4 * lax.axis_index("x") + 2 * lax.axis_index("y") + lax.axis_index("c")) >> 1
        pad_ref[:, cols_pad - LANES:] = jnp.zeros((LANES, LANES), F32)
        for r0 in range(0, LANES, 32):
            g = None
            for slot in range(n_slots):
                part = jnp.where(chip == slot, own_ref[slot, r0:r0 + 32], r_ref[slot, r0:r0 + 32])
                g = part.astype(F32) if g is None else g + part.astype(F32)
            pad_ref[r0:r0 + 32, :cols] = g
        for c0 in range(0, cols_pad, LANES):
            gt_ref[c0:c0 + LANES, :] = pad_ref[:, c0:c0 + LANES].T
        def update_plane(plane):
            for c0 in range(0, cols, 64):
                n = min(64, cols - c0)
                at = pl.ds(c0 * groups + plane, n, stride=groups)
                gt = gt_ref[c0:c0 + n, :]
                g_ref[at, :] = gt
                d_ref[at, :], nm_ref[at, :], nv_ref[at, :] = _adamw(w_ref[at, :], gt, m_ref[at, :], v_ref[at, :])

        for plane in range(groups):
            pl.when(j == plane)(lambda plane=plane: update_plane(plane))

    flat = lambda a: _entry_view(a).reshape(cols * groups, LANES)
    whole = pl.BlockSpec((cols * groups, LANES), lambda j: (0, 0))
    slots = pl.BlockSpec((n_slots, LANES, cols), lambda j: (0, j, 0))
    *outs, slab_g = _call(
        body, name=name, args=(landing, own, flat(w), flat(m), flat(v)), grid=(groups,),
        in_specs=[slots, slots, whole, whole, whole], out_specs=[whole] * 4,
        out_shape=[_sds((cols * groups, LANES), F32)] * 4,
        scratch_shapes=[pltpu.VMEM((LANES, cols_pad), F32), pltpu.VMEM((cols_pad, LANES), F32)],
        rider=[("gather_rows", slab)])
    return [_from_entry_view(o.reshape(cols, groups, LANES)) for o in outs], slab_g


SLAB_ROWS = 16
SLOT = {"kv_norm_g": (8, 0, D), "norm_b_g": (9, 0, D), "b_forget": (10, 0, 16), "qnorm_a_g": (10, 128, HD),
        "knorm_a_g": (10, 256, HD), "knorm_b_g": (10, 384, HD), "qnorm_b_g": (10, 512, HD), "sinks": (10, 640, 16)}
SMALL = ["norm_a_g", "b_forget", "qnorm_a_g", "knorm_a_g", "kv_norm_g", "knorm_b_g", "norm_b_g", "qnorm_b_g", "sinks"]


LOSS_ROW = 11


def _pack_small(dg_a, dg_kv, dg_b, db_f, dgq_a, dgk_a, dgk_b, dgq_b, dsinks, lsum):
    def fold(ref):
        return ref[:, 0:HD] + ref[:, HD:2 * HD]

    def body(dga_ref, dgkv_ref, dgb_ref, dbf_ref, dgqa_ref, dgka_ref, dgkb_ref, dgqb_ref, dsk_ref, ls_ref, slab_ref):
        slab_ref[...] = jnp.zeros_like(slab_ref)
        for r in range(N_DEV):
            slab_ref[r:r + 1, 0:LANES] = dga_ref[:, LANES * r:LANES * (r + 1)]
        slab_ref[8:9, :] = dgkv_ref[...]
        slab_ref[9:10, :] = dgb_ref[...]
        slab_ref[10:11, 0:LANES] = dbf_ref[...]
        slab_ref[10:11, 128:128 + HD] = fold(dgqa_ref)
        slab_ref[10:11, 256:256 + HD] = fold(dgka_ref)
        slab_ref[10:11, 384:384 + HD] = fold(dgkb_ref)
        slab_ref[10:11, 512:512 + HD] = fold(dgqb_ref)
        slab_ref[10:11, 640:640 + LANES] = dsk_ref[...]
        slab_ref[LOSS_ROW:LOSS_ROW + 1, 0:LANES] = ls_ref[...]

    return pl.pallas_call(body, name="pack_small", out_shape=_sds((SLAB_ROWS, D), F32), compiler_params=_params())(
        dg_a, dg_kv, dg_b, db_f, dgq_a, dgk_a, dgk_b, dgq_b, dsinks, lsum)


def _small_adamw(recv, ws, ms, vs):
    k = len(SMALL)

    def body(*refs):
        r_ref = refs[0]
        w_refs, m_refs, v_refs = refs[1:1 + k], refs[1 + k:1 + 2 * k], refs[1 + 2 * k:1 + 3 * k]
        outs = refs[1 + 3 * k:1 + 7 * k]
        loss_ref, tot = refs[1 + 7 * k], refs[2 + 7 * k]
        g = r_ref[0]
        for dev in range(1, N_DEV):
            g = g + r_ref[dev]
        tot[...] = g
        loss_ref[...] = tot[LOSS_ROW:LOSS_ROW + 1, 0:LANES] * (0.5 / D)
        me = 4 * lax.axis_index("x") + 2 * lax.axis_index("y") + lax.axis_index("c")
        for p, name in enumerate(SMALL):
            if name == "norm_a_g":
                mine = lax.broadcasted_iota(jnp.int32, (N_DEV, LANES), 0) == me
                gp = jnp.sum(jnp.where(mine, tot[0:N_DEV, 0:LANES], 0.0), axis=0, keepdims=True)
            else:
                row, lo, width = SLOT[name]
                gp = tot[row:row + 1, lo:lo + width]
            d, nm, nv = _adamw(w_refs[p][...], gp, m_refs[p][...], v_refs[p][...])
            outs[p][...] = gp
            outs[k + p][...] = d
            outs[2 * k + p][...] = nm
            outs[3 * k + p][...] = nv

    shapes = [_sds(w.shape, F32) for w in ws]
    return pl.pallas_call(body, name="small_adamw", out_shape=shapes * 4 + [_sds((1, LANES), F32)],
                          scratch_shapes=[pltpu.VMEM((SLAB_ROWS, D), F32)],
                          compiler_params=_params())(recv, *ws, *ms, *vs)


def _rope_tables(positions):
    inv_freq = jnp.power(jnp.float32(ROPE_THETA), -jnp.arange(0, ROT, 2, dtype=F32) / ROT)
    ang = positions.astype(F32)[:, None] * inv_freq[None, :]
    cos, sin = jnp.cos(ang), jnp.sin(ang)
    c64 = jnp.concatenate([cos, cos, jnp.ones((S, HD - ROT), F32)], axis=-1)
    s64 = jnp.concatenate([-sin, sin, jnp.zeros((S, HD - ROT), F32)], axis=-1)
    return jnp.tile(c64, (1, 2)), jnp.tile(s64, (1, 2))


def _local_step(x, tgt, positions, g_a, wa, b_forget, gq_a, gk_a, g_kv, gk_b, g_b, gq_b, sinks,
                woa_s, wkv_s, wib_s, wob_s, adamw_others):
    nq = S // TQ
    cos2, sin2 = _rope_tables(positions)
    b_pad = jnp.pad(b_forget, ((0, 0), (0, LANES - N_HEADS)))

    u_a, proj, qn, kn, vb, ccol, cbc = _head_a(x, g_a, wa, gq_a, gk_a, b_pad)
    crow = ccol[:, :N_HEADS].T.reshape(N_HEADS, nq, 1, TQ)
    o_a, z_a, lse_a, woa_g, wkv_g, w_in_b, wob_g = _fox_fwd(
        qn, kn, vb, proj, crow, cbc,
        rider=[("gather_rows", woa_s), ("gather_rows", wkv_s), ("gather_cols", wib_s), ("gather_rows", wob_s)])
    w_out_a, w_kv, w_out_b = woa_g.reshape(D, D), wkv_g.reshape(D, 512), wob_g.reshape(D, D)
    h1, u_kv, u_b, kv, pb, qb, ksh, vsh = _head_b(x, z_a, w_out_a, g_kv, g_b, w_kv, w_in_b, gq_b, gk_b, cos2, sin2)
    sinks1 = sinks.reshape(N_HEADS)
    o_b, z_b, lse_b = _swa_fwd(qb, ksh, vsh, pb, sinks1)
    dy, lsum = _out_b_loss(z_b, w_out_b, h1, tgt)
    dw_out_b = _mm(z_b, dy, "tn", 512, 512, S, out_dtype=BF16, name="mm_dw_out_b")
    dz_b = _mm(dy, w_out_b, "nt", 1024, 512, D, name="mm_dz_b")
    dpb, dka, dkb, dva, dvb, dsinks, dgq_b = _swa_bwd(qb, ksh, vsh, dz_b, o_b, lse_b, pb, sinks1, gq_b, cos2, sin2)
    dkv, dgk_b = _prep_kv_bwd(dka, dkb, dva, dvb, kv, gk_b, cos2, sin2)
    dw_in_b = _mm(u_b, dpb, "tn", 512, 512, S, out_dtype=BF16, name="mm_dw_in_b")
    dw_kv = _mm(u_kv, dkv, "tn", 512, 512, S, out_dtype=BF16, name="mm_dw_kv")
    dh1, dg_b, dg_kv = _du_b_rms_bwd(dpb, w_in_b, dkv, w_kv, h1, g_b, g_kv, dy)
    dw_out_a = _mm(z_a, dh1, "tn", 512, 512, S, out_dtype=BF16, name="mm_dw_out_a")
    do_a, dgate_a, delta_a = _fox_bwd_pre(dh1, w_out_a, proj, o_a)
    dk_a, dv_a, dcs, dq_a, drow, r_wob, r_wib, r_wkv, r_woa = _fox_bwd(
        qn, kn, vb, do_a, lse_a, delta_a, crow, cbc,
        rider=[("a2a_rows", dw_out_b), ("a2a_cols", dw_in_b), ("a2a_rows", dw_kv), ("a2a_rows", dw_out_a)])
    drow_col = jnp.pad(drow.reshape(N_HEADS, S).T, ((0, 0), (0, LANES - N_HEADS)))
    dproj, dgq_a, dgk_a, db_f = _prep_a_bwd(dq_a, dk_a, dv_a, dgate_a, drow_col, dcs, proj, b_pad, gq_a, gk_a)
    dwa = _mm(u_a, dproj, "tn", 1024, 256, S, out_dtype=BF16, name="mm_dw_in_a")
    partial = _pair_reduce(_reshard_dwa(dwa))
    started = _chip_exchange_start(partial)
    dx, dg_a = _du_a_rms_bwd(dproj, wa, x, g_a, dh1, after=started[-1])
    others = adamw_others(dict(w_out_a=r_woa, w_kv=r_wkv, w_in_b=r_wib, w_out_b=r_wob), dg_a)
    partial, landed = _chip_exchange_wait(started, [res[0] for res in others.values()])
    slab = _pack_small(dg_a, dg_kv, dg_b, db_f, dgq_a, dgk_a, dgk_b, dgq_b, dsinks, lsum)
    return dx, (landed, partial), others, slab


def kernel(x, positions, norm_a_g, w_in_a, b_forget, qnorm_a_g, knorm_a_g, w_out_a, kv_norm_g, w_kv, knorm_b_g, norm_b_g, w_in_b, qnorm_b_g, sinks, w_out_b, loss_target, m_norm_a_g, m_w_in_a, m_b_forget, m_qnorm_a_g, m_knorm_a_g, m_w_out_a, m_kv_norm_g, m_w_kv, m_knorm_b_g, m_norm_b_g, m_w_in_b, m_qnorm_b_g, m_sinks, m_w_out_b, v_norm_a_g, v_w_in_a, v_b_forget, v_qnorm_a_g, v_knorm_a_g, v_w_out_a, v_kv_norm_g, v_w_kv, v_knorm_b_g, v_norm_b_g, v_w_in_b, v_qnorm_b_g, v_sinks, v_w_out_b):
    wa_g, ga_g, woa_s, wkv_s, wib_s, wob_s = _gather_first(w_in_a, w_out_a, w_kv, w_in_b, w_out_b, norm_a_g)
    state = dict(w_in_a=(w_in_a, m_w_in_a, v_w_in_a), w_out_a=(w_out_a, m_w_out_a, v_w_out_a),
                 w_kv=(w_kv, m_w_kv, v_w_kv), w_in_b=(w_in_b, m_w_in_b, v_w_in_b),
                 w_out_b=(w_out_b, m_w_out_b, v_w_out_b))

    def adamw_others(landed, after):
        return {n: _sum_adamw(r, *state[n], "adamw_" + n, after=after) for n, r in landed.items()}

    dx, r_wa, big, slab = _local_step(
        x[0], loss_target[0], positions, ga_g.reshape(1, D), _unshard_wa(wa_g), b_forget, qnorm_a_g, knorm_a_g,
        kv_norm_g.reshape(1, D), knorm_b_g.reshape(1, HD), norm_b_g, qnorm_b_g, sinks, woa_s, wkv_s, wib_s, wob_s,
        adamw_others)
    big["w_in_a"], slab_g = _sum_adamw_entry_view(*r_wa, *state["w_in_a"], slab, "adamw_w_in_a")

    r2 = lambda a: a.reshape(1, -1)
    small_w = dict(norm_a_g=norm_a_g, b_forget=b_forget, qnorm_a_g=qnorm_a_g, knorm_a_g=knorm_a_g,
                   kv_norm_g=kv_norm_g, knorm_b_g=knorm_b_g, norm_b_g=norm_b_g, qnorm_b_g=qnorm_b_g, sinks=sinks)
    small_m = dict(norm_a_g=m_norm_a_g, b_forget=m_b_forget, qnorm_a_g=m_qnorm_a_g, knorm_a_g=m_knorm_a_g,
                   kv_norm_g=m_kv_norm_g, knorm_b_g=m_knorm_b_g, norm_b_g=m_norm_b_g, qnorm_b_g=m_qnorm_b_g,
                   sinks=m_sinks)
    small_v = dict(norm_a_g=v_norm_a_g, b_forget=v_b_forget, qnorm_a_g=v_qnorm_a_g, knorm_a_g=v_knorm_a_g,
                   kv_norm_g=v_kv_norm_g, knorm_b_g=v_knorm_b_g, norm_b_g=v_norm_b_g, qnorm_b_g=v_qnorm_b_g,
                   sinks=v_sinks)
    res = _small_adamw(slab_g, [r2(small_w[n]) for n in SMALL], [r2(small_m[n]) for n in SMALL],
                       [r2(small_v[n]) for n in SMALL])
    k = len(SMALL)
    small = {n: [res[q * k + p].reshape(small_w[n].shape) for q in range(4)] for p, n in enumerate(SMALL)}
    loss = res[4 * k][0, 0]

    order = ["norm_a_g", "w_in_a", "b_forget", "qnorm_a_g", "knorm_a_g", "w_out_a", "kv_norm_g", "w_kv",
             "knorm_b_g", "norm_b_g", "w_in_b", "qnorm_b_g", "sinks", "w_out_b"]

    def leaf(n, q):
        return big[n][q] if n in big else small[n][q]

    outs = [loss, dx[None]]
    for q in range(4):
        outs.extend(leaf(n, q) for n in order)
    return tuple(outs)
```

```python
import jax
import jax.numpy as jnp
from jax import lax
from jax.experimental import pallas as pl
from jax.experimental.pallas import tpu as pltpu

F32, BF16 = jnp.float32, jnp.bfloat16

S = 2048
D = 1024
HD = 64
N_HEADS = 16
N_DEV = 8
NA = 4352
GOFF = 3072
FOFF = 4096
RAW_F = 3072
RAW_G = RAW_F + N_HEADS
NA_RAW = 4112
EPS = 1e-6
QSCALE = 0.125
ROPE_THETA = 500000.0
ROT = 16
WIN = 128
TQ = 256
TK = 256
KS = TQ // 2
HPS = 8
HW = HPS * HD
TM = 256
RT = 256
RB = 512
CB = 256
LANES = 128

ADAM_LR, ADAM_B1, ADAM_B2, ADAM_EPS, ADAM_WD, ADAM_STEP = 0.001, 0.9, 0.999, 1e-08, 0.01, 10

VMEM_LIMIT = 56 * 1024 * 1024


def _params():
    return pltpu.CompilerParams(vmem_limit_bytes=VMEM_LIMIT)


def _sds(shape, dtype):
    return jax.ShapeDtypeStruct(shape, dtype)


def _dot_nt(a, b):
    return lax.dot_general(a, b, (((1,), (1,)), ((), ())), preferred_element_type=F32)


def _dot_tn(a, b):
    return lax.dot_general(a, b, (((0,), (0,)), ((), ())), preferred_element_type=F32)


def _dot_nn(a, b):
    return lax.dot_general(a, b, (((1,), (0,)), ((), ())), preferred_element_type=F32)


def _sigmoid(g):
    return 1.0 / (1.0 + jnp.exp(-g))


def _lane_iota(shape):
    return lax.broadcasted_iota(jnp.int32, shape, len(shape) - 1)


def _flips(kind):
    return (2, 4, 6) if kind == "a2a_chips" else tuple(range(1, N_DEV))


def _send_view(kind, ref, dev):
    if kind in ("gather_rows", "gather_cols"):
        return ref
    if kind == "a2a_slots":
        return ref.at[dev]
    if kind == "a2a_chips":
        return ref.at[dev >> 1]
    if kind == "a2a_rows":
        rows = ref.shape[0] // N_DEV
        return ref.at[pl.ds(pl.multiple_of(dev * rows, rows), rows)]
    cols = ref.shape[1] // N_DEV
    return ref.at[:, pl.ds(pl.multiple_of(dev * cols, cols), cols)]


def _land_view(kind, ref, dev):
    if kind == "gather_cols":
        cols = ref.shape[1] // N_DEV
        return ref.at[:, pl.ds(pl.multiple_of(dev * cols, cols), cols)]
    if kind == "a2a_chips":
        return ref.at[dev >> 1]
    return ref.at[dev]


def _landing_sds(kind, arr):
    if kind == "gather_rows":
        return _sds((N_DEV,) + arr.shape, arr.dtype)
    if kind == "gather_cols":
        return _sds((arr.shape[0], N_DEV * arr.shape[1]), arr.dtype)
    if kind == "a2a_rows":
        return _sds((N_DEV, arr.shape[0] // N_DEV, arr.shape[1]), arr.dtype)
    if kind == "a2a_cols":
        return _sds((N_DEV, arr.shape[0], arr.shape[1] // N_DEV), arr.dtype)
    return _sds(arr.shape, arr.dtype)


def _exchange_sems(n_parts):
    n = n_parts * (N_DEV - 1)
    return [pltpu.SemaphoreType.DMA((n,)), pltpu.SemaphoreType.DMA((n,)), pltpu.SemaphoreType.DMA((n_parts,))]


def _exchange_ops(kinds, srcs, dsts, sems, start, wait):
    send_sems, recv_sems, local_sems = sems
    x, y, c = lax.axis_index("x"), lax.axis_index("y"), lax.axis_index("c")
    me = 4 * x + 2 * y + c

    def local(a):
        return pltpu.make_async_copy(_send_view(kinds[a], srcs[a], me), _land_view(kinds[a], dsts[a], me),
                                     local_sems.at[a])

    def remote(a, k, landing_dev):
        peer = (x ^ ((k >> 2) & 1), y ^ ((k >> 1) & 1), c ^ (k & 1))
        sem = a * (N_DEV - 1) + k - 1
        return pltpu.make_async_remote_copy(
            src_ref=_send_view(kinds[a], srcs[a], me ^ k), dst_ref=_land_view(kinds[a], dsts[a], landing_dev),
            send_sem=send_sems.at[sem], recv_sem=recv_sems.at[sem], device_id=peer,
            device_id_type=pl.DeviceIdType.MESH)

    pairs = [(a, k) for k in range(1, N_DEV) for a in range(len(kinds)) if k in _flips(kinds[a])]
    if start:
        for a in range(len(kinds)):
            local(a).start()
        for a, k in pairs:
            remote(a, k, me).start()
    if wait:
        for a, k in pairs:
            remote(a, k, me ^ k).wait_recv()
            remote(a, k, me).wait_send()
        for a in range(len(kinds)):
            local(a).wait()


def _gather_two_level(srcs, dsts, sems, meanwhile=None):
    send_sems, recv_sems, local_sems = sems
    x, y, c = lax.axis_index("x"), lax.axis_index("y"), lax.axis_index("c")
    me, sibling = (x, y, c), (x, y, 1 - c)
    chips = [(1 - x, y), (x, 1 - y), (1 - x, 1 - y)]

    def slot(ref, dev):
        return ref.at[4 * dev[0] + 2 * dev[1] + dev[2]]

    def copy(a, k, block, to, src=None):
        return pltpu.make_async_remote_copy(
            src_ref=slot(dsts[a], block) if src is None else src, dst_ref=slot(dsts[a], block),
            send_sem=send_sems.at[a * (N_DEV - 1) + k], recv_sem=recv_sems.at[a * (N_DEV - 1) + k],
            device_id=to, device_id_type=pl.DeviceIdType.MESH)

    parts = range(len(srcs))
    mine = [pltpu.make_async_copy(srcs[a], slot(dsts[a], me), local_sems.at[a]) for a in parts]
    first = [copy(a, 0, me, sibling, src=srcs[a]) for a in parts]
    first += [copy(a, 1 + j, me, (*chip, c), src=srcs[a]) for j, chip in enumerate(chips) for a in parts]
    for cp in mine + first:
        cp.start()
    if meanwhile is not None:
        meanwhile()
    passed = []
    for j, chip in enumerate(chips):
        for a in parts:
            copy(a, 1 + j, (*chip, c), me).wait_recv()
            fwd = copy(a, 4 + j, (*chip, c), sibling)
            fwd.start()
            passed.append(fwd)
    for a in parts:
        copy(a, 0, sibling, me).wait_recv()
        for j, chip in enumerate(chips):
            copy(a, 4 + j, (*chip, 1 - c), me).wait_recv()
    for cp in first + passed:
        cp.wait_send()
    for cp in mine:
        cp.wait()


def _call(body, *, name, args, in_specs, out_specs, out_shape, grid=(), scratch_shapes=(), aliases=None, rider=()):
    n_in, n_out, n_scr, n_r = len(in_specs), len(out_specs), len(scratch_shapes), len(rider)
    kinds = [kind for kind, _ in rider]

    def kernel_body(*refs):
        c_in, r_in = refs[:n_in], refs[n_in:n_in + n_r]
        c_out = refs[n_in + n_r:n_in + n_r + n_out]
        r_out = refs[n_in + n_r + n_out:n_in + 2 * n_r + n_out]
        rest = refs[n_in + 2 * n_r + n_out:]
        c_scr, sems = rest[:n_scr], rest[n_scr:]
        if n_r:
            assert grid, "a rider needs a gridded call"
            ids = [pl.program_id(ax) for ax in range(len(grid))]
            first, last = ids[0] == 0, ids[0] == grid[0] - 1
            for pid, size in zip(ids[1:], grid[1:]):
                first = first & (pid == 0)
                last = last & (pid == size - 1)
            pl.when(first)(lambda: _exchange_ops(kinds, r_in, r_out, sems, True, False))
        body(*c_in, *c_out, *c_scr)
        if n_r:
            pl.when(last)(lambda: _exchange_ops(kinds, r_in, r_out, sems, False, True))

    anyspec = pl.BlockSpec(memory_space=pl.ANY)
    params = pltpu.CompilerParams(vmem_limit_bytes=VMEM_LIMIT, has_side_effects=bool(n_r))
    outs = pl.pallas_call(
        kernel_body, name=name, grid=grid, in_specs=list(in_specs) + [anyspec] * n_r,
        out_specs=list(out_specs) + [anyspec] * n_r,
        out_shape=list(out_shape) + [_landing_sds(kind, arr) for kind, arr in rider],
        scratch_shapes=list(scratch_shapes) + (_exchange_sems(n_r) if n_r else []),
        input_output_aliases=aliases or {}, compiler_params=params)(*args, *[arr for _, arr in rider])
    return list(outs)


def _mm(a, b, mode, tm, tn, tk, out_dtype=F32, add=None, name="mm", rider=()):
    if mode == "nn":
        (m, k), n = a.shape, b.shape[1]
        a_spec = pl.BlockSpec((tm, tk), lambda i, j, kk: (i, kk))
        b_spec = pl.BlockSpec((tk, tn), lambda i, j, kk: (kk, j))
        dot = _dot_nn
    elif mode == "nt":
        (m, k), n = a.shape, b.shape[0]
        a_spec = pl.BlockSpec((tm, tk), lambda i, j, kk: (i, kk))
        b_spec = pl.BlockSpec((tn, tk), lambda i, j, kk: (j, kk))
        dot = _dot_nt
    else:
        (k, m), n = a.shape, b.shape[1]
        a_spec = pl.BlockSpec((tk, tm), lambda i, j, kk: (kk, i))
        b_spec = pl.BlockSpec((tk, tn), lambda i, j, kk: (kk, j))
        dot = _dot_tn
    assert m % tm == 0 and n % tn == 0 and k % tk == 0, (m, n, k, tm, tn, tk)
    nk = k // tk
    has_add = add is not None

    def body(*refs):
        if has_add:
            a_ref, b_ref, add_ref, o_ref, acc = refs
        else:
            a_ref, b_ref, o_ref, acc = refs
        p = dot(a_ref[...].astype(BF16), b_ref[...].astype(BF16))

        def finish(total):
            if has_add:
                total = add_ref[...] + total
            o_ref[...] = total.astype(out_dtype)

        if nk == 1:
            finish(p)
        else:
            kk = pl.program_id(2)

            @pl.when(kk == 0)
            def _():
                acc[...] = p

            @pl.when(kk > 0)
            def _():
                acc[...] += p

            @pl.when(kk == nk - 1)
            def _():
                finish(acc[...])

    in_specs = [a_spec, b_spec]
    args = [a, b]
    if has_add:
        in_specs.append(pl.BlockSpec((tm, tn), lambda i, j, kk: (i, j)))
        args.append(add)
    acc_shape = (tm, tn) if nk > 1 else (8, LANES)
    outs = _call(body, name=name, args=args, grid=(m // tm, n // tn, nk), in_specs=in_specs,
                 out_specs=[pl.BlockSpec((tm, tn), lambda i, j, kk: (i, j))], out_shape=[_sds((m, n), out_dtype)],
                 scratch_shapes=[pltpu.VMEM(acc_shape, F32)], rider=rider)
    return outs if rider else outs[0]


def _rms_rinv(x):
    return lax.rsqrt(jnp.mean(x * x, axis=-1, keepdims=True) + EPS)


def _rms_bwd_core(du, x, g):
    r = _rms_rinv(x)
    dug = du * g
    dx = r * (dug - x * ((r * r) * jnp.mean(dug * x, axis=-1, keepdims=True)))
    dg = jnp.sum(du * (x * r), axis=0, keepdims=True)
    return dx, dg


def _half_ones():
    r = lax.broadcasted_iota(jnp.int32, (LANES, LANES), 0)
    c = lax.broadcasted_iota(jnp.int32, (LANES, LANES), 1)
    return ((r < HD) == (c < HD)).astype(BF16)


def _half_sum(v, lo_half):
    if lo_half.dtype == jnp.bool_:
        s0 = jnp.sum(jnp.where(lo_half, v, 0.0), axis=-1, keepdims=True)
        s1 = jnp.sum(jnp.where(lo_half, 0.0, v), axis=-1, keepdims=True)
        return jnp.where(lo_half, s0, s1)
    hi = v.astype(BF16)
    lo = (v - hi.astype(F32)).astype(BF16)
    return _dot_nn(hi, lo_half) + _dot_nn(lo, lo_half)


def _head_rinv(x, lo_half):
    return lax.rsqrt(_half_sum(x * x, lo_half) * (1.0 / HD) + EPS)


def _head_norm_bwd(dn, x, g, lo_half):
    r = _head_rinv(x, lo_half)
    dng = dn * g
    dx = r * (dng - x * ((r * r) * (_half_sum(dng * x, lo_half) * (1.0 / HD))))
    dg = jnp.sum(dn * (x * r), axis=0, keepdims=True)
    return dx, dg


def _rope_swap(x, lane):
    l64 = lane & (HD - 1)
    return jnp.where(l64 < ROT // 2, pltpu.roll(x, LANES - ROT // 2, 1), pltpu.roll(x, ROT // 2, 1))


def _rope_fwd(x, cos, sin, lane):
    return x * cos + _rope_swap(x, lane) * sin


def _rope_bwd(dy, cos, sin, lane):
    return dy * cos + jnp.where((lane & (HD - 1)) < ROT, _rope_swap(dy * sin, lane), 0.0)


def _g2(g_ref):
    g = g_ref[...]
    return jnp.concatenate([g, g], axis=-1)


def _pairs(width):
    return [slice(LANES * c, LANES * (c + 1)) for c in range(width // LANES)]


def _pick_lane(block, lane, idx):
    return jnp.sum(jnp.where(lane == idx, block, 0.0), axis=-1, keepdims=True)


def _head_a(x, g, wa, gq, gk, b_pad):
    def body(x_ref, g_ref, w_ref, gq_ref, gk_ref, b_ref, u_ref, p_ref, qo_ref, ko_ref, vo_ref, c_ref, cbc_ref, carry):
        @pl.when(pl.program_id(0) == 0)
        def _():
            carry[...] = jnp.zeros_like(carry)

        xv = x_ref[...]
        u = ((xv * _rms_rinv(xv)) * g_ref[...]).astype(BF16)
        u_ref[...] = u
        for lo in range(0, NA, D):
            hi = min(lo + D, NA)
            p_ref[:, lo:hi] = _dot_nn(u, w_ref[:, lo:hi])
        lane = _lane_iota((RT, LANES))
        lo_half = lane < HD
        gq2, gk2 = _g2(gq_ref), _g2(gk_ref)
        for c in _pairs(D):
            q = p_ref[:, c]
            k = p_ref[:, D + c.start:D + c.stop]
            qo_ref[:, c] = (((q * _head_rinv(q, lo_half)) * gq2) * QSCALE).astype(BF16)
            ko_ref[:, c] = ((k * _head_rinv(k, lo_half)) * gk2).astype(BF16)
        vo_ref[...] = p_ref[:, 2 * D:3 * D].astype(BF16)

        z = p_ref[:, FOFF:FOFF + LANES] + b_ref[...]
        logf = jnp.minimum(z, 0.0) - jnp.log1p(jnp.exp(-jnp.abs(z)))
        r = lax.broadcasted_iota(jnp.int32, (RT, RT), 0)
        cc = lax.broadcasted_iota(jnp.int32, (RT, RT), 1)
        tri = (r >= cc).astype(F32)
        loc = jnp.dot(tri, logf, precision=lax.Precision.HIGHEST, preferred_element_type=F32) + carry[0:1, :]
        c_ref[...] = loc
        carry[0:1, :] = loc[RT - 1:RT, :]
        for h in range(N_HEADS):
            cbc_ref[:, LANES * h:LANES * (h + 1)] = jnp.broadcast_to(_pick_lane(loc, lane, h), (RT, LANES))

    row = lambda width: pl.BlockSpec((RT, width), lambda i: (i, 0))
    whole = lambda arr: pl.BlockSpec(arr.shape, lambda i: (0,) * arr.ndim, pipeline_mode=pl.Buffered(1))
    return pl.pallas_call(
        body, name="head_a", grid=(S // RT,),
        in_specs=[row(D), whole(g), whole(wa), whole(gq), whole(gk), whole(b_pad)],
        out_specs=[row(D), row(NA), row(D), row(D), row(D), row(LANES), row(N_HEADS * LANES)],
        out_shape=[_sds((S, D), BF16), _sds((S, NA), F32)] + [_sds((S, D), BF16)] * 3
        + [_sds((S, LANES), F32), _sds((S, N_HEADS * LANES), F32)],
        scratch_shapes=[pltpu.VMEM((8, LANES), F32)], compiler_params=_params())(x, g, wa, gq, gk, b_pad)


def _key_le_query(offset, keys=TK):
    r = lax.broadcasted_iota(jnp.int32, (keys, TQ), 0)
    c = lax.broadcasted_iota(jnp.int32, (keys, TQ), 1)
    return (r + offset) <= c


def _widen(tile):
    return jnp.concatenate([tile] * (TQ // LANES), axis=1)


def _fox_fwd(qn, kn, vb, proj, crow, cbc, rider=()):
    nq = S // TQ

    def body(q_ref, k_ref, v_ref, g_ref, cq_ref, cbc_ref, o_ref, z_ref, lse_ref, st_s, pt_s):
        i = pl.program_id(1)
        qs = [q_ref[:, HD * hh:HD * (hh + 1)] for hh in range(HPS)]
        cqs = [cq_ref[hh, 0] for hh in range(HPS)]

        def scores(s, hh):
            off = pl.multiple_of(s * KS, KS)
            kj = k_ref[pl.ds(off, KS), HD * hh:HD * (hh + 1)]
            return (_dot_nt(kj, qs[hh]) + cqs[hh]) - _widen(cbc_ref[pl.ds(off, KS), LANES * hh:LANES * (hh + 1)])

        def values(s, hh, pt):
            off = pl.multiple_of(s * KS, KS)
            return _dot_tn(v_ref[pl.ds(off, KS), HD * hh:HD * (hh + 1)], pt)

        def step(s, slot, carries, mask=None, last=False):
            if not last:
                for hh in range(HPS):
                    st_s[1 - slot, hh] = scores(s + 1, hh)
            pvs = [values(jnp.maximum(s - 1, 0), hh, pt_s[1 - slot, hh]) for hh in range(HPS)]
            out = []
            for hh in range(HPS):
                m, l, acc = carries[hh]
                st = st_s[slot, hh]
                if mask is not None:
                    st = jnp.where(mask, st, -jnp.inf)
                m_new = jnp.maximum(m, jnp.max(st, axis=0, keepdims=True))
                pt = jnp.exp(st - m_new)
                alpha = jnp.exp(m - m_new)
                pt_s[slot, hh] = pt.astype(BF16)
                out.append((m_new, alpha * l + jnp.sum(pt, axis=0, keepdims=True), alpha * (acc + pvs[hh])))
            return tuple(out)

        for hh in range(HPS):
            st_s[0, hh] = scores(0, hh)
            pt_s[1, hh] = jnp.zeros((KS, TQ), BF16)
        one = (jnp.full((1, TQ), -jnp.inf, F32), jnp.zeros((1, TQ), F32), jnp.zeros((HD, TQ), F32))
        carries = lax.fori_loop(0, i, lambda t, cr: step(2 * t + 1, 1, step(2 * t, 0, cr)), (one,) * HPS)
        carries = step(2 * i, 0, carries, mask=_key_le_query(0, KS))
        carries = step(2 * i + 1, 1, carries, mask=_key_le_query(KS, KS), last=True)
        accs = []
        for hh in range(HPS):
            m, l, acc = carries[hh]
            acc = acc + values(2 * i + 1, hh, pt_s[1, hh])
            accs.append(acc / l)
            lse_ref[hh, 0] = m + jnp.log(l)
        o = jnp.concatenate(accs, axis=0).T
        o_ref[...] = o
        g = g_ref[...]
        z_ref[...] = (o * (g * _sigmoid(g))).astype(BF16)

    qblk = pl.BlockSpec((TQ, HW), lambda hp, i: (i, hp))
    full = pl.BlockSpec((S, HW), lambda hp, i: (0, hp))
    rows = pl.BlockSpec((HPS, 1, 1, TQ), lambda hp, i: (hp, i, 0, 0))
    return _call(
        body, name="fox_fwd", args=(qn, kn, vb, proj, crow, cbc), grid=(N_HEADS // HPS, nq),
        in_specs=[qblk, full, full,
                  pl.BlockSpec((TQ, HW), lambda hp, i: (i, GOFF // HW + hp)),
                  rows, pl.BlockSpec((S, HPS * LANES), lambda hp, i: (0, hp))],
        out_specs=[qblk, qblk, rows],
        out_shape=[_sds((S, D), F32), _sds((S, D), BF16), _sds((N_HEADS, nq, 1, TQ), F32)],
        scratch_shapes=[pltpu.VMEM((2, HPS, KS, TQ), F32), pltpu.VMEM((2, HPS, KS, TQ), BF16)], rider=rider)


def _fox_bwd_pre(dh, w_out, proj, o):
    nq, rows = S // TQ, 2 * TQ
    per = rows // TQ

    def body(dh_ref, w_ref, g_ref, o_ref, do_ref, dg_ref, delta_ref):
        g = g_ref[...]
        sg = _sigmoid(g)
        dzv = _dot_nt(dh_ref[...].astype(BF16), w_ref[...])
        ov = o_ref[...]
        do = dzv * (g * sg)
        dg_ref[...] = (dzv * ov * (sg * (1.0 + g * (1.0 - sg)))).astype(BF16)
        do_ref[...] = do.astype(BF16)
        prod_t = (do * ov).T
        for h in range(N_HEADS):
            for b in range(per):
                delta_ref[h, b] = jnp.sum(prod_t[HD * h:HD * (h + 1), TQ * b:TQ * (b + 1)], axis=0, keepdims=True)

    row = pl.BlockSpec((rows, D), lambda i: (i, 0))
    return pl.pallas_call(
        body, name="fox_bwd_pre", grid=(S // rows,),
        in_specs=[row, pl.BlockSpec(w_out.shape, lambda i: (0, 0), pipeline_mode=pl.Buffered(1)),
                  pl.BlockSpec((rows, D), lambda i: (i, GOFF // D)), row],
        out_specs=[row, row, pl.BlockSpec((N_HEADS, per, 1, TQ), lambda i: (0, i, 0, 0))],
        out_shape=[_sds((S, D), BF16), _sds((S, D), BF16), _sds((N_HEADS, nq, 1, TQ), F32)],
        compiler_params=_params())(dh, w_out, proj, o)


def _fox_bwd(qn, kn, vb, dob, lse, delta, crow, cbc, rider=()):
    nq, nkb = S // TQ, S // TK

    def body(q_ref, k_ref, v_ref, do_ref, lse_ref, del_ref, cq_ref, cbc_ref,
             dk_ref, dv_ref, dcs_ref, dq_ref, dr_ref, st_s, dp_s, pt_s, ds_s, dq_acc, dr_acc):
        j = pl.program_id(1)

        @pl.when(j == 0)
        def _():
            dq_acc[...] = jnp.zeros_like(dq_acc)
            dr_acc[...] = jnp.zeros_like(dr_acc)

        kjs = [k_ref[:, HD * hh:HD * (hh + 1)] for hh in range(HPS)]
        vjs = [v_ref[:, HD * hh:HD * (hh + 1)] for hh in range(HPS)]

        def rows_of(ref, u, hh):
            off = pl.multiple_of(u * TQ, TQ)
            return ref[pl.ds(off, TQ), HD * hh:HD * (hh + 1)]

        def products(u, hh):
            st = (_dot_nt(kjs[hh], rows_of(q_ref, u, hh)) + cq_ref[hh, u]) - _widen(
                cbc_ref[:, LANES * hh:LANES * (hh + 1)])
            return st, _dot_nt(vjs[hh], rows_of(do_ref, u, hh))

        def step(u, slot, carries, masked=False):
            nxt = jnp.minimum(u + 1, nq - 1)
            for hh in range(HPS):
                st_s[1 - slot, hh], dp_s[1 - slot, hh] = products(nxt, hh)
            prev = jnp.maximum(u - 1, 0)
            dvs = [_dot_nn(pt_s[1 - slot, hh], rows_of(do_ref, prev, hh)) for hh in range(HPS)]
            dks = [_dot_nn(ds_s[1 - slot, hh], rows_of(q_ref, prev, hh)) for hh in range(HPS)]
            for hh in range(HPS):
                dq_acc[hh, prev] += _dot_tn(kjs[hh], ds_s[1 - slot, hh])
            out = []
            for hh in range(HPS):
                dk, dv, dcs = carries[hh]
                st = st_s[slot, hh]
                if masked:
                    st = jnp.where(_key_le_query((j - u) * TQ), st, -jnp.inf)
                pt = jnp.exp(st - lse_ref[hh, u])
                dst = pt * (dp_s[slot, hh] - del_ref[hh, u])
                pt_s[slot, hh] = pt.astype(BF16)
                ds_s[slot, hh] = dst.astype(BF16)
                dr_acc[hh, u] += jnp.sum(dst, axis=0, keepdims=True)
                out.append((dk + dks[hh], dv + dvs[hh], dcs + (dst[:, :LANES] + dst[:, LANES:])))
            return tuple(out)

        t0 = j // 2
        for hh in range(HPS):
            st_s[0, hh], dp_s[0, hh] = products(2 * t0, hh)
            pt_s[1, hh] = jnp.zeros((TK, TQ), BF16)
            ds_s[1, hh] = jnp.zeros((TK, TQ), BF16)
        one = (jnp.zeros((TK, HD), F32), jnp.zeros((TK, HD), F32), jnp.zeros((TK, LANES), F32))
        carries = step(2 * t0 + 1, 1, step(2 * t0, 0, (one,) * HPS, masked=True), masked=True)
        carries = lax.fori_loop(t0 + 1, nq // 2, lambda t, cr: step(2 * t + 1, 1, step(2 * t, 0, cr)), carries)
        dks, dvs = [], []
        lane = _lane_iota((TK, LANES))
        dcs_all = jnp.zeros((TK, LANES), F32)
        for hh in range(HPS):
            dk, dv, dcs = carries[hh]
            dks.append(dk + _dot_nn(ds_s[1, hh], rows_of(q_ref, nq - 1, hh)))
            dvs.append(dv + _dot_nn(pt_s[1, hh], rows_of(do_ref, nq - 1, hh)))
            dq_acc[hh, nq - 1] += _dot_tn(kjs[hh], ds_s[1, hh])
            dcs_all = jnp.where(lane == HPS * pl.program_id(0) + hh, -jnp.sum(dcs, axis=-1, keepdims=True), dcs_all)
        dcs_ref[0] = dcs_all
        dk_ref[...] = jnp.concatenate(dks, axis=-1)
        dv_ref[...] = jnp.concatenate(dvs, axis=-1).astype(BF16)

        @pl.when(j == nkb - 1)
        def _():
            for i in range(nq):
                dq_ref[TQ * i:TQ * (i + 1), :] = jnp.concatenate([dq_acc[hh, i] for hh in range(HPS)], axis=0).T
            dr_ref[...] = dr_acc[...]

    kblk = pl.BlockSpec((TK, HW), lambda hp, j: (j, hp))
    full = pl.BlockSpec((S, HW), lambda hp, j: (0, hp))
    rows = pl.BlockSpec((HPS, nq, 1, TQ), lambda hp, j: (hp, 0, 0, 0))
    cblk = pl.BlockSpec((TK, HPS * LANES), lambda hp, j: (j, hp))
    return _call(
        body, name="fox_bwd", args=(qn, kn, vb, dob, lse, delta, crow, cbc), grid=(N_HEADS // HPS, nkb),
        in_specs=[full, kblk, kblk, full, rows, rows, rows, cblk],
        out_specs=[kblk, kblk, pl.BlockSpec((1, TK, LANES), lambda hp, j: (hp, j, 0)), full, rows],
        out_shape=[_sds((S, D), F32), _sds((S, D), BF16), _sds((N_HEADS // HPS, S, LANES), F32), _sds((S, D), F32),
                   _sds((N_HEADS, nq, 1, TQ), F32)],
        scratch_shapes=[pltpu.VMEM((2, HPS, TK, TQ), F32), pltpu.VMEM((2, HPS, TK, TQ), F32),
                        pltpu.VMEM((2, HPS, TK, TQ), BF16), pltpu.VMEM((2, HPS, TK, TQ), BF16),
                        pltpu.VMEM((HPS, nq, HD, TQ), F32), pltpu.VMEM((HPS, nq, 1, TQ), F32)], rider=rider)


def _prep_a_bwd(dq, dk, dv, dgate, drow, dcs, proj, b_pad, gq, gk):
    nt = S // TM

    def body(dq_ref, dk_ref, dv_ref, dgt_ref, dr_ref, dcs_ref, xq_ref, xk_ref, f_ref, b_ref, gq_ref, gk_ref,
             o_ref, dgq_ref, dgk_ref, db_ref, carry):
        @pl.when(pl.program_id(0) == 0)
        def _():
            carry[...] = jnp.zeros_like(carry)
            dgq_ref[...] = jnp.zeros_like(dgq_ref)
            dgk_ref[...] = jnp.zeros_like(dgk_ref)
            db_ref[...] = jnp.zeros_like(db_ref)

        lane = _lane_iota((TM, LANES))
        lo_half = _half_ones()
        gq2, gk2 = _g2(gq_ref), _g2(gk_ref)
        dgq, dgk = jnp.zeros((1, LANES), F32), jnp.zeros((1, LANES), F32)
        for c in _pairs(D):
            dxq, dg = _head_norm_bwd(dq_ref[:, c] * QSCALE, xq_ref[:, c], gq2, lo_half)
            o_ref[:, c] = dxq.astype(BF16)
            dgq = dgq + dg
            dxk, dg = _head_norm_bwd(dk_ref[:, c], xk_ref[:, c], gk2, lo_half)
            o_ref[:, D + c.start:D + c.stop] = dxk.astype(BF16)
            dgk = dgk + dg
        dgq_ref[...] += dgq
        dgk_ref[...] += dgk
        o_ref[:, 2 * D:3 * D] = dv_ref[...]
        o_ref[:, GOFF:GOFF + D] = dgt_ref[...]

        dc = dr_ref[...]
        for group in range(N_HEADS // HPS):
            dc = dc + dcs_ref[group]
        r = lax.broadcasted_iota(jnp.int32, (TM, TM), 0)
        c = lax.broadcasted_iota(jnp.int32, (TM, TM), 1)
        tri = (c >= r).astype(F32)
        dlogf = jnp.dot(tri, dc, precision=lax.Precision.HIGHEST, preferred_element_type=F32) + carry[0:1, :]
        carry[0:1, :] = dlogf[0:1, :]
        df = dlogf * (1.0 / (1.0 + jnp.exp(f_ref[...] + b_ref[...])))
        db_ref[...] += jnp.sum(df, axis=0, keepdims=True)
        o_ref[:, FOFF:FOFF + LANES] = df.astype(BF16)
        o_ref[:, FOFF + LANES:NA] = jnp.zeros((TM, NA - FOFF - LANES), BF16)

    rev = lambda width, col: pl.BlockSpec((TM, width), lambda i: (nt - 1 - i, col))
    gspec = pl.BlockSpec((1, HD), lambda i: (0, 0))
    acc = pl.BlockSpec((1, LANES), lambda i: (0, 0))
    return pl.pallas_call(
        body, name="prep_a_bwd", grid=(nt,),
        in_specs=[rev(D, 0), rev(D, 0), rev(D, 0), rev(D, 0), rev(LANES, 0),
                  pl.BlockSpec((N_HEADS // HPS, TM, LANES), lambda i: (0, nt - 1 - i, 0)),
                  rev(D, 0), rev(D, 1), rev(LANES, FOFF // LANES), acc, gspec, gspec],
        out_specs=[rev(NA, 0), acc, acc, acc],
        out_shape=[_sds((S, NA), BF16)] + [_sds((1, LANES), F32)] * 3,
        scratch_shapes=[pltpu.VMEM((8, LANES), F32)],
        compiler_params=_params())(dq, dk, dv, dgate, drow, dcs, proj, proj, proj, b_pad, gq, gk)


def _head_b(x, z_a, w_out_a, g_kv, g_b, w_kv, w_in_b, gq, gk, cos2, sin2):
    nkv = w_kv.shape[1] // 2

    def body(x_ref, z_ref, wo_ref, gkv_ref, gb_ref, wkv_ref, wb_ref, gq_ref, gk_ref, c_ref, s_ref,
             h_ref, ukv_ref, ub_ref, kv_ref, pb_ref, qo_ref, ko_ref, vo_ref):
        xv = x_ref[...] + _dot_nn(z_ref[...], wo_ref[...])
        h_ref[...] = xv
        xn = xv * _rms_rinv(xv)
        ukv = (xn * gkv_ref[...]).astype(BF16)
        ub = (xn * gb_ref[...]).astype(BF16)
        ukv_ref[...] = ukv
        ub_ref[...] = ub
        kv_ref[...] = _dot_nn(ukv, wkv_ref[...])
        for lo in range(0, 2 * D, D):
            pb_ref[:, lo:lo + D] = _dot_nn(ub, wb_ref[:, lo:lo + D])
        lane = _lane_iota((RT, LANES))
        lo_half = lane < HD
        cos, sin = c_ref[...], s_ref[...]
        gq2, gk2 = _g2(gq_ref), _g2(gk_ref)
        for c in _pairs(D):
            q = pb_ref[:, c]
            qo_ref[:, c] = (_rope_fwd((q * _head_rinv(q, lo_half)) * gq2, cos, sin, lane) * QSCALE).astype(BF16)
        for c in _pairs(nkv):
            k = kv_ref[:, c]
            ko_ref[:, c] = _rope_fwd((k * _head_rinv(k, lo_half)) * gk2, cos, sin, lane).astype(BF16)
        vo_ref[...] = kv_ref[:, nkv:2 * nkv].astype(BF16)

    row = lambda width: pl.BlockSpec((RT, width), lambda i: (i, 0))
    whole = lambda arr: pl.BlockSpec(arr.shape, lambda i: (0,) * arr.ndim, pipeline_mode=pl.Buffered(1))
    return pl.pallas_call(
        body, name="head_b", grid=(S // RT,),
        in_specs=[row(D), row(D), whole(w_out_a), whole(g_kv), whole(g_b), whole(w_kv), whole(w_in_b), whole(gq),
                  whole(gk), row(LANES), row(LANES)],
        out_specs=[row(D), row(D), row(D), row(2 * nkv), row(2 * D), row(D), row(nkv), row(nkv)],
        out_shape=[_sds((S, D), F32), _sds((S, D), BF16), _sds((S, D), BF16), _sds((S, 2 * nkv), F32),
                   _sds((S, 2 * D), F32), _sds((S, D), BF16), _sds((S, nkv), BF16), _sds((S, nkv), BF16)],
        compiler_params=_params())(x, z_a, w_out_a, g_kv, g_b, w_kv, w_in_b, gq, gk, cos2, sin2)


N_KV, GRP = 4, 4


def _swa_mask(n):
    r = lax.broadcasted_iota(jnp.int32, (2 * WIN, GRP * WIN), 0)
    q = lax.broadcasted_iota(jnp.int32, (2 * WIN, GRP * WIN), 1) & (WIN - 1)
    return (r > q) & (r <= q + WIN) & ((r >= WIN) | (n > 0))


def _stack4(ref_or_val, base):
    return jnp.concatenate([ref_or_val[:, base + HD * g: base + HD * (g + 1)] for g in range(GRP)], axis=0)


def _unstack4(xt):
    return jnp.concatenate([xt[:, WIN * g:WIN * (g + 1)] for g in range(GRP)], axis=0).T


def _band(prev_ref, cur_ref, kh):
    return jnp.concatenate([prev_ref[:, HD * kh:HD * (kh + 1)], cur_ref[:, HD * kh:HD * (kh + 1)]], axis=0)


def _sink_row(s_ref, first):
    lane = _lane_iota((1, GRP * WIN))
    row = jnp.full((1, GRP * WIN), s_ref[first + GRP - 1], F32)
    for g in range(GRP - 2, -1, -1):
        row = jnp.where(lane < WIN * (g + 1), s_ref[first + g], row)
    return row


def _swa_fwd(qb, ksh, vsh, pb, sinks):
    nb = S // WIN

    def body(q_ref, kp_ref, kc_ref, vp_ref, vc_ref, g_ref, s_ref, o_ref, z_ref, lse_ref):
        n = pl.program_id(0)
        valid = _swa_mask(n)
        outs = []
        for kh in range(N_KV):
            kb, vb = _band(kp_ref, kc_ref, kh), _band(vp_ref, vc_ref, kh)
            st = jnp.where(valid, _dot_nt(kb, _stack4(q_ref, GRP * HD * kh)), -jnp.inf)
            sink = _sink_row(s_ref, GRP * kh)
            m = jnp.maximum(jnp.max(st, axis=0, keepdims=True), sink)
            pt = jnp.exp(st - m)
            l = jnp.sum(pt, axis=0, keepdims=True) + jnp.exp(sink - m)
            outs.append(_unstack4(_dot_tn(vb, pt.astype(BF16)) / l))
            lse = m + jnp.log(l)
            for g in range(GRP):
                lse_ref[GRP * kh + g, 0] = lse[:, WIN * g:WIN * (g + 1)]
        o = jnp.concatenate(outs, axis=-1)
        o_ref[...] = o
        g = g_ref[...]
        z_ref[...] = (o * (g * _sigmoid(g))).astype(BF16)

    row = pl.BlockSpec((WIN, D), lambda n: (n, 0))
    prev = pl.BlockSpec((WIN, N_KV * HD), lambda n: (jnp.maximum(n - 1, 0), 0))
    cur = pl.BlockSpec((WIN, N_KV * HD), lambda n: (n, 0))
    return pl.pallas_call(
        body, name="swa_fwd", grid=(nb,),
        in_specs=[row, prev, cur, prev, cur, pl.BlockSpec((WIN, D), lambda n: (n, 1)),
                  pl.BlockSpec(memory_space=pltpu.SMEM)],
        out_specs=[row, row, pl.BlockSpec((N_HEADS, 1, 1, WIN), lambda n: (0, n, 0, 0))],
        out_shape=[_sds((S, D), F32), _sds((S, D), BF16), _sds((N_HEADS, nb, 1, WIN), F32)],
        compiler_params=_params())(qb, ksh, ksh, vsh, vsh, pb, sinks)


def _swa_bwd(qb, ksh, vsh, dz, o, lse, pb, sinks, gq, cos2, sin2):
    nb = S // WIN

    def body(q_ref, kp_ref, kc_ref, vp_ref, vc_ref, dz_ref, o_ref, lse_ref, x_ref, g_ref, s_ref, gq_ref, c_ref, sn_ref,
             dpb_ref, dka_ref, dkb_ref, dva_ref, dvb_ref, dsink_ref, dgq_ref):
        n = pl.program_id(0)

        @pl.when(n == 0)
        def _():
            dsink_ref[...] = jnp.zeros_like(dsink_ref)
            dgq_ref[...] = jnp.zeros_like(dgq_ref)

        valid = _swa_mask(n)
        g = g_ref[...]
        sg = _sigmoid(g)
        dzv = dz_ref[...]
        ov = o_ref[...]
        do = dzv * (g * sg)
        dpb_ref[:, D:2 * D] = (dzv * ov * (sg * (1.0 + g * (1.0 - sg)))).astype(BF16)
        prod_t = (do * ov).T
        lane1 = _lane_iota((1, LANES))
        dqs, dkas, dkbs, dvas, dvbs = [], [], [], [], []
        dsink = jnp.zeros((1, LANES), F32)
        for kh in range(N_KV):
            kb, vb = _band(kp_ref, kc_ref, kh), _band(vp_ref, vc_ref, kh)
            base = GRP * HD * kh
            qs = _stack4(q_ref, base)
            dos = _stack4(do, base).astype(BF16)
            delta = jnp.concatenate(
                [jnp.sum(prod_t[base + HD * gg:base + HD * (gg + 1), :], axis=0, keepdims=True)
                 for gg in range(GRP)], axis=1)
            lse = jnp.concatenate([lse_ref[GRP * kh + gg, 0] for gg in range(GRP)], axis=1)
            st = jnp.where(valid, _dot_nt(kb, qs), -jnp.inf)
            pt = jnp.exp(st - lse)
            dst = pt * (_dot_nt(vb, dos) - delta)
            dsb = dst.astype(BF16)
            dqs.append(_unstack4(_dot_tn(kb, dsb)))
            dkband = _dot_nn(dsb, qs)
            dvband = _dot_nn(pt.astype(BF16), dos)
            dkbs.append(dkband[0:WIN, :])
            dkas.append(dkband[WIN:2 * WIN, :])
            dvbs.append(dvband[0:WIN, :])
            dvas.append(dvband[WIN:2 * WIN, :])
            ps_delta = jnp.exp(_sink_row(s_ref, GRP * kh) - lse) * delta
            for gg in range(GRP):
                val = jnp.sum(ps_delta[:, WIN * gg:WIN * (gg + 1)], axis=1, keepdims=True)
                dsink = dsink - jnp.where(lane1 == GRP * kh + gg, val, 0.0)
        dka_ref[...] = jnp.concatenate(dkas, axis=-1)
        dkb_ref[...] = jnp.concatenate(dkbs, axis=-1)
        dva_ref[...] = jnp.concatenate(dvas, axis=-1)
        dvb_ref[...] = jnp.concatenate(dvbs, axis=-1)
        dsink_ref[...] += dsink

        lane = _lane_iota((WIN, LANES))
        g2, cos, sin = _g2(gq_ref), c_ref[...], sn_ref[...]
        lo_half = _half_ones()
        dg_tot = jnp.zeros((1, LANES), F32)
        for kh in range(N_KV):
            for c in _pairs(GRP * HD):
                cols = slice(GRP * HD * kh + c.start, GRP * HD * kh + c.stop)
                dn = _rope_bwd(dqs[kh][:, c] * QSCALE, cos, sin, lane)
                dx, dg = _head_norm_bwd(dn, x_ref[:, cols], g2, lo_half)
                dpb_ref[:, cols] = dx.astype(BF16)
                dg_tot = dg_tot + dg
        dgq_ref[...] += dg_tot

    row = pl.BlockSpec((WIN, D), lambda n: (n, 0))
    prev = pl.BlockSpec((WIN, N_KV * HD), lambda n: (jnp.maximum(n - 1, 0), 0))
    cur = pl.BlockSpec((WIN, N_KV * HD), lambda n: (n, 0))
    acc = pl.BlockSpec((1, LANES), lambda n: (0, 0))
    tab = pl.BlockSpec((WIN, LANES), lambda n: (n, 0))
    return pl.pallas_call(
        body, name="swa_bwd", grid=(nb,),
        in_specs=[row, prev, cur, prev, cur, row, row, pl.BlockSpec((N_HEADS, 1, 1, WIN), lambda n: (0, n, 0, 0)),
                  row, pl.BlockSpec((WIN, D), lambda n: (n, 1)), pl.BlockSpec(memory_space=pltpu.SMEM),
                  pl.BlockSpec((1, HD), lambda n: (0, 0)), tab, tab],
        out_specs=[pl.BlockSpec((WIN, 2 * D), lambda n: (n, 0)), cur, cur, cur, cur, acc, acc],
        out_shape=[_sds((S, 2 * D), BF16)] + [_sds((S, 256), F32)] * 4 + [_sds((1, LANES), F32)] * 2,
        compiler_params=_params())(qb, ksh, ksh, vsh, vsh, dz, o, lse, pb, pb, sinks, gq, cos2, sin2)


def _prep_kv_bwd(dka, dkb, dva, dvb, kv, gk, cos2, sin2):
    nt = S // RB
    per = RB // WIN

    def shifted(cur_ref, nxt_ref, has_next):
        return jnp.concatenate([cur_ref[WIN:RB, :], jnp.where(has_next, nxt_ref[...], 0.0)], axis=0)

    def body(dka_ref, dkb_ref, dkn_ref, dva_ref, dvb_ref, dvn_ref, x_ref, g_ref, c_ref, s_ref, o_ref, dgk_ref):
        i, j = pl.program_id(0), pl.program_id(1)

        @pl.when((i == 0) & (j == 0))
        def _():
            dgk_ref[...] = jnp.zeros_like(dgk_ref)

        has_next = i < nt - 1

        @pl.when(j == 0)
        def _():
            lane = _lane_iota((RB, LANES))
            g2, cos, sin = _g2(g_ref), c_ref[...], s_ref[...]
            dy_all = dka_ref[...] + shifted(dkb_ref, dkn_ref, has_next)
            dg_tot = jnp.zeros((1, LANES), F32)
            lo_half = _half_ones()
            for c in _pairs(CB):
                dn = _rope_bwd(dy_all[:, c], cos, sin, lane)
                dx, dg = _head_norm_bwd(dn, x_ref[:, c], g2, lo_half)
                o_ref[:, c] = dx.astype(BF16)
                dg_tot = dg_tot + dg
            dgk_ref[...] += dg_tot

        @pl.when(j == 1)
        def _():
            o_ref[...] = (dva_ref[...] + shifted(dvb_ref, dvn_ref, has_next)).astype(BF16)

    cur = pl.BlockSpec((RB, CB), lambda i, j: (i, 0))
    nxt = pl.BlockSpec((WIN, CB), lambda i, j: (jnp.minimum(per * (i + 1), S // WIN - 1), 0))
    tab = pl.BlockSpec((RB, LANES), lambda i, j: (i, 0))
    return pl.pallas_call(
        body, name="prep_kv_bwd", grid=(nt, 2),
        in_specs=[cur, cur, nxt, cur, cur, nxt, cur, pl.BlockSpec((1, HD), lambda i, j: (0, 0)), tab, tab],
        out_specs=[pl.BlockSpec((RB, CB), lambda i, j: (i, j)), pl.BlockSpec((1, LANES), lambda i, j: (0, 0))],
        out_shape=[_sds((S, 2 * CB), BF16), _sds((1, LANES), F32)],
        compiler_params=_params())(dka, dkb, dkb, dva, dvb, dvb, kv, gk, cos2, sin2)


def _out_b_loss(z, w_out, h1, tgt):
    tm = 2 * RT

    def body(z_ref, w_ref, h_ref, t_ref, dy_ref, l_ref):
        @pl.when(pl.program_id(0) == 0)
        def _():
            l_ref[...] = jnp.zeros_like(l_ref)

        e = (h_ref[...] + _dot_nn(z_ref[...], w_ref[...])) - t_ref[...]
        dy_ref[...] = e * (1.0 / D)
        l_ref[...] += jnp.sum(jnp.sum(e * e, axis=-1, keepdims=True), axis=0, keepdims=True)

    row = pl.BlockSpec((tm, D), lambda i: (i, 0))
    return pl.pallas_call(
        body, name="out_b_loss", grid=(S // tm,),
        in_specs=[row, pl.BlockSpec(w_out.shape, lambda i: (0, 0), pipeline_mode=pl.Buffered(1)), row, row],
        out_specs=[row, pl.BlockSpec((1, LANES), lambda i: (0, 0))],
        out_shape=[_sds((S, D), F32), _sds((1, LANES), F32)], compiler_params=_params())(z, w_out, h1, tgt)


def _du_a_rms_bwd(dproj, wa, x, g, dres, after):
    tm, tk = 1024, NA // 2
    nk = NA // tk

    def body(a_ref, b_ref, x_ref, g_ref, dr_ref, after_ref, dx_ref, dg_ref, acc):
        i, kk = pl.program_id(0), pl.program_id(1)

        @pl.when((i == 0) & (kk == 0))
        def _():
            dg_ref[...] = jnp.zeros_like(dg_ref)

        p = _dot_nt(a_ref[...], b_ref[...])

        @pl.when(kk == 0)
        def _():
            acc[...] = p

        @pl.when(kk == nk - 1)
        def _():
            dx, dg = _rms_bwd_core(acc[...] + p, x_ref[...], g_ref[...])
            dx_ref[...] = dr_ref[...] + dx
            dg_ref[...] += dg

    assert nk == 2
    row = pl.BlockSpec((tm, D), lambda i, kk: (i, 0))
    vec = pl.BlockSpec((1, D), lambda i, kk: (0, 0))
    return pl.pallas_call(
        body, name="du_a_rms_bwd", grid=(S // tm, nk),
        in_specs=[pl.BlockSpec((tm, tk), lambda i, kk: (i, kk)), pl.BlockSpec((D, tk), lambda i, kk: (0, kk)),
                  row, vec, row, pl.BlockSpec(after.shape, lambda i, kk: (0, 0))],
        out_specs=[row, vec], out_shape=[_sds((S, D), F32), _sds((1, D), F32)],
        scratch_shapes=[pltpu.VMEM((tm, D), F32)], compiler_params=_params())(dproj, wa, x, g, dres, after)


def _du_b_rms_bwd(dpb, w_in_b, dkv, w_kv, h1, g_b, g_kv, dy):
    tm = 2 * RT

    def body(ab_ref, wb_ref, akv_ref, wkv_ref, x_ref, gb_ref, gkv_ref, dy_ref, dh_ref, dgb_ref, dgkv_ref):
        @pl.when(pl.program_id(0) == 0)
        def _():
            dgb_ref[...] = jnp.zeros_like(dgb_ref)
            dgkv_ref[...] = jnp.zeros_like(dgkv_ref)

        x = x_ref[...]
        dx1, dg1 = _rms_bwd_core(_dot_nt(ab_ref[...], wb_ref[...]), x, gb_ref[...])
        dx2, dg2 = _rms_bwd_core(_dot_nt(akv_ref[...], wkv_ref[...]), x, gkv_ref[...])
        dh_ref[...] = dy_ref[...] + dx1 + dx2
        dgb_ref[...] += dg1
        dgkv_ref[...] += dg2

    row = lambda width: pl.BlockSpec((tm, width), lambda i: (i, 0))
    whole = lambda arr: pl.BlockSpec(arr.shape, lambda i: (0, 0), pipeline_mode=pl.Buffered(1))
    vec = pl.BlockSpec((1, D), lambda i: (0, 0))
    return pl.pallas_call(
        body, name="du_b_rms_bwd", grid=(S // tm,),
        in_specs=[row(dpb.shape[1]), whole(w_in_b), row(dkv.shape[1]), whole(w_kv), row(D), vec, vec, row(D)],
        out_specs=[row(D), vec, vec], out_shape=[_sds((S, D), F32), _sds((1, D), F32), _sds((1, D), F32)],
        compiler_params=_params())(dpb, w_in_b, dkv, w_kv, h1, g_b, g_kv, dy)


def _gather_first(w_in_a, w_out_a, w_kv, w_in_b, w_out_b, norm_a_g):
    rows, cols = w_in_a.shape[-2:]
    groups = rows // LANES
    cols_pad = -(-cols // LANES) * LANES

    def body(wia_ref, woa_ref, wkv_ref, wib_ref, wob_ref, ga_ref,
             wa_g, ga_g, woa_s, wkv_s, wib_s, wob_s, wa_s, st_a, st_oa, st_kv, st_ib, st_ob, plane_t, load_sems, *sems):
        sources = [wia_ref, woa_ref.at[0], wkv_ref, wib_ref.at[0], wob_ref.at[0]]
        stages = [st_a, st_oa, st_kv, st_ib, st_ob]
        loads = [pltpu.make_async_copy(src, dst, load_sems.at[i]) for i, (src, dst) in enumerate(zip(sources, stages))]
        for cp in loads:
            cp.start()
        loads[0].wait()
        plane_t[cols_pad - LANES:, :] = jnp.zeros((LANES, LANES), F32)
        for j in range(groups):
            for c0 in range(0, cols, 64):
                n = min(64, cols - c0)
                plane_t[c0:c0 + n, :] = st_a[pl.ds(c0 * groups + j, n, stride=groups), :]
            for c0 in range(0, cols, LANES):
                n = min(LANES, cols - c0)
                wa_s[j * LANES:(j + 1) * LANES, c0:c0 + n] = plane_t[c0:c0 + LANES, :].T[:, :n].astype(BF16)

        def cast_the_rest():
            for cp, stage, out in zip(loads[1:], stages[1:], [woa_s, wkv_s, wib_s, wob_s]):
                cp.wait()
                out[...] = stage[...].astype(BF16)

        _gather_two_level([wa_s, ga_ref], [wa_g, ga_g], sems, meanwhile=cast_the_rest)

    vmem = pl.BlockSpec(memory_space=pltpu.VMEM)
    anyspec = pl.BlockSpec(memory_space=pl.ANY)
    shard = lambda w: _sds(w.shape[-2:], BF16)
    stage = lambda w: pltpu.VMEM(w.shape[-2:], F32)
    return pl.pallas_call(
        body, name="gather_first", in_specs=[anyspec] * 5 + [vmem],
        out_specs=[anyspec, anyspec, vmem, vmem, vmem, vmem],
        out_shape=[_sds((N_DEV,) + w_in_a.shape[-2:], BF16), _sds((N_DEV,) + norm_a_g.shape, F32),
                   shard(w_out_a), shard(w_kv), shard(w_in_b), shard(w_out_b)],
        scratch_shapes=[pltpu.VMEM(w_in_a.shape[-2:], BF16), pltpu.VMEM((cols * groups, LANES), F32), stage(w_out_a),
                        stage(w_kv), stage(w_in_b), stage(w_out_b), pltpu.VMEM((cols_pad, LANES), F32),
                        pltpu.SemaphoreType.DMA((5,))] + _exchange_sems(2),
        compiler_params=pltpu.CompilerParams(vmem_limit_bytes=VMEM_LIMIT, has_side_effects=True))(
            _entry_view(w_in_a).reshape(cols * groups, LANES), w_out_a, w_kv, w_in_b, w_out_b, norm_a_g)


def _pair_reduce(slots):
    n_chip = N_DEV // 2
    _, rows, cols = slots.shape

    def body(s_ref, o_ref, own_v, sib_v, send_sems, recv_sems, local_sems):
        x, y, c = lax.axis_index("x"), lax.axis_index("y"), lax.axis_index("c")
        copies = []
        for j in range(n_chip):
            own = pltpu.make_async_copy(s_ref.at[2 * j + c], own_v.at[j], local_sems.at[j])
            give = pltpu.make_async_remote_copy(
                src_ref=s_ref.at[2 * j + 1 - c], dst_ref=sib_v.at[j], send_sem=send_sems.at[j],
                recv_sem=recv_sems.at[j], device_id=(x, y, 1 - c), device_id_type=pl.DeviceIdType.MESH)
            own.start()
            give.start()
            copies.append((own, give))
        for j, (own, give) in enumerate(copies):
            own.wait()
            give.wait()
            o_ref[j] = (own_v[j].astype(F32) + sib_v[j].astype(F32)).astype(BF16)

    half = _sds((n_chip, rows, cols), slots.dtype)
    return pl.pallas_call(
        body, name="pair_reduce", in_specs=[pl.BlockSpec(memory_space=pl.ANY)],
        out_specs=pl.BlockSpec(memory_space=pltpu.VMEM), out_shape=half,
        scratch_shapes=[pltpu.VMEM(half.shape, half.dtype), pltpu.VMEM(half.shape, half.dtype),
                        pltpu.SemaphoreType.DMA((n_chip,)), pltpu.SemaphoreType.DMA((n_chip,)),
                        pltpu.SemaphoreType.DMA((n_chip,))],
        compiler_params=pltpu.CompilerParams(vmem_limit_bytes=VMEM_LIMIT, has_side_effects=True))(slots)


def _padded_col(c):
    if c < RAW_F:
        return c
    return FOFF + (c - RAW_F) if c < RAW_G else GOFF + (c - RAW_G)


def _shard_pieces():
    width = NA_RAW // N_DEV
    pieces = []
    for d in range(N_DEV):
        cuts = [width * d] + [c for c in (RAW_F, RAW_G) if width * d < c < width * (d + 1)] + [width * (d + 1)]
        for lo, hi in zip(cuts[:-1], cuts[1:]):
            pieces.append((d, lo - width * d, _padded_col(lo), hi - lo))
    return pieces


def _unshard_wa(wa_g):
    def body(w_ref, o_ref):
        o_ref[:, FOFF + N_HEADS:NA] = jnp.zeros((TM, NA - FOFF - N_HEADS), BF16)
        for d, src, dst, width in _shard_pieces():
            o_ref[:, dst:dst + width] = w_ref[d, :, src:src + width]

    return pl.pallas_call(
        body, name="unshard_wa", grid=(D // TM,),
        in_specs=[pl.BlockSpec((N_DEV, TM, NA_RAW // N_DEV), lambda i: (0, i, 0))],
        out_specs=pl.BlockSpec((TM, NA), lambda i: (i, 0)), out_shape=_sds((D, NA), BF16),
        compiler_params=_params())(wa_g)


def _reshard_dwa(dwa):
    def body(g_ref, o_ref):
        for d, src, dst, width in _shard_pieces():
            o_ref[d, :, src:src + width] = g_ref[:, dst:dst + width]

    return pl.pallas_call(
        body, name="reshard_dwa", grid=(D // TM,), in_specs=[pl.BlockSpec((TM, NA), lambda i: (i, 0))],
        out_specs=pl.BlockSpec((N_DEV, TM, NA_RAW // N_DEV), lambda i: (0, i, 0)),
        out_shape=_sds((N_DEV, D, NA_RAW // N_DEV), dwa.dtype), compiler_params=_params())(dwa)


def _reshard_pair_reduce(dwa):
    n_chip, nt = N_DEV // 2, D // TM
    width = NA_RAW // N_DEV

    def body(g_ref, o_ref, slots_v, sib_v, send_sems, recv_sems):
        i = pl.program_id(0)
        x, y, c = lax.axis_index("x"), lax.axis_index("y"), lax.axis_index("c")

        def tile_rows(tile):
            return pl.ds(tile * TM if isinstance(tile, int) else pl.multiple_of(tile * TM, TM), TM)

        def give(j, tile):
            return pltpu.make_async_remote_copy(
                src_ref=slots_v.at[2 * j + 1 - c, tile_rows(tile)], dst_ref=sib_v.at[j, tile_rows(tile)],
                send_sem=send_sems.at[j, tile], recv_sem=recv_sems.at[j, tile], device_id=(x, y, 1 - c),
                device_id_type=pl.DeviceIdType.MESH)

        for d, src, dst, w in _shard_pieces():
            slots_v[d, tile_rows(i), src:src + w] = g_ref[:, dst:dst + w]
        for j in range(n_chip):
            give(j, i).start()

        @pl.when(i == nt - 1)
        def _():
            for tile in range(nt):
                for j in range(n_chip):
                    give(j, tile).wait()
            for j in range(n_chip):
                for r0 in range(0, D, 64):
                    mine = slots_v[2 * j + c, r0:r0 + 64, :].astype(F32)
                    o_ref[j, r0:r0 + 64, :] = (mine + sib_v[j, r0:r0 + 64, :].astype(F32)).astype(BF16)

    half = _sds((n_chip, D, width), dwa.dtype)
    return pl.pallas_call(
        body, name="reshard_pair_reduce", grid=(nt,), in_specs=[pl.BlockSpec((TM, NA), lambda i: (i, 0))],
        out_specs=pl.BlockSpec(half.shape, lambda i: (0, 0, 0)), out_shape=half,
        scratch_shapes=[pltpu.VMEM((N_DEV, D, width), dwa.dtype), pltpu.VMEM(half.shape, dwa.dtype),
                        pltpu.SemaphoreType.DMA((n_chip, nt)), pltpu.SemaphoreType.DMA((n_chip, nt))],
        compiler_params=pltpu.CompilerParams(vmem_limit_bytes=VMEM_LIMIT, has_side_effects=True))(dwa)


CHIP_FLIPS = (2, 4, 6)


def _chip_exchange_start(partial):
    n = len(CHIP_FLIPS)

    def body(p_ref, land_ref, *rest):
        sends, recvs, token = rest[:n], rest[n:2 * n], rest[2 * n + 2]
        x, y, c = lax.axis_index("x"), lax.axis_index("y"), lax.axis_index("c")
        me = 4 * x + 2 * y + c
        for idx, k in enumerate(CHIP_FLIPS):
            pltpu.make_async_remote_copy(
                src_ref=p_ref.at[(me ^ k) >> 1], dst_ref=land_ref.at[me >> 1], send_sem=sends[idx],
                recv_sem=recvs[idx], device_id=(x ^ ((k >> 2) & 1), y ^ ((k >> 1) & 1), c),
                device_id_type=pl.DeviceIdType.MESH).start()
        token[...] = jnp.zeros_like(token)

    hbm = pl.BlockSpec(memory_space=pltpu.HBM)
    sem = pl.BlockSpec(memory_space=pltpu.SEMAPHORE)
    buf = pltpu.HBM(partial.shape, partial.dtype)
    return pl.pallas_call(
        body, name="chip_exchange_start",
        out_shape=(pltpu.SemaphoreType.DMA(()),) * (2 * n) + (buf, buf, _sds((8, LANES), F32)),
        in_specs=(hbm, hbm), out_specs=(sem,) * (2 * n) + (hbm, hbm, pl.BlockSpec(memory_space=pltpu.VMEM)),
        input_output_aliases={0: 2 * n, 1: 2 * n + 1},
        compiler_params=pltpu.CompilerParams(has_side_effects=pltpu.SideEffectType.DATAFLOW_SIDE_EFFECTING))(
            pltpu.with_memory_space_constraint(partial, pltpu.HBM),
            pltpu.with_memory_space_constraint(lax.empty(partial.shape, partial.dtype), pltpu.HBM))


def _chip_exchange_wait(started, after):
    n = len(CHIP_FLIPS)
    sems, (p_thru, land_thru) = started[:2 * n], started[2 * n:2 * n + 2]

    def body(p_ref, land_ref, *rest):
        sends, recvs = rest[:n], rest[n:2 * n]
        x, y, c = lax.axis_index("x"), lax.axis_index("y"), lax.axis_index("c")
        me = 4 * x + 2 * y + c
        for idx, k in enumerate(CHIP_FLIPS):
            copy = pltpu.make_async_remote_copy(
                src_ref=p_ref.at[(me ^ k) >> 1], dst_ref=land_ref.at[(me ^ k) >> 1], send_sem=sends[idx],
                recv_sem=recvs[idx], device_id=(x ^ ((k >> 2) & 1), y ^ ((k >> 1) & 1), c),
                device_id_type=pl.DeviceIdType.MESH)
            copy.wait_send()
            copy.wait_recv()

    hbm = pl.BlockSpec(memory_space=pltpu.HBM)
    sem = pl.BlockSpec(memory_space=pltpu.SEMAPHORE)
    buf = pltpu.HBM(p_thru.shape, p_thru.dtype)
    return pl.pallas_call(
        body, name="chip_exchange_wait", out_shape=(buf, buf),
        in_specs=(hbm, hbm) + (sem,) * (2 * n) + (pl.BlockSpec(memory_space=pl.ANY),) * len(after),
        out_specs=(hbm, hbm), input_output_aliases={0: 0, 1: 1},
        compiler_params=pltpu.CompilerParams(has_side_effects=pltpu.SideEffectType.DATAFLOW_SIDE_EFFECTING))(
            p_thru, land_thru, *sems, *after)


def _gather_slab(slab, after):
    def body(s_ref, after_ref, o_ref, *sems):
        _exchange_ops(["gather_rows"], [s_ref], [o_ref], sems, True, True)

    anyspec = pl.BlockSpec(memory_space=pl.ANY)
    return pl.pallas_call(
        body, name="gather_slab", in_specs=[anyspec, anyspec], out_specs=anyspec,
        out_shape=_sds((N_DEV,) + slab.shape, slab.dtype), scratch_shapes=_exchange_sems(1),
        compiler_params=pltpu.CompilerParams(has_side_effects=True))(slab, after)


def _adamw(w, g, m, v):
    m = ADAM_B1 * m + (1.0 - ADAM_B1) * g
    v = ADAM_B2 * v + (1.0 - ADAM_B2) * (g * g)
    m_hat = m / (1.0 - ADAM_B1 ** ADAM_STEP)
    v_hat = v / (1.0 - ADAM_B2 ** ADAM_STEP)
    delta = -ADAM_LR * (m_hat / (jnp.sqrt(v_hat) + ADAM_EPS) + ADAM_WD * w)
    return delta, m, v


def _sum_adamw(recv, w, m, v, name, after=None):
    lead = w.ndim - 2
    rows, cols = w.shape[-2:]
    tr = 256 if rows % 256 == 0 else 128
    n_slots = recv.shape[0]
    extra = [] if after is None else [after]

    def body(*refs):
        r_ref = refs[0]
        w_ref, m_ref, v_ref, g_ref, d_ref, nm_ref, nv_ref = refs[1 + len(extra):]
        g = None
        for slot in range(n_slots):
            g = r_ref[slot].astype(F32) if g is None else g + r_ref[slot].astype(F32)
        g_ref[...] = g
        d_ref[...], nm_ref[...], nv_ref[...] = _adamw(w_ref[...], g, m_ref[...], v_ref[...])

    blk = pl.BlockSpec((None,) * lead + (tr, cols), lambda i: (0,) * lead + (i, 0))
    slots = pl.BlockSpec((n_slots, tr, cols), lambda i: (0, i, 0))
    return pl.pallas_call(
        body, name=name, grid=(rows // tr,),
        in_specs=[slots] + [pl.BlockSpec(a.shape, lambda i: (0, 0)) for a in extra] + [blk, blk, blk],
        out_specs=[blk] * 4, out_shape=[_sds(w.shape, F32)] * 4, compiler_params=_params())(recv, *extra, w, m, v)


def _entry_view(a):
    _, rows, cols = a.shape
    return jnp.transpose(a, (2, 0, 1)).reshape(cols, rows // LANES, LANES)


def _from_entry_view(a):
    cols, groups, lanes = a.shape
    return jnp.transpose(a, (1, 2, 0)).reshape(1, groups * lanes, cols)


def _sum_adamw_entry_view(landing, own, w, m, v, name):
    n_slots, rows, cols = landing.shape
    groups = rows // LANES
    cols_pad = -(-cols // LANES) * LANES

    def body(r_ref, own_ref, w_ref, m_ref, v_ref, g_ref, d_ref, nm_ref, nv_ref, pad_ref, gt_ref):
        j = pl.program_id(0)
        chip = (4 * lax.axis_index("x") + 2 * lax.axis_index("y") + lax.axis_index("c")) >> 1
        pad_ref[:, cols_pad - LANES:] = jnp.zeros((LANES, LANES), F32)
        for r0 in range(0, LANES, 32):
            g = None
            for slot in range(n_slots):
                part = jnp.where(chip == slot, own_ref[slot, r0:r0 + 32], r_ref[slot, r0:r0 + 32])
                g = part.astype(F32) if g is None else g + part.astype(F32)
            pad_ref[r0:r0 + 32, :cols] = g
        for c0 in range(0, cols_pad, LANES):
            gt_ref[c0:c0 + LANES, :] = pad_ref[:, c0:c0 + LANES].T
        def update_plane(plane):
            for c0 in range(0, cols, 64):
                n = min(64, cols - c0)
                at = pl.ds(c0 * groups + plane, n, stride=groups)
                gt = gt_ref[c0:c0 + n, :]
                g_ref[at, :] = gt
                d_ref[at, :], nm_ref[at, :], nv_ref[at, :] = _adamw(w_ref[at, :], gt, m_ref[at, :], v_ref[at, :])

        for plane in range(groups):
            pl.when(j == plane)(lambda plane=plane: update_plane(plane))

    flat = lambda a: _entry_view(a).reshape(cols * groups, LANES)
    whole = pl.BlockSpec((cols * groups, LANES), lambda j: (0, 0))
    slots = pl.BlockSpec((n_slots, LANES, cols), lambda j: (0, j, 0))
    outs = pl.pallas_call(
        body, name=name, grid=(groups,), in_specs=[slots, slots, whole, whole, whole], out_specs=[whole] * 4,
        out_shape=[_sds((cols * groups, LANES), F32)] * 4,
        scratch_shapes=[pltpu.VMEM((LANES, cols_pad), F32), pltpu.VMEM((cols_pad, LANES), F32)],
        compiler_params=_params())(landing, own, flat(w), flat(m), flat(v))
    return [_from_entry_view(o.reshape(cols, groups, LANES)) for o in outs]


SLAB_ROWS = 16
SLOT = {"kv_norm_g": (8, 0, D), "norm_b_g": (9, 0, D), "b_forget": (10, 0, 16), "qnorm_a_g": (10, 128, HD),
        "knorm_a_g": (10, 256, HD), "knorm_b_g": (10, 384, HD), "qnorm_b_g": (10, 512, HD), "sinks": (10, 640, 16)}
SMALL = ["norm_a_g", "b_forget", "qnorm_a_g", "knorm_a_g", "kv_norm_g", "knorm_b_g", "norm_b_g", "qnorm_b_g", "sinks"]


LOSS_ROW = 11


def _pack_small(dg_a, dg_kv, dg_b, db_f, dgq_a, dgk_a, dgk_b, dgq_b, dsinks, lsum):
    def fold(ref):
        return ref[:, 0:HD] + ref[:, HD:2 * HD]

    def body(dga_ref, dgkv_ref, dgb_ref, dbf_ref, dgqa_ref, dgka_ref, dgkb_ref, dgqb_ref, dsk_ref, ls_ref, slab_ref):
        slab_ref[...] = jnp.zeros_like(slab_ref)
        for r in range(N_DEV):
            slab_ref[r:r + 1, 0:LANES] = dga_ref[:, LANES * r:LANES * (r + 1)]
        slab_ref[8:9, :] = dgkv_ref[...]
        slab_ref[9:10, :] = dgb_ref[...]
        slab_ref[10:11, 0:LANES] = dbf_ref[...]
        slab_ref[10:11, 128:128 + HD] = fold(dgqa_ref)
        slab_ref[10:11, 256:256 + HD] = fold(dgka_ref)
        slab_ref[10:11, 384:384 + HD] = fold(dgkb_ref)
        slab_ref[10:11, 512:512 + HD] = fold(dgqb_ref)
        slab_ref[10:11, 640:640 + LANES] = dsk_ref[...]
        slab_ref[LOSS_ROW:LOSS_ROW + 1, 0:LANES] = ls_ref[...]

    return pl.pallas_call(body, name="pack_small", out_shape=_sds((SLAB_ROWS, D), F32), compiler_params=_params())(
        dg_a, dg_kv, dg_b, db_f, dgq_a, dgk_a, dgk_b, dgq_b, dsinks, lsum)


def _small_adamw(recv, ws, ms, vs):
    k = len(SMALL)

    def body(*refs):
        r_ref = refs[0]
        w_refs, m_refs, v_refs = refs[1:1 + k], refs[1 + k:1 + 2 * k], refs[1 + 2 * k:1 + 3 * k]
        outs = refs[1 + 3 * k:1 + 7 * k]
        loss_ref, tot = refs[1 + 7 * k], refs[2 + 7 * k]
        g = r_ref[0]
        for dev in range(1, N_DEV):
            g = g + r_ref[dev]
        tot[...] = g
        loss_ref[...] = tot[LOSS_ROW:LOSS_ROW + 1, 0:LANES] * (0.5 / D)
        me = 4 * lax.axis_index("x") + 2 * lax.axis_index("y") + lax.axis_index("c")
        for p, name in enumerate(SMALL):
            if name == "norm_a_g":
                mine = lax.broadcasted_iota(jnp.int32, (N_DEV, LANES), 0) == me
                gp = jnp.sum(jnp.where(mine, tot[0:N_DEV, 0:LANES], 0.0), axis=0, keepdims=True)
            else:
                row, lo, width = SLOT[name]
                gp = tot[row:row + 1, lo:lo + width]
            d, nm, nv = _adamw(w_refs[p][...], gp, m_refs[p][...], v_refs[p][...])
            outs[p][...] = gp
            outs[k + p][...] = d
            outs[2 * k + p][...] = nm
            outs[3 * k + p][...] = nv

    shapes = [_sds(w.shape, F32) for w in ws]
    return pl.pallas_call(body, name="small_adamw", out_shape=shapes * 4 + [_sds((1, LANES), F32)],
                          scratch_shapes=[pltpu.VMEM((SLAB_ROWS, D), F32)],
                          compiler_params=_params())(recv, *ws, *ms, *vs)


def _rope_tables(positions):
    inv_freq = jnp.power(jnp.float32(ROPE_THETA), -jnp.arange(0, ROT, 2, dtype=F32) / ROT)
    ang = positions.astype(F32)[:, None] * inv_freq[None, :]
    cos, sin = jnp.cos(ang), jnp.sin(ang)
    c64 = jnp.concatenate([cos, cos, jnp.ones((S, HD - ROT), F32)], axis=-1)
    s64 = jnp.concatenate([-sin, sin, jnp.zeros((S, HD - ROT), F32)], axis=-1)
    return jnp.tile(c64, (1, 2)), jnp.tile(s64, (1, 2))


def _local_step(x, tgt, positions, g_a, wa, b_forget, gq_a, gk_a, g_kv, gk_b, g_b, gq_b, sinks,
                woa_s, wkv_s, wib_s, wob_s, adamw_others):
    nq = S // TQ
    cos2, sin2 = _rope_tables(positions)
    b_pad = jnp.pad(b_forget, ((0, 0), (0, LANES - N_HEADS)))

    u_a, proj, qn, kn, vb, ccol, cbc = _head_a(x, g_a, wa, gq_a, gk_a, b_pad)
    crow = ccol[:, :N_HEADS].T.reshape(N_HEADS, nq, 1, TQ)
    o_a, z_a, lse_a, woa_g, wkv_g, w_in_b, wob_g = _fox_fwd(
        qn, kn, vb, proj, crow, cbc,
        rider=[("gather_rows", woa_s), ("gather_rows", wkv_s), ("gather_cols", wib_s), ("gather_rows", wob_s)])
    w_out_a, w_kv, w_out_b = woa_g.reshape(D, D), wkv_g.reshape(D, 512), wob_g.reshape(D, D)
    h1, u_kv, u_b, kv, pb, qb, ksh, vsh = _head_b(x, z_a, w_out_a, g_kv, g_b, w_kv, w_in_b, gq_b, gk_b, cos2, sin2)
    sinks1 = sinks.reshape(N_HEADS)
    o_b, z_b, lse_b = _swa_fwd(qb, ksh, vsh, pb, sinks1)
    dy, lsum = _out_b_loss(z_b, w_out_b, h1, tgt)
    dw_out_b = _mm(z_b, dy, "tn", 512, 512, S, out_dtype=BF16, name="mm_dw_out_b")
    dz_b = _mm(dy, w_out_b, "nt", 1024, 512, D, name="mm_dz_b")
    dpb, dka, dkb, dva, dvb, dsinks, dgq_b = _swa_bwd(qb, ksh, vsh, dz_b, o_b, lse_b, pb, sinks1, gq_b, cos2, sin2)
    dkv, dgk_b = _prep_kv_bwd(dka, dkb, dva, dvb, kv, gk_b, cos2, sin2)
    dw_in_b = _mm(u_b, dpb, "tn", 512, 512, S, out_dtype=BF16, name="mm_dw_in_b")
    dw_kv = _mm(u_kv, dkv, "tn", 512, 512, S, out_dtype=BF16, name="mm_dw_kv")
    dh1, dg_b, dg_kv = _du_b_rms_bwd(dpb, w_in_b, dkv, w_kv, h1, g_b, g_kv, dy)
    dw_out_a = _mm(z_a, dh1, "tn", 512, 512, S, out_dtype=BF16, name="mm_dw_out_a")
    do_a, dgate_a, delta_a = _fox_bwd_pre(dh1, w_out_a, proj, o_a)
    dk_a, dv_a, dcs, dq_a, drow, r_wob, r_wib, r_wkv, r_woa = _fox_bwd(
        qn, kn, vb, do_a, lse_a, delta_a, crow, cbc,
        rider=[("a2a_rows", dw_out_b), ("a2a_cols", dw_in_b), ("a2a_rows", dw_kv), ("a2a_rows", dw_out_a)])
    drow_col = jnp.pad(drow.reshape(N_HEADS, S).T, ((0, 0), (0, LANES - N_HEADS)))
    dproj, dgq_a, dgk_a, db_f = _prep_a_bwd(dq_a, dk_a, dv_a, dgate_a, drow_col, dcs, proj, b_pad, gq_a, gk_a)
    dwa = _mm(u_a, dproj, "tn", 1024, 256, S, out_dtype=BF16, name="mm_dw_in_a")
    partial = _reshard_pair_reduce(dwa)
    started = _chip_exchange_start(partial)
    dx, dg_a = _du_a_rms_bwd(dproj, wa, x, g_a, dh1, after=started[-1])
    others = adamw_others(dict(w_out_a=r_woa, w_kv=r_wkv, w_in_b=r_wib, w_out_b=r_wob), dg_a)
    partial, landed = _chip_exchange_wait(started, [res[0] for res in others.values()])
    slab = _pack_small(dg_a, dg_kv, dg_b, db_f, dgq_a, dgk_a, dgk_b, dgq_b, dsinks, lsum)
    return dx, (landed, partial), others, _gather_slab(slab, landed)


def kernel(x, positions, norm_a_g, w_in_a, b_forget, qnorm_a_g, knorm_a_g, w_out_a, kv_norm_g, w_kv, knorm_b_g, norm_b_g, w_in_b, qnorm_b_g, sinks, w_out_b, loss_target, m_norm_a_g, m_w_in_a, m_b_forget, m_qnorm_a_g, m_knorm_a_g, m_w_out_a, m_kv_norm_g, m_w_kv, m_knorm_b_g, m_norm_b_g, m_w_in_b, m_qnorm_b_g, m_sinks, m_w_out_b, v_norm_a_g, v_w_in_a, v_b_forget, v_qnorm_a_g, v_knorm_a_g, v_w_out_a, v_kv_norm_g, v_w_kv, v_knorm_b_g, v_norm_b_g, v_w_in_b, v_qnorm_b_g, v_sinks, v_w_out_b):
    wa_g, ga_g, woa_s, wkv_s, wib_s, wob_s = _gather_first(w_in_a, w_out_a, w_kv, w_in_b, w_out_b, norm_a_g)
    state = dict(w_in_a=(w_in_a, m_w_in_a, v_w_in_a), w_out_a=(w_out_a, m_w_out_a, v_w_out_a),
                 w_kv=(w_kv, m_w_kv, v_w_kv), w_in_b=(w_in_b, m_w_in_b, v_w_in_b),
                 w_out_b=(w_out_b, m_w_out_b, v_w_out_b))

    def adamw_others(landed, after):
        return {n: _sum_adamw(r, *state[n], "adamw_" + n, after=after) for n, r in landed.items()}

    dx, r_wa, big, slab_g = _local_step(
        x[0], loss_target[0], positions, ga_g.reshape(1, D), _unshard_wa(wa_g), b_forget, qnorm_a_g, knorm_a_g,
        kv_norm_g.reshape(1, D), knorm_b_g.reshape(1, HD), norm_b_g, qnorm_b_g, sinks, woa_s, wkv_s, wib_s, wob_s,
        adamw_others)
    big["w_in_a"] = _sum_adamw_entry_view(*r_wa, *state["w_in_a"], "adamw_w_in_a")

    r2 = lambda a: a.reshape(1, -1)
    small_w = dict(norm_a_g=norm_a_g, b_forget=b_forget, qnorm_a_g=qnorm_a_g, knorm_a_g=knorm_a_g,
                   kv_norm_g=kv_norm_g, knorm_b_g=knorm_b_g, norm_b_g=norm_b_g, qnorm_b_g=qnorm_b_g, sinks=sinks)
    small_m = dict(norm_a_g=m_norm_a_g, b_forget=m_b_forget, qnorm_a_g=m_qnorm_a_g, knorm_a_g=m_knorm_a_g,
                   kv_norm_g=m_kv_norm_g, knorm_b_g=m_knorm_b_g, norm_b_g=m_norm_b_g, qnorm_b_g=m_qnorm_b_g,
                   sinks=m_sinks)
    small_v = dict(norm_a_g=v_norm_a_g, b_forget=v_b_forget, qnorm_a_g=v_qnorm_a_g, knorm_a_g=v_knorm_a_g,
                   kv_norm_g=v_kv_norm_g, knorm_b_g=v_knorm_b_g, norm_b_g=v_norm_b_g, qnorm_b_g=v_qnorm_b_g,
                   sinks=v_sinks)
    res = _small_adamw(slab_g, [r2(small_w[n]) for n in SMALL], [r2(small_m[n]) for n in SMALL],
                       [r2(small_v[n]) for n in SMALL])
    k = len(SMALL)
    small = {n: [res[q * k + p].reshape(small_w[n].shape) for q in range(4)] for p, n in enumerate(SMALL)}
    loss = res[4 * k][0, 0]

    order = ["norm_a_g", "w_in_a", "b_forget", "qnorm_a_g", "knorm_a_g", "w_out_a", "kv_norm_g", "w_kv",
             "knorm_b_g", "norm_b_g", "w_in_b", "qnorm_b_g", "sinks", "w_out_b"]

    def leaf(n, q):
        return big[n][q] if n in big else small[n][q]

    outs = [loss, dx[None]]
    for q in range(4):
        outs.extend(leaf(n, q) for n in order)
    return tuple(outs)
```

```python
import jax
import jax.numpy as jnp
from jax import lax
from jax.experimental import pallas as pl
from jax.experimental.pallas import tpu as pltpu

F32, BF16 = jnp.float32, jnp.bfloat16

S = 2048
D = 1024
HD = 64
N_HEADS = 16
N_DEV = 8
NA = 4352
GOFF = 3072
FOFF = 4096
RAW_F = 3072
RAW_G = RAW_F + N_HEADS
NA_RAW = 4112
EPS = 1e-6
QSCALE = 0.125
ROPE_THETA = 500000.0
ROT = 16
WIN = 128
TQ = 256
TK = 256
KS = TQ // 2
HPS = 8
HW = HPS * HD
TM = 256
RT = 256
RB = 512
CB = 256
LANES = 128

ADAM_LR, ADAM_B1, ADAM_B2, ADAM_EPS, ADAM_WD, ADAM_STEP = 0.001, 0.9, 0.999, 1e-08, 0.01, 10

VMEM_LIMIT = 56 * 1024 * 1024


def _params():
    return pltpu.CompilerParams(vmem_limit_bytes=VMEM_LIMIT)


def _sds(shape, dtype):
    return jax.ShapeDtypeStruct(shape, dtype)


def _dot_nt(a, b):
    return lax.dot_general(a, b, (((1,), (1,)), ((), ())), preferred_element_type=F32)


def _dot_tn(a, b):
    return lax.dot_general(a, b, (((0,), (0,)), ((), ())), preferred_element_type=F32)


def _dot_nn(a, b):
    return lax.dot_general(a, b, (((1,), (0,)), ((), ())), preferred_element_type=F32)


def _sigmoid(g):
    return 1.0 / (1.0 + jnp.exp(-g))


def _lane_iota(shape):
    return lax.broadcasted_iota(jnp.int32, shape, len(shape) - 1)


def _flips(kind):
    return (2, 4, 6) if kind == "a2a_chips" else tuple(range(1, N_DEV))


def _send_view(kind, ref, dev):
    if kind in ("gather_rows", "gather_cols"):
        return ref
    if kind == "a2a_slots":
        return ref.at[dev]
    if kind == "a2a_chips":
        return ref.at[dev >> 1]
    if kind == "a2a_rows":
        rows = ref.shape[0] // N_DEV
        return ref.at[pl.ds(pl.multiple_of(dev * rows, rows), rows)]
    cols = ref.shape[1] // N_DEV
    return ref.at[:, pl.ds(pl.multiple_of(dev * cols, cols), cols)]


def _land_view(kind, ref, dev):
    if kind == "gather_cols":
        cols = ref.shape[1] // N_DEV
        return ref.at[:, pl.ds(pl.multiple_of(dev * cols, cols), cols)]
    if kind == "a2a_chips":
        return ref.at[dev >> 1]
    return ref.at[dev]


def _landing_sds(kind, arr):
    if kind == "gather_rows":
        return _sds((N_DEV,) + arr.shape, arr.dtype)
    if kind == "gather_cols":
        return _sds((arr.shape[0], N_DEV * arr.shape[1]), arr.dtype)
    if kind == "a2a_rows":
        return _sds((N_DEV, arr.shape[0] // N_DEV, arr.shape[1]), arr.dtype)
    if kind == "a2a_cols":
        return _sds((N_DEV, arr.shape[0], arr.shape[1] // N_DEV), arr.dtype)
    return _sds(arr.shape, arr.dtype)


def _exchange_sems(n_parts):
    n = n_parts * (N_DEV - 1)
    return [pltpu.SemaphoreType.DMA((n,)), pltpu.SemaphoreType.DMA((n,)), pltpu.SemaphoreType.DMA((n_parts,))]


def _exchange_ops(kinds, srcs, dsts, sems, start, wait):
    send_sems, recv_sems, local_sems = sems
    x, y, c = lax.axis_index("x"), lax.axis_index("y"), lax.axis_index("c")
    me = 4 * x + 2 * y + c

    def local(a):
        return pltpu.make_async_copy(_send_view(kinds[a], srcs[a], me), _land_view(kinds[a], dsts[a], me),
                                     local_sems.at[a])

    def remote(a, k, landing_dev):
        peer = (x ^ ((k >> 2) & 1), y ^ ((k >> 1) & 1), c ^ (k & 1))
        sem = a * (N_DEV - 1) + k - 1
        return pltpu.make_async_remote_copy(
            src_ref=_send_view(kinds[a], srcs[a], me ^ k), dst_ref=_land_view(kinds[a], dsts[a], landing_dev),
            send_sem=send_sems.at[sem], recv_sem=recv_sems.at[sem], device_id=peer,
            device_id_type=pl.DeviceIdType.MESH)

    pairs = [(a, k) for k in range(1, N_DEV) for a in range(len(kinds)) if k in _flips(kinds[a])]
    if start:
        for a in range(len(kinds)):
            local(a).start()
        for a, k in pairs:
            remote(a, k, me).start()
    if wait:
        for a, k in pairs:
            remote(a, k, me ^ k).wait_recv()
            remote(a, k, me).wait_send()
        for a in range(len(kinds)):
            local(a).wait()


def _gather_two_level(srcs, landing_of, sems, meanwhile=None):
    send_sems, recv_sems, local_sems = sems
    x, y, c = lax.axis_index("x"), lax.axis_index("y"), lax.axis_index("c")
    me, sibling = (x, y, c), (x, y, 1 - c)
    chips = [(1 - x, y), (x, 1 - y), (1 - x, 1 - y)]

    def slot(a, dev):
        return landing_of[a](4 * dev[0] + 2 * dev[1] + dev[2])

    def copy(a, k, block, to, src=None):
        return pltpu.make_async_remote_copy(
            src_ref=slot(a, block) if src is None else src, dst_ref=slot(a, block),
            send_sem=send_sems.at[a * (N_DEV - 1) + k], recv_sem=recv_sems.at[a * (N_DEV - 1) + k],
            device_id=to, device_id_type=pl.DeviceIdType.MESH)

    parts = range(len(srcs))
    mine = [pltpu.make_async_copy(srcs[a], slot(a, me), local_sems.at[a]) for a in parts]
    first = [copy(a, 1 + j, me, (*chip, c), src=srcs[a]) for a in parts for j, chip in enumerate(chips)]
    first += [copy(a, 0, me, sibling, src=srcs[a]) for a in parts]
    for cp in first + mine:
        cp.start()
    if meanwhile is not None:
        meanwhile()
    passed = []
    for a in parts:
        for j, chip in enumerate(chips):
            copy(a, 1 + j, (*chip, c), me).wait_recv()
            fwd = copy(a, 4 + j, (*chip, c), sibling)
            fwd.start()
            passed.append(fwd)
    for a in parts:
        copy(a, 0, sibling, me).wait_recv()
        for j, chip in enumerate(chips):
            copy(a, 4 + j, (*chip, 1 - c), me).wait_recv()
    for cp in first + passed:
        cp.wait_send()
    for cp in mine:
        cp.wait()


def _call(body, *, name, args, in_specs, out_specs, out_shape, grid=(), scratch_shapes=(), aliases=None, rider=()):
    n_in, n_out, n_scr, n_r = len(in_specs), len(out_specs), len(scratch_shapes), len(rider)
    kinds = [kind for kind, _ in rider]

    def kernel_body(*refs):
        c_in, r_in = refs[:n_in], refs[n_in:n_in + n_r]
        c_out = refs[n_in + n_r:n_in + n_r + n_out]
        r_out = refs[n_in + n_r + n_out:n_in + 2 * n_r + n_out]
        rest = refs[n_in + 2 * n_r + n_out:]
        c_scr, sems = rest[:n_scr], rest[n_scr:]
        if n_r:
            assert grid, "a rider needs a gridded call"
            ids = [pl.program_id(ax) for ax in range(len(grid))]
            first, last = ids[0] == 0, ids[0] == grid[0] - 1
            for pid, size in zip(ids[1:], grid[1:]):
                first = first & (pid == 0)
                last = last & (pid == size - 1)
            pl.when(first)(lambda: _exchange_ops(kinds, r_in, r_out, sems, True, False))
        body(*c_in, *c_out, *c_scr)
        if n_r:
            pl.when(last)(lambda: _exchange_ops(kinds, r_in, r_out, sems, False, True))

    anyspec = pl.BlockSpec(memory_space=pl.ANY)
    params = pltpu.CompilerParams(vmem_limit_bytes=VMEM_LIMIT, has_side_effects=bool(n_r))
    outs = pl.pallas_call(
        kernel_body, name=name, grid=grid, in_specs=list(in_specs) + [anyspec] * n_r,
        out_specs=list(out_specs) + [anyspec] * n_r,
        out_shape=list(out_shape) + [_landing_sds(kind, arr) for kind, arr in rider],
        scratch_shapes=list(scratch_shapes) + (_exchange_sems(n_r) if n_r else []),
        input_output_aliases=aliases or {}, compiler_params=params)(*args, *[arr for _, arr in rider])
    return list(outs)


def _mm(a, b, mode, tm, tn, tk, out_dtype=F32, add=None, name="mm", rider=()):
    if mode == "nn":
        (m, k), n = a.shape, b.shape[1]
        a_spec = pl.BlockSpec((tm, tk), lambda i, j, kk: (i, kk))
        b_spec = pl.BlockSpec((tk, tn), lambda i, j, kk: (kk, j))
        dot = _dot_nn
    elif mode == "nt":
        (m, k), n = a.shape, b.shape[0]
        a_spec = pl.BlockSpec((tm, tk), lambda i, j, kk: (i, kk))
        b_spec = pl.BlockSpec((tn, tk), lambda i, j, kk: (j, kk))
        dot = _dot_nt
    else:
        (k, m), n = a.shape, b.shape[1]
        a_spec = pl.BlockSpec((tk, tm), lambda i, j, kk: (kk, i))
        b_spec = pl.BlockSpec((tk, tn), lambda i, j, kk: (kk, j))
        dot = _dot_tn
    assert m % tm == 0 and n % tn == 0 and k % tk == 0, (m, n, k, tm, tn, tk)
    nk = k // tk
    has_add = add is not None

    def body(*refs):
        if has_add:
            a_ref, b_ref, add_ref, o_ref, acc = refs
        else:
            a_ref, b_ref, o_ref, acc = refs
        p = dot(a_ref[...].astype(BF16), b_ref[...].astype(BF16))

        def finish(total):
            if has_add:
                total = add_ref[...] + total
            o_ref[...] = total.astype(out_dtype)

        if nk == 1:
            finish(p)
        else:
            kk = pl.program_id(2)

            @pl.when(kk == 0)
            def _():
                acc[...] = p

            @pl.when(kk > 0)
            def _():
                acc[...] += p

            @pl.when(kk == nk - 1)
            def _():
                finish(acc[...])

    in_specs = [a_spec, b_spec]
    args = [a, b]
    if has_add:
        in_specs.append(pl.BlockSpec((tm, tn), lambda i, j, kk: (i, j)))
        args.append(add)
    acc_shape = (tm, tn) if nk > 1 else (8, LANES)
    outs = _call(body, name=name, args=args, grid=(m // tm, n // tn, nk), in_specs=in_specs,
                 out_specs=[pl.BlockSpec((tm, tn), lambda i, j, kk: (i, j))], out_shape=[_sds((m, n), out_dtype)],
                 scratch_shapes=[pltpu.VMEM(acc_shape, F32)], rider=rider)
    return outs if rider else outs[0]


def _rms_rinv(x):
    return lax.rsqrt(jnp.mean(x * x, axis=-1, keepdims=True) + EPS)


def _rms_bwd_core(du, x, g):
    r = _rms_rinv(x)
    dug = du * g
    dx = r * (dug - x * ((r * r) * jnp.mean(dug * x, axis=-1, keepdims=True)))
    dg = jnp.sum(du * (x * r), axis=0, keepdims=True)
    return dx, dg


def _half_ones():
    r = lax.broadcasted_iota(jnp.int32, (LANES, LANES), 0)
    c = lax.broadcasted_iota(jnp.int32, (LANES, LANES), 1)
    return ((r < HD) == (c < HD)).astype(BF16)


def _half_sum(v, lo_half):
    if lo_half.dtype == jnp.bool_:
        s0 = jnp.sum(jnp.where(lo_half, v, 0.0), axis=-1, keepdims=True)
        s1 = jnp.sum(jnp.where(lo_half, 0.0, v), axis=-1, keepdims=True)
        return jnp.where(lo_half, s0, s1)
    hi = v.astype(BF16)
    lo = (v - hi.astype(F32)).astype(BF16)
    return _dot_nn(hi, lo_half) + _dot_nn(lo, lo_half)


def _head_rinv(x, lo_half):
    return lax.rsqrt(_half_sum(x * x, lo_half) * (1.0 / HD) + EPS)


def _head_norm_bwd(dn, x, g, lo_half):
    r = _head_rinv(x, lo_half)
    dng = dn * g
    dx = r * (dng - x * ((r * r) * (_half_sum(dng * x, lo_half) * (1.0 / HD))))
    dg = jnp.sum(dn * (x * r), axis=0, keepdims=True)
    return dx, dg


def _rope_swap(x, lane):
    l64 = lane & (HD - 1)
    return jnp.where(l64 < ROT // 2, pltpu.roll(x, LANES - ROT // 2, 1), pltpu.roll(x, ROT // 2, 1))


def _rope_fwd(x, cos, sin, lane):
    return x * cos + _rope_swap(x, lane) * sin


def _rope_bwd(dy, cos, sin, lane):
    return dy * cos + jnp.where((lane & (HD - 1)) < ROT, _rope_swap(dy * sin, lane), 0.0)


def _g2(g_ref):
    g = g_ref[...]
    return jnp.concatenate([g, g], axis=-1)


def _pairs(width):
    return [slice(LANES * c, LANES * (c + 1)) for c in range(width // LANES)]


def _pick_lane(block, lane, idx):
    return jnp.sum(jnp.where(lane == idx, block, 0.0), axis=-1, keepdims=True)


def _head_a(x, g, wa, gq, gk, b_pad):
    def body(x_ref, g_ref, w_ref, gq_ref, gk_ref, b_ref, u_ref, p_ref, qo_ref, ko_ref, vo_ref, c_ref, cbc_ref, carry):
        @pl.when(pl.program_id(0) == 0)
        def _():
            carry[...] = jnp.zeros_like(carry)

        xv = x_ref[...]
        u = ((xv * _rms_rinv(xv)) * g_ref[...]).astype(BF16)
        u_ref[...] = u
        for lo in range(0, NA, D):
            hi = min(lo + D, NA)
            p_ref[:, lo:hi] = _dot_nn(u, w_ref[:, lo:hi])
        lane = _lane_iota((RT, LANES))
        lo_half = lane < HD
        gq2, gk2 = _g2(gq_ref), _g2(gk_ref)
        for c in _pairs(D):
            q = p_ref[:, c]
            k = p_ref[:, D + c.start:D + c.stop]
            qo_ref[:, c] = (((q * _head_rinv(q, lo_half)) * gq2) * QSCALE).astype(BF16)
            ko_ref[:, c] = ((k * _head_rinv(k, lo_half)) * gk2).astype(BF16)
        vo_ref[...] = p_ref[:, 2 * D:3 * D].astype(BF16)

        z = p_ref[:, FOFF:FOFF + LANES] + b_ref[...]
        logf = jnp.minimum(z, 0.0) - jnp.log1p(jnp.exp(-jnp.abs(z)))
        r = lax.broadcasted_iota(jnp.int32, (RT, RT), 0)
        cc = lax.broadcasted_iota(jnp.int32, (RT, RT), 1)
        tri = (r >= cc).astype(F32)
        loc = jnp.dot(tri, logf, precision=lax.Precision.HIGHEST, preferred_element_type=F32) + carry[0:1, :]
        c_ref[...] = loc
        carry[0:1, :] = loc[RT - 1:RT, :]
        for h in range(N_HEADS):
            cbc_ref[:, LANES * h:LANES * (h + 1)] = jnp.broadcast_to(_pick_lane(loc, lane, h), (RT, LANES))

    row = lambda width: pl.BlockSpec((RT, width), lambda i: (i, 0))
    whole = lambda arr: pl.BlockSpec(arr.shape, lambda i: (0,) * arr.ndim, pipeline_mode=pl.Buffered(1))
    return pl.pallas_call(
        body, name="head_a", grid=(S // RT,),
        in_specs=[row(D), whole(g), whole(wa), whole(gq), whole(gk), whole(b_pad)],
        out_specs=[row(D), row(NA), row(D), row(D), row(D), row(LANES), row(N_HEADS * LANES)],
        out_shape=[_sds((S, D), BF16), _sds((S, NA), F32)] + [_sds((S, D), BF16)] * 3
        + [_sds((S, LANES), F32), _sds((S, N_HEADS * LANES), F32)],
        scratch_shapes=[pltpu.VMEM((8, LANES), F32)], compiler_params=_params())(x, g, wa, gq, gk, b_pad)


def _key_le_query(offset, keys=TK):
    r = lax.broadcasted_iota(jnp.int32, (keys, TQ), 0)
    c = lax.broadcasted_iota(jnp.int32, (keys, TQ), 1)
    return (r + offset) <= c


def _widen(tile):
    return jnp.concatenate([tile] * (TQ // LANES), axis=1)


def _fox_fwd(qn, kn, vb, proj, crow, cbc, rider=()):
    nq = S // TQ

    def body(q_ref, k_ref, v_ref, g_ref, cq_ref, cbc_ref, o_ref, z_ref, lse_ref, st_s, pt_s):
        i = pl.program_id(1)
        qs = [q_ref[:, HD * hh:HD * (hh + 1)] for hh in range(HPS)]
        cqs = [cq_ref[hh, 0] for hh in range(HPS)]

        def scores(s, hh):
            off = pl.multiple_of(s * KS, KS)
            kj = k_ref[pl.ds(off, KS), HD * hh:HD * (hh + 1)]
            return (_dot_nt(kj, qs[hh]) + cqs[hh]) - _widen(cbc_ref[pl.ds(off, KS), LANES * hh:LANES * (hh + 1)])

        def values(s, hh, pt):
            off = pl.multiple_of(s * KS, KS)
            return _dot_tn(v_ref[pl.ds(off, KS), HD * hh:HD * (hh + 1)], pt)

        def step(s, slot, carries, mask=None, last=False):
            if not last:
                for hh in range(HPS):
                    st_s[1 - slot, hh] = scores(s + 1, hh)
            pvs = [values(jnp.maximum(s - 1, 0), hh, pt_s[1 - slot, hh]) for hh in range(HPS)]
            out = []
            for hh in range(HPS):
                m, l, acc = carries[hh]
                st = st_s[slot, hh]
                if mask is not None:
                    st = jnp.where(mask, st, -jnp.inf)
                m_new = jnp.maximum(m, jnp.max(st, axis=0, keepdims=True))
                pt = jnp.exp(st - m_new)
                alpha = jnp.exp(m - m_new)
                pt_s[slot, hh] = pt.astype(BF16)
                out.append((m_new, alpha * l + jnp.sum(pt, axis=0, keepdims=True), alpha * (acc + pvs[hh])))
            return tuple(out)

        for hh in range(HPS):
            st_s[0, hh] = scores(0, hh)
            pt_s[1, hh] = jnp.zeros((KS, TQ), BF16)
        one = (jnp.full((1, TQ), -jnp.inf, F32), jnp.zeros((1, TQ), F32), jnp.zeros((HD, TQ), F32))
        carries = lax.fori_loop(0, i, lambda t, cr: step(2 * t + 1, 1, step(2 * t, 0, cr)), (one,) * HPS)
        carries = step(2 * i, 0, carries, mask=_key_le_query(0, KS))
        carries = step(2 * i + 1, 1, carries, mask=_key_le_query(KS, KS), last=True)
        accs = []
        for hh in range(HPS):
            m, l, acc = carries[hh]
            acc = acc + values(2 * i + 1, hh, pt_s[1, hh])
            accs.append(acc / l)
            lse_ref[hh, 0] = m + jnp.log(l)
        o = jnp.concatenate(accs, axis=0).T
        o_ref[...] = o
        g = g_ref[...]
        z_ref[...] = (o * (g * _sigmoid(g))).astype(BF16)

    qblk = pl.BlockSpec((TQ, HW), lambda hp, i: (i, hp))
    full = pl.BlockSpec((S, HW), lambda hp, i: (0, hp))
    rows = pl.BlockSpec((HPS, 1, 1, TQ), lambda hp, i: (hp, i, 0, 0))
    return _call(
        body, name="fox_fwd", args=(qn, kn, vb, proj, crow, cbc), grid=(N_HEADS // HPS, nq),
        in_specs=[qblk, full, full,
                  pl.BlockSpec((TQ, HW), lambda hp, i: (i, GOFF // HW + hp)),
                  rows, pl.BlockSpec((S, HPS * LANES), lambda hp, i: (0, hp))],
        out_specs=[qblk, qblk, rows],
        out_shape=[_sds((S, D), F32), _sds((S, D), BF16), _sds((N_HEADS, nq, 1, TQ), F32)],
        scratch_shapes=[pltpu.VMEM((2, HPS, KS, TQ), F32), pltpu.VMEM((2, HPS, KS, TQ), BF16)], rider=rider)


def _fox_bwd_pre(dh, w_out, proj, o):
    nq, rows = S // TQ, 2 * TQ
    per = rows // TQ

    def body(dh_ref, w_ref, g_ref, o_ref, do_ref, dg_ref, delta_ref):
        g = g_ref[...]
        sg = _sigmoid(g)
        dzv = _dot_nt(dh_ref[...].astype(BF16), w_ref[...])
        ov = o_ref[...]
        do = dzv * (g * sg)
        dg_ref[...] = (dzv * ov * (sg * (1.0 + g * (1.0 - sg)))).astype(BF16)
        do_ref[...] = do.astype(BF16)
        prod_t = (do * ov).T
        for h in range(N_HEADS):
            for b in range(per):
                delta_ref[h, b] = jnp.sum(prod_t[HD * h:HD * (h + 1), TQ * b:TQ * (b + 1)], axis=0, keepdims=True)

    row = pl.BlockSpec((rows, D), lambda i: (i, 0))
    return pl.pallas_call(
        body, name="fox_bwd_pre", grid=(S // rows,),
        in_specs=[row, pl.BlockSpec(w_out.shape, lambda i: (0, 0), pipeline_mode=pl.Buffered(1)),
                  pl.BlockSpec((rows, D), lambda i: (i, GOFF // D)), row],
        out_specs=[row, row, pl.BlockSpec((N_HEADS, per, 1, TQ), lambda i: (0, i, 0, 0))],
        out_shape=[_sds((S, D), BF16), _sds((S, D), BF16), _sds((N_HEADS, nq, 1, TQ), F32)],
        compiler_params=_params())(dh, w_out, proj, o)


def _fox_bwd(qn, kn, vb, dob, lse, delta, crow, cbc, rider=()):
    nq, nkb = S // TQ, S // TK

    def body(q_ref, k_ref, v_ref, do_ref, lse_ref, del_ref, cq_ref, cbc_ref,
             dk_ref, dv_ref, dcs_ref, dq_ref, dr_ref, st_s, dp_s, pt_s, ds_s, dq_acc, dr_acc):
        j = pl.program_id(1)

        @pl.when(j == 0)
        def _():
            dq_acc[...] = jnp.zeros_like(dq_acc)
            dr_acc[...] = jnp.zeros_like(dr_acc)

        kjs = [k_ref[:, HD * hh:HD * (hh + 1)] for hh in range(HPS)]
        vjs = [v_ref[:, HD * hh:HD * (hh + 1)] for hh in range(HPS)]

        def rows_of(ref, u, hh):
            off = pl.multiple_of(u * TQ, TQ)
            return ref[pl.ds(off, TQ), HD * hh:HD * (hh + 1)]

        def products(u, hh):
            st = (_dot_nt(kjs[hh], rows_of(q_ref, u, hh)) + cq_ref[hh, u]) - _widen(
                cbc_ref[:, LANES * hh:LANES * (hh + 1)])
            return st, _dot_nt(vjs[hh], rows_of(do_ref, u, hh))

        def step(u, slot, carries, masked=False):
            nxt = jnp.minimum(u + 1, nq - 1)
            for hh in range(HPS):
                st_s[1 - slot, hh], dp_s[1 - slot, hh] = products(nxt, hh)
            prev = jnp.maximum(u - 1, 0)
            dvs = [_dot_nn(pt_s[1 - slot, hh], rows_of(do_ref, prev, hh)) for hh in range(HPS)]
            dks = [_dot_nn(ds_s[1 - slot, hh], rows_of(q_ref, prev, hh)) for hh in range(HPS)]
            for hh in range(HPS):
                dq_acc[hh, prev] += _dot_tn(kjs[hh], ds_s[1 - slot, hh])
            out = []
            for hh in range(HPS):
                dk, dv, dcs = carries[hh]
                st = st_s[slot, hh]
                if masked:
                    st = jnp.where(_key_le_query((j - u) * TQ), st, -jnp.inf)
                pt = jnp.exp(st - lse_ref[hh, u])
                dst = pt * (dp_s[slot, hh] - del_ref[hh, u])
                pt_s[slot, hh] = pt.astype(BF16)
                ds_s[slot, hh] = dst.astype(BF16)
                dr_acc[hh, u] += jnp.sum(dst, axis=0, keepdims=True)
                out.append((dk + dks[hh], dv + dvs[hh], dcs + (dst[:, :LANES] + dst[:, LANES:])))
            return tuple(out)

        t0 = j // 2
        for hh in range(HPS):
            st_s[0, hh], dp_s[0, hh] = products(2 * t0, hh)
            pt_s[1, hh] = jnp.zeros((TK, TQ), BF16)
            ds_s[1, hh] = jnp.zeros((TK, TQ), BF16)
        one = (jnp.zeros((TK, HD), F32), jnp.zeros((TK, HD), F32), jnp.zeros((TK, LANES), F32))
        carries = step(2 * t0 + 1, 1, step(2 * t0, 0, (one,) * HPS, masked=True), masked=True)
        carries = lax.fori_loop(t0 + 1, nq // 2, lambda t, cr: step(2 * t + 1, 1, step(2 * t, 0, cr)), carries)
        dks, dvs = [], []
        lane = _lane_iota((TK, LANES))
        dcs_all = jnp.zeros((TK, LANES), F32)
        for hh in range(HPS):
            dk, dv, dcs = carries[hh]
            dks.append(dk + _dot_nn(ds_s[1, hh], rows_of(q_ref, nq - 1, hh)))
            dvs.append(dv + _dot_nn(pt_s[1, hh], rows_of(do_ref, nq - 1, hh)))
            dq_acc[hh, nq - 1] += _dot_tn(kjs[hh], ds_s[1, hh])
            dcs_all = jnp.where(lane == HPS * pl.program_id(0) + hh, -jnp.sum(dcs, axis=-1, keepdims=True), dcs_all)
        dcs_ref[0] = dcs_all
        dk_ref[...] = jnp.concatenate(dks, axis=-1)
        dv_ref[...] = jnp.concatenate(dvs, axis=-1).astype(BF16)

        @pl.when(j == nkb - 1)
        def _():
            for i in range(nq):
                dq_ref[TQ * i:TQ * (i + 1), :] = jnp.concatenate([dq_acc[hh, i] for hh in range(HPS)], axis=0).T
            dr_ref[...] = dr_acc[...]

    kblk = pl.BlockSpec((TK, HW), lambda hp, j: (j, hp))
    full = pl.BlockSpec((S, HW), lambda hp, j: (0, hp))
    rows = pl.BlockSpec((HPS, nq, 1, TQ), lambda hp, j: (hp, 0, 0, 0))
    cblk = pl.BlockSpec((TK, HPS * LANES), lambda hp, j: (j, hp))
    return _call(
        body, name="fox_bwd", args=(qn, kn, vb, dob, lse, delta, crow, cbc), grid=(N_HEADS // HPS, nkb),
        in_specs=[full, kblk, kblk, full, rows, rows, rows, cblk],
        out_specs=[kblk, kblk, pl.BlockSpec((1, TK, LANES), lambda hp, j: (hp, j, 0)), full, rows],
        out_shape=[_sds((S, D), F32), _sds((S, D), BF16), _sds((N_HEADS // HPS, S, LANES), F32), _sds((S, D), F32),
                   _sds((N_HEADS, nq, 1, TQ), F32)],
        scratch_shapes=[pltpu.VMEM((2, HPS, TK, TQ), F32), pltpu.VMEM((2, HPS, TK, TQ), F32),
                        pltpu.VMEM((2, HPS, TK, TQ), BF16), pltpu.VMEM((2, HPS, TK, TQ), BF16),
                        pltpu.VMEM((HPS, nq, HD, TQ), F32), pltpu.VMEM((HPS, nq, 1, TQ), F32)], rider=rider)


def _prep_a_bwd(dq, dk, dv, dgate, drow, dcs, proj, b_pad, gq, gk):
    nt = S // TM

    def body(dq_ref, dk_ref, dv_ref, dgt_ref, dr_ref, dcs_ref, xq_ref, xk_ref, f_ref, b_ref, gq_ref, gk_ref,
             o_ref, dgq_ref, dgk_ref, db_ref, carry):
        @pl.when(pl.program_id(0) == 0)
        def _():
            carry[...] = jnp.zeros_like(carry)
            dgq_ref[...] = jnp.zeros_like(dgq_ref)
            dgk_ref[...] = jnp.zeros_like(dgk_ref)
            db_ref[...] = jnp.zeros_like(db_ref)

        lane = _lane_iota((TM, LANES))
        lo_half = _half_ones()
        gq2, gk2 = _g2(gq_ref), _g2(gk_ref)
        dgq, dgk = jnp.zeros((1, LANES), F32), jnp.zeros((1, LANES), F32)
        for c in _pairs(D):
            dxq, dg = _head_norm_bwd(dq_ref[:, c] * QSCALE, xq_ref[:, c], gq2, lo_half)
            o_ref[:, c] = dxq.astype(BF16)
            dgq = dgq + dg
            dxk, dg = _head_norm_bwd(dk_ref[:, c], xk_ref[:, c], gk2, lo_half)
            o_ref[:, D + c.start:D + c.stop] = dxk.astype(BF16)
            dgk = dgk + dg
        dgq_ref[...] += dgq
        dgk_ref[...] += dgk
        o_ref[:, 2 * D:3 * D] = dv_ref[...]
        o_ref[:, GOFF:GOFF + D] = dgt_ref[...]

        dc = jnp.concatenate([dr_ref[:, 0, :], jnp.zeros((LANES - N_HEADS, TM), F32)], axis=0).T
        for group in range(N_HEADS // HPS):
            dc = dc + dcs_ref[group]
        r = lax.broadcasted_iota(jnp.int32, (TM, TM), 0)
        c = lax.broadcasted_iota(jnp.int32, (TM, TM), 1)
        tri = (c >= r).astype(F32)
        dlogf = jnp.dot(tri, dc, precision=lax.Precision.HIGHEST, preferred_element_type=F32) + carry[0:1, :]
        carry[0:1, :] = dlogf[0:1, :]
        df = dlogf * (1.0 / (1.0 + jnp.exp(f_ref[...] + b_ref[...])))
        db_ref[...] += jnp.sum(df, axis=0, keepdims=True)
        o_ref[:, FOFF:FOFF + LANES] = df.astype(BF16)
        o_ref[:, FOFF + LANES:NA] = jnp.zeros((TM, NA - FOFF - LANES), BF16)

    rev = lambda width, col: pl.BlockSpec((TM, width), lambda i: (nt - 1 - i, col))
    gspec = pl.BlockSpec((1, HD), lambda i: (0, 0))
    acc = pl.BlockSpec((1, LANES), lambda i: (0, 0))
    return pl.pallas_call(
        body, name="prep_a_bwd", grid=(nt,),
        in_specs=[rev(D, 0), rev(D, 0), rev(D, 0), rev(D, 0),
                  pl.BlockSpec((N_HEADS, None, 1, TM), lambda i: (0, nt - 1 - i, 0, 0)),
                  pl.BlockSpec((N_HEADS // HPS, TM, LANES), lambda i: (0, nt - 1 - i, 0)),
                  rev(D, 0), rev(D, 1), rev(LANES, FOFF // LANES), acc, gspec, gspec],
        out_specs=[rev(NA, 0), acc, acc, acc],
        out_shape=[_sds((S, NA), BF16)] + [_sds((1, LANES), F32)] * 3,
        scratch_shapes=[pltpu.VMEM((8, LANES), F32)],
        compiler_params=_params())(dq, dk, dv, dgate, drow, dcs, proj, proj, proj, b_pad, gq, gk)


def _head_b(x, z_a, w_out_a, g_kv, g_b, w_kv, w_in_b, gq, gk, cos2, sin2):
    nkv = w_kv.shape[1] // 2

    def body(x_ref, z_ref, wo_ref, gkv_ref, gb_ref, wkv_ref, wb_ref, gq_ref, gk_ref, c_ref, s_ref,
             h_ref, ukv_ref, ub_ref, kv_ref, pb_ref, qo_ref, ko_ref, vo_ref):
        xv = x_ref[...] + _dot_nn(z_ref[...], wo_ref[...])
        h_ref[...] = xv
        xn = xv * _rms_rinv(xv)
        ukv = (xn * gkv_ref[...]).astype(BF16)
        ub = (xn * gb_ref[...]).astype(BF16)
        ukv_ref[...] = ukv
        ub_ref[...] = ub
        kv_ref[...] = _dot_nn(ukv, wkv_ref[...])
        for lo in range(0, 2 * D, D):
            pb_ref[:, lo:lo + D] = _dot_nn(ub, wb_ref[:, lo:lo + D])
        lane = _lane_iota((RT, LANES))
        lo_half = lane < HD
        cos, sin = c_ref[...], s_ref[...]
        gq2, gk2 = _g2(gq_ref), _g2(gk_ref)
        for c in _pairs(D):
            q = pb_ref[:, c]
            qo_ref[:, c] = (_rope_fwd((q * _head_rinv(q, lo_half)) * gq2, cos, sin, lane) * QSCALE).astype(BF16)
        for c in _pairs(nkv):
            k = kv_ref[:, c]
            ko_ref[:, c] = _rope_fwd((k * _head_rinv(k, lo_half)) * gk2, cos, sin, lane).astype(BF16)
        vo_ref[...] = kv_ref[:, nkv:2 * nkv].astype(BF16)

    row = lambda width: pl.BlockSpec((RT, width), lambda i: (i, 0))
    whole = lambda arr: pl.BlockSpec(arr.shape, lambda i: (0,) * arr.ndim, pipeline_mode=pl.Buffered(1))
    return pl.pallas_call(
        body, name="head_b", grid=(S // RT,),
        in_specs=[row(D), row(D), whole(w_out_a), whole(g_kv), whole(g_b), whole(w_kv), whole(w_in_b), whole(gq),
                  whole(gk), row(LANES), row(LANES)],
        out_specs=[row(D), row(D), row(D), row(2 * nkv), row(2 * D), row(D), row(nkv), row(nkv)],
        out_shape=[_sds((S, D), F32), _sds((S, D), BF16), _sds((S, D), BF16), _sds((S, 2 * nkv), F32),
                   _sds((S, 2 * D), F32), _sds((S, D), BF16), _sds((S, nkv), BF16), _sds((S, nkv), BF16)],
        compiler_params=_params())(x, z_a, w_out_a, g_kv, g_b, w_kv, w_in_b, gq, gk, cos2, sin2)


N_KV, GRP = 4, 4


def _swa_mask(n):
    r = lax.broadcasted_iota(jnp.int32, (2 * WIN, GRP * WIN), 0)
    q = lax.broadcasted_iota(jnp.int32, (2 * WIN, GRP * WIN), 1) & (WIN - 1)
    return (r > q) & (r <= q + WIN) & ((r >= WIN) | (n > 0))


def _stack4(ref_or_val, base):
    return jnp.concatenate([ref_or_val[:, base + HD * g: base + HD * (g + 1)] for g in range(GRP)], axis=0)


def _unstack4(xt):
    return jnp.concatenate([xt[:, WIN * g:WIN * (g + 1)] for g in range(GRP)], axis=0).T


def _band(prev_ref, cur_ref, kh):
    return jnp.concatenate([prev_ref[:, HD * kh:HD * (kh + 1)], cur_ref[:, HD * kh:HD * (kh + 1)]], axis=0)


def _sink_row(s_ref, first):
    lane = _lane_iota((1, GRP * WIN))
    row = jnp.full((1, GRP * WIN), s_ref[first + GRP - 1], F32)
    for g in range(GRP - 2, -1, -1):
        row = jnp.where(lane < WIN * (g + 1), s_ref[first + g], row)
    return row


def _swa_fwd(qb, ksh, vsh, pb, sinks):
    nb = S // WIN

    def body(q_ref, kp_ref, kc_ref, vp_ref, vc_ref, g_ref, s_ref, o_ref, z_ref, lse_ref):
        n = pl.program_id(0)
        valid = _swa_mask(n)
        outs = []
        for kh in range(N_KV):
            kb, vb = _band(kp_ref, kc_ref, kh), _band(vp_ref, vc_ref, kh)
            st = jnp.where(valid, _dot_nt(kb, _stack4(q_ref, GRP * HD * kh)), -jnp.inf)
            sink = _sink_row(s_ref, GRP * kh)
            m = jnp.maximum(jnp.max(st, axis=0, keepdims=True), sink)
            pt = jnp.exp(st - m)
            l = jnp.sum(pt, axis=0, keepdims=True) + jnp.exp(sink - m)
            outs.append(_unstack4(_dot_tn(vb, pt.astype(BF16)) / l))
            lse = m + jnp.log(l)
            for g in range(GRP):
                lse_ref[GRP * kh + g, 0] = lse[:, WIN * g:WIN * (g + 1)]
        o = jnp.concatenate(outs, axis=-1)
        o_ref[...] = o
        g = g_ref[...]
        z_ref[...] = (o * (g * _sigmoid(g))).astype(BF16)

    row = pl.BlockSpec((WIN, D), lambda n: (n, 0))
    prev = pl.BlockSpec((WIN, N_KV * HD), lambda n: (jnp.maximum(n - 1, 0), 0))
    cur = pl.BlockSpec((WIN, N_KV * HD), lambda n: (n, 0))
    return pl.pallas_call(
        body, name="swa_fwd", grid=(nb,),
        in_specs=[row, prev, cur, prev, cur, pl.BlockSpec((WIN, D), lambda n: (n, 1)),
                  pl.BlockSpec(memory_space=pltpu.SMEM)],
        out_specs=[row, row, pl.BlockSpec((N_HEADS, 1, 1, WIN), lambda n: (0, n, 0, 0))],
        out_shape=[_sds((S, D), F32), _sds((S, D), BF16), _sds((N_HEADS, nb, 1, WIN), F32)],
        compiler_params=_params())(qb, ksh, ksh, vsh, vsh, pb, sinks)


def _swa_bwd(qb, ksh, vsh, dz, o, lse, pb, sinks, gq, cos2, sin2):
    nb = S // WIN

    def body(q_ref, kp_ref, kc_ref, vp_ref, vc_ref, dz_ref, o_ref, lse_ref, x_ref, g_ref, s_ref, gq_ref, c_ref, sn_ref,
             dpb_ref, dka_ref, dkb_ref, dva_ref, dvb_ref, dsink_ref, dgq_ref):
        n = pl.program_id(0)

        @pl.when(n == 0)
        def _():
            dsink_ref[...] = jnp.zeros_like(dsink_ref)
            dgq_ref[...] = jnp.zeros_like(dgq_ref)

        valid = _swa_mask(n)
        g = g_ref[...]
        sg = _sigmoid(g)
        dzv = dz_ref[...]
        ov = o_ref[...]
        do = dzv * (g * sg)
        dpb_ref[:, D:2 * D] = (dzv * ov * (sg * (1.0 + g * (1.0 - sg)))).astype(BF16)
        prod_t = (do * ov).T
        lane1 = _lane_iota((1, LANES))
        dqs, dkas, dkbs, dvas, dvbs = [], [], [], [], []
        dsink = jnp.zeros((1, LANES), F32)
        for kh in range(N_KV):
            kb, vb = _band(kp_ref, kc_ref, kh), _band(vp_ref, vc_ref, kh)
            base = GRP * HD * kh
            qs = _stack4(q_ref, base)
            dos = _stack4(do, base).astype(BF16)
            delta = jnp.concatenate(
                [jnp.sum(prod_t[base + HD * gg:base + HD * (gg + 1), :], axis=0, keepdims=True)
                 for gg in range(GRP)], axis=1)
            lse = jnp.concatenate([lse_ref[GRP * kh + gg, 0] for gg in range(GRP)], axis=1)
            st = jnp.where(valid, _dot_nt(kb, qs), -jnp.inf)
            pt = jnp.exp(st - lse)
            dst = pt * (_dot_nt(vb, dos) - delta)
            dsb = dst.astype(BF16)
            dqs.append(_unstack4(_dot_tn(kb, dsb)))
            dkband = _dot_nn(dsb, qs)
            dvband = _dot_nn(pt.astype(BF16), dos)
            dkbs.append(dkband[0:WIN, :])
            dkas.append(dkband[WIN:2 * WIN, :])
            dvbs.append(dvband[0:WIN, :])
            dvas.append(dvband[WIN:2 * WIN, :])
            ps_delta = jnp.exp(_sink_row(s_ref, GRP * kh) - lse) * delta
            for gg in range(GRP):
                val = jnp.sum(ps_delta[:, WIN * gg:WIN * (gg + 1)], axis=1, keepdims=True)
                dsink = dsink - jnp.where(lane1 == GRP * kh + gg, val, 0.0)
        dka_ref[...] = jnp.concatenate(dkas, axis=-1)
        dkb_ref[...] = jnp.concatenate(dkbs, axis=-1)
        dva_ref[...] = jnp.concatenate(dvas, axis=-1)
        dvb_ref[...] = jnp.concatenate(dvbs, axis=-1)
        dsink_ref[...] += dsink

        lane = _lane_iota((WIN, LANES))
        g2, cos, sin = _g2(gq_ref), c_ref[...], sn_ref[...]
        lo_half = _half_ones()
        dg_tot = jnp.zeros((1, LANES), F32)
        for kh in range(N_KV):
            for c in _pairs(GRP * HD):
                cols = slice(GRP * HD * kh + c.start, GRP * HD * kh + c.stop)
                dn = _rope_bwd(dqs[kh][:, c] * QSCALE, cos, sin, lane)
                dx, dg = _head_norm_bwd(dn, x_ref[:, cols], g2, lo_half)
                dpb_ref[:, cols] = dx.astype(BF16)
                dg_tot = dg_tot + dg
        dgq_ref[...] += dg_tot

    row = pl.BlockSpec((WIN, D), lambda n: (n, 0))
    prev = pl.BlockSpec((WIN, N_KV * HD), lambda n: (jnp.maximum(n - 1, 0), 0))
    cur = pl.BlockSpec((WIN, N_KV * HD), lambda n: (n, 0))
    acc = pl.BlockSpec((1, LANES), lambda n: (0, 0))
    tab = pl.BlockSpec((WIN, LANES), lambda n: (n, 0))
    return pl.pallas_call(
        body, name="swa_bwd", grid=(nb,),
        in_specs=[row, prev, cur, prev, cur, row, row, pl.BlockSpec((N_HEADS, 1, 1, WIN), lambda n: (0, n, 0, 0)),
                  row, pl.BlockSpec((WIN, D), lambda n: (n, 1)), pl.BlockSpec(memory_space=pltpu.SMEM),
                  pl.BlockSpec((1, HD), lambda n: (0, 0)), tab, tab],
        out_specs=[pl.BlockSpec((WIN, 2 * D), lambda n: (n, 0)), cur, cur, cur, cur, acc, acc],
        out_shape=[_sds((S, 2 * D), BF16)] + [_sds((S, 256), F32)] * 4 + [_sds((1, LANES), F32)] * 2,
        compiler_params=_params())(qb, ksh, ksh, vsh, vsh, dz, o, lse, pb, pb, sinks, gq, cos2, sin2)


def _prep_kv_bwd(dka, dkb, dva, dvb, kv, gk, cos2, sin2):
    nt = S // RB
    per = RB // WIN

    def shifted(cur_ref, nxt_ref, has_next):
        return jnp.concatenate([cur_ref[WIN:RB, :], jnp.where(has_next, nxt_ref[...], 0.0)], axis=0)

    def body(dka_ref, dkb_ref, dkn_ref, dva_ref, dvb_ref, dvn_ref, x_ref, g_ref, c_ref, s_ref, o_ref, dgk_ref):
        i, j = pl.program_id(0), pl.program_id(1)

        @pl.when((i == 0) & (j == 0))
        def _():
            dgk_ref[...] = jnp.zeros_like(dgk_ref)

        has_next = i < nt - 1

        @pl.when(j == 0)
        def _():
            lane = _lane_iota((RB, LANES))
            g2, cos, sin = _g2(g_ref), c_ref[...], s_ref[...]
            dy_all = dka_ref[...] + shifted(dkb_ref, dkn_ref, has_next)
            dg_tot = jnp.zeros((1, LANES), F32)
            lo_half = _half_ones()
            for c in _pairs(CB):
                dn = _rope_bwd(dy_all[:, c], cos, sin, lane)
                dx, dg = _head_norm_bwd(dn, x_ref[:, c], g2, lo_half)
                o_ref[:, c] = dx.astype(BF16)
                dg_tot = dg_tot + dg
            dgk_ref[...] += dg_tot

        @pl.when(j == 1)
        def _():
            o_ref[...] = (dva_ref[...] + shifted(dvb_ref, dvn_ref, has_next)).astype(BF16)

    cur = pl.BlockSpec((RB, CB), lambda i, j: (i, 0))
    nxt = pl.BlockSpec((WIN, CB), lambda i, j: (jnp.minimum(per * (i + 1), S // WIN - 1), 0))
    tab = pl.BlockSpec((RB, LANES), lambda i, j: (i, 0))
    return pl.pallas_call(
        body, name="prep_kv_bwd", grid=(nt, 2),
        in_specs=[cur, cur, nxt, cur, cur, nxt, cur, pl.BlockSpec((1, HD), lambda i, j: (0, 0)), tab, tab],
        out_specs=[pl.BlockSpec((RB, CB), lambda i, j: (i, j)), pl.BlockSpec((1, LANES), lambda i, j: (0, 0))],
        out_shape=[_sds((S, 2 * CB), BF16), _sds((1, LANES), F32)],
        compiler_params=_params())(dka, dkb, dkb, dva, dvb, dvb, kv, gk, cos2, sin2)


def _out_b_loss(z, w_out, h1, tgt):
    tm = 2 * RT

    def body(z_ref, w_ref, h_ref, t_ref, dy_ref, l_ref):
        @pl.when(pl.program_id(0) == 0)
        def _():
            l_ref[...] = jnp.zeros_like(l_ref)

        e = (h_ref[...] + _dot_nn(z_ref[...], w_ref[...])) - t_ref[...]
        dy_ref[...] = e * (1.0 / D)
        l_ref[...] += jnp.sum(jnp.sum(e * e, axis=-1, keepdims=True), axis=0, keepdims=True)

    row = pl.BlockSpec((tm, D), lambda i: (i, 0))
    return pl.pallas_call(
        body, name="out_b_loss", grid=(S // tm,),
        in_specs=[row, pl.BlockSpec(w_out.shape, lambda i: (0, 0), pipeline_mode=pl.Buffered(1)), row, row],
        out_specs=[row, pl.BlockSpec((1, LANES), lambda i: (0, 0))],
        out_shape=[_sds((S, D), F32), _sds((1, LANES), F32)], compiler_params=_params())(z, w_out, h1, tgt)


def _du_a_rms_bwd(dproj, wa, x, g, dres, after):
    tm, tk = 1024, NA // 2
    nk = NA // tk

    def body(a_ref, b_ref, x_ref, g_ref, dr_ref, after_ref, dx_ref, dg_ref, acc):
        i, kk = pl.program_id(0), pl.program_id(1)

        @pl.when((i == 0) & (kk == 0))
        def _():
            dg_ref[...] = jnp.zeros_like(dg_ref)

        p = _dot_nt(a_ref[...], b_ref[...])

        @pl.when(kk == 0)
        def _():
            acc[...] = p

        @pl.when(kk == nk - 1)
        def _():
            dx, dg = _rms_bwd_core(acc[...] + p, x_ref[...], g_ref[...])
            dx_ref[...] = dr_ref[...] + dx
            dg_ref[...] += dg

    assert nk == 2
    row = pl.BlockSpec((tm, D), lambda i, kk: (i, 0))
    vec = pl.BlockSpec((1, D), lambda i, kk: (0, 0))
    return pl.pallas_call(
        body, name="du_a_rms_bwd", grid=(S // tm, nk),
        in_specs=[pl.BlockSpec((tm, tk), lambda i, kk: (i, kk)), pl.BlockSpec((D, tk), lambda i, kk: (0, kk)),
                  row, vec, row, pl.BlockSpec(after.shape, lambda i, kk: (0, 0))],
        out_specs=[row, vec], out_shape=[_sds((S, D), F32), _sds((1, D), F32)],
        scratch_shapes=[pltpu.VMEM((tm, D), F32)], compiler_params=_params())(dproj, wa, x, g, dres, after)


def _du_b_rms_bwd(dpb, w_in_b, dkv, w_kv, h1, g_b, g_kv, dy):
    tm = 2 * RT

    def body(ab_ref, wb_ref, akv_ref, wkv_ref, x_ref, gb_ref, gkv_ref, dy_ref, dh_ref, dgb_ref, dgkv_ref):
        @pl.when(pl.program_id(0) == 0)
        def _():
            dgb_ref[...] = jnp.zeros_like(dgb_ref)
            dgkv_ref[...] = jnp.zeros_like(dgkv_ref)

        x = x_ref[...]
        dx1, dg1 = _rms_bwd_core(_dot_nt(ab_ref[...], wb_ref[...]), x, gb_ref[...])
        dx2, dg2 = _rms_bwd_core(_dot_nt(akv_ref[...], wkv_ref[...]), x, gkv_ref[...])
        dh_ref[...] = dy_ref[...] + dx1 + dx2
        dgb_ref[...] += dg1
        dgkv_ref[...] += dg2

    row = lambda width: pl.BlockSpec((tm, width), lambda i: (i, 0))
    whole = lambda arr: pl.BlockSpec(arr.shape, lambda i: (0, 0), pipeline_mode=pl.Buffered(1))
    vec = pl.BlockSpec((1, D), lambda i: (0, 0))
    return pl.pallas_call(
        body, name="du_b_rms_bwd", grid=(S // tm,),
        in_specs=[row(dpb.shape[1]), whole(w_in_b), row(dkv.shape[1]), whole(w_kv), row(D), vec, vec, row(D)],
        out_specs=[row(D), vec, vec], out_shape=[_sds((S, D), F32), _sds((1, D), F32), _sds((1, D), F32)],
        compiler_params=_params())(dpb, w_in_b, dkv, w_kv, h1, g_b, g_kv, dy)


GATHER_CHUNKS = 4


def _gather_first(w_in_a, w_out_a, w_kv, w_in_b, w_out_b, norm_a_g):
    rows, cols = w_in_a.shape[-2:]
    groups = rows // LANES
    cols_pad = -(-cols // LANES) * LANES

    def body(wia_ref, woa_ref, wkv_ref, wib_ref, wob_ref, ga_ref,
             wa_g, ga_g, woa_s, wkv_s, wib_s, wob_s, wa_s, st_a, st_oa, st_kv, st_ib, st_ob, plane_t, load_sems, *sems):
        sources = [wia_ref, woa_ref.at[0], wkv_ref, wib_ref.at[0], wob_ref.at[0]]
        stages = [st_a, st_oa, st_kv, st_ib, st_ob]
        loads = [pltpu.make_async_copy(src, dst, load_sems.at[i]) for i, (src, dst) in enumerate(zip(sources, stages))]
        for cp in loads:
            cp.start()
        loads[0].wait()
        plane_t[cols_pad - LANES:, :] = jnp.zeros((LANES, LANES), F32)
        for j in range(groups):
            for c0 in range(0, cols, 64):
                n = min(64, cols - c0)
                plane_t[c0:c0 + n, :] = st_a[pl.ds(c0 * groups + j, n, stride=groups), :]
            for c0 in range(0, cols, LANES):
                n = min(LANES, cols - c0)
                wa_s[j * LANES:(j + 1) * LANES, c0:c0 + n] = plane_t[c0:c0 + LANES, :].T[:, :n].astype(BF16)

        def cast_the_rest():
            for cp, stage, out in zip(loads[1:], stages[1:], [woa_s, wkv_s, wib_s, wob_s]):
                cp.wait()
                out[...] = stage[...].astype(BF16)

        chunks = [pl.ds(r0, rows // GATHER_CHUNKS) for r0 in range(0, rows, rows // GATHER_CHUNKS)]
        _gather_two_level(
            [wa_s.at[rc] for rc in chunks] + [ga_ref],
            [lambda dev, rc=rc: wa_g.at[dev, rc] for rc in chunks] + [lambda dev: ga_g.at[dev]],
            sems, meanwhile=cast_the_rest)

    vmem = pl.BlockSpec(memory_space=pltpu.VMEM)
    anyspec = pl.BlockSpec(memory_space=pl.ANY)
    shard = lambda w: _sds(w.shape[-2:], BF16)
    stage = lambda w: pltpu.VMEM(w.shape[-2:], F32)
    return pl.pallas_call(
        body, name="gather_first", in_specs=[anyspec] * 5 + [vmem],
        out_specs=[anyspec, anyspec, vmem, vmem, vmem, vmem],
        out_shape=[_sds((N_DEV,) + w_in_a.shape[-2:], BF16), _sds((N_DEV,) + norm_a_g.shape, F32),
                   shard(w_out_a), shard(w_kv), shard(w_in_b), shard(w_out_b)],
        scratch_shapes=[pltpu.VMEM(w_in_a.shape[-2:], BF16), pltpu.VMEM((cols * groups, LANES), F32), stage(w_out_a),
                        stage(w_kv), stage(w_in_b), stage(w_out_b), pltpu.VMEM((cols_pad, LANES), F32),
                        pltpu.SemaphoreType.DMA((5,))] + _exchange_sems(GATHER_CHUNKS + 1),
        compiler_params=pltpu.CompilerParams(vmem_limit_bytes=VMEM_LIMIT, has_side_effects=True))(
            _entry_view(w_in_a).reshape(cols * groups, LANES), w_out_a, w_kv, w_in_b, w_out_b, norm_a_g)


def _padded_col(c):
    if c < RAW_F:
        return c
    return FOFF + (c - RAW_F) if c < RAW_G else GOFF + (c - RAW_G)


def _shard_pieces():
    width = NA_RAW // N_DEV
    pieces = []
    for d in range(N_DEV):
        cuts = [width * d] + [c for c in (RAW_F, RAW_G) if width * d < c < width * (d + 1)] + [width * (d + 1)]
        for lo, hi in zip(cuts[:-1], cuts[1:]):
            pieces.append((d, lo - width * d, _padded_col(lo), hi - lo))
    return pieces


def _unshard_wa(wa_g):
    def body(w_ref, o_ref):
        o_ref[:, FOFF + N_HEADS:NA] = jnp.zeros((TM, NA - FOFF - N_HEADS), BF16)
        for d, src, dst, width in _shard_pieces():
            o_ref[:, dst:dst + width] = w_ref[d, :, src:src + width]

    return pl.pallas_call(
        body, name="unshard_wa", grid=(D // TM,),
        in_specs=[pl.BlockSpec((N_DEV, TM, NA_RAW // N_DEV), lambda i: (0, i, 0))],
        out_specs=pl.BlockSpec((TM, NA), lambda i: (i, 0)), out_shape=_sds((D, NA), BF16),
        compiler_params=_params())(wa_g)


def _reshard_pair_reduce(dwa):
    n_chip, nt = N_DEV // 2, D // TM
    width = NA_RAW // N_DEV

    def body(g_ref, o_ref, slots_v, sib_v, send_sems, recv_sems):
        i = pl.program_id(0)
        x, y, c = lax.axis_index("x"), lax.axis_index("y"), lax.axis_index("c")

        def tile_rows(tile):
            return pl.ds(tile * TM if isinstance(tile, int) else pl.multiple_of(tile * TM, TM), TM)

        def give(j, tile):
            return pltpu.make_async_remote_copy(
                src_ref=slots_v.at[2 * j + 1 - c, tile_rows(tile)], dst_ref=sib_v.at[j, tile_rows(tile)],
                send_sem=send_sems.at[j, tile], recv_sem=recv_sems.at[j, tile], device_id=(x, y, 1 - c),
                device_id_type=pl.DeviceIdType.MESH)

        for d, src, dst, w in _shard_pieces():
            slots_v[d, tile_rows(i), src:src + w] = g_ref[:, dst:dst + w]
        for j in range(n_chip):
            give(j, i).start()

        @pl.when(i == nt - 1)
        def _():
            for tile in range(nt):
                for j in range(n_chip):
                    give(j, tile).wait()
            for j in range(n_chip):
                for r0 in range(0, D, 64):
                    mine = slots_v[2 * j + c, r0:r0 + 64, :].astype(F32)
                    o_ref[j, r0:r0 + 64, :] = (mine + sib_v[j, r0:r0 + 64, :].astype(F32)).astype(BF16)

    half = _sds((n_chip, D, width), dwa.dtype)
    return pl.pallas_call(
        body, name="reshard_pair_reduce", grid=(nt,), in_specs=[pl.BlockSpec((TM, NA), lambda i: (i, 0))],
        out_specs=pl.BlockSpec(half.shape, lambda i: (0, 0, 0)), out_shape=half,
        scratch_shapes=[pltpu.VMEM((N_DEV, D, width), dwa.dtype), pltpu.VMEM(half.shape, dwa.dtype),
                        pltpu.SemaphoreType.DMA((n_chip, nt)), pltpu.SemaphoreType.DMA((n_chip, nt))],
        compiler_params=pltpu.CompilerParams(vmem_limit_bytes=VMEM_LIMIT, has_side_effects=True))(dwa)


CHIP_FLIPS = (2, 4, 6)


def _chip_exchange_start(partial):
    n = len(CHIP_FLIPS)

    def body(p_ref, land_ref, *rest):
        sends, recvs, token = rest[:n], rest[n:2 * n], rest[2 * n + 2]
        x, y, c = lax.axis_index("x"), lax.axis_index("y"), lax.axis_index("c")
        me = 4 * x + 2 * y + c
        for idx, k in enumerate(CHIP_FLIPS):
            pltpu.make_async_remote_copy(
                src_ref=p_ref.at[(me ^ k) >> 1], dst_ref=land_ref.at[me >> 1], send_sem=sends[idx],
                recv_sem=recvs[idx], device_id=(x ^ ((k >> 2) & 1), y ^ ((k >> 1) & 1), c),
                device_id_type=pl.DeviceIdType.MESH).start()
        token[...] = jnp.zeros_like(token)

    hbm = pl.BlockSpec(memory_space=pltpu.HBM)
    sem = pl.BlockSpec(memory_space=pltpu.SEMAPHORE)
    buf = pltpu.HBM(partial.shape, partial.dtype)
    return pl.pallas_call(
        body, name="chip_exchange_start",
        out_shape=(pltpu.SemaphoreType.DMA(()),) * (2 * n) + (buf, buf, _sds((8, LANES), F32)),
        in_specs=(hbm, hbm), out_specs=(sem,) * (2 * n) + (hbm, hbm, pl.BlockSpec(memory_space=pltpu.VMEM)),
        input_output_aliases={0: 2 * n, 1: 2 * n + 1},
        compiler_params=pltpu.CompilerParams(has_side_effects=pltpu.SideEffectType.DATAFLOW_SIDE_EFFECTING))(
            pltpu.with_memory_space_constraint(partial, pltpu.HBM),
            pltpu.with_memory_space_constraint(lax.empty(partial.shape, partial.dtype), pltpu.HBM))


def _chip_exchange_wait(started, after):
    n = len(CHIP_FLIPS)
    sems, (p_thru, land_thru) = started[:2 * n], started[2 * n:2 * n + 2]

    def body(p_ref, land_ref, *rest):
        sends, recvs = rest[:n], rest[n:2 * n]
        x, y, c = lax.axis_index("x"), lax.axis_index("y"), lax.axis_index("c")
        me = 4 * x + 2 * y + c
        for idx, k in enumerate(CHIP_FLIPS):
            copy = pltpu.make_async_remote_copy(
                src_ref=p_ref.at[(me ^ k) >> 1], dst_ref=land_ref.at[(me ^ k) >> 1], send_sem=sends[idx],
                recv_sem=recvs[idx], device_id=(x ^ ((k >> 2) & 1), y ^ ((k >> 1) & 1), c),
                device_id_type=pl.DeviceIdType.MESH)
            copy.wait_send()
            copy.wait_recv()

    hbm = pl.BlockSpec(memory_space=pltpu.HBM)
    sem = pl.BlockSpec(memory_space=pltpu.SEMAPHORE)
    buf = pltpu.HBM(p_thru.shape, p_thru.dtype)
    return pl.pallas_call(
        body, name="chip_exchange_wait", out_shape=(buf, buf),
        in_specs=(hbm, hbm) + (sem,) * (2 * n) + (pl.BlockSpec(memory_space=pl.ANY),) * len(after),
        out_specs=(hbm, hbm), input_output_aliases={0: 0, 1: 1},
        compiler_params=pltpu.CompilerParams(has_side_effects=pltpu.SideEffectType.DATAFLOW_SIDE_EFFECTING))(
            p_thru, land_thru, *sems, *after)


def _gather_slab(slab, after):
    def body(s_ref, after_ref, o_ref, *sems):
        _exchange_ops(["gather_rows"], [s_ref], [o_ref], sems, True, True)

    anyspec = pl.BlockSpec(memory_space=pl.ANY)
    return pl.pallas_call(
        body, name="gather_slab", in_specs=[anyspec, anyspec], out_specs=anyspec,
        out_shape=_sds((N_DEV,) + slab.shape, slab.dtype), scratch_shapes=_exchange_sems(1),
        compiler_params=pltpu.CompilerParams(has_side_effects=True))(slab, after)


def _adamw(w, g, m, v):
    m = ADAM_B1 * m + (1.0 - ADAM_B1) * g
    v = ADAM_B2 * v + (1.0 - ADAM_B2) * (g * g)
    m_hat = m / (1.0 - ADAM_B1 ** ADAM_STEP)
    v_hat = v / (1.0 - ADAM_B2 ** ADAM_STEP)
    delta = -ADAM_LR * (m_hat / (jnp.sqrt(v_hat) + ADAM_EPS) + ADAM_WD * w)
    return delta, m, v


def _sum_adamw(recv, w, m, v, name, after=None):
    lead = w.ndim - 2
    rows, cols = w.shape[-2:]
    tr = 256 if rows % 256 == 0 else 128
    n_slots = recv.shape[0]
    extra = [] if after is None else [after]

    def body(*refs):
        r_ref = refs[0]
        w_ref, m_ref, v_ref, g_ref, d_ref, nm_ref, nv_ref = refs[1 + len(extra):]
        g = None
        for slot in range(n_slots):
            g = r_ref[slot].astype(F32) if g is None else g + r_ref[slot].astype(F32)
        g_ref[...] = g
        d_ref[...], nm_ref[...], nv_ref[...] = _adamw(w_ref[...], g, m_ref[...], v_ref[...])

    blk = pl.BlockSpec((None,) * lead + (tr, cols), lambda i: (0,) * lead + (i, 0))
    slots = pl.BlockSpec((n_slots, tr, cols), lambda i: (0, i, 0))
    return pl.pallas_call(
        body, name=name, grid=(rows // tr,),
        in_specs=[slots] + [pl.BlockSpec(a.shape, lambda i: (0, 0)) for a in extra] + [blk, blk, blk],
        out_specs=[blk] * 4, out_shape=[_sds(w.shape, F32)] * 4, compiler_params=_params())(recv, *extra, w, m, v)


def _entry_view(a):
    _, rows, cols = a.shape
    return jnp.transpose(a, (2, 0, 1)).reshape(cols, rows // LANES, LANES)


def _from_entry_view(a):
    cols, groups, lanes = a.shape
    return jnp.transpose(a, (1, 2, 0)).reshape(1, groups * lanes, cols)


def _sum_adamw_entry_view(landing, own, w, m, v, name):
    n_slots, rows, cols = landing.shape
    groups = rows // LANES
    cols_pad = -(-cols // LANES) * LANES

    def body(r_ref, own_ref, w_ref, m_ref, v_ref, g_ref, d_ref, nm_ref, nv_ref, pad_ref, gt_ref):
        j = pl.program_id(0)
        chip = (4 * lax.axis_index("x") + 2 * lax.axis_index("y") + lax.axis_index("c")) >> 1
        pad_ref[:, cols_pad - LANES:] = jnp.zeros((LANES, LANES), F32)
        for r0 in range(0, LANES, 32):
            g = None
            for slot in range(n_slots):
                part = jnp.where(chip == slot, own_ref[slot, r0:r0 + 32], r_ref[slot, r0:r0 + 32])
                g = part.astype(F32) if g is None else g + part.astype(F32)
            pad_ref[r0:r0 + 32, :cols] = g
        for c0 in range(0, cols_pad, LANES):
            gt_ref[c0:c0 + LANES, :] = pad_ref[:, c0:c0 + LANES].T
        def update_plane(plane):
            for c0 in range(0, cols, 64):
                n = min(64, cols - c0)
                at = pl.ds(c0 * groups + plane, n, stride=groups)
                gt = gt_ref[c0:c0 + n, :]
                g_ref[at, :] = gt
                d_ref[at, :], nm_ref[at, :], nv_ref[at, :] = _adamw(w_ref[at, :], gt, m_ref[at, :], v_ref[at, :])

        for plane in range(groups):
            pl.when(j == plane)(lambda plane=plane: update_plane(plane))

    flat = lambda a: _entry_view(a).reshape(cols * groups, LANES)
    whole = pl.BlockSpec((cols * groups, LANES), lambda j: (0, 0))
    slots = pl.BlockSpec((n_slots, LANES, cols), lambda j: (0, j, 0))
    outs = pl.pallas_call(
        body, name=name, grid=(groups,), in_specs=[slots, slots, whole, whole, whole], out_specs=[whole] * 4,
        out_shape=[_sds((cols * groups, LANES), F32)] * 4,
        scratch_shapes=[pltpu.VMEM((LANES, cols_pad), F32), pltpu.VMEM((cols_pad, LANES), F32)],
        compiler_params=_params())(landing, own, flat(w), flat(m), flat(v))
    return [_from_entry_view(o.reshape(cols, groups, LANES)) for o in outs]


SLAB_ROWS = 16
SLOT = {"kv_norm_g": (8, 0, D), "norm_b_g": (9, 0, D), "b_forget": (10, 0, 16), "qnorm_a_g": (10, 128, HD),
        "knorm_a_g": (10, 256, HD), "knorm_b_g": (10, 384, HD), "qnorm_b_g": (10, 512, HD), "sinks": (10, 640, 16)}
SMALL = ["norm_a_g", "b_forget", "qnorm_a_g", "knorm_a_g", "kv_norm_g", "knorm_b_g", "norm_b_g", "qnorm_b_g", "sinks"]


LOSS_ROW = 11


def _pack_small(dg_a, dg_kv, dg_b, db_f, dgq_a, dgk_a, dgk_b, dgq_b, dsinks, lsum):
    def fold(ref):
        return ref[:, 0:HD] + ref[:, HD:2 * HD]

    def body(dga_ref, dgkv_ref, dgb_ref, dbf_ref, dgqa_ref, dgka_ref, dgkb_ref, dgqb_ref, dsk_ref, ls_ref, slab_ref):
        slab_ref[...] = jnp.zeros_like(slab_ref)
        for r in range(N_DEV):
            slab_ref[r:r + 1, 0:LANES] = dga_ref[:, LANES * r:LANES * (r + 1)]
        slab_ref[8:9, :] = dgkv_ref[...]
        slab_ref[9:10, :] = dgb_ref[...]
        slab_ref[10:11, 0:LANES] = dbf_ref[...]
        slab_ref[10:11, 128:128 + HD] = fold(dgqa_ref)
        slab_ref[10:11, 256:256 + HD] = fold(dgka_ref)
        slab_ref[10:11, 384:384 + HD] = fold(dgkb_ref)
        slab_ref[10:11, 512:512 + HD] = fold(dgqb_ref)
        slab_ref[10:11, 640:640 + LANES] = dsk_ref[...]
        slab_ref[LOSS_ROW:LOSS_ROW + 1, 0:LANES] = ls_ref[...]

    return pl.pallas_call(body, name="pack_small", out_shape=_sds((SLAB_ROWS, D), F32), compiler_params=_params())(
        dg_a, dg_kv, dg_b, db_f, dgq_a, dgk_a, dgk_b, dgq_b, dsinks, lsum)


def _small_adamw(recv, ws, ms, vs):
    k = len(SMALL)

    def body(*refs):
        r_ref = refs[0]
        w_refs, m_refs, v_refs = refs[1:1 + k], refs[1 + k:1 + 2 * k], refs[1 + 2 * k:1 + 3 * k]
        outs = refs[1 + 3 * k:1 + 7 * k]
        loss_ref, tot = refs[1 + 7 * k], refs[2 + 7 * k]
        g = r_ref[0]
        for dev in range(1, N_DEV):
            g = g + r_ref[dev]
        tot[...] = g
        loss_ref[...] = tot[LOSS_ROW:LOSS_ROW + 1, 0:LANES] * (0.5 / D)
        me = 4 * lax.axis_index("x") + 2 * lax.axis_index("y") + lax.axis_index("c")
        for p, name in enumerate(SMALL):
            if name == "norm_a_g":
                mine = lax.broadcasted_iota(jnp.int32, (N_DEV, LANES), 0) == me
                gp = jnp.sum(jnp.where(mine, tot[0:N_DEV, 0:LANES], 0.0), axis=0, keepdims=True)
            else:
                row, lo, width = SLOT[name]
                gp = tot[row:row + 1, lo:lo + width]
            d, nm, nv = _adamw(w_refs[p][...], gp, m_refs[p][...], v_refs[p][...])
            outs[p][...] = gp
            outs[k + p][...] = d
            outs[2 * k + p][...] = nm
            outs[3 * k + p][...] = nv

    shapes = [_sds(w.shape, F32) for w in ws]
    return pl.pallas_call(body, name="small_adamw", out_shape=shapes * 4 + [_sds((1, LANES), F32)],
                          scratch_shapes=[pltpu.VMEM((SLAB_ROWS, D), F32)],
                          compiler_params=_params())(recv, *ws, *ms, *vs)


def _rope_tables(positions):
    inv_freq = jnp.power(jnp.float32(ROPE_THETA), -jnp.arange(0, ROT, 2, dtype=F32) / ROT)
    ang = positions.astype(F32)[:, None] * inv_freq[None, :]
    cos, sin = jnp.cos(ang), jnp.sin(ang)
    c64 = jnp.concatenate([cos, cos, jnp.ones((S, HD - ROT), F32)], axis=-1)
    s64 = jnp.concatenate([-sin, sin, jnp.zeros((S, HD - ROT), F32)], axis=-1)
    return jnp.tile(c64, (1, 2)), jnp.tile(s64, (1, 2))


def _local_step(x, tgt, positions, g_a, wa, b_forget, gq_a, gk_a, g_kv, gk_b, g_b, gq_b, sinks,
                woa_s, wkv_s, wib_s, wob_s, adamw_others):
    nq = S // TQ
    cos2, sin2 = _rope_tables(positions)
    b_pad = jnp.pad(b_forget, ((0, 0), (0, LANES - N_HEADS)))

    u_a, proj, qn, kn, vb, ccol, cbc = _head_a(x, g_a, wa, gq_a, gk_a, b_pad)
    crow = ccol[:, :N_HEADS].T.reshape(N_HEADS, nq, 1, TQ)
    o_a, z_a, lse_a, woa_g, wkv_g, w_in_b, wob_g = _fox_fwd(
        qn, kn, vb, proj, crow, cbc,
        rider=[("gather_rows", woa_s), ("gather_rows", wkv_s), ("gather_cols", wib_s), ("gather_rows", wob_s)])
    w_out_a, w_kv, w_out_b = woa_g.reshape(D, D), wkv_g.reshape(D, 512), wob_g.reshape(D, D)
    h1, u_kv, u_b, kv, pb, qb, ksh, vsh = _head_b(x, z_a, w_out_a, g_kv, g_b, w_kv, w_in_b, gq_b, gk_b, cos2, sin2)
    sinks1 = sinks.reshape(N_HEADS)
    o_b, z_b, lse_b = _swa_fwd(qb, ksh, vsh, pb, sinks1)
    dy, lsum = _out_b_loss(z_b, w_out_b, h1, tgt)
    dw_out_b = _mm(z_b, dy, "tn", 512, 512, S, out_dtype=BF16, name="mm_dw_out_b")
    dz_b = _mm(dy, w_out_b, "nt", 1024, 512, D, name="mm_dz_b")
    dpb, dka, dkb, dva, dvb, dsinks, dgq_b = _swa_bwd(qb, ksh, vsh, dz_b, o_b, lse_b, pb, sinks1, gq_b, cos2, sin2)
    dkv, dgk_b = _prep_kv_bwd(dka, dkb, dva, dvb, kv, gk_b, cos2, sin2)
    dw_in_b = _mm(u_b, dpb, "tn", 512, 512, S, out_dtype=BF16, name="mm_dw_in_b")
    dw_kv = _mm(u_kv, dkv, "tn", 512, 512, S, out_dtype=BF16, name="mm_dw_kv")
    dh1, dg_b, dg_kv = _du_b_rms_bwd(dpb, w_in_b, dkv, w_kv, h1, g_b, g_kv, dy)
    dw_out_a = _mm(z_a, dh1, "tn", 512, 512, S, out_dtype=BF16, name="mm_dw_out_a")
    do_a, dgate_a, delta_a = _fox_bwd_pre(dh1, w_out_a, proj, o_a)
    dk_a, dv_a, dcs, dq_a, drow, r_wob, r_wib, r_wkv, r_woa = _fox_bwd(
        qn, kn, vb, do_a, lse_a, delta_a, crow, cbc,
        rider=[("a2a_rows", dw_out_b), ("a2a_cols", dw_in_b), ("a2a_rows", dw_kv), ("a2a_rows", dw_out_a)])
    dproj, dgq_a, dgk_a, db_f = _prep_a_bwd(dq_a, dk_a, dv_a, dgate_a, drow, dcs, proj, b_pad, gq_a, gk_a)
    dwa = _mm(u_a, dproj, "tn", 1024, 256, S, out_dtype=BF16, name="mm_dw_in_a")
    partial = _reshard_pair_reduce(dwa)
    started = _chip_exchange_start(partial)
    dx, dg_a = _du_a_rms_bwd(dproj, wa, x, g_a, dh1, after=started[-1])
    others = adamw_others(dict(w_out_a=r_woa, w_kv=r_wkv, w_in_b=r_wib, w_out_b=r_wob), dg_a)
    partial, landed = _chip_exchange_wait(started, [res[0] for res in others.values()])
    slab = _pack_small(dg_a, dg_kv, dg_b, db_f, dgq_a, dgk_a, dgk_b, dgq_b, dsinks, lsum)
    return dx, (landed, partial), others, _gather_slab(slab, landed)


def kernel(x, positions, norm_a_g, w_in_a, b_forget, qnorm_a_g, knorm_a_g, w_out_a, kv_norm_g, w_kv, knorm_b_g, norm_b_g, w_in_b, qnorm_b_g, sinks, w_out_b, loss_target, m_norm_a_g, m_w_in_a, m_b_forget, m_qnorm_a_g, m_knorm_a_g, m_w_out_a, m_kv_norm_g, m_w_kv, m_knorm_b_g, m_norm_b_g, m_w_in_b, m_qnorm_b_g, m_sinks, m_w_out_b, v_norm_a_g, v_w_in_a, v_b_forget, v_qnorm_a_g, v_knorm_a_g, v_w_out_a, v_kv_norm_g, v_w_kv, v_knorm_b_g, v_norm_b_g, v_w_in_b, v_qnorm_b_g, v_sinks, v_w_out_b):
    wa_g, ga_g, woa_s, wkv_s, wib_s, wob_s = _gather_first(w_in_a, w_out_a, w_kv, w_in_b, w_out_b, norm_a_g)
    state = dict(w_in_a=(w_in_a, m_w_in_a, v_w_in_a), w_out_a=(w_out_a, m_w_out_a, v_w_out_a),
                 w_kv=(w_kv, m_w_kv, v_w_kv), w_in_b=(w_in_b, m_w_in_b, v_w_in_b),
                 w_out_b=(w_out_b, m_w_out_b, v_w_out_b))

    def adamw_others(landed, after):
        return {n: _sum_adamw(r, *state[n], "adamw_" + n, after=after) for n, r in landed.items()}

    dx, r_wa, big, slab_g = _local_step(
        x[0], loss_target[0], positions, ga_g.reshape(1, D), _unshard_wa(wa_g), b_forget, qnorm_a_g, knorm_a_g,
        kv_norm_g.reshape(1, D), knorm_b_g.reshape(1, HD), norm_b_g, qnorm_b_g, sinks, woa_s, wkv_s, wib_s, wob_s,
        adamw_others)
    big["w_in_a"] = _sum_adamw_entry_view(*r_wa, *state["w_in_a"], "adamw_w_in_a")

    r2 = lambda a: a.reshape(1, -1)
    small_w = dict(norm_a_g=norm_a_g, b_forget=b_forget, qnorm_a_g=qnorm_a_g, knorm_a_g=knorm_a_g,
                   kv_norm_g=kv_norm_g, knorm_b_g=knorm_b_g, norm_b_g=norm_b_g, qnorm_b_g=qnorm_b_g, sinks=sinks)
    small_m = dict(norm_a_g=m_norm_a_g, b_forget=m_b_forget, qnorm_a_g=m_qnorm_a_g, knorm_a_g=m_knorm_a_g,
                   kv_norm_g=m_kv_norm_g, knorm_b_g=m_knorm_b_g, norm_b_g=m_norm_b_g, qnorm_b_g=m_qnorm_b_g,
                   sinks=m_sinks)
    small_v = dict(norm_a_g=v_norm_a_g, b_forget=v_b_forget, qnorm_a_g=v_qnorm_a_g, knorm_a_g=v_knorm_a_g,
                   kv_norm_g=v_kv_norm_g, knorm_b_g=v_knorm_b_g, norm_b_g=v_norm_b_g, qnorm_b_g=v_qnorm_b_g,
                   sinks=v_sinks)
    res = _small_adamw(slab_g, [r2(small_w[n]) for n in SMALL], [r2(small_m[n]) for n in SMALL],
                       [r2(small_v[n]) for n in SMALL])
    k = len(SMALL)
    small = {n: [res[q * k + p].reshape(small_w[n].shape) for q in range(4)] for p, n in enumerate(SMALL)}
    loss = res[4 * k][0, 0]

    order = ["norm_a_g", "w_in_a", "b_forget", "qnorm_a_g", "knorm_a_g", "w_out_a", "kv_norm_g", "w_kv",
             "knorm_b_g", "norm_b_g", "w_in_b", "qnorm_b_g", "sinks", "w_out_b"]

    def leaf(n, q):
        return big[n][q] if n in big else small[n][q]

    outs = [loss, dx[None]]
    for q in range(4):
        outs.extend(leaf(n, q) for n in order)
    return tuple(outs)
```

```python
import jax
import jax.numpy as jnp
from jax import lax
from jax.experimental import pallas as pl
from jax.experimental.pallas import tpu as pltpu

F32, BF16 = jnp.float32, jnp.bfloat16

S = 2048
D = 1024
HD = 64
N_HEADS = 16
N_DEV = 8
NA = 4352
GOFF = 3072
FOFF = 4096
RAW_F = 3072
RAW_G = RAW_F + N_HEADS
NA_RAW = 4112
EPS = 1e-6
QSCALE = 0.125
ROPE_THETA = 500000.0
ROT = 16
WIN = 128
TQ = 256
TK = 256
KS = TQ // 2
HPS = 8
HW = HPS * HD
TM = 256
RT = 256
RB = 512
CB = 256
LANES = 128

ADAM_LR, ADAM_B1, ADAM_B2, ADAM_EPS, ADAM_WD, ADAM_STEP = 0.001, 0.9, 0.999, 1e-08, 0.01, 10

VMEM_LIMIT = 56 * 1024 * 1024


def _params():
    return pltpu.CompilerParams(vmem_limit_bytes=VMEM_LIMIT)


def _sds(shape, dtype):
    return jax.ShapeDtypeStruct(shape, dtype)


def _dot_nt(a, b):
    return lax.dot_general(a, b, (((1,), (1,)), ((), ())), preferred_element_type=F32)


def _dot_tn(a, b):
    return lax.dot_general(a, b, (((0,), (0,)), ((), ())), preferred_element_type=F32)


def _dot_nn(a, b):
    return lax.dot_general(a, b, (((1,), (0,)), ((), ())), preferred_element_type=F32)


def _sigmoid(g):
    return 1.0 / (1.0 + jnp.exp(-g))


def _lane_iota(shape):
    return lax.broadcasted_iota(jnp.int32, shape, len(shape) - 1)


def _flips(kind):
    return (2, 4, 6) if kind == "a2a_chips" else tuple(range(1, N_DEV))


def _send_view(kind, ref, dev):
    if kind in ("gather_rows", "gather_cols"):
        return ref
    if kind == "a2a_slots":
        return ref.at[dev]
    if kind == "a2a_chips":
        return ref.at[dev >> 1]
    if kind == "a2a_rows":
        rows = ref.shape[0] // N_DEV
        return ref.at[pl.ds(pl.multiple_of(dev * rows, rows), rows)]
    cols = ref.shape[1] // N_DEV
    return ref.at[:, pl.ds(pl.multiple_of(dev * cols, cols), cols)]


def _land_view(kind, ref, dev):
    if kind == "gather_cols":
        cols = ref.shape[1] // N_DEV
        return ref.at[:, pl.ds(pl.multiple_of(dev * cols, cols), cols)]
    if kind == "a2a_chips":
        return ref.at[dev >> 1]
    return ref.at[dev]


def _landing_sds(kind, arr):
    if kind == "gather_rows":
        return _sds((N_DEV,) + arr.shape, arr.dtype)
    if kind == "gather_cols":
        return _sds((arr.shape[0], N_DEV * arr.shape[1]), arr.dtype)
    if kind == "a2a_rows":
        return _sds((N_DEV, arr.shape[0] // N_DEV, arr.shape[1]), arr.dtype)
    if kind == "a2a_cols":
        return _sds((N_DEV, arr.shape[0], arr.shape[1] // N_DEV), arr.dtype)
    return _sds(arr.shape, arr.dtype)


def _exchange_sems(n_parts):
    n = n_parts * (N_DEV - 1)
    return [pltpu.SemaphoreType.DMA((n,)), pltpu.SemaphoreType.DMA((n,)), pltpu.SemaphoreType.DMA((n_parts,))]


def _exchange_ops(kinds, srcs, dsts, sems, start, wait):
    send_sems, recv_sems, local_sems = sems
    x, y, c = lax.axis_index("x"), lax.axis_index("y"), lax.axis_index("c")
    me = 4 * x + 2 * y + c

    def local(a):
        return pltpu.make_async_copy(_send_view(kinds[a], srcs[a], me), _land_view(kinds[a], dsts[a], me),
                                     local_sems.at[a])

    def remote(a, k, landing_dev):
        peer = (x ^ ((k >> 2) & 1), y ^ ((k >> 1) & 1), c ^ (k & 1))
        sem = a * (N_DEV - 1) + k - 1
        return pltpu.make_async_remote_copy(
            src_ref=_send_view(kinds[a], srcs[a], me ^ k), dst_ref=_land_view(kinds[a], dsts[a], landing_dev),
            send_sem=send_sems.at[sem], recv_sem=recv_sems.at[sem], device_id=peer,
            device_id_type=pl.DeviceIdType.MESH)

    pairs = [(a, k) for k in range(1, N_DEV) for a in range(len(kinds)) if k in _flips(kinds[a])]
    if start:
        for a in range(len(kinds)):
            local(a).start()
        for a, k in pairs:
            remote(a, k, me).start()
    if wait:
        for a, k in pairs:
            remote(a, k, me ^ k).wait_recv()
            remote(a, k, me).wait_send()
        for a in range(len(kinds)):
            local(a).wait()


def _gather_two_level(srcs, landing_of, sems, meanwhile=None):
    send_sems, recv_sems, local_sems = sems
    x, y, c = lax.axis_index("x"), lax.axis_index("y"), lax.axis_index("c")
    me, sibling = (x, y, c), (x, y, 1 - c)
    chips = [(1 - x, y), (x, 1 - y), (1 - x, 1 - y)]

    def slot(a, dev):
        return landing_of[a](4 * dev[0] + 2 * dev[1] + dev[2])

    def copy(a, k, block, to, src=None):
        return pltpu.make_async_remote_copy(
            src_ref=slot(a, block) if src is None else src, dst_ref=slot(a, block),
            send_sem=send_sems.at[a * (N_DEV - 1) + k], recv_sem=recv_sems.at[a * (N_DEV - 1) + k],
            device_id=to, device_id_type=pl.DeviceIdType.MESH)

    parts = range(len(srcs))
    mine = [pltpu.make_async_copy(srcs[a], slot(a, me), local_sems.at[a]) for a in parts]
    first = [copy(a, 1 + j, me, (*chip, c), src=srcs[a]) for a in parts for j, chip in enumerate(chips)]
    first += [copy(a, 0, me, sibling, src=srcs[a]) for a in parts]
    for cp in first + mine:
        cp.start()
    if meanwhile is not None:
        meanwhile()
    passed = []
    for a in parts:
        for j, chip in enumerate(chips):
            copy(a, 1 + j, (*chip, c), me).wait_recv()
            fwd = copy(a, 4 + j, (*chip, c), sibling)
            fwd.start()
            passed.append(fwd)
    for a in parts:
        copy(a, 0, sibling, me).wait_recv()
        for j, chip in enumerate(chips):
            copy(a, 4 + j, (*chip, 1 - c), me).wait_recv()
    for cp in first + passed:
        cp.wait_send()
    for cp in mine:
        cp.wait()


def _call(body, *, name, args, in_specs, out_specs, out_shape, grid=(), scratch_shapes=(), aliases=None, rider=()):
    n_in, n_out, n_scr, n_r = len(in_specs), len(out_specs), len(scratch_shapes), len(rider)
    kinds = [kind for kind, _ in rider]

    def kernel_body(*refs):
        c_in, r_in = refs[:n_in], refs[n_in:n_in + n_r]
        c_out = refs[n_in + n_r:n_in + n_r + n_out]
        r_out = refs[n_in + n_r + n_out:n_in + 2 * n_r + n_out]
        rest = refs[n_in + 2 * n_r + n_out:]
        c_scr, sems = rest[:n_scr], rest[n_scr:]
        if n_r:
            assert grid, "a rider needs a gridded call"
            ids = [pl.program_id(ax) for ax in range(len(grid))]
            first, last = ids[0] == 0, ids[0] == grid[0] - 1
            for pid, size in zip(ids[1:], grid[1:]):
                first = first & (pid == 0)
                last = last & (pid == size - 1)
            pl.when(first)(lambda: _exchange_ops(kinds, r_in, r_out, sems, True, False))
        body(*c_in, *c_out, *c_scr)
        if n_r:
            pl.when(last)(lambda: _exchange_ops(kinds, r_in, r_out, sems, False, True))

    anyspec = pl.BlockSpec(memory_space=pl.ANY)
    params = pltpu.CompilerParams(vmem_limit_bytes=VMEM_LIMIT, has_side_effects=bool(n_r))
    outs = pl.pallas_call(
        kernel_body, name=name, grid=grid, in_specs=list(in_specs) + [anyspec] * n_r,
        out_specs=list(out_specs) + [anyspec] * n_r,
        out_shape=list(out_shape) + [_landing_sds(kind, arr) for kind, arr in rider],
        scratch_shapes=list(scratch_shapes) + (_exchange_sems(n_r) if n_r else []),
        input_output_aliases=aliases or {}, compiler_params=params)(*args, *[arr for _, arr in rider])
    return list(outs)


def _mm(a, b, mode, tm, tn, tk, out_dtype=F32, add=None, name="mm", rider=()):
    if mode == "nn":
        (m, k), n = a.shape, b.shape[1]
        a_spec = pl.BlockSpec((tm, tk), lambda i, j, kk: (i, kk))
        b_spec = pl.BlockSpec((tk, tn), lambda i, j, kk: (kk, j))
        dot = _dot_nn
    elif mode == "nt":
        (m, k), n = a.shape, b.shape[0]
        a_spec = pl.BlockSpec((tm, tk), lambda i, j, kk: (i, kk))
        b_spec = pl.BlockSpec((tn, tk), lambda i, j, kk: (j, kk))
        dot = _dot_nt
    else:
        (k, m), n = a.shape, b.shape[1]
        a_spec = pl.BlockSpec((tk, tm), lambda i, j, kk: (kk, i))
        b_spec = pl.BlockSpec((tk, tn), lambda i, j, kk: (kk, j))
        dot = _dot_tn
    assert m % tm == 0 and n % tn == 0 and k % tk == 0, (m, n, k, tm, tn, tk)
    nk = k // tk
    has_add = add is not None

    def body(*refs):
        if has_add:
            a_ref, b_ref, add_ref, o_ref, acc = refs
        else:
            a_ref, b_ref, o_ref, acc = refs
        p = dot(a_ref[...].astype(BF16), b_ref[...].astype(BF16))

        def finish(total):
            if has_add:
                total = add_ref[...] + total
            o_ref[...] = total.astype(out_dtype)

        if nk == 1:
            finish(p)
        else:
            kk = pl.program_id(2)

            @pl.when(kk == 0)
            def _():
                acc[...] = p

            @pl.when(kk > 0)
            def _():
                acc[...] += p

            @pl.when(kk == nk - 1)
            def _():
                finish(acc[...])

    in_specs = [a_spec, b_spec]
    args = [a, b]
    if has_add:
        in_specs.append(pl.BlockSpec((tm, tn), lambda i, j, kk: (i, j)))
        args.append(add)
    acc_shape = (tm, tn) if nk > 1 else (8, LANES)
    outs = _call(body, name=name, args=args, grid=(m // tm, n // tn, nk), in_specs=in_specs,
                 out_specs=[pl.BlockSpec((tm, tn), lambda i, j, kk: (i, j))], out_shape=[_sds((m, n), out_dtype)],
                 scratch_shapes=[pltpu.VMEM(acc_shape, F32)], rider=rider)
    return outs if rider else outs[0]


def _rms_rinv(x):
    return lax.rsqrt(jnp.mean(x * x, axis=-1, keepdims=True) + EPS)


def _rms_bwd_core(du, x, g):
    r = _rms_rinv(x)
    dug = du * g
    dx = r * (dug - x * ((r * r) * jnp.mean(dug * x, axis=-1, keepdims=True)))
    dg = jnp.sum(du * (x * r), axis=0, keepdims=True)
    return dx, dg


def _half_ones():
    r = lax.broadcasted_iota(jnp.int32, (LANES, LANES), 0)
    c = lax.broadcasted_iota(jnp.int32, (LANES, LANES), 1)
    return ((r < HD) == (c < HD)).astype(BF16)


def _half_sum(v, lo_half):
    if lo_half.dtype == jnp.bool_:
        s0 = jnp.sum(jnp.where(lo_half, v, 0.0), axis=-1, keepdims=True)
        s1 = jnp.sum(jnp.where(lo_half, 0.0, v), axis=-1, keepdims=True)
        return jnp.where(lo_half, s0, s1)
    hi = v.astype(BF16)
    lo = (v - hi.astype(F32)).astype(BF16)
    return _dot_nn(hi, lo_half) + _dot_nn(lo, lo_half)


def _head_rinv(x, lo_half):
    return lax.rsqrt(_half_sum(x * x, lo_half) * (1.0 / HD) + EPS)


def _head_norm_bwd(dn, x, g, lo_half):
    r = _head_rinv(x, lo_half)
    dng = dn * g
    dx = r * (dng - x * ((r * r) * (_half_sum(dng * x, lo_half) * (1.0 / HD))))
    dg = jnp.sum(dn * (x * r), axis=0, keepdims=True)
    return dx, dg


def _rope_swap(x, lane):
    l64 = lane & (HD - 1)
    return jnp.where(l64 < ROT // 2, pltpu.roll(x, LANES - ROT // 2, 1), pltpu.roll(x, ROT // 2, 1))


def _rope_fwd(x, cos, sin, lane):
    return x * cos + _rope_swap(x, lane) * sin


def _rope_bwd(dy, cos, sin, lane):
    return dy * cos + jnp.where((lane & (HD - 1)) < ROT, _rope_swap(dy * sin, lane), 0.0)


def _g2(g_ref):
    g = g_ref[...]
    return jnp.concatenate([g, g], axis=-1)


def _pairs(width):
    return [slice(LANES * c, LANES * (c + 1)) for c in range(width // LANES)]


def _pick_lane(block, lane, idx):
    return jnp.sum(jnp.where(lane == idx, block, 0.0), axis=-1, keepdims=True)


def _head_a(x, g, wa, gq, gk, b_pad):
    def body(x_ref, g_ref, w_ref, gq_ref, gk_ref, b_ref, u_ref, p_ref, qo_ref, ko_ref, vo_ref, c_ref, cbc_ref, carry):
        @pl.when(pl.program_id(0) == 0)
        def _():
            carry[...] = jnp.zeros_like(carry)

        xv = x_ref[...]
        u = ((xv * _rms_rinv(xv)) * g_ref[...]).astype(BF16)
        u_ref[...] = u
        for lo in range(0, NA, D):
            hi = min(lo + D, NA)
            p_ref[:, lo:hi] = _dot_nn(u, w_ref[:, lo:hi])
        lane = _lane_iota((RT, LANES))
        lo_half = lane < HD
        gq2, gk2 = _g2(gq_ref), _g2(gk_ref)
        for c in _pairs(D):
            q = p_ref[:, c]
            k = p_ref[:, D + c.start:D + c.stop]
            qo_ref[:, c] = (((q * _head_rinv(q, lo_half)) * gq2) * QSCALE).astype(BF16)
            ko_ref[:, c] = ((k * _head_rinv(k, lo_half)) * gk2).astype(BF16)
        vo_ref[...] = p_ref[:, 2 * D:3 * D].astype(BF16)

        z = p_ref[:, FOFF:FOFF + LANES] + b_ref[...]
        logf = jnp.minimum(z, 0.0) - jnp.log1p(jnp.exp(-jnp.abs(z)))
        r = lax.broadcasted_iota(jnp.int32, (RT, RT), 0)
        cc = lax.broadcasted_iota(jnp.int32, (RT, RT), 1)
        tri = (r >= cc).astype(F32)
        loc = jnp.dot(tri, logf, precision=lax.Precision.HIGHEST, preferred_element_type=F32) + carry[0:1, :]
        c_ref[...] = loc
        carry[0:1, :] = loc[RT - 1:RT, :]
        for h in range(N_HEADS):
            cbc_ref[:, LANES * h:LANES * (h + 1)] = jnp.broadcast_to(_pick_lane(loc, lane, h), (RT, LANES))

    row = lambda width: pl.BlockSpec((RT, width), lambda i: (i, 0))
    whole = lambda arr: pl.BlockSpec(arr.shape, lambda i: (0,) * arr.ndim, pipeline_mode=pl.Buffered(1))
    return pl.pallas_call(
        body, name="head_a", grid=(S // RT,),
        in_specs=[row(D), whole(g), whole(wa), whole(gq), whole(gk), whole(b_pad)],
        out_specs=[row(D), row(NA), row(D), row(D), row(D), row(LANES), row(N_HEADS * LANES)],
        out_shape=[_sds((S, D), BF16), _sds((S, NA), F32)] + [_sds((S, D), BF16)] * 3
        + [_sds((S, LANES), F32), _sds((S, N_HEADS * LANES), F32)],
        scratch_shapes=[pltpu.VMEM((8, LANES), F32)], compiler_params=_params())(x, g, wa, gq, gk, b_pad)


def _key_le_query(offset, keys=TK):
    r = lax.broadcasted_iota(jnp.int32, (keys, TQ), 0)
    c = lax.broadcasted_iota(jnp.int32, (keys, TQ), 1)
    return (r + offset) <= c


def _widen(tile):
    return jnp.concatenate([tile] * (TQ // LANES), axis=1)


def _fox_fwd(qn, kn, vb, proj, crow, cbc, rider=()):
    nq = S // TQ

    def body(q_ref, k_ref, v_ref, g_ref, cq_ref, cbc_ref, o_ref, z_ref, lse_ref, st_s, pt_s):
        i = pl.program_id(1)
        qs = [q_ref[:, HD * hh:HD * (hh + 1)] for hh in range(HPS)]
        cqs = [cq_ref[hh, 0] for hh in range(HPS)]

        def scores(s, hh):
            off = pl.multiple_of(s * KS, KS)
            kj = k_ref[pl.ds(off, KS), HD * hh:HD * (hh + 1)]
            return (_dot_nt(kj, qs[hh]) + cqs[hh]) - _widen(cbc_ref[pl.ds(off, KS), LANES * hh:LANES * (hh + 1)])

        def values(s, hh, pt):
            off = pl.multiple_of(s * KS, KS)
            return _dot_tn(v_ref[pl.ds(off, KS), HD * hh:HD * (hh + 1)], pt)

        def step(s, slot, carries, mask=None, last=False):
            if not last:
                for hh in range(HPS):
                    st_s[1 - slot, hh] = scores(s + 1, hh)
            pvs = [values(jnp.maximum(s - 1, 0), hh, pt_s[1 - slot, hh]) for hh in range(HPS)]
            out = []
            for hh in range(HPS):
                m, l, acc = carries[hh]
                st = st_s[slot, hh]
                if mask is not None:
                    st = jnp.where(mask, st, -jnp.inf)
                m_new = jnp.maximum(m, jnp.max(st, axis=0, keepdims=True))
                pt = jnp.exp(st - m_new)
                alpha = jnp.exp(m - m_new)
                pt_s[slot, hh] = pt.astype(BF16)
                out.append((m_new, alpha * l + jnp.sum(pt, axis=0, keepdims=True), alpha * (acc + pvs[hh])))
            return tuple(out)

        for hh in range(HPS):
            st_s[0, hh] = scores(0, hh)
            pt_s[1, hh] = jnp.zeros((KS, TQ), BF16)
        one = (jnp.full((1, TQ), -jnp.inf, F32), jnp.zeros((1, TQ), F32), jnp.zeros((HD, TQ), F32))
        carries = lax.fori_loop(0, i, lambda t, cr: step(2 * t + 1, 1, step(2 * t, 0, cr)), (one,) * HPS)
        carries = step(2 * i, 0, carries, mask=_key_le_query(0, KS))
        carries = step(2 * i + 1, 1, carries, mask=_key_le_query(KS, KS), last=True)
        accs = []
        for hh in range(HPS):
            m, l, acc = carries[hh]
            acc = acc + values(2 * i + 1, hh, pt_s[1, hh])
            accs.append(acc / l)
            lse_ref[hh, 0] = m + jnp.log(l)
        o = jnp.concatenate(accs, axis=0).T
        o_ref[...] = o
        g = g_ref[...]
        z_ref[...] = (o * (g * _sigmoid(g))).astype(BF16)

    qblk = pl.BlockSpec((TQ, HW), lambda hp, i: (i, hp))
    full = pl.BlockSpec((S, HW), lambda hp, i: (0, hp))
    rows = pl.BlockSpec((HPS, 1, 1, TQ), lambda hp, i: (hp, i, 0, 0))
    return _call(
        body, name="fox_fwd", args=(qn, kn, vb, proj, crow, cbc), grid=(N_HEADS // HPS, nq),
        in_specs=[qblk, full, full,
                  pl.BlockSpec((TQ, HW), lambda hp, i: (i, GOFF // HW + hp)),
                  rows, pl.BlockSpec((S, HPS * LANES), lambda hp, i: (0, hp))],
        out_specs=[qblk, qblk, rows],
        out_shape=[_sds((S, D), F32), _sds((S, D), BF16), _sds((N_HEADS, nq, 1, TQ), F32)],
        scratch_shapes=[pltpu.VMEM((2, HPS, KS, TQ), F32), pltpu.VMEM((2, HPS, KS, TQ), BF16)], rider=rider)


def _fox_bwd_pre(dh, w_out, proj, o):
    nq, rows = S // TQ, 2 * TQ
    per = rows // TQ

    def body(dh_ref, w_ref, g_ref, o_ref, do_ref, dg_ref, delta_ref):
        g = g_ref[...]
        sg = _sigmoid(g)
        dzv = _dot_nt(dh_ref[...].astype(BF16), w_ref[...])
        ov = o_ref[...]
        do = dzv * (g * sg)
        dg_ref[...] = (dzv * ov * (sg * (1.0 + g * (1.0 - sg)))).astype(BF16)
        do_ref[...] = do.astype(BF16)
        prod_t = (do * ov).T
        for h in range(N_HEADS):
            for b in range(per):
                delta_ref[h, b] = jnp.sum(prod_t[HD * h:HD * (h + 1), TQ * b:TQ * (b + 1)], axis=0, keepdims=True)

    row = pl.BlockSpec((rows, D), lambda i: (i, 0))
    return pl.pallas_call(
        body, name="fox_bwd_pre", grid=(S // rows,),
        in_specs=[row, pl.BlockSpec(w_out.shape, lambda i: (0, 0), pipeline_mode=pl.Buffered(1)),
                  pl.BlockSpec((rows, D), lambda i: (i, GOFF // D)), row],
        out_specs=[row, row, pl.BlockSpec((N_HEADS, per, 1, TQ), lambda i: (0, i, 0, 0))],
        out_shape=[_sds((S, D), BF16), _sds((S, D), BF16), _sds((N_HEADS, nq, 1, TQ), F32)],
        compiler_params=_params())(dh, w_out, proj, o)


def _fox_bwd(qn, kn, vb, dob, lse, delta, crow, cbc, rider=()):
    nq, nkb = S // TQ, S // TK

    def body(q_ref, k_ref, v_ref, do_ref, lse_ref, del_ref, cq_ref, cbc_ref,
             dk_ref, dv_ref, dcs_ref, dq_ref, dr_ref, st_s, dp_s, pt_s, ds_s, dq_acc, dr_acc):
        j = pl.program_id(1)

        @pl.when(j == 0)
        def _():
            dq_acc[...] = jnp.zeros_like(dq_acc)
            dr_acc[...] = jnp.zeros_like(dr_acc)

        kjs = [k_ref[:, HD * hh:HD * (hh + 1)] for hh in range(HPS)]
        vjs = [v_ref[:, HD * hh:HD * (hh + 1)] for hh in range(HPS)]

        def rows_of(ref, u, hh):
            off = pl.multiple_of(u * TQ, TQ)
            return ref[pl.ds(off, TQ), HD * hh:HD * (hh + 1)]

        def products(u, hh):
            st = (_dot_nt(kjs[hh], rows_of(q_ref, u, hh)) + cq_ref[hh, u]) - _widen(
                cbc_ref[:, LANES * hh:LANES * (hh + 1)])
            return st, _dot_nt(vjs[hh], rows_of(do_ref, u, hh))

        def step(u, slot, carries, masked=False):
            nxt = jnp.minimum(u + 1, nq - 1)
            for hh in range(HPS):
                st_s[1 - slot, hh], dp_s[1 - slot, hh] = products(nxt, hh)
            prev = jnp.maximum(u - 1, 0)
            dvs = [_dot_nn(pt_s[1 - slot, hh], rows_of(do_ref, prev, hh)) for hh in range(HPS)]
            dks = [_dot_nn(ds_s[1 - slot, hh], rows_of(q_ref, prev, hh)) for hh in range(HPS)]
            for hh in range(HPS):
                dq_acc[hh, prev] += _dot_tn(kjs[hh], ds_s[1 - slot, hh])
            out = []
            for hh in range(HPS):
                dk, dv, dcs = carries[hh]
                st = st_s[slot, hh]
                if masked:
                    st = jnp.where(_key_le_query((j - u) * TQ), st, -jnp.inf)
                pt = jnp.exp(st - lse_ref[hh, u])
                dst = pt * (dp_s[slot, hh] - del_ref[hh, u])
                pt_s[slot, hh] = pt.astype(BF16)
                ds_s[slot, hh] = dst.astype(BF16)
                dr_acc[hh, u] += jnp.sum(dst, axis=0, keepdims=True)
                out.append((dk + dks[hh], dv + dvs[hh], dcs + (dst[:, :LANES] + dst[:, LANES:])))
            return tuple(out)

        t0 = j // 2
        for hh in range(HPS):
            st_s[0, hh], dp_s[0, hh] = products(2 * t0, hh)
            pt_s[1, hh] = jnp.zeros((TK, TQ), BF16)
            ds_s[1, hh] = jnp.zeros((TK, TQ), BF16)
        one = (jnp.zeros((TK, HD), F32), jnp.zeros((TK, HD), F32), jnp.zeros((TK, LANES), F32))
        carries = step(2 * t0 + 1, 1, step(2 * t0, 0, (one,) * HPS, masked=True), masked=True)
        carries = lax.fori_loop(t0 + 1, nq // 2, lambda t, cr: step(2 * t + 1, 1, step(2 * t, 0, cr)), carries)
        dks, dvs = [], []
        lane = _lane_iota((TK, LANES))
        dcs_all = jnp.zeros((TK, LANES), F32)
        for hh in range(HPS):
            dk, dv, dcs = carries[hh]
            dks.append(dk + _dot_nn(ds_s[1, hh], rows_of(q_ref, nq - 1, hh)))
            dvs.append(dv + _dot_nn(pt_s[1, hh], rows_of(do_ref, nq - 1, hh)))
            dq_acc[hh, nq - 1] += _dot_tn(kjs[hh], ds_s[1, hh])
            dcs_all = jnp.where(lane == HPS * pl.program_id(0) + hh, -jnp.sum(dcs, axis=-1, keepdims=True), dcs_all)
        dcs_ref[0] = dcs_all
        dk_ref[...] = jnp.concatenate(dks, axis=-1)
        dv_ref[...] = jnp.concatenate(dvs, axis=-1).astype(BF16)

        @pl.when(j == nkb - 1)
        def _():
            for i in range(nq):
                dq_ref[TQ * i:TQ * (i + 1), :] = jnp.concatenate([dq_acc[hh, i] for hh in range(HPS)], axis=0).T
            dr_ref[...] = dr_acc[...]

    kblk = pl.BlockSpec((TK, HW), lambda hp, j: (j, hp))
    full = pl.BlockSpec((S, HW), lambda hp, j: (0, hp))
    rows = pl.BlockSpec((HPS, nq, 1, TQ), lambda hp, j: (hp, 0, 0, 0))
    cblk = pl.BlockSpec((TK, HPS * LANES), lambda hp, j: (j, hp))
    return _call(
        body, name="fox_bwd", args=(qn, kn, vb, dob, lse, delta, crow, cbc), grid=(N_HEADS // HPS, nkb),
        in_specs=[full, kblk, kblk, full, rows, rows, rows, cblk],
        out_specs=[kblk, kblk, pl.BlockSpec((1, TK, LANES), lambda hp, j: (hp, j, 0)), full, rows],
        out_shape=[_sds((S, D), F32), _sds((S, D), BF16), _sds((N_HEADS // HPS, S, LANES), F32), _sds((S, D), F32),
                   _sds((N_HEADS, nq, 1, TQ), F32)],
        scratch_shapes=[pltpu.VMEM((2, HPS, TK, TQ), F32), pltpu.VMEM((2, HPS, TK, TQ), F32),
                        pltpu.VMEM((2, HPS, TK, TQ), BF16), pltpu.VMEM((2, HPS, TK, TQ), BF16),
                        pltpu.VMEM((HPS, nq, HD, TQ), F32), pltpu.VMEM((HPS, nq, 1, TQ), F32)], rider=rider)


def _prep_a_bwd(dq, dk, dv, dgate, drow, dcs, proj, b_pad, gq, gk):
    nt = S // TM

    def body(dq_ref, dk_ref, dv_ref, dgt_ref, dr_ref, dcs_ref, xq_ref, xk_ref, f_ref, b_ref, gq_ref, gk_ref,
             o_ref, dgq_ref, dgk_ref, db_ref, carry):
        @pl.when(pl.program_id(0) == 0)
        def _():
            carry[...] = jnp.zeros_like(carry)
            dgq_ref[...] = jnp.zeros_like(dgq_ref)
            dgk_ref[...] = jnp.zeros_like(dgk_ref)
            db_ref[...] = jnp.zeros_like(db_ref)

        lane = _lane_iota((TM, LANES))
        lo_half = _half_ones()
        gq2, gk2 = _g2(gq_ref), _g2(gk_ref)
        dgq, dgk = jnp.zeros((1, LANES), F32), jnp.zeros((1, LANES), F32)
        for c in _pairs(D):
            dxq, dg = _head_norm_bwd(dq_ref[:, c] * QSCALE, xq_ref[:, c], gq2, lo_half)
            o_ref[:, c] = dxq.astype(BF16)
            dgq = dgq + dg
            dxk, dg = _head_norm_bwd(dk_ref[:, c], xk_ref[:, c], gk2, lo_half)
            o_ref[:, D + c.start:D + c.stop] = dxk.astype(BF16)
            dgk = dgk + dg
        dgq_ref[...] += dgq
        dgk_ref[...] += dgk
        o_ref[:, 2 * D:3 * D] = dv_ref[...]
        o_ref[:, GOFF:GOFF + D] = dgt_ref[...]

        dc = jnp.concatenate([dr_ref[:, 0, :], jnp.zeros((LANES - N_HEADS, TM), F32)], axis=0).T
        for group in range(N_HEADS // HPS):
            dc = dc + dcs_ref[group]
        r = lax.broadcasted_iota(jnp.int32, (TM, TM), 0)
        c = lax.broadcasted_iota(jnp.int32, (TM, TM), 1)
        tri = (c >= r).astype(F32)
        dlogf = jnp.dot(tri, dc, precision=lax.Precision.HIGHEST, preferred_element_type=F32) + carry[0:1, :]
        carry[0:1, :] = dlogf[0:1, :]
        df = dlogf * (1.0 / (1.0 + jnp.exp(f_ref[...] + b_ref[...])))
        db_ref[...] += jnp.sum(df, axis=0, keepdims=True)
        o_ref[:, FOFF:FOFF + LANES] = df.astype(BF16)
        o_ref[:, FOFF + LANES:NA] = jnp.zeros((TM, NA - FOFF - LANES), BF16)

    rev = lambda width, col: pl.BlockSpec((TM, width), lambda i: (nt - 1 - i, col))
    gspec = pl.BlockSpec((1, HD), lambda i: (0, 0))
    acc = pl.BlockSpec((1, LANES), lambda i: (0, 0))
    return pl.pallas_call(
        body, name="prep_a_bwd", grid=(nt,),
        in_specs=[rev(D, 0), rev(D, 0), rev(D, 0), rev(D, 0),
                  pl.BlockSpec((N_HEADS, None, 1, TM), lambda i: (0, nt - 1 - i, 0, 0)),
                  pl.BlockSpec((N_HEADS // HPS, TM, LANES), lambda i: (0, nt - 1 - i, 0)),
                  rev(D, 0), rev(D, 1), rev(LANES, FOFF // LANES), acc, gspec, gspec],
        out_specs=[rev(NA, 0), acc, acc, acc],
        out_shape=[_sds((S, NA), BF16)] + [_sds((1, LANES), F32)] * 3,
        scratch_shapes=[pltpu.VMEM((8, LANES), F32)],
        compiler_params=_params())(dq, dk, dv, dgate, drow, dcs, proj, proj, proj, b_pad, gq, gk)


def _head_b(x, z_a, w_out_a, g_kv, g_b, w_kv, w_in_b, gq, gk, cos2, sin2):
    nkv = w_kv.shape[1] // 2

    def body(x_ref, z_ref, wo_ref, gkv_ref, gb_ref, wkv_ref, wb_ref, gq_ref, gk_ref, c_ref, s_ref,
             h_ref, ukv_ref, ub_ref, kv_ref, pb_ref, qo_ref, ko_ref, vo_ref):
        xv = x_ref[...] + _dot_nn(z_ref[...], wo_ref[...])
        h_ref[...] = xv
        xn = xv * _rms_rinv(xv)
        ukv = (xn * gkv_ref[...]).astype(BF16)
        ub = (xn * gb_ref[...]).astype(BF16)
        ukv_ref[...] = ukv
        ub_ref[...] = ub
        kv_ref[...] = _dot_nn(ukv, wkv_ref[...])
        for lo in range(0, 2 * D, D):
            pb_ref[:, lo:lo + D] = _dot_nn(ub, wb_ref[:, lo:lo + D])
        lane = _lane_iota((RT, LANES))
        lo_half = lane < HD
        cos, sin = c_ref[...], s_ref[...]
        gq2, gk2 = _g2(gq_ref), _g2(gk_ref)
        for c in _pairs(D):
            q = pb_ref[:, c]
            qo_ref[:, c] = (_rope_fwd((q * _head_rinv(q, lo_half)) * gq2, cos, sin, lane) * QSCALE).astype(BF16)
        for c in _pairs(nkv):
            k = kv_ref[:, c]
            ko_ref[:, c] = _rope_fwd((k * _head_rinv(k, lo_half)) * gk2, cos, sin, lane).astype(BF16)
        vo_ref[...] = kv_ref[:, nkv:2 * nkv].astype(BF16)

    row = lambda width: pl.BlockSpec((RT, width), lambda i: (i, 0))
    whole = lambda arr: pl.BlockSpec(arr.shape, lambda i: (0,) * arr.ndim, pipeline_mode=pl.Buffered(1))
    return pl.pallas_call(
        body, name="head_b", grid=(S // RT,),
        in_specs=[row(D), row(D), whole(w_out_a), whole(g_kv), whole(g_b), whole(w_kv), whole(w_in_b), whole(gq),
                  whole(gk), row(LANES), row(LANES)],
        out_specs=[row(D), row(D), row(D), row(2 * nkv), row(2 * D), row(D), row(nkv), row(nkv)],
        out_shape=[_sds((S, D), F32), _sds((S, D), BF16), _sds((S, D), BF16), _sds((S, 2 * nkv), F32),
                   _sds((S, 2 * D), F32), _sds((S, D), BF16), _sds((S, nkv), BF16), _sds((S, nkv), BF16)],
        compiler_params=_params())(x, z_a, w_out_a, g_kv, g_b, w_kv, w_in_b, gq, gk, cos2, sin2)


N_KV, GRP = 4, 4


def _swa_mask(n):
    r = lax.broadcasted_iota(jnp.int32, (2 * WIN, GRP * WIN), 0)
    q = lax.broadcasted_iota(jnp.int32, (2 * WIN, GRP * WIN), 1) & (WIN - 1)
    return (r > q) & (r <= q + WIN) & ((r >= WIN) | (n > 0))


def _stack4(ref_or_val, base):
    return jnp.concatenate([ref_or_val[:, base + HD * g: base + HD * (g + 1)] for g in range(GRP)], axis=0)


def _unstack4(xt):
    return jnp.concatenate([xt[:, WIN * g:WIN * (g + 1)] for g in range(GRP)], axis=0).T


def _band(prev_ref, cur_ref, kh):
    return jnp.concatenate([prev_ref[:, HD * kh:HD * (kh + 1)], cur_ref[:, HD * kh:HD * (kh + 1)]], axis=0)


def _sink_row(s_ref, first):
    lane = _lane_iota((1, GRP * WIN))
    row = jnp.full((1, GRP * WIN), s_ref[first + GRP - 1], F32)
    for g in range(GRP - 2, -1, -1):
        row = jnp.where(lane < WIN * (g + 1), s_ref[first + g], row)
    return row


def _swa_fwd(qb, ksh, vsh, pb, sinks):
    nb = S // WIN

    def body(q_ref, kp_ref, kc_ref, vp_ref, vc_ref, g_ref, s_ref, o_ref, z_ref, lse_ref):
        n = pl.program_id(0)
        valid = _swa_mask(n)
        outs = []
        for kh in range(N_KV):
            kb, vb = _band(kp_ref, kc_ref, kh), _band(vp_ref, vc_ref, kh)
            st = jnp.where(valid, _dot_nt(kb, _stack4(q_ref, GRP * HD * kh)), -jnp.inf)
            sink = _sink_row(s_ref, GRP * kh)
            m = jnp.maximum(jnp.max(st, axis=0, keepdims=True), sink)
            pt = jnp.exp(st - m)
            l = jnp.sum(pt, axis=0, keepdims=True) + jnp.exp(sink - m)
            outs.append(_unstack4(_dot_tn(vb, pt.astype(BF16)) / l))
            lse = m + jnp.log(l)
            for g in range(GRP):
                lse_ref[GRP * kh + g, 0] = lse[:, WIN * g:WIN * (g + 1)]
        o = jnp.concatenate(outs, axis=-1)
        o_ref[...] = o
        g = g_ref[...]
        z_ref[...] = (o * (g * _sigmoid(g))).astype(BF16)

    row = pl.BlockSpec((WIN, D), lambda n: (n, 0))
    prev = pl.BlockSpec((WIN, N_KV * HD), lambda n: (jnp.maximum(n - 1, 0), 0))
    cur = pl.BlockSpec((WIN, N_KV * HD), lambda n: (n, 0))
    return pl.pallas_call(
        body, name="swa_fwd", grid=(nb,),
        in_specs=[row, prev, cur, prev, cur, pl.BlockSpec((WIN, D), lambda n: (n, 1)),
                  pl.BlockSpec(memory_space=pltpu.SMEM)],
        out_specs=[row, row, pl.BlockSpec((N_HEADS, 1, 1, WIN), lambda n: (0, n, 0, 0))],
        out_shape=[_sds((S, D), F32), _sds((S, D), BF16), _sds((N_HEADS, nb, 1, WIN), F32)],
        compiler_params=_params())(qb, ksh, ksh, vsh, vsh, pb, sinks)


def _swa_bwd(qb, ksh, vsh, dz, o, lse, pb, sinks, gq, cos2, sin2):
    nb = S // WIN

    def body(q_ref, kp_ref, kc_ref, vp_ref, vc_ref, dz_ref, o_ref, lse_ref, x_ref, g_ref, s_ref, gq_ref, c_ref, sn_ref,
             dpb_ref, dka_ref, dkb_ref, dva_ref, dvb_ref, dsink_ref, dgq_ref):
        n = pl.program_id(0)

        @pl.when(n == 0)
        def _():
            dsink_ref[...] = jnp.zeros_like(dsink_ref)
            dgq_ref[...] = jnp.zeros_like(dgq_ref)

        valid = _swa_mask(n)
        g = g_ref[...]
        sg = _sigmoid(g)
        dzv = dz_ref[...]
        ov = o_ref[...]
        do = dzv * (g * sg)
        dpb_ref[:, D:2 * D] = (dzv * ov * (sg * (1.0 + g * (1.0 - sg)))).astype(BF16)
        prod_t = (do * ov).T
        lane1 = _lane_iota((1, LANES))
        dqs, dkas, dkbs, dvas, dvbs = [], [], [], [], []
        dsink = jnp.zeros((1, LANES), F32)
        for kh in range(N_KV):
            kb, vb = _band(kp_ref, kc_ref, kh), _band(vp_ref, vc_ref, kh)
            base = GRP * HD * kh
            qs = _stack4(q_ref, base)
            dos = _stack4(do, base).astype(BF16)
            delta = jnp.concatenate(
                [jnp.sum(prod_t[base + HD * gg:base + HD * (gg + 1), :], axis=0, keepdims=True)
                 for gg in range(GRP)], axis=1)
            lse = jnp.concatenate([lse_ref[GRP * kh + gg, 0] for gg in range(GRP)], axis=1)
            st = jnp.where(valid, _dot_nt(kb, qs), -jnp.inf)
            pt = jnp.exp(st - lse)
            dst = pt * (_dot_nt(vb, dos) - delta)
            dsb = dst.astype(BF16)
            dqs.append(_unstack4(_dot_tn(kb, dsb)))
            dkband = _dot_nn(dsb, qs)
            dvband = _dot_nn(pt.astype(BF16), dos)
            dkbs.append(dkband[0:WIN, :])
            dkas.append(dkband[WIN:2 * WIN, :])
            dvbs.append(dvband[0:WIN, :])
            dvas.append(dvband[WIN:2 * WIN, :])
            ps_delta = jnp.exp(_sink_row(s_ref, GRP * kh) - lse) * delta
            for gg in range(GRP):
                val = jnp.sum(ps_delta[:, WIN * gg:WIN * (gg + 1)], axis=1, keepdims=True)
                dsink = dsink - jnp.where(lane1 == GRP * kh + gg, val, 0.0)
        dka_ref[...] = jnp.concatenate(dkas, axis=-1)
        dkb_ref[...] = jnp.concatenate(dkbs, axis=-1)
        dva_ref[...] = jnp.concatenate(dvas, axis=-1)
        dvb_ref[...] = jnp.concatenate(dvbs, axis=-1)
        dsink_ref[...] += dsink

        lane = _lane_iota((WIN, LANES))
        g2, cos, sin = _g2(gq_ref), c_ref[...], sn_ref[...]
        lo_half = _half_ones()
        dg_tot = jnp.zeros((1, LANES), F32)
        for kh in range(N_KV):
            for c in _pairs(GRP * HD):
                cols = slice(GRP * HD * kh + c.start, GRP * HD * kh + c.stop)
                dn = _rope_bwd(dqs[kh][:, c] * QSCALE, cos, sin, lane)
                dx, dg = _head_norm_bwd(dn, x_ref[:, cols], g2, lo_half)
                dpb_ref[:, cols] = dx.astype(BF16)
                dg_tot = dg_tot + dg
        dgq_ref[...] += dg_tot

    row = pl.BlockSpec((WIN, D), lambda n: (n, 0))
    prev = pl.BlockSpec((WIN, N_KV * HD), lambda n: (jnp.maximum(n - 1, 0), 0))
    cur = pl.BlockSpec((WIN, N_KV * HD), lambda n: (n, 0))
    acc = pl.BlockSpec((1, LANES), lambda n: (0, 0))
    tab = pl.BlockSpec((WIN, LANES), lambda n: (n, 0))
    return pl.pallas_call(
        body, name="swa_bwd", grid=(nb,),
        in_specs=[row, prev, cur, prev, cur, row, row, pl.BlockSpec((N_HEADS, 1, 1, WIN), lambda n: (0, n, 0, 0)),
                  row, pl.BlockSpec((WIN, D), lambda n: (n, 1)), pl.BlockSpec(memory_space=pltpu.SMEM),
                  pl.BlockSpec((1, HD), lambda n: (0, 0)), tab, tab],
        out_specs=[pl.BlockSpec((WIN, 2 * D), lambda n: (n, 0)), cur, cur, cur, cur, acc, acc],
        out_shape=[_sds((S, 2 * D), BF16)] + [_sds((S, 256), F32)] * 4 + [_sds((1, LANES), F32)] * 2,
        compiler_params=_params())(qb, ksh, ksh, vsh, vsh, dz, o, lse, pb, pb, sinks, gq, cos2, sin2)


def _prep_kv_bwd(dka, dkb, dva, dvb, kv, gk, cos2, sin2):
    nt = S // RB
    per = RB // WIN

    def shifted(cur_ref, nxt_ref, has_next):
        return jnp.concatenate([cur_ref[WIN:RB, :], jnp.where(has_next, nxt_ref[...], 0.0)], axis=0)

    def body(dka_ref, dkb_ref, dkn_ref, dva_ref, dvb_ref, dvn_ref, x_ref, g_ref, c_ref, s_ref, o_ref, dgk_ref):
        i, j = pl.program_id(0), pl.program_id(1)

        @pl.when((i == 0) & (j == 0))
        def _():
            dgk_ref[...] = jnp.zeros_like(dgk_ref)

        has_next = i < nt - 1

        @pl.when(j == 0)
        def _():
            lane = _lane_iota((RB, LANES))
            g2, cos, sin = _g2(g_ref), c_ref[...], s_ref[...]
            dy_all = dka_ref[...] + shifted(dkb_ref, dkn_ref, has_next)
            dg_tot = jnp.zeros((1, LANES), F32)
            lo_half = _half_ones()
            for c in _pairs(CB):
                dn = _rope_bwd(dy_all[:, c], cos, sin, lane)
                dx, dg = _head_norm_bwd(dn, x_ref[:, c], g2, lo_half)
                o_ref[:, c] = dx.astype(BF16)
                dg_tot = dg_tot + dg
            dgk_ref[...] += dg_tot

        @pl.when(j == 1)
        def _():
            o_ref[...] = (dva_ref[...] + shifted(dvb_ref, dvn_ref, has_next)).astype(BF16)

    cur = pl.BlockSpec((RB, CB), lambda i, j: (i, 0))
    nxt = pl.BlockSpec((WIN, CB), lambda i, j: (jnp.minimum(per * (i + 1), S // WIN - 1), 0))
    tab = pl.BlockSpec((RB, LANES), lambda i, j: (i, 0))
    return pl.pallas_call(
        body, name="prep_kv_bwd", grid=(nt, 2),
        in_specs=[cur, cur, nxt, cur, cur, nxt, cur, pl.BlockSpec((1, HD), lambda i, j: (0, 0)), tab, tab],
        out_specs=[pl.BlockSpec((RB, CB), lambda i, j: (i, j)), pl.BlockSpec((1, LANES), lambda i, j: (0, 0))],
        out_shape=[_sds((S, 2 * CB), BF16), _sds((1, LANES), F32)],
        compiler_params=_params())(dka, dkb, dkb, dva, dvb, dvb, kv, gk, cos2, sin2)


def _out_b_loss(z, w_out, h1, tgt):
    tm = 2 * RT

    def body(z_ref, w_ref, h_ref, t_ref, dy_ref, l_ref):
        @pl.when(pl.program_id(0) == 0)
        def _():
            l_ref[...] = jnp.zeros_like(l_ref)

        e = (h_ref[...] + _dot_nn(z_ref[...], w_ref[...])) - t_ref[...]
        dy_ref[...] = e * (1.0 / D)
        l_ref[...] += jnp.sum(jnp.sum(e * e, axis=-1, keepdims=True), axis=0, keepdims=True)

    row = pl.BlockSpec((tm, D), lambda i: (i, 0))
    return pl.pallas_call(
        body, name="out_b_loss", grid=(S // tm,),
        in_specs=[row, pl.BlockSpec(w_out.shape, lambda i: (0, 0), pipeline_mode=pl.Buffered(1)), row, row],
        out_specs=[row, pl.BlockSpec((1, LANES), lambda i: (0, 0))],
        out_shape=[_sds((S, D), F32), _sds((1, LANES), F32)], compiler_params=_params())(z, w_out, h1, tgt)


def _du_a_rms_bwd(dproj, wa, x, g, dres, after):
    tm, tk = 1024, NA // 2
    nk = NA // tk

    def body(a_ref, b_ref, x_ref, g_ref, dr_ref, after_ref, dx_ref, dg_ref, acc):
        i, kk = pl.program_id(0), pl.program_id(1)

        @pl.when((i == 0) & (kk == 0))
        def _():
            dg_ref[...] = jnp.zeros_like(dg_ref)

        p = _dot_nt(a_ref[...], b_ref[...])

        @pl.when(kk == 0)
        def _():
            acc[...] = p

        @pl.when(kk == nk - 1)
        def _():
            dx, dg = _rms_bwd_core(acc[...] + p, x_ref[...], g_ref[...])
            dx_ref[...] = dr_ref[...] + dx
            dg_ref[...] += dg

    assert nk == 2
    row = pl.BlockSpec((tm, D), lambda i, kk: (i, 0))
    vec = pl.BlockSpec((1, D), lambda i, kk: (0, 0))
    return pl.pallas_call(
        body, name="du_a_rms_bwd", grid=(S // tm, nk),
        in_specs=[pl.BlockSpec((tm, tk), lambda i, kk: (i, kk)), pl.BlockSpec((D, tk), lambda i, kk: (0, kk)),
                  row, vec, row, pl.BlockSpec(after.shape, lambda i, kk: (0, 0))],
        out_specs=[row, vec], out_shape=[_sds((S, D), F32), _sds((1, D), F32)],
        scratch_shapes=[pltpu.VMEM((tm, D), F32)], compiler_params=_params())(dproj, wa, x, g, dres, after)


def _du_b_rms_bwd(dpb, w_in_b, dkv, w_kv, h1, g_b, g_kv, dy):
    tm = 2 * RT

    def body(ab_ref, wb_ref, akv_ref, wkv_ref, x_ref, gb_ref, gkv_ref, dy_ref, dh_ref, dgb_ref, dgkv_ref):
        @pl.when(pl.program_id(0) == 0)
        def _():
            dgb_ref[...] = jnp.zeros_like(dgb_ref)
            dgkv_ref[...] = jnp.zeros_like(dgkv_ref)

        x = x_ref[...]
        dx1, dg1 = _rms_bwd_core(_dot_nt(ab_ref[...], wb_ref[...]), x, gb_ref[...])
        dx2, dg2 = _rms_bwd_core(_dot_nt(akv_ref[...], wkv_ref[...]), x, gkv_ref[...])
        dh_ref[...] = dy_ref[...] + dx1 + dx2
        dgb_ref[...] += dg1
        dgkv_ref[...] += dg2

    row = lambda width: pl.BlockSpec((tm, width), lambda i: (i, 0))
    whole = lambda arr: pl.BlockSpec(arr.shape, lambda i: (0, 0), pipeline_mode=pl.Buffered(1))
    vec = pl.BlockSpec((1, D), lambda i: (0, 0))
    return pl.pallas_call(
        body, name="du_b_rms_bwd", grid=(S // tm,),
        in_specs=[row(dpb.shape[1]), whole(w_in_b), row(dkv.shape[1]), whole(w_kv), row(D), vec, vec, row(D)],
        out_specs=[row(D), vec, vec], out_shape=[_sds((S, D), F32), _sds((1, D), F32), _sds((1, D), F32)],
        compiler_params=_params())(dpb, w_in_b, dkv, w_kv, h1, g_b, g_kv, dy)


GATHER_CHUNKS = 4


def _gather_first(w_in_a, w_out_a, w_kv, w_in_b, w_out_b, norm_a_g):
    rows, cols = w_in_a.shape[-2:]
    groups = rows // LANES
    cols_pad = -(-cols // LANES) * LANES

    def body(wia_ref, woa_ref, wkv_ref, wib_ref, wob_ref, ga_ref,
             wa_g, ga_g, woa_s, wkv_s, wib_s, wob_s, wa_s, st_a, st_oa, st_kv, st_ib, st_ob, plane_t, load_sems, *sems):
        sources = [wia_ref, woa_ref.at[0], wkv_ref, wib_ref.at[0], wob_ref.at[0]]
        stages = [st_a, st_oa, st_kv, st_ib, st_ob]
        loads = [pltpu.make_async_copy(src, dst, load_sems.at[i]) for i, (src, dst) in enumerate(zip(sources, stages))]
        for cp in loads:
            cp.start()
        loads[0].wait()
        plane_t[cols_pad - LANES:, :] = jnp.zeros((LANES, LANES), F32)
        for j in range(groups):
            for c0 in range(0, cols, 64):
                n = min(64, cols - c0)
                plane_t[c0:c0 + n, :] = st_a[pl.ds(c0 * groups + j, n, stride=groups), :]
            for c0 in range(0, cols, LANES):
                n = min(LANES, cols - c0)
                wa_s[j * LANES:(j + 1) * LANES, c0:c0 + n] = plane_t[c0:c0 + LANES, :].T[:, :n].astype(BF16)

        def cast_the_rest():
            for cp, stage, out in zip(loads[1:], stages[1:], [woa_s, wkv_s, wib_s, wob_s]):
                cp.wait()
                out[...] = stage[...].astype(BF16)

        chunks = [pl.ds(r0, rows // GATHER_CHUNKS) for r0 in range(0, rows, rows // GATHER_CHUNKS)]
        _gather_two_level(
            [wa_s.at[rc] for rc in chunks] + [ga_ref],
            [lambda dev, rc=rc: wa_g.at[dev, rc] for rc in chunks] + [lambda dev: ga_g.at[dev]],
            sems, meanwhile=cast_the_rest)

    vmem = pl.BlockSpec(memory_space=pltpu.VMEM)
    anyspec = pl.BlockSpec(memory_space=pl.ANY)
    shard = lambda w: _sds(w.shape[-2:], BF16)
    stage = lambda w: pltpu.VMEM(w.shape[-2:], F32)
    return pl.pallas_call(
        body, name="gather_first", in_specs=[anyspec] * 5 + [vmem],
        out_specs=[anyspec, anyspec, vmem, vmem, vmem, vmem],
        out_shape=[_sds((N_DEV,) + w_in_a.shape[-2:], BF16), _sds((N_DEV,) + norm_a_g.shape, F32),
                   shard(w_out_a), shard(w_kv), shard(w_in_b), shard(w_out_b)],
        scratch_shapes=[pltpu.VMEM(w_in_a.shape[-2:], BF16), pltpu.VMEM((cols * groups, LANES), F32), stage(w_out_a),
                        stage(w_kv), stage(w_in_b), stage(w_out_b), pltpu.VMEM((cols_pad, LANES), F32),
                        pltpu.SemaphoreType.DMA((5,))] + _exchange_sems(GATHER_CHUNKS + 1),
        compiler_params=pltpu.CompilerParams(vmem_limit_bytes=VMEM_LIMIT, has_side_effects=True))(
            _entry_view(w_in_a).reshape(cols * groups, LANES), w_out_a, w_kv, w_in_b, w_out_b, norm_a_g)


def _padded_col(c):
    if c < RAW_F:
        return c
    return FOFF + (c - RAW_F) if c < RAW_G else GOFF + (c - RAW_G)


def _shard_pieces():
    width = NA_RAW // N_DEV
    pieces = []
    for d in range(N_DEV):
        cuts = [width * d] + [c for c in (RAW_F, RAW_G) if width * d < c < width * (d + 1)] + [width * (d + 1)]
        for lo, hi in zip(cuts[:-1], cuts[1:]):
            pieces.append((d, lo - width * d, _padded_col(lo), hi - lo))
    return pieces


def _unshard_wa(wa_g):
    def body(w_ref, o_ref):
        o_ref[:, FOFF + N_HEADS:NA] = jnp.zeros((TM, NA - FOFF - N_HEADS), BF16)
        for d, src, dst, width in _shard_pieces():
            o_ref[:, dst:dst + width] = w_ref[d, :, src:src + width]

    return pl.pallas_call(
        body, name="unshard_wa", grid=(D // TM,),
        in_specs=[pl.BlockSpec((N_DEV, TM, NA_RAW // N_DEV), lambda i: (0, i, 0))],
        out_specs=pl.BlockSpec((TM, NA), lambda i: (i, 0)), out_shape=_sds((D, NA), BF16),
        compiler_params=_params())(wa_g)


def _reshard_pair_reduce(dwa):
    n_chip, nt = N_DEV // 2, D // TM
    width = NA_RAW // N_DEV

    def body(g_ref, o_ref, slots_v, sib_v, send_sems, recv_sems):
        i = pl.program_id(0)
        x, y, c = lax.axis_index("x"), lax.axis_index("y"), lax.axis_index("c")

        def tile_rows(tile):
            return pl.ds(tile * TM if isinstance(tile, int) else pl.multiple_of(tile * TM, TM), TM)

        def give(j, tile):
            return pltpu.make_async_remote_copy(
                src_ref=slots_v.at[2 * j + 1 - c, tile_rows(tile)], dst_ref=sib_v.at[j, tile_rows(tile)],
                send_sem=send_sems.at[j, tile], recv_sem=recv_sems.at[j, tile], device_id=(x, y, 1 - c),
                device_id_type=pl.DeviceIdType.MESH)

        for d, src, dst, w in _shard_pieces():
            slots_v[d, tile_rows(i), src:src + w] = g_ref[:, dst:dst + w]
        for j in range(n_chip):
            give(j, i).start()

        @pl.when(i == nt - 1)
        def _():
            for tile in range(nt):
                for j in range(n_chip):
                    give(j, tile).wait()
            for j in range(n_chip):
                for r0 in range(0, D, 64):
                    mine = slots_v[2 * j + c, r0:r0 + 64, :].astype(F32)
                    o_ref[j, r0:r0 + 64, :] = (mine + sib_v[j, r0:r0 + 64, :].astype(F32)).astype(BF16)

    half = _sds((n_chip, D, width), dwa.dtype)
    return pl.pallas_call(
        body, name="reshard_pair_reduce", grid=(nt,), in_specs=[pl.BlockSpec((TM, NA), lambda i: (i, 0))],
        out_specs=pl.BlockSpec(half.shape, lambda i: (0, 0, 0)), out_shape=half,
        scratch_shapes=[pltpu.VMEM((N_DEV, D, width), dwa.dtype), pltpu.VMEM(half.shape, dwa.dtype),
                        pltpu.SemaphoreType.DMA((n_chip, nt)), pltpu.SemaphoreType.DMA((n_chip, nt))],
        compiler_params=pltpu.CompilerParams(vmem_limit_bytes=VMEM_LIMIT, has_side_effects=True))(dwa)


CHIP_FLIPS = (2, 4, 6)


def _chip_exchange_start(partial):
    n = len(CHIP_FLIPS)

    def body(p_ref, land_ref, *rest):
        sends, recvs, token = rest[:n], rest[n:2 * n], rest[2 * n + 2]
        x, y, c = lax.axis_index("x"), lax.axis_index("y"), lax.axis_index("c")
        me = 4 * x + 2 * y + c
        for idx, k in enumerate(CHIP_FLIPS):
            pltpu.make_async_remote_copy(
                src_ref=p_ref.at[(me ^ k) >> 1], dst_ref=land_ref.at[me >> 1], send_sem=sends[idx],
                recv_sem=recvs[idx], device_id=(x ^ ((k >> 2) & 1), y ^ ((k >> 1) & 1), c),
                device_id_type=pl.DeviceIdType.MESH).start()
        token[...] = jnp.zeros_like(token)

    hbm = pl.BlockSpec(memory_space=pltpu.HBM)
    sem = pl.BlockSpec(memory_space=pltpu.SEMAPHORE)
    buf = pltpu.HBM(partial.shape, partial.dtype)
    return pl.pallas_call(
        body, name="chip_exchange_start",
        out_shape=(pltpu.SemaphoreType.DMA(()),) * (2 * n) + (buf, buf, _sds((8, LANES), F32)),
        in_specs=(hbm, hbm), out_specs=(sem,) * (2 * n) + (hbm, hbm, pl.BlockSpec(memory_space=pltpu.VMEM)),
        input_output_aliases={0: 2 * n, 1: 2 * n + 1},
        compiler_params=pltpu.CompilerParams(has_side_effects=pltpu.SideEffectType.DATAFLOW_SIDE_EFFECTING))(
            pltpu.with_memory_space_constraint(partial, pltpu.HBM),
            pltpu.with_memory_space_constraint(lax.empty(partial.shape, partial.dtype), pltpu.HBM))


def _chip_exchange_wait(started, after):
    n = len(CHIP_FLIPS)
    sems, (p_thru, land_thru) = started[:2 * n], started[2 * n:2 * n + 2]

    def body(p_ref, land_ref, *rest):
        sends, recvs = rest[:n], rest[n:2 * n]
        x, y, c = lax.axis_index("x"), lax.axis_index("y"), lax.axis_index("c")
        me = 4 * x + 2 * y + c
        for idx, k in enumerate(CHIP_FLIPS):
            copy = pltpu.make_async_remote_copy(
                src_ref=p_ref.at[(me ^ k) >> 1], dst_ref=land_ref.at[(me ^ k) >> 1], send_sem=sends[idx],
                recv_sem=recvs[idx], device_id=(x ^ ((k >> 2) & 1), y ^ ((k >> 1) & 1), c),
                device_id_type=pl.DeviceIdType.MESH)
            copy.wait_send()
            copy.wait_recv()

    hbm = pl.BlockSpec(memory_space=pltpu.HBM)
    sem = pl.BlockSpec(memory_space=pltpu.SEMAPHORE)
    buf = pltpu.HBM(p_thru.shape, p_thru.dtype)
    return pl.pallas_call(
        body, name="chip_exchange_wait", out_shape=(buf, buf),
        in_specs=(hbm, hbm) + (sem,) * (2 * n) + (pl.BlockSpec(memory_space=pl.ANY),) * len(after),
        out_specs=(hbm, hbm), input_output_aliases={0: 0, 1: 1},
        compiler_params=pltpu.CompilerParams(has_side_effects=pltpu.SideEffectType.DATAFLOW_SIDE_EFFECTING))(
            p_thru, land_thru, *sems, *after)


def _slab_peer(x, y, c, k):
    return (x ^ ((k >> 2) & 1), y ^ ((k >> 1) & 1), c ^ (k & 1))


def _gather_slab_start(slab, carried):
    n = N_DEV - 1

    def body(s_ref, land_ref, carried_ref, *rest):
        sends, recvs, own_sem = rest[:n], rest[n:2 * n], rest[2 * n + 3]
        x, y, c = lax.axis_index("x"), lax.axis_index("y"), lax.axis_index("c")
        me = 4 * x + 2 * y + c
        for k in range(1, N_DEV):
            pltpu.make_async_remote_copy(
                src_ref=s_ref, dst_ref=land_ref.at[me], send_sem=sends[k - 1], recv_sem=recvs[k - 1],
                device_id=_slab_peer(x, y, c, k), device_id_type=pl.DeviceIdType.MESH).start()
        own = pltpu.make_async_copy(s_ref, land_ref.at[me], own_sem)
        own.start()
        own.wait()

    hbm = pl.BlockSpec(memory_space=pltpu.HBM)
    sem = pl.BlockSpec(memory_space=pltpu.SEMAPHORE)
    land = (N_DEV,) + slab.shape
    return pl.pallas_call(
        body, name="gather_slab_start",
        out_shape=(pltpu.SemaphoreType.DMA(()),) * (2 * n) + (
            pltpu.HBM(slab.shape, slab.dtype), pltpu.HBM(land, slab.dtype), pltpu.HBM(carried.shape, carried.dtype)),
        in_specs=(hbm, hbm, hbm), out_specs=(sem,) * (2 * n) + (hbm, hbm, hbm),
        input_output_aliases={0: 2 * n, 1: 2 * n + 1, 2: 2 * n + 2}, scratch_shapes=[pltpu.SemaphoreType.DMA(())],
        compiler_params=pltpu.CompilerParams(has_side_effects=pltpu.SideEffectType.DATAFLOW_SIDE_EFFECTING))(
            pltpu.with_memory_space_constraint(slab, pltpu.HBM),
            pltpu.with_memory_space_constraint(lax.empty(land, slab.dtype), pltpu.HBM),
            pltpu.with_memory_space_constraint(carried, pltpu.HBM))


def _gather_slab_wait(started, after):
    n = N_DEV - 1
    sems, (s_thru, land_thru) = started[:2 * n], started[2 * n:2 * n + 2]

    def body(s_ref, land_ref, *rest):
        sends, recvs = rest[:n], rest[n:2 * n]
        x, y, c = lax.axis_index("x"), lax.axis_index("y"), lax.axis_index("c")
        me = 4 * x + 2 * y + c
        for k in range(1, N_DEV):
            copy = pltpu.make_async_remote_copy(
                src_ref=s_ref, dst_ref=land_ref.at[me ^ k], send_sem=sends[k - 1], recv_sem=recvs[k - 1],
                device_id=_slab_peer(x, y, c, k), device_id_type=pl.DeviceIdType.MESH)
            copy.wait_send()
            copy.wait_recv()

    hbm = pl.BlockSpec(memory_space=pltpu.HBM)
    sem = pl.BlockSpec(memory_space=pltpu.SEMAPHORE)
    return pl.pallas_call(
        body, name="gather_slab_wait",
        out_shape=(pltpu.HBM(s_thru.shape, s_thru.dtype), pltpu.HBM(land_thru.shape, land_thru.dtype)),
        in_specs=(hbm, hbm) + (sem,) * (2 * n) + (pl.BlockSpec(memory_space=pl.ANY),) * len(after),
        out_specs=(hbm, hbm), input_output_aliases={0: 0, 1: 1},
        compiler_params=pltpu.CompilerParams(has_side_effects=pltpu.SideEffectType.DATAFLOW_SIDE_EFFECTING))(
            s_thru, land_thru, *sems, *after)[1]


def _adamw(w, g, m, v):
    m = ADAM_B1 * m + (1.0 - ADAM_B1) * g
    v = ADAM_B2 * v + (1.0 - ADAM_B2) * (g * g)
    m_hat = m / (1.0 - ADAM_B1 ** ADAM_STEP)
    v_hat = v / (1.0 - ADAM_B2 ** ADAM_STEP)
    delta = -ADAM_LR * (m_hat / (jnp.sqrt(v_hat) + ADAM_EPS) + ADAM_WD * w)
    return delta, m, v


def _sum_adamw(recv, w, m, v, name, after=None):
    lead = w.ndim - 2
    rows, cols = w.shape[-2:]
    tr = 256 if rows % 256 == 0 else 128
    n_slots = recv.shape[0]
    extra = [] if after is None else [after]

    def body(*refs):
        r_ref = refs[0]
        w_ref, m_ref, v_ref, g_ref, d_ref, nm_ref, nv_ref = refs[1 + len(extra):]
        g = None
        for slot in range(n_slots):
            g = r_ref[slot].astype(F32) if g is None else g + r_ref[slot].astype(F32)
        g_ref[...] = g
        d_ref[...], nm_ref[...], nv_ref[...] = _adamw(w_ref[...], g, m_ref[...], v_ref[...])

    blk = pl.BlockSpec((None,) * lead + (tr, cols), lambda i: (0,) * lead + (i, 0))
    slots = pl.BlockSpec((n_slots, tr, cols), lambda i: (0, i, 0))
    return pl.pallas_call(
        body, name=name, grid=(rows // tr,),
        in_specs=[slots] + [pl.BlockSpec(a.shape, lambda i: (0, 0)) for a in extra] + [blk, blk, blk],
        out_specs=[blk] * 4, out_shape=[_sds(w.shape, F32)] * 4, compiler_params=_params())(recv, *extra, w, m, v)


def _entry_view(a):
    _, rows, cols = a.shape
    return jnp.transpose(a, (2, 0, 1)).reshape(cols, rows // LANES, LANES)


def _from_entry_view(a):
    cols, groups, lanes = a.shape
    return jnp.transpose(a, (1, 2, 0)).reshape(1, groups * lanes, cols)


def _sum_adamw_entry_view(landing, own, w, m, v, name):
    n_slots, rows, cols = landing.shape
    groups = rows // LANES
    cols_pad = -(-cols // LANES) * LANES

    def body(r_ref, own_ref, w_ref, m_ref, v_ref, g_ref, d_ref, nm_ref, nv_ref, pad_ref, gt_ref):
        j = pl.program_id(0)
        chip = (4 * lax.axis_index("x") + 2 * lax.axis_index("y") + lax.axis_index("c")) >> 1
        pad_ref[:, cols_pad - LANES:] = jnp.zeros((LANES, LANES), F32)
        for r0 in range(0, LANES, 32):
            g = None
            for slot in range(n_slots):
                part = jnp.where(chip == slot, own_ref[slot, r0:r0 + 32], r_ref[slot, r0:r0 + 32])
                g = part.astype(F32) if g is None else g + part.astype(F32)
            pad_ref[r0:r0 + 32, :cols] = g
        for c0 in range(0, cols_pad, LANES):
            gt_ref[c0:c0 + LANES, :] = pad_ref[:, c0:c0 + LANES].T
        def update_plane(plane):
            for c0 in range(0, cols, 64):
                n = min(64, cols - c0)
                at = pl.ds(c0 * groups + plane, n, stride=groups)
                gt = gt_ref[c0:c0 + n, :]
                g_ref[at, :] = gt
                d_ref[at, :], nm_ref[at, :], nv_ref[at, :] = _adamw(w_ref[at, :], gt, m_ref[at, :], v_ref[at, :])

        for plane in range(groups):
            pl.when(j == plane)(lambda plane=plane: update_plane(plane))

    flat = lambda a: _entry_view(a).reshape(cols * groups, LANES)
    whole = pl.BlockSpec((cols * groups, LANES), lambda j: (0, 0))
    slots = pl.BlockSpec((n_slots, LANES, cols), lambda j: (0, j, 0))
    outs = pl.pallas_call(
        body, name=name, grid=(groups,), in_specs=[slots, slots, whole, whole, whole], out_specs=[whole] * 4,
        out_shape=[_sds((cols * groups, LANES), F32)] * 4,
        scratch_shapes=[pltpu.VMEM((LANES, cols_pad), F32), pltpu.VMEM((cols_pad, LANES), F32)],
        compiler_params=_params())(landing, own, flat(w), flat(m), flat(v))
    return [_from_entry_view(o.reshape(cols, groups, LANES)) for o in outs]


SLAB_ROWS = 16
SLOT = {"kv_norm_g": (8, 0, D), "norm_b_g": (9, 0, D), "b_forget": (10, 0, 16), "qnorm_a_g": (10, 128, HD),
        "knorm_a_g": (10, 256, HD), "knorm_b_g": (10, 384, HD), "qnorm_b_g": (10, 512, HD), "sinks": (10, 640, 16)}
SMALL = ["norm_a_g", "b_forget", "qnorm_a_g", "knorm_a_g", "kv_norm_g", "knorm_b_g", "norm_b_g", "qnorm_b_g", "sinks"]


LOSS_ROW = 11


def _pack_small(dg_a, dg_kv, dg_b, db_f, dgq_a, dgk_a, dgk_b, dgq_b, dsinks, lsum):
    def fold(ref):
        return ref[:, 0:HD] + ref[:, HD:2 * HD]

    def body(dga_ref, dgkv_ref, dgb_ref, dbf_ref, dgqa_ref, dgka_ref, dgkb_ref, dgqb_ref, dsk_ref, ls_ref, slab_ref):
        slab_ref[...] = jnp.zeros_like(slab_ref)
        for r in range(N_DEV):
            slab_ref[r:r + 1, 0:LANES] = dga_ref[:, LANES * r:LANES * (r + 1)]
        slab_ref[8:9, :] = dgkv_ref[...]
        slab_ref[9:10, :] = dgb_ref[...]
        slab_ref[10:11, 0:LANES] = dbf_ref[...]
        slab_ref[10:11, 128:128 + HD] = fold(dgqa_ref)
        slab_ref[10:11, 256:256 + HD] = fold(dgka_ref)
        slab_ref[10:11, 384:384 + HD] = fold(dgkb_ref)
        slab_ref[10:11, 512:512 + HD] = fold(dgqb_ref)
        slab_ref[10:11, 640:640 + LANES] = dsk_ref[...]
        slab_ref[LOSS_ROW:LOSS_ROW + 1, 0:LANES] = ls_ref[...]

    return pl.pallas_call(body, name="pack_small", out_shape=_sds((SLAB_ROWS, D), F32), compiler_params=_params())(
        dg_a, dg_kv, dg_b, db_f, dgq_a, dgk_a, dgk_b, dgq_b, dsinks, lsum)


def _small_adamw(recv, ws, ms, vs):
    k = len(SMALL)

    def body(*refs):
        r_ref = refs[0]
        w_refs, m_refs, v_refs = refs[1:1 + k], refs[1 + k:1 + 2 * k], refs[1 + 2 * k:1 + 3 * k]
        outs = refs[1 + 3 * k:1 + 7 * k]
        loss_ref, tot = refs[1 + 7 * k], refs[2 + 7 * k]
        g = r_ref[0]
        for dev in range(1, N_DEV):
            g = g + r_ref[dev]
        tot[...] = g
        loss_ref[...] = tot[LOSS_ROW:LOSS_ROW + 1, 0:LANES] * (0.5 / D)
        me = 4 * lax.axis_index("x") + 2 * lax.axis_index("y") + lax.axis_index("c")
        for p, name in enumerate(SMALL):
            if name == "norm_a_g":
                mine = lax.broadcasted_iota(jnp.int32, (N_DEV, LANES), 0) == me
                gp = jnp.sum(jnp.where(mine, tot[0:N_DEV, 0:LANES], 0.0), axis=0, keepdims=True)
            else:
                row, lo, width = SLOT[name]
                gp = tot[row:row + 1, lo:lo + width]
            d, nm, nv = _adamw(w_refs[p][...], gp, m_refs[p][...], v_refs[p][...])
            outs[p][...] = gp
            outs[k + p][...] = d
            outs[2 * k + p][...] = nm
            outs[3 * k + p][...] = nv

    shapes = [_sds(w.shape, F32) for w in ws]
    return pl.pallas_call(body, name="small_adamw", out_shape=shapes * 4 + [_sds((1, LANES), F32)],
                          scratch_shapes=[pltpu.VMEM((SLAB_ROWS, D), F32)],
                          compiler_params=_params())(recv, *ws, *ms, *vs)


def _rope_tables(positions):
    inv_freq = jnp.power(jnp.float32(ROPE_THETA), -jnp.arange(0, ROT, 2, dtype=F32) / ROT)
    ang = positions.astype(F32)[:, None] * inv_freq[None, :]
    cos, sin = jnp.cos(ang), jnp.sin(ang)
    c64 = jnp.concatenate([cos, cos, jnp.ones((S, HD - ROT), F32)], axis=-1)
    s64 = jnp.concatenate([-sin, sin, jnp.zeros((S, HD - ROT), F32)], axis=-1)
    return jnp.tile(c64, (1, 2)), jnp.tile(s64, (1, 2))


def _local_step(x, tgt, positions, g_a, wa, b_forget, gq_a, gk_a, g_kv, gk_b, g_b, gq_b, sinks,
                woa_s, wkv_s, wib_s, wob_s, adamw_others):
    nq = S // TQ
    cos2, sin2 = _rope_tables(positions)
    b_pad = jnp.pad(b_forget, ((0, 0), (0, LANES - N_HEADS)))

    u_a, proj, qn, kn, vb, ccol, cbc = _head_a(x, g_a, wa, gq_a, gk_a, b_pad)
    crow = ccol[:, :N_HEADS].T.reshape(N_HEADS, nq, 1, TQ)
    o_a, z_a, lse_a, woa_g, wkv_g, w_in_b, wob_g = _fox_fwd(
        qn, kn, vb, proj, crow, cbc,
        rider=[("gather_rows", woa_s), ("gather_rows", wkv_s), ("gather_cols", wib_s), ("gather_rows", wob_s)])
    w_out_a, w_kv, w_out_b = woa_g.reshape(D, D), wkv_g.reshape(D, 512), wob_g.reshape(D, D)
    h1, u_kv, u_b, kv, pb, qb, ksh, vsh = _head_b(x, z_a, w_out_a, g_kv, g_b, w_kv, w_in_b, gq_b, gk_b, cos2, sin2)
    sinks1 = sinks.reshape(N_HEADS)
    o_b, z_b, lse_b = _swa_fwd(qb, ksh, vsh, pb, sinks1)
    dy, lsum = _out_b_loss(z_b, w_out_b, h1, tgt)
    dw_out_b = _mm(z_b, dy, "tn", 512, 512, S, out_dtype=BF16, name="mm_dw_out_b")
    dz_b = _mm(dy, w_out_b, "nt", 1024, 512, D, name="mm_dz_b")
    dpb, dka, dkb, dva, dvb, dsinks, dgq_b = _swa_bwd(qb, ksh, vsh, dz_b, o_b, lse_b, pb, sinks1, gq_b, cos2, sin2)
    dkv, dgk_b = _prep_kv_bwd(dka, dkb, dva, dvb, kv, gk_b, cos2, sin2)
    dw_in_b = _mm(u_b, dpb, "tn", 512, 512, S, out_dtype=BF16, name="mm_dw_in_b")
    dw_kv = _mm(u_kv, dkv, "tn", 512, 512, S, out_dtype=BF16, name="mm_dw_kv")
    dh1, dg_b, dg_kv = _du_b_rms_bwd(dpb, w_in_b, dkv, w_kv, h1, g_b, g_kv, dy)
    dw_out_a = _mm(z_a, dh1, "tn", 512, 512, S, out_dtype=BF16, name="mm_dw_out_a")
    do_a, dgate_a, delta_a = _fox_bwd_pre(dh1, w_out_a, proj, o_a)
    dk_a, dv_a, dcs, dq_a, drow, r_wob, r_wib, r_wkv, r_woa = _fox_bwd(
        qn, kn, vb, do_a, lse_a, delta_a, crow, cbc,
        rider=[("a2a_rows", dw_out_b), ("a2a_cols", dw_in_b), ("a2a_rows", dw_kv), ("a2a_rows", dw_out_a)])
    dproj, dgq_a, dgk_a, db_f = _prep_a_bwd(dq_a, dk_a, dv_a, dgate_a, drow, dcs, proj, b_pad, gq_a, gk_a)
    dwa = _mm(u_a, dproj, "tn", 1024, 256, S, out_dtype=BF16, name="mm_dw_in_a")
    partial = _reshard_pair_reduce(dwa)
    started = _chip_exchange_start(partial)
    dx, dg_a = _du_a_rms_bwd(dproj, wa, x, g_a, dh1, after=started[-1])
    others = adamw_others(dict(w_out_a=r_woa, w_kv=r_wkv, w_in_b=r_wib, w_out_b=r_wob), dg_a)
    partial, landed = _chip_exchange_wait(started, [res[0] for res in others.values()])
    slab = _pack_small(dg_a, dg_kv, dg_b, db_f, dgq_a, dgk_a, dgk_b, dgq_b, dsinks, lsum)
    slab_started = _gather_slab_start(slab, landed)
    return dx, (slab_started[-1], partial), others, slab_started


def kernel(x, positions, norm_a_g, w_in_a, b_forget, qnorm_a_g, knorm_a_g, w_out_a, kv_norm_g, w_kv, knorm_b_g, norm_b_g, w_in_b, qnorm_b_g, sinks, w_out_b, loss_target, m_norm_a_g, m_w_in_a, m_b_forget, m_qnorm_a_g, m_knorm_a_g, m_w_out_a, m_kv_norm_g, m_w_kv, m_knorm_b_g, m_norm_b_g, m_w_in_b, m_qnorm_b_g, m_sinks, m_w_out_b, v_norm_a_g, v_w_in_a, v_b_forget, v_qnorm_a_g, v_knorm_a_g, v_w_out_a, v_kv_norm_g, v_w_kv, v_knorm_b_g, v_norm_b_g, v_w_in_b, v_qnorm_b_g, v_sinks, v_w_out_b):
    wa_g, ga_g, woa_s, wkv_s, wib_s, wob_s = _gather_first(w_in_a, w_out_a, w_kv, w_in_b, w_out_b, norm_a_g)
    state = dict(w_in_a=(w_in_a, m_w_in_a, v_w_in_a), w_out_a=(w_out_a, m_w_out_a, v_w_out_a),
                 w_kv=(w_kv, m_w_kv, v_w_kv), w_in_b=(w_in_b, m_w_in_b, v_w_in_b),
                 w_out_b=(w_out_b, m_w_out_b, v_w_out_b))

    def adamw_others(landed, after):
        return {n: _sum_adamw(r, *state[n], "adamw_" + n, after=after) for n, r in landed.items()}

    dx, r_wa, big, slab_started = _local_step(
        x[0], loss_target[0], positions, ga_g.reshape(1, D), _unshard_wa(wa_g), b_forget, qnorm_a_g, knorm_a_g,
        kv_norm_g.reshape(1, D), knorm_b_g.reshape(1, HD), norm_b_g, qnorm_b_g, sinks, woa_s, wkv_s, wib_s, wob_s,
        adamw_others)
    big["w_in_a"] = _sum_adamw_entry_view(*r_wa, *state["w_in_a"], "adamw_w_in_a")
    slab_g = _gather_slab_wait(slab_started, [big["w_in_a"][0]])

    r2 = lambda a: a.reshape(1, -1)
    small_w = dict(norm_a_g=norm_a_g, b_forget=b_forget, qnorm_a_g=qnorm_a_g, knorm_a_g=knorm_a_g,
                   kv_norm_g=kv_norm_g, knorm_b_g=knorm_b_g, norm_b_g=norm_b_g, qnorm_b_g=qnorm_b_g, sinks=sinks)
    small_m = dict(norm_a_g=m_norm_a_g, b_forget=m_b_forget, qnorm_a_g=m_qnorm_a_g, knorm_a_g=m_knorm_a_g,
                   kv_norm_g=m_kv_norm_g, knorm_b_g=m_knorm_b_g, norm_b_g=m_norm_b_g, qnorm_b_g=m_qnorm_b_g,
                   sinks=m_sinks)
    small_v = dict(norm_a_g=v_norm_a_g, b_forget=v_b_forget, qnorm_a_g=v_qnorm_a_g, knorm_a_g=v_knorm_a_g,
                   kv_norm_g=v_kv_norm_g, knorm_b_g=v_knorm_b_g, norm_b_g=v_norm_b_g, qnorm_b_g=v_qnorm_b_g,
                   sinks=v_sinks)
    res = _small_adamw(slab_g, [r2(small_w[n]) for n in SMALL], [r2(small_m[n]) for n in SMALL],
                       [r2(small_v[n]) for n in SMALL])
    k = len(SMALL)
    small = {n: [res[q * k + p].reshape(small_w[n].shape) for q in range(4)] for p, n in enumerate(SMALL)}
    loss = res[4 * k][0, 0]

    order = ["norm_a_g", "w_in_a", "b_forget", "qnorm_a_g", "knorm_a_g", "w_out_a", "kv_norm_g", "w_kv",
             "knorm_b_g", "norm_b_g", "w_in_b", "qnorm_b_g", "sinks", "w_out_b"]

    def leaf(n, q):
        return big[n][q] if n in big else small[n][q]

    outs = [loss, dx[None]]
    for q in range(4):
        outs.extend(leaf(n, q) for n in order)
    return tuple(outs)
```

```python
import jax
import jax.numpy as jnp
from jax import lax
from jax.experimental import pallas as pl
from jax.experimental.pallas import tpu as pltpu

F32, BF16 = jnp.float32, jnp.bfloat16

S = 2048
D = 1024
HD = 64
N_HEADS = 16
N_DEV = 8
NA = 4352
GOFF = 3072
FOFF = 4096
RAW_F = 3072
RAW_G = RAW_F + N_HEADS
NA_RAW = 4112
EPS = 1e-6
QSCALE = 0.125
ROPE_THETA = 500000.0
ROT = 16
WIN = 128
TQ = 256
TK = 256
KS = TQ // 2
HPS = 8
HW = HPS * HD
TM = 256
RT = 256
RB = 512
CB = 256
LANES = 128

ADAM_LR, ADAM_B1, ADAM_B2, ADAM_EPS, ADAM_WD, ADAM_STEP = 0.001, 0.9, 0.999, 1e-08, 0.01, 10

VMEM_LIMIT = 56 * 1024 * 1024


def _params():
    return pltpu.CompilerParams(vmem_limit_bytes=VMEM_LIMIT)


def _sds(shape, dtype):
    return jax.ShapeDtypeStruct(shape, dtype)


def _dot_nt(a, b):
    return lax.dot_general(a, b, (((1,), (1,)), ((), ())), preferred_element_type=F32)


def _dot_tn(a, b):
    return lax.dot_general(a, b, (((0,), (0,)), ((), ())), preferred_element_type=F32)


def _dot_nn(a, b):
    return lax.dot_general(a, b, (((1,), (0,)), ((), ())), preferred_element_type=F32)


def _sigmoid(g):
    return 1.0 / (1.0 + jnp.exp(-g))


def _lane_iota(shape):
    return lax.broadcasted_iota(jnp.int32, shape, len(shape) - 1)


def _flips(kind):
    return (2, 4, 6) if kind == "a2a_chips" else tuple(range(1, N_DEV))


def _send_view(kind, ref, dev):
    if kind in ("gather_rows", "gather_cols"):
        return ref
    if kind == "a2a_slots":
        return ref.at[dev]
    if kind == "a2a_chips":
        return ref.at[dev >> 1]
    if kind == "a2a_rows":
        rows = ref.shape[0] // N_DEV
        return ref.at[pl.ds(pl.multiple_of(dev * rows, rows), rows)]
    cols = ref.shape[1] // N_DEV
    return ref.at[:, pl.ds(pl.multiple_of(dev * cols, cols), cols)]


def _land_view(kind, ref, dev):
    if kind == "gather_cols":
        cols = ref.shape[1] // N_DEV
        return ref.at[:, pl.ds(pl.multiple_of(dev * cols, cols), cols)]
    if kind == "a2a_chips":
        return ref.at[dev >> 1]
    return ref.at[dev]


def _landing_sds(kind, arr):
    if kind == "gather_rows":
        return _sds((N_DEV,) + arr.shape, arr.dtype)
    if kind == "gather_cols":
        return _sds((arr.shape[0], N_DEV * arr.shape[1]), arr.dtype)
    if kind == "a2a_rows":
        return _sds((N_DEV, arr.shape[0] // N_DEV, arr.shape[1]), arr.dtype)
    if kind == "a2a_cols":
        return _sds((N_DEV, arr.shape[0], arr.shape[1] // N_DEV), arr.dtype)
    return _sds(arr.shape, arr.dtype)


def _exchange_sems(n_parts):
    n = n_parts * (N_DEV - 1)
    return [pltpu.SemaphoreType.DMA((n,)), pltpu.SemaphoreType.DMA((n,)), pltpu.SemaphoreType.DMA((n_parts,))]


def _exchange_ops(kinds, srcs, dsts, sems, start, wait):
    send_sems, recv_sems, local_sems = sems
    x, y, c = lax.axis_index("x"), lax.axis_index("y"), lax.axis_index("c")
    me = 4 * x + 2 * y + c

    def local(a):
        return pltpu.make_async_copy(_send_view(kinds[a], srcs[a], me), _land_view(kinds[a], dsts[a], me),
                                     local_sems.at[a])

    def remote(a, k, landing_dev):
        peer = (x ^ ((k >> 2) & 1), y ^ ((k >> 1) & 1), c ^ (k & 1))
        sem = a * (N_DEV - 1) + k - 1
        return pltpu.make_async_remote_copy(
            src_ref=_send_view(kinds[a], srcs[a], me ^ k), dst_ref=_land_view(kinds[a], dsts[a], landing_dev),
            send_sem=send_sems.at[sem], recv_sem=recv_sems.at[sem], device_id=peer,
            device_id_type=pl.DeviceIdType.MESH)

    pairs = [(a, k) for k in range(1, N_DEV) for a in range(len(kinds)) if k in _flips(kinds[a])]
    if start:
        for a in range(len(kinds)):
            local(a).start()
        for a, k in pairs:
            remote(a, k, me).start()
    if wait:
        for a, k in pairs:
            remote(a, k, me ^ k).wait_recv()
            remote(a, k, me).wait_send()
        for a in range(len(kinds)):
            local(a).wait()


def _gather_two_level(srcs, landing_of, sems, meanwhile=None):
    send_sems, recv_sems, local_sems = sems
    x, y, c = lax.axis_index("x"), lax.axis_index("y"), lax.axis_index("c")
    me, sibling = (x, y, c), (x, y, 1 - c)
    chips = [(1 - x, y), (x, 1 - y), (1 - x, 1 - y)]

    def slot(a, dev):
        return landing_of[a](4 * dev[0] + 2 * dev[1] + dev[2])

    def copy(a, k, block, to, src=None):
        return pltpu.make_async_remote_copy(
            src_ref=slot(a, block) if src is None else src, dst_ref=slot(a, block),
            send_sem=send_sems.at[a * (N_DEV - 1) + k], recv_sem=recv_sems.at[a * (N_DEV - 1) + k],
            device_id=to, device_id_type=pl.DeviceIdType.MESH)

    parts = range(len(srcs))
    mine = [pltpu.make_async_copy(srcs[a], slot(a, me), local_sems.at[a]) for a in parts]
    first = [copy(a, 1 + j, me, (*chip, c), src=srcs[a]) for a in parts for j, chip in enumerate(chips)]
    first += [copy(a, 0, me, sibling, src=srcs[a]) for a in parts]
    for cp in first + mine:
        cp.start()
    if meanwhile is not None:
        meanwhile()
    passed = []
    for a in parts:
        for j, chip in enumerate(chips):
            copy(a, 1 + j, (*chip, c), me).wait_recv()
            fwd = copy(a, 4 + j, (*chip, c), sibling)
            fwd.start()
            passed.append(fwd)
    for a in parts:
        copy(a, 0, sibling, me).wait_recv()
        for j, chip in enumerate(chips):
            copy(a, 4 + j, (*chip, 1 - c), me).wait_recv()
    for cp in first + passed:
        cp.wait_send()
    for cp in mine:
        cp.wait()


def _call(body, *, name, args, in_specs, out_specs, out_shape, grid=(), scratch_shapes=(), aliases=None, rider=()):
    n_in, n_out, n_scr, n_r = len(in_specs), len(out_specs), len(scratch_shapes), len(rider)
    kinds = [kind for kind, _ in rider]

    def kernel_body(*refs):
        c_in, r_in = refs[:n_in], refs[n_in:n_in + n_r]
        c_out = refs[n_in + n_r:n_in + n_r + n_out]
        r_out = refs[n_in + n_r + n_out:n_in + 2 * n_r + n_out]
        rest = refs[n_in + 2 * n_r + n_out:]
        c_scr, sems = rest[:n_scr], rest[n_scr:]
        if n_r:
            assert grid, "a rider needs a gridded call"
            ids = [pl.program_id(ax) for ax in range(len(grid))]
            first, last = ids[0] == 0, ids[0] == grid[0] - 1
            for pid, size in zip(ids[1:], grid[1:]):
                first = first & (pid == 0)
                last = last & (pid == size - 1)
            pl.when(first)(lambda: _exchange_ops(kinds, r_in, r_out, sems, True, False))
        body(*c_in, *c_out, *c_scr)
        if n_r:
            pl.when(last)(lambda: _exchange_ops(kinds, r_in, r_out, sems, False, True))

    anyspec = pl.BlockSpec(memory_space=pl.ANY)
    params = pltpu.CompilerParams(vmem_limit_bytes=VMEM_LIMIT, has_side_effects=bool(n_r))
    outs = pl.pallas_call(
        kernel_body, name=name, grid=grid, in_specs=list(in_specs) + [anyspec] * n_r,
        out_specs=list(out_specs) + [anyspec] * n_r,
        out_shape=list(out_shape) + [_landing_sds(kind, arr) for kind, arr in rider],
        scratch_shapes=list(scratch_shapes) + (_exchange_sems(n_r) if n_r else []),
        input_output_aliases=aliases or {}, compiler_params=params)(*args, *[arr for _, arr in rider])
    return list(outs)


def _mm(a, b, mode, tm, tn, tk, out_dtype=F32, add=None, name="mm", rider=()):
    if mode == "nn":
        (m, k), n = a.shape, b.shape[1]
        a_spec = pl.BlockSpec((tm, tk), lambda i, j, kk: (i, kk))
        b_spec = pl.BlockSpec((tk, tn), lambda i, j, kk: (kk, j))
        dot = _dot_nn
    elif mode == "nt":
        (m, k), n = a.shape, b.shape[0]
        a_spec = pl.BlockSpec((tm, tk), lambda i, j, kk: (i, kk))
        b_spec = pl.BlockSpec((tn, tk), lambda i, j, kk: (j, kk))
        dot = _dot_nt
    else:
        (k, m), n = a.shape, b.shape[1]
        a_spec = pl.BlockSpec((tk, tm), lambda i, j, kk: (kk, i))
        b_spec = pl.BlockSpec((tk, tn), lambda i, j, kk: (kk, j))
        dot = _dot_tn
    assert m % tm == 0 and n % tn == 0 and k % tk == 0, (m, n, k, tm, tn, tk)
    nk = k // tk
    has_add = add is not None

    def body(*refs):
        if has_add:
            a_ref, b_ref, add_ref, o_ref, acc = refs
        else:
            a_ref, b_ref, o_ref, acc = refs
        p = dot(a_ref[...].astype(BF16), b_ref[...].astype(BF16))

        def finish(total):
            if has_add:
                total = add_ref[...] + total
            o_ref[...] = total.astype(out_dtype)

        if nk == 1:
            finish(p)
        else:
            kk = pl.program_id(2)

            @pl.when(kk == 0)
            def _():
                acc[...] = p

            @pl.when(kk > 0)
            def _():
                acc[...] += p

            @pl.when(kk == nk - 1)
            def _():
                finish(acc[...])

    in_specs = [a_spec, b_spec]
    args = [a, b]
    if has_add:
        in_specs.append(pl.BlockSpec((tm, tn), lambda i, j, kk: (i, j)))
        args.append(add)
    acc_shape = (tm, tn) if nk > 1 else (8, LANES)
    outs = _call(body, name=name, args=args, grid=(m // tm, n // tn, nk), in_specs=in_specs,
                 out_specs=[pl.BlockSpec((tm, tn), lambda i, j, kk: (i, j))], out_shape=[_sds((m, n), out_dtype)],
                 scratch_shapes=[pltpu.VMEM(acc_shape, F32)], rider=rider)
    return outs if rider else outs[0]


def _rms_rinv(x):
    return lax.rsqrt(jnp.mean(x * x, axis=-1, keepdims=True) + EPS)


def _rms_bwd_core(du, x, g):
    r = _rms_rinv(x)
    dug = du * g
    dx = r * (dug - x * ((r * r) * jnp.mean(dug * x, axis=-1, keepdims=True)))
    dg = jnp.sum(du * (x * r), axis=0, keepdims=True)
    return dx, dg


def _half_ones():
    r = lax.broadcasted_iota(jnp.int32, (LANES, LANES), 0)
    c = lax.broadcasted_iota(jnp.int32, (LANES, LANES), 1)
    return ((r < HD) == (c < HD)).astype(BF16)


def _half_sum(v, lo_half):
    if lo_half.dtype == jnp.bool_:
        s0 = jnp.sum(jnp.where(lo_half, v, 0.0), axis=-1, keepdims=True)
        s1 = jnp.sum(jnp.where(lo_half, 0.0, v), axis=-1, keepdims=True)
        return jnp.where(lo_half, s0, s1)
    hi = v.astype(BF16)
    lo = (v - hi.astype(F32)).astype(BF16)
    return _dot_nn(hi, lo_half) + _dot_nn(lo, lo_half)


def _head_rinv(x, lo_half):
    return lax.rsqrt(_half_sum(x * x, lo_half) * (1.0 / HD) + EPS)


def _head_norm_bwd(dn, x, g, lo_half):
    r = _head_rinv(x, lo_half)
    dng = dn * g
    dx = r * (dng - x * ((r * r) * (_half_sum(dng * x, lo_half) * (1.0 / HD))))
    dg = jnp.sum(dn * (x * r), axis=0, keepdims=True)
    return dx, dg


def _rope_swap(x, lane):
    l64 = lane & (HD - 1)
    return jnp.where(l64 < ROT // 2, pltpu.roll(x, LANES - ROT // 2, 1), pltpu.roll(x, ROT // 2, 1))


def _rope_fwd(x, cos, sin, lane):
    return x * cos + _rope_swap(x, lane) * sin


def _rope_bwd(dy, cos, sin, lane):
    return dy * cos + jnp.where((lane & (HD - 1)) < ROT, _rope_swap(dy * sin, lane), 0.0)


def _g2(g_ref):
    g = g_ref[...]
    return jnp.concatenate([g, g], axis=-1)


def _pairs(width):
    return [slice(LANES * c, LANES * (c + 1)) for c in range(width // LANES)]


def _pick_lane(block, lane, idx):
    return jnp.sum(jnp.where(lane == idx, block, 0.0), axis=-1, keepdims=True)


def _head_a(x, g, wa, gq, gk, b_pad):
    def body(x_ref, g_ref, w_ref, gq_ref, gk_ref, b_ref, u_ref, p_ref, qo_ref, ko_ref, vo_ref, c_ref, cbc_ref, carry):
        @pl.when(pl.program_id(0) == 0)
        def _():
            carry[...] = jnp.zeros_like(carry)

        xv = x_ref[...]
        u = ((xv * _rms_rinv(xv)) * g_ref[...]).astype(BF16)
        u_ref[...] = u
        for lo in range(0, NA, D):
            hi = min(lo + D, NA)
            p_ref[:, lo:hi] = _dot_nn(u, w_ref[:, lo:hi])
        lane = _lane_iota((RT, LANES))
        lo_half = lane < HD
        gq2, gk2 = _g2(gq_ref), _g2(gk_ref)
        for c in _pairs(D):
            q = p_ref[:, c]
            k = p_ref[:, D + c.start:D + c.stop]
            qo_ref[:, c] = (((q * _head_rinv(q, lo_half)) * gq2) * QSCALE).astype(BF16)
            ko_ref[:, c] = ((k * _head_rinv(k, lo_half)) * gk2).astype(BF16)
        vo_ref[...] = p_ref[:, 2 * D:3 * D].astype(BF16)

        z = p_ref[:, FOFF:FOFF + LANES] + b_ref[...]
        logf = jnp.minimum(z, 0.0) - jnp.log1p(jnp.exp(-jnp.abs(z)))
        r = lax.broadcasted_iota(jnp.int32, (RT, RT), 0)
        cc = lax.broadcasted_iota(jnp.int32, (RT, RT), 1)
        tri = (r >= cc).astype(F32)
        loc = jnp.dot(tri, logf, precision=lax.Precision.HIGHEST, preferred_element_type=F32) + carry[0:1, :]
        c_ref[...] = loc
        carry[0:1, :] = loc[RT - 1:RT, :]
        for h in range(N_HEADS):
            cbc_ref[:, LANES * h:LANES * (h + 1)] = jnp.broadcast_to(_pick_lane(loc, lane, h), (RT, LANES))

    row = lambda width: pl.BlockSpec((RT, width), lambda i: (i, 0))
    whole = lambda arr: pl.BlockSpec(arr.shape, lambda i: (0,) * arr.ndim, pipeline_mode=pl.Buffered(1))
    return pl.pallas_call(
        body, name="head_a", grid=(S // RT,),
        in_specs=[row(D), whole(g), whole(wa), whole(gq), whole(gk), whole(b_pad)],
        out_specs=[row(D), row(NA), row(D), row(D), row(D), row(LANES), row(N_HEADS * LANES)],
        out_shape=[_sds((S, D), BF16), _sds((S, NA), F32)] + [_sds((S, D), BF16)] * 3
        + [_sds((S, LANES), F32), _sds((S, N_HEADS * LANES), F32)],
        scratch_shapes=[pltpu.VMEM((8, LANES), F32)], compiler_params=_params())(x, g, wa, gq, gk, b_pad)


def _key_le_query(offset, keys=TK):
    r = lax.broadcasted_iota(jnp.int32, (keys, TQ), 0)
    c = lax.broadcasted_iota(jnp.int32, (keys, TQ), 1)
    return (r + offset) <= c


def _widen(tile):
    return jnp.concatenate([tile] * (TQ // LANES), axis=1)


def _fox_fwd(qn, kn, vb, proj, crow, cbc, rider=()):
    nq = S // TQ

    def body(q_ref, k_ref, v_ref, g_ref, cq_ref, cbc_ref, o_ref, z_ref, lse_ref, st_s, pt_s):
        i = pl.program_id(1)
        qs = [q_ref[:, HD * hh:HD * (hh + 1)] for hh in range(HPS)]
        cqs = [cq_ref[hh, 0] for hh in range(HPS)]

        def scores(s, hh):
            off = pl.multiple_of(s * KS, KS)
            kj = k_ref[pl.ds(off, KS), HD * hh:HD * (hh + 1)]
            return (_dot_nt(kj, qs[hh]) + cqs[hh]) - _widen(cbc_ref[pl.ds(off, KS), LANES * hh:LANES * (hh + 1)])

        def values(s, hh, pt):
            off = pl.multiple_of(s * KS, KS)
            return _dot_tn(v_ref[pl.ds(off, KS), HD * hh:HD * (hh + 1)], pt)

        def step(s, slot, carries, mask=None, last=False):
            if not last:
                for hh in range(HPS):
                    st_s[1 - slot, hh] = scores(s + 1, hh)
            pvs = [values(jnp.maximum(s - 1, 0), hh, pt_s[1 - slot, hh]) for hh in range(HPS)]
            out = []
            for hh in range(HPS):
                m, l, acc = carries[hh]
                st = st_s[slot, hh]
                if mask is not None:
                    st = jnp.where(mask, st, -jnp.inf)
                m_new = jnp.maximum(m, jnp.max(st, axis=0, keepdims=True))
                pt = jnp.exp(st - m_new)
                alpha = jnp.exp(m - m_new)
                pt_s[slot, hh] = pt.astype(BF16)
                out.append((m_new, alpha * l + jnp.sum(pt, axis=0, keepdims=True), alpha * (acc + pvs[hh])))
            return tuple(out)

        for hh in range(HPS):
            st_s[0, hh] = scores(0, hh)
            pt_s[1, hh] = jnp.zeros((KS, TQ), BF16)
        one = (jnp.full((1, TQ), -jnp.inf, F32), jnp.zeros((1, TQ), F32), jnp.zeros((HD, TQ), F32))
        carries = lax.fori_loop(0, i, lambda t, cr: step(2 * t + 1, 1, step(2 * t, 0, cr)), (one,) * HPS)
        carries = step(2 * i, 0, carries, mask=_key_le_query(0, KS))
        carries = step(2 * i + 1, 1, carries, mask=_key_le_query(KS, KS), last=True)
        accs = []
        for hh in range(HPS):
            m, l, acc = carries[hh]
            acc = acc + values(2 * i + 1, hh, pt_s[1, hh])
            accs.append(acc / l)
            lse_ref[hh, 0] = m + jnp.log(l)
        o = jnp.concatenate(accs, axis=0).T
        o_ref[...] = o
        g = g_ref[...]
        z_ref[...] = (o * (g * _sigmoid(g))).astype(BF16)

    qblk = pl.BlockSpec((TQ, HW), lambda hp, i: (i, hp))
    full = pl.BlockSpec((S, HW), lambda hp, i: (0, hp))
    rows = pl.BlockSpec((HPS, 1, 1, TQ), lambda hp, i: (hp, i, 0, 0))
    return _call(
        body, name="fox_fwd", args=(qn, kn, vb, proj, crow, cbc), grid=(N_HEADS // HPS, nq),
        in_specs=[qblk, full, full,
                  pl.BlockSpec((TQ, HW), lambda hp, i: (i, GOFF // HW + hp)),
                  rows, pl.BlockSpec((S, HPS * LANES), lambda hp, i: (0, hp))],
        out_specs=[qblk, qblk, rows],
        out_shape=[_sds((S, D), F32), _sds((S, D), BF16), _sds((N_HEADS, nq, 1, TQ), F32)],
        scratch_shapes=[pltpu.VMEM((2, HPS, KS, TQ), F32), pltpu.VMEM((2, HPS, KS, TQ), BF16)], rider=rider)


def _fox_bwd_pre(dh, w_out, proj, o):
    nq, rows = S // TQ, 2 * TQ
    per = rows // TQ

    def body(dh_ref, w_ref, g_ref, o_ref, do_ref, dg_ref, delta_ref):
        g = g_ref[...]
        sg = _sigmoid(g)
        dzv = _dot_nt(dh_ref[...].astype(BF16), w_ref[...])
        ov = o_ref[...]
        do = dzv * (g * sg)
        dg_ref[...] = (dzv * ov * (sg * (1.0 + g * (1.0 - sg)))).astype(BF16)
        do_ref[...] = do.astype(BF16)
        prod_t = (do * ov).T
        for h in range(N_HEADS):
            for b in range(per):
                delta_ref[h, b] = jnp.sum(prod_t[HD * h:HD * (h + 1), TQ * b:TQ * (b + 1)], axis=0, keepdims=True)

    row = pl.BlockSpec((rows, D), lambda i: (i, 0))
    return pl.pallas_call(
        body, name="fox_bwd_pre", grid=(S // rows,),
        in_specs=[row, pl.BlockSpec(w_out.shape, lambda i: (0, 0), pipeline_mode=pl.Buffered(1)),
                  pl.BlockSpec((rows, D), lambda i: (i, GOFF // D)), row],
        out_specs=[row, row, pl.BlockSpec((N_HEADS, per, 1, TQ), lambda i: (0, i, 0, 0))],
        out_shape=[_sds((S, D), BF16), _sds((S, D), BF16), _sds((N_HEADS, nq, 1, TQ), F32)],
        compiler_params=_params())(dh, w_out, proj, o)


def _fox_bwd(qn, kn, vb, dob, lse, delta, crow, cbc, rider=()):
    nq, nkb = S // TQ, S // TK

    def body(q_ref, k_ref, v_ref, do_ref, lse_ref, del_ref, cq_ref, cbc_ref,
             dk_ref, dv_ref, dcs_ref, dq_ref, dr_ref, st_s, dp_s, pt_s, ds_s, dq_acc, dr_acc):
        j = pl.program_id(1)

        @pl.when(j == 0)
        def _():
            dq_acc[...] = jnp.zeros_like(dq_acc)
            dr_acc[...] = jnp.zeros_like(dr_acc)

        kjs = [k_ref[:, HD * hh:HD * (hh + 1)] for hh in range(HPS)]
        vjs = [v_ref[:, HD * hh:HD * (hh + 1)] for hh in range(HPS)]

        def rows_of(ref, u, hh):
            off = pl.multiple_of(u * TQ, TQ)
            return ref[pl.ds(off, TQ), HD * hh:HD * (hh + 1)]

        def products(u, hh):
            st = (_dot_nt(kjs[hh], rows_of(q_ref, u, hh)) + cq_ref[hh, u]) - _widen(
                cbc_ref[:, LANES * hh:LANES * (hh + 1)])
            return st, _dot_nt(vjs[hh], rows_of(do_ref, u, hh))

        def step(u, slot, carries, masked=False):
            nxt = jnp.minimum(u + 1, nq - 1)
            for hh in range(HPS):
                st_s[1 - slot, hh], dp_s[1 - slot, hh] = products(nxt, hh)
            prev = jnp.maximum(u - 1, 0)
            dvs = [_dot_nn(pt_s[1 - slot, hh], rows_of(do_ref, prev, hh)) for hh in range(HPS)]
            dks = [_dot_nn(ds_s[1 - slot, hh], rows_of(q_ref, prev, hh)) for hh in range(HPS)]
            for hh in range(HPS):
                dq_acc[hh, prev] += _dot_tn(kjs[hh], ds_s[1 - slot, hh])
            out = []
            for hh in range(HPS):
                dk, dv, dcs = carries[hh]
                st = st_s[slot, hh]
                if masked:
                    st = jnp.where(_key_le_query((j - u) * TQ), st, -jnp.inf)
                pt = jnp.exp(st - lse_ref[hh, u])
                dst = pt * (dp_s[slot, hh] - del_ref[hh, u])
                pt_s[slot, hh] = pt.astype(BF16)
                ds_s[slot, hh] = dst.astype(BF16)
                dr_acc[hh, u] += jnp.sum(dst, axis=0, keepdims=True)
                out.append((dk + dks[hh], dv + dvs[hh], dcs + (dst[:, :LANES] + dst[:, LANES:])))
            return tuple(out)

        t0 = j // 2
        for hh in range(HPS):
            st_s[0, hh], dp_s[0, hh] = products(2 * t0, hh)
            pt_s[1, hh] = jnp.zeros((TK, TQ), BF16)
            ds_s[1, hh] = jnp.zeros((TK, TQ), BF16)
        one = (jnp.zeros((TK, HD), F32), jnp.zeros((TK, HD), F32), jnp.zeros((TK, LANES), F32))
        carries = step(2 * t0 + 1, 1, step(2 * t0, 0, (one,) * HPS, masked=True), masked=True)
        carries = lax.fori_loop(t0 + 1, nq // 2, lambda t, cr: step(2 * t + 1, 1, step(2 * t, 0, cr)), carries)
        dks, dvs = [], []
        lane = _lane_iota((TK, LANES))
        dcs_all = jnp.zeros((TK, LANES), F32)
        for hh in range(HPS):
            dk, dv, dcs = carries[hh]
            dks.append(dk + _dot_nn(ds_s[1, hh], rows_of(q_ref, nq - 1, hh)))
            dvs.append(dv + _dot_nn(pt_s[1, hh], rows_of(do_ref, nq - 1, hh)))
            dq_acc[hh, nq - 1] += _dot_tn(kjs[hh], ds_s[1, hh])
            dcs_all = jnp.where(lane == HPS * pl.program_id(0) + hh, -jnp.sum(dcs, axis=-1, keepdims=True), dcs_all)
        dcs_ref[0] = dcs_all
        dk_ref[...] = jnp.concatenate(dks, axis=-1)
        dv_ref[...] = jnp.concatenate(dvs, axis=-1).astype(BF16)

        @pl.when(j == nkb - 1)
        def _():
            for i in range(nq):
                dq_ref[TQ * i:TQ * (i + 1), :] = jnp.concatenate([dq_acc[hh, i] for hh in range(HPS)], axis=0).T
            dr_ref[...] = dr_acc[...]

    kblk = pl.BlockSpec((TK, HW), lambda hp, j: (j, hp))
    full = pl.BlockSpec((S, HW), lambda hp, j: (0, hp))
    rows = pl.BlockSpec((HPS, nq, 1, TQ), lambda hp, j: (hp, 0, 0, 0))
    cblk = pl.BlockSpec((TK, HPS * LANES), lambda hp, j: (j, hp))
    return _call(
        body, name="fox_bwd", args=(qn, kn, vb, dob, lse, delta, crow, cbc), grid=(N_HEADS // HPS, nkb),
        in_specs=[full, kblk, kblk, full, rows, rows, rows, cblk],
        out_specs=[kblk, kblk, pl.BlockSpec((1, TK, LANES), lambda hp, j: (hp, j, 0)), full, rows],
        out_shape=[_sds((S, D), F32), _sds((S, D), BF16), _sds((N_HEADS // HPS, S, LANES), F32), _sds((S, D), F32),
                   _sds((N_HEADS, nq, 1, TQ), F32)],
        scratch_shapes=[pltpu.VMEM((2, HPS, TK, TQ), F32), pltpu.VMEM((2, HPS, TK, TQ), F32),
                        pltpu.VMEM((2, HPS, TK, TQ), BF16), pltpu.VMEM((2, HPS, TK, TQ), BF16),
                        pltpu.VMEM((HPS, nq, HD, TQ), F32), pltpu.VMEM((HPS, nq, 1, TQ), F32)], rider=rider)


def _prep_a_bwd(dq, dk, dv, dgate, drow, dcs, proj, b_pad, gq, gk):
    nt = S // TM

    def body(dq_ref, dk_ref, dv_ref, dgt_ref, dr_ref, dcs_ref, xq_ref, xk_ref, f_ref, b_ref, gq_ref, gk_ref,
             o_ref, dgq_ref, dgk_ref, db_ref, carry):
        @pl.when(pl.program_id(0) == 0)
        def _():
            carry[...] = jnp.zeros_like(carry)
            dgq_ref[...] = jnp.zeros_like(dgq_ref)
            dgk_ref[...] = jnp.zeros_like(dgk_ref)
            db_ref[...] = jnp.zeros_like(db_ref)

        lane = _lane_iota((TM, LANES))
        lo_half = _half_ones()
        gq2, gk2 = _g2(gq_ref), _g2(gk_ref)
        dgq, dgk = jnp.zeros((1, LANES), F32), jnp.zeros((1, LANES), F32)
        for c in _pairs(D):
            dxq, dg = _head_norm_bwd(dq_ref[:, c] * QSCALE, xq_ref[:, c], gq2, lo_half)
            o_ref[:, c] = dxq.astype(BF16)
            dgq = dgq + dg
            dxk, dg = _head_norm_bwd(dk_ref[:, c], xk_ref[:, c], gk2, lo_half)
            o_ref[:, D + c.start:D + c.stop] = dxk.astype(BF16)
            dgk = dgk + dg
        dgq_ref[...] += dgq
        dgk_ref[...] += dgk
        o_ref[:, 2 * D:3 * D] = dv_ref[...]
        o_ref[:, GOFF:GOFF + D] = dgt_ref[...]

        dc = jnp.concatenate([dr_ref[:, 0, :], jnp.zeros((LANES - N_HEADS, TM), F32)], axis=0).T
        for group in range(N_HEADS // HPS):
            dc = dc + dcs_ref[group]
        r = lax.broadcasted_iota(jnp.int32, (TM, TM), 0)
        c = lax.broadcasted_iota(jnp.int32, (TM, TM), 1)
        tri = (c >= r).astype(F32)
        dlogf = jnp.dot(tri, dc, precision=lax.Precision.HIGHEST, preferred_element_type=F32) + carry[0:1, :]
        carry[0:1, :] = dlogf[0:1, :]
        df = dlogf * (1.0 / (1.0 + jnp.exp(f_ref[...] + b_ref[...])))
        db_ref[...] += jnp.sum(df, axis=0, keepdims=True)
        o_ref[:, FOFF:FOFF + LANES] = df.astype(BF16)
        o_ref[:, FOFF + LANES:NA] = jnp.zeros((TM, NA - FOFF - LANES), BF16)

    rev = lambda width, col: pl.BlockSpec((TM, width), lambda i: (nt - 1 - i, col))
    gspec = pl.BlockSpec((1, HD), lambda i: (0, 0))
    acc = pl.BlockSpec((1, LANES), lambda i: (0, 0))
    return pl.pallas_call(
        body, name="prep_a_bwd", grid=(nt,),
        in_specs=[rev(D, 0), rev(D, 0), rev(D, 0), rev(D, 0),
                  pl.BlockSpec((N_HEADS, None, 1, TM), lambda i: (0, nt - 1 - i, 0, 0)),
                  pl.BlockSpec((N_HEADS // HPS, TM, LANES), lambda i: (0, nt - 1 - i, 0)),
                  rev(D, 0), rev(D, 1), rev(LANES, FOFF // LANES), acc, gspec, gspec],
        out_specs=[rev(NA, 0), acc, acc, acc],
        out_shape=[_sds((S, NA), BF16)] + [_sds((1, LANES), F32)] * 3,
        scratch_shapes=[pltpu.VMEM((8, LANES), F32)],
        compiler_params=_params())(dq, dk, dv, dgate, drow, dcs, proj, proj, proj, b_pad, gq, gk)


def _head_b(x, z_a, w_out_a, g_kv, g_b, w_kv, w_in_b, gq, gk, cos2, sin2):
    nkv = w_kv.shape[1] // 2

    def body(x_ref, z_ref, wo_ref, gkv_ref, gb_ref, wkv_ref, wb_ref, gq_ref, gk_ref, c_ref, s_ref,
             h_ref, ukv_ref, ub_ref, kv_ref, pb_ref, qo_ref, ko_ref, vo_ref):
        xv = x_ref[...] + _dot_nn(z_ref[...], wo_ref[...])
        h_ref[...] = xv
        xn = xv * _rms_rinv(xv)
        ukv = (xn * gkv_ref[...]).astype(BF16)
        ub = (xn * gb_ref[...]).astype(BF16)
        ukv_ref[...] = ukv
        ub_ref[...] = ub
        kv_ref[...] = _dot_nn(ukv, wkv_ref[...])
        for lo in range(0, 2 * D, D):
            pb_ref[:, lo:lo + D] = _dot_nn(ub, wb_ref[:, lo:lo + D])
        lane = _lane_iota((RT, LANES))
        lo_half = lane < HD
        cos, sin = c_ref[...], s_ref[...]
        gq2, gk2 = _g2(gq_ref), _g2(gk_ref)
        for c in _pairs(D):
            q = pb_ref[:, c]
            qo_ref[:, c] = (_rope_fwd((q * _head_rinv(q, lo_half)) * gq2, cos, sin, lane) * QSCALE).astype(BF16)
        for c in _pairs(nkv):
            k = kv_ref[:, c]
            ko_ref[:, c] = _rope_fwd((k * _head_rinv(k, lo_half)) * gk2, cos, sin, lane).astype(BF16)
        vo_ref[...] = kv_ref[:, nkv:2 * nkv].astype(BF16)

    row = lambda width: pl.BlockSpec((RT, width), lambda i: (i, 0))
    whole = lambda arr: pl.BlockSpec(arr.shape, lambda i: (0,) * arr.ndim, pipeline_mode=pl.Buffered(1))
    return pl.pallas_call(
        body, name="head_b", grid=(S // RT,),
        in_specs=[row(D), row(D), whole(w_out_a), whole(g_kv), whole(g_b), whole(w_kv), whole(w_in_b), whole(gq),
                  whole(gk), row(LANES), row(LANES)],
        out_specs=[row(D), row(D), row(D), row(2 * nkv), row(2 * D), row(D), row(nkv), row(nkv)],
        out_shape=[_sds((S, D), F32), _sds((S, D), BF16), _sds((S, D), BF16), _sds((S, 2 * nkv), F32),
                   _sds((S, 2 * D), F32), _sds((S, D), BF16), _sds((S, nkv), BF16), _sds((S, nkv), BF16)],
        compiler_params=_params())(x, z_a, w_out_a, g_kv, g_b, w_kv, w_in_b, gq, gk, cos2, sin2)


N_KV, GRP = 4, 4


def _swa_mask(n):
    r = lax.broadcasted_iota(jnp.int32, (2 * WIN, GRP * WIN), 0)
    q = lax.broadcasted_iota(jnp.int32, (2 * WIN, GRP * WIN), 1) & (WIN - 1)
    return (r > q) & (r <= q + WIN) & ((r >= WIN) | (n > 0))


def _stack4(ref_or_val, base):
    return jnp.concatenate([ref_or_val[:, base + HD * g: base + HD * (g + 1)] for g in range(GRP)], axis=0)


def _unstack4(xt):
    return jnp.concatenate([xt[:, WIN * g:WIN * (g + 1)] for g in range(GRP)], axis=0).T


def _band(prev_ref, cur_ref, kh):
    return jnp.concatenate([prev_ref[:, HD * kh:HD * (kh + 1)], cur_ref[:, HD * kh:HD * (kh + 1)]], axis=0)


def _sink_row(s_ref, first):
    lane = _lane_iota((1, GRP * WIN))
    row = jnp.full((1, GRP * WIN), s_ref[first + GRP - 1], F32)
    for g in range(GRP - 2, -1, -1):
        row = jnp.where(lane < WIN * (g + 1), s_ref[first + g], row)
    return row


def _swa_fwd(qb, ksh, vsh, pb, sinks):
    nb = S // WIN

    def body(q_ref, kp_ref, kc_ref, vp_ref, vc_ref, g_ref, s_ref, o_ref, z_ref, lse_ref):
        n = pl.program_id(0)
        valid = _swa_mask(n)
        outs = []
        for kh in range(N_KV):
            kb, vb = _band(kp_ref, kc_ref, kh), _band(vp_ref, vc_ref, kh)
            st = jnp.where(valid, _dot_nt(kb, _stack4(q_ref, GRP * HD * kh)), -jnp.inf)
            sink = _sink_row(s_ref, GRP * kh)
            m = jnp.maximum(jnp.max(st, axis=0, keepdims=True), sink)
            pt = jnp.exp(st - m)
            l = jnp.sum(pt, axis=0, keepdims=True) + jnp.exp(sink - m)
            outs.append(_unstack4(_dot_tn(vb, pt.astype(BF16)) / l))
            lse = m + jnp.log(l)
            for g in range(GRP):
                lse_ref[GRP * kh + g, 0] = lse[:, WIN * g:WIN * (g + 1)]
        o = jnp.concatenate(outs, axis=-1)
        o_ref[...] = o
        g = g_ref[...]
        z_ref[...] = (o * (g * _sigmoid(g))).astype(BF16)

    row = pl.BlockSpec((WIN, D), lambda n: (n, 0))
    prev = pl.BlockSpec((WIN, N_KV * HD), lambda n: (jnp.maximum(n - 1, 0), 0))
    cur = pl.BlockSpec((WIN, N_KV * HD), lambda n: (n, 0))
    return pl.pallas_call(
        body, name="swa_fwd", grid=(nb,),
        in_specs=[row, prev, cur, prev, cur, pl.BlockSpec((WIN, D), lambda n: (n, 1)),
                  pl.BlockSpec(memory_space=pltpu.SMEM)],
        out_specs=[row, row, pl.BlockSpec((N_HEADS, 1, 1, WIN), lambda n: (0, n, 0, 0))],
        out_shape=[_sds((S, D), F32), _sds((S, D), BF16), _sds((N_HEADS, nb, 1, WIN), F32)],
        compiler_params=_params())(qb, ksh, ksh, vsh, vsh, pb, sinks)


def _swa_bwd(qb, ksh, vsh, dz, o, lse, pb, sinks, gq, cos2, sin2):
    nb = S // WIN

    def body(q_ref, kp_ref, kc_ref, vp_ref, vc_ref, dz_ref, o_ref, lse_ref, x_ref, g_ref, s_ref, gq_ref, c_ref, sn_ref,
             dpb_ref, dka_ref, dkb_ref, dva_ref, dvb_ref, dsink_ref, dgq_ref):
        n = pl.program_id(0)

        @pl.when(n == 0)
        def _():
            dsink_ref[...] = jnp.zeros_like(dsink_ref)
            dgq_ref[...] = jnp.zeros_like(dgq_ref)

        valid = _swa_mask(n)
        g = g_ref[...]
        sg = _sigmoid(g)
        dzv = dz_ref[...]
        ov = o_ref[...]
        do = dzv * (g * sg)
        dpb_ref[:, D:2 * D] = (dzv * ov * (sg * (1.0 + g * (1.0 - sg)))).astype(BF16)
        prod_t = (do * ov).T
        lane1 = _lane_iota((1, LANES))
        dqs, dkas, dkbs, dvas, dvbs = [], [], [], [], []
        dsink = jnp.zeros((1, LANES), F32)
        for kh in range(N_KV):
            kb, vb = _band(kp_ref, kc_ref, kh), _band(vp_ref, vc_ref, kh)
            base = GRP * HD * kh
            qs = _stack4(q_ref, base)
            dos = _stack4(do, base).astype(BF16)
            delta = jnp.concatenate(
                [jnp.sum(prod_t[base + HD * gg:base + HD * (gg + 1), :], axis=0, keepdims=True)
                 for gg in range(GRP)], axis=1)
            lse = jnp.concatenate([lse_ref[GRP * kh + gg, 0] for gg in range(GRP)], axis=1)
            st = jnp.where(valid, _dot_nt(kb, qs), -jnp.inf)
            pt = jnp.exp(st - lse)
            dst = pt * (_dot_nt(vb, dos) - delta)
            dsb = dst.astype(BF16)
            dqs.append(_unstack4(_dot_tn(kb, dsb)))
            dkband = _dot_nn(dsb, qs)
            dvband = _dot_nn(pt.astype(BF16), dos)
            dkbs.append(dkband[0:WIN, :])
            dkas.append(dkband[WIN:2 * WIN, :])
            dvbs.append(dvband[0:WIN, :])
            dvas.append(dvband[WIN:2 * WIN, :])
            ps_delta = jnp.exp(_sink_row(s_ref, GRP * kh) - lse) * delta
            for gg in range(GRP):
                val = jnp.sum(ps_delta[:, WIN * gg:WIN * (gg + 1)], axis=1, keepdims=True)
                dsink = dsink - jnp.where(lane1 == GRP * kh + gg, val, 0.0)
        dka_ref[...] = jnp.concatenate(dkas, axis=-1)
        dkb_ref[...] = jnp.concatenate(dkbs, axis=-1)
        dva_ref[...] = jnp.concatenate(dvas, axis=-1)
        dvb_ref[...] = jnp.concatenate(dvbs, axis=-1)
        dsink_ref[...] += dsink

        lane = _lane_iota((WIN, LANES))
        g2, cos, sin = _g2(gq_ref), c_ref[...], sn_ref[...]
        lo_half = _half_ones()
        dg_tot = jnp.zeros((1, LANES), F32)
        for kh in range(N_KV):
            for c in _pairs(GRP * HD):
                cols = slice(GRP * HD * kh + c.start, GRP * HD * kh + c.stop)
                dn = _rope_bwd(dqs[kh][:, c] * QSCALE, cos, sin, lane)
                dx, dg = _head_norm_bwd(dn, x_ref[:, cols], g2, lo_half)
                dpb_ref[:, cols] = dx.astype(BF16)
                dg_tot = dg_tot + dg
        dgq_ref[...] += dg_tot

    row = pl.BlockSpec((WIN, D), lambda n: (n, 0))
    prev = pl.BlockSpec((WIN, N_KV * HD), lambda n: (jnp.maximum(n - 1, 0), 0))
    cur = pl.BlockSpec((WIN, N_KV * HD), lambda n: (n, 0))
    acc = pl.BlockSpec((1, LANES), lambda n: (0, 0))
    tab = pl.BlockSpec((WIN, LANES), lambda n: (n, 0))
    return pl.pallas_call(
        body, name="swa_bwd", grid=(nb,),
        in_specs=[row, prev, cur, prev, cur, row, row, pl.BlockSpec((N_HEADS, 1, 1, WIN), lambda n: (0, n, 0, 0)),
                  row, pl.BlockSpec((WIN, D), lambda n: (n, 1)), pl.BlockSpec(memory_space=pltpu.SMEM),
                  pl.BlockSpec((1, HD), lambda n: (0, 0)), tab, tab],
        out_specs=[pl.BlockSpec((WIN, 2 * D), lambda n: (n, 0)), cur, cur, cur, cur, acc, acc],
        out_shape=[_sds((S, 2 * D), BF16)] + [_sds((S, 256), F32)] * 4 + [_sds((1, LANES), F32)] * 2,
        compiler_params=_params())(qb, ksh, ksh, vsh, vsh, dz, o, lse, pb, pb, sinks, gq, cos2, sin2)


def _prep_kv_bwd(dka, dkb, dva, dvb, kv, gk, cos2, sin2):
    nt = S // RB
    per = RB // WIN

    def shifted(cur_ref, nxt_ref, has_next):
        return jnp.concatenate([cur_ref[WIN:RB, :], jnp.where(has_next, nxt_ref[...], 0.0)], axis=0)

    def body(dka_ref, dkb_ref, dkn_ref, dva_ref, dvb_ref, dvn_ref, x_ref, g_ref, c_ref, s_ref, o_ref, dgk_ref):
        i, j = pl.program_id(0), pl.program_id(1)

        @pl.when((i == 0) & (j == 0))
        def _():
            dgk_ref[...] = jnp.zeros_like(dgk_ref)

        has_next = i < nt - 1

        @pl.when(j == 0)
        def _():
            lane = _lane_iota((RB, LANES))
            g2, cos, sin = _g2(g_ref), c_ref[...], s_ref[...]
            dy_all = dka_ref[...] + shifted(dkb_ref, dkn_ref, has_next)
            dg_tot = jnp.zeros((1, LANES), F32)
            lo_half = _half_ones()
            for c in _pairs(CB):
                dn = _rope_bwd(dy_all[:, c], cos, sin, lane)
                dx, dg = _head_norm_bwd(dn, x_ref[:, c], g2, lo_half)
                o_ref[:, c] = dx.astype(BF16)
                dg_tot = dg_tot + dg
            dgk_ref[...] += dg_tot

        @pl.when(j == 1)
        def _():
            o_ref[...] = (dva_ref[...] + shifted(dvb_ref, dvn_ref, has_next)).astype(BF16)

    cur = pl.BlockSpec((RB, CB), lambda i, j: (i, 0))
    nxt = pl.BlockSpec((WIN, CB), lambda i, j: (jnp.minimum(per * (i + 1), S // WIN - 1), 0))
    tab = pl.BlockSpec((RB, LANES), lambda i, j: (i, 0))
    return pl.pallas_call(
        body, name="prep_kv_bwd", grid=(nt, 2),
        in_specs=[cur, cur, nxt, cur, cur, nxt, cur, pl.BlockSpec((1, HD), lambda i, j: (0, 0)), tab, tab],
        out_specs=[pl.BlockSpec((RB, CB), lambda i, j: (i, j)), pl.BlockSpec((1, LANES), lambda i, j: (0, 0))],
        out_shape=[_sds((S, 2 * CB), BF16), _sds((1, LANES), F32)],
        compiler_params=_params())(dka, dkb, dkb, dva, dvb, dvb, kv, gk, cos2, sin2)


def _out_b_loss(z, w_out, h1, tgt):
    tm = 2 * RT

    def body(z_ref, w_ref, h_ref, t_ref, dy_ref, l_ref):
        @pl.when(pl.program_id(0) == 0)
        def _():
            l_ref[...] = jnp.zeros_like(l_ref)

        e = (h_ref[...] + _dot_nn(z_ref[...], w_ref[...])) - t_ref[...]
        dy_ref[...] = e * (1.0 / D)
        l_ref[...] += jnp.sum(jnp.sum(e * e, axis=-1, keepdims=True), axis=0, keepdims=True)

    row = pl.BlockSpec((tm, D), lambda i: (i, 0))
    return pl.pallas_call(
        body, name="out_b_loss", grid=(S // tm,),
        in_specs=[row, pl.BlockSpec(w_out.shape, lambda i: (0, 0), pipeline_mode=pl.Buffered(1)), row, row],
        out_specs=[row, pl.BlockSpec((1, LANES), lambda i: (0, 0))],
        out_shape=[_sds((S, D), F32), _sds((1, LANES), F32)], compiler_params=_params())(z, w_out, h1, tgt)


def _du_a_rms_bwd(dproj, wa, x, g, dres, after):
    tm, tk = 1024, NA // 2
    nk = NA // tk

    def body(a_ref, b_ref, x_ref, g_ref, dr_ref, after_ref, dx_ref, dg_ref, acc):
        i, kk = pl.program_id(0), pl.program_id(1)

        @pl.when((i == 0) & (kk == 0))
        def _():
            dg_ref[...] = jnp.zeros_like(dg_ref)

        p = _dot_nt(a_ref[...], b_ref[...])

        @pl.when(kk == 0)
        def _():
            acc[...] = p

        @pl.when(kk == nk - 1)
        def _():
            dx, dg = _rms_bwd_core(acc[...] + p, x_ref[...], g_ref[...])
            dx_ref[...] = dr_ref[...] + dx
            dg_ref[...] += dg

    assert nk == 2
    row = pl.BlockSpec((tm, D), lambda i, kk: (i, 0))
    vec = pl.BlockSpec((1, D), lambda i, kk: (0, 0))
    return pl.pallas_call(
        body, name="du_a_rms_bwd", grid=(S // tm, nk),
        in_specs=[pl.BlockSpec((tm, tk), lambda i, kk: (i, kk)), pl.BlockSpec((D, tk), lambda i, kk: (0, kk)),
                  row, vec, row, pl.BlockSpec(after.shape, lambda i, kk: (0, 0))],
        out_specs=[row, vec], out_shape=[_sds((S, D), F32), _sds((1, D), F32)],
        scratch_shapes=[pltpu.VMEM((tm, D), F32)], compiler_params=_params())(dproj, wa, x, g, dres, after)


def _du_b_rms_bwd(dpb, w_in_b, dkv, w_kv, h1, g_b, g_kv, dy):
    tm = 2 * RT

    def body(ab_ref, wb_ref, akv_ref, wkv_ref, x_ref, gb_ref, gkv_ref, dy_ref, dh_ref, dgb_ref, dgkv_ref):
        @pl.when(pl.program_id(0) == 0)
        def _():
            dgb_ref[...] = jnp.zeros_like(dgb_ref)
            dgkv_ref[...] = jnp.zeros_like(dgkv_ref)

        x = x_ref[...]
        dx1, dg1 = _rms_bwd_core(_dot_nt(ab_ref[...], wb_ref[...]), x, gb_ref[...])
        dx2, dg2 = _rms_bwd_core(_dot_nt(akv_ref[...], wkv_ref[...]), x, gkv_ref[...])
        dh_ref[...] = dy_ref[...] + dx1 + dx2
        dgb_ref[...] += dg1
        dgkv_ref[...] += dg2

    row = lambda width: pl.BlockSpec((tm, width), lambda i: (i, 0))
    whole = lambda arr: pl.BlockSpec(arr.shape, lambda i: (0, 0), pipeline_mode=pl.Buffered(1))
    vec = pl.BlockSpec((1, D), lambda i: (0, 0))
    return pl.pallas_call(
        body, name="du_b_rms_bwd", grid=(S // tm,),
        in_specs=[row(dpb.shape[1]), whole(w_in_b), row(dkv.shape[1]), whole(w_kv), row(D), vec, vec, row(D)],
        out_specs=[row(D), vec, vec], out_shape=[_sds((S, D), F32), _sds((1, D), F32), _sds((1, D), F32)],
        compiler_params=_params())(dpb, w_in_b, dkv, w_kv, h1, g_b, g_kv, dy)


GATHER_CHUNKS = 4


def _gather_first(w_in_a, w_out_a, w_kv, w_in_b, w_out_b, norm_a_g):
    rows, cols = w_in_a.shape[-2:]
    groups = rows // LANES
    cols_pad = -(-cols // LANES) * LANES

    def body(wia_ref, woa_ref, wkv_ref, wib_ref, wob_ref, ga_ref,
             wa_g, ga_g, woa_s, wkv_s, wib_s, wob_s, wa_s, st_a, st_oa, st_kv, st_ib, st_ob, plane_t, load_sems, *sems):
        sources = [wia_ref, woa_ref.at[0], wkv_ref, wib_ref.at[0], wob_ref.at[0]]
        stages = [st_a, st_oa, st_kv, st_ib, st_ob]
        loads = [pltpu.make_async_copy(src, dst, load_sems.at[i]) for i, (src, dst) in enumerate(zip(sources, stages))]
        for cp in loads:
            cp.start()
        loads[0].wait()
        plane_t[cols_pad - LANES:, :] = jnp.zeros((LANES, LANES), F32)
        for j in range(groups):
            for c0 in range(0, cols, 64):
                n = min(64, cols - c0)
                plane_t[c0:c0 + n, :] = st_a[pl.ds(c0 * groups + j, n, stride=groups), :]
            for c0 in range(0, cols, LANES):
                n = min(LANES, cols - c0)
                wa_s[j * LANES:(j + 1) * LANES, c0:c0 + n] = plane_t[c0:c0 + LANES, :].T[:, :n].astype(BF16)

        def cast_the_rest():
            for cp, stage, out in zip(loads[1:], stages[1:], [woa_s, wkv_s, wib_s, wob_s]):
                cp.wait()
                out[...] = stage[...].astype(BF16)

        chunks = [pl.ds(r0, rows // GATHER_CHUNKS) for r0 in range(0, rows, rows // GATHER_CHUNKS)]
        _gather_two_level(
            [wa_s.at[rc] for rc in chunks] + [ga_ref],
            [lambda dev, rc=rc: wa_g.at[dev, rc] for rc in chunks] + [lambda dev: ga_g.at[dev]],
            sems, meanwhile=cast_the_rest)

    vmem = pl.BlockSpec(memory_space=pltpu.VMEM)
    anyspec = pl.BlockSpec(memory_space=pl.ANY)
    shard = lambda w: _sds(w.shape[-2:], BF16)
    stage = lambda w: pltpu.VMEM(w.shape[-2:], F32)
    return pl.pallas_call(
        body, name="gather_first", in_specs=[anyspec] * 5 + [vmem],
        out_specs=[anyspec, anyspec, vmem, vmem, vmem, vmem],
        out_shape=[_sds((N_DEV,) + w_in_a.shape[-2:], BF16), _sds((N_DEV,) + norm_a_g.shape, F32),
                   shard(w_out_a), shard(w_kv), shard(w_in_b), shard(w_out_b)],
        scratch_shapes=[pltpu.VMEM(w_in_a.shape[-2:], BF16), pltpu.VMEM((cols * groups, LANES), F32), stage(w_out_a),
                        stage(w_kv), stage(w_in_b), stage(w_out_b), pltpu.VMEM((cols_pad, LANES), F32),
                        pltpu.SemaphoreType.DMA((5,))] + _exchange_sems(GATHER_CHUNKS + 1),
        compiler_params=pltpu.CompilerParams(vmem_limit_bytes=VMEM_LIMIT, has_side_effects=True))(
            _entry_view(w_in_a).reshape(cols * groups, LANES), w_out_a, w_kv, w_in_b, w_out_b, norm_a_g)


def _padded_col(c):
    if c < RAW_F:
        return c
    return FOFF + (c - RAW_F) if c < RAW_G else GOFF + (c - RAW_G)


def _shard_pieces():
    width = NA_RAW // N_DEV
    pieces = []
    for d in range(N_DEV):
        cuts = [width * d] + [c for c in (RAW_F, RAW_G) if width * d < c < width * (d + 1)] + [width * (d + 1)]
        for lo, hi in zip(cuts[:-1], cuts[1:]):
            pieces.append((d, lo - width * d, _padded_col(lo), hi - lo))
    return pieces


def _unshard_wa(wa_g):
    def body(w_ref, o_ref):
        o_ref[:, FOFF + N_HEADS:NA] = jnp.zeros((TM, NA - FOFF - N_HEADS), BF16)
        for d, src, dst, width in _shard_pieces():
            o_ref[:, dst:dst + width] = w_ref[d, :, src:src + width]

    return pl.pallas_call(
        body, name="unshard_wa", grid=(D // TM,),
        in_specs=[pl.BlockSpec((N_DEV, TM, NA_RAW // N_DEV), lambda i: (0, i, 0))],
        out_specs=pl.BlockSpec((TM, NA), lambda i: (i, 0)), out_shape=_sds((D, NA), BF16),
        compiler_params=_params())(wa_g)


def _reshard_pair_reduce(dwa):
    n_chip, nt = N_DEV // 2, D // TM
    width = NA_RAW // N_DEV

    def body(g_ref, o_ref, slots_v, sib_v, send_sems, recv_sems):
        i = pl.program_id(0)
        x, y, c = lax.axis_index("x"), lax.axis_index("y"), lax.axis_index("c")

        def tile_rows(tile):
            return pl.ds(tile * TM if isinstance(tile, int) else pl.multiple_of(tile * TM, TM), TM)

        def give(j, tile):
            return pltpu.make_async_remote_copy(
                src_ref=slots_v.at[2 * j + 1 - c, tile_rows(tile)], dst_ref=sib_v.at[j, tile_rows(tile)],
                send_sem=send_sems.at[j, tile], recv_sem=recv_sems.at[j, tile], device_id=(x, y, 1 - c),
                device_id_type=pl.DeviceIdType.MESH)

        for d, src, dst, w in _shard_pieces():
            slots_v[d, tile_rows(i), src:src + w] = g_ref[:, dst:dst + w]
        for j in range(n_chip):
            give(j, i).start()

        @pl.when(i == nt - 1)
        def _():
            for tile in range(nt):
                for j in range(n_chip):
                    give(j, tile).wait()
            for j in range(n_chip):
                for r0 in range(0, D, 64):
                    mine = slots_v[2 * j + c, r0:r0 + 64, :].astype(F32)
                    o_ref[j, r0:r0 + 64, :] = (mine + sib_v[j, r0:r0 + 64, :].astype(F32)).astype(BF16)

    half = _sds((n_chip, D, width), dwa.dtype)
    return pl.pallas_call(
        body, name="reshard_pair_reduce", grid=(nt,), in_specs=[pl.BlockSpec((TM, NA), lambda i: (i, 0))],
        out_specs=pl.BlockSpec(half.shape, lambda i: (0, 0, 0)), out_shape=half,
        scratch_shapes=[pltpu.VMEM((N_DEV, D, width), dwa.dtype), pltpu.VMEM(half.shape, dwa.dtype),
                        pltpu.SemaphoreType.DMA((n_chip, nt)), pltpu.SemaphoreType.DMA((n_chip, nt))],
        compiler_params=pltpu.CompilerParams(vmem_limit_bytes=VMEM_LIMIT, has_side_effects=True))(dwa)


CHIP_FLIPS = (2, 4, 6)


def _chip_exchange_start(partial):
    n = len(CHIP_FLIPS)

    def body(p_ref, land_ref, *rest):
        sends, recvs, token = rest[:n], rest[n:2 * n], rest[2 * n + 2]
        x, y, c = lax.axis_index("x"), lax.axis_index("y"), lax.axis_index("c")
        me = 4 * x + 2 * y + c
        for idx, k in enumerate(CHIP_FLIPS):
            pltpu.make_async_remote_copy(
                src_ref=p_ref.at[(me ^ k) >> 1], dst_ref=land_ref.at[me >> 1], send_sem=sends[idx],
                recv_sem=recvs[idx], device_id=(x ^ ((k >> 2) & 1), y ^ ((k >> 1) & 1), c),
                device_id_type=pl.DeviceIdType.MESH).start()
        token[...] = jnp.zeros_like(token)

    hbm = pl.BlockSpec(memory_space=pltpu.HBM)
    sem = pl.BlockSpec(memory_space=pltpu.SEMAPHORE)
    buf = pltpu.HBM(partial.shape, partial.dtype)
    return pl.pallas_call(
        body, name="chip_exchange_start",
        out_shape=(pltpu.SemaphoreType.DMA(()),) * (2 * n) + (buf, buf, _sds((8, LANES), F32)),
        in_specs=(hbm, hbm), out_specs=(sem,) * (2 * n) + (hbm, hbm, pl.BlockSpec(memory_space=pltpu.VMEM)),
        input_output_aliases={0: 2 * n, 1: 2 * n + 1},
        compiler_params=pltpu.CompilerParams(has_side_effects=pltpu.SideEffectType.DATAFLOW_SIDE_EFFECTING))(
            pltpu.with_memory_space_constraint(partial, pltpu.HBM),
            pltpu.with_memory_space_constraint(lax.empty(partial.shape, partial.dtype), pltpu.HBM))


def _chip_exchange_wait(started, after):
    n = len(CHIP_FLIPS)
    sems, (p_thru, land_thru) = started[:2 * n], started[2 * n:2 * n + 2]

    def body(p_ref, land_ref, *rest):
        sends, recvs = rest[:n], rest[n:2 * n]
        x, y, c = lax.axis_index("x"), lax.axis_index("y"), lax.axis_index("c")
        me = 4 * x + 2 * y + c
        for idx, k in enumerate(CHIP_FLIPS):
            copy = pltpu.make_async_remote_copy(
                src_ref=p_ref.at[(me ^ k) >> 1], dst_ref=land_ref.at[(me ^ k) >> 1], send_sem=sends[idx],
                recv_sem=recvs[idx], device_id=(x ^ ((k >> 2) & 1), y ^ ((k >> 1) & 1), c),
                device_id_type=pl.DeviceIdType.MESH)
            copy.wait_send()
            copy.wait_recv()

    hbm = pl.BlockSpec(memory_space=pltpu.HBM)
    sem = pl.BlockSpec(memory_space=pltpu.SEMAPHORE)
    buf = pltpu.HBM(p_thru.shape, p_thru.dtype)
    return pl.pallas_call(
        body, name="chip_exchange_wait", out_shape=(buf, buf),
        in_specs=(hbm, hbm) + (sem,) * (2 * n) + (pl.BlockSpec(memory_space=pl.ANY),) * len(after),
        out_specs=(hbm, hbm), input_output_aliases={0: 0, 1: 1},
        compiler_params=pltpu.CompilerParams(has_side_effects=pltpu.SideEffectType.DATAFLOW_SIDE_EFFECTING))(
            p_thru, land_thru, *sems, *after)


def _slab_peer(x, y, c, k):
    return (x ^ ((k >> 2) & 1), y ^ ((k >> 1) & 1), c ^ (k & 1))


def _gather_slab_start(slab, carried):
    n = N_DEV - 1

    def body(s_ref, land_ref, carried_ref, *rest):
        sends, recvs, own_sem = rest[:n], rest[n:2 * n], rest[2 * n + 3]
        x, y, c = lax.axis_index("x"), lax.axis_index("y"), lax.axis_index("c")
        me = 4 * x + 2 * y + c
        for k in range(1, N_DEV):
            pltpu.make_async_remote_copy(
                src_ref=s_ref, dst_ref=land_ref.at[me], send_sem=sends[k - 1], recv_sem=recvs[k - 1],
                device_id=_slab_peer(x, y, c, k), device_id_type=pl.DeviceIdType.MESH).start()
        own = pltpu.make_async_copy(s_ref, land_ref.at[me], own_sem)
        own.start()
        own.wait()

    hbm = pl.BlockSpec(memory_space=pltpu.HBM)
    sem = pl.BlockSpec(memory_space=pltpu.SEMAPHORE)
    land = (N_DEV,) + slab.shape
    return pl.pallas_call(
        body, name="gather_slab_start",
        out_shape=(pltpu.SemaphoreType.DMA(()),) * (2 * n) + (
            pltpu.HBM(slab.shape, slab.dtype), pltpu.HBM(land, slab.dtype), pltpu.HBM(carried.shape, carried.dtype)),
        in_specs=(hbm, hbm, hbm), out_specs=(sem,) * (2 * n) + (hbm, hbm, hbm),
        input_output_aliases={0: 2 * n, 1: 2 * n + 1, 2: 2 * n + 2}, scratch_shapes=[pltpu.SemaphoreType.DMA(())],
        compiler_params=pltpu.CompilerParams(has_side_effects=pltpu.SideEffectType.DATAFLOW_SIDE_EFFECTING))(
            pltpu.with_memory_space_constraint(slab, pltpu.HBM),
            pltpu.with_memory_space_constraint(lax.empty(land, slab.dtype), pltpu.HBM),
            pltpu.with_memory_space_constraint(carried, pltpu.HBM))


def _gather_slab_wait(started, after):
    n = N_DEV - 1
    sems, (s_thru, land_thru) = started[:2 * n], started[2 * n:2 * n + 2]

    def body(s_ref, land_ref, *rest):
        sends, recvs = rest[:n], rest[n:2 * n]
        x, y, c = lax.axis_index("x"), lax.axis_index("y"), lax.axis_index("c")
        me = 4 * x + 2 * y + c
        for k in range(1, N_DEV):
            copy = pltpu.make_async_remote_copy(
                src_ref=s_ref, dst_ref=land_ref.at[me ^ k], send_sem=sends[k - 1], recv_sem=recvs[k - 1],
                device_id=_slab_peer(x, y, c, k), device_id_type=pl.DeviceIdType.MESH)
            copy.wait_send()
            copy.wait_recv()

    hbm = pl.BlockSpec(memory_space=pltpu.HBM)
    sem = pl.BlockSpec(memory_space=pltpu.SEMAPHORE)
    return pl.pallas_call(
        body, name="gather_slab_wait",
        out_shape=(pltpu.HBM(s_thru.shape, s_thru.dtype), pltpu.HBM(land_thru.shape, land_thru.dtype)),
        in_specs=(hbm, hbm) + (sem,) * (2 * n) + (pl.BlockSpec(memory_space=pl.ANY),) * len(after),
        out_specs=(hbm, hbm), input_output_aliases={0: 0, 1: 1},
        compiler_params=pltpu.CompilerParams(has_side_effects=pltpu.SideEffectType.DATAFLOW_SIDE_EFFECTING))(
            s_thru, land_thru, *sems, *after)[1]


def _adamw(w, g, m, v):
    m = ADAM_B1 * m + (1.0 - ADAM_B1) * g
    v = ADAM_B2 * v + (1.0 - ADAM_B2) * (g * g)
    m_hat = m / (1.0 - ADAM_B1 ** ADAM_STEP)
    v_hat = v / (1.0 - ADAM_B2 ** ADAM_STEP)
    delta = -ADAM_LR * (m_hat / (jnp.sqrt(v_hat) + ADAM_EPS) + ADAM_WD * w)
    return delta, m, v


def _sum_adamw(recv, w, m, v, name, after=None):
    lead = w.ndim - 2
    rows, cols = w.shape[-2:]
    tr = 256 if rows % 256 == 0 else 128
    n_slots = recv.shape[0]
    extra = [] if after is None else [after]

    def body(*refs):
        r_ref = refs[0]
        w_ref, m_ref, v_ref, g_ref, d_ref, nm_ref, nv_ref = refs[1 + len(extra):]
        g = None
        for slot in range(n_slots):
            g = r_ref[slot].astype(F32) if g is None else g + r_ref[slot].astype(F32)
        g_ref[...] = g
        d_ref[...], nm_ref[...], nv_ref[...] = _adamw(w_ref[...], g, m_ref[...], v_ref[...])

    blk = pl.BlockSpec((None,) * lead + (tr, cols), lambda i: (0,) * lead + (i, 0))
    slots = pl.BlockSpec((n_slots, tr, cols), lambda i: (0, i, 0))
    return pl.pallas_call(
        body, name=name, grid=(rows // tr,),
        in_specs=[slots] + [pl.BlockSpec(a.shape, lambda i: (0, 0)) for a in extra] + [blk, blk, blk],
        out_specs=[blk] * 4, out_shape=[_sds(w.shape, F32)] * 4, compiler_params=_params())(recv, *extra, w, m, v)


def _entry_view(a):
    _, rows, cols = a.shape
    return jnp.transpose(a, (2, 0, 1)).reshape(cols, rows // LANES, LANES)


def _from_entry_view(a):
    cols, groups, lanes = a.shape
    return jnp.transpose(a, (1, 2, 0)).reshape(1, groups * lanes, cols)


def _sum_adamw_entry_view(landing, own, w, m, v, name):
    n_slots, rows, cols = landing.shape
    groups = rows // LANES
    cols_pad = -(-cols // LANES) * LANES

    def body(r_ref, own_ref, w_ref, m_ref, v_ref, g_ref, d_ref, nm_ref, nv_ref, pad_ref, gt_ref):
        j = pl.program_id(0)
        chip = (4 * lax.axis_index("x") + 2 * lax.axis_index("y") + lax.axis_index("c")) >> 1
        pad_ref[:, cols_pad - LANES:] = jnp.zeros((LANES, LANES), F32)
        for r0 in range(0, LANES, 32):
            g = None
            for slot in range(n_slots):
                part = jnp.where(chip == slot, own_ref[slot, r0:r0 + 32], r_ref[slot, r0:r0 + 32])
                g = part.astype(F32) if g is None else g + part.astype(F32)
            pad_ref[r0:r0 + 32, :cols] = g
        for c0 in range(0, cols_pad, LANES):
            gt_ref[c0:c0 + LANES, :] = pad_ref[:, c0:c0 + LANES].T
        def update_plane(plane):
            for c0 in range(0, cols, 64):
                n = min(64, cols - c0)
                at = pl.ds(c0 * groups + plane, n, stride=groups)
                gt = gt_ref[c0:c0 + n, :]
                g_ref[at, :] = gt
                d_ref[at, :], nm_ref[at, :], nv_ref[at, :] = _adamw(w_ref[at, :], gt, m_ref[at, :], v_ref[at, :])

        for plane in range(groups):
            pl.when(j == plane)(lambda plane=plane: update_plane(plane))

    flat = lambda a: _entry_view(a).reshape(cols * groups, LANES)
    whole = pl.BlockSpec((cols * groups, LANES), lambda j: (0, 0))
    slots = pl.BlockSpec((n_slots, LANES, cols), lambda j: (0, j, 0))
    outs = pl.pallas_call(
        body, name=name, grid=(groups,), in_specs=[slots, slots, whole, whole, whole], out_specs=[whole] * 4,
        out_shape=[_sds((cols * groups, LANES), F32)] * 4,
        scratch_shapes=[pltpu.VMEM((LANES, cols_pad), F32), pltpu.VMEM((cols_pad, LANES), F32)],
        compiler_params=_params())(landing, own, flat(w), flat(m), flat(v))
    return [_from_entry_view(o.reshape(cols, groups, LANES)) for o in outs], outs[0]


SLAB_ROWS = 16
SLOT = {"kv_norm_g": (8, 0, D), "norm_b_g": (9, 0, D), "b_forget": (10, 0, 16), "qnorm_a_g": (10, 128, HD),
        "knorm_a_g": (10, 256, HD), "knorm_b_g": (10, 384, HD), "qnorm_b_g": (10, 512, HD), "sinks": (10, 640, 16)}
SMALL = ["norm_a_g", "b_forget", "qnorm_a_g", "knorm_a_g", "kv_norm_g", "knorm_b_g", "norm_b_g", "qnorm_b_g", "sinks"]


LOSS_ROW = 11


def _pack_small(dg_a, dg_kv, dg_b, db_f, dgq_a, dgk_a, dgk_b, dgq_b, dsinks, lsum):
    def fold(ref):
        return ref[:, 0:HD] + ref[:, HD:2 * HD]

    def body(dga_ref, dgkv_ref, dgb_ref, dbf_ref, dgqa_ref, dgka_ref, dgkb_ref, dgqb_ref, dsk_ref, ls_ref, slab_ref):
        slab_ref[...] = jnp.zeros_like(slab_ref)
        for r in range(N_DEV):
            slab_ref[r:r + 1, 0:LANES] = dga_ref[:, LANES * r:LANES * (r + 1)]
        slab_ref[8:9, :] = dgkv_ref[...]
        slab_ref[9:10, :] = dgb_ref[...]
        slab_ref[10:11, 0:LANES] = dbf_ref[...]
        slab_ref[10:11, 128:128 + HD] = fold(dgqa_ref)
        slab_ref[10:11, 256:256 + HD] = fold(dgka_ref)
        slab_ref[10:11, 384:384 + HD] = fold(dgkb_ref)
        slab_ref[10:11, 512:512 + HD] = fold(dgqb_ref)
        slab_ref[10:11, 640:640 + LANES] = dsk_ref[...]
        slab_ref[LOSS_ROW:LOSS_ROW + 1, 0:LANES] = ls_ref[...]

    return pl.pallas_call(body, name="pack_small", out_shape=_sds((SLAB_ROWS, D), F32), compiler_params=_params())(
        dg_a, dg_kv, dg_b, db_f, dgq_a, dgk_a, dgk_b, dgq_b, dsinks, lsum)


def _small_adamw(recv, ws, ms, vs):
    k = len(SMALL)

    def body(*refs):
        r_ref = refs[0]
        w_refs, m_refs, v_refs = refs[1:1 + k], refs[1 + k:1 + 2 * k], refs[1 + 2 * k:1 + 3 * k]
        outs = refs[1 + 3 * k:1 + 7 * k]
        loss_ref, tot = refs[1 + 7 * k], refs[2 + 7 * k]
        g = r_ref[0]
        for dev in range(1, N_DEV):
            g = g + r_ref[dev]
        tot[...] = g
        loss_ref[...] = tot[LOSS_ROW:LOSS_ROW + 1, 0:LANES] * (0.5 / D)
        me = 4 * lax.axis_index("x") + 2 * lax.axis_index("y") + lax.axis_index("c")
        for p, name in enumerate(SMALL):
            if name == "norm_a_g":
                mine = lax.broadcasted_iota(jnp.int32, (N_DEV, LANES), 0) == me
                gp = jnp.sum(jnp.where(mine, tot[0:N_DEV, 0:LANES], 0.0), axis=0, keepdims=True)
            else:
                row, lo, width = SLOT[name]
                gp = tot[row:row + 1, lo:lo + width]
            d, nm, nv = _adamw(w_refs[p][...], gp, m_refs[p][...], v_refs[p][...])
            outs[p][...] = gp
            outs[k + p][...] = d
            outs[2 * k + p][...] = nm
            outs[3 * k + p][...] = nv

    shapes = [_sds(w.shape, F32) for w in ws]
    return pl.pallas_call(body, name="small_adamw", out_shape=shapes * 4 + [_sds((1, LANES), F32)],
                          scratch_shapes=[pltpu.VMEM((SLAB_ROWS, D), F32)],
                          compiler_params=_params())(recv, *ws, *ms, *vs)


def _rope_tables(positions):
    inv_freq = jnp.power(jnp.float32(ROPE_THETA), -jnp.arange(0, ROT, 2, dtype=F32) / ROT)
    ang = positions.astype(F32)[:, None] * inv_freq[None, :]
    cos, sin = jnp.cos(ang), jnp.sin(ang)
    c64 = jnp.concatenate([cos, cos, jnp.ones((S, HD - ROT), F32)], axis=-1)
    s64 = jnp.concatenate([-sin, sin, jnp.zeros((S, HD - ROT), F32)], axis=-1)
    return jnp.tile(c64, (1, 2)), jnp.tile(s64, (1, 2))


def _local_step(x, tgt, positions, g_a, wa, b_forget, gq_a, gk_a, g_kv, gk_b, g_b, gq_b, sinks,
                woa_s, wkv_s, wib_s, wob_s, adamw_others):
    nq = S // TQ
    cos2, sin2 = _rope_tables(positions)
    b_pad = jnp.pad(b_forget, ((0, 0), (0, LANES - N_HEADS)))

    u_a, proj, qn, kn, vb, ccol, cbc = _head_a(x, g_a, wa, gq_a, gk_a, b_pad)
    crow = ccol[:, :N_HEADS].T.reshape(N_HEADS, nq, 1, TQ)
    o_a, z_a, lse_a, woa_g, wkv_g, w_in_b, wob_g = _fox_fwd(
        qn, kn, vb, proj, crow, cbc,
        rider=[("gather_rows", woa_s), ("gather_rows", wkv_s), ("gather_cols", wib_s), ("gather_rows", wob_s)])
    w_out_a, w_kv, w_out_b = woa_g.reshape(D, D), wkv_g.reshape(D, 512), wob_g.reshape(D, D)
    h1, u_kv, u_b, kv, pb, qb, ksh, vsh = _head_b(x, z_a, w_out_a, g_kv, g_b, w_kv, w_in_b, gq_b, gk_b, cos2, sin2)
    sinks1 = sinks.reshape(N_HEADS)
    o_b, z_b, lse_b = _swa_fwd(qb, ksh, vsh, pb, sinks1)
    dy, lsum = _out_b_loss(z_b, w_out_b, h1, tgt)
    dw_out_b = _mm(z_b, dy, "tn", 512, 512, S, out_dtype=BF16, name="mm_dw_out_b")
    dz_b = _mm(dy, w_out_b, "nt", 1024, 512, D, name="mm_dz_b")
    dpb, dka, dkb, dva, dvb, dsinks, dgq_b = _swa_bwd(qb, ksh, vsh, dz_b, o_b, lse_b, pb, sinks1, gq_b, cos2, sin2)
    dkv, dgk_b = _prep_kv_bwd(dka, dkb, dva, dvb, kv, gk_b, cos2, sin2)
    dw_in_b = _mm(u_b, dpb, "tn", 512, 512, S, out_dtype=BF16, name="mm_dw_in_b")
    dw_kv = _mm(u_kv, dkv, "tn", 512, 512, S, out_dtype=BF16, name="mm_dw_kv")
    dh1, dg_b, dg_kv = _du_b_rms_bwd(dpb, w_in_b, dkv, w_kv, h1, g_b, g_kv, dy)
    dw_out_a = _mm(z_a, dh1, "tn", 512, 512, S, out_dtype=BF16, name="mm_dw_out_a")
    do_a, dgate_a, delta_a = _fox_bwd_pre(dh1, w_out_a, proj, o_a)
    dk_a, dv_a, dcs, dq_a, drow, r_wob, r_wib, r_wkv, r_woa = _fox_bwd(
        qn, kn, vb, do_a, lse_a, delta_a, crow, cbc,
        rider=[("a2a_rows", dw_out_b), ("a2a_cols", dw_in_b), ("a2a_rows", dw_kv), ("a2a_rows", dw_out_a)])
    dproj, dgq_a, dgk_a, db_f = _prep_a_bwd(dq_a, dk_a, dv_a, dgate_a, drow, dcs, proj, b_pad, gq_a, gk_a)
    dwa = _mm(u_a, dproj, "tn", 1024, 256, S, out_dtype=BF16, name="mm_dw_in_a")
    partial = _reshard_pair_reduce(dwa)
    started = _chip_exchange_start(partial)
    dx, dg_a = _du_a_rms_bwd(dproj, wa, x, g_a, dh1, after=started[-1])
    others = adamw_others(dict(w_out_a=r_woa, w_kv=r_wkv, w_in_b=r_wib, w_out_b=r_wob), dg_a)
    partial, landed = _chip_exchange_wait(started, [res[0] for res in others.values()])
    slab = _pack_small(dg_a, dg_kv, dg_b, db_f, dgq_a, dgk_a, dgk_b, dgq_b, dsinks, lsum)
    slab_started = _gather_slab_start(slab, landed)
    return dx, (slab_started[-1], partial), others, slab_started


def kernel(x, positions, norm_a_g, w_in_a, b_forget, qnorm_a_g, knorm_a_g, w_out_a, kv_norm_g, w_kv, knorm_b_g, norm_b_g, w_in_b, qnorm_b_g, sinks, w_out_b, loss_target, m_norm_a_g, m_w_in_a, m_b_forget, m_qnorm_a_g, m_knorm_a_g, m_w_out_a, m_kv_norm_g, m_w_kv, m_knorm_b_g, m_norm_b_g, m_w_in_b, m_qnorm_b_g, m_sinks, m_w_out_b, v_norm_a_g, v_w_in_a, v_b_forget, v_qnorm_a_g, v_knorm_a_g, v_w_out_a, v_kv_norm_g, v_w_kv, v_knorm_b_g, v_norm_b_g, v_w_in_b, v_qnorm_b_g, v_sinks, v_w_out_b):
    wa_g, ga_g, woa_s, wkv_s, wib_s, wob_s = _gather_first(w_in_a, w_out_a, w_kv, w_in_b, w_out_b, norm_a_g)
    state = dict(w_in_a=(w_in_a, m_w_in_a, v_w_in_a), w_out_a=(w_out_a, m_w_out_a, v_w_out_a),
                 w_kv=(w_kv, m_w_kv, v_w_kv), w_in_b=(w_in_b, m_w_in_b, v_w_in_b),
                 w_out_b=(w_out_b, m_w_out_b, v_w_out_b))

    def adamw_others(landed, after):
        return {n: _sum_adamw(r, *state[n], "adamw_" + n, after=after) for n, r in landed.items()}

    dx, r_wa, big, slab_started = _local_step(
        x[0], loss_target[0], positions, ga_g.reshape(1, D), _unshard_wa(wa_g), b_forget, qnorm_a_g, knorm_a_g,
        kv_norm_g.reshape(1, D), knorm_b_g.reshape(1, HD), norm_b_g, qnorm_b_g, sinks, woa_s, wkv_s, wib_s, wob_s,
        adamw_others)
    big["w_in_a"], done = _sum_adamw_entry_view(*r_wa, *state["w_in_a"], "adamw_w_in_a")
    slab_g = _gather_slab_wait(slab_started, [done])

    r2 = lambda a: a.reshape(1, -1)
    small_w = dict(norm_a_g=norm_a_g, b_forget=b_forget, qnorm_a_g=qnorm_a_g, knorm_a_g=knorm_a_g,
                   kv_norm_g=kv_norm_g, knorm_b_g=knorm_b_g, norm_b_g=norm_b_g, qnorm_b_g=qnorm_b_g, sinks=sinks)
    small_m = dict(norm_a_g=m_norm_a_g, b_forget=m_b_forget, qnorm_a_g=m_qnorm_a_g, knorm_a_g=m_knorm_a_g,
                   kv_norm_g=m_kv_norm_g, knorm_b_g=m_knorm_b_g, norm_b_g=m_norm_b_g, qnorm_b_g=m_qnorm_b_g,
                   sinks=m_sinks)
    small_v = dict(norm_a_g=v_norm_a_g, b_forget=v_b_forget, qnorm_a_g=v_qnorm_a_g, knorm_a_g=v_knorm_a_g,
                   kv_norm_g=v_kv_norm_g, knorm_b_g=v_knorm_b_g, norm_b_g=v_norm_b_g, qnorm_b_g=v_qnorm_b_g,
                   sinks=v_sinks)
    res = _small_adamw(slab_g, [r2(small_w[n]) for n in SMALL], [r2(small_m[n]) for n in SMALL],
                       [r2(small_v[n]) for n in SMALL])
    k = len(SMALL)
    small = {n: [res[q * k + p].reshape(small_w[n].shape) for q in range(4)] for p, n in enumerate(SMALL)}
    loss = res[4 * k][0, 0]

    order = ["norm_a_g", "w_in_a", "b_forget", "qnorm_a_g", "knorm_a_g", "w_out_a", "kv_norm_g", "w_kv",
             "knorm_b_g", "norm_b_g", "w_in_b", "qnorm_b_g", "sinks", "w_out_b"]

    def leaf(n, q):
        return big[n][q] if n in big else small[n][q]

    outs = [loss, dx[None]]
    for q in range(4):
        outs.extend(leaf(n, q) for n in order)
    return tuple(outs)
```

```python
import jax
import jax.numpy as jnp
from jax import lax
from jax.experimental import pallas as pl
from jax.experimental.pallas import tpu as pltpu

F32, BF16 = jnp.float32, jnp.bfloat16

S = 2048
D = 1024
HD = 64
N_HEADS = 16
N_DEV = 8
NA = 4352
GOFF = 3072
FOFF = 4096
RAW_F = 3072
RAW_G = RAW_F + N_HEADS
NA_RAW = 4112
EPS = 1e-6
QSCALE = 0.125
ROPE_THETA = 500000.0
ROT = 16
WIN = 128
TQ = 256
TK = 256
KS = TQ // 2
HPS = 8
HW = HPS * HD
TM = 256
RT = 256
RB = 512
CB = 256
LANES = 128

ADAM_LR, ADAM_B1, ADAM_B2, ADAM_EPS, ADAM_WD, ADAM_STEP = 0.001, 0.9, 0.999, 1e-08, 0.01, 10

VMEM_LIMIT = 56 * 1024 * 1024


def _params():
    return pltpu.CompilerParams(vmem_limit_bytes=VMEM_LIMIT)


def _sds(shape, dtype):
    return jax.ShapeDtypeStruct(shape, dtype)


def _dot_nt(a, b):
    return lax.dot_general(a, b, (((1,), (1,)), ((), ())), preferred_element_type=F32)


def _dot_tn(a, b):
    return lax.dot_general(a, b, (((0,), (0,)), ((), ())), preferred_element_type=F32)


def _dot_nn(a, b):
    return lax.dot_general(a, b, (((1,), (0,)), ((), ())), preferred_element_type=F32)


def _sigmoid(g):
    return 1.0 / (1.0 + jnp.exp(-g))


def _lane_iota(shape):
    return lax.broadcasted_iota(jnp.int32, shape, len(shape) - 1)


def _flips(kind):
    return (2, 4, 6) if kind == "a2a_chips" else tuple(range(1, N_DEV))


def _send_view(kind, ref, dev):
    if kind in ("gather_rows", "gather_cols"):
        return ref
    if kind == "a2a_slots":
        return ref.at[dev]
    if kind == "a2a_chips":
        return ref.at[dev >> 1]
    if kind == "a2a_rows":
        rows = ref.shape[0] // N_DEV
        return ref.at[pl.ds(pl.multiple_of(dev * rows, rows), rows)]
    cols = ref.shape[1] // N_DEV
    return ref.at[:, pl.ds(pl.multiple_of(dev * cols, cols), cols)]


def _land_view(kind, ref, dev):
    if kind == "gather_cols":
        cols = ref.shape[1] // N_DEV
        return ref.at[:, pl.ds(pl.multiple_of(dev * cols, cols), cols)]
    if kind == "a2a_chips":
        return ref.at[dev >> 1]
    return ref.at[dev]


def _landing_sds(kind, arr):
    if kind == "gather_rows":
        return _sds((N_DEV,) + arr.shape, arr.dtype)
    if kind == "gather_cols":
        return _sds((arr.shape[0], N_DEV * arr.shape[1]), arr.dtype)
    if kind == "a2a_rows":
        return _sds((N_DEV, arr.shape[0] // N_DEV, arr.shape[1]), arr.dtype)
    if kind == "a2a_cols":
        return _sds((N_DEV, arr.shape[0], arr.shape[1] // N_DEV), arr.dtype)
    return _sds(arr.shape, arr.dtype)


def _exchange_sems(n_parts):
    n = n_parts * (N_DEV - 1)
    return [pltpu.SemaphoreType.DMA((n,)), pltpu.SemaphoreType.DMA((n,)), pltpu.SemaphoreType.DMA((n_parts,))]


def _exchange_ops(kinds, srcs, dsts, sems, start, wait):
    send_sems, recv_sems, local_sems = sems
    x, y, c = lax.axis_index("x"), lax.axis_index("y"), lax.axis_index("c")
    me = 4 * x + 2 * y + c

    def local(a):
        return pltpu.make_async_copy(_send_view(kinds[a], srcs[a], me), _land_view(kinds[a], dsts[a], me),
                                     local_sems.at[a])

    def remote(a, k, landing_dev):
        peer = (x ^ ((k >> 2) & 1), y ^ ((k >> 1) & 1), c ^ (k & 1))
        sem = a * (N_DEV - 1) + k - 1
        return pltpu.make_async_remote_copy(
            src_ref=_send_view(kinds[a], srcs[a], me ^ k), dst_ref=_land_view(kinds[a], dsts[a], landing_dev),
            send_sem=send_sems.at[sem], recv_sem=recv_sems.at[sem], device_id=peer,
            device_id_type=pl.DeviceIdType.MESH)

    pairs = [(a, k) for k in range(1, N_DEV) for a in range(len(kinds)) if k in _flips(kinds[a])]
    if start:
        for a in range(len(kinds)):
            local(a).start()
        for a, k in pairs:
            remote(a, k, me).start()
    if wait:
        for a, k in pairs:
            remote(a, k, me ^ k).wait_recv()
            remote(a, k, me).wait_send()
        for a in range(len(kinds)):
            local(a).wait()


def _gather_two_level(srcs, landing_of, sems, meanwhile=None):
    send_sems, recv_sems, local_sems = sems
    x, y, c = lax.axis_index("x"), lax.axis_index("y"), lax.axis_index("c")
    me, sibling = (x, y, c), (x, y, 1 - c)
    chips = [(1 - x, y), (x, 1 - y), (1 - x, 1 - y)]

    def slot(a, dev):
        return landing_of[a](4 * dev[0] + 2 * dev[1] + dev[2])

    def copy(a, k, block, to, src=None):
        return pltpu.make_async_remote_copy(
            src_ref=slot(a, block) if src is None else src, dst_ref=slot(a, block),
            send_sem=send_sems.at[a * (N_DEV - 1) + k], recv_sem=recv_sems.at[a * (N_DEV - 1) + k],
            device_id=to, device_id_type=pl.DeviceIdType.MESH)

    parts = range(len(srcs))
    mine = [pltpu.make_async_copy(srcs[a], slot(a, me), local_sems.at[a]) for a in parts]
    first = [copy(a, 1 + j, me, (*chip, c), src=srcs[a]) for a in parts for j, chip in enumerate(chips)]
    first += [copy(a, 0, me, sibling, src=srcs[a]) for a in parts]
    for cp in first + mine:
        cp.start()
    if meanwhile is not None:
        meanwhile()
    passed = []
    for a in parts:
        for j, chip in enumerate(chips):
            copy(a, 1 + j, (*chip, c), me).wait_recv()
            fwd = copy(a, 4 + j, (*chip, c), sibling)
            fwd.start()
            passed.append(fwd)
    for a in parts:
        copy(a, 0, sibling, me).wait_recv()
        for j, chip in enumerate(chips):
            copy(a, 4 + j, (*chip, 1 - c), me).wait_recv()
    for cp in first + passed:
        cp.wait_send()
    for cp in mine:
        cp.wait()


def _call(body, *, name, args, in_specs, out_specs, out_shape, grid=(), scratch_shapes=(), aliases=None, rider=()):
    n_in, n_out, n_scr, n_r = len(in_specs), len(out_specs), len(scratch_shapes), len(rider)
    kinds = [kind for kind, _ in rider]

    def kernel_body(*refs):
        c_in, r_in = refs[:n_in], refs[n_in:n_in + n_r]
        c_out = refs[n_in + n_r:n_in + n_r + n_out]
        r_out = refs[n_in + n_r + n_out:n_in + 2 * n_r + n_out]
        rest = refs[n_in + 2 * n_r + n_out:]
        c_scr, sems = rest[:n_scr], rest[n_scr:]
        if n_r:
            assert grid, "a rider needs a gridded call"
            ids = [pl.program_id(ax) for ax in range(len(grid))]
            first, last = ids[0] == 0, ids[0] == grid[0] - 1
            for pid, size in zip(ids[1:], grid[1:]):
                first = first & (pid == 0)
                last = last & (pid == size - 1)
            pl.when(first)(lambda: _exchange_ops(kinds, r_in, r_out, sems, True, False))
        body(*c_in, *c_out, *c_scr)
        if n_r:
            pl.when(last)(lambda: _exchange_ops(kinds, r_in, r_out, sems, False, True))

    anyspec = pl.BlockSpec(memory_space=pl.ANY)
    params = pltpu.CompilerParams(vmem_limit_bytes=VMEM_LIMIT, has_side_effects=bool(n_r))
    outs = pl.pallas_call(
        kernel_body, name=name, grid=grid, in_specs=list(in_specs) + [anyspec] * n_r,
        out_specs=list(out_specs) + [anyspec] * n_r,
        out_shape=list(out_shape) + [_landing_sds(kind, arr) for kind, arr in rider],
        scratch_shapes=list(scratch_shapes) + (_exchange_sems(n_r) if n_r else []),
        input_output_aliases=aliases or {}, compiler_params=params)(*args, *[arr for _, arr in rider])
    return list(outs)


def _mm(a, b, mode, tm, tn, tk, out_dtype=F32, add=None, name="mm", rider=()):
    if mode == "nn":
        (m, k), n = a.shape, b.shape[1]
        a_spec = pl.BlockSpec((tm, tk), lambda i, j, kk: (i, kk))
        b_spec = pl.BlockSpec((tk, tn), lambda i, j, kk: (kk, j))
        dot = _dot_nn
    elif mode == "nt":
        (m, k), n = a.shape, b.shape[0]
        a_spec = pl.BlockSpec((tm, tk), lambda i, j, kk: (i, kk))
        b_spec = pl.BlockSpec((tn, tk), lambda i, j, kk: (j, kk))
        dot = _dot_nt
    else:
        (k, m), n = a.shape, b.shape[1]
        a_spec = pl.BlockSpec((tk, tm), lambda i, j, kk: (kk, i))
        b_spec = pl.BlockSpec((tk, tn), lambda i, j, kk: (kk, j))
        dot = _dot_tn
    assert m % tm == 0 and n % tn == 0 and k % tk == 0, (m, n, k, tm, tn, tk)
    nk = k // tk
    has_add = add is not None

    def body(*refs):
        if has_add:
            a_ref, b_ref, add_ref, o_ref, acc = refs
        else:
            a_ref, b_ref, o_ref, acc = refs
        p = dot(a_ref[...].astype(BF16), b_ref[...].astype(BF16))

        def finish(total):
            if has_add:
                total = add_ref[...] + total
            o_ref[...] = total.astype(out_dtype)

        if nk == 1:
            finish(p)
        else:
            kk = pl.program_id(2)

            @pl.when(kk == 0)
            def _():
                acc[...] = p

            @pl.when(kk > 0)
            def _():
                acc[...] += p

            @pl.when(kk == nk - 1)
            def _():
                finish(acc[...])

    in_specs = [a_spec, b_spec]
    args = [a, b]
    if has_add:
        in_specs.append(pl.BlockSpec((tm, tn), lambda i, j, kk: (i, j)))
        args.append(add)
    acc_shape = (tm, tn) if nk > 1 else (8, LANES)
    outs = _call(body, name=name, args=args, grid=(m // tm, n // tn, nk), in_specs=in_specs,
                 out_specs=[pl.BlockSpec((tm, tn), lambda i, j, kk: (i, j))], out_shape=[_sds((m, n), out_dtype)],
                 scratch_shapes=[pltpu.VMEM(acc_shape, F32)], rider=rider)
    return outs if rider else outs[0]


def _rms_rinv(x):
    return lax.rsqrt(jnp.mean(x * x, axis=-1, keepdims=True) + EPS)


def _rms_bwd_core(du, x, g):
    r = _rms_rinv(x)
    dug = du * g
    dx = r * (dug - x * ((r * r) * jnp.mean(dug * x, axis=-1, keepdims=True)))
    dg = jnp.sum(du * (x * r), axis=0, keepdims=True)
    return dx, dg


def _half_ones():
    r = lax.broadcasted_iota(jnp.int32, (LANES, LANES), 0)
    c = lax.broadcasted_iota(jnp.int32, (LANES, LANES), 1)
    return ((r < HD) == (c < HD)).astype(BF16)


def _half_sum(v, lo_half):
    if lo_half.dtype == jnp.bool_:
        s0 = jnp.sum(jnp.where(lo_half, v, 0.0), axis=-1, keepdims=True)
        s1 = jnp.sum(jnp.where(lo_half, 0.0, v), axis=-1, keepdims=True)
        return jnp.where(lo_half, s0, s1)
    hi = v.astype(BF16)
    lo = (v - hi.astype(F32)).astype(BF16)
    return _dot_nn(hi, lo_half) + _dot_nn(lo, lo_half)


def _head_rinv(x, lo_half):
    return lax.rsqrt(_half_sum(x * x, lo_half) * (1.0 / HD) + EPS)


def _head_norm_bwd(dn, x, g, lo_half):
    r = _head_rinv(x, lo_half)
    dng = dn * g
    dx = r * (dng - x * ((r * r) * (_half_sum(dng * x, lo_half) * (1.0 / HD))))
    dg = jnp.sum(dn * (x * r), axis=0, keepdims=True)
    return dx, dg


def _rope_swap(x, lane):
    l64 = lane & (HD - 1)
    return jnp.where(l64 < ROT // 2, pltpu.roll(x, LANES - ROT // 2, 1), pltpu.roll(x, ROT // 2, 1))


def _rope_fwd(x, cos, sin, lane):
    return x * cos + _rope_swap(x, lane) * sin


def _rope_bwd(dy, cos, sin, lane):
    return dy * cos + jnp.where((lane & (HD - 1)) < ROT, _rope_swap(dy * sin, lane), 0.0)


def _g2(g_ref):
    g = g_ref[...]
    return jnp.concatenate([g, g], axis=-1)


def _pairs(width):
    return [slice(LANES * c, LANES * (c + 1)) for c in range(width // LANES)]


def _pick_lane(block, lane, idx):
    return jnp.sum(jnp.where(lane == idx, block, 0.0), axis=-1, keepdims=True)


def _head_a(x, g, wa, gq, gk, b_pad):
    assert RT == TQ
    def body(x_ref, g_ref, w_ref, gq_ref, gk_ref, b_ref, u_ref, p_ref, qo_ref, ko_ref, vo_ref, c_ref, cbc_ref, carry):
        @pl.when(pl.program_id(0) == 0)
        def _():
            carry[...] = jnp.zeros_like(carry)

        xv = x_ref[...]
        u = ((xv * _rms_rinv(xv)) * g_ref[...]).astype(BF16)
        u_ref[...] = u
        for lo in range(0, NA, D):
            hi = min(lo + D, NA)
            p_ref[:, lo:hi] = _dot_nn(u, w_ref[:, lo:hi])
        lane = _lane_iota((RT, LANES))
        lo_half = lane < HD
        gq2, gk2 = _g2(gq_ref), _g2(gk_ref)
        for c in _pairs(D):
            q = p_ref[:, c]
            k = p_ref[:, D + c.start:D + c.stop]
            qo_ref[:, c] = (((q * _head_rinv(q, lo_half)) * gq2) * QSCALE).astype(BF16)
            ko_ref[:, c] = ((k * _head_rinv(k, lo_half)) * gk2).astype(BF16)
        vo_ref[...] = p_ref[:, 2 * D:3 * D].astype(BF16)

        z = p_ref[:, FOFF:FOFF + LANES] + b_ref[...]
        logf = jnp.minimum(z, 0.0) - jnp.log1p(jnp.exp(-jnp.abs(z)))
        r = lax.broadcasted_iota(jnp.int32, (RT, RT), 0)
        cc = lax.broadcasted_iota(jnp.int32, (RT, RT), 1)
        tri = (r >= cc).astype(F32)
        loc = jnp.dot(tri, logf, precision=lax.Precision.HIGHEST, preferred_element_type=F32) + carry[0:1, :]
        c_ref[:, 0, :] = loc.T[:N_HEADS]
        carry[0:1, :] = loc[RT - 1:RT, :]
        for h in range(N_HEADS):
            cbc_ref[:, LANES * h:LANES * (h + 1)] = jnp.broadcast_to(_pick_lane(loc, lane, h), (RT, LANES))

    row = lambda width: pl.BlockSpec((RT, width), lambda i: (i, 0))
    whole = lambda arr: pl.BlockSpec(arr.shape, lambda i: (0,) * arr.ndim, pipeline_mode=pl.Buffered(1))
    return pl.pallas_call(
        body, name="head_a", grid=(S // RT,),
        in_specs=[row(D), whole(g), whole(wa), whole(gq), whole(gk), whole(b_pad)],
        out_specs=[row(D), row(NA), row(D), row(D), row(D),
                   pl.BlockSpec((N_HEADS, None, 1, TQ), lambda i: (0, i, 0, 0)), row(N_HEADS * LANES)],
        out_shape=[_sds((S, D), BF16), _sds((S, NA), F32)] + [_sds((S, D), BF16)] * 3
        + [_sds((N_HEADS, S // TQ, 1, TQ), F32), _sds((S, N_HEADS * LANES), F32)],
        scratch_shapes=[pltpu.VMEM((8, LANES), F32)], compiler_params=_params())(x, g, wa, gq, gk, b_pad)


def _key_le_query(offset, keys=TK):
    r = lax.broadcasted_iota(jnp.int32, (keys, TQ), 0)
    c = lax.broadcasted_iota(jnp.int32, (keys, TQ), 1)
    return (r + offset) <= c


def _widen(tile):
    return jnp.concatenate([tile] * (TQ // LANES), axis=1)


def _fox_fwd(qn, kn, vb, proj, crow, cbc, rider=()):
    nq = S // TQ

    def body(q_ref, k_ref, v_ref, g_ref, cq_ref, cbc_ref, o_ref, z_ref, lse_ref, st_s, pt_s):
        i = pl.program_id(1)
        qs = [q_ref[:, HD * hh:HD * (hh + 1)] for hh in range(HPS)]
        cqs = [cq_ref[hh, 0] for hh in range(HPS)]

        def scores(s, hh):
            off = pl.multiple_of(s * KS, KS)
            kj = k_ref[pl.ds(off, KS), HD * hh:HD * (hh + 1)]
            return (_dot_nt(kj, qs[hh]) + cqs[hh]) - _widen(cbc_ref[pl.ds(off, KS), LANES * hh:LANES * (hh + 1)])

        def values(s, hh, pt):
            off = pl.multiple_of(s * KS, KS)
            return _dot_tn(v_ref[pl.ds(off, KS), HD * hh:HD * (hh + 1)], pt)

        def step(s, slot, carries, mask=None, last=False):
            if not last:
                for hh in range(HPS):
                    st_s[1 - slot, hh] = scores(s + 1, hh)
            pvs = [values(jnp.maximum(s - 1, 0), hh, pt_s[1 - slot, hh]) for hh in range(HPS)]
            out = []
            for hh in range(HPS):
                m, l, acc = carries[hh]
                st = st_s[slot, hh]
                if mask is not None:
                    st = jnp.where(mask, st, -jnp.inf)
                m_new = jnp.maximum(m, jnp.max(st, axis=0, keepdims=True))
                pt = jnp.exp(st - m_new)
                alpha = jnp.exp(m - m_new)
                pt_s[slot, hh] = pt.astype(BF16)
                out.append((m_new, alpha * l + jnp.sum(pt, axis=0, keepdims=True), alpha * (acc + pvs[hh])))
            return tuple(out)

        for hh in range(HPS):
            st_s[0, hh] = scores(0, hh)
            pt_s[1, hh] = jnp.zeros((KS, TQ), BF16)
        one = (jnp.full((1, TQ), -jnp.inf, F32), jnp.zeros((1, TQ), F32), jnp.zeros((HD, TQ), F32))
        carries = lax.fori_loop(0, i, lambda t, cr: step(2 * t + 1, 1, step(2 * t, 0, cr)), (one,) * HPS)
        carries = step(2 * i, 0, carries, mask=_key_le_query(0, KS))
        carries = step(2 * i + 1, 1, carries, mask=_key_le_query(KS, KS), last=True)
        accs = []
        for hh in range(HPS):
            m, l, acc = carries[hh]
            acc = acc + values(2 * i + 1, hh, pt_s[1, hh])
            accs.append(acc / l)
            lse_ref[hh, 0] = m + jnp.log(l)
        o = jnp.concatenate(accs, axis=0).T
        o_ref[...] = o
        g = g_ref[...]
        z_ref[...] = (o * (g * _sigmoid(g))).astype(BF16)

    qblk = pl.BlockSpec((TQ, HW), lambda hp, i: (i, hp))
    full = pl.BlockSpec((S, HW), lambda hp, i: (0, hp))
    rows = pl.BlockSpec((HPS, 1, 1, TQ), lambda hp, i: (hp, i, 0, 0))
    return _call(
        body, name="fox_fwd", args=(qn, kn, vb, proj, crow, cbc), grid=(N_HEADS // HPS, nq),
        in_specs=[qblk, full, full,
                  pl.BlockSpec((TQ, HW), lambda hp, i: (i, GOFF // HW + hp)),
                  rows, pl.BlockSpec((S, HPS * LANES), lambda hp, i: (0, hp))],
        out_specs=[qblk, qblk, rows],
        out_shape=[_sds((S, D), F32), _sds((S, D), BF16), _sds((N_HEADS, nq, 1, TQ), F32)],
        scratch_shapes=[pltpu.VMEM((2, HPS, KS, TQ), F32), pltpu.VMEM((2, HPS, KS, TQ), BF16)], rider=rider)


def _fox_bwd_pre(dh, w_out, proj, o):
    nq, rows = S // TQ, 2 * TQ
    per = rows // TQ

    def body(dh_ref, w_ref, g_ref, o_ref, do_ref, dg_ref, delta_ref):
        g = g_ref[...]
        sg = _sigmoid(g)
        dzv = _dot_nt(dh_ref[...].astype(BF16), w_ref[...])
        ov = o_ref[...]
        do = dzv * (g * sg)
        dg_ref[...] = (dzv * ov * (sg * (1.0 + g * (1.0 - sg)))).astype(BF16)
        do_ref[...] = do.astype(BF16)
        prod_t = (do * ov).T
        for h in range(N_HEADS):
            for b in range(per):
                delta_ref[h, b] = jnp.sum(prod_t[HD * h:HD * (h + 1), TQ * b:TQ * (b + 1)], axis=0, keepdims=True)

    row = pl.BlockSpec((rows, D), lambda i: (i, 0))
    return pl.pallas_call(
        body, name="fox_bwd_pre", grid=(S // rows,),
        in_specs=[row, pl.BlockSpec(w_out.shape, lambda i: (0, 0), pipeline_mode=pl.Buffered(1)),
                  pl.BlockSpec((rows, D), lambda i: (i, GOFF // D)), row],
        out_specs=[row, row, pl.BlockSpec((N_HEADS, per, 1, TQ), lambda i: (0, i, 0, 0))],
        out_shape=[_sds((S, D), BF16), _sds((S, D), BF16), _sds((N_HEADS, nq, 1, TQ), F32)],
        compiler_params=_params())(dh, w_out, proj, o)


def _fox_bwd(qn, kn, vb, dob, lse, delta, crow, cbc, rider=()):
    nq, nkb = S // TQ, S // TK

    def body(q_ref, k_ref, v_ref, do_ref, lse_ref, del_ref, cq_ref, cbc_ref,
             dk_ref, dv_ref, dcs_ref, dq_ref, dr_ref, st_s, dp_s, pt_s, ds_s, dq_acc, dr_acc):
        j = pl.program_id(1)

        @pl.when(j == 0)
        def _():
            dq_acc[...] = jnp.zeros_like(dq_acc)
            dr_acc[...] = jnp.zeros_like(dr_acc)

        kjs = [k_ref[:, HD * hh:HD * (hh + 1)] for hh in range(HPS)]
        vjs = [v_ref[:, HD * hh:HD * (hh + 1)] for hh in range(HPS)]

        def rows_of(ref, u, hh):
            off = pl.multiple_of(u * TQ, TQ)
            return ref[pl.ds(off, TQ), HD * hh:HD * (hh + 1)]

        def products(u, hh):
            st = (_dot_nt(kjs[hh], rows_of(q_ref, u, hh)) + cq_ref[hh, u]) - _widen(
                cbc_ref[:, LANES * hh:LANES * (hh + 1)])
            return st, _dot_nt(vjs[hh], rows_of(do_ref, u, hh))

        def step(u, slot, carries, masked=False):
            nxt = jnp.minimum(u + 1, nq - 1)
            for hh in range(HPS):
                st_s[1 - slot, hh], dp_s[1 - slot, hh] = products(nxt, hh)
            prev = jnp.maximum(u - 1, 0)
            dvs = [_dot_nn(pt_s[1 - slot, hh], rows_of(do_ref, prev, hh)) for hh in range(HPS)]
            dks = [_dot_nn(ds_s[1 - slot, hh], rows_of(q_ref, prev, hh)) for hh in range(HPS)]
            for hh in range(HPS):
                dq_acc[hh, prev] += _dot_tn(kjs[hh], ds_s[1 - slot, hh])
            out = []
            for hh in range(HPS):
                dk, dv, dcs = carries[hh]
                st = st_s[slot, hh]
                if masked:
                    st = jnp.where(_key_le_query((j - u) * TQ), st, -jnp.inf)
                pt = jnp.exp(st - lse_ref[hh, u])
                dst = pt * (dp_s[slot, hh] - del_ref[hh, u])
                pt_s[slot, hh] = pt.astype(BF16)
                ds_s[slot, hh] = dst.astype(BF16)
                dr_acc[hh, u] += jnp.sum(dst, axis=0, keepdims=True)
                out.append((dk + dks[hh], dv + dvs[hh], dcs + (dst[:, :LANES] + dst[:, LANES:])))
            return tuple(out)

        t0 = j // 2
        for hh in range(HPS):
            st_s[0, hh], dp_s[0, hh] = products(2 * t0, hh)
            pt_s[1, hh] = jnp.zeros((TK, TQ), BF16)
            ds_s[1, hh] = jnp.zeros((TK, TQ), BF16)
        one = (jnp.zeros((TK, HD), F32), jnp.zeros((TK, HD), F32), jnp.zeros((TK, LANES), F32))
        carries = step(2 * t0 + 1, 1, step(2 * t0, 0, (one,) * HPS, masked=True), masked=True)
        carries = lax.fori_loop(t0 + 1, nq // 2, lambda t, cr: step(2 * t + 1, 1, step(2 * t, 0, cr)), carries)
        dks, dvs = [], []
        lane = _lane_iota((TK, LANES))
        dcs_all = jnp.zeros((TK, LANES), F32)
        for hh in range(HPS):
            dk, dv, dcs = carries[hh]
            dks.append(dk + _dot_nn(ds_s[1, hh], rows_of(q_ref, nq - 1, hh)))
            dvs.append(dv + _dot_nn(pt_s[1, hh], rows_of(do_ref, nq - 1, hh)))
            dq_acc[hh, nq - 1] += _dot_tn(kjs[hh], ds_s[1, hh])
            dcs_all = jnp.where(lane == HPS * pl.program_id(0) + hh, -jnp.sum(dcs, axis=-1, keepdims=True), dcs_all)
        dcs_ref[0] = dcs_all
        dk_ref[...] = jnp.concatenate(dks, axis=-1)
        dv_ref[...] = jnp.concatenate(dvs, axis=-1).astype(BF16)

        @pl.when(j == nkb - 1)
        def _():
            for i in range(nq):
                dq_ref[TQ * i:TQ * (i + 1), :] = jnp.concatenate([dq_acc[hh, i] for hh in range(HPS)], axis=0).T
            dr_ref[...] = dr_acc[...]

    kblk = pl.BlockSpec((TK, HW), lambda hp, j: (j, hp))
    full = pl.BlockSpec((S, HW), lambda hp, j: (0, hp))
    rows = pl.BlockSpec((HPS, nq, 1, TQ), lambda hp, j: (hp, 0, 0, 0))
    cblk = pl.BlockSpec((TK, HPS * LANES), lambda hp, j: (j, hp))
    return _call(
        body, name="fox_bwd", args=(qn, kn, vb, dob, lse, delta, crow, cbc), grid=(N_HEADS // HPS, nkb),
        in_specs=[full, kblk, kblk, full, rows, rows, rows, cblk],
        out_specs=[kblk, kblk, pl.BlockSpec((1, TK, LANES), lambda hp, j: (hp, j, 0)), full, rows],
        out_shape=[_sds((S, D), F32), _sds((S, D), BF16), _sds((N_HEADS // HPS, S, LANES), F32), _sds((S, D), F32),
                   _sds((N_HEADS, nq, 1, TQ), F32)],
        scratch_shapes=[pltpu.VMEM((2, HPS, TK, TQ), F32), pltpu.VMEM((2, HPS, TK, TQ), F32),
                        pltpu.VMEM((2, HPS, TK, TQ), BF16), pltpu.VMEM((2, HPS, TK, TQ), BF16),
                        pltpu.VMEM((HPS, nq, HD, TQ), F32), pltpu.VMEM((HPS, nq, 1, TQ), F32)], rider=rider)


def _prep_a_bwd(dq, dk, dv, dgate, drow, dcs, proj, b_pad, gq, gk):
    nt = S // TM

    def body(dq_ref, dk_ref, dv_ref, dgt_ref, dr_ref, dcs_ref, xq_ref, xk_ref, f_ref, b_ref, gq_ref, gk_ref,
             o_ref, dgq_ref, dgk_ref, db_ref, carry):
        @pl.when(pl.program_id(0) == 0)
        def _():
            carry[...] = jnp.zeros_like(carry)
            dgq_ref[...] = jnp.zeros_like(dgq_ref)
            dgk_ref[...] = jnp.zeros_like(dgk_ref)
            db_ref[...] = jnp.zeros_like(db_ref)

        lane = _lane_iota((TM, LANES))
        lo_half = _half_ones()
        gq2, gk2 = _g2(gq_ref), _g2(gk_ref)
        dgq, dgk = jnp.zeros((1, LANES), F32), jnp.zeros((1, LANES), F32)
        for c in _pairs(D):
            dxq, dg = _head_norm_bwd(dq_ref[:, c] * QSCALE, xq_ref[:, c], gq2, lo_half)
            o_ref[:, c] = dxq.astype(BF16)
            dgq = dgq + dg
            dxk, dg = _head_norm_bwd(dk_ref[:, c], xk_ref[:, c], gk2, lo_half)
            o_ref[:, D + c.start:D + c.stop] = dxk.astype(BF16)
            dgk = dgk + dg
        dgq_ref[...] += dgq
        dgk_ref[...] += dgk
        o_ref[:, 2 * D:3 * D] = dv_ref[...]
        o_ref[:, GOFF:GOFF + D] = dgt_ref[...]

        dc = jnp.concatenate([dr_ref[:, 0, :], jnp.zeros((LANES - N_HEADS, TM), F32)], axis=0).T
        for group in range(N_HEADS // HPS):
            dc = dc + dcs_ref[group]
        r = lax.broadcasted_iota(jnp.int32, (TM, TM), 0)
        c = lax.broadcasted_iota(jnp.int32, (TM, TM), 1)
        tri = (c >= r).astype(F32)
        dlogf = jnp.dot(tri, dc, precision=lax.Precision.HIGHEST, preferred_element_type=F32) + carry[0:1, :]
        carry[0:1, :] = dlogf[0:1, :]
        df = dlogf * (1.0 / (1.0 + jnp.exp(f_ref[...] + b_ref[...])))
        db_ref[...] += jnp.sum(df, axis=0, keepdims=True)
        o_ref[:, FOFF:FOFF + LANES] = df.astype(BF16)
        o_ref[:, FOFF + LANES:NA] = jnp.zeros((TM, NA - FOFF - LANES), BF16)

    rev = lambda width, col: pl.BlockSpec((TM, width), lambda i: (nt - 1 - i, col))
    gspec = pl.BlockSpec((1, HD), lambda i: (0, 0))
    acc = pl.BlockSpec((1, LANES), lambda i: (0, 0))
    return pl.pallas_call(
        body, name="prep_a_bwd", grid=(nt,),
        in_specs=[rev(D, 0), rev(D, 0), rev(D, 0), rev(D, 0),
                  pl.BlockSpec((N_HEADS, None, 1, TM), lambda i: (0, nt - 1 - i, 0, 0)),
                  pl.BlockSpec((N_HEADS // HPS, TM, LANES), lambda i: (0, nt - 1 - i, 0)),
                  rev(D, 0), rev(D, 1), rev(LANES, FOFF // LANES), acc, gspec, gspec],
        out_specs=[rev(NA, 0), acc, acc, acc],
        out_shape=[_sds((S, NA), BF16)] + [_sds((1, LANES), F32)] * 3,
        scratch_shapes=[pltpu.VMEM((8, LANES), F32)],
        compiler_params=_params())(dq, dk, dv, dgate, drow, dcs, proj, proj, proj, b_pad, gq, gk)


def _head_b(x, z_a, w_out_a, g_kv, g_b, w_kv, w_in_b, gq, gk, cos2, sin2):
    nkv = w_kv.shape[1] // 2

    def body(x_ref, z_ref, wo_ref, gkv_ref, gb_ref, wkv_ref, wb_ref, gq_ref, gk_ref, c_ref, s_ref,
             h_ref, ukv_ref, ub_ref, kv_ref, pb_ref, qo_ref, ko_ref, vo_ref):
        xv = x_ref[...] + _dot_nn(z_ref[...], wo_ref[...])
        h_ref[...] = xv
        xn = xv * _rms_rinv(xv)
        ukv = (xn * gkv_ref[...]).astype(BF16)
        ub = (xn * gb_ref[...]).astype(BF16)
        ukv_ref[...] = ukv
        ub_ref[...] = ub
        kv_ref[...] = _dot_nn(ukv, wkv_ref[...])
        for lo in range(0, 2 * D, D):
            pb_ref[:, lo:lo + D] = _dot_nn(ub, wb_ref[:, lo:lo + D])
        lane = _lane_iota((RT, LANES))
        lo_half = lane < HD
        cos, sin = c_ref[...], s_ref[...]
        gq2, gk2 = _g2(gq_ref), _g2(gk_ref)
        for c in _pairs(D):
            q = pb_ref[:, c]
            qo_ref[:, c] = (_rope_fwd((q * _head_rinv(q, lo_half)) * gq2, cos, sin, lane) * QSCALE).astype(BF16)
        for c in _pairs(nkv):
            k = kv_ref[:, c]
            ko_ref[:, c] = _rope_fwd((k * _head_rinv(k, lo_half)) * gk2, cos, sin, lane).astype(BF16)
        vo_ref[...] = kv_ref[:, nkv:2 * nkv].astype(BF16)

    row = lambda width: pl.BlockSpec((RT, width), lambda i: (i, 0))
    whole = lambda arr: pl.BlockSpec(arr.shape, lambda i: (0,) * arr.ndim, pipeline_mode=pl.Buffered(1))
    return pl.pallas_call(
        body, name="head_b", grid=(S // RT,),
        in_specs=[row(D), row(D), whole(w_out_a), whole(g_kv), whole(g_b), whole(w_kv), whole(w_in_b), whole(gq),
                  whole(gk), row(LANES), row(LANES)],
        out_specs=[row(D), row(D), row(D), row(2 * nkv), row(2 * D), row(D), row(nkv), row(nkv)],
        out_shape=[_sds((S, D), F32), _sds((S, D), BF16), _sds((S, D), BF16), _sds((S, 2 * nkv), F32),
                   _sds((S, 2 * D), F32), _sds((S, D), BF16), _sds((S, nkv), BF16), _sds((S, nkv), BF16)],
        compiler_params=_params())(x, z_a, w_out_a, g_kv, g_b, w_kv, w_in_b, gq, gk, cos2, sin2)


N_KV, GRP = 4, 4


def _swa_mask(n):
    r = lax.broadcasted_iota(jnp.int32, (2 * WIN, GRP * WIN), 0)
    q = lax.broadcasted_iota(jnp.int32, (2 * WIN, GRP * WIN), 1) & (WIN - 1)
    return (r > q) & (r <= q + WIN) & ((r >= WIN) | (n > 0))


def _stack4(ref_or_val, base):
    return jnp.concatenate([ref_or_val[:, base + HD * g: base + HD * (g + 1)] for g in range(GRP)], axis=0)


def _unstack4(xt):
    return jnp.concatenate([xt[:, WIN * g:WIN * (g + 1)] for g in range(GRP)], axis=0).T


def _band(prev_ref, cur_ref, kh):
    return jnp.concatenate([prev_ref[:, HD * kh:HD * (kh + 1)], cur_ref[:, HD * kh:HD * (kh + 1)]], axis=0)


def _sink_row(s_ref, first):
    lane = _lane_iota((1, GRP * WIN))
    row = jnp.full((1, GRP * WIN), s_ref[first + GRP - 1], F32)
    for g in range(GRP - 2, -1, -1):
        row = jnp.where(lane < WIN * (g + 1), s_ref[first + g], row)
    return row


def _swa_fwd(qb, ksh, vsh, pb, sinks):
    nb = S // WIN

    def body(q_ref, kp_ref, kc_ref, vp_ref, vc_ref, g_ref, s_ref, o_ref, z_ref, lse_ref):
        n = pl.program_id(0)
        valid = _swa_mask(n)
        outs = []
        for kh in range(N_KV):
            kb, vb = _band(kp_ref, kc_ref, kh), _band(vp_ref, vc_ref, kh)
            st = jnp.where(valid, _dot_nt(kb, _stack4(q_ref, GRP * HD * kh)), -jnp.inf)
            sink = _sink_row(s_ref, GRP * kh)
            m = jnp.maximum(jnp.max(st, axis=0, keepdims=True), sink)
            pt = jnp.exp(st - m)
            l = jnp.sum(pt, axis=0, keepdims=True) + jnp.exp(sink - m)
            outs.append(_unstack4(_dot_tn(vb, pt.astype(BF16)) / l))
            lse = m + jnp.log(l)
            for g in range(GRP):
                lse_ref[GRP * kh + g, 0] = lse[:, WIN * g:WIN * (g + 1)]
        o = jnp.concatenate(outs, axis=-1)
        o_ref[...] = o
        g = g_ref[...]
        z_ref[...] = (o * (g * _sigmoid(g))).astype(BF16)

    row = pl.BlockSpec((WIN, D), lambda n: (n, 0))
    prev = pl.BlockSpec((WIN, N_KV * HD), lambda n: (jnp.maximum(n - 1, 0), 0))
    cur = pl.BlockSpec((WIN, N_KV * HD), lambda n: (n, 0))
    return pl.pallas_call(
        body, name="swa_fwd", grid=(nb,),
        in_specs=[row, prev, cur, prev, cur, pl.BlockSpec((WIN, D), lambda n: (n, 1)),
                  pl.BlockSpec(memory_space=pltpu.SMEM)],
        out_specs=[row, row, pl.BlockSpec((N_HEADS, 1, 1, WIN), lambda n: (0, n, 0, 0))],
        out_shape=[_sds((S, D), F32), _sds((S, D), BF16), _sds((N_HEADS, nb, 1, WIN), F32)],
        compiler_params=_params())(qb, ksh, ksh, vsh, vsh, pb, sinks)


def _swa_bwd(qb, ksh, vsh, dz, o, lse, pb, sinks, gq, cos2, sin2):
    nb = S // WIN

    def body(q_ref, kp_ref, kc_ref, vp_ref, vc_ref, dz_ref, o_ref, lse_ref, x_ref, g_ref, s_ref, gq_ref, c_ref, sn_ref,
             dpb_ref, dka_ref, dkb_ref, dva_ref, dvb_ref, dsink_ref, dgq_ref):
        n = pl.program_id(0)

        @pl.when(n == 0)
        def _():
            dsink_ref[...] = jnp.zeros_like(dsink_ref)
            dgq_ref[...] = jnp.zeros_like(dgq_ref)

        valid = _swa_mask(n)
        g = g_ref[...]
        sg = _sigmoid(g)
        dzv = dz_ref[...]
        ov = o_ref[...]
        do = dzv * (g * sg)
        dpb_ref[:, D:2 * D] = (dzv * ov * (sg * (1.0 + g * (1.0 - sg)))).astype(BF16)
        prod_t = (do * ov).T
        lane1 = _lane_iota((1, LANES))
        dqs, dkas, dkbs, dvas, dvbs = [], [], [], [], []
        dsink = jnp.zeros((1, LANES), F32)
        for kh in range(N_KV):
            kb, vb = _band(kp_ref, kc_ref, kh), _band(vp_ref, vc_ref, kh)
            base = GRP * HD * kh
            qs = _stack4(q_ref, base)
            dos = _stack4(do, base).astype(BF16)
            delta = jnp.concatenate(
                [jnp.sum(prod_t[base + HD * gg:base + HD * (gg + 1), :], axis=0, keepdims=True)
                 for gg in range(GRP)], axis=1)
            lse = jnp.concatenate([lse_ref[GRP * kh + gg, 0] for gg in range(GRP)], axis=1)
            st = jnp.where(valid, _dot_nt(kb, qs), -jnp.inf)
            pt = jnp.exp(st - lse)
            dst = pt * (_dot_nt(vb, dos) - delta)
            dsb = dst.astype(BF16)
            dqs.append(_unstack4(_dot_tn(kb, dsb)))
            dkband = _dot_nn(dsb, qs)
            dvband = _dot_nn(pt.astype(BF16), dos)
            dkbs.append(dkband[0:WIN, :])
            dkas.append(dkband[WIN:2 * WIN, :])
            dvbs.append(dvband[0:WIN, :])
            dvas.append(dvband[WIN:2 * WIN, :])
            ps_delta = jnp.exp(_sink_row(s_ref, GRP * kh) - lse) * delta
            for gg in range(GRP):
                val = jnp.sum(ps_delta[:, WIN * gg:WIN * (gg + 1)], axis=1, keepdims=True)
                dsink = dsink - jnp.where(lane1 == GRP * kh + gg, val, 0.0)
        dka_ref[...] = jnp.concatenate(dkas, axis=-1)
        dkb_ref[...] = jnp.concatenate(dkbs, axis=-1)
        dva_ref[...] = jnp.concatenate(dvas, axis=-1)
        dvb_ref[...] = jnp.concatenate(dvbs, axis=-1)
        dsink_ref[...] += dsink

        lane = _lane_iota((WIN, LANES))
        g2, cos, sin = _g2(gq_ref), c_ref[...], sn_ref[...]
        lo_half = _half_ones()
        dg_tot = jnp.zeros((1, LANES), F32)
        for kh in range(N_KV):
            for c in _pairs(GRP * HD):
                cols = slice(GRP * HD * kh + c.start, GRP * HD * kh + c.stop)
                dn = _rope_bwd(dqs[kh][:, c] * QSCALE, cos, sin, lane)
                dx, dg = _head_norm_bwd(dn, x_ref[:, cols], g2, lo_half)
                dpb_ref[:, cols] = dx.astype(BF16)
                dg_tot = dg_tot + dg
        dgq_ref[...] += dg_tot

    row = pl.BlockSpec((WIN, D), lambda n: (n, 0))
    prev = pl.BlockSpec((WIN, N_KV * HD), lambda n: (jnp.maximum(n - 1, 0), 0))
    cur = pl.BlockSpec((WIN, N_KV * HD), lambda n: (n, 0))
    acc = pl.BlockSpec((1, LANES), lambda n: (0, 0))
    tab = pl.BlockSpec((WIN, LANES), lambda n: (n, 0))
    return pl.pallas_call(
        body, name="swa_bwd", grid=(nb,),
        in_specs=[row, prev, cur, prev, cur, row, row, pl.BlockSpec((N_HEADS, 1, 1, WIN), lambda n: (0, n, 0, 0)),
                  row, pl.BlockSpec((WIN, D), lambda n: (n, 1)), pl.BlockSpec(memory_space=pltpu.SMEM),
                  pl.BlockSpec((1, HD), lambda n: (0, 0)), tab, tab],
        out_specs=[pl.BlockSpec((WIN, 2 * D), lambda n: (n, 0)), cur, cur, cur, cur, acc, acc],
        out_shape=[_sds((S, 2 * D), BF16)] + [_sds((S, 256), F32)] * 4 + [_sds((1, LANES), F32)] * 2,
        compiler_params=_params())(qb, ksh, ksh, vsh, vsh, dz, o, lse, pb, pb, sinks, gq, cos2, sin2)


def _prep_kv_bwd(dka, dkb, dva, dvb, kv, gk, cos2, sin2):
    nt = S // RB
    per = RB // WIN

    def shifted(cur_ref, nxt_ref, has_next):
        return jnp.concatenate([cur_ref[WIN:RB, :], jnp.where(has_next, nxt_ref[...], 0.0)], axis=0)

    def body(dka_ref, dkb_ref, dkn_ref, dva_ref, dvb_ref, dvn_ref, x_ref, g_ref, c_ref, s_ref, o_ref, dgk_ref):
        i, j = pl.program_id(0), pl.program_id(1)

        @pl.when((i == 0) & (j == 0))
        def _():
            dgk_ref[...] = jnp.zeros_like(dgk_ref)

        has_next = i < nt - 1

        @pl.when(j == 0)
        def _():
            lane = _lane_iota((RB, LANES))
            g2, cos, sin = _g2(g_ref), c_ref[...], s_ref[...]
            dy_all = dka_ref[...] + shifted(dkb_ref, dkn_ref, has_next)
            dg_tot = jnp.zeros((1, LANES), F32)
            lo_half = _half_ones()
            for c in _pairs(CB):
                dn = _rope_bwd(dy_all[:, c], cos, sin, lane)
                dx, dg = _head_norm_bwd(dn, x_ref[:, c], g2, lo_half)
                o_ref[:, c] = dx.astype(BF16)
                dg_tot = dg_tot + dg
            dgk_ref[...] += dg_tot

        @pl.when(j == 1)
        def _():
            o_ref[...] = (dva_ref[...] + shifted(dvb_ref, dvn_ref, has_next)).astype(BF16)

    cur = pl.BlockSpec((RB, CB), lambda i, j: (i, 0))
    nxt = pl.BlockSpec((WIN, CB), lambda i, j: (jnp.minimum(per * (i + 1), S // WIN - 1), 0))
    tab = pl.BlockSpec((RB, LANES), lambda i, j: (i, 0))
    return pl.pallas_call(
        body, name="prep_kv_bwd", grid=(nt, 2),
        in_specs=[cur, cur, nxt, cur, cur, nxt, cur, pl.BlockSpec((1, HD), lambda i, j: (0, 0)), tab, tab],
        out_specs=[pl.BlockSpec((RB, CB), lambda i, j: (i, j)), pl.BlockSpec((1, LANES), lambda i, j: (0, 0))],
        out_shape=[_sds((S, 2 * CB), BF16), _sds((1, LANES), F32)],
        compiler_params=_params())(dka, dkb, dkb, dva, dvb, dvb, kv, gk, cos2, sin2)


def _out_b_loss(z, w_out, h1, tgt):
    tm = 2 * RT

    def body(z_ref, w_ref, h_ref, t_ref, dy_ref, l_ref):
        @pl.when(pl.program_id(0) == 0)
        def _():
            l_ref[...] = jnp.zeros_like(l_ref)

        e = (h_ref[...] + _dot_nn(z_ref[...], w_ref[...])) - t_ref[...]
        dy_ref[...] = e * (1.0 / D)
        l_ref[...] += jnp.sum(jnp.sum(e * e, axis=-1, keepdims=True), axis=0, keepdims=True)

    row = pl.BlockSpec((tm, D), lambda i: (i, 0))
    return pl.pallas_call(
        body, name="out_b_loss", grid=(S // tm,),
        in_specs=[row, pl.BlockSpec(w_out.shape, lambda i: (0, 0), pipeline_mode=pl.Buffered(1)), row, row],
        out_specs=[row, pl.BlockSpec((1, LANES), lambda i: (0, 0))],
        out_shape=[_sds((S, D), F32), _sds((1, LANES), F32)], compiler_params=_params())(z, w_out, h1, tgt)


def _du_a_rms_bwd(dproj, wa, x, g, dres, after):
    tm, tk = 1024, NA // 2
    nk = NA // tk

    def body(a_ref, b_ref, x_ref, g_ref, dr_ref, after_ref, dx_ref, dg_ref, acc):
        i, kk = pl.program_id(0), pl.program_id(1)

        @pl.when((i == 0) & (kk == 0))
        def _():
            dg_ref[...] = jnp.zeros_like(dg_ref)

        p = _dot_nt(a_ref[...], b_ref[...])

        @pl.when(kk == 0)
        def _():
            acc[...] = p

        @pl.when(kk == nk - 1)
        def _():
            dx, dg = _rms_bwd_core(acc[...] + p, x_ref[...], g_ref[...])
            dx_ref[...] = dr_ref[...] + dx
            dg_ref[...] += dg

    assert nk == 2
    row = pl.BlockSpec((tm, D), lambda i, kk: (i, 0))
    vec = pl.BlockSpec((1, D), lambda i, kk: (0, 0))
    return pl.pallas_call(
        body, name="du_a_rms_bwd", grid=(S // tm, nk),
        in_specs=[pl.BlockSpec((tm, tk), lambda i, kk: (i, kk)), pl.BlockSpec((D, tk), lambda i, kk: (0, kk)),
                  row, vec, row, pl.BlockSpec(after.shape, lambda i, kk: (0, 0))],
        out_specs=[row, vec], out_shape=[_sds((S, D), F32), _sds((1, D), F32)],
        scratch_shapes=[pltpu.VMEM((tm, D), F32)], compiler_params=_params())(dproj, wa, x, g, dres, after)


def _du_b_rms_bwd(dpb, w_in_b, dkv, w_kv, h1, g_b, g_kv, dy):
    tm = 2 * RT

    def body(ab_ref, wb_ref, akv_ref, wkv_ref, x_ref, gb_ref, gkv_ref, dy_ref, dh_ref, dgb_ref, dgkv_ref):
        @pl.when(pl.program_id(0) == 0)
        def _():
            dgb_ref[...] = jnp.zeros_like(dgb_ref)
            dgkv_ref[...] = jnp.zeros_like(dgkv_ref)

        x = x_ref[...]
        dx1, dg1 = _rms_bwd_core(_dot_nt(ab_ref[...], wb_ref[...]), x, gb_ref[...])
        dx2, dg2 = _rms_bwd_core(_dot_nt(akv_ref[...], wkv_ref[...]), x, gkv_ref[...])
        dh_ref[...] = dy_ref[...] + dx1 + dx2
        dgb_ref[...] += dg1
        dgkv_ref[...] += dg2

    row = lambda width: pl.BlockSpec((tm, width), lambda i: (i, 0))
    whole = lambda arr: pl.BlockSpec(arr.shape, lambda i: (0, 0), pipeline_mode=pl.Buffered(1))
    vec = pl.BlockSpec((1, D), lambda i: (0, 0))
    return pl.pallas_call(
        body, name="du_b_rms_bwd", grid=(S // tm,),
        in_specs=[row(dpb.shape[1]), whole(w_in_b), row(dkv.shape[1]), whole(w_kv), row(D), vec, vec, row(D)],
        out_specs=[row(D), vec, vec], out_shape=[_sds((S, D), F32), _sds((1, D), F32), _sds((1, D), F32)],
        compiler_params=_params())(dpb, w_in_b, dkv, w_kv, h1, g_b, g_kv, dy)


GATHER_CHUNKS = 4


def _gather_first(w_in_a, w_out_a, w_kv, w_in_b, w_out_b, norm_a_g):
    rows, cols = w_in_a.shape[-2:]
    groups = rows // LANES
    cols_pad = -(-cols // LANES) * LANES

    def body(wia_ref, woa_ref, wkv_ref, wib_ref, wob_ref, ga_ref,
             wa_g, ga_g, woa_s, wkv_s, wib_s, wob_s, wa_s, st_a, st_oa, st_kv, st_ib, st_ob, plane_t, load_sems, *sems):
        sources = [wia_ref, woa_ref.at[0], wkv_ref, wib_ref.at[0], wob_ref.at[0]]
        stages = [st_a, st_oa, st_kv, st_ib, st_ob]
        loads = [pltpu.make_async_copy(src, dst, load_sems.at[i]) for i, (src, dst) in enumerate(zip(sources, stages))]
        for cp in loads:
            cp.start()
        loads[0].wait()
        plane_t[cols_pad - LANES:, :] = jnp.zeros((LANES, LANES), F32)
        for j in range(groups):
            for c0 in range(0, cols, 64):
                n = min(64, cols - c0)
                plane_t[c0:c0 + n, :] = st_a[pl.ds(c0 * groups + j, n, stride=groups), :]
            for c0 in range(0, cols, LANES):
                n = min(LANES, cols - c0)
                wa_s[j * LANES:(j + 1) * LANES, c0:c0 + n] = plane_t[c0:c0 + LANES, :].T[:, :n].astype(BF16)

        def cast_the_rest():
            for cp, stage, out in zip(loads[1:], stages[1:], [woa_s, wkv_s, wib_s, wob_s]):
                cp.wait()
                out[...] = stage[...].astype(BF16)

        chunks = [pl.ds(r0, rows // GATHER_CHUNKS) for r0 in range(0, rows, rows // GATHER_CHUNKS)]
        _gather_two_level(
            [wa_s.at[rc] for rc in chunks] + [ga_ref],
            [lambda dev, rc=rc: wa_g.at[dev, rc] for rc in chunks] + [lambda dev: ga_g.at[dev]],
            sems, meanwhile=cast_the_rest)

    vmem = pl.BlockSpec(memory_space=pltpu.VMEM)
    anyspec = pl.BlockSpec(memory_space=pl.ANY)
    shard = lambda w: _sds(w.shape[-2:], BF16)
    stage = lambda w: pltpu.VMEM(w.shape[-2:], F32)
    return pl.pallas_call(
        body, name="gather_first", in_specs=[anyspec] * 5 + [vmem],
        out_specs=[anyspec, anyspec, vmem, vmem, vmem, vmem],
        out_shape=[_sds((N_DEV,) + w_in_a.shape[-2:], BF16), _sds((N_DEV,) + norm_a_g.shape, F32),
                   shard(w_out_a), shard(w_kv), shard(w_in_b), shard(w_out_b)],
        scratch_shapes=[pltpu.VMEM(w_in_a.shape[-2:], BF16), pltpu.VMEM((cols * groups, LANES), F32), stage(w_out_a),
                        stage(w_kv), stage(w_in_b), stage(w_out_b), pltpu.VMEM((cols_pad, LANES), F32),
                        pltpu.SemaphoreType.DMA((5,))] + _exchange_sems(GATHER_CHUNKS + 1),
        compiler_params=pltpu.CompilerParams(vmem_limit_bytes=VMEM_LIMIT, has_side_effects=True))(
            _entry_view(w_in_a).reshape(cols * groups, LANES), w_out_a, w_kv, w_in_b, w_out_b, norm_a_g)


def _padded_col(c):
    if c < RAW_F:
        return c
    return FOFF + (c - RAW_F) if c < RAW_G else GOFF + (c - RAW_G)


def _shard_pieces():
    width = NA_RAW // N_DEV
    pieces = []
    for d in range(N_DEV):
        cuts = [width * d] + [c for c in (RAW_F, RAW_G) if width * d < c < width * (d + 1)] + [width * (d + 1)]
        for lo, hi in zip(cuts[:-1], cuts[1:]):
            pieces.append((d, lo - width * d, _padded_col(lo), hi - lo))
    return pieces


def _unshard_wa(wa_g):
    def body(w_ref, o_ref):
        o_ref[:, FOFF + N_HEADS:NA] = jnp.zeros((TM, NA - FOFF - N_HEADS), BF16)
        for d, src, dst, width in _shard_pieces():
            o_ref[:, dst:dst + width] = w_ref[d, :, src:src + width]

    return pl.pallas_call(
        body, name="unshard_wa", grid=(D // TM,),
        in_specs=[pl.BlockSpec((N_DEV, TM, NA_RAW // N_DEV), lambda i: (0, i, 0))],
        out_specs=pl.BlockSpec((TM, NA), lambda i: (i, 0)), out_shape=_sds((D, NA), BF16),
        compiler_params=_params())(wa_g)


def _reshard_pair_reduce(dwa):
    n_chip, nt = N_DEV // 2, D // TM
    width = NA_RAW // N_DEV

    def body(g_ref, o_ref, slots_v, sib_v, send_sems, recv_sems):
        i = pl.program_id(0)
        x, y, c = lax.axis_index("x"), lax.axis_index("y"), lax.axis_index("c")

        def tile_rows(tile):
            return pl.ds(tile * TM if isinstance(tile, int) else pl.multiple_of(tile * TM, TM), TM)

        def give(j, tile):
            return pltpu.make_async_remote_copy(
                src_ref=slots_v.at[2 * j + 1 - c, tile_rows(tile)], dst_ref=sib_v.at[j, tile_rows(tile)],
                send_sem=send_sems.at[j, tile], recv_sem=recv_sems.at[j, tile], device_id=(x, y, 1 - c),
                device_id_type=pl.DeviceIdType.MESH)

        for d, src, dst, w in _shard_pieces():
            slots_v[d, tile_rows(i), src:src + w] = g_ref[:, dst:dst + w]
        for j in range(n_chip):
            give(j, i).start()

        @pl.when(i == nt - 1)
        def _():
            for tile in range(nt):
                for j in range(n_chip):
                    give(j, tile).wait()
            for j in range(n_chip):
                for r0 in range(0, D, 64):
                    mine = slots_v[2 * j + c, r0:r0 + 64, :].astype(F32)
                    o_ref[j, r0:r0 + 64, :] = (mine + sib_v[j, r0:r0 + 64, :].astype(F32)).astype(BF16)

    half = _sds((n_chip, D, width), dwa.dtype)
    return pl.pallas_call(
        body, name="reshard_pair_reduce", grid=(nt,), in_specs=[pl.BlockSpec((TM, NA), lambda i: (i, 0))],
        out_specs=pl.BlockSpec(half.shape, lambda i: (0, 0, 0)), out_shape=half,
        scratch_shapes=[pltpu.VMEM((N_DEV, D, width), dwa.dtype), pltpu.VMEM(half.shape, dwa.dtype),
                        pltpu.SemaphoreType.DMA((n_chip, nt)), pltpu.SemaphoreType.DMA((n_chip, nt))],
        compiler_params=pltpu.CompilerParams(vmem_limit_bytes=VMEM_LIMIT, has_side_effects=True))(dwa)


CHIP_FLIPS = (2, 4, 6)


def _chip_exchange_start(partial):
    n = len(CHIP_FLIPS)

    def body(p_ref, land_ref, *rest):
        sends, recvs, token = rest[:n], rest[n:2 * n], rest[2 * n + 2]
        x, y, c = lax.axis_index("x"), lax.axis_index("y"), lax.axis_index("c")
        me = 4 * x + 2 * y + c
        for idx, k in enumerate(CHIP_FLIPS):
            pltpu.make_async_remote_copy(
                src_ref=p_ref.at[(me ^ k) >> 1], dst_ref=land_ref.at[me >> 1], send_sem=sends[idx],
                recv_sem=recvs[idx], device_id=(x ^ ((k >> 2) & 1), y ^ ((k >> 1) & 1), c),
                device_id_type=pl.DeviceIdType.MESH).start()
        token[...] = jnp.zeros_like(token)

    hbm = pl.BlockSpec(memory_space=pltpu.HBM)
    sem = pl.BlockSpec(memory_space=pltpu.SEMAPHORE)
    buf = pltpu.HBM(partial.shape, partial.dtype)
    return pl.pallas_call(
        body, name="chip_exchange_start",
        out_shape=(pltpu.SemaphoreType.DMA(()),) * (2 * n) + (buf, buf, _sds((8, LANES), F32)),
        in_specs=(hbm, hbm), out_specs=(sem,) * (2 * n) + (hbm, hbm, pl.BlockSpec(memory_space=pltpu.VMEM)),
        input_output_aliases={0: 2 * n, 1: 2 * n + 1},
        compiler_params=pltpu.CompilerParams(has_side_effects=pltpu.SideEffectType.DATAFLOW_SIDE_EFFECTING))(
            pltpu.with_memory_space_constraint(partial, pltpu.HBM),
            pltpu.with_memory_space_constraint(lax.empty(partial.shape, partial.dtype), pltpu.HBM))


def _chip_exchange_wait(started, after):
    n = len(CHIP_FLIPS)
    sems, (p_thru, land_thru) = started[:2 * n], started[2 * n:2 * n + 2]

    def body(p_ref, land_ref, *rest):
        sends, recvs = rest[:n], rest[n:2 * n]
        x, y, c = lax.axis_index("x"), lax.axis_index("y"), lax.axis_index("c")
        me = 4 * x + 2 * y + c
        for idx, k in enumerate(CHIP_FLIPS):
            copy = pltpu.make_async_remote_copy(
                src_ref=p_ref.at[(me ^ k) >> 1], dst_ref=land_ref.at[(me ^ k) >> 1], send_sem=sends[idx],
                recv_sem=recvs[idx], device_id=(x ^ ((k >> 2) & 1), y ^ ((k >> 1) & 1), c),
                device_id_type=pl.DeviceIdType.MESH)
            copy.wait_send()
            copy.wait_recv()

    hbm = pl.BlockSpec(memory_space=pltpu.HBM)
    sem = pl.BlockSpec(memory_space=pltpu.SEMAPHORE)
    buf = pltpu.HBM(p_thru.shape, p_thru.dtype)
    return pl.pallas_call(
        body, name="chip_exchange_wait", out_shape=(buf, buf),
        in_specs=(hbm, hbm) + (sem,) * (2 * n) + (pl.BlockSpec(memory_space=pl.ANY),) * len(after),
        out_specs=(hbm, hbm), input_output_aliases={0: 0, 1: 1},
        compiler_params=pltpu.CompilerParams(has_side_effects=pltpu.SideEffectType.DATAFLOW_SIDE_EFFECTING))(
            p_thru, land_thru, *sems, *after)


def _slab_peer(x, y, c, k):
    return (x ^ ((k >> 2) & 1), y ^ ((k >> 1) & 1), c ^ (k & 1))


def _gather_slab_start(slab, carried):
    n = N_DEV - 1

    def body(s_ref, land_ref, carried_ref, *rest):
        sends, recvs, own_sem = rest[:n], rest[n:2 * n], rest[2 * n + 3]
        x, y, c = lax.axis_index("x"), lax.axis_index("y"), lax.axis_index("c")
        me = 4 * x + 2 * y + c
        for k in range(1, N_DEV):
            pltpu.make_async_remote_copy(
                src_ref=s_ref, dst_ref=land_ref.at[me], send_sem=sends[k - 1], recv_sem=recvs[k - 1],
                device_id=_slab_peer(x, y, c, k), device_id_type=pl.DeviceIdType.MESH).start()
        own = pltpu.make_async_copy(s_ref, land_ref.at[me], own_sem)
        own.start()
        own.wait()

    hbm = pl.BlockSpec(memory_space=pltpu.HBM)
    sem = pl.BlockSpec(memory_space=pltpu.SEMAPHORE)
    land = (N_DEV,) + slab.shape
    return pl.pallas_call(
        body, name="gather_slab_start",
        out_shape=(pltpu.SemaphoreType.DMA(()),) * (2 * n) + (
            pltpu.HBM(slab.shape, slab.dtype), pltpu.HBM(land, slab.dtype), pltpu.HBM(carried.shape, carried.dtype)),
        in_specs=(hbm, hbm, hbm), out_specs=(sem,) * (2 * n) + (hbm, hbm, hbm),
        input_output_aliases={0: 2 * n, 1: 2 * n + 1, 2: 2 * n + 2}, scratch_shapes=[pltpu.SemaphoreType.DMA(())],
        compiler_params=pltpu.CompilerParams(has_side_effects=pltpu.SideEffectType.DATAFLOW_SIDE_EFFECTING))(
            pltpu.with_memory_space_constraint(slab, pltpu.HBM),
            pltpu.with_memory_space_constraint(lax.empty(land, slab.dtype), pltpu.HBM),
            pltpu.with_memory_space_constraint(carried, pltpu.HBM))


def _gather_slab_wait(started, after):
    n = N_DEV - 1
    sems, (s_thru, land_thru) = started[:2 * n], started[2 * n:2 * n + 2]

    def body(s_ref, land_ref, *rest):
        sends, recvs = rest[:n], rest[n:2 * n]
        x, y, c = lax.axis_index("x"), lax.axis_index("y"), lax.axis_index("c")
        me = 4 * x + 2 * y + c
        for k in range(1, N_DEV):
            copy = pltpu.make_async_remote_copy(
                src_ref=s_ref, dst_ref=land_ref.at[me ^ k], send_sem=sends[k - 1], recv_sem=recvs[k - 1],
                device_id=_slab_peer(x, y, c, k), device_id_type=pl.DeviceIdType.MESH)
            copy.wait_send()
            copy.wait_recv()

    hbm = pl.BlockSpec(memory_space=pltpu.HBM)
    sem = pl.BlockSpec(memory_space=pltpu.SEMAPHORE)
    return pl.pallas_call(
        body, name="gather_slab_wait",
        out_shape=(pltpu.HBM(s_thru.shape, s_thru.dtype), pltpu.HBM(land_thru.shape, land_thru.dtype)),
        in_specs=(hbm, hbm) + (sem,) * (2 * n) + (pl.BlockSpec(memory_space=pl.ANY),) * len(after),
        out_specs=(hbm, hbm), input_output_aliases={0: 0, 1: 1},
        compiler_params=pltpu.CompilerParams(has_side_effects=pltpu.SideEffectType.DATAFLOW_SIDE_EFFECTING))(
            s_thru, land_thru, *sems, *after)[1]


def _adamw(w, g, m, v):
    m = ADAM_B1 * m + (1.0 - ADAM_B1) * g
    v = ADAM_B2 * v + (1.0 - ADAM_B2) * (g * g)
    m_hat = m / (1.0 - ADAM_B1 ** ADAM_STEP)
    v_hat = v / (1.0 - ADAM_B2 ** ADAM_STEP)
    delta = -ADAM_LR * (m_hat / (jnp.sqrt(v_hat) + ADAM_EPS) + ADAM_WD * w)
    return delta, m, v


def _sum_adamw(recv, w, m, v, name, after=None):
    lead = w.ndim - 2
    rows, cols = w.shape[-2:]
    tr = 256 if rows % 256 == 0 else 128
    n_slots = recv.shape[0]
    extra = [] if after is None else [after]

    def body(*refs):
        r_ref = refs[0]
        w_ref, m_ref, v_ref, g_ref, d_ref, nm_ref, nv_ref = refs[1 + len(extra):]
        g = None
        for slot in range(n_slots):
            g = r_ref[slot].astype(F32) if g is None else g + r_ref[slot].astype(F32)
        g_ref[...] = g
        d_ref[...], nm_ref[...], nv_ref[...] = _adamw(w_ref[...], g, m_ref[...], v_ref[...])

    blk = pl.BlockSpec((None,) * lead + (tr, cols), lambda i: (0,) * lead + (i, 0))
    slots = pl.BlockSpec((n_slots, tr, cols), lambda i: (0, i, 0))
    return pl.pallas_call(
        body, name=name, grid=(rows // tr,),
        in_specs=[slots] + [pl.BlockSpec(a.shape, lambda i: (0, 0)) for a in extra] + [blk, blk, blk],
        out_specs=[blk] * 4, out_shape=[_sds(w.shape, F32)] * 4, compiler_params=_params())(recv, *extra, w, m, v)


def _entry_view(a):
    _, rows, cols = a.shape
    return jnp.transpose(a, (2, 0, 1)).reshape(cols, rows // LANES, LANES)


def _from_entry_view(a):
    cols, groups, lanes = a.shape
    return jnp.transpose(a, (1, 2, 0)).reshape(1, groups * lanes, cols)


def _sum_adamw_entry_view(landing, own, w, m, v, name):
    n_slots, rows, cols = landing.shape
    groups = rows // LANES
    cols_pad = -(-cols // LANES) * LANES

    def body(r_ref, own_ref, w_ref, m_ref, v_ref, g_ref, d_ref, nm_ref, nv_ref, pad_ref, gt_ref):
        j = pl.program_id(0)
        chip = (4 * lax.axis_index("x") + 2 * lax.axis_index("y") + lax.axis_index("c")) >> 1
        pad_ref[:, cols_pad - LANES:] = jnp.zeros((LANES, LANES), F32)
        for r0 in range(0, LANES, 32):
            g = None
            for slot in range(n_slots):
                part = jnp.where(chip == slot, own_ref[slot, r0:r0 + 32], r_ref[slot, r0:r0 + 32])
                g = part.astype(F32) if g is None else g + part.astype(F32)
            pad_ref[r0:r0 + 32, :cols] = g
        for c0 in range(0, cols_pad, LANES):
            gt_ref[c0:c0 + LANES, :] = pad_ref[:, c0:c0 + LANES].T
        def update_plane(plane):
            for c0 in range(0, cols, 64):
                n = min(64, cols - c0)
                at = pl.ds(c0 * groups + plane, n, stride=groups)
                gt = gt_ref[c0:c0 + n, :]
                g_ref[at, :] = gt
                d_ref[at, :], nm_ref[at, :], nv_ref[at, :] = _adamw(w_ref[at, :], gt, m_ref[at, :], v_ref[at, :])

        for plane in range(groups):
            pl.when(j == plane)(lambda plane=plane: update_plane(plane))

    flat = lambda a: _entry_view(a).reshape(cols * groups, LANES)
    whole = pl.BlockSpec((cols * groups, LANES), lambda j: (0, 0))
    slots = pl.BlockSpec((n_slots, LANES, cols), lambda j: (0, j, 0))
    outs = pl.pallas_call(
        body, name=name, grid=(groups,), in_specs=[slots, slots, whole, whole, whole], out_specs=[whole] * 4,
        out_shape=[_sds((cols * groups, LANES), F32)] * 4,
        scratch_shapes=[pltpu.VMEM((LANES, cols_pad), F32), pltpu.VMEM((cols_pad, LANES), F32)],
        compiler_params=_params())(landing, own, flat(w), flat(m), flat(v))
    return [_from_entry_view(o.reshape(cols, groups, LANES)) for o in outs], outs[0]


SLAB_ROWS = 16
SLOT = {"kv_norm_g": (8, 0, D), "norm_b_g": (9, 0, D), "b_forget": (10, 0, 16), "qnorm_a_g": (10, 128, HD),
        "knorm_a_g": (10, 256, HD), "knorm_b_g": (10, 384, HD), "qnorm_b_g": (10, 512, HD), "sinks": (10, 640, 16)}
SMALL = ["norm_a_g", "b_forget", "qnorm_a_g", "knorm_a_g", "kv_norm_g", "knorm_b_g", "norm_b_g", "qnorm_b_g", "sinks"]


LOSS_ROW = 11


def _pack_small(dg_a, dg_kv, dg_b, db_f, dgq_a, dgk_a, dgk_b, dgq_b, dsinks, lsum):
    def fold(ref):
        return ref[:, 0:HD] + ref[:, HD:2 * HD]

    def body(dga_ref, dgkv_ref, dgb_ref, dbf_ref, dgqa_ref, dgka_ref, dgkb_ref, dgqb_ref, dsk_ref, ls_ref, slab_ref):
        slab_ref[...] = jnp.zeros_like(slab_ref)
        for r in range(N_DEV):
            slab_ref[r:r + 1, 0:LANES] = dga_ref[:, LANES * r:LANES * (r + 1)]
        slab_ref[8:9, :] = dgkv_ref[...]
        slab_ref[9:10, :] = dgb_ref[...]
        slab_ref[10:11, 0:LANES] = dbf_ref[...]
        slab_ref[10:11, 128:128 + HD] = fold(dgqa_ref)
        slab_ref[10:11, 256:256 + HD] = fold(dgka_ref)
        slab_ref[10:11, 384:384 + HD] = fold(dgkb_ref)
        slab_ref[10:11, 512:512 + HD] = fold(dgqb_ref)
        slab_ref[10:11, 640:640 + LANES] = dsk_ref[...]
        slab_ref[LOSS_ROW:LOSS_ROW + 1, 0:LANES] = ls_ref[...]

    return pl.pallas_call(body, name="pack_small", out_shape=_sds((SLAB_ROWS, D), F32), compiler_params=_params())(
        dg_a, dg_kv, dg_b, db_f, dgq_a, dgk_a, dgk_b, dgq_b, dsinks, lsum)


def _small_adamw(recv, ws, ms, vs):
    k = len(SMALL)

    def body(*refs):
        r_ref = refs[0]
        w_refs, m_refs, v_refs = refs[1:1 + k], refs[1 + k:1 + 2 * k], refs[1 + 2 * k:1 + 3 * k]
        outs = refs[1 + 3 * k:1 + 7 * k]
        loss_ref, tot = refs[1 + 7 * k], refs[2 + 7 * k]
        g = r_ref[0]
        for dev in range(1, N_DEV):
            g = g + r_ref[dev]
        tot[...] = g
        loss_ref[...] = tot[LOSS_ROW:LOSS_ROW + 1, 0:LANES] * (0.5 / D)
        me = 4 * lax.axis_index("x") + 2 * lax.axis_index("y") + lax.axis_index("c")
        for p, name in enumerate(SMALL):
            if name == "norm_a_g":
                mine = lax.broadcasted_iota(jnp.int32, (N_DEV, LANES), 0) == me
                gp = jnp.sum(jnp.where(mine, tot[0:N_DEV, 0:LANES], 0.0), axis=0, keepdims=True)
            else:
                row, lo, width = SLOT[name]
                gp = tot[row:row + 1, lo:lo + width]
            d, nm, nv = _adamw(w_refs[p][...], gp, m_refs[p][...], v_refs[p][...])
            outs[p][...] = gp
            outs[k + p][...] = d
            outs[2 * k + p][...] = nm
            outs[3 * k + p][...] = nv

    shapes = [_sds(w.shape, F32) for w in ws]
    return pl.pallas_call(body, name="small_adamw", out_shape=shapes * 4 + [_sds((1, LANES), F32)],
                          scratch_shapes=[pltpu.VMEM((SLAB_ROWS, D), F32)],
                          compiler_params=_params())(recv, *ws, *ms, *vs)


def _rope_tables(positions):
    inv_freq = jnp.power(jnp.float32(ROPE_THETA), -jnp.arange(0, ROT, 2, dtype=F32) / ROT)
    ang = positions.astype(F32)[:, None] * inv_freq[None, :]
    cos, sin = jnp.cos(ang), jnp.sin(ang)
    c64 = jnp.concatenate([cos, cos, jnp.ones((S, HD - ROT), F32)], axis=-1)
    s64 = jnp.concatenate([-sin, sin, jnp.zeros((S, HD - ROT), F32)], axis=-1)
    return jnp.tile(c64, (1, 2)), jnp.tile(s64, (1, 2))


def _local_step(x, tgt, positions, g_a, wa, b_forget, gq_a, gk_a, g_kv, gk_b, g_b, gq_b, sinks,
                woa_s, wkv_s, wib_s, wob_s, adamw_others):
    nq = S // TQ
    cos2, sin2 = _rope_tables(positions)
    b_pad = jnp.pad(b_forget, ((0, 0), (0, LANES - N_HEADS)))

    u_a, proj, qn, kn, vb, crow, cbc = _head_a(x, g_a, wa, gq_a, gk_a, b_pad)
    o_a, z_a, lse_a, woa_g, wkv_g, w_in_b, wob_g = _fox_fwd(
        qn, kn, vb, proj, crow, cbc,
        rider=[("gather_rows", woa_s), ("gather_rows", wkv_s), ("gather_cols", wib_s), ("gather_rows", wob_s)])
    w_out_a, w_kv, w_out_b = woa_g.reshape(D, D), wkv_g.reshape(D, 512), wob_g.reshape(D, D)
    h1, u_kv, u_b, kv, pb, qb, ksh, vsh = _head_b(x, z_a, w_out_a, g_kv, g_b, w_kv, w_in_b, gq_b, gk_b, cos2, sin2)
    sinks1 = sinks.reshape(N_HEADS)
    o_b, z_b, lse_b = _swa_fwd(qb, ksh, vsh, pb, sinks1)
    dy, lsum = _out_b_loss(z_b, w_out_b, h1, tgt)
    dw_out_b = _mm(z_b, dy, "tn", 512, 512, S, out_dtype=BF16, name="mm_dw_out_b")
    dz_b = _mm(dy, w_out_b, "nt", 1024, 512, D, name="mm_dz_b")
    dpb, dka, dkb, dva, dvb, dsinks, dgq_b = _swa_bwd(qb, ksh, vsh, dz_b, o_b, lse_b, pb, sinks1, gq_b, cos2, sin2)
    dkv, dgk_b = _prep_kv_bwd(dka, dkb, dva, dvb, kv, gk_b, cos2, sin2)
    dw_in_b = _mm(u_b, dpb, "tn", 512, 512, S, out_dtype=BF16, name="mm_dw_in_b")
    dw_kv = _mm(u_kv, dkv, "tn", 512, 512, S, out_dtype=BF16, name="mm_dw_kv")
    dh1, dg_b, dg_kv = _du_b_rms_bwd(dpb, w_in_b, dkv, w_kv, h1, g_b, g_kv, dy)
    dw_out_a = _mm(z_a, dh1, "tn", 512, 512, S, out_dtype=BF16, name="mm_dw_out_a")
    do_a, dgate_a, delta_a = _fox_bwd_pre(dh1, w_out_a, proj, o_a)
    dk_a, dv_a, dcs, dq_a, drow, r_wob, r_wib, r_wkv, r_woa = _fox_bwd(
        qn, kn, vb, do_a, lse_a, delta_a, crow, cbc,
        rider=[("a2a_rows", dw_out_b), ("a2a_cols", dw_in_b), ("a2a_rows", dw_kv), ("a2a_rows", dw_out_a)])
    dproj, dgq_a, dgk_a, db_f = _prep_a_bwd(dq_a, dk_a, dv_a, dgate_a, drow, dcs, proj, b_pad, gq_a, gk_a)
    dwa = _mm(u_a, dproj, "tn", 1024, 256, S, out_dtype=BF16, name="mm_dw_in_a")
    partial = _reshard_pair_reduce(dwa)
    started = _chip_exchange_start(partial)
    dx, dg_a = _du_a_rms_bwd(dproj, wa, x, g_a, dh1, after=started[-1])
    others = adamw_others(dict(w_out_a=r_woa, w_kv=r_wkv, w_in_b=r_wib, w_out_b=r_wob), dg_a)
    partial, landed = _chip_exchange_wait(started, [res[0] for res in others.values()])
    slab = _pack_small(dg_a, dg_kv, dg_b, db_f, dgq_a, dgk_a, dgk_b, dgq_b, dsinks, lsum)
    slab_started = _gather_slab_start(slab, landed)
    return dx, (slab_started[-1], partial), others, slab_started


def kernel(x, positions, norm_a_g, w_in_a, b_forget, qnorm_a_g, knorm_a_g, w_out_a, kv_norm_g, w_kv, knorm_b_g, norm_b_g, w_in_b, qnorm_b_g, sinks, w_out_b, loss_target, m_norm_a_g, m_w_in_a, m_b_forget, m_qnorm_a_g, m_knorm_a_g, m_w_out_a, m_kv_norm_g, m_w_kv, m_knorm_b_g, m_norm_b_g, m_w_in_b, m_qnorm_b_g, m_sinks, m_w_out_b, v_norm_a_g, v_w_in_a, v_b_forget, v_qnorm_a_g, v_knorm_a_g, v_w_out_a, v_kv_norm_g, v_w_kv, v_knorm_b_g, v_norm_b_g, v_w_in_b, v_qnorm_b_g, v_sinks, v_w_out_b):
    wa_g, ga_g, woa_s, wkv_s, wib_s, wob_s = _gather_first(w_in_a, w_out_a, w_kv, w_in_b, w_out_b, norm_a_g)
    state = dict(w_in_a=(w_in_a, m_w_in_a, v_w_in_a), w_out_a=(w_out_a, m_w_out_a, v_w_out_a),
                 w_kv=(w_kv, m_w_kv, v_w_kv), w_in_b=(w_in_b, m_w_in_b, v_w_in_b),
                 w_out_b=(w_out_b, m_w_out_b, v_w_out_b))

    def adamw_others(landed, after):
        return {n: _sum_adamw(r, *state[n], "adamw_" + n, after=after) for n, r in landed.items()}

    dx, r_wa, big, slab_started = _local_step(
        x[0], loss_target[0], positions, ga_g.reshape(1, D), _unshard_wa(wa_g), b_forget, qnorm_a_g, knorm_a_g,
        kv_norm_g.reshape(1, D), knorm_b_g.reshape(1, HD), norm_b_g, qnorm_b_g, sinks, woa_s, wkv_s, wib_s, wob_s,
        adamw_others)
    big["w_in_a"], done = _sum_adamw_entry_view(*r_wa, *state["w_in_a"], "adamw_w_in_a")
    slab_g = _gather_slab_wait(slab_started, [done])

    r2 = lambda a: a.reshape(1, -1)
    small_w = dict(norm_a_g=norm_a_g, b_forget=b_forget, qnorm_a_g=qnorm_a_g, knorm_a_g=knorm_a_g,
                   kv_norm_g=kv_norm_g, knorm_b_g=knorm_b_g, norm_b_g=norm_b_g, qnorm_b_g=qnorm_b_g, sinks=sinks)
    small_m = dict(norm_a_g=m_norm_a_g, b_forget=m_b_forget, qnorm_a_g=m_qnorm_a_g, knorm_a_g=m_knorm_a_g,
                   kv_norm_g=m_kv_norm_g, knorm_b_g=m_knorm_b_g, norm_b_g=m_norm_b_g, qnorm_b_g=m_qnorm_b_g,
                   sinks=m_sinks)
    small_v = dict(norm_a_g=v_norm_a_g, b_forget=v_b_forget, qnorm_a_g=v_qnorm_a_g, knorm_a_g=v_knorm_a_g,
                   kv_norm_g=v_kv_norm_g, knorm_b_g=v_knorm_b_g, norm_b_g=v_norm_b_g, qnorm_b_g=v_qnorm_b_g,
                   sinks=v_sinks)
    res = _small_adamw(slab_g, [r2(small_w[n]) for n in SMALL], [r2(small_m[n]) for n in SMALL],
                       [r2(small_v[n]) for n in SMALL])
    k = len(SMALL)
    small = {n: [res[q * k + p].reshape(small_w[n].shape) for q in range(4)] for p, n in enumerate(SMALL)}
    loss = res[4 * k][0, 0]

    order = ["norm_a_g", "w_in_a", "b_forget", "qnorm_a_g", "knorm_a_g", "w_out_a", "kv_norm_g", "w_kv",
             "knorm_b_g", "norm_b_g", "w_in_b", "qnorm_b_g", "sinks", "w_out_b"]

    def leaf(n, q):
        return big[n][q] if n in big else small[n][q]

    outs = [loss, dx[None]]
    for q in range(4):
        outs.extend(leaf(n, q) for n in order)
    return tuple(outs)
```

```python
import jax
import jax.numpy as jnp
from jax import lax
from jax.experimental import pallas as pl
from jax.experimental.pallas import tpu as pltpu

F32, BF16 = jnp.float32, jnp.bfloat16

S = 2048
D = 1024
HD = 64
N_HEADS = 16
N_DEV = 8
NA = 4352
GOFF = 3072
FOFF = 4096
RAW_F = 3072
RAW_G = RAW_F + N_HEADS
NA_RAW = 4112
EPS = 1e-6
QSCALE = 0.125
ROPE_THETA = 500000.0
ROT = 16
WIN = 128
TQ = 256
TK = 256
KS = TQ // 2
HPS = 8
HW = HPS * HD
TM = 256
RT = 256
RB = 512
CB = 256
LANES = 128

ADAM_LR, ADAM_B1, ADAM_B2, ADAM_EPS, ADAM_WD, ADAM_STEP = 0.001, 0.9, 0.999, 1e-08, 0.01, 10

VMEM_LIMIT = 56 * 1024 * 1024


def _params():
    return pltpu.CompilerParams(vmem_limit_bytes=VMEM_LIMIT)


def _sds(shape, dtype):
    return jax.ShapeDtypeStruct(shape, dtype)


def _dot_nt(a, b):
    return lax.dot_general(a, b, (((1,), (1,)), ((), ())), preferred_element_type=F32)


def _dot_tn(a, b):
    return lax.dot_general(a, b, (((0,), (0,)), ((), ())), preferred_element_type=F32)


def _dot_nn(a, b):
    return lax.dot_general(a, b, (((1,), (0,)), ((), ())), preferred_element_type=F32)


def _sigmoid(g):
    return 1.0 / (1.0 + jnp.exp(-g))


def _lane_iota(shape):
    return lax.broadcasted_iota(jnp.int32, shape, len(shape) - 1)


def _flips(kind):
    return (2, 4, 6) if kind == "a2a_chips" else tuple(range(1, N_DEV))


def _send_view(kind, ref, dev):
    if kind in ("gather_rows", "gather_cols"):
        return ref
    if kind == "a2a_slots":
        return ref.at[dev]
    if kind == "a2a_chips":
        return ref.at[dev >> 1]
    if kind == "a2a_rows":
        rows = ref.shape[0] // N_DEV
        return ref.at[pl.ds(pl.multiple_of(dev * rows, rows), rows)]
    cols = ref.shape[1] // N_DEV
    return ref.at[:, pl.ds(pl.multiple_of(dev * cols, cols), cols)]


def _land_view(kind, ref, dev):
    if kind == "gather_cols":
        cols = ref.shape[1] // N_DEV
        return ref.at[:, pl.ds(pl.multiple_of(dev * cols, cols), cols)]
    if kind == "a2a_chips":
        return ref.at[dev >> 1]
    return ref.at[dev]


def _landing_sds(kind, arr):
    if kind == "gather_rows":
        return _sds((N_DEV,) + arr.shape, arr.dtype)
    if kind == "gather_cols":
        return _sds((arr.shape[0], N_DEV * arr.shape[1]), arr.dtype)
    if kind == "a2a_rows":
        return _sds((N_DEV, arr.shape[0] // N_DEV, arr.shape[1]), arr.dtype)
    if kind == "a2a_cols":
        return _sds((N_DEV, arr.shape[0], arr.shape[1] // N_DEV), arr.dtype)
    return _sds(arr.shape, arr.dtype)


def _exchange_sems(n_parts):
    n = n_parts * (N_DEV - 1)
    return [pltpu.SemaphoreType.DMA((n,)), pltpu.SemaphoreType.DMA((n,)), pltpu.SemaphoreType.DMA((n_parts,))]


def _exchange_ops(kinds, srcs, dsts, sems, start, wait):
    send_sems, recv_sems, local_sems = sems
    x, y, c = lax.axis_index("x"), lax.axis_index("y"), lax.axis_index("c")
    me = 4 * x + 2 * y + c

    def local(a):
        return pltpu.make_async_copy(_send_view(kinds[a], srcs[a], me), _land_view(kinds[a], dsts[a], me),
                                     local_sems.at[a])

    def remote(a, k, landing_dev):
        peer = (x ^ ((k >> 2) & 1), y ^ ((k >> 1) & 1), c ^ (k & 1))
        sem = a * (N_DEV - 1) + k - 1
        return pltpu.make_async_remote_copy(
            src_ref=_send_view(kinds[a], srcs[a], me ^ k), dst_ref=_land_view(kinds[a], dsts[a], landing_dev),
            send_sem=send_sems.at[sem], recv_sem=recv_sems.at[sem], device_id=peer,
            device_id_type=pl.DeviceIdType.MESH)

    pairs = [(a, k) for k in range(1, N_DEV) for a in range(len(kinds)) if k in _flips(kinds[a])]
    if start:
        for a in range(len(kinds)):
            local(a).start()
        for a, k in pairs:
            remote(a, k, me).start()
    if wait:
        for a, k in pairs:
            remote(a, k, me ^ k).wait_recv()
            remote(a, k, me).wait_send()
        for a in range(len(kinds)):
            local(a).wait()


def _gather_two_level(srcs, landing_of, sems, meanwhile=None):
    send_sems, recv_sems, local_sems = sems
    x, y, c = lax.axis_index("x"), lax.axis_index("y"), lax.axis_index("c")
    me, sibling = (x, y, c), (x, y, 1 - c)
    chips = [(1 - x, y), (x, 1 - y), (1 - x, 1 - y)]

    def slot(a, dev):
        return landing_of[a](4 * dev[0] + 2 * dev[1] + dev[2])

    def copy(a, k, block, to, src=None):
        return pltpu.make_async_remote_copy(
            src_ref=slot(a, block) if src is None else src, dst_ref=slot(a, block),
            send_sem=send_sems.at[a * (N_DEV - 1) + k], recv_sem=recv_sems.at[a * (N_DEV - 1) + k],
            device_id=to, device_id_type=pl.DeviceIdType.MESH)

    parts = range(len(srcs))
    mine = [pltpu.make_async_copy(srcs[a], slot(a, me), local_sems.at[a]) for a in parts]
    first = [copy(a, 1 + j, me, (*chip, c), src=srcs[a]) for a in parts for j, chip in enumerate(chips)]
    first += [copy(a, 0, me, sibling, src=srcs[a]) for a in parts]
    for cp in first + mine:
        cp.start()
    if meanwhile is not None:
        meanwhile()
    passed = []
    for a in parts:
        for j, chip in enumerate(chips):
            copy(a, 1 + j, (*chip, c), me).wait_recv()
            fwd = copy(a, 4 + j, (*chip, c), sibling)
            fwd.start()
            passed.append(fwd)
    for a in parts:
        copy(a, 0, sibling, me).wait_recv()
        for j, chip in enumerate(chips):
            copy(a, 4 + j, (*chip, 1 - c), me).wait_recv()
    for cp in first + passed:
        cp.wait_send()
    for cp in mine:
        cp.wait()


def _call(body, *, name, args, in_specs, out_specs, out_shape, grid=(), scratch_shapes=(), aliases=None, rider=()):
    n_in, n_out, n_scr, n_r = len(in_specs), len(out_specs), len(scratch_shapes), len(rider)
    kinds = [kind for kind, _ in rider]

    def kernel_body(*refs):
        c_in, r_in = refs[:n_in], refs[n_in:n_in + n_r]
        c_out = refs[n_in + n_r:n_in + n_r + n_out]
        r_out = refs[n_in + n_r + n_out:n_in + 2 * n_r + n_out]
        rest = refs[n_in + 2 * n_r + n_out:]
        c_scr, sems = rest[:n_scr], rest[n_scr:]
        if n_r:
            assert grid, "a rider needs a gridded call"
            ids = [pl.program_id(ax) for ax in range(len(grid))]
            first, last = ids[0] == 0, ids[0] == grid[0] - 1
            for pid, size in zip(ids[1:], grid[1:]):
                first = first & (pid == 0)
                last = last & (pid == size - 1)
            pl.when(first)(lambda: _exchange_ops(kinds, r_in, r_out, sems, True, False))
        body(*c_in, *c_out, *c_scr)
        if n_r:
            pl.when(last)(lambda: _exchange_ops(kinds, r_in, r_out, sems, False, True))

    anyspec = pl.BlockSpec(memory_space=pl.ANY)
    params = pltpu.CompilerParams(vmem_limit_bytes=VMEM_LIMIT, has_side_effects=bool(n_r))
    outs = pl.pallas_call(
        kernel_body, name=name, grid=grid, in_specs=list(in_specs) + [anyspec] * n_r,
        out_specs=list(out_specs) + [anyspec] * n_r,
        out_shape=list(out_shape) + [_landing_sds(kind, arr) for kind, arr in rider],
        scratch_shapes=list(scratch_shapes) + (_exchange_sems(n_r) if n_r else []),
        input_output_aliases=aliases or {}, compiler_params=params)(*args, *[arr for _, arr in rider])
    return list(outs)


def _mm(a, b, mode, tm, tn, tk, out_dtype=F32, add=None, name="mm", rider=()):
    if mode == "nn":
        (m, k), n = a.shape, b.shape[1]
        a_spec = pl.BlockSpec((tm, tk), lambda i, j, kk: (i, kk))
        b_spec = pl.BlockSpec((tk, tn), lambda i, j, kk: (kk, j))
        dot = _dot_nn
    elif mode == "nt":
        (m, k), n = a.shape, b.shape[0]
        a_spec = pl.BlockSpec((tm, tk), lambda i, j, kk: (i, kk))
        b_spec = pl.BlockSpec((tn, tk), lambda i, j, kk: (j, kk))
        dot = _dot_nt
    else:
        (k, m), n = a.shape, b.shape[1]
        a_spec = pl.BlockSpec((tk, tm), lambda i, j, kk: (kk, i))
        b_spec = pl.BlockSpec((tk, tn), lambda i, j, kk: (kk, j))
        dot = _dot_tn
    assert m % tm == 0 and n % tn == 0 and k % tk == 0, (m, n, k, tm, tn, tk)
    nk = k // tk
    has_add = add is not None

    def body(*refs):
        if has_add:
            a_ref, b_ref, add_ref, o_ref, acc = refs
        else:
            a_ref, b_ref, o_ref, acc = refs
        p = dot(a_ref[...].astype(BF16), b_ref[...].astype(BF16))

        def finish(total):
            if has_add:
                total = add_ref[...] + total
            o_ref[...] = total.astype(out_dtype)

        if nk == 1:
            finish(p)
        else:
            kk = pl.program_id(2)

            @pl.when(kk == 0)
            def _():
                acc[...] = p

            @pl.when(kk > 0)
            def _():
                acc[...] += p

            @pl.when(kk == nk - 1)
            def _():
                finish(acc[...])

    in_specs = [a_spec, b_spec]
    args = [a, b]
    if has_add:
        in_specs.append(pl.BlockSpec((tm, tn), lambda i, j, kk: (i, j)))
        args.append(add)
    acc_shape = (tm, tn) if nk > 1 else (8, LANES)
    outs = _call(body, name=name, args=args, grid=(m // tm, n // tn, nk), in_specs=in_specs,
                 out_specs=[pl.BlockSpec((tm, tn), lambda i, j, kk: (i, j))], out_shape=[_sds((m, n), out_dtype)],
                 scratch_shapes=[pltpu.VMEM(acc_shape, F32)], rider=rider)
    return outs if rider else outs[0]


def _rms_rinv(x):
    return lax.rsqrt(jnp.mean(x * x, axis=-1, keepdims=True) + EPS)


def _rms_bwd_core(du, x, g):
    r = _rms_rinv(x)
    dug = du * g
    dx = r * (dug - x * ((r * r) * jnp.mean(dug * x, axis=-1, keepdims=True)))
    dg = jnp.sum(du * (x * r), axis=0, keepdims=True)
    return dx, dg


def _half_ones():
    r = lax.broadcasted_iota(jnp.int32, (LANES, LANES), 0)
    c = lax.broadcasted_iota(jnp.int32, (LANES, LANES), 1)
    return ((r < HD) == (c < HD)).astype(BF16)


def _half_sum(v, lo_half):
    if lo_half.dtype == jnp.bool_:
        s0 = jnp.sum(jnp.where(lo_half, v, 0.0), axis=-1, keepdims=True)
        s1 = jnp.sum(jnp.where(lo_half, 0.0, v), axis=-1, keepdims=True)
        return jnp.where(lo_half, s0, s1)
    hi = v.astype(BF16)
    lo = (v - hi.astype(F32)).astype(BF16)
    return _dot_nn(hi, lo_half) + _dot_nn(lo, lo_half)


def _head_rinv(x, lo_half):
    return lax.rsqrt(_half_sum(x * x, lo_half) * (1.0 / HD) + EPS)


def _head_norm_bwd(dn, x, g, lo_half):
    r = _head_rinv(x, lo_half)
    dng = dn * g
    dx = r * (dng - x * ((r * r) * (_half_sum(dng * x, lo_half) * (1.0 / HD))))
    dg = jnp.sum(dn * (x * r), axis=0, keepdims=True)
    return dx, dg


def _rope_swap(x, lane):
    l64 = lane & (HD - 1)
    return jnp.where(l64 < ROT // 2, pltpu.roll(x, LANES - ROT // 2, 1), pltpu.roll(x, ROT // 2, 1))


def _rope_fwd(x, cos, sin, lane):
    return x * cos + _rope_swap(x, lane) * sin


def _rope_bwd(dy, cos, sin, lane):
    return dy * cos + jnp.where((lane & (HD - 1)) < ROT, _rope_swap(dy * sin, lane), 0.0)


def _g2(g_ref):
    g = g_ref[...]
    return jnp.concatenate([g, g], axis=-1)


def _pairs(width):
    return [slice(LANES * c, LANES * (c + 1)) for c in range(width // LANES)]


def _pick_lane(block, lane, idx):
    return jnp.sum(jnp.where(lane == idx, block, 0.0), axis=-1, keepdims=True)


def _head_a(x, g, wa, gq, gk, b_pad):
    assert RT == TQ
    def body(x_ref, g_ref, w_ref, gq_ref, gk_ref, b_ref, u_ref, p_ref, qo_ref, ko_ref, vo_ref, c_ref, cbc_ref, carry):
        @pl.when(pl.program_id(0) == 0)
        def _():
            carry[...] = jnp.zeros_like(carry)

        xv = x_ref[...]
        u = ((xv * _rms_rinv(xv)) * g_ref[...]).astype(BF16)
        u_ref[...] = u
        for lo in range(0, NA, D):
            hi = min(lo + D, NA)
            p_ref[:, lo:hi] = _dot_nn(u, w_ref[:, lo:hi])
        lane = _lane_iota((RT, LANES))
        lo_half = lane < HD
        gq2, gk2 = _g2(gq_ref), _g2(gk_ref)
        for c in _pairs(D):
            q = p_ref[:, c]
            k = p_ref[:, D + c.start:D + c.stop]
            qo_ref[:, c] = (((q * _head_rinv(q, lo_half)) * gq2) * QSCALE).astype(BF16)
            ko_ref[:, c] = ((k * _head_rinv(k, lo_half)) * gk2).astype(BF16)
        vo_ref[...] = p_ref[:, 2 * D:3 * D].astype(BF16)

        z = p_ref[:, FOFF:FOFF + LANES] + b_ref[...]
        logf = jnp.minimum(z, 0.0) - jnp.log1p(jnp.exp(-jnp.abs(z)))
        r = lax.broadcasted_iota(jnp.int32, (RT, RT), 0)
        cc = lax.broadcasted_iota(jnp.int32, (RT, RT), 1)
        tri = (r >= cc).astype(F32)
        loc = jnp.dot(tri, logf, precision=lax.Precision.HIGHEST, preferred_element_type=F32) + carry[0:1, :]
        c_ref[:, 0, :] = loc.T[:N_HEADS]
        carry[0:1, :] = loc[RT - 1:RT, :]
        for h in range(N_HEADS):
            cbc_ref[:, LANES * h:LANES * (h + 1)] = jnp.broadcast_to(_pick_lane(loc, lane, h), (RT, LANES))

    row = lambda width: pl.BlockSpec((RT, width), lambda i: (i, 0))
    whole = lambda arr: pl.BlockSpec(arr.shape, lambda i: (0,) * arr.ndim, pipeline_mode=pl.Buffered(1))
    return pl.pallas_call(
        body, name="head_a", grid=(S // RT,),
        in_specs=[row(D), whole(g), whole(wa), whole(gq), whole(gk), whole(b_pad)],
        out_specs=[row(D), row(NA), row(D), row(D), row(D),
                   pl.BlockSpec((N_HEADS, None, 1, TQ), lambda i: (0, i, 0, 0)), row(N_HEADS * LANES)],
        out_shape=[_sds((S, D), BF16), _sds((S, NA), F32)] + [_sds((S, D), BF16)] * 3
        + [_sds((N_HEADS, S // TQ, 1, TQ), F32), _sds((S, N_HEADS * LANES), F32)],
        scratch_shapes=[pltpu.VMEM((8, LANES), F32)], compiler_params=_params())(x, g, wa, gq, gk, b_pad)


def _key_le_query(offset, keys=TK):
    r = lax.broadcasted_iota(jnp.int32, (keys, TQ), 0)
    c = lax.broadcasted_iota(jnp.int32, (keys, TQ), 1)
    return (r + offset) <= c


def _widen(tile):
    return jnp.concatenate([tile] * (TQ // LANES), axis=1)


def _fox_fwd(qn, kn, vb, proj, crow, cbc, rider=()):
    nq = S // TQ

    def body(q_ref, k_ref, v_ref, g_ref, cq_ref, cbc_ref, o_ref, z_ref, lse_ref, st_s, pt_s):
        i = pl.program_id(1)
        qs = [q_ref[:, HD * hh:HD * (hh + 1)] for hh in range(HPS)]
        cqs = [cq_ref[hh, 0] for hh in range(HPS)]

        def scores(s, hh):
            off = pl.multiple_of(s * KS, KS)
            kj = k_ref[pl.ds(off, KS), HD * hh:HD * (hh + 1)]
            return (_dot_nt(kj, qs[hh]) + cqs[hh]) - _widen(cbc_ref[pl.ds(off, KS), LANES * hh:LANES * (hh + 1)])

        def values(s, hh, pt):
            off = pl.multiple_of(s * KS, KS)
            return _dot_tn(v_ref[pl.ds(off, KS), HD * hh:HD * (hh + 1)], pt)

        def step(s, slot, carries, mask=None, last=False):
            if not last:
                for hh in range(HPS):
                    st_s[1 - slot, hh] = scores(s + 1, hh)
            pvs = [values(jnp.maximum(s - 1, 0), hh, pt_s[1 - slot, hh]) for hh in range(HPS)]
            out = []
            for hh in range(HPS):
                m, l, acc = carries[hh]
                st = st_s[slot, hh]
                if mask is not None:
                    st = jnp.where(mask, st, -jnp.inf)
                m_new = jnp.maximum(m, jnp.max(st, axis=0, keepdims=True))
                pt = jnp.exp(st - m_new)
                alpha = jnp.exp(m - m_new)
                pt_s[slot, hh] = pt.astype(BF16)
                out.append((m_new, alpha * l + jnp.sum(pt, axis=0, keepdims=True), alpha * (acc + pvs[hh])))
            return tuple(out)

        for hh in range(HPS):
            st_s[0, hh] = scores(0, hh)
            pt_s[1, hh] = jnp.zeros((KS, TQ), BF16)
        one = (jnp.full((1, TQ), -jnp.inf, F32), jnp.zeros((1, TQ), F32), jnp.zeros((HD, TQ), F32))
        carries = lax.fori_loop(0, i, lambda t, cr: step(2 * t + 1, 1, step(2 * t, 0, cr)), (one,) * HPS)
        carries = step(2 * i, 0, carries, mask=_key_le_query(0, KS))
        carries = step(2 * i + 1, 1, carries, mask=_key_le_query(KS, KS), last=True)
        accs = []
        for hh in range(HPS):
            m, l, acc = carries[hh]
            acc = acc + values(2 * i + 1, hh, pt_s[1, hh])
            accs.append(acc / l)
            lse_ref[hh, 0] = m + jnp.log(l)
        o = jnp.concatenate(accs, axis=0).T
        o_ref[...] = o
        g = g_ref[...]
        z_ref[...] = (o * (g * _sigmoid(g))).astype(BF16)

    qblk = pl.BlockSpec((TQ, HW), lambda hp, i: (i, hp))
    full = pl.BlockSpec((S, HW), lambda hp, i: (0, hp))
    rows = pl.BlockSpec((HPS, 1, 1, TQ), lambda hp, i: (hp, i, 0, 0))
    return _call(
        body, name="fox_fwd", args=(qn, kn, vb, proj, crow, cbc), grid=(N_HEADS // HPS, nq),
        in_specs=[qblk, full, full,
                  pl.BlockSpec((TQ, HW), lambda hp, i: (i, GOFF // HW + hp)),
                  rows, pl.BlockSpec((S, HPS * LANES), lambda hp, i: (0, hp))],
        out_specs=[qblk, qblk, rows],
        out_shape=[_sds((S, D), F32), _sds((S, D), BF16), _sds((N_HEADS, nq, 1, TQ), F32)],
        scratch_shapes=[pltpu.VMEM((2, HPS, KS, TQ), F32), pltpu.VMEM((2, HPS, KS, TQ), BF16)], rider=rider)


def _fox_bwd_pre(dh, w_out, proj, o):
    nq, rows = S // TQ, 2 * TQ
    per = rows // TQ

    def body(dh_ref, w_ref, g_ref, o_ref, do_ref, dg_ref, delta_ref):
        g = g_ref[...]
        sg = _sigmoid(g)
        dzv = _dot_nt(dh_ref[...].astype(BF16), w_ref[...])
        ov = o_ref[...]
        do = dzv * (g * sg)
        dg_ref[...] = (dzv * ov * (sg * (1.0 + g * (1.0 - sg)))).astype(BF16)
        do_ref[...] = do.astype(BF16)
        prod_t = (do * ov).T
        for h in range(N_HEADS):
            for b in range(per):
                delta_ref[h, b] = jnp.sum(prod_t[HD * h:HD * (h + 1), TQ * b:TQ * (b + 1)], axis=0, keepdims=True)

    row = pl.BlockSpec((rows, D), lambda i: (i, 0))
    return pl.pallas_call(
        body, name="fox_bwd_pre", grid=(S // rows,),
        in_specs=[row, pl.BlockSpec(w_out.shape, lambda i: (0, 0), pipeline_mode=pl.Buffered(1)),
                  pl.BlockSpec((rows, D), lambda i: (i, GOFF // D)), row],
        out_specs=[row, row, pl.BlockSpec((N_HEADS, per, 1, TQ), lambda i: (0, i, 0, 0))],
        out_shape=[_sds((S, D), BF16), _sds((S, D), BF16), _sds((N_HEADS, nq, 1, TQ), F32)],
        compiler_params=_params())(dh, w_out, proj, o)


def _fox_bwd(qn, kn, vb, dob, lse, delta, crow, cbc, rider=()):
    nq, nkb = S // TQ, S // TK

    def body(q_ref, k_ref, v_ref, do_ref, lse_ref, del_ref, cq_ref, cbc_ref,
             dk_ref, dv_ref, dcs_ref, dq_ref, dr_ref, st_s, dp_s, pt_s, ds_s, dq_acc, dr_acc):
        j = pl.program_id(1)

        @pl.when(j == 0)
        def _():
            dq_acc[...] = jnp.zeros_like(dq_acc)
            dr_acc[...] = jnp.zeros_like(dr_acc)

        kjs = [k_ref[:, HD * hh:HD * (hh + 1)] for hh in range(HPS)]
        vjs = [v_ref[:, HD * hh:HD * (hh + 1)] for hh in range(HPS)]

        def rows_of(ref, u, hh):
            off = pl.multiple_of(u * TQ, TQ)
            return ref[pl.ds(off, TQ), HD * hh:HD * (hh + 1)]

        def products(u, hh):
            st = (_dot_nt(kjs[hh], rows_of(q_ref, u, hh)) + cq_ref[hh, u]) - _widen(
                cbc_ref[:, LANES * hh:LANES * (hh + 1)])
            return st, _dot_nt(vjs[hh], rows_of(do_ref, u, hh))

        def step(u, slot, carries, masked=False):
            nxt = jnp.minimum(u + 1, nq - 1)
            for hh in range(HPS):
                st_s[1 - slot, hh], dp_s[1 - slot, hh] = products(nxt, hh)
            prev = jnp.maximum(u - 1, 0)
            dvs = [_dot_nn(pt_s[1 - slot, hh], rows_of(do_ref, prev, hh)) for hh in range(HPS)]
            dks = [_dot_nn(ds_s[1 - slot, hh], rows_of(q_ref, prev, hh)) for hh in range(HPS)]
            for hh in range(HPS):
                dq_acc[hh, prev] += _dot_tn(kjs[hh], ds_s[1 - slot, hh])
            out = []
            for hh in range(HPS):
                dk, dv, dcs = carries[hh]
                st = st_s[slot, hh]
                if masked:
                    st = jnp.where(_key_le_query((j - u) * TQ), st, -jnp.inf)
                pt = jnp.exp(st - lse_ref[hh, u])
                dst = pt * (dp_s[slot, hh] - del_ref[hh, u])
                pt_s[slot, hh] = pt.astype(BF16)
                ds_s[slot, hh] = dst.astype(BF16)
                dr_acc[hh, u] += jnp.sum(dst, axis=0, keepdims=True)
                out.append((dk + dks[hh], dv + dvs[hh], dcs + (dst[:, :LANES] + dst[:, LANES:])))
            return tuple(out)

        t0 = j // 2
        for hh in range(HPS):
            st_s[0, hh], dp_s[0, hh] = products(2 * t0, hh)
            pt_s[1, hh] = jnp.zeros((TK, TQ), BF16)
            ds_s[1, hh] = jnp.zeros((TK, TQ), BF16)
        one = (jnp.zeros((TK, HD), F32), jnp.zeros((TK, HD), F32), jnp.zeros((TK, LANES), F32))
        carries = step(2 * t0 + 1, 1, step(2 * t0, 0, (one,) * HPS, masked=True), masked=True)
        carries = lax.fori_loop(t0 + 1, nq // 2, lambda t, cr: step(2 * t + 1, 1, step(2 * t, 0, cr)), carries)
        dks, dvs = [], []
        lane = _lane_iota((TK, LANES))
        dcs_all = jnp.zeros((TK, LANES), F32)
        for hh in range(HPS):
            dk, dv, dcs = carries[hh]
            dks.append(dk + _dot_nn(ds_s[1, hh], rows_of(q_ref, nq - 1, hh)))
            dvs.append(dv + _dot_nn(pt_s[1, hh], rows_of(do_ref, nq - 1, hh)))
            dq_acc[hh, nq - 1] += _dot_tn(kjs[hh], ds_s[1, hh])
            dcs_all = jnp.where(lane == HPS * pl.program_id(0) + hh, -jnp.sum(dcs, axis=-1, keepdims=True), dcs_all)
        dcs_ref[0] = dcs_all
        dk_ref[...] = jnp.concatenate(dks, axis=-1)
        dv_ref[...] = jnp.concatenate(dvs, axis=-1).astype(BF16)

        @pl.when(j == nkb - 1)
        def _():
            for i in range(nq):
                dq_ref[TQ * i:TQ * (i + 1), :] = jnp.concatenate([dq_acc[hh, i] for hh in range(HPS)], axis=0).T
            dr_ref[...] = dr_acc[...]

    kblk = pl.BlockSpec((TK, HW), lambda hp, j: (j, hp))
    full = pl.BlockSpec((S, HW), lambda hp, j: (0, hp))
    rows = pl.BlockSpec((HPS, nq, 1, TQ), lambda hp, j: (hp, 0, 0, 0))
    cblk = pl.BlockSpec((TK, HPS * LANES), lambda hp, j: (j, hp))
    return _call(
        body, name="fox_bwd", args=(qn, kn, vb, dob, lse, delta, crow, cbc), grid=(N_HEADS // HPS, nkb),
        in_specs=[full, kblk, kblk, full, rows, rows, rows, cblk],
        out_specs=[kblk, kblk, pl.BlockSpec((1, TK, LANES), lambda hp, j: (hp, j, 0)), full, rows],
        out_shape=[_sds((S, D), F32), _sds((S, D), BF16), _sds((N_HEADS // HPS, S, LANES), F32), _sds((S, D), F32),
                   _sds((N_HEADS, nq, 1, TQ), F32)],
        scratch_shapes=[pltpu.VMEM((2, HPS, TK, TQ), F32), pltpu.VMEM((2, HPS, TK, TQ), F32),
                        pltpu.VMEM((2, HPS, TK, TQ), BF16), pltpu.VMEM((2, HPS, TK, TQ), BF16),
                        pltpu.VMEM((HPS, nq, HD, TQ), F32), pltpu.VMEM((HPS, nq, 1, TQ), F32)], rider=rider)


def _prep_a_bwd(dq, dk, dv, dgate, drow, dcs, proj, b_pad, gq, gk):
    nt = S // TM

    def body(dq_ref, dk_ref, dv_ref, dgt_ref, dr_ref, dcs_ref, xq_ref, xk_ref, f_ref, b_ref, gq_ref, gk_ref,
             o_ref, dgq_ref, dgk_ref, db_ref, carry):
        @pl.when(pl.program_id(0) == 0)
        def _():
            carry[...] = jnp.zeros_like(carry)
            dgq_ref[...] = jnp.zeros_like(dgq_ref)
            dgk_ref[...] = jnp.zeros_like(dgk_ref)
            db_ref[...] = jnp.zeros_like(db_ref)

        lane = _lane_iota((TM, LANES))
        lo_half = _half_ones()
        gq2, gk2 = _g2(gq_ref), _g2(gk_ref)
        dgq, dgk = jnp.zeros((1, LANES), F32), jnp.zeros((1, LANES), F32)
        for c in _pairs(D):
            dxq, dg = _head_norm_bwd(dq_ref[:, c] * QSCALE, xq_ref[:, c], gq2, lo_half)
            o_ref[:, c] = dxq.astype(BF16)
            dgq = dgq + dg
            dxk, dg = _head_norm_bwd(dk_ref[:, c], xk_ref[:, c], gk2, lo_half)
            o_ref[:, D + c.start:D + c.stop] = dxk.astype(BF16)
            dgk = dgk + dg
        dgq_ref[...] += dgq
        dgk_ref[...] += dgk
        o_ref[:, 2 * D:3 * D] = dv_ref[...]
        o_ref[:, GOFF:GOFF + D] = dgt_ref[...]

        dc = jnp.concatenate([dr_ref[:, 0, :], jnp.zeros((LANES - N_HEADS, TM), F32)], axis=0).T
        for group in range(N_HEADS // HPS):
            dc = dc + dcs_ref[group]
        r = lax.broadcasted_iota(jnp.int32, (TM, TM), 0)
        c = lax.broadcasted_iota(jnp.int32, (TM, TM), 1)
        tri = (c >= r).astype(F32)
        dlogf = jnp.dot(tri, dc, precision=lax.Precision.HIGHEST, preferred_element_type=F32) + carry[0:1, :]
        carry[0:1, :] = dlogf[0:1, :]
        df = dlogf * (1.0 / (1.0 + jnp.exp(f_ref[...] + b_ref[...])))
        db_ref[...] += jnp.sum(df, axis=0, keepdims=True)
        o_ref[:, FOFF:FOFF + LANES] = df.astype(BF16)
        o_ref[:, FOFF + LANES:NA] = jnp.zeros((TM, NA - FOFF - LANES), BF16)

    rev = lambda width, col: pl.BlockSpec((TM, width), lambda i: (nt - 1 - i, col))
    gspec = pl.BlockSpec((1, HD), lambda i: (0, 0))
    acc = pl.BlockSpec((1, LANES), lambda i: (0, 0))
    return pl.pallas_call(
        body, name="prep_a_bwd", grid=(nt,),
        in_specs=[rev(D, 0), rev(D, 0), rev(D, 0), rev(D, 0),
                  pl.BlockSpec((N_HEADS, None, 1, TM), lambda i: (0, nt - 1 - i, 0, 0)),
                  pl.BlockSpec((N_HEADS // HPS, TM, LANES), lambda i: (0, nt - 1 - i, 0)),
                  rev(D, 0), rev(D, 1), rev(LANES, FOFF // LANES), acc, gspec, gspec],
        out_specs=[rev(NA, 0), acc, acc, acc],
        out_shape=[_sds((S, NA), BF16)] + [_sds((1, LANES), F32)] * 3,
        scratch_shapes=[pltpu.VMEM((8, LANES), F32)],
        compiler_params=_params())(dq, dk, dv, dgate, drow, dcs, proj, proj, proj, b_pad, gq, gk)


def _head_b(x, z_a, w_out_a, g_kv, g_b, w_kv, w_in_b, gq, gk, cos2, sin2):
    nkv = w_kv.shape[1] // 2

    def body(x_ref, z_ref, wo_ref, gkv_ref, gb_ref, wkv_ref, wb_ref, gq_ref, gk_ref, c_ref, s_ref,
             h_ref, ukv_ref, ub_ref, kv_ref, pb_ref, qo_ref, ko_ref, vo_ref):
        xv = x_ref[...] + _dot_nn(z_ref[...], wo_ref[...])
        h_ref[...] = xv
        xn = xv * _rms_rinv(xv)
        ukv = (xn * gkv_ref[...]).astype(BF16)
        ub = (xn * gb_ref[...]).astype(BF16)
        ukv_ref[...] = ukv
        ub_ref[...] = ub
        kv_ref[...] = _dot_nn(ukv, wkv_ref[...])
        for lo in range(0, 2 * D, D):
            pb_ref[:, lo:lo + D] = _dot_nn(ub, wb_ref[:, lo:lo + D])
        lane = _lane_iota((RT, LANES))
        lo_half = lane < HD
        cos, sin = c_ref[...], s_ref[...]
        gq2, gk2 = _g2(gq_ref), _g2(gk_ref)
        for c in _pairs(D):
            q = pb_ref[:, c]
            qo_ref[:, c] = (_rope_fwd((q * _head_rinv(q, lo_half)) * gq2, cos, sin, lane) * QSCALE).astype(BF16)
        for c in _pairs(nkv):
            k = kv_ref[:, c]
            ko_ref[:, c] = _rope_fwd((k * _head_rinv(k, lo_half)) * gk2, cos, sin, lane).astype(BF16)
        vo_ref[...] = kv_ref[:, nkv:2 * nkv].astype(BF16)

    row = lambda width: pl.BlockSpec((RT, width), lambda i: (i, 0))
    whole = lambda arr: pl.BlockSpec(arr.shape, lambda i: (0,) * arr.ndim, pipeline_mode=pl.Buffered(1))
    return pl.pallas_call(
        body, name="head_b", grid=(S // RT,),
        in_specs=[row(D), row(D), whole(w_out_a), whole(g_kv), whole(g_b), whole(w_kv), whole(w_in_b), whole(gq),
                  whole(gk), row(LANES), row(LANES)],
        out_specs=[row(D), row(D), row(D), row(2 * nkv), row(2 * D), row(D), row(nkv), row(nkv)],
        out_shape=[_sds((S, D), F32), _sds((S, D), BF16), _sds((S, D), BF16), _sds((S, 2 * nkv), F32),
                   _sds((S, 2 * D), F32), _sds((S, D), BF16), _sds((S, nkv), BF16), _sds((S, nkv), BF16)],
        compiler_params=_params())(x, z_a, w_out_a, g_kv, g_b, w_kv, w_in_b, gq, gk, cos2, sin2)


N_KV, GRP = 4, 4


def _swa_mask(n):
    r = lax.broadcasted_iota(jnp.int32, (2 * WIN, GRP * WIN), 0)
    q = lax.broadcasted_iota(jnp.int32, (2 * WIN, GRP * WIN), 1) & (WIN - 1)
    return (r > q) & (r <= q + WIN) & ((r >= WIN) | (n > 0))


def _stack4(ref_or_val, base):
    return jnp.concatenate([ref_or_val[:, base + HD * g: base + HD * (g + 1)] for g in range(GRP)], axis=0)


def _unstack4(xt):
    return jnp.concatenate([xt[:, WIN * g:WIN * (g + 1)] for g in range(GRP)], axis=0).T


def _band(prev_ref, cur_ref, kh):
    return jnp.concatenate([prev_ref[:, HD * kh:HD * (kh + 1)], cur_ref[:, HD * kh:HD * (kh + 1)]], axis=0)


def _sink_row(s_ref, first):
    lane = _lane_iota((1, GRP * WIN))
    row = jnp.full((1, GRP * WIN), s_ref[first + GRP - 1], F32)
    for g in range(GRP - 2, -1, -1):
        row = jnp.where(lane < WIN * (g + 1), s_ref[first + g], row)
    return row


def _swa_fwd(qb, ksh, vsh, pb, sinks):
    nb = S // WIN

    def body(q_ref, kp_ref, kc_ref, vp_ref, vc_ref, g_ref, s_ref, o_ref, z_ref, lse_ref):
        n = pl.program_id(0)
        valid = _swa_mask(n)
        outs = []
        for kh in range(N_KV):
            kb, vb = _band(kp_ref, kc_ref, kh), _band(vp_ref, vc_ref, kh)
            st = jnp.where(valid, _dot_nt(kb, _stack4(q_ref, GRP * HD * kh)), -jnp.inf)
            sink = _sink_row(s_ref, GRP * kh)
            m = jnp.maximum(jnp.max(st, axis=0, keepdims=True), sink)
            pt = jnp.exp(st - m)
            l = jnp.sum(pt, axis=0, keepdims=True) + jnp.exp(sink - m)
            outs.append(_unstack4(_dot_tn(vb, pt.astype(BF16)) / l))
            lse = m + jnp.log(l)
            for g in range(GRP):
                lse_ref[GRP * kh + g, 0] = lse[:, WIN * g:WIN * (g + 1)]
        o = jnp.concatenate(outs, axis=-1)
        o_ref[...] = o
        g = g_ref[...]
        z_ref[...] = (o * (g * _sigmoid(g))).astype(BF16)

    row = pl.BlockSpec((WIN, D), lambda n: (n, 0))
    prev = pl.BlockSpec((WIN, N_KV * HD), lambda n: (jnp.maximum(n - 1, 0), 0))
    cur = pl.BlockSpec((WIN, N_KV * HD), lambda n: (n, 0))
    return pl.pallas_call(
        body, name="swa_fwd", grid=(nb,),
        in_specs=[row, prev, cur, prev, cur, pl.BlockSpec((WIN, D), lambda n: (n, 1)),
                  pl.BlockSpec(memory_space=pltpu.SMEM)],
        out_specs=[row, row, pl.BlockSpec((N_HEADS, 1, 1, WIN), lambda n: (0, n, 0, 0))],
        out_shape=[_sds((S, D), F32), _sds((S, D), BF16), _sds((N_HEADS, nb, 1, WIN), F32)],
        compiler_params=_params())(qb, ksh, ksh, vsh, vsh, pb, sinks)


def _swa_bwd(qb, ksh, vsh, dz, o, lse, pb, sinks, gq, cos2, sin2):
    nb = S // WIN

    def body(q_ref, kp_ref, kc_ref, vp_ref, vc_ref, dz_ref, o_ref, lse_ref, x_ref, g_ref, s_ref, gq_ref, c_ref, sn_ref,
             dpb_ref, dka_ref, dkb_ref, dva_ref, dvb_ref, dsink_ref, dgq_ref):
        n = pl.program_id(0)

        @pl.when(n == 0)
        def _():
            dsink_ref[...] = jnp.zeros_like(dsink_ref)
            dgq_ref[...] = jnp.zeros_like(dgq_ref)

        valid = _swa_mask(n)
        g = g_ref[...]
        sg = _sigmoid(g)
        dzv = dz_ref[...]
        ov = o_ref[...]
        do = dzv * (g * sg)
        dpb_ref[:, D:2 * D] = (dzv * ov * (sg * (1.0 + g * (1.0 - sg)))).astype(BF16)
        prod_t = (do * ov).T
        lane1 = _lane_iota((1, LANES))
        dqs, dkas, dkbs, dvas, dvbs = [], [], [], [], []
        dsink = jnp.zeros((1, LANES), F32)
        for kh in range(N_KV):
            kb, vb = _band(kp_ref, kc_ref, kh), _band(vp_ref, vc_ref, kh)
            base = GRP * HD * kh
            qs = _stack4(q_ref, base)
            dos = _stack4(do, base).astype(BF16)
            delta = jnp.concatenate(
                [jnp.sum(prod_t[base + HD * gg:base + HD * (gg + 1), :], axis=0, keepdims=True)
                 for gg in range(GRP)], axis=1)
            lse = jnp.concatenate([lse_ref[GRP * kh + gg, 0] for gg in range(GRP)], axis=1)
            st = jnp.where(valid, _dot_nt(kb, qs), -jnp.inf)
            pt = jnp.exp(st - lse)
            dst = pt * (_dot_nt(vb, dos) - delta)
            dsb = dst.astype(BF16)
            dqs.append(_unstack4(_dot_tn(kb, dsb)))
            dkband = _dot_nn(dsb, qs)
            dvband = _dot_nn(pt.astype(BF16), dos)
            dkbs.append(dkband[0:WIN, :])
            dkas.append(dkband[WIN:2 * WIN, :])
            dvbs.append(dvband[0:WIN, :])
            dvas.append(dvband[WIN:2 * WIN, :])
            ps_delta = jnp.exp(_sink_row(s_ref, GRP * kh) - lse) * delta
            for gg in range(GRP):
                val = jnp.sum(ps_delta[:, WIN * gg:WIN * (gg + 1)], axis=1, keepdims=True)
                dsink = dsink - jnp.where(lane1 == GRP * kh + gg, val, 0.0)
        dka_ref[...] = jnp.concatenate(dkas, axis=-1)
        dkb_ref[...] = jnp.concatenate(dkbs, axis=-1)
        dva_ref[...] = jnp.concatenate(dvas, axis=-1)
        dvb_ref[...] = jnp.concatenate(dvbs, axis=-1)
        dsink_ref[...] += dsink

        lane = _lane_iota((WIN, LANES))
        g2, cos, sin = _g2(gq_ref), c_ref[...], sn_ref[...]
        lo_half = _half_ones()
        dg_tot = jnp.zeros((1, LANES), F32)
        for kh in range(N_KV):
            for c in _pairs(GRP * HD):
                cols = slice(GRP * HD * kh + c.start, GRP * HD * kh + c.stop)
                dn = _rope_bwd(dqs[kh][:, c] * QSCALE, cos, sin, lane)
                dx, dg = _head_norm_bwd(dn, x_ref[:, cols], g2, lo_half)
                dpb_ref[:, cols] = dx.astype(BF16)
                dg_tot = dg_tot + dg
        dgq_ref[...] += dg_tot

    row = pl.BlockSpec((WIN, D), lambda n: (n, 0))
    prev = pl.BlockSpec((WIN, N_KV * HD), lambda n: (jnp.maximum(n - 1, 0), 0))
    cur = pl.BlockSpec((WIN, N_KV * HD), lambda n: (n, 0))
    acc = pl.BlockSpec((1, LANES), lambda n: (0, 0))
    tab = pl.BlockSpec((WIN, LANES), lambda n: (n, 0))
    return pl.pallas_call(
        body, name="swa_bwd", grid=(nb,),
        in_specs=[row, prev, cur, prev, cur, row, row, pl.BlockSpec((N_HEADS, 1, 1, WIN), lambda n: (0, n, 0, 0)),
                  row, pl.BlockSpec((WIN, D), lambda n: (n, 1)), pl.BlockSpec(memory_space=pltpu.SMEM),
                  pl.BlockSpec((1, HD), lambda n: (0, 0)), tab, tab],
        out_specs=[pl.BlockSpec((WIN, 2 * D), lambda n: (n, 0)), cur, cur, cur, cur, acc, acc],
        out_shape=[_sds((S, 2 * D), BF16)] + [_sds((S, 256), F32)] * 4 + [_sds((1, LANES), F32)] * 2,
        compiler_params=_params())(qb, ksh, ksh, vsh, vsh, dz, o, lse, pb, pb, sinks, gq, cos2, sin2)


def _prep_kv_bwd(dka, dkb, dva, dvb, kv, gk, cos2, sin2):
    nt = S // RB
    per = RB // WIN

    def shifted(cur_ref, nxt_ref, has_next):
        return jnp.concatenate([cur_ref[WIN:RB, :], jnp.where(has_next, nxt_ref[...], 0.0)], axis=0)

    def body(dka_ref, dkb_ref, dkn_ref, dva_ref, dvb_ref, dvn_ref, x_ref, g_ref, c_ref, s_ref, o_ref, dgk_ref):
        i, j = pl.program_id(0), pl.program_id(1)

        @pl.when((i == 0) & (j == 0))
        def _():
            dgk_ref[...] = jnp.zeros_like(dgk_ref)

        has_next = i < nt - 1

        @pl.when(j == 0)
        def _():
            lane = _lane_iota((RB, LANES))
            g2, cos, sin = _g2(g_ref), c_ref[...], s_ref[...]
            dy_all = dka_ref[...] + shifted(dkb_ref, dkn_ref, has_next)
            dg_tot = jnp.zeros((1, LANES), F32)
            lo_half = _half_ones()
            for c in _pairs(CB):
                dn = _rope_bwd(dy_all[:, c], cos, sin, lane)
                dx, dg = _head_norm_bwd(dn, x_ref[:, c], g2, lo_half)
                o_ref[:, c] = dx.astype(BF16)
                dg_tot = dg_tot + dg
            dgk_ref[...] += dg_tot

        @pl.when(j == 1)
        def _():
            o_ref[...] = (dva_ref[...] + shifted(dvb_ref, dvn_ref, has_next)).astype(BF16)

    cur = pl.BlockSpec((RB, CB), lambda i, j: (i, 0))
    nxt = pl.BlockSpec((WIN, CB), lambda i, j: (jnp.minimum(per * (i + 1), S // WIN - 1), 0))
    tab = pl.BlockSpec((RB, LANES), lambda i, j: (i, 0))
    return pl.pallas_call(
        body, name="prep_kv_bwd", grid=(nt, 2),
        in_specs=[cur, cur, nxt, cur, cur, nxt, cur, pl.BlockSpec((1, HD), lambda i, j: (0, 0)), tab, tab],
        out_specs=[pl.BlockSpec((RB, CB), lambda i, j: (i, j)), pl.BlockSpec((1, LANES), lambda i, j: (0, 0))],
        out_shape=[_sds((S, 2 * CB), BF16), _sds((1, LANES), F32)],
        compiler_params=_params())(dka, dkb, dkb, dva, dvb, dvb, kv, gk, cos2, sin2)


def _out_b_loss(z, w_out, h1, tgt):
    tm = 2 * RT

    def body(z_ref, w_ref, h_ref, t_ref, dy_ref, l_ref):
        @pl.when(pl.program_id(0) == 0)
        def _():
            l_ref[...] = jnp.zeros_like(l_ref)

        e = (h_ref[...] + _dot_nn(z_ref[...], w_ref[...])) - t_ref[...]
        dy_ref[...] = e * (1.0 / D)
        l_ref[...] += jnp.sum(jnp.sum(e * e, axis=-1, keepdims=True), axis=0, keepdims=True)

    row = pl.BlockSpec((tm, D), lambda i: (i, 0))
    return pl.pallas_call(
        body, name="out_b_loss", grid=(S // tm,),
        in_specs=[row, pl.BlockSpec(w_out.shape, lambda i: (0, 0), pipeline_mode=pl.Buffered(1)), row, row],
        out_specs=[row, pl.BlockSpec((1, LANES), lambda i: (0, 0))],
        out_shape=[_sds((S, D), F32), _sds((1, LANES), F32)], compiler_params=_params())(z, w_out, h1, tgt)


def _du_a_rms_bwd(dproj, wa, x, g, dres, after):
    tm, tk = 1024, NA // 2
    nk = NA // tk

    def body(a_ref, b_ref, x_ref, g_ref, dr_ref, after_ref, dx_ref, dg_ref, acc):
        i, kk = pl.program_id(0), pl.program_id(1)

        @pl.when((i == 0) & (kk == 0))
        def _():
            dg_ref[...] = jnp.zeros_like(dg_ref)

        p = _dot_nt(a_ref[...], b_ref[...])

        @pl.when(kk == 0)
        def _():
            acc[...] = p

        @pl.when(kk == nk - 1)
        def _():
            dx, dg = _rms_bwd_core(acc[...] + p, x_ref[...], g_ref[...])
            dx_ref[...] = dr_ref[...] + dx
            dg_ref[...] += dg

    assert nk == 2
    row = pl.BlockSpec((tm, D), lambda i, kk: (i, 0))
    vec = pl.BlockSpec((1, D), lambda i, kk: (0, 0))
    return pl.pallas_call(
        body, name="du_a_rms_bwd", grid=(S // tm, nk),
        in_specs=[pl.BlockSpec((tm, tk), lambda i, kk: (i, kk)), pl.BlockSpec((D, tk), lambda i, kk: (0, kk)),
                  row, vec, row, pl.BlockSpec(after.shape, lambda i, kk: (0, 0))],
        out_specs=[row, vec], out_shape=[_sds((S, D), F32), _sds((1, D), F32)],
        scratch_shapes=[pltpu.VMEM((tm, D), F32)], compiler_params=_params())(dproj, wa, x, g, dres, after)


def _du_b_rms_bwd(dpb, w_in_b, dkv, w_kv, h1, g_b, g_kv, dy):
    tm = 2 * RT

    def body(ab_ref, wb_ref, akv_ref, wkv_ref, x_ref, gb_ref, gkv_ref, dy_ref, dh_ref, dgb_ref, dgkv_ref):
        @pl.when(pl.program_id(0) == 0)
        def _():
            dgb_ref[...] = jnp.zeros_like(dgb_ref)
            dgkv_ref[...] = jnp.zeros_like(dgkv_ref)

        x = x_ref[...]
        dx1, dg1 = _rms_bwd_core(_dot_nt(ab_ref[...], wb_ref[...]), x, gb_ref[...])
        dx2, dg2 = _rms_bwd_core(_dot_nt(akv_ref[...], wkv_ref[...]), x, gkv_ref[...])
        dh_ref[...] = dy_ref[...] + dx1 + dx2
        dgb_ref[...] += dg1
        dgkv_ref[...] += dg2

    row = lambda width: pl.BlockSpec((tm, width), lambda i: (i, 0))
    whole = lambda arr: pl.BlockSpec(arr.shape, lambda i: (0, 0), pipeline_mode=pl.Buffered(1))
    vec = pl.BlockSpec((1, D), lambda i: (0, 0))
    return pl.pallas_call(
        body, name="du_b_rms_bwd", grid=(S // tm,),
        in_specs=[row(dpb.shape[1]), whole(w_in_b), row(dkv.shape[1]), whole(w_kv), row(D), vec, vec, row(D)],
        out_specs=[row(D), vec, vec], out_shape=[_sds((S, D), F32), _sds((1, D), F32), _sds((1, D), F32)],
        compiler_params=_params())(dpb, w_in_b, dkv, w_kv, h1, g_b, g_kv, dy)


GATHER_CHUNKS = 4


def _gather_first(w_in_a, w_out_a, w_kv, w_in_b, w_out_b, norm_a_g):
    rows, cols = w_in_a.shape[-2:]
    groups = rows // LANES
    cols_pad = -(-cols // LANES) * LANES

    def body(wia_ref, woa_ref, wkv_ref, wib_ref, wob_ref, ga_ref,
             wa_g, ga_g, woa_s, wkv_s, wib_s, wob_s, wa_s, st_a, st_oa, st_kv, st_ib, st_ob, plane_t, load_sems, *sems):
        sources = [wia_ref, woa_ref.at[0], wkv_ref, wib_ref.at[0], wob_ref.at[0]]
        stages = [st_a, st_oa, st_kv, st_ib, st_ob]
        loads = [pltpu.make_async_copy(src, dst, load_sems.at[i]) for i, (src, dst) in enumerate(zip(sources, stages))]
        for cp in loads:
            cp.start()
        loads[0].wait()
        plane_t[cols_pad - LANES:, :] = jnp.zeros((LANES, LANES), F32)
        for j in range(groups):
            for c0 in range(0, cols, 64):
                n = min(64, cols - c0)
                plane_t[c0:c0 + n, :] = st_a[pl.ds(c0 * groups + j, n, stride=groups), :]
            for c0 in range(0, cols, LANES):
                n = min(LANES, cols - c0)
                wa_s[j * LANES:(j + 1) * LANES, c0:c0 + n] = plane_t[c0:c0 + LANES, :].T[:, :n].astype(BF16)

        def cast_the_rest():
            for cp, stage, out in zip(loads[1:], stages[1:], [woa_s, wkv_s, wib_s, wob_s]):
                cp.wait()
                out[...] = stage[...].astype(BF16)

        chunks = [pl.ds(r0, rows // GATHER_CHUNKS) for r0 in range(0, rows, rows // GATHER_CHUNKS)]
        _gather_two_level(
            [wa_s.at[rc] for rc in chunks] + [ga_ref],
            [lambda dev, rc=rc: wa_g.at[dev, rc] for rc in chunks] + [lambda dev: ga_g.at[dev]],
            sems, meanwhile=cast_the_rest)

    vmem = pl.BlockSpec(memory_space=pltpu.VMEM)
    anyspec = pl.BlockSpec(memory_space=pl.ANY)
    shard = lambda w: _sds(w.shape[-2:], BF16)
    stage = lambda w: pltpu.VMEM(w.shape[-2:], F32)
    return pl.pallas_call(
        body, name="gather_first", in_specs=[anyspec] * 5 + [vmem],
        out_specs=[anyspec, anyspec, vmem, vmem, vmem, vmem],
        out_shape=[_sds((N_DEV,) + w_in_a.shape[-2:], BF16), _sds((N_DEV,) + norm_a_g.shape, F32),
                   shard(w_out_a), shard(w_kv), shard(w_in_b), shard(w_out_b)],
        scratch_shapes=[pltpu.VMEM(w_in_a.shape[-2:], BF16), pltpu.VMEM((cols * groups, LANES), F32), stage(w_out_a),
                        stage(w_kv), stage(w_in_b), stage(w_out_b), pltpu.VMEM((cols_pad, LANES), F32),
                        pltpu.SemaphoreType.DMA((5,))] + _exchange_sems(GATHER_CHUNKS + 1),
        compiler_params=pltpu.CompilerParams(vmem_limit_bytes=VMEM_LIMIT, has_side_effects=True))(
            _entry_view(w_in_a).reshape(cols * groups, LANES), w_out_a, w_kv, w_in_b, w_out_b, norm_a_g)


def _padded_col(c):
    if c < RAW_F:
        return c
    return FOFF + (c - RAW_F) if c < RAW_G else GOFF + (c - RAW_G)


def _shard_pieces():
    width = NA_RAW // N_DEV
    pieces = []
    for d in range(N_DEV):
        cuts = [width * d] + [c for c in (RAW_F, RAW_G) if width * d < c < width * (d + 1)] + [width * (d + 1)]
        for lo, hi in zip(cuts[:-1], cuts[1:]):
            pieces.append((d, lo - width * d, _padded_col(lo), hi - lo))
    return pieces


def _unshard_wa(wa_g):
    def body(w_ref, o_ref):
        o_ref[:, FOFF + N_HEADS:NA] = jnp.zeros((TM, NA - FOFF - N_HEADS), BF16)
        for d, src, dst, width in _shard_pieces():
            o_ref[:, dst:dst + width] = w_ref[d, :, src:src + width]

    return pl.pallas_call(
        body, name="unshard_wa", grid=(D // TM,),
        in_specs=[pl.BlockSpec((N_DEV, TM, NA_RAW // N_DEV), lambda i: (0, i, 0))],
        out_specs=pl.BlockSpec((TM, NA), lambda i: (i, 0)), out_shape=_sds((D, NA), BF16),
        compiler_params=_params())(wa_g)


def _reshard_pair_reduce(dwa):
    n_chip, nt = N_DEV // 2, D // TM
    width = NA_RAW // N_DEV

    def body(g_ref, o_ref, slots_v, sib_v, send_sems, recv_sems):
        i = pl.program_id(0)
        x, y, c = lax.axis_index("x"), lax.axis_index("y"), lax.axis_index("c")

        def tile_rows(tile):
            return pl.ds(tile * TM if isinstance(tile, int) else pl.multiple_of(tile * TM, TM), TM)

        def give(j, tile):
            return pltpu.make_async_remote_copy(
                src_ref=slots_v.at[2 * j + 1 - c, tile_rows(tile)], dst_ref=sib_v.at[j, tile_rows(tile)],
                send_sem=send_sems.at[j, tile], recv_sem=recv_sems.at[j, tile], device_id=(x, y, 1 - c),
                device_id_type=pl.DeviceIdType.MESH)

        for d, src, dst, w in _shard_pieces():
            slots_v[d, tile_rows(i), src:src + w] = g_ref[:, dst:dst + w]
        for j in range(n_chip):
            give(j, i).start()

        @pl.when(i == nt - 1)
        def _():
            for tile in range(nt):
                for j in range(n_chip):
                    give(j, tile).wait()
            for j in range(n_chip):
                for r0 in range(0, D, 64):
                    mine = slots_v[2 * j + c, r0:r0 + 64, :].astype(F32)
                    o_ref[j, r0:r0 + 64, :] = (mine + sib_v[j, r0:r0 + 64, :].astype(F32)).astype(BF16)

    half = _sds((n_chip, D, width), dwa.dtype)
    return pl.pallas_call(
        body, name="reshard_pair_reduce", grid=(nt,), in_specs=[pl.BlockSpec((TM, NA), lambda i: (i, 0))],
        out_specs=pl.BlockSpec(half.shape, lambda i: (0, 0, 0)), out_shape=half,
        scratch_shapes=[pltpu.VMEM((N_DEV, D, width), dwa.dtype), pltpu.VMEM(half.shape, dwa.dtype),
                        pltpu.SemaphoreType.DMA((n_chip, nt)), pltpu.SemaphoreType.DMA((n_chip, nt))],
        compiler_params=pltpu.CompilerParams(vmem_limit_bytes=VMEM_LIMIT, has_side_effects=True))(dwa)


CHIP_FLIPS = (2, 4, 6)


def _chip_exchange_start(partial):
    n = len(CHIP_FLIPS)

    def body(p_ref, land_ref, *rest):
        sends, recvs, token = rest[:n], rest[n:2 * n], rest[2 * n + 2]
        x, y, c = lax.axis_index("x"), lax.axis_index("y"), lax.axis_index("c")
        me = 4 * x + 2 * y + c
        for idx, k in enumerate(CHIP_FLIPS):
            pltpu.make_async_remote_copy(
                src_ref=p_ref.at[(me ^ k) >> 1], dst_ref=land_ref.at[me >> 1], send_sem=sends[idx],
                recv_sem=recvs[idx], device_id=(x ^ ((k >> 2) & 1), y ^ ((k >> 1) & 1), c),
                device_id_type=pl.DeviceIdType.MESH).start()
        token[...] = jnp.zeros_like(token)

    hbm = pl.BlockSpec(memory_space=pltpu.HBM)
    sem = pl.BlockSpec(memory_space=pltpu.SEMAPHORE)
    buf = pltpu.HBM(partial.shape, partial.dtype)
    return pl.pallas_call(
        body, name="chip_exchange_start",
        out_shape=(pltpu.SemaphoreType.DMA(()),) * (2 * n) + (buf, buf, _sds((8, LANES), F32)),
        in_specs=(hbm, hbm), out_specs=(sem,) * (2 * n) + (hbm, hbm, pl.BlockSpec(memory_space=pltpu.VMEM)),
        input_output_aliases={0: 2 * n, 1: 2 * n + 1},
        compiler_params=pltpu.CompilerParams(has_side_effects=pltpu.SideEffectType.DATAFLOW_SIDE_EFFECTING))(
            pltpu.with_memory_space_constraint(partial, pltpu.HBM),
            pltpu.with_memory_space_constraint(lax.empty(partial.shape, partial.dtype), pltpu.HBM))


def _chip_exchange_wait(started, after):
    n = len(CHIP_FLIPS)
    sems, (p_thru, land_thru) = started[:2 * n], started[2 * n:2 * n + 2]

    def body(p_ref, land_ref, *rest):
        sends, recvs = rest[:n], rest[n:2 * n]
        x, y, c = lax.axis_index("x"), lax.axis_index("y"), lax.axis_index("c")
        me = 4 * x + 2 * y + c
        for idx, k in enumerate(CHIP_FLIPS):
            copy = pltpu.make_async_remote_copy(
                src_ref=p_ref.at[(me ^ k) >> 1], dst_ref=land_ref.at[(me ^ k) >> 1], send_sem=sends[idx],
                recv_sem=recvs[idx], device_id=(x ^ ((k >> 2) & 1), y ^ ((k >> 1) & 1), c),
                device_id_type=pl.DeviceIdType.MESH)
            copy.wait_send()
            copy.wait_recv()

    hbm = pl.BlockSpec(memory_space=pltpu.HBM)
    sem = pl.BlockSpec(memory_space=pltpu.SEMAPHORE)
    buf = pltpu.HBM(p_thru.shape, p_thru.dtype)
    return pl.pallas_call(
        body, name="chip_exchange_wait", out_shape=(buf, buf),
        in_specs=(hbm, hbm) + (sem,) * (2 * n) + (pl.BlockSpec(memory_space=pl.ANY),) * len(after),
        out_specs=(hbm, hbm), input_output_aliases={0: 0, 1: 1},
        compiler_params=pltpu.CompilerParams(has_side_effects=pltpu.SideEffectType.DATAFLOW_SIDE_EFFECTING))(
            p_thru, land_thru, *sems, *after)


def _slab_peer(x, y, c, k):
    return (x ^ ((k >> 2) & 1), y ^ ((k >> 1) & 1), c ^ (k & 1))


def _gather_slab_start(slab, carried):
    n = N_DEV - 1

    def body(s_ref, land_ref, carried_ref, *rest):
        sends, recvs, own_sem = rest[:n], rest[n:2 * n], rest[2 * n + 3]
        x, y, c = lax.axis_index("x"), lax.axis_index("y"), lax.axis_index("c")
        me = 4 * x + 2 * y + c
        for k in range(1, N_DEV):
            pltpu.make_async_remote_copy(
                src_ref=s_ref, dst_ref=land_ref.at[me], send_sem=sends[k - 1], recv_sem=recvs[k - 1],
                device_id=_slab_peer(x, y, c, k), device_id_type=pl.DeviceIdType.MESH).start()
        own = pltpu.make_async_copy(s_ref, land_ref.at[me], own_sem)
        own.start()
        own.wait()

    hbm = pl.BlockSpec(memory_space=pltpu.HBM)
    sem = pl.BlockSpec(memory_space=pltpu.SEMAPHORE)
    land = (N_DEV,) + slab.shape
    return pl.pallas_call(
        body, name="gather_slab_start",
        out_shape=(pltpu.SemaphoreType.DMA(()),) * (2 * n) + (
            pltpu.HBM(slab.shape, slab.dtype), pltpu.HBM(land, slab.dtype), pltpu.HBM(carried.shape, carried.dtype)),
        in_specs=(hbm, hbm, hbm), out_specs=(sem,) * (2 * n) + (hbm, hbm, hbm),
        input_output_aliases={0: 2 * n, 1: 2 * n + 1, 2: 2 * n + 2}, scratch_shapes=[pltpu.SemaphoreType.DMA(())],
        compiler_params=pltpu.CompilerParams(has_side_effects=pltpu.SideEffectType.DATAFLOW_SIDE_EFFECTING))(
            pltpu.with_memory_space_constraint(slab, pltpu.HBM),
            pltpu.with_memory_space_constraint(lax.empty(land, slab.dtype), pltpu.HBM),
            pltpu.with_memory_space_constraint(carried, pltpu.HBM))


def _gather_slab_wait(started, after):
    n = N_DEV - 1
    sems, (s_thru, land_thru) = started[:2 * n], started[2 * n:2 * n + 2]

    def body(s_ref, land_ref, *rest):
        sends, recvs = rest[:n], rest[n:2 * n]
        x, y, c = lax.axis_index("x"), lax.axis_index("y"), lax.axis_index("c")
        me = 4 * x + 2 * y + c
        for k in range(1, N_DEV):
            copy = pltpu.make_async_remote_copy(
                src_ref=s_ref, dst_ref=land_ref.at[me ^ k], send_sem=sends[k - 1], recv_sem=recvs[k - 1],
                device_id=_slab_peer(x, y, c, k), device_id_type=pl.DeviceIdType.MESH)
            copy.wait_send()
            copy.wait_recv()

    hbm = pl.BlockSpec(memory_space=pltpu.HBM)
    sem = pl.BlockSpec(memory_space=pltpu.SEMAPHORE)
    return pl.pallas_call(
        body, name="gather_slab_wait",
        out_shape=(pltpu.HBM(s_thru.shape, s_thru.dtype), pltpu.HBM(land_thru.shape, land_thru.dtype)),
        in_specs=(hbm, hbm) + (sem,) * (2 * n) + (pl.BlockSpec(memory_space=pl.ANY),) * len(after),
        out_specs=(hbm, hbm), input_output_aliases={0: 0, 1: 1},
        compiler_params=pltpu.CompilerParams(has_side_effects=pltpu.SideEffectType.DATAFLOW_SIDE_EFFECTING))(
            s_thru, land_thru, *sems, *after)[1]


def _adamw(w, g, m, v):
    m = ADAM_B1 * m + (1.0 - ADAM_B1) * g
    v = ADAM_B2 * v + (1.0 - ADAM_B2) * (g * g)
    m_hat = m / (1.0 - ADAM_B1 ** ADAM_STEP)
    v_hat = v / (1.0 - ADAM_B2 ** ADAM_STEP)
    delta = -ADAM_LR * (m_hat / (jnp.sqrt(v_hat) + ADAM_EPS) + ADAM_WD * w)
    return delta, m, v


def _sum_adamw(recv, w, m, v, name, after=None):
    lead = w.ndim - 2
    rows, cols = w.shape[-2:]
    tr = 256 if rows % 256 == 0 else 128
    n_slots = recv.shape[0]
    extra = [] if after is None else [after]

    def body(*refs):
        r_ref = refs[0]
        w_ref, m_ref, v_ref, g_ref, d_ref, nm_ref, nv_ref = refs[1 + len(extra):]
        g = None
        for slot in range(n_slots):
            g = r_ref[slot].astype(F32) if g is None else g + r_ref[slot].astype(F32)
        g_ref[...] = g
        d_ref[...], nm_ref[...], nv_ref[...] = _adamw(w_ref[...], g, m_ref[...], v_ref[...])

    blk = pl.BlockSpec((None,) * lead + (tr, cols), lambda i: (0,) * lead + (i, 0))
    slots = pl.BlockSpec((n_slots, tr, cols), lambda i: (0, i, 0))
    return pl.pallas_call(
        body, name=name, grid=(rows // tr,),
        in_specs=[slots] + [pl.BlockSpec(a.shape, lambda i: (0, 0)) for a in extra] + [blk, blk, blk],
        out_specs=[blk] * 4, out_shape=[_sds(w.shape, F32)] * 4, compiler_params=_params())(recv, *extra, w, m, v)


def _entry_view(a):
    _, rows, cols = a.shape
    return jnp.transpose(a, (2, 0, 1)).reshape(cols, rows // LANES, LANES)


def _from_entry_view(a):
    cols, groups, lanes = a.shape
    return jnp.transpose(a, (1, 2, 0)).reshape(1, groups * lanes, cols)


def _sum_adamw_entry_view(landing, own, w, m, v, name):
    n_slots, rows, cols = landing.shape
    groups = rows // LANES
    cols_pad = -(-cols // LANES) * LANES

    def body(r_ref, own_ref, w_ref, m_ref, v_ref, g_ref, d_ref, nm_ref, nv_ref, pad_ref, gt_ref):
        j = pl.program_id(0)
        chip = (4 * lax.axis_index("x") + 2 * lax.axis_index("y") + lax.axis_index("c")) >> 1
        pad_ref[:, cols_pad - LANES:] = jnp.zeros((LANES, LANES), F32)
        for r0 in range(0, LANES, 32):
            g = None
            for slot in range(n_slots):
                part = jnp.where(chip == slot, own_ref[slot, r0:r0 + 32], r_ref[slot, r0:r0 + 32])
                g = part.astype(F32) if g is None else g + part.astype(F32)
            pad_ref[r0:r0 + 32, :cols] = g
        for c0 in range(0, cols_pad, LANES):
            gt_ref[c0:c0 + LANES, :] = pad_ref[:, c0:c0 + LANES].T
        def update_plane(plane):
            for c0 in range(0, cols, 64):
                n = min(64, cols - c0)
                at = pl.ds(c0 * groups + plane, n, stride=groups)
                gt = gt_ref[c0:c0 + n, :]
                g_ref[at, :] = gt
                d_ref[at, :], nm_ref[at, :], nv_ref[at, :] = _adamw(w_ref[at, :], gt, m_ref[at, :], v_ref[at, :])

        for plane in range(groups):
            pl.when(j == plane)(lambda plane=plane: update_plane(plane))

    flat = lambda a: _entry_view(a).reshape(cols * groups, LANES)
    whole = pl.BlockSpec((cols * groups, LANES), lambda j: (0, 0))
    slots = pl.BlockSpec((n_slots, LANES, cols), lambda j: (0, j, 0))
    outs = pl.pallas_call(
        body, name=name, grid=(groups,), in_specs=[slots, slots, whole, whole, whole], out_specs=[whole] * 4,
        out_shape=[_sds((cols * groups, LANES), F32)] * 4,
        scratch_shapes=[pltpu.VMEM((LANES, cols_pad), F32), pltpu.VMEM((cols_pad, LANES), F32)],
        compiler_params=_params())(landing, own, flat(w), flat(m), flat(v))
    return [_from_entry_view(o.reshape(cols, groups, LANES)) for o in outs], outs[0]


SLAB_ROWS = 16
SLOT = {"kv_norm_g": (8, 0, D), "norm_b_g": (9, 0, D), "b_forget": (10, 0, 16), "qnorm_a_g": (10, 128, HD),
        "knorm_a_g": (10, 256, HD), "knorm_b_g": (10, 384, HD), "qnorm_b_g": (10, 512, HD), "sinks": (10, 640, 16)}
SMALL = ["norm_a_g", "b_forget", "qnorm_a_g", "knorm_a_g", "kv_norm_g", "knorm_b_g", "norm_b_g", "qnorm_b_g", "sinks"]


LOSS_ROW = 11


def _pack_small(dg_a, dg_kv, dg_b, db_f, dgq_a, dgk_a, dgk_b, dgq_b, dsinks, lsum):
    def fold(ref):
        return ref[:, 0:HD] + ref[:, HD:2 * HD]

    def body(dga_ref, dgkv_ref, dgb_ref, dbf_ref, dgqa_ref, dgka_ref, dgkb_ref, dgqb_ref, dsk_ref, ls_ref, slab_ref):
        slab_ref[...] = jnp.zeros_like(slab_ref)
        for r in range(N_DEV):
            slab_ref[r:r + 1, 0:LANES] = dga_ref[:, LANES * r:LANES * (r + 1)]
        slab_ref[8:9, :] = dgkv_ref[...]
        slab_ref[9:10, :] = dgb_ref[...]
        slab_ref[10:11, 0:LANES] = dbf_ref[...]
        slab_ref[10:11, 128:128 + HD] = fold(dgqa_ref)
        slab_ref[10:11, 256:256 + HD] = fold(dgka_ref)
        slab_ref[10:11, 384:384 + HD] = fold(dgkb_ref)
        slab_ref[10:11, 512:512 + HD] = fold(dgqb_ref)
        slab_ref[10:11, 640:640 + LANES] = dsk_ref[...]
        slab_ref[LOSS_ROW:LOSS_ROW + 1, 0:LANES] = ls_ref[...]

    return pl.pallas_call(body, name="pack_small", out_shape=_sds((SLAB_ROWS, D), F32), compiler_params=_params())(
        dg_a, dg_kv, dg_b, db_f, dgq_a, dgk_a, dgk_b, dgq_b, dsinks, lsum)


def _small_adamw(recv, ws, ms, vs):
    k = len(SMALL)

    def body(*refs):
        r_ref = refs[0]
        w_refs, m_refs, v_refs = refs[1:1 + k], refs[1 + k:1 + 2 * k], refs[1 + 2 * k:1 + 3 * k]
        outs = refs[1 + 3 * k:1 + 7 * k]
        loss_ref, tot = refs[1 + 7 * k], refs[2 + 7 * k]
        g = r_ref[0]
        for dev in range(1, N_DEV):
            g = g + r_ref[dev]
        tot[...] = g
        loss_ref[...] = tot[LOSS_ROW:LOSS_ROW + 1, 0:LANES] * (0.5 / D)
        me = 4 * lax.axis_index("x") + 2 * lax.axis_index("y") + lax.axis_index("c")
        for p, name in enumerate(SMALL):
            if name == "norm_a_g":
                mine = lax.broadcasted_iota(jnp.int32, (N_DEV, LANES), 0) == me
                gp = jnp.sum(jnp.where(mine, tot[0:N_DEV, 0:LANES], 0.0), axis=0, keepdims=True)
            else:
                row, lo, width = SLOT[name]
                gp = tot[row:row + 1, lo:lo + width]
            d, nm, nv = _adamw(w_refs[p][...], gp, m_refs[p][...], v_refs[p][...])
            outs[p][...] = gp
            outs[k + p][...] = d
            outs[2 * k + p][...] = nm
            outs[3 * k + p][...] = nv

    shapes = [_sds(w.shape, F32) for w in ws]
    return pl.pallas_call(body, name="small_adamw", out_shape=shapes * 4 + [_sds((1, LANES), F32)],
                          scratch_shapes=[pltpu.VMEM((SLAB_ROWS, D), F32)],
                          compiler_params=_params())(recv, *ws, *ms, *vs)


def _rope_tables(positions):
    inv_freq = jnp.power(jnp.float32(ROPE_THETA), -jnp.arange(0, ROT, 2, dtype=F32) / ROT)
    ang = positions.astype(F32)[:, None] * inv_freq[None, :]
    cos, sin = jnp.cos(ang), jnp.sin(ang)
    c64 = jnp.concatenate([cos, cos, jnp.ones((S, HD - ROT), F32)], axis=-1)
    s64 = jnp.concatenate([-sin, sin, jnp.zeros((S, HD - ROT), F32)], axis=-1)
    return jnp.tile(c64, (1, 2)), jnp.tile(s64, (1, 2))


def _local_step(x, tgt, positions, g_a, wa, b_forget, gq_a, gk_a, g_kv, gk_b, g_b, gq_b, sinks,
                woa_s, wkv_s, wib_s, wob_s, adamw_others):
    nq = S // TQ
    cos2, sin2 = _rope_tables(positions)
    b_pad = jnp.pad(b_forget, ((0, 0), (0, LANES - N_HEADS)))

    u_a, proj, qn, kn, vb, crow, cbc = _head_a(x, g_a, wa, gq_a, gk_a, b_pad)
    o_a, z_a, lse_a, woa_g, wkv_g, w_in_b, wob_g = _fox_fwd(
        qn, kn, vb, proj, crow, cbc,
        rider=[("gather_rows", woa_s), ("gather_rows", wkv_s), ("gather_cols", wib_s), ("gather_rows", wob_s)])
    w_out_a, w_kv, w_out_b = woa_g.reshape(D, D), wkv_g.reshape(D, 512), wob_g.reshape(D, D)
    h1, u_kv, u_b, kv, pb, qb, ksh, vsh = _head_b(x, z_a, w_out_a, g_kv, g_b, w_kv, w_in_b, gq_b, gk_b, cos2, sin2)
    sinks1 = sinks.reshape(N_HEADS)
    o_b, z_b, lse_b = _swa_fwd(qb, ksh, vsh, pb, sinks1)
    dy, lsum = _out_b_loss(z_b, w_out_b, h1, tgt)
    dw_out_b = _mm(z_b, dy, "tn", 512, 512, S, out_dtype=BF16, name="mm_dw_out_b")
    dz_b = _mm(dy, w_out_b, "nt", 1024, 512, D, name="mm_dz_b")
    dpb, dka, dkb, dva, dvb, dsinks, dgq_b = _swa_bwd(qb, ksh, vsh, dz_b, o_b, lse_b, pb, sinks1, gq_b, cos2, sin2)
    dkv, dgk_b = _prep_kv_bwd(dka, dkb, dva, dvb, kv, gk_b, cos2, sin2)
    dw_in_b = _mm(u_b, dpb, "tn", 512, 512, S, out_dtype=BF16, name="mm_dw_in_b")
    dw_kv = _mm(u_kv, dkv, "tn", 512, 512, S, out_dtype=BF16, name="mm_dw_kv")
    dh1, dg_b, dg_kv = _du_b_rms_bwd(dpb, w_in_b, dkv, w_kv, h1, g_b, g_kv, dy)
    dw_out_a = _mm(z_a, dh1, "tn", 512, 512, S, out_dtype=BF16, name="mm_dw_out_a")
    do_a, dgate_a, delta_a = _fox_bwd_pre(dh1, w_out_a, proj, o_a)
    dk_a, dv_a, dcs, dq_a, drow, r_wob, r_wib, r_wkv, r_woa = _fox_bwd(
        qn, kn, vb, do_a, lse_a, delta_a, crow, cbc,
        rider=[("a2a_rows", dw_out_b), ("a2a_cols", dw_in_b), ("a2a_rows", dw_kv), ("a2a_rows", dw_out_a)])
    dproj, dgq_a, dgk_a, db_f = _prep_a_bwd(dq_a, dk_a, dv_a, dgate_a, drow, dcs, proj, b_pad, gq_a, gk_a)
    dwa = _mm(u_a, dproj, "tn", 1024, 256, S, out_dtype=BF16, name="mm_dw_in_a")
    partial = _reshard_pair_reduce(dwa)
    started = _chip_exchange_start(partial)
    dx, dg_a = _du_a_rms_bwd(dproj, wa, x, g_a, dh1, after=started[-1])
    others = adamw_others(dict(w_out_a=r_woa, w_kv=r_wkv, w_in_b=r_wib, w_out_b=r_wob), dg_a)
    slab = _pack_small(dg_a, dg_kv, dg_b, db_f, dgq_a, dgk_a, dgk_b, dgq_b, dsinks, lsum)
    partial, landed = _chip_exchange_wait(started, [res[0] for res in others.values()] + [slab])
    slab_started = _gather_slab_start(slab, landed)
    return dx, (slab_started[-1], partial), others, slab_started


def kernel(x, positions, norm_a_g, w_in_a, b_forget, qnorm_a_g, knorm_a_g, w_out_a, kv_norm_g, w_kv, knorm_b_g, norm_b_g, w_in_b, qnorm_b_g, sinks, w_out_b, loss_target, m_norm_a_g, m_w_in_a, m_b_forget, m_qnorm_a_g, m_knorm_a_g, m_w_out_a, m_kv_norm_g, m_w_kv, m_knorm_b_g, m_norm_b_g, m_w_in_b, m_qnorm_b_g, m_sinks, m_w_out_b, v_norm_a_g, v_w_in_a, v_b_forget, v_qnorm_a_g, v_knorm_a_g, v_w_out_a, v_kv_norm_g, v_w_kv, v_knorm_b_g, v_norm_b_g, v_w_in_b, v_qnorm_b_g, v_sinks, v_w_out_b):
    wa_g, ga_g, woa_s, wkv_s, wib_s, wob_s = _gather_first(w_in_a, w_out_a, w_kv, w_in_b, w_out_b, norm_a_g)
    state = dict(w_in_a=(w_in_a, m_w_in_a, v_w_in_a), w_out_a=(w_out_a, m_w_out_a, v_w_out_a),
                 w_kv=(w_kv, m_w_kv, v_w_kv), w_in_b=(w_in_b, m_w_in_b, v_w_in_b),
                 w_out_b=(w_out_b, m_w_out_b, v_w_out_b))

    def adamw_others(landed, after):
        return {n: _sum_adamw(r, *state[n], "adamw_" + n, after=after) for n, r in landed.items()}

    dx, r_wa, big, slab_started = _local_step(
        x[0], loss_target[0], positions, ga_g.reshape(1, D), _unshard_wa(wa_g), b_forget, qnorm_a_g, knorm_a_g,
        kv_norm_g.reshape(1, D), knorm_b_g.reshape(1, HD), norm_b_g, qnorm_b_g, sinks, woa_s, wkv_s, wib_s, wob_s,
        adamw_others)
    big["w_in_a"], done = _sum_adamw_entry_view(*r_wa, *state["w_in_a"], "adamw_w_in_a")
    slab_g = _gather_slab_wait(slab_started, [done])

    r2 = lambda a: a.reshape(1, -1)
    small_w = dict(norm_a_g=norm_a_g, b_forget=b_forget, qnorm_a_g=qnorm_a_g, knorm_a_g=knorm_a_g,
                   kv_norm_g=kv_norm_g, knorm_b_g=knorm_b_g, norm_b_g=norm_b_g, qnorm_b_g=qnorm_b_g, sinks=sinks)
    small_m = dict(norm_a_g=m_norm_a_g, b_forget=m_b_forget, qnorm_a_g=m_qnorm_a_g, knorm_a_g=m_knorm_a_g,
                   kv_norm_g=m_kv_norm_g, knorm_b_g=m_knorm_b_g, norm_b_g=m_norm_b_g, qnorm_b_g=m_qnorm_b_g,
                   sinks=m_sinks)
    small_v = dict(norm_a_g=v_norm_a_g, b_forget=v_b_forget, qnorm_a_g=v_qnorm_a_g, knorm_a_g=v_knorm_a_g,
                   kv_norm_g=v_kv_norm_g, knorm_b_g=v_knorm_b_g, norm_b_g=v_norm_b_g, qnorm_b_g=v_qnorm_b_g,
                   sinks=v_sinks)
    res = _small_adamw(slab_g, [r2(small_w[n]) for n in SMALL], [r2(small_m[n]) for n in SMALL],
                       [r2(small_v[n]) for n in SMALL])
    k = len(SMALL)
    small = {n: [res[q * k + p].reshape(small_w[n].shape) for q in range(4)] for p, n in enumerate(SMALL)}
    loss = res[4 * k][0, 0]

    order = ["norm_a_g", "w_in_a", "b_forget", "qnorm_a_g", "knorm_a_g", "w_out_a", "kv_norm_g", "w_kv",
             "knorm_b_g", "norm_b_g", "w_in_b", "qnorm_b_g", "sinks", "w_out_b"]

    def leaf(n, q):
        return big[n][q] if n in big else small[n][q]

    outs = [loss, dx[None]]
    for q in range(4):
        outs.extend(leaf(n, q) for n in order)
    return tuple(outs)
```

```python
import jax
import jax.numpy as jnp
from jax import lax
from jax.experimental import pallas as pl
from jax.experimental.pallas import tpu as pltpu

F32, BF16 = jnp.float32, jnp.bfloat16

S = 2048
D = 1024
HD = 64
N_HEADS = 16
N_DEV = 8
NA = 4352
GOFF = 3072
FOFF = 4096
RAW_F = 3072
RAW_G = RAW_F + N_HEADS
NA_RAW = 4112
EPS = 1e-6
QSCALE = 0.125
ROPE_THETA = 500000.0
ROT = 16
WIN = 128
TQ = 256
TK = 256
KS = TQ // 2
HPS = 8
HW = HPS * HD
TM = 256
RT = 256
RB = 512
CB = 256
LANES = 128

ADAM_LR, ADAM_B1, ADAM_B2, ADAM_EPS, ADAM_WD, ADAM_STEP = 0.001, 0.9, 0.999, 1e-08, 0.01, 10

VMEM_LIMIT = 56 * 1024 * 1024


def _params():
    return pltpu.CompilerParams(vmem_limit_bytes=VMEM_LIMIT)


def _sds(shape, dtype):
    return jax.ShapeDtypeStruct(shape, dtype)


def _dot_nt(a, b):
    return lax.dot_general(a, b, (((1,), (1,)), ((), ())), preferred_element_type=F32)


def _dot_tn(a, b):
    return lax.dot_general(a, b, (((0,), (0,)), ((), ())), preferred_element_type=F32)


def _dot_nn(a, b):
    return lax.dot_general(a, b, (((1,), (0,)), ((), ())), preferred_element_type=F32)


def _sigmoid(g):
    return 1.0 / (1.0 + jnp.exp(-g))


def _lane_iota(shape):
    return lax.broadcasted_iota(jnp.int32, shape, len(shape) - 1)


def _flips(kind):
    return (2, 4, 6) if kind == "a2a_chips" else tuple(range(1, N_DEV))


def _send_view(kind, ref, dev):
    if kind in ("gather_rows", "gather_cols"):
        return ref
    if kind == "a2a_slots":
        return ref.at[dev]
    if kind == "a2a_chips":
        return ref.at[dev >> 1]
    if kind == "a2a_rows":
        rows = ref.shape[0] // N_DEV
        return ref.at[pl.ds(pl.multiple_of(dev * rows, rows), rows)]
    cols = ref.shape[1] // N_DEV
    return ref.at[:, pl.ds(pl.multiple_of(dev * cols, cols), cols)]


def _land_view(kind, ref, dev):
    if kind == "gather_cols":
        cols = ref.shape[1] // N_DEV
        return ref.at[:, pl.ds(pl.multiple_of(dev * cols, cols), cols)]
    if kind == "a2a_chips":
        return ref.at[dev >> 1]
    return ref.at[dev]


def _landing_sds(kind, arr):
    if kind == "gather_rows":
        return _sds((N_DEV,) + arr.shape, arr.dtype)
    if kind == "gather_cols":
        return _sds((arr.shape[0], N_DEV * arr.shape[1]), arr.dtype)
    if kind == "a2a_rows":
        return _sds((N_DEV, arr.shape[0] // N_DEV, arr.shape[1]), arr.dtype)
    if kind == "a2a_cols":
        return _sds((N_DEV, arr.shape[0], arr.shape[1] // N_DEV), arr.dtype)
    return _sds(arr.shape, arr.dtype)


def _exchange_sems(n_parts):
    n = n_parts * (N_DEV - 1)
    return [pltpu.SemaphoreType.DMA((n,)), pltpu.SemaphoreType.DMA((n,)), pltpu.SemaphoreType.DMA((n_parts,))]


def _exchange_ops(kinds, srcs, dsts, sems, start, wait):
    send_sems, recv_sems, local_sems = sems
    x, y, c = lax.axis_index("x"), lax.axis_index("y"), lax.axis_index("c")
    me = 4 * x + 2 * y + c

    def local(a):
        return pltpu.make_async_copy(_send_view(kinds[a], srcs[a], me), _land_view(kinds[a], dsts[a], me),
                                     local_sems.at[a])

    def remote(a, k, landing_dev):
        peer = (x ^ ((k >> 2) & 1), y ^ ((k >> 1) & 1), c ^ (k & 1))
        sem = a * (N_DEV - 1) + k - 1
        return pltpu.make_async_remote_copy(
            src_ref=_send_view(kinds[a], srcs[a], me ^ k), dst_ref=_land_view(kinds[a], dsts[a], landing_dev),
            send_sem=send_sems.at[sem], recv_sem=recv_sems.at[sem], device_id=peer,
            device_id_type=pl.DeviceIdType.MESH)

    pairs = [(a, k) for k in range(1, N_DEV) for a in range(len(kinds)) if k in _flips(kinds[a])]
    if start:
        for a in range(len(kinds)):
            local(a).start()
        for a, k in pairs:
            remote(a, k, me).start()
    if wait:
        for a, k in pairs:
            remote(a, k, me ^ k).wait_recv()
            remote(a, k, me).wait_send()
        for a in range(len(kinds)):
            local(a).wait()


def _gather_two_level(srcs, landing_of, sems, meanwhile=None):
    send_sems, recv_sems, local_sems = sems
    x, y, c = lax.axis_index("x"), lax.axis_index("y"), lax.axis_index("c")
    me, sibling = (x, y, c), (x, y, 1 - c)
    chips = [(1 - x, y), (x, 1 - y), (1 - x, 1 - y)]

    def slot(a, dev):
        return landing_of[a](4 * dev[0] + 2 * dev[1] + dev[2])

    def copy(a, k, block, to, src=None):
        return pltpu.make_async_remote_copy(
            src_ref=slot(a, block) if src is None else src, dst_ref=slot(a, block),
            send_sem=send_sems.at[a * (N_DEV - 1) + k], recv_sem=recv_sems.at[a * (N_DEV - 1) + k],
            device_id=to, device_id_type=pl.DeviceIdType.MESH)

    parts = range(len(srcs))
    mine = [pltpu.make_async_copy(srcs[a], slot(a, me), local_sems.at[a]) for a in parts]
    first = [copy(a, 1 + j, me, (*chip, c), src=srcs[a]) for a in parts for j, chip in enumerate(chips)]
    first += [copy(a, 0, me, sibling, src=srcs[a]) for a in parts]
    for cp in first + mine:
        cp.start()
    if meanwhile is not None:
        meanwhile()
    passed = []
    for a in parts:
        for j, chip in enumerate(chips):
            copy(a, 1 + j, (*chip, c), me).wait_recv()
            fwd = copy(a, 4 + j, (*chip, c), sibling)
            fwd.start()
            passed.append(fwd)
    for a in parts:
        copy(a, 0, sibling, me).wait_recv()
        for j, chip in enumerate(chips):
            copy(a, 4 + j, (*chip, 1 - c), me).wait_recv()
    for cp in first + passed:
        cp.wait_send()
    for cp in mine:
        cp.wait()


def _call(body, *, name, args, in_specs, out_specs, out_shape, grid=(), scratch_shapes=(), aliases=None, rider=()):
    n_in, n_out, n_scr, n_r = len(in_specs), len(out_specs), len(scratch_shapes), len(rider)
    kinds = [kind for kind, _ in rider]

    def kernel_body(*refs):
        c_in, r_in = refs[:n_in], refs[n_in:n_in + n_r]
        c_out = refs[n_in + n_r:n_in + n_r + n_out]
        r_out = refs[n_in + n_r + n_out:n_in + 2 * n_r + n_out]
        rest = refs[n_in + 2 * n_r + n_out:]
        c_scr, sems = rest[:n_scr], rest[n_scr:]
        if n_r:
            assert grid, "a rider needs a gridded call"
            ids = [pl.program_id(ax) for ax in range(len(grid))]
            first, last = ids[0] == 0, ids[0] == grid[0] - 1
            for pid, size in zip(ids[1:], grid[1:]):
                first = first & (pid == 0)
                last = last & (pid == size - 1)
            pl.when(first)(lambda: _exchange_ops(kinds, r_in, r_out, sems, True, False))
        body(*c_in, *c_out, *c_scr)
        if n_r:
            pl.when(last)(lambda: _exchange_ops(kinds, r_in, r_out, sems, False, True))

    anyspec = pl.BlockSpec(memory_space=pl.ANY)
    params = pltpu.CompilerParams(vmem_limit_bytes=VMEM_LIMIT, has_side_effects=bool(n_r))
    outs = pl.pallas_call(
        kernel_body, name=name, grid=grid, in_specs=list(in_specs) + [anyspec] * n_r,
        out_specs=list(out_specs) + [anyspec] * n_r,
        out_shape=list(out_shape) + [_landing_sds(kind, arr) for kind, arr in rider],
        scratch_shapes=list(scratch_shapes) + (_exchange_sems(n_r) if n_r else []),
        input_output_aliases=aliases or {}, compiler_params=params)(*args, *[arr for _, arr in rider])
    return list(outs)


def _mm(a, b, mode, tm, tn, tk, out_dtype=F32, add=None, name="mm", rider=()):
    if mode == "nn":
        (m, k), n = a.shape, b.shape[1]
        a_spec = pl.BlockSpec((tm, tk), lambda i, j, kk: (i, kk))
        b_spec = pl.BlockSpec((tk, tn), lambda i, j, kk: (kk, j))
        dot = _dot_nn
    elif mode == "nt":
        (m, k), n = a.shape, b.shape[0]
        a_spec = pl.BlockSpec((tm, tk), lambda i, j, kk: (i, kk))
        b_spec = pl.BlockSpec((tn, tk), lambda i, j, kk: (j, kk))
        dot = _dot_nt
    else:
        (k, m), n = a.shape, b.shape[1]
        a_spec = pl.BlockSpec((tk, tm), lambda i, j, kk: (kk, i))
        b_spec = pl.BlockSpec((tk, tn), lambda i, j, kk: (kk, j))
        dot = _dot_tn
    assert m % tm == 0 and n % tn == 0 and k % tk == 0, (m, n, k, tm, tn, tk)
    nk = k // tk
    has_add = add is not None

    def body(*refs):
        if has_add:
            a_ref, b_ref, add_ref, o_ref, acc = refs
        else:
            a_ref, b_ref, o_ref, acc = refs
        p = dot(a_ref[...].astype(BF16), b_ref[...].astype(BF16))

        def finish(total):
            if has_add:
                total = add_ref[...] + total
            o_ref[...] = total.astype(out_dtype)

        if nk == 1:
            finish(p)
        else:
            kk = pl.program_id(2)

            @pl.when(kk == 0)
            def _():
                acc[...] = p

            @pl.when(kk > 0)
            def _():
                acc[...] += p

            @pl.when(kk == nk - 1)
            def _():
                finish(acc[...])

    in_specs = [a_spec, b_spec]
    args = [a, b]
    if has_add:
        in_specs.append(pl.BlockSpec((tm, tn), lambda i, j, kk: (i, j)))
        args.append(add)
    acc_shape = (tm, tn) if nk > 1 else (8, LANES)
    outs = _call(body, name=name, args=args, grid=(m // tm, n // tn, nk), in_specs=in_specs,
                 out_specs=[pl.BlockSpec((tm, tn), lambda i, j, kk: (i, j))], out_shape=[_sds((m, n), out_dtype)],
                 scratch_shapes=[pltpu.VMEM(acc_shape, F32)], rider=rider)
    return outs if rider else outs[0]


def _rms_rinv(x):
    return lax.rsqrt(jnp.mean(x * x, axis=-1, keepdims=True) + EPS)


def _rms_bwd_core(du, x, g):
    r = _rms_rinv(x)
    dug = du * g
    dx = r * (dug - x * ((r * r) * jnp.mean(dug * x, axis=-1, keepdims=True)))
    dg = jnp.sum(du * (x * r), axis=0, keepdims=True)
    return dx, dg


def _half_ones():
    r = lax.broadcasted_iota(jnp.int32, (LANES, LANES), 0)
    c = lax.broadcasted_iota(jnp.int32, (LANES, LANES), 1)
    return ((r < HD) == (c < HD)).astype(BF16)


def _half_sum(v, lo_half):
    if lo_half.dtype == jnp.bool_:
        s0 = jnp.sum(jnp.where(lo_half, v, 0.0), axis=-1, keepdims=True)
        s1 = jnp.sum(jnp.where(lo_half, 0.0, v), axis=-1, keepdims=True)
        return jnp.where(lo_half, s0, s1)
    hi = v.astype(BF16)
    lo = (v - hi.astype(F32)).astype(BF16)
    return _dot_nn(hi, lo_half) + _dot_nn(lo, lo_half)


def _head_rinv(x, lo_half):
    return lax.rsqrt(_half_sum(x * x, lo_half) * (1.0 / HD) + EPS)


def _head_norm_bwd(dn, x, g, lo_half):
    r = _head_rinv(x, lo_half)
    dng = dn * g
    dx = r * (dng - x * ((r * r) * (_half_sum(dng * x, lo_half) * (1.0 / HD))))
    dg = jnp.sum(dn * (x * r), axis=0, keepdims=True)
    return dx, dg


def _rope_swap(x, lane):
    l64 = lane & (HD - 1)
    return jnp.where(l64 < ROT // 2, pltpu.roll(x, LANES - ROT // 2, 1), pltpu.roll(x, ROT // 2, 1))


def _rope_fwd(x, cos, sin, lane):
    return x * cos + _rope_swap(x, lane) * sin


def _rope_bwd(dy, cos, sin, lane):
    return dy * cos + jnp.where((lane & (HD - 1)) < ROT, _rope_swap(dy * sin, lane), 0.0)


def _g2(g_ref):
    g = g_ref[...]
    return jnp.concatenate([g, g], axis=-1)


def _pairs(width):
    return [slice(LANES * c, LANES * (c + 1)) for c in range(width // LANES)]


def _pick_lane(block, lane, idx):
    return jnp.sum(jnp.where(lane == idx, block, 0.0), axis=-1, keepdims=True)


def _head_a(x, g, wa, gq, gk, b_pad):
    assert RT == TQ
    def body(x_ref, g_ref, w_ref, gq_ref, gk_ref, b_ref, u_ref, p_ref, qo_ref, ko_ref, vo_ref, c_ref, cbc_ref, carry):
        @pl.when(pl.program_id(0) == 0)
        def _():
            carry[...] = jnp.zeros_like(carry)

        xv = x_ref[...]
        u = ((xv * _rms_rinv(xv)) * g_ref[...]).astype(BF16)
        u_ref[...] = u
        for lo in range(0, NA, D):
            hi = min(lo + D, NA)
            p_ref[:, lo:hi] = _dot_nn(u, w_ref[:, lo:hi])
        lane = _lane_iota((RT, LANES))
        lo_half = lane < HD
        gq2, gk2 = _g2(gq_ref), _g2(gk_ref)
        for c in _pairs(D):
            q = p_ref[:, c]
            k = p_ref[:, D + c.start:D + c.stop]
            qo_ref[:, c] = (((q * _head_rinv(q, lo_half)) * gq2) * QSCALE).astype(BF16)
            ko_ref[:, c] = ((k * _head_rinv(k, lo_half)) * gk2).astype(BF16)
        vo_ref[...] = p_ref[:, 2 * D:3 * D].astype(BF16)

        z = p_ref[:, FOFF:FOFF + LANES] + b_ref[...]
        logf = jnp.minimum(z, 0.0) - jnp.log1p(jnp.exp(-jnp.abs(z)))
        r = lax.broadcasted_iota(jnp.int32, (RT, RT), 0)
        cc = lax.broadcasted_iota(jnp.int32, (RT, RT), 1)
        tri = (r >= cc).astype(F32)
        loc = jnp.dot(tri, logf, precision=lax.Precision.HIGHEST, preferred_element_type=F32) + carry[0:1, :]
        c_ref[:, 0, :] = loc.T[:N_HEADS]
        carry[0:1, :] = loc[RT - 1:RT, :]
        for h in range(N_HEADS):
            cbc_ref[:, LANES * h:LANES * (h + 1)] = jnp.broadcast_to(_pick_lane(loc, lane, h), (RT, LANES))

    row = lambda width: pl.BlockSpec((RT, width), lambda i: (i, 0))
    whole = lambda arr: pl.BlockSpec(arr.shape, lambda i: (0,) * arr.ndim, pipeline_mode=pl.Buffered(1))
    return pl.pallas_call(
        body, name="head_a", grid=(S // RT,),
        in_specs=[row(D), whole(g), whole(wa), whole(gq), whole(gk), whole(b_pad)],
        out_specs=[row(D), row(NA), row(D), row(D), row(D),
                   pl.BlockSpec((N_HEADS, None, 1, TQ), lambda i: (0, i, 0, 0)), row(N_HEADS * LANES)],
        out_shape=[_sds((S, D), BF16), _sds((S, NA), F32)] + [_sds((S, D), BF16)] * 3
        + [_sds((N_HEADS, S // TQ, 1, TQ), F32), _sds((S, N_HEADS * LANES), F32)],
        scratch_shapes=[pltpu.VMEM((8, LANES), F32)], compiler_params=_params())(x, g, wa, gq, gk, b_pad)


def _key_le_query(offset, keys=TK):
    r = lax.broadcasted_iota(jnp.int32, (keys, TQ), 0)
    c = lax.broadcasted_iota(jnp.int32, (keys, TQ), 1)
    return (r + offset) <= c


def _widen(tile):
    return jnp.concatenate([tile] * (TQ // LANES), axis=1)


def _fox_fwd(qn, kn, vb, proj, crow, cbc, rider=()):
    nq = S // TQ

    def body(q_ref, k_ref, v_ref, g_ref, cq_ref, cbc_ref, o_ref, z_ref, lse_ref, st_s, pt_s):
        i = pl.program_id(1)
        qs = [q_ref[:, HD * hh:HD * (hh + 1)] for hh in range(HPS)]
        cqs = [cq_ref[hh, 0] for hh in range(HPS)]

        def scores(s, hh):
            off = pl.multiple_of(s * KS, KS)
            kj = k_ref[pl.ds(off, KS), HD * hh:HD * (hh + 1)]
            return (_dot_nt(kj, qs[hh]) + cqs[hh]) - _widen(cbc_ref[pl.ds(off, KS), LANES * hh:LANES * (hh + 1)])

        def values(s, hh, pt):
            off = pl.multiple_of(s * KS, KS)
            return _dot_tn(v_ref[pl.ds(off, KS), HD * hh:HD * (hh + 1)], pt)

        def step(s, slot, carries, mask=None, last=False):
            if not last:
                for hh in range(HPS):
                    st_s[1 - slot, hh] = scores(s + 1, hh)
            pvs = [values(jnp.maximum(s - 1, 0), hh, pt_s[1 - slot, hh]) for hh in range(HPS)]
            out = []
            for hh in range(HPS):
                m, l, acc = carries[hh]
                st = st_s[slot, hh]
                if mask is not None:
                    st = jnp.where(mask, st, -jnp.inf)
                m_new = jnp.maximum(m, jnp.max(st, axis=0, keepdims=True))
                pt = jnp.exp(st - m_new)
                alpha = jnp.exp(m - m_new)
                pt_s[slot, hh] = pt.astype(BF16)
                out.append((m_new, alpha * l + jnp.sum(pt, axis=0, keepdims=True), alpha * (acc + pvs[hh])))
            return tuple(out)

        for hh in range(HPS):
            st_s[0, hh] = scores(0, hh)
            pt_s[1, hh] = jnp.zeros((KS, TQ), BF16)
        one = (jnp.full((1, TQ), -jnp.inf, F32), jnp.zeros((1, TQ), F32), jnp.zeros((HD, TQ), F32))
        carries = lax.fori_loop(0, i, lambda t, cr: step(2 * t + 1, 1, step(2 * t, 0, cr)), (one,) * HPS)
        carries = step(2 * i, 0, carries, mask=_key_le_query(0, KS))
        carries = step(2 * i + 1, 1, carries, mask=_key_le_query(KS, KS), last=True)
        accs = []
        for hh in range(HPS):
            m, l, acc = carries[hh]
            acc = acc + values(2 * i + 1, hh, pt_s[1, hh])
            accs.append(acc / l)
            lse_ref[hh, 0] = m + jnp.log(l)
        o = jnp.concatenate(accs, axis=0).T
        o_ref[...] = o
        g = g_ref[...]
        z_ref[...] = (o * (g * _sigmoid(g))).astype(BF16)

    qblk = pl.BlockSpec((TQ, HW), lambda hp, i: (i, hp))
    full = pl.BlockSpec((S, HW), lambda hp, i: (0, hp))
    rows = pl.BlockSpec((HPS, 1, 1, TQ), lambda hp, i: (hp, i, 0, 0))
    return _call(
        body, name="fox_fwd", args=(qn, kn, vb, proj, crow, cbc), grid=(N_HEADS // HPS, nq),
        in_specs=[qblk, full, full,
                  pl.BlockSpec((TQ, HW), lambda hp, i: (i, GOFF // HW + hp)),
                  rows, pl.BlockSpec((S, HPS * LANES), lambda hp, i: (0, hp))],
        out_specs=[qblk, qblk, rows],
        out_shape=[_sds((S, D), F32), _sds((S, D), BF16), _sds((N_HEADS, nq, 1, TQ), F32)],
        scratch_shapes=[pltpu.VMEM((2, HPS, KS, TQ), F32), pltpu.VMEM((2, HPS, KS, TQ), BF16)], rider=rider)


def _fox_bwd_pre(dh, w_out, proj, o):
    nq, rows = S // TQ, 2 * TQ
    per = rows // TQ

    def body(dh_ref, w_ref, g_ref, o_ref, do_ref, dg_ref, delta_ref):
        g = g_ref[...]
        sg = _sigmoid(g)
        dzv = _dot_nt(dh_ref[...].astype(BF16), w_ref[...])
        ov = o_ref[...]
        do = dzv * (g * sg)
        dg_ref[...] = (dzv * ov * (sg * (1.0 + g * (1.0 - sg)))).astype(BF16)
        do_ref[...] = do.astype(BF16)
        prod_t = (do * ov).T
        for h in range(N_HEADS):
            for b in range(per):
                delta_ref[h, b] = jnp.sum(prod_t[HD * h:HD * (h + 1), TQ * b:TQ * (b + 1)], axis=0, keepdims=True)

    row = pl.BlockSpec((rows, D), lambda i: (i, 0))
    return pl.pallas_call(
        body, name="fox_bwd_pre", grid=(S // rows,),
        in_specs=[row, pl.BlockSpec(w_out.shape, lambda i: (0, 0), pipeline_mode=pl.Buffered(1)),
                  pl.BlockSpec((rows, D), lambda i: (i, GOFF // D)), row],
        out_specs=[row, row, pl.BlockSpec((N_HEADS, per, 1, TQ), lambda i: (0, i, 0, 0))],
        out_shape=[_sds((S, D), BF16), _sds((S, D), BF16), _sds((N_HEADS, nq, 1, TQ), F32)],
        compiler_params=_params())(dh, w_out, proj, o)


def _fox_bwd(qn, kn, vb, dob, lse, delta, crow, cbc, rider=()):
    nq, nkb = S // TQ, S // TK

    def body(q_ref, k_ref, v_ref, do_ref, lse_ref, del_ref, cq_ref, cbc_ref,
             dk_ref, dv_ref, dcs_ref, dq_ref, dr_ref, st_s, dp_s, pt_s, ds_s, dq_acc, dr_acc):
        j = pl.program_id(1)

        @pl.when(j == 0)
        def _():
            dq_acc[...] = jnp.zeros_like(dq_acc)
            dr_acc[...] = jnp.zeros_like(dr_acc)

        kjs = [k_ref[:, HD * hh:HD * (hh + 1)] for hh in range(HPS)]
        vjs = [v_ref[:, HD * hh:HD * (hh + 1)] for hh in range(HPS)]

        def rows_of(ref, u, hh):
            off = pl.multiple_of(u * TQ, TQ)
            return ref[pl.ds(off, TQ), HD * hh:HD * (hh + 1)]

        def products(u, hh):
            st = (_dot_nt(kjs[hh], rows_of(q_ref, u, hh)) + cq_ref[hh, u]) - _widen(
                cbc_ref[:, LANES * hh:LANES * (hh + 1)])
            return st, _dot_nt(vjs[hh], rows_of(do_ref, u, hh))

        def step(u, slot, carries, masked=False):
            nxt = jnp.minimum(u + 1, nq - 1)
            for hh in range(HPS):
                st_s[1 - slot, hh], dp_s[1 - slot, hh] = products(nxt, hh)
            prev = jnp.maximum(u - 1, 0)
            dvs = [_dot_nn(pt_s[1 - slot, hh], rows_of(do_ref, prev, hh)) for hh in range(HPS)]
            dks = [_dot_nn(ds_s[1 - slot, hh], rows_of(q_ref, prev, hh)) for hh in range(HPS)]
            for hh in range(HPS):
                dq_acc[hh, prev] += _dot_tn(kjs[hh], ds_s[1 - slot, hh])
            out = []
            for hh in range(HPS):
                dk, dv, dcs = carries[hh]
                st = st_s[slot, hh]
                if masked:
                    st = jnp.where(_key_le_query((j - u) * TQ), st, -jnp.inf)
                pt = jnp.exp(st - lse_ref[hh, u])
                dst = pt * (dp_s[slot, hh] - del_ref[hh, u])
                pt_s[slot, hh] = pt.astype(BF16)
                ds_s[slot, hh] = dst.astype(BF16)
                dr_acc[hh, u] += jnp.sum(dst, axis=0, keepdims=True)
                out.append((dk + dks[hh], dv + dvs[hh], dcs + (dst[:, :LANES] + dst[:, LANES:])))
            return tuple(out)

        t0 = j // 2
        for hh in range(HPS):
            st_s[0, hh], dp_s[0, hh] = products(2 * t0, hh)
            pt_s[1, hh] = jnp.zeros((TK, TQ), BF16)
            ds_s[1, hh] = jnp.zeros((TK, TQ), BF16)
        one = (jnp.zeros((TK, HD), F32), jnp.zeros((TK, HD), F32), jnp.zeros((TK, LANES), F32))
        carries = step(2 * t0 + 1, 1, step(2 * t0, 0, (one,) * HPS, masked=True), masked=True)
        carries = lax.fori_loop(t0 + 1, nq // 2, lambda t, cr: step(2 * t + 1, 1, step(2 * t, 0, cr)), carries)
        dks, dvs = [], []
        lane = _lane_iota((TK, LANES))
        dcs_all = jnp.zeros((TK, LANES), F32)
        for hh in range(HPS):
            dk, dv, dcs = carries[hh]
            dks.append(dk + _dot_nn(ds_s[1, hh], rows_of(q_ref, nq - 1, hh)))
            dvs.append(dv + _dot_nn(pt_s[1, hh], rows_of(do_ref, nq - 1, hh)))
            dq_acc[hh, nq - 1] += _dot_tn(kjs[hh], ds_s[1, hh])
            dcs_all = jnp.where(lane == HPS * pl.program_id(0) + hh, -jnp.sum(dcs, axis=-1, keepdims=True), dcs_all)
        dcs_ref[0] = dcs_all
        dk_ref[...] = jnp.concatenate(dks, axis=-1)
        dv_ref[...] = jnp.concatenate(dvs, axis=-1).astype(BF16)

        @pl.when(j == nkb - 1)
        def _():
            for i in range(nq):
                dq_ref[TQ * i:TQ * (i + 1), :] = jnp.concatenate([dq_acc[hh, i] for hh in range(HPS)], axis=0).T
            dr_ref[...] = dr_acc[...]

    kblk = pl.BlockSpec((TK, HW), lambda hp, j: (j, hp))
    full = pl.BlockSpec((S, HW), lambda hp, j: (0, hp))
    rows = pl.BlockSpec((HPS, nq, 1, TQ), lambda hp, j: (hp, 0, 0, 0))
    cblk = pl.BlockSpec((TK, HPS * LANES), lambda hp, j: (j, hp))
    return _call(
        body, name="fox_bwd", args=(qn, kn, vb, dob, lse, delta, crow, cbc), grid=(N_HEADS // HPS, nkb),
        in_specs=[full, kblk, kblk, full, rows, rows, rows, cblk],
        out_specs=[kblk, kblk, pl.BlockSpec((1, TK, LANES), lambda hp, j: (hp, j, 0)), full, rows],
        out_shape=[_sds((S, D), F32), _sds((S, D), BF16), _sds((N_HEADS // HPS, S, LANES), F32), _sds((S, D), F32),
                   _sds((N_HEADS, nq, 1, TQ), F32)],
        scratch_shapes=[pltpu.VMEM((2, HPS, TK, TQ), F32), pltpu.VMEM((2, HPS, TK, TQ), F32),
                        pltpu.VMEM((2, HPS, TK, TQ), BF16), pltpu.VMEM((2, HPS, TK, TQ), BF16),
                        pltpu.VMEM((HPS, nq, HD, TQ), F32), pltpu.VMEM((HPS, nq, 1, TQ), F32)], rider=rider)


def _prep_a_bwd(dq, dk, dv, dgate, drow, dcs, proj, b_pad, gq, gk):
    nt = S // TM

    def body(dq_ref, dk_ref, dv_ref, dgt_ref, dr_ref, dcs_ref, xq_ref, xk_ref, f_ref, b_ref, gq_ref, gk_ref,
             o_ref, dgq_ref, dgk_ref, db_ref, carry):
        @pl.when(pl.program_id(0) == 0)
        def _():
            carry[...] = jnp.zeros_like(carry)
            dgq_ref[...] = jnp.zeros_like(dgq_ref)
            dgk_ref[...] = jnp.zeros_like(dgk_ref)
            db_ref[...] = jnp.zeros_like(db_ref)

        lane = _lane_iota((TM, LANES))
        lo_half = _half_ones()
        gq2, gk2 = _g2(gq_ref), _g2(gk_ref)
        dgq, dgk = jnp.zeros((1, LANES), F32), jnp.zeros((1, LANES), F32)
        for c in _pairs(D):
            dxq, dg = _head_norm_bwd(dq_ref[:, c] * QSCALE, xq_ref[:, c], gq2, lo_half)
            o_ref[:, c] = dxq.astype(BF16)
            dgq = dgq + dg
            dxk, dg = _head_norm_bwd(dk_ref[:, c], xk_ref[:, c], gk2, lo_half)
            o_ref[:, D + c.start:D + c.stop] = dxk.astype(BF16)
            dgk = dgk + dg
        dgq_ref[...] += dgq
        dgk_ref[...] += dgk
        o_ref[:, 2 * D:3 * D] = dv_ref[...]
        o_ref[:, GOFF:GOFF + D] = dgt_ref[...]

        dc = jnp.concatenate([dr_ref[:, 0, :], jnp.zeros((LANES - N_HEADS, TM), F32)], axis=0).T
        for group in range(N_HEADS // HPS):
            dc = dc + dcs_ref[group]
        r = lax.broadcasted_iota(jnp.int32, (TM, TM), 0)
        c = lax.broadcasted_iota(jnp.int32, (TM, TM), 1)
        tri = (c >= r).astype(F32)
        dlogf = jnp.dot(tri, dc, precision=lax.Precision.HIGHEST, preferred_element_type=F32) + carry[0:1, :]
        carry[0:1, :] = dlogf[0:1, :]
        df = dlogf * (1.0 / (1.0 + jnp.exp(f_ref[...] + b_ref[...])))
        db_ref[...] += jnp.sum(df, axis=0, keepdims=True)
        o_ref[:, FOFF:FOFF + LANES] = df.astype(BF16)
        o_ref[:, FOFF + LANES:NA] = jnp.zeros((TM, NA - FOFF - LANES), BF16)

    rev = lambda width, col: pl.BlockSpec((TM, width), lambda i: (nt - 1 - i, col))
    gspec = pl.BlockSpec((1, HD), lambda i: (0, 0))
    acc = pl.BlockSpec((1, LANES), lambda i: (0, 0))
    return pl.pallas_call(
        body, name="prep_a_bwd", grid=(nt,),
        in_specs=[rev(D, 0), rev(D, 0), rev(D, 0), rev(D, 0),
                  pl.BlockSpec((N_HEADS, None, 1, TM), lambda i: (0, nt - 1 - i, 0, 0)),
                  pl.BlockSpec((N_HEADS // HPS, TM, LANES), lambda i: (0, nt - 1 - i, 0)),
                  rev(D, 0), rev(D, 1), rev(LANES, FOFF // LANES), acc, gspec, gspec],
        out_specs=[rev(NA, 0), acc, acc, acc],
        out_shape=[_sds((S, NA), BF16)] + [_sds((1, LANES), F32)] * 3,
        scratch_shapes=[pltpu.VMEM((8, LANES), F32)],
        compiler_params=_params())(dq, dk, dv, dgate, drow, dcs, proj, proj, proj, b_pad, gq, gk)


def _head_b(x, z_a, w_out_a, g_kv, g_b, w_kv, w_in_b, gq, gk, cos2, sin2):
    nkv = w_kv.shape[1] // 2

    def body(x_ref, z_ref, wo_ref, gkv_ref, gb_ref, wkv_ref, wb_ref, gq_ref, gk_ref, c_ref, s_ref,
             h_ref, ukv_ref, ub_ref, kv_ref, pb_ref, qo_ref, ko_ref, vo_ref):
        xv = x_ref[...] + _dot_nn(z_ref[...], wo_ref[...])
        h_ref[...] = xv
        xn = xv * _rms_rinv(xv)
        ukv = (xn * gkv_ref[...]).astype(BF16)
        ub = (xn * gb_ref[...]).astype(BF16)
        ukv_ref[...] = ukv
        ub_ref[...] = ub
        kv_ref[...] = _dot_nn(ukv, wkv_ref[...])
        for lo in range(0, 2 * D, D):
            pb_ref[:, lo:lo + D] = _dot_nn(ub, wb_ref[:, lo:lo + D])
        lane = _lane_iota((RT, LANES))
        lo_half = lane < HD
        cos, sin = c_ref[...], s_ref[...]
        gq2, gk2 = _g2(gq_ref), _g2(gk_ref)
        for c in _pairs(D):
            q = pb_ref[:, c]
            qo_ref[:, c] = (_rope_fwd((q * _head_rinv(q, lo_half)) * gq2, cos, sin, lane) * QSCALE).astype(BF16)
        for c in _pairs(nkv):
            k = kv_ref[:, c]
            ko_ref[:, c] = _rope_fwd((k * _head_rinv(k, lo_half)) * gk2, cos, sin, lane).astype(BF16)
        vo_ref[...] = kv_ref[:, nkv:2 * nkv].astype(BF16)

    row = lambda width: pl.BlockSpec((RT, width), lambda i: (i, 0))
    whole = lambda arr: pl.BlockSpec(arr.shape, lambda i: (0,) * arr.ndim, pipeline_mode=pl.Buffered(1))
    return pl.pallas_call(
        body, name="head_b", grid=(S // RT,),
        in_specs=[row(D), row(D), whole(w_out_a), whole(g_kv), whole(g_b), whole(w_kv), whole(w_in_b), whole(gq),
                  whole(gk), row(LANES), row(LANES)],
        out_specs=[row(D), row(D), row(D), row(2 * nkv), row(2 * D), row(D), row(nkv), row(nkv)],
        out_shape=[_sds((S, D), F32), _sds((S, D), BF16), _sds((S, D), BF16), _sds((S, 2 * nkv), F32),
                   _sds((S, 2 * D), F32), _sds((S, D), BF16), _sds((S, nkv), BF16), _sds((S, nkv), BF16)],
        compiler_params=_params())(x, z_a, w_out_a, g_kv, g_b, w_kv, w_in_b, gq, gk, cos2, sin2)


N_KV, GRP = 4, 4


def _swa_mask(n):
    r = lax.broadcasted_iota(jnp.int32, (2 * WIN, GRP * WIN), 0)
    q = lax.broadcasted_iota(jnp.int32, (2 * WIN, GRP * WIN), 1) & (WIN - 1)
    return (r > q) & (r <= q + WIN) & ((r >= WIN) | (n > 0))


def _stack4(ref_or_val, base):
    return jnp.concatenate([ref_or_val[:, base + HD * g: base + HD * (g + 1)] for g in range(GRP)], axis=0)


def _unstack4(xt):
    return jnp.concatenate([xt[:, WIN * g:WIN * (g + 1)] for g in range(GRP)], axis=0).T


def _band(prev_ref, cur_ref, kh):
    return jnp.concatenate([prev_ref[:, HD * kh:HD * (kh + 1)], cur_ref[:, HD * kh:HD * (kh + 1)]], axis=0)


def _sink_row(s_ref, first):
    lane = _lane_iota((1, GRP * WIN))
    row = jnp.full((1, GRP * WIN), s_ref[first + GRP - 1], F32)
    for g in range(GRP - 2, -1, -1):
        row = jnp.where(lane < WIN * (g + 1), s_ref[first + g], row)
    return row


def _swa_fwd(qb, ksh, vsh, pb, sinks):
    nb = S // WIN

    def body(q_ref, kp_ref, kc_ref, vp_ref, vc_ref, g_ref, s_ref, o_ref, z_ref, lse_ref):
        n = pl.program_id(0)
        valid = _swa_mask(n)
        outs = []
        for kh in range(N_KV):
            kb, vb = _band(kp_ref, kc_ref, kh), _band(vp_ref, vc_ref, kh)
            st = jnp.where(valid, _dot_nt(kb, _stack4(q_ref, GRP * HD * kh)), -jnp.inf)
            sink = _sink_row(s_ref, GRP * kh)
            m = jnp.maximum(jnp.max(st, axis=0, keepdims=True), sink)
            pt = jnp.exp(st - m)
            l = jnp.sum(pt, axis=0, keepdims=True) + jnp.exp(sink - m)
            outs.append(_unstack4(_dot_tn(vb, pt.astype(BF16)) / l))
            lse = m + jnp.log(l)
            for g in range(GRP):
                lse_ref[GRP * kh + g, 0] = lse[:, WIN * g:WIN * (g + 1)]
        o = jnp.concatenate(outs, axis=-1)
        o_ref[...] = o
        g = g_ref[...]
        z_ref[...] = (o * (g * _sigmoid(g))).astype(BF16)

    row = pl.BlockSpec((WIN, D), lambda n: (n, 0))
    prev = pl.BlockSpec((WIN, N_KV * HD), lambda n: (jnp.maximum(n - 1, 0), 0))
    cur = pl.BlockSpec((WIN, N_KV * HD), lambda n: (n, 0))
    return pl.pallas_call(
        body, name="swa_fwd", grid=(nb,),
        in_specs=[row, prev, cur, prev, cur, pl.BlockSpec((WIN, D), lambda n: (n, 1)),
                  pl.BlockSpec(memory_space=pltpu.SMEM)],
        out_specs=[row, row, pl.BlockSpec((N_HEADS, 1, 1, WIN), lambda n: (0, n, 0, 0))],
        out_shape=[_sds((S, D), F32), _sds((S, D), BF16), _sds((N_HEADS, nb, 1, WIN), F32)],
        compiler_params=_params())(qb, ksh, ksh, vsh, vsh, pb, sinks)


def _swa_bwd(qb, ksh, vsh, dz, o, lse, pb, sinks, gq, cos2, sin2):
    nb = S // WIN

    def body(q_ref, kp_ref, kc_ref, vp_ref, vc_ref, dz_ref, o_ref, lse_ref, x_ref, g_ref, s_ref, gq_ref, c_ref, sn_ref,
             dpb_ref, dka_ref, dkb_ref, dva_ref, dvb_ref, dsink_ref, dgq_ref):
        n = pl.program_id(0)

        @pl.when(n == 0)
        def _():
            dsink_ref[...] = jnp.zeros_like(dsink_ref)
            dgq_ref[...] = jnp.zeros_like(dgq_ref)

        valid = _swa_mask(n)
        g = g_ref[...]
        sg = _sigmoid(g)
        dzv = dz_ref[...]
        ov = o_ref[...]
        do = dzv * (g * sg)
        dpb_ref[:, D:2 * D] = (dzv * ov * (sg * (1.0 + g * (1.0 - sg)))).astype(BF16)
        prod_t = (do * ov).T
        lane1 = _lane_iota((1, LANES))
        dqs, dkas, dkbs, dvas, dvbs = [], [], [], [], []
        dsink = jnp.zeros((1, LANES), F32)
        for kh in range(N_KV):
            kb, vb = _band(kp_ref, kc_ref, kh), _band(vp_ref, vc_ref, kh)
            base = GRP * HD * kh
            qs = _stack4(q_ref, base)
            dos = _stack4(do, base).astype(BF16)
            delta = jnp.concatenate(
                [jnp.sum(prod_t[base + HD * gg:base + HD * (gg + 1), :], axis=0, keepdims=True)
                 for gg in range(GRP)], axis=1)
            lse = jnp.concatenate([lse_ref[GRP * kh + gg, 0] for gg in range(GRP)], axis=1)
            st = jnp.where(valid, _dot_nt(kb, qs), -jnp.inf)
            pt = jnp.exp(st - lse)
            dst = pt * (_dot_nt(vb, dos) - delta)
            dsb = dst.astype(BF16)
            dqs.append(_unstack4(_dot_tn(kb, dsb)))
            dkband = _dot_nn(dsb, qs)
            dvband = _dot_nn(pt.astype(BF16), dos)
            dkbs.append(dkband[0:WIN, :])
            dkas.append(dkband[WIN:2 * WIN, :])
            dvbs.append(dvband[0:WIN, :])
            dvas.append(dvband[WIN:2 * WIN, :])
            ps_delta = jnp.exp(_sink_row(s_ref, GRP * kh) - lse) * delta
            for gg in range(GRP):
                val = jnp.sum(ps_delta[:, WIN * gg:WIN * (gg + 1)], axis=1, keepdims=True)
                dsink = dsink - jnp.where(lane1 == GRP * kh + gg, val, 0.0)
        dka_ref[...] = jnp.concatenate(dkas, axis=-1)
        dkb_ref[...] = jnp.concatenate(dkbs, axis=-1)
        dva_ref[...] = jnp.concatenate(dvas, axis=-1)
        dvb_ref[...] = jnp.concatenate(dvbs, axis=-1)
        dsink_ref[...] += dsink

        lane = _lane_iota((WIN, LANES))
        g2, cos, sin = _g2(gq_ref), c_ref[...], sn_ref[...]
        lo_half = _half_ones()
        dg_tot = jnp.zeros((1, LANES), F32)
        for kh in range(N_KV):
            for c in _pairs(GRP * HD):
                cols = slice(GRP * HD * kh + c.start, GRP * HD * kh + c.stop)
                dn = _rope_bwd(dqs[kh][:, c] * QSCALE, cos, sin, lane)
                dx, dg = _head_norm_bwd(dn, x_ref[:, cols], g2, lo_half)
                dpb_ref[:, cols] = dx.astype(BF16)
                dg_tot = dg_tot + dg
        dgq_ref[...] += dg_tot

    row = pl.BlockSpec((WIN, D), lambda n: (n, 0))
    prev = pl.BlockSpec((WIN, N_KV * HD), lambda n: (jnp.maximum(n - 1, 0), 0))
    cur = pl.BlockSpec((WIN, N_KV * HD), lambda n: (n, 0))
    acc = pl.BlockSpec((1, LANES), lambda n: (0, 0))
    tab = pl.BlockSpec((WIN, LANES), lambda n: (n, 0))
    return pl.pallas_call(
        body, name="swa_bwd", grid=(nb,),
        in_specs=[row, prev, cur, prev, cur, row, row, pl.BlockSpec((N_HEADS, 1, 1, WIN), lambda n: (0, n, 0, 0)),
                  row, pl.BlockSpec((WIN, D), lambda n: (n, 1)), pl.BlockSpec(memory_space=pltpu.SMEM),
                  pl.BlockSpec((1, HD), lambda n: (0, 0)), tab, tab],
        out_specs=[pl.BlockSpec((WIN, 2 * D), lambda n: (n, 0)), cur, cur, cur, cur, acc, acc],
        out_shape=[_sds((S, 2 * D), BF16)] + [_sds((S, 256), F32)] * 4 + [_sds((1, LANES), F32)] * 2,
        compiler_params=_params())(qb, ksh, ksh, vsh, vsh, dz, o, lse, pb, pb, sinks, gq, cos2, sin2)


def _prep_kv_bwd(dka, dkb, dva, dvb, kv, gk, cos2, sin2):
    nt = S // RB
    per = RB // WIN

    def shifted(cur_ref, nxt_ref, has_next):
        return jnp.concatenate([cur_ref[WIN:RB, :], jnp.where(has_next, nxt_ref[...], 0.0)], axis=0)

    def body(dka_ref, dkb_ref, dkn_ref, dva_ref, dvb_ref, dvn_ref, x_ref, g_ref, c_ref, s_ref, o_ref, dgk_ref):
        i, j = pl.program_id(0), pl.program_id(1)

        @pl.when((i == 0) & (j == 0))
        def _():
            dgk_ref[...] = jnp.zeros_like(dgk_ref)

        has_next = i < nt - 1

        @pl.when(j == 0)
        def _():
            lane = _lane_iota((RB, LANES))
            g2, cos, sin = _g2(g_ref), c_ref[...], s_ref[...]
            dy_all = dka_ref[...] + shifted(dkb_ref, dkn_ref, has_next)
            dg_tot = jnp.zeros((1, LANES), F32)
            lo_half = _half_ones()
            for c in _pairs(CB):
                dn = _rope_bwd(dy_all[:, c], cos, sin, lane)
                dx, dg = _head_norm_bwd(dn, x_ref[:, c], g2, lo_half)
                o_ref[:, c] = dx.astype(BF16)
                dg_tot = dg_tot + dg
            dgk_ref[...] += dg_tot

        @pl.when(j == 1)
        def _():
            o_ref[...] = (dva_ref[...] + shifted(dvb_ref, dvn_ref, has_next)).astype(BF16)

    cur = pl.BlockSpec((RB, CB), lambda i, j: (i, 0))
    nxt = pl.BlockSpec((WIN, CB), lambda i, j: (jnp.minimum(per * (i + 1), S // WIN - 1), 0))
    tab = pl.BlockSpec((RB, LANES), lambda i, j: (i, 0))
    return pl.pallas_call(
        body, name="prep_kv_bwd", grid=(nt, 2),
        in_specs=[cur, cur, nxt, cur, cur, nxt, cur, pl.BlockSpec((1, HD), lambda i, j: (0, 0)), tab, tab],
        out_specs=[pl.BlockSpec((RB, CB), lambda i, j: (i, j)), pl.BlockSpec((1, LANES), lambda i, j: (0, 0))],
        out_shape=[_sds((S, 2 * CB), BF16), _sds((1, LANES), F32)],
        compiler_params=_params())(dka, dkb, dkb, dva, dvb, dvb, kv, gk, cos2, sin2)


def _out_b_loss(z, w_out, h1, tgt):
    tm = 2 * RT

    def body(z_ref, w_ref, h_ref, t_ref, dy_ref, l_ref):
        @pl.when(pl.program_id(0) == 0)
        def _():
            l_ref[...] = jnp.zeros_like(l_ref)

        e = (h_ref[...] + _dot_nn(z_ref[...], w_ref[...])) - t_ref[...]
        dy_ref[...] = e * (1.0 / D)
        l_ref[...] += jnp.sum(jnp.sum(e * e, axis=-1, keepdims=True), axis=0, keepdims=True)

    row = pl.BlockSpec((tm, D), lambda i: (i, 0))
    return pl.pallas_call(
        body, name="out_b_loss", grid=(S // tm,),
        in_specs=[row, pl.BlockSpec(w_out.shape, lambda i: (0, 0), pipeline_mode=pl.Buffered(1)), row, row],
        out_specs=[row, pl.BlockSpec((1, LANES), lambda i: (0, 0))],
        out_shape=[_sds((S, D), F32), _sds((1, LANES), F32)], compiler_params=_params())(z, w_out, h1, tgt)


def _du_a_rms_bwd(dproj, wa, x, g, dres, after):
    tm, tk = 1024, NA // 2
    nk = NA // tk

    def body(a_ref, b_ref, x_ref, g_ref, dr_ref, after_ref, dx_ref, dg_ref, acc):
        i, kk = pl.program_id(0), pl.program_id(1)

        @pl.when((i == 0) & (kk == 0))
        def _():
            dg_ref[...] = jnp.zeros_like(dg_ref)

        p = _dot_nt(a_ref[...], b_ref[...])

        @pl.when(kk == 0)
        def _():
            acc[...] = p

        @pl.when(kk == nk - 1)
        def _():
            dx, dg = _rms_bwd_core(acc[...] + p, x_ref[...], g_ref[...])
            dx_ref[...] = dr_ref[...] + dx
            dg_ref[...] += dg

    assert nk == 2
    row = pl.BlockSpec((tm, D), lambda i, kk: (i, 0))
    vec = pl.BlockSpec((1, D), lambda i, kk: (0, 0))
    return pl.pallas_call(
        body, name="du_a_rms_bwd", grid=(S // tm, nk),
        in_specs=[pl.BlockSpec((tm, tk), lambda i, kk: (i, kk)), pl.BlockSpec((D, tk), lambda i, kk: (0, kk)),
                  row, vec, row, pl.BlockSpec(after.shape, lambda i, kk: (0, 0))],
        out_specs=[row, vec], out_shape=[_sds((S, D), F32), _sds((1, D), F32)],
        scratch_shapes=[pltpu.VMEM((tm, D), F32)], compiler_params=_params())(dproj, wa, x, g, dres, after)


def _du_b_rms_bwd(dpb, w_in_b, dkv, w_kv, h1, g_b, g_kv, dy):
    tm = 2 * RT

    def body(ab_ref, wb_ref, akv_ref, wkv_ref, x_ref, gb_ref, gkv_ref, dy_ref, dh_ref, dgb_ref, dgkv_ref):
        @pl.when(pl.program_id(0) == 0)
        def _():
            dgb_ref[...] = jnp.zeros_like(dgb_ref)
            dgkv_ref[...] = jnp.zeros_like(dgkv_ref)

        x = x_ref[...]
        dx1, dg1 = _rms_bwd_core(_dot_nt(ab_ref[...], wb_ref[...]), x, gb_ref[...])
        dx2, dg2 = _rms_bwd_core(_dot_nt(akv_ref[...], wkv_ref[...]), x, gkv_ref[...])
        dh_ref[...] = dy_ref[...] + dx1 + dx2
        dgb_ref[...] += dg1
        dgkv_ref[...] += dg2

    row = lambda width: pl.BlockSpec((tm, width), lambda i: (i, 0))
    whole = lambda arr: pl.BlockSpec(arr.shape, lambda i: (0, 0), pipeline_mode=pl.Buffered(1))
    vec = pl.BlockSpec((1, D), lambda i: (0, 0))
    return pl.pallas_call(
        body, name="du_b_rms_bwd", grid=(S // tm,),
        in_specs=[row(dpb.shape[1]), whole(w_in_b), row(dkv.shape[1]), whole(w_kv), row(D), vec, vec, row(D)],
        out_specs=[row(D), vec, vec], out_shape=[_sds((S, D), F32), _sds((1, D), F32), _sds((1, D), F32)],
        compiler_params=_params())(dpb, w_in_b, dkv, w_kv, h1, g_b, g_kv, dy)


GATHER_CHUNKS = 4


def _gather_first(w_in_a, w_out_a, w_kv, w_in_b, w_out_b, norm_a_g):
    rows, cols = w_in_a.shape[-2:]
    groups = rows // LANES
    cols_pad = -(-cols // LANES) * LANES

    def body(wia_ref, woa_ref, wkv_ref, wib_ref, wob_ref, ga_ref,
             wa_g, ga_g, woa_s, wkv_s, wib_s, wob_s, wa_s, st_a, st_oa, st_kv, st_ib, st_ob, plane_t, load_sems, *sems):
        sources = [wia_ref, woa_ref.at[0], wkv_ref, wib_ref.at[0], wob_ref.at[0]]
        stages = [st_a, st_oa, st_kv, st_ib, st_ob]
        loads = [pltpu.make_async_copy(src, dst, load_sems.at[i]) for i, (src, dst) in enumerate(zip(sources, stages))]
        for cp in loads:
            cp.start()
        loads[0].wait()
        plane_t[cols_pad - LANES:, :] = jnp.zeros((LANES, LANES), F32)
        for j in range(groups):
            for c0 in range(0, cols, 64):
                n = min(64, cols - c0)
                plane_t[c0:c0 + n, :] = st_a[pl.ds(c0 * groups + j, n, stride=groups), :]
            for c0 in range(0, cols, LANES):
                n = min(LANES, cols - c0)
                wa_s[j * LANES:(j + 1) * LANES, c0:c0 + n] = plane_t[c0:c0 + LANES, :].T[:, :n].astype(BF16)

        def cast_the_rest():
            for cp, stage, out in zip(loads[1:], stages[1:], [woa_s, wkv_s, wib_s, wob_s]):
                cp.wait()
                out[...] = stage[...].astype(BF16)

        chunks = [pl.ds(r0, rows // GATHER_CHUNKS) for r0 in range(0, rows, rows // GATHER_CHUNKS)]
        _gather_two_level(
            [wa_s.at[rc] for rc in chunks] + [ga_ref],
            [lambda dev, rc=rc: wa_g.at[dev, rc] for rc in chunks] + [lambda dev: ga_g.at[dev]],
            sems, meanwhile=cast_the_rest)

    vmem = pl.BlockSpec(memory_space=pltpu.VMEM)
    anyspec = pl.BlockSpec(memory_space=pl.ANY)
    shard = lambda w: _sds(w.shape[-2:], BF16)
    stage = lambda w: pltpu.VMEM(w.shape[-2:], F32)
    return pl.pallas_call(
        body, name="gather_first", in_specs=[anyspec] * 5 + [vmem],
        out_specs=[anyspec, anyspec, vmem, vmem, vmem, vmem],
        out_shape=[_sds((N_DEV,) + w_in_a.shape[-2:], BF16), _sds((N_DEV,) + norm_a_g.shape, F32),
                   shard(w_out_a), shard(w_kv), shard(w_in_b), shard(w_out_b)],
        scratch_shapes=[pltpu.VMEM(w_in_a.shape[-2:], BF16), pltpu.VMEM((cols * groups, LANES), F32), stage(w_out_a),
                        stage(w_kv), stage(w_in_b), stage(w_out_b), pltpu.VMEM((cols_pad, LANES), F32),
                        pltpu.SemaphoreType.DMA((5,))] + _exchange_sems(GATHER_CHUNKS + 1),
        compiler_params=pltpu.CompilerParams(vmem_limit_bytes=VMEM_LIMIT, has_side_effects=True))(
            _entry_view(w_in_a).reshape(cols * groups, LANES), w_out_a, w_kv, w_in_b, w_out_b, norm_a_g)


def _padded_col(c):
    if c < RAW_F:
        return c
    return FOFF + (c - RAW_F) if c < RAW_G else GOFF + (c - RAW_G)


def _shard_pieces():
    width = NA_RAW // N_DEV
    pieces = []
    for d in range(N_DEV):
        cuts = [width * d] + [c for c in (RAW_F, RAW_G) if width * d < c < width * (d + 1)] + [width * (d + 1)]
        for lo, hi in zip(cuts[:-1], cuts[1:]):
            pieces.append((d, lo - width * d, _padded_col(lo), hi - lo))
    return pieces


def _unshard_wa(wa_g):
    def body(w_ref, o_ref):
        o_ref[:, FOFF + N_HEADS:NA] = jnp.zeros((TM, NA - FOFF - N_HEADS), BF16)
        for d, src, dst, width in _shard_pieces():
            o_ref[:, dst:dst + width] = w_ref[d, :, src:src + width]

    return pl.pallas_call(
        body, name="unshard_wa", grid=(D // TM,),
        in_specs=[pl.BlockSpec((N_DEV, TM, NA_RAW // N_DEV), lambda i: (0, i, 0))],
        out_specs=pl.BlockSpec((TM, NA), lambda i: (i, 0)), out_shape=_sds((D, NA), BF16),
        compiler_params=_params())(wa_g)


def _reshard_pair_reduce(dwa):
    n_chip, nt = N_DEV // 2, D // TM
    width = NA_RAW // N_DEV

    def body(g_ref, o_ref, slots_v, sib_v, send_sems, recv_sems):
        i = pl.program_id(0)
        x, y, c = lax.axis_index("x"), lax.axis_index("y"), lax.axis_index("c")

        def tile_rows(tile):
            return pl.ds(tile * TM if isinstance(tile, int) else pl.multiple_of(tile * TM, TM), TM)

        def give(j, tile):
            return pltpu.make_async_remote_copy(
                src_ref=slots_v.at[2 * j + 1 - c, tile_rows(tile)], dst_ref=sib_v.at[j, tile_rows(tile)],
                send_sem=send_sems.at[j, tile], recv_sem=recv_sems.at[j, tile], device_id=(x, y, 1 - c),
                device_id_type=pl.DeviceIdType.MESH)

        for d, src, dst, w in _shard_pieces():
            slots_v[d, tile_rows(i), src:src + w] = g_ref[:, dst:dst + w]
        for j in range(n_chip):
            give(j, i).start()

        @pl.when(i == nt - 1)
        def _():
            for tile in range(nt):
                for j in range(n_chip):
                    give(j, tile).wait()
            for j in range(n_chip):
                for r0 in range(0, D, 64):
                    mine = slots_v[2 * j + c, r0:r0 + 64, :].astype(F32)
                    o_ref[j, r0:r0 + 64, :] = (mine + sib_v[j, r0:r0 + 64, :].astype(F32)).astype(BF16)

    half = _sds((n_chip, D, width), dwa.dtype)
    return pl.pallas_call(
        body, name="reshard_pair_reduce", grid=(nt,), in_specs=[pl.BlockSpec((TM, NA), lambda i: (i, 0))],
        out_specs=pl.BlockSpec(half.shape, lambda i: (0, 0, 0)), out_shape=half,
        scratch_shapes=[pltpu.VMEM((N_DEV, D, width), dwa.dtype), pltpu.VMEM(half.shape, dwa.dtype),
                        pltpu.SemaphoreType.DMA((n_chip, nt)), pltpu.SemaphoreType.DMA((n_chip, nt))],
        compiler_params=pltpu.CompilerParams(vmem_limit_bytes=VMEM_LIMIT, has_side_effects=True))(dwa)


CHIP_FLIPS = (2, 4, 6)


def _chip_exchange_start(partial):
    n = len(CHIP_FLIPS)

    def body(p_ref, land_ref, *rest):
        sends, recvs, token = rest[:n], rest[n:2 * n], rest[2 * n + 2]
        x, y, c = lax.axis_index("x"), lax.axis_index("y"), lax.axis_index("c")
        me = 4 * x + 2 * y + c
        for idx, k in enumerate(CHIP_FLIPS):
            pltpu.make_async_remote_copy(
                src_ref=p_ref.at[(me ^ k) >> 1], dst_ref=land_ref.at[me >> 1], send_sem=sends[idx],
                recv_sem=recvs[idx], device_id=(x ^ ((k >> 2) & 1), y ^ ((k >> 1) & 1), c),
                device_id_type=pl.DeviceIdType.MESH).start()
        token[...] = jnp.zeros_like(token)

    hbm = pl.BlockSpec(memory_space=pltpu.HBM)
    sem = pl.BlockSpec(memory_space=pltpu.SEMAPHORE)
    buf = pltpu.HBM(partial.shape, partial.dtype)
    return pl.pallas_call(
        body, name="chip_exchange_start",
        out_shape=(pltpu.SemaphoreType.DMA(()),) * (2 * n) + (buf, buf, _sds((8, LANES), F32)),
        in_specs=(hbm, hbm), out_specs=(sem,) * (2 * n) + (hbm, hbm, pl.BlockSpec(memory_space=pltpu.VMEM)),
        input_output_aliases={0: 2 * n, 1: 2 * n + 1},
        compiler_params=pltpu.CompilerParams(has_side_effects=pltpu.SideEffectType.DATAFLOW_SIDE_EFFECTING))(
            pltpu.with_memory_space_constraint(partial, pltpu.HBM),
            pltpu.with_memory_space_constraint(lax.empty(partial.shape, partial.dtype), pltpu.HBM))


def _chip_exchange_wait(started, after):
    n = len(CHIP_FLIPS)
    sems, (p_thru, land_thru) = started[:2 * n], started[2 * n:2 * n + 2]

    def body(p_ref, land_ref, *rest):
        sends, recvs = rest[:n], rest[n:2 * n]
        x, y, c = lax.axis_index("x"), lax.axis_index("y"), lax.axis_index("c")
        me = 4 * x + 2 * y + c
        for idx, k in enumerate(CHIP_FLIPS):
            copy = pltpu.make_async_remote_copy(
                src_ref=p_ref.at[(me ^ k) >> 1], dst_ref=land_ref.at[(me ^ k) >> 1], send_sem=sends[idx],
                recv_sem=recvs[idx], device_id=(x ^ ((k >> 2) & 1), y ^ ((k >> 1) & 1), c),
                device_id_type=pl.DeviceIdType.MESH)
            copy.wait_send()
            copy.wait_recv()

    hbm = pl.BlockSpec(memory_space=pltpu.HBM)
    sem = pl.BlockSpec(memory_space=pltpu.SEMAPHORE)
    buf = pltpu.HBM(p_thru.shape, p_thru.dtype)
    return pl.pallas_call(
        body, name="chip_exchange_wait", out_shape=(buf, buf),
        in_specs=(hbm, hbm) + (sem,) * (2 * n) + (pl.BlockSpec(memory_space=pl.ANY),) * len(after),
        out_specs=(hbm, hbm), input_output_aliases={0: 0, 1: 1},
        compiler_params=pltpu.CompilerParams(has_side_effects=pltpu.SideEffectType.DATAFLOW_SIDE_EFFECTING))(
            p_thru, land_thru, *sems, *after)


def _slab_peer(x, y, c, k):
    return (x ^ ((k >> 2) & 1), y ^ ((k >> 1) & 1), c ^ (k & 1))


def _gather_slab_start(slab, carried):
    n = N_DEV - 1

    def body(s_ref, land_ref, carried_ref, *rest):
        sends, recvs, own_sem = rest[:n], rest[n:2 * n], rest[2 * n + 3]
        x, y, c = lax.axis_index("x"), lax.axis_index("y"), lax.axis_index("c")
        me = 4 * x + 2 * y + c
        for k in range(1, N_DEV):
            pltpu.make_async_remote_copy(
                src_ref=s_ref, dst_ref=land_ref.at[me], send_sem=sends[k - 1], recv_sem=recvs[k - 1],
                device_id=_slab_peer(x, y, c, k), device_id_type=pl.DeviceIdType.MESH).start()
        own = pltpu.make_async_copy(s_ref, land_ref.at[me], own_sem)
        own.start()
        own.wait()

    hbm = pl.BlockSpec(memory_space=pltpu.HBM)
    sem = pl.BlockSpec(memory_space=pltpu.SEMAPHORE)
    land = (N_DEV,) + slab.shape
    return pl.pallas_call(
        body, name="gather_slab_start",
        out_shape=(pltpu.SemaphoreType.DMA(()),) * (2 * n) + (
            pltpu.HBM(slab.shape, slab.dtype), pltpu.HBM(land, slab.dtype), pltpu.HBM(carried.shape, carried.dtype)),
        in_specs=(hbm, hbm, hbm), out_specs=(sem,) * (2 * n) + (hbm, hbm, hbm),
        input_output_aliases={0: 2 * n, 1: 2 * n + 1, 2: 2 * n + 2}, scratch_shapes=[pltpu.SemaphoreType.DMA(())],
        compiler_params=pltpu.CompilerParams(has_side_effects=pltpu.SideEffectType.DATAFLOW_SIDE_EFFECTING))(
            pltpu.with_memory_space_constraint(slab, pltpu.HBM),
            pltpu.with_memory_space_constraint(lax.empty(land, slab.dtype), pltpu.HBM),
            pltpu.with_memory_space_constraint(carried, pltpu.HBM))


def _chip_wait_slab_start(chip_started, slab, after):
    nc, n, na = len(CHIP_FLIPS), N_DEV - 1, len(after)
    chip_sems, (p_thru, land_thru) = chip_started[:2 * nc], chip_started[2 * nc:2 * nc + 2]

    def body(p_ref, land_ref, *rest):
        chip_sends, chip_recvs, s_ref, sl_ref = rest[:nc], rest[nc:2 * nc], rest[2 * nc], rest[2 * nc + 1]
        outs = rest[2 * nc + 2 + na:]
        sends, recvs, own_sem = outs[2:2 + n], outs[2 + n:2 + 2 * n], outs[4 + 2 * n]
        x, y, c = lax.axis_index("x"), lax.axis_index("y"), lax.axis_index("c")
        me = 4 * x + 2 * y + c
        for idx, k in enumerate(CHIP_FLIPS):
            copy = pltpu.make_async_remote_copy(
                src_ref=p_ref.at[(me ^ k) >> 1], dst_ref=land_ref.at[(me ^ k) >> 1], send_sem=chip_sends[idx],
                recv_sem=chip_recvs[idx], device_id=(x ^ ((k >> 2) & 1), y ^ ((k >> 1) & 1), c),
                device_id_type=pl.DeviceIdType.MESH)
            copy.wait_send()
            copy.wait_recv()
        for k in range(1, N_DEV):
            pltpu.make_async_remote_copy(
                src_ref=s_ref, dst_ref=sl_ref.at[me], send_sem=sends[k - 1], recv_sem=recvs[k - 1],
                device_id=_slab_peer(x, y, c, k), device_id_type=pl.DeviceIdType.MESH).start()
        own = pltpu.make_async_copy(s_ref, sl_ref.at[me], own_sem)
        own.start()
        own.wait()

    hbm = pl.BlockSpec(memory_space=pltpu.HBM)
    sem = pl.BlockSpec(memory_space=pltpu.SEMAPHORE)
    buf = pltpu.HBM(p_thru.shape, p_thru.dtype)
    land = (N_DEV,) + slab.shape
    outs = pl.pallas_call(
        body, name="chip_wait_slab_start",
        out_shape=(buf, buf) + (pltpu.SemaphoreType.DMA(()),) * (2 * n) + (
            pltpu.HBM(slab.shape, slab.dtype), pltpu.HBM(land, slab.dtype)),
        in_specs=(hbm, hbm) + (sem,) * (2 * nc) + (hbm, hbm) + (pl.BlockSpec(memory_space=pl.ANY),) * na,
        out_specs=(hbm, hbm) + (sem,) * (2 * n) + (hbm, hbm),
        input_output_aliases={0: 0, 1: 1, 2 * nc + 2: 2 * n + 2, 2 * nc + 3: 2 * n + 3},
        scratch_shapes=[pltpu.SemaphoreType.DMA(())],
        compiler_params=pltpu.CompilerParams(has_side_effects=pltpu.SideEffectType.DATAFLOW_SIDE_EFFECTING))(
            p_thru, land_thru, *chip_sems, pltpu.with_memory_space_constraint(slab, pltpu.HBM),
            pltpu.with_memory_space_constraint(lax.empty(land, slab.dtype), pltpu.HBM), *after)
    return outs[0], outs[1], tuple(outs[2:])


def _gather_slab_wait(started, after):
    n = N_DEV - 1
    sems, (s_thru, land_thru) = started[:2 * n], started[2 * n:2 * n + 2]

    def body(s_ref, land_ref, *rest):
        sends, recvs = rest[:n], rest[n:2 * n]
        x, y, c = lax.axis_index("x"), lax.axis_index("y"), lax.axis_index("c")
        me = 4 * x + 2 * y + c
        for k in range(1, N_DEV):
            copy = pltpu.make_async_remote_copy(
                src_ref=s_ref, dst_ref=land_ref.at[me ^ k], send_sem=sends[k - 1], recv_sem=recvs[k - 1],
                device_id=_slab_peer(x, y, c, k), device_id_type=pl.DeviceIdType.MESH)
            copy.wait_send()
            copy.wait_recv()

    hbm = pl.BlockSpec(memory_space=pltpu.HBM)
    sem = pl.BlockSpec(memory_space=pltpu.SEMAPHORE)
    return pl.pallas_call(
        body, name="gather_slab_wait",
        out_shape=(pltpu.HBM(s_thru.shape, s_thru.dtype), pltpu.HBM(land_thru.shape, land_thru.dtype)),
        in_specs=(hbm, hbm) + (sem,) * (2 * n) + (pl.BlockSpec(memory_space=pl.ANY),) * len(after),
        out_specs=(hbm, hbm), input_output_aliases={0: 0, 1: 1},
        compiler_params=pltpu.CompilerParams(has_side_effects=pltpu.SideEffectType.DATAFLOW_SIDE_EFFECTING))(
            s_thru, land_thru, *sems, *after)[1]


def _adamw(w, g, m, v):
    m = ADAM_B1 * m + (1.0 - ADAM_B1) * g
    v = ADAM_B2 * v + (1.0 - ADAM_B2) * (g * g)
    m_hat = m / (1.0 - ADAM_B1 ** ADAM_STEP)
    v_hat = v / (1.0 - ADAM_B2 ** ADAM_STEP)
    delta = -ADAM_LR * (m_hat / (jnp.sqrt(v_hat) + ADAM_EPS) + ADAM_WD * w)
    return delta, m, v


def _sum_adamw(recv, w, m, v, name, after=None):
    lead = w.ndim - 2
    rows, cols = w.shape[-2:]
    tr = 256 if rows % 256 == 0 else 128
    n_slots = recv.shape[0]
    extra = [] if after is None else [after]

    def body(*refs):
        r_ref = refs[0]
        w_ref, m_ref, v_ref, g_ref, d_ref, nm_ref, nv_ref = refs[1 + len(extra):]
        g = None
        for slot in range(n_slots):
            g = r_ref[slot].astype(F32) if g is None else g + r_ref[slot].astype(F32)
        g_ref[...] = g
        d_ref[...], nm_ref[...], nv_ref[...] = _adamw(w_ref[...], g, m_ref[...], v_ref[...])

    blk = pl.BlockSpec((None,) * lead + (tr, cols), lambda i: (0,) * lead + (i, 0))
    slots = pl.BlockSpec((n_slots, tr, cols), lambda i: (0, i, 0))
    return pl.pallas_call(
        body, name=name, grid=(rows // tr,),
        in_specs=[slots] + [pl.BlockSpec(a.shape, lambda i: (0, 0)) for a in extra] + [blk, blk, blk],
        out_specs=[blk] * 4, out_shape=[_sds(w.shape, F32)] * 4, compiler_params=_params())(recv, *extra, w, m, v)


def _entry_view(a):
    _, rows, cols = a.shape
    return jnp.transpose(a, (2, 0, 1)).reshape(cols, rows // LANES, LANES)


def _from_entry_view(a):
    cols, groups, lanes = a.shape
    return jnp.transpose(a, (1, 2, 0)).reshape(1, groups * lanes, cols)


def _sum_adamw_entry_view(landing, own, w, m, v, name):
    n_slots, rows, cols = landing.shape
    groups = rows // LANES
    cols_pad = -(-cols // LANES) * LANES

    def body(r_ref, own_ref, w_ref, m_ref, v_ref, g_ref, d_ref, nm_ref, nv_ref, pad_ref, gt_ref):
        j = pl.program_id(0)
        chip = (4 * lax.axis_index("x") + 2 * lax.axis_index("y") + lax.axis_index("c")) >> 1
        pad_ref[:, cols_pad - LANES:] = jnp.zeros((LANES, LANES), F32)
        for r0 in range(0, LANES, 32):
            g = None
            for slot in range(n_slots):
                part = jnp.where(chip == slot, own_ref[slot, r0:r0 + 32], r_ref[slot, r0:r0 + 32])
                g = part.astype(F32) if g is None else g + part.astype(F32)
            pad_ref[r0:r0 + 32, :cols] = g
        for c0 in range(0, cols_pad, LANES):
            gt_ref[c0:c0 + LANES, :] = pad_ref[:, c0:c0 + LANES].T
        def update_plane(plane):
            for c0 in range(0, cols, 64):
                n = min(64, cols - c0)
                at = pl.ds(c0 * groups + plane, n, stride=groups)
                gt = gt_ref[c0:c0 + n, :]
                g_ref[at, :] = gt
                d_ref[at, :], nm_ref[at, :], nv_ref[at, :] = _adamw(w_ref[at, :], gt, m_ref[at, :], v_ref[at, :])

        for plane in range(groups):
            pl.when(j == plane)(lambda plane=plane: update_plane(plane))

    flat = lambda a: _entry_view(a).reshape(cols * groups, LANES)
    whole = pl.BlockSpec((cols * groups, LANES), lambda j: (0, 0))
    slots = pl.BlockSpec((n_slots, LANES, cols), lambda j: (0, j, 0))
    outs = pl.pallas_call(
        body, name=name, grid=(groups,), in_specs=[slots, slots, whole, whole, whole], out_specs=[whole] * 4,
        out_shape=[_sds((cols * groups, LANES), F32)] * 4,
        scratch_shapes=[pltpu.VMEM((LANES, cols_pad), F32), pltpu.VMEM((cols_pad, LANES), F32)],
        compiler_params=_params())(landing, own, flat(w), flat(m), flat(v))
    return [_from_entry_view(o.reshape(cols, groups, LANES)) for o in outs], outs[0]


SLAB_ROWS = 16
SLOT = {"kv_norm_g": (8, 0, D), "norm_b_g": (9, 0, D), "b_forget": (10, 0, 16), "qnorm_a_g": (10, 128, HD),
        "knorm_a_g": (10, 256, HD), "knorm_b_g": (10, 384, HD), "qnorm_b_g": (10, 512, HD), "sinks": (10, 640, 16)}
SMALL = ["norm_a_g", "b_forget", "qnorm_a_g", "knorm_a_g", "kv_norm_g", "knorm_b_g", "norm_b_g", "qnorm_b_g", "sinks"]


LOSS_ROW = 11


def _pack_small(dg_a, dg_kv, dg_b, db_f, dgq_a, dgk_a, dgk_b, dgq_b, dsinks, lsum):
    def fold(ref):
        return ref[:, 0:HD] + ref[:, HD:2 * HD]

    def body(dga_ref, dgkv_ref, dgb_ref, dbf_ref, dgqa_ref, dgka_ref, dgkb_ref, dgqb_ref, dsk_ref, ls_ref, slab_ref):
        slab_ref[...] = jnp.zeros_like(slab_ref)
        for r in range(N_DEV):
            slab_ref[r:r + 1, 0:LANES] = dga_ref[:, LANES * r:LANES * (r + 1)]
        slab_ref[8:9, :] = dgkv_ref[...]
        slab_ref[9:10, :] = dgb_ref[...]
        slab_ref[10:11, 0:LANES] = dbf_ref[...]
        slab_ref[10:11, 128:128 + HD] = fold(dgqa_ref)
        slab_ref[10:11, 256:256 + HD] = fold(dgka_ref)
        slab_ref[10:11, 384:384 + HD] = fold(dgkb_ref)
        slab_ref[10:11, 512:512 + HD] = fold(dgqb_ref)
        slab_ref[10:11, 640:640 + LANES] = dsk_ref[...]
        slab_ref[LOSS_ROW:LOSS_ROW + 1, 0:LANES] = ls_ref[...]

    return pl.pallas_call(body, name="pack_small", out_shape=_sds((SLAB_ROWS, D), F32), compiler_params=_params())(
        dg_a, dg_kv, dg_b, db_f, dgq_a, dgk_a, dgk_b, dgq_b, dsinks, lsum)


def _small_adamw(recv, ws, ms, vs):
    k = len(SMALL)

    def body(*refs):
        r_ref = refs[0]
        w_refs, m_refs, v_refs = refs[1:1 + k], refs[1 + k:1 + 2 * k], refs[1 + 2 * k:1 + 3 * k]
        outs = refs[1 + 3 * k:1 + 7 * k]
        loss_ref, tot = refs[1 + 7 * k], refs[2 + 7 * k]
        g = r_ref[0]
        for dev in range(1, N_DEV):
            g = g + r_ref[dev]
        tot[...] = g
        loss_ref[...] = tot[LOSS_ROW:LOSS_ROW + 1, 0:LANES] * (0.5 / D)
        me = 4 * lax.axis_index("x") + 2 * lax.axis_index("y") + lax.axis_index("c")
        for p, name in enumerate(SMALL):
            if name == "norm_a_g":
                mine = lax.broadcasted_iota(jnp.int32, (N_DEV, LANES), 0) == me
                gp = jnp.sum(jnp.where(mine, tot[0:N_DEV, 0:LANES], 0.0), axis=0, keepdims=True)
            else:
                row, lo, width = SLOT[name]
                gp = tot[row:row + 1, lo:lo + width]
            d, nm, nv = _adamw(w_refs[p][...], gp, m_refs[p][...], v_refs[p][...])
            outs[p][...] = gp
            outs[k + p][...] = d
            outs[2 * k + p][...] = nm
            outs[3 * k + p][...] = nv

    shapes = [_sds(w.shape, F32) for w in ws]
    return pl.pallas_call(body, name="small_adamw", out_shape=shapes * 4 + [_sds((1, LANES), F32)],
                          scratch_shapes=[pltpu.VMEM((SLAB_ROWS, D), F32)],
                          compiler_params=_params())(recv, *ws, *ms, *vs)


def _rope_tables(positions):
    inv_freq = jnp.power(jnp.float32(ROPE_THETA), -jnp.arange(0, ROT, 2, dtype=F32) / ROT)
    ang = positions.astype(F32)[:, None] * inv_freq[None, :]
    cos, sin = jnp.cos(ang), jnp.sin(ang)
    c64 = jnp.concatenate([cos, cos, jnp.ones((S, HD - ROT), F32)], axis=-1)
    s64 = jnp.concatenate([-sin, sin, jnp.zeros((S, HD - ROT), F32)], axis=-1)
    return jnp.tile(c64, (1, 2)), jnp.tile(s64, (1, 2))


def _local_step(x, tgt, positions, g_a, wa, b_forget, gq_a, gk_a, g_kv, gk_b, g_b, gq_b, sinks,
                woa_s, wkv_s, wib_s, wob_s, adamw_others):
    nq = S // TQ
    cos2, sin2 = _rope_tables(positions)
    b_pad = jnp.pad(b_forget, ((0, 0), (0, LANES - N_HEADS)))

    u_a, proj, qn, kn, vb, crow, cbc = _head_a(x, g_a, wa, gq_a, gk_a, b_pad)
    o_a, z_a, lse_a, woa_g, wkv_g, w_in_b, wob_g = _fox_fwd(
        qn, kn, vb, proj, crow, cbc,
        rider=[("gather_rows", woa_s), ("gather_rows", wkv_s), ("gather_cols", wib_s), ("gather_rows", wob_s)])
    w_out_a, w_kv, w_out_b = woa_g.reshape(D, D), wkv_g.reshape(D, 512), wob_g.reshape(D, D)
    h1, u_kv, u_b, kv, pb, qb, ksh, vsh = _head_b(x, z_a, w_out_a, g_kv, g_b, w_kv, w_in_b, gq_b, gk_b, cos2, sin2)
    sinks1 = sinks.reshape(N_HEADS)
    o_b, z_b, lse_b = _swa_fwd(qb, ksh, vsh, pb, sinks1)
    dy, lsum = _out_b_loss(z_b, w_out_b, h1, tgt)
    dw_out_b = _mm(z_b, dy, "tn", 512, 512, S, out_dtype=BF16, name="mm_dw_out_b")
    dz_b = _mm(dy, w_out_b, "nt", 1024, 512, D, name="mm_dz_b")
    dpb, dka, dkb, dva, dvb, dsinks, dgq_b = _swa_bwd(qb, ksh, vsh, dz_b, o_b, lse_b, pb, sinks1, gq_b, cos2, sin2)
    dkv, dgk_b = _prep_kv_bwd(dka, dkb, dva, dvb, kv, gk_b, cos2, sin2)
    dw_in_b = _mm(u_b, dpb, "tn", 512, 512, S, out_dtype=BF16, name="mm_dw_in_b")
    dw_kv = _mm(u_kv, dkv, "tn", 512, 512, S, out_dtype=BF16, name="mm_dw_kv")
    dh1, dg_b, dg_kv = _du_b_rms_bwd(dpb, w_in_b, dkv, w_kv, h1, g_b, g_kv, dy)
    dw_out_a = _mm(z_a, dh1, "tn", 512, 512, S, out_dtype=BF16, name="mm_dw_out_a")
    do_a, dgate_a, delta_a = _fox_bwd_pre(dh1, w_out_a, proj, o_a)
    dk_a, dv_a, dcs, dq_a, drow, r_wob, r_wib, r_wkv, r_woa = _fox_bwd(
        qn, kn, vb, do_a, lse_a, delta_a, crow, cbc,
        rider=[("a2a_rows", dw_out_b), ("a2a_cols", dw_in_b), ("a2a_rows", dw_kv), ("a2a_rows", dw_out_a)])
    dproj, dgq_a, dgk_a, db_f = _prep_a_bwd(dq_a, dk_a, dv_a, dgate_a, drow, dcs, proj, b_pad, gq_a, gk_a)
    dwa = _mm(u_a, dproj, "tn", 1024, 256, S, out_dtype=BF16, name="mm_dw_in_a")
    partial = _reshard_pair_reduce(dwa)
    started = _chip_exchange_start(partial)
    dx, dg_a = _du_a_rms_bwd(dproj, wa, x, g_a, dh1, after=started[-1])
    others = adamw_others(dict(w_out_a=r_woa, w_kv=r_wkv, w_in_b=r_wib, w_out_b=r_wob), dg_a)
    slab = _pack_small(dg_a, dg_kv, dg_b, db_f, dgq_a, dgk_a, dgk_b, dgq_b, dsinks, lsum)
    partial, landed, slab_started = _chip_wait_slab_start(started, slab, [res[0] for res in others.values()])
    return dx, (landed, partial), others, slab_started


def kernel(x, positions, norm_a_g, w_in_a, b_forget, qnorm_a_g, knorm_a_g, w_out_a, kv_norm_g, w_kv, knorm_b_g, norm_b_g, w_in_b, qnorm_b_g, sinks, w_out_b, loss_target, m_norm_a_g, m_w_in_a, m_b_forget, m_qnorm_a_g, m_knorm_a_g, m_w_out_a, m_kv_norm_g, m_w_kv, m_knorm_b_g, m_norm_b_g, m_w_in_b, m_qnorm_b_g, m_sinks, m_w_out_b, v_norm_a_g, v_w_in_a, v_b_forget, v_qnorm_a_g, v_knorm_a_g, v_w_out_a, v_kv_norm_g, v_w_kv, v_knorm_b_g, v_norm_b_g, v_w_in_b, v_qnorm_b_g, v_sinks, v_w_out_b):
    wa_g, ga_g, woa_s, wkv_s, wib_s, wob_s = _gather_first(w_in_a, w_out_a, w_kv, w_in_b, w_out_b, norm_a_g)
    state = dict(w_in_a=(w_in_a, m_w_in_a, v_w_in_a), w_out_a=(w_out_a, m_w_out_a, v_w_out_a),
                 w_kv=(w_kv, m_w_kv, v_w_kv), w_in_b=(w_in_b, m_w_in_b, v_w_in_b),
                 w_out_b=(w_out_b, m_w_out_b, v_w_out_b))

    def adamw_others(landed, after):
        return {n: _sum_adamw(r, *state[n], "adamw_" + n, after=after) for n, r in landed.items()}

    dx, r_wa, big, slab_started = _local_step(
        x[0], loss_target[0], positions, ga_g.reshape(1, D), _unshard_wa(wa_g), b_forget, qnorm_a_g, knorm_a_g,
        kv_norm_g.reshape(1, D), knorm_b_g.reshape(1, HD), norm_b_g, qnorm_b_g, sinks, woa_s, wkv_s, wib_s, wob_s,
        adamw_others)
    big["w_in_a"], done = _sum_adamw_entry_view(*r_wa, *state["w_in_a"], "adamw_w_in_a")
    slab_g = _gather_slab_wait(slab_started, [done])

    r2 = lambda a: a.reshape(1, -1)
    small_w = dict(norm_a_g=norm_a_g, b_forget=b_forget, qnorm_a_g=qnorm_a_g, knorm_a_g=knorm_a_g,
                   kv_norm_g=kv_norm_g, knorm_b_g=knorm_b_g, norm_b_g=norm_b_g, qnorm_b_g=qnorm_b_g, sinks=sinks)
    small_m = dict(norm_a_g=m_norm_a_g, b_forget=m_b_forget, qnorm_a_g=m_qnorm_a_g, knorm_a_g=m_knorm_a_g,
                   kv_norm_g=m_kv_norm_g, knorm_b_g=m_knorm_b_g, norm_b_g=m_norm_b_g, qnorm_b_g=m_qnorm_b_g,
                   sinks=m_sinks)
    small_v = dict(norm_a_g=v_norm_a_g, b_forget=v_b_forget, qnorm_a_g=v_qnorm_a_g, knorm_a_g=v_knorm_a_g,
                   kv_norm_g=v_kv_norm_g, knorm_b_g=v_knorm_b_g, norm_b_g=v_norm_b_g, qnorm_b_g=v_qnorm_b_g,
                   sinks=v_sinks)
    res = _small_adamw(slab_g, [r2(small_w[n]) for n in SMALL], [r2(small_m[n]) for n in SMALL],
                       [r2(small_v[n]) for n in SMALL])
    k = len(SMALL)
    small = {n: [res[q * k + p].reshape(small_w[n].shape) for q in range(4)] for p, n in enumerate(SMALL)}
    loss = res[4 * k][0, 0]

    order = ["norm_a_g", "w_in_a", "b_forget", "qnorm_a_g", "knorm_a_g", "w_out_a", "kv_norm_g", "w_kv",
             "knorm_b_g", "norm_b_g", "w_in_b", "qnorm_b_g", "sinks", "w_out_b"]

    def leaf(n, q):
        return big[n][q] if n in big else small[n][q]

    outs = [loss, dx[None]]
    for q in range(4):
        outs.extend(leaf(n, q) for n in order)
    return tuple(outs)
```

```python
import jax
import jax.numpy as jnp
from jax import lax
from jax.experimental import pallas as pl
from jax.experimental.pallas import tpu as pltpu

F32, BF16 = jnp.float32, jnp.bfloat16

S = 2048
D = 1024
HD = 64
N_HEADS = 16
N_DEV = 8
NA = 4352
GOFF = 3072
FOFF = 4096
RAW_F = 3072
RAW_G = RAW_F + N_HEADS
NA_RAW = 4112
EPS = 1e-6
QSCALE = 0.125
ROPE_THETA = 500000.0
ROT = 16
WIN = 128
TQ = 256
TK = 256
KS = TQ // 2
HPS = 8
HW = HPS * HD
TM = 256
RT = 256
RB = 512
CB = 256
LANES = 128

ADAM_LR, ADAM_B1, ADAM_B2, ADAM_EPS, ADAM_WD, ADAM_STEP = 0.001, 0.9, 0.999, 1e-08, 0.01, 10

VMEM_LIMIT = 56 * 1024 * 1024


def _params():
    return pltpu.CompilerParams(vmem_limit_bytes=VMEM_LIMIT)


def _sds(shape, dtype):
    return jax.ShapeDtypeStruct(shape, dtype)


def _dot_nt(a, b):
    return lax.dot_general(a, b, (((1,), (1,)), ((), ())), preferred_element_type=F32)


def _dot_tn(a, b):
    return lax.dot_general(a, b, (((0,), (0,)), ((), ())), preferred_element_type=F32)


def _dot_nn(a, b):
    return lax.dot_general(a, b, (((1,), (0,)), ((), ())), preferred_element_type=F32)


def _sigmoid(g):
    return 1.0 / (1.0 + jnp.exp(-g))


def _lane_iota(shape):
    return lax.broadcasted_iota(jnp.int32, shape, len(shape) - 1)


def _flips(kind):
    return (2, 4, 6) if kind == "a2a_chips" else tuple(range(1, N_DEV))


def _send_view(kind, ref, dev):
    if kind in ("gather_rows", "gather_cols"):
        return ref
    if kind == "a2a_slots":
        return ref.at[dev]
    if kind == "a2a_chips":
        return ref.at[dev >> 1]
    if kind == "a2a_rows":
        rows = ref.shape[0] // N_DEV
        return ref.at[pl.ds(pl.multiple_of(dev * rows, rows), rows)]
    cols = ref.shape[1] // N_DEV
    return ref.at[:, pl.ds(pl.multiple_of(dev * cols, cols), cols)]


def _land_view(kind, ref, dev):
    if kind == "gather_cols":
        cols = ref.shape[1] // N_DEV
        return ref.at[:, pl.ds(pl.multiple_of(dev * cols, cols), cols)]
    if kind == "a2a_chips":
        return ref.at[dev >> 1]
    return ref.at[dev]


def _landing_sds(kind, arr):
    if kind == "gather_rows":
        return _sds((N_DEV,) + arr.shape, arr.dtype)
    if kind == "gather_cols":
        return _sds((arr.shape[0], N_DEV * arr.shape[1]), arr.dtype)
    if kind == "a2a_rows":
        return _sds((N_DEV, arr.shape[0] // N_DEV, arr.shape[1]), arr.dtype)
    if kind == "a2a_cols":
        return _sds((N_DEV, arr.shape[0], arr.shape[1] // N_DEV), arr.dtype)
    return _sds(arr.shape, arr.dtype)


def _exchange_sems(n_parts):
    n = n_parts * (N_DEV - 1)
    return [pltpu.SemaphoreType.DMA((n,)), pltpu.SemaphoreType.DMA((n,)), pltpu.SemaphoreType.DMA((n_parts,))]


def _exchange_ops(kinds, srcs, dsts, sems, start, wait):
    send_sems, recv_sems, local_sems = sems
    x, y, c = lax.axis_index("x"), lax.axis_index("y"), lax.axis_index("c")
    me = 4 * x + 2 * y + c

    def local(a):
        return pltpu.make_async_copy(_send_view(kinds[a], srcs[a], me), _land_view(kinds[a], dsts[a], me),
                                     local_sems.at[a])

    def remote(a, k, landing_dev):
        peer = (x ^ ((k >> 2) & 1), y ^ ((k >> 1) & 1), c ^ (k & 1))
        sem = a * (N_DEV - 1) + k - 1
        return pltpu.make_async_remote_copy(
            src_ref=_send_view(kinds[a], srcs[a], me ^ k), dst_ref=_land_view(kinds[a], dsts[a], landing_dev),
            send_sem=send_sems.at[sem], recv_sem=recv_sems.at[sem], device_id=peer,
            device_id_type=pl.DeviceIdType.MESH)

    pairs = [(a, k) for k in range(1, N_DEV) for a in range(len(kinds)) if k in _flips(kinds[a])]
    if start:
        for a in range(len(kinds)):
            local(a).start()
        for a, k in pairs:
            remote(a, k, me).start()
    if wait:
        for a, k in pairs:
            remote(a, k, me ^ k).wait_recv()
            remote(a, k, me).wait_send()
        for a in range(len(kinds)):
            local(a).wait()


def _gather_two_level(srcs, landing_of, sems, meanwhile=None):
    send_sems, recv_sems, local_sems = sems
    x, y, c = lax.axis_index("x"), lax.axis_index("y"), lax.axis_index("c")
    me, sibling = (x, y, c), (x, y, 1 - c)
    chips = [(1 - x, y), (x, 1 - y), (1 - x, 1 - y)]

    def slot(a, dev):
        return landing_of[a](4 * dev[0] + 2 * dev[1] + dev[2])

    def copy(a, k, block, to, src=None):
        return pltpu.make_async_remote_copy(
            src_ref=slot(a, block) if src is None else src, dst_ref=slot(a, block),
            send_sem=send_sems.at[a * (N_DEV - 1) + k], recv_sem=recv_sems.at[a * (N_DEV - 1) + k],
            device_id=to, device_id_type=pl.DeviceIdType.MESH)

    parts = range(len(srcs))
    mine = [pltpu.make_async_copy(srcs[a], slot(a, me), local_sems.at[a]) for a in parts]
    first = [copy(a, 1 + j, me, (*chip, c), src=srcs[a]) for a in parts for j, chip in enumerate(chips)]
    first += [copy(a, 0, me, sibling, src=srcs[a]) for a in parts]
    for cp in first + mine:
        cp.start()
    if meanwhile is not None:
        meanwhile()
    passed = []
    for a in parts:
        for j, chip in enumerate(chips):
            copy(a, 1 + j, (*chip, c), me).wait_recv()
            fwd = copy(a, 4 + j, (*chip, c), sibling)
            fwd.start()
            passed.append(fwd)
    for a in parts:
        copy(a, 0, sibling, me).wait_recv()
        for j, chip in enumerate(chips):
            copy(a, 4 + j, (*chip, 1 - c), me).wait_recv()
    for cp in first + passed:
        cp.wait_send()
    for cp in mine:
        cp.wait()


def _call(body, *, name, args, in_specs, out_specs, out_shape, grid=(), scratch_shapes=(), aliases=None, rider=()):
    n_in, n_out, n_scr, n_r = len(in_specs), len(out_specs), len(scratch_shapes), len(rider)
    kinds = [kind for kind, _ in rider]

    def kernel_body(*refs):
        c_in, r_in = refs[:n_in], refs[n_in:n_in + n_r]
        c_out = refs[n_in + n_r:n_in + n_r + n_out]
        r_out = refs[n_in + n_r + n_out:n_in + 2 * n_r + n_out]
        rest = refs[n_in + 2 * n_r + n_out:]
        c_scr, sems = rest[:n_scr], rest[n_scr:]
        if n_r:
            assert grid, "a rider needs a gridded call"
            ids = [pl.program_id(ax) for ax in range(len(grid))]
            first, last = ids[0] == 0, ids[0] == grid[0] - 1
            for pid, size in zip(ids[1:], grid[1:]):
                first = first & (pid == 0)
                last = last & (pid == size - 1)
            pl.when(first)(lambda: _exchange_ops(kinds, r_in, r_out, sems, True, False))
        body(*c_in, *c_out, *c_scr)
        if n_r:
            pl.when(last)(lambda: _exchange_ops(kinds, r_in, r_out, sems, False, True))

    anyspec = pl.BlockSpec(memory_space=pl.ANY)
    params = pltpu.CompilerParams(vmem_limit_bytes=VMEM_LIMIT, has_side_effects=bool(n_r))
    outs = pl.pallas_call(
        kernel_body, name=name, grid=grid, in_specs=list(in_specs) + [anyspec] * n_r,
        out_specs=list(out_specs) + [anyspec] * n_r,
        out_shape=list(out_shape) + [_landing_sds(kind, arr) for kind, arr in rider],
        scratch_shapes=list(scratch_shapes) + (_exchange_sems(n_r) if n_r else []),
        input_output_aliases=aliases or {}, compiler_params=params)(*args, *[arr for _, arr in rider])
    return list(outs)


def _mm(a, b, mode, tm, tn, tk, out_dtype=F32, add=None, name="mm", rider=()):
    if mode == "nn":
        (m, k), n = a.shape, b.shape[1]
        a_spec = pl.BlockSpec((tm, tk), lambda i, j, kk: (i, kk))
        b_spec = pl.BlockSpec((tk, tn), lambda i, j, kk: (kk, j))
        dot = _dot_nn
    elif mode == "nt":
        (m, k), n = a.shape, b.shape[0]
        a_spec = pl.BlockSpec((tm, tk), lambda i, j, kk: (i, kk))
        b_spec = pl.BlockSpec((tn, tk), lambda i, j, kk: (j, kk))
        dot = _dot_nt
    else:
        (k, m), n = a.shape, b.shape[1]
        a_spec = pl.BlockSpec((tk, tm), lambda i, j, kk: (kk, i))
        b_spec = pl.BlockSpec((tk, tn), lambda i, j, kk: (kk, j))
        dot = _dot_tn
    assert m % tm == 0 and n % tn == 0 and k % tk == 0, (m, n, k, tm, tn, tk)
    nk = k // tk
    has_add = add is not None

    def body(*refs):
        if has_add:
            a_ref, b_ref, add_ref, o_ref, acc = refs
        else:
            a_ref, b_ref, o_ref, acc = refs
        p = dot(a_ref[...].astype(BF16), b_ref[...].astype(BF16))

        def finish(total):
            if has_add:
                total = add_ref[...] + total
            o_ref[...] = total.astype(out_dtype)

        if nk == 1:
            finish(p)
        else:
            kk = pl.program_id(2)

            @pl.when(kk == 0)
            def _():
                acc[...] = p

            @pl.when(kk > 0)
            def _():
                acc[...] += p

            @pl.when(kk == nk - 1)
            def _():
                finish(acc[...])

    in_specs = [a_spec, b_spec]
    args = [a, b]
    if has_add:
        in_specs.append(pl.BlockSpec((tm, tn), lambda i, j, kk: (i, j)))
        args.append(add)
    acc_shape = (tm, tn) if nk > 1 else (8, LANES)
    outs = _call(body, name=name, args=args, grid=(m // tm, n // tn, nk), in_specs=in_specs,
                 out_specs=[pl.BlockSpec((tm, tn), lambda i, j, kk: (i, j))], out_shape=[_sds((m, n), out_dtype)],
                 scratch_shapes=[pltpu.VMEM(acc_shape, F32)], rider=rider)
    return outs if rider else outs[0]


def _rms_rinv(x):
    return lax.rsqrt(jnp.mean(x * x, axis=-1, keepdims=True) + EPS)


def _rms_bwd_core(du, x, g):
    r = _rms_rinv(x)
    dug = du * g
    dx = r * (dug - x * ((r * r) * jnp.mean(dug * x, axis=-1, keepdims=True)))
    dg = jnp.sum(du * (x * r), axis=0, keepdims=True)
    return dx, dg


def _half_ones():
    r = lax.broadcasted_iota(jnp.int32, (LANES, LANES), 0)
    c = lax.broadcasted_iota(jnp.int32, (LANES, LANES), 1)
    return ((r < HD) == (c < HD)).astype(BF16)


def _half_sum(v, lo_half):
    if lo_half.dtype == jnp.bool_:
        s0 = jnp.sum(jnp.where(lo_half, v, 0.0), axis=-1, keepdims=True)
        s1 = jnp.sum(jnp.where(lo_half, 0.0, v), axis=-1, keepdims=True)
        return jnp.where(lo_half, s0, s1)
    hi = v.astype(BF16)
    lo = (v - hi.astype(F32)).astype(BF16)
    return _dot_nn(hi, lo_half) + _dot_nn(lo, lo_half)


def _head_rinv(x, lo_half):
    return lax.rsqrt(_half_sum(x * x, lo_half) * (1.0 / HD) + EPS)


def _head_norm_bwd(dn, x, g, lo_half):
    r = _head_rinv(x, lo_half)
    dng = dn * g
    dx = r * (dng - x * ((r * r) * (_half_sum(dng * x, lo_half) * (1.0 / HD))))
    dg = jnp.sum(dn * (x * r), axis=0, keepdims=True)
    return dx, dg


def _rope_swap(x, lane):
    l64 = lane & (HD - 1)
    return jnp.where(l64 < ROT // 2, pltpu.roll(x, LANES - ROT // 2, 1), pltpu.roll(x, ROT // 2, 1))


def _rope_fwd(x, cos, sin, lane):
    return x * cos + _rope_swap(x, lane) * sin


def _rope_bwd(dy, cos, sin, lane):
    return dy * cos + jnp.where((lane & (HD - 1)) < ROT, _rope_swap(dy * sin, lane), 0.0)


def _g2(g_ref):
    g = g_ref[...]
    return jnp.concatenate([g, g], axis=-1)


def _pairs(width):
    return [slice(LANES * c, LANES * (c + 1)) for c in range(width // LANES)]


def _pick_lane(block, lane, idx):
    return jnp.sum(jnp.where(lane == idx, block, 0.0), axis=-1, keepdims=True)


def _head_a(x, g, wa, gq, gk, b_pad):
    assert RT == TQ
    def body(x_ref, g_ref, w_ref, gq_ref, gk_ref, b_ref, u_ref, p_ref, qo_ref, ko_ref, vo_ref, c_ref, cbc_ref, carry):
        @pl.when(pl.program_id(0) == 0)
        def _():
            carry[...] = jnp.zeros_like(carry)

        xv = x_ref[...]
        u = ((xv * _rms_rinv(xv)) * g_ref[...]).astype(BF16)
        u_ref[...] = u
        for lo in range(0, NA, D):
            hi = min(lo + D, NA)
            p_ref[:, lo:hi] = _dot_nn(u, w_ref[:, lo:hi])
        lane = _lane_iota((RT, LANES))
        lo_half = lane < HD
        gq2, gk2 = _g2(gq_ref), _g2(gk_ref)
        for c in _pairs(D):
            q = p_ref[:, c]
            k = p_ref[:, D + c.start:D + c.stop]
            qo_ref[:, c] = (((q * _head_rinv(q, lo_half)) * gq2) * QSCALE).astype(BF16)
            ko_ref[:, c] = ((k * _head_rinv(k, lo_half)) * gk2).astype(BF16)
        vo_ref[...] = p_ref[:, 2 * D:3 * D].astype(BF16)

        z = p_ref[:, FOFF:FOFF + LANES] + b_ref[...]
        logf = jnp.minimum(z, 0.0) - jnp.log1p(jnp.exp(-jnp.abs(z)))
        r = lax.broadcasted_iota(jnp.int32, (RT, RT), 0)
        cc = lax.broadcasted_iota(jnp.int32, (RT, RT), 1)
        tri = (r >= cc).astype(F32)
        loc = jnp.dot(tri, logf, precision=lax.Precision.HIGHEST, preferred_element_type=F32) + carry[0:1, :]
        c_ref[:, 0, :] = loc.T[:N_HEADS]
        carry[0:1, :] = loc[RT - 1:RT, :]
        for h in range(N_HEADS):
            cbc_ref[:, LANES * h:LANES * (h + 1)] = jnp.broadcast_to(_pick_lane(loc, lane, h), (RT, LANES))

    row = lambda width: pl.BlockSpec((RT, width), lambda i: (i, 0))
    whole = lambda arr: pl.BlockSpec(arr.shape, lambda i: (0,) * arr.ndim, pipeline_mode=pl.Buffered(1))
    return pl.pallas_call(
        body, name="head_a", grid=(S // RT,),
        in_specs=[row(D), whole(g), whole(wa), whole(gq), whole(gk), whole(b_pad)],
        out_specs=[row(D), row(NA), row(D), row(D), row(D),
                   pl.BlockSpec((N_HEADS, None, 1, TQ), lambda i: (0, i, 0, 0)), row(N_HEADS * LANES)],
        out_shape=[_sds((S, D), BF16), _sds((S, NA), F32)] + [_sds((S, D), BF16)] * 3
        + [_sds((N_HEADS, S // TQ, 1, TQ), F32), _sds((S, N_HEADS * LANES), F32)],
        scratch_shapes=[pltpu.VMEM((8, LANES), F32)], compiler_params=_params())(x, g, wa, gq, gk, b_pad)


def _key_le_query(offset, keys=TK):
    r = lax.broadcasted_iota(jnp.int32, (keys, TQ), 0)
    c = lax.broadcasted_iota(jnp.int32, (keys, TQ), 1)
    return (r + offset) <= c


def _widen(tile):
    return jnp.concatenate([tile] * (TQ // LANES), axis=1)


def _fox_fwd(qn, kn, vb, proj, crow, cbc, rider=()):
    nq = S // TQ

    def body(q_ref, k_ref, v_ref, g_ref, cq_ref, cbc_ref, o_ref, z_ref, lse_ref, st_s, pt_s):
        i = pl.program_id(1)
        qs = [q_ref[:, HD * hh:HD * (hh + 1)] for hh in range(HPS)]
        cqs = [cq_ref[hh, 0] for hh in range(HPS)]

        def scores(s, hh):
            off = pl.multiple_of(s * KS, KS)
            kj = k_ref[pl.ds(off, KS), HD * hh:HD * (hh + 1)]
            return (_dot_nt(kj, qs[hh]) + cqs[hh]) - _widen(cbc_ref[pl.ds(off, KS), LANES * hh:LANES * (hh + 1)])

        def values(s, hh, pt):
            off = pl.multiple_of(s * KS, KS)
            return _dot_tn(v_ref[pl.ds(off, KS), HD * hh:HD * (hh + 1)], pt)

        def step(s, slot, carries, mask=None, last=False):
            if not last:
                for hh in range(HPS):
                    st_s[1 - slot, hh] = scores(s + 1, hh)
            pvs = [values(jnp.maximum(s - 1, 0), hh, pt_s[1 - slot, hh]) for hh in range(HPS)]
            out = []
            for hh in range(HPS):
                m, l, acc = carries[hh]
                st = st_s[slot, hh]
                if mask is not None:
                    st = jnp.where(mask, st, -jnp.inf)
                m_new = jnp.maximum(m, jnp.max(st, axis=0, keepdims=True))
                pt = jnp.exp(st - m_new)
                alpha = jnp.exp(m - m_new)
                pt_s[slot, hh] = pt.astype(BF16)
                out.append((m_new, alpha * l + jnp.sum(pt, axis=0, keepdims=True), alpha * (acc + pvs[hh])))
            return tuple(out)

        for hh in range(HPS):
            st_s[0, hh] = scores(0, hh)
            pt_s[1, hh] = jnp.zeros((KS, TQ), BF16)
        one = (jnp.full((1, TQ), -jnp.inf, F32), jnp.zeros((1, TQ), F32), jnp.zeros((HD, TQ), F32))
        carries = lax.fori_loop(0, i, lambda t, cr: step(2 * t + 1, 1, step(2 * t, 0, cr)), (one,) * HPS)
        carries = step(2 * i, 0, carries, mask=_key_le_query(0, KS))
        carries = step(2 * i + 1, 1, carries, mask=_key_le_query(KS, KS), last=True)
        accs = []
        for hh in range(HPS):
            m, l, acc = carries[hh]
            acc = acc + values(2 * i + 1, hh, pt_s[1, hh])
            accs.append(acc / l)
            lse_ref[hh, 0] = m + jnp.log(l)
        o = jnp.concatenate(accs, axis=0).T
        o_ref[...] = o
        g = g_ref[...]
        z_ref[...] = (o * (g * _sigmoid(g))).astype(BF16)

    qblk = pl.BlockSpec((TQ, HW), lambda hp, i: (i, hp))
    full = pl.BlockSpec((S, HW), lambda hp, i: (0, hp))
    rows = pl.BlockSpec((HPS, 1, 1, TQ), lambda hp, i: (hp, i, 0, 0))
    return _call(
        body, name="fox_fwd", args=(qn, kn, vb, proj, crow, cbc), grid=(N_HEADS // HPS, nq),
        in_specs=[qblk, full, full,
                  pl.BlockSpec((TQ, HW), lambda hp, i: (i, GOFF // HW + hp)),
                  rows, pl.BlockSpec((S, HPS * LANES), lambda hp, i: (0, hp))],
        out_specs=[qblk, qblk, rows],
        out_shape=[_sds((S, D), F32), _sds((S, D), BF16), _sds((N_HEADS, nq, 1, TQ), F32)],
        scratch_shapes=[pltpu.VMEM((2, HPS, KS, TQ), F32), pltpu.VMEM((2, HPS, KS, TQ), BF16)], rider=rider)


def _fox_bwd_pre(dh, w_out, proj, o):
    nq, rows = S // TQ, 2 * TQ
    per = rows // TQ

    def body(dh_ref, w_ref, g_ref, o_ref, do_ref, dg_ref, delta_ref):
        g = g_ref[...]
        sg = _sigmoid(g)
        dzv = _dot_nt(dh_ref[...].astype(BF16), w_ref[...])
        ov = o_ref[...]
        do = dzv * (g * sg)
        dg_ref[...] = (dzv * ov * (sg * (1.0 + g * (1.0 - sg)))).astype(BF16)
        do_ref[...] = do.astype(BF16)
        prod_t = (do * ov).T
        for h in range(N_HEADS):
            for b in range(per):
                delta_ref[h, b] = jnp.sum(prod_t[HD * h:HD * (h + 1), TQ * b:TQ * (b + 1)], axis=0, keepdims=True)

    row = pl.BlockSpec((rows, D), lambda i: (i, 0))
    return pl.pallas_call(
        body, name="fox_bwd_pre", grid=(S // rows,),
        in_specs=[row, pl.BlockSpec(w_out.shape, lambda i: (0, 0), pipeline_mode=pl.Buffered(1)),
                  pl.BlockSpec((rows, D), lambda i: (i, GOFF // D)), row],
        out_specs=[row, row, pl.BlockSpec((N_HEADS, per, 1, TQ), lambda i: (0, i, 0, 0))],
        out_shape=[_sds((S, D), BF16), _sds((S, D), BF16), _sds((N_HEADS, nq, 1, TQ), F32)],
        compiler_params=_params())(dh, w_out, proj, o)


def _fox_bwd(qn, kn, vb, dob, lse, delta, crow, cbc, rider=()):
    nq, nkb = S // TQ, S // TK

    def body(q_ref, k_ref, v_ref, do_ref, lse_ref, del_ref, cq_ref, cbc_ref,
             dk_ref, dv_ref, dcs_ref, dq_ref, dr_ref, st_s, dp_s, pt_s, ds_s, dq_acc, dr_acc):
        j = pl.program_id(1)

        @pl.when(j == 0)
        def _():
            dq_acc[...] = jnp.zeros_like(dq_acc)
            dr_acc[...] = jnp.zeros_like(dr_acc)

        kjs = [k_ref[:, HD * hh:HD * (hh + 1)] for hh in range(HPS)]
        vjs = [v_ref[:, HD * hh:HD * (hh + 1)] for hh in range(HPS)]

        def rows_of(ref, u, hh):
            off = pl.multiple_of(u * TQ, TQ)
            return ref[pl.ds(off, TQ), HD * hh:HD * (hh + 1)]

        def products(u, hh):
            st = (_dot_nt(kjs[hh], rows_of(q_ref, u, hh)) + cq_ref[hh, u]) - _widen(
                cbc_ref[:, LANES * hh:LANES * (hh + 1)])
            return st, _dot_nt(vjs[hh], rows_of(do_ref, u, hh))

        def step(u, slot, carries, masked=False):
            nxt = jnp.minimum(u + 1, nq - 1)
            for hh in range(HPS):
                st_s[1 - slot, hh], dp_s[1 - slot, hh] = products(nxt, hh)
            prev = jnp.maximum(u - 1, 0)
            dvs = [_dot_nn(pt_s[1 - slot, hh], rows_of(do_ref, prev, hh)) for hh in range(HPS)]
            dks = [_dot_nn(ds_s[1 - slot, hh], rows_of(q_ref, prev, hh)) for hh in range(HPS)]
            for hh in range(HPS):
                dq_acc[hh, prev] += _dot_tn(kjs[hh], ds_s[1 - slot, hh])
            out = []
            for hh in range(HPS):
                dk, dv, dcs = carries[hh]
                st = st_s[slot, hh]
                if masked:
                    st = jnp.where(_key_le_query((j - u) * TQ), st, -jnp.inf)
                pt = jnp.exp(st - lse_ref[hh, u])
                dst = pt * (dp_s[slot, hh] - del_ref[hh, u])
                pt_s[slot, hh] = pt.astype(BF16)
                ds_s[slot, hh] = dst.astype(BF16)
                dr_acc[hh, u] += jnp.sum(dst, axis=0, keepdims=True)
                out.append((dk + dks[hh], dv + dvs[hh], dcs + (dst[:, :LANES] + dst[:, LANES:])))
            return tuple(out)

        t0 = j // 2
        for hh in range(HPS):
            st_s[0, hh], dp_s[0, hh] = products(2 * t0, hh)
            pt_s[1, hh] = jnp.zeros((TK, TQ), BF16)
            ds_s[1, hh] = jnp.zeros((TK, TQ), BF16)
        one = (jnp.zeros((TK, HD), F32), jnp.zeros((TK, HD), F32), jnp.zeros((TK, LANES), F32))
        carries = step(2 * t0 + 1, 1, step(2 * t0, 0, (one,) * HPS, masked=True), masked=True)
        carries = lax.fori_loop(t0 + 1, nq // 2, lambda t, cr: step(2 * t + 1, 1, step(2 * t, 0, cr)), carries)
        dks, dvs = [], []
        lane = _lane_iota((TK, LANES))
        dcs_all = jnp.zeros((TK, LANES), F32)
        for hh in range(HPS):
            dk, dv, dcs = carries[hh]
            dks.append(dk + _dot_nn(ds_s[1, hh], rows_of(q_ref, nq - 1, hh)))
            dvs.append(dv + _dot_nn(pt_s[1, hh], rows_of(do_ref, nq - 1, hh)))
            dq_acc[hh, nq - 1] += _dot_tn(kjs[hh], ds_s[1, hh])
            dcs_all = jnp.where(lane == HPS * pl.program_id(0) + hh, -jnp.sum(dcs, axis=-1, keepdims=True), dcs_all)
        dcs_ref[0] = dcs_all
        dk_ref[...] = jnp.concatenate(dks, axis=-1)
        dv_ref[...] = jnp.concatenate(dvs, axis=-1).astype(BF16)

        @pl.when(j == nkb - 1)
        def _():
            for i in range(nq):
                dq_ref[TQ * i:TQ * (i + 1), :] = jnp.concatenate([dq_acc[hh, i] for hh in range(HPS)], axis=0).T
            dr_ref[...] = dr_acc[...]

    kblk = pl.BlockSpec((TK, HW), lambda hp, j: (j, hp))
    full = pl.BlockSpec((S, HW), lambda hp, j: (0, hp))
    rows = pl.BlockSpec((HPS, nq, 1, TQ), lambda hp, j: (hp, 0, 0, 0))
    cblk = pl.BlockSpec((TK, HPS * LANES), lambda hp, j: (j, hp))
    return _call(
        body, name="fox_bwd", args=(qn, kn, vb, dob, lse, delta, crow, cbc), grid=(N_HEADS // HPS, nkb),
        in_specs=[full, kblk, kblk, full, rows, rows, rows, cblk],
        out_specs=[kblk, kblk, pl.BlockSpec((1, TK, LANES), lambda hp, j: (hp, j, 0)), full, rows],
        out_shape=[_sds((S, D), F32), _sds((S, D), BF16), _sds((N_HEADS // HPS, S, LANES), F32), _sds((S, D), F32),
                   _sds((N_HEADS, nq, 1, TQ), F32)],
        scratch_shapes=[pltpu.VMEM((2, HPS, TK, TQ), F32), pltpu.VMEM((2, HPS, TK, TQ), F32),
                        pltpu.VMEM((2, HPS, TK, TQ), BF16), pltpu.VMEM((2, HPS, TK, TQ), BF16),
                        pltpu.VMEM((HPS, nq, HD, TQ), F32), pltpu.VMEM((HPS, nq, 1, TQ), F32)], rider=rider)


def _prep_a_bwd(dq, dk, dv, dgate, drow, dcs, proj, b_pad, gq, gk):
    nt = S // TM

    def body(dq_ref, dk_ref, dv_ref, dgt_ref, dr_ref, dcs_ref, xq_ref, xk_ref, f_ref, b_ref, gq_ref, gk_ref,
             o_ref, dgq_ref, dgk_ref, db_ref, carry):
        @pl.when(pl.program_id(0) == 0)
        def _():
            carry[...] = jnp.zeros_like(carry)
            dgq_ref[...] = jnp.zeros_like(dgq_ref)
            dgk_ref[...] = jnp.zeros_like(dgk_ref)
            db_ref[...] = jnp.zeros_like(db_ref)

        lane = _lane_iota((TM, LANES))
        lo_half = _half_ones()
        gq2, gk2 = _g2(gq_ref), _g2(gk_ref)
        dgq, dgk = jnp.zeros((1, LANES), F32), jnp.zeros((1, LANES), F32)
        for c in _pairs(D):
            dxq, dg = _head_norm_bwd(dq_ref[:, c] * QSCALE, xq_ref[:, c], gq2, lo_half)
            o_ref[:, c] = dxq.astype(BF16)
            dgq = dgq + dg
            dxk, dg = _head_norm_bwd(dk_ref[:, c], xk_ref[:, c], gk2, lo_half)
            o_ref[:, D + c.start:D + c.stop] = dxk.astype(BF16)
            dgk = dgk + dg
        dgq_ref[...] += dgq
        dgk_ref[...] += dgk
        o_ref[:, 2 * D:3 * D] = dv_ref[...]
        o_ref[:, GOFF:GOFF + D] = dgt_ref[...]

        dc = jnp.concatenate([dr_ref[:, 0, :], jnp.zeros((LANES - N_HEADS, TM), F32)], axis=0).T
        for group in range(N_HEADS // HPS):
            dc = dc + dcs_ref[group]
        r = lax.broadcasted_iota(jnp.int32, (TM, TM), 0)
        c = lax.broadcasted_iota(jnp.int32, (TM, TM), 1)
        tri = (c >= r).astype(F32)
        dlogf = jnp.dot(tri, dc, precision=lax.Precision.HIGHEST, preferred_element_type=F32) + carry[0:1, :]
        carry[0:1, :] = dlogf[0:1, :]
        df = dlogf * (1.0 / (1.0 + jnp.exp(f_ref[...] + b_ref[...])))
        db_ref[...] += jnp.sum(df, axis=0, keepdims=True)
        o_ref[:, FOFF:FOFF + LANES] = df.astype(BF16)
        o_ref[:, FOFF + LANES:NA] = jnp.zeros((TM, NA - FOFF - LANES), BF16)

    rev = lambda width, col: pl.BlockSpec((TM, width), lambda i: (nt - 1 - i, col))
    gspec = pl.BlockSpec((1, HD), lambda i: (0, 0))
    acc = pl.BlockSpec((1, LANES), lambda i: (0, 0))
    return pl.pallas_call(
        body, name="prep_a_bwd", grid=(nt,),
        in_specs=[rev(D, 0), rev(D, 0), rev(D, 0), rev(D, 0),
                  pl.BlockSpec((N_HEADS, None, 1, TM), lambda i: (0, nt - 1 - i, 0, 0)),
                  pl.BlockSpec((N_HEADS // HPS, TM, LANES), lambda i: (0, nt - 1 - i, 0)),
                  rev(D, 0), rev(D, 1), rev(LANES, FOFF // LANES), acc, gspec, gspec],
        out_specs=[rev(NA, 0), acc, acc, acc],
        out_shape=[_sds((S, NA), BF16)] + [_sds((1, LANES), F32)] * 3,
        scratch_shapes=[pltpu.VMEM((8, LANES), F32)],
        compiler_params=_params())(dq, dk, dv, dgate, drow, dcs, proj, proj, proj, b_pad, gq, gk)


def _head_b(x, z_a, w_out_a, g_kv, g_b, w_kv, w_in_b, gq, gk, cos2, sin2):
    nkv = w_kv.shape[1] // 2

    def body(x_ref, z_ref, wo_ref, gkv_ref, gb_ref, wkv_ref, wb_ref, gq_ref, gk_ref, c_ref, s_ref,
             h_ref, ukv_ref, ub_ref, kv_ref, pb_ref, qo_ref, ko_ref, vo_ref):
        xv = x_ref[...] + _dot_nn(z_ref[...], wo_ref[...])
        h_ref[...] = xv
        xn = xv * _rms_rinv(xv)
        ukv = (xn * gkv_ref[...]).astype(BF16)
        ub = (xn * gb_ref[...]).astype(BF16)
        ukv_ref[...] = ukv
        ub_ref[...] = ub
        kv_ref[...] = _dot_nn(ukv, wkv_ref[...])
        for lo in range(0, 2 * D, D):
            pb_ref[:, lo:lo + D] = _dot_nn(ub, wb_ref[:, lo:lo + D])
        lane = _lane_iota((RT, LANES))
        lo_half = lane < HD
        cos, sin = c_ref[...], s_ref[...]
        gq2, gk2 = _g2(gq_ref), _g2(gk_ref)
        for c in _pairs(D):
            q = pb_ref[:, c]
            qo_ref[:, c] = (_rope_fwd((q * _head_rinv(q, lo_half)) * gq2, cos, sin, lane) * QSCALE).astype(BF16)
        for c in _pairs(nkv):
            k = kv_ref[:, c]
            ko_ref[:, c] = _rope_fwd((k * _head_rinv(k, lo_half)) * gk2, cos, sin, lane).astype(BF16)
        vo_ref[...] = kv_ref[:, nkv:2 * nkv].astype(BF16)

    row = lambda width: pl.BlockSpec((RT, width), lambda i: (i, 0))
    whole = lambda arr: pl.BlockSpec(arr.shape, lambda i: (0,) * arr.ndim, pipeline_mode=pl.Buffered(1))
    return pl.pallas_call(
        body, name="head_b", grid=(S // RT,),
        in_specs=[row(D), row(D), whole(w_out_a), whole(g_kv), whole(g_b), whole(w_kv), whole(w_in_b), whole(gq),
                  whole(gk), row(LANES), row(LANES)],
        out_specs=[row(D), row(D), row(D), row(2 * nkv), row(2 * D), row(D), row(nkv), row(nkv)],
        out_shape=[_sds((S, D), F32), _sds((S, D), BF16), _sds((S, D), BF16), _sds((S, 2 * nkv), F32),
                   _sds((S, 2 * D), F32), _sds((S, D), BF16), _sds((S, nkv), BF16), _sds((S, nkv), BF16)],
        compiler_params=_params())(x, z_a, w_out_a, g_kv, g_b, w_kv, w_in_b, gq, gk, cos2, sin2)


N_KV, GRP = 4, 4


def _swa_mask(n):
    r = lax.broadcasted_iota(jnp.int32, (2 * WIN, GRP * WIN), 0)
    q = lax.broadcasted_iota(jnp.int32, (2 * WIN, GRP * WIN), 1) & (WIN - 1)
    return (r > q) & (r <= q + WIN) & ((r >= WIN) | (n > 0))


def _stack4(ref_or_val, base):
    return jnp.concatenate([ref_or_val[:, base + HD * g: base + HD * (g + 1)] for g in range(GRP)], axis=0)


def _unstack4(xt):
    return jnp.concatenate([xt[:, WIN * g:WIN * (g + 1)] for g in range(GRP)], axis=0).T


def _band(prev_ref, cur_ref, kh):
    return jnp.concatenate([prev_ref[:, HD * kh:HD * (kh + 1)], cur_ref[:, HD * kh:HD * (kh + 1)]], axis=0)


def _sink_row(s_ref, first):
    lane = _lane_iota((1, GRP * WIN))
    row = jnp.full((1, GRP * WIN), s_ref[first + GRP - 1], F32)
    for g in range(GRP - 2, -1, -1):
        row = jnp.where(lane < WIN * (g + 1), s_ref[first + g], row)
    return row


def _swa_fwd(qb, ksh, vsh, pb, sinks):
    nb = S // WIN

    def body(q_ref, kp_ref, kc_ref, vp_ref, vc_ref, g_ref, s_ref, o_ref, z_ref, lse_ref):
        n = pl.program_id(0)
        valid = _swa_mask(n)
        outs = []
        for kh in range(N_KV):
            kb, vb = _band(kp_ref, kc_ref, kh), _band(vp_ref, vc_ref, kh)
            st = jnp.where(valid, _dot_nt(kb, _stack4(q_ref, GRP * HD * kh)), -jnp.inf)
            sink = _sink_row(s_ref, GRP * kh)
            m = jnp.maximum(jnp.max(st, axis=0, keepdims=True), sink)
            pt = jnp.exp(st - m)
            l = jnp.sum(pt, axis=0, keepdims=True) + jnp.exp(sink - m)
            outs.append(_unstack4(_dot_tn(vb, pt.astype(BF16)) / l))
            lse = m + jnp.log(l)
            for g in range(GRP):
                lse_ref[GRP * kh + g, 0] = lse[:, WIN * g:WIN * (g + 1)]
        o = jnp.concatenate(outs, axis=-1)
        o_ref[...] = o
        g = g_ref[...]
        z_ref[...] = (o * (g * _sigmoid(g))).astype(BF16)

    row = pl.BlockSpec((WIN, D), lambda n: (n, 0))
    prev = pl.BlockSpec((WIN, N_KV * HD), lambda n: (jnp.maximum(n - 1, 0), 0))
    cur = pl.BlockSpec((WIN, N_KV * HD), lambda n: (n, 0))
    return pl.pallas_call(
        body, name="swa_fwd", grid=(nb,),
        in_specs=[row, prev, cur, prev, cur, pl.BlockSpec((WIN, D), lambda n: (n, 1)),
                  pl.BlockSpec(memory_space=pltpu.SMEM)],
        out_specs=[row, row, pl.BlockSpec((N_HEADS, 1, 1, WIN), lambda n: (0, n, 0, 0))],
        out_shape=[_sds((S, D), F32), _sds((S, D), BF16), _sds((N_HEADS, nb, 1, WIN), F32)],
        compiler_params=_params())(qb, ksh, ksh, vsh, vsh, pb, sinks)


def _swa_bwd(qb, ksh, vsh, dz, o, lse, pb, sinks, gq, cos2, sin2):
    nb = S // WIN

    def body(q_ref, kp_ref, kc_ref, vp_ref, vc_ref, dz_ref, o_ref, lse_ref, x_ref, g_ref, s_ref, gq_ref, c_ref, sn_ref,
             dpb_ref, dka_ref, dkb_ref, dva_ref, dvb_ref, dsink_ref, dgq_ref):
        n = pl.program_id(0)

        @pl.when(n == 0)
        def _():
            dsink_ref[...] = jnp.zeros_like(dsink_ref)
            dgq_ref[...] = jnp.zeros_like(dgq_ref)

        valid = _swa_mask(n)
        g = g_ref[...]
        sg = _sigmoid(g)
        dzv = dz_ref[...]
        ov = o_ref[...]
        do = dzv * (g * sg)
        dpb_ref[:, D:2 * D] = (dzv * ov * (sg * (1.0 + g * (1.0 - sg)))).astype(BF16)
        prod_t = (do * ov).T
        lane1 = _lane_iota((1, LANES))
        dqs, dkas, dkbs, dvas, dvbs = [], [], [], [], []
        dsink = jnp.zeros((1, LANES), F32)
        for kh in range(N_KV):
            kb, vb = _band(kp_ref, kc_ref, kh), _band(vp_ref, vc_ref, kh)
            base = GRP * HD * kh
            qs = _stack4(q_ref, base)
            dos = _stack4(do, base).astype(BF16)
            delta = jnp.concatenate(
                [jnp.sum(prod_t[base + HD * gg:base + HD * (gg + 1), :], axis=0, keepdims=True)
                 for gg in range(GRP)], axis=1)
            lse = jnp.concatenate([lse_ref[GRP * kh + gg, 0] for gg in range(GRP)], axis=1)
            st = jnp.where(valid, _dot_nt(kb, qs), -jnp.inf)
            pt = jnp.exp(st - lse)
            dst = pt * (_dot_nt(vb, dos) - delta)
            dsb = dst.astype(BF16)
            dqs.append(_unstack4(_dot_tn(kb, dsb)))
            dkband = _dot_nn(dsb, qs)
            dvband = _dot_nn(pt.astype(BF16), dos)
            dkbs.append(dkband[0:WIN, :])
            dkas.append(dkband[WIN:2 * WIN, :])
            dvbs.append(dvband[0:WIN, :])
            dvas.append(dvband[WIN:2 * WIN, :])
            ps_delta = jnp.exp(_sink_row(s_ref, GRP * kh) - lse) * delta
            for gg in range(GRP):
                val = jnp.sum(ps_delta[:, WIN * gg:WIN * (gg + 1)], axis=1, keepdims=True)
                dsink = dsink - jnp.where(lane1 == GRP * kh + gg, val, 0.0)
        dka_ref[...] = jnp.concatenate(dkas, axis=-1)
        dkb_ref[...] = jnp.concatenate(dkbs, axis=-1)
        dva_ref[...] = jnp.concatenate(dvas, axis=-1)
        dvb_ref[...] = jnp.concatenate(dvbs, axis=-1)
        dsink_ref[...] += dsink

        lane = _lane_iota((WIN, LANES))
        g2, cos, sin = _g2(gq_ref), c_ref[...], sn_ref[...]
        lo_half = _half_ones()
        dg_tot = jnp.zeros((1, LANES), F32)
        for kh in range(N_KV):
            for c in _pairs(GRP * HD):
                cols = slice(GRP * HD * kh + c.start, GRP * HD * kh + c.stop)
                dn = _rope_bwd(dqs[kh][:, c] * QSCALE, cos, sin, lane)
                dx, dg = _head_norm_bwd(dn, x_ref[:, cols], g2, lo_half)
                dpb_ref[:, cols] = dx.astype(BF16)
                dg_tot = dg_tot + dg
        dgq_ref[...] += dg_tot

    row = pl.BlockSpec((WIN, D), lambda n: (n, 0))
    prev = pl.BlockSpec((WIN, N_KV * HD), lambda n: (jnp.maximum(n - 1, 0), 0))
    cur = pl.BlockSpec((WIN, N_KV * HD), lambda n: (n, 0))
    acc = pl.BlockSpec((1, LANES), lambda n: (0, 0))
    tab = pl.BlockSpec((WIN, LANES), lambda n: (n, 0))
    return pl.pallas_call(
        body, name="swa_bwd", grid=(nb,),
        in_specs=[row, prev, cur, prev, cur, row, row, pl.BlockSpec((N_HEADS, 1, 1, WIN), lambda n: (0, n, 0, 0)),
                  row, pl.BlockSpec((WIN, D), lambda n: (n, 1)), pl.BlockSpec(memory_space=pltpu.SMEM),
                  pl.BlockSpec((1, HD), lambda n: (0, 0)), tab, tab],
        out_specs=[pl.BlockSpec((WIN, 2 * D), lambda n: (n, 0)), cur, cur, cur, cur, acc, acc],
        out_shape=[_sds((S, 2 * D), BF16)] + [_sds((S, 256), F32)] * 4 + [_sds((1, LANES), F32)] * 2,
        compiler_params=_params())(qb, ksh, ksh, vsh, vsh, dz, o, lse, pb, pb, sinks, gq, cos2, sin2)


def _prep_kv_bwd(dka, dkb, dva, dvb, kv, gk, cos2, sin2):
    nt = S // RB
    per = RB // WIN

    def shifted(cur_ref, nxt_ref, has_next):
        return jnp.concatenate([cur_ref[WIN:RB, :], jnp.where(has_next, nxt_ref[...], 0.0)], axis=0)

    def body(dka_ref, dkb_ref, dkn_ref, dva_ref, dvb_ref, dvn_ref, x_ref, g_ref, c_ref, s_ref, o_ref, dgk_ref):
        i, j = pl.program_id(0), pl.program_id(1)

        @pl.when((i == 0) & (j == 0))
        def _():
            dgk_ref[...] = jnp.zeros_like(dgk_ref)

        has_next = i < nt - 1

        @pl.when(j == 0)
        def _():
            lane = _lane_iota((RB, LANES))
            g2, cos, sin = _g2(g_ref), c_ref[...], s_ref[...]
            dy_all = dka_ref[...] + shifted(dkb_ref, dkn_ref, has_next)
            dg_tot = jnp.zeros((1, LANES), F32)
            lo_half = _half_ones()
            for c in _pairs(CB):
                dn = _rope_bwd(dy_all[:, c], cos, sin, lane)
                dx, dg = _head_norm_bwd(dn, x_ref[:, c], g2, lo_half)
                o_ref[:, c] = dx.astype(BF16)
                dg_tot = dg_tot + dg
            dgk_ref[...] += dg_tot

        @pl.when(j == 1)
        def _():
            o_ref[...] = (dva_ref[...] + shifted(dvb_ref, dvn_ref, has_next)).astype(BF16)

    cur = pl.BlockSpec((RB, CB), lambda i, j: (i, 0))
    nxt = pl.BlockSpec((WIN, CB), lambda i, j: (jnp.minimum(per * (i + 1), S // WIN - 1), 0))
    tab = pl.BlockSpec((RB, LANES), lambda i, j: (i, 0))
    return pl.pallas_call(
        body, name="prep_kv_bwd", grid=(nt, 2),
        in_specs=[cur, cur, nxt, cur, cur, nxt, cur, pl.BlockSpec((1, HD), lambda i, j: (0, 0)), tab, tab],
        out_specs=[pl.BlockSpec((RB, CB), lambda i, j: (i, j)), pl.BlockSpec((1, LANES), lambda i, j: (0, 0))],
        out_shape=[_sds((S, 2 * CB), BF16), _sds((1, LANES), F32)],
        compiler_params=_params())(dka, dkb, dkb, dva, dvb, dvb, kv, gk, cos2, sin2)


def _out_b_loss(z, w_out, h1, tgt):
    tm = 2 * RT

    def body(z_ref, w_ref, h_ref, t_ref, dy_ref, l_ref):
        @pl.when(pl.program_id(0) == 0)
        def _():
            l_ref[...] = jnp.zeros_like(l_ref)

        e = (h_ref[...] + _dot_nn(z_ref[...], w_ref[...])) - t_ref[...]
        dy_ref[...] = e * (1.0 / D)
        l_ref[...] += jnp.sum(jnp.sum(e * e, axis=-1, keepdims=True), axis=0, keepdims=True)

    row = pl.BlockSpec((tm, D), lambda i: (i, 0))
    return pl.pallas_call(
        body, name="out_b_loss", grid=(S // tm,),
        in_specs=[row, pl.BlockSpec(w_out.shape, lambda i: (0, 0), pipeline_mode=pl.Buffered(1)), row, row],
        out_specs=[row, pl.BlockSpec((1, LANES), lambda i: (0, 0))],
        out_shape=[_sds((S, D), F32), _sds((1, LANES), F32)], compiler_params=_params())(z, w_out, h1, tgt)


def _du_a_rms_bwd(dproj, wa, x, g, dres, after):
    tm, tk = 1024, NA // 2
    nk = NA // tk

    def body(a_ref, b_ref, x_ref, g_ref, dr_ref, after_ref, dx_ref, dg_ref, acc):
        i, kk = pl.program_id(0), pl.program_id(1)

        @pl.when((i == 0) & (kk == 0))
        def _():
            dg_ref[...] = jnp.zeros_like(dg_ref)

        p = _dot_nt(a_ref[...], b_ref[...])

        @pl.when(kk == 0)
        def _():
            acc[...] = p

        @pl.when(kk == nk - 1)
        def _():
            dx, dg = _rms_bwd_core(acc[...] + p, x_ref[...], g_ref[...])
            dx_ref[...] = dr_ref[...] + dx
            dg_ref[...] += dg

    assert nk == 2
    row = pl.BlockSpec((tm, D), lambda i, kk: (i, 0))
    vec = pl.BlockSpec((1, D), lambda i, kk: (0, 0))
    return pl.pallas_call(
        body, name="du_a_rms_bwd", grid=(S // tm, nk),
        in_specs=[pl.BlockSpec((tm, tk), lambda i, kk: (i, kk)), pl.BlockSpec((D, tk), lambda i, kk: (0, kk)),
                  row, vec, row, pl.BlockSpec(after.shape, lambda i, kk: (0, 0))],
        out_specs=[row, vec], out_shape=[_sds((S, D), F32), _sds((1, D), F32)],
        scratch_shapes=[pltpu.VMEM((tm, D), F32)], compiler_params=_params())(dproj, wa, x, g, dres, after)


def _du_b_rms_bwd(dpb, w_in_b, dkv, w_kv, h1, g_b, g_kv, dy):
    tm = 2 * RT

    def body(ab_ref, wb_ref, akv_ref, wkv_ref, x_ref, gb_ref, gkv_ref, dy_ref, dh_ref, dgb_ref, dgkv_ref):
        @pl.when(pl.program_id(0) == 0)
        def _():
            dgb_ref[...] = jnp.zeros_like(dgb_ref)
            dgkv_ref[...] = jnp.zeros_like(dgkv_ref)

        x = x_ref[...]
        dx1, dg1 = _rms_bwd_core(_dot_nt(ab_ref[...], wb_ref[...]), x, gb_ref[...])
        dx2, dg2 = _rms_bwd_core(_dot_nt(akv_ref[...], wkv_ref[...]), x, gkv_ref[...])
        dh_ref[...] = dy_ref[...] + dx1 + dx2
        dgb_ref[...] += dg1
        dgkv_ref[...] += dg2

    row = lambda width: pl.BlockSpec((tm, width), lambda i: (i, 0))
    whole = lambda arr: pl.BlockSpec(arr.shape, lambda i: (0, 0), pipeline_mode=pl.Buffered(1))
    vec = pl.BlockSpec((1, D), lambda i: (0, 0))
    return pl.pallas_call(
        body, name="du_b_rms_bwd", grid=(S // tm,),
        in_specs=[row(dpb.shape[1]), whole(w_in_b), row(dkv.shape[1]), whole(w_kv), row(D), vec, vec, row(D)],
        out_specs=[row(D), vec, vec], out_shape=[_sds((S, D), F32), _sds((1, D), F32), _sds((1, D), F32)],
        compiler_params=_params())(dpb, w_in_b, dkv, w_kv, h1, g_b, g_kv, dy)


GATHER_CHUNKS = 4


def _gather_first(w_in_a, w_out_a, w_kv, w_in_b, w_out_b, norm_a_g):
    rows, cols = w_in_a.shape[-2:]
    groups = rows // LANES
    cols_pad = -(-cols // LANES) * LANES

    def body(wia_ref, woa_ref, wkv_ref, wib_ref, wob_ref, ga_ref,
             wa_pad, ga_g, woa_s, wkv_s, wib_s, wob_s, wa_s, st_a, st_oa, st_kv, st_ib, st_ob, plane_t, wa_g,
             load_sems, *sems):
        sources = [wia_ref, woa_ref.at[0], wkv_ref, wib_ref.at[0], wob_ref.at[0]]
        stages = [st_a, st_oa, st_kv, st_ib, st_ob]
        loads = [pltpu.make_async_copy(src, dst, load_sems.at[i]) for i, (src, dst) in enumerate(zip(sources, stages))]
        for cp in loads:
            cp.start()
        loads[0].wait()
        plane_t[cols_pad - LANES:, :] = jnp.zeros((LANES, LANES), F32)
        for j in range(groups):
            for c0 in range(0, cols, 64):
                n = min(64, cols - c0)
                plane_t[c0:c0 + n, :] = st_a[pl.ds(c0 * groups + j, n, stride=groups), :]
            for c0 in range(0, cols, LANES):
                n = min(LANES, cols - c0)
                wa_s[j * LANES:(j + 1) * LANES, c0:c0 + n] = plane_t[c0:c0 + LANES, :].T[:, :n].astype(BF16)

        def cast_the_rest():
            for cp, stage, out in zip(loads[1:], stages[1:], [woa_s, wkv_s, wib_s, wob_s]):
                cp.wait()
                out[...] = stage[...].astype(BF16)

        chunks = [pl.ds(r0, rows // GATHER_CHUNKS) for r0 in range(0, rows, rows // GATHER_CHUNKS)]
        _gather_two_level(
            [wa_s.at[rc] for rc in chunks] + [ga_ref],
            [lambda dev, rc=rc: wa_g.at[dev, rc] for rc in chunks] + [lambda dev: ga_g.at[dev]],
            sems, meanwhile=cast_the_rest)
        for r0 in range(0, rows, TM):
            wa_pad[r0:r0 + TM, FOFF + N_HEADS:NA] = jnp.zeros((TM, NA - FOFF - N_HEADS), BF16)
            for d, src, dst, width in _shard_pieces():
                wa_pad[r0:r0 + TM, dst:dst + width] = wa_g[d, r0:r0 + TM, src:src + width]

    vmem = pl.BlockSpec(memory_space=pltpu.VMEM)
    anyspec = pl.BlockSpec(memory_space=pl.ANY)
    shard = lambda w: _sds(w.shape[-2:], BF16)
    stage = lambda w: pltpu.VMEM(w.shape[-2:], F32)
    return pl.pallas_call(
        body, name="gather_first", in_specs=[anyspec] * 5 + [vmem],
        out_specs=[vmem, anyspec, vmem, vmem, vmem, vmem],
        out_shape=[_sds((rows, NA), BF16), _sds((N_DEV,) + norm_a_g.shape, F32),
                   shard(w_out_a), shard(w_kv), shard(w_in_b), shard(w_out_b)],
        scratch_shapes=[pltpu.VMEM(w_in_a.shape[-2:], BF16), pltpu.VMEM((cols * groups, LANES), F32), stage(w_out_a),
                        stage(w_kv), stage(w_in_b), stage(w_out_b), pltpu.VMEM((cols_pad, LANES), F32),
                        pltpu.VMEM((N_DEV,) + w_in_a.shape[-2:], BF16),
                        pltpu.SemaphoreType.DMA((5,))] + _exchange_sems(GATHER_CHUNKS + 1),
        compiler_params=pltpu.CompilerParams(vmem_limit_bytes=VMEM_LIMIT, has_side_effects=True))(
            _entry_view(w_in_a).reshape(cols * groups, LANES), w_out_a, w_kv, w_in_b, w_out_b, norm_a_g)


def _padded_col(c):
    if c < RAW_F:
        return c
    return FOFF + (c - RAW_F) if c < RAW_G else GOFF + (c - RAW_G)


def _shard_pieces():
    width = NA_RAW // N_DEV
    pieces = []
    for d in range(N_DEV):
        cuts = [width * d] + [c for c in (RAW_F, RAW_G) if width * d < c < width * (d + 1)] + [width * (d + 1)]
        for lo, hi in zip(cuts[:-1], cuts[1:]):
            pieces.append((d, lo - width * d, _padded_col(lo), hi - lo))
    return pieces


def _reshard_pair_reduce(dwa):
    n_chip, nt = N_DEV // 2, D // TM
    width = NA_RAW // N_DEV

    def body(g_ref, o_ref, slots_v, sib_v, send_sems, recv_sems):
        i = pl.program_id(0)
        x, y, c = lax.axis_index("x"), lax.axis_index("y"), lax.axis_index("c")

        def tile_rows(tile):
            return pl.ds(tile * TM if isinstance(tile, int) else pl.multiple_of(tile * TM, TM), TM)

        def give(j, tile):
            return pltpu.make_async_remote_copy(
                src_ref=slots_v.at[2 * j + 1 - c, tile_rows(tile)], dst_ref=sib_v.at[j, tile_rows(tile)],
                send_sem=send_sems.at[j, tile], recv_sem=recv_sems.at[j, tile], device_id=(x, y, 1 - c),
                device_id_type=pl.DeviceIdType.MESH)

        for d, src, dst, w in _shard_pieces():
            slots_v[d, tile_rows(i), src:src + w] = g_ref[:, dst:dst + w]
        for j in range(n_chip):
            give(j, i).start()

        @pl.when(i == nt - 1)
        def _():
            for tile in range(nt):
                for j in range(n_chip):
                    give(j, tile).wait()
            for j in range(n_chip):
                for r0 in range(0, D, 64):
                    mine = slots_v[2 * j + c, r0:r0 + 64, :].astype(F32)
                    o_ref[j, r0:r0 + 64, :] = (mine + sib_v[j, r0:r0 + 64, :].astype(F32)).astype(BF16)

    half = _sds((n_chip, D, width), dwa.dtype)
    return pl.pallas_call(
        body, name="reshard_pair_reduce", grid=(nt,), in_specs=[pl.BlockSpec((TM, NA), lambda i: (i, 0))],
        out_specs=pl.BlockSpec(half.shape, lambda i: (0, 0, 0)), out_shape=half,
        scratch_shapes=[pltpu.VMEM((N_DEV, D, width), dwa.dtype), pltpu.VMEM(half.shape, dwa.dtype),
                        pltpu.SemaphoreType.DMA((n_chip, nt)), pltpu.SemaphoreType.DMA((n_chip, nt))],
        compiler_params=pltpu.CompilerParams(vmem_limit_bytes=VMEM_LIMIT, has_side_effects=True))(dwa)


CHIP_FLIPS = (2, 4, 6)


def _chip_exchange_start(partial):
    n = len(CHIP_FLIPS)

    def body(p_ref, land_ref, *rest):
        sends, recvs, token = rest[:n], rest[n:2 * n], rest[2 * n + 2]
        x, y, c = lax.axis_index("x"), lax.axis_index("y"), lax.axis_index("c")
        me = 4 * x + 2 * y + c
        for idx, k in enumerate(CHIP_FLIPS):
            pltpu.make_async_remote_copy(
                src_ref=p_ref.at[(me ^ k) >> 1], dst_ref=land_ref.at[me >> 1], send_sem=sends[idx],
                recv_sem=recvs[idx], device_id=(x ^ ((k >> 2) & 1), y ^ ((k >> 1) & 1), c),
                device_id_type=pl.DeviceIdType.MESH).start()
        token[...] = jnp.zeros_like(token)

    hbm = pl.BlockSpec(memory_space=pltpu.HBM)
    sem = pl.BlockSpec(memory_space=pltpu.SEMAPHORE)
    buf = pltpu.HBM(partial.shape, partial.dtype)
    return pl.pallas_call(
        body, name="chip_exchange_start",
        out_shape=(pltpu.SemaphoreType.DMA(()),) * (2 * n) + (buf, buf, _sds((8, LANES), F32)),
        in_specs=(hbm, hbm), out_specs=(sem,) * (2 * n) + (hbm, hbm, pl.BlockSpec(memory_space=pltpu.VMEM)),
        input_output_aliases={0: 2 * n, 1: 2 * n + 1},
        compiler_params=pltpu.CompilerParams(has_side_effects=pltpu.SideEffectType.DATAFLOW_SIDE_EFFECTING))(
            pltpu.with_memory_space_constraint(partial, pltpu.HBM),
            pltpu.with_memory_space_constraint(lax.empty(partial.shape, partial.dtype), pltpu.HBM))


def _slab_peer(x, y, c, k):
    return (x ^ ((k >> 2) & 1), y ^ ((k >> 1) & 1), c ^ (k & 1))


def _chip_wait_slab_start(chip_started, slab, after):
    nc, n, na = len(CHIP_FLIPS), N_DEV - 1, len(after)
    chip_sems, (p_thru, land_thru) = chip_started[:2 * nc], chip_started[2 * nc:2 * nc + 2]

    def body(p_ref, land_ref, *rest):
        chip_sends, chip_recvs, s_ref, sl_ref = rest[:nc], rest[nc:2 * nc], rest[2 * nc], rest[2 * nc + 1]
        outs = rest[2 * nc + 2 + na:]
        sends, recvs, own_sem = outs[2:2 + n], outs[2 + n:2 + 2 * n], outs[4 + 2 * n]
        x, y, c = lax.axis_index("x"), lax.axis_index("y"), lax.axis_index("c")
        me = 4 * x + 2 * y + c
        for idx, k in enumerate(CHIP_FLIPS):
            copy = pltpu.make_async_remote_copy(
                src_ref=p_ref.at[(me ^ k) >> 1], dst_ref=land_ref.at[(me ^ k) >> 1], send_sem=chip_sends[idx],
                recv_sem=chip_recvs[idx], device_id=(x ^ ((k >> 2) & 1), y ^ ((k >> 1) & 1), c),
                device_id_type=pl.DeviceIdType.MESH)
            copy.wait_send()
            copy.wait_recv()
        for k in range(1, N_DEV):
            pltpu.make_async_remote_copy(
                src_ref=s_ref, dst_ref=sl_ref.at[me], send_sem=sends[k - 1], recv_sem=recvs[k - 1],
                device_id=_slab_peer(x, y, c, k), device_id_type=pl.DeviceIdType.MESH).start()
        own = pltpu.make_async_copy(s_ref, sl_ref.at[me], own_sem)
        own.start()
        own.wait()

    hbm = pl.BlockSpec(memory_space=pltpu.HBM)
    sem = pl.BlockSpec(memory_space=pltpu.SEMAPHORE)
    buf = pltpu.HBM(p_thru.shape, p_thru.dtype)
    land = (N_DEV,) + slab.shape
    outs = pl.pallas_call(
        body, name="chip_wait_slab_start",
        out_shape=(buf, buf) + (pltpu.SemaphoreType.DMA(()),) * (2 * n) + (
            pltpu.HBM(slab.shape, slab.dtype), pltpu.HBM(land, slab.dtype)),
        in_specs=(hbm, hbm) + (sem,) * (2 * nc) + (hbm, hbm) + (pl.BlockSpec(memory_space=pl.ANY),) * na,
        out_specs=(hbm, hbm) + (sem,) * (2 * n) + (hbm, hbm),
        input_output_aliases={0: 0, 1: 1, 2 * nc + 2: 2 * n + 2, 2 * nc + 3: 2 * n + 3},
        scratch_shapes=[pltpu.SemaphoreType.DMA(())],
        compiler_params=pltpu.CompilerParams(has_side_effects=pltpu.SideEffectType.DATAFLOW_SIDE_EFFECTING))(
            p_thru, land_thru, *chip_sems, pltpu.with_memory_space_constraint(slab, pltpu.HBM),
            pltpu.with_memory_space_constraint(lax.empty(land, slab.dtype), pltpu.HBM), *after)
    return outs[0], outs[1], tuple(outs[2:])


def _gather_slab_wait(started, after):
    n = N_DEV - 1
    sems, (s_thru, land_thru) = started[:2 * n], started[2 * n:2 * n + 2]

    def body(s_ref, land_ref, *rest):
        sends, recvs = rest[:n], rest[n:2 * n]
        x, y, c = lax.axis_index("x"), lax.axis_index("y"), lax.axis_index("c")
        me = 4 * x + 2 * y + c
        for k in range(1, N_DEV):
            copy = pltpu.make_async_remote_copy(
                src_ref=s_ref, dst_ref=land_ref.at[me ^ k], send_sem=sends[k - 1], recv_sem=recvs[k - 1],
                device_id=_slab_peer(x, y, c, k), device_id_type=pl.DeviceIdType.MESH)
            copy.wait_send()
            copy.wait_recv()

    hbm = pl.BlockSpec(memory_space=pltpu.HBM)
    sem = pl.BlockSpec(memory_space=pltpu.SEMAPHORE)
    return pl.pallas_call(
        body, name="gather_slab_wait",
        out_shape=(pltpu.HBM(s_thru.shape, s_thru.dtype), pltpu.HBM(land_thru.shape, land_thru.dtype)),
        in_specs=(hbm, hbm) + (sem,) * (2 * n) + (pl.BlockSpec(memory_space=pl.ANY),) * len(after),
        out_specs=(hbm, hbm), input_output_aliases={0: 0, 1: 1},
        compiler_params=pltpu.CompilerParams(has_side_effects=pltpu.SideEffectType.DATAFLOW_SIDE_EFFECTING))(
            s_thru, land_thru, *sems, *after)[1]


def _adamw(w, g, m, v):
    m = ADAM_B1 * m + (1.0 - ADAM_B1) * g
    v = ADAM_B2 * v + (1.0 - ADAM_B2) * (g * g)
    m_hat = m / (1.0 - ADAM_B1 ** ADAM_STEP)
    v_hat = v / (1.0 - ADAM_B2 ** ADAM_STEP)
    delta = -ADAM_LR * (m_hat / (jnp.sqrt(v_hat) + ADAM_EPS) + ADAM_WD * w)
    return delta, m, v


def _sum_adamw(recv, w, m, v, name, after=None):
    lead = w.ndim - 2
    rows, cols = w.shape[-2:]
    tr = 256 if rows % 256 == 0 else 128
    n_slots = recv.shape[0]
    extra = [] if after is None else [after]

    def body(*refs):
        r_ref = refs[0]
        w_ref, m_ref, v_ref, g_ref, d_ref, nm_ref, nv_ref = refs[1 + len(extra):]
        g = None
        for slot in range(n_slots):
            g = r_ref[slot].astype(F32) if g is None else g + r_ref[slot].astype(F32)
        g_ref[...] = g
        d_ref[...], nm_ref[...], nv_ref[...] = _adamw(w_ref[...], g, m_ref[...], v_ref[...])

    blk = pl.BlockSpec((None,) * lead + (tr, cols), lambda i: (0,) * lead + (i, 0))
    slots = pl.BlockSpec((n_slots, tr, cols), lambda i: (0, i, 0))
    return pl.pallas_call(
        body, name=name, grid=(rows // tr,),
        in_specs=[slots] + [pl.BlockSpec(a.shape, lambda i: (0, 0)) for a in extra] + [blk, blk, blk],
        out_specs=[blk] * 4, out_shape=[_sds(w.shape, F32)] * 4, compiler_params=_params())(recv, *extra, w, m, v)


def _entry_view(a):
    _, rows, cols = a.shape
    return jnp.transpose(a, (2, 0, 1)).reshape(cols, rows // LANES, LANES)


def _from_entry_view(a):
    cols, groups, lanes = a.shape
    return jnp.transpose(a, (1, 2, 0)).reshape(1, groups * lanes, cols)


def _sum_adamw_entry_view(landing, own, w, m, v, name):
    n_slots, rows, cols = landing.shape
    groups = rows // LANES
    cols_pad = -(-cols // LANES) * LANES

    def body(r_ref, own_ref, w_ref, m_ref, v_ref, g_ref, d_ref, nm_ref, nv_ref, pad_ref, gt_ref):
        j = pl.program_id(0)
        chip = (4 * lax.axis_index("x") + 2 * lax.axis_index("y") + lax.axis_index("c")) >> 1
        pad_ref[:, cols_pad - LANES:] = jnp.zeros((LANES, LANES), F32)
        for r0 in range(0, LANES, 32):
            g = None
            for slot in range(n_slots):
                part = jnp.where(chip == slot, own_ref[slot, r0:r0 + 32], r_ref[slot, r0:r0 + 32])
                g = part.astype(F32) if g is None else g + part.astype(F32)
            pad_ref[r0:r0 + 32, :cols] = g
        for c0 in range(0, cols_pad, LANES):
            gt_ref[c0:c0 + LANES, :] = pad_ref[:, c0:c0 + LANES].T
        def update_plane(plane):
            for c0 in range(0, cols, 64):
                n = min(64, cols - c0)
                at = pl.ds(c0 * groups + plane, n, stride=groups)
                gt = gt_ref[c0:c0 + n, :]
                g_ref[at, :] = gt
                d_ref[at, :], nm_ref[at, :], nv_ref[at, :] = _adamw(w_ref[at, :], gt, m_ref[at, :], v_ref[at, :])

        for plane in range(groups):
            pl.when(j == plane)(lambda plane=plane: update_plane(plane))

    flat = lambda a: _entry_view(a).reshape(cols * groups, LANES)
    whole = pl.BlockSpec((cols * groups, LANES), lambda j: (0, 0))
    slots = pl.BlockSpec((n_slots, LANES, cols), lambda j: (0, j, 0))
    outs = pl.pallas_call(
        body, name=name, grid=(groups,), in_specs=[slots, slots, whole, whole, whole], out_specs=[whole] * 4,
        out_shape=[_sds((cols * groups, LANES), F32)] * 4,
        scratch_shapes=[pltpu.VMEM((LANES, cols_pad), F32), pltpu.VMEM((cols_pad, LANES), F32)],
        compiler_params=_params())(landing, own, flat(w), flat(m), flat(v))
    return [_from_entry_view(o.reshape(cols, groups, LANES)) for o in outs], outs[0]


SLAB_ROWS = 16
SLOT = {"kv_norm_g": (8, 0, D), "norm_b_g": (9, 0, D), "b_forget": (10, 0, 16), "qnorm_a_g": (10, 128, HD),
        "knorm_a_g": (10, 256, HD), "knorm_b_g": (10, 384, HD), "qnorm_b_g": (10, 512, HD), "sinks": (10, 640, 16)}
SMALL = ["norm_a_g", "b_forget", "qnorm_a_g", "knorm_a_g", "kv_norm_g", "knorm_b_g", "norm_b_g", "qnorm_b_g", "sinks"]


LOSS_ROW = 11


def _pack_small(dg_a, dg_kv, dg_b, db_f, dgq_a, dgk_a, dgk_b, dgq_b, dsinks, lsum):
    def fold(ref):
        return ref[:, 0:HD] + ref[:, HD:2 * HD]

    def body(dga_ref, dgkv_ref, dgb_ref, dbf_ref, dgqa_ref, dgka_ref, dgkb_ref, dgqb_ref, dsk_ref, ls_ref, slab_ref):
        slab_ref[...] = jnp.zeros_like(slab_ref)
        for r in range(N_DEV):
            slab_ref[r:r + 1, 0:LANES] = dga_ref[:, LANES * r:LANES * (r + 1)]
        slab_ref[8:9, :] = dgkv_ref[...]
        slab_ref[9:10, :] = dgb_ref[...]
        slab_ref[10:11, 0:LANES] = dbf_ref[...]
        slab_ref[10:11, 128:128 + HD] = fold(dgqa_ref)
        slab_ref[10:11, 256:256 + HD] = fold(dgka_ref)
        slab_ref[10:11, 384:384 + HD] = fold(dgkb_ref)
        slab_ref[10:11, 512:512 + HD] = fold(dgqb_ref)
        slab_ref[10:11, 640:640 + LANES] = dsk_ref[...]
        slab_ref[LOSS_ROW:LOSS_ROW + 1, 0:LANES] = ls_ref[...]

    return pl.pallas_call(body, name="pack_small", out_shape=_sds((SLAB_ROWS, D), F32), compiler_params=_params())(
        dg_a, dg_kv, dg_b, db_f, dgq_a, dgk_a, dgk_b, dgq_b, dsinks, lsum)


def _small_adamw(recv, ws, ms, vs):
    k = len(SMALL)

    def body(*refs):
        r_ref = refs[0]
        w_refs, m_refs, v_refs = refs[1:1 + k], refs[1 + k:1 + 2 * k], refs[1 + 2 * k:1 + 3 * k]
        outs = refs[1 + 3 * k:1 + 7 * k]
        loss_ref, tot = refs[1 + 7 * k], refs[2 + 7 * k]
        g = r_ref[0]
        for dev in range(1, N_DEV):
            g = g + r_ref[dev]
        tot[...] = g
        loss_ref[...] = tot[LOSS_ROW:LOSS_ROW + 1, 0:LANES] * (0.5 / D)
        me = 4 * lax.axis_index("x") + 2 * lax.axis_index("y") + lax.axis_index("c")
        for p, name in enumerate(SMALL):
            if name == "norm_a_g":
                mine = lax.broadcasted_iota(jnp.int32, (N_DEV, LANES), 0) == me
                gp = jnp.sum(jnp.where(mine, tot[0:N_DEV, 0:LANES], 0.0), axis=0, keepdims=True)
            else:
                row, lo, width = SLOT[name]
                gp = tot[row:row + 1, lo:lo + width]
            d, nm, nv = _adamw(w_refs[p][...], gp, m_refs[p][...], v_refs[p][...])
            outs[p][...] = gp
            outs[k + p][...] = d
            outs[2 * k + p][...] = nm
            outs[3 * k + p][...] = nv

    shapes = [_sds(w.shape, F32) for w in ws]
    return pl.pallas_call(body, name="small_adamw", out_shape=shapes * 4 + [_sds((1, LANES), F32)],
                          scratch_shapes=[pltpu.VMEM((SLAB_ROWS, D), F32)],
                          compiler_params=_params())(recv, *ws, *ms, *vs)


def _rope_tables(positions):
    inv_freq = jnp.power(jnp.float32(ROPE_THETA), -jnp.arange(0, ROT, 2, dtype=F32) / ROT)
    ang = positions.astype(F32)[:, None] * inv_freq[None, :]
    cos, sin = jnp.cos(ang), jnp.sin(ang)
    c64 = jnp.concatenate([cos, cos, jnp.ones((S, HD - ROT), F32)], axis=-1)
    s64 = jnp.concatenate([-sin, sin, jnp.zeros((S, HD - ROT), F32)], axis=-1)
    return jnp.tile(c64, (1, 2)), jnp.tile(s64, (1, 2))


def _local_step(x, tgt, positions, g_a, wa, b_forget, gq_a, gk_a, g_kv, gk_b, g_b, gq_b, sinks,
                woa_s, wkv_s, wib_s, wob_s, adamw_others):
    nq = S // TQ
    cos2, sin2 = _rope_tables(positions)
    b_pad = jnp.pad(b_forget, ((0, 0), (0, LANES - N_HEADS)))

    u_a, proj, qn, kn, vb, crow, cbc = _head_a(x, g_a, wa, gq_a, gk_a, b_pad)
    o_a, z_a, lse_a, woa_g, wkv_g, w_in_b, wob_g = _fox_fwd(
        qn, kn, vb, proj, crow, cbc,
        rider=[("gather_rows", woa_s), ("gather_rows", wkv_s), ("gather_cols", wib_s), ("gather_rows", wob_s)])
    w_out_a, w_kv, w_out_b = woa_g.reshape(D, D), wkv_g.reshape(D, 512), wob_g.reshape(D, D)
    h1, u_kv, u_b, kv, pb, qb, ksh, vsh = _head_b(x, z_a, w_out_a, g_kv, g_b, w_kv, w_in_b, gq_b, gk_b, cos2, sin2)
    sinks1 = sinks.reshape(N_HEADS)
    o_b, z_b, lse_b = _swa_fwd(qb, ksh, vsh, pb, sinks1)
    dy, lsum = _out_b_loss(z_b, w_out_b, h1, tgt)
    dw_out_b = _mm(z_b, dy, "tn", 512, 512, S, out_dtype=BF16, name="mm_dw_out_b")
    dz_b = _mm(dy, w_out_b, "nt", 1024, 512, D, name="mm_dz_b")
    dpb, dka, dkb, dva, dvb, dsinks, dgq_b = _swa_bwd(qb, ksh, vsh, dz_b, o_b, lse_b, pb, sinks1, gq_b, cos2, sin2)
    dkv, dgk_b = _prep_kv_bwd(dka, dkb, dva, dvb, kv, gk_b, cos2, sin2)
    dw_in_b = _mm(u_b, dpb, "tn", 512, 512, S, out_dtype=BF16, name="mm_dw_in_b")
    dw_kv = _mm(u_kv, dkv, "tn", 512, 512, S, out_dtype=BF16, name="mm_dw_kv")
    dh1, dg_b, dg_kv = _du_b_rms_bwd(dpb, w_in_b, dkv, w_kv, h1, g_b, g_kv, dy)
    dw_out_a = _mm(z_a, dh1, "tn", 512, 512, S, out_dtype=BF16, name="mm_dw_out_a")
    do_a, dgate_a, delta_a = _fox_bwd_pre(dh1, w_out_a, proj, o_a)
    dk_a, dv_a, dcs, dq_a, drow, r_wob, r_wib, r_wkv, r_woa = _fox_bwd(
        qn, kn, vb, do_a, lse_a, delta_a, crow, cbc,
        rider=[("a2a_rows", dw_out_b), ("a2a_cols", dw_in_b), ("a2a_rows", dw_kv), ("a2a_rows", dw_out_a)])
    dproj, dgq_a, dgk_a, db_f = _prep_a_bwd(dq_a, dk_a, dv_a, dgate_a, drow, dcs, proj, b_pad, gq_a, gk_a)
    dwa = _mm(u_a, dproj, "tn", 1024, 256, S, out_dtype=BF16, name="mm_dw_in_a")
    partial = _reshard_pair_reduce(dwa)
    started = _chip_exchange_start(partial)
    dx, dg_a = _du_a_rms_bwd(dproj, wa, x, g_a, dh1, after=started[-1])
    others = adamw_others(dict(w_out_a=r_woa, w_kv=r_wkv, w_in_b=r_wib, w_out_b=r_wob), dg_a)
    slab = _pack_small(dg_a, dg_kv, dg_b, db_f, dgq_a, dgk_a, dgk_b, dgq_b, dsinks, lsum)
    partial, landed, slab_started = _chip_wait_slab_start(started, slab, [res[0] for res in others.values()])
    return dx, (landed, partial), others, slab_started


def kernel(x, positions, norm_a_g, w_in_a, b_forget, qnorm_a_g, knorm_a_g, w_out_a, kv_norm_g, w_kv, knorm_b_g, norm_b_g, w_in_b, qnorm_b_g, sinks, w_out_b, loss_target, m_norm_a_g, m_w_in_a, m_b_forget, m_qnorm_a_g, m_knorm_a_g, m_w_out_a, m_kv_norm_g, m_w_kv, m_knorm_b_g, m_norm_b_g, m_w_in_b, m_qnorm_b_g, m_sinks, m_w_out_b, v_norm_a_g, v_w_in_a, v_b_forget, v_qnorm_a_g, v_knorm_a_g, v_w_out_a, v_kv_norm_g, v_w_kv, v_knorm_b_g, v_norm_b_g, v_w_in_b, v_qnorm_b_g, v_sinks, v_w_out_b):
    wa, ga_g, woa_s, wkv_s, wib_s, wob_s = _gather_first(w_in_a, w_out_a, w_kv, w_in_b, w_out_b, norm_a_g)
    state = dict(w_in_a=(w_in_a, m_w_in_a, v_w_in_a), w_out_a=(w_out_a, m_w_out_a, v_w_out_a),
                 w_kv=(w_kv, m_w_kv, v_w_kv), w_in_b=(w_in_b, m_w_in_b, v_w_in_b),
                 w_out_b=(w_out_b, m_w_out_b, v_w_out_b))

    def adamw_others(landed, after):
        return {n: _sum_adamw(r, *state[n], "adamw_" + n, after=after) for n, r in landed.items()}

    dx, r_wa, big, slab_started = _local_step(
        x[0], loss_target[0], positions, ga_g.reshape(1, D), wa, b_forget, qnorm_a_g, knorm_a_g,
        kv_norm_g.reshape(1, D), knorm_b_g.reshape(1, HD), norm_b_g, qnorm_b_g, sinks, woa_s, wkv_s, wib_s, wob_s,
        adamw_others)
    big["w_in_a"], done = _sum_adamw_entry_view(*r_wa, *state["w_in_a"], "adamw_w_in_a")
    slab_g = _gather_slab_wait(slab_started, [done])

    r2 = lambda a: a.reshape(1, -1)
    small_w = dict(norm_a_g=norm_a_g, b_forget=b_forget, qnorm_a_g=qnorm_a_g, knorm_a_g=knorm_a_g,
                   kv_norm_g=kv_norm_g, knorm_b_g=knorm_b_g, norm_b_g=norm_b_g, qnorm_b_g=qnorm_b_g, sinks=sinks)
    small_m = dict(norm_a_g=m_norm_a_g, b_forget=m_b_forget, qnorm_a_g=m_qnorm_a_g, knorm_a_g=m_knorm_a_g,
                   kv_norm_g=m_kv_norm_g, knorm_b_g=m_knorm_b_g, norm_b_g=m_norm_b_g, qnorm_b_g=m_qnorm_b_g,
                   sinks=m_sinks)
    small_v = dict(norm_a_g=v_norm_a_g, b_forget=v_b_forget, qnorm_a_g=v_qnorm_a_g, knorm_a_g=v_knorm_a_g,
                   kv_norm_g=v_kv_norm_g, knorm_b_g=v_knorm_b_g, norm_b_g=v_norm_b_g, qnorm_b_g=v_qnorm_b_g,
                   sinks=v_sinks)
    res = _small_adamw(slab_g, [r2(small_w[n]) for n in SMALL], [r2(small_m[n]) for n in SMALL],
                       [r2(small_v[n]) for n in SMALL])
    k = len(SMALL)
    small = {n: [res[q * k + p].reshape(small_w[n].shape) for q in range(4)] for p, n in enumerate(SMALL)}
    loss = res[4 * k][0, 0]

    order = ["norm_a_g", "w_in_a", "b_forget", "qnorm_a_g", "knorm_a_g", "w_out_a", "kv_norm_g", "w_kv",
             "knorm_b_g", "norm_b_g", "w_in_b", "qnorm_b_g", "sinks", "w_out_b"]

    def leaf(n, q):
        return big[n][q] if n in big else small[n][q]

    outs = [loss, dx[None]]
    for q in range(4):
        outs.extend(leaf(n, q) for n in order)
    return tuple(outs)
```

```python
import jax
import jax.numpy as jnp
from jax import lax
from jax.experimental import pallas as pl
from jax.experimental.pallas import tpu as pltpu

F32, BF16 = jnp.float32, jnp.bfloat16

S = 2048
D = 1024
HD = 64
N_HEADS = 16
N_DEV = 8
NA = 4352
GOFF = 3072
FOFF = 4096
RAW_F = 3072
RAW_G = RAW_F + N_HEADS
NA_RAW = 4112
EPS = 1e-6
QSCALE = 0.125
ROPE_THETA = 500000.0
ROT = 16
WIN = 128
TQ = 256
TK = 256
KS = TQ // 2
HPS = 8
HW = HPS * HD
TM = 256
RT = 256
RB = 512
CB = 256
LANES = 128

ADAM_LR, ADAM_B1, ADAM_B2, ADAM_EPS, ADAM_WD, ADAM_STEP = 0.001, 0.9, 0.999, 1e-08, 0.01, 10

VMEM_LIMIT = 56 * 1024 * 1024


def _params():
    return pltpu.CompilerParams(vmem_limit_bytes=VMEM_LIMIT)


def _sds(shape, dtype):
    return jax.ShapeDtypeStruct(shape, dtype)


def _dot_nt(a, b):
    return lax.dot_general(a, b, (((1,), (1,)), ((), ())), preferred_element_type=F32)


def _dot_tn(a, b):
    return lax.dot_general(a, b, (((0,), (0,)), ((), ())), preferred_element_type=F32)


def _dot_nn(a, b):
    return lax.dot_general(a, b, (((1,), (0,)), ((), ())), preferred_element_type=F32)


def _sigmoid(g):
    return 1.0 / (1.0 + jnp.exp(-g))


def _lane_iota(shape):
    return lax.broadcasted_iota(jnp.int32, shape, len(shape) - 1)


def _flips(kind):
    return (2, 4, 6) if kind == "a2a_chips" else tuple(range(1, N_DEV))


def _send_view(kind, ref, dev):
    if kind in ("gather_rows", "gather_cols"):
        return ref
    if kind == "a2a_slots":
        return ref.at[dev]
    if kind == "a2a_chips":
        return ref.at[dev >> 1]
    if kind == "a2a_rows":
        rows = ref.shape[0] // N_DEV
        return ref.at[pl.ds(pl.multiple_of(dev * rows, rows), rows)]
    cols = ref.shape[1] // N_DEV
    return ref.at[:, pl.ds(pl.multiple_of(dev * cols, cols), cols)]


def _land_view(kind, ref, dev):
    if kind == "gather_cols":
        cols = ref.shape[1] // N_DEV
        return ref.at[:, pl.ds(pl.multiple_of(dev * cols, cols), cols)]
    if kind == "a2a_chips":
        return ref.at[dev >> 1]
    return ref.at[dev]


def _landing_sds(kind, arr):
    if kind == "gather_rows":
        return _sds((N_DEV,) + arr.shape, arr.dtype)
    if kind == "gather_cols":
        return _sds((arr.shape[0], N_DEV * arr.shape[1]), arr.dtype)
    if kind == "a2a_rows":
        return _sds((N_DEV, arr.shape[0] // N_DEV, arr.shape[1]), arr.dtype)
    if kind == "a2a_cols":
        return _sds((N_DEV, arr.shape[0], arr.shape[1] // N_DEV), arr.dtype)
    return _sds(arr.shape, arr.dtype)


def _exchange_sems(n_parts):
    n = n_parts * (N_DEV - 1)
    return [pltpu.SemaphoreType.DMA((n,)), pltpu.SemaphoreType.DMA((n,)), pltpu.SemaphoreType.DMA((n_parts,))]


def _exchange_ops(kinds, srcs, dsts, sems, start, wait):
    send_sems, recv_sems, local_sems = sems
    x, y, c = lax.axis_index("x"), lax.axis_index("y"), lax.axis_index("c")
    me = 4 * x + 2 * y + c

    def local(a):
        return pltpu.make_async_copy(_send_view(kinds[a], srcs[a], me), _land_view(kinds[a], dsts[a], me),
                                     local_sems.at[a])

    def remote(a, k, landing_dev):
        peer = (x ^ ((k >> 2) & 1), y ^ ((k >> 1) & 1), c ^ (k & 1))
        sem = a * (N_DEV - 1) + k - 1
        return pltpu.make_async_remote_copy(
            src_ref=_send_view(kinds[a], srcs[a], me ^ k), dst_ref=_land_view(kinds[a], dsts[a], landing_dev),
            send_sem=send_sems.at[sem], recv_sem=recv_sems.at[sem], device_id=peer,
            device_id_type=pl.DeviceIdType.MESH)

    pairs = [(a, k) for k in range(1, N_DEV) for a in range(len(kinds)) if k in _flips(kinds[a])]
    if start:
        for a in range(len(kinds)):
            local(a).start()
        for a, k in pairs:
            remote(a, k, me).start()
    if wait:
        for a, k in pairs:
            remote(a, k, me ^ k).wait_recv()
            remote(a, k, me).wait_send()
        for a in range(len(kinds)):
            local(a).wait()


def _gather_two_level(srcs, landing_of, sems, meanwhile=None, landed=None):
    send_sems, recv_sems, local_sems = sems
    x, y, c = lax.axis_index("x"), lax.axis_index("y"), lax.axis_index("c")
    me, sibling = (x, y, c), (x, y, 1 - c)
    chips = [(1 - x, y), (x, 1 - y), (1 - x, 1 - y)]

    def slot(a, dev):
        return landing_of[a](4 * dev[0] + 2 * dev[1] + dev[2])

    def copy(a, k, block, to, src=None):
        return pltpu.make_async_remote_copy(
            src_ref=slot(a, block) if src is None else src, dst_ref=slot(a, block),
            send_sem=send_sems.at[a * (N_DEV - 1) + k], recv_sem=recv_sems.at[a * (N_DEV - 1) + k],
            device_id=to, device_id_type=pl.DeviceIdType.MESH)

    parts = range(len(srcs))
    mine = [pltpu.make_async_copy(srcs[a], slot(a, me), local_sems.at[a]) for a in parts]
    first = [copy(a, 1 + j, me, (*chip, c), src=srcs[a]) for a in parts for j, chip in enumerate(chips)]
    first += [copy(a, 0, me, sibling, src=srcs[a]) for a in parts]
    for cp in first + mine:
        cp.start()
    if meanwhile is not None:
        meanwhile()
    passed = []
    for a in parts:
        for j, chip in enumerate(chips):
            copy(a, 1 + j, (*chip, c), me).wait_recv()
            fwd = copy(a, 4 + j, (*chip, c), sibling)
            fwd.start()
            passed.append(fwd)
    for a in parts:
        copy(a, 0, sibling, me).wait_recv()
        for j, chip in enumerate(chips):
            copy(a, 4 + j, (*chip, 1 - c), me).wait_recv()
        mine[a].wait()
        if landed is not None:
            landed(a)
    for cp in first + passed:
        cp.wait_send()


def _call(body, *, name, args, in_specs, out_specs, out_shape, grid=(), scratch_shapes=(), aliases=None, rider=()):
    n_in, n_out, n_scr, n_r = len(in_specs), len(out_specs), len(scratch_shapes), len(rider)
    kinds = [kind for kind, _ in rider]

    def kernel_body(*refs):
        c_in, r_in = refs[:n_in], refs[n_in:n_in + n_r]
        c_out = refs[n_in + n_r:n_in + n_r + n_out]
        r_out = refs[n_in + n_r + n_out:n_in + 2 * n_r + n_out]
        rest = refs[n_in + 2 * n_r + n_out:]
        c_scr, sems = rest[:n_scr], rest[n_scr:]
        if n_r:
            assert grid, "a rider needs a gridded call"
            ids = [pl.program_id(ax) for ax in range(len(grid))]
            first, last = ids[0] == 0, ids[0] == grid[0] - 1
            for pid, size in zip(ids[1:], grid[1:]):
                first = first & (pid == 0)
                last = last & (pid == size - 1)
            pl.when(first)(lambda: _exchange_ops(kinds, r_in, r_out, sems, True, False))
        body(*c_in, *c_out, *c_scr)
        if n_r:
            pl.when(last)(lambda: _exchange_ops(kinds, r_in, r_out, sems, False, True))

    anyspec = pl.BlockSpec(memory_space=pl.ANY)
    params = pltpu.CompilerParams(vmem_limit_bytes=VMEM_LIMIT, has_side_effects=bool(n_r))
    outs = pl.pallas_call(
        kernel_body, name=name, grid=grid, in_specs=list(in_specs) + [anyspec] * n_r,
        out_specs=list(out_specs) + [anyspec] * n_r,
        out_shape=list(out_shape) + [_landing_sds(kind, arr) for kind, arr in rider],
        scratch_shapes=list(scratch_shapes) + (_exchange_sems(n_r) if n_r else []),
        input_output_aliases=aliases or {}, compiler_params=params)(*args, *[arr for _, arr in rider])
    return list(outs)


def _mm(a, b, mode, tm, tn, tk, out_dtype=F32, add=None, name="mm", rider=()):
    if mode == "nn":
        (m, k), n = a.shape, b.shape[1]
        a_spec = pl.BlockSpec((tm, tk), lambda i, j, kk: (i, kk))
        b_spec = pl.BlockSpec((tk, tn), lambda i, j, kk: (kk, j))
        dot = _dot_nn
    elif mode == "nt":
        (m, k), n = a.shape, b.shape[0]
        a_spec = pl.BlockSpec((tm, tk), lambda i, j, kk: (i, kk))
        b_spec = pl.BlockSpec((tn, tk), lambda i, j, kk: (j, kk))
        dot = _dot_nt
    else:
        (k, m), n = a.shape, b.shape[1]
        a_spec = pl.BlockSpec((tk, tm), lambda i, j, kk: (kk, i))
        b_spec = pl.BlockSpec((tk, tn), lambda i, j, kk: (kk, j))
        dot = _dot_tn
    assert m % tm == 0 and n % tn == 0 and k % tk == 0, (m, n, k, tm, tn, tk)
    nk = k // tk
    has_add = add is not None

    def body(*refs):
        if has_add:
            a_ref, b_ref, add_ref, o_ref, acc = refs
        else:
            a_ref, b_ref, o_ref, acc = refs
        p = dot(a_ref[...].astype(BF16), b_ref[...].astype(BF16))

        def finish(total):
            if has_add:
                total = add_ref[...] + total
            o_ref[...] = total.astype(out_dtype)

        if nk == 1:
            finish(p)
        else:
            kk = pl.program_id(2)

            @pl.when(kk == 0)
            def _():
                acc[...] = p

            @pl.when(kk > 0)
            def _():
                acc[...] += p

            @pl.when(kk == nk - 1)
            def _():
                finish(acc[...])

    in_specs = [a_spec, b_spec]
    args = [a, b]
    if has_add:
        in_specs.append(pl.BlockSpec((tm, tn), lambda i, j, kk: (i, j)))
        args.append(add)
    acc_shape = (tm, tn) if nk > 1 else (8, LANES)
    outs = _call(body, name=name, args=args, grid=(m // tm, n // tn, nk), in_specs=in_specs,
                 out_specs=[pl.BlockSpec((tm, tn), lambda i, j, kk: (i, j))], out_shape=[_sds((m, n), out_dtype)],
                 scratch_shapes=[pltpu.VMEM(acc_shape, F32)], rider=rider)
    return outs if rider else outs[0]


def _rms_rinv(x):
    return lax.rsqrt(jnp.mean(x * x, axis=-1, keepdims=True) + EPS)


def _rms_bwd_core(du, x, g):
    r = _rms_rinv(x)
    dug = du * g
    dx = r * (dug - x * ((r * r) * jnp.mean(dug * x, axis=-1, keepdims=True)))
    dg = jnp.sum(du * (x * r), axis=0, keepdims=True)
    return dx, dg


def _half_ones():
    r = lax.broadcasted_iota(jnp.int32, (LANES, LANES), 0)
    c = lax.broadcasted_iota(jnp.int32, (LANES, LANES), 1)
    return ((r < HD) == (c < HD)).astype(BF16)


def _half_sum(v, lo_half):
    if lo_half.dtype == jnp.bool_:
        s0 = jnp.sum(jnp.where(lo_half, v, 0.0), axis=-1, keepdims=True)
        s1 = jnp.sum(jnp.where(lo_half, 0.0, v), axis=-1, keepdims=True)
        return jnp.where(lo_half, s0, s1)
    hi = v.astype(BF16)
    lo = (v - hi.astype(F32)).astype(BF16)
    return _dot_nn(hi, lo_half) + _dot_nn(lo, lo_half)


def _head_rinv(x, lo_half):
    return lax.rsqrt(_half_sum(x * x, lo_half) * (1.0 / HD) + EPS)


def _head_norm_bwd(dn, x, g, lo_half):
    r = _head_rinv(x, lo_half)
    dng = dn * g
    dx = r * (dng - x * ((r * r) * (_half_sum(dng * x, lo_half) * (1.0 / HD))))
    dg = jnp.sum(dn * (x * r), axis=0, keepdims=True)
    return dx, dg


def _rope_swap(x, lane):
    l64 = lane & (HD - 1)
    return jnp.where(l64 < ROT // 2, pltpu.roll(x, LANES - ROT // 2, 1), pltpu.roll(x, ROT // 2, 1))


def _rope_fwd(x, cos, sin, lane):
    return x * cos + _rope_swap(x, lane) * sin


def _rope_bwd(dy, cos, sin, lane):
    return dy * cos + jnp.where((lane & (HD - 1)) < ROT, _rope_swap(dy * sin, lane), 0.0)


def _g2(g_ref):
    g = g_ref[...]
    return jnp.concatenate([g, g], axis=-1)


def _pairs(width):
    return [slice(LANES * c, LANES * (c + 1)) for c in range(width // LANES)]


def _pick_lane(block, lane, idx):
    return jnp.sum(jnp.where(lane == idx, block, 0.0), axis=-1, keepdims=True)


def _head_a(x, g, wa, gq, gk, b_pad):
    assert RT == TQ
    def body(x_ref, g_ref, w_ref, gq_ref, gk_ref, b_ref, u_ref, p_ref, qo_ref, ko_ref, vo_ref, c_ref, cbc_ref, carry):
        @pl.when(pl.program_id(0) == 0)
        def _():
            carry[...] = jnp.zeros_like(carry)

        xv = x_ref[...]
        u = ((xv * _rms_rinv(xv)) * g_ref[...]).astype(BF16)
        u_ref[...] = u
        for lo in range(0, NA, D):
            hi = min(lo + D, NA)
            p_ref[:, lo:hi] = _dot_nn(u, w_ref[:, lo:hi])
        lane = _lane_iota((RT, LANES))
        lo_half = lane < HD
        gq2, gk2 = _g2(gq_ref), _g2(gk_ref)
        for c in _pairs(D):
            q = p_ref[:, c]
            k = p_ref[:, D + c.start:D + c.stop]
            qo_ref[:, c] = (((q * _head_rinv(q, lo_half)) * gq2) * QSCALE).astype(BF16)
            ko_ref[:, c] = ((k * _head_rinv(k, lo_half)) * gk2).astype(BF16)
        vo_ref[...] = p_ref[:, 2 * D:3 * D].astype(BF16)

        z = p_ref[:, FOFF:FOFF + LANES] + b_ref[...]
        logf = jnp.minimum(z, 0.0) - jnp.log1p(jnp.exp(-jnp.abs(z)))
        r = lax.broadcasted_iota(jnp.int32, (RT, RT), 0)
        cc = lax.broadcasted_iota(jnp.int32, (RT, RT), 1)
        tri = (r >= cc).astype(F32)
        loc = jnp.dot(tri, logf, precision=lax.Precision.HIGHEST, preferred_element_type=F32) + carry[0:1, :]
        c_ref[:, 0, :] = loc.T[:N_HEADS]
        carry[0:1, :] = loc[RT - 1:RT, :]
        for h in range(N_HEADS):
            cbc_ref[:, LANES * h:LANES * (h + 1)] = jnp.broadcast_to(_pick_lane(loc, lane, h), (RT, LANES))

    row = lambda width: pl.BlockSpec((RT, width), lambda i: (i, 0))
    whole = lambda arr: pl.BlockSpec(arr.shape, lambda i: (0,) * arr.ndim, pipeline_mode=pl.Buffered(1))
    return pl.pallas_call(
        body, name="head_a", grid=(S // RT,),
        in_specs=[row(D), whole(g), whole(wa), whole(gq), whole(gk), whole(b_pad)],
        out_specs=[row(D), row(NA), row(D), row(D), row(D),
                   pl.BlockSpec((N_HEADS, None, 1, TQ), lambda i: (0, i, 0, 0)), row(N_HEADS * LANES)],
        out_shape=[_sds((S, D), BF16), _sds((S, NA), F32)] + [_sds((S, D), BF16)] * 3
        + [_sds((N_HEADS, S // TQ, 1, TQ), F32), _sds((S, N_HEADS * LANES), F32)],
        scratch_shapes=[pltpu.VMEM((8, LANES), F32)], compiler_params=_params())(x, g, wa, gq, gk, b_pad)


def _key_le_query(offset, keys=TK):
    r = lax.broadcasted_iota(jnp.int32, (keys, TQ), 0)
    c = lax.broadcasted_iota(jnp.int32, (keys, TQ), 1)
    return (r + offset) <= c


def _widen(tile):
    return jnp.concatenate([tile] * (TQ // LANES), axis=1)


def _fox_fwd(qn, kn, vb, proj, crow, cbc, rider=()):
    nq = S // TQ

    def body(q_ref, k_ref, v_ref, g_ref, cq_ref, cbc_ref, o_ref, z_ref, lse_ref, st_s, pt_s):
        i = pl.program_id(1)
        qs = [q_ref[:, HD * hh:HD * (hh + 1)] for hh in range(HPS)]
        cqs = [cq_ref[hh, 0] for hh in range(HPS)]

        def scores(s, hh):
            off = pl.multiple_of(s * KS, KS)
            kj = k_ref[pl.ds(off, KS), HD * hh:HD * (hh + 1)]
            return (_dot_nt(kj, qs[hh]) + cqs[hh]) - _widen(cbc_ref[pl.ds(off, KS), LANES * hh:LANES * (hh + 1)])

        def values(s, hh, pt):
            off = pl.multiple_of(s * KS, KS)
            return _dot_tn(v_ref[pl.ds(off, KS), HD * hh:HD * (hh + 1)], pt)

        def step(s, slot, carries, mask=None, last=False):
            if not last:
                for hh in range(HPS):
                    st_s[1 - slot, hh] = scores(s + 1, hh)
            pvs = [values(jnp.maximum(s - 1, 0), hh, pt_s[1 - slot, hh]) for hh in range(HPS)]
            out = []
            for hh in range(HPS):
                m, l, acc = carries[hh]
                st = st_s[slot, hh]
                if mask is not None:
                    st = jnp.where(mask, st, -jnp.inf)
                m_new = jnp.maximum(m, jnp.max(st, axis=0, keepdims=True))
                pt = jnp.exp(st - m_new)
                alpha = jnp.exp(m - m_new)
                pt_s[slot, hh] = pt.astype(BF16)
                out.append((m_new, alpha * l + jnp.sum(pt, axis=0, keepdims=True), alpha * (acc + pvs[hh])))
            return tuple(out)

        for hh in range(HPS):
            st_s[0, hh] = scores(0, hh)
            pt_s[1, hh] = jnp.zeros((KS, TQ), BF16)
        one = (jnp.full((1, TQ), -jnp.inf, F32), jnp.zeros((1, TQ), F32), jnp.zeros((HD, TQ), F32))
        carries = lax.fori_loop(0, i, lambda t, cr: step(2 * t + 1, 1, step(2 * t, 0, cr)), (one,) * HPS)
        carries = step(2 * i, 0, carries, mask=_key_le_query(0, KS))
        carries = step(2 * i + 1, 1, carries, mask=_key_le_query(KS, KS), last=True)
        accs = []
        for hh in range(HPS):
            m, l, acc = carries[hh]
            acc = acc + values(2 * i + 1, hh, pt_s[1, hh])
            accs.append(acc / l)
            lse_ref[hh, 0] = m + jnp.log(l)
        o = jnp.concatenate(accs, axis=0).T
        o_ref[...] = o
        g = g_ref[...]
        z_ref[...] = (o * (g * _sigmoid(g))).astype(BF16)

    qblk = pl.BlockSpec((TQ, HW), lambda hp, i: (i, hp))
    full = pl.BlockSpec((S, HW), lambda hp, i: (0, hp))
    rows = pl.BlockSpec((HPS, 1, 1, TQ), lambda hp, i: (hp, i, 0, 0))
    return _call(
        body, name="fox_fwd", args=(qn, kn, vb, proj, crow, cbc), grid=(N_HEADS // HPS, nq),
        in_specs=[qblk, full, full,
                  pl.BlockSpec((TQ, HW), lambda hp, i: (i, GOFF // HW + hp)),
                  rows, pl.BlockSpec((S, HPS * LANES), lambda hp, i: (0, hp))],
        out_specs=[qblk, qblk, rows],
        out_shape=[_sds((S, D), F32), _sds((S, D), BF16), _sds((N_HEADS, nq, 1, TQ), F32)],
        scratch_shapes=[pltpu.VMEM((2, HPS, KS, TQ), F32), pltpu.VMEM((2, HPS, KS, TQ), BF16)], rider=rider)


def _fox_bwd_pre(dh, w_out, proj, o):
    nq, rows = S // TQ, 2 * TQ
    per = rows // TQ

    def body(dh_ref, w_ref, g_ref, o_ref, do_ref, dg_ref, delta_ref):
        g = g_ref[...]
        sg = _sigmoid(g)
        dzv = _dot_nt(dh_ref[...].astype(BF16), w_ref[...])
        ov = o_ref[...]
        do = dzv * (g * sg)
        dg_ref[...] = (dzv * ov * (sg * (1.0 + g * (1.0 - sg)))).astype(BF16)
        do_ref[...] = do.astype(BF16)
        prod_t = (do * ov).T
        for h in range(N_HEADS):
            for b in range(per):
                delta_ref[h, b] = jnp.sum(prod_t[HD * h:HD * (h + 1), TQ * b:TQ * (b + 1)], axis=0, keepdims=True)

    row = pl.BlockSpec((rows, D), lambda i: (i, 0))
    return pl.pallas_call(
        body, name="fox_bwd_pre", grid=(S // rows,),
        in_specs=[row, pl.BlockSpec(w_out.shape, lambda i: (0, 0), pipeline_mode=pl.Buffered(1)),
                  pl.BlockSpec((rows, D), lambda i: (i, GOFF // D)), row],
        out_specs=[row, row, pl.BlockSpec((N_HEADS, per, 1, TQ), lambda i: (0, i, 0, 0))],
        out_shape=[_sds((S, D), BF16), _sds((S, D), BF16), _sds((N_HEADS, nq, 1, TQ), F32)],
        compiler_params=_params())(dh, w_out, proj, o)


def _fox_bwd(qn, kn, vb, dob, lse, delta, crow, cbc, rider=()):
    nq, nkb = S // TQ, S // TK

    def body(q_ref, k_ref, v_ref, do_ref, lse_ref, del_ref, cq_ref, cbc_ref,
             dk_ref, dv_ref, dcs_ref, dq_ref, dr_ref, st_s, dp_s, pt_s, ds_s, dq_acc, dr_acc):
        j = pl.program_id(1)

        @pl.when(j == 0)
        def _():
            dq_acc[...] = jnp.zeros_like(dq_acc)
            dr_acc[...] = jnp.zeros_like(dr_acc)

        kjs = [k_ref[:, HD * hh:HD * (hh + 1)] for hh in range(HPS)]
        vjs = [v_ref[:, HD * hh:HD * (hh + 1)] for hh in range(HPS)]

        def rows_of(ref, u, hh):
            off = pl.multiple_of(u * TQ, TQ)
            return ref[pl.ds(off, TQ), HD * hh:HD * (hh + 1)]

        def products(u, hh):
            st = (_dot_nt(kjs[hh], rows_of(q_ref, u, hh)) + cq_ref[hh, u]) - _widen(
                cbc_ref[:, LANES * hh:LANES * (hh + 1)])
            return st, _dot_nt(vjs[hh], rows_of(do_ref, u, hh))

        def step(u, slot, carries, masked=False):
            nxt = jnp.minimum(u + 1, nq - 1)
            for hh in range(HPS):
                st_s[1 - slot, hh], dp_s[1 - slot, hh] = products(nxt, hh)
            prev = jnp.maximum(u - 1, 0)
            dvs = [_dot_nn(pt_s[1 - slot, hh], rows_of(do_ref, prev, hh)) for hh in range(HPS)]
            dks = [_dot_nn(ds_s[1 - slot, hh], rows_of(q_ref, prev, hh)) for hh in range(HPS)]
            for hh in range(HPS):
                dq_acc[hh, prev] += _dot_tn(kjs[hh], ds_s[1 - slot, hh])
            out = []
            for hh in range(HPS):
                dk, dv, dcs = carries[hh]
                st = st_s[slot, hh]
                if masked:
                    st = jnp.where(_key_le_query((j - u) * TQ), st, -jnp.inf)
                pt = jnp.exp(st - lse_ref[hh, u])
                dst = pt * (dp_s[slot, hh] - del_ref[hh, u])
                pt_s[slot, hh] = pt.astype(BF16)
                ds_s[slot, hh] = dst.astype(BF16)
                dr_acc[hh, u] += jnp.sum(dst, axis=0, keepdims=True)
                out.append((dk + dks[hh], dv + dvs[hh], dcs + (dst[:, :LANES] + dst[:, LANES:])))
            return tuple(out)

        t0 = j // 2
        for hh in range(HPS):
            st_s[0, hh], dp_s[0, hh] = products(2 * t0, hh)
            pt_s[1, hh] = jnp.zeros((TK, TQ), BF16)
            ds_s[1, hh] = jnp.zeros((TK, TQ), BF16)
        one = (jnp.zeros((TK, HD), F32), jnp.zeros((TK, HD), F32), jnp.zeros((TK, LANES), F32))
        carries = step(2 * t0 + 1, 1, step(2 * t0, 0, (one,) * HPS, masked=True), masked=True)
        carries = lax.fori_loop(t0 + 1, nq // 2, lambda t, cr: step(2 * t + 1, 1, step(2 * t, 0, cr)), carries)
        dks, dvs = [], []
        lane = _lane_iota((TK, LANES))
        dcs_all = jnp.zeros((TK, LANES), F32)
        for hh in range(HPS):
            dk, dv, dcs = carries[hh]
            dks.append(dk + _dot_nn(ds_s[1, hh], rows_of(q_ref, nq - 1, hh)))
            dvs.append(dv + _dot_nn(pt_s[1, hh], rows_of(do_ref, nq - 1, hh)))
            dq_acc[hh, nq - 1] += _dot_tn(kjs[hh], ds_s[1, hh])
            dcs_all = jnp.where(lane == HPS * pl.program_id(0) + hh, -jnp.sum(dcs, axis=-1, keepdims=True), dcs_all)
        dcs_ref[0] = dcs_all
        dk_ref[...] = jnp.concatenate(dks, axis=-1)
        dv_ref[...] = jnp.concatenate(dvs, axis=-1).astype(BF16)

        @pl.when(j == nkb - 1)
        def _():
            for i in range(nq):
                dq_ref[TQ * i:TQ * (i + 1), :] = jnp.concatenate([dq_acc[hh, i] for hh in range(HPS)], axis=0).T
            dr_ref[...] = dr_acc[...]

    kblk = pl.BlockSpec((TK, HW), lambda hp, j: (j, hp))
    full = pl.BlockSpec((S, HW), lambda hp, j: (0, hp))
    rows = pl.BlockSpec((HPS, nq, 1, TQ), lambda hp, j: (hp, 0, 0, 0))
    cblk = pl.BlockSpec((TK, HPS * LANES), lambda hp, j: (j, hp))
    return _call(
        body, name="fox_bwd", args=(qn, kn, vb, dob, lse, delta, crow, cbc), grid=(N_HEADS // HPS, nkb),
        in_specs=[full, kblk, kblk, full, rows, rows, rows, cblk],
        out_specs=[kblk, kblk, pl.BlockSpec((1, TK, LANES), lambda hp, j: (hp, j, 0)), full, rows],
        out_shape=[_sds((S, D), F32), _sds((S, D), BF16), _sds((N_HEADS // HPS, S, LANES), F32), _sds((S, D), F32),
                   _sds((N_HEADS, nq, 1, TQ), F32)],
        scratch_shapes=[pltpu.VMEM((2, HPS, TK, TQ), F32), pltpu.VMEM((2, HPS, TK, TQ), F32),
                        pltpu.VMEM((2, HPS, TK, TQ), BF16), pltpu.VMEM((2, HPS, TK, TQ), BF16),
                        pltpu.VMEM((HPS, nq, HD, TQ), F32), pltpu.VMEM((HPS, nq, 1, TQ), F32)], rider=rider)


def _prep_a_bwd(dq, dk, dv, dgate, drow, dcs, proj, b_pad, gq, gk):
    nt = S // TM

    def body(dq_ref, dk_ref, dv_ref, dgt_ref, dr_ref, dcs_ref, xq_ref, xk_ref, f_ref, b_ref, gq_ref, gk_ref,
             o_ref, dgq_ref, dgk_ref, db_ref, carry):
        @pl.when(pl.program_id(0) == 0)
        def _():
            carry[...] = jnp.zeros_like(carry)
            dgq_ref[...] = jnp.zeros_like(dgq_ref)
            dgk_ref[...] = jnp.zeros_like(dgk_ref)
            db_ref[...] = jnp.zeros_like(db_ref)

        lane = _lane_iota((TM, LANES))
        lo_half = _half_ones()
        gq2, gk2 = _g2(gq_ref), _g2(gk_ref)
        dgq, dgk = jnp.zeros((1, LANES), F32), jnp.zeros((1, LANES), F32)
        for c in _pairs(D):
            dxq, dg = _head_norm_bwd(dq_ref[:, c] * QSCALE, xq_ref[:, c], gq2, lo_half)
            o_ref[:, c] = dxq.astype(BF16)
            dgq = dgq + dg
            dxk, dg = _head_norm_bwd(dk_ref[:, c], xk_ref[:, c], gk2, lo_half)
            o_ref[:, D + c.start:D + c.stop] = dxk.astype(BF16)
            dgk = dgk + dg
        dgq_ref[...] += dgq
        dgk_ref[...] += dgk
        o_ref[:, 2 * D:3 * D] = dv_ref[...]
        o_ref[:, GOFF:GOFF + D] = dgt_ref[...]

        dc = jnp.concatenate([dr_ref[:, 0, :], jnp.zeros((LANES - N_HEADS, TM), F32)], axis=0).T
        for group in range(N_HEADS // HPS):
            dc = dc + dcs_ref[group]
        r = lax.broadcasted_iota(jnp.int32, (TM, TM), 0)
        c = lax.broadcasted_iota(jnp.int32, (TM, TM), 1)
        tri = (c >= r).astype(F32)
        dlogf = jnp.dot(tri, dc, precision=lax.Precision.HIGHEST, preferred_element_type=F32) + carry[0:1, :]
        carry[0:1, :] = dlogf[0:1, :]
        df = dlogf * (1.0 / (1.0 + jnp.exp(f_ref[...] + b_ref[...])))
        db_ref[...] += jnp.sum(df, axis=0, keepdims=True)
        o_ref[:, FOFF:FOFF + LANES] = df.astype(BF16)
        o_ref[:, FOFF + LANES:NA] = jnp.zeros((TM, NA - FOFF - LANES), BF16)

    rev = lambda width, col: pl.BlockSpec((TM, width), lambda i: (nt - 1 - i, col))
    gspec = pl.BlockSpec((1, HD), lambda i: (0, 0))
    acc = pl.BlockSpec((1, LANES), lambda i: (0, 0))
    return pl.pallas_call(
        body, name="prep_a_bwd", grid=(nt,),
        in_specs=[rev(D, 0), rev(D, 0), rev(D, 0), rev(D, 0),
                  pl.BlockSpec((N_HEADS, None, 1, TM), lambda i: (0, nt - 1 - i, 0, 0)),
                  pl.BlockSpec((N_HEADS // HPS, TM, LANES), lambda i: (0, nt - 1 - i, 0)),
                  rev(D, 0), rev(D, 1), rev(LANES, FOFF // LANES), acc, gspec, gspec],
        out_specs=[rev(NA, 0), acc, acc, acc],
        out_shape=[_sds((S, NA), BF16)] + [_sds((1, LANES), F32)] * 3,
        scratch_shapes=[pltpu.VMEM((8, LANES), F32)],
        compiler_params=_params())(dq, dk, dv, dgate, drow, dcs, proj, proj, proj, b_pad, gq, gk)


def _head_b(x, z_a, w_out_a, g_kv, g_b, w_kv, w_in_b, gq, gk, cos2, sin2):
    nkv = w_kv.shape[1] // 2

    def body(x_ref, z_ref, wo_ref, gkv_ref, gb_ref, wkv_ref, wb_ref, gq_ref, gk_ref, c_ref, s_ref,
             h_ref, ukv_ref, ub_ref, kv_ref, pb_ref, qo_ref, ko_ref, vo_ref):
        xv = x_ref[...] + _dot_nn(z_ref[...], wo_ref[...])
        h_ref[...] = xv
        xn = xv * _rms_rinv(xv)
        ukv = (xn * gkv_ref[...]).astype(BF16)
        ub = (xn * gb_ref[...]).astype(BF16)
        ukv_ref[...] = ukv
        ub_ref[...] = ub
        kv_ref[...] = _dot_nn(ukv, wkv_ref[...])
        for lo in range(0, 2 * D, D):
            pb_ref[:, lo:lo + D] = _dot_nn(ub, wb_ref[:, lo:lo + D])
        lane = _lane_iota((RT, LANES))
        lo_half = lane < HD
        cos, sin = c_ref[...], s_ref[...]
        gq2, gk2 = _g2(gq_ref), _g2(gk_ref)
        for c in _pairs(D):
            q = pb_ref[:, c]
            qo_ref[:, c] = (_rope_fwd((q * _head_rinv(q, lo_half)) * gq2, cos, sin, lane) * QSCALE).astype(BF16)
        for c in _pairs(nkv):
            k = kv_ref[:, c]
            ko_ref[:, c] = _rope_fwd((k * _head_rinv(k, lo_half)) * gk2, cos, sin, lane).astype(BF16)
        vo_ref[...] = kv_ref[:, nkv:2 * nkv].astype(BF16)

    row = lambda width: pl.BlockSpec((RT, width), lambda i: (i, 0))
    whole = lambda arr: pl.BlockSpec(arr.shape, lambda i: (0,) * arr.ndim, pipeline_mode=pl.Buffered(1))
    return pl.pallas_call(
        body, name="head_b", grid=(S // RT,),
        in_specs=[row(D), row(D), whole(w_out_a), whole(g_kv), whole(g_b), whole(w_kv), whole(w_in_b), whole(gq),
                  whole(gk), row(LANES), row(LANES)],
        out_specs=[row(D), row(D), row(D), row(2 * nkv), row(2 * D), row(D), row(nkv), row(nkv)],
        out_shape=[_sds((S, D), F32), _sds((S, D), BF16), _sds((S, D), BF16), _sds((S, 2 * nkv), F32),
                   _sds((S, 2 * D), F32), _sds((S, D), BF16), _sds((S, nkv), BF16), _sds((S, nkv), BF16)],
        compiler_params=_params())(x, z_a, w_out_a, g_kv, g_b, w_kv, w_in_b, gq, gk, cos2, sin2)


N_KV, GRP = 4, 4


def _swa_mask(n):
    r = lax.broadcasted_iota(jnp.int32, (2 * WIN, GRP * WIN), 0)
    q = lax.broadcasted_iota(jnp.int32, (2 * WIN, GRP * WIN), 1) & (WIN - 1)
    return (r > q) & (r <= q + WIN) & ((r >= WIN) | (n > 0))


def _stack4(ref_or_val, base):
    return jnp.concatenate([ref_or_val[:, base + HD * g: base + HD * (g + 1)] for g in range(GRP)], axis=0)


def _unstack4(xt):
    return jnp.concatenate([xt[:, WIN * g:WIN * (g + 1)] for g in range(GRP)], axis=0).T


def _band(prev_ref, cur_ref, kh):
    return jnp.concatenate([prev_ref[:, HD * kh:HD * (kh + 1)], cur_ref[:, HD * kh:HD * (kh + 1)]], axis=0)


def _sink_row(s_ref, first):
    lane = _lane_iota((1, GRP * WIN))
    row = jnp.full((1, GRP * WIN), s_ref[first + GRP - 1], F32)
    for g in range(GRP - 2, -1, -1):
        row = jnp.where(lane < WIN * (g + 1), s_ref[first + g], row)
    return row


def _swa_fwd(qb, ksh, vsh, pb, sinks):
    nb = S // WIN

    def body(q_ref, kp_ref, kc_ref, vp_ref, vc_ref, g_ref, s_ref, o_ref, z_ref, lse_ref):
        n = pl.program_id(0)
        valid = _swa_mask(n)
        outs = []
        for kh in range(N_KV):
            kb, vb = _band(kp_ref, kc_ref, kh), _band(vp_ref, vc_ref, kh)
            st = jnp.where(valid, _dot_nt(kb, _stack4(q_ref, GRP * HD * kh)), -jnp.inf)
            sink = _sink_row(s_ref, GRP * kh)
            m = jnp.maximum(jnp.max(st, axis=0, keepdims=True), sink)
            pt = jnp.exp(st - m)
            l = jnp.sum(pt, axis=0, keepdims=True) + jnp.exp(sink - m)
            outs.append(_unstack4(_dot_tn(vb, pt.astype(BF16)) / l))
            lse = m + jnp.log(l)
            for g in range(GRP):
                lse_ref[GRP * kh + g, 0] = lse[:, WIN * g:WIN * (g + 1)]
        o = jnp.concatenate(outs, axis=-1)
        o_ref[...] = o
        g = g_ref[...]
        z_ref[...] = (o * (g * _sigmoid(g))).astype(BF16)

    row = pl.BlockSpec((WIN, D), lambda n: (n, 0))
    prev = pl.BlockSpec((WIN, N_KV * HD), lambda n: (jnp.maximum(n - 1, 0), 0))
    cur = pl.BlockSpec((WIN, N_KV * HD), lambda n: (n, 0))
    return pl.pallas_call(
        body, name="swa_fwd", grid=(nb,),
        in_specs=[row, prev, cur, prev, cur, pl.BlockSpec((WIN, D), lambda n: (n, 1)),
                  pl.BlockSpec(memory_space=pltpu.SMEM)],
        out_specs=[row, row, pl.BlockSpec((N_HEADS, 1, 1, WIN), lambda n: (0, n, 0, 0))],
        out_shape=[_sds((S, D), F32), _sds((S, D), BF16), _sds((N_HEADS, nb, 1, WIN), F32)],
        compiler_params=_params())(qb, ksh, ksh, vsh, vsh, pb, sinks)


def _swa_bwd(qb, ksh, vsh, dz, o, lse, pb, sinks, gq, cos2, sin2):
    nb = S // WIN

    def body(q_ref, kp_ref, kc_ref, vp_ref, vc_ref, dz_ref, o_ref, lse_ref, x_ref, g_ref, s_ref, gq_ref, c_ref, sn_ref,
             dpb_ref, dka_ref, dkb_ref, dva_ref, dvb_ref, dsink_ref, dgq_ref):
        n = pl.program_id(0)

        @pl.when(n == 0)
        def _():
            dsink_ref[...] = jnp.zeros_like(dsink_ref)
            dgq_ref[...] = jnp.zeros_like(dgq_ref)

        valid = _swa_mask(n)
        g = g_ref[...]
        sg = _sigmoid(g)
        dzv = dz_ref[...]
        ov = o_ref[...]
        do = dzv * (g * sg)
        dpb_ref[:, D:2 * D] = (dzv * ov * (sg * (1.0 + g * (1.0 - sg)))).astype(BF16)
        prod_t = (do * ov).T
        lane1 = _lane_iota((1, LANES))
        dqs, dkas, dkbs, dvas, dvbs = [], [], [], [], []
        dsink = jnp.zeros((1, LANES), F32)
        for kh in range(N_KV):
            kb, vb = _band(kp_ref, kc_ref, kh), _band(vp_ref, vc_ref, kh)
            base = GRP * HD * kh
            qs = _stack4(q_ref, base)
            dos = _stack4(do, base).astype(BF16)
            delta = jnp.concatenate(
                [jnp.sum(prod_t[base + HD * gg:base + HD * (gg + 1), :], axis=0, keepdims=True)
                 for gg in range(GRP)], axis=1)
            lse = jnp.concatenate([lse_ref[GRP * kh + gg, 0] for gg in range(GRP)], axis=1)
            st = jnp.where(valid, _dot_nt(kb, qs), -jnp.inf)
            pt = jnp.exp(st - lse)
            dst = pt * (_dot_nt(vb, dos) - delta)
            dsb = dst.astype(BF16)
            dqs.append(_unstack4(_dot_tn(kb, dsb)))
            dkband = _dot_nn(dsb, qs)
            dvband = _dot_nn(pt.astype(BF16), dos)
            dkbs.append(dkband[0:WIN, :])
            dkas.append(dkband[WIN:2 * WIN, :])
            dvbs.append(dvband[0:WIN, :])
            dvas.append(dvband[WIN:2 * WIN, :])
            ps_delta = jnp.exp(_sink_row(s_ref, GRP * kh) - lse) * delta
            for gg in range(GRP):
                val = jnp.sum(ps_delta[:, WIN * gg:WIN * (gg + 1)], axis=1, keepdims=True)
                dsink = dsink - jnp.where(lane1 == GRP * kh + gg, val, 0.0)
        dka_ref[...] = jnp.concatenate(dkas, axis=-1)
        dkb_ref[...] = jnp.concatenate(dkbs, axis=-1)
        dva_ref[...] = jnp.concatenate(dvas, axis=-1)
        dvb_ref[...] = jnp.concatenate(dvbs, axis=-1)
        dsink_ref[...] += dsink

        lane = _lane_iota((WIN, LANES))
        g2, cos, sin = _g2(gq_ref), c_ref[...], sn_ref[...]
        lo_half = _half_ones()
        dg_tot = jnp.zeros((1, LANES), F32)
        for kh in range(N_KV):
            for c in _pairs(GRP * HD):
                cols = slice(GRP * HD * kh + c.start, GRP * HD * kh + c.stop)
                dn = _rope_bwd(dqs[kh][:, c] * QSCALE, cos, sin, lane)
                dx, dg = _head_norm_bwd(dn, x_ref[:, cols], g2, lo_half)
                dpb_ref[:, cols] = dx.astype(BF16)
                dg_tot = dg_tot + dg
        dgq_ref[...] += dg_tot

    row = pl.BlockSpec((WIN, D), lambda n: (n, 0))
    prev = pl.BlockSpec((WIN, N_KV * HD), lambda n: (jnp.maximum(n - 1, 0), 0))
    cur = pl.BlockSpec((WIN, N_KV * HD), lambda n: (n, 0))
    acc = pl.BlockSpec((1, LANES), lambda n: (0, 0))
    tab = pl.BlockSpec((WIN, LANES), lambda n: (n, 0))
    return pl.pallas_call(
        body, name="swa_bwd", grid=(nb,),
        in_specs=[row, prev, cur, prev, cur, row, row, pl.BlockSpec((N_HEADS, 1, 1, WIN), lambda n: (0, n, 0, 0)),
                  row, pl.BlockSpec((WIN, D), lambda n: (n, 1)), pl.BlockSpec(memory_space=pltpu.SMEM),
                  pl.BlockSpec((1, HD), lambda n: (0, 0)), tab, tab],
        out_specs=[pl.BlockSpec((WIN, 2 * D), lambda n: (n, 0)), cur, cur, cur, cur, acc, acc],
        out_shape=[_sds((S, 2 * D), BF16)] + [_sds((S, 256), F32)] * 4 + [_sds((1, LANES), F32)] * 2,
        compiler_params=_params())(qb, ksh, ksh, vsh, vsh, dz, o, lse, pb, pb, sinks, gq, cos2, sin2)


def _prep_kv_bwd(dka, dkb, dva, dvb, kv, gk, cos2, sin2):
    nt = S // RB
    per = RB // WIN

    def shifted(cur_ref, nxt_ref, has_next):
        return jnp.concatenate([cur_ref[WIN:RB, :], jnp.where(has_next, nxt_ref[...], 0.0)], axis=0)

    def body(dka_ref, dkb_ref, dkn_ref, dva_ref, dvb_ref, dvn_ref, x_ref, g_ref, c_ref, s_ref, o_ref, dgk_ref):
        i, j = pl.program_id(0), pl.program_id(1)

        @pl.when((i == 0) & (j == 0))
        def _():
            dgk_ref[...] = jnp.zeros_like(dgk_ref)

        has_next = i < nt - 1

        @pl.when(j == 0)
        def _():
            lane = _lane_iota((RB, LANES))
            g2, cos, sin = _g2(g_ref), c_ref[...], s_ref[...]
            dy_all = dka_ref[...] + shifted(dkb_ref, dkn_ref, has_next)
            dg_tot = jnp.zeros((1, LANES), F32)
            lo_half = _half_ones()
            for c in _pairs(CB):
                dn = _rope_bwd(dy_all[:, c], cos, sin, lane)
                dx, dg = _head_norm_bwd(dn, x_ref[:, c], g2, lo_half)
                o_ref[:, c] = dx.astype(BF16)
                dg_tot = dg_tot + dg
            dgk_ref[...] += dg_tot

        @pl.when(j == 1)
        def _():
            o_ref[...] = (dva_ref[...] + shifted(dvb_ref, dvn_ref, has_next)).astype(BF16)

    cur = pl.BlockSpec((RB, CB), lambda i, j: (i, 0))
    nxt = pl.BlockSpec((WIN, CB), lambda i, j: (jnp.minimum(per * (i + 1), S // WIN - 1), 0))
    tab = pl.BlockSpec((RB, LANES), lambda i, j: (i, 0))
    return pl.pallas_call(
        body, name="prep_kv_bwd", grid=(nt, 2),
        in_specs=[cur, cur, nxt, cur, cur, nxt, cur, pl.BlockSpec((1, HD), lambda i, j: (0, 0)), tab, tab],
        out_specs=[pl.BlockSpec((RB, CB), lambda i, j: (i, j)), pl.BlockSpec((1, LANES), lambda i, j: (0, 0))],
        out_shape=[_sds((S, 2 * CB), BF16), _sds((1, LANES), F32)],
        compiler_params=_params())(dka, dkb, dkb, dva, dvb, dvb, kv, gk, cos2, sin2)


def _out_b_loss(z, w_out, h1, tgt):
    tm = 2 * RT

    def body(z_ref, w_ref, h_ref, t_ref, dy_ref, l_ref):
        @pl.when(pl.program_id(0) == 0)
        def _():
            l_ref[...] = jnp.zeros_like(l_ref)

        e = (h_ref[...] + _dot_nn(z_ref[...], w_ref[...])) - t_ref[...]
        dy_ref[...] = e * (1.0 / D)
        l_ref[...] += jnp.sum(jnp.sum(e * e, axis=-1, keepdims=True), axis=0, keepdims=True)

    row = pl.BlockSpec((tm, D), lambda i: (i, 0))
    return pl.pallas_call(
        body, name="out_b_loss", grid=(S // tm,),
        in_specs=[row, pl.BlockSpec(w_out.shape, lambda i: (0, 0), pipeline_mode=pl.Buffered(1)), row, row],
        out_specs=[row, pl.BlockSpec((1, LANES), lambda i: (0, 0))],
        out_shape=[_sds((S, D), F32), _sds((1, LANES), F32)], compiler_params=_params())(z, w_out, h1, tgt)


def _du_a_rms_bwd(dproj, wa, x, g, dres, after):
    tm, tk = 1024, NA // 2
    nk = NA // tk

    def body(a_ref, b_ref, x_ref, g_ref, dr_ref, after_ref, dx_ref, dg_ref, acc):
        i, kk = pl.program_id(0), pl.program_id(1)

        @pl.when((i == 0) & (kk == 0))
        def _():
            dg_ref[...] = jnp.zeros_like(dg_ref)

        p = _dot_nt(a_ref[...], b_ref[...])

        @pl.when(kk == 0)
        def _():
            acc[...] = p

        @pl.when(kk == nk - 1)
        def _():
            dx, dg = _rms_bwd_core(acc[...] + p, x_ref[...], g_ref[...])
            dx_ref[...] = dr_ref[...] + dx
            dg_ref[...] += dg

    assert nk == 2
    row = pl.BlockSpec((tm, D), lambda i, kk: (i, 0))
    vec = pl.BlockSpec((1, D), lambda i, kk: (0, 0))
    return pl.pallas_call(
        body, name="du_a_rms_bwd", grid=(S // tm, nk),
        in_specs=[pl.BlockSpec((tm, tk), lambda i, kk: (i, kk)), pl.BlockSpec((D, tk), lambda i, kk: (0, kk)),
                  row, vec, row, pl.BlockSpec(after.shape, lambda i, kk: (0, 0))],
        out_specs=[row, vec], out_shape=[_sds((S, D), F32), _sds((1, D), F32)],
        scratch_shapes=[pltpu.VMEM((tm, D), F32)], compiler_params=_params())(dproj, wa, x, g, dres, after)


def _du_b_rms_bwd(dpb, w_in_b, dkv, w_kv, h1, g_b, g_kv, dy):
    tm = 2 * RT

    def body(ab_ref, wb_ref, akv_ref, wkv_ref, x_ref, gb_ref, gkv_ref, dy_ref, dh_ref, dgb_ref, dgkv_ref):
        @pl.when(pl.program_id(0) == 0)
        def _():
            dgb_ref[...] = jnp.zeros_like(dgb_ref)
            dgkv_ref[...] = jnp.zeros_like(dgkv_ref)

        x = x_ref[...]
        dx1, dg1 = _rms_bwd_core(_dot_nt(ab_ref[...], wb_ref[...]), x, gb_ref[...])
        dx2, dg2 = _rms_bwd_core(_dot_nt(akv_ref[...], wkv_ref[...]), x, gkv_ref[...])
        dh_ref[...] = dy_ref[...] + dx1 + dx2
        dgb_ref[...] += dg1
        dgkv_ref[...] += dg2

    row = lambda width: pl.BlockSpec((tm, width), lambda i: (i, 0))
    whole = lambda arr: pl.BlockSpec(arr.shape, lambda i: (0, 0), pipeline_mode=pl.Buffered(1))
    vec = pl.BlockSpec((1, D), lambda i: (0, 0))
    return pl.pallas_call(
        body, name="du_b_rms_bwd", grid=(S // tm,),
        in_specs=[row(dpb.shape[1]), whole(w_in_b), row(dkv.shape[1]), whole(w_kv), row(D), vec, vec, row(D)],
        out_specs=[row(D), vec, vec], out_shape=[_sds((S, D), F32), _sds((1, D), F32), _sds((1, D), F32)],
        compiler_params=_params())(dpb, w_in_b, dkv, w_kv, h1, g_b, g_kv, dy)


GATHER_CHUNKS = 4


def _gather_first(w_in_a, w_out_a, w_kv, w_in_b, w_out_b, norm_a_g):
    rows, cols = w_in_a.shape[-2:]
    groups = rows // LANES
    cols_pad = -(-cols // LANES) * LANES

    def body(wia_ref, woa_ref, wkv_ref, wib_ref, wob_ref, ga_ref,
             wa_pad, ga_g, woa_s, wkv_s, wib_s, wob_s, wa_s, st_a, st_oa, st_kv, st_ib, st_ob, plane_t, wa_g, wa_v,
             out_sems, load_sems, *sems):
        sources = [wia_ref, woa_ref.at[0], wkv_ref, wib_ref.at[0], wob_ref.at[0]]
        stages = [st_a, st_oa, st_kv, st_ib, st_ob]
        loads = [pltpu.make_async_copy(src, dst, load_sems.at[i]) for i, (src, dst) in enumerate(zip(sources, stages))]
        for cp in loads:
            cp.start()
        loads[0].wait()
        plane_t[cols_pad - LANES:, :] = jnp.zeros((LANES, LANES), F32)
        for j in range(groups):
            for c0 in range(0, cols, 64):
                n = min(64, cols - c0)
                plane_t[c0:c0 + n, :] = st_a[pl.ds(c0 * groups + j, n, stride=groups), :]
            for c0 in range(0, cols, LANES):
                n = min(LANES, cols - c0)
                wa_s[j * LANES:(j + 1) * LANES, c0:c0 + n] = plane_t[c0:c0 + LANES, :].T[:, :n].astype(BF16)

        def cast_the_rest():
            for cp, stage, out in zip(loads[1:], stages[1:], [woa_s, wkv_s, wib_s, wob_s]):
                cp.wait()
                out[...] = stage[...].astype(BF16)

        written = []

        def lay_out(a):
            if a >= GATHER_CHUNKS:
                return
            rc = slice(a * (rows // GATHER_CHUNKS), (a + 1) * (rows // GATHER_CHUNKS))
            wa_v[rc, FOFF + N_HEADS:NA] = jnp.zeros((rows // GATHER_CHUNKS, NA - FOFF - N_HEADS), BF16)
            for d, src, dst, width in _shard_pieces():
                wa_v[rc, dst:dst + width] = wa_g[d, rc, src:src + width]
            written.append(pltpu.make_async_copy(wa_v.at[rc], wa_pad.at[rc], out_sems.at[a]))
            written[-1].start()

        chunks = [pl.ds(r0, rows // GATHER_CHUNKS) for r0 in range(0, rows, rows // GATHER_CHUNKS)]
        _gather_two_level(
            [wa_s.at[rc] for rc in chunks] + [ga_ref],
            [lambda dev, rc=rc: wa_g.at[dev, rc] for rc in chunks] + [lambda dev: ga_g.at[dev]],
            sems, meanwhile=cast_the_rest, landed=lay_out)
        for cp in written:
            cp.wait()

    vmem = pl.BlockSpec(memory_space=pltpu.VMEM)
    anyspec = pl.BlockSpec(memory_space=pl.ANY)
    shard = lambda w: _sds(w.shape[-2:], BF16)
    stage = lambda w: pltpu.VMEM(w.shape[-2:], F32)
    return pl.pallas_call(
        body, name="gather_first", in_specs=[anyspec] * 5 + [vmem],
        out_specs=[anyspec, anyspec, vmem, vmem, vmem, vmem],
        out_shape=[_sds((rows, NA), BF16), _sds((N_DEV,) + norm_a_g.shape, F32),
                   shard(w_out_a), shard(w_kv), shard(w_in_b), shard(w_out_b)],
        scratch_shapes=[pltpu.VMEM(w_in_a.shape[-2:], BF16), pltpu.VMEM((cols * groups, LANES), F32), stage(w_out_a),
                        stage(w_kv), stage(w_in_b), stage(w_out_b), pltpu.VMEM((cols_pad, LANES), F32),
                        pltpu.VMEM((N_DEV,) + w_in_a.shape[-2:], BF16), pltpu.VMEM((rows, NA), BF16),
                        pltpu.SemaphoreType.DMA((GATHER_CHUNKS,)), pltpu.SemaphoreType.DMA((5,))] + _exchange_sems(GATHER_CHUNKS + 1),
        compiler_params=pltpu.CompilerParams(vmem_limit_bytes=VMEM_LIMIT, has_side_effects=True))(
            _entry_view(w_in_a).reshape(cols * groups, LANES), w_out_a, w_kv, w_in_b, w_out_b, norm_a_g)


def _padded_col(c):
    if c < RAW_F:
        return c
    return FOFF + (c - RAW_F) if c < RAW_G else GOFF + (c - RAW_G)


def _shard_pieces():
    width = NA_RAW // N_DEV
    pieces = []
    for d in range(N_DEV):
        cuts = [width * d] + [c for c in (RAW_F, RAW_G) if width * d < c < width * (d + 1)] + [width * (d + 1)]
        for lo, hi in zip(cuts[:-1], cuts[1:]):
            pieces.append((d, lo - width * d, _padded_col(lo), hi - lo))
    return pieces


def _reshard_pair_reduce(dwa):
    n_chip, nt = N_DEV // 2, D // TM
    width = NA_RAW // N_DEV

    def body(g_ref, o_ref, slots_v, sib_v, send_sems, recv_sems):
        i = pl.program_id(0)
        x, y, c = lax.axis_index("x"), lax.axis_index("y"), lax.axis_index("c")

        def tile_rows(tile):
            return pl.ds(tile * TM if isinstance(tile, int) else pl.multiple_of(tile * TM, TM), TM)

        def give(j, tile):
            return pltpu.make_async_remote_copy(
                src_ref=slots_v.at[2 * j + 1 - c, tile_rows(tile)], dst_ref=sib_v.at[j, tile_rows(tile)],
                send_sem=send_sems.at[j, tile], recv_sem=recv_sems.at[j, tile], device_id=(x, y, 1 - c),
                device_id_type=pl.DeviceIdType.MESH)

        for d, src, dst, w in _shard_pieces():
            slots_v[d, tile_rows(i), src:src + w] = g_ref[:, dst:dst + w]
        for j in range(n_chip):
            give(j, i).start()

        @pl.when(i == nt - 1)
        def _():
            for tile in range(nt):
                for j in range(n_chip):
                    give(j, tile).wait()
            for j in range(n_chip):
                for r0 in range(0, D, 64):
                    mine = slots_v[2 * j + c, r0:r0 + 64, :].astype(F32)
                    o_ref[j, r0:r0 + 64, :] = (mine + sib_v[j, r0:r0 + 64, :].astype(F32)).astype(BF16)

    half = _sds((n_chip, D, width), dwa.dtype)
    return pl.pallas_call(
        body, name="reshard_pair_reduce", grid=(nt,), in_specs=[pl.BlockSpec((TM, NA), lambda i: (i, 0))],
        out_specs=pl.BlockSpec(half.shape, lambda i: (0, 0, 0)), out_shape=half,
        scratch_shapes=[pltpu.VMEM((N_DEV, D, width), dwa.dtype), pltpu.VMEM(half.shape, dwa.dtype),
                        pltpu.SemaphoreType.DMA((n_chip, nt)), pltpu.SemaphoreType.DMA((n_chip, nt))],
        compiler_params=pltpu.CompilerParams(vmem_limit_bytes=VMEM_LIMIT, has_side_effects=True))(dwa)


CHIP_FLIPS = (2, 4, 6)


def _chip_exchange_start(partial):
    n = len(CHIP_FLIPS)

    def body(p_ref, land_ref, *rest):
        sends, recvs, token = rest[:n], rest[n:2 * n], rest[2 * n + 2]
        x, y, c = lax.axis_index("x"), lax.axis_index("y"), lax.axis_index("c")
        me = 4 * x + 2 * y + c
        for idx, k in enumerate(CHIP_FLIPS):
            pltpu.make_async_remote_copy(
                src_ref=p_ref.at[(me ^ k) >> 1], dst_ref=land_ref.at[me >> 1], send_sem=sends[idx],
                recv_sem=recvs[idx], device_id=(x ^ ((k >> 2) & 1), y ^ ((k >> 1) & 1), c),
                device_id_type=pl.DeviceIdType.MESH).start()
        token[...] = jnp.zeros_like(token)

    hbm = pl.BlockSpec(memory_space=pltpu.HBM)
    sem = pl.BlockSpec(memory_space=pltpu.SEMAPHORE)
    buf = pltpu.HBM(partial.shape, partial.dtype)
    return pl.pallas_call(
        body, name="chip_exchange_start",
        out_shape=(pltpu.SemaphoreType.DMA(()),) * (2 * n) + (buf, buf, _sds((8, LANES), F32)),
        in_specs=(hbm, hbm), out_specs=(sem,) * (2 * n) + (hbm, hbm, pl.BlockSpec(memory_space=pltpu.VMEM)),
        input_output_aliases={0: 2 * n, 1: 2 * n + 1},
        compiler_params=pltpu.CompilerParams(has_side_effects=pltpu.SideEffectType.DATAFLOW_SIDE_EFFECTING))(
            pltpu.with_memory_space_constraint(partial, pltpu.HBM),
            pltpu.with_memory_space_constraint(lax.empty(partial.shape, partial.dtype), pltpu.HBM))


def _slab_peer(x, y, c, k):
    return (x ^ ((k >> 2) & 1), y ^ ((k >> 1) & 1), c ^ (k & 1))


def _chip_wait_slab_start(chip_started, slab, after):
    nc, n, na = len(CHIP_FLIPS), N_DEV - 1, len(after)
    chip_sems, (p_thru, land_thru) = chip_started[:2 * nc], chip_started[2 * nc:2 * nc + 2]

    def body(p_ref, land_ref, *rest):
        chip_sends, chip_recvs, s_ref, sl_ref = rest[:nc], rest[nc:2 * nc], rest[2 * nc], rest[2 * nc + 1]
        outs = rest[2 * nc + 2 + na:]
        sends, recvs, own_sem = outs[2:2 + n], outs[2 + n:2 + 2 * n], outs[4 + 2 * n]
        x, y, c = lax.axis_index("x"), lax.axis_index("y"), lax.axis_index("c")
        me = 4 * x + 2 * y + c
        for idx, k in enumerate(CHIP_FLIPS):
            copy = pltpu.make_async_remote_copy(
                src_ref=p_ref.at[(me ^ k) >> 1], dst_ref=land_ref.at[(me ^ k) >> 1], send_sem=chip_sends[idx],
                recv_sem=chip_recvs[idx], device_id=(x ^ ((k >> 2) & 1), y ^ ((k >> 1) & 1), c),
                device_id_type=pl.DeviceIdType.MESH)
            copy.wait_send()
            copy.wait_recv()
        for k in range(1, N_DEV):
            pltpu.make_async_remote_copy(
                src_ref=s_ref, dst_ref=sl_ref.at[me], send_sem=sends[k - 1], recv_sem=recvs[k - 1],
                device_id=_slab_peer(x, y, c, k), device_id_type=pl.DeviceIdType.MESH).start()
        own = pltpu.make_async_copy(s_ref, sl_ref.at[me], own_sem)
        own.start()
        own.wait()

    hbm = pl.BlockSpec(memory_space=pltpu.HBM)
    sem = pl.BlockSpec(memory_space=pltpu.SEMAPHORE)
    buf = pltpu.HBM(p_thru.shape, p_thru.dtype)
    land = (N_DEV,) + slab.shape
    outs = pl.pallas_call(
        body, name="chip_wait_slab_start",
        out_shape=(buf, buf) + (pltpu.SemaphoreType.DMA(()),) * (2 * n) + (
            pltpu.HBM(slab.shape, slab.dtype), pltpu.HBM(land, slab.dtype)),
        in_specs=(hbm, hbm) + (sem,) * (2 * nc) + (hbm, hbm) + (pl.BlockSpec(memory_space=pl.ANY),) * na,
        out_specs=(hbm, hbm) + (sem,) * (2 * n) + (hbm, hbm),
        input_output_aliases={0: 0, 1: 1, 2 * nc + 2: 2 * n + 2, 2 * nc + 3: 2 * n + 3},
        scratch_shapes=[pltpu.SemaphoreType.DMA(())],
        compiler_params=pltpu.CompilerParams(has_side_effects=pltpu.SideEffectType.DATAFLOW_SIDE_EFFECTING))(
            p_thru, land_thru, *chip_sems, pltpu.with_memory_space_constraint(slab, pltpu.HBM),
            pltpu.with_memory_space_constraint(lax.empty(land, slab.dtype), pltpu.HBM), *after)
    return outs[0], outs[1], tuple(outs[2:])


def _gather_slab_wait(started, after):
    n = N_DEV - 1
    sems, (s_thru, land_thru) = started[:2 * n], started[2 * n:2 * n + 2]

    def body(s_ref, land_ref, *rest):
        sends, recvs = rest[:n], rest[n:2 * n]
        x, y, c = lax.axis_index("x"), lax.axis_index("y"), lax.axis_index("c")
        me = 4 * x + 2 * y + c
        for k in range(1, N_DEV):
            copy = pltpu.make_async_remote_copy(
                src_ref=s_ref, dst_ref=land_ref.at[me ^ k], send_sem=sends[k - 1], recv_sem=recvs[k - 1],
                device_id=_slab_peer(x, y, c, k), device_id_type=pl.DeviceIdType.MESH)
            copy.wait_send()
            copy.wait_recv()

    hbm = pl.BlockSpec(memory_space=pltpu.HBM)
    sem = pl.BlockSpec(memory_space=pltpu.SEMAPHORE)
    return pl.pallas_call(
        body, name="gather_slab_wait",
        out_shape=(pltpu.HBM(s_thru.shape, s_thru.dtype), pltpu.HBM(land_thru.shape, land_thru.dtype)),
        in_specs=(hbm, hbm) + (sem,) * (2 * n) + (pl.BlockSpec(memory_space=pl.ANY),) * len(after),
        out_specs=(hbm, hbm), input_output_aliases={0: 0, 1: 1},
        compiler_params=pltpu.CompilerParams(has_side_effects=pltpu.SideEffectType.DATAFLOW_SIDE_EFFECTING))(
            s_thru, land_thru, *sems, *after)[1]


def _adamw(w, g, m, v):
    m = ADAM_B1 * m + (1.0 - ADAM_B1) * g
    v = ADAM_B2 * v + (1.0 - ADAM_B2) * (g * g)
    m_hat = m / (1.0 - ADAM_B1 ** ADAM_STEP)
    v_hat = v / (1.0 - ADAM_B2 ** ADAM_STEP)
    delta = -ADAM_LR * (m_hat / (jnp.sqrt(v_hat) + ADAM_EPS) + ADAM_WD * w)
    return delta, m, v


def _sum_adamw(recv, w, m, v, name, after=None):
    lead = w.ndim - 2
    rows, cols = w.shape[-2:]
    tr = 256 if rows % 256 == 0 else 128
    n_slots = recv.shape[0]
    extra = [] if after is None else [after]

    def body(*refs):
        r_ref = refs[0]
        w_ref, m_ref, v_ref, g_ref, d_ref, nm_ref, nv_ref = refs[1 + len(extra):]
        g = None
        for slot in range(n_slots):
            g = r_ref[slot].astype(F32) if g is None else g + r_ref[slot].astype(F32)
        g_ref[...] = g
        d_ref[...], nm_ref[...], nv_ref[...] = _adamw(w_ref[...], g, m_ref[...], v_ref[...])

    blk = pl.BlockSpec((None,) * lead + (tr, cols), lambda i: (0,) * lead + (i, 0))
    slots = pl.BlockSpec((n_slots, tr, cols), lambda i: (0, i, 0))
    return pl.pallas_call(
        body, name=name, grid=(rows // tr,),
        in_specs=[slots] + [pl.BlockSpec(a.shape, lambda i: (0, 0)) for a in extra] + [blk, blk, blk],
        out_specs=[blk] * 4, out_shape=[_sds(w.shape, F32)] * 4, compiler_params=_params())(recv, *extra, w, m, v)


def _entry_view(a):
    _, rows, cols = a.shape
    return jnp.transpose(a, (2, 0, 1)).reshape(cols, rows // LANES, LANES)


def _from_entry_view(a):
    cols, groups, lanes = a.shape
    return jnp.transpose(a, (1, 2, 0)).reshape(1, groups * lanes, cols)


def _sum_adamw_entry_view(landing, own, w, m, v, name):
    n_slots, rows, cols = landing.shape
    groups = rows // LANES
    cols_pad = -(-cols // LANES) * LANES

    def body(r_ref, own_ref, w_ref, m_ref, v_ref, g_ref, d_ref, nm_ref, nv_ref, pad_ref, gt_ref):
        j = pl.program_id(0)
        chip = (4 * lax.axis_index("x") + 2 * lax.axis_index("y") + lax.axis_index("c")) >> 1
        pad_ref[:, cols_pad - LANES:] = jnp.zeros((LANES, LANES), F32)
        for r0 in range(0, LANES, 32):
            g = None
            for slot in range(n_slots):
                part = jnp.where(chip == slot, own_ref[slot, r0:r0 + 32], r_ref[slot, r0:r0 + 32])
                g = part.astype(F32) if g is None else g + part.astype(F32)
            pad_ref[r0:r0 + 32, :cols] = g
        for c0 in range(0, cols_pad, LANES):
            gt_ref[c0:c0 + LANES, :] = pad_ref[:, c0:c0 + LANES].T
        def update_plane(plane):
            for c0 in range(0, cols, 64):
                n = min(64, cols - c0)
                at = pl.ds(c0 * groups + plane, n, stride=groups)
                gt = gt_ref[c0:c0 + n, :]
                g_ref[at, :] = gt
                d_ref[at, :], nm_ref[at, :], nv_ref[at, :] = _adamw(w_ref[at, :], gt, m_ref[at, :], v_ref[at, :])

        for plane in range(groups):
            pl.when(j == plane)(lambda plane=plane: update_plane(plane))

    flat = lambda a: _entry_view(a).reshape(cols * groups, LANES)
    whole = pl.BlockSpec((cols * groups, LANES), lambda j: (0, 0))
    slots = pl.BlockSpec((n_slots, LANES, cols), lambda j: (0, j, 0))
    outs = pl.pallas_call(
        body, name=name, grid=(groups,), in_specs=[slots, slots, whole, whole, whole], out_specs=[whole] * 4,
        out_shape=[_sds((cols * groups, LANES), F32)] * 4,
        scratch_shapes=[pltpu.VMEM((LANES, cols_pad), F32), pltpu.VMEM((cols_pad, LANES), F32)],
        compiler_params=_params())(landing, own, flat(w), flat(m), flat(v))
    return [_from_entry_view(o.reshape(cols, groups, LANES)) for o in outs], outs[0]


SLAB_ROWS = 16
SLOT = {"kv_norm_g": (8, 0, D), "norm_b_g": (9, 0, D), "b_forget": (10, 0, 16), "qnorm_a_g": (10, 128, HD),
        "knorm_a_g": (10, 256, HD), "knorm_b_g": (10, 384, HD), "qnorm_b_g": (10, 512, HD), "sinks": (10, 640, 16)}
SMALL = ["norm_a_g", "b_forget", "qnorm_a_g", "knorm_a_g", "kv_norm_g", "knorm_b_g", "norm_b_g", "qnorm_b_g", "sinks"]


LOSS_ROW = 11


def _pack_small(dg_a, dg_kv, dg_b, db_f, dgq_a, dgk_a, dgk_b, dgq_b, dsinks, lsum):
    def fold(ref):
        return ref[:, 0:HD] + ref[:, HD:2 * HD]

    def body(dga_ref, dgkv_ref, dgb_ref, dbf_ref, dgqa_ref, dgka_ref, dgkb_ref, dgqb_ref, dsk_ref, ls_ref, slab_ref):
        slab_ref[...] = jnp.zeros_like(slab_ref)
        for r in range(N_DEV):
            slab_ref[r:r + 1, 0:LANES] = dga_ref[:, LANES * r:LANES * (r + 1)]
        slab_ref[8:9, :] = dgkv_ref[...]
        slab_ref[9:10, :] = dgb_ref[...]
        slab_ref[10:11, 0:LANES] = dbf_ref[...]
        slab_ref[10:11, 128:128 + HD] = fold(dgqa_ref)
        slab_ref[10:11, 256:256 + HD] = fold(dgka_ref)
        slab_ref[10:11, 384:384 + HD] = fold(dgkb_ref)
        slab_ref[10:11, 512:512 + HD] = fold(dgqb_ref)
        slab_ref[10:11, 640:640 + LANES] = dsk_ref[...]
        slab_ref[LOSS_ROW:LOSS_ROW + 1, 0:LANES] = ls_ref[...]

    return pl.pallas_call(body, name="pack_small", out_shape=_sds((SLAB_ROWS, D), F32), compiler_params=_params())(
        dg_a, dg_kv, dg_b, db_f, dgq_a, dgk_a, dgk_b, dgq_b, dsinks, lsum)


def _small_adamw(recv, ws, ms, vs):
    k = len(SMALL)

    def body(*refs):
        r_ref = refs[0]
        w_refs, m_refs, v_refs = refs[1:1 + k], refs[1 + k:1 + 2 * k], refs[1 + 2 * k:1 + 3 * k]
        outs = refs[1 + 3 * k:1 + 7 * k]
        loss_ref, tot = refs[1 + 7 * k], refs[2 + 7 * k]
        g = r_ref[0]
        for dev in range(1, N_DEV):
            g = g + r_ref[dev]
        tot[...] = g
        loss_ref[...] = tot[LOSS_ROW:LOSS_ROW + 1, 0:LANES] * (0.5 / D)
        me = 4 * lax.axis_index("x") + 2 * lax.axis_index("y") + lax.axis_index("c")
        for p, name in enumerate(SMALL):
            if name == "norm_a_g":
                mine = lax.broadcasted_iota(jnp.int32, (N_DEV, LANES), 0) == me
                gp = jnp.sum(jnp.where(mine, tot[0:N_DEV, 0:LANES], 0.0), axis=0, keepdims=True)
            else:
                row, lo, width = SLOT[name]
                gp = tot[row:row + 1, lo:lo + width]
            d, nm, nv = _adamw(w_refs[p][...], gp, m_refs[p][...], v_refs[p][...])
            outs[p][...] = gp
            outs[k + p][...] = d
            outs[2 * k + p][...] = nm
            outs[3 * k + p][...] = nv

    shapes = [_sds(w.shape, F32) for w in ws]
    return pl.pallas_call(body, name="small_adamw", out_shape=shapes * 4 + [_sds((1, LANES), F32)],
                          scratch_shapes=[pltpu.VMEM((SLAB_ROWS, D), F32)],
                          compiler_params=_params())(recv, *ws, *ms, *vs)


def _rope_tables(positions):
    inv_freq = jnp.power(jnp.float32(ROPE_THETA), -jnp.arange(0, ROT, 2, dtype=F32) / ROT)
    ang = positions.astype(F32)[:, None] * inv_freq[None, :]
    cos, sin = jnp.cos(ang), jnp.sin(ang)
    c64 = jnp.concatenate([cos, cos, jnp.ones((S, HD - ROT), F32)], axis=-1)
    s64 = jnp.concatenate([-sin, sin, jnp.zeros((S, HD - ROT), F32)], axis=-1)
    return jnp.tile(c64, (1, 2)), jnp.tile(s64, (1, 2))


def _local_step(x, tgt, positions, g_a, wa, b_forget, gq_a, gk_a, g_kv, gk_b, g_b, gq_b, sinks,
                woa_s, wkv_s, wib_s, wob_s, adamw_others):
    nq = S // TQ
    cos2, sin2 = _rope_tables(positions)
    b_pad = jnp.pad(b_forget, ((0, 0), (0, LANES - N_HEADS)))

    u_a, proj, qn, kn, vb, crow, cbc = _head_a(x, g_a, wa, gq_a, gk_a, b_pad)
    o_a, z_a, lse_a, woa_g, wkv_g, w_in_b, wob_g = _fox_fwd(
        qn, kn, vb, proj, crow, cbc,
        rider=[("gather_rows", woa_s), ("gather_rows", wkv_s), ("gather_cols", wib_s), ("gather_rows", wob_s)])
    w_out_a, w_kv, w_out_b = woa_g.reshape(D, D), wkv_g.reshape(D, 512), wob_g.reshape(D, D)
    h1, u_kv, u_b, kv, pb, qb, ksh, vsh = _head_b(x, z_a, w_out_a, g_kv, g_b, w_kv, w_in_b, gq_b, gk_b, cos2, sin2)
    sinks1 = sinks.reshape(N_HEADS)
    o_b, z_b, lse_b = _swa_fwd(qb, ksh, vsh, pb, sinks1)
    dy, lsum = _out_b_loss(z_b, w_out_b, h1, tgt)
    dw_out_b = _mm(z_b, dy, "tn", 512, 512, S, out_dtype=BF16, name="mm_dw_out_b")
    dz_b = _mm(dy, w_out_b, "nt", 1024, 512, D, name="mm_dz_b")
    dpb, dka, dkb, dva, dvb, dsinks, dgq_b = _swa_bwd(qb, ksh, vsh, dz_b, o_b, lse_b, pb, sinks1, gq_b, cos2, sin2)
    dkv, dgk_b = _prep_kv_bwd(dka, dkb, dva, dvb, kv, gk_b, cos2, sin2)
    dw_in_b = _mm(u_b, dpb, "tn", 512, 512, S, out_dtype=BF16, name="mm_dw_in_b")
    dw_kv = _mm(u_kv, dkv, "tn", 512, 512, S, out_dtype=BF16, name="mm_dw_kv")
    dh1, dg_b, dg_kv = _du_b_rms_bwd(dpb, w_in_b, dkv, w_kv, h1, g_b, g_kv, dy)
    dw_out_a = _mm(z_a, dh1, "tn", 512, 512, S, out_dtype=BF16, name="mm_dw_out_a")
    do_a, dgate_a, delta_a = _fox_bwd_pre(dh1, w_out_a, proj, o_a)
    dk_a, dv_a, dcs, dq_a, drow, r_wob, r_wib, r_wkv, r_woa = _fox_bwd(
        qn, kn, vb, do_a, lse_a, delta_a, crow, cbc,
        rider=[("a2a_rows", dw_out_b), ("a2a_cols", dw_in_b), ("a2a_rows", dw_kv), ("a2a_rows", dw_out_a)])
    dproj, dgq_a, dgk_a, db_f = _prep_a_bwd(dq_a, dk_a, dv_a, dgate_a, drow, dcs, proj, b_pad, gq_a, gk_a)
    dwa = _mm(u_a, dproj, "tn", 1024, 256, S, out_dtype=BF16, name="mm_dw_in_a")
    partial = _reshard_pair_reduce(dwa)
    started = _chip_exchange_start(partial)
    dx, dg_a = _du_a_rms_bwd(dproj, wa, x, g_a, dh1, after=started[-1])
    others = adamw_others(dict(w_out_a=r_woa, w_kv=r_wkv, w_in_b=r_wib, w_out_b=r_wob), dg_a)
    slab = _pack_small(dg_a, dg_kv, dg_b, db_f, dgq_a, dgk_a, dgk_b, dgq_b, dsinks, lsum)
    partial, landed, slab_started = _chip_wait_slab_start(started, slab, [res[0] for res in others.values()])
    return dx, (landed, partial), others, slab_started


def kernel(x, positions, norm_a_g, w_in_a, b_forget, qnorm_a_g, knorm_a_g, w_out_a, kv_norm_g, w_kv, knorm_b_g, norm_b_g, w_in_b, qnorm_b_g, sinks, w_out_b, loss_target, m_norm_a_g, m_w_in_a, m_b_forget, m_qnorm_a_g, m_knorm_a_g, m_w_out_a, m_kv_norm_g, m_w_kv, m_knorm_b_g, m_norm_b_g, m_w_in_b, m_qnorm_b_g, m_sinks, m_w_out_b, v_norm_a_g, v_w_in_a, v_b_forget, v_qnorm_a_g, v_knorm_a_g, v_w_out_a, v_kv_norm_g, v_w_kv, v_knorm_b_g, v_norm_b_g, v_w_in_b, v_qnorm_b_g, v_sinks, v_w_out_b):
    wa, ga_g, woa_s, wkv_s, wib_s, wob_s = _gather_first(w_in_a, w_out_a, w_kv, w_in_b, w_out_b, norm_a_g)
    state = dict(w_in_a=(w_in_a, m_w_in_a, v_w_in_a), w_out_a=(w_out_a, m_w_out_a, v_w_out_a),
                 w_kv=(w_kv, m_w_kv, v_w_kv), w_in_b=(w_in_b, m_w_in_b, v_w_in_b),
                 w_out_b=(w_out_b, m_w_out_b, v_w_out_b))

    def adamw_others(landed, after):
        return {n: _sum_adamw(r, *state[n], "adamw_" + n, after=after) for n, r in landed.items()}

    dx, r_wa, big, slab_started = _local_step(
        x[0], loss_target[0], positions, ga_g.reshape(1, D), wa, b_forget, qnorm_a_g, knorm_a_g,
        kv_norm_g.reshape(1, D), knorm_b_g.reshape(1, HD), norm_b_g, qnorm_b_g, sinks, woa_s, wkv_s, wib_s, wob_s,
        adamw_others)
    big["w_in_a"], done = _sum_adamw_entry_view(*r_wa, *state["w_in_a"], "adamw_w_in_a")
    slab_g = _gather_slab_wait(slab_started, [done])

    r2 = lambda a: a.reshape(1, -1)
    small_w = dict(norm_a_g=norm_a_g, b_forget=b_forget, qnorm_a_g=qnorm_a_g, knorm_a_g=knorm_a_g,
                   kv_norm_g=kv_norm_g, knorm_b_g=knorm_b_g, norm_b_g=norm_b_g, qnorm_b_g=qnorm_b_g, sinks=sinks)
    small_m = dict(norm_a_g=m_norm_a_g, b_forget=m_b_forget, qnorm_a_g=m_qnorm_a_g, knorm_a_g=m_knorm_a_g,
                   kv_norm_g=m_kv_norm_g, knorm_b_g=m_knorm_b_g, norm_b_g=m_norm_b_g, qnorm_b_g=m_qnorm_b_g,
                   sinks=m_sinks)
    small_v = dict(norm_a_g=v_norm_a_g, b_forget=v_b_forget, qnorm_a_g=v_qnorm_a_g, knorm_a_g=v_knorm_a_g,
                   kv_norm_g=v_kv_norm_g, knorm_b_g=v_knorm_b_g, norm_b_g=v_norm_b_g, qnorm_b_g=v_qnorm_b_g,
                   sinks=v_sinks)
    res = _small_adamw(slab_g, [r2(small_w[n]) for n in SMALL], [r2(small_m[n]) for n in SMALL],
                       [r2(small_v[n]) for n in SMALL])
    k = len(SMALL)
    small = {n: [res[q * k + p].reshape(small_w[n].shape) for q in range(4)] for p, n in enumerate(SMALL)}
    loss = res[4 * k][0, 0]

    order = ["norm_a_g", "w_in_a", "b_forget", "qnorm_a_g", "knorm_a_g", "w_out_a", "kv_norm_g", "w_kv",
             "knorm_b_g", "norm_b_g", "w_in_b", "qnorm_b_g", "sinks", "w_out_b"]

    def leaf(n, q):
        return big[n][q] if n in big else small[n][q]

    outs = [loss, dx[None]]
    for q in range(4):
        outs.extend(leaf(n, q) for n in order)
    return tuple(outs)
```
